```python
import math
import jax, jax.numpy as jnp
from jax import lax
import numpy as np

D_MODEL = 1024
BATCH = 8
SEQ = 8192
DEPTH = 1

D_MIX = D_MODEL
HG_HEADS = 4
HG_DK = 128
HG_DV = 128
HG_WIDTH = HG_HEADS * HG_DV
HG_CHUNK = 64
AT_HEADS = 8
AT_HEAD_DIM = 64
AT_WIDTH = AT_HEADS * AT_HEAD_DIM
DILATED_PATTERNS = ((128, 1), (512, 4), (2048, 16))
ATT_BLOCK = 128
ROPE_THETA = 10000.0
NORM_EPS = 1e-6
IN_SPLITS = (HG_HEADS * HG_DK, HG_HEADS * HG_DK, HG_WIDTH, HG_WIDTH,
             AT_WIDTH, AT_WIDTH, AT_WIDTH, AT_WIDTH)
IN_COLS = sum(IN_SPLITS)

kernel_name = "hgrn2_dilated_attn_parallel_hybrid"


def rms_norm(x, w):
    xf = x.astype(jnp.float32)
    y = xf * lax.rsqrt(jnp.mean(xf * xf, axis=-1, keepdims=True) + NORM_EPS)
    return y * w.astype(jnp.float32)


def rope(x, positions):
    half = x.shape[-1] // 2
    inv_freq = 1.0 / (ROPE_THETA ** (jnp.arange(half, dtype=jnp.float32) / half))
    ang = positions.astype(jnp.float32)[:, None] * inv_freq[None, :]
    cos = jnp.cos(ang)[:, None, :]
    sin = jnp.sin(ang)[:, None, :]
    x1, x2 = x[..., :half], x[..., half:]
    return jnp.concatenate([x1 * cos - x2 * sin, x2 * cos + x1 * sin], axis=-1)


def hgrn2_mixer(q, f_logit, i, lb):
    B, S, H, DK = q.shape
    DV = i.shape[-1]
    C = HG_CHUNK
    nC = S // C
    log_f = jnp.logaddexp(jnp.log(lb), jnp.log1p(-lb) + jax.nn.log_sigmoid(f_logit))
    k = -jnp.expm1(log_f)
    q = jax.nn.silu(q)

    def chunks(t):
        return t.reshape(B, nC, C, H, t.shape[-1]).transpose(1, 0, 3, 2, 4)

    causal = (jnp.arange(C)[:, None] >= jnp.arange(C)[None, :])[:, :, None]

    def step(state, inp):
        qc, kc, vc, gc = inp
        b = jnp.cumsum(gc, axis=2)
        o_inter = jnp.einsum('bhck,bhkv->bhcv', qc * jnp.exp(b), state)
        diff = b[:, :, :, None, :] - b[:, :, None, :, :]
        decay = jnp.exp(jnp.where(causal, diff, -jnp.inf))
        scores = jnp.einsum('bhtk,bhtsk,bhsk->bhts', qc, decay, kc)
        o_intra = jnp.einsum('bhts,bhsv->bhtv', scores, vc)
        b_last = b[:, :, -1:, :]
        k_dec = kc * jnp.exp(b_last - b)
        new_state = jnp.exp(b_last[:, :, 0, :])[..., None] * state + \
            jnp.einsum('bhsk,bhsv->bhkv', k_dec, vc)
        return new_state, o_inter + o_intra

    state0 = jnp.zeros((B, H, DK, DV), jnp.float32)
    _, out = lax.scan(step, state0, (chunks(q), chunks(k), chunks(i), chunks(log_f)))
    return out.transpose(1, 0, 3, 2, 4).reshape(B, S, H, DV)


def dilated_window_attention(q, k, v, window, dilation):
    B, S, H, Dh = q.shape
    span = window // dilation
    assert span <= ATT_BLOCK
    BLK = ATT_BLOCK
    L = S // dilation
    nb = -(-L // BLK)
    Lp = nb * BLK

    def to_sub(t):
        t = t.reshape(B, L, dilation, H, Dh).transpose(0, 2, 3, 1, 4)
        return jnp.pad(t, ((0, 0), (0, 0), (0, 0), (0, Lp - L), (0, 0)))

    def band(t):
        tp = jnp.pad(t, ((0, 0), (0, 0), (0, 0), (BLK, 0), (0, 0)))
        tp = tp.reshape(B, dilation, H, nb + 1, BLK, Dh)
        return jnp.concatenate([tp[:, :, :, :-1], tp[:, :, :, 1:]], axis=-2)

    qb = to_sub(q).reshape(B, dilation, H, nb, BLK, Dh)
    kb = band(to_sub(k))
    vb = band(to_sub(v))
    s = jnp.einsum('bdhnqc,bdhnkc->bdhnqk', qb, kb) * (Dh ** -0.5)
    qi = jnp.arange(BLK)[:, None]
    kj = jnp.arange(2 * BLK)[None, :]
    dist = qi + BLK - kj
    key_idx = jnp.arange(nb)[:, None, None] * BLK + kj[None] - BLK
    valid = (dist >= 0)[None] & (dist <= span)[None] & (key_idx >= 0)
    s = jnp.where(valid, s, -jnp.inf)
    m = jnp.max(s, axis=-1, keepdims=True)
    p = jnp.exp(s - m)
    l = jnp.sum(p, axis=-1, keepdims=True)
    o = jnp.einsum('bdhnqk,bdhnkc->bdhnqc', p, vb) / l
    lse = (m + jnp.log(l))[..., 0]
    o = o.reshape(B, dilation, H, Lp, Dh)[:, :, :, :L].transpose(0, 3, 1, 2, 4).reshape(B, S, H, Dh)
    lse = lse.reshape(B, dilation, H, Lp)[..., :L].transpose(0, 3, 1, 2).reshape(B, S, H)
    return o, lse


def _fwd_setup_inputs(seed: int = 0) -> dict:
    key = jax.random.key(seed)
    ks = jax.random.split(key, 8)
    x = jax.random.normal(ks[0], (BATCH, SEQ, D_MODEL), jnp.float32)
    norm_w = 1.0 + 0.02 * jax.random.normal(ks[1], (DEPTH, D_MODEL), jnp.float32)
    w_in = jax.random.normal(ks[2], (DEPTH, D_MODEL, IN_COLS), jnp.float32) * D_MODEL ** -0.5
    hgrn_lb_logits = 0.5 * jax.random.normal(ks[3], (DEPTH + 1, HG_HEADS * HG_DK), jnp.float32)
    hg_norm_w = 1.0 + 0.02 * jax.random.normal(ks[4], (DEPTH, HG_HEADS * HG_DV), jnp.float32)
    w_out = jax.random.normal(ks[5], (DEPTH, D_MIX, D_MODEL), jnp.float32) * D_MIX ** -0.5
    final_norm_w = 1.0 + 0.02 * jax.random.normal(ks[6], (D_MODEL,), jnp.float32)
    return {"x": x, "norm_w": norm_w, "w_in": w_in, "hgrn_lb_logits": hgrn_lb_logits,
            "hg_norm_w": hg_norm_w, "w_out": w_out, "final_norm_w": final_norm_w}


def _fwd_reference(x, norm_w, w_in, hgrn_lb_logits, hg_norm_w, w_out, final_norm_w):
    B, S, _ = x.shape
    positions = jnp.arange(S, dtype=jnp.int32)
    lb_table = jnp.cumsum(jax.nn.softmax(hgrn_lb_logits.astype(jnp.float32), axis=0), axis=0)
    split_points = list(np.cumsum(IN_SPLITS)[:-1])
    h = x
    for layer in range(DEPTH):
        u = rms_norm(h, norm_w[layer])
        proj = jnp.einsum('bsd,dc->bsc', u, w_in[layer].astype(jnp.float32))
        hg_q, hg_f, hg_i, hg_z, at_q, at_k, at_v, at_z = jnp.split(proj, split_points, axis=-1)

        lb = jnp.clip(lb_table[layer], 1e-6, 1.0 - 1e-6).reshape(HG_HEADS, HG_DK)
        o_hg = hgrn2_mixer(hg_q.reshape(B, S, HG_HEADS, HG_DK),
                           hg_f.reshape(B, S, HG_HEADS, HG_DK),
                           hg_i.reshape(B, S, HG_HEADS, HG_DV), lb)
        g = hg_norm_w[layer].astype(jnp.float32).reshape(HG_HEADS, HG_DV)
        o_hg = o_hg * lax.rsqrt(jnp.mean(o_hg * o_hg, axis=-1, keepdims=True) + NORM_EPS) * g
        o_hg = o_hg.reshape(B, S, HG_WIDTH) * jax.nn.silu(hg_z)

        q = rope(at_q.reshape(B, S, AT_HEADS, AT_HEAD_DIM), positions)
        k = rope(at_k.reshape(B, S, AT_HEADS, AT_HEAD_DIM), positions)
        v = at_v.reshape(B, S, AT_HEADS, AT_HEAD_DIM)
        outs, lses = [], []
        for window, dilation in DILATED_PATTERNS:
            o_i, lse_i = dilated_window_attention(q, k, v, window, dilation)
            outs.append(o_i)
            lses.append(lse_i)
        mix_w = jax.nn.softmax(jnp.stack(lses, axis=0), axis=0)[..., None]
        o_at = jnp.sum(mix_w * jnp.stack(outs, axis=0), axis=0)
        o_at = o_at.reshape(B, S, AT_WIDTH) * jax.nn.silu(at_z)

        mixed = jnp.concatenate([o_hg, o_at], axis=-1)
        y = jnp.einsum('bsc,cd->bsd', mixed, w_out[layer].astype(jnp.float32))
        h = h + y.astype(h.dtype)
    return rms_norm(h, final_norm_w).astype(x.dtype)


import jax as _jax
import jax.numpy as _jnp

TWIN_FORMAT = 'train_step'
FWD_PARAMS = ['x', 'norm_w', 'w_in', 'hgrn_lb_logits', 'hg_norm_w', 'w_out', 'final_norm_w']
TWIN_WEIGHTS = ['norm_w', 'w_in', 'hgrn_lb_logits', 'hg_norm_w', 'w_out', 'final_norm_w']
TWIN_DIFF_INPUT = 'x'
TWIN_INPUTS = ['x', 'norm_w', 'w_in', 'hgrn_lb_logits', 'hg_norm_w', 'w_out', 'final_norm_w', 'loss_target', 'm_norm_w', 'm_w_in', 'm_hgrn_lb_logits', 'm_hg_norm_w', 'm_w_out', 'm_final_norm_w', 'v_norm_w', 'v_w_in', 'v_hgrn_lb_logits', 'v_hg_norm_w', 'v_w_out', 'v_final_norm_w']
TWIN_OUTPUTS = ['loss', 'grad_x', 'grad_norm_w', 'grad_w_in', 'grad_hgrn_lb_logits', 'grad_hg_norm_w', 'grad_w_out', 'grad_final_norm_w', 'delta_norm_w', 'delta_w_in', 'delta_hgrn_lb_logits', 'delta_hg_norm_w', 'delta_w_out', 'delta_final_norm_w', 'new_m_norm_w', 'new_m_w_in', 'new_m_hgrn_lb_logits', 'new_m_hg_norm_w', 'new_m_w_out', 'new_m_final_norm_w', 'new_v_norm_w', 'new_v_w_in', 'new_v_hgrn_lb_logits', 'new_v_hg_norm_w', 'new_v_w_out', 'new_v_final_norm_w']
TWIN_LEAF_KINDS = {'loss': 'loss', 'grad_x': 'grad_x', 'grad_norm_w': 'grad_w', 'grad_w_in': 'grad_w', 'grad_hgrn_lb_logits': 'grad_w', 'grad_hg_norm_w': 'grad_w', 'grad_w_out': 'grad_w', 'grad_final_norm_w': 'grad_w', 'delta_norm_w': 'delta_w', 'delta_w_in': 'delta_w', 'delta_hgrn_lb_logits': 'delta_w', 'delta_hg_norm_w': 'delta_w', 'delta_w_out': 'delta_w', 'delta_final_norm_w': 'delta_w', 'new_m_norm_w': 'new_m', 'new_m_w_in': 'new_m', 'new_m_hgrn_lb_logits': 'new_m', 'new_m_hg_norm_w': 'new_m', 'new_m_w_out': 'new_m', 'new_m_final_norm_w': 'new_m', 'new_v_norm_w': 'new_v', 'new_v_w_in': 'new_v', 'new_v_hgrn_lb_logits': 'new_v', 'new_v_hg_norm_w': 'new_v', 'new_v_w_out': 'new_v', 'new_v_final_norm_w': 'new_v'}


def _forward(args):
    return _fwd_reference(*[args[k] for k in FWD_PARAMS])


def _output_shape():
    def fwd():
        inp = _fwd_setup_inputs(0)
        return _fwd_reference(*[inp[k] for k in FWD_PARAMS])
    out = _jax.eval_shape(fwd)
    return out.shape, out.dtype

N_MICROBATCH = 1
ADAM_LR = 0.001
ADAM_B1 = 0.9
ADAM_B2 = 0.999
ADAM_EPS = 1e-08
ADAM_WD = 0.01
ADAM_STEP = 10
PER_EXAMPLE_BATCH_AXIS = {'x': 0, 'loss_target': 0}
SHARED_INPUTS = []
_WEIGHT_DTYPES = {'norm_w': _jnp.float32, 'w_in': _jnp.float32, 'hgrn_lb_logits': _jnp.float32, 'hg_norm_w': _jnp.float32, 'w_out': _jnp.float32, 'final_norm_w': _jnp.float32}
MOMENT_SCALE = {'norm_w': 1.450089e-01, 'w_in': 7.296852e-02, 'hgrn_lb_logits': 1.182935e-02, 'hg_norm_w': 1.480042e-01, 'w_out': 9.884687e-02, 'final_norm_w': 6.401626e+01}


def _to_microbatches(a, axis):
    t = _jnp.moveaxis(a, axis, 0)
    t = t.reshape((N_MICROBATCH, t.shape[0] // N_MICROBATCH) + t.shape[1:])
    return _jnp.moveaxis(t, 1, axis + 1)


def setup_inputs(seed: int = 0) -> dict:
    inp = _fwd_setup_inputs(seed)
    key = _jax.random.fold_in(_jax.random.key(seed), 7919)
    shape, _ = _output_shape()
    out = dict(inp)
    out["loss_target"] = _jax.random.normal(_jax.random.fold_in(key, 0), shape, _jnp.float32)
    for i, name in enumerate(TWIN_WEIGHTS):
        w = inp[name].astype(_jnp.float32)
        if MOMENT_SCALE is None:
            s = _jnp.sqrt(_jnp.mean(_jnp.square(w)) + 1e-30)
        else:
            s = MOMENT_SCALE[name]
        km, kv = _jax.random.split(_jax.random.fold_in(key, i + 1))
        out[name] = w
        out["m_" + name] = s * _jax.random.normal(km, w.shape, _jnp.float32)
        out["v_" + name] = (s * s) * _jax.random.uniform(kv, w.shape, _jnp.float32, 0.5, 1.5)
    if N_MICROBATCH > 1:
        for name, axis in PER_EXAMPLE_BATCH_AXIS.items():
            out[name] = _to_microbatches(out[name], axis)
    return {'x': out['x'], 'norm_w': out['norm_w'], 'w_in': out['w_in'], 'hgrn_lb_logits': out['hgrn_lb_logits'], 'hg_norm_w': out['hg_norm_w'], 'w_out': out['w_out'], 'final_norm_w': out['final_norm_w'], 'loss_target': out['loss_target'], 'm_norm_w': out['m_norm_w'], 'm_w_in': out['m_w_in'], 'm_hgrn_lb_logits': out['m_hgrn_lb_logits'], 'm_hg_norm_w': out['m_hg_norm_w'], 'm_w_out': out['m_w_out'], 'm_final_norm_w': out['m_final_norm_w'], 'v_norm_w': out['v_norm_w'], 'v_w_in': out['v_w_in'], 'v_hgrn_lb_logits': out['v_hgrn_lb_logits'], 'v_hg_norm_w': out['v_hg_norm_w'], 'v_w_out': out['v_w_out'], 'v_final_norm_w': out['v_final_norm_w']}


def _loss(weights, diff, rest, loss_target):
    with _jax.named_scope("forward"):
        args = {**rest, TWIN_DIFF_INPUT: diff, **{k: w.astype(_WEIGHT_DTYPES[k]) for k, w in weights.items()}}
        y = _forward(args)
    with _jax.named_scope("loss_head"):
        err = _jnp.square(y.astype(_jnp.float32) - loss_target)
        return 0.5 * _jnp.sum(_jnp.mean(err, axis=-1)) if err.ndim else 0.5 * err


def _adamw(w, g, m, v):
    m = ADAM_B1 * m + (1.0 - ADAM_B1) * g
    v = ADAM_B2 * v + (1.0 - ADAM_B2) * _jnp.square(g)
    m_hat = m / (1.0 - ADAM_B1 ** ADAM_STEP)
    v_hat = v / (1.0 - ADAM_B2 ** ADAM_STEP)
    delta = -ADAM_LR * (m_hat / (_jnp.sqrt(v_hat) + ADAM_EPS) + ADAM_WD * w)
    return delta, m, v


def reference(x, norm_w, w_in, hgrn_lb_logits, hg_norm_w, w_out, final_norm_w, loss_target, m_norm_w, m_w_in, m_hgrn_lb_logits, m_hg_norm_w, m_w_out, m_final_norm_w, v_norm_w, v_w_in, v_hgrn_lb_logits, v_hg_norm_w, v_w_out, v_final_norm_w):
    given = dict(x=x, norm_w=norm_w, w_in=w_in, hgrn_lb_logits=hgrn_lb_logits, hg_norm_w=hg_norm_w, w_out=w_out, final_norm_w=final_norm_w, loss_target=loss_target, m_norm_w=m_norm_w, m_w_in=m_w_in, m_hgrn_lb_logits=m_hgrn_lb_logits, m_hg_norm_w=m_hg_norm_w, m_w_out=m_w_out, m_final_norm_w=m_final_norm_w, v_norm_w=v_norm_w, v_w_in=v_w_in, v_hgrn_lb_logits=v_hgrn_lb_logits, v_hg_norm_w=v_hg_norm_w, v_w_out=v_w_out, v_final_norm_w=v_final_norm_w)
    weights = {n: given[n] for n in TWIN_WEIGHTS}
    shared = {n: given[n] for n in SHARED_INPUTS}
    per_example = {n: given[n] for n in ['x']}
    grad_fn = _jax.value_and_grad(_loss, argnums=(0, 1))

    def one_microbatch(ex, loss_target):
        ex = dict(ex)
        diff = ex.pop(TWIN_DIFF_INPUT)
        return grad_fn(weights, diff, {**shared, **ex}, loss_target)

    if N_MICROBATCH == 1:
        loss, (grad_w, grad_x) = one_microbatch(per_example, given["loss_target"])
    else:
        def body(carry, xs):
            loss_sum, grad_sum = carry
            l_k, (gw_k, gx_k) = one_microbatch(xs[0], xs[1])
            with _jax.named_scope("update"):
                return (loss_sum + l_k, _jax.tree.map(_jnp.add, grad_sum, gw_k)), gx_k

        init = (_jnp.zeros((), _jnp.float32), _jax.tree.map(_jnp.zeros_like, weights))
        (loss, grad_w), grad_x = _jax.lax.scan(body, init, (per_example, given["loss_target"]))
    with _jax.named_scope("update"):
        delta_w, new_m, new_v = {}, {}, {}
        for n in TWIN_WEIGHTS:
            delta_w[n], new_m[n], new_v[n] = _adamw(weights[n], grad_w[n], given["m_" + n], given["v_" + n])
    return (loss, grad_x, *[grad_w[n] for n in TWIN_WEIGHTS], *[delta_w[n] for n in TWIN_WEIGHTS],
            *[new_m[n] for n in TWIN_WEIGHTS], *[new_v[n] for n in TWIN_WEIGHTS])
```

```python
import functools

import jax
import jax.numpy as jnp
from jax import lax
from jax.experimental import pallas as pl
from jax.experimental.pallas import tpu as pltpu

F32 = jnp.float32
BF16 = jnp.bfloat16
MESH = pl.DeviceIdType.MESH

D_MODEL = 1024
HG_W = 512
AT_W = 512
HEAD_PAIR = 128
ROPE_HALF = 32
ROPE_THETA = 10000.0
EPS = 1e-6
CHUNK = 128
LEVELS = (64, 32, 16, 8)
DIAG = 8
ATT_BLK = 128
DILATIONS = (1, 4, 16)
ATT_SCALE = 0.125
NEG = -1e30
VMEM_LIMIT = 56 * 1024 * 1024

ADAM_LR = 0.001
ADAM_B1 = 0.9
ADAM_B2 = 0.999
ADAM_EPS = 1e-08
ADAM_WD = 0.01
ADAM_STEP = 10


def _iota(shape, dim):
    return lax.broadcasted_iota(jnp.int32, shape, dim)


def _dot(a, b):
    return jnp.dot(a, b, preferred_element_type=F32)


def _dot_nt(a, b):
    return lax.dot_general(a, b, (((1,), (1,)), ((), ())), preferred_element_type=F32)


def _dot_hi(a, b):
    return jnp.dot(a, b, preferred_element_type=F32, precision=lax.Precision.HIGHEST)


def _sig(v):
    return 1.0 / (1.0 + jnp.exp(-v))


def _params(sem=None, vmem=VMEM_LIMIT):
    return pltpu.CompilerParams(dimension_semantics=sem, vmem_limit_bytes=vmem)


def _full(shape):
    n = len(shape)
    return pl.BlockSpec(shape, lambda *_: (0,) * n)


def _rope_rot(y):
    n = y.shape[1]
    first = (_iota(y.shape, 1) & (2 * ROPE_HALF - 1)) < ROPE_HALF
    return jnp.where(first, pltpu.roll(y, n - ROPE_HALF, 1), pltpu.roll(y, ROPE_HALF, 1))


def _inproj(x2, norm_w, w_in_full, cos2, sin2):
    S = x2.shape[0]
    tm = 256

    def body(x_ref, nw_ref, w_ref, cos_ref, sin_ref, p0_ref, p1_ref, qkv_ref, p3_ref, ut_ref):
        x = x_ref[...]
        r = lax.rsqrt(jnp.mean(x * x, axis=-1, keepdims=True) + EPS)
        u = x * r * nw_ref[...]
        ub = u.astype(BF16)
        ut_ref[...] = u.T.astype(BF16)
        p0_ref[...] = _dot(ub, w_ref[0])
        p1_ref[...] = _dot(ub, w_ref[1])
        y2 = _dot(ub, w_ref[2])
        cosf = jnp.tile(cos_ref[...], (1, 8))
        sinf = jnp.tile(sin_ref[...], (1, 8))
        qkv_ref[:, : 2 * AT_W] = (y2 * cosf + _rope_rot(y2) * sinf).astype(BF16)
        y3 = _dot(ub, w_ref[3])
        p3_ref[...] = y3
        qkv_ref[:, 2 * AT_W:] = y3[:, :AT_W].astype(BF16)

    row = lambda w: pl.BlockSpec((tm, w), lambda s: (s, 0))
    return pl.pallas_call(
        body, name="inproj", grid=(S // tm,),
        in_specs=[row(D_MODEL), _full((1, D_MODEL)), _full((4, D_MODEL, 1024)), row(128), row(128)],
        out_specs=[row(1024), row(1024), row(3 * AT_W), row(1024), pl.BlockSpec((D_MODEL, tm), lambda s: (0, s))],
        out_shape=[jax.ShapeDtypeStruct((S, 1024), F32), jax.ShapeDtypeStruct((S, 1024), F32),
                   jax.ShapeDtypeStruct((S, 3 * AT_W), BF16), jax.ShapeDtypeStruct((S, 1024), F32),
                   jax.ShapeDtypeStruct((D_MODEL, S), BF16)],
        compiler_params=_params(("parallel",)),
    )(x2, norm_w, w_in_full, cos2, sin2)


def _hg_consts():
    row = _iota((CHUNK, CHUNK), 0)
    col = _iota((CHUNK, CHUNK), 1)
    mats = [(row >= col).astype(F32)]
    masks, masks_t = [], []
    for m in LEVELS:
        blk = -2 * m
        mats.append((col <= (row & blk) + (m - 1)).astype(F32))
        same = (row & blk) == (col & blk)
        masks.append(same & ((row & (2 * m - 1)) >= m) & ((col & (2 * m - 1)) < m))
        masks_t.append(same & ((col & (2 * m - 1)) >= m) & ((row & (2 * m - 1)) < m))
    return jnp.concatenate(mats, axis=0), masks, masks_t, row, col


def _hg_lb(lbl_ref):
    l0 = lbl_ref[0:1, :]
    l1 = lbl_ref[1:2, :]
    mx = jnp.maximum(l0, l1)
    e0 = jnp.exp(l0 - mx)
    e1 = jnp.exp(l1 - mx)
    p0 = e0 / (e0 + e1)
    lb = jnp.clip(p0, 1e-6, 1.0 - 1e-6)
    inside = (p0 >= 1e-6) & (p0 <= 1.0 - 1e-6)
    dlb_dl0 = jnp.where(inside, p0 * (e1 / (e0 + e1)), 0.0)
    return lb, dlb_dl0


def _hg_gates(qr, fl, lb):
    sig = _sig(fl)
    f = lb + (1.0 - lb) * sig
    g = jnp.log(f)
    k = (1.0 - lb) * _sig(-fl)
    sq = _sig(qr)
    q = qr * sq
    return sig, f, g, k, sq, q


def _hg_levels(q, k, b, refs, masks):
    a = jnp.zeros((CHUNK, CHUNK), F32)
    eqs, eks, qts, kts = [], [], [], []
    for i in range(len(LEVELS)):
        eq = jnp.exp(jnp.minimum(b - refs[i], 0.0))
        ek = jnp.exp(jnp.minimum(refs[i] - b, 0.0))
        qt = (q * eq).astype(BF16)
        kt = (k * ek).astype(BF16)
        a = a + jnp.where(masks[i], _dot_nt(qt, kt), 0.0)
        eqs.append(eq); eks.append(ek); qts.append(qt); kts.append(kt)
    return a, eqs, eks, qts, kts


def _hg_fwd(p0, p1, lbl, gw):
    S = p0.shape[0]
    nc = S // CHUNK

    def body(qr_ref, fl_ref, v_ref, z_ref, lbl_ref, gw_ref, o_ref, mix_ref, mixt_ref, st_ref, state):
        c = pl.program_id(1)

        @pl.when(c == 0)
        def _():
            state[...] = jnp.zeros_like(state)

        mstack, masks, _, row, col = _hg_consts()
        lb, _ = _hg_lb(lbl_ref)
        qr, fl, v = qr_ref[...], fl_ref[...], v_ref[...]
        _, _, g, k, _, q = _hg_gates(qr, fl, lb)
        br = _dot_hi(mstack, g)
        b = br[:CHUNK]
        refs = [br[CHUNK * (i + 1): CHUNK * (i + 2)] for i in range(len(LEVELS))]
        b_last = b[CHUNK - 1: CHUNK, :]
        a, _, _, _, _ = _hg_levels(q, k, b, refs, masks)
        for dl in range(DIAG):
            if dl == 0:
                term = q * k
            else:
                e = jnp.exp(jnp.minimum(b - pltpu.roll(b, dl, 0), 0.0))
                term = q * pltpu.roll(k, dl, 0) * e
            cdl = jnp.sum(term, axis=-1, keepdims=True)
            a = a + jnp.where((col == row - dl) & ((row & (DIAG - 1)) >= dl), cdl, 0.0)
        st = state[...]
        st_ref[0, 0] = st
        vb = v.astype(BF16)
        o = _dot_nt((q * jnp.exp(b)).astype(BF16), st.astype(BF16)) + _dot(a.astype(BF16), vb)
        state[...] = st * jnp.exp(b_last) + _dot_hi(v.T, k * jnp.exp(b_last - b))
        o_ref[...] = o
        rs = lax.rsqrt(jnp.mean(o * o, axis=-1, keepdims=True) + EPS)
        z = z_ref[...]
        mixed = o * rs * gw_ref[...] * (z * _sig(z))
        mix_ref[...] = mixed.astype(BF16)
        mixt_ref[...] = mixed.T.astype(BF16)

    blk = lambda off: pl.BlockSpec((CHUNK, 128), lambda h, c: (c, h + off))
    vec = pl.BlockSpec((1, 128), lambda h, c: (0, h))
    return pl.pallas_call(
        body, name="hg_fwd", grid=(4, nc),
        in_specs=[blk(0), blk(4), blk(0), blk(4), pl.BlockSpec((2, 128), lambda h, c: (0, h)), vec],
        out_specs=[blk(0), blk(0), pl.BlockSpec((128, CHUNK), lambda h, c: (h, c)),
                   pl.BlockSpec((1, 1, 128, 128), lambda h, c: (c, h, 0, 0))],
        out_shape=[jax.ShapeDtypeStruct((S, HG_W), F32), jax.ShapeDtypeStruct((S, HG_W), BF16),
                   jax.ShapeDtypeStruct((HG_W, S), BF16), jax.ShapeDtypeStruct((nc, 4, 128, 128), F32)],
        scratch_shapes=[pltpu.VMEM((128, 128), F32)],
        compiler_params=_params(("parallel", "arbitrary")),
    )(p0, p0, p1, p1, lbl, gw)


def _hg_bwd(p0, p1, o_raw, dm, states, lbl, gw):
    S = p0.shape[0]
    nc = S // CHUNK

    def body(qr_ref, fl_ref, v_ref, z_ref, o_ref, dm_ref, st_ref, lbl_ref, gw_ref,
             dqr_ref, dfl_ref, dv_ref, dz_ref, glbl_ref, ggw_ref, dstate, carry, acc_lb, acc_gw):
        c = pl.program_id(1)

        @pl.when(c == 0)
        def _():
            dstate[...] = jnp.zeros_like(dstate)
            carry[...] = jnp.zeros_like(carry)
            acc_lb[...] = jnp.zeros_like(acc_lb)
            acc_gw[...] = jnp.zeros_like(acc_gw)

        mstack, masks, masks_t, row, col = _hg_consts()
        lb, dlb_dl0 = _hg_lb(lbl_ref)
        qr, fl, v = qr_ref[...], fl_ref[...], v_ref[...]
        sig, f, g, k, sq, q = _hg_gates(qr, fl, lb)
        br = _dot_hi(mstack, g)
        b = br[:CHUNK]
        refs = [br[CHUNK * (i + 1): CHUNK * (i + 2)] for i in range(len(LEVELS))]
        b_last = b[CHUNK - 1: CHUNK, :]
        a, eqs, eks, qts, kts = _hg_levels(q, k, b, refs, masks)

        o = o_ref[...]
        z = z_ref[...]
        dmix = dm_ref[...]
        gwv = gw_ref[...]
        rs = lax.rsqrt(jnp.mean(o * o, axis=-1, keepdims=True) + EPS)
        oh = o * rs
        sz = _sig(z)
        silu_z = z * sz
        dz_ref[...] = dmix * (oh * gwv) * (sz * (1.0 + z * (1.0 - sz)))
        don = dmix * silu_z
        acc_gw[0:1, :] += jnp.sum(don * oh, axis=0, keepdims=True)
        dy = don * gwv
        do = rs * (dy - oh * jnp.mean(dy * oh, axis=-1, keepdims=True))

        st = st_ref[0, 0]
        dst = dstate[...]
        dob = do.astype(BF16)
        vb = v.astype(BF16)
        eb = jnp.exp(b)
        edec = jnp.exp(b_last - b)
        qe = q * eb
        kdec = k * edec
        dq = _dot_hi(do, st) * eb
        dk = _dot_hi(v, dst) * edec
        db = q * dq - k * dk
        da = _dot_nt(dob, vb)
        da_t = da.T
        for i in range(len(LEVELS)):
            gq = _dot(jnp.where(masks[i], da, 0.0).astype(BF16), kts[i])
            gk = _dot(jnp.where(masks_t[i], da_t, 0.0).astype(BF16), qts[i])
            dq = dq + eqs[i] * gq
            dk = dk + eks[i] * gk
            db = db + (qts[i].astype(F32) * gq - kts[i].astype(F32) * gk)
        for dl in range(DIAG):
            dmask = (col == row - dl) & ((row & (DIAG - 1)) >= dl)
            dc = jnp.sum(jnp.where(dmask, da, 0.0), axis=-1, keepdims=True)
            if dl == 0:
                a = a + jnp.where(dmask, jnp.sum(q * k, axis=-1, keepdims=True), 0.0)
                dq = dq + dc * k
                dk = dk + dc * q
            else:
                e = jnp.exp(jnp.minimum(b - pltpu.roll(b, dl, 0), 0.0))
                ks = pltpu.roll(k, dl, 0)
                a = a + jnp.where(dmask, jnp.sum(q * ks * e, axis=-1, keepdims=True), 0.0)
                tq = dc * ks * e
                tk = pltpu.roll(dc * q * e, CHUNK - dl, 0)
                dq = dq + tq
                dk = dk + tk
                db = db + (q * tq - k * tk)
        dv_ref[...] = _dot(a.T.astype(BF16), dob) + _dot_nt(kdec.astype(BF16), dst.astype(BF16))
        dstate[...] = dst * jnp.exp(b_last) + _dot_hi(do.T, qe)

        upper = (row <= col).astype(F32)
        dg = _dot_hi(upper, db) + carry[0:1, :]
        carry[0:1, :] += jnp.sum(db, axis=0, keepdims=True)
        t = dg / f - dk
        dfl_ref[...] = t * (1.0 - lb) * sig * (1.0 - sig)
        acc_lb[0:1, :] += jnp.sum(t * (1.0 - sig), axis=0, keepdims=True)
        dqr_ref[...] = dq * (sq * (1.0 + qr * (1.0 - sq)))

        @pl.when(c == nc - 1)
        def _():
            gl0 = acc_lb[0:1, :] * dlb_dl0
            glbl_ref[0:1, :] = gl0
            glbl_ref[1:2, :] = -gl0
            ggw_ref[...] = acc_gw[0:1, :]

    blk = lambda off: pl.BlockSpec((CHUNK, 128), lambda h, c: (nc - 1 - c, h + off))
    vec = pl.BlockSpec((1, 128), lambda h, c: (0, h))
    lb2 = pl.BlockSpec((2, 128), lambda h, c: (0, h))
    act = jax.ShapeDtypeStruct((S, HG_W), F32)
    return pl.pallas_call(
        body, name="hg_bwd", grid=(4, nc),
        in_specs=[blk(0), blk(4), blk(0), blk(4), blk(0), blk(0),
                  pl.BlockSpec((1, 1, 128, 128), lambda h, c: (nc - 1 - c, h, 0, 0)), lb2, vec],
        out_specs=[blk(0), blk(0), blk(0), blk(0), lb2, vec],
        out_shape=[act, act, act, act, jax.ShapeDtypeStruct((2, HG_W), F32), jax.ShapeDtypeStruct((1, HG_W), F32)],
        scratch_shapes=[pltpu.VMEM((128, 128), F32), pltpu.VMEM((8, 128), F32),
                        pltpu.VMEM((8, 128), F32), pltpu.VMEM((8, 128), F32)],
        compiler_params=_params(("parallel", "arbitrary")),
    )(p0, p0, p1, p1, o_raw, dm, states, lbl, gw)


def _to_dil(a, d):
    S, C = a.shape
    return a.reshape(S // d, d, C).transpose(1, 0, 2)


def _from_dil(a):
    d, L, C = a.shape
    return a.transpose(1, 0, 2).reshape(d * L, C)


def _att_fwd(qkv_d):
    d, L, _ = qkv_d.shape
    nb = L // ATT_BLK

    def body(q_ref, kp_ref, kc_ref, vp_ref, vc_ref, ol_ref):
        n = pl.program_id(1)
        qi = _iota((ATT_BLK, 2 * ATT_BLK), 0)
        kj = _iota((ATT_BLK, 2 * ATT_BLK), 1)
        valid = (kj >= qi) & (kj <= qi + ATT_BLK) & ((kj >= ATT_BLK) | (n > 0))
        first = _iota((ATT_BLK, HEAD_PAIR), 1) < 64
        for hp in range(4):
            sl = slice(HEAD_PAIR * hp, HEAD_PAIR * (hp + 1))
            q2 = q_ref[0, :, sl]
            kcat = jnp.concatenate([kp_ref[0, :, sl], kc_ref[0, :, sl]], axis=0)
            vcat = jnp.concatenate([vp_ref[0, :, sl], vc_ref[0, :, sl]], axis=0)
            res = []
            for hm in (first, jnp.logical_not(first)):
                qm = jnp.where(hm, q2, jnp.zeros_like(q2))
                s = jnp.where(valid, _dot_nt(qm, kcat) * ATT_SCALE, NEG)
                m = jnp.max(s, axis=-1, keepdims=True)
                p = jnp.exp(s - m)
                l = jnp.sum(p, axis=-1, keepdims=True)
                res.append((_dot(p.astype(BF16), vcat) / l, m + jnp.log(l)))
            ol_ref[0, :, sl] = jnp.where(first, res[0][0], res[1][0])
            ol_ref[0, :, AT_W + HEAD_PAIR * hp: AT_W + HEAD_PAIR * (hp + 1)] = jnp.where(first, res[0][1], res[1][1])

    cur = lambda j: pl.BlockSpec((1, ATT_BLK, AT_W), lambda r, n: (r, n, j))
    prev = lambda j: pl.BlockSpec((1, ATT_BLK, AT_W), lambda r, n: (r, jnp.maximum(n - 1, 0), j))
    return pl.pallas_call(
        body, name=f"att_fwd_d{d}", grid=(d, nb),
        in_specs=[cur(0), prev(1), cur(1), prev(2), cur(2)],
        out_specs=pl.BlockSpec((1, ATT_BLK, 2 * AT_W), lambda r, n: (r, n, 0)),
        out_shape=jax.ShapeDtypeStruct((d, L, 2 * AT_W), F32),
        compiler_params=_params(("parallel", "parallel")),
    )(qkv_d, qkv_d, qkv_d, qkv_d, qkv_d)


def _att_combine(ols, p3):
    S = p3.shape[0]
    tm = 512

    def body(a_ref, b_ref, c_ref, z_ref, ol_ref, mix_ref, mixt_ref):
        os_ = [r[:, :AT_W] for r in (a_ref, b_ref, c_ref)]
        ls = [r[:, AT_W:] for r in (a_ref, b_ref, c_ref)]
        mx = jnp.maximum(jnp.maximum(ls[0], ls[1]), ls[2])
        es = [jnp.exp(l - mx) for l in ls]
        zs = es[0] + es[1] + es[2]
        o = (es[0] * os_[0] + es[1] * os_[1] + es[2] * os_[2]) / zs
        ol_ref[:, :AT_W] = o
        ol_ref[:, AT_W:] = mx + jnp.log(zs)
        z = z_ref[...]
        mixed = o * (z * _sig(z))
        mix_ref[...] = mixed.astype(BF16)
        mixt_ref[...] = mixed.T.astype(BF16)

    row = lambda w: pl.BlockSpec((tm, w), lambda s: (s, 0))
    return pl.pallas_call(
        body, name="att_combine", grid=(S // tm,),
        in_specs=[row(1024), row(1024), row(1024), pl.BlockSpec((tm, AT_W), lambda s: (s, 1))],
        out_specs=[row(1024), row(AT_W), pl.BlockSpec((AT_W, tm), lambda s: (0, s))],
        out_shape=[jax.ShapeDtypeStruct((S, 1024), F32), jax.ShapeDtypeStruct((S, AT_W), BF16),
                   jax.ShapeDtypeStruct((AT_W, S), BF16)],
        compiler_params=_params(("parallel",)),
    )(*ols, p3)


def _att_gate_bwd(dm_at, ol, p3):
    S = p3.shape[0]
    tm = 512

    def body(dm_ref, ol_ref, z_ref, aux_ref, dz_ref):
        o = ol_ref[:, :AT_W]
        z = z_ref[...]
        dm = dm_ref[...]
        sz = _sig(z)
        dz_ref[...] = dm * o * (sz * (1.0 + z * (1.0 - sz)))
        do = dm * (z * sz)
        aux_ref[:, :AT_W] = do
        ones = ((_iota((HEAD_PAIR, HEAD_PAIR), 0) < 64) == (_iota((HEAD_PAIR, HEAD_PAIR), 1) < 64)).astype(F32)
        prod = do * o
        for hp in range(4):
            sl = slice(HEAD_PAIR * hp, HEAD_PAIR * (hp + 1))
            aux_ref[:, AT_W + HEAD_PAIR * hp: AT_W + HEAD_PAIR * (hp + 1)] = _dot_hi(prod[:, sl], ones)
        aux_ref[:, 2 * AT_W:] = ol_ref[:, AT_W:]

    row = lambda w: pl.BlockSpec((tm, w), lambda s: (s, 0))
    return pl.pallas_call(
        body, name="att_gate_bwd", grid=(S // tm,),
        in_specs=[row(AT_W), row(1024), pl.BlockSpec((tm, AT_W), lambda s: (s, 1))],
        out_specs=[row(3 * AT_W), row(AT_W)],
        out_shape=[jax.ShapeDtypeStruct((S, 3 * AT_W), F32), jax.ShapeDtypeStruct((S, AT_W), F32)],
        compiler_params=_params(("parallel",)),
    )(dm_at, ol, p3)


def _att_bwd(qkv_d, aux_d):
    d, L, _ = qkv_d.shape
    nb = L // ATT_BLK

    def body(qc_ref, qn_ref, kp_ref, kc_ref, vp_ref, vc_ref, ac_ref, an_ref, dqkv_ref):
        n = pl.program_id(1)
        qi = _iota((ATT_BLK, 2 * ATT_BLK), 0)
        kj = _iota((ATT_BLK, 2 * ATT_BLK), 1)
        valid = (kj >= qi) & (kj <= qi + ATT_BLK) & ((kj >= ATT_BLK) | (n > 0))
        valid_t = ((kj < ATT_BLK) & (qi <= kj)) | ((kj >= ATT_BLK) & (qi >= kj - ATT_BLK) & (n < nb - 1))
        first = _iota((ATT_BLK, HEAD_PAIR), 1) < 64
        for hp in range(4):
            sl = slice(HEAD_PAIR * hp, HEAD_PAIR * (hp + 1))
            sl_d = slice(AT_W + HEAD_PAIR * hp, AT_W + HEAD_PAIR * (hp + 1))
            sl_l = slice(2 * AT_W + HEAD_PAIR * hp, 2 * AT_W + HEAD_PAIR * (hp + 1))
            q2 = qc_ref[0, :, sl]
            k2 = kc_ref[0, :, sl]
            v2 = vc_ref[0, :, sl]
            kcat = jnp.concatenate([kp_ref[0, :, sl], k2], axis=0)
            vcat = jnp.concatenate([vp_ref[0, :, sl], v2], axis=0)
            qcat = jnp.concatenate([q2, qn_ref[0, :, sl]], axis=0)
            do_c = ac_ref[0, :, sl]
            docat = jnp.concatenate([do_c, an_ref[0, :, sl]], axis=0).astype(BF16)
            dd_c = ac_ref[0, :, sl_d]
            ls_c = ac_ref[0, :, sl_l]
            dd_t = jnp.concatenate([dd_c, an_ref[0, :, sl_d]], axis=0).T
            ls_t = jnp.concatenate([ls_c, an_ref[0, :, sl_l]], axis=0).T
            do_cb = do_c.astype(BF16)
            dqs, dks, dvs = [], [], []
            for hh, hm in enumerate((first, jnp.logical_not(first))):
                lane0 = 64 * hh
                qm = jnp.where(hm, q2, jnp.zeros_like(q2))
                s = jnp.where(valid, _dot_nt(qm, kcat) * ATT_SCALE, NEG)
                p = jnp.exp(s - ls_c[:, lane0: lane0 + 1])
                dp = _dot_nt(jnp.where(hm, do_cb, jnp.zeros_like(do_cb)), vcat)
                ds = p * (dp - dd_c[:, lane0: lane0 + 1])
                dqs.append(_dot(ds.astype(BF16), kcat) * ATT_SCALE)
                km = jnp.where(hm, k2, jnp.zeros_like(k2))
                vm = jnp.where(hm, v2, jnp.zeros_like(v2))
                st = jnp.where(valid_t, _dot_nt(km, qcat) * ATT_SCALE, NEG)
                pt = jnp.exp(st - ls_t[lane0: lane0 + 1, :])
                dpt = _dot_nt(vm, docat)
                dst = pt * (dpt - dd_t[lane0: lane0 + 1, :])
                dvs.append(_dot(pt.astype(BF16), docat))
                dks.append(_dot(dst.astype(BF16), qcat) * ATT_SCALE)
            dqkv_ref[0, :, sl] = jnp.where(first, dqs[0], dqs[1])
            dqkv_ref[0, :, AT_W + HEAD_PAIR * hp: AT_W + HEAD_PAIR * (hp + 1)] = jnp.where(first, dks[0], dks[1])
            dqkv_ref[0, :, 2 * AT_W + HEAD_PAIR * hp: 2 * AT_W + HEAD_PAIR * (hp + 1)] = jnp.where(first, dvs[0], dvs[1])

    cur = lambda j: pl.BlockSpec((1, ATT_BLK, AT_W), lambda r, n: (r, n, j))
    prev = lambda j: pl.BlockSpec((1, ATT_BLK, AT_W), lambda r, n: (r, jnp.maximum(n - 1, 0), j))
    nxt = lambda j: pl.BlockSpec((1, ATT_BLK, AT_W), lambda r, n: (r, jnp.minimum(n + 1, nb - 1), j))
    return pl.pallas_call(
        body, name=f"att_bwd_d{d}", grid=(d, nb),
        in_specs=[cur(0), nxt(0), prev(1), cur(1), prev(2), cur(2),
                  pl.BlockSpec((1, ATT_BLK, 3 * AT_W), lambda r, n: (r, n, 0)),
                  pl.BlockSpec((1, ATT_BLK, 3 * AT_W), lambda r, n: (r, jnp.minimum(n + 1, nb - 1), 0))],
        out_specs=pl.BlockSpec((1, ATT_BLK, 3 * AT_W), lambda r, n: (r, n, 0)),
        out_shape=jax.ShapeDtypeStruct((d, L, 3 * AT_W), F32),
        compiler_params=_params(("parallel", "parallel")),
    )(qkv_d, qkv_d, qkv_d, qkv_d, qkv_d, qkv_d, aux_d, aux_d)


def _att_bwd_combine(dqkvs, cos2, sin2):
    S = dqkvs[0].shape[0]
    tm = 512

    def body(a_ref, b_ref, c_ref, cos_ref, sin_ref, dq_ref, dk_ref, dv_ref):
        t = a_ref[...] + b_ref[...] + c_ref[...]
        dy = t[:, : 2 * AT_W]
        cosf = jnp.tile(cos_ref[...], (1, 8))
        sinf = jnp.tile(sin_ref[...], (1, 8))
        dx = dy * cosf - _rope_rot(dy) * sinf
        dq_ref[...] = dx[:, :AT_W]
        dk_ref[...] = dx[:, AT_W:]
        dv_ref[...] = t[:, 2 * AT_W:]

    row = lambda w: pl.BlockSpec((tm, w), lambda s: (s, 0))
    act = jax.ShapeDtypeStruct((S, AT_W), F32)
    return pl.pallas_call(
        body, name="att_bwd_combine", grid=(S // tm,),
        in_specs=[row(3 * AT_W), row(3 * AT_W), row(3 * AT_W), row(128), row(128)],
        out_specs=[row(AT_W), row(AT_W), row(AT_W)],
        out_shape=[act, act, act],
        compiler_params=_params(("parallel",)),
    )(*dqkvs, cos2, sin2)


def _outproj(x2, tgt2, mix_hg, mix_at, mixt_hg, mixt_at, w_out_full, fnw):
    S = x2.shape[0]
    tm = 256
    ns = S // tm

    def body(x_ref, t_ref, mh_ref, ma_ref, mht_ref, mat_ref, w_ref, fw_ref,
             dh_ref, dmh_ref, dma_ref, gw_ref, gfw_ref, loss_ref):
        s = pl.program_id(0)

        @pl.when(s == 0)
        def _():
            gw_ref[...] = jnp.zeros_like(gw_ref)
            gfw_ref[...] = jnp.zeros_like(gfw_ref)
            loss_ref[...] = jnp.zeros_like(loss_ref)

        y = _dot(mh_ref[...], w_ref[:HG_W, :]) + _dot(ma_ref[...], w_ref[HG_W:, :])
        h = x_ref[...] + y
        r = lax.rsqrt(jnp.mean(h * h, axis=-1, keepdims=True) + EPS)
        hn = h * r
        fw = fw_ref[...]
        err = hn * fw - t_ref[...]
        loss_ref[...] += 0.5 * jnp.sum(jnp.mean(err * err, axis=-1, keepdims=True))
        dout = err * (1.0 / D_MODEL)
        gfw_ref[...] += jnp.sum(dout * hn, axis=0, keepdims=True)
        dhn = dout * fw
        dh = r * (dhn - hn * jnp.mean(dhn * hn, axis=-1, keepdims=True))
        dh_ref[...] = dh
        dhb = dh.astype(BF16)
        dmh_ref[...] = _dot_nt(dhb, w_ref[:HG_W, :])
        dma_ref[...] = _dot_nt(dhb, w_ref[HG_W:, :])
        gw_ref[:HG_W, :] += _dot(mht_ref[...], dhb)
        gw_ref[HG_W:, :] += _dot(mat_ref[...], dhb)

    row = lambda w: pl.BlockSpec((tm, w), lambda s: (s, 0))
    colb = pl.BlockSpec((HG_W, tm), lambda s: (0, s))
    return pl.pallas_call(
        body, name="outproj", grid=(ns,),
        in_specs=[row(D_MODEL), row(D_MODEL), row(HG_W), row(AT_W), colb, colb,
                  _full((D_MODEL, D_MODEL)), _full((1, D_MODEL))],
        out_specs=[row(D_MODEL), row(HG_W), row(AT_W), _full((D_MODEL, D_MODEL)), _full((1, D_MODEL)), _full((8, 128))],
        out_shape=[jax.ShapeDtypeStruct((S, D_MODEL), F32), jax.ShapeDtypeStruct((S, HG_W), F32),
                   jax.ShapeDtypeStruct((S, AT_W), F32), jax.ShapeDtypeStruct((D_MODEL, D_MODEL), F32),
                   jax.ShapeDtypeStruct((1, D_MODEL), F32), jax.ShapeDtypeStruct((8, 128), F32)],
        compiler_params=_params(("arbitrary",)),
    )(x2, tgt2, mix_hg, mix_at, mixt_hg, mixt_at, w_out_full, fnw)


def _inproj_bwd_x(dps, w_in_full, x2, norm_w, dh):
    S = x2.shape[0]
    tm = 256

    def body(d0, d1, d2, d3, d4, d5, d6, d7, w_ref, x_ref, nw_ref, dh_ref, gx_ref, gnw_ref):
        s = pl.program_id(0)

        @pl.when(s == 0)
        def _():
            gnw_ref[...] = jnp.zeros_like(gnw_ref)

        du = jnp.zeros((tm, D_MODEL), F32)
        for i, dref in enumerate((d0, d1, d2, d3, d4, d5, d6, d7)):
            j, half = divmod(i, 2)
            du = du + _dot_nt(dref[...].astype(BF16), w_ref[j, :, 512 * half: 512 * (half + 1)])
        x = x_ref[...]
        r = lax.rsqrt(jnp.mean(x * x, axis=-1, keepdims=True) + EPS)
        xh = x * r
        gnw_ref[...] += jnp.sum(du * xh, axis=0, keepdims=True)
        dun = du * nw_ref[...]
        gx_ref[...] = dh_ref[...] + r * (dun - xh * jnp.mean(dun * xh, axis=-1, keepdims=True))

    row = lambda w: pl.BlockSpec((tm, w), lambda s: (s, 0))
    return pl.pallas_call(
        body, name="inproj_bwd_x", grid=(S // tm,),
        in_specs=[row(512)] * 8 + [_full((4, D_MODEL, 1024)), row(D_MODEL), _full((1, D_MODEL)), row(D_MODEL)],
        out_specs=[row(D_MODEL), _full((1, D_MODEL))],
        out_shape=[jax.ShapeDtypeStruct((S, D_MODEL), F32), jax.ShapeDtypeStruct((1, D_MODEL), F32)],
        compiler_params=_params(("arbitrary",)),
    )(*dps, w_in_full, x2, norm_w, dh)


def _inproj_bwd_w(ut, dps):
    S = ut.shape[1]
    tm = 512
    outs = []
    for i, dp in enumerate(dps):
        def body(ut_ref, dp_ref, g_ref):
            @pl.when(pl.program_id(0) == 0)
            def _():
                g_ref[...] = jnp.zeros_like(g_ref)
            g_ref[...] += _dot(ut_ref[...], dp_ref[...].astype(BF16))

        outs.append(pl.pallas_call(
            body, name=f"inproj_bwd_w{i}", grid=(S // tm,),
            in_specs=[pl.BlockSpec((D_MODEL, tm), lambda s: (0, s)), pl.BlockSpec((tm, 512), lambda s: (s, 0))],
            out_specs=_full((D_MODEL, 512)),
            out_shape=jax.ShapeDtypeStruct((D_MODEL, 512), F32),
            compiler_params=_params(("arbitrary",)),
        )(ut, dp))
    return outs


def _adamw(w, g, m, v, name):
    rows, cols = w.shape
    tr = min(rows, 256)

    def body(w_ref, g_ref, m_ref, v_ref, d_ref, nm_ref, nv_ref):
        gg = g_ref[...]
        nm = ADAM_B1 * m_ref[...] + (1.0 - ADAM_B1) * gg
        nv = ADAM_B2 * v_ref[...] + (1.0 - ADAM_B2) * (gg * gg)
        m_hat = nm / (1.0 - ADAM_B1 ** ADAM_STEP)
        v_hat = nv / (1.0 - ADAM_B2 ** ADAM_STEP)
        d_ref[...] = -ADAM_LR * (m_hat / (jnp.sqrt(v_hat) + ADAM_EPS) + ADAM_WD * w_ref[...])
        nm_ref[...] = nm
        nv_ref[...] = nv

    spec = pl.BlockSpec((tr, cols), lambda i: (i, 0))
    sds = jax.ShapeDtypeStruct((rows, cols), F32)
    return pl.pallas_call(
        body, name=name, grid=(rows // tr,),
        in_specs=[spec] * 4, out_specs=[spec] * 3, out_shape=[sds] * 3,
        compiler_params=_params(("parallel",)),
    )(w, g, m, v)


def _place():
    return lax.axis_index("x"), lax.axis_index("y"), lax.axis_index("c")


def _gather_weights(w_in_s, w_out_s):
    def body(win_ref, wout_ref, fin_ref, fout_ref, send_sems, recv_sems):
        x, y, c = _place()
        me = (x, y, c)
        sib = (x, y, 1 - c)
        mine = 2 * x + y
        fin_ref[mine] = win_ref[...].astype(BF16)
        fout_ref[mine] = wout_ref[...].astype(BF16)
        chips = [(1 - x, y), (x, 1 - y), (1 - x, 1 - y)]

        def halves(chip, half):
            return (fin_ref.at[chip, pl.ds(half * 512, 512), :], fout_ref.at[chip, pl.ds(half * 128, 128), :])

        def copy(k, ref, to):
            return pltpu.make_async_remote_copy(src_ref=ref, dst_ref=ref, send_sem=send_sems.at[k],
                                                recv_sem=recv_sems.at[k], device_id=to, device_id_type=MESH)

        first, passed = [], []
        for j, (cx, cy) in enumerate(chips):
            for a, ref in enumerate(halves(mine, c)):
                first.append(copy(2 * j + a, ref, (cx, cy, c)))
        for cp in first:
            cp.start()
        for j, (cx, cy) in enumerate(chips):
            for a, ref in enumerate(halves(2 * cx + cy, c)):
                copy(2 * j + a, ref, me).wait_recv()
                fwd = copy(6 + 2 * j + a, ref, sib)
                fwd.start()
                passed.append(fwd)
        for j, (cx, cy) in enumerate(chips):
            for a, ref in enumerate(halves(2 * cx + cy, 1 - c)):
                copy(6 + 2 * j + a, ref, me).wait_recv()
        for cp in first + passed:
            cp.wait_send()

    vm = pl.BlockSpec(memory_space=pltpu.VMEM)
    return pl.pallas_call(
        body, name="gather_weights",
        in_specs=[vm, vm], out_specs=[vm, vm],
        out_shape=[jax.ShapeDtypeStruct((4, D_MODEL, 1024), BF16), jax.ShapeDtypeStruct((4, 256, D_MODEL), BF16)],
        scratch_shapes=[pltpu.SemaphoreType.DMA((12,)), pltpu.SemaphoreType.DMA((12,))],
        compiler_params=pltpu.CompilerParams(vmem_limit_bytes=VMEM_LIMIT),
    )(w_in_s, w_out_s)


def _swap_halves(g_in, g_out):
    def body(gin_ref, gout_ref, rin_ref, rout_ref, send_sems, recv_sems):
        x, y, c = _place()
        sib = (x, y, 1 - c)
        cps = [pltpu.make_async_remote_copy(src_ref=src.at[:, 1 - c], dst_ref=dst, send_sem=send_sems.at[k],
                                            recv_sem=recv_sems.at[k], device_id=sib, device_id_type=MESH)
               for k, (src, dst) in enumerate(((gin_ref, rin_ref), (gout_ref, rout_ref)))]
        for cp in cps:
            cp.start()
        for cp in cps:
            cp.wait()

    hbm = pl.BlockSpec(memory_space=pl.ANY)
    return pl.pallas_call(
        body, name="swap_halves",
        in_specs=[hbm, hbm], out_specs=[hbm, hbm],
        out_shape=[jax.ShapeDtypeStruct((4,) + g.shape[2:], F32) for g in (g_in, g_out)],
        scratch_shapes=[pltpu.SemaphoreType.DMA((2,)), pltpu.SemaphoreType.DMA((2,))],
    )(g_in, g_out)


def _add_half(g, r, cidx, name):
    n, _, rows, cols = g.shape

    def body(c_ref, g_ref, r_ref, o_ref):
        o_ref[0] = g_ref[0, 0] + r_ref[0]

    return pl.pallas_call(
        body, name=name,
        grid_spec=pltpu.PrefetchScalarGridSpec(
            num_scalar_prefetch=1, grid=(n,),
            in_specs=[pl.BlockSpec((1, 1, rows, cols), lambda j, c_ref: (j, c_ref[0], 0, 0)),
                      pl.BlockSpec((1, rows, cols), lambda j, c_ref: (j, 0, 0))],
            out_specs=pl.BlockSpec((1, rows, cols), lambda j, c_ref: (j, 0, 0))),
        out_shape=jax.ShapeDtypeStruct((n, rows, cols), F32),
        compiler_params=_params(("parallel",)),
    )(cidx, g, r)


def _exchange_chips(cs_in, cs_out):
    def body(in_ref, out_ref, rin_ref, rout_ref, send_sems, recv_sems, local_sems):
        x, y, c = _place()
        mine = 2 * x + y
        chips = [(1 - x, y), (x, 1 - y), (1 - x, 1 - y)]
        local = [pltpu.make_async_copy(src.at[mine], dst.at[mine], local_sems.at[k])
                 for k, (src, dst) in enumerate(((in_ref, rin_ref), (out_ref, rout_ref)))]
        for cp in local:
            cp.start()
        cps = []
        for j, (cx, cy) in enumerate(chips):
            for a, (src, dst) in enumerate(((in_ref, rin_ref), (out_ref, rout_ref))):
                cps.append(pltpu.make_async_remote_copy(
                    src_ref=src.at[2 * cx + cy], dst_ref=dst.at[mine], send_sem=send_sems.at[2 * j + a],
                    recv_sem=recv_sems.at[2 * j + a], device_id=(cx, cy, c), device_id_type=MESH))
        for cp in cps:
            cp.start()
        for cp in cps:
            cp.wait()
        for cp in local:
            cp.wait()

    hbm = pl.BlockSpec(memory_space=pl.ANY)
    return pl.pallas_call(
        body, name="exchange_chips",
        in_specs=[hbm, hbm], out_specs=[hbm, hbm],
        out_shape=[jax.ShapeDtypeStruct(a.shape, F32) for a in (cs_in, cs_out)],
        scratch_shapes=[pltpu.SemaphoreType.DMA((6,)), pltpu.SemaphoreType.DMA((6,)), pltpu.SemaphoreType.DMA((2,))],
    )(cs_in, cs_out)


def _sum_chips(r, name):
    _, rows, cols = r.shape
    tr = min(rows, 256)

    def body(r_ref, o_ref):
        o_ref[...] = ((r_ref[0] + r_ref[1]) + r_ref[2]) + r_ref[3]

    return pl.pallas_call(
        body, name=name, grid=(rows // tr,),
        in_specs=[pl.BlockSpec((4, tr, cols), lambda i: (0, i, 0))],
        out_specs=pl.BlockSpec((tr, cols), lambda i: (i, 0)),
        out_shape=jax.ShapeDtypeStruct((rows, cols), F32),
        compiler_params=_params(("parallel",)),
    )(r)


def _join_halves(h_in, h_out):
    def body(in_ref, out_ref, fin_ref, fout_ref, send_sems, recv_sems, local_sems):
        x, y, c = _place()
        sib = (x, y, 1 - c)
        pairs = ((in_ref, fin_ref.at[:, pl.ds(c * 512, 512)]), (out_ref, fout_ref.at[c]))
        local = [pltpu.make_async_copy(src, dst, local_sems.at[k]) for k, (src, dst) in enumerate(pairs)]
        cps = [pltpu.make_async_remote_copy(src_ref=src, dst_ref=dst, send_sem=send_sems.at[k],
                                            recv_sem=recv_sems.at[k], device_id=sib, device_id_type=MESH)
               for k, (src, dst) in enumerate(pairs)]
        for cp in local + cps:
            cp.start()
        for cp in cps + local:
            cp.wait()

    hbm = pl.BlockSpec(memory_space=pl.ANY)
    return pl.pallas_call(
        body, name="join_halves",
        in_specs=[hbm, hbm], out_specs=[hbm, hbm],
        out_shape=[jax.ShapeDtypeStruct((D_MODEL, 1024), F32), jax.ShapeDtypeStruct((2, 128, D_MODEL), F32)],
        scratch_shapes=[pltpu.SemaphoreType.DMA((2,)), pltpu.SemaphoreType.DMA((2,)), pltpu.SemaphoreType.DMA((2,))],
    )(h_in, h_out)


def _allreduce_small(g_nw, g_fw, g_hgw, g_lbl, loss8):
    def body(nw_ref, fw_ref, hgw_ref, lbl_ref, loss_ref, out_ref, slots, send_sems, recv_sems):
        x, y, c = _place()
        me = 4 * x + 2 * y + c
        slots[me] = jnp.zeros((8, D_MODEL), F32)
        slots[me, 0:1, :] = nw_ref[...]
        slots[me, 1:2, :] = fw_ref[...]
        slots[me, 2:3, 0:HG_W] = hgw_ref[...]
        slots[me, 3:4, 0:HG_W] = lbl_ref[0:1, :]
        slots[me, 3:4, HG_W:] = lbl_ref[1:2, :]
        slots[me, 4:5, 0:128] = loss_ref[0:1, :]
        cps = []
        for k in range(1, 8):
            dx, dy, dc = (k >> 2) & 1, (k >> 1) & 1, k & 1
            to = (x ^ dx, y ^ dy, c ^ dc)
            cps.append(pltpu.make_async_remote_copy(
                src_ref=slots.at[me], dst_ref=slots.at[me], send_sem=send_sems.at[k - 1],
                recv_sem=recv_sems.at[k - 1], device_id=to, device_id_type=MESH))
        for cp in cps:
            cp.start()
        for cp in cps:
            cp.wait()
        acc = slots[0]
        for i in range(1, 8):
            acc = acc + slots[i]
        out_ref[...] = acc

    vm = pl.BlockSpec(memory_space=pltpu.VMEM)
    return pl.pallas_call(
        body, name="allreduce_small",
        in_specs=[vm] * 5, out_specs=vm,
        out_shape=jax.ShapeDtypeStruct((8, D_MODEL), F32),
        scratch_shapes=[pltpu.VMEM((8, 8, D_MODEL), F32), pltpu.SemaphoreType.DMA((7,)), pltpu.SemaphoreType.DMA((7,))],
    )(g_nw, g_fw, g_hgw, g_lbl, loss8)


def _rope_tables(S):
    inv_freq = 1.0 / (ROPE_THETA ** (jnp.arange(ROPE_HALF, dtype=F32) / ROPE_HALF))
    ang = jnp.arange(S, dtype=jnp.int32).astype(F32)[:, None] * inv_freq[None, :]
    cos, sin = jnp.cos(ang), jnp.sin(ang)
    cos2 = jnp.concatenate([cos, cos, cos, cos], axis=-1)
    sin2 = jnp.concatenate([-sin, sin, -sin, sin], axis=-1)
    return cos2, sin2


def _local_step(x2, tgt2, norm_w, w_in_full, lbl, hg_norm_w, w_out_full, fnw):
    S = x2.shape[0]
    cos2, sin2 = _rope_tables(S)
    p0, p1, qkv, p3, ut = _inproj(x2, norm_w, w_in_full, cos2, sin2)
    o_hg, mix_hg, mixt_hg, states = _hg_fwd(p0, p1, lbl, hg_norm_w)
    qkv_ds = [qkv.reshape(1, S, 3 * AT_W)] + [_to_dil(qkv, d) for d in DILATIONS[1:]]
    ols = [_from_dil(_att_fwd(q)) for q in qkv_ds]
    ol, mix_at, mixt_at = _att_combine(ols, p3)
    dh, dm_hg, dm_at, g_wout, g_fw, loss8 = _outproj(x2, tgt2, mix_hg, mix_at, mixt_hg, mixt_at, w_out_full, fnw)
    dqr, dfl, dv_hg, dz_hg, g_lbl, g_hgw = _hg_bwd(p0, p1, o_hg, dm_hg, states, lbl, hg_norm_w)
    aux, dz_at = _att_gate_bwd(dm_at, ol, p3)
    aux_ds = [aux.reshape(1, S, 3 * AT_W)] + [_to_dil(aux, d) for d in DILATIONS[1:]]
    dqkvs = [_from_dil(_att_bwd(q, a)) for q, a in zip(qkv_ds, aux_ds)]
    dq_at, dk_at, dv_at = _att_bwd_combine(dqkvs, cos2, sin2)
    dps = [dqr, dfl, dv_hg, dz_hg, dq_at, dk_at, dv_at, dz_at]
    grad_x, g_nw = _inproj_bwd_x(dps, w_in_full, x2, norm_w, dh)
    g_win = _inproj_bwd_w(ut, dps)
    return loss8, grad_x, g_nw, g_win, g_lbl, g_hgw, g_wout, g_fw


def kernel(x, norm_w, w_in, hgrn_lb_logits, hg_norm_w, w_out, final_norm_w, loss_target, m_norm_w, m_w_in, m_hgrn_lb_logits, m_hg_norm_w, m_w_out, m_final_norm_w, v_norm_w, v_w_in, v_hgrn_lb_logits, v_hg_norm_w, v_w_out, v_final_norm_w):
    S = x.shape[1]
    w_in_full, w_out_full = _gather_weights(w_in[0], w_out[0])
    loss8, grad_x, g_nw, g_win, g_lbl, g_hgw, g_wout, g_fw = _local_step(
        x[0], loss_target[0], norm_w, w_in_full, hgrn_lb_logits, hg_norm_w,
        w_out_full.reshape(D_MODEL, D_MODEL), final_norm_w.reshape(1, D_MODEL))

    cidx = lax.axis_index("c").astype(jnp.int32).reshape(1)
    g_in4 = jnp.stack(g_win).reshape(4, 2, D_MODEL, 512)
    g_out4 = g_wout.reshape(4, 2, 128, D_MODEL)
    r_in, r_out = _swap_halves(g_in4, g_out4)
    cs_in = _add_half(g_in4, r_in, cidx, "add_half_in")
    cs_out = _add_half(g_out4, r_out, cidx, "add_half_out")
    x_in, x_out = _exchange_chips(cs_in, cs_out)
    h_in = _sum_chips(x_in, "sum_chips_in")
    h_out = _sum_chips(x_out, "sum_chips_out")
    grad_w_in, f_out = _join_halves(h_in, h_out)
    grad_w_out = f_out.reshape(256, D_MODEL)

    red = _allreduce_small(g_nw, g_fw, g_hgw, g_lbl, loss8)
    loss = red[4, 0]
    grad_norm_w = red[0:1, :]
    grad_final_norm_w = red[1, :]
    grad_hg_norm_w = red[2:3, :HG_W]
    grad_lbl = jnp.concatenate([red[3:4, :HG_W], red[3:4, HG_W:]], axis=0)

    d_nw, m_nw, v_nw = _adamw(norm_w, grad_norm_w, m_norm_w, v_norm_w, "adamw_norm_w")
    d_win, m_win, v_win = _adamw(w_in[0], grad_w_in, m_w_in[0], v_w_in[0], "adamw_w_in")
    d_lbl, m_lbl, v_lbl = _adamw(hgrn_lb_logits, grad_lbl, m_hgrn_lb_logits, v_hgrn_lb_logits, "adamw_lb_logits")
    d_hgw, m_hgw, v_hgw = _adamw(hg_norm_w, grad_hg_norm_w, m_hg_norm_w, v_hg_norm_w, "adamw_hg_norm_w")
    d_wout, m_wout, v_wout = _adamw(w_out[0], grad_w_out, m_w_out[0], v_w_out[0], "adamw_w_out")
    d_fw, m_fw, v_fw = _adamw(final_norm_w.reshape(1, D_MODEL), grad_final_norm_w.reshape(1, D_MODEL),
                              m_final_norm_w.reshape(1, D_MODEL), v_final_norm_w.reshape(1, D_MODEL), "adamw_final_norm_w")
    e1 = lambda a: a[None]
    flat = lambda a: a.reshape(D_MODEL)
    return (loss, grad_x[None], grad_norm_w, e1(grad_w_in), grad_lbl, grad_hg_norm_w, e1(grad_w_out), grad_final_norm_w,
            d_nw, e1(d_win), d_lbl, d_hgw, e1(d_wout), flat(d_fw),
            m_nw, e1(m_win), m_lbl, m_hgw, e1(m_wout), flat(m_fw),
            v_nw, e1(v_win), v_lbl, v_hgw, e1(v_wout), flat(v_fw))
```

```python
import functools

import jax
import jax.numpy as jnp
import numpy as np
from jax import lax
from jax.experimental import pallas as pl
from jax.experimental.pallas import tpu as pltpu

F32 = jnp.float32
BF16 = jnp.bfloat16
MESH = pl.DeviceIdType.MESH

D_MODEL = 1024
HG_W = 512
AT_W = 512
HEAD_PAIR = 128
ROPE_HALF = 32
ROPE_THETA = 10000.0
EPS = 1e-6
CHUNK = 128
LEVELS = (64, 32, 16, 8)
DIAG = 8
ATT_BLK = 128
DILATIONS = (1, 4, 16)
ATT_SCALE = 0.125
NEG = -1e30
VMEM_LIMIT = 56 * 1024 * 1024

ADAM_LR = 0.001
ADAM_B1 = 0.9
ADAM_B2 = 0.999
ADAM_EPS = 1e-08
ADAM_WD = 0.01
ADAM_STEP = 10


def _iota(shape, dim):
    return lax.broadcasted_iota(jnp.int32, shape, dim)


def _dot(a, b):
    return jnp.dot(a, b, preferred_element_type=F32)


def _dot_nt(a, b):
    return lax.dot_general(a, b, (((1,), (1,)), ((), ())), preferred_element_type=F32)


def _dot_hi(a, b):
    return jnp.dot(a, b, preferred_element_type=F32, precision=lax.Precision.HIGHEST)


def _sig(v):
    return 1.0 / (1.0 + jnp.exp(-v))


def _params(sem=None, vmem=VMEM_LIMIT):
    return pltpu.CompilerParams(dimension_semantics=sem, vmem_limit_bytes=vmem)


def _full(shape):
    n = len(shape)
    return pl.BlockSpec(shape, lambda *_: (0,) * n)


def _rope_rot(y):
    n = y.shape[1]
    first = (_iota(y.shape, 1) & (2 * ROPE_HALF - 1)) < ROPE_HALF
    return jnp.where(first, pltpu.roll(y, n - ROPE_HALF, 1), pltpu.roll(y, ROPE_HALF, 1))


def _inproj(x2, norm_w, w_in_full, cos2, sin2):
    S = x2.shape[0]
    tm = 256

    def body(x_ref, nw_ref, w_ref, cos_ref, sin_ref, p0_ref, p1_ref, qkv_ref, p3_ref, ut_ref):
        x = x_ref[...]
        r = lax.rsqrt(jnp.mean(x * x, axis=-1, keepdims=True) + EPS)
        u = x * r * nw_ref[...]
        ub = u.astype(BF16)
        ut_ref[...] = u.T.astype(BF16)
        p0_ref[...] = _dot(ub, w_ref[0])
        p1_ref[...] = _dot(ub, w_ref[1])
        y2 = _dot(ub, w_ref[2])
        cosf = jnp.tile(cos_ref[...], (1, 8))
        sinf = jnp.tile(sin_ref[...], (1, 8))
        qkv_ref[:, : 2 * AT_W] = (y2 * cosf + _rope_rot(y2) * sinf).astype(BF16)
        y3 = _dot(ub, w_ref[3])
        p3_ref[...] = y3
        qkv_ref[:, 2 * AT_W:] = y3[:, :AT_W].astype(BF16)

    row = lambda w: pl.BlockSpec((tm, w), lambda s: (s, 0))
    return pl.pallas_call(
        body, name="inproj", grid=(S // tm,),
        in_specs=[row(D_MODEL), _full((1, D_MODEL)), _full((4, D_MODEL, 1024)), row(128), row(128)],
        out_specs=[row(1024), row(1024), row(3 * AT_W), row(1024), pl.BlockSpec((D_MODEL, tm), lambda s: (0, s))],
        out_shape=[jax.ShapeDtypeStruct((S, 1024), F32), jax.ShapeDtypeStruct((S, 1024), F32),
                   jax.ShapeDtypeStruct((S, 3 * AT_W), BF16), jax.ShapeDtypeStruct((S, 1024), F32),
                   jax.ShapeDtypeStruct((D_MODEL, S), BF16)],
        compiler_params=_params(("parallel",)),
    )(x2, norm_w, w_in_full, cos2, sin2)


HG_HPS = 4
N_LEV = len(LEVELS)


def _hg_const_arrays():
    r = np.arange(CHUNK)[:, None]
    c = np.arange(CHUNK)[None, :]
    tris = np.stack([r >= c, r <= c])
    lm = [((r // (2 * m)) == (c // (2 * m))) & (r % (2 * m) >= m) & (c % (2 * m) < m) for m in LEVELS]
    dm = [(c == r - dl) & (r % DIAG >= dl) for dl in range(DIAG)]
    masks = np.stack(lm + [x.T for x in lm] + dm)
    return jnp.asarray(tris, BF16), jnp.asarray(masks, F32)


def _split2(a):
    hi = a.astype(BF16)
    return hi, (a - hi.astype(F32)).astype(BF16)


def _dot3(a, b):
    ah, al = _split2(a)
    bh, bl = _split2(b)
    n = b.shape[1]
    p = _dot(ah, jnp.concatenate([bh, bl], axis=1))
    return (p[:, :n] + p[:, n:]) + _dot(al, bh)


def _tri_dot(tri, a):
    a1 = a.astype(BF16)
    r1 = a - a1.astype(F32)
    a2 = r1.astype(BF16)
    a3 = (r1 - a2.astype(F32)).astype(BF16)
    n = a.shape[1]
    p = _dot(tri, jnp.concatenate([a1, a2, a3], axis=1))
    return (p[:, :n] + p[:, n:2 * n]) + p[:, 2 * n:]


def _rowsum(t):
    return _dot(t.astype(BF16), jnp.ones((t.shape[1], t.shape[1]), BF16))


def _level_refs(b):
    refs = []
    for m in LEVELS:
        parts = [jnp.broadcast_to(b[r0 + m - 1: r0 + m, :], (2 * m, b.shape[1])) for r0 in range(0, CHUNK, 2 * m)]
        refs.append(parts[0] if len(parts) == 1 else jnp.concatenate(parts, axis=0))
    return refs


def _hg_lb(lbl_ref):
    l0 = lbl_ref[0:1, :]
    l1 = lbl_ref[1:2, :]
    mx = jnp.maximum(l0, l1)
    e0 = jnp.exp(l0 - mx)
    e1 = jnp.exp(l1 - mx)
    p0 = e0 / (e0 + e1)
    lb = jnp.clip(p0, 1e-6, 1.0 - 1e-6)
    inside = (p0 >= 1e-6) & (p0 <= 1.0 - 1e-6)
    dlb_dl0 = jnp.where(inside, p0 * (e1 / (e0 + e1)), 0.0)
    return lb, dlb_dl0


def _sigmoid(v):
    return 0.5 * jnp.tanh(0.5 * v) + 0.5


def _hg_gates(qr, fl, lb):
    sig = _sigmoid(fl)
    f = lb + (1.0 - lb) * sig
    g = jnp.log(f)
    k = (1.0 - lb) * (1.0 - sig)
    sq = _sigmoid(qr)
    q = qr * sq
    return sig, f, g, k, sq, q


def _hg_levels(q, k, b, mk_ref):
    refs = _level_refs(b)
    a = jnp.zeros((CHUNK, CHUNK), F32)
    eqs, eks, qts, kts = [], [], [], []
    for i in range(N_LEV):
        eq = jnp.exp(jnp.minimum(b - refs[i], 0.0))
        ek = jnp.exp(jnp.minimum(refs[i] - b, 0.0))
        qt = (q * eq).astype(BF16)
        kt = (k * ek).astype(BF16)
        a = a + _dot_nt(qt, kt) * mk_ref[i]
        eqs.append(eq); eks.append(ek); qts.append(qt); kts.append(kt)
    return a, eqs, eks, qts, kts


def _hg_specs(nc, rev):
    cc = (lambda c: nc - 1 - c) if rev else (lambda c: c)
    w = 128 * HG_HPS
    blk = lambda off: pl.BlockSpec((CHUNK, w), lambda h, c: (cc(c), h + off))
    vec = pl.BlockSpec((1, w), lambda h, c: (0, h))
    lb2 = pl.BlockSpec((2, w), lambda h, c: (0, h))
    st = pl.BlockSpec((1, HG_HPS, 128, 128), lambda h, c: (cc(c), h, 0, 0))
    consts = [_full((2, CHUNK, CHUNK)), _full((2 * N_LEV + DIAG, CHUNK, CHUNK))]
    return blk, vec, lb2, st, consts


def _hg_fwd(p0, p1, lbl, gw):
    S = p0.shape[0]
    nc = S // CHUNK
    ng = 4 // HG_HPS

    def body(qr_ref, fl_ref, v_ref, z_ref, lbl_ref, gw_ref, tri_ref, mk_ref,
             o_ref, mix_ref, mixt_ref, st_ref, state):
        c = pl.program_id(1)

        @pl.when(c == 0)
        def _():
            state[...] = jnp.zeros_like(state)

        lb_all, _ = _hg_lb(lbl_ref)
        for hh in range(HG_HPS):
            sl = slice(128 * hh, 128 * (hh + 1))
            qr, fl, v, z = qr_ref[:, sl], fl_ref[:, sl], v_ref[:, sl], z_ref[:, sl]
            _, _, g, k, _, q = _hg_gates(qr, fl, lb_all[:, sl])
            b = _tri_dot(tri_ref[0], g)
            b_last = b[CHUNK - 1: CHUNK, :]
            a, _, _, _, _ = _hg_levels(q, k, b, mk_ref)
            for dl in range(DIAG):
                if dl == 0:
                    term = q * k
                else:
                    e = jnp.exp(jnp.minimum(b - pltpu.roll(b, dl, 0), 0.0))
                    term = q * pltpu.roll(k, dl, 0) * e
                a = a + _rowsum(term) * mk_ref[2 * N_LEV + dl]
            st = state[hh]
            st_ref[0, hh] = st
            o = _dot_nt((q * jnp.exp(b)).astype(BF16), st.astype(BF16)) + _dot(a.astype(BF16), v.astype(BF16))
            state[hh] = st * jnp.exp(b_last) + _dot3(v.T, k * jnp.exp(b_last - b))
            o_ref[:, sl] = o
            rs = lax.rsqrt(jnp.mean(o * o, axis=-1, keepdims=True) + EPS)
            mixed = o * rs * gw_ref[:, sl] * (z * _sigmoid(z))
            mix_ref[:, sl] = mixed.astype(BF16)
            mixt_ref[sl, :] = mixed.T.astype(BF16)

    blk, vec, lb2, st_spec, consts = _hg_specs(nc, False)
    tris, masks = _hg_const_arrays()
    return pl.pallas_call(
        body, name="hg_fwd", grid=(ng, nc),
        in_specs=[blk(0), blk(ng), blk(0), blk(ng), lb2, vec] + consts,
        out_specs=[blk(0), blk(0), pl.BlockSpec((128 * HG_HPS, CHUNK), lambda h, c: (h, c)), st_spec],
        out_shape=[jax.ShapeDtypeStruct((S, HG_W), F32), jax.ShapeDtypeStruct((S, HG_W), BF16),
                   jax.ShapeDtypeStruct((HG_W, S), BF16), jax.ShapeDtypeStruct((nc, 4, 128, 128), F32)],
        scratch_shapes=[pltpu.VMEM((HG_HPS, 128, 128), F32)],
        compiler_params=_params(("parallel", "arbitrary")),
    )(p0, p0, p1, p1, lbl, gw, tris, masks)


def _hg_bwd(p0, p1, o_raw, dm, states, lbl, gw):
    S = p0.shape[0]
    nc = S // CHUNK
    ng = 4 // HG_HPS
    w = 128 * HG_HPS

    def body(qr_ref, fl_ref, v_ref, z_ref, o_ref, dm_ref, st_ref, lbl_ref, gw_ref, tri_ref, mk_ref,
             dqr_ref, dfl_ref, dv_ref, dz_ref, glbl_ref, ggw_ref, dstate, carry, acc_lb, acc_gw):
        c = pl.program_id(1)

        @pl.when(c == 0)
        def _():
            dstate[...] = jnp.zeros_like(dstate)
            carry[...] = jnp.zeros_like(carry)
            acc_lb[...] = jnp.zeros_like(acc_lb)
            acc_gw[...] = jnp.zeros_like(acc_gw)

        lb_all, dlb_dl0 = _hg_lb(lbl_ref)
        for hh in range(HG_HPS):
            sl = slice(128 * hh, 128 * (hh + 1))
            lb = lb_all[:, sl]
            qr, fl, v, z = qr_ref[:, sl], fl_ref[:, sl], v_ref[:, sl], z_ref[:, sl]
            sig, f, g, k, sq, q = _hg_gates(qr, fl, lb)
            b = _tri_dot(tri_ref[0], g)
            b_last = b[CHUNK - 1: CHUNK, :]
            a, eqs, eks, qts, kts = _hg_levels(q, k, b, mk_ref)

            o = o_ref[:, sl]
            dmix = dm_ref[:, sl]
            gwv = gw_ref[:, sl]
            rs = lax.rsqrt(jnp.mean(o * o, axis=-1, keepdims=True) + EPS)
            oh = o * rs
            sz = _sigmoid(z)
            dz_ref[:, sl] = dmix * (oh * gwv) * (sz * (1.0 + z * (1.0 - sz)))
            don = dmix * (z * sz)
            acc_gw[0:1, sl] += jnp.sum(don * oh, axis=0, keepdims=True)
            dy = don * gwv
            do = rs * (dy - oh * jnp.mean(dy * oh, axis=-1, keepdims=True))

            st = st_ref[0, hh]
            dst = dstate[hh]
            dob = do.astype(BF16)
            eb = jnp.exp(b)
            edec = jnp.exp(b_last - b)
            dq = _dot3(do, st) * eb
            dk = _dot3(v, dst) * edec
            db = q * dq - k * dk
            da = _dot_nt(dob, v.astype(BF16))
            da_t = da.T
            for i in range(N_LEV):
                gq = _dot((da * mk_ref[i]).astype(BF16), kts[i])
                gk = _dot((da_t * mk_ref[N_LEV + i]).astype(BF16), qts[i])
                dq = dq + eqs[i] * gq
                dk = dk + eks[i] * gk
                db = db + (qts[i].astype(F32) * gq - kts[i].astype(F32) * gk)
            for dl in range(DIAG):
                dmask = mk_ref[2 * N_LEV + dl]
                dc = _rowsum(da * dmask)
                if dl == 0:
                    a = a + _rowsum(q * k) * dmask
                    dq = dq + dc * k
                    dk = dk + dc * q
                else:
                    e = jnp.exp(jnp.minimum(b - pltpu.roll(b, dl, 0), 0.0))
                    ks = pltpu.roll(k, dl, 0)
                    a = a + _rowsum(q * ks * e) * dmask
                    tq = dc * ks * e
                    tk = pltpu.roll(dc * q * e, CHUNK - dl, 0)
                    dq = dq + tq
                    dk = dk + tk
                    db = db + (q * tq - k * tk)
            dv_ref[:, sl] = _dot(a.T.astype(BF16), dob) + _dot_nt((k * edec).astype(BF16), dst.astype(BF16))
            dstate[hh] = dst * jnp.exp(b_last) + _dot3(do.T, q * eb)

            dg = _tri_dot(tri_ref[1], db) + carry[0:1, sl]
            carry[0:1, sl] += jnp.sum(db, axis=0, keepdims=True)
            t = dg / f - dk
            dfl_ref[:, sl] = t * (1.0 - lb) * sig * (1.0 - sig)
            acc_lb[0:1, sl] += jnp.sum(t * (1.0 - sig), axis=0, keepdims=True)
            dqr_ref[:, sl] = dq * (sq * (1.0 + qr * (1.0 - sq)))

        @pl.when(c == nc - 1)
        def _():
            gl0 = acc_lb[0:1, :] * dlb_dl0
            glbl_ref[0:1, :] = gl0
            glbl_ref[1:2, :] = -gl0
            ggw_ref[...] = acc_gw[0:1, :]

    blk, vec, lb2, st_spec, consts = _hg_specs(nc, True)
    tris, masks = _hg_const_arrays()
    act = jax.ShapeDtypeStruct((S, HG_W), F32)
    return pl.pallas_call(
        body, name="hg_bwd", grid=(ng, nc),
        in_specs=[blk(0), blk(ng), blk(0), blk(ng), blk(0), blk(0), st_spec, lb2, vec] + consts,
        out_specs=[blk(0), blk(0), blk(0), blk(0), lb2, vec],
        out_shape=[act, act, act, act, jax.ShapeDtypeStruct((2, HG_W), F32), jax.ShapeDtypeStruct((1, HG_W), F32)],
        scratch_shapes=[pltpu.VMEM((HG_HPS, 128, 128), F32), pltpu.VMEM((8, w), F32),
                        pltpu.VMEM((8, w), F32), pltpu.VMEM((8, w), F32)],
        compiler_params=_params(("parallel", "arbitrary")),
    )(p0, p0, p1, p1, o_raw, dm, states, lbl, gw, tris, masks)


def _to_dil(a, d):
    S, C = a.shape
    return a.reshape(S // d, d, C).transpose(1, 0, 2)


def _from_dil(a):
    d, L, C = a.shape
    return a.transpose(1, 0, 2).reshape(d * L, C)


def _att_bias():
    qi = np.arange(ATT_BLK)[:, None]
    kj = np.arange(2 * ATT_BLK)[None, :]
    band = (kj >= qi) & (kj <= qi + ATT_BLK)
    qm = np.stack([band & (kj >= ATT_BLK), band])
    cur = (kj < ATT_BLK) & (qi <= kj)
    km = np.stack([cur, cur | ((kj >= ATT_BLK) & (qi >= kj - ATT_BLK))])
    to_bias = lambda m: jnp.asarray(np.where(m, 0.0, NEG), F32)
    return to_bias(qm), to_bias(km)


def _att_fwd(qkv_d):
    d, L, _ = qkv_d.shape
    nb = L // ATT_BLK

    def body(q_ref, kp_ref, kc_ref, vp_ref, vc_ref, bias_ref, ol_ref):
        bias = bias_ref[jnp.minimum(pl.program_id(1), 1)]
        first = _iota((ATT_BLK, HEAD_PAIR), 1) < 64
        for hp in range(4):
            sl = slice(HEAD_PAIR * hp, HEAD_PAIR * (hp + 1))
            q2 = q_ref[0, :, sl] * ATT_SCALE
            kcat = jnp.concatenate([kp_ref[0, :, sl], kc_ref[0, :, sl]], axis=0)
            vcat = jnp.concatenate([vp_ref[0, :, sl], vc_ref[0, :, sl]], axis=0)
            res = []
            for hm in (first, jnp.logical_not(first)):
                qm = jnp.where(hm, q2, jnp.zeros_like(q2))
                s = _dot_nt(qm, kcat) + bias
                m = jnp.max(s, axis=-1, keepdims=True)
                p = jnp.exp(s - m)
                l = jnp.sum(p, axis=-1, keepdims=True)
                res.append((_dot(p.astype(BF16), vcat) / l, m + jnp.log(l)))
            ol_ref[0, :, sl] = jnp.where(first, res[0][0], res[1][0])
            ol_ref[0, :, AT_W + HEAD_PAIR * hp: AT_W + HEAD_PAIR * (hp + 1)] = jnp.where(first, res[0][1], res[1][1])

    cur = lambda j: pl.BlockSpec((1, ATT_BLK, AT_W), lambda r, n: (r, n, j))
    prev = lambda j: pl.BlockSpec((1, ATT_BLK, AT_W), lambda r, n: (r, jnp.maximum(n - 1, 0), j))
    return pl.pallas_call(
        body, name=f"att_fwd_d{d}", grid=(d, nb),
        in_specs=[cur(0), prev(1), cur(1), prev(2), cur(2), _full((2, ATT_BLK, 2 * ATT_BLK))],
        out_specs=pl.BlockSpec((1, ATT_BLK, 2 * AT_W), lambda r, n: (r, n, 0)),
        out_shape=jax.ShapeDtypeStruct((d, L, 2 * AT_W), F32),
        compiler_params=_params(("parallel", "parallel")),
    )(qkv_d, qkv_d, qkv_d, qkv_d, qkv_d, _att_bias()[0])


def _att_combine(ols, p3):
    S = p3.shape[0]
    tm = 512

    def body(a_ref, b_ref, c_ref, z_ref, ol_ref, mix_ref, mixt_ref):
        os_ = [r[:, :AT_W] for r in (a_ref, b_ref, c_ref)]
        ls = [r[:, AT_W:] for r in (a_ref, b_ref, c_ref)]
        mx = jnp.maximum(jnp.maximum(ls[0], ls[1]), ls[2])
        es = [jnp.exp(l - mx) for l in ls]
        zs = es[0] + es[1] + es[2]
        o = (es[0] * os_[0] + es[1] * os_[1] + es[2] * os_[2]) / zs
        ol_ref[:, :AT_W] = o
        ol_ref[:, AT_W:] = mx + jnp.log(zs)
        z = z_ref[...]
        mixed = o * (z * _sig(z))
        mix_ref[...] = mixed.astype(BF16)
        mixt_ref[...] = mixed.T.astype(BF16)

    row = lambda w: pl.BlockSpec((tm, w), lambda s: (s, 0))
    return pl.pallas_call(
        body, name="att_combine", grid=(S // tm,),
        in_specs=[row(1024), row(1024), row(1024), pl.BlockSpec((tm, AT_W), lambda s: (s, 1))],
        out_specs=[row(1024), row(AT_W), pl.BlockSpec((AT_W, tm), lambda s: (0, s))],
        out_shape=[jax.ShapeDtypeStruct((S, 1024), F32), jax.ShapeDtypeStruct((S, AT_W), BF16),
                   jax.ShapeDtypeStruct((AT_W, S), BF16)],
        compiler_params=_params(("parallel",)),
    )(*ols, p3)


def _att_gate_bwd(dm_at, ol, p3):
    S = p3.shape[0]
    tm = 512

    def body(dm_ref, ol_ref, z_ref, aux_ref, dz_ref):
        o = ol_ref[:, :AT_W]
        z = z_ref[...]
        dm = dm_ref[...]
        sz = _sig(z)
        dz_ref[...] = dm * o * (sz * (1.0 + z * (1.0 - sz)))
        do = dm * (z * sz)
        aux_ref[:, :AT_W] = do
        ones = ((_iota((HEAD_PAIR, HEAD_PAIR), 0) < 64) == (_iota((HEAD_PAIR, HEAD_PAIR), 1) < 64)).astype(F32)
        prod = do * o
        for hp in range(4):
            sl = slice(HEAD_PAIR * hp, HEAD_PAIR * (hp + 1))
            aux_ref[:, AT_W + HEAD_PAIR * hp: AT_W + HEAD_PAIR * (hp + 1)] = _dot_hi(prod[:, sl], ones)
        aux_ref[:, 2 * AT_W:] = ol_ref[:, AT_W:]

    row = lambda w: pl.BlockSpec((tm, w), lambda s: (s, 0))
    return pl.pallas_call(
        body, name="att_gate_bwd", grid=(S // tm,),
        in_specs=[row(AT_W), row(1024), pl.BlockSpec((tm, AT_W), lambda s: (s, 1))],
        out_specs=[row(3 * AT_W), row(AT_W)],
        out_shape=[jax.ShapeDtypeStruct((S, 3 * AT_W), F32), jax.ShapeDtypeStruct((S, AT_W), F32)],
        compiler_params=_params(("parallel",)),
    )(dm_at, ol, p3)


def _att_bwd(qkv_d, aux_d):
    d, L, _ = qkv_d.shape
    nb = L // ATT_BLK

    def body(qc_ref, qn_ref, kp_ref, kc_ref, vp_ref, vc_ref, ac_ref, an_ref, bq_ref, bk_ref, dqkv_ref):
        n = pl.program_id(1)
        bias = bq_ref[jnp.minimum(n, 1)]
        bias_t = bk_ref[jnp.minimum(nb - 1 - n, 1)]
        first = _iota((ATT_BLK, HEAD_PAIR), 1) < 64
        for hp in range(4):
            sl = slice(HEAD_PAIR * hp, HEAD_PAIR * (hp + 1))
            sl_d = slice(AT_W + HEAD_PAIR * hp, AT_W + HEAD_PAIR * (hp + 1))
            sl_l = slice(2 * AT_W + HEAD_PAIR * hp, 2 * AT_W + HEAD_PAIR * (hp + 1))
            q2 = qc_ref[0, :, sl]
            k2 = kc_ref[0, :, sl]
            v2 = vc_ref[0, :, sl]
            kcat = jnp.concatenate([kp_ref[0, :, sl], k2], axis=0)
            vcat = jnp.concatenate([vp_ref[0, :, sl], v2], axis=0)
            qcat = jnp.concatenate([q2, qn_ref[0, :, sl]], axis=0)
            q2s = q2 * ATT_SCALE
            k2s = k2 * ATT_SCALE
            do_c = ac_ref[0, :, sl]
            docat = jnp.concatenate([do_c, an_ref[0, :, sl]], axis=0).astype(BF16)
            dd_c = ac_ref[0, :, sl_d]
            ls_c = ac_ref[0, :, sl_l]
            dd_t = jnp.concatenate([dd_c, an_ref[0, :, sl_d]], axis=0).T
            ls_t = jnp.concatenate([ls_c, an_ref[0, :, sl_l]], axis=0).T
            do_cb = do_c.astype(BF16)
            dqs, dks, dvs = [], [], []
            for hh, hm in enumerate((first, jnp.logical_not(first))):
                lane0 = 64 * hh
                qm = jnp.where(hm, q2s, jnp.zeros_like(q2))
                s = _dot_nt(qm, kcat) + bias
                p = jnp.exp(s - ls_c[:, lane0: lane0 + 1])
                dp = _dot_nt(jnp.where(hm, do_cb, jnp.zeros_like(do_cb)), vcat)
                ds = p * (dp - dd_c[:, lane0: lane0 + 1])
                dqs.append(_dot(ds.astype(BF16), kcat) * ATT_SCALE)
                km = jnp.where(hm, k2s, jnp.zeros_like(k2))
                vm = jnp.where(hm, v2, jnp.zeros_like(v2))
                st = _dot_nt(km, qcat) + bias_t
                pt = jnp.exp(st - ls_t[lane0: lane0 + 1, :])
                dpt = _dot_nt(vm, docat)
                dst = pt * (dpt - dd_t[lane0: lane0 + 1, :])
                dvs.append(_dot(pt.astype(BF16), docat))
                dks.append(_dot(dst.astype(BF16), qcat) * ATT_SCALE)
            dqkv_ref[0, :, sl] = jnp.where(first, dqs[0], dqs[1])
            dqkv_ref[0, :, AT_W + HEAD_PAIR * hp: AT_W + HEAD_PAIR * (hp + 1)] = jnp.where(first, dks[0], dks[1])
            dqkv_ref[0, :, 2 * AT_W + HEAD_PAIR * hp: 2 * AT_W + HEAD_PAIR * (hp + 1)] = jnp.where(first, dvs[0], dvs[1])

    cur = lambda j: pl.BlockSpec((1, ATT_BLK, AT_W), lambda r, n: (r, n, j))
    prev = lambda j: pl.BlockSpec((1, ATT_BLK, AT_W), lambda r, n: (r, jnp.maximum(n - 1, 0), j))
    nxt = lambda j: pl.BlockSpec((1, ATT_BLK, AT_W), lambda r, n: (r, jnp.minimum(n + 1, nb - 1), j))
    return pl.pallas_call(
        body, name=f"att_bwd_d{d}", grid=(d, nb),
        in_specs=[cur(0), nxt(0), prev(1), cur(1), prev(2), cur(2),
                  pl.BlockSpec((1, ATT_BLK, 3 * AT_W), lambda r, n: (r, n, 0)),
                  pl.BlockSpec((1, ATT_BLK, 3 * AT_W), lambda r, n: (r, jnp.minimum(n + 1, nb - 1), 0)),
                  _full((2, ATT_BLK, 2 * ATT_BLK)), _full((2, ATT_BLK, 2 * ATT_BLK))],
        out_specs=pl.BlockSpec((1, ATT_BLK, 3 * AT_W), lambda r, n: (r, n, 0)),
        out_shape=jax.ShapeDtypeStruct((d, L, 3 * AT_W), F32),
        compiler_params=_params(("parallel", "parallel")),
    )(qkv_d, qkv_d, qkv_d, qkv_d, qkv_d, qkv_d, aux_d, aux_d, *_att_bias())


def _att_bwd_combine(dqkvs, cos2, sin2):
    S = dqkvs[0].shape[0]
    tm = 512

    def body(a_ref, b_ref, c_ref, cos_ref, sin_ref, dq_ref, dk_ref, dv_ref):
        t = a_ref[...] + b_ref[...] + c_ref[...]
        dy = t[:, : 2 * AT_W]
        cosf = jnp.tile(cos_ref[...], (1, 8))
        sinf = jnp.tile(sin_ref[...], (1, 8))
        dx = dy * cosf - _rope_rot(dy) * sinf
        dq_ref[...] = dx[:, :AT_W]
        dk_ref[...] = dx[:, AT_W:]
        dv_ref[...] = t[:, 2 * AT_W:]

    row = lambda w: pl.BlockSpec((tm, w), lambda s: (s, 0))
    act = jax.ShapeDtypeStruct((S, AT_W), F32)
    return pl.pallas_call(
        body, name="att_bwd_combine", grid=(S // tm,),
        in_specs=[row(3 * AT_W), row(3 * AT_W), row(3 * AT_W), row(128), row(128)],
        out_specs=[row(AT_W), row(AT_W), row(AT_W)],
        out_shape=[act, act, act],
        compiler_params=_params(("parallel",)),
    )(*dqkvs, cos2, sin2)


def _outproj(x2, tgt2, mix_hg, mix_at, mixt_hg, mixt_at, w_out_full, fnw):
    S = x2.shape[0]
    tm = 256
    ns = S // tm

    def body(x_ref, t_ref, mh_ref, ma_ref, mht_ref, mat_ref, w_ref, fw_ref,
             dh_ref, dmh_ref, dma_ref, gw_ref, gfw_ref, loss_ref):
        s = pl.program_id(0)

        @pl.when(s == 0)
        def _():
            gw_ref[...] = jnp.zeros_like(gw_ref)
            gfw_ref[...] = jnp.zeros_like(gfw_ref)
            loss_ref[...] = jnp.zeros_like(loss_ref)

        y = _dot(mh_ref[...], w_ref[:HG_W, :]) + _dot(ma_ref[...], w_ref[HG_W:, :])
        h = x_ref[...] + y
        r = lax.rsqrt(jnp.mean(h * h, axis=-1, keepdims=True) + EPS)
        hn = h * r
        fw = fw_ref[...]
        err = hn * fw - t_ref[...]
        loss_ref[...] += 0.5 * jnp.sum(jnp.mean(err * err, axis=-1, keepdims=True))
        dout = err * (1.0 / D_MODEL)
        gfw_ref[...] += jnp.sum(dout * hn, axis=0, keepdims=True)
        dhn = dout * fw
        dh = r * (dhn - hn * jnp.mean(dhn * hn, axis=-1, keepdims=True))
        dh_ref[...] = dh
        dhb = dh.astype(BF16)
        dmh_ref[...] = _dot_nt(dhb, w_ref[:HG_W, :])
        dma_ref[...] = _dot_nt(dhb, w_ref[HG_W:, :])
        gw_ref[:HG_W, :] += _dot(mht_ref[...], dhb)
        gw_ref[HG_W:, :] += _dot(mat_ref[...], dhb)

    row = lambda w: pl.BlockSpec((tm, w), lambda s: (s, 0))
    colb = pl.BlockSpec((HG_W, tm), lambda s: (0, s))
    return pl.pallas_call(
        body, name="outproj", grid=(ns,),
        in_specs=[row(D_MODEL), row(D_MODEL), row(HG_W), row(AT_W), colb, colb,
                  _full((D_MODEL, D_MODEL)), _full((1, D_MODEL))],
        out_specs=[row(D_MODEL), row(HG_W), row(AT_W), _full((D_MODEL, D_MODEL)), _full((1, D_MODEL)), _full((8, 128))],
        out_shape=[jax.ShapeDtypeStruct((S, D_MODEL), F32), jax.ShapeDtypeStruct((S, HG_W), F32),
                   jax.ShapeDtypeStruct((S, AT_W), F32), jax.ShapeDtypeStruct((D_MODEL, D_MODEL), F32),
                   jax.ShapeDtypeStruct((1, D_MODEL), F32), jax.ShapeDtypeStruct((8, 128), F32)],
        compiler_params=_params(("arbitrary",)),
    )(x2, tgt2, mix_hg, mix_at, mixt_hg, mixt_at, w_out_full, fnw)


def _inproj_bwd_x(dps, w_in_full, x2, norm_w, dh):
    S = x2.shape[0]
    tm = 256

    def body(d0, d1, d2, d3, d4, d5, d6, d7, w_ref, x_ref, nw_ref, dh_ref, gx_ref, gnw_ref):
        s = pl.program_id(0)

        @pl.when(s == 0)
        def _():
            gnw_ref[...] = jnp.zeros_like(gnw_ref)

        du = jnp.zeros((tm, D_MODEL), F32)
        for i, dref in enumerate((d0, d1, d2, d3, d4, d5, d6, d7)):
            j, half = divmod(i, 2)
            du = du + _dot_nt(dref[...].astype(BF16), w_ref[j, :, 512 * half: 512 * (half + 1)])
        x = x_ref[...]
        r = lax.rsqrt(jnp.mean(x * x, axis=-1, keepdims=True) + EPS)
        xh = x * r
        gnw_ref[...] += jnp.sum(du * xh, axis=0, keepdims=True)
        dun = du * nw_ref[...]
        gx_ref[...] = dh_ref[...] + r * (dun - xh * jnp.mean(dun * xh, axis=-1, keepdims=True))

    row = lambda w: pl.BlockSpec((tm, w), lambda s: (s, 0))
    return pl.pallas_call(
        body, name="inproj_bwd_x", grid=(S // tm,),
        in_specs=[row(512)] * 8 + [_full((4, D_MODEL, 1024)), row(D_MODEL), _full((1, D_MODEL)), row(D_MODEL)],
        out_specs=[row(D_MODEL), _full((1, D_MODEL))],
        out_shape=[jax.ShapeDtypeStruct((S, D_MODEL), F32), jax.ShapeDtypeStruct((1, D_MODEL), F32)],
        compiler_params=_params(("arbitrary",)),
    )(*dps, w_in_full, x2, norm_w, dh)


def _inproj_bwd_w(ut, dps):
    S = ut.shape[1]
    tm = 512
    outs = []
    for i, dp in enumerate(dps):
        def body(ut_ref, dp_ref, g_ref):
            @pl.when(pl.program_id(0) == 0)
            def _():
                g_ref[...] = jnp.zeros_like(g_ref)
            g_ref[...] += _dot(ut_ref[...], dp_ref[...].astype(BF16))

        outs.append(pl.pallas_call(
            body, name=f"inproj_bwd_w{i}", grid=(S // tm,),
            in_specs=[pl.BlockSpec((D_MODEL, tm), lambda s: (0, s)), pl.BlockSpec((tm, 512), lambda s: (s, 0))],
            out_specs=_full((D_MODEL, 512)),
            out_shape=jax.ShapeDtypeStruct((D_MODEL, 512), F32),
            compiler_params=_params(("arbitrary",)),
        )(ut, dp))
    return outs


def _adamw(w, g, m, v, name):
    rows, cols = w.shape
    tr = min(rows, 256)

    def body(w_ref, g_ref, m_ref, v_ref, d_ref, nm_ref, nv_ref):
        gg = g_ref[...]
        nm = ADAM_B1 * m_ref[...] + (1.0 - ADAM_B1) * gg
        nv = ADAM_B2 * v_ref[...] + (1.0 - ADAM_B2) * (gg * gg)
        m_hat = nm / (1.0 - ADAM_B1 ** ADAM_STEP)
        v_hat = nv / (1.0 - ADAM_B2 ** ADAM_STEP)
        d_ref[...] = -ADAM_LR * (m_hat / (jnp.sqrt(v_hat) + ADAM_EPS) + ADAM_WD * w_ref[...])
        nm_ref[...] = nm
        nv_ref[...] = nv

    spec = pl.BlockSpec((tr, cols), lambda i: (i, 0))
    sds = jax.ShapeDtypeStruct((rows, cols), F32)
    return pl.pallas_call(
        body, name=name, grid=(rows // tr,),
        in_specs=[spec] * 4, out_specs=[spec] * 3, out_shape=[sds] * 3,
        compiler_params=_params(("parallel",)),
    )(w, g, m, v)


def _place():
    return lax.axis_index("x"), lax.axis_index("y"), lax.axis_index("c")


def _gather_weights(w_in_s, w_out_s):
    def body(win_ref, wout_ref, fin_ref, fout_ref, send_sems, recv_sems):
        x, y, c = _place()
        me = (x, y, c)
        sib = (x, y, 1 - c)
        mine = 2 * x + y
        fin_ref[mine] = win_ref[...].astype(BF16)
        fout_ref[mine] = wout_ref[...].astype(BF16)
        chips = [(1 - x, y), (x, 1 - y), (1 - x, 1 - y)]

        def halves(chip, half):
            return (fin_ref.at[chip, pl.ds(half * 512, 512), :], fout_ref.at[chip, pl.ds(half * 128, 128), :])

        def copy(k, ref, to):
            return pltpu.make_async_remote_copy(src_ref=ref, dst_ref=ref, send_sem=send_sems.at[k],
                                                recv_sem=recv_sems.at[k], device_id=to, device_id_type=MESH)

        first, passed = [], []
        for j, (cx, cy) in enumerate(chips):
            for a, ref in enumerate(halves(mine, c)):
                first.append(copy(2 * j + a, ref, (cx, cy, c)))
        for cp in first:
            cp.start()
        for j, (cx, cy) in enumerate(chips):
            for a, ref in enumerate(halves(2 * cx + cy, c)):
                copy(2 * j + a, ref, me).wait_recv()
                fwd = copy(6 + 2 * j + a, ref, sib)
                fwd.start()
                passed.append(fwd)
        for j, (cx, cy) in enumerate(chips):
            for a, ref in enumerate(halves(2 * cx + cy, 1 - c)):
                copy(6 + 2 * j + a, ref, me).wait_recv()
        for cp in first + passed:
            cp.wait_send()

    vm = pl.BlockSpec(memory_space=pltpu.VMEM)
    return pl.pallas_call(
        body, name="gather_weights",
        in_specs=[vm, vm], out_specs=[vm, vm],
        out_shape=[jax.ShapeDtypeStruct((4, D_MODEL, 1024), BF16), jax.ShapeDtypeStruct((4, 256, D_MODEL), BF16)],
        scratch_shapes=[pltpu.SemaphoreType.DMA((12,)), pltpu.SemaphoreType.DMA((12,))],
        compiler_params=pltpu.CompilerParams(vmem_limit_bytes=VMEM_LIMIT),
    )(w_in_s, w_out_s)


def _swap_halves(g_in, g_out):
    def body(gin_ref, gout_ref, rin_ref, rout_ref, send_sems, recv_sems):
        x, y, c = _place()
        sib = (x, y, 1 - c)
        cps = [pltpu.make_async_remote_copy(src_ref=src.at[:, 1 - c], dst_ref=dst, send_sem=send_sems.at[k],
                                            recv_sem=recv_sems.at[k], device_id=sib, device_id_type=MESH)
               for k, (src, dst) in enumerate(((gin_ref, rin_ref), (gout_ref, rout_ref)))]
        for cp in cps:
            cp.start()
        for cp in cps:
            cp.wait()

    hbm = pl.BlockSpec(memory_space=pl.ANY)
    return pl.pallas_call(
        body, name="swap_halves",
        in_specs=[hbm, hbm], out_specs=[hbm, hbm],
        out_shape=[jax.ShapeDtypeStruct((4,) + g.shape[2:], F32) for g in (g_in, g_out)],
        scratch_shapes=[pltpu.SemaphoreType.DMA((2,)), pltpu.SemaphoreType.DMA((2,))],
    )(g_in, g_out)


def _add_half(g, r, cidx, name):
    n, _, rows, cols = g.shape

    def body(c_ref, g_ref, r_ref, o_ref):
        o_ref[0] = g_ref[0, 0] + r_ref[0]

    return pl.pallas_call(
        body, name=name,
        grid_spec=pltpu.PrefetchScalarGridSpec(
            num_scalar_prefetch=1, grid=(n,),
            in_specs=[pl.BlockSpec((1, 1, rows, cols), lambda j, c_ref: (j, c_ref[0], 0, 0)),
                      pl.BlockSpec((1, rows, cols), lambda j, c_ref: (j, 0, 0))],
            out_specs=pl.BlockSpec((1, rows, cols), lambda j, c_ref: (j, 0, 0))),
        out_shape=jax.ShapeDtypeStruct((n, rows, cols), F32),
        compiler_params=_params(("parallel",)),
    )(cidx, g, r)


def _exchange_chips(cs_in, cs_out):
    def body(in_ref, out_ref, rin_ref, rout_ref, send_sems, recv_sems, local_sems):
        x, y, c = _place()
        mine = 2 * x + y
        chips = [(1 - x, y), (x, 1 - y), (1 - x, 1 - y)]
        local = [pltpu.make_async_copy(src.at[mine], dst.at[mine], local_sems.at[k])
                 for k, (src, dst) in enumerate(((in_ref, rin_ref), (out_ref, rout_ref)))]
        for cp in local:
            cp.start()
        cps = []
        for j, (cx, cy) in enumerate(chips):
            for a, (src, dst) in enumerate(((in_ref, rin_ref), (out_ref, rout_ref))):
                cps.append(pltpu.make_async_remote_copy(
                    src_ref=src.at[2 * cx + cy], dst_ref=dst.at[mine], send_sem=send_sems.at[2 * j + a],
                    recv_sem=recv_sems.at[2 * j + a], device_id=(cx, cy, c), device_id_type=MESH))
        for cp in cps:
            cp.start()
        for cp in cps:
            cp.wait()
        for cp in local:
            cp.wait()

    hbm = pl.BlockSpec(memory_space=pl.ANY)
    return pl.pallas_call(
        body, name="exchange_chips",
        in_specs=[hbm, hbm], out_specs=[hbm, hbm],
        out_shape=[jax.ShapeDtypeStruct(a.shape, F32) for a in (cs_in, cs_out)],
        scratch_shapes=[pltpu.SemaphoreType.DMA((6,)), pltpu.SemaphoreType.DMA((6,)), pltpu.SemaphoreType.DMA((2,))],
    )(cs_in, cs_out)


def _sum_chips(r, name):
    _, rows, cols = r.shape
    tr = min(rows, 256)

    def body(r_ref, o_ref):
        o_ref[...] = ((r_ref[0] + r_ref[1]) + r_ref[2]) + r_ref[3]

    return pl.pallas_call(
        body, name=name, grid=(rows // tr,),
        in_specs=[pl.BlockSpec((4, tr, cols), lambda i: (0, i, 0))],
        out_specs=pl.BlockSpec((tr, cols), lambda i: (i, 0)),
        out_shape=jax.ShapeDtypeStruct((rows, cols), F32),
        compiler_params=_params(("parallel",)),
    )(r)


def _join_halves(h_in, h_out):
    def body(in_ref, out_ref, fin_ref, fout_ref, send_sems, recv_sems, local_sems):
        x, y, c = _place()
        sib = (x, y, 1 - c)
        pairs = ((in_ref, fin_ref.at[:, pl.ds(c * 512, 512)]), (out_ref, fout_ref.at[c]))
        local = [pltpu.make_async_copy(src, dst, local_sems.at[k]) for k, (src, dst) in enumerate(pairs)]
        cps = [pltpu.make_async_remote_copy(src_ref=src, dst_ref=dst, send_sem=send_sems.at[k],
                                            recv_sem=recv_sems.at[k], device_id=sib, device_id_type=MESH)
               for k, (src, dst) in enumerate(pairs)]
        for cp in local + cps:
            cp.start()
        for cp in cps + local:
            cp.wait()

    hbm = pl.BlockSpec(memory_space=pl.ANY)
    return pl.pallas_call(
        body, name="join_halves",
        in_specs=[hbm, hbm], out_specs=[hbm, hbm],
        out_shape=[jax.ShapeDtypeStruct((D_MODEL, 1024), F32), jax.ShapeDtypeStruct((2, 128, D_MODEL), F32)],
        scratch_shapes=[pltpu.SemaphoreType.DMA((2,)), pltpu.SemaphoreType.DMA((2,)), pltpu.SemaphoreType.DMA((2,))],
    )(h_in, h_out)


def _allreduce_small(g_nw, g_fw, g_hgw, g_lbl, loss8):
    def body(nw_ref, fw_ref, hgw_ref, lbl_ref, loss_ref, out_ref, slots, send_sems, recv_sems):
        x, y, c = _place()
        me = 4 * x + 2 * y + c
        slots[me] = jnp.zeros((8, D_MODEL), F32)
        slots[me, 0:1, :] = nw_ref[...]
        slots[me, 1:2, :] = fw_ref[...]
        slots[me, 2:3, 0:HG_W] = hgw_ref[...]
        slots[me, 3:4, 0:HG_W] = lbl_ref[0:1, :]
        slots[me, 3:4, HG_W:] = lbl_ref[1:2, :]
        slots[me, 4:5, 0:128] = loss_ref[0:1, :]
        cps = []
        for k in range(1, 8):
            dx, dy, dc = (k >> 2) & 1, (k >> 1) & 1, k & 1
            to = (x ^ dx, y ^ dy, c ^ dc)
            cps.append(pltpu.make_async_remote_copy(
                src_ref=slots.at[me], dst_ref=slots.at[me], send_sem=send_sems.at[k - 1],
                recv_sem=recv_sems.at[k - 1], device_id=to, device_id_type=MESH))
        for cp in cps:
            cp.start()
        for cp in cps:
            cp.wait()
        acc = slots[0]
        for i in range(1, 8):
            acc = acc + slots[i]
        out_ref[...] = acc

    vm = pl.BlockSpec(memory_space=pltpu.VMEM)
    return pl.pallas_call(
        body, name="allreduce_small",
        in_specs=[vm] * 5, out_specs=vm,
        out_shape=jax.ShapeDtypeStruct((8, D_MODEL), F32),
        scratch_shapes=[pltpu.VMEM((8, 8, D_MODEL), F32), pltpu.SemaphoreType.DMA((7,)), pltpu.SemaphoreType.DMA((7,))],
    )(g_nw, g_fw, g_hgw, g_lbl, loss8)


def _rope_tables(S):
    inv_freq = 1.0 / (ROPE_THETA ** (jnp.arange(ROPE_HALF, dtype=F32) / ROPE_HALF))
    ang = jnp.arange(S, dtype=jnp.int32).astype(F32)[:, None] * inv_freq[None, :]
    cos, sin = jnp.cos(ang), jnp.sin(ang)
    cos2 = jnp.concatenate([cos, cos, cos, cos], axis=-1)
    sin2 = jnp.concatenate([-sin, sin, -sin, sin], axis=-1)
    return cos2, sin2


def _local_step(x2, tgt2, norm_w, w_in_full, lbl, hg_norm_w, w_out_full, fnw):
    S = x2.shape[0]
    cos2, sin2 = _rope_tables(S)
    p0, p1, qkv, p3, ut = _inproj(x2, norm_w, w_in_full, cos2, sin2)
    o_hg, mix_hg, mixt_hg, states = _hg_fwd(p0, p1, lbl, hg_norm_w)
    qkv_ds = [qkv.reshape(1, S, 3 * AT_W)] + [_to_dil(qkv, d) for d in DILATIONS[1:]]
    ols = [_from_dil(_att_fwd(q)) for q in qkv_ds]
    ol, mix_at, mixt_at = _att_combine(ols, p3)
    dh, dm_hg, dm_at, g_wout, g_fw, loss8 = _outproj(x2, tgt2, mix_hg, mix_at, mixt_hg, mixt_at, w_out_full, fnw)
    dqr, dfl, dv_hg, dz_hg, g_lbl, g_hgw = _hg_bwd(p0, p1, o_hg, dm_hg, states, lbl, hg_norm_w)
    aux, dz_at = _att_gate_bwd(dm_at, ol, p3)
    aux_ds = [aux.reshape(1, S, 3 * AT_W)] + [_to_dil(aux, d) for d in DILATIONS[1:]]
    dqkvs = [_from_dil(_att_bwd(q, a)) for q, a in zip(qkv_ds, aux_ds)]
    dq_at, dk_at, dv_at = _att_bwd_combine(dqkvs, cos2, sin2)
    dps = [dqr, dfl, dv_hg, dz_hg, dq_at, dk_at, dv_at, dz_at]
    grad_x, g_nw = _inproj_bwd_x(dps, w_in_full, x2, norm_w, dh)
    g_win = _inproj_bwd_w(ut, dps)
    return loss8, grad_x, g_nw, g_win, g_lbl, g_hgw, g_wout, g_fw


def kernel(x, norm_w, w_in, hgrn_lb_logits, hg_norm_w, w_out, final_norm_w, loss_target, m_norm_w, m_w_in, m_hgrn_lb_logits, m_hg_norm_w, m_w_out, m_final_norm_w, v_norm_w, v_w_in, v_hgrn_lb_logits, v_hg_norm_w, v_w_out, v_final_norm_w):
    S = x.shape[1]
    w_in_full, w_out_full = _gather_weights(w_in[0], w_out[0])
    loss8, grad_x, g_nw, g_win, g_lbl, g_hgw, g_wout, g_fw = _local_step(
        x[0], loss_target[0], norm_w, w_in_full, hgrn_lb_logits, hg_norm_w,
        w_out_full.reshape(D_MODEL, D_MODEL), final_norm_w.reshape(1, D_MODEL))

    cidx = lax.axis_index("c").astype(jnp.int32).reshape(1)
    g_in4 = jnp.stack(g_win).reshape(4, 2, D_MODEL, 512)
    g_out4 = g_wout.reshape(4, 2, 128, D_MODEL)
    r_in, r_out = _swap_halves(g_in4, g_out4)
    cs_in = _add_half(g_in4, r_in, cidx, "add_half_in")
    cs_out = _add_half(g_out4, r_out, cidx, "add_half_out")
    x_in, x_out = _exchange_chips(cs_in, cs_out)
    h_in = _sum_chips(x_in, "sum_chips_in")
    h_out = _sum_chips(x_out, "sum_chips_out")
    grad_w_in, f_out = _join_halves(h_in, h_out)
    grad_w_out = f_out.reshape(256, D_MODEL)

    red = _allreduce_small(g_nw, g_fw, g_hgw, g_lbl, loss8)
    loss = red[4, 0]
    grad_norm_w = red[0:1, :]
    grad_final_norm_w = red[1, :]
    grad_hg_norm_w = red[2:3, :HG_W]
    grad_lbl = jnp.concatenate([red[3:4, :HG_W], red[3:4, HG_W:]], axis=0)

    d_nw, m_nw, v_nw = _adamw(norm_w, grad_norm_w, m_norm_w, v_norm_w, "adamw_norm_w")
    d_win, m_win, v_win = _adamw(w_in[0], grad_w_in, m_w_in[0], v_w_in[0], "adamw_w_in")
    d_lbl, m_lbl, v_lbl = _adamw(hgrn_lb_logits, grad_lbl, m_hgrn_lb_logits, v_hgrn_lb_logits, "adamw_lb_logits")
    d_hgw, m_hgw, v_hgw = _adamw(hg_norm_w, grad_hg_norm_w, m_hg_norm_w, v_hg_norm_w, "adamw_hg_norm_w")
    d_wout, m_wout, v_wout = _adamw(w_out[0], grad_w_out, m_w_out[0], v_w_out[0], "adamw_w_out")
    d_fw, m_fw, v_fw = _adamw(final_norm_w.reshape(1, D_MODEL), grad_final_norm_w.reshape(1, D_MODEL),
                              m_final_norm_w.reshape(1, D_MODEL), v_final_norm_w.reshape(1, D_MODEL), "adamw_final_norm_w")
    e1 = lambda a: a[None]
    flat = lambda a: a.reshape(D_MODEL)
    return (loss, grad_x[None], grad_norm_w, e1(grad_w_in), grad_lbl, grad_hg_norm_w, e1(grad_w_out), grad_final_norm_w,
            d_nw, e1(d_win), d_lbl, d_hgw, e1(d_wout), flat(d_fw),
            m_nw, e1(m_win), m_lbl, m_hgw, e1(m_wout), flat(m_fw),
            v_nw, e1(v_win), v_lbl, v_hgw, e1(v_wout), flat(v_fw))
```

```python
import functools

import jax
import jax.numpy as jnp
import numpy as np
from jax import lax
from jax.experimental import pallas as pl
from jax.experimental.pallas import tpu as pltpu

F32 = jnp.float32
BF16 = jnp.bfloat16
MESH = pl.DeviceIdType.MESH

D_MODEL = 1024
HG_W = 512
AT_W = 512
HEAD_PAIR = 128
ROPE_HALF = 32
ROPE_THETA = 10000.0
EPS = 1e-6
CHUNK = 128
LEVELS = (64, 32, 16, 8)
DIAG = 8
ATT_BLK = 128
DILATIONS = (1, 4, 16)
ATT_SCALE = 0.125
NEG = -1e30
VMEM_LIMIT = 56 * 1024 * 1024

ADAM_LR = 0.001
ADAM_B1 = 0.9
ADAM_B2 = 0.999
ADAM_EPS = 1e-08
ADAM_WD = 0.01
ADAM_STEP = 10


def _iota(shape, dim):
    return lax.broadcasted_iota(jnp.int32, shape, dim)


def _dot(a, b):
    return jnp.dot(a, b, preferred_element_type=F32)


def _dot_nt(a, b):
    return lax.dot_general(a, b, (((1,), (1,)), ((), ())), preferred_element_type=F32)


def _dot_hi(a, b):
    return jnp.dot(a, b, preferred_element_type=F32, precision=lax.Precision.HIGHEST)


def _sig(v):
    return 1.0 / (1.0 + jnp.exp(-v))


def _params(sem=None, vmem=VMEM_LIMIT):
    return pltpu.CompilerParams(dimension_semantics=sem, vmem_limit_bytes=vmem)


def _full(shape):
    n = len(shape)
    return pl.BlockSpec(shape, lambda *_: (0,) * n)


def _rope_rot(y):
    n = y.shape[1]
    first = (_iota(y.shape, 1) & (2 * ROPE_HALF - 1)) < ROPE_HALF
    return jnp.where(first, pltpu.roll(y, n - ROPE_HALF, 1), pltpu.roll(y, ROPE_HALF, 1))


def _dil_spec(d, tm, width):
    return pl.BlockSpec((d, tm // d, width), lambda s: (0, s, 0))


LANES = 128


def _slab_scratch(tm, width):
    return pltpu.VMEM((width // LANES, tm, LANES), F32)


def _to_slabs(v, slabs_ref):
    for j in range(slabs_ref.shape[0]):
        slabs_ref[j] = v[:, LANES * j: LANES * (j + 1)]


def _from_slabs(slabs_ref):
    return jnp.concatenate([slabs_ref[j] for j in range(slabs_ref.shape[0])], axis=1)


def _split_residues(slabs_ref, dst_ref, d, dtype):
    nslab, tm, _ = slabs_ref.shape
    for r in range(d):
        for j in range(nslab):
            dst_ref[r, :, LANES * j: LANES * (j + 1)] = slabs_ref[j, pl.ds(r, tm // d, stride=d), :].astype(dtype)


def _merge_residues(src_ref, slabs_ref, d):
    nslab, tm, _ = slabs_ref.shape
    for r in range(d):
        for j in range(nslab):
            slabs_ref[j, pl.ds(r, tm // d, stride=d), :] = src_ref[r, :, LANES * j: LANES * (j + 1)]


def _inproj(x2, norm_w, w_in_full, cos2, sin2):
    S = x2.shape[0]
    tm = 256

    def body(x_ref, nw_ref, w_ref, cos_ref, sin_ref, p0_ref, p1_ref, qkv_ref, qkv4_ref, qkv16_ref, p3_ref, ut_ref, scr):
        x = x_ref[...]
        r = lax.rsqrt(jnp.mean(x * x, axis=-1, keepdims=True) + EPS)
        u = x * r * nw_ref[...]
        ub = u.astype(BF16)
        ut_ref[...] = u.T.astype(BF16)
        p0_ref[...] = _dot(ub, w_ref[0])
        p1_ref[...] = _dot(ub, w_ref[1])
        y2 = _dot(ub, w_ref[2])
        cosf = jnp.tile(cos_ref[...], (1, 8))
        sinf = jnp.tile(sin_ref[...], (1, 8))
        y3 = _dot(ub, w_ref[3])
        p3_ref[...] = y3
        qkv = jnp.concatenate([y2 * cosf + _rope_rot(y2) * sinf, y3[:, :AT_W]], axis=1)
        qkv_ref[...] = qkv.astype(BF16)
        _to_slabs(qkv, scr)
        _split_residues(scr, qkv4_ref, 4, BF16)
        _split_residues(scr, qkv16_ref, 16, BF16)

    row = lambda w: pl.BlockSpec((tm, w), lambda s: (s, 0))
    qkv_w = 3 * AT_W
    return pl.pallas_call(
        body, name="inproj", grid=(S // tm,),
        in_specs=[row(D_MODEL), _full((1, D_MODEL)), _full((4, D_MODEL, 1024)), row(128), row(128)],
        out_specs=[row(1024), row(1024), row(qkv_w), _dil_spec(4, tm, qkv_w), _dil_spec(16, tm, qkv_w), row(1024),
                   pl.BlockSpec((D_MODEL, tm), lambda s: (0, s))],
        out_shape=[jax.ShapeDtypeStruct((S, 1024), F32), jax.ShapeDtypeStruct((S, 1024), F32),
                   jax.ShapeDtypeStruct((S, qkv_w), BF16), jax.ShapeDtypeStruct((4, S // 4, qkv_w), BF16),
                   jax.ShapeDtypeStruct((16, S // 16, qkv_w), BF16), jax.ShapeDtypeStruct((S, 1024), F32),
                   jax.ShapeDtypeStruct((D_MODEL, S), BF16)],
        scratch_shapes=[_slab_scratch(tm, qkv_w)],
        compiler_params=_params(("parallel",)),
    )(x2, norm_w, w_in_full, cos2, sin2)


HG_HPS = 4
N_LEV = len(LEVELS)


def _hg_const_arrays():
    r = np.arange(CHUNK)[:, None]
    c = np.arange(CHUNK)[None, :]
    tris = np.stack([r >= c, r <= c])
    lm = [((r // (2 * m)) == (c // (2 * m))) & (r % (2 * m) >= m) & (c % (2 * m) < m) for m in LEVELS]
    dm = [(c == r - dl) & (r % DIAG >= dl) for dl in range(DIAG)]
    masks = np.stack(lm + [x.T for x in lm] + dm)
    return jnp.asarray(tris, BF16), jnp.asarray(masks, F32)


def _split2(a):
    hi = a.astype(BF16)
    return hi, (a - hi.astype(F32)).astype(BF16)


def _dot3(a, b):
    ah, al = _split2(a)
    bh, bl = _split2(b)
    n = b.shape[1]
    p = _dot(ah, jnp.concatenate([bh, bl], axis=1))
    return (p[:, :n] + p[:, n:]) + _dot(al, bh)


def _tri_dot(tri, a):
    a1 = a.astype(BF16)
    r1 = a - a1.astype(F32)
    a2 = r1.astype(BF16)
    a3 = (r1 - a2.astype(F32)).astype(BF16)
    n = a.shape[1]
    p = _dot(tri, jnp.concatenate([a1, a2, a3], axis=1))
    return (p[:, :n] + p[:, n:2 * n]) + p[:, 2 * n:]


def _rowsum(t):
    return _dot(t.astype(BF16), jnp.ones((t.shape[1], t.shape[1]), BF16))


def _level_refs(b):
    refs = []
    for m in LEVELS:
        parts = [jnp.broadcast_to(b[r0 + m - 1: r0 + m, :], (2 * m, b.shape[1])) for r0 in range(0, CHUNK, 2 * m)]
        refs.append(parts[0] if len(parts) == 1 else jnp.concatenate(parts, axis=0))
    return refs


def _hg_lb(lbl_ref):
    l0 = lbl_ref[0:1, :]
    l1 = lbl_ref[1:2, :]
    mx = jnp.maximum(l0, l1)
    e0 = jnp.exp(l0 - mx)
    e1 = jnp.exp(l1 - mx)
    p0 = e0 / (e0 + e1)
    lb = jnp.clip(p0, 1e-6, 1.0 - 1e-6)
    inside = (p0 >= 1e-6) & (p0 <= 1.0 - 1e-6)
    dlb_dl0 = jnp.where(inside, p0 * (e1 / (e0 + e1)), 0.0)
    return lb, dlb_dl0


def _sigmoid(v):
    return 0.5 * jnp.tanh(0.5 * v) + 0.5


def _hg_gates(qr, fl, lb):
    sig = _sigmoid(fl)
    f = lb + (1.0 - lb) * sig
    g = jnp.log(f)
    k = (1.0 - lb) * (1.0 - sig)
    sq = _sigmoid(qr)
    q = qr * sq
    return sig, f, g, k, sq, q


def _hg_levels(q, k, b, mk_ref):
    refs = _level_refs(b)
    a = jnp.zeros((CHUNK, CHUNK), F32)
    eqs, eks, qts, kts = [], [], [], []
    for i in range(N_LEV):
        eq = jnp.exp(jnp.minimum(b - refs[i], 0.0))
        ek = jnp.exp(jnp.minimum(refs[i] - b, 0.0))
        qt = (q * eq).astype(BF16)
        kt = (k * ek).astype(BF16)
        a = a + _dot_nt(qt, kt) * mk_ref[i]
        eqs.append(eq); eks.append(ek); qts.append(qt); kts.append(kt)
    return a, eqs, eks, qts, kts


def _hg_specs(nc, rev):
    cc = (lambda c: nc - 1 - c) if rev else (lambda c: c)
    w = 128 * HG_HPS
    blk = lambda off: pl.BlockSpec((CHUNK, w), lambda h, c: (cc(c), h + off))
    vec = pl.BlockSpec((1, w), lambda h, c: (0, h))
    lb2 = pl.BlockSpec((2, w), lambda h, c: (0, h))
    st = pl.BlockSpec((1, HG_HPS, 128, 128), lambda h, c: (cc(c), h, 0, 0))
    consts = [_full((2, CHUNK, CHUNK)), _full((2 * N_LEV + DIAG, CHUNK, CHUNK))]
    return blk, vec, lb2, st, consts


def _hg_fwd(p0, p1, lbl, gw):
    S = p0.shape[0]
    nc = S // CHUNK
    ng = 4 // HG_HPS

    def body(qr_ref, fl_ref, v_ref, z_ref, lbl_ref, gw_ref, tri_ref, mk_ref,
             o_ref, mix_ref, mixt_ref, st_ref, state):
        c = pl.program_id(1)

        @pl.when(c == 0)
        def _():
            state[...] = jnp.zeros_like(state)

        lb_all, _ = _hg_lb(lbl_ref)
        for hh in range(HG_HPS):
            sl = slice(128 * hh, 128 * (hh + 1))
            qr, fl, v, z = qr_ref[:, sl], fl_ref[:, sl], v_ref[:, sl], z_ref[:, sl]
            _, _, g, k, _, q = _hg_gates(qr, fl, lb_all[:, sl])
            b = _tri_dot(tri_ref[0], g)
            b_last = b[CHUNK - 1: CHUNK, :]
            a, _, _, _, _ = _hg_levels(q, k, b, mk_ref)
            for dl in range(DIAG):
                if dl == 0:
                    term = q * k
                else:
                    e = jnp.exp(jnp.minimum(b - pltpu.roll(b, dl, 0), 0.0))
                    term = q * pltpu.roll(k, dl, 0) * e
                a = a + _rowsum(term) * mk_ref[2 * N_LEV + dl]
            st = state[hh]
            st_ref[0, hh] = st
            o = _dot_nt((q * jnp.exp(b)).astype(BF16), st.astype(BF16)) + _dot(a.astype(BF16), v.astype(BF16))
            state[hh] = st * jnp.exp(b_last) + _dot3(v.T, k * jnp.exp(b_last - b))
            o_ref[:, sl] = o
            rs = lax.rsqrt(jnp.mean(o * o, axis=-1, keepdims=True) + EPS)
            mixed = o * rs * gw_ref[:, sl] * (z * _sigmoid(z))
            mix_ref[:, sl] = mixed.astype(BF16)
            mixt_ref[sl, :] = mixed.T.astype(BF16)

    blk, vec, lb2, st_spec, consts = _hg_specs(nc, False)
    tris, masks = _hg_const_arrays()
    return pl.pallas_call(
        body, name="hg_fwd", grid=(ng, nc),
        in_specs=[blk(0), blk(ng), blk(0), blk(ng), lb2, vec] + consts,
        out_specs=[blk(0), blk(0), pl.BlockSpec((128 * HG_HPS, CHUNK), lambda h, c: (h, c)), st_spec],
        out_shape=[jax.ShapeDtypeStruct((S, HG_W), F32), jax.ShapeDtypeStruct((S, HG_W), BF16),
                   jax.ShapeDtypeStruct((HG_W, S), BF16), jax.ShapeDtypeStruct((nc, 4, 128, 128), F32)],
        scratch_shapes=[pltpu.VMEM((HG_HPS, 128, 128), F32)],
        compiler_params=_params(("parallel", "arbitrary")),
    )(p0, p0, p1, p1, lbl, gw, tris, masks)


def _hg_bwd(p0, p1, o_raw, dm, states, lbl, gw):
    S = p0.shape[0]
    nc = S // CHUNK
    ng = 4 // HG_HPS
    w = 128 * HG_HPS

    def body(qr_ref, fl_ref, v_ref, z_ref, o_ref, dm_ref, st_ref, lbl_ref, gw_ref, tri_ref, mk_ref,
             dqr_ref, dfl_ref, dv_ref, dz_ref, glbl_ref, ggw_ref, dstate, carry, acc_lb, acc_gw):
        c = pl.program_id(1)

        @pl.when(c == 0)
        def _():
            dstate[...] = jnp.zeros_like(dstate)
            carry[...] = jnp.zeros_like(carry)
            acc_lb[...] = jnp.zeros_like(acc_lb)
            acc_gw[...] = jnp.zeros_like(acc_gw)

        lb_all, dlb_dl0 = _hg_lb(lbl_ref)
        for hh in range(HG_HPS):
            sl = slice(128 * hh, 128 * (hh + 1))
            lb = lb_all[:, sl]
            qr, fl, v, z = qr_ref[:, sl], fl_ref[:, sl], v_ref[:, sl], z_ref[:, sl]
            sig, f, g, k, sq, q = _hg_gates(qr, fl, lb)
            b = _tri_dot(tri_ref[0], g)
            b_last = b[CHUNK - 1: CHUNK, :]
            a, eqs, eks, qts, kts = _hg_levels(q, k, b, mk_ref)

            o = o_ref[:, sl]
            dmix = dm_ref[:, sl]
            gwv = gw_ref[:, sl]
            rs = lax.rsqrt(jnp.mean(o * o, axis=-1, keepdims=True) + EPS)
            oh = o * rs
            sz = _sigmoid(z)
            dz_ref[:, sl] = dmix * (oh * gwv) * (sz * (1.0 + z * (1.0 - sz)))
            don = dmix * (z * sz)
            acc_gw[0:1, sl] += jnp.sum(don * oh, axis=0, keepdims=True)
            dy = don * gwv
            do = rs * (dy - oh * jnp.mean(dy * oh, axis=-1, keepdims=True))

            st = st_ref[0, hh]
            dst = dstate[hh]
            dob = do.astype(BF16)
            eb = jnp.exp(b)
            edec = jnp.exp(b_last - b)
            dq = _dot3(do, st) * eb
            dk = _dot3(v, dst) * edec
            db = q * dq - k * dk
            da = _dot_nt(dob, v.astype(BF16))
            da_t = da.T
            for i in range(N_LEV):
                gq = _dot((da * mk_ref[i]).astype(BF16), kts[i])
                gk = _dot((da_t * mk_ref[N_LEV + i]).astype(BF16), qts[i])
                dq = dq + eqs[i] * gq
                dk = dk + eks[i] * gk
                db = db + (qts[i].astype(F32) * gq - kts[i].astype(F32) * gk)
            for dl in range(DIAG):
                dmask = mk_ref[2 * N_LEV + dl]
                dc = _rowsum(da * dmask)
                if dl == 0:
                    a = a + _rowsum(q * k) * dmask
                    dq = dq + dc * k
                    dk = dk + dc * q
                else:
                    e = jnp.exp(jnp.minimum(b - pltpu.roll(b, dl, 0), 0.0))
                    ks = pltpu.roll(k, dl, 0)
                    a = a + _rowsum(q * ks * e) * dmask
                    tq = dc * ks * e
                    tk = pltpu.roll(dc * q * e, CHUNK - dl, 0)
                    dq = dq + tq
                    dk = dk + tk
                    db = db + (q * tq - k * tk)
            dv_ref[:, sl] = _dot(a.T.astype(BF16), dob) + _dot_nt((k * edec).astype(BF16), dst.astype(BF16))
            dstate[hh] = dst * jnp.exp(b_last) + _dot3(do.T, q * eb)

            dg = _tri_dot(tri_ref[1], db) + carry[0:1, sl]
            carry[0:1, sl] += jnp.sum(db, axis=0, keepdims=True)
            t = dg / f - dk
            dfl_ref[:, sl] = t * (1.0 - lb) * sig * (1.0 - sig)
            acc_lb[0:1, sl] += jnp.sum(t * (1.0 - sig), axis=0, keepdims=True)
            dqr_ref[:, sl] = dq * (sq * (1.0 + qr * (1.0 - sq)))

        @pl.when(c == nc - 1)
        def _():
            gl0 = acc_lb[0:1, :] * dlb_dl0
            glbl_ref[0:1, :] = gl0
            glbl_ref[1:2, :] = -gl0
            ggw_ref[...] = acc_gw[0:1, :]

    blk, vec, lb2, st_spec, consts = _hg_specs(nc, True)
    tris, masks = _hg_const_arrays()
    act = jax.ShapeDtypeStruct((S, HG_W), F32)
    return pl.pallas_call(
        body, name="hg_bwd", grid=(ng, nc),
        in_specs=[blk(0), blk(ng), blk(0), blk(ng), blk(0), blk(0), st_spec, lb2, vec] + consts,
        out_specs=[blk(0), blk(0), blk(0), blk(0), lb2, vec],
        out_shape=[act, act, act, act, jax.ShapeDtypeStruct((2, HG_W), F32), jax.ShapeDtypeStruct((1, HG_W), F32)],
        scratch_shapes=[pltpu.VMEM((HG_HPS, 128, 128), F32), pltpu.VMEM((8, w), F32),
                        pltpu.VMEM((8, w), F32), pltpu.VMEM((8, w), F32)],
        compiler_params=_params(("parallel", "arbitrary")),
    )(p0, p0, p1, p1, o_raw, dm, states, lbl, gw, tris, masks)


def _att_bias():
    qi = np.arange(ATT_BLK)[:, None]
    kj = np.arange(2 * ATT_BLK)[None, :]
    band = (kj >= qi) & (kj <= qi + ATT_BLK)
    qm = np.stack([band & (kj >= ATT_BLK), band])
    cur = (kj < ATT_BLK) & (qi <= kj)
    km = np.stack([cur, cur | ((kj >= ATT_BLK) & (qi >= kj - ATT_BLK))])
    to_bias = lambda m: jnp.asarray(np.where(m, 0.0, NEG), F32)
    return to_bias(qm), to_bias(km)


def _att_fwd(qkv_d):
    d, L, _ = qkv_d.shape
    nb = L // ATT_BLK

    def body(q_ref, kp_ref, kc_ref, vp_ref, vc_ref, bias_ref, ol_ref):
        bias = bias_ref[jnp.minimum(pl.program_id(1), 1)]
        bias2 = jnp.concatenate([bias, bias], axis=0)
        first = _iota((ATT_BLK, HEAD_PAIR), 1) < 64
        for hp in range(4):
            sl = slice(HEAD_PAIR * hp, HEAD_PAIR * (hp + 1))
            q2 = q_ref[0, :, sl] * ATT_SCALE
            zero = jnp.zeros_like(q2)
            kcat = jnp.concatenate([kp_ref[0, :, sl], kc_ref[0, :, sl]], axis=0)
            vcat = jnp.concatenate([vp_ref[0, :, sl], vc_ref[0, :, sl]], axis=0)
            qs = jnp.concatenate([jnp.where(first, q2, zero), jnp.where(first, zero, q2)], axis=0)
            s = _dot_nt(qs, kcat) + bias2
            m = jnp.max(s, axis=-1, keepdims=True)
            p = jnp.exp(s - m)
            l = jnp.sum(p, axis=-1, keepdims=True)
            o = _dot(p.astype(BF16), vcat) / l
            lse = m + jnp.log(l)
            ol_ref[0, :, sl] = jnp.where(first, o[:ATT_BLK], o[ATT_BLK:])
            ol_ref[0, :, AT_W + HEAD_PAIR * hp: AT_W + HEAD_PAIR * (hp + 1)] = jnp.where(first, lse[:ATT_BLK], lse[ATT_BLK:])

    cur = lambda j: pl.BlockSpec((1, ATT_BLK, AT_W), lambda r, n: (r, n, j))
    prev = lambda j: pl.BlockSpec((1, ATT_BLK, AT_W), lambda r, n: (r, jnp.maximum(n - 1, 0), j))
    return pl.pallas_call(
        body, name=f"att_fwd_d{d}", grid=(d, nb),
        in_specs=[cur(0), prev(1), cur(1), prev(2), cur(2), _full((2, ATT_BLK, 2 * ATT_BLK))],
        out_specs=pl.BlockSpec((1, ATT_BLK, 2 * AT_W), lambda r, n: (r, n, 0)),
        out_shape=jax.ShapeDtypeStruct((d, L, 2 * AT_W), F32),
        compiler_params=_params(("parallel", "parallel")),
    )(qkv_d, qkv_d, qkv_d, qkv_d, qkv_d, _att_bias()[0])


def _att_combine(ols, p3):
    S = p3.shape[0]
    tm = 512

    def body(a_ref, b4_ref, c16_ref, z_ref, ol_ref, mix_ref, mixt_ref, b_scr, c_scr):
        _merge_residues(b4_ref, b_scr, 4)
        _merge_residues(c16_ref, c_scr, 16)
        pats = (a_ref[...], _from_slabs(b_scr), _from_slabs(c_scr))
        os_ = [v[:, :AT_W] for v in pats]
        ls = [v[:, AT_W:] for v in pats]
        mx = jnp.maximum(jnp.maximum(ls[0], ls[1]), ls[2])
        es = [jnp.exp(l - mx) for l in ls]
        zs = es[0] + es[1] + es[2]
        o = (es[0] * os_[0] + es[1] * os_[1] + es[2] * os_[2]) / zs
        ol_ref[:, :AT_W] = o
        ol_ref[:, AT_W:] = mx + jnp.log(zs)
        z = z_ref[...]
        mixed = o * (z * _sig(z))
        mix_ref[...] = mixed.astype(BF16)
        mixt_ref[...] = mixed.T.astype(BF16)

    row = lambda w: pl.BlockSpec((tm, w), lambda s: (s, 0))
    return pl.pallas_call(
        body, name="att_combine", grid=(S // tm,),
        in_specs=[row(1024), _dil_spec(4, tm, 1024), _dil_spec(16, tm, 1024), pl.BlockSpec((tm, AT_W), lambda s: (s, 1))],
        out_specs=[row(1024), row(AT_W), pl.BlockSpec((AT_W, tm), lambda s: (0, s))],
        out_shape=[jax.ShapeDtypeStruct((S, 1024), F32), jax.ShapeDtypeStruct((S, AT_W), BF16),
                   jax.ShapeDtypeStruct((AT_W, S), BF16)],
        scratch_shapes=[_slab_scratch(tm, 1024), _slab_scratch(tm, 1024)],
        compiler_params=_params(("parallel",)),
    )(ols[0].reshape(S, 1024), ols[1], ols[2], p3)


def _att_gate_bwd(dm_at, ol, p3):
    S = p3.shape[0]
    tm = 512

    def body(dm_ref, ol_ref, z_ref, aux_ref, aux4_ref, aux16_ref, dz_ref, scr):
        o = ol_ref[:, :AT_W]
        z = z_ref[...]
        dm = dm_ref[...]
        sz = _sig(z)
        dz_ref[...] = dm * o * (sz * (1.0 + z * (1.0 - sz)))
        do = dm * (z * sz)
        aux_ref[:, :AT_W] = do
        ones = ((_iota((HEAD_PAIR, HEAD_PAIR), 0) < 64) == (_iota((HEAD_PAIR, HEAD_PAIR), 1) < 64)).astype(F32)
        prod = do * o
        for hp in range(4):
            sl = slice(HEAD_PAIR * hp, HEAD_PAIR * (hp + 1))
            aux_ref[:, AT_W + HEAD_PAIR * hp: AT_W + HEAD_PAIR * (hp + 1)] = _dot_hi(prod[:, sl], ones)
        aux_ref[:, 2 * AT_W:] = ol_ref[:, AT_W:]
        _to_slabs(aux_ref[...], scr)
        _split_residues(scr, aux4_ref, 4, F32)
        _split_residues(scr, aux16_ref, 16, F32)

    row = lambda w: pl.BlockSpec((tm, w), lambda s: (s, 0))
    aw = 3 * AT_W
    return pl.pallas_call(
        body, name="att_gate_bwd", grid=(S // tm,),
        in_specs=[row(AT_W), row(1024), pl.BlockSpec((tm, AT_W), lambda s: (s, 1))],
        out_specs=[row(aw), _dil_spec(4, tm, aw), _dil_spec(16, tm, aw), row(AT_W)],
        out_shape=[jax.ShapeDtypeStruct((S, aw), F32), jax.ShapeDtypeStruct((4, S // 4, aw), F32),
                   jax.ShapeDtypeStruct((16, S // 16, aw), F32), jax.ShapeDtypeStruct((S, AT_W), F32)],
        scratch_shapes=[_slab_scratch(tm, aw)],
        compiler_params=_params(("parallel",)),
    )(dm_at, ol, p3)


def _att_bwd(qkv_d, aux_d):
    d, L, _ = qkv_d.shape
    nb = L // ATT_BLK

    def body(qc_ref, qn_ref, kp_ref, kc_ref, vp_ref, vc_ref, ac_ref, an_ref, bq_ref, bk_ref, dqkv_ref):
        n = pl.program_id(1)
        bias = bq_ref[jnp.minimum(n, 1)]
        bias_t = bk_ref[jnp.minimum(nb - 1 - n, 1)]
        first = _iota((ATT_BLK, HEAD_PAIR), 1) < 64
        for hp in range(4):
            sl = slice(HEAD_PAIR * hp, HEAD_PAIR * (hp + 1))
            sl_d = slice(AT_W + HEAD_PAIR * hp, AT_W + HEAD_PAIR * (hp + 1))
            sl_l = slice(2 * AT_W + HEAD_PAIR * hp, 2 * AT_W + HEAD_PAIR * (hp + 1))
            q2 = qc_ref[0, :, sl]
            k2 = kc_ref[0, :, sl]
            v2 = vc_ref[0, :, sl]
            kcat = jnp.concatenate([kp_ref[0, :, sl], k2], axis=0)
            vcat = jnp.concatenate([vp_ref[0, :, sl], v2], axis=0)
            qcat = jnp.concatenate([q2, qn_ref[0, :, sl]], axis=0)
            q2s = q2 * ATT_SCALE
            k2s = k2 * ATT_SCALE
            do_c = ac_ref[0, :, sl]
            docat = jnp.concatenate([do_c, an_ref[0, :, sl]], axis=0).astype(BF16)
            dd_c = ac_ref[0, :, sl_d]
            ls_c = ac_ref[0, :, sl_l]
            dd_t = jnp.concatenate([dd_c, an_ref[0, :, sl_d]], axis=0).T
            ls_t = jnp.concatenate([ls_c, an_ref[0, :, sl_l]], axis=0).T
            do_cb = do_c.astype(BF16)
            dqs, dks, dvs = [], [], []
            for hh, hm in enumerate((first, jnp.logical_not(first))):
                lane0 = 64 * hh
                qm = jnp.where(hm, q2s, jnp.zeros_like(q2))
                s = _dot_nt(qm, kcat) + bias
                p = jnp.exp(s - ls_c[:, lane0: lane0 + 1])
                dp = _dot_nt(jnp.where(hm, do_cb, jnp.zeros_like(do_cb)), vcat)
                ds = p * (dp - dd_c[:, lane0: lane0 + 1])
                dqs.append(_dot(ds.astype(BF16), kcat) * ATT_SCALE)
                km = jnp.where(hm, k2s, jnp.zeros_like(k2))
                vm = jnp.where(hm, v2, jnp.zeros_like(v2))
                st = _dot_nt(km, qcat) + bias_t
                pt = jnp.exp(st - ls_t[lane0: lane0 + 1, :])
                dpt = _dot_nt(vm, docat)
                dst = pt * (dpt - dd_t[lane0: lane0 + 1, :])
                dvs.append(_dot(pt.astype(BF16), docat))
                dks.append(_dot(dst.astype(BF16), qcat) * ATT_SCALE)
            dqkv_ref[0, :, sl] = jnp.where(first, dqs[0], dqs[1])
            dqkv_ref[0, :, AT_W + HEAD_PAIR * hp: AT_W + HEAD_PAIR * (hp + 1)] = jnp.where(first, dks[0], dks[1])
            dqkv_ref[0, :, 2 * AT_W + HEAD_PAIR * hp: 2 * AT_W + HEAD_PAIR * (hp + 1)] = jnp.where(first, dvs[0], dvs[1])

    cur = lambda j: pl.BlockSpec((1, ATT_BLK, AT_W), lambda r, n: (r, n, j))
    prev = lambda j: pl.BlockSpec((1, ATT_BLK, AT_W), lambda r, n: (r, jnp.maximum(n - 1, 0), j))
    nxt = lambda j: pl.BlockSpec((1, ATT_BLK, AT_W), lambda r, n: (r, jnp.minimum(n + 1, nb - 1), j))
    return pl.pallas_call(
        body, name=f"att_bwd_d{d}", grid=(d, nb),
        in_specs=[cur(0), nxt(0), prev(1), cur(1), prev(2), cur(2),
                  pl.BlockSpec((1, ATT_BLK, 3 * AT_W), lambda r, n: (r, n, 0)),
                  pl.BlockSpec((1, ATT_BLK, 3 * AT_W), lambda r, n: (r, jnp.minimum(n + 1, nb - 1), 0)),
                  _full((2, ATT_BLK, 2 * ATT_BLK)), _full((2, ATT_BLK, 2 * ATT_BLK))],
        out_specs=pl.BlockSpec((1, ATT_BLK, 3 * AT_W), lambda r, n: (r, n, 0)),
        out_shape=jax.ShapeDtypeStruct((d, L, 3 * AT_W), F32),
        compiler_params=_params(("parallel", "parallel")),
    )(qkv_d, qkv_d, qkv_d, qkv_d, qkv_d, qkv_d, aux_d, aux_d, *_att_bias())


def _att_bwd_combine(dqkvs, cos2, sin2):
    S = dqkvs[0].shape[1]
    tm = 512

    def body(a_ref, b4_ref, c16_ref, cos_ref, sin_ref, dq_ref, dk_ref, dv_ref, b_scr, c_scr):
        _merge_residues(b4_ref, b_scr, 4)
        _merge_residues(c16_ref, c_scr, 16)
        t = a_ref[...] + _from_slabs(b_scr) + _from_slabs(c_scr)
        dy = t[:, : 2 * AT_W]
        cosf = jnp.tile(cos_ref[...], (1, 8))
        sinf = jnp.tile(sin_ref[...], (1, 8))
        dx = dy * cosf - _rope_rot(dy) * sinf
        dq_ref[...] = dx[:, :AT_W]
        dk_ref[...] = dx[:, AT_W:]
        dv_ref[...] = t[:, 2 * AT_W:]

    row = lambda w: pl.BlockSpec((tm, w), lambda s: (s, 0))
    act = jax.ShapeDtypeStruct((S, AT_W), F32)
    return pl.pallas_call(
        body, name="att_bwd_combine", grid=(S // tm,),
        in_specs=[row(3 * AT_W), _dil_spec(4, tm, 3 * AT_W), _dil_spec(16, tm, 3 * AT_W), row(128), row(128)],
        out_specs=[row(AT_W), row(AT_W), row(AT_W)],
        out_shape=[act, act, act],
        scratch_shapes=[_slab_scratch(tm, 3 * AT_W), _slab_scratch(tm, 3 * AT_W)],
        compiler_params=_params(("parallel",)),
    )(dqkvs[0].reshape(S, 3 * AT_W), dqkvs[1], dqkvs[2], cos2, sin2)


def _outproj(x2, tgt2, mix_hg, mix_at, mixt_hg, mixt_at, w_out_full, fnw):
    S = x2.shape[0]
    tm = 256
    ns = S // tm

    def body(x_ref, t_ref, mh_ref, ma_ref, mht_ref, mat_ref, w_ref, fw_ref,
             dh_ref, dmh_ref, dma_ref, gw_ref, gfw_ref, loss_ref):
        s = pl.program_id(0)

        @pl.when(s == 0)
        def _():
            gw_ref[...] = jnp.zeros_like(gw_ref)
            gfw_ref[...] = jnp.zeros_like(gfw_ref)
            loss_ref[...] = jnp.zeros_like(loss_ref)

        y = _dot(mh_ref[...], w_ref[:HG_W, :]) + _dot(ma_ref[...], w_ref[HG_W:, :])
        h = x_ref[...] + y
        r = lax.rsqrt(jnp.mean(h * h, axis=-1, keepdims=True) + EPS)
        hn = h * r
        fw = fw_ref[...]
        err = hn * fw - t_ref[...]
        loss_ref[...] += 0.5 * jnp.sum(jnp.mean(err * err, axis=-1, keepdims=True))
        dout = err * (1.0 / D_MODEL)
        gfw_ref[...] += jnp.sum(dout * hn, axis=0, keepdims=True)
        dhn = dout * fw
        dh = r * (dhn - hn * jnp.mean(dhn * hn, axis=-1, keepdims=True))
        dh_ref[...] = dh
        dhb = dh.astype(BF16)
        dmh_ref[...] = _dot_nt(dhb, w_ref[:HG_W, :])
        dma_ref[...] = _dot_nt(dhb, w_ref[HG_W:, :])
        gw_ref[:HG_W, :] += _dot(mht_ref[...], dhb)
        gw_ref[HG_W:, :] += _dot(mat_ref[...], dhb)

    row = lambda w: pl.BlockSpec((tm, w), lambda s: (s, 0))
    colb = pl.BlockSpec((HG_W, tm), lambda s: (0, s))
    return pl.pallas_call(
        body, name="outproj", grid=(ns,),
        in_specs=[row(D_MODEL), row(D_MODEL), row(HG_W), row(AT_W), colb, colb,
                  _full((D_MODEL, D_MODEL)), _full((1, D_MODEL))],
        out_specs=[row(D_MODEL), row(HG_W), row(AT_W), _full((D_MODEL, D_MODEL)), _full((1, D_MODEL)), _full((8, 128))],
        out_shape=[jax.ShapeDtypeStruct((S, D_MODEL), F32), jax.ShapeDtypeStruct((S, HG_W), F32),
                   jax.ShapeDtypeStruct((S, AT_W), F32), jax.ShapeDtypeStruct((D_MODEL, D_MODEL), F32),
                   jax.ShapeDtypeStruct((1, D_MODEL), F32), jax.ShapeDtypeStruct((8, 128), F32)],
        compiler_params=_params(("arbitrary",)),
    )(x2, tgt2, mix_hg, mix_at, mixt_hg, mixt_at, w_out_full, fnw)


def _inproj_bwd_x(dps, w_in_full, x2, norm_w, dh):
    S = x2.shape[0]
    tm = 256

    def body(d0, d1, d2, d3, d4, d5, d6, d7, w_ref, x_ref, nw_ref, dh_ref, gx_ref, gnw_ref):
        s = pl.program_id(0)

        @pl.when(s == 0)
        def _():
            gnw_ref[...] = jnp.zeros_like(gnw_ref)

        du = jnp.zeros((tm, D_MODEL), F32)
        for i, dref in enumerate((d0, d1, d2, d3, d4, d5, d6, d7)):
            j, half = divmod(i, 2)
            du = du + _dot_nt(dref[...].astype(BF16), w_ref[j, :, 512 * half: 512 * (half + 1)])
        x = x_ref[...]
        r = lax.rsqrt(jnp.mean(x * x, axis=-1, keepdims=True) + EPS)
        xh = x * r
        gnw_ref[...] += jnp.sum(du * xh, axis=0, keepdims=True)
        dun = du * nw_ref[...]
        gx_ref[...] = dh_ref[...] + r * (dun - xh * jnp.mean(dun * xh, axis=-1, keepdims=True))

    row = lambda w: pl.BlockSpec((tm, w), lambda s: (s, 0))
    return pl.pallas_call(
        body, name="inproj_bwd_x", grid=(S // tm,),
        in_specs=[row(512)] * 8 + [_full((4, D_MODEL, 1024)), row(D_MODEL), _full((1, D_MODEL)), row(D_MODEL)],
        out_specs=[row(D_MODEL), _full((1, D_MODEL))],
        out_shape=[jax.ShapeDtypeStruct((S, D_MODEL), F32), jax.ShapeDtypeStruct((1, D_MODEL), F32)],
        compiler_params=_params(("arbitrary",)),
    )(*dps, w_in_full, x2, norm_w, dh)


def _inproj_bwd_w(ut, dps):
    S = ut.shape[1]
    tm = 256

    def body(ut_ref, d0, d1, d2, d3, d4, d5, d6, d7, g_ref):
        @pl.when(pl.program_id(0) == 0)
        def _():
            g_ref[...] = jnp.zeros_like(g_ref)

        utb = ut_ref[...]
        for i, dref in enumerate((d0, d1, d2, d3, d4, d5, d6, d7)):
            g_ref[i] += _dot(utb, dref[...].astype(BF16))

    return pl.pallas_call(
        body, name="inproj_bwd_w", grid=(S // tm,),
        in_specs=[pl.BlockSpec((D_MODEL, tm), lambda s: (0, s))] + [pl.BlockSpec((tm, 512), lambda s: (s, 0))] * 8,
        out_specs=_full((8, D_MODEL, 512)),
        out_shape=jax.ShapeDtypeStruct((8, D_MODEL, 512), F32),
        compiler_params=_params(("arbitrary",)),
    )(ut, *dps)


def _adamw(w, g, m, v, name):
    rows, cols = w.shape
    tr = min(rows, 256)

    def body(w_ref, g_ref, m_ref, v_ref, d_ref, nm_ref, nv_ref):
        gg = g_ref[...]
        nm = ADAM_B1 * m_ref[...] + (1.0 - ADAM_B1) * gg
        nv = ADAM_B2 * v_ref[...] + (1.0 - ADAM_B2) * (gg * gg)
        m_hat = nm / (1.0 - ADAM_B1 ** ADAM_STEP)
        v_hat = nv / (1.0 - ADAM_B2 ** ADAM_STEP)
        d_ref[...] = -ADAM_LR * (m_hat / (jnp.sqrt(v_hat) + ADAM_EPS) + ADAM_WD * w_ref[...])
        nm_ref[...] = nm
        nv_ref[...] = nv

    spec = pl.BlockSpec((tr, cols), lambda i: (i, 0))
    sds = jax.ShapeDtypeStruct((rows, cols), F32)
    return pl.pallas_call(
        body, name=name, grid=(rows // tr,),
        in_specs=[spec] * 4, out_specs=[spec] * 3, out_shape=[sds] * 3,
        compiler_params=_params(("parallel",)),
    )(w, g, m, v)


def _place():
    return lax.axis_index("x"), lax.axis_index("y"), lax.axis_index("c")


def _gather_weights(w_in_s, w_out_s):
    def body(win_ref, wout_ref, fin_ref, fout_ref, send_sems, recv_sems):
        x, y, c = _place()
        me = (x, y, c)
        sib = (x, y, 1 - c)
        mine = 2 * x + y
        fin_ref[mine] = win_ref[...].astype(BF16)
        fout_ref[mine] = wout_ref[...].astype(BF16)
        chips = [(1 - x, y), (x, 1 - y), (1 - x, 1 - y)]

        def halves(chip, half):
            return (fin_ref.at[chip, pl.ds(half * 512, 512), :], fout_ref.at[chip, pl.ds(half * 128, 128), :])

        def copy(k, ref, to):
            return pltpu.make_async_remote_copy(src_ref=ref, dst_ref=ref, send_sem=send_sems.at[k],
                                                recv_sem=recv_sems.at[k], device_id=to, device_id_type=MESH)

        first, passed = [], []
        for j, (cx, cy) in enumerate(chips):
            for a, ref in enumerate(halves(mine, c)):
                first.append(copy(2 * j + a, ref, (cx, cy, c)))
        for cp in first:
            cp.start()
        for j, (cx, cy) in enumerate(chips):
            for a, ref in enumerate(halves(2 * cx + cy, c)):
                copy(2 * j + a, ref, me).wait_recv()
                fwd = copy(6 + 2 * j + a, ref, sib)
                fwd.start()
                passed.append(fwd)
        for j, (cx, cy) in enumerate(chips):
            for a, ref in enumerate(halves(2 * cx + cy, 1 - c)):
                copy(6 + 2 * j + a, ref, me).wait_recv()
        for cp in first + passed:
            cp.wait_send()

    vm = pl.BlockSpec(memory_space=pltpu.VMEM)
    return pl.pallas_call(
        body, name="gather_weights",
        in_specs=[vm, vm], out_specs=[vm, vm],
        out_shape=[jax.ShapeDtypeStruct((4, D_MODEL, 1024), BF16), jax.ShapeDtypeStruct((4, 256, D_MODEL), BF16)],
        scratch_shapes=[pltpu.SemaphoreType.DMA((12,)), pltpu.SemaphoreType.DMA((12,))],
        compiler_params=pltpu.CompilerParams(vmem_limit_bytes=VMEM_LIMIT),
    )(w_in_s, w_out_s)


def _swap_halves(g_in, g_out):
    def body(gin_ref, gout_ref, rin_ref, rout_ref, send_sems, recv_sems):
        x, y, c = _place()
        sib = (x, y, 1 - c)
        cps = [pltpu.make_async_remote_copy(src_ref=src.at[:, 1 - c], dst_ref=dst, send_sem=send_sems.at[k],
                                            recv_sem=recv_sems.at[k], device_id=sib, device_id_type=MESH)
               for k, (src, dst) in enumerate(((gin_ref, rin_ref), (gout_ref, rout_ref)))]
        for cp in cps:
            cp.start()
        for cp in cps:
            cp.wait()

    hbm = pl.BlockSpec(memory_space=pl.ANY)
    return pl.pallas_call(
        body, name="swap_halves",
        in_specs=[hbm, hbm], out_specs=[hbm, hbm],
        out_shape=[jax.ShapeDtypeStruct((4,) + g.shape[2:], F32) for g in (g_in, g_out)],
        scratch_shapes=[pltpu.SemaphoreType.DMA((2,)), pltpu.SemaphoreType.DMA((2,))],
    )(g_in, g_out)


def _add_half(g, r, cidx, name):
    n, _, rows, cols = g.shape

    def body(c_ref, g_ref, r_ref, o_ref):
        o_ref[0] = g_ref[0, 0] + r_ref[0]

    return pl.pallas_call(
        body, name=name,
        grid_spec=pltpu.PrefetchScalarGridSpec(
            num_scalar_prefetch=1, grid=(n,),
            in_specs=[pl.BlockSpec((1, 1, rows, cols), lambda j, c_ref: (j, c_ref[0], 0, 0)),
                      pl.BlockSpec((1, rows, cols), lambda j, c_ref: (j, 0, 0))],
            out_specs=pl.BlockSpec((1, rows, cols), lambda j, c_ref: (j, 0, 0))),
        out_shape=jax.ShapeDtypeStruct((n, rows, cols), F32),
        compiler_params=_params(("parallel",)),
    )(cidx, g, r)


def _exchange_chips(cs_in, cs_out):
    def body(in_ref, out_ref, rin_ref, rout_ref, send_sems, recv_sems, local_sems):
        x, y, c = _place()
        mine = 2 * x + y
        chips = [(1 - x, y), (x, 1 - y), (1 - x, 1 - y)]
        local = [pltpu.make_async_copy(src.at[mine], dst.at[mine], local_sems.at[k])
                 for k, (src, dst) in enumerate(((in_ref, rin_ref), (out_ref, rout_ref)))]
        for cp in local:
            cp.start()
        cps = []
        for j, (cx, cy) in enumerate(chips):
            for a, (src, dst) in enumerate(((in_ref, rin_ref), (out_ref, rout_ref))):
                cps.append(pltpu.make_async_remote_copy(
                    src_ref=src.at[2 * cx + cy], dst_ref=dst.at[mine], send_sem=send_sems.at[2 * j + a],
                    recv_sem=recv_sems.at[2 * j + a], device_id=(cx, cy, c), device_id_type=MESH))
        for cp in cps:
            cp.start()
        for cp in cps:
            cp.wait()
        for cp in local:
            cp.wait()

    hbm = pl.BlockSpec(memory_space=pl.ANY)
    return pl.pallas_call(
        body, name="exchange_chips",
        in_specs=[hbm, hbm], out_specs=[hbm, hbm],
        out_shape=[jax.ShapeDtypeStruct(a.shape, F32) for a in (cs_in, cs_out)],
        scratch_shapes=[pltpu.SemaphoreType.DMA((6,)), pltpu.SemaphoreType.DMA((6,)), pltpu.SemaphoreType.DMA((2,))],
    )(cs_in, cs_out)


def _sum_chips(r, name):
    _, rows, cols = r.shape
    tr = min(rows, 256)

    def body(r_ref, o_ref):
        o_ref[...] = ((r_ref[0] + r_ref[1]) + r_ref[2]) + r_ref[3]

    return pl.pallas_call(
        body, name=name, grid=(rows // tr,),
        in_specs=[pl.BlockSpec((4, tr, cols), lambda i: (0, i, 0))],
        out_specs=pl.BlockSpec((tr, cols), lambda i: (i, 0)),
        out_shape=jax.ShapeDtypeStruct((rows, cols), F32),
        compiler_params=_params(("parallel",)),
    )(r)


def _join_halves(h_in, h_out):
    def body(in_ref, out_ref, fin_ref, fout_ref, send_sems, recv_sems, local_sems):
        x, y, c = _place()
        sib = (x, y, 1 - c)
        pairs = ((in_ref, fin_ref.at[:, pl.ds(c * 512, 512)]), (out_ref, fout_ref.at[c]))
        local = [pltpu.make_async_copy(src, dst, local_sems.at[k]) for k, (src, dst) in enumerate(pairs)]
        cps = [pltpu.make_async_remote_copy(src_ref=src, dst_ref=dst, send_sem=send_sems.at[k],
                                            recv_sem=recv_sems.at[k], device_id=sib, device_id_type=MESH)
               for k, (src, dst) in enumerate(pairs)]
        for cp in local + cps:
            cp.start()
        for cp in cps + local:
            cp.wait()

    hbm = pl.BlockSpec(memory_space=pl.ANY)
    return pl.pallas_call(
        body, name="join_halves",
        in_specs=[hbm, hbm], out_specs=[hbm, hbm],
        out_shape=[jax.ShapeDtypeStruct((D_MODEL, 1024), F32), jax.ShapeDtypeStruct((2, 128, D_MODEL), F32)],
        scratch_shapes=[pltpu.SemaphoreType.DMA((2,)), pltpu.SemaphoreType.DMA((2,)), pltpu.SemaphoreType.DMA((2,))],
    )(h_in, h_out)


def _allreduce_small(g_nw, g_fw, g_hgw, g_lbl, loss8):
    def body(nw_ref, fw_ref, hgw_ref, lbl_ref, loss_ref, out_ref, slots, send_sems, recv_sems):
        x, y, c = _place()
        me = 4 * x + 2 * y + c
        slots[me] = jnp.zeros((8, D_MODEL), F32)
        slots[me, 0:1, :] = nw_ref[...]
        slots[me, 1:2, :] = fw_ref[...]
        slots[me, 2:3, 0:HG_W] = hgw_ref[...]
        slots[me, 3:4, 0:HG_W] = lbl_ref[0:1, :]
        slots[me, 3:4, HG_W:] = lbl_ref[1:2, :]
        slots[me, 4:5, 0:128] = loss_ref[0:1, :]
        cps = []
        for k in range(1, 8):
            dx, dy, dc = (k >> 2) & 1, (k >> 1) & 1, k & 1
            to = (x ^ dx, y ^ dy, c ^ dc)
            cps.append(pltpu.make_async_remote_copy(
                src_ref=slots.at[me], dst_ref=slots.at[me], send_sem=send_sems.at[k - 1],
                recv_sem=recv_sems.at[k - 1], device_id=to, device_id_type=MESH))
        for cp in cps:
            cp.start()
        for cp in cps:
            cp.wait()
        acc = slots[0]
        for i in range(1, 8):
            acc = acc + slots[i]
        out_ref[...] = acc

    vm = pl.BlockSpec(memory_space=pltpu.VMEM)
    return pl.pallas_call(
        body, name="allreduce_small",
        in_specs=[vm] * 5, out_specs=vm,
        out_shape=jax.ShapeDtypeStruct((8, D_MODEL), F32),
        scratch_shapes=[pltpu.VMEM((8, 8, D_MODEL), F32), pltpu.SemaphoreType.DMA((7,)), pltpu.SemaphoreType.DMA((7,))],
    )(g_nw, g_fw, g_hgw, g_lbl, loss8)


def _rope_tables(S):
    inv_freq = 1.0 / (ROPE_THETA ** (jnp.arange(ROPE_HALF, dtype=F32) / ROPE_HALF))
    ang = jnp.arange(S, dtype=jnp.int32).astype(F32)[:, None] * inv_freq[None, :]
    cos, sin = jnp.cos(ang), jnp.sin(ang)
    cos2 = jnp.concatenate([cos, cos, cos, cos], axis=-1)
    sin2 = jnp.concatenate([-sin, sin, -sin, sin], axis=-1)
    return cos2, sin2


def _local_step(x2, tgt2, norm_w, w_in_full, lbl, hg_norm_w, w_out_full, fnw):
    S = x2.shape[0]
    cos2, sin2 = _rope_tables(S)
    p0, p1, qkv, qkv4, qkv16, p3, ut = _inproj(x2, norm_w, w_in_full, cos2, sin2)
    o_hg, mix_hg, mixt_hg, states = _hg_fwd(p0, p1, lbl, hg_norm_w)
    qkv_ds = [qkv.reshape(1, S, 3 * AT_W), qkv4, qkv16]
    ols = [_att_fwd(q) for q in qkv_ds]
    ol, mix_at, mixt_at = _att_combine(ols, p3)
    dh, dm_hg, dm_at, g_wout, g_fw, loss8 = _outproj(x2, tgt2, mix_hg, mix_at, mixt_hg, mixt_at, w_out_full, fnw)
    dqr, dfl, dv_hg, dz_hg, g_lbl, g_hgw = _hg_bwd(p0, p1, o_hg, dm_hg, states, lbl, hg_norm_w)
    aux, aux4, aux16, dz_at = _att_gate_bwd(dm_at, ol, p3)
    aux_ds = [aux.reshape(1, S, 3 * AT_W), aux4, aux16]
    dqkvs = [_att_bwd(q, a) for q, a in zip(qkv_ds, aux_ds)]
    dq_at, dk_at, dv_at = _att_bwd_combine(dqkvs, cos2, sin2)
    dps = [dqr, dfl, dv_hg, dz_hg, dq_at, dk_at, dv_at, dz_at]
    grad_x, g_nw = _inproj_bwd_x(dps, w_in_full, x2, norm_w, dh)
    g_win = _inproj_bwd_w(ut, dps)
    return loss8, grad_x, g_nw, g_win, g_lbl, g_hgw, g_wout, g_fw


def kernel(x, norm_w, w_in, hgrn_lb_logits, hg_norm_w, w_out, final_norm_w, loss_target, m_norm_w, m_w_in, m_hgrn_lb_logits, m_hg_norm_w, m_w_out, m_final_norm_w, v_norm_w, v_w_in, v_hgrn_lb_logits, v_hg_norm_w, v_w_out, v_final_norm_w):
    S = x.shape[1]
    w_in_full, w_out_full = _gather_weights(w_in[0], w_out[0])
    loss8, grad_x, g_nw, g_win, g_lbl, g_hgw, g_wout, g_fw = _local_step(
        x[0], loss_target[0], norm_w, w_in_full, hgrn_lb_logits, hg_norm_w,
        w_out_full.reshape(D_MODEL, D_MODEL), final_norm_w.reshape(1, D_MODEL))

    cidx = lax.axis_index("c").astype(jnp.int32).reshape(1)
    g_in4 = g_win.reshape(4, 2, D_MODEL, 512)
    g_out4 = g_wout.reshape(4, 2, 128, D_MODEL)
    r_in, r_out = _swap_halves(g_in4, g_out4)
    cs_in = _add_half(g_in4, r_in, cidx, "add_half_in")
    cs_out = _add_half(g_out4, r_out, cidx, "add_half_out")
    x_in, x_out = _exchange_chips(cs_in, cs_out)
    h_in = _sum_chips(x_in, "sum_chips_in")
    h_out = _sum_chips(x_out, "sum_chips_out")
    grad_w_in, f_out = _join_halves(h_in, h_out)
    grad_w_out = f_out.reshape(256, D_MODEL)

    red = _allreduce_small(g_nw, g_fw, g_hgw, g_lbl, loss8)
    loss = red[4, 0]
    grad_norm_w = red[0:1, :]
    grad_final_norm_w = red[1, :]
    grad_hg_norm_w = red[2:3, :HG_W]
    grad_lbl = jnp.concatenate([red[3:4, :HG_W], red[3:4, HG_W:]], axis=0)

    d_nw, m_nw, v_nw = _adamw(norm_w, grad_norm_w, m_norm_w, v_norm_w, "adamw_norm_w")
    d_win, m_win, v_win = _adamw(w_in[0], grad_w_in, m_w_in[0], v_w_in[0], "adamw_w_in")
    d_lbl, m_lbl, v_lbl = _adamw(hgrn_lb_logits, grad_lbl, m_hgrn_lb_logits, v_hgrn_lb_logits, "adamw_lb_logits")
    d_hgw, m_hgw, v_hgw = _adamw(hg_norm_w, grad_hg_norm_w, m_hg_norm_w, v_hg_norm_w, "adamw_hg_norm_w")
    d_wout, m_wout, v_wout = _adamw(w_out[0], grad_w_out, m_w_out[0], v_w_out[0], "adamw_w_out")
    d_fw, m_fw, v_fw = _adamw(final_norm_w.reshape(1, D_MODEL), grad_final_norm_w.reshape(1, D_MODEL),
                              m_final_norm_w.reshape(1, D_MODEL), v_final_norm_w.reshape(1, D_MODEL), "adamw_final_norm_w")
    e1 = lambda a: a[None]
    flat = lambda a: a.reshape(D_MODEL)
    return (loss, grad_x[None], grad_norm_w, e1(grad_w_in), grad_lbl, grad_hg_norm_w, e1(grad_w_out), grad_final_norm_w,
            d_nw, e1(d_win), d_lbl, d_hgw, e1(d_wout), flat(d_fw),
            m_nw, e1(m_win), m_lbl, m_hgw, e1(m_wout), flat(m_fw),
            v_nw, e1(v_win), v_lbl, v_hgw, e1(v_wout), flat(v_fw))
```

```python
import functools

import jax
import jax.numpy as jnp
import numpy as np
from jax import lax
from jax.experimental import pallas as pl
from jax.experimental.pallas import tpu as pltpu

F32 = jnp.float32
BF16 = jnp.bfloat16
MESH = pl.DeviceIdType.MESH

D_MODEL = 1024
HG_W = 512
AT_W = 512
HEAD_PAIR = 128
ROPE_HALF = 32
ROPE_THETA = 10000.0
EPS = 1e-6
CHUNK = 128
LEVELS = (64, 32, 16, 8)
DIAG = 8
ATT_BLK = 128
DILATIONS = (1, 4, 16)
ATT_SCALE = 0.125
NEG = -1e30
VMEM_LIMIT = 56 * 1024 * 1024

ADAM_LR = 0.001
ADAM_B1 = 0.9
ADAM_B2 = 0.999
ADAM_EPS = 1e-08
ADAM_WD = 0.01
ADAM_STEP = 10


def _iota(shape, dim):
    return lax.broadcasted_iota(jnp.int32, shape, dim)


def _dot(a, b):
    return jnp.dot(a, b, preferred_element_type=F32)


def _dot_nt(a, b):
    return lax.dot_general(a, b, (((1,), (1,)), ((), ())), preferred_element_type=F32)


def _dot_hi(a, b):
    return jnp.dot(a, b, preferred_element_type=F32, precision=lax.Precision.HIGHEST)


def _sig(v):
    return 1.0 / (1.0 + jnp.exp(-v))


def _params(sem=None, vmem=VMEM_LIMIT):
    return pltpu.CompilerParams(dimension_semantics=sem, vmem_limit_bytes=vmem)


def _full(shape):
    n = len(shape)
    return pl.BlockSpec(shape, lambda *_: (0,) * n)


def _rope_rot(y):
    n = y.shape[1]
    first = (_iota(y.shape, 1) & (2 * ROPE_HALF - 1)) < ROPE_HALF
    return jnp.where(first, pltpu.roll(y, n - ROPE_HALF, 1), pltpu.roll(y, ROPE_HALF, 1))


def _dil_spec(d, tm, width):
    return pl.BlockSpec((d, tm // d, width), lambda s: (0, s, 0))


LANES = 128


def _slab_scratch(tm, width):
    return pltpu.VMEM((width // LANES, tm, LANES), F32)


def _to_slabs(v, slabs_ref):
    for j in range(slabs_ref.shape[0]):
        slabs_ref[j] = v[:, LANES * j: LANES * (j + 1)]


def _from_slabs(slabs_ref):
    return jnp.concatenate([slabs_ref[j] for j in range(slabs_ref.shape[0])], axis=1)


def _split_residues(slabs_ref, dst_ref, d, dtype):
    nslab, tm, _ = slabs_ref.shape
    for r in range(d):
        for j in range(nslab):
            dst_ref[r, :, LANES * j: LANES * (j + 1)] = slabs_ref[j, pl.ds(r, tm // d, stride=d), :].astype(dtype)


def _merge_residues(src_ref, slabs_ref, d):
    nslab, tm, _ = slabs_ref.shape
    for r in range(d):
        for j in range(nslab):
            slabs_ref[j, pl.ds(r, tm // d, stride=d), :] = src_ref[r, :, LANES * j: LANES * (j + 1)]


def _inproj(x2, norm_w, w_in_full, cos2, sin2):
    S = x2.shape[0]
    tm = 256

    def body(x_ref, nw_ref, w_ref, cos_ref, sin_ref, p0_ref, p1_ref, qkv_ref, qkv4_ref, qkv16_ref, p3_ref, ut_ref, scr):
        x = x_ref[...]
        r = lax.rsqrt(jnp.mean(x * x, axis=-1, keepdims=True) + EPS)
        u = x * r * nw_ref[...]
        ub = u.astype(BF16)
        ut_ref[...] = u.T.astype(BF16)
        p0_ref[...] = _dot(ub, w_ref[0])
        p1_ref[...] = _dot(ub, w_ref[1])
        y2 = _dot(ub, w_ref[2])
        cosf = jnp.tile(cos_ref[...], (1, 8))
        sinf = jnp.tile(sin_ref[...], (1, 8))
        y3 = _dot(ub, w_ref[3])
        p3_ref[...] = y3
        qkv = jnp.concatenate([y2 * cosf + _rope_rot(y2) * sinf, y3[:, :AT_W]], axis=1)
        qkv_ref[...] = qkv.astype(BF16)
        _to_slabs(qkv, scr)
        _split_residues(scr, qkv4_ref, 4, BF16)
        _split_residues(scr, qkv16_ref, 16, BF16)

    row = lambda w: pl.BlockSpec((tm, w), lambda s: (s, 0))
    qkv_w = 3 * AT_W
    return pl.pallas_call(
        body, name="inproj", grid=(S // tm,),
        in_specs=[row(D_MODEL), _full((1, D_MODEL)), _full((4, D_MODEL, 1024)), row(128), row(128)],
        out_specs=[row(1024), row(1024), row(qkv_w), _dil_spec(4, tm, qkv_w), _dil_spec(16, tm, qkv_w), row(1024),
                   pl.BlockSpec((D_MODEL, tm), lambda s: (0, s))],
        out_shape=[jax.ShapeDtypeStruct((S, 1024), F32), jax.ShapeDtypeStruct((S, 1024), F32),
                   jax.ShapeDtypeStruct((S, qkv_w), BF16), jax.ShapeDtypeStruct((4, S // 4, qkv_w), BF16),
                   jax.ShapeDtypeStruct((16, S // 16, qkv_w), BF16), jax.ShapeDtypeStruct((S, 1024), F32),
                   jax.ShapeDtypeStruct((D_MODEL, S), BF16)],
        scratch_shapes=[_slab_scratch(tm, qkv_w)],
        compiler_params=_params(("parallel",)),
    )(x2, norm_w, w_in_full, cos2, sin2)


HG_HPS = 4
N_LEV = len(LEVELS)


def _hg_const_arrays():
    r = np.arange(CHUNK)[:, None]
    c = np.arange(CHUNK)[None, :]
    tris = np.stack([r >= c, r <= c])
    lm = [((r // (2 * m)) == (c // (2 * m))) & (r % (2 * m) >= m) & (c % (2 * m) < m) for m in LEVELS]
    dm = [(c == r - dl) & (r % DIAG >= dl) for dl in range(DIAG)]
    masks = np.stack(lm + [x.T for x in lm] + dm)
    return jnp.asarray(tris, BF16), jnp.asarray(masks, F32)


def _split2(a):
    hi = a.astype(BF16)
    return hi, (a - hi.astype(F32)).astype(BF16)


def _dot3(a, b):
    ah, al = _split2(a)
    bh, bl = _split2(b)
    n = b.shape[1]
    p = _dot(ah, jnp.concatenate([bh, bl], axis=1))
    return (p[:, :n] + p[:, n:]) + _dot(al, bh)


def _tri_dot(tri, a):
    a1 = a.astype(BF16)
    r1 = a - a1.astype(F32)
    a2 = r1.astype(BF16)
    a3 = (r1 - a2.astype(F32)).astype(BF16)
    n = a.shape[1]
    p = _dot(tri, jnp.concatenate([a1, a2, a3], axis=1))
    return (p[:, :n] + p[:, n:2 * n]) + p[:, 2 * n:]


def _rowsum(t):
    return _dot(t.astype(BF16), jnp.ones((t.shape[1], t.shape[1]), BF16))


def _level_refs(b):
    refs = []
    for m in LEVELS:
        parts = [jnp.broadcast_to(b[r0 + m - 1: r0 + m, :], (2 * m, b.shape[1])) for r0 in range(0, CHUNK, 2 * m)]
        refs.append(parts[0] if len(parts) == 1 else jnp.concatenate(parts, axis=0))
    return refs


def _hg_lb(lbl_ref):
    l0 = lbl_ref[0:1, :]
    l1 = lbl_ref[1:2, :]
    mx = jnp.maximum(l0, l1)
    e0 = jnp.exp(l0 - mx)
    e1 = jnp.exp(l1 - mx)
    p0 = e0 / (e0 + e1)
    lb = jnp.clip(p0, 1e-6, 1.0 - 1e-6)
    inside = (p0 >= 1e-6) & (p0 <= 1.0 - 1e-6)
    dlb_dl0 = jnp.where(inside, p0 * (e1 / (e0 + e1)), 0.0)
    return lb, dlb_dl0


def _sigmoid(v):
    return 0.5 * jnp.tanh(0.5 * v) + 0.5


def _hg_gates(qr, fl, lb):
    sig = _sigmoid(fl)
    f = lb + (1.0 - lb) * sig
    g = jnp.log(f)
    k = (1.0 - lb) * (1.0 - sig)
    sq = _sigmoid(qr)
    q = qr * sq
    return sig, f, g, k, sq, q


def _hg_levels(q, k, b, mk_ref):
    refs = _level_refs(b)
    a = jnp.zeros((CHUNK, CHUNK), F32)
    eqs, eks, qts, kts = [], [], [], []
    for i in range(N_LEV):
        eq = jnp.exp(jnp.minimum(b - refs[i], 0.0))
        ek = jnp.exp(jnp.minimum(refs[i] - b, 0.0))
        qt = (q * eq).astype(BF16)
        kt = (k * ek).astype(BF16)
        a = a + _dot_nt(qt, kt) * mk_ref[i]
        eqs.append(eq); eks.append(ek); qts.append(qt); kts.append(kt)
    return a, eqs, eks, qts, kts


def _hg_specs(nc, rev):
    cc = (lambda c: nc - 1 - c) if rev else (lambda c: c)
    w = 128 * HG_HPS
    blk = lambda off: pl.BlockSpec((CHUNK, w), lambda h, c: (cc(c), h + off))
    vec = pl.BlockSpec((1, w), lambda h, c: (0, h))
    lb2 = pl.BlockSpec((2, w), lambda h, c: (0, h))
    st = pl.BlockSpec((1, HG_HPS, 128, 128), lambda h, c: (cc(c), h, 0, 0))
    consts = [_full((2, CHUNK, CHUNK)), _full((2 * N_LEV + DIAG, CHUNK, CHUNK))]
    return blk, vec, lb2, st, consts


def _hg_fwd(p0, p1, lbl, gw):
    S = p0.shape[0]
    nc = S // CHUNK
    ng = 4 // HG_HPS

    def body(qr_ref, fl_ref, v_ref, z_ref, lbl_ref, gw_ref, tri_ref, mk_ref,
             o_ref, mix_ref, mixt_ref, st_ref, state):
        c = pl.program_id(1)

        @pl.when(c == 0)
        def _():
            state[...] = jnp.zeros_like(state)

        lb_all, _ = _hg_lb(lbl_ref)
        for hh in range(HG_HPS):
            sl = slice(128 * hh, 128 * (hh + 1))
            qr, fl, v, z = qr_ref[:, sl], fl_ref[:, sl], v_ref[:, sl], z_ref[:, sl]
            _, _, g, k, _, q = _hg_gates(qr, fl, lb_all[:, sl])
            b = _tri_dot(tri_ref[0], g)
            b_last = b[CHUNK - 1: CHUNK, :]
            a, _, _, _, _ = _hg_levels(q, k, b, mk_ref)
            for dl in range(DIAG):
                if dl == 0:
                    term = q * k
                else:
                    e = jnp.exp(jnp.minimum(b - pltpu.roll(b, dl, 0), 0.0))
                    term = q * pltpu.roll(k, dl, 0) * e
                a = a + _rowsum(term) * mk_ref[2 * N_LEV + dl]
            st = state[hh]
            st_ref[0, hh] = st
            o = _dot_nt((q * jnp.exp(b)).astype(BF16), st.astype(BF16)) + _dot(a.astype(BF16), v.astype(BF16))
            state[hh] = st * jnp.exp(b_last) + _dot3(v.T, k * jnp.exp(b_last - b))
            o_ref[:, sl] = o
            rs = lax.rsqrt(jnp.mean(o * o, axis=-1, keepdims=True) + EPS)
            mixed = o * rs * gw_ref[:, sl] * (z * _sigmoid(z))
            mix_ref[:, sl] = mixed.astype(BF16)
            mixt_ref[sl, :] = mixed.T.astype(BF16)

    blk, vec, lb2, st_spec, consts = _hg_specs(nc, False)
    tris, masks = _hg_const_arrays()
    return pl.pallas_call(
        body, name="hg_fwd", grid=(ng, nc),
        in_specs=[blk(0), blk(ng), blk(0), blk(ng), lb2, vec] + consts,
        out_specs=[blk(0), blk(0), pl.BlockSpec((128 * HG_HPS, CHUNK), lambda h, c: (h, c)), st_spec],
        out_shape=[jax.ShapeDtypeStruct((S, HG_W), F32), jax.ShapeDtypeStruct((S, HG_W), BF16),
                   jax.ShapeDtypeStruct((HG_W, S), BF16), jax.ShapeDtypeStruct((nc, 4, 128, 128), F32)],
        scratch_shapes=[pltpu.VMEM((HG_HPS, 128, 128), F32)],
        compiler_params=_params(("parallel", "arbitrary")),
    )(p0, p0, p1, p1, lbl, gw, tris, masks)


def _hg_bwd(p0, p1, o_raw, dm, states, lbl, gw):
    S = p0.shape[0]
    nc = S // CHUNK
    ng = 4 // HG_HPS
    w = 128 * HG_HPS

    def body(qr_ref, fl_ref, v_ref, z_ref, o_ref, dm_ref, st_ref, lbl_ref, gw_ref, tri_ref, mk_ref,
             dqr_ref, dfl_ref, dv_ref, dz_ref, glbl_ref, ggw_ref, dstate, carry, acc_lb, acc_gw):
        c = pl.program_id(1)

        @pl.when(c == 0)
        def _():
            dstate[...] = jnp.zeros_like(dstate)
            carry[...] = jnp.zeros_like(carry)
            acc_lb[...] = jnp.zeros_like(acc_lb)
            acc_gw[...] = jnp.zeros_like(acc_gw)

        lb_all, dlb_dl0 = _hg_lb(lbl_ref)
        for hh in range(HG_HPS):
            sl = slice(128 * hh, 128 * (hh + 1))
            lb = lb_all[:, sl]
            qr, fl, v, z = qr_ref[:, sl], fl_ref[:, sl], v_ref[:, sl], z_ref[:, sl]
            sig, f, g, k, sq, q = _hg_gates(qr, fl, lb)
            b = _tri_dot(tri_ref[0], g)
            b_last = b[CHUNK - 1: CHUNK, :]
            a, eqs, eks, qts, kts = _hg_levels(q, k, b, mk_ref)

            o = o_ref[:, sl]
            dmix = dm_ref[:, sl]
            gwv = gw_ref[:, sl]
            rs = lax.rsqrt(jnp.mean(o * o, axis=-1, keepdims=True) + EPS)
            oh = o * rs
            sz = _sigmoid(z)
            dz_ref[:, sl] = dmix * (oh * gwv) * (sz * (1.0 + z * (1.0 - sz)))
            don = dmix * (z * sz)
            acc_gw[0:1, sl] += jnp.sum(don * oh, axis=0, keepdims=True)
            dy = don * gwv
            do = rs * (dy - oh * jnp.mean(dy * oh, axis=-1, keepdims=True))

            st = st_ref[0, hh]
            dst = dstate[hh]
            dob = do.astype(BF16)
            eb = jnp.exp(b)
            edec = jnp.exp(b_last - b)
            dq = _dot3(do, st) * eb
            dk = _dot3(v, dst) * edec
            db = q * dq - k * dk
            da = _dot_nt(dob, v.astype(BF16))
            da_t = da.T
            for i in range(N_LEV):
                gq = _dot((da * mk_ref[i]).astype(BF16), kts[i])
                gk = _dot((da_t * mk_ref[N_LEV + i]).astype(BF16), qts[i])
                dq = dq + eqs[i] * gq
                dk = dk + eks[i] * gk
                db = db + (qts[i].astype(F32) * gq - kts[i].astype(F32) * gk)
            for dl in range(DIAG):
                dmask = mk_ref[2 * N_LEV + dl]
                dc = _rowsum(da * dmask)
                if dl == 0:
                    a = a + _rowsum(q * k) * dmask
                    dq = dq + dc * k
                    dk = dk + dc * q
                else:
                    e = jnp.exp(jnp.minimum(b - pltpu.roll(b, dl, 0), 0.0))
                    ks = pltpu.roll(k, dl, 0)
                    a = a + _rowsum(q * ks * e) * dmask
                    tq = dc * ks * e
                    tk = pltpu.roll(dc * q * e, CHUNK - dl, 0)
                    dq = dq + tq
                    dk = dk + tk
                    db = db + (q * tq - k * tk)
            dv_ref[:, sl] = _dot(a.T.astype(BF16), dob) + _dot_nt((k * edec).astype(BF16), dst.astype(BF16))
            dstate[hh] = dst * jnp.exp(b_last) + _dot3(do.T, q * eb)

            dg = _tri_dot(tri_ref[1], db) + carry[0:1, sl]
            carry[0:1, sl] += jnp.sum(db, axis=0, keepdims=True)
            t = dg / f - dk
            dfl_ref[:, sl] = t * (1.0 - lb) * sig * (1.0 - sig)
            acc_lb[0:1, sl] += jnp.sum(t * (1.0 - sig), axis=0, keepdims=True)
            dqr_ref[:, sl] = dq * (sq * (1.0 + qr * (1.0 - sq)))

        @pl.when(c == nc - 1)
        def _():
            gl0 = acc_lb[0:1, :] * dlb_dl0
            glbl_ref[0:1, :] = gl0
            glbl_ref[1:2, :] = -gl0
            ggw_ref[...] = acc_gw[0:1, :]

    blk, vec, lb2, st_spec, consts = _hg_specs(nc, True)
    tris, masks = _hg_const_arrays()
    act = jax.ShapeDtypeStruct((S, HG_W), F32)
    return pl.pallas_call(
        body, name="hg_bwd", grid=(ng, nc),
        in_specs=[blk(0), blk(ng), blk(0), blk(ng), blk(0), blk(0), st_spec, lb2, vec] + consts,
        out_specs=[blk(0), blk(0), blk(0), blk(0), lb2, vec],
        out_shape=[act, act, act, act, jax.ShapeDtypeStruct((2, HG_W), F32), jax.ShapeDtypeStruct((1, HG_W), F32)],
        scratch_shapes=[pltpu.VMEM((HG_HPS, 128, 128), F32), pltpu.VMEM((8, w), F32),
                        pltpu.VMEM((8, w), F32), pltpu.VMEM((8, w), F32)],
        compiler_params=_params(("parallel", "arbitrary")),
    )(p0, p0, p1, p1, o_raw, dm, states, lbl, gw, tris, masks)


def _att_bias():
    qi = np.arange(ATT_BLK)[:, None]
    kj = np.arange(2 * ATT_BLK)[None, :]
    band = (kj >= qi) & (kj <= qi + ATT_BLK)
    qm = np.stack([band & (kj >= ATT_BLK), band])
    cur = (kj < ATT_BLK) & (qi <= kj)
    km = np.stack([cur, cur | ((kj >= ATT_BLK) & (qi >= kj - ATT_BLK))])
    to_bias = lambda m: jnp.asarray(np.where(m, 0.0, NEG), F32)
    return to_bias(qm), to_bias(km)


def _att_fwd(qkv_d):
    d, L, _ = qkv_d.shape
    nb = L // ATT_BLK

    def body(q_ref, kp_ref, kc_ref, vp_ref, vc_ref, bias_ref, ol_ref, s_scr, p_scr):
        bias = bias_ref[jnp.minimum(pl.program_id(1), 1)]
        bias2 = jnp.concatenate([bias, bias], axis=0)
        first = _iota((ATT_BLK, HEAD_PAIR), 1) < 64
        pairs = [slice(HEAD_PAIR * hp, HEAD_PAIR * (hp + 1)) for hp in range(4)]
        for hp, sl in enumerate(pairs):
            q2 = q_ref[0, :, sl] * ATT_SCALE
            zero = jnp.zeros_like(q2)
            kcat = jnp.concatenate([kp_ref[0, :, sl], kc_ref[0, :, sl]], axis=0)
            qs = jnp.concatenate([jnp.where(first, q2, zero), jnp.where(first, zero, q2)], axis=0)
            s_scr[hp] = _dot_nt(qs, kcat)
        stats = []
        for hp in range(4):
            s = s_scr[hp] + bias2
            m = jnp.max(s, axis=-1, keepdims=True)
            p = jnp.exp(s - m)
            l = jnp.sum(p, axis=-1, keepdims=True)
            p_scr[hp] = p.astype(BF16)
            stats.append((l, m + jnp.log(l)))
        for hp, sl in enumerate(pairs):
            l, lse = stats[hp]
            vcat = jnp.concatenate([vp_ref[0, :, sl], vc_ref[0, :, sl]], axis=0)
            o = _dot(p_scr[hp], vcat) / l
            ol_ref[0, :, sl] = jnp.where(first, o[:ATT_BLK], o[ATT_BLK:])
            ol_ref[0, :, AT_W + HEAD_PAIR * hp: AT_W + HEAD_PAIR * (hp + 1)] = jnp.where(first, lse[:ATT_BLK], lse[ATT_BLK:])

    cur = lambda j: pl.BlockSpec((1, ATT_BLK, AT_W), lambda r, n: (r, n, j))
    prev = lambda j: pl.BlockSpec((1, ATT_BLK, AT_W), lambda r, n: (r, jnp.maximum(n - 1, 0), j))
    return pl.pallas_call(
        body, name=f"att_fwd_d{d}", grid=(d, nb),
        in_specs=[cur(0), prev(1), cur(1), prev(2), cur(2), _full((2, ATT_BLK, 2 * ATT_BLK))],
        out_specs=pl.BlockSpec((1, ATT_BLK, 2 * AT_W), lambda r, n: (r, n, 0)),
        out_shape=jax.ShapeDtypeStruct((d, L, 2 * AT_W), F32),
        scratch_shapes=[pltpu.VMEM((4, 2 * ATT_BLK, 2 * ATT_BLK), F32), pltpu.VMEM((4, 2 * ATT_BLK, 2 * ATT_BLK), BF16)],
        compiler_params=_params(("parallel", "parallel")),
    )(qkv_d, qkv_d, qkv_d, qkv_d, qkv_d, _att_bias()[0])


def _att_combine(ols, p3):
    S = p3.shape[0]
    tm = 512

    def body(a_ref, b4_ref, c16_ref, z_ref, ol_ref, mix_ref, mixt_ref, b_scr, c_scr):
        _merge_residues(b4_ref, b_scr, 4)
        _merge_residues(c16_ref, c_scr, 16)
        pats = (a_ref[...], _from_slabs(b_scr), _from_slabs(c_scr))
        os_ = [v[:, :AT_W] for v in pats]
        ls = [v[:, AT_W:] for v in pats]
        mx = jnp.maximum(jnp.maximum(ls[0], ls[1]), ls[2])
        es = [jnp.exp(l - mx) for l in ls]
        zs = es[0] + es[1] + es[2]
        o = (es[0] * os_[0] + es[1] * os_[1] + es[2] * os_[2]) / zs
        ol_ref[:, :AT_W] = o
        ol_ref[:, AT_W:] = mx + jnp.log(zs)
        z = z_ref[...]
        mixed = o * (z * _sig(z))
        mix_ref[...] = mixed.astype(BF16)
        mixt_ref[...] = mixed.T.astype(BF16)

    row = lambda w: pl.BlockSpec((tm, w), lambda s: (s, 0))
    return pl.pallas_call(
        body, name="att_combine", grid=(S // tm,),
        in_specs=[row(1024), _dil_spec(4, tm, 1024), _dil_spec(16, tm, 1024), pl.BlockSpec((tm, AT_W), lambda s: (s, 1))],
        out_specs=[row(1024), row(AT_W), pl.BlockSpec((AT_W, tm), lambda s: (0, s))],
        out_shape=[jax.ShapeDtypeStruct((S, 1024), F32), jax.ShapeDtypeStruct((S, AT_W), BF16),
                   jax.ShapeDtypeStruct((AT_W, S), BF16)],
        scratch_shapes=[_slab_scratch(tm, 1024), _slab_scratch(tm, 1024)],
        compiler_params=_params(("parallel",)),
    )(ols[0].reshape(S, 1024), ols[1], ols[2], p3)


def _att_gate_bwd(dm_at, ol, p3):
    S = p3.shape[0]
    tm = 512

    def body(dm_ref, ol_ref, z_ref, aux_ref, aux4_ref, aux16_ref, dz_ref, scr):
        o = ol_ref[:, :AT_W]
        z = z_ref[...]
        dm = dm_ref[...]
        sz = _sig(z)
        dz_ref[...] = dm * o * (sz * (1.0 + z * (1.0 - sz)))
        do = dm * (z * sz)
        aux_ref[:, :AT_W] = do
        ones = ((_iota((HEAD_PAIR, HEAD_PAIR), 0) < 64) == (_iota((HEAD_PAIR, HEAD_PAIR), 1) < 64)).astype(F32)
        prod = do * o
        for hp in range(4):
            sl = slice(HEAD_PAIR * hp, HEAD_PAIR * (hp + 1))
            aux_ref[:, AT_W + HEAD_PAIR * hp: AT_W + HEAD_PAIR * (hp + 1)] = _dot_hi(prod[:, sl], ones)
        aux_ref[:, 2 * AT_W:] = ol_ref[:, AT_W:]
        _to_slabs(aux_ref[...], scr)
        _split_residues(scr, aux4_ref, 4, F32)
        _split_residues(scr, aux16_ref, 16, F32)

    row = lambda w: pl.BlockSpec((tm, w), lambda s: (s, 0))
    aw = 3 * AT_W
    return pl.pallas_call(
        body, name="att_gate_bwd", grid=(S // tm,),
        in_specs=[row(AT_W), row(1024), pl.BlockSpec((tm, AT_W), lambda s: (s, 1))],
        out_specs=[row(aw), _dil_spec(4, tm, aw), _dil_spec(16, tm, aw), row(AT_W)],
        out_shape=[jax.ShapeDtypeStruct((S, aw), F32), jax.ShapeDtypeStruct((4, S // 4, aw), F32),
                   jax.ShapeDtypeStruct((16, S // 16, aw), F32), jax.ShapeDtypeStruct((S, AT_W), F32)],
        scratch_shapes=[_slab_scratch(tm, aw)],
        compiler_params=_params(("parallel",)),
    )(dm_at, ol, p3)


def _att_bwd(qkv_d, aux_d):
    d, L, _ = qkv_d.shape
    nb = L // ATT_BLK

    def body(qc_ref, qn_ref, kp_ref, kc_ref, vp_ref, vc_ref, ac_ref, an_ref, bq_ref, bk_ref, dqkv_ref,
             s_scr, dp_scr, st_scr, dpt_scr, ds_scr, pt_scr, dst_scr):
        n = pl.program_id(1)
        bias = bq_ref[jnp.minimum(n, 1)]
        bias_t = bk_ref[jnp.minimum(nb - 1 - n, 1)]
        bias2 = jnp.concatenate([bias, bias], axis=0)
        bias_t2 = jnp.concatenate([bias_t, bias_t], axis=0)
        first = _iota((ATT_BLK, HEAD_PAIR), 1) < 64
        pairs = [slice(HEAD_PAIR * hp, HEAD_PAIR * (hp + 1)) for hp in range(4)]

        def stack(t):
            zero = jnp.zeros_like(t)
            return jnp.concatenate([jnp.where(first, t, zero), jnp.where(first, zero, t)], axis=0)

        def unstack(t2):
            return jnp.where(first, t2[:ATT_BLK], t2[ATT_BLK:])

        def operands(sl):
            q2, k2, v2 = qc_ref[0, :, sl], kc_ref[0, :, sl], vc_ref[0, :, sl]
            kcat = jnp.concatenate([kp_ref[0, :, sl], k2], axis=0)
            vcat = jnp.concatenate([vp_ref[0, :, sl], v2], axis=0)
            qcat = jnp.concatenate([q2, qn_ref[0, :, sl]], axis=0)
            do_c = ac_ref[0, :, sl]
            docat = jnp.concatenate([do_c, an_ref[0, :, sl]], axis=0).astype(BF16)
            return q2, k2, v2, kcat, vcat, qcat, do_c, docat

        for hp, sl in enumerate(pairs):
            q2, k2, v2, kcat, vcat, qcat, do_c, docat = operands(sl)
            s_scr[hp] = _dot_nt(stack(q2 * ATT_SCALE), kcat)
            dp_scr[hp] = _dot_nt(stack(do_c.astype(BF16)), vcat)
            st_scr[hp] = _dot_nt(stack(k2 * ATT_SCALE), qcat)
            dpt_scr[hp] = _dot_nt(stack(v2), docat)
        for hp in range(4):
            sl_d = slice(AT_W + HEAD_PAIR * hp, AT_W + HEAD_PAIR * (hp + 1))
            sl_l = slice(2 * AT_W + HEAD_PAIR * hp, 2 * AT_W + HEAD_PAIR * (hp + 1))
            dd_c, ls_c = ac_ref[0, :, sl_d], ac_ref[0, :, sl_l]
            col = lambda t: jnp.concatenate([t[:, 0:1], t[:, 64:65]], axis=0)
            p = jnp.exp(s_scr[hp] + bias2 - col(ls_c))
            ds_scr[hp] = (p * (dp_scr[hp] - col(dd_c))).astype(BF16)
            dd_t = jnp.concatenate([dd_c, an_ref[0, :, sl_d]], axis=0).T
            ls_t = jnp.concatenate([ls_c, an_ref[0, :, sl_l]], axis=0).T
            row = lambda t: jnp.concatenate([jnp.broadcast_to(t[0:1, :], (ATT_BLK, 2 * ATT_BLK)),
                                             jnp.broadcast_to(t[64:65, :], (ATT_BLK, 2 * ATT_BLK))], axis=0)
            pt = jnp.exp(st_scr[hp] + bias_t2 - row(ls_t))
            pt_scr[hp] = pt.astype(BF16)
            dst_scr[hp] = (pt * (dpt_scr[hp] - row(dd_t))).astype(BF16)
        for hp, sl in enumerate(pairs):
            _, _, _, kcat, _, qcat, _, docat = operands(sl)
            dqkv_ref[0, :, sl] = unstack(_dot(ds_scr[hp], kcat)) * ATT_SCALE
            dqkv_ref[0, :, AT_W + HEAD_PAIR * hp: AT_W + HEAD_PAIR * (hp + 1)] = unstack(_dot(dst_scr[hp], qcat)) * ATT_SCALE
            dqkv_ref[0, :, 2 * AT_W + HEAD_PAIR * hp: 2 * AT_W + HEAD_PAIR * (hp + 1)] = unstack(_dot(pt_scr[hp], docat))

    cur = lambda j: pl.BlockSpec((1, ATT_BLK, AT_W), lambda r, n: (r, n, j))
    prev = lambda j: pl.BlockSpec((1, ATT_BLK, AT_W), lambda r, n: (r, jnp.maximum(n - 1, 0), j))
    nxt = lambda j: pl.BlockSpec((1, ATT_BLK, AT_W), lambda r, n: (r, jnp.minimum(n + 1, nb - 1), j))
    return pl.pallas_call(
        body, name=f"att_bwd_d{d}", grid=(d, nb),
        in_specs=[cur(0), nxt(0), prev(1), cur(1), prev(2), cur(2),
                  pl.BlockSpec((1, ATT_BLK, 3 * AT_W), lambda r, n: (r, n, 0)),
                  pl.BlockSpec((1, ATT_BLK, 3 * AT_W), lambda r, n: (r, jnp.minimum(n + 1, nb - 1), 0)),
                  _full((2, ATT_BLK, 2 * ATT_BLK)), _full((2, ATT_BLK, 2 * ATT_BLK))],
        out_specs=pl.BlockSpec((1, ATT_BLK, 3 * AT_W), lambda r, n: (r, n, 0)),
        out_shape=jax.ShapeDtypeStruct((d, L, 3 * AT_W), F32),
        scratch_shapes=[pltpu.VMEM((4, 2 * ATT_BLK, 2 * ATT_BLK), F32)] * 4 + [pltpu.VMEM((4, 2 * ATT_BLK, 2 * ATT_BLK), BF16)] * 3,
        compiler_params=_params(("parallel", "parallel")),
    )(qkv_d, qkv_d, qkv_d, qkv_d, qkv_d, qkv_d, aux_d, aux_d, *_att_bias())


def _att_bwd_combine(dqkvs, cos2, sin2):
    S = dqkvs[0].shape[1]
    tm = 512

    def body(a_ref, b4_ref, c16_ref, cos_ref, sin_ref, dq_ref, dk_ref, dv_ref, b_scr, c_scr):
        _merge_residues(b4_ref, b_scr, 4)
        _merge_residues(c16_ref, c_scr, 16)
        t = a_ref[...] + _from_slabs(b_scr) + _from_slabs(c_scr)
        dy = t[:, : 2 * AT_W]
        cosf = jnp.tile(cos_ref[...], (1, 8))
        sinf = jnp.tile(sin_ref[...], (1, 8))
        dx = dy * cosf - _rope_rot(dy) * sinf
        dq_ref[...] = dx[:, :AT_W]
        dk_ref[...] = dx[:, AT_W:]
        dv_ref[...] = t[:, 2 * AT_W:]

    row = lambda w: pl.BlockSpec((tm, w), lambda s: (s, 0))
    act = jax.ShapeDtypeStruct((S, AT_W), F32)
    return pl.pallas_call(
        body, name="att_bwd_combine", grid=(S // tm,),
        in_specs=[row(3 * AT_W), _dil_spec(4, tm, 3 * AT_W), _dil_spec(16, tm, 3 * AT_W), row(128), row(128)],
        out_specs=[row(AT_W), row(AT_W), row(AT_W)],
        out_shape=[act, act, act],
        scratch_shapes=[_slab_scratch(tm, 3 * AT_W), _slab_scratch(tm, 3 * AT_W)],
        compiler_params=_params(("parallel",)),
    )(dqkvs[0].reshape(S, 3 * AT_W), dqkvs[1], dqkvs[2], cos2, sin2)


def _outproj(x2, tgt2, mix_hg, mix_at, mixt_hg, mixt_at, w_out_full, fnw):
    S = x2.shape[0]
    tm = 256
    ns = S // tm

    def body(x_ref, t_ref, mh_ref, ma_ref, mht_ref, mat_ref, w_ref, fw_ref,
             dh_ref, dmh_ref, dma_ref, gw_ref, gfw_ref, loss_ref):
        s = pl.program_id(0)

        @pl.when(s == 0)
        def _():
            gw_ref[...] = jnp.zeros_like(gw_ref)
            gfw_ref[...] = jnp.zeros_like(gfw_ref)
            loss_ref[...] = jnp.zeros_like(loss_ref)

        y = _dot(mh_ref[...], w_ref[:HG_W, :]) + _dot(ma_ref[...], w_ref[HG_W:, :])
        h = x_ref[...] + y
        r = lax.rsqrt(jnp.mean(h * h, axis=-1, keepdims=True) + EPS)
        hn = h * r
        fw = fw_ref[...]
        err = hn * fw - t_ref[...]
        loss_ref[...] += 0.5 * jnp.sum(jnp.mean(err * err, axis=-1, keepdims=True))
        dout = err * (1.0 / D_MODEL)
        gfw_ref[...] += jnp.sum(dout * hn, axis=0, keepdims=True)
        dhn = dout * fw
        dh = r * (dhn - hn * jnp.mean(dhn * hn, axis=-1, keepdims=True))
        dh_ref[...] = dh
        dhb = dh.astype(BF16)
        dmh_ref[...] = _dot_nt(dhb, w_ref[:HG_W, :])
        dma_ref[...] = _dot_nt(dhb, w_ref[HG_W:, :])
        gw_ref[:HG_W, :] += _dot(mht_ref[...], dhb)
        gw_ref[HG_W:, :] += _dot(mat_ref[...], dhb)

    row = lambda w: pl.BlockSpec((tm, w), lambda s: (s, 0))
    colb = pl.BlockSpec((HG_W, tm), lambda s: (0, s))
    return pl.pallas_call(
        body, name="outproj", grid=(ns,),
        in_specs=[row(D_MODEL), row(D_MODEL), row(HG_W), row(AT_W), colb, colb,
                  _full((D_MODEL, D_MODEL)), _full((1, D_MODEL))],
        out_specs=[row(D_MODEL), row(HG_W), row(AT_W), _full((D_MODEL, D_MODEL)), _full((1, D_MODEL)), _full((8, 128))],
        out_shape=[jax.ShapeDtypeStruct((S, D_MODEL), F32), jax.ShapeDtypeStruct((S, HG_W), F32),
                   jax.ShapeDtypeStruct((S, AT_W), F32), jax.ShapeDtypeStruct((D_MODEL, D_MODEL), F32),
                   jax.ShapeDtypeStruct((1, D_MODEL), F32), jax.ShapeDtypeStruct((8, 128), F32)],
        compiler_params=_params(("arbitrary",)),
    )(x2, tgt2, mix_hg, mix_at, mixt_hg, mixt_at, w_out_full, fnw)


def _inproj_bwd_x(dps, w_in_full, x2, norm_w, dh):
    S = x2.shape[0]
    tm = 256

    def body(d0, d1, d2, d3, d4, d5, d6, d7, w_ref, x_ref, nw_ref, dh_ref, gx_ref, gnw_ref):
        s = pl.program_id(0)

        @pl.when(s == 0)
        def _():
            gnw_ref[...] = jnp.zeros_like(gnw_ref)

        du = jnp.zeros((tm, D_MODEL), F32)
        for i, dref in enumerate((d0, d1, d2, d3, d4, d5, d6, d7)):
            j, half = divmod(i, 2)
            du = du + _dot_nt(dref[...].astype(BF16), w_ref[j, :, 512 * half: 512 * (half + 1)])
        x = x_ref[...]
        r = lax.rsqrt(jnp.mean(x * x, axis=-1, keepdims=True) + EPS)
        xh = x * r
        gnw_ref[...] += jnp.sum(du * xh, axis=0, keepdims=True)
        dun = du * nw_ref[...]
        gx_ref[...] = dh_ref[...] + r * (dun - xh * jnp.mean(dun * xh, axis=-1, keepdims=True))

    row = lambda w: pl.BlockSpec((tm, w), lambda s: (s, 0))
    return pl.pallas_call(
        body, name="inproj_bwd_x", grid=(S // tm,),
        in_specs=[row(512)] * 8 + [_full((4, D_MODEL, 1024)), row(D_MODEL), _full((1, D_MODEL)), row(D_MODEL)],
        out_specs=[row(D_MODEL), _full((1, D_MODEL))],
        out_shape=[jax.ShapeDtypeStruct((S, D_MODEL), F32), jax.ShapeDtypeStruct((1, D_MODEL), F32)],
        compiler_params=_params(("arbitrary",)),
    )(*dps, w_in_full, x2, norm_w, dh)


def _inproj_bwd_w(ut, dps):
    S = ut.shape[1]
    tm = 256

    def body(ut_ref, d0, d1, d2, d3, d4, d5, d6, d7, g_ref):
        @pl.when(pl.program_id(0) == 0)
        def _():
            g_ref[...] = jnp.zeros_like(g_ref)

        utb = ut_ref[...]
        for i, dref in enumerate((d0, d1, d2, d3, d4, d5, d6, d7)):
            j, half = divmod(i, 2)
            g_ref[j, :, 512 * half: 512 * (half + 1)] += _dot(utb, dref[...].astype(BF16))

    return pl.pallas_call(
        body, name="inproj_bwd_w", grid=(S // tm,),
        in_specs=[pl.BlockSpec((D_MODEL, tm), lambda s: (0, s))] + [pl.BlockSpec((tm, 512), lambda s: (s, 0))] * 8,
        out_specs=_full((4, D_MODEL, 1024)),
        out_shape=jax.ShapeDtypeStruct((4, D_MODEL, 1024), F32),
        compiler_params=_params(("arbitrary",)),
    )(ut, *dps)


def _adamw(w, g, m, v, name):
    rows, cols = w.shape
    tr = min(rows, 256)

    def body(w_ref, g_ref, m_ref, v_ref, d_ref, nm_ref, nv_ref):
        gg = g_ref[...]
        nm = ADAM_B1 * m_ref[...] + (1.0 - ADAM_B1) * gg
        nv = ADAM_B2 * v_ref[...] + (1.0 - ADAM_B2) * (gg * gg)
        m_hat = nm / (1.0 - ADAM_B1 ** ADAM_STEP)
        v_hat = nv / (1.0 - ADAM_B2 ** ADAM_STEP)
        d_ref[...] = -ADAM_LR * (m_hat / (jnp.sqrt(v_hat) + ADAM_EPS) + ADAM_WD * w_ref[...])
        nm_ref[...] = nm
        nv_ref[...] = nv

    spec = pl.BlockSpec((tr, cols), lambda i: (i, 0))
    sds = jax.ShapeDtypeStruct((rows, cols), F32)
    return pl.pallas_call(
        body, name=name, grid=(rows // tr,),
        in_specs=[spec] * 4, out_specs=[spec] * 3, out_shape=[sds] * 3,
        compiler_params=_params(("parallel",)),
    )(w, g, m, v)


def _place():
    return lax.axis_index("x"), lax.axis_index("y"), lax.axis_index("c")


def _gather_weights(w_in_s, w_out_s):
    def body(win_ref, wout_ref, fin_ref, fout_ref, send_sems, recv_sems):
        x, y, c = _place()
        me = (x, y, c)
        sib = (x, y, 1 - c)
        mine = 2 * x + y
        fin_ref[mine] = win_ref[...].astype(BF16)
        fout_ref[mine] = wout_ref[...].astype(BF16)
        chips = [(1 - x, y), (x, 1 - y), (1 - x, 1 - y)]

        def halves(chip, half):
            return (fin_ref.at[chip, pl.ds(half * 512, 512), :], fout_ref.at[chip, pl.ds(half * 128, 128), :])

        def copy(k, ref, to):
            return pltpu.make_async_remote_copy(src_ref=ref, dst_ref=ref, send_sem=send_sems.at[k],
                                                recv_sem=recv_sems.at[k], device_id=to, device_id_type=MESH)

        first, passed = [], []
        for j, (cx, cy) in enumerate(chips):
            for a, ref in enumerate(halves(mine, c)):
                first.append(copy(2 * j + a, ref, (cx, cy, c)))
        for cp in first:
            cp.start()
        for j, (cx, cy) in enumerate(chips):
            for a, ref in enumerate(halves(2 * cx + cy, c)):
                copy(2 * j + a, ref, me).wait_recv()
                fwd = copy(6 + 2 * j + a, ref, sib)
                fwd.start()
                passed.append(fwd)
        for j, (cx, cy) in enumerate(chips):
            for a, ref in enumerate(halves(2 * cx + cy, 1 - c)):
                copy(6 + 2 * j + a, ref, me).wait_recv()
        for cp in first + passed:
            cp.wait_send()

    vm = pl.BlockSpec(memory_space=pltpu.VMEM)
    return pl.pallas_call(
        body, name="gather_weights",
        in_specs=[vm, vm], out_specs=[vm, vm],
        out_shape=[jax.ShapeDtypeStruct((4, D_MODEL, 1024), BF16), jax.ShapeDtypeStruct((4, 256, D_MODEL), BF16)],
        scratch_shapes=[pltpu.SemaphoreType.DMA((12,)), pltpu.SemaphoreType.DMA((12,))],
        compiler_params=pltpu.CompilerParams(vmem_limit_bytes=VMEM_LIMIT),
    )(w_in_s, w_out_s)


def _swap_halves(g_in, g_out):
    def body(gin_ref, gout_ref, rin_ref, rout_ref, send_sems, recv_sems):
        x, y, c = _place()
        sib = (x, y, 1 - c)
        cps = [pltpu.make_async_remote_copy(src_ref=src.at[:, 1 - c], dst_ref=dst, send_sem=send_sems.at[k],
                                            recv_sem=recv_sems.at[k], device_id=sib, device_id_type=MESH)
               for k, (src, dst) in enumerate(((gin_ref, rin_ref), (gout_ref, rout_ref)))]
        for cp in cps:
            cp.start()
        for cp in cps:
            cp.wait()

    hbm = pl.BlockSpec(memory_space=pl.ANY)
    return pl.pallas_call(
        body, name="swap_halves",
        in_specs=[hbm, hbm], out_specs=[hbm, hbm],
        out_shape=[jax.ShapeDtypeStruct((4,) + g.shape[2:], F32) for g in (g_in, g_out)],
        scratch_shapes=[pltpu.SemaphoreType.DMA((2,)), pltpu.SemaphoreType.DMA((2,))],
    )(g_in, g_out)


def _add_half(g, r, cidx, name):
    n, _, rows, cols = g.shape

    def body(c_ref, g_ref, r_ref, o_ref):
        o_ref[0] = (g_ref[0, 0] + r_ref[0]).astype(BF16)

    return pl.pallas_call(
        body, name=name,
        grid_spec=pltpu.PrefetchScalarGridSpec(
            num_scalar_prefetch=1, grid=(n,),
            in_specs=[pl.BlockSpec((1, 1, rows, cols), lambda j, c_ref: (j, c_ref[0], 0, 0)),
                      pl.BlockSpec((1, rows, cols), lambda j, c_ref: (j, 0, 0))],
            out_specs=pl.BlockSpec((1, rows, cols), lambda j, c_ref: (j, 0, 0))),
        out_shape=jax.ShapeDtypeStruct((n, rows, cols), BF16),
        compiler_params=_params(("parallel",)),
    )(cidx, g, r)


def _exchange_chips(cs_in, cs_out):
    def body(in_ref, out_ref, rin_ref, rout_ref, send_sems, recv_sems, local_sems):
        x, y, c = _place()
        mine = 2 * x + y
        chips = [(1 - x, y), (x, 1 - y), (1 - x, 1 - y)]
        local = [pltpu.make_async_copy(src.at[mine], dst.at[mine], local_sems.at[k])
                 for k, (src, dst) in enumerate(((in_ref, rin_ref), (out_ref, rout_ref)))]
        for cp in local:
            cp.start()
        cps = []
        for j, (cx, cy) in enumerate(chips):
            for a, (src, dst) in enumerate(((in_ref, rin_ref), (out_ref, rout_ref))):
                cps.append(pltpu.make_async_remote_copy(
                    src_ref=src.at[2 * cx + cy], dst_ref=dst.at[mine], send_sem=send_sems.at[2 * j + a],
                    recv_sem=recv_sems.at[2 * j + a], device_id=(cx, cy, c), device_id_type=MESH))
        for cp in cps:
            cp.start()
        for cp in cps:
            cp.wait()
        for cp in local:
            cp.wait()

    hbm = pl.BlockSpec(memory_space=pl.ANY)
    return pl.pallas_call(
        body, name="exchange_chips",
        in_specs=[hbm, hbm], out_specs=[hbm, hbm],
        out_shape=[jax.ShapeDtypeStruct(a.shape, a.dtype) for a in (cs_in, cs_out)],
        scratch_shapes=[pltpu.SemaphoreType.DMA((6,)), pltpu.SemaphoreType.DMA((6,)), pltpu.SemaphoreType.DMA((2,))],
    )(cs_in, cs_out)


def _sum_chips(r, name):
    _, rows, cols = r.shape
    tr = min(rows, 256)

    def body(r_ref, o_ref):
        o_ref[...] = ((r_ref[0].astype(F32) + r_ref[1].astype(F32)) + r_ref[2].astype(F32)) + r_ref[3].astype(F32)

    return pl.pallas_call(
        body, name=name, grid=(rows // tr,),
        in_specs=[pl.BlockSpec((4, tr, cols), lambda i: (0, i, 0))],
        out_specs=pl.BlockSpec((tr, cols), lambda i: (i, 0)),
        out_shape=jax.ShapeDtypeStruct((rows, cols), F32),
        compiler_params=_params(("parallel",)),
    )(r)


def _join_halves(h_in, h_out):
    def body(in_ref, out_ref, fin_ref, fout_ref, send_sems, recv_sems, local_sems):
        x, y, c = _place()
        sib = (x, y, 1 - c)
        pairs = ((in_ref, fin_ref.at[c]), (out_ref, fout_ref.at[c]))
        local = [pltpu.make_async_copy(src, dst, local_sems.at[k]) for k, (src, dst) in enumerate(pairs)]
        cps = [pltpu.make_async_remote_copy(src_ref=src, dst_ref=dst, send_sem=send_sems.at[k],
                                            recv_sem=recv_sems.at[k], device_id=sib, device_id_type=MESH)
               for k, (src, dst) in enumerate(pairs)]
        for cp in local + cps:
            cp.start()
        for cp in cps + local:
            cp.wait()

    hbm = pl.BlockSpec(memory_space=pl.ANY)
    return pl.pallas_call(
        body, name="join_halves",
        in_specs=[hbm, hbm], out_specs=[hbm, hbm],
        out_shape=[jax.ShapeDtypeStruct((2,) + h.shape, F32) for h in (h_in, h_out)],
        scratch_shapes=[pltpu.SemaphoreType.DMA((2,)), pltpu.SemaphoreType.DMA((2,)), pltpu.SemaphoreType.DMA((2,))],
    )(h_in, h_out)


def _allreduce_small(g_nw, g_fw, g_hgw, g_lbl, loss8):
    def body(nw_ref, fw_ref, hgw_ref, lbl_ref, loss_ref, out_ref, slots, send_sems, recv_sems):
        x, y, c = _place()
        me = 4 * x + 2 * y + c
        slots[me] = jnp.zeros((8, D_MODEL), F32)
        slots[me, 0:1, :] = nw_ref[...]
        slots[me, 1:2, :] = fw_ref[...]
        slots[me, 2:3, 0:HG_W] = hgw_ref[...]
        slots[me, 3:4, 0:HG_W] = lbl_ref[0:1, :]
        slots[me, 3:4, HG_W:] = lbl_ref[1:2, :]
        slots[me, 4:5, 0:128] = loss_ref[0:1, :]
        cps = []
        for k in range(1, 8):
            dx, dy, dc = (k >> 2) & 1, (k >> 1) & 1, k & 1
            to = (x ^ dx, y ^ dy, c ^ dc)
            cps.append(pltpu.make_async_remote_copy(
                src_ref=slots.at[me], dst_ref=slots.at[me], send_sem=send_sems.at[k - 1],
                recv_sem=recv_sems.at[k - 1], device_id=to, device_id_type=MESH))
        for cp in cps:
            cp.start()
        for cp in cps:
            cp.wait()
        acc = slots[0]
        for i in range(1, 8):
            acc = acc + slots[i]
        out_ref[...] = acc

    vm = pl.BlockSpec(memory_space=pltpu.VMEM)
    return pl.pallas_call(
        body, name="allreduce_small",
        in_specs=[vm] * 5, out_specs=vm,
        out_shape=jax.ShapeDtypeStruct((8, D_MODEL), F32),
        scratch_shapes=[pltpu.VMEM((8, 8, D_MODEL), F32), pltpu.SemaphoreType.DMA((7,)), pltpu.SemaphoreType.DMA((7,))],
    )(g_nw, g_fw, g_hgw, g_lbl, loss8)


def _rope_tables(S):
    inv_freq = 1.0 / (ROPE_THETA ** (jnp.arange(ROPE_HALF, dtype=F32) / ROPE_HALF))
    ang = jnp.arange(S, dtype=jnp.int32).astype(F32)[:, None] * inv_freq[None, :]
    cos, sin = jnp.cos(ang), jnp.sin(ang)
    cos2 = jnp.concatenate([cos, cos, cos, cos], axis=-1)
    sin2 = jnp.concatenate([-sin, sin, -sin, sin], axis=-1)
    return cos2, sin2


def _local_step(x2, tgt2, norm_w, w_in_full, lbl, hg_norm_w, w_out_full, fnw):
    S = x2.shape[0]
    cos2, sin2 = _rope_tables(S)
    p0, p1, qkv, qkv4, qkv16, p3, ut = _inproj(x2, norm_w, w_in_full, cos2, sin2)
    o_hg, mix_hg, mixt_hg, states = _hg_fwd(p0, p1, lbl, hg_norm_w)
    qkv_ds = [qkv.reshape(1, S, 3 * AT_W), qkv4, qkv16]
    ols = [_att_fwd(q) for q in qkv_ds]
    ol, mix_at, mixt_at = _att_combine(ols, p3)
    dh, dm_hg, dm_at, g_wout, g_fw, loss8 = _outproj(x2, tgt2, mix_hg, mix_at, mixt_hg, mixt_at, w_out_full, fnw)
    dqr, dfl, dv_hg, dz_hg, g_lbl, g_hgw = _hg_bwd(p0, p1, o_hg, dm_hg, states, lbl, hg_norm_w)
    aux, aux4, aux16, dz_at = _att_gate_bwd(dm_at, ol, p3)
    aux_ds = [aux.reshape(1, S, 3 * AT_W), aux4, aux16]
    dqkvs = [_att_bwd(q, a) for q, a in zip(qkv_ds, aux_ds)]
    dq_at, dk_at, dv_at = _att_bwd_combine(dqkvs, cos2, sin2)
    dps = [dqr, dfl, dv_hg, dz_hg, dq_at, dk_at, dv_at, dz_at]
    grad_x, g_nw = _inproj_bwd_x(dps, w_in_full, x2, norm_w, dh)
    g_win = _inproj_bwd_w(ut, dps)
    return loss8, grad_x, g_nw, g_win, g_lbl, g_hgw, g_wout, g_fw


def kernel(x, norm_w, w_in, hgrn_lb_logits, hg_norm_w, w_out, final_norm_w, loss_target, m_norm_w, m_w_in, m_hgrn_lb_logits, m_hg_norm_w, m_w_out, m_final_norm_w, v_norm_w, v_w_in, v_hgrn_lb_logits, v_hg_norm_w, v_w_out, v_final_norm_w):
    S = x.shape[1]
    w_in_full, w_out_full = _gather_weights(w_in[0], w_out[0])
    loss8, grad_x, g_nw, g_win, g_lbl, g_hgw, g_wout, g_fw = _local_step(
        x[0], loss_target[0], norm_w, w_in_full, hgrn_lb_logits, hg_norm_w,
        w_out_full.reshape(D_MODEL, D_MODEL), final_norm_w.reshape(1, D_MODEL))

    cidx = lax.axis_index("c").astype(jnp.int32).reshape(1)
    g_in4 = g_win.reshape(4, 2, 512, 1024)
    g_out4 = g_wout.reshape(4, 2, 128, D_MODEL)
    r_in, r_out = _swap_halves(g_in4, g_out4)
    cs_in = _add_half(g_in4, r_in, cidx, "add_half_in")
    cs_out = _add_half(g_out4, r_out, cidx, "add_half_out")
    x_in, x_out = _exchange_chips(cs_in, cs_out)
    h_in = _sum_chips(x_in, "sum_chips_in")
    h_out = _sum_chips(x_out, "sum_chips_out")
    f_in, f_out = _join_halves(h_in, h_out)
    grad_w_in = f_in.reshape(D_MODEL, 1024)
    grad_w_out = f_out.reshape(256, D_MODEL)

    red = _allreduce_small(g_nw, g_fw, g_hgw, g_lbl, loss8)
    loss = red[4, 0]
    grad_norm_w = red[0:1, :]
    grad_final_norm_w = red[1, :]
    grad_hg_norm_w = red[2:3, :HG_W]
    grad_lbl = jnp.concatenate([red[3:4, :HG_W], red[3:4, HG_W:]], axis=0)

    d_nw, m_nw, v_nw = _adamw(norm_w, grad_norm_w, m_norm_w, v_norm_w, "adamw_norm_w")
    d_win, m_win, v_win = _adamw(w_in[0], grad_w_in, m_w_in[0], v_w_in[0], "adamw_w_in")
    d_lbl, m_lbl, v_lbl = _adamw(hgrn_lb_logits, grad_lbl, m_hgrn_lb_logits, v_hgrn_lb_logits, "adamw_lb_logits")
    d_hgw, m_hgw, v_hgw = _adamw(hg_norm_w, grad_hg_norm_w, m_hg_norm_w, v_hg_norm_w, "adamw_hg_norm_w")
    d_wout, m_wout, v_wout = _adamw(w_out[0], grad_w_out, m_w_out[0], v_w_out[0], "adamw_w_out")
    d_fw, m_fw, v_fw = _adamw(final_norm_w.reshape(1, D_MODEL), grad_final_norm_w.reshape(1, D_MODEL),
                              m_final_norm_w.reshape(1, D_MODEL), v_final_norm_w.reshape(1, D_MODEL), "adamw_final_norm_w")
    e1 = lambda a: a[None]
    flat = lambda a: a.reshape(D_MODEL)
    return (loss, grad_x[None], grad_norm_w, e1(grad_w_in), grad_lbl, grad_hg_norm_w, e1(grad_w_out), grad_final_norm_w,
            d_nw, e1(d_win), d_lbl, d_hgw, e1(d_wout), flat(d_fw),
            m_nw, e1(m_win), m_lbl, m_hgw, e1(m_wout), flat(m_fw),
            v_nw, e1(v_win), v_lbl, v_hgw, e1(v_wout), flat(v_fw))
```

```python
import functools

import jax
import jax.numpy as jnp
import numpy as np
from jax import lax
from jax.experimental import pallas as pl
from jax.experimental.pallas import tpu as pltpu

F32 = jnp.float32
BF16 = jnp.bfloat16
MESH = pl.DeviceIdType.MESH

D_MODEL = 1024
HG_W = 512
AT_W = 512
HEAD_PAIR = 128
ROPE_HALF = 32
ROPE_THETA = 10000.0
EPS = 1e-6
CHUNK = 128
LEVELS = (64, 32, 16, 8)
DIAG = 8
ATT_BLK = 128
DILATIONS = (1, 4, 16)
ATT_SCALE = 0.125
NEG = -1e30
VMEM_LIMIT = 56 * 1024 * 1024

ADAM_LR = 0.001
ADAM_B1 = 0.9
ADAM_B2 = 0.999
ADAM_EPS = 1e-08
ADAM_WD = 0.01
ADAM_STEP = 10


def _iota(shape, dim):
    return lax.broadcasted_iota(jnp.int32, shape, dim)


def _dot(a, b):
    return jnp.dot(a, b, preferred_element_type=F32)


def _dot_nt(a, b):
    return lax.dot_general(a, b, (((1,), (1,)), ((), ())), preferred_element_type=F32)


def _dot_hi(a, b):
    return jnp.dot(a, b, preferred_element_type=F32, precision=lax.Precision.HIGHEST)


def _sig(v):
    return 1.0 / (1.0 + jnp.exp(-v))


def _params(sem=None, vmem=VMEM_LIMIT):
    return pltpu.CompilerParams(dimension_semantics=sem, vmem_limit_bytes=vmem)


def _full(shape):
    n = len(shape)
    return pl.BlockSpec(shape, lambda *_: (0,) * n)


def _rope_rot(y):
    n = y.shape[1]
    first = (_iota(y.shape, 1) & (2 * ROPE_HALF - 1)) < ROPE_HALF
    return jnp.where(first, pltpu.roll(y, n - ROPE_HALF, 1), pltpu.roll(y, ROPE_HALF, 1))


def _dil_spec(d, tm, width):
    return pl.BlockSpec((d, tm // d, width), lambda s: (0, s, 0))


LANES = 128


def _slab_scratch(tm, width):
    return pltpu.VMEM((width // LANES, tm, LANES), F32)


def _to_slabs(v, slabs_ref):
    for j in range(slabs_ref.shape[0]):
        slabs_ref[j] = v[:, LANES * j: LANES * (j + 1)]


def _from_slabs(slabs_ref):
    return jnp.concatenate([slabs_ref[j] for j in range(slabs_ref.shape[0])], axis=1)


def _split_residues(slabs_ref, dst_ref, d, dtype):
    nslab, tm, _ = slabs_ref.shape
    for r in range(d):
        for j in range(nslab):
            dst_ref[r, :, LANES * j: LANES * (j + 1)] = slabs_ref[j, pl.ds(r, tm // d, stride=d), :].astype(dtype)


def _merge_residues(src_ref, slabs_ref, d):
    nslab, tm, _ = slabs_ref.shape
    for r in range(d):
        for j in range(nslab):
            slabs_ref[j, pl.ds(r, tm // d, stride=d), :] = src_ref[r, :, LANES * j: LANES * (j + 1)]


def _inproj(x2, norm_w, w_in_full, cos2, sin2):
    S = x2.shape[0]
    tm = 256

    def body(x_ref, nw_ref, w_ref, cos_ref, sin_ref, p0_ref, p1_ref, qkv_ref, qkv4_ref, qkv16_ref, p3_ref, ut_ref, scr):
        x = x_ref[...]
        r = lax.rsqrt(jnp.mean(x * x, axis=-1, keepdims=True) + EPS)
        u = x * r * nw_ref[...]
        ub = u.astype(BF16)
        ut_ref[...] = u.T.astype(BF16)
        p0_ref[...] = _dot(ub, w_ref[0])
        p1_ref[...] = _dot(ub, w_ref[1])
        y2 = _dot(ub, w_ref[2])
        cosf = jnp.tile(cos_ref[...], (1, 8))
        sinf = jnp.tile(sin_ref[...], (1, 8))
        y3 = _dot(ub, w_ref[3])
        p3_ref[...] = y3
        qkv = jnp.concatenate([y2 * cosf + _rope_rot(y2) * sinf, y3[:, :AT_W]], axis=1)
        qkv_ref[...] = qkv.astype(BF16)
        _to_slabs(qkv, scr)
        _split_residues(scr, qkv4_ref, 4, BF16)
        _split_residues(scr, qkv16_ref, 16, BF16)

    row = lambda w: pl.BlockSpec((tm, w), lambda s: (s, 0))
    qkv_w = 3 * AT_W
    return pl.pallas_call(
        body, name="inproj", grid=(S // tm,),
        in_specs=[row(D_MODEL), _full((1, D_MODEL)), _full((4, D_MODEL, 1024)), row(128), row(128)],
        out_specs=[row(1024), row(1024), row(qkv_w), _dil_spec(4, tm, qkv_w), _dil_spec(16, tm, qkv_w), row(1024),
                   pl.BlockSpec((D_MODEL, tm), lambda s: (0, s))],
        out_shape=[jax.ShapeDtypeStruct((S, 1024), F32), jax.ShapeDtypeStruct((S, 1024), F32),
                   jax.ShapeDtypeStruct((S, qkv_w), BF16), jax.ShapeDtypeStruct((4, S // 4, qkv_w), BF16),
                   jax.ShapeDtypeStruct((16, S // 16, qkv_w), BF16), jax.ShapeDtypeStruct((S, 1024), F32),
                   jax.ShapeDtypeStruct((D_MODEL, S), BF16)],
        scratch_shapes=[_slab_scratch(tm, qkv_w)],
        compiler_params=_params(("parallel",)),
    )(x2, norm_w, w_in_full, cos2, sin2)


HG_HPS = 4
N_LEV = len(LEVELS)


def _hg_const_arrays():
    r = np.arange(CHUNK)[:, None]
    c = np.arange(CHUNK)[None, :]
    tris = np.stack([r >= c, r <= c])
    lm = [((r // (2 * m)) == (c // (2 * m))) & (r % (2 * m) >= m) & (c % (2 * m) < m) for m in LEVELS]
    dm = [(c == r - dl) & (r % DIAG >= dl) for dl in range(DIAG)]
    masks = np.stack(lm + [x.T for x in lm] + dm)
    return jnp.asarray(tris, BF16), jnp.asarray(masks, F32)


def _split2(a):
    hi = a.astype(BF16)
    return hi, (a - hi.astype(F32)).astype(BF16)


def _dot3(a, b):
    ah, al = _split2(a)
    bh, bl = _split2(b)
    n = b.shape[1]
    p = _dot(ah, jnp.concatenate([bh, bl], axis=1))
    return (p[:, :n] + p[:, n:]) + _dot(al, bh)


def _tri_dot(tri, a):
    a1 = a.astype(BF16)
    r1 = a - a1.astype(F32)
    a2 = r1.astype(BF16)
    a3 = (r1 - a2.astype(F32)).astype(BF16)
    n = a.shape[1]
    p = _dot(tri, jnp.concatenate([a1, a2, a3], axis=1))
    return (p[:, :n] + p[:, n:2 * n]) + p[:, 2 * n:]


def _rowsum(t):
    return _dot(t.astype(BF16), jnp.ones((t.shape[1], t.shape[1]), BF16))


def _level_refs(b):
    refs = []
    for m in LEVELS:
        parts = [jnp.broadcast_to(b[r0 + m - 1: r0 + m, :], (2 * m, b.shape[1])) for r0 in range(0, CHUNK, 2 * m)]
        refs.append(parts[0] if len(parts) == 1 else jnp.concatenate(parts, axis=0))
    return refs


def _hg_lb(lbl_ref):
    l0 = lbl_ref[0:1, :]
    l1 = lbl_ref[1:2, :]
    mx = jnp.maximum(l0, l1)
    e0 = jnp.exp(l0 - mx)
    e1 = jnp.exp(l1 - mx)
    p0 = e0 / (e0 + e1)
    lb = jnp.clip(p0, 1e-6, 1.0 - 1e-6)
    inside = (p0 >= 1e-6) & (p0 <= 1.0 - 1e-6)
    dlb_dl0 = jnp.where(inside, p0 * (e1 / (e0 + e1)), 0.0)
    return lb, dlb_dl0


def _sigmoid(v):
    return 0.5 * jnp.tanh(0.5 * v) + 0.5


def _hg_gates(qr, fl, lb):
    sig = _sigmoid(fl)
    f = lb + (1.0 - lb) * sig
    g = jnp.log(f)
    k = (1.0 - lb) * (1.0 - sig)
    sq = _sigmoid(qr)
    q = qr * sq
    return sig, f, g, k, sq, q


def _hg_levels(q, k, b, mk_ref):
    refs = _level_refs(b)
    a = jnp.zeros((CHUNK, CHUNK), F32)
    eqs, eks, qts, kts = [], [], [], []
    for i in range(N_LEV):
        eq = jnp.exp(jnp.minimum(b - refs[i], 0.0))
        ek = jnp.exp(jnp.minimum(refs[i] - b, 0.0))
        qt = (q * eq).astype(BF16)
        kt = (k * ek).astype(BF16)
        a = a + _dot_nt(qt, kt) * mk_ref[i]
        eqs.append(eq); eks.append(ek); qts.append(qt); kts.append(kt)
    return a, eqs, eks, qts, kts


def _hg_specs(nc, rev):
    cc = (lambda c: nc - 1 - c) if rev else (lambda c: c)
    w = 128 * HG_HPS
    blk = lambda off: pl.BlockSpec((CHUNK, w), lambda h, c: (cc(c), h + off))
    vec = pl.BlockSpec((1, w), lambda h, c: (0, h))
    lb2 = pl.BlockSpec((2, w), lambda h, c: (0, h))
    st = pl.BlockSpec((1, HG_HPS, 128, 128), lambda h, c: (cc(c), h, 0, 0))
    consts = [_full((2, CHUNK, CHUNK)), _full((2 * N_LEV + DIAG, CHUNK, CHUNK))]
    return blk, vec, lb2, st, consts


def _hg_fwd(p0, p1, lbl, gw):
    S = p0.shape[0]
    nc = S // CHUNK
    ng = 4 // HG_HPS

    def body(qr_ref, fl_ref, v_ref, z_ref, lbl_ref, gw_ref, tri_ref, mk_ref,
             o_ref, mix_ref, mixt_ref, st_ref, state):
        c = pl.program_id(1)

        @pl.when(c == 0)
        def _():
            state[...] = jnp.zeros_like(state)

        lb_all, _ = _hg_lb(lbl_ref)
        for hh in range(HG_HPS):
            sl = slice(128 * hh, 128 * (hh + 1))
            qr, fl, v, z = qr_ref[:, sl], fl_ref[:, sl], v_ref[:, sl], z_ref[:, sl]
            _, _, g, k, _, q = _hg_gates(qr, fl, lb_all[:, sl])
            b = _tri_dot(tri_ref[0], g)
            b_last = b[CHUNK - 1: CHUNK, :]
            a, _, _, _, _ = _hg_levels(q, k, b, mk_ref)
            for dl in range(DIAG):
                if dl == 0:
                    term = q * k
                else:
                    e = jnp.exp(jnp.minimum(b - pltpu.roll(b, dl, 0), 0.0))
                    term = q * pltpu.roll(k, dl, 0) * e
                a = a + _rowsum(term) * mk_ref[2 * N_LEV + dl]
            st = state[hh]
            st_ref[0, hh] = st
            o = _dot_nt((q * jnp.exp(b)).astype(BF16), st.astype(BF16)) + _dot(a.astype(BF16), v.astype(BF16))
            state[hh] = st * jnp.exp(b_last) + _dot3(v.T, k * jnp.exp(b_last - b))
            o_ref[:, sl] = o
            rs = lax.rsqrt(jnp.mean(o * o, axis=-1, keepdims=True) + EPS)
            mixed = o * rs * gw_ref[:, sl] * (z * _sigmoid(z))
            mix_ref[:, sl] = mixed.astype(BF16)
            mixt_ref[sl, :] = mixed.T.astype(BF16)

    blk, vec, lb2, st_spec, consts = _hg_specs(nc, False)
    tris, masks = _hg_const_arrays()
    return pl.pallas_call(
        body, name="hg_fwd", grid=(ng, nc),
        in_specs=[blk(0), blk(ng), blk(0), blk(ng), lb2, vec] + consts,
        out_specs=[blk(0), blk(0), pl.BlockSpec((128 * HG_HPS, CHUNK), lambda h, c: (h, c)), st_spec],
        out_shape=[jax.ShapeDtypeStruct((S, HG_W), F32), jax.ShapeDtypeStruct((S, HG_W), BF16),
                   jax.ShapeDtypeStruct((HG_W, S), BF16), jax.ShapeDtypeStruct((nc, 4, 128, 128), F32)],
        scratch_shapes=[pltpu.VMEM((HG_HPS, 128, 128), F32)],
        compiler_params=_params(("parallel", "arbitrary")),
    )(p0, p0, p1, p1, lbl, gw, tris, masks)


def _hg_bwd(p0, p1, o_raw, dm, states, lbl, gw):
    S = p0.shape[0]
    nc = S // CHUNK
    ng = 4 // HG_HPS
    w = 128 * HG_HPS

    def body(qr_ref, fl_ref, v_ref, z_ref, o_ref, dm_ref, st_ref, lbl_ref, gw_ref, tri_ref, mk_ref,
             dqr_ref, dfl_ref, dv_ref, dz_ref, glbl_ref, ggw_ref, dstate, carry, acc_lb, acc_gw):
        c = pl.program_id(1)

        @pl.when(c == 0)
        def _():
            dstate[...] = jnp.zeros_like(dstate)
            carry[...] = jnp.zeros_like(carry)
            acc_lb[...] = jnp.zeros_like(acc_lb)
            acc_gw[...] = jnp.zeros_like(acc_gw)

        lb_all, dlb_dl0 = _hg_lb(lbl_ref)
        for hh in range(HG_HPS):
            sl = slice(128 * hh, 128 * (hh + 1))
            lb = lb_all[:, sl]
            qr, fl, v, z = qr_ref[:, sl], fl_ref[:, sl], v_ref[:, sl], z_ref[:, sl]
            sig, f, g, k, sq, q = _hg_gates(qr, fl, lb)
            b = _tri_dot(tri_ref[0], g)
            b_last = b[CHUNK - 1: CHUNK, :]
            a, eqs, eks, qts, kts = _hg_levels(q, k, b, mk_ref)

            o = o_ref[:, sl]
            dmix = dm_ref[:, sl]
            gwv = gw_ref[:, sl]
            rs = lax.rsqrt(jnp.mean(o * o, axis=-1, keepdims=True) + EPS)
            oh = o * rs
            sz = _sigmoid(z)
            dz_ref[:, sl] = (dmix * (oh * gwv) * (sz * (1.0 + z * (1.0 - sz)))).astype(BF16)
            don = dmix * (z * sz)
            acc_gw[0:1, sl] += jnp.sum(don * oh, axis=0, keepdims=True)
            dy = don * gwv
            do = rs * (dy - oh * jnp.mean(dy * oh, axis=-1, keepdims=True))

            st = st_ref[0, hh]
            dst = dstate[hh]
            dob = do.astype(BF16)
            eb = jnp.exp(b)
            edec = jnp.exp(b_last - b)
            dq = _dot3(do, st) * eb
            dk = _dot3(v, dst) * edec
            db = q * dq - k * dk
            da = _dot_nt(dob, v.astype(BF16))
            da_t = da.T
            for i in range(N_LEV):
                gq = _dot((da * mk_ref[i]).astype(BF16), kts[i])
                gk = _dot((da_t * mk_ref[N_LEV + i]).astype(BF16), qts[i])
                dq = dq + eqs[i] * gq
                dk = dk + eks[i] * gk
                db = db + (qts[i].astype(F32) * gq - kts[i].astype(F32) * gk)
            for dl in range(DIAG):
                dmask = mk_ref[2 * N_LEV + dl]
                dc = _rowsum(da * dmask)
                if dl == 0:
                    a = a + _rowsum(q * k) * dmask
                    dq = dq + dc * k
                    dk = dk + dc * q
                else:
                    e = jnp.exp(jnp.minimum(b - pltpu.roll(b, dl, 0), 0.0))
                    ks = pltpu.roll(k, dl, 0)
                    a = a + _rowsum(q * ks * e) * dmask
                    tq = dc * ks * e
                    tk = pltpu.roll(dc * q * e, CHUNK - dl, 0)
                    dq = dq + tq
                    dk = dk + tk
                    db = db + (q * tq - k * tk)
            dv_ref[:, sl] = (_dot(a.T.astype(BF16), dob) + _dot_nt((k * edec).astype(BF16), dst.astype(BF16))).astype(BF16)
            dstate[hh] = dst * jnp.exp(b_last) + _dot3(do.T, q * eb)

            dg = _tri_dot(tri_ref[1], db) + carry[0:1, sl]
            carry[0:1, sl] += jnp.sum(db, axis=0, keepdims=True)
            t = dg / f - dk
            dfl_ref[:, sl] = (t * (1.0 - lb) * sig * (1.0 - sig)).astype(BF16)
            acc_lb[0:1, sl] += jnp.sum(t * (1.0 - sig), axis=0, keepdims=True)
            dqr_ref[:, sl] = (dq * (sq * (1.0 + qr * (1.0 - sq)))).astype(BF16)

        @pl.when(c == nc - 1)
        def _():
            gl0 = acc_lb[0:1, :] * dlb_dl0
            glbl_ref[0:1, :] = gl0
            glbl_ref[1:2, :] = -gl0
            ggw_ref[...] = acc_gw[0:1, :]

    blk, vec, lb2, st_spec, consts = _hg_specs(nc, True)
    tris, masks = _hg_const_arrays()
    act = jax.ShapeDtypeStruct((S, HG_W), BF16)
    return pl.pallas_call(
        body, name="hg_bwd", grid=(ng, nc),
        in_specs=[blk(0), blk(ng), blk(0), blk(ng), blk(0), blk(0), st_spec, lb2, vec] + consts,
        out_specs=[blk(0), blk(0), blk(0), blk(0), lb2, vec],
        out_shape=[act, act, act, act, jax.ShapeDtypeStruct((2, HG_W), F32), jax.ShapeDtypeStruct((1, HG_W), F32)],
        scratch_shapes=[pltpu.VMEM((HG_HPS, 128, 128), F32), pltpu.VMEM((8, w), F32),
                        pltpu.VMEM((8, w), F32), pltpu.VMEM((8, w), F32)],
        compiler_params=_params(("parallel", "arbitrary")),
    )(p0, p0, p1, p1, o_raw, dm, states, lbl, gw, tris, masks)


def _att_bias():
    qi = np.arange(ATT_BLK)[:, None]
    kj = np.arange(2 * ATT_BLK)[None, :]
    band = (kj >= qi) & (kj <= qi + ATT_BLK)
    qm = np.stack([band & (kj >= ATT_BLK), band])
    cur = (kj < ATT_BLK) & (qi <= kj)
    km = np.stack([cur, cur | ((kj >= ATT_BLK) & (qi >= kj - ATT_BLK))])
    to_bias = lambda m: jnp.asarray(np.where(m, 0.0, NEG), F32)
    return to_bias(qm), to_bias(km)


def _att_fwd(qkv_d):
    d, L, _ = qkv_d.shape
    nb = L // ATT_BLK

    def body(q_ref, kp_ref, kc_ref, vp_ref, vc_ref, bias_ref, ol_ref, s_scr, p_scr):
        bias = bias_ref[jnp.minimum(pl.program_id(1), 1)]
        bias2 = jnp.concatenate([bias, bias], axis=0)
        first = _iota((ATT_BLK, HEAD_PAIR), 1) < 64
        pairs = [slice(HEAD_PAIR * hp, HEAD_PAIR * (hp + 1)) for hp in range(4)]
        for hp, sl in enumerate(pairs):
            q2 = q_ref[0, :, sl] * ATT_SCALE
            zero = jnp.zeros_like(q2)
            kcat = jnp.concatenate([kp_ref[0, :, sl], kc_ref[0, :, sl]], axis=0)
            qs = jnp.concatenate([jnp.where(first, q2, zero), jnp.where(first, zero, q2)], axis=0)
            s_scr[hp] = _dot_nt(qs, kcat)
        stats = []
        for hp in range(4):
            s = s_scr[hp] + bias2
            m = jnp.max(s, axis=-1, keepdims=True)
            p = jnp.exp(s - m)
            l = jnp.sum(p, axis=-1, keepdims=True)
            p_scr[hp] = p.astype(BF16)
            stats.append((l, m + jnp.log(l)))
        for hp, sl in enumerate(pairs):
            l, lse = stats[hp]
            vcat = jnp.concatenate([vp_ref[0, :, sl], vc_ref[0, :, sl]], axis=0)
            o = _dot(p_scr[hp], vcat) / l
            ol_ref[0, :, sl] = jnp.where(first, o[:ATT_BLK], o[ATT_BLK:])
            ol_ref[0, :, AT_W + HEAD_PAIR * hp: AT_W + HEAD_PAIR * (hp + 1)] = jnp.where(first, lse[:ATT_BLK], lse[ATT_BLK:])

    cur = lambda j: pl.BlockSpec((1, ATT_BLK, AT_W), lambda r, n: (r, n, j))
    prev = lambda j: pl.BlockSpec((1, ATT_BLK, AT_W), lambda r, n: (r, jnp.maximum(n - 1, 0), j))
    return pl.pallas_call(
        body, name=f"att_fwd_d{d}", grid=(d, nb),
        in_specs=[cur(0), prev(1), cur(1), prev(2), cur(2), _full((2, ATT_BLK, 2 * ATT_BLK))],
        out_specs=pl.BlockSpec((1, ATT_BLK, 2 * AT_W), lambda r, n: (r, n, 0)),
        out_shape=jax.ShapeDtypeStruct((d, L, 2 * AT_W), F32),
        scratch_shapes=[pltpu.VMEM((4, 2 * ATT_BLK, 2 * ATT_BLK), F32), pltpu.VMEM((4, 2 * ATT_BLK, 2 * ATT_BLK), BF16)],
        compiler_params=_params(("parallel", "parallel")),
    )(qkv_d, qkv_d, qkv_d, qkv_d, qkv_d, _att_bias()[0])


def _att_combine(ols, p3):
    S = p3.shape[0]
    tm = 512

    def body(a_ref, b4_ref, c16_ref, z_ref, ol_ref, mix_ref, mixt_ref, b_scr, c_scr):
        _merge_residues(b4_ref, b_scr, 4)
        _merge_residues(c16_ref, c_scr, 16)
        pats = (a_ref[...], _from_slabs(b_scr), _from_slabs(c_scr))
        os_ = [v[:, :AT_W] for v in pats]
        ls = [v[:, AT_W:] for v in pats]
        mx = jnp.maximum(jnp.maximum(ls[0], ls[1]), ls[2])
        es = [jnp.exp(l - mx) for l in ls]
        zs = es[0] + es[1] + es[2]
        o = (es[0] * os_[0] + es[1] * os_[1] + es[2] * os_[2]) / zs
        ol_ref[:, :AT_W] = o
        ol_ref[:, AT_W:] = mx + jnp.log(zs)
        z = z_ref[...]
        mixed = o * (z * _sig(z))
        mix_ref[...] = mixed.astype(BF16)
        mixt_ref[...] = mixed.T.astype(BF16)

    row = lambda w: pl.BlockSpec((tm, w), lambda s: (s, 0))
    return pl.pallas_call(
        body, name="att_combine", grid=(S // tm,),
        in_specs=[row(1024), _dil_spec(4, tm, 1024), _dil_spec(16, tm, 1024), pl.BlockSpec((tm, AT_W), lambda s: (s, 1))],
        out_specs=[row(1024), row(AT_W), pl.BlockSpec((AT_W, tm), lambda s: (0, s))],
        out_shape=[jax.ShapeDtypeStruct((S, 1024), F32), jax.ShapeDtypeStruct((S, AT_W), BF16),
                   jax.ShapeDtypeStruct((AT_W, S), BF16)],
        scratch_shapes=[_slab_scratch(tm, 1024), _slab_scratch(tm, 1024)],
        compiler_params=_params(("parallel",)),
    )(ols[0].reshape(S, 1024), ols[1], ols[2], p3)


AUX_W = 2 * AT_W
AUX_LSE_LANE = 32


def _att_gate_bwd(dm_at, ol, p3):
    S = p3.shape[0]
    tm = 512

    def body(dm_ref, ol_ref, z_ref, aux_ref, aux4_ref, aux16_ref, dz_ref, scr):
        o = ol_ref[:, :AT_W]
        z = z_ref[...]
        dm = dm_ref[...]
        sz = _sig(z)
        dz_ref[...] = (dm * o * (sz * (1.0 + z * (1.0 - sz)))).astype(BF16)
        do = dm * (z * sz)
        aux_ref[:, :AT_W] = do
        ones = ((_iota((HEAD_PAIR, HEAD_PAIR), 0) < 64) == (_iota((HEAD_PAIR, HEAD_PAIR), 1) < 64)).astype(F32)
        d_lanes = (_iota((tm, HEAD_PAIR), 1) & 63) < AUX_LSE_LANE
        prod = do * o
        for hp in range(4):
            sl = slice(HEAD_PAIR * hp, HEAD_PAIR * (hp + 1))
            lse = ol_ref[:, AT_W + HEAD_PAIR * hp: AT_W + HEAD_PAIR * (hp + 1)]
            aux_ref[:, AT_W + HEAD_PAIR * hp: AT_W + HEAD_PAIR * (hp + 1)] = jnp.where(d_lanes, _dot_hi(prod[:, sl], ones), lse)
        _to_slabs(aux_ref[...], scr)
        _split_residues(scr, aux4_ref, 4, F32)
        _split_residues(scr, aux16_ref, 16, F32)

    row = lambda w: pl.BlockSpec((tm, w), lambda s: (s, 0))
    aw = AUX_W
    return pl.pallas_call(
        body, name="att_gate_bwd", grid=(S // tm,),
        in_specs=[row(AT_W), row(1024), pl.BlockSpec((tm, AT_W), lambda s: (s, 1))],
        out_specs=[row(aw), _dil_spec(4, tm, aw), _dil_spec(16, tm, aw), row(AT_W)],
        out_shape=[jax.ShapeDtypeStruct((S, aw), F32), jax.ShapeDtypeStruct((4, S // 4, aw), F32),
                   jax.ShapeDtypeStruct((16, S // 16, aw), F32), jax.ShapeDtypeStruct((S, AT_W), BF16)],
        scratch_shapes=[_slab_scratch(tm, aw)],
        compiler_params=_params(("parallel",)),
    )(dm_at, ol, p3)


def _att_bwd(qkv_d, aux_d):
    d, L, _ = qkv_d.shape
    nb = L // ATT_BLK

    def body(qc_ref, qn_ref, kp_ref, kc_ref, vp_ref, vc_ref, ac_ref, an_ref, bq_ref, bk_ref, dqkv_ref,
             s_scr, dp_scr, st_scr, dpt_scr, ds_scr, pt_scr, dst_scr):
        n = pl.program_id(1)
        bias = bq_ref[jnp.minimum(n, 1)]
        bias_t = bk_ref[jnp.minimum(nb - 1 - n, 1)]
        bias2 = jnp.concatenate([bias, bias], axis=0)
        bias_t2 = jnp.concatenate([bias_t, bias_t], axis=0)
        first = _iota((ATT_BLK, HEAD_PAIR), 1) < 64
        pairs = [slice(HEAD_PAIR * hp, HEAD_PAIR * (hp + 1)) for hp in range(4)]

        def stack(t):
            zero = jnp.zeros_like(t)
            return jnp.concatenate([jnp.where(first, t, zero), jnp.where(first, zero, t)], axis=0)

        def unstack(t2):
            return jnp.where(first, t2[:ATT_BLK], t2[ATT_BLK:])

        def operands(sl):
            q2, k2, v2 = qc_ref[0, :, sl], kc_ref[0, :, sl], vc_ref[0, :, sl]
            kcat = jnp.concatenate([kp_ref[0, :, sl], k2], axis=0)
            vcat = jnp.concatenate([vp_ref[0, :, sl], v2], axis=0)
            qcat = jnp.concatenate([q2, qn_ref[0, :, sl]], axis=0)
            do_c = ac_ref[0, :, sl]
            docat = jnp.concatenate([do_c, an_ref[0, :, sl]], axis=0).astype(BF16)
            return q2, k2, v2, kcat, vcat, qcat, do_c, docat

        for hp, sl in enumerate(pairs):
            q2, k2, v2, kcat, vcat, qcat, do_c, docat = operands(sl)
            s_scr[hp] = _dot_nt(stack(q2 * ATT_SCALE), kcat)
            dp_scr[hp] = _dot_nt(stack(do_c.astype(BF16)), vcat)
            st_scr[hp] = _dot_nt(stack(k2 * ATT_SCALE), qcat)
            dpt_scr[hp] = _dot_nt(stack(v2), docat)
        for hp in range(4):
            sl_a = slice(AT_W + HEAD_PAIR * hp, AT_W + HEAD_PAIR * (hp + 1))
            dl_c = ac_ref[0, :, sl_a]
            col = lambda t, o: jnp.concatenate([t[:, o: o + 1], t[:, 64 + o: 65 + o]], axis=0)
            p = jnp.exp(s_scr[hp] + bias2 - col(dl_c, AUX_LSE_LANE))
            ds_scr[hp] = (p * (dp_scr[hp] - col(dl_c, 0))).astype(BF16)
            dl_t = jnp.concatenate([dl_c, an_ref[0, :, sl_a]], axis=0).T
            row = lambda t, o: jnp.concatenate([jnp.broadcast_to(t[o: o + 1, :], (ATT_BLK, 2 * ATT_BLK)),
                                                jnp.broadcast_to(t[64 + o: 65 + o, :], (ATT_BLK, 2 * ATT_BLK))], axis=0)
            pt = jnp.exp(st_scr[hp] + bias_t2 - row(dl_t, AUX_LSE_LANE))
            pt_scr[hp] = pt.astype(BF16)
            dst_scr[hp] = (pt * (dpt_scr[hp] - row(dl_t, 0))).astype(BF16)
        for hp, sl in enumerate(pairs):
            _, _, _, kcat, _, qcat, _, docat = operands(sl)
            dqkv_ref[0, :, sl] = unstack(_dot(ds_scr[hp], kcat)) * ATT_SCALE
            dqkv_ref[0, :, AT_W + HEAD_PAIR * hp: AT_W + HEAD_PAIR * (hp + 1)] = unstack(_dot(dst_scr[hp], qcat)) * ATT_SCALE
            dqkv_ref[0, :, 2 * AT_W + HEAD_PAIR * hp: 2 * AT_W + HEAD_PAIR * (hp + 1)] = unstack(_dot(pt_scr[hp], docat))

    cur = lambda j: pl.BlockSpec((1, ATT_BLK, AT_W), lambda r, n: (r, n, j))
    prev = lambda j: pl.BlockSpec((1, ATT_BLK, AT_W), lambda r, n: (r, jnp.maximum(n - 1, 0), j))
    nxt = lambda j: pl.BlockSpec((1, ATT_BLK, AT_W), lambda r, n: (r, jnp.minimum(n + 1, nb - 1), j))
    return pl.pallas_call(
        body, name=f"att_bwd_d{d}", grid=(d, nb),
        in_specs=[cur(0), nxt(0), prev(1), cur(1), prev(2), cur(2),
                  pl.BlockSpec((1, ATT_BLK, AUX_W), lambda r, n: (r, n, 0)),
                  pl.BlockSpec((1, ATT_BLK, AUX_W), lambda r, n: (r, jnp.minimum(n + 1, nb - 1), 0)),
                  _full((2, ATT_BLK, 2 * ATT_BLK)), _full((2, ATT_BLK, 2 * ATT_BLK))],
        out_specs=pl.BlockSpec((1, ATT_BLK, 3 * AT_W), lambda r, n: (r, n, 0)),
        out_shape=jax.ShapeDtypeStruct((d, L, 3 * AT_W), F32),
        scratch_shapes=[pltpu.VMEM((4, 2 * ATT_BLK, 2 * ATT_BLK), F32)] * 4 + [pltpu.VMEM((4, 2 * ATT_BLK, 2 * ATT_BLK), BF16)] * 3,
        compiler_params=_params(("parallel", "parallel")),
    )(qkv_d, qkv_d, qkv_d, qkv_d, qkv_d, qkv_d, aux_d, aux_d, *_att_bias())


def _att_bwd_combine(dqkvs, cos2, sin2):
    S = dqkvs[0].shape[1]
    tm = 512

    def body(a_ref, b4_ref, c16_ref, cos_ref, sin_ref, dq_ref, dk_ref, dv_ref, b_scr, c_scr):
        _merge_residues(b4_ref, b_scr, 4)
        _merge_residues(c16_ref, c_scr, 16)
        t = a_ref[...] + _from_slabs(b_scr) + _from_slabs(c_scr)
        dy = t[:, : 2 * AT_W]
        cosf = jnp.tile(cos_ref[...], (1, 8))
        sinf = jnp.tile(sin_ref[...], (1, 8))
        dx = dy * cosf - _rope_rot(dy) * sinf
        dq_ref[...] = dx[:, :AT_W].astype(BF16)
        dk_ref[...] = dx[:, AT_W:].astype(BF16)
        dv_ref[...] = t[:, 2 * AT_W:].astype(BF16)

    row = lambda w: pl.BlockSpec((tm, w), lambda s: (s, 0))
    act = jax.ShapeDtypeStruct((S, AT_W), BF16)
    return pl.pallas_call(
        body, name="att_bwd_combine", grid=(S // tm,),
        in_specs=[row(3 * AT_W), _dil_spec(4, tm, 3 * AT_W), _dil_spec(16, tm, 3 * AT_W), row(128), row(128)],
        out_specs=[row(AT_W), row(AT_W), row(AT_W)],
        out_shape=[act, act, act],
        scratch_shapes=[_slab_scratch(tm, 3 * AT_W), _slab_scratch(tm, 3 * AT_W)],
        compiler_params=_params(("parallel",)),
    )(dqkvs[0].reshape(S, 3 * AT_W), dqkvs[1], dqkvs[2], cos2, sin2)


def _outproj(x2, tgt2, mix_hg, mix_at, mixt_hg, mixt_at, w_out_full, fnw):
    S = x2.shape[0]
    tm = 256
    ns = S // tm

    def body(x_ref, t_ref, mh_ref, ma_ref, mht_ref, mat_ref, w_ref, fw_ref,
             dh_ref, dmh_ref, dma_ref, gw_ref, gfw_ref, loss_ref):
        s = pl.program_id(0)

        @pl.when(s == 0)
        def _():
            gw_ref[...] = jnp.zeros_like(gw_ref)
            gfw_ref[...] = jnp.zeros_like(gfw_ref)
            loss_ref[...] = jnp.zeros_like(loss_ref)

        y = _dot(mh_ref[...], w_ref[:HG_W, :]) + _dot(ma_ref[...], w_ref[HG_W:, :])
        h = x_ref[...] + y
        r = lax.rsqrt(jnp.mean(h * h, axis=-1, keepdims=True) + EPS)
        hn = h * r
        fw = fw_ref[...]
        err = hn * fw - t_ref[...]
        loss_ref[...] += 0.5 * jnp.sum(jnp.mean(err * err, axis=-1, keepdims=True))
        dout = err * (1.0 / D_MODEL)
        gfw_ref[...] += jnp.sum(dout * hn, axis=0, keepdims=True)
        dhn = dout * fw
        dh = r * (dhn - hn * jnp.mean(dhn * hn, axis=-1, keepdims=True))
        dh_ref[...] = dh
        dhb = dh.astype(BF16)
        dmh_ref[...] = _dot_nt(dhb, w_ref[:HG_W, :])
        dma_ref[...] = _dot_nt(dhb, w_ref[HG_W:, :])
        gw_ref[:HG_W, :] += _dot(mht_ref[...], dhb)
        gw_ref[HG_W:, :] += _dot(mat_ref[...], dhb)

    row = lambda w: pl.BlockSpec((tm, w), lambda s: (s, 0))
    colb = pl.BlockSpec((HG_W, tm), lambda s: (0, s))
    return pl.pallas_call(
        body, name="outproj", grid=(ns,),
        in_specs=[row(D_MODEL), row(D_MODEL), row(HG_W), row(AT_W), colb, colb,
                  _full((D_MODEL, D_MODEL)), _full((1, D_MODEL))],
        out_specs=[row(D_MODEL), row(HG_W), row(AT_W), _full((D_MODEL, D_MODEL)), _full((1, D_MODEL)), _full((8, 128))],
        out_shape=[jax.ShapeDtypeStruct((S, D_MODEL), F32), jax.ShapeDtypeStruct((S, HG_W), F32),
                   jax.ShapeDtypeStruct((S, AT_W), F32), jax.ShapeDtypeStruct((D_MODEL, D_MODEL), F32),
                   jax.ShapeDtypeStruct((1, D_MODEL), F32), jax.ShapeDtypeStruct((8, 128), F32)],
        compiler_params=_params(("arbitrary",)),
    )(x2, tgt2, mix_hg, mix_at, mixt_hg, mixt_at, w_out_full, fnw)


def _inproj_bwd_x(dps, w_in_full, x2, norm_w, dh):
    S = x2.shape[0]
    tm = 256

    def body(d0, d1, d2, d3, d4, d5, d6, d7, w_ref, x_ref, nw_ref, dh_ref, gx_ref, gnw_ref):
        s = pl.program_id(0)

        @pl.when(s == 0)
        def _():
            gnw_ref[...] = jnp.zeros_like(gnw_ref)

        du = jnp.zeros((tm, D_MODEL), F32)
        for i, dref in enumerate((d0, d1, d2, d3, d4, d5, d6, d7)):
            j, half = divmod(i, 2)
            du = du + _dot_nt(dref[...], w_ref[j, :, 512 * half: 512 * (half + 1)])
        x = x_ref[...]
        r = lax.rsqrt(jnp.mean(x * x, axis=-1, keepdims=True) + EPS)
        xh = x * r
        gnw_ref[...] += jnp.sum(du * xh, axis=0, keepdims=True)
        dun = du * nw_ref[...]
        gx_ref[...] = dh_ref[...] + r * (dun - xh * jnp.mean(dun * xh, axis=-1, keepdims=True))

    row = lambda w: pl.BlockSpec((tm, w), lambda s: (s, 0))
    return pl.pallas_call(
        body, name="inproj_bwd_x", grid=(S // tm,),
        in_specs=[row(512)] * 8 + [_full((4, D_MODEL, 1024)), row(D_MODEL), _full((1, D_MODEL)), row(D_MODEL)],
        out_specs=[row(D_MODEL), _full((1, D_MODEL))],
        out_shape=[jax.ShapeDtypeStruct((S, D_MODEL), F32), jax.ShapeDtypeStruct((1, D_MODEL), F32)],
        compiler_params=_params(("arbitrary",)),
    )(*dps, w_in_full, x2, norm_w, dh)


def _inproj_bwd_w(ut, dps):
    S = ut.shape[1]
    tm = 256

    def body(ut_ref, d0, d1, d2, d3, d4, d5, d6, d7, g_ref):
        @pl.when(pl.program_id(0) == 0)
        def _():
            g_ref[...] = jnp.zeros_like(g_ref)

        utb = ut_ref[...]
        for i, dref in enumerate((d0, d1, d2, d3, d4, d5, d6, d7)):
            j, half = divmod(i, 2)
            g_ref[j, :, 512 * half: 512 * (half + 1)] += _dot(utb, dref[...])

    return pl.pallas_call(
        body, name="inproj_bwd_w", grid=(S // tm,),
        in_specs=[pl.BlockSpec((D_MODEL, tm), lambda s: (0, s))] + [pl.BlockSpec((tm, 512), lambda s: (s, 0))] * 8,
        out_specs=_full((4, D_MODEL, 1024)),
        out_shape=jax.ShapeDtypeStruct((4, D_MODEL, 1024), F32),
        compiler_params=_params(("arbitrary",)),
    )(ut, *dps)


def _adamw_update(gg, w_ref, m_ref, v_ref, d_ref, nm_ref, nv_ref):
    nm = ADAM_B1 * m_ref[...] + (1.0 - ADAM_B1) * gg
    nv = ADAM_B2 * v_ref[...] + (1.0 - ADAM_B2) * (gg * gg)
    m_hat = nm / (1.0 - ADAM_B1 ** ADAM_STEP)
    v_hat = nv / (1.0 - ADAM_B2 ** ADAM_STEP)
    d_ref[...] = -ADAM_LR * (m_hat / (jnp.sqrt(v_hat) + ADAM_EPS) + ADAM_WD * w_ref[...])
    nm_ref[...] = nm
    nv_ref[...] = nv


def _adamw(w, g, m, v, name):
    rows, cols = w.shape
    tr = min(rows, 256)

    def body(w_ref, g_ref, m_ref, v_ref, d_ref, nm_ref, nv_ref):
        _adamw_update(g_ref[...], w_ref, m_ref, v_ref, d_ref, nm_ref, nv_ref)

    spec = pl.BlockSpec((tr, cols), lambda i: (i, 0))
    sds = jax.ShapeDtypeStruct((rows, cols), F32)
    return pl.pallas_call(
        body, name=name, grid=(rows // tr,),
        in_specs=[spec] * 4, out_specs=[spec] * 3, out_shape=[sds] * 3,
        compiler_params=_params(("parallel",)),
    )(w, g, m, v)


def _place():
    return lax.axis_index("x"), lax.axis_index("y"), lax.axis_index("c")


def _gather_weights(w_in_s, w_out_s):
    def body(win_ref, wout_ref, fin_ref, fout_ref, send_sems, recv_sems):
        x, y, c = _place()
        me = (x, y, c)
        sib = (x, y, 1 - c)
        mine = 2 * x + y
        fin_ref[mine] = win_ref[...].astype(BF16)
        fout_ref[mine] = wout_ref[...].astype(BF16)
        chips = [(1 - x, y), (x, 1 - y), (1 - x, 1 - y)]

        def halves(chip, half):
            return (fin_ref.at[chip, pl.ds(half * 512, 512), :], fout_ref.at[chip, pl.ds(half * 128, 128), :])

        def copy(k, ref, to):
            return pltpu.make_async_remote_copy(src_ref=ref, dst_ref=ref, send_sem=send_sems.at[k],
                                                recv_sem=recv_sems.at[k], device_id=to, device_id_type=MESH)

        first, passed = [], []
        for j, (cx, cy) in enumerate(chips):
            for a, ref in enumerate(halves(mine, c)):
                first.append(copy(2 * j + a, ref, (cx, cy, c)))
        for cp in first:
            cp.start()
        for j, (cx, cy) in enumerate(chips):
            for a, ref in enumerate(halves(2 * cx + cy, c)):
                copy(2 * j + a, ref, me).wait_recv()
                fwd = copy(6 + 2 * j + a, ref, sib)
                fwd.start()
                passed.append(fwd)
        for j, (cx, cy) in enumerate(chips):
            for a, ref in enumerate(halves(2 * cx + cy, 1 - c)):
                copy(6 + 2 * j + a, ref, me).wait_recv()
        for cp in first + passed:
            cp.wait_send()

    vm = pl.BlockSpec(memory_space=pltpu.VMEM)
    return pl.pallas_call(
        body, name="gather_weights",
        in_specs=[vm, vm], out_specs=[vm, vm],
        out_shape=[jax.ShapeDtypeStruct((4, D_MODEL, 1024), BF16), jax.ShapeDtypeStruct((4, 256, D_MODEL), BF16)],
        scratch_shapes=[pltpu.SemaphoreType.DMA((12,)), pltpu.SemaphoreType.DMA((12,))],
        compiler_params=pltpu.CompilerParams(vmem_limit_bytes=VMEM_LIMIT),
    )(w_in_s, w_out_s)


def _swap_halves(g_in, g_out):
    def body(gin_ref, gout_ref, rin_ref, rout_ref, send_sems, recv_sems):
        x, y, c = _place()
        sib = (x, y, 1 - c)
        cps = [pltpu.make_async_remote_copy(src_ref=src.at[:, 1 - c], dst_ref=dst, send_sem=send_sems.at[k],
                                            recv_sem=recv_sems.at[k], device_id=sib, device_id_type=MESH)
               for k, (src, dst) in enumerate(((gin_ref, rin_ref), (gout_ref, rout_ref)))]
        for cp in cps:
            cp.start()
        for cp in cps:
            cp.wait()

    hbm = pl.BlockSpec(memory_space=pl.ANY)
    return pl.pallas_call(
        body, name="swap_halves",
        in_specs=[hbm, hbm], out_specs=[hbm, hbm],
        out_shape=[jax.ShapeDtypeStruct((4,) + g.shape[2:], F32) for g in (g_in, g_out)],
        scratch_shapes=[pltpu.SemaphoreType.DMA((2,)), pltpu.SemaphoreType.DMA((2,))],
    )(g_in, g_out)


def _add_half(g, r, cidx, name):
    n, _, rows, cols = g.shape

    def body(c_ref, g_ref, r_ref, o_ref):
        o_ref[0] = (g_ref[0, 0] + r_ref[0]).astype(BF16)

    return pl.pallas_call(
        body, name=name,
        grid_spec=pltpu.PrefetchScalarGridSpec(
            num_scalar_prefetch=1, grid=(n,),
            in_specs=[pl.BlockSpec((1, 1, rows, cols), lambda j, c_ref: (j, c_ref[0], 0, 0)),
                      pl.BlockSpec((1, rows, cols), lambda j, c_ref: (j, 0, 0))],
            out_specs=pl.BlockSpec((1, rows, cols), lambda j, c_ref: (j, 0, 0))),
        out_shape=jax.ShapeDtypeStruct((n, rows, cols), BF16),
        compiler_params=_params(("parallel",)),
    )(cidx, g, r)


def _exchange_chips(cs_in, cs_out):
    def body(in_ref, out_ref, rin_ref, rout_ref, send_sems, recv_sems):
        x, y, c = _place()
        chips = [(1 - x, y), (x, 1 - y), (1 - x, 1 - y)]
        cps = []
        for k, (cx, cy) in enumerate(chips):
            for a, (src, dst) in enumerate(((in_ref, rin_ref), (out_ref, rout_ref))):
                cps.append(pltpu.make_async_remote_copy(
                    src_ref=src.at[2 * cx + cy], dst_ref=dst.at[k], send_sem=send_sems.at[2 * k + a],
                    recv_sem=recv_sems.at[2 * k + a], device_id=(cx, cy, c), device_id_type=MESH))
        for cp in cps:
            cp.start()
        for cp in cps:
            cp.wait()

    hbm = pl.BlockSpec(memory_space=pl.ANY)
    return pl.pallas_call(
        body, name="exchange_chips",
        in_specs=[hbm, hbm], out_specs=[hbm, hbm],
        out_shape=[jax.ShapeDtypeStruct((3,) + a.shape[1:], a.dtype) for a in (cs_in, cs_out)],
        scratch_shapes=[pltpu.SemaphoreType.DMA((6,)), pltpu.SemaphoreType.DMA((6,))],
    )(cs_in, cs_out)


PEER_XOR = (2, 1, 3)


def _sum_chips(cs, r, chip_idx, name):
    _, rows, cols = r.shape
    tr = min(rows, 256)

    def body(m_ref, cs_ref, r_ref, o_ref):
        mine = m_ref[0]
        own = cs_ref[0].astype(F32)
        got = [r_ref[k].astype(F32) for k in range(3)]
        acc = None
        for s in range(4):
            rel = mine ^ s
            term = jnp.where(rel == 0, own, jnp.where(rel == PEER_XOR[0], got[0],
                                                      jnp.where(rel == PEER_XOR[1], got[1], got[2])))
            acc = term if acc is None else acc + term
        o_ref[...] = acc

    return pl.pallas_call(
        body, name=name,
        grid_spec=pltpu.PrefetchScalarGridSpec(
            num_scalar_prefetch=1, grid=(rows // tr,),
            in_specs=[pl.BlockSpec((1, tr, cols), lambda i, m_ref: (m_ref[0], i, 0)),
                      pl.BlockSpec((3, tr, cols), lambda i, m_ref: (0, i, 0))],
            out_specs=pl.BlockSpec((tr, cols), lambda i, m_ref: (i, 0))),
        out_shape=jax.ShapeDtypeStruct((rows, cols), F32),
        compiler_params=_params(("parallel",)),
    )(chip_idx, cs, r)


def _swap_reduced(h_in, h_out):
    def body(in_ref, out_ref, rin_ref, rout_ref, send_sems, recv_sems):
        x, y, c = _place()
        cps = [pltpu.make_async_remote_copy(src_ref=src, dst_ref=dst, send_sem=send_sems.at[k],
                                            recv_sem=recv_sems.at[k], device_id=(x, y, 1 - c), device_id_type=MESH)
               for k, (src, dst) in enumerate(((in_ref, rin_ref), (out_ref, rout_ref)))]
        for cp in cps:
            cp.start()
        for cp in cps:
            cp.wait()

    hbm = pl.BlockSpec(memory_space=pl.ANY)
    return pl.pallas_call(
        body, name="swap_reduced",
        in_specs=[hbm, hbm], out_specs=[hbm, hbm],
        out_shape=[jax.ShapeDtypeStruct(h.shape, F32) for h in (h_in, h_out)],
        scratch_shapes=[pltpu.SemaphoreType.DMA((2,)), pltpu.SemaphoreType.DMA((2,))],
    )(h_in, h_out)


def _adamw_halves(w, mine, theirs, m, v, cidx, name):
    rows, cols = w.shape
    half = rows // 2
    tr = min(half, 256)
    nbh = half // tr

    def body(c_ref, w_ref, a_ref, b_ref, m_ref, v_ref, g_ref, d_ref, nm_ref, nv_ref):
        gg = jnp.where(pl.program_id(0) // nbh == c_ref[0], a_ref[...], b_ref[...])
        g_ref[...] = gg
        _adamw_update(gg, w_ref, m_ref, v_ref, d_ref, nm_ref, nv_ref)

    spec = pl.BlockSpec((tr, cols), lambda i, c_ref: (i, 0))
    hspec = pl.BlockSpec((tr, cols), lambda i, c_ref: (i % nbh, 0))
    sds = jax.ShapeDtypeStruct((rows, cols), F32)
    return pl.pallas_call(
        body, name=name,
        grid_spec=pltpu.PrefetchScalarGridSpec(
            num_scalar_prefetch=1, grid=(rows // tr,),
            in_specs=[spec, hspec, hspec, spec, spec], out_specs=[spec] * 4),
        out_shape=[sds] * 4,
        compiler_params=_params(("parallel",)),
    )(cidx, w, mine, theirs, m, v)


def _allreduce_small(g_nw, g_fw, g_hgw, g_lbl, loss8):
    def body(nw_ref, fw_ref, hgw_ref, lbl_ref, loss_ref, out_ref, slots, send_sems, recv_sems):
        x, y, c = _place()
        me = 4 * x + 2 * y + c
        slots[me] = jnp.zeros((8, D_MODEL), F32)
        slots[me, 0:1, :] = nw_ref[...]
        slots[me, 1:2, :] = fw_ref[...]
        slots[me, 2:3, 0:HG_W] = hgw_ref[...]
        slots[me, 3:4, 0:HG_W] = lbl_ref[0:1, :]
        slots[me, 3:4, HG_W:] = lbl_ref[1:2, :]
        slots[me, 4:5, 0:128] = loss_ref[0:1, :]
        cps = []
        for k in range(1, 8):
            dx, dy, dc = (k >> 2) & 1, (k >> 1) & 1, k & 1
            to = (x ^ dx, y ^ dy, c ^ dc)
            cps.append(pltpu.make_async_remote_copy(
                src_ref=slots.at[me], dst_ref=slots.at[me], send_sem=send_sems.at[k - 1],
                recv_sem=recv_sems.at[k - 1], device_id=to, device_id_type=MESH))
        for cp in cps:
            cp.start()
        for cp in cps:
            cp.wait()
        acc = slots[0]
        for i in range(1, 8):
            acc = acc + slots[i]
        out_ref[...] = acc

    vm = pl.BlockSpec(memory_space=pltpu.VMEM)
    return pl.pallas_call(
        body, name="allreduce_small",
        in_specs=[vm] * 5, out_specs=vm,
        out_shape=jax.ShapeDtypeStruct((8, D_MODEL), F32),
        scratch_shapes=[pltpu.VMEM((8, 8, D_MODEL), F32), pltpu.SemaphoreType.DMA((7,)), pltpu.SemaphoreType.DMA((7,))],
    )(g_nw, g_fw, g_hgw, g_lbl, loss8)


def _rope_tables(S):
    inv_freq = 1.0 / (ROPE_THETA ** (jnp.arange(ROPE_HALF, dtype=F32) / ROPE_HALF))
    ang = jnp.arange(S, dtype=jnp.int32).astype(F32)[:, None] * inv_freq[None, :]
    cos, sin = jnp.cos(ang), jnp.sin(ang)
    cos2 = jnp.concatenate([cos, cos, cos, cos], axis=-1)
    sin2 = jnp.concatenate([-sin, sin, -sin, sin], axis=-1)
    return cos2, sin2


def _local_step(x2, tgt2, norm_w, w_in_full, lbl, hg_norm_w, w_out_full, fnw):
    S = x2.shape[0]
    cos2, sin2 = _rope_tables(S)
    p0, p1, qkv, qkv4, qkv16, p3, ut = _inproj(x2, norm_w, w_in_full, cos2, sin2)
    o_hg, mix_hg, mixt_hg, states = _hg_fwd(p0, p1, lbl, hg_norm_w)
    qkv_ds = [qkv.reshape(1, S, 3 * AT_W), qkv4, qkv16]
    ols = [_att_fwd(q) for q in qkv_ds]
    ol, mix_at, mixt_at = _att_combine(ols, p3)
    dh, dm_hg, dm_at, g_wout, g_fw, loss8 = _outproj(x2, tgt2, mix_hg, mix_at, mixt_hg, mixt_at, w_out_full, fnw)
    dqr, dfl, dv_hg, dz_hg, g_lbl, g_hgw = _hg_bwd(p0, p1, o_hg, dm_hg, states, lbl, hg_norm_w)
    aux, aux4, aux16, dz_at = _att_gate_bwd(dm_at, ol, p3)
    aux_ds = [aux.reshape(1, S, AUX_W), aux4, aux16]
    dqkvs = [_att_bwd(q, a) for q, a in zip(qkv_ds, aux_ds)]
    dq_at, dk_at, dv_at = _att_bwd_combine(dqkvs, cos2, sin2)
    dps = [dqr, dfl, dv_hg, dz_hg, dq_at, dk_at, dv_at, dz_at]
    grad_x, g_nw = _inproj_bwd_x(dps, w_in_full, x2, norm_w, dh)
    g_win = _inproj_bwd_w(ut, dps)
    return loss8, grad_x, g_nw, g_win, g_lbl, g_hgw, g_wout, g_fw


def kernel(x, norm_w, w_in, hgrn_lb_logits, hg_norm_w, w_out, final_norm_w, loss_target, m_norm_w, m_w_in, m_hgrn_lb_logits, m_hg_norm_w, m_w_out, m_final_norm_w, v_norm_w, v_w_in, v_hgrn_lb_logits, v_hg_norm_w, v_w_out, v_final_norm_w):
    S = x.shape[1]
    w_in_full, w_out_full = _gather_weights(w_in[0], w_out[0])
    loss8, grad_x, g_nw, g_win, g_lbl, g_hgw, g_wout, g_fw = _local_step(
        x[0], loss_target[0], norm_w, w_in_full, hgrn_lb_logits, hg_norm_w,
        w_out_full.reshape(D_MODEL, D_MODEL), final_norm_w.reshape(1, D_MODEL))

    cidx = lax.axis_index("c").astype(jnp.int32).reshape(1)
    g_in4 = g_win.reshape(4, 2, 512, 1024)
    g_out4 = g_wout.reshape(4, 2, 128, D_MODEL)
    r_in, r_out = _swap_halves(g_in4, g_out4)
    cs_in = _add_half(g_in4, r_in, cidx, "add_half_in")
    cs_out = _add_half(g_out4, r_out, cidx, "add_half_out")
    x_in, x_out = _exchange_chips(cs_in, cs_out)
    chip_idx = (2 * lax.axis_index("x") + lax.axis_index("y")).astype(jnp.int32).reshape(1)
    h_in = _sum_chips(cs_in, x_in, chip_idx, "sum_chips_in")
    h_out = _sum_chips(cs_out, x_out, chip_idx, "sum_chips_out")
    s_in, s_out = _swap_reduced(h_in, h_out)

    red = _allreduce_small(g_nw, g_fw, g_hgw, g_lbl, loss8)
    loss = red[4, 0]
    grad_norm_w = red[0:1, :]
    grad_final_norm_w = red[1, :]
    grad_hg_norm_w = red[2:3, :HG_W]
    grad_lbl = jnp.concatenate([red[3:4, :HG_W], red[3:4, HG_W:]], axis=0)

    d_nw, m_nw, v_nw = _adamw(norm_w, grad_norm_w, m_norm_w, v_norm_w, "adamw_norm_w")
    grad_w_in, d_win, m_win, v_win = _adamw_halves(w_in[0], h_in, s_in, m_w_in[0], v_w_in[0], cidx, "adamw_w_in")
    d_lbl, m_lbl, v_lbl = _adamw(hgrn_lb_logits, grad_lbl, m_hgrn_lb_logits, v_hgrn_lb_logits, "adamw_lb_logits")
    d_hgw, m_hgw, v_hgw = _adamw(hg_norm_w, grad_hg_norm_w, m_hg_norm_w, v_hg_norm_w, "adamw_hg_norm_w")
    grad_w_out, d_wout, m_wout, v_wout = _adamw_halves(w_out[0], h_out, s_out, m_w_out[0], v_w_out[0], cidx, "adamw_w_out")
    d_fw, m_fw, v_fw = _adamw(final_norm_w.reshape(1, D_MODEL), grad_final_norm_w.reshape(1, D_MODEL),
                              m_final_norm_w.reshape(1, D_MODEL), v_final_norm_w.reshape(1, D_MODEL), "adamw_final_norm_w")
    e1 = lambda a: a[None]
    flat = lambda a: a.reshape(D_MODEL)
    return (loss, grad_x[None], grad_norm_w, e1(grad_w_in), grad_lbl, grad_hg_norm_w, e1(grad_w_out), grad_final_norm_w,
            d_nw, e1(d_win), d_lbl, d_hgw, e1(d_wout), flat(d_fw),
            m_nw, e1(m_win), m_lbl, m_hgw, e1(m_wout), flat(m_fw),
            v_nw, e1(v_win), v_lbl, v_hgw, e1(v_wout), flat(v_fw))
```

```python
import functools

import jax
import jax.numpy as jnp
import numpy as np
from jax import lax
from jax.experimental import pallas as pl
from jax.experimental.pallas import tpu as pltpu

F32 = jnp.float32
BF16 = jnp.bfloat16
MESH = pl.DeviceIdType.MESH

D_MODEL = 1024
HG_W = 512
AT_W = 512
HEAD_PAIR = 128
ROPE_HALF = 32
ROPE_THETA = 10000.0
EPS = 1e-6
CHUNK = 128
LEVELS = (64, 32, 16, 8, 4, 2, 1)
DIAG = 1
SUBLANES = 8
ATT_BLK = 128
DILATIONS = (1, 4, 16)
ATT_SCALE = 0.125
NEG = -1e30
VMEM_LIMIT = 56 * 1024 * 1024

ADAM_LR = 0.001
ADAM_B1 = 0.9
ADAM_B2 = 0.999
ADAM_EPS = 1e-08
ADAM_WD = 0.01
ADAM_STEP = 10


def _iota(shape, dim):
    return lax.broadcasted_iota(jnp.int32, shape, dim)


def _dot(a, b):
    return jnp.dot(a, b, preferred_element_type=F32)


def _dot_nt(a, b):
    return lax.dot_general(a, b, (((1,), (1,)), ((), ())), preferred_element_type=F32)


def _dot_hi(a, b):
    return jnp.dot(a, b, preferred_element_type=F32, precision=lax.Precision.HIGHEST)


def _sig(v):
    return 1.0 / (1.0 + jnp.exp(-v))


def _params(sem=None, vmem=VMEM_LIMIT):
    return pltpu.CompilerParams(dimension_semantics=sem, vmem_limit_bytes=vmem)


def _full(shape):
    n = len(shape)
    return pl.BlockSpec(shape, lambda *_: (0,) * n)


def _rope_rot(y):
    n = y.shape[1]
    first = (_iota(y.shape, 1) & (2 * ROPE_HALF - 1)) < ROPE_HALF
    return jnp.where(first, pltpu.roll(y, n - ROPE_HALF, 1), pltpu.roll(y, ROPE_HALF, 1))


def _dil_spec(d, tm, width):
    return pl.BlockSpec((d, tm // d, width), lambda s: (0, s, 0))


LANES = 128


def _slab_scratch(tm, width):
    return pltpu.VMEM((width // LANES, tm, LANES), F32)


def _to_slabs(v, slabs_ref):
    for j in range(slabs_ref.shape[0]):
        slabs_ref[j] = v[:, LANES * j: LANES * (j + 1)]


def _from_slabs(slabs_ref):
    return jnp.concatenate([slabs_ref[j] for j in range(slabs_ref.shape[0])], axis=1)


def _split_residues(slabs_ref, dst_ref, d, dtype):
    nslab, tm, _ = slabs_ref.shape
    for r in range(d):
        for j in range(nslab):
            dst_ref[r, :, LANES * j: LANES * (j + 1)] = slabs_ref[j, pl.ds(r, tm // d, stride=d), :].astype(dtype)


def _merge_residues(src_ref, slabs_ref, d):
    nslab, tm, _ = slabs_ref.shape
    for r in range(d):
        for j in range(nslab):
            slabs_ref[j, pl.ds(r, tm // d, stride=d), :] = src_ref[r, :, LANES * j: LANES * (j + 1)]


def _inproj(x2, norm_w, w_in_full, cos2, sin2):
    S = x2.shape[0]
    tm = 256

    def body(x_ref, nw_ref, w_ref, cos_ref, sin_ref, p0_ref, p1_ref, qkv_ref, qkv4_ref, qkv16_ref, p3_ref, ut_ref, scr):
        x = x_ref[...]
        r = lax.rsqrt(jnp.mean(x * x, axis=-1, keepdims=True) + EPS)
        u = x * r * nw_ref[...]
        ub = u.astype(BF16)
        ut_ref[...] = u.T.astype(BF16)
        p0_ref[...] = _dot(ub, w_ref[0])
        p1_ref[...] = _dot(ub, w_ref[1])
        y2 = _dot(ub, w_ref[2])
        cosf = jnp.tile(cos_ref[...], (1, 8))
        sinf = jnp.tile(sin_ref[...], (1, 8))
        y3 = _dot(ub, w_ref[3])
        p3_ref[...] = y3
        qkv = jnp.concatenate([y2 * cosf + _rope_rot(y2) * sinf, y3[:, :AT_W]], axis=1)
        qkv_ref[...] = qkv.astype(BF16)
        _to_slabs(qkv, scr)
        _split_residues(scr, qkv4_ref, 4, BF16)
        _split_residues(scr, qkv16_ref, 16, BF16)

    row = lambda w: pl.BlockSpec((tm, w), lambda s: (s, 0))
    qkv_w = 3 * AT_W
    return pl.pallas_call(
        body, name="inproj", grid=(S // tm,),
        in_specs=[row(D_MODEL), _full((1, D_MODEL)), _full((4, D_MODEL, 1024)), row(128), row(128)],
        out_specs=[row(1024), row(1024), row(qkv_w), _dil_spec(4, tm, qkv_w), _dil_spec(16, tm, qkv_w), row(1024),
                   pl.BlockSpec((D_MODEL, tm), lambda s: (0, s))],
        out_shape=[jax.ShapeDtypeStruct((S, 1024), F32), jax.ShapeDtypeStruct((S, 1024), F32),
                   jax.ShapeDtypeStruct((S, qkv_w), BF16), jax.ShapeDtypeStruct((4, S // 4, qkv_w), BF16),
                   jax.ShapeDtypeStruct((16, S // 16, qkv_w), BF16), jax.ShapeDtypeStruct((S, 1024), F32),
                   jax.ShapeDtypeStruct((D_MODEL, S), BF16)],
        scratch_shapes=[_slab_scratch(tm, qkv_w)],
        compiler_params=_params(("parallel",)),
    )(x2, norm_w, w_in_full, cos2, sin2)


HG_HPS = 4
N_LEV = len(LEVELS)


def _hg_const_arrays():
    r = np.arange(CHUNK)[:, None]
    c = np.arange(CHUNK)[None, :]
    tris = np.stack([r >= c, r <= c])
    lm = [((r // (2 * m)) == (c // (2 * m))) & (r % (2 * m) >= m) & (c % (2 * m) < m) for m in LEVELS]
    dm = [(c == r - dl) & (r % DIAG >= dl) for dl in range(DIAG)]
    masks = np.stack(lm + [x.T for x in lm] + dm)
    return jnp.asarray(tris, BF16), jnp.asarray(masks, F32)


def _split2(a):
    hi = a.astype(BF16)
    return hi, (a - hi.astype(F32)).astype(BF16)


def _dot3(a, b):
    ah, al = _split2(a)
    bh, bl = _split2(b)
    n = b.shape[1]
    p = _dot(ah, jnp.concatenate([bh, bl], axis=1))
    return (p[:, :n] + p[:, n:]) + _dot(al, bh)


def _tri_dot(tri, a):
    a1 = a.astype(BF16)
    r1 = a - a1.astype(F32)
    a2 = r1.astype(BF16)
    a3 = (r1 - a2.astype(F32)).astype(BF16)
    n = a.shape[1]
    p = _dot(tri, jnp.concatenate([a1, a2, a3], axis=1))
    return (p[:, :n] + p[:, n:2 * n]) + p[:, 2 * n:]


def _rowsum(t):
    return _dot(t.astype(BF16), jnp.ones((t.shape[1], t.shape[1]), BF16))


def _level_refs(b):
    refs = []
    pos = _iota(b.shape, 0)
    for m in LEVELS:
        if 2 * m >= SUBLANES:
            parts = [jnp.broadcast_to(b[r0 + m - 1: r0 + m, :], (2 * m, b.shape[1])) for r0 in range(0, CHUNK, 2 * m)]
            refs.append(parts[0] if len(parts) == 1 else jnp.concatenate(parts, axis=0))
        else:
            p = pos & (2 * m - 1)
            ref = b
            for off in range(-(m - 1), m + 1):
                if off != 0:
                    ref = jnp.where(p == m - 1 + off, pltpu.roll(b, off % CHUNK, 0), ref)
            refs.append(ref)
    return refs


def _hg_lb(lbl_ref):
    l0 = lbl_ref[0:1, :]
    l1 = lbl_ref[1:2, :]
    mx = jnp.maximum(l0, l1)
    e0 = jnp.exp(l0 - mx)
    e1 = jnp.exp(l1 - mx)
    p0 = e0 / (e0 + e1)
    lb = jnp.clip(p0, 1e-6, 1.0 - 1e-6)
    inside = (p0 >= 1e-6) & (p0 <= 1.0 - 1e-6)
    dlb_dl0 = jnp.where(inside, p0 * (e1 / (e0 + e1)), 0.0)
    return lb, dlb_dl0


def _sigmoid(v):
    return 0.5 * jnp.tanh(0.5 * v) + 0.5


def _hg_gates(qr, fl, lb):
    sig = _sigmoid(fl)
    f = lb + (1.0 - lb) * sig
    g = jnp.log(f)
    k = (1.0 - lb) * (1.0 - sig)
    sq = _sigmoid(qr)
    q = qr * sq
    return sig, f, g, k, sq, q


def _hg_levels(q, k, b, mk_ref):
    refs = _level_refs(b)
    a = jnp.zeros((CHUNK, CHUNK), F32)
    es, qts, kts = [], [], []
    for i in range(N_LEV):
        diff = b - refs[i]
        e = jnp.exp(jnp.minimum(diff, -diff))
        qt = (q * e).astype(BF16)
        kt = (k * e).astype(BF16)
        a = a + _dot_nt(qt, kt) * mk_ref[i]
        es.append(e); qts.append(qt); kts.append(kt)
    return a, es, qts, kts


def _hg_specs(nc, rev):
    cc = (lambda c: nc - 1 - c) if rev else (lambda c: c)
    w = 128 * HG_HPS
    blk = lambda off: pl.BlockSpec((CHUNK, w), lambda h, c: (cc(c), h + off))
    vec = pl.BlockSpec((1, w), lambda h, c: (0, h))
    lb2 = pl.BlockSpec((2, w), lambda h, c: (0, h))
    st = pl.BlockSpec((1, HG_HPS, 128, 128), lambda h, c: (cc(c), h, 0, 0))
    consts = [_full((2, CHUNK, CHUNK)), _full((2 * N_LEV + DIAG, CHUNK, CHUNK))]
    return blk, vec, lb2, st, consts


def _hg_fwd(p0, p1, lbl, gw):
    S = p0.shape[0]
    nc = S // CHUNK
    ng = 4 // HG_HPS

    def body(qr_ref, fl_ref, v_ref, z_ref, lbl_ref, gw_ref, tri_ref, mk_ref,
             o_ref, mix_ref, mixt_ref, st_ref, state):
        c = pl.program_id(1)

        @pl.when(c == 0)
        def _():
            state[...] = jnp.zeros_like(state)

        lb_all, _ = _hg_lb(lbl_ref)
        heads = [slice(128 * hh, 128 * (hh + 1)) for hh in range(HG_HPS)]
        qs, ks, bs, mats = [], [], [], []
        for sl in heads:
            _, _, g, k, _, q = _hg_gates(qr_ref[:, sl], fl_ref[:, sl], lb_all[:, sl])
            qs.append(q); ks.append(k); bs.append(_tri_dot(tri_ref[0], g))
        for hh in range(HG_HPS):
            a, _, _, _ = _hg_levels(qs[hh], ks[hh], bs[hh], mk_ref)
            mats.append(a + _rowsum(qs[hh] * ks[hh]) * mk_ref[2 * N_LEV])
        for hh, sl in enumerate(heads):
            q, k, b, a, v, z = qs[hh], ks[hh], bs[hh], mats[hh], v_ref[:, sl], z_ref[:, sl]
            b_last = b[CHUNK - 1: CHUNK, :]
            st = state[hh]
            st_ref[0, hh] = st
            o = _dot_nt((q * jnp.exp(b)).astype(BF16), st.astype(BF16)) + _dot(a.astype(BF16), v.astype(BF16))
            state[hh] = st * jnp.exp(b_last) + _dot3(v.T, k * jnp.exp(b_last - b))
            o_ref[:, sl] = o
            rs = lax.rsqrt(jnp.mean(o * o, axis=-1, keepdims=True) + EPS)
            mixed = o * rs * gw_ref[:, sl] * (z * _sigmoid(z))
            mix_ref[:, sl] = mixed.astype(BF16)
            mixt_ref[sl, :] = mixed.T.astype(BF16)

    blk, vec, lb2, st_spec, consts = _hg_specs(nc, False)
    tris, masks = _hg_const_arrays()
    return pl.pallas_call(
        body, name="hg_fwd", grid=(ng, nc),
        in_specs=[blk(0), blk(ng), blk(0), blk(ng), lb2, vec] + consts,
        out_specs=[blk(0), blk(0), pl.BlockSpec((128 * HG_HPS, CHUNK), lambda h, c: (h, c)), st_spec],
        out_shape=[jax.ShapeDtypeStruct((S, HG_W), F32), jax.ShapeDtypeStruct((S, HG_W), BF16),
                   jax.ShapeDtypeStruct((HG_W, S), BF16), jax.ShapeDtypeStruct((nc, 4, 128, 128), F32)],
        scratch_shapes=[pltpu.VMEM((HG_HPS, 128, 128), F32)],
        compiler_params=_params(("parallel", "arbitrary")),
    )(p0, p0, p1, p1, lbl, gw, tris, masks)


def _hg_bwd(p0, p1, o_raw, dm, states, lbl, gw):
    S = p0.shape[0]
    nc = S // CHUNK
    ng = 4 // HG_HPS
    w = 128 * HG_HPS

    def body(qr_ref, fl_ref, v_ref, z_ref, o_ref, dm_ref, st_ref, lbl_ref, gw_ref, tri_ref, mk_ref,
             dqr_ref, dfl_ref, dv_ref, dz_ref, glbl_ref, ggw_ref, dstate, carry, acc_lb, acc_gw):
        c = pl.program_id(1)

        @pl.when(c == 0)
        def _():
            dstate[...] = jnp.zeros_like(dstate)
            carry[...] = jnp.zeros_like(carry)
            acc_lb[...] = jnp.zeros_like(acc_lb)
            acc_gw[...] = jnp.zeros_like(acc_gw)

        lb_all, dlb_dl0 = _hg_lb(lbl_ref)
        heads = [slice(128 * hh, 128 * (hh + 1)) for hh in range(HG_HPS)]
        diag_mask = mk_ref[2 * N_LEV]
        gates, bs, lev, dos = [], [], [], []
        for sl in heads:
            gt = _hg_gates(qr_ref[:, sl], fl_ref[:, sl], lb_all[:, sl])
            gates.append(gt)
            bs.append(_tri_dot(tri_ref[0], gt[2]))
        for hh, sl in enumerate(heads):
            _, _, _, k, _, q = gates[hh]
            lev.append(_hg_levels(q, k, bs[hh], mk_ref))
            o, z, dmix, gwv = o_ref[:, sl], z_ref[:, sl], dm_ref[:, sl], gw_ref[:, sl]
            rs = lax.rsqrt(jnp.mean(o * o, axis=-1, keepdims=True) + EPS)
            oh = o * rs
            sz = _sigmoid(z)
            dz_ref[:, sl] = (dmix * (oh * gwv) * (sz * (1.0 + z * (1.0 - sz)))).astype(BF16)
            don = dmix * (z * sz)
            acc_gw[0:1, sl] += jnp.sum(don * oh, axis=0, keepdims=True)
            dy = don * gwv
            dos.append(rs * (dy - oh * jnp.mean(dy * oh, axis=-1, keepdims=True)))
        inter = []
        for hh, sl in enumerate(heads):
            _, _, _, k, _, q = gates[hh]
            b, do, v = bs[hh], dos[hh], v_ref[:, sl]
            b_last = b[CHUNK - 1: CHUNK, :]
            eb = jnp.exp(b)
            edec = jnp.exp(b_last - b)
            dst = dstate[hh]
            dq = _dot3(do, st_ref[0, hh]) * eb
            dk = _dot3(v, dst) * edec
            dob = do.astype(BF16)
            da = _dot_nt(dob, v.astype(BF16))
            a, _, _, _ = lev[hh]
            a = a + _rowsum(q * k) * diag_mask
            dv_ref[:, sl] = (_dot(a.T.astype(BF16), dob) + _dot_nt((k * edec).astype(BF16), dst.astype(BF16))).astype(BF16)
            dstate[hh] = dst * jnp.exp(b_last) + _dot3(do.T, q * eb)
            inter.append((dq, dk, da))
        for hh, sl in enumerate(heads):
            sig, f, _, k, sq, q = gates[hh]
            _, es, qts, kts = lev[hh]
            dq, dk, da = inter[hh]
            db = q * dq - k * dk
            da_t = da.T
            for i in range(N_LEV):
                gq = _dot((da * mk_ref[i]).astype(BF16), kts[i])
                gk = _dot((da_t * mk_ref[N_LEV + i]).astype(BF16), qts[i])
                dq = dq + es[i] * gq
                dk = dk + es[i] * gk
                db = db + (qts[i].astype(F32) * gq - kts[i].astype(F32) * gk)
            dc = _rowsum(da * diag_mask)
            dq = dq + dc * k
            dk = dk + dc * q
            dg = _tri_dot(tri_ref[1], db) + carry[0:1, sl]
            carry[0:1, sl] += jnp.sum(db, axis=0, keepdims=True)
            lb = lb_all[:, sl]
            qr = qr_ref[:, sl]
            t = dg / f - dk
            dfl_ref[:, sl] = (t * (1.0 - lb) * sig * (1.0 - sig)).astype(BF16)
            acc_lb[0:1, sl] += jnp.sum(t * (1.0 - sig), axis=0, keepdims=True)
            dqr_ref[:, sl] = (dq * (sq * (1.0 + qr * (1.0 - sq)))).astype(BF16)

        @pl.when(c == nc - 1)
        def _():
            gl0 = acc_lb[0:1, :] * dlb_dl0
            glbl_ref[0:1, :] = gl0
            glbl_ref[1:2, :] = -gl0
            ggw_ref[...] = acc_gw[0:1, :]

    blk, vec, lb2, st_spec, consts = _hg_specs(nc, True)
    tris, masks = _hg_const_arrays()
    act = jax.ShapeDtypeStruct((S, HG_W), BF16)
    return pl.pallas_call(
        body, name="hg_bwd", grid=(ng, nc),
        in_specs=[blk(0), blk(ng), blk(0), blk(ng), blk(0), blk(0), st_spec, lb2, vec] + consts,
        out_specs=[blk(0), blk(0), blk(0), blk(0), lb2, vec],
        out_shape=[act, act, act, act, jax.ShapeDtypeStruct((2, HG_W), F32), jax.ShapeDtypeStruct((1, HG_W), F32)],
        scratch_shapes=[pltpu.VMEM((HG_HPS, 128, 128), F32), pltpu.VMEM((8, w), F32),
                        pltpu.VMEM((8, w), F32), pltpu.VMEM((8, w), F32)],
        compiler_params=_params(("parallel", "arbitrary")),
    )(p0, p0, p1, p1, o_raw, dm, states, lbl, gw, tris, masks)


def _att_bias():
    qi = np.arange(ATT_BLK)[:, None]
    kj = np.arange(2 * ATT_BLK)[None, :]
    band = (kj >= qi) & (kj <= qi + ATT_BLK)
    qm = np.stack([band & (kj >= ATT_BLK), band])
    cur = (kj < ATT_BLK) & (qi <= kj)
    km = np.stack([cur, cur | ((kj >= ATT_BLK) & (qi >= kj - ATT_BLK))])
    to_bias = lambda m: jnp.asarray(np.where(m, 0.0, NEG), F32)
    return to_bias(qm), to_bias(km)


def _att_fwd(qkv_d):
    d, L, _ = qkv_d.shape
    nb = L // ATT_BLK

    def body(q_ref, kp_ref, kc_ref, vp_ref, vc_ref, bias_ref, ol_ref, s_scr, p_scr):
        bias = bias_ref[jnp.minimum(pl.program_id(1), 1)]
        bias2 = jnp.concatenate([bias, bias], axis=0)
        first = _iota((ATT_BLK, HEAD_PAIR), 1) < 64
        pairs = [slice(HEAD_PAIR * hp, HEAD_PAIR * (hp + 1)) for hp in range(4)]
        for hp, sl in enumerate(pairs):
            q2 = q_ref[0, :, sl] * ATT_SCALE
            zero = jnp.zeros_like(q2)
            kcat = jnp.concatenate([kp_ref[0, :, sl], kc_ref[0, :, sl]], axis=0)
            qs = jnp.concatenate([jnp.where(first, q2, zero), jnp.where(first, zero, q2)], axis=0)
            s_scr[hp] = _dot_nt(qs, kcat)
        stats = []
        for hp in range(4):
            s = s_scr[hp] + bias2
            m = jnp.max(s, axis=-1, keepdims=True)
            p = jnp.exp(s - m)
            l = jnp.sum(p, axis=-1, keepdims=True)
            p_scr[hp] = p.astype(BF16)
            stats.append((l, m + jnp.log(l)))
        for hp, sl in enumerate(pairs):
            l, lse = stats[hp]
            vcat = jnp.concatenate([vp_ref[0, :, sl], vc_ref[0, :, sl]], axis=0)
            o = _dot(p_scr[hp], vcat) / l
            ol_ref[0, :, sl] = jnp.where(first, o[:ATT_BLK], o[ATT_BLK:])
            ol_ref[0, :, AT_W + HEAD_PAIR * hp: AT_W + HEAD_PAIR * (hp + 1)] = jnp.where(first, lse[:ATT_BLK], lse[ATT_BLK:])

    cur = lambda j: pl.BlockSpec((1, ATT_BLK, AT_W), lambda r, n: (r, n, j))
    prev = lambda j: pl.BlockSpec((1, ATT_BLK, AT_W), lambda r, n: (r, jnp.maximum(n - 1, 0), j))
    return pl.pallas_call(
        body, name=f"att_fwd_d{d}", grid=(d, nb),
        in_specs=[cur(0), prev(1), cur(1), prev(2), cur(2), _full((2, ATT_BLK, 2 * ATT_BLK))],
        out_specs=pl.BlockSpec((1, ATT_BLK, 2 * AT_W), lambda r, n: (r, n, 0)),
        out_shape=jax.ShapeDtypeStruct((d, L, 2 * AT_W), F32),
        scratch_shapes=[pltpu.VMEM((4, 2 * ATT_BLK, 2 * ATT_BLK), F32), pltpu.VMEM((4, 2 * ATT_BLK, 2 * ATT_BLK), BF16)],
        compiler_params=_params(("parallel", "parallel")),
    )(qkv_d, qkv_d, qkv_d, qkv_d, qkv_d, _att_bias()[0])


def _att_combine(ols, p3):
    S = p3.shape[0]
    tm = 512

    def body(a_ref, b4_ref, c16_ref, z_ref, ol_ref, mix_ref, mixt_ref, b_scr, c_scr):
        _merge_residues(b4_ref, b_scr, 4)
        _merge_residues(c16_ref, c_scr, 16)
        pats = (a_ref[...], _from_slabs(b_scr), _from_slabs(c_scr))
        os_ = [v[:, :AT_W] for v in pats]
        ls = [v[:, AT_W:] for v in pats]
        mx = jnp.maximum(jnp.maximum(ls[0], ls[1]), ls[2])
        es = [jnp.exp(l - mx) for l in ls]
        zs = es[0] + es[1] + es[2]
        o = (es[0] * os_[0] + es[1] * os_[1] + es[2] * os_[2]) / zs
        ol_ref[:, :AT_W] = o
        ol_ref[:, AT_W:] = mx + jnp.log(zs)
        z = z_ref[...]
        mixed = o * (z * _sig(z))
        mix_ref[...] = mixed.astype(BF16)
        mixt_ref[...] = mixed.T.astype(BF16)

    row = lambda w: pl.BlockSpec((tm, w), lambda s: (s, 0))
    return pl.pallas_call(
        body, name="att_combine", grid=(S // tm,),
        in_specs=[row(1024), _dil_spec(4, tm, 1024), _dil_spec(16, tm, 1024), pl.BlockSpec((tm, AT_W), lambda s: (s, 1))],
        out_specs=[row(1024), row(AT_W), pl.BlockSpec((AT_W, tm), lambda s: (0, s))],
        out_shape=[jax.ShapeDtypeStruct((S, 1024), F32), jax.ShapeDtypeStruct((S, AT_W), BF16),
                   jax.ShapeDtypeStruct((AT_W, S), BF16)],
        scratch_shapes=[_slab_scratch(tm, 1024), _slab_scratch(tm, 1024)],
        compiler_params=_params(("parallel",)),
    )(ols[0].reshape(S, 1024), ols[1], ols[2], p3)


AUX_W = 2 * AT_W
AUX_LSE_LANE = 32


def _att_gate_bwd(dm_at, ol, p3):
    S = p3.shape[0]
    tm = 512

    def body(dm_ref, ol_ref, z_ref, aux_ref, aux4_ref, aux16_ref, dz_ref, scr):
        o = ol_ref[:, :AT_W]
        z = z_ref[...]
        dm = dm_ref[...]
        sz = _sig(z)
        dz_ref[...] = (dm * o * (sz * (1.0 + z * (1.0 - sz)))).astype(BF16)
        do = dm * (z * sz)
        aux_ref[:, :AT_W] = do
        ones = ((_iota((HEAD_PAIR, HEAD_PAIR), 0) < 64) == (_iota((HEAD_PAIR, HEAD_PAIR), 1) < 64)).astype(F32)
        d_lanes = (_iota((tm, HEAD_PAIR), 1) & 63) < AUX_LSE_LANE
        prod = do * o
        for hp in range(4):
            sl = slice(HEAD_PAIR * hp, HEAD_PAIR * (hp + 1))
            lse = ol_ref[:, AT_W + HEAD_PAIR * hp: AT_W + HEAD_PAIR * (hp + 1)]
            aux_ref[:, AT_W + HEAD_PAIR * hp: AT_W + HEAD_PAIR * (hp + 1)] = jnp.where(d_lanes, _dot_hi(prod[:, sl], ones), lse)
        _to_slabs(aux_ref[...], scr)
        _split_residues(scr, aux4_ref, 4, F32)
        _split_residues(scr, aux16_ref, 16, F32)

    row = lambda w: pl.BlockSpec((tm, w), lambda s: (s, 0))
    aw = AUX_W
    return pl.pallas_call(
        body, name="att_gate_bwd", grid=(S // tm,),
        in_specs=[row(AT_W), row(1024), pl.BlockSpec((tm, AT_W), lambda s: (s, 1))],
        out_specs=[row(aw), _dil_spec(4, tm, aw), _dil_spec(16, tm, aw), row(AT_W)],
        out_shape=[jax.ShapeDtypeStruct((S, aw), F32), jax.ShapeDtypeStruct((4, S // 4, aw), F32),
                   jax.ShapeDtypeStruct((16, S // 16, aw), F32), jax.ShapeDtypeStruct((S, AT_W), BF16)],
        scratch_shapes=[_slab_scratch(tm, aw)],
        compiler_params=_params(("parallel",)),
    )(dm_at, ol, p3)


def _att_bwd(qkv_d, aux_d):
    d, L, _ = qkv_d.shape
    nb = L // ATT_BLK

    def body(qc_ref, qn_ref, kp_ref, kc_ref, vp_ref, vc_ref, ac_ref, an_ref, bq_ref, bk_ref, dqkv_ref,
             s_scr, dp_scr, st_scr, dpt_scr, ds_scr, pt_scr, dst_scr):
        n = pl.program_id(1)
        bias = bq_ref[jnp.minimum(n, 1)]
        bias_t = bk_ref[jnp.minimum(nb - 1 - n, 1)]
        bias2 = jnp.concatenate([bias, bias], axis=0)
        bias_t2 = jnp.concatenate([bias_t, bias_t], axis=0)
        first = _iota((ATT_BLK, HEAD_PAIR), 1) < 64
        pairs = [slice(HEAD_PAIR * hp, HEAD_PAIR * (hp + 1)) for hp in range(4)]

        def stack(t):
            zero = jnp.zeros_like(t)
            return jnp.concatenate([jnp.where(first, t, zero), jnp.where(first, zero, t)], axis=0)

        def unstack(t2):
            return jnp.where(first, t2[:ATT_BLK], t2[ATT_BLK:])

        def operands(sl):
            q2, k2, v2 = qc_ref[0, :, sl], kc_ref[0, :, sl], vc_ref[0, :, sl]
            kcat = jnp.concatenate([kp_ref[0, :, sl], k2], axis=0)
            vcat = jnp.concatenate([vp_ref[0, :, sl], v2], axis=0)
            qcat = jnp.concatenate([q2, qn_ref[0, :, sl]], axis=0)
            do_c = ac_ref[0, :, sl]
            docat = jnp.concatenate([do_c, an_ref[0, :, sl]], axis=0).astype(BF16)
            return q2, k2, v2, kcat, vcat, qcat, do_c, docat

        for hp, sl in enumerate(pairs):
            q2, k2, v2, kcat, vcat, qcat, do_c, docat = operands(sl)
            s_scr[hp] = _dot_nt(stack(q2 * ATT_SCALE), kcat)
            dp_scr[hp] = _dot_nt(stack(do_c.astype(BF16)), vcat)
            st_scr[hp] = _dot_nt(stack(k2 * ATT_SCALE), qcat)
            dpt_scr[hp] = _dot_nt(stack(v2), docat)
        for hp in range(4):
            sl_a = slice(AT_W + HEAD_PAIR * hp, AT_W + HEAD_PAIR * (hp + 1))
            dl_c = ac_ref[0, :, sl_a]
            col = lambda t, o: jnp.concatenate([t[:, o: o + 1], t[:, 64 + o: 65 + o]], axis=0)
            p = jnp.exp(s_scr[hp] + bias2 - col(dl_c, AUX_LSE_LANE))
            ds_scr[hp] = (p * (dp_scr[hp] - col(dl_c, 0))).astype(BF16)
            dl_t = jnp.concatenate([dl_c, an_ref[0, :, sl_a]], axis=0).T
            row = lambda t, o: jnp.concatenate([jnp.broadcast_to(t[o: o + 1, :], (ATT_BLK, 2 * ATT_BLK)),
                                                jnp.broadcast_to(t[64 + o: 65 + o, :], (ATT_BLK, 2 * ATT_BLK))], axis=0)
            pt = jnp.exp(st_scr[hp] + bias_t2 - row(dl_t, AUX_LSE_LANE))
            pt_scr[hp] = pt.astype(BF16)
            dst_scr[hp] = (pt * (dpt_scr[hp] - row(dl_t, 0))).astype(BF16)
        for hp, sl in enumerate(pairs):
            _, _, _, kcat, _, qcat, _, docat = operands(sl)
            dqkv_ref[0, :, sl] = unstack(_dot(ds_scr[hp], kcat)) * ATT_SCALE
            dqkv_ref[0, :, AT_W + HEAD_PAIR * hp: AT_W + HEAD_PAIR * (hp + 1)] = unstack(_dot(dst_scr[hp], qcat)) * ATT_SCALE
            dqkv_ref[0, :, 2 * AT_W + HEAD_PAIR * hp: 2 * AT_W + HEAD_PAIR * (hp + 1)] = unstack(_dot(pt_scr[hp], docat))

    cur = lambda j: pl.BlockSpec((1, ATT_BLK, AT_W), lambda r, n: (r, n, j))
    prev = lambda j: pl.BlockSpec((1, ATT_BLK, AT_W), lambda r, n: (r, jnp.maximum(n - 1, 0), j))
    nxt = lambda j: pl.BlockSpec((1, ATT_BLK, AT_W), lambda r, n: (r, jnp.minimum(n + 1, nb - 1), j))
    return pl.pallas_call(
        body, name=f"att_bwd_d{d}", grid=(d, nb),
        in_specs=[cur(0), nxt(0), prev(1), cur(1), prev(2), cur(2),
                  pl.BlockSpec((1, ATT_BLK, AUX_W), lambda r, n: (r, n, 0)),
                  pl.BlockSpec((1, ATT_BLK, AUX_W), lambda r, n: (r, jnp.minimum(n + 1, nb - 1), 0)),
                  _full((2, ATT_BLK, 2 * ATT_BLK)), _full((2, ATT_BLK, 2 * ATT_BLK))],
        out_specs=pl.BlockSpec((1, ATT_BLK, 3 * AT_W), lambda r, n: (r, n, 0)),
        out_shape=jax.ShapeDtypeStruct((d, L, 3 * AT_W), F32),
        scratch_shapes=[pltpu.VMEM((4, 2 * ATT_BLK, 2 * ATT_BLK), F32)] * 4 + [pltpu.VMEM((4, 2 * ATT_BLK, 2 * ATT_BLK), BF16)] * 3,
        compiler_params=_params(("parallel", "parallel")),
    )(qkv_d, qkv_d, qkv_d, qkv_d, qkv_d, qkv_d, aux_d, aux_d, *_att_bias())


def _att_bwd_combine(dqkvs, cos2, sin2):
    S = dqkvs[0].shape[1]
    tm = 512

    def body(a_ref, b4_ref, c16_ref, cos_ref, sin_ref, dq_ref, dk_ref, dv_ref, b_scr, c_scr):
        _merge_residues(b4_ref, b_scr, 4)
        _merge_residues(c16_ref, c_scr, 16)
        t = a_ref[...] + _from_slabs(b_scr) + _from_slabs(c_scr)
        dy = t[:, : 2 * AT_W]
        cosf = jnp.tile(cos_ref[...], (1, 8))
        sinf = jnp.tile(sin_ref[...], (1, 8))
        dx = dy * cosf - _rope_rot(dy) * sinf
        dq_ref[...] = dx[:, :AT_W].astype(BF16)
        dk_ref[...] = dx[:, AT_W:].astype(BF16)
        dv_ref[...] = t[:, 2 * AT_W:].astype(BF16)

    row = lambda w: pl.BlockSpec((tm, w), lambda s: (s, 0))
    act = jax.ShapeDtypeStruct((S, AT_W), BF16)
    return pl.pallas_call(
        body, name="att_bwd_combine", grid=(S // tm,),
        in_specs=[row(3 * AT_W), _dil_spec(4, tm, 3 * AT_W), _dil_spec(16, tm, 3 * AT_W), row(128), row(128)],
        out_specs=[row(AT_W), row(AT_W), row(AT_W)],
        out_shape=[act, act, act],
        scratch_shapes=[_slab_scratch(tm, 3 * AT_W), _slab_scratch(tm, 3 * AT_W)],
        compiler_params=_params(("parallel",)),
    )(dqkvs[0].reshape(S, 3 * AT_W), dqkvs[1], dqkvs[2], cos2, sin2)


def _outproj(x2, tgt2, mix_hg, mix_at, mixt_hg, mixt_at, w_out_full, fnw):
    S = x2.shape[0]
    tm = 256
    ns = S // tm

    def body(x_ref, t_ref, mh_ref, ma_ref, mht_ref, mat_ref, w_ref, fw_ref,
             dh_ref, dmh_ref, dma_ref, gw_ref, gfw_ref, loss_ref):
        s = pl.program_id(0)

        @pl.when(s == 0)
        def _():
            gw_ref[...] = jnp.zeros_like(gw_ref)
            gfw_ref[...] = jnp.zeros_like(gfw_ref)
            loss_ref[...] = jnp.zeros_like(loss_ref)

        y = _dot(mh_ref[...], w_ref[:HG_W, :]) + _dot(ma_ref[...], w_ref[HG_W:, :])
        h = x_ref[...] + y
        r = lax.rsqrt(jnp.mean(h * h, axis=-1, keepdims=True) + EPS)
        hn = h * r
        fw = fw_ref[...]
        err = hn * fw - t_ref[...]
        loss_ref[...] += 0.5 * jnp.sum(jnp.mean(err * err, axis=-1, keepdims=True))
        dout = err * (1.0 / D_MODEL)
        gfw_ref[...] += jnp.sum(dout * hn, axis=0, keepdims=True)
        dhn = dout * fw
        dh = r * (dhn - hn * jnp.mean(dhn * hn, axis=-1, keepdims=True))
        dh_ref[...] = dh
        dhb = dh.astype(BF16)
        dmh_ref[...] = _dot_nt(dhb, w_ref[:HG_W, :])
        dma_ref[...] = _dot_nt(dhb, w_ref[HG_W:, :])
        gw_ref[:HG_W, :] += _dot(mht_ref[...], dhb)
        gw_ref[HG_W:, :] += _dot(mat_ref[...], dhb)

    row = lambda w: pl.BlockSpec((tm, w), lambda s: (s, 0))
    colb = pl.BlockSpec((HG_W, tm), lambda s: (0, s))
    return pl.pallas_call(
        body, name="outproj", grid=(ns,),
        in_specs=[row(D_MODEL), row(D_MODEL), row(HG_W), row(AT_W), colb, colb,
                  _full((D_MODEL, D_MODEL)), _full((1, D_MODEL))],
        out_specs=[row(D_MODEL), row(HG_W), row(AT_W), _full((D_MODEL, D_MODEL)), _full((1, D_MODEL)), _full((8, 128))],
        out_shape=[jax.ShapeDtypeStruct((S, D_MODEL), F32), jax.ShapeDtypeStruct((S, HG_W), F32),
                   jax.ShapeDtypeStruct((S, AT_W), F32), jax.ShapeDtypeStruct((D_MODEL, D_MODEL), F32),
                   jax.ShapeDtypeStruct((1, D_MODEL), F32), jax.ShapeDtypeStruct((8, 128), F32)],
        compiler_params=_params(("arbitrary",)),
    )(x2, tgt2, mix_hg, mix_at, mixt_hg, mixt_at, w_out_full, fnw)


def _inproj_bwd_x(dps, w_in_full, x2, norm_w, dh):
    S = x2.shape[0]
    tm = 256

    def body(d0, d1, d2, d3, d4, d5, d6, d7, w_ref, x_ref, nw_ref, dh_ref, gx_ref, gnw_ref):
        s = pl.program_id(0)

        @pl.when(s == 0)
        def _():
            gnw_ref[...] = jnp.zeros_like(gnw_ref)

        du = jnp.zeros((tm, D_MODEL), F32)
        for i, dref in enumerate((d0, d1, d2, d3, d4, d5, d6, d7)):
            j, half = divmod(i, 2)
            du = du + _dot_nt(dref[...], w_ref[j, :, 512 * half: 512 * (half + 1)])
        x = x_ref[...]
        r = lax.rsqrt(jnp.mean(x * x, axis=-1, keepdims=True) + EPS)
        xh = x * r
        gnw_ref[...] += jnp.sum(du * xh, axis=0, keepdims=True)
        dun = du * nw_ref[...]
        gx_ref[...] = dh_ref[...] + r * (dun - xh * jnp.mean(dun * xh, axis=-1, keepdims=True))

    row = lambda w: pl.BlockSpec((tm, w), lambda s: (s, 0))
    return pl.pallas_call(
        body, name="inproj_bwd_x", grid=(S // tm,),
        in_specs=[row(512)] * 8 + [_full((4, D_MODEL, 1024)), row(D_MODEL), _full((1, D_MODEL)), row(D_MODEL)],
        out_specs=[row(D_MODEL), _full((1, D_MODEL))],
        out_shape=[jax.ShapeDtypeStruct((S, D_MODEL), F32), jax.ShapeDtypeStruct((1, D_MODEL), F32)],
        compiler_params=_params(("arbitrary",)),
    )(*dps, w_in_full, x2, norm_w, dh)


def _inproj_bwd_w(ut, dps):
    S = ut.shape[1]
    tm = 256

    def body(ut_ref, d0, d1, d2, d3, d4, d5, d6, d7, g_ref):
        @pl.when(pl.program_id(0) == 0)
        def _():
            g_ref[...] = jnp.zeros_like(g_ref)

        utb = ut_ref[...]
        for i, dref in enumerate((d0, d1, d2, d3, d4, d5, d6, d7)):
            j, half = divmod(i, 2)
            g_ref[j, :, 512 * half: 512 * (half + 1)] += _dot(utb, dref[...])

    return pl.pallas_call(
        body, name="inproj_bwd_w", grid=(S // tm,),
        in_specs=[pl.BlockSpec((D_MODEL, tm), lambda s: (0, s))] + [pl.BlockSpec((tm, 512), lambda s: (s, 0))] * 8,
        out_specs=_full((4, D_MODEL, 1024)),
        out_shape=jax.ShapeDtypeStruct((4, D_MODEL, 1024), F32),
        compiler_params=_params(("arbitrary",)),
    )(ut, *dps)


def _adamw_update(gg, w_ref, m_ref, v_ref, d_ref, nm_ref, nv_ref):
    nm = ADAM_B1 * m_ref[...] + (1.0 - ADAM_B1) * gg
    nv = ADAM_B2 * v_ref[...] + (1.0 - ADAM_B2) * (gg * gg)
    m_hat = nm / (1.0 - ADAM_B1 ** ADAM_STEP)
    v_hat = nv / (1.0 - ADAM_B2 ** ADAM_STEP)
    d_ref[...] = -ADAM_LR * (m_hat / (jnp.sqrt(v_hat) + ADAM_EPS) + ADAM_WD * w_ref[...])
    nm_ref[...] = nm
    nv_ref[...] = nv


def _adamw(w, g, m, v, name):
    rows, cols = w.shape
    tr = min(rows, 256)

    def body(w_ref, g_ref, m_ref, v_ref, d_ref, nm_ref, nv_ref):
        _adamw_update(g_ref[...], w_ref, m_ref, v_ref, d_ref, nm_ref, nv_ref)

    spec = pl.BlockSpec((tr, cols), lambda i: (i, 0))
    sds = jax.ShapeDtypeStruct((rows, cols), F32)
    return pl.pallas_call(
        body, name=name, grid=(rows // tr,),
        in_specs=[spec] * 4, out_specs=[spec] * 3, out_shape=[sds] * 3,
        compiler_params=_params(("parallel",)),
    )(w, g, m, v)


def _place():
    return lax.axis_index("x"), lax.axis_index("y"), lax.axis_index("c")


def _gather_weights(w_in_s, w_out_s):
    def body(win_ref, wout_ref, fin_ref, fout_ref, send_sems, recv_sems):
        x, y, c = _place()
        me = (x, y, c)
        sib = (x, y, 1 - c)
        mine = 2 * x + y
        fin_ref[mine] = win_ref[...].astype(BF16)
        fout_ref[mine] = wout_ref[...].astype(BF16)
        chips = [(1 - x, y), (x, 1 - y), (1 - x, 1 - y)]

        def halves(chip, half):
            return (fin_ref.at[chip, pl.ds(half * 512, 512), :], fout_ref.at[chip, pl.ds(half * 128, 128), :])

        def copy(k, ref, to):
            return pltpu.make_async_remote_copy(src_ref=ref, dst_ref=ref, send_sem=send_sems.at[k],
                                                recv_sem=recv_sems.at[k], device_id=to, device_id_type=MESH)

        first, passed = [], []
        for j, (cx, cy) in enumerate(chips):
            for a, ref in enumerate(halves(mine, c)):
                first.append(copy(2 * j + a, ref, (cx, cy, c)))
        for cp in first:
            cp.start()
        for j, (cx, cy) in enumerate(chips):
            for a, ref in enumerate(halves(2 * cx + cy, c)):
                copy(2 * j + a, ref, me).wait_recv()
                fwd = copy(6 + 2 * j + a, ref, sib)
                fwd.start()
                passed.append(fwd)
        for j, (cx, cy) in enumerate(chips):
            for a, ref in enumerate(halves(2 * cx + cy, 1 - c)):
                copy(6 + 2 * j + a, ref, me).wait_recv()
        for cp in first + passed:
            cp.wait_send()

    vm = pl.BlockSpec(memory_space=pltpu.VMEM)
    return pl.pallas_call(
        body, name="gather_weights",
        in_specs=[vm, vm], out_specs=[vm, vm],
        out_shape=[jax.ShapeDtypeStruct((4, D_MODEL, 1024), BF16), jax.ShapeDtypeStruct((4, 256, D_MODEL), BF16)],
        scratch_shapes=[pltpu.SemaphoreType.DMA((12,)), pltpu.SemaphoreType.DMA((12,))],
        compiler_params=pltpu.CompilerParams(vmem_limit_bytes=VMEM_LIMIT),
    )(w_in_s, w_out_s)


def _swap_halves(g_in, g_out):
    def body(gin_ref, gout_ref, rin_ref, rout_ref, send_sems, recv_sems):
        x, y, c = _place()
        sib = (x, y, 1 - c)
        cps = [pltpu.make_async_remote_copy(src_ref=src.at[:, 1 - c], dst_ref=dst, send_sem=send_sems.at[k],
                                            recv_sem=recv_sems.at[k], device_id=sib, device_id_type=MESH)
               for k, (src, dst) in enumerate(((gin_ref, rin_ref), (gout_ref, rout_ref)))]
        for cp in cps:
            cp.start()
        for cp in cps:
            cp.wait()

    hbm = pl.BlockSpec(memory_space=pl.ANY)
    return pl.pallas_call(
        body, name="swap_halves",
        in_specs=[hbm, hbm], out_specs=[hbm, hbm],
        out_shape=[jax.ShapeDtypeStruct((4,) + g.shape[2:], F32) for g in (g_in, g_out)],
        scratch_shapes=[pltpu.SemaphoreType.DMA((2,)), pltpu.SemaphoreType.DMA((2,))],
    )(g_in, g_out)


def _add_half(g, r, cidx, name):
    n, _, rows, cols = g.shape

    def body(c_ref, g_ref, r_ref, o_ref):
        o_ref[0] = (g_ref[0, 0] + r_ref[0]).astype(BF16)

    return pl.pallas_call(
        body, name=name,
        grid_spec=pltpu.PrefetchScalarGridSpec(
            num_scalar_prefetch=1, grid=(n,),
            in_specs=[pl.BlockSpec((1, 1, rows, cols), lambda j, c_ref: (j, c_ref[0], 0, 0)),
                      pl.BlockSpec((1, rows, cols), lambda j, c_ref: (j, 0, 0))],
            out_specs=pl.BlockSpec((1, rows, cols), lambda j, c_ref: (j, 0, 0))),
        out_shape=jax.ShapeDtypeStruct((n, rows, cols), BF16),
        compiler_params=_params(("parallel",)),
    )(cidx, g, r)


def _exchange_chips(cs_in, cs_out):
    def body(in_ref, out_ref, rin_ref, rout_ref, send_sems, recv_sems):
        x, y, c = _place()
        chips = [(1 - x, y), (x, 1 - y), (1 - x, 1 - y)]
        cps = []
        for k, (cx, cy) in enumerate(chips):
            for a, (src, dst) in enumerate(((in_ref, rin_ref), (out_ref, rout_ref))):
                cps.append(pltpu.make_async_remote_copy(
                    src_ref=src.at[2 * cx + cy], dst_ref=dst.at[k], send_sem=send_sems.at[2 * k + a],
                    recv_sem=recv_sems.at[2 * k + a], device_id=(cx, cy, c), device_id_type=MESH))
        for cp in cps:
            cp.start()
        for cp in cps:
            cp.wait()

    hbm = pl.BlockSpec(memory_space=pl.ANY)
    return pl.pallas_call(
        body, name="exchange_chips",
        in_specs=[hbm, hbm], out_specs=[hbm, hbm],
        out_shape=[jax.ShapeDtypeStruct((3,) + a.shape[1:], a.dtype) for a in (cs_in, cs_out)],
        scratch_shapes=[pltpu.SemaphoreType.DMA((6,)), pltpu.SemaphoreType.DMA((6,))],
    )(cs_in, cs_out)


PEER_XOR = (2, 1, 3)


def _sum_chips(cs, r, chip_idx, name):
    _, rows, cols = r.shape
    tr = min(rows, 256)

    def body(m_ref, cs_ref, r_ref, o_ref):
        mine = m_ref[0]
        own = cs_ref[0].astype(F32)
        got = [r_ref[k].astype(F32) for k in range(3)]
        acc = None
        for s in range(4):
            rel = mine ^ s
            term = jnp.where(rel == 0, own, jnp.where(rel == PEER_XOR[0], got[0],
                                                      jnp.where(rel == PEER_XOR[1], got[1], got[2])))
            acc = term if acc is None else acc + term
        o_ref[...] = acc

    return pl.pallas_call(
        body, name=name,
        grid_spec=pltpu.PrefetchScalarGridSpec(
            num_scalar_prefetch=1, grid=(rows // tr,),
            in_specs=[pl.BlockSpec((1, tr, cols), lambda i, m_ref: (m_ref[0], i, 0)),
                      pl.BlockSpec((3, tr, cols), lambda i, m_ref: (0, i, 0))],
            out_specs=pl.BlockSpec((tr, cols), lambda i, m_ref: (i, 0))),
        out_shape=jax.ShapeDtypeStruct((rows, cols), F32),
        compiler_params=_params(("parallel",)),
    )(chip_idx, cs, r)


def _swap_reduced(h_in, h_out):
    def body(in_ref, out_ref, rin_ref, rout_ref, send_sems, recv_sems):
        x, y, c = _place()
        cps = [pltpu.make_async_remote_copy(src_ref=src, dst_ref=dst, send_sem=send_sems.at[k],
                                            recv_sem=recv_sems.at[k], device_id=(x, y, 1 - c), device_id_type=MESH)
               for k, (src, dst) in enumerate(((in_ref, rin_ref), (out_ref, rout_ref)))]
        for cp in cps:
            cp.start()
        for cp in cps:
            cp.wait()

    hbm = pl.BlockSpec(memory_space=pl.ANY)
    return pl.pallas_call(
        body, name="swap_reduced",
        in_specs=[hbm, hbm], out_specs=[hbm, hbm],
        out_shape=[jax.ShapeDtypeStruct(h.shape, F32) for h in (h_in, h_out)],
        scratch_shapes=[pltpu.SemaphoreType.DMA((2,)), pltpu.SemaphoreType.DMA((2,))],
    )(h_in, h_out)


def _adamw_halves(w, mine, theirs, m, v, cidx, name):
    rows, cols = w.shape
    half = rows // 2
    tr = min(half, 256)
    nbh = half // tr

    def body(c_ref, w_ref, a_ref, b_ref, m_ref, v_ref, g_ref, d_ref, nm_ref, nv_ref):
        gg = jnp.where(pl.program_id(0) // nbh == c_ref[0], a_ref[...], b_ref[...])
        g_ref[...] = gg
        _adamw_update(gg, w_ref, m_ref, v_ref, d_ref, nm_ref, nv_ref)

    spec = pl.BlockSpec((tr, cols), lambda i, c_ref: (i, 0))
    hspec = pl.BlockSpec((tr, cols), lambda i, c_ref: (i % nbh, 0))
    sds = jax.ShapeDtypeStruct((rows, cols), F32)
    return pl.pallas_call(
        body, name=name,
        grid_spec=pltpu.PrefetchScalarGridSpec(
            num_scalar_prefetch=1, grid=(rows // tr,),
            in_specs=[spec, hspec, hspec, spec, spec], out_specs=[spec] * 4),
        out_shape=[sds] * 4,
        compiler_params=_params(("parallel",)),
    )(cidx, w, mine, theirs, m, v)


def _allreduce_small(g_nw, g_fw, g_hgw, g_lbl, loss8):
    def body(nw_ref, fw_ref, hgw_ref, lbl_ref, loss_ref, out_ref, slots, send_sems, recv_sems):
        x, y, c = _place()
        me = 4 * x + 2 * y + c
        slots[me] = jnp.zeros((8, D_MODEL), F32)
        slots[me, 0:1, :] = nw_ref[...]
        slots[me, 1:2, :] = fw_ref[...]
        slots[me, 2:3, 0:HG_W] = hgw_ref[...]
        slots[me, 3:4, 0:HG_W] = lbl_ref[0:1, :]
        slots[me, 3:4, HG_W:] = lbl_ref[1:2, :]
        slots[me, 4:5, 0:128] = loss_ref[0:1, :]
        cps = []
        for k in range(1, 8):
            dx, dy, dc = (k >> 2) & 1, (k >> 1) & 1, k & 1
            to = (x ^ dx, y ^ dy, c ^ dc)
            cps.append(pltpu.make_async_remote_copy(
                src_ref=slots.at[me], dst_ref=slots.at[me], send_sem=send_sems.at[k - 1],
                recv_sem=recv_sems.at[k - 1], device_id=to, device_id_type=MESH))
        for cp in cps:
            cp.start()
        for cp in cps:
            cp.wait()
        acc = slots[0]
        for i in range(1, 8):
            acc = acc + slots[i]
        out_ref[...] = acc

    vm = pl.BlockSpec(memory_space=pltpu.VMEM)
    return pl.pallas_call(
        body, name="allreduce_small",
        in_specs=[vm] * 5, out_specs=vm,
        out_shape=jax.ShapeDtypeStruct((8, D_MODEL), F32),
        scratch_shapes=[pltpu.VMEM((8, 8, D_MODEL), F32), pltpu.SemaphoreType.DMA((7,)), pltpu.SemaphoreType.DMA((7,))],
    )(g_nw, g_fw, g_hgw, g_lbl, loss8)


def _rope_tables(S):
    inv_freq = 1.0 / (ROPE_THETA ** (jnp.arange(ROPE_HALF, dtype=F32) / ROPE_HALF))
    ang = jnp.arange(S, dtype=jnp.int32).astype(F32)[:, None] * inv_freq[None, :]
    cos, sin = jnp.cos(ang), jnp.sin(ang)
    cos2 = jnp.concatenate([cos, cos, cos, cos], axis=-1)
    sin2 = jnp.concatenate([-sin, sin, -sin, sin], axis=-1)
    return cos2, sin2


def _local_step(x2, tgt2, norm_w, w_in_full, lbl, hg_norm_w, w_out_full, fnw):
    S = x2.shape[0]
    cos2, sin2 = _rope_tables(S)
    p0, p1, qkv, qkv4, qkv16, p3, ut = _inproj(x2, norm_w, w_in_full, cos2, sin2)
    o_hg, mix_hg, mixt_hg, states = _hg_fwd(p0, p1, lbl, hg_norm_w)
    qkv_ds = [qkv.reshape(1, S, 3 * AT_W), qkv4, qkv16]
    ols = [_att_fwd(q) for q in qkv_ds]
    ol, mix_at, mixt_at = _att_combine(ols, p3)
    dh, dm_hg, dm_at, g_wout, g_fw, loss8 = _outproj(x2, tgt2, mix_hg, mix_at, mixt_hg, mixt_at, w_out_full, fnw)
    dqr, dfl, dv_hg, dz_hg, g_lbl, g_hgw = _hg_bwd(p0, p1, o_hg, dm_hg, states, lbl, hg_norm_w)
    aux, aux4, aux16, dz_at = _att_gate_bwd(dm_at, ol, p3)
    aux_ds = [aux.reshape(1, S, AUX_W), aux4, aux16]
    dqkvs = [_att_bwd(q, a) for q, a in zip(qkv_ds, aux_ds)]
    dq_at, dk_at, dv_at = _att_bwd_combine(dqkvs, cos2, sin2)
    dps = [dqr, dfl, dv_hg, dz_hg, dq_at, dk_at, dv_at, dz_at]
    grad_x, g_nw = _inproj_bwd_x(dps, w_in_full, x2, norm_w, dh)
    g_win = _inproj_bwd_w(ut, dps)
    return loss8, grad_x, g_nw, g_win, g_lbl, g_hgw, g_wout, g_fw


def kernel(x, norm_w, w_in, hgrn_lb_logits, hg_norm_w, w_out, final_norm_w, loss_target, m_norm_w, m_w_in, m_hgrn_lb_logits, m_hg_norm_w, m_w_out, m_final_norm_w, v_norm_w, v_w_in, v_hgrn_lb_logits, v_hg_norm_w, v_w_out, v_final_norm_w):
    S = x.shape[1]
    w_in_full, w_out_full = _gather_weights(w_in[0], w_out[0])
    loss8, grad_x, g_nw, g_win, g_lbl, g_hgw, g_wout, g_fw = _local_step(
        x[0], loss_target[0], norm_w, w_in_full, hgrn_lb_logits, hg_norm_w,
        w_out_full.reshape(D_MODEL, D_MODEL), final_norm_w.reshape(1, D_MODEL))

    cidx = lax.axis_index("c").astype(jnp.int32).reshape(1)
    g_in4 = g_win.reshape(4, 2, 512, 1024)
    g_out4 = g_wout.reshape(4, 2, 128, D_MODEL)
    r_in, r_out = _swap_halves(g_in4, g_out4)
    cs_in = _add_half(g_in4, r_in, cidx, "add_half_in")
    cs_out = _add_half(g_out4, r_out, cidx, "add_half_out")
    x_in, x_out = _exchange_chips(cs_in, cs_out)
    chip_idx = (2 * lax.axis_index("x") + lax.axis_index("y")).astype(jnp.int32).reshape(1)
    h_in = _sum_chips(cs_in, x_in, chip_idx, "sum_chips_in")
    h_out = _sum_chips(cs_out, x_out, chip_idx, "sum_chips_out")
    s_in, s_out = _swap_reduced(h_in, h_out)

    red = _allreduce_small(g_nw, g_fw, g_hgw, g_lbl, loss8)
    loss = red[4, 0]
    grad_norm_w = red[0:1, :]
    grad_final_norm_w = red[1, :]
    grad_hg_norm_w = red[2:3, :HG_W]
    grad_lbl = jnp.concatenate([red[3:4, :HG_W], red[3:4, HG_W:]], axis=0)

    d_nw, m_nw, v_nw = _adamw(norm_w, grad_norm_w, m_norm_w, v_norm_w, "adamw_norm_w")
    grad_w_in, d_win, m_win, v_win = _adamw_halves(w_in[0], h_in, s_in, m_w_in[0], v_w_in[0], cidx, "adamw_w_in")
    d_lbl, m_lbl, v_lbl = _adamw(hgrn_lb_logits, grad_lbl, m_hgrn_lb_logits, v_hgrn_lb_logits, "adamw_lb_logits")
    d_hgw, m_hgw, v_hgw = _adamw(hg_norm_w, grad_hg_norm_w, m_hg_norm_w, v_hg_norm_w, "adamw_hg_norm_w")
    grad_w_out, d_wout, m_wout, v_wout = _adamw_halves(w_out[0], h_out, s_out, m_w_out[0], v_w_out[0], cidx, "adamw_w_out")
    d_fw, m_fw, v_fw = _adamw(final_norm_w.reshape(1, D_MODEL), grad_final_norm_w.reshape(1, D_MODEL),
                              m_final_norm_w.reshape(1, D_MODEL), v_final_norm_w.reshape(1, D_MODEL), "adamw_final_norm_w")
    e1 = lambda a: a[None]
    flat = lambda a: a.reshape(D_MODEL)
    return (loss, grad_x[None], grad_norm_w, e1(grad_w_in), grad_lbl, grad_hg_norm_w, e1(grad_w_out), grad_final_norm_w,
            d_nw, e1(d_win), d_lbl, d_hgw, e1(d_wout), flat(d_fw),
            m_nw, e1(m_win), m_lbl, m_hgw, e1(m_wout), flat(m_fw),
            v_nw, e1(v_win), v_lbl, v_hgw, e1(v_wout), flat(v_fw))
```

```python
import functools

import jax
import jax.numpy as jnp
import numpy as np
from jax import lax
from jax.experimental import pallas as pl
from jax.experimental.pallas import tpu as pltpu

F32 = jnp.float32
BF16 = jnp.bfloat16
MESH = pl.DeviceIdType.MESH

D_MODEL = 1024
HG_W = 512
AT_W = 512
HEAD_PAIR = 128
ROPE_HALF = 32
ROPE_THETA = 10000.0
EPS = 1e-6
CHUNK = 128
LEVELS = (64, 32, 16, 8, 4, 2, 1)
DIAG = 1
SUBLANES = 8
ATT_BLK = 128
ATT_QB = 4
DILATIONS = (1, 4, 16)
ATT_SCALE = 0.125
NEG = -1e30
VMEM_LIMIT = 56 * 1024 * 1024

ADAM_LR = 0.001
ADAM_B1 = 0.9
ADAM_B2 = 0.999
ADAM_EPS = 1e-08
ADAM_WD = 0.01
ADAM_STEP = 10


def _iota(shape, dim):
    return lax.broadcasted_iota(jnp.int32, shape, dim)


def _dot(a, b):
    return jnp.dot(a, b, preferred_element_type=F32)


def _dot_nt(a, b):
    return lax.dot_general(a, b, (((1,), (1,)), ((), ())), preferred_element_type=F32)


def _dot_hi(a, b):
    return jnp.dot(a, b, preferred_element_type=F32, precision=lax.Precision.HIGHEST)


def _sig(v):
    return 1.0 / (1.0 + jnp.exp(-v))


def _params(sem=None, vmem=VMEM_LIMIT):
    return pltpu.CompilerParams(dimension_semantics=sem, vmem_limit_bytes=vmem)


def _full(shape):
    n = len(shape)
    return pl.BlockSpec(shape, lambda *_: (0,) * n)


def _rope_rot(y):
    n = y.shape[1]
    first = (_iota(y.shape, 1) & (2 * ROPE_HALF - 1)) < ROPE_HALF
    return jnp.where(first, pltpu.roll(y, n - ROPE_HALF, 1), pltpu.roll(y, ROPE_HALF, 1))


def _dil_spec(d, tm, width):
    return pl.BlockSpec((d, tm // d, width), lambda s: (0, s, 0))


LANES = 128


def _slab_scratch(tm, width):
    return pltpu.VMEM((width // LANES, tm, LANES), F32)


def _to_slabs(v, slabs_ref):
    for j in range(slabs_ref.shape[0]):
        slabs_ref[j] = v[:, LANES * j: LANES * (j + 1)]


def _from_slabs(slabs_ref):
    return jnp.concatenate([slabs_ref[j] for j in range(slabs_ref.shape[0])], axis=1)


def _split_residues(slabs_ref, dst_ref, d, dtype):
    nslab, tm, _ = slabs_ref.shape
    for r in range(d):
        for j in range(nslab):
            dst_ref[r, :, LANES * j: LANES * (j + 1)] = slabs_ref[j, pl.ds(r, tm // d, stride=d), :].astype(dtype)


def _merge_residues(src_ref, slabs_ref, d):
    nslab, tm, _ = slabs_ref.shape
    for r in range(d):
        for j in range(nslab):
            slabs_ref[j, pl.ds(r, tm // d, stride=d), :] = src_ref[r, :, LANES * j: LANES * (j + 1)].astype(F32)


def _inproj(x2, norm_w, w_in_full, cos2, sin2):
    S = x2.shape[0]
    tm = 256

    def body(x_ref, nw_ref, w_ref, cos_ref, sin_ref, p0_ref, p1_ref, qkv_ref, qkv4_ref, qkv16_ref, p3_ref, ut_ref, scr):
        x = x_ref[...]
        r = lax.rsqrt(jnp.mean(x * x, axis=-1, keepdims=True) + EPS)
        u = x * r * nw_ref[...]
        ub = u.astype(BF16)
        ut_ref[...] = u.T.astype(BF16)
        p0_ref[...] = _dot(ub, w_ref[0])
        p1_ref[...] = _dot(ub, w_ref[1])
        y2 = _dot(ub, w_ref[2])
        cosf = jnp.tile(cos_ref[...], (1, 8))
        sinf = jnp.tile(sin_ref[...], (1, 8))
        y3 = _dot(ub, w_ref[3])
        p3_ref[...] = y3
        qkv = jnp.concatenate([y2 * cosf + _rope_rot(y2) * sinf, y3[:, :AT_W]], axis=1)
        qkv_ref[...] = qkv.astype(BF16)
        _to_slabs(qkv, scr)
        _split_residues(scr, qkv4_ref, 4, BF16)
        _split_residues(scr, qkv16_ref, 16, BF16)

    row = lambda w: pl.BlockSpec((tm, w), lambda s: (s, 0))
    qkv_w = 3 * AT_W
    return pl.pallas_call(
        body, name="inproj", grid=(S // tm,),
        in_specs=[row(D_MODEL), _full((1, D_MODEL)), _full((4, D_MODEL, 1024)), row(128), row(128)],
        out_specs=[row(1024), row(1024), row(qkv_w), _dil_spec(4, tm, qkv_w), _dil_spec(16, tm, qkv_w), row(1024),
                   pl.BlockSpec((D_MODEL, tm), lambda s: (0, s))],
        out_shape=[jax.ShapeDtypeStruct((S, 1024), F32), jax.ShapeDtypeStruct((S, 1024), F32),
                   jax.ShapeDtypeStruct((S, qkv_w), BF16), jax.ShapeDtypeStruct((4, S // 4, qkv_w), BF16),
                   jax.ShapeDtypeStruct((16, S // 16, qkv_w), BF16), jax.ShapeDtypeStruct((S, 1024), F32),
                   jax.ShapeDtypeStruct((D_MODEL, S), BF16)],
        scratch_shapes=[_slab_scratch(tm, qkv_w)],
        compiler_params=_params(("parallel",)),
    )(x2, norm_w, w_in_full, cos2, sin2)


HG_HPS = 4
N_LEV = len(LEVELS)


def _hg_const_arrays():
    r = np.arange(CHUNK)[:, None]
    c = np.arange(CHUNK)[None, :]
    tris = np.stack([r >= c, r <= c])
    lm = [((r // (2 * m)) == (c // (2 * m))) & (r % (2 * m) >= m) & (c % (2 * m) < m) for m in LEVELS]
    dm = [(c == r - dl) & (r % DIAG >= dl) for dl in range(DIAG)]
    masks = np.stack(lm + [x.T for x in lm] + dm)
    return jnp.asarray(tris, BF16), jnp.asarray(masks, F32)


def _split2(a):
    hi = a.astype(BF16)
    return hi, (a - hi.astype(F32)).astype(BF16)


def _dot3(a, b):
    ah, al = _split2(a)
    bh, bl = _split2(b)
    n = b.shape[1]
    p = _dot(ah, jnp.concatenate([bh, bl], axis=1))
    return (p[:, :n] + p[:, n:]) + _dot(al, bh)


def _tri_dot(tri, a):
    a1 = a.astype(BF16)
    r1 = a - a1.astype(F32)
    a2 = r1.astype(BF16)
    a3 = (r1 - a2.astype(F32)).astype(BF16)
    n = a.shape[1]
    p = _dot(tri, jnp.concatenate([a1, a2, a3], axis=1))
    return (p[:, :n] + p[:, n:2 * n]) + p[:, 2 * n:]


def _rowsum(t):
    return _dot(t.astype(BF16), jnp.ones((t.shape[1], t.shape[1]), BF16))


def _level_refs(b):
    refs = []
    pos = _iota(b.shape, 0)
    for m in LEVELS:
        if 2 * m >= SUBLANES:
            parts = [jnp.broadcast_to(b[r0 + m - 1: r0 + m, :], (2 * m, b.shape[1])) for r0 in range(0, CHUNK, 2 * m)]
            refs.append(parts[0] if len(parts) == 1 else jnp.concatenate(parts, axis=0))
        else:
            p = pos & (2 * m - 1)
            ref = b
            for off in range(-(m - 1), m + 1):
                if off != 0:
                    ref = jnp.where(p == m - 1 + off, pltpu.roll(b, off % CHUNK, 0), ref)
            refs.append(ref)
    return refs


def _hg_lb(lbl_ref):
    l0 = lbl_ref[0:1, :]
    l1 = lbl_ref[1:2, :]
    mx = jnp.maximum(l0, l1)
    e0 = jnp.exp(l0 - mx)
    e1 = jnp.exp(l1 - mx)
    p0 = e0 / (e0 + e1)
    lb = jnp.clip(p0, 1e-6, 1.0 - 1e-6)
    inside = (p0 >= 1e-6) & (p0 <= 1.0 - 1e-6)
    dlb_dl0 = jnp.where(inside, p0 * (e1 / (e0 + e1)), 0.0)
    return lb, dlb_dl0


def _sigmoid(v):
    return 0.5 * jnp.tanh(0.5 * v) + 0.5


def _hg_gates(qr, fl, lb):
    sig = _sigmoid(fl)
    f = lb + (1.0 - lb) * sig
    g = jnp.log(f)
    k = (1.0 - lb) * (1.0 - sig)
    sq = _sigmoid(qr)
    q = qr * sq
    return sig, f, g, k, sq, q


def _hg_levels(q, k, b, mk_ref):
    refs = _level_refs(b)
    a = jnp.zeros((CHUNK, CHUNK), F32)
    es, qts, kts = [], [], []
    for i in range(N_LEV):
        diff = b - refs[i]
        e = jnp.exp(jnp.minimum(diff, -diff))
        qt = (q * e).astype(BF16)
        kt = (k * e).astype(BF16)
        a = a + _dot_nt(qt, kt) * mk_ref[i]
        es.append(e); qts.append(qt); kts.append(kt)
    return a, es, qts, kts


def _hg_specs(nc, rev):
    cc = (lambda c: nc - 1 - c) if rev else (lambda c: c)
    w = 128 * HG_HPS
    blk = lambda off: pl.BlockSpec((CHUNK, w), lambda h, c: (cc(c), h + off))
    vec = pl.BlockSpec((1, w), lambda h, c: (0, h))
    lb2 = pl.BlockSpec((2, w), lambda h, c: (0, h))
    st = pl.BlockSpec((1, HG_HPS, 128, 128), lambda h, c: (cc(c), h, 0, 0))
    consts = [_full((2, CHUNK, CHUNK)), _full((2 * N_LEV + DIAG, CHUNK, CHUNK))]
    return blk, vec, lb2, st, consts


def _hg_fwd(p0, p1, lbl, gw):
    S = p0.shape[0]
    nc = S // CHUNK
    ng = 4 // HG_HPS

    def body(qr_ref, fl_ref, v_ref, z_ref, lbl_ref, gw_ref, tri_ref, mk_ref,
             o_ref, mix_ref, mixt_ref, st_ref, state):
        c = pl.program_id(1)

        @pl.when(c == 0)
        def _():
            state[...] = jnp.zeros_like(state)

        lb_all, _ = _hg_lb(lbl_ref)
        heads = [slice(128 * hh, 128 * (hh + 1)) for hh in range(HG_HPS)]
        qs, ks, bs, mats = [], [], [], []
        for sl in heads:
            _, _, g, k, _, q = _hg_gates(qr_ref[:, sl], fl_ref[:, sl], lb_all[:, sl])
            qs.append(q); ks.append(k); bs.append(_tri_dot(tri_ref[0], g))
        for hh in range(HG_HPS):
            a, _, _, _ = _hg_levels(qs[hh], ks[hh], bs[hh], mk_ref)
            mats.append(a + _rowsum(qs[hh] * ks[hh]) * mk_ref[2 * N_LEV])
        for hh, sl in enumerate(heads):
            q, k, b, a, v, z = qs[hh], ks[hh], bs[hh], mats[hh], v_ref[:, sl], z_ref[:, sl]
            b_last = b[CHUNK - 1: CHUNK, :]
            st = state[hh]
            st_ref[0, hh] = st
            o = _dot_nt((q * jnp.exp(b)).astype(BF16), st.astype(BF16)) + _dot(a.astype(BF16), v.astype(BF16))
            state[hh] = st * jnp.exp(b_last) + _dot3(v.T, k * jnp.exp(b_last - b))
            o_ref[:, sl] = o
            rs = lax.rsqrt(jnp.mean(o * o, axis=-1, keepdims=True) + EPS)
            mixed = o * rs * gw_ref[:, sl] * (z * _sigmoid(z))
            mix_ref[:, sl] = mixed.astype(BF16)
            mixt_ref[sl, :] = mixed.T.astype(BF16)

    blk, vec, lb2, st_spec, consts = _hg_specs(nc, False)
    tris, masks = _hg_const_arrays()
    return pl.pallas_call(
        body, name="hg_fwd", grid=(ng, nc),
        in_specs=[blk(0), blk(ng), blk(0), blk(ng), lb2, vec] + consts,
        out_specs=[blk(0), blk(0), pl.BlockSpec((128 * HG_HPS, CHUNK), lambda h, c: (h, c)), st_spec],
        out_shape=[jax.ShapeDtypeStruct((S, HG_W), F32), jax.ShapeDtypeStruct((S, HG_W), BF16),
                   jax.ShapeDtypeStruct((HG_W, S), BF16), jax.ShapeDtypeStruct((nc, 4, 128, 128), F32)],
        scratch_shapes=[pltpu.VMEM((HG_HPS, 128, 128), F32)],
        compiler_params=_params(("parallel", "arbitrary")),
    )(p0, p0, p1, p1, lbl, gw, tris, masks)


def _hg_bwd(p0, p1, o_raw, dm, states, lbl, gw):
    S = p0.shape[0]
    nc = S // CHUNK
    ng = 4 // HG_HPS
    w = 128 * HG_HPS

    def body(qr_ref, fl_ref, v_ref, z_ref, o_ref, dm_ref, st_ref, lbl_ref, gw_ref, tri_ref, mk_ref,
             dqr_ref, dfl_ref, dv_ref, dz_ref, glbl_ref, ggw_ref, dstate, carry, acc_lb, acc_gw):
        c = pl.program_id(1)

        @pl.when(c == 0)
        def _():
            dstate[...] = jnp.zeros_like(dstate)
            carry[...] = jnp.zeros_like(carry)
            acc_lb[...] = jnp.zeros_like(acc_lb)
            acc_gw[...] = jnp.zeros_like(acc_gw)

        lb_all, dlb_dl0 = _hg_lb(lbl_ref)
        heads = [slice(128 * hh, 128 * (hh + 1)) for hh in range(HG_HPS)]
        diag_mask = mk_ref[2 * N_LEV]
        gates, bs, lev, dos = [], [], [], []
        for sl in heads:
            gt = _hg_gates(qr_ref[:, sl], fl_ref[:, sl], lb_all[:, sl])
            gates.append(gt)
            bs.append(_tri_dot(tri_ref[0], gt[2]))
        for hh, sl in enumerate(heads):
            _, _, _, k, _, q = gates[hh]
            lev.append(_hg_levels(q, k, bs[hh], mk_ref))
            o, z, dmix, gwv = o_ref[:, sl], z_ref[:, sl], dm_ref[:, sl], gw_ref[:, sl]
            rs = lax.rsqrt(jnp.mean(o * o, axis=-1, keepdims=True) + EPS)
            oh = o * rs
            sz = _sigmoid(z)
            dz_ref[:, sl] = (dmix * (oh * gwv) * (sz * (1.0 + z * (1.0 - sz)))).astype(BF16)
            don = dmix * (z * sz)
            acc_gw[0:1, sl] += jnp.sum(don * oh, axis=0, keepdims=True)
            dy = don * gwv
            dos.append(rs * (dy - oh * jnp.mean(dy * oh, axis=-1, keepdims=True)))
        inter = []
        for hh, sl in enumerate(heads):
            _, _, _, k, _, q = gates[hh]
            b, do, v = bs[hh], dos[hh], v_ref[:, sl]
            b_last = b[CHUNK - 1: CHUNK, :]
            eb = jnp.exp(b)
            edec = jnp.exp(b_last - b)
            dst = dstate[hh]
            dq = _dot3(do, st_ref[0, hh]) * eb
            dk = _dot3(v, dst) * edec
            dob = do.astype(BF16)
            da = _dot_nt(dob, v.astype(BF16))
            a, _, _, _ = lev[hh]
            a = a + _rowsum(q * k) * diag_mask
            dv_ref[:, sl] = (_dot(a.T.astype(BF16), dob) + _dot_nt((k * edec).astype(BF16), dst.astype(BF16))).astype(BF16)
            dstate[hh] = dst * jnp.exp(b_last) + _dot3(do.T, q * eb)
            inter.append((dq, dk, da))
        for hh, sl in enumerate(heads):
            sig, f, _, k, sq, q = gates[hh]
            _, es, qts, kts = lev[hh]
            dq, dk, da = inter[hh]
            db = q * dq - k * dk
            da_t = da.T
            for i in range(N_LEV):
                gq = _dot((da * mk_ref[i]).astype(BF16), kts[i])
                gk = _dot((da_t * mk_ref[N_LEV + i]).astype(BF16), qts[i])
                dq = dq + es[i] * gq
                dk = dk + es[i] * gk
                db = db + (qts[i].astype(F32) * gq - kts[i].astype(F32) * gk)
            dc = _rowsum(da * diag_mask)
            dq = dq + dc * k
            dk = dk + dc * q
            dg = _tri_dot(tri_ref[1], db) + carry[0:1, sl]
            carry[0:1, sl] += jnp.sum(db, axis=0, keepdims=True)
            lb = lb_all[:, sl]
            qr = qr_ref[:, sl]
            t = dg / f - dk
            dfl_ref[:, sl] = (t * (1.0 - lb) * sig * (1.0 - sig)).astype(BF16)
            acc_lb[0:1, sl] += jnp.sum(t * (1.0 - sig), axis=0, keepdims=True)
            dqr_ref[:, sl] = (dq * (sq * (1.0 + qr * (1.0 - sq)))).astype(BF16)

        @pl.when(c == nc - 1)
        def _():
            gl0 = acc_lb[0:1, :] * dlb_dl0
            glbl_ref[0:1, :] = gl0
            glbl_ref[1:2, :] = -gl0
            ggw_ref[...] = acc_gw[0:1, :]

    blk, vec, lb2, st_spec, consts = _hg_specs(nc, True)
    tris, masks = _hg_const_arrays()
    act = jax.ShapeDtypeStruct((S, HG_W), BF16)
    return pl.pallas_call(
        body, name="hg_bwd", grid=(ng, nc),
        in_specs=[blk(0), blk(ng), blk(0), blk(ng), blk(0), blk(0), st_spec, lb2, vec] + consts,
        out_specs=[blk(0), blk(0), blk(0), blk(0), lb2, vec],
        out_shape=[act, act, act, act, jax.ShapeDtypeStruct((2, HG_W), F32), jax.ShapeDtypeStruct((1, HG_W), F32)],
        scratch_shapes=[pltpu.VMEM((HG_HPS, 128, 128), F32), pltpu.VMEM((8, w), F32),
                        pltpu.VMEM((8, w), F32), pltpu.VMEM((8, w), F32)],
        compiler_params=_params(("parallel", "arbitrary")),
    )(p0, p0, p1, p1, o_raw, dm, states, lbl, gw, tris, masks)


def _att_bias():
    qi = np.arange(ATT_BLK)[:, None]
    kj = np.arange(2 * ATT_BLK)[None, :]
    band = (kj >= qi) & (kj <= qi + ATT_BLK)
    qm = np.stack([band & (kj >= ATT_BLK), band])
    cur = (kj < ATT_BLK) & (qi <= kj)
    km = np.stack([cur, cur | ((kj >= ATT_BLK) & (qi >= kj - ATT_BLK))])
    to_bias = lambda m: jnp.asarray(np.where(m, 0.0, NEG), F32)
    return to_bias(qm), to_bias(km)


def _att_fwd(qkv_d):
    d, L, _ = qkv_d.shape
    qb = ATT_QB
    rows = qb * ATT_BLK

    def body(q_ref, kp_ref, kc_ref, vp_ref, vc_ref, bias_ref, ol_ref, s_scr, p_scr):
        first = _iota((ATT_BLK, HEAD_PAIR), 1) < 64
        pairs = [slice(HEAD_PAIR * hp, HEAD_PAIR * (hp + 1)) for hp in range(4)]
        blk = lambda i: slice(ATT_BLK * i, ATT_BLK * (i + 1))

        def keys(i, sl, prev_ref, cur_ref):
            before = prev_ref[0, :, sl] if i == 0 else cur_ref[0, blk(i - 1), sl]
            return jnp.concatenate([before, cur_ref[0, blk(i), sl]], axis=0)

        for i in range(qb):
            for hp, sl in enumerate(pairs):
                q2 = q_ref[0, blk(i), sl] * ATT_SCALE
                zero = jnp.zeros_like(q2)
                qs = jnp.concatenate([jnp.where(first, q2, zero), jnp.where(first, zero, q2)], axis=0)
                s_scr[4 * i + hp] = _dot_nt(qs, keys(i, sl, kp_ref, kc_ref))
        stats = []
        for i in range(qb):
            bias = bias_ref[jnp.minimum(pl.program_id(1), 1)] if i == 0 else bias_ref[1]
            bias2 = jnp.concatenate([bias, bias], axis=0)
            for hp in range(4):
                s = s_scr[4 * i + hp] + bias2
                m = jnp.max(s, axis=-1, keepdims=True)
                p = jnp.exp(s - m)
                l = jnp.sum(p, axis=-1, keepdims=True)
                p_scr[4 * i + hp] = p.astype(BF16)
                stats.append((l, m + jnp.log(l)))
        for i in range(qb):
            for hp, sl in enumerate(pairs):
                l, lse = stats[4 * i + hp]
                o = _dot(p_scr[4 * i + hp], keys(i, sl, vp_ref, vc_ref)) / l
                ol_ref[0, blk(i), sl] = jnp.where(first, o[:ATT_BLK], o[ATT_BLK:])
                ol_ref[0, blk(i), AT_W + HEAD_PAIR * hp: AT_W + HEAD_PAIR * (hp + 1)] = jnp.where(first, lse[:ATT_BLK], lse[ATT_BLK:])

    cur = lambda j: pl.BlockSpec((1, rows, AT_W), lambda r, n: (r, n, j))
    prev = lambda j: pl.BlockSpec((1, ATT_BLK, AT_W), lambda r, n: (r, jnp.maximum(qb * n - 1, 0), j))
    return pl.pallas_call(
        body, name=f"att_fwd_d{d}", grid=(d, L // rows),
        in_specs=[cur(0), prev(1), cur(1), prev(2), cur(2), _full((2, ATT_BLK, 2 * ATT_BLK))],
        out_specs=pl.BlockSpec((1, rows, 2 * AT_W), lambda r, n: (r, n, 0)),
        out_shape=jax.ShapeDtypeStruct((d, L, 2 * AT_W), F32),
        scratch_shapes=[pltpu.VMEM((4 * qb, 2 * ATT_BLK, 2 * ATT_BLK), F32), pltpu.VMEM((4 * qb, 2 * ATT_BLK, 2 * ATT_BLK), BF16)],
        compiler_params=_params(("parallel", "parallel")),
    )(qkv_d, qkv_d, qkv_d, qkv_d, qkv_d, _att_bias()[0])


def _att_combine(ols, p3):
    S = p3.shape[0]
    tm = 512

    def body(a_ref, b4_ref, c16_ref, z_ref, ol_ref, mix_ref, mixt_ref, b_scr, c_scr):
        _merge_residues(b4_ref, b_scr, 4)
        _merge_residues(c16_ref, c_scr, 16)
        pats = (a_ref[...], _from_slabs(b_scr), _from_slabs(c_scr))
        os_ = [v[:, :AT_W] for v in pats]
        ls = [v[:, AT_W:] for v in pats]
        mx = jnp.maximum(jnp.maximum(ls[0], ls[1]), ls[2])
        es = [jnp.exp(l - mx) for l in ls]
        zs = es[0] + es[1] + es[2]
        o = (es[0] * os_[0] + es[1] * os_[1] + es[2] * os_[2]) / zs
        ol_ref[:, :AT_W] = o
        ol_ref[:, AT_W:] = mx + jnp.log(zs)
        z = z_ref[...]
        mixed = o * (z * _sig(z))
        mix_ref[...] = mixed.astype(BF16)
        mixt_ref[...] = mixed.T.astype(BF16)

    row = lambda w: pl.BlockSpec((tm, w), lambda s: (s, 0))
    return pl.pallas_call(
        body, name="att_combine", grid=(S // tm,),
        in_specs=[row(1024), _dil_spec(4, tm, 1024), _dil_spec(16, tm, 1024), pl.BlockSpec((tm, AT_W), lambda s: (s, 1))],
        out_specs=[row(1024), row(AT_W), pl.BlockSpec((AT_W, tm), lambda s: (0, s))],
        out_shape=[jax.ShapeDtypeStruct((S, 1024), F32), jax.ShapeDtypeStruct((S, AT_W), BF16),
                   jax.ShapeDtypeStruct((AT_W, S), BF16)],
        scratch_shapes=[_slab_scratch(tm, 1024), _slab_scratch(tm, 1024)],
        compiler_params=_params(("parallel",)),
    )(ols[0].reshape(S, 1024), ols[1], ols[2], p3)


AUX_W = 2 * AT_W
AUX_LSE_LANE = 32


def _att_gate_bwd(dm_at, ol, p3):
    S = p3.shape[0]
    tm = 512

    def body(dm_ref, ol_ref, z_ref, aux_ref, aux4_ref, aux16_ref, dz_ref, scr):
        o = ol_ref[:, :AT_W]
        z = z_ref[...]
        dm = dm_ref[...]
        sz = _sig(z)
        dz_ref[...] = (dm * o * (sz * (1.0 + z * (1.0 - sz)))).astype(BF16)
        do = dm * (z * sz)
        aux_ref[:, :AT_W] = do
        ones = ((_iota((HEAD_PAIR, HEAD_PAIR), 0) < 64) == (_iota((HEAD_PAIR, HEAD_PAIR), 1) < 64)).astype(F32)
        d_lanes = (_iota((tm, HEAD_PAIR), 1) & 63) < AUX_LSE_LANE
        prod = do * o
        for hp in range(4):
            sl = slice(HEAD_PAIR * hp, HEAD_PAIR * (hp + 1))
            lse = ol_ref[:, AT_W + HEAD_PAIR * hp: AT_W + HEAD_PAIR * (hp + 1)]
            aux_ref[:, AT_W + HEAD_PAIR * hp: AT_W + HEAD_PAIR * (hp + 1)] = jnp.where(d_lanes, _dot_hi(prod[:, sl], ones), lse)
        _to_slabs(aux_ref[...], scr)
        _split_residues(scr, aux4_ref, 4, F32)
        _split_residues(scr, aux16_ref, 16, F32)

    row = lambda w: pl.BlockSpec((tm, w), lambda s: (s, 0))
    aw = AUX_W
    return pl.pallas_call(
        body, name="att_gate_bwd", grid=(S // tm,),
        in_specs=[row(AT_W), row(1024), pl.BlockSpec((tm, AT_W), lambda s: (s, 1))],
        out_specs=[row(aw), _dil_spec(4, tm, aw), _dil_spec(16, tm, aw), row(AT_W)],
        out_shape=[jax.ShapeDtypeStruct((S, aw), F32), jax.ShapeDtypeStruct((4, S // 4, aw), F32),
                   jax.ShapeDtypeStruct((16, S // 16, aw), F32), jax.ShapeDtypeStruct((S, AT_W), BF16)],
        scratch_shapes=[_slab_scratch(tm, aw)],
        compiler_params=_params(("parallel",)),
    )(dm_at, ol, p3)


def _att_bwd(qkv_d, aux_d):
    d, L, _ = qkv_d.shape
    nb = L // ATT_BLK
    qb = ATT_QB
    rows = qb * ATT_BLK
    nsteps = L // rows

    def body(qc_ref, qn_ref, kp_ref, kc_ref, vp_ref, vc_ref, ac_ref, an_ref, bq_ref, bk_ref, dqkv_ref,
             s_scr, dp_scr, st_scr, dpt_scr, ds_scr, pt_scr, dst_scr):
        n = pl.program_id(1)
        first = _iota((ATT_BLK, HEAD_PAIR), 1) < 64
        pairs = [slice(HEAD_PAIR * hp, HEAD_PAIR * (hp + 1)) for hp in range(4)]
        blk = lambda i: slice(ATT_BLK * i, ATT_BLK * (i + 1))

        def stack(t):
            zero = jnp.zeros_like(t)
            return jnp.concatenate([jnp.where(first, t, zero), jnp.where(first, zero, t)], axis=0)

        def unstack(t2):
            return jnp.where(first, t2[:ATT_BLK], t2[ATT_BLK:])

        def with_prev(i, sl, prev_ref, cur_ref):
            before = prev_ref[0, :, sl] if i == 0 else cur_ref[0, blk(i - 1), sl]
            return jnp.concatenate([before, cur_ref[0, blk(i), sl]], axis=0)

        def with_next(i, sl, cur_ref, next_ref):
            after = next_ref[0, :, sl] if i == qb - 1 else cur_ref[0, blk(i + 1), sl]
            return jnp.concatenate([cur_ref[0, blk(i), sl], after], axis=0)

        for i in range(qb):
            for hp, sl in enumerate(pairs):
                j = 4 * i + hp
                s_scr[j] = _dot_nt(stack(qc_ref[0, blk(i), sl] * ATT_SCALE), with_prev(i, sl, kp_ref, kc_ref))
                dp_scr[j] = _dot_nt(stack(ac_ref[0, blk(i), sl].astype(BF16)), with_prev(i, sl, vp_ref, vc_ref))
                st_scr[j] = _dot_nt(stack(kc_ref[0, blk(i), sl] * ATT_SCALE), with_next(i, sl, qc_ref, qn_ref))
                dpt_scr[j] = _dot_nt(stack(vc_ref[0, blk(i), sl]), with_next(i, sl, ac_ref, an_ref).astype(BF16))
        for i in range(qb):
            bias = bq_ref[jnp.minimum(n, 1)] if i == 0 else bq_ref[1]
            bias_t = bk_ref[jnp.minimum(nsteps - 1 - n, 1)] if i == qb - 1 else bk_ref[1]
            bias2 = jnp.concatenate([bias, bias], axis=0)
            bias_t2 = jnp.concatenate([bias_t, bias_t], axis=0)
            for hp in range(4):
                j = 4 * i + hp
                sl_a = slice(AT_W + HEAD_PAIR * hp, AT_W + HEAD_PAIR * (hp + 1))
                dl_c = ac_ref[0, blk(i), sl_a]
                col = lambda t, o: jnp.concatenate([t[:, o: o + 1], t[:, 64 + o: 65 + o]], axis=0)
                p = jnp.exp(s_scr[j] + bias2 - col(dl_c, AUX_LSE_LANE))
                ds_scr[j] = (p * (dp_scr[j] - col(dl_c, 0))).astype(BF16)
                dl_t = with_next(i, sl_a, ac_ref, an_ref).T
                row = lambda t, o: jnp.concatenate([jnp.broadcast_to(t[o: o + 1, :], (ATT_BLK, 2 * ATT_BLK)),
                                                    jnp.broadcast_to(t[64 + o: 65 + o, :], (ATT_BLK, 2 * ATT_BLK))], axis=0)
                pt = jnp.exp(st_scr[j] + bias_t2 - row(dl_t, AUX_LSE_LANE))
                pt_scr[j] = pt.astype(BF16)
                dst_scr[j] = (pt * (dpt_scr[j] - row(dl_t, 0))).astype(BF16)
        for i in range(qb):
            for hp, sl in enumerate(pairs):
                j = 4 * i + hp
                dq = unstack(_dot(ds_scr[j], with_prev(i, sl, kp_ref, kc_ref))) * ATT_SCALE
                dk = unstack(_dot(dst_scr[j], with_next(i, sl, qc_ref, qn_ref))) * ATT_SCALE
                dv = unstack(_dot(pt_scr[j], with_next(i, sl, ac_ref, an_ref).astype(BF16)))
                dqkv_ref[0, blk(i), sl] = dq.astype(BF16)
                dqkv_ref[0, blk(i), AT_W + HEAD_PAIR * hp: AT_W + HEAD_PAIR * (hp + 1)] = dk.astype(BF16)
                dqkv_ref[0, blk(i), 2 * AT_W + HEAD_PAIR * hp: 2 * AT_W + HEAD_PAIR * (hp + 1)] = dv.astype(BF16)

    cur = lambda j: pl.BlockSpec((1, rows, AT_W), lambda r, n: (r, n, j))
    prev = lambda j: pl.BlockSpec((1, ATT_BLK, AT_W), lambda r, n: (r, jnp.maximum(qb * n - 1, 0), j))
    nxt_blk = lambda n: jnp.minimum(qb * (n + 1), nb - 1)
    sq = (4 * qb, 2 * ATT_BLK, 2 * ATT_BLK)
    return pl.pallas_call(
        body, name=f"att_bwd_d{d}", grid=(d, nsteps),
        in_specs=[cur(0), pl.BlockSpec((1, ATT_BLK, AT_W), lambda r, n: (r, nxt_blk(n), 0)), prev(1), cur(1), prev(2), cur(2),
                  pl.BlockSpec((1, rows, AUX_W), lambda r, n: (r, n, 0)),
                  pl.BlockSpec((1, ATT_BLK, AUX_W), lambda r, n: (r, nxt_blk(n), 0)),
                  _full((2, ATT_BLK, 2 * ATT_BLK)), _full((2, ATT_BLK, 2 * ATT_BLK))],
        out_specs=pl.BlockSpec((1, rows, 3 * AT_W), lambda r, n: (r, n, 0)),
        out_shape=jax.ShapeDtypeStruct((d, L, 3 * AT_W), BF16),
        scratch_shapes=[pltpu.VMEM(sq, F32)] * 4 + [pltpu.VMEM(sq, BF16)] * 3,
        compiler_params=_params(("parallel", "parallel")),
    )(qkv_d, qkv_d, qkv_d, qkv_d, qkv_d, qkv_d, aux_d, aux_d, *_att_bias())


def _att_bwd_combine(dqkvs, cos2, sin2):
    S = dqkvs[0].shape[1]
    tm = 512

    def body(a_ref, b4_ref, c16_ref, cos_ref, sin_ref, dq_ref, dk_ref, dv_ref, b_scr, c_scr):
        _merge_residues(b4_ref, b_scr, 4)
        _merge_residues(c16_ref, c_scr, 16)
        t = a_ref[...].astype(F32) + _from_slabs(b_scr) + _from_slabs(c_scr)
        dy = t[:, : 2 * AT_W]
        cosf = jnp.tile(cos_ref[...], (1, 8))
        sinf = jnp.tile(sin_ref[...], (1, 8))
        dx = dy * cosf - _rope_rot(dy) * sinf
        dq_ref[...] = dx[:, :AT_W].astype(BF16)
        dk_ref[...] = dx[:, AT_W:].astype(BF16)
        dv_ref[...] = t[:, 2 * AT_W:].astype(BF16)

    row = lambda w: pl.BlockSpec((tm, w), lambda s: (s, 0))
    act = jax.ShapeDtypeStruct((S, AT_W), BF16)
    return pl.pallas_call(
        body, name="att_bwd_combine", grid=(S // tm,),
        in_specs=[row(3 * AT_W), _dil_spec(4, tm, 3 * AT_W), _dil_spec(16, tm, 3 * AT_W), row(128), row(128)],
        out_specs=[row(AT_W), row(AT_W), row(AT_W)],
        out_shape=[act, act, act],
        scratch_shapes=[_slab_scratch(tm, 3 * AT_W), _slab_scratch(tm, 3 * AT_W)],
        compiler_params=_params(("parallel",)),
    )(dqkvs[0].reshape(S, 3 * AT_W), dqkvs[1], dqkvs[2], cos2, sin2)


def _outproj(x2, tgt2, mix_hg, mix_at, mixt_hg, mixt_at, w_out_full, fnw):
    S = x2.shape[0]
    tm = 256
    ns = S // tm

    def body(x_ref, t_ref, mh_ref, ma_ref, mht_ref, mat_ref, w_ref, fw_ref,
             dh_ref, dmh_ref, dma_ref, gw_ref, gfw_ref, loss_ref):
        s = pl.program_id(0)

        @pl.when(s == 0)
        def _():
            gw_ref[...] = jnp.zeros_like(gw_ref)
            gfw_ref[...] = jnp.zeros_like(gfw_ref)
            loss_ref[...] = jnp.zeros_like(loss_ref)

        y = _dot(mh_ref[...], w_ref[:HG_W, :]) + _dot(ma_ref[...], w_ref[HG_W:, :])
        h = x_ref[...] + y
        r = lax.rsqrt(jnp.mean(h * h, axis=-1, keepdims=True) + EPS)
        hn = h * r
        fw = fw_ref[...]
        err = hn * fw - t_ref[...]
        loss_ref[...] += 0.5 * jnp.sum(jnp.mean(err * err, axis=-1, keepdims=True))
        dout = err * (1.0 / D_MODEL)
        gfw_ref[...] += jnp.sum(dout * hn, axis=0, keepdims=True)
        dhn = dout * fw
        dh = r * (dhn - hn * jnp.mean(dhn * hn, axis=-1, keepdims=True))
        dh_ref[...] = dh
        dhb = dh.astype(BF16)
        dmh_ref[...] = _dot_nt(dhb, w_ref[:HG_W, :])
        dma_ref[...] = _dot_nt(dhb, w_ref[HG_W:, :])
        gw_ref[:HG_W, :] += _dot(mht_ref[...], dhb)
        gw_ref[HG_W:, :] += _dot(mat_ref[...], dhb)

    row = lambda w: pl.BlockSpec((tm, w), lambda s: (s, 0))
    colb = pl.BlockSpec((HG_W, tm), lambda s: (0, s))
    return pl.pallas_call(
        body, name="outproj", grid=(ns,),
        in_specs=[row(D_MODEL), row(D_MODEL), row(HG_W), row(AT_W), colb, colb,
                  _full((D_MODEL, D_MODEL)), _full((1, D_MODEL))],
        out_specs=[row(D_MODEL), row(HG_W), row(AT_W), _full((D_MODEL, D_MODEL)), _full((1, D_MODEL)), _full((8, 128))],
        out_shape=[jax.ShapeDtypeStruct((S, D_MODEL), F32), jax.ShapeDtypeStruct((S, HG_W), F32),
                   jax.ShapeDtypeStruct((S, AT_W), F32), jax.ShapeDtypeStruct((D_MODEL, D_MODEL), F32),
                   jax.ShapeDtypeStruct((1, D_MODEL), F32), jax.ShapeDtypeStruct((8, 128), F32)],
        compiler_params=_params(("arbitrary",)),
    )(x2, tgt2, mix_hg, mix_at, mixt_hg, mixt_at, w_out_full, fnw)


def _inproj_bwd_x(dps, w_in_full, x2, norm_w, dh):
    S = x2.shape[0]
    tm = 256

    def body(d0, d1, d2, d3, d4, d5, d6, d7, w_ref, x_ref, nw_ref, dh_ref, gx_ref, gnw_ref):
        s = pl.program_id(0)

        @pl.when(s == 0)
        def _():
            gnw_ref[...] = jnp.zeros_like(gnw_ref)

        du = jnp.zeros((tm, D_MODEL), F32)
        for i, dref in enumerate((d0, d1, d2, d3, d4, d5, d6, d7)):
            j, half = divmod(i, 2)
            du = du + _dot_nt(dref[...], w_ref[j, :, 512 * half: 512 * (half + 1)])
        x = x_ref[...]
        r = lax.rsqrt(jnp.mean(x * x, axis=-1, keepdims=True) + EPS)
        xh = x * r
        gnw_ref[...] += jnp.sum(du * xh, axis=0, keepdims=True)
        dun = du * nw_ref[...]
        gx_ref[...] = dh_ref[...] + r * (dun - xh * jnp.mean(dun * xh, axis=-1, keepdims=True))

    row = lambda w: pl.BlockSpec((tm, w), lambda s: (s, 0))
    return pl.pallas_call(
        body, name="inproj_bwd_x", grid=(S // tm,),
        in_specs=[row(512)] * 8 + [_full((4, D_MODEL, 1024)), row(D_MODEL), _full((1, D_MODEL)), row(D_MODEL)],
        out_specs=[row(D_MODEL), _full((1, D_MODEL))],
        out_shape=[jax.ShapeDtypeStruct((S, D_MODEL), F32), jax.ShapeDtypeStruct((1, D_MODEL), F32)],
        compiler_params=_params(("arbitrary",)),
    )(*dps, w_in_full, x2, norm_w, dh)


def _inproj_bwd_w(ut, dps):
    S = ut.shape[1]
    tm = 256

    def body(ut_ref, d0, d1, d2, d3, d4, d5, d6, d7, g_ref):
        @pl.when(pl.program_id(0) == 0)
        def _():
            g_ref[...] = jnp.zeros_like(g_ref)

        utb = ut_ref[...]
        for i, dref in enumerate((d0, d1, d2, d3, d4, d5, d6, d7)):
            j, half = divmod(i, 2)
            g_ref[j, :, 512 * half: 512 * (half + 1)] += _dot(utb, dref[...])

    return pl.pallas_call(
        body, name="inproj_bwd_w", grid=(S // tm,),
        in_specs=[pl.BlockSpec((D_MODEL, tm), lambda s: (0, s))] + [pl.BlockSpec((tm, 512), lambda s: (s, 0))] * 8,
        out_specs=_full((4, D_MODEL, 1024)),
        out_shape=jax.ShapeDtypeStruct((4, D_MODEL, 1024), F32),
        compiler_params=_params(("arbitrary",)),
    )(ut, *dps)


def _adamw_update(gg, w_ref, m_ref, v_ref, d_ref, nm_ref, nv_ref):
    nm = ADAM_B1 * m_ref[...] + (1.0 - ADAM_B1) * gg
    nv = ADAM_B2 * v_ref[...] + (1.0 - ADAM_B2) * (gg * gg)
    m_hat = nm / (1.0 - ADAM_B1 ** ADAM_STEP)
    v_hat = nv / (1.0 - ADAM_B2 ** ADAM_STEP)
    d_ref[...] = -ADAM_LR * (m_hat / (jnp.sqrt(v_hat) + ADAM_EPS) + ADAM_WD * w_ref[...])
    nm_ref[...] = nm
    nv_ref[...] = nv


def _adamw(w, g, m, v, name):
    rows, cols = w.shape
    tr = min(rows, 256)

    def body(w_ref, g_ref, m_ref, v_ref, d_ref, nm_ref, nv_ref):
        _adamw_update(g_ref[...], w_ref, m_ref, v_ref, d_ref, nm_ref, nv_ref)

    spec = pl.BlockSpec((tr, cols), lambda i: (i, 0))
    sds = jax.ShapeDtypeStruct((rows, cols), F32)
    return pl.pallas_call(
        body, name=name, grid=(rows // tr,),
        in_specs=[spec] * 4, out_specs=[spec] * 3, out_shape=[sds] * 3,
        compiler_params=_params(("parallel",)),
    )(w, g, m, v)


def _place():
    return lax.axis_index("x"), lax.axis_index("y"), lax.axis_index("c")


def _gather_weights(w_in_s, w_out_s):
    def body(win_ref, wout_ref, fin_ref, fout_ref, send_sems, recv_sems):
        x, y, c = _place()
        me = (x, y, c)
        sib = (x, y, 1 - c)
        mine = 2 * x + y
        fin_ref[mine] = win_ref[...].astype(BF16)
        fout_ref[mine] = wout_ref[...].astype(BF16)
        chips = [(1 - x, y), (x, 1 - y), (1 - x, 1 - y)]

        def halves(chip, half):
            return (fin_ref.at[chip, pl.ds(half * 512, 512), :], fout_ref.at[chip, pl.ds(half * 128, 128), :])

        def copy(k, ref, to):
            return pltpu.make_async_remote_copy(src_ref=ref, dst_ref=ref, send_sem=send_sems.at[k],
                                                recv_sem=recv_sems.at[k], device_id=to, device_id_type=MESH)

        first, passed = [], []
        for j, (cx, cy) in enumerate(chips):
            for a, ref in enumerate(halves(mine, c)):
                first.append(copy(2 * j + a, ref, (cx, cy, c)))
        for cp in first:
            cp.start()
        for j, (cx, cy) in enumerate(chips):
            for a, ref in enumerate(halves(2 * cx + cy, c)):
                copy(2 * j + a, ref, me).wait_recv()
                fwd = copy(6 + 2 * j + a, ref, sib)
                fwd.start()
                passed.append(fwd)
        for j, (cx, cy) in enumerate(chips):
            for a, ref in enumerate(halves(2 * cx + cy, 1 - c)):
                copy(6 + 2 * j + a, ref, me).wait_recv()
        for cp in first + passed:
            cp.wait_send()

    vm = pl.BlockSpec(memory_space=pltpu.VMEM)
    return pl.pallas_call(
        body, name="gather_weights",
        in_specs=[vm, vm], out_specs=[vm, vm],
        out_shape=[jax.ShapeDtypeStruct((4, D_MODEL, 1024), BF16), jax.ShapeDtypeStruct((4, 256, D_MODEL), BF16)],
        scratch_shapes=[pltpu.SemaphoreType.DMA((12,)), pltpu.SemaphoreType.DMA((12,))],
        compiler_params=pltpu.CompilerParams(vmem_limit_bytes=VMEM_LIMIT),
    )(w_in_s, w_out_s)


def _swap_halves(g_in, g_out):
    def body(gin_ref, gout_ref, rin_ref, rout_ref, send_sems, recv_sems):
        x, y, c = _place()
        sib = (x, y, 1 - c)
        cps = [pltpu.make_async_remote_copy(src_ref=src.at[:, 1 - c], dst_ref=dst, send_sem=send_sems.at[k],
                                            recv_sem=recv_sems.at[k], device_id=sib, device_id_type=MESH)
               for k, (src, dst) in enumerate(((gin_ref, rin_ref), (gout_ref, rout_ref)))]
        for cp in cps:
            cp.start()
        for cp in cps:
            cp.wait()

    hbm = pl.BlockSpec(memory_space=pl.ANY)
    return pl.pallas_call(
        body, name="swap_halves",
        in_specs=[hbm, hbm], out_specs=[hbm, hbm],
        out_shape=[jax.ShapeDtypeStruct((4,) + g.shape[2:], F32) for g in (g_in, g_out)],
        scratch_shapes=[pltpu.SemaphoreType.DMA((2,)), pltpu.SemaphoreType.DMA((2,))],
    )(g_in, g_out)


def _add_half(g, r, cidx, name):
    n, _, rows, cols = g.shape

    def body(c_ref, g_ref, r_ref, o_ref):
        o_ref[0] = (g_ref[0, 0] + r_ref[0]).astype(BF16)

    return pl.pallas_call(
        body, name=name,
        grid_spec=pltpu.PrefetchScalarGridSpec(
            num_scalar_prefetch=1, grid=(n,),
            in_specs=[pl.BlockSpec((1, 1, rows, cols), lambda j, c_ref: (j, c_ref[0], 0, 0)),
                      pl.BlockSpec((1, rows, cols), lambda j, c_ref: (j, 0, 0))],
            out_specs=pl.BlockSpec((1, rows, cols), lambda j, c_ref: (j, 0, 0))),
        out_shape=jax.ShapeDtypeStruct((n, rows, cols), BF16),
        compiler_params=_params(("parallel",)),
    )(cidx, g, r)


def _exchange_chips(cs_in, cs_out):
    def body(in_ref, out_ref, rin_ref, rout_ref, send_sems, recv_sems):
        x, y, c = _place()
        chips = [(1 - x, y), (x, 1 - y), (1 - x, 1 - y)]
        cps = []
        for k, (cx, cy) in enumerate(chips):
            for a, (src, dst) in enumerate(((in_ref, rin_ref), (out_ref, rout_ref))):
                cps.append(pltpu.make_async_remote_copy(
                    src_ref=src.at[2 * cx + cy], dst_ref=dst.at[k], send_sem=send_sems.at[2 * k + a],
                    recv_sem=recv_sems.at[2 * k + a], device_id=(cx, cy, c), device_id_type=MESH))
        for cp in cps:
            cp.start()
        for cp in cps:
            cp.wait()

    hbm = pl.BlockSpec(memory_space=pl.ANY)
    return pl.pallas_call(
        body, name="exchange_chips",
        in_specs=[hbm, hbm], out_specs=[hbm, hbm],
        out_shape=[jax.ShapeDtypeStruct((3,) + a.shape[1:], a.dtype) for a in (cs_in, cs_out)],
        scratch_shapes=[pltpu.SemaphoreType.DMA((6,)), pltpu.SemaphoreType.DMA((6,))],
    )(cs_in, cs_out)


PEER_XOR = (2, 1, 3)


def _sum_chips(cs, r, chip_idx, name):
    _, rows, cols = r.shape
    tr = min(rows, 256)

    def body(m_ref, cs_ref, r_ref, o_ref):
        mine = m_ref[0]
        own = cs_ref[0].astype(F32)
        got = [r_ref[k].astype(F32) for k in range(3)]
        acc = None
        for s in range(4):
            rel = mine ^ s
            term = jnp.where(rel == 0, own, jnp.where(rel == PEER_XOR[0], got[0],
                                                      jnp.where(rel == PEER_XOR[1], got[1], got[2])))
            acc = term if acc is None else acc + term
        o_ref[...] = acc

    return pl.pallas_call(
        body, name=name,
        grid_spec=pltpu.PrefetchScalarGridSpec(
            num_scalar_prefetch=1, grid=(rows // tr,),
            in_specs=[pl.BlockSpec((1, tr, cols), lambda i, m_ref: (m_ref[0], i, 0)),
                      pl.BlockSpec((3, tr, cols), lambda i, m_ref: (0, i, 0))],
            out_specs=pl.BlockSpec((tr, cols), lambda i, m_ref: (i, 0))),
        out_shape=jax.ShapeDtypeStruct((rows, cols), F32),
        compiler_params=_params(("parallel",)),
    )(chip_idx, cs, r)


def _swap_reduced(h_in, h_out):
    def body(in_ref, out_ref, rin_ref, rout_ref, send_sems, recv_sems):
        x, y, c = _place()
        cps = [pltpu.make_async_remote_copy(src_ref=src, dst_ref=dst, send_sem=send_sems.at[k],
                                            recv_sem=recv_sems.at[k], device_id=(x, y, 1 - c), device_id_type=MESH)
               for k, (src, dst) in enumerate(((in_ref, rin_ref), (out_ref, rout_ref)))]
        for cp in cps:
            cp.start()
        for cp in cps:
            cp.wait()

    hbm = pl.BlockSpec(memory_space=pl.ANY)
    return pl.pallas_call(
        body, name="swap_reduced",
        in_specs=[hbm, hbm], out_specs=[hbm, hbm],
        out_shape=[jax.ShapeDtypeStruct(h.shape, F32) for h in (h_in, h_out)],
        scratch_shapes=[pltpu.SemaphoreType.DMA((2,)), pltpu.SemaphoreType.DMA((2,))],
    )(h_in, h_out)


def _adamw_halves(w, mine, theirs, m, v, cidx, name):
    rows, cols = w.shape
    half = rows // 2
    tr = min(half, 256)
    nbh = half // tr

    def body(c_ref, w_ref, a_ref, b_ref, m_ref, v_ref, g_ref, d_ref, nm_ref, nv_ref):
        gg = jnp.where(pl.program_id(0) // nbh == c_ref[0], a_ref[...], b_ref[...])
        g_ref[...] = gg
        _adamw_update(gg, w_ref, m_ref, v_ref, d_ref, nm_ref, nv_ref)

    spec = pl.BlockSpec((tr, cols), lambda i, c_ref: (i, 0))
    hspec = pl.BlockSpec((tr, cols), lambda i, c_ref: (i % nbh, 0))
    sds = jax.ShapeDtypeStruct((rows, cols), F32)
    return pl.pallas_call(
        body, name=name,
        grid_spec=pltpu.PrefetchScalarGridSpec(
            num_scalar_prefetch=1, grid=(rows // tr,),
            in_specs=[spec, hspec, hspec, spec, spec], out_specs=[spec] * 4),
        out_shape=[sds] * 4,
        compiler_params=_params(("parallel",)),
    )(cidx, w, mine, theirs, m, v)


def _allreduce_small(g_nw, g_fw, g_hgw, g_lbl, loss8):
    def body(nw_ref, fw_ref, hgw_ref, lbl_ref, loss_ref, out_ref, slots, send_sems, recv_sems):
        x, y, c = _place()
        me = 4 * x + 2 * y + c
        slots[me] = jnp.zeros((8, D_MODEL), F32)
        slots[me, 0:1, :] = nw_ref[...]
        slots[me, 1:2, :] = fw_ref[...]
        slots[me, 2:3, 0:HG_W] = hgw_ref[...]
        slots[me, 3:4, 0:HG_W] = lbl_ref[0:1, :]
        slots[me, 3:4, HG_W:] = lbl_ref[1:2, :]
        slots[me, 4:5, 0:128] = loss_ref[0:1, :]
        cps = []
        for k in range(1, 8):
            dx, dy, dc = (k >> 2) & 1, (k >> 1) & 1, k & 1
            to = (x ^ dx, y ^ dy, c ^ dc)
            cps.append(pltpu.make_async_remote_copy(
                src_ref=slots.at[me], dst_ref=slots.at[me], send_sem=send_sems.at[k - 1],
                recv_sem=recv_sems.at[k - 1], device_id=to, device_id_type=MESH))
        for cp in cps:
            cp.start()
        for cp in cps:
            cp.wait()
        acc = slots[0]
        for i in range(1, 8):
            acc = acc + slots[i]
        out_ref[...] = acc

    vm = pl.BlockSpec(memory_space=pltpu.VMEM)
    return pl.pallas_call(
        body, name="allreduce_small",
        in_specs=[vm] * 5, out_specs=vm,
        out_shape=jax.ShapeDtypeStruct((8, D_MODEL), F32),
        scratch_shapes=[pltpu.VMEM((8, 8, D_MODEL), F32), pltpu.SemaphoreType.DMA((7,)), pltpu.SemaphoreType.DMA((7,))],
    )(g_nw, g_fw, g_hgw, g_lbl, loss8)


def _rope_tables(S):
    inv_freq = 1.0 / (ROPE_THETA ** (jnp.arange(ROPE_HALF, dtype=F32) / ROPE_HALF))
    ang = jnp.arange(S, dtype=jnp.int32).astype(F32)[:, None] * inv_freq[None, :]
    cos, sin = jnp.cos(ang), jnp.sin(ang)
    cos2 = jnp.concatenate([cos, cos, cos, cos], axis=-1)
    sin2 = jnp.concatenate([-sin, sin, -sin, sin], axis=-1)
    return cos2, sin2


def _local_step(x2, tgt2, norm_w, w_in_full, lbl, hg_norm_w, w_out_full, fnw):
    S = x2.shape[0]
    cos2, sin2 = _rope_tables(S)
    p0, p1, qkv, qkv4, qkv16, p3, ut = _inproj(x2, norm_w, w_in_full, cos2, sin2)
    o_hg, mix_hg, mixt_hg, states = _hg_fwd(p0, p1, lbl, hg_norm_w)
    qkv_ds = [qkv.reshape(1, S, 3 * AT_W), qkv4, qkv16]
    ols = [_att_fwd(q) for q in qkv_ds]
    ol, mix_at, mixt_at = _att_combine(ols, p3)
    dh, dm_hg, dm_at, g_wout, g_fw, loss8 = _outproj(x2, tgt2, mix_hg, mix_at, mixt_hg, mixt_at, w_out_full, fnw)
    dqr, dfl, dv_hg, dz_hg, g_lbl, g_hgw = _hg_bwd(p0, p1, o_hg, dm_hg, states, lbl, hg_norm_w)
    aux, aux4, aux16, dz_at = _att_gate_bwd(dm_at, ol, p3)
    aux_ds = [aux.reshape(1, S, AUX_W), aux4, aux16]
    dqkvs = [_att_bwd(q, a) for q, a in zip(qkv_ds, aux_ds)]
    dq_at, dk_at, dv_at = _att_bwd_combine(dqkvs, cos2, sin2)
    dps = [dqr, dfl, dv_hg, dz_hg, dq_at, dk_at, dv_at, dz_at]
    grad_x, g_nw = _inproj_bwd_x(dps, w_in_full, x2, norm_w, dh)
    g_win = _inproj_bwd_w(ut, dps)
    return loss8, grad_x, g_nw, g_win, g_lbl, g_hgw, g_wout, g_fw


def kernel(x, norm_w, w_in, hgrn_lb_logits, hg_norm_w, w_out, final_norm_w, loss_target, m_norm_w, m_w_in, m_hgrn_lb_logits, m_hg_norm_w, m_w_out, m_final_norm_w, v_norm_w, v_w_in, v_hgrn_lb_logits, v_hg_norm_w, v_w_out, v_final_norm_w):
    S = x.shape[1]
    w_in_full, w_out_full = _gather_weights(w_in[0], w_out[0])
    loss8, grad_x, g_nw, g_win, g_lbl, g_hgw, g_wout, g_fw = _local_step(
        x[0], loss_target[0], norm_w, w_in_full, hgrn_lb_logits, hg_norm_w,
        w_out_full.reshape(D_MODEL, D_MODEL), final_norm_w.reshape(1, D_MODEL))

    cidx = lax.axis_index("c").astype(jnp.int32).reshape(1)
    g_in4 = g_win.reshape(4, 2, 512, 1024)
    g_out4 = g_wout.reshape(4, 2, 128, D_MODEL)
    r_in, r_out = _swap_halves(g_in4, g_out4)
    cs_in = _add_half(g_in4, r_in, cidx, "add_half_in")
    cs_out = _add_half(g_out4, r_out, cidx, "add_half_out")
    x_in, x_out = _exchange_chips(cs_in, cs_out)
    chip_idx = (2 * lax.axis_index("x") + lax.axis_index("y")).astype(jnp.int32).reshape(1)
    h_in = _sum_chips(cs_in, x_in, chip_idx, "sum_chips_in")
    h_out = _sum_chips(cs_out, x_out, chip_idx, "sum_chips_out")
    s_in, s_out = _swap_reduced(h_in, h_out)

    red = _allreduce_small(g_nw, g_fw, g_hgw, g_lbl, loss8)
    loss = red[4, 0]
    grad_norm_w = red[0:1, :]
    grad_final_norm_w = red[1, :]
    grad_hg_norm_w = red[2:3, :HG_W]
    grad_lbl = jnp.concatenate([red[3:4, :HG_W], red[3:4, HG_W:]], axis=0)

    d_nw, m_nw, v_nw = _adamw(norm_w, grad_norm_w, m_norm_w, v_norm_w, "adamw_norm_w")
    grad_w_in, d_win, m_win, v_win = _adamw_halves(w_in[0], h_in, s_in, m_w_in[0], v_w_in[0], cidx, "adamw_w_in")
    d_lbl, m_lbl, v_lbl = _adamw(hgrn_lb_logits, grad_lbl, m_hgrn_lb_logits, v_hgrn_lb_logits, "adamw_lb_logits")
    d_hgw, m_hgw, v_hgw = _adamw(hg_norm_w, grad_hg_norm_w, m_hg_norm_w, v_hg_norm_w, "adamw_hg_norm_w")
    grad_w_out, d_wout, m_wout, v_wout = _adamw_halves(w_out[0], h_out, s_out, m_w_out[0], v_w_out[0], cidx, "adamw_w_out")
    d_fw, m_fw, v_fw = _adamw(final_norm_w.reshape(1, D_MODEL), grad_final_norm_w.reshape(1, D_MODEL),
                              m_final_norm_w.reshape(1, D_MODEL), v_final_norm_w.reshape(1, D_MODEL), "adamw_final_norm_w")
    e1 = lambda a: a[None]
    flat = lambda a: a.reshape(D_MODEL)
    return (loss, grad_x[None], grad_norm_w, e1(grad_w_in), grad_lbl, grad_hg_norm_w, e1(grad_w_out), grad_final_norm_w,
            d_nw, e1(d_win), d_lbl, d_hgw, e1(d_wout), flat(d_fw),
            m_nw, e1(m_win), m_lbl, m_hgw, e1(m_wout), flat(m_fw),
            v_nw, e1(v_win), v_lbl, v_hgw, e1(v_wout), flat(v_fw))
```

```python
import functools

import jax
import jax.numpy as jnp
import numpy as np
from jax import lax
from jax.experimental import pallas as pl
from jax.experimental.pallas import tpu as pltpu

F32 = jnp.float32
BF16 = jnp.bfloat16
MESH = pl.DeviceIdType.MESH

D_MODEL = 1024
HG_W = 512
AT_W = 512
HEAD_PAIR = 128
ROPE_HALF = 32
ROPE_THETA = 10000.0
EPS = 1e-6
CHUNK = 128
LEVELS = (64, 32, 16, 8, 4, 2, 1)
DIAG = 1
SUBLANES = 8
ATT_BLK = 128
ATT_QB = 4
DILATIONS = (1, 4, 16)
ATT_SCALE = 0.125
STAT_W = 128
STAT_LANES = 16
STAT_LSE_LANE = 8
NEG = -1e30
VMEM_LIMIT = 56 * 1024 * 1024

ADAM_LR = 0.001
ADAM_B1 = 0.9
ADAM_B2 = 0.999
ADAM_EPS = 1e-08
ADAM_WD = 0.01
ADAM_STEP = 10


def _iota(shape, dim):
    return lax.broadcasted_iota(jnp.int32, shape, dim)


def _dot(a, b):
    return jnp.dot(a, b, preferred_element_type=F32)


def _dot_nt(a, b):
    return lax.dot_general(a, b, (((1,), (1,)), ((), ())), preferred_element_type=F32)


def _dot_hi(a, b):
    return jnp.dot(a, b, preferred_element_type=F32, precision=lax.Precision.HIGHEST)


def _sig(v):
    return 1.0 / (1.0 + jnp.exp(-v))


def _params(sem=None, vmem=VMEM_LIMIT):
    return pltpu.CompilerParams(dimension_semantics=sem, vmem_limit_bytes=vmem)


def _full(shape):
    n = len(shape)
    return pl.BlockSpec(shape, lambda *_: (0,) * n)


def _rope_rot(y):
    n = y.shape[1]
    first = (_iota(y.shape, 1) & (2 * ROPE_HALF - 1)) < ROPE_HALF
    return jnp.where(first, pltpu.roll(y, n - ROPE_HALF, 1), pltpu.roll(y, ROPE_HALF, 1))


def _dil_spec(d, tm, width):
    return pl.BlockSpec((d, tm // d, width), lambda s: (0, s, 0))


LANES = 128


def _slab_scratch(tm, width):
    return pltpu.VMEM((width // LANES, tm, LANES), F32)


def _to_slabs(v, slabs_ref):
    for j in range(slabs_ref.shape[0]):
        slabs_ref[j] = v[:, LANES * j: LANES * (j + 1)]


def _from_slabs(slabs_ref):
    return jnp.concatenate([slabs_ref[j] for j in range(slabs_ref.shape[0])], axis=1)


def _split_residues(slabs_ref, dst_ref, d, dtype):
    nslab, tm, _ = slabs_ref.shape
    for r in range(d):
        for j in range(nslab):
            dst_ref[r, :, LANES * j: LANES * (j + 1)] = slabs_ref[j, pl.ds(r, tm // d, stride=d), :].astype(dtype)


def _merge_residues(src_ref, slabs_ref, d):
    nslab, tm, _ = slabs_ref.shape
    for r in range(d):
        for j in range(nslab):
            slabs_ref[j, pl.ds(r, tm // d, stride=d), :] = src_ref[r, :, LANES * j: LANES * (j + 1)].astype(F32)


def _inproj(x2, norm_w, w_in_full, cos2, sin2):
    S = x2.shape[0]
    tm = 256

    def body(x_ref, nw_ref, w_ref, cos_ref, sin_ref, p0_ref, p1_ref, qkv_ref, qkv4_ref, qkv16_ref, p3_ref, ut_ref, scr):
        x = x_ref[...]
        r = lax.rsqrt(jnp.mean(x * x, axis=-1, keepdims=True) + EPS)
        u = x * r * nw_ref[...]
        ub = u.astype(BF16)
        ut_ref[...] = u.T.astype(BF16)
        p0_ref[...] = _dot(ub, w_ref[0])
        p1_ref[...] = _dot(ub, w_ref[1])
        y2 = _dot(ub, w_ref[2])
        cosf = jnp.tile(cos_ref[...], (1, 8))
        sinf = jnp.tile(sin_ref[...], (1, 8))
        y3 = _dot(ub, w_ref[3])
        p3_ref[...] = y3
        qkv = jnp.concatenate([y2 * cosf + _rope_rot(y2) * sinf, y3[:, :AT_W]], axis=1)
        qkv_ref[...] = qkv.astype(BF16)
        _to_slabs(qkv, scr)
        _split_residues(scr, qkv4_ref, 4, BF16)
        _split_residues(scr, qkv16_ref, 16, BF16)

    row = lambda w: pl.BlockSpec((tm, w), lambda s: (s, 0))
    qkv_w = 3 * AT_W
    return pl.pallas_call(
        body, name="inproj", grid=(S // tm,),
        in_specs=[row(D_MODEL), _full((1, D_MODEL)), _full((4, D_MODEL, 1024)), row(128), row(128)],
        out_specs=[row(1024), row(1024), row(qkv_w), _dil_spec(4, tm, qkv_w), _dil_spec(16, tm, qkv_w), row(1024),
                   pl.BlockSpec((D_MODEL, tm), lambda s: (0, s))],
        out_shape=[jax.ShapeDtypeStruct((S, 1024), F32), jax.ShapeDtypeStruct((S, 1024), F32),
                   jax.ShapeDtypeStruct((S, qkv_w), BF16), jax.ShapeDtypeStruct((4, S // 4, qkv_w), BF16),
                   jax.ShapeDtypeStruct((16, S // 16, qkv_w), BF16), jax.ShapeDtypeStruct((S, 1024), F32),
                   jax.ShapeDtypeStruct((D_MODEL, S), BF16)],
        scratch_shapes=[_slab_scratch(tm, qkv_w)],
        compiler_params=_params(("parallel",)),
    )(x2, norm_w, w_in_full, cos2, sin2)


HG_HPS = 4
N_LEV = len(LEVELS)


def _hg_const_arrays():
    r = np.arange(CHUNK)[:, None]
    c = np.arange(CHUNK)[None, :]
    tris = np.stack([r >= c, r <= c])
    lm = [((r // (2 * m)) == (c // (2 * m))) & (r % (2 * m) >= m) & (c % (2 * m) < m) for m in LEVELS]
    dm = [(c == r - dl) & (r % DIAG >= dl) for dl in range(DIAG)]
    masks = np.stack(lm + [x.T for x in lm] + dm)
    return jnp.asarray(tris, BF16), jnp.asarray(masks, F32)


def _split2(a):
    hi = a.astype(BF16)
    return hi, (a - hi.astype(F32)).astype(BF16)


def _dot3(a, b):
    ah, al = _split2(a)
    bh, bl = _split2(b)
    n = b.shape[1]
    p = _dot(ah, jnp.concatenate([bh, bl], axis=1))
    return (p[:, :n] + p[:, n:]) + _dot(al, bh)


def _split3(a):
    a1 = a.astype(BF16)
    r1 = a - a1.astype(F32)
    a2 = r1.astype(BF16)
    return a1, a2, (r1 - a2.astype(F32)).astype(BF16)


def _tri_dot(tri, a):
    n = a.shape[1]
    p = _dot(tri, jnp.concatenate(_split3(a), axis=1))
    return (p[:, :n] + p[:, n:2 * n]) + p[:, 2 * n:]


def _dot_sel(a, sel):
    a1, a2, a3 = _split3(a)
    return (_dot(a1, sel) + _dot(a2, sel)) + _dot(a3, sel)


def _rowsum(t):
    return _dot(t.astype(BF16), jnp.ones((t.shape[1], t.shape[1]), BF16))


def _level_refs(b):
    refs = []
    pos = _iota(b.shape, 0)
    for m in LEVELS:
        if 2 * m >= SUBLANES:
            parts = [jnp.broadcast_to(b[r0 + m - 1: r0 + m, :], (2 * m, b.shape[1])) for r0 in range(0, CHUNK, 2 * m)]
            refs.append(parts[0] if len(parts) == 1 else jnp.concatenate(parts, axis=0))
        else:
            p = pos & (2 * m - 1)
            ref = b
            for off in range(-(m - 1), m + 1):
                if off != 0:
                    ref = jnp.where(p == m - 1 + off, pltpu.roll(b, off % CHUNK, 0), ref)
            refs.append(ref)
    return refs


def _hg_lb(lbl_ref):
    l0 = lbl_ref[0:1, :]
    l1 = lbl_ref[1:2, :]
    mx = jnp.maximum(l0, l1)
    e0 = jnp.exp(l0 - mx)
    e1 = jnp.exp(l1 - mx)
    p0 = e0 / (e0 + e1)
    lb = jnp.clip(p0, 1e-6, 1.0 - 1e-6)
    inside = (p0 >= 1e-6) & (p0 <= 1.0 - 1e-6)
    dlb_dl0 = jnp.where(inside, p0 * (e1 / (e0 + e1)), 0.0)
    return lb, dlb_dl0


def _sigmoid(v):
    return 0.5 * jnp.tanh(0.5 * v) + 0.5


def _hg_gates(qr, fl, lb):
    sig = _sigmoid(fl)
    f = lb + (1.0 - lb) * sig
    g = jnp.log(f)
    k = (1.0 - lb) * (1.0 - sig)
    sq = _sigmoid(qr)
    q = qr * sq
    return sig, f, g, k, sq, q


def _hg_levels(q, k, b, mk_ref):
    refs = _level_refs(b)
    a = jnp.zeros((CHUNK, CHUNK), F32)
    es, qts, kts = [], [], []
    for i in range(N_LEV):
        diff = b - refs[i]
        e = jnp.exp(jnp.minimum(diff, -diff))
        qt = (q * e).astype(BF16)
        kt = (k * e).astype(BF16)
        a = a + _dot_nt(qt, kt) * mk_ref[i]
        es.append(e); qts.append(qt); kts.append(kt)
    return a, es, qts, kts


def _hg_specs(nc, rev):
    cc = (lambda c: nc - 1 - c) if rev else (lambda c: c)
    w = 128 * HG_HPS
    blk = lambda off: pl.BlockSpec((CHUNK, w), lambda h, c: (cc(c), h + off))
    vec = pl.BlockSpec((1, w), lambda h, c: (0, h))
    lb2 = pl.BlockSpec((2, w), lambda h, c: (0, h))
    st = pl.BlockSpec((1, HG_HPS, 128, 128), lambda h, c: (cc(c), h, 0, 0))
    consts = [_full((2, CHUNK, CHUNK)), _full((2 * N_LEV + DIAG, CHUNK, CHUNK))]
    return blk, vec, lb2, st, consts


def _hg_fwd(p0, p1, lbl, gw):
    S = p0.shape[0]
    nc = S // CHUNK
    ng = 4 // HG_HPS

    def body(qr_ref, fl_ref, v_ref, z_ref, lbl_ref, gw_ref, tri_ref, mk_ref,
             o_ref, mix_ref, mixt_ref, st_ref, state):
        c = pl.program_id(1)

        @pl.when(c == 0)
        def _():
            state[...] = jnp.zeros_like(state)

        lb_all, _ = _hg_lb(lbl_ref)
        heads = [slice(128 * hh, 128 * (hh + 1)) for hh in range(HG_HPS)]
        qs, ks, bs, mats = [], [], [], []
        for sl in heads:
            _, _, g, k, _, q = _hg_gates(qr_ref[:, sl], fl_ref[:, sl], lb_all[:, sl])
            qs.append(q); ks.append(k); bs.append(_tri_dot(tri_ref[0], g))
        for hh in range(HG_HPS):
            a, _, _, _ = _hg_levels(qs[hh], ks[hh], bs[hh], mk_ref)
            mats.append(a + _rowsum(qs[hh] * ks[hh]) * mk_ref[2 * N_LEV])
        for hh, sl in enumerate(heads):
            q, k, b, a, v, z = qs[hh], ks[hh], bs[hh], mats[hh], v_ref[:, sl], z_ref[:, sl]
            b_last = b[CHUNK - 1: CHUNK, :]
            st = state[hh]
            st_ref[0, hh] = st
            o = _dot_nt((q * jnp.exp(b)).astype(BF16), st.astype(BF16)) + _dot(a.astype(BF16), v.astype(BF16))
            state[hh] = st * jnp.exp(b_last) + _dot3(v.T, k * jnp.exp(b_last - b))
            o_ref[:, sl] = o
            rs = lax.rsqrt(jnp.mean(o * o, axis=-1, keepdims=True) + EPS)
            mixed = o * rs * gw_ref[:, sl] * (z * _sigmoid(z))
            mix_ref[:, sl] = mixed.astype(BF16)
            mixt_ref[sl, :] = mixed.T.astype(BF16)

    blk, vec, lb2, st_spec, consts = _hg_specs(nc, False)
    tris, masks = _hg_const_arrays()
    return pl.pallas_call(
        body, name="hg_fwd", grid=(ng, nc),
        in_specs=[blk(0), blk(ng), blk(0), blk(ng), lb2, vec] + consts,
        out_specs=[blk(0), blk(0), pl.BlockSpec((128 * HG_HPS, CHUNK), lambda h, c: (h, c)), st_spec],
        out_shape=[jax.ShapeDtypeStruct((S, HG_W), F32), jax.ShapeDtypeStruct((S, HG_W), BF16),
                   jax.ShapeDtypeStruct((HG_W, S), BF16), jax.ShapeDtypeStruct((nc, 4, 128, 128), F32)],
        scratch_shapes=[pltpu.VMEM((HG_HPS, 128, 128), F32)],
        compiler_params=_params(("parallel", "arbitrary")),
    )(p0, p0, p1, p1, lbl, gw, tris, masks)


def _hg_bwd(p0, p1, o_raw, dm, states, lbl, gw):
    S = p0.shape[0]
    nc = S // CHUNK
    ng = 4 // HG_HPS
    w = 128 * HG_HPS

    def body(qr_ref, fl_ref, v_ref, z_ref, o_ref, dm_ref, st_ref, lbl_ref, gw_ref, tri_ref, mk_ref,
             dqr_ref, dfl_ref, dv_ref, dz_ref, glbl_ref, ggw_ref, dstate, carry, acc_lb, acc_gw):
        c = pl.program_id(1)

        @pl.when(c == 0)
        def _():
            dstate[...] = jnp.zeros_like(dstate)
            carry[...] = jnp.zeros_like(carry)
            acc_lb[...] = jnp.zeros_like(acc_lb)
            acc_gw[...] = jnp.zeros_like(acc_gw)

        lb_all, dlb_dl0 = _hg_lb(lbl_ref)
        heads = [slice(128 * hh, 128 * (hh + 1)) for hh in range(HG_HPS)]
        diag_mask = mk_ref[2 * N_LEV]
        gates, bs, lev, dos = [], [], [], []
        for sl in heads:
            gt = _hg_gates(qr_ref[:, sl], fl_ref[:, sl], lb_all[:, sl])
            gates.append(gt)
            bs.append(_tri_dot(tri_ref[0], gt[2]))
        for hh, sl in enumerate(heads):
            _, _, _, k, _, q = gates[hh]
            lev.append(_hg_levels(q, k, bs[hh], mk_ref))
            o, z, dmix, gwv = o_ref[:, sl], z_ref[:, sl], dm_ref[:, sl], gw_ref[:, sl]
            rs = lax.rsqrt(jnp.mean(o * o, axis=-1, keepdims=True) + EPS)
            oh = o * rs
            sz = _sigmoid(z)
            dz_ref[:, sl] = (dmix * (oh * gwv) * (sz * (1.0 + z * (1.0 - sz)))).astype(BF16)
            don = dmix * (z * sz)
            acc_gw[0:1, sl] += jnp.sum(don * oh, axis=0, keepdims=True)
            dy = don * gwv
            dos.append(rs * (dy - oh * jnp.mean(dy * oh, axis=-1, keepdims=True)))
        inter = []
        for hh, sl in enumerate(heads):
            _, _, _, k, _, q = gates[hh]
            b, do, v = bs[hh], dos[hh], v_ref[:, sl]
            b_last = b[CHUNK - 1: CHUNK, :]
            eb = jnp.exp(b)
            edec = jnp.exp(b_last - b)
            dst = dstate[hh]
            dq = _dot3(do, st_ref[0, hh]) * eb
            dk = _dot3(v, dst) * edec
            dob = do.astype(BF16)
            da = _dot_nt(dob, v.astype(BF16))
            a, _, _, _ = lev[hh]
            a = a + _rowsum(q * k) * diag_mask
            dv_ref[:, sl] = (_dot(a.T.astype(BF16), dob) + _dot_nt((k * edec).astype(BF16), dst.astype(BF16))).astype(BF16)
            dstate[hh] = dst * jnp.exp(b_last) + _dot3(do.T, q * eb)
            inter.append((dq, dk, da))
        for hh, sl in enumerate(heads):
            sig, f, _, k, sq, q = gates[hh]
            _, es, qts, kts = lev[hh]
            dq, dk, da = inter[hh]
            db = q * dq - k * dk
            da_t = da.T
            for i in range(N_LEV):
                gq = _dot((da * mk_ref[i]).astype(BF16), kts[i])
                gk = _dot((da_t * mk_ref[N_LEV + i]).astype(BF16), qts[i])
                dq = dq + es[i] * gq
                dk = dk + es[i] * gk
                db = db + (qts[i].astype(F32) * gq - kts[i].astype(F32) * gk)
            dc = _rowsum(da * diag_mask)
            dq = dq + dc * k
            dk = dk + dc * q
            dg = _tri_dot(tri_ref[1], db) + carry[0:1, sl]
            carry[0:1, sl] += jnp.sum(db, axis=0, keepdims=True)
            lb = lb_all[:, sl]
            qr = qr_ref[:, sl]
            t = dg / f - dk
            dfl_ref[:, sl] = (t * (1.0 - lb) * sig * (1.0 - sig)).astype(BF16)
            acc_lb[0:1, sl] += jnp.sum(t * (1.0 - sig), axis=0, keepdims=True)
            dqr_ref[:, sl] = (dq * (sq * (1.0 + qr * (1.0 - sq)))).astype(BF16)

        @pl.when(c == nc - 1)
        def _():
            gl0 = acc_lb[0:1, :] * dlb_dl0
            glbl_ref[0:1, :] = gl0
            glbl_ref[1:2, :] = -gl0
            ggw_ref[...] = acc_gw[0:1, :]

    blk, vec, lb2, st_spec, consts = _hg_specs(nc, True)
    tris, masks = _hg_const_arrays()
    act = jax.ShapeDtypeStruct((S, HG_W), BF16)
    return pl.pallas_call(
        body, name="hg_bwd", grid=(ng, nc),
        in_specs=[blk(0), blk(ng), blk(0), blk(ng), blk(0), blk(0), st_spec, lb2, vec] + consts,
        out_specs=[blk(0), blk(0), blk(0), blk(0), lb2, vec],
        out_shape=[act, act, act, act, jax.ShapeDtypeStruct((2, HG_W), F32), jax.ShapeDtypeStruct((1, HG_W), F32)],
        scratch_shapes=[pltpu.VMEM((HG_HPS, 128, 128), F32), pltpu.VMEM((8, w), F32),
                        pltpu.VMEM((8, w), F32), pltpu.VMEM((8, w), F32)],
        compiler_params=_params(("parallel", "arbitrary")),
    )(p0, p0, p1, p1, o_raw, dm, states, lbl, gw, tris, masks)


def _stat_head(lane):
    return lane >> 4


def _stat_lane(lane):
    return lane & (STAT_LANES - 1)


def _o_head(lane):
    return lane >> 6


def _att_bias():
    qi = np.arange(ATT_BLK)[:, None]
    kj = np.arange(2 * ATT_BLK)[None, :]
    band = (kj >= qi) & (kj <= qi + ATT_BLK)
    qm = np.stack([band & (kj >= ATT_BLK), band])
    cur = (kj < ATT_BLK) & (qi <= kj)
    km = np.stack([cur, cur | ((kj >= ATT_BLK) & (qi >= kj - ATT_BLK))])
    to_bias = lambda m: jnp.asarray(np.where(m, 0.0, NEG), F32)
    return to_bias(qm), to_bias(km)


def _att_fwd(qkv_d):
    d, L, _ = qkv_d.shape
    qb = ATT_QB
    rows = qb * ATT_BLK

    def body(q_ref, kp_ref, kc_ref, vp_ref, vc_ref, bias_ref, o_ref, lse_ref, s_scr, p_scr):
        first = _iota((ATT_BLK, HEAD_PAIR), 1) < 64
        head_of_lane = _stat_head(_iota((ATT_BLK, STAT_W), 1))
        pairs = [slice(HEAD_PAIR * hp, HEAD_PAIR * (hp + 1)) for hp in range(4)]
        blk = lambda i: slice(ATT_BLK * i, ATT_BLK * (i + 1))

        def keys(i, sl, prev_ref, cur_ref):
            before = prev_ref[0, :, sl] if i == 0 else cur_ref[0, blk(i - 1), sl]
            return jnp.concatenate([before, cur_ref[0, blk(i), sl]], axis=0)

        for i in range(qb):
            for hp, sl in enumerate(pairs):
                q2 = q_ref[0, blk(i), sl] * ATT_SCALE
                zero = jnp.zeros_like(q2)
                qs = jnp.concatenate([jnp.where(first, q2, zero), jnp.where(first, zero, q2)], axis=0)
                s_scr[4 * i + hp] = _dot_nt(qs, keys(i, sl, kp_ref, kc_ref))
        stats = []
        for i in range(qb):
            bias = bias_ref[jnp.minimum(pl.program_id(1), 1)] if i == 0 else bias_ref[1]
            bias2 = jnp.concatenate([bias, bias], axis=0)
            for hp in range(4):
                s = s_scr[4 * i + hp] + bias2
                m = jnp.max(s, axis=-1, keepdims=True)
                p = jnp.exp(s - m)
                l = jnp.sum(p, axis=-1, keepdims=True)
                p_scr[4 * i + hp] = p.astype(BF16)
                stats.append((l, m + jnp.log(l)))
        for i in range(qb):
            lse_blk = jnp.zeros((ATT_BLK, STAT_W), F32)
            for hp, sl in enumerate(pairs):
                l, lse = stats[4 * i + hp]
                o = _dot(p_scr[4 * i + hp], keys(i, sl, vp_ref, vc_ref)) / l
                o_ref[0, blk(i), sl] = jnp.where(first, o[:ATT_BLK], o[ATT_BLK:]).astype(BF16)
                lse_blk = jnp.where(head_of_lane == 2 * hp, lse[:ATT_BLK],
                                    jnp.where(head_of_lane == 2 * hp + 1, lse[ATT_BLK:], lse_blk))
            lse_ref[0, blk(i), :] = lse_blk

    cur = lambda j: pl.BlockSpec((1, rows, AT_W), lambda r, n: (r, n, j))
    prev = lambda j: pl.BlockSpec((1, ATT_BLK, AT_W), lambda r, n: (r, jnp.maximum(qb * n - 1, 0), j))
    return pl.pallas_call(
        body, name=f"att_fwd_d{d}", grid=(d, L // rows),
        in_specs=[cur(0), prev(1), cur(1), prev(2), cur(2), _full((2, ATT_BLK, 2 * ATT_BLK))],
        out_specs=[pl.BlockSpec((1, rows, AT_W), lambda r, n: (r, n, 0)), pl.BlockSpec((1, rows, STAT_W), lambda r, n: (r, n, 0))],
        out_shape=[jax.ShapeDtypeStruct((d, L, AT_W), BF16), jax.ShapeDtypeStruct((d, L, STAT_W), F32)],
        scratch_shapes=[pltpu.VMEM((4 * qb, 2 * ATT_BLK, 2 * ATT_BLK), F32), pltpu.VMEM((4 * qb, 2 * ATT_BLK, 2 * ATT_BLK), BF16)],
        compiler_params=_params(("parallel", "parallel")),
    )(qkv_d, qkv_d, qkv_d, qkv_d, qkv_d, _att_bias()[0])


def _att_combine(os_d, ls_d, p3):
    S = p3.shape[0]
    tm = 512

    def body(oa_ref, ob4_ref, oc16_ref, la_ref, lb4_ref, lc16_ref, z_ref,
             oat_ref, lse_ref, mix_ref, mixt_ref, ob_scr, oc_scr, lb_scr, lc_scr):
        _merge_residues(ob4_ref, ob_scr, 4)
        _merge_residues(oc16_ref, oc_scr, 16)
        _merge_residues(lb4_ref, lb_scr, 4)
        _merge_residues(lc16_ref, lc_scr, 16)
        ls = (la_ref[...], lb_scr[0], lc_scr[0])
        mx = jnp.maximum(jnp.maximum(ls[0], ls[1]), ls[2])
        es = [jnp.exp(l - mx) for l in ls]
        zs = es[0] + es[1] + es[2]
        lse_ref[...] = mx + jnp.log(zs)
        spread = ((_o_head(_iota((STAT_W, AT_W), 1)) == _stat_head(_iota((STAT_W, AT_W), 0)))
                  & (_stat_lane(_iota((STAT_W, AT_W), 0)) == 0)).astype(BF16)
        os_ = (oa_ref[...].astype(F32), _from_slabs(ob_scr), _from_slabs(oc_scr))
        o = jnp.zeros((tm, AT_W), F32)
        for e, oi in zip(es, os_):
            hi, lo = _split2(e / zs)
            o = o + (_dot(hi, spread) + _dot(lo, spread)) * oi
        oat_ref[...] = o
        z = z_ref[...]
        mixed = o * (z * _sig(z))
        mix_ref[...] = mixed.astype(BF16)
        mixt_ref[...] = mixed.T.astype(BF16)

    row = lambda w: pl.BlockSpec((tm, w), lambda s: (s, 0))
    return pl.pallas_call(
        body, name="att_combine", grid=(S // tm,),
        in_specs=[row(AT_W), _dil_spec(4, tm, AT_W), _dil_spec(16, tm, AT_W),
                  row(STAT_W), _dil_spec(4, tm, STAT_W), _dil_spec(16, tm, STAT_W),
                  pl.BlockSpec((tm, AT_W), lambda s: (s, 1))],
        out_specs=[row(AT_W), row(STAT_W), row(AT_W), pl.BlockSpec((AT_W, tm), lambda s: (0, s))],
        out_shape=[jax.ShapeDtypeStruct((S, AT_W), F32), jax.ShapeDtypeStruct((S, STAT_W), F32),
                   jax.ShapeDtypeStruct((S, AT_W), BF16), jax.ShapeDtypeStruct((AT_W, S), BF16)],
        scratch_shapes=[_slab_scratch(tm, AT_W), _slab_scratch(tm, AT_W), _slab_scratch(tm, STAT_W), _slab_scratch(tm, STAT_W)],
        compiler_params=_params(("parallel",)),
    )(os_d[0].reshape(S, AT_W), os_d[1], os_d[2], ls_d[0].reshape(S, STAT_W), ls_d[1], ls_d[2], p3)


def _att_gate_bwd(dm_at, o_at, lse, p3):
    S = p3.shape[0]
    tm = 512

    def body(dm_ref, o_ref, l_ref, z_ref, do_ref, do4_ref, do16_ref, dl_ref, dl4_ref, dl16_ref, dz_ref, do_scr, dl_scr):
        o = o_ref[...]
        z = z_ref[...]
        dm = dm_ref[...]
        sz = _sig(z)
        dz_ref[...] = (dm * o * (sz * (1.0 + z * (1.0 - sz)))).astype(BF16)
        do = dm * (z * sz)
        do_ref[...] = do.astype(BF16)
        gather = (_o_head(_iota((AT_W, STAT_W), 0)) == _stat_head(_iota((AT_W, STAT_W), 1))).astype(BF16)
        dl = jnp.where(_stat_lane(_iota((tm, STAT_W), 1)) < STAT_LSE_LANE, _dot_sel(do * o, gather), l_ref[...])
        dl_ref[...] = dl
        _to_slabs(do, do_scr)
        _to_slabs(dl, dl_scr)
        _split_residues(do_scr, do4_ref, 4, BF16)
        _split_residues(do_scr, do16_ref, 16, BF16)
        _split_residues(dl_scr, dl4_ref, 4, F32)
        _split_residues(dl_scr, dl16_ref, 16, F32)

    row = lambda w: pl.BlockSpec((tm, w), lambda s: (s, 0))
    sds = jax.ShapeDtypeStruct
    return pl.pallas_call(
        body, name="att_gate_bwd", grid=(S // tm,),
        in_specs=[row(AT_W), row(AT_W), row(STAT_W), pl.BlockSpec((tm, AT_W), lambda s: (s, 1))],
        out_specs=[row(AT_W), _dil_spec(4, tm, AT_W), _dil_spec(16, tm, AT_W),
                   row(STAT_W), _dil_spec(4, tm, STAT_W), _dil_spec(16, tm, STAT_W), row(AT_W)],
        out_shape=[sds((S, AT_W), BF16), sds((4, S // 4, AT_W), BF16), sds((16, S // 16, AT_W), BF16),
                   sds((S, STAT_W), F32), sds((4, S // 4, STAT_W), F32), sds((16, S // 16, STAT_W), F32),
                   sds((S, AT_W), BF16)],
        scratch_shapes=[_slab_scratch(tm, AT_W), _slab_scratch(tm, STAT_W)],
        compiler_params=_params(("parallel",)),
    )(dm_at, o_at, lse, p3)


def _att_bwd(qkv_d, do_d, dl_d):
    d, L, _ = qkv_d.shape
    nb = L // ATT_BLK
    qb = ATT_QB
    rows = qb * ATT_BLK
    nsteps = L // rows

    def body(qc_ref, qn_ref, kp_ref, kc_ref, vp_ref, vc_ref, ac_ref, an_ref, lc_ref, ln_ref, bq_ref, bk_ref, dqkv_ref,
             s_scr, dp_scr, st_scr, dpt_scr, ds_scr, pt_scr, dst_scr):
        n = pl.program_id(1)
        first = _iota((ATT_BLK, HEAD_PAIR), 1) < 64
        pairs = [slice(HEAD_PAIR * hp, HEAD_PAIR * (hp + 1)) for hp in range(4)]
        blk = lambda i: slice(ATT_BLK * i, ATT_BLK * (i + 1))

        def stack(t):
            zero = jnp.zeros_like(t)
            return jnp.concatenate([jnp.where(first, t, zero), jnp.where(first, zero, t)], axis=0)

        def unstack(t2):
            return jnp.where(first, t2[:ATT_BLK], t2[ATT_BLK:])

        def with_prev(i, sl, prev_ref, cur_ref):
            before = prev_ref[0, :, sl] if i == 0 else cur_ref[0, blk(i - 1), sl]
            return jnp.concatenate([before, cur_ref[0, blk(i), sl]], axis=0)

        def with_next(i, sl, cur_ref, next_ref):
            after = next_ref[0, :, sl] if i == qb - 1 else cur_ref[0, blk(i + 1), sl]
            return jnp.concatenate([cur_ref[0, blk(i), sl], after], axis=0)

        for i in range(qb):
            for hp, sl in enumerate(pairs):
                j = 4 * i + hp
                s_scr[j] = _dot_nt(stack(qc_ref[0, blk(i), sl] * ATT_SCALE), with_prev(i, sl, kp_ref, kc_ref))
                dp_scr[j] = _dot_nt(stack(ac_ref[0, blk(i), sl]), with_prev(i, sl, vp_ref, vc_ref))
                st_scr[j] = _dot_nt(stack(kc_ref[0, blk(i), sl] * ATT_SCALE), with_next(i, sl, qc_ref, qn_ref))
                dpt_scr[j] = _dot_nt(stack(vc_ref[0, blk(i), sl]), with_next(i, sl, ac_ref, an_ref))
        for i in range(qb):
            bias = bq_ref[jnp.minimum(n, 1)] if i == 0 else bq_ref[1]
            bias_t = bk_ref[jnp.minimum(nsteps - 1 - n, 1)] if i == qb - 1 else bk_ref[1]
            bias2 = jnp.concatenate([bias, bias], axis=0)
            bias_t2 = jnp.concatenate([bias_t, bias_t], axis=0)
            dl_c = lc_ref[0, blk(i), :]
            dl_t = with_next(i, slice(None), lc_ref, ln_ref).T
            for hp in range(4):
                j = 4 * i + hp
                at = [STAT_LANES * (2 * hp), STAT_LANES * (2 * hp + 1)]
                col = lambda t, o: jnp.concatenate([t[:, a + o: a + o + 1] for a in at], axis=0)
                p = jnp.exp(s_scr[j] + bias2 - col(dl_c, STAT_LSE_LANE))
                ds_scr[j] = (p * (dp_scr[j] - col(dl_c, 0))).astype(BF16)
                row = lambda t, o: jnp.concatenate([jnp.broadcast_to(t[a + o: a + o + 1, :], (ATT_BLK, 2 * ATT_BLK)) for a in at], axis=0)
                pt = jnp.exp(st_scr[j] + bias_t2 - row(dl_t, STAT_LSE_LANE))
                pt_scr[j] = pt.astype(BF16)
                dst_scr[j] = (pt * (dpt_scr[j] - row(dl_t, 0))).astype(BF16)
        for i in range(qb):
            for hp, sl in enumerate(pairs):
                j = 4 * i + hp
                dq = unstack(_dot(ds_scr[j], with_prev(i, sl, kp_ref, kc_ref))) * ATT_SCALE
                dk = unstack(_dot(dst_scr[j], with_next(i, sl, qc_ref, qn_ref))) * ATT_SCALE
                dv = unstack(_dot(pt_scr[j], with_next(i, sl, ac_ref, an_ref)))
                dqkv_ref[0, blk(i), sl] = dq.astype(BF16)
                dqkv_ref[0, blk(i), AT_W + HEAD_PAIR * hp: AT_W + HEAD_PAIR * (hp + 1)] = dk.astype(BF16)
                dqkv_ref[0, blk(i), 2 * AT_W + HEAD_PAIR * hp: 2 * AT_W + HEAD_PAIR * (hp + 1)] = dv.astype(BF16)

    cur = lambda j: pl.BlockSpec((1, rows, AT_W), lambda r, n: (r, n, j))
    prev = lambda j: pl.BlockSpec((1, ATT_BLK, AT_W), lambda r, n: (r, jnp.maximum(qb * n - 1, 0), j))
    nxt_blk = lambda n: jnp.minimum(qb * (n + 1), nb - 1)
    sq = (4 * qb, 2 * ATT_BLK, 2 * ATT_BLK)
    return pl.pallas_call(
        body, name=f"att_bwd_d{d}", grid=(d, nsteps),
        in_specs=[cur(0), pl.BlockSpec((1, ATT_BLK, AT_W), lambda r, n: (r, nxt_blk(n), 0)), prev(1), cur(1), prev(2), cur(2),
                  pl.BlockSpec((1, rows, AT_W), lambda r, n: (r, n, 0)),
                  pl.BlockSpec((1, ATT_BLK, AT_W), lambda r, n: (r, nxt_blk(n), 0)),
                  pl.BlockSpec((1, rows, STAT_W), lambda r, n: (r, n, 0)),
                  pl.BlockSpec((1, ATT_BLK, STAT_W), lambda r, n: (r, nxt_blk(n), 0)),
                  _full((2, ATT_BLK, 2 * ATT_BLK)), _full((2, ATT_BLK, 2 * ATT_BLK))],
        out_specs=pl.BlockSpec((1, rows, 3 * AT_W), lambda r, n: (r, n, 0)),
        out_shape=jax.ShapeDtypeStruct((d, L, 3 * AT_W), BF16),
        scratch_shapes=[pltpu.VMEM(sq, F32)] * 4 + [pltpu.VMEM(sq, BF16)] * 3,
        compiler_params=_params(("parallel", "parallel")),
    )(qkv_d, qkv_d, qkv_d, qkv_d, qkv_d, qkv_d, do_d, do_d, dl_d, dl_d, *_att_bias())


def _att_bwd_combine(dqkvs, cos2, sin2):
    S = dqkvs[0].shape[1]
    tm = 512

    def body(a_ref, b4_ref, c16_ref, cos_ref, sin_ref, dq_ref, dk_ref, dv_ref, b_scr, c_scr):
        _merge_residues(b4_ref, b_scr, 4)
        _merge_residues(c16_ref, c_scr, 16)
        t = a_ref[...].astype(F32) + _from_slabs(b_scr) + _from_slabs(c_scr)
        dy = t[:, : 2 * AT_W]
        cosf = jnp.tile(cos_ref[...], (1, 8))
        sinf = jnp.tile(sin_ref[...], (1, 8))
        dx = dy * cosf - _rope_rot(dy) * sinf
        dq_ref[...] = dx[:, :AT_W].astype(BF16)
        dk_ref[...] = dx[:, AT_W:].astype(BF16)
        dv_ref[...] = t[:, 2 * AT_W:].astype(BF16)

    row = lambda w: pl.BlockSpec((tm, w), lambda s: (s, 0))
    act = jax.ShapeDtypeStruct((S, AT_W), BF16)
    return pl.pallas_call(
        body, name="att_bwd_combine", grid=(S // tm,),
        in_specs=[row(3 * AT_W), _dil_spec(4, tm, 3 * AT_W), _dil_spec(16, tm, 3 * AT_W), row(128), row(128)],
        out_specs=[row(AT_W), row(AT_W), row(AT_W)],
        out_shape=[act, act, act],
        scratch_shapes=[_slab_scratch(tm, 3 * AT_W), _slab_scratch(tm, 3 * AT_W)],
        compiler_params=_params(("parallel",)),
    )(dqkvs[0].reshape(S, 3 * AT_W), dqkvs[1], dqkvs[2], cos2, sin2)


def _outproj(x2, tgt2, mix_hg, mix_at, mixt_hg, mixt_at, w_out_full, fnw):
    S = x2.shape[0]
    tm = 256
    ns = S // tm

    def body(x_ref, t_ref, mh_ref, ma_ref, mht_ref, mat_ref, w_ref, fw_ref,
             dh_ref, dmh_ref, dma_ref, gw_ref, gfw_ref, loss_ref):
        s = pl.program_id(0)

        @pl.when(s == 0)
        def _():
            gw_ref[...] = jnp.zeros_like(gw_ref)
            gfw_ref[...] = jnp.zeros_like(gfw_ref)
            loss_ref[...] = jnp.zeros_like(loss_ref)

        y = _dot(mh_ref[...], w_ref[:HG_W, :]) + _dot(ma_ref[...], w_ref[HG_W:, :])
        h = x_ref[...] + y
        r = lax.rsqrt(jnp.mean(h * h, axis=-1, keepdims=True) + EPS)
        hn = h * r
        fw = fw_ref[...]
        err = hn * fw - t_ref[...]
        loss_ref[...] += 0.5 * jnp.sum(jnp.mean(err * err, axis=-1, keepdims=True))
        dout = err * (1.0 / D_MODEL)
        gfw_ref[...] += jnp.sum(dout * hn, axis=0, keepdims=True)
        dhn = dout * fw
        dh = r * (dhn - hn * jnp.mean(dhn * hn, axis=-1, keepdims=True))
        dh_ref[...] = dh
        dhb = dh.astype(BF16)
        dmh_ref[...] = _dot_nt(dhb, w_ref[:HG_W, :])
        dma_ref[...] = _dot_nt(dhb, w_ref[HG_W:, :])
        gw_ref[:HG_W, :] += _dot(mht_ref[...], dhb)
        gw_ref[HG_W:, :] += _dot(mat_ref[...], dhb)

    row = lambda w: pl.BlockSpec((tm, w), lambda s: (s, 0))
    colb = pl.BlockSpec((HG_W, tm), lambda s: (0, s))
    return pl.pallas_call(
        body, name="outproj", grid=(ns,),
        in_specs=[row(D_MODEL), row(D_MODEL), row(HG_W), row(AT_W), colb, colb,
                  _full((D_MODEL, D_MODEL)), _full((1, D_MODEL))],
        out_specs=[row(D_MODEL), row(HG_W), row(AT_W), _full((D_MODEL, D_MODEL)), _full((1, D_MODEL)), _full((8, 128))],
        out_shape=[jax.ShapeDtypeStruct((S, D_MODEL), F32), jax.ShapeDtypeStruct((S, HG_W), F32),
                   jax.ShapeDtypeStruct((S, AT_W), F32), jax.ShapeDtypeStruct((D_MODEL, D_MODEL), F32),
                   jax.ShapeDtypeStruct((1, D_MODEL), F32), jax.ShapeDtypeStruct((8, 128), F32)],
        compiler_params=_params(("arbitrary",)),
    )(x2, tgt2, mix_hg, mix_at, mixt_hg, mixt_at, w_out_full, fnw)


def _inproj_bwd_x(dps, w_in_full, x2, norm_w, dh):
    S = x2.shape[0]
    tm = 256

    def body(d0, d1, d2, d3, d4, d5, d6, d7, w_ref, x_ref, nw_ref, dh_ref, gx_ref, gnw_ref):
        s = pl.program_id(0)

        @pl.when(s == 0)
        def _():
            gnw_ref[...] = jnp.zeros_like(gnw_ref)

        du = jnp.zeros((tm, D_MODEL), F32)
        for i, dref in enumerate((d0, d1, d2, d3, d4, d5, d6, d7)):
            j, half = divmod(i, 2)
            du = du + _dot_nt(dref[...], w_ref[j, :, 512 * half: 512 * (half + 1)])
        x = x_ref[...]
        r = lax.rsqrt(jnp.mean(x * x, axis=-1, keepdims=True) + EPS)
        xh = x * r
        gnw_ref[...] += jnp.sum(du * xh, axis=0, keepdims=True)
        dun = du * nw_ref[...]
        gx_ref[...] = dh_ref[...] + r * (dun - xh * jnp.mean(dun * xh, axis=-1, keepdims=True))

    row = lambda w: pl.BlockSpec((tm, w), lambda s: (s, 0))
    return pl.pallas_call(
        body, name="inproj_bwd_x", grid=(S // tm,),
        in_specs=[row(512)] * 8 + [_full((4, D_MODEL, 1024)), row(D_MODEL), _full((1, D_MODEL)), row(D_MODEL)],
        out_specs=[row(D_MODEL), _full((1, D_MODEL))],
        out_shape=[jax.ShapeDtypeStruct((S, D_MODEL), F32), jax.ShapeDtypeStruct((1, D_MODEL), F32)],
        compiler_params=_params(("arbitrary",)),
    )(*dps, w_in_full, x2, norm_w, dh)


def _inproj_bwd_w(ut, dps):
    S = ut.shape[1]
    tm = 256

    def body(ut_ref, d0, d1, d2, d3, d4, d5, d6, d7, g_ref):
        @pl.when(pl.program_id(0) == 0)
        def _():
            g_ref[...] = jnp.zeros_like(g_ref)

        utb = ut_ref[...]
        for i, dref in enumerate((d0, d1, d2, d3, d4, d5, d6, d7)):
            j, half = divmod(i, 2)
            g_ref[j, :, 512 * half: 512 * (half + 1)] += _dot(utb, dref[...])

    return pl.pallas_call(
        body, name="inproj_bwd_w", grid=(S // tm,),
        in_specs=[pl.BlockSpec((D_MODEL, tm), lambda s: (0, s))] + [pl.BlockSpec((tm, 512), lambda s: (s, 0))] * 8,
        out_specs=_full((4, D_MODEL, 1024)),
        out_shape=jax.ShapeDtypeStruct((4, D_MODEL, 1024), F32),
        compiler_params=_params(("arbitrary",)),
    )(ut, *dps)


def _adamw_update(gg, w_ref, m_ref, v_ref, d_ref, nm_ref, nv_ref):
    nm = ADAM_B1 * m_ref[...] + (1.0 - ADAM_B1) * gg
    nv = ADAM_B2 * v_ref[...] + (1.0 - ADAM_B2) * (gg * gg)
    m_hat = nm / (1.0 - ADAM_B1 ** ADAM_STEP)
    v_hat = nv / (1.0 - ADAM_B2 ** ADAM_STEP)
    d_ref[...] = -ADAM_LR * (m_hat / (jnp.sqrt(v_hat) + ADAM_EPS) + ADAM_WD * w_ref[...])
    nm_ref[...] = nm
    nv_ref[...] = nv


def _adamw(w, g, m, v, name):
    rows, cols = w.shape
    tr = min(rows, 256)

    def body(w_ref, g_ref, m_ref, v_ref, d_ref, nm_ref, nv_ref):
        _adamw_update(g_ref[...], w_ref, m_ref, v_ref, d_ref, nm_ref, nv_ref)

    spec = pl.BlockSpec((tr, cols), lambda i: (i, 0))
    sds = jax.ShapeDtypeStruct((rows, cols), F32)
    return pl.pallas_call(
        body, name=name, grid=(rows // tr,),
        in_specs=[spec] * 4, out_specs=[spec] * 3, out_shape=[sds] * 3,
        compiler_params=_params(("parallel",)),
    )(w, g, m, v)


def _place():
    return lax.axis_index("x"), lax.axis_index("y"), lax.axis_index("c")


def _gather_weights(w_in_s, w_out_s):
    def body(win_ref, wout_ref, fin_ref, fout_ref, send_sems, recv_sems):
        x, y, c = _place()
        me = (x, y, c)
        sib = (x, y, 1 - c)
        mine = 2 * x + y
        fin_ref[mine] = win_ref[...].astype(BF16)
        fout_ref[mine] = wout_ref[...].astype(BF16)
        chips = [(1 - x, y), (x, 1 - y), (1 - x, 1 - y)]

        def halves(chip, half):
            return (fin_ref.at[chip, pl.ds(half * 512, 512), :], fout_ref.at[chip, pl.ds(half * 128, 128), :])

        def copy(k, ref, to):
            return pltpu.make_async_remote_copy(src_ref=ref, dst_ref=ref, send_sem=send_sems.at[k],
                                                recv_sem=recv_sems.at[k], device_id=to, device_id_type=MESH)

        first, passed = [], []
        for j, (cx, cy) in enumerate(chips):
            for a, ref in enumerate(halves(mine, c)):
                first.append(copy(2 * j + a, ref, (cx, cy, c)))
        for cp in first:
            cp.start()
        for j, (cx, cy) in enumerate(chips):
            for a, ref in enumerate(halves(2 * cx + cy, c)):
                copy(2 * j + a, ref, me).wait_recv()
                fwd = copy(6 + 2 * j + a, ref, sib)
                fwd.start()
                passed.append(fwd)
        for j, (cx, cy) in enumerate(chips):
            for a, ref in enumerate(halves(2 * cx + cy, 1 - c)):
                copy(6 + 2 * j + a, ref, me).wait_recv()
        for cp in first + passed:
            cp.wait_send()

    vm = pl.BlockSpec(memory_space=pltpu.VMEM)
    return pl.pallas_call(
        body, name="gather_weights",
        in_specs=[vm, vm], out_specs=[vm, vm],
        out_shape=[jax.ShapeDtypeStruct((4, D_MODEL, 1024), BF16), jax.ShapeDtypeStruct((4, 256, D_MODEL), BF16)],
        scratch_shapes=[pltpu.SemaphoreType.DMA((12,)), pltpu.SemaphoreType.DMA((12,))],
        compiler_params=pltpu.CompilerParams(vmem_limit_bytes=VMEM_LIMIT),
    )(w_in_s, w_out_s)


def _swap_halves(g_in, g_out):
    def body(gin_ref, gout_ref, rin_ref, rout_ref, send_sems, recv_sems):
        x, y, c = _place()
        sib = (x, y, 1 - c)
        cps = [pltpu.make_async_remote_copy(src_ref=src.at[:, 1 - c], dst_ref=dst, send_sem=send_sems.at[k],
                                            recv_sem=recv_sems.at[k], device_id=sib, device_id_type=MESH)
               for k, (src, dst) in enumerate(((gin_ref, rin_ref), (gout_ref, rout_ref)))]
        for cp in cps:
            cp.start()
        for cp in cps:
            cp.wait()

    hbm = pl.BlockSpec(memory_space=pl.ANY)
    return pl.pallas_call(
        body, name="swap_halves",
        in_specs=[hbm, hbm], out_specs=[hbm, hbm],
        out_shape=[jax.ShapeDtypeStruct((4,) + g.shape[2:], F32) for g in (g_in, g_out)],
        scratch_shapes=[pltpu.SemaphoreType.DMA((2,)), pltpu.SemaphoreType.DMA((2,))],
    )(g_in, g_out)


def _add_half(g, r, cidx, name):
    n, _, rows, cols = g.shape

    def body(c_ref, g_ref, r_ref, o_ref):
        o_ref[0] = (g_ref[0, 0] + r_ref[0]).astype(BF16)

    return pl.pallas_call(
        body, name=name,
        grid_spec=pltpu.PrefetchScalarGridSpec(
            num_scalar_prefetch=1, grid=(n,),
            in_specs=[pl.BlockSpec((1, 1, rows, cols), lambda j, c_ref: (j, c_ref[0], 0, 0)),
                      pl.BlockSpec((1, rows, cols), lambda j, c_ref: (j, 0, 0))],
            out_specs=pl.BlockSpec((1, rows, cols), lambda j, c_ref: (j, 0, 0))),
        out_shape=jax.ShapeDtypeStruct((n, rows, cols), BF16),
        compiler_params=_params(("parallel",)),
    )(cidx, g, r)


def _exchange_chips(cs_in, cs_out):
    def body(in_ref, out_ref, rin_ref, rout_ref, send_sems, recv_sems):
        x, y, c = _place()
        chips = [(1 - x, y), (x, 1 - y), (1 - x, 1 - y)]
        cps = []
        for k, (cx, cy) in enumerate(chips):
            for a, (src, dst) in enumerate(((in_ref, rin_ref), (out_ref, rout_ref))):
                cps.append(pltpu.make_async_remote_copy(
                    src_ref=src.at[2 * cx + cy], dst_ref=dst.at[k], send_sem=send_sems.at[2 * k + a],
                    recv_sem=recv_sems.at[2 * k + a], device_id=(cx, cy, c), device_id_type=MESH))
        for cp in cps:
            cp.start()
        for cp in cps:
            cp.wait()

    hbm = pl.BlockSpec(memory_space=pl.ANY)
    return pl.pallas_call(
        body, name="exchange_chips",
        in_specs=[hbm, hbm], out_specs=[hbm, hbm],
        out_shape=[jax.ShapeDtypeStruct((3,) + a.shape[1:], a.dtype) for a in (cs_in, cs_out)],
        scratch_shapes=[pltpu.SemaphoreType.DMA((6,)), pltpu.SemaphoreType.DMA((6,))],
    )(cs_in, cs_out)


PEER_XOR = (2, 1, 3)


def _sum_chips(cs, r, chip_idx, name):
    _, rows, cols = r.shape
    tr = min(rows, 256)

    def body(m_ref, cs_ref, r_ref, o_ref):
        mine = m_ref[0]
        own = cs_ref[0].astype(F32)
        got = [r_ref[k].astype(F32) for k in range(3)]
        acc = None
        for s in range(4):
            rel = mine ^ s
            term = jnp.where(rel == 0, own, jnp.where(rel == PEER_XOR[0], got[0],
                                                      jnp.where(rel == PEER_XOR[1], got[1], got[2])))
            acc = term if acc is None else acc + term
        o_ref[...] = acc

    return pl.pallas_call(
        body, name=name,
        grid_spec=pltpu.PrefetchScalarGridSpec(
            num_scalar_prefetch=1, grid=(rows // tr,),
            in_specs=[pl.BlockSpec((1, tr, cols), lambda i, m_ref: (m_ref[0], i, 0)),
                      pl.BlockSpec((3, tr, cols), lambda i, m_ref: (0, i, 0))],
            out_specs=pl.BlockSpec((tr, cols), lambda i, m_ref: (i, 0))),
        out_shape=jax.ShapeDtypeStruct((rows, cols), F32),
        compiler_params=_params(("parallel",)),
    )(chip_idx, cs, r)


def _swap_reduced(h_in, h_out):
    def body(in_ref, out_ref, rin_ref, rout_ref, send_sems, recv_sems):
        x, y, c = _place()
        cps = [pltpu.make_async_remote_copy(src_ref=src, dst_ref=dst, send_sem=send_sems.at[k],
                                            recv_sem=recv_sems.at[k], device_id=(x, y, 1 - c), device_id_type=MESH)
               for k, (src, dst) in enumerate(((in_ref, rin_ref), (out_ref, rout_ref)))]
        for cp in cps:
            cp.start()
        for cp in cps:
            cp.wait()

    hbm = pl.BlockSpec(memory_space=pl.ANY)
    return pl.pallas_call(
        body, name="swap_reduced",
        in_specs=[hbm, hbm], out_specs=[hbm, hbm],
        out_shape=[jax.ShapeDtypeStruct(h.shape, F32) for h in (h_in, h_out)],
        scratch_shapes=[pltpu.SemaphoreType.DMA((2,)), pltpu.SemaphoreType.DMA((2,))],
    )(h_in, h_out)


def _adamw_halves(w, mine, theirs, m, v, cidx, name):
    rows, cols = w.shape
    half = rows // 2
    tr = min(half, 256)
    nbh = half // tr

    def body(c_ref, w_ref, a_ref, b_ref, m_ref, v_ref, g_ref, d_ref, nm_ref, nv_ref):
        gg = jnp.where(pl.program_id(0) // nbh == c_ref[0], a_ref[...], b_ref[...])
        g_ref[...] = gg
        _adamw_update(gg, w_ref, m_ref, v_ref, d_ref, nm_ref, nv_ref)

    spec = pl.BlockSpec((tr, cols), lambda i, c_ref: (i, 0))
    hspec = pl.BlockSpec((tr, cols), lambda i, c_ref: (i % nbh, 0))
    sds = jax.ShapeDtypeStruct((rows, cols), F32)
    return pl.pallas_call(
        body, name=name,
        grid_spec=pltpu.PrefetchScalarGridSpec(
            num_scalar_prefetch=1, grid=(rows // tr,),
            in_specs=[spec, hspec, hspec, spec, spec], out_specs=[spec] * 4),
        out_shape=[sds] * 4,
        compiler_params=_params(("parallel",)),
    )(cidx, w, mine, theirs, m, v)


def _allreduce_small(g_nw, g_fw, g_hgw, g_lbl, loss8):
    def body(nw_ref, fw_ref, hgw_ref, lbl_ref, loss_ref, out_ref, slots, send_sems, recv_sems):
        x, y, c = _place()
        me = 4 * x + 2 * y + c
        slots[me] = jnp.zeros((8, D_MODEL), F32)
        slots[me, 0:1, :] = nw_ref[...]
        slots[me, 1:2, :] = fw_ref[...]
        slots[me, 2:3, 0:HG_W] = hgw_ref[...]
        slots[me, 3:4, 0:HG_W] = lbl_ref[0:1, :]
        slots[me, 3:4, HG_W:] = lbl_ref[1:2, :]
        slots[me, 4:5, 0:128] = loss_ref[0:1, :]
        cps = []
        for k in range(1, 8):
            dx, dy, dc = (k >> 2) & 1, (k >> 1) & 1, k & 1
            to = (x ^ dx, y ^ dy, c ^ dc)
            cps.append(pltpu.make_async_remote_copy(
                src_ref=slots.at[me], dst_ref=slots.at[me], send_sem=send_sems.at[k - 1],
                recv_sem=recv_sems.at[k - 1], device_id=to, device_id_type=MESH))
        for cp in cps:
            cp.start()
        for cp in cps:
            cp.wait()
        acc = slots[0]
        for i in range(1, 8):
            acc = acc + slots[i]
        out_ref[...] = acc

    vm = pl.BlockSpec(memory_space=pltpu.VMEM)
    return pl.pallas_call(
        body, name="allreduce_small",
        in_specs=[vm] * 5, out_specs=vm,
        out_shape=jax.ShapeDtypeStruct((8, D_MODEL), F32),
        scratch_shapes=[pltpu.VMEM((8, 8, D_MODEL), F32), pltpu.SemaphoreType.DMA((7,)), pltpu.SemaphoreType.DMA((7,))],
    )(g_nw, g_fw, g_hgw, g_lbl, loss8)


def _rope_tables(S):
    inv_freq = 1.0 / (ROPE_THETA ** (jnp.arange(ROPE_HALF, dtype=F32) / ROPE_HALF))
    ang = jnp.arange(S, dtype=jnp.int32).astype(F32)[:, None] * inv_freq[None, :]
    cos, sin = jnp.cos(ang), jnp.sin(ang)
    cos2 = jnp.concatenate([cos, cos, cos, cos], axis=-1)
    sin2 = jnp.concatenate([-sin, sin, -sin, sin], axis=-1)
    return cos2, sin2


def _local_step(x2, tgt2, norm_w, w_in_full, lbl, hg_norm_w, w_out4, fnw):
    S = x2.shape[0]
    cos2, sin2 = _rope_tables(S)
    w_out_full = w_out4.reshape(D_MODEL, D_MODEL)
    p0, p1, qkv, qkv4, qkv16, p3, ut = _inproj(x2, norm_w, w_in_full, cos2, sin2)
    o_hg, mix_hg, mixt_hg, states = _hg_fwd(p0, p1, lbl, hg_norm_w)
    qkv_ds = [qkv.reshape(1, S, 3 * AT_W), qkv4, qkv16]
    os_d, ls_d = zip(*[_att_fwd(q) for q in qkv_ds])
    o_at, lse, mix_at, mixt_at = _att_combine(os_d, ls_d, p3)
    dh, dm_hg, dm_at, g_wout, g_fw, loss8 = _outproj(x2, tgt2, mix_hg, mix_at, mixt_hg, mixt_at, w_out_full, fnw)
    dqr, dfl, dv_hg, dz_hg, g_lbl, g_hgw = _hg_bwd(p0, p1, o_hg, dm_hg, states, lbl, hg_norm_w)
    do1, do4, do16, dl1, dl4, dl16, dz_at = _att_gate_bwd(dm_at, o_at, lse, p3)
    do_ds = [do1.reshape(1, S, AT_W), do4, do16]
    dl_ds = [dl1.reshape(1, S, STAT_W), dl4, dl16]
    dqkvs = [_att_bwd(q, a, b) for q, a, b in zip(qkv_ds, do_ds, dl_ds)]
    dq_at, dk_at, dv_at = _att_bwd_combine(dqkvs, cos2, sin2)
    dps = [dqr, dfl, dv_hg, dz_hg, dq_at, dk_at, dv_at, dz_at]
    grad_x, g_nw = _inproj_bwd_x(dps, w_in_full, x2, norm_w, dh)
    g_win = _inproj_bwd_w(ut, dps)
    return loss8, grad_x, g_nw, g_win, g_lbl, g_hgw, g_wout, g_fw


def kernel(x, norm_w, w_in, hgrn_lb_logits, hg_norm_w, w_out, final_norm_w, loss_target, m_norm_w, m_w_in, m_hgrn_lb_logits, m_hg_norm_w, m_w_out, m_final_norm_w, v_norm_w, v_w_in, v_hgrn_lb_logits, v_hg_norm_w, v_w_out, v_final_norm_w):
    S = x.shape[1]
    w_in_full, w_out_full = _gather_weights(w_in[0], w_out[0])
    loss8, grad_x, g_nw, g_win, g_lbl, g_hgw, g_wout, g_fw = _local_step(
        x[0], loss_target[0], norm_w, w_in_full, hgrn_lb_logits, hg_norm_w,
        w_out_full, final_norm_w.reshape(1, D_MODEL))

    cidx = lax.axis_index("c").astype(jnp.int32).reshape(1)
    g_in4 = g_win.reshape(4, 2, 512, 1024)
    g_out4 = g_wout.reshape(4, 2, 128, D_MODEL)
    r_in, r_out = _swap_halves(g_in4, g_out4)
    cs_in = _add_half(g_in4, r_in, cidx, "add_half_in")
    cs_out = _add_half(g_out4, r_out, cidx, "add_half_out")
    x_in, x_out = _exchange_chips(cs_in, cs_out)
    chip_idx = (2 * lax.axis_index("x") + lax.axis_index("y")).astype(jnp.int32).reshape(1)
    h_in = _sum_chips(cs_in, x_in, chip_idx, "sum_chips_in")
    h_out = _sum_chips(cs_out, x_out, chip_idx, "sum_chips_out")
    s_in, s_out = _swap_reduced(h_in, h_out)

    red = _allreduce_small(g_nw, g_fw, g_hgw, g_lbl, loss8)
    loss = red[4, 0]
    grad_norm_w = red[0:1, :]
    grad_final_norm_w = red[1, :]
    grad_hg_norm_w = red[2:3, :HG_W]
    grad_lbl = jnp.concatenate([red[3:4, :HG_W], red[3:4, HG_W:]], axis=0)

    d_nw, m_nw, v_nw = _adamw(norm_w, grad_norm_w, m_norm_w, v_norm_w, "adamw_norm_w")
    grad_w_in, d_win, m_win, v_win = _adamw_halves(w_in[0], h_in, s_in, m_w_in[0], v_w_in[0], cidx, "adamw_w_in")
    d_lbl, m_lbl, v_lbl = _adamw(hgrn_lb_logits, grad_lbl, m_hgrn_lb_logits, v_hgrn_lb_logits, "adamw_lb_logits")
    d_hgw, m_hgw, v_hgw = _adamw(hg_norm_w, grad_hg_norm_w, m_hg_norm_w, v_hg_norm_w, "adamw_hg_norm_w")
    grad_w_out, d_wout, m_wout, v_wout = _adamw_halves(w_out[0], h_out, s_out, m_w_out[0], v_w_out[0], cidx, "adamw_w_out")
    d_fw, m_fw, v_fw = _adamw(final_norm_w.reshape(1, D_MODEL), grad_final_norm_w.reshape(1, D_MODEL),
                              m_final_norm_w.reshape(1, D_MODEL), v_final_norm_w.reshape(1, D_MODEL), "adamw_final_norm_w")
    e1 = lambda a: a[None]
    flat = lambda a: a.reshape(D_MODEL)
    return (loss, grad_x[None], grad_norm_w, e1(grad_w_in), grad_lbl, grad_hg_norm_w, e1(grad_w_out), grad_final_norm_w,
            d_nw, e1(d_win), d_lbl, d_hgw, e1(d_wout), flat(d_fw),
            m_nw, e1(m_win), m_lbl, m_hgw, e1(m_wout), flat(m_fw),
            v_nw, e1(v_win), v_lbl, v_hgw, e1(v_wout), flat(v_fw))
```

```python
import functools

import jax
import jax.numpy as jnp
import numpy as np
from jax import lax
from jax.experimental import pallas as pl
from jax.experimental.pallas import tpu as pltpu

F32 = jnp.float32
BF16 = jnp.bfloat16
MESH = pl.DeviceIdType.MESH

D_MODEL = 1024
HG_W = 512
AT_W = 512
HEAD_PAIR = 128
ROPE_HALF = 32
ROPE_THETA = 10000.0
EPS = 1e-6
CHUNK = 128
LEVELS = (64, 32, 16, 8, 4, 2, 1)
DIAG = 1
SUBLANES = 8
ATT_BLK = 128
ATT_QB = 4
DILATIONS = (1, 4, 16)
ATT_SCALE = 0.125
STAT_W = 128
STAT_LANES = 16
STAT_LSE_LANE = 8
NEG = -1e30
VMEM_LIMIT = 56 * 1024 * 1024

ADAM_LR = 0.001
ADAM_B1 = 0.9
ADAM_B2 = 0.999
ADAM_EPS = 1e-08
ADAM_WD = 0.01
ADAM_STEP = 10


def _iota(shape, dim):
    return lax.broadcasted_iota(jnp.int32, shape, dim)


def _dot(a, b):
    return jnp.dot(a, b, preferred_element_type=F32)


def _dot_nt(a, b):
    return lax.dot_general(a, b, (((1,), (1,)), ((), ())), preferred_element_type=F32)


def _dot_hi(a, b):
    return jnp.dot(a, b, preferred_element_type=F32, precision=lax.Precision.HIGHEST)


def _sig(v):
    return 1.0 / (1.0 + jnp.exp(-v))


def _params(sem=None, vmem=VMEM_LIMIT):
    return pltpu.CompilerParams(dimension_semantics=sem, vmem_limit_bytes=vmem)


def _full(shape):
    n = len(shape)
    return pl.BlockSpec(shape, lambda *_: (0,) * n)


def _rope_rot(y):
    n = y.shape[1]
    first = (_iota(y.shape, 1) & (2 * ROPE_HALF - 1)) < ROPE_HALF
    return jnp.where(first, pltpu.roll(y, n - ROPE_HALF, 1), pltpu.roll(y, ROPE_HALF, 1))


def _dil_spec(d, tm, width):
    return pl.BlockSpec((d, tm // d, width), lambda s: (0, s, 0))


LANES = 128


def _slab_scratch(tm, width):
    return pltpu.VMEM((width // LANES, tm, LANES), F32)


def _to_slabs(v, slabs_ref):
    for j in range(slabs_ref.shape[0]):
        slabs_ref[j] = v[:, LANES * j: LANES * (j + 1)]


def _from_slabs(slabs_ref):
    return jnp.concatenate([slabs_ref[j] for j in range(slabs_ref.shape[0])], axis=1)


def _split_residues(slabs_ref, dst_ref, d, dtype):
    nslab, tm, _ = slabs_ref.shape
    for r in range(d):
        for j in range(nslab):
            dst_ref[r, :, LANES * j: LANES * (j + 1)] = slabs_ref[j, pl.ds(r, tm // d, stride=d), :].astype(dtype)


def _merge_residues(src_ref, slabs_ref, d):
    nslab, tm, _ = slabs_ref.shape
    for r in range(d):
        for j in range(nslab):
            slabs_ref[j, pl.ds(r, tm // d, stride=d), :] = src_ref[r, :, LANES * j: LANES * (j + 1)].astype(F32)


def _inproj(x2, norm_w, w_in_full, cos2, sin2):
    S = x2.shape[0]
    tm = 256

    def body(x_ref, nw_ref, w_ref, cos_ref, sin_ref, p0_ref, p1_ref, qkv_ref, qkv4_ref, qkv16_ref, p3_ref, ut_ref, scr):
        x = x_ref[...]
        r = lax.rsqrt(jnp.mean(x * x, axis=-1, keepdims=True) + EPS)
        u = x * r * nw_ref[...]
        ub = u.astype(BF16)
        ut_ref[...] = u.T.astype(BF16)
        p0_ref[...] = _dot(ub, w_ref[0])
        p1_ref[...] = _dot(ub, w_ref[1])
        y2 = _dot(ub, w_ref[2])
        cosf = jnp.tile(cos_ref[...], (1, 8))
        sinf = jnp.tile(sin_ref[...], (1, 8))
        y3 = _dot(ub, w_ref[3])
        p3_ref[...] = y3
        qkv = jnp.concatenate([y2 * cosf + _rope_rot(y2) * sinf, y3[:, :AT_W]], axis=1)
        qkv_ref[...] = qkv.astype(BF16)
        _to_slabs(qkv, scr)
        _split_residues(scr, qkv4_ref, 4, BF16)
        _split_residues(scr, qkv16_ref, 16, BF16)

    row = lambda w: pl.BlockSpec((tm, w), lambda s: (s, 0))
    qkv_w = 3 * AT_W
    return pl.pallas_call(
        body, name="inproj", grid=(S // tm,),
        in_specs=[row(D_MODEL), _full((1, D_MODEL)), _full((4, D_MODEL, 1024)), row(128), row(128)],
        out_specs=[row(1024), row(1024), row(qkv_w), _dil_spec(4, tm, qkv_w), _dil_spec(16, tm, qkv_w), row(1024),
                   pl.BlockSpec((D_MODEL, tm), lambda s: (0, s))],
        out_shape=[jax.ShapeDtypeStruct((S, 1024), F32), jax.ShapeDtypeStruct((S, 1024), F32),
                   jax.ShapeDtypeStruct((S, qkv_w), BF16), jax.ShapeDtypeStruct((4, S // 4, qkv_w), BF16),
                   jax.ShapeDtypeStruct((16, S // 16, qkv_w), BF16), jax.ShapeDtypeStruct((S, 1024), F32),
                   jax.ShapeDtypeStruct((D_MODEL, S), BF16)],
        scratch_shapes=[_slab_scratch(tm, qkv_w)],
        compiler_params=_params(("parallel",)),
    )(x2, norm_w, w_in_full, cos2, sin2)


HG_HPS = 4
N_LEV = len(LEVELS)


def _hg_const_arrays():
    r = np.arange(CHUNK)[:, None]
    c = np.arange(CHUNK)[None, :]
    tris = np.stack([r >= c, r <= c])
    lm = [((r // (2 * m)) == (c // (2 * m))) & (r % (2 * m) >= m) & (c % (2 * m) < m) for m in LEVELS]
    dm = [(c == r - dl) & (r % DIAG >= dl) for dl in range(DIAG)]
    masks = np.stack(lm + [x.T for x in lm] + dm)
    return jnp.asarray(tris, BF16), jnp.asarray(masks, F32)


def _split2(a):
    hi = a.astype(BF16)
    return hi, (a - hi.astype(F32)).astype(BF16)


def _dot3(a, b):
    ah, al = _split2(a)
    bh, bl = _split2(b)
    n = b.shape[1]
    p = _dot(ah, jnp.concatenate([bh, bl], axis=1))
    return (p[:, :n] + p[:, n:]) + _dot(al, bh)


def _split3(a):
    a1 = a.astype(BF16)
    r1 = a - a1.astype(F32)
    a2 = r1.astype(BF16)
    return a1, a2, (r1 - a2.astype(F32)).astype(BF16)


def _tri_dot(tri, a):
    n = a.shape[1]
    p = _dot(tri, jnp.concatenate(_split3(a), axis=1))
    return (p[:, :n] + p[:, n:2 * n]) + p[:, 2 * n:]


def _dot_sel(a, sel):
    a1, a2, a3 = _split3(a)
    return (_dot(a1, sel) + _dot(a2, sel)) + _dot(a3, sel)


def _rowsum(t):
    return _dot(t.astype(BF16), jnp.ones((t.shape[1], t.shape[1]), BF16))


def _level_refs(b):
    refs = []
    pos = _iota(b.shape, 0)
    for m in LEVELS:
        if 2 * m >= SUBLANES:
            parts = [jnp.broadcast_to(b[r0 + m - 1: r0 + m, :], (2 * m, b.shape[1])) for r0 in range(0, CHUNK, 2 * m)]
            refs.append(parts[0] if len(parts) == 1 else jnp.concatenate(parts, axis=0))
        else:
            p = pos & (2 * m - 1)
            ref = b
            for off in range(-(m - 1), m + 1):
                if off != 0:
                    ref = jnp.where(p == m - 1 + off, pltpu.roll(b, off % CHUNK, 0), ref)
            refs.append(ref)
    return refs


def _hg_lb(lbl_ref):
    l0 = lbl_ref[0:1, :]
    l1 = lbl_ref[1:2, :]
    mx = jnp.maximum(l0, l1)
    e0 = jnp.exp(l0 - mx)
    e1 = jnp.exp(l1 - mx)
    p0 = e0 / (e0 + e1)
    lb = jnp.clip(p0, 1e-6, 1.0 - 1e-6)
    inside = (p0 >= 1e-6) & (p0 <= 1.0 - 1e-6)
    dlb_dl0 = jnp.where(inside, p0 * (e1 / (e0 + e1)), 0.0)
    return lb, dlb_dl0


def _sigmoid(v):
    return 0.5 * jnp.tanh(0.5 * v) + 0.5


def _hg_gates(qr, fl, lb):
    sig = _sigmoid(fl)
    f = lb + (1.0 - lb) * sig
    g = jnp.log(f)
    k = (1.0 - lb) * (1.0 - sig)
    sq = _sigmoid(qr)
    q = qr * sq
    return sig, f, g, k, sq, q


def _hg_levels(q, k, b, mk_ref):
    refs = _level_refs(b)
    a = jnp.zeros((CHUNK, CHUNK), F32)
    es, qts, kts = [], [], []
    for i in range(N_LEV):
        diff = b - refs[i]
        e = jnp.exp(jnp.minimum(diff, -diff))
        qt = (q * e).astype(BF16)
        kt = (k * e).astype(BF16)
        a = a + _dot_nt(qt, kt) * mk_ref[i]
        es.append(e); qts.append(qt); kts.append(kt)
    return a, es, qts, kts


def _hg_specs(nc, rev):
    cc = (lambda c: nc - 1 - c) if rev else (lambda c: c)
    w = 128 * HG_HPS
    blk = lambda off: pl.BlockSpec((CHUNK, w), lambda h, c: (cc(c), h + off))
    vec = pl.BlockSpec((1, w), lambda h, c: (0, h))
    lb2 = pl.BlockSpec((2, w), lambda h, c: (0, h))
    st = pl.BlockSpec((1, HG_HPS, 128, 128), lambda h, c: (cc(c), h, 0, 0))
    consts = [_full((2, CHUNK, CHUNK)), _full((2 * N_LEV + DIAG, CHUNK, CHUNK))]
    return blk, vec, lb2, st, consts


def _hg_fwd(p0, p1, lbl, gw):
    S = p0.shape[0]
    nc = S // CHUNK
    ng = 4 // HG_HPS

    def body(qr_ref, fl_ref, v_ref, z_ref, lbl_ref, gw_ref, tri_ref, mk_ref,
             o_ref, mix_ref, mixt_ref, st_ref, state):
        c = pl.program_id(1)

        @pl.when(c == 0)
        def _():
            state[...] = jnp.zeros_like(state)

        lb_all, _ = _hg_lb(lbl_ref)
        heads = [slice(128 * hh, 128 * (hh + 1)) for hh in range(HG_HPS)]
        qs, ks, bs, mats = [], [], [], []
        for sl in heads:
            _, _, g, k, _, q = _hg_gates(qr_ref[:, sl], fl_ref[:, sl], lb_all[:, sl])
            qs.append(q); ks.append(k); bs.append(_tri_dot(tri_ref[0], g))
        for hh in range(HG_HPS):
            a, _, _, _ = _hg_levels(qs[hh], ks[hh], bs[hh], mk_ref)
            mats.append(a + _rowsum(qs[hh] * ks[hh]) * mk_ref[2 * N_LEV])
        for hh, sl in enumerate(heads):
            q, k, b, a, v, z = qs[hh], ks[hh], bs[hh], mats[hh], v_ref[:, sl], z_ref[:, sl]
            b_last = b[CHUNK - 1: CHUNK, :]
            st = state[hh]
            st_ref[0, hh] = st
            o = _dot_nt((q * jnp.exp(b)).astype(BF16), st.astype(BF16)) + _dot(a.astype(BF16), v.astype(BF16))
            state[hh] = st * jnp.exp(b_last) + _dot3(v.T, k * jnp.exp(b_last - b))
            o_ref[:, sl] = o
            rs = lax.rsqrt(jnp.mean(o * o, axis=-1, keepdims=True) + EPS)
            mixed = o * rs * gw_ref[:, sl] * (z * _sigmoid(z))
            mix_ref[:, sl] = mixed.astype(BF16)
            mixt_ref[sl, :] = mixed.T.astype(BF16)

    blk, vec, lb2, st_spec, consts = _hg_specs(nc, False)
    tris, masks = _hg_const_arrays()
    return pl.pallas_call(
        body, name="hg_fwd", grid=(ng, nc),
        in_specs=[blk(0), blk(ng), blk(0), blk(ng), lb2, vec] + consts,
        out_specs=[blk(0), blk(0), pl.BlockSpec((128 * HG_HPS, CHUNK), lambda h, c: (h, c)), st_spec],
        out_shape=[jax.ShapeDtypeStruct((S, HG_W), F32), jax.ShapeDtypeStruct((S, HG_W), BF16),
                   jax.ShapeDtypeStruct((HG_W, S), BF16), jax.ShapeDtypeStruct((nc, 4, 128, 128), F32)],
        scratch_shapes=[pltpu.VMEM((HG_HPS, 128, 128), F32)],
        compiler_params=_params(("parallel", "arbitrary")),
    )(p0, p0, p1, p1, lbl, gw, tris, masks)


def _hg_bwd(p0, p1, o_raw, dm, states, lbl, gw):
    S = p0.shape[0]
    nc = S // CHUNK
    ng = 4 // HG_HPS
    w = 128 * HG_HPS

    def body(qr_ref, fl_ref, v_ref, z_ref, o_ref, dm_ref, st_ref, lbl_ref, gw_ref, tri_ref, mk_ref, mkb_ref,
             dqr_ref, dfl_ref, dv_ref, dz_ref, glbl_ref, ggw_ref, dstate, carry, acc_lb, acc_gw):
        c = pl.program_id(1)

        @pl.when(c == 0)
        def _():
            dstate[...] = jnp.zeros_like(dstate)
            carry[...] = jnp.zeros_like(carry)
            acc_lb[...] = jnp.zeros_like(acc_lb)
            acc_gw[...] = jnp.zeros_like(acc_gw)

        lb_all, dlb_dl0 = _hg_lb(lbl_ref)
        heads = [slice(128 * hh, 128 * (hh + 1)) for hh in range(HG_HPS)]
        diag_mask = mk_ref[2 * N_LEV]
        gates, bs, dos = [], [], []
        for sl in heads:
            gt = _hg_gates(qr_ref[:, sl], fl_ref[:, sl], lb_all[:, sl])
            gates.append(gt)
            bs.append(_tri_dot(tri_ref[0], gt[2]))
        for hh, sl in enumerate(heads):
            o, z, dmix, gwv = o_ref[:, sl], z_ref[:, sl], dm_ref[:, sl], gw_ref[:, sl]
            rs = lax.rsqrt(jnp.mean(o * o, axis=-1, keepdims=True) + EPS)
            oh = o * rs
            sz = _sigmoid(z)
            dz_ref[:, sl] = (dmix * (oh * gwv) * (sz * (1.0 + z * (1.0 - sz)))).astype(BF16)
            don = dmix * (z * sz)
            acc_gw[0:1, sl] += jnp.sum(don * oh, axis=0, keepdims=True)
            dy = don * gwv
            dos.append(rs * (dy - oh * jnp.mean(dy * oh, axis=-1, keepdims=True)))
        inter = []
        for hh, sl in enumerate(heads):
            _, _, _, k, _, q = gates[hh]
            b, do, v = bs[hh], dos[hh], v_ref[:, sl]
            b_last = b[CHUNK - 1: CHUNK, :]
            eb = jnp.exp(b)
            edec = jnp.exp(b_last - b)
            dst = dstate[hh]
            dq = _dot3(do, st_ref[0, hh]) * eb
            dk = _dot3(v, dst) * edec
            da = _dot_nt(do.astype(BF16), v.astype(BF16))
            dv_state = _dot_nt((k * edec).astype(BF16), dst.astype(BF16))
            dstate[hh] = dst * jnp.exp(b_last) + _dot3(do.T, q * eb)
            inter.append((dq, dk, da, dv_state))
        for hh, sl in enumerate(heads):
            sig, f, _, k, sq, q = gates[hh]
            dq, dk, da, dv_state = inter[hh]
            b = bs[hh]
            db = q * dq - k * dk
            dab = da.astype(BF16)
            datb = da.T.astype(BF16)
            refs = _level_refs(b)
            a = _rowsum(q * k) * diag_mask
            for i in range(N_LEV):
                diff = b - refs[i]
                e = jnp.exp(jnp.minimum(diff, -diff))
                qt = (q * e).astype(BF16)
                kt = (k * e).astype(BF16)
                a = a + _dot_nt(qt, kt) * mk_ref[i]
                gq = _dot(dab * mkb_ref[i], kt)
                gk = _dot(datb * mkb_ref[N_LEV + i], qt)
                dq = dq + e * gq
                dk = dk + e * gk
                db = db + (qt.astype(F32) * gq - kt.astype(F32) * gk)
            dc = _rowsum(da * diag_mask)
            dq = dq + dc * k
            dk = dk + dc * q
            dv_ref[:, sl] = (_dot(a.T.astype(BF16), dos[hh].astype(BF16)) + dv_state).astype(BF16)
            dg = _tri_dot(tri_ref[1], db) + carry[0:1, sl]
            carry[0:1, sl] += jnp.sum(db, axis=0, keepdims=True)
            lb = lb_all[:, sl]
            qr = qr_ref[:, sl]
            t = dg / f - dk
            dfl_ref[:, sl] = (t * (1.0 - lb) * sig * (1.0 - sig)).astype(BF16)
            acc_lb[0:1, sl] += jnp.sum(t * (1.0 - sig), axis=0, keepdims=True)
            dqr_ref[:, sl] = (dq * (sq * (1.0 + qr * (1.0 - sq)))).astype(BF16)

        @pl.when(c == nc - 1)
        def _():
            gl0 = acc_lb[0:1, :] * dlb_dl0
            glbl_ref[0:1, :] = gl0
            glbl_ref[1:2, :] = -gl0
            ggw_ref[...] = acc_gw[0:1, :]

    blk, vec, lb2, st_spec, consts = _hg_specs(nc, True)
    tris, masks = _hg_const_arrays()
    act = jax.ShapeDtypeStruct((S, HG_W), BF16)
    return pl.pallas_call(
        body, name="hg_bwd", grid=(ng, nc),
        in_specs=[blk(0), blk(ng), blk(0), blk(ng), blk(0), blk(0), st_spec, lb2, vec] + consts + consts[1:],
        out_specs=[blk(0), blk(0), blk(0), blk(0), lb2, vec],
        out_shape=[act, act, act, act, jax.ShapeDtypeStruct((2, HG_W), F32), jax.ShapeDtypeStruct((1, HG_W), F32)],
        scratch_shapes=[pltpu.VMEM((HG_HPS, 128, 128), F32), pltpu.VMEM((8, w), F32),
                        pltpu.VMEM((8, w), F32), pltpu.VMEM((8, w), F32)],
        compiler_params=_params(("parallel", "arbitrary")),
    )(p0, p0, p1, p1, o_raw, dm, states, lbl, gw, tris, masks, masks.astype(BF16))


def _stat_head(lane):
    return lane >> 4


def _stat_lane(lane):
    return lane & (STAT_LANES - 1)


def _o_head(lane):
    return lane >> 6


def _att_bias():
    qi = np.arange(ATT_BLK)[:, None]
    kj = np.arange(2 * ATT_BLK)[None, :]
    band = (kj >= qi) & (kj <= qi + ATT_BLK)
    qm = np.stack([band & (kj >= ATT_BLK), band])
    cur = (kj < ATT_BLK) & (qi <= kj)
    km = np.stack([cur, cur | ((kj >= ATT_BLK) & (qi >= kj - ATT_BLK))])
    to_bias = lambda m: jnp.asarray(np.where(m, 0.0, NEG), F32)
    return to_bias(qm), to_bias(km)


def _att_fwd(qkv_d):
    d, L, _ = qkv_d.shape
    qb = ATT_QB
    rows = qb * ATT_BLK

    def body(q_ref, kp_ref, kc_ref, vp_ref, vc_ref, bias_ref, o_ref, lse_ref, s_scr, p_scr):
        first = _iota((ATT_BLK, HEAD_PAIR), 1) < 64
        head_of_lane = _stat_head(_iota((ATT_BLK, STAT_W), 1))
        pairs = [slice(HEAD_PAIR * hp, HEAD_PAIR * (hp + 1)) for hp in range(4)]
        blk = lambda i: slice(ATT_BLK * i, ATT_BLK * (i + 1))

        def keys(i, sl, prev_ref, cur_ref):
            before = prev_ref[0, :, sl] if i == 0 else cur_ref[0, blk(i - 1), sl]
            return jnp.concatenate([before, cur_ref[0, blk(i), sl]], axis=0)

        for i in range(qb):
            for hp, sl in enumerate(pairs):
                q2 = q_ref[0, blk(i), sl] * ATT_SCALE
                zero = jnp.zeros_like(q2)
                qs = jnp.concatenate([jnp.where(first, q2, zero), jnp.where(first, zero, q2)], axis=0)
                s_scr[4 * i + hp] = _dot_nt(qs, keys(i, sl, kp_ref, kc_ref))
        stats = []
        for i in range(qb):
            bias = bias_ref[jnp.minimum(pl.program_id(1), 1)] if i == 0 else bias_ref[1]
            bias2 = jnp.concatenate([bias, bias], axis=0)
            for hp in range(4):
                s = s_scr[4 * i + hp] + bias2
                m = jnp.max(s, axis=-1, keepdims=True)
                p = jnp.exp(s - m)
                l = jnp.sum(p, axis=-1, keepdims=True)
                p_scr[4 * i + hp] = p.astype(BF16)
                stats.append((l, m + jnp.log(l)))
        for i in range(qb):
            lse_blk = jnp.zeros((ATT_BLK, STAT_W), F32)
            for hp, sl in enumerate(pairs):
                l, lse = stats[4 * i + hp]
                o = _dot(p_scr[4 * i + hp], keys(i, sl, vp_ref, vc_ref)) / l
                o_ref[0, blk(i), sl] = jnp.where(first, o[:ATT_BLK], o[ATT_BLK:]).astype(BF16)
                lse_blk = jnp.where(head_of_lane == 2 * hp, lse[:ATT_BLK],
                                    jnp.where(head_of_lane == 2 * hp + 1, lse[ATT_BLK:], lse_blk))
            lse_ref[0, blk(i), :] = lse_blk

    cur = lambda j: pl.BlockSpec((1, rows, AT_W), lambda r, n: (r, n, j))
    prev = lambda j: pl.BlockSpec((1, ATT_BLK, AT_W), lambda r, n: (r, jnp.maximum(qb * n - 1, 0), j))
    return pl.pallas_call(
        body, name=f"att_fwd_d{d}", grid=(d, L // rows),
        in_specs=[cur(0), prev(1), cur(1), prev(2), cur(2), _full((2, ATT_BLK, 2 * ATT_BLK))],
        out_specs=[pl.BlockSpec((1, rows, AT_W), lambda r, n: (r, n, 0)), pl.BlockSpec((1, rows, STAT_W), lambda r, n: (r, n, 0))],
        out_shape=[jax.ShapeDtypeStruct((d, L, AT_W), BF16), jax.ShapeDtypeStruct((d, L, STAT_W), F32)],
        scratch_shapes=[pltpu.VMEM((4 * qb, 2 * ATT_BLK, 2 * ATT_BLK), F32), pltpu.VMEM((4 * qb, 2 * ATT_BLK, 2 * ATT_BLK), BF16)],
        compiler_params=_params(("parallel", "parallel")),
    )(qkv_d, qkv_d, qkv_d, qkv_d, qkv_d, _att_bias()[0])


def _att_combine(os_d, ls_d, p3):
    S = p3.shape[0]
    tm = 512

    def body(oa_ref, ob4_ref, oc16_ref, la_ref, lb4_ref, lc16_ref, z_ref,
             oat_ref, lse_ref, mix_ref, mixt_ref, ob_scr, oc_scr, lb_scr, lc_scr):
        _merge_residues(ob4_ref, ob_scr, 4)
        _merge_residues(oc16_ref, oc_scr, 16)
        _merge_residues(lb4_ref, lb_scr, 4)
        _merge_residues(lc16_ref, lc_scr, 16)
        ls = (la_ref[...], lb_scr[0], lc_scr[0])
        mx = jnp.maximum(jnp.maximum(ls[0], ls[1]), ls[2])
        es = [jnp.exp(l - mx) for l in ls]
        zs = es[0] + es[1] + es[2]
        lse_ref[...] = mx + jnp.log(zs)
        spread = ((_o_head(_iota((STAT_W, AT_W), 1)) == _stat_head(_iota((STAT_W, AT_W), 0)))
                  & (_stat_lane(_iota((STAT_W, AT_W), 0)) == 0)).astype(BF16)
        os_ = (oa_ref[...].astype(F32), _from_slabs(ob_scr), _from_slabs(oc_scr))
        o = jnp.zeros((tm, AT_W), F32)
        for e, oi in zip(es, os_):
            hi, lo = _split2(e / zs)
            o = o + (_dot(hi, spread) + _dot(lo, spread)) * oi
        oat_ref[...] = o
        z = z_ref[...]
        mixed = o * (z * _sig(z))
        mix_ref[...] = mixed.astype(BF16)
        mixt_ref[...] = mixed.T.astype(BF16)

    row = lambda w: pl.BlockSpec((tm, w), lambda s: (s, 0))
    return pl.pallas_call(
        body, name="att_combine", grid=(S // tm,),
        in_specs=[row(AT_W), _dil_spec(4, tm, AT_W), _dil_spec(16, tm, AT_W),
                  row(STAT_W), _dil_spec(4, tm, STAT_W), _dil_spec(16, tm, STAT_W),
                  pl.BlockSpec((tm, AT_W), lambda s: (s, 1))],
        out_specs=[row(AT_W), row(STAT_W), row(AT_W), pl.BlockSpec((AT_W, tm), lambda s: (0, s))],
        out_shape=[jax.ShapeDtypeStruct((S, AT_W), F32), jax.ShapeDtypeStruct((S, STAT_W), F32),
                   jax.ShapeDtypeStruct((S, AT_W), BF16), jax.ShapeDtypeStruct((AT_W, S), BF16)],
        scratch_shapes=[_slab_scratch(tm, AT_W), _slab_scratch(tm, AT_W), _slab_scratch(tm, STAT_W), _slab_scratch(tm, STAT_W)],
        compiler_params=_params(("parallel",)),
    )(os_d[0].reshape(S, AT_W), os_d[1], os_d[2], ls_d[0].reshape(S, STAT_W), ls_d[1], ls_d[2], p3)


def _att_gate_bwd(dm_at, o_at, lse, p3):
    S = p3.shape[0]
    tm = 512

    def body(dm_ref, o_ref, l_ref, z_ref, do_ref, do4_ref, do16_ref, dl_ref, dl4_ref, dl16_ref, dz_ref, do_scr, dl_scr):
        o = o_ref[...]
        z = z_ref[...]
        dm = dm_ref[...]
        sz = _sig(z)
        dz_ref[...] = (dm * o * (sz * (1.0 + z * (1.0 - sz)))).astype(BF16)
        do = dm * (z * sz)
        do_ref[...] = do.astype(BF16)
        gather = (_o_head(_iota((AT_W, STAT_W), 0)) == _stat_head(_iota((AT_W, STAT_W), 1))).astype(BF16)
        dl = jnp.where(_stat_lane(_iota((tm, STAT_W), 1)) < STAT_LSE_LANE, _dot_sel(do * o, gather), l_ref[...])
        dl_ref[...] = dl
        _to_slabs(do, do_scr)
        _to_slabs(dl, dl_scr)
        _split_residues(do_scr, do4_ref, 4, BF16)
        _split_residues(do_scr, do16_ref, 16, BF16)
        _split_residues(dl_scr, dl4_ref, 4, F32)
        _split_residues(dl_scr, dl16_ref, 16, F32)

    row = lambda w: pl.BlockSpec((tm, w), lambda s: (s, 0))
    sds = jax.ShapeDtypeStruct
    return pl.pallas_call(
        body, name="att_gate_bwd", grid=(S // tm,),
        in_specs=[row(AT_W), row(AT_W), row(STAT_W), pl.BlockSpec((tm, AT_W), lambda s: (s, 1))],
        out_specs=[row(AT_W), _dil_spec(4, tm, AT_W), _dil_spec(16, tm, AT_W),
                   row(STAT_W), _dil_spec(4, tm, STAT_W), _dil_spec(16, tm, STAT_W), row(AT_W)],
        out_shape=[sds((S, AT_W), BF16), sds((4, S // 4, AT_W), BF16), sds((16, S // 16, AT_W), BF16),
                   sds((S, STAT_W), F32), sds((4, S // 4, STAT_W), F32), sds((16, S // 16, STAT_W), F32),
                   sds((S, AT_W), BF16)],
        scratch_shapes=[_slab_scratch(tm, AT_W), _slab_scratch(tm, STAT_W)],
        compiler_params=_params(("parallel",)),
    )(dm_at, o_at, lse, p3)


def _att_bwd(qkv_d, do_d, dl_d):
    d, L, _ = qkv_d.shape
    nb = L // ATT_BLK
    qb = ATT_QB
    rows = qb * ATT_BLK
    nsteps = L // rows

    def body(qc_ref, qn_ref, kp_ref, kc_ref, vp_ref, vc_ref, ac_ref, an_ref, lc_ref, ln_ref, bq_ref, bk_ref, dqkv_ref,
             s_scr, dp_scr, st_scr, dpt_scr, ds_scr, pt_scr, dst_scr):
        n = pl.program_id(1)
        first = _iota((ATT_BLK, HEAD_PAIR), 1) < 64
        pairs = [slice(HEAD_PAIR * hp, HEAD_PAIR * (hp + 1)) for hp in range(4)]
        blk = lambda i: slice(ATT_BLK * i, ATT_BLK * (i + 1))

        def stack(t):
            zero = jnp.zeros_like(t)
            return jnp.concatenate([jnp.where(first, t, zero), jnp.where(first, zero, t)], axis=0)

        def unstack(t2):
            return jnp.where(first, t2[:ATT_BLK], t2[ATT_BLK:])

        def with_prev(i, sl, prev_ref, cur_ref):
            before = prev_ref[0, :, sl] if i == 0 else cur_ref[0, blk(i - 1), sl]
            return jnp.concatenate([before, cur_ref[0, blk(i), sl]], axis=0)

        def with_next(i, sl, cur_ref, next_ref):
            after = next_ref[0, :, sl] if i == qb - 1 else cur_ref[0, blk(i + 1), sl]
            return jnp.concatenate([cur_ref[0, blk(i), sl], after], axis=0)

        for i in range(qb):
            for hp, sl in enumerate(pairs):
                j = 4 * i + hp
                s_scr[j] = _dot_nt(stack(qc_ref[0, blk(i), sl] * ATT_SCALE), with_prev(i, sl, kp_ref, kc_ref))
                dp_scr[j] = _dot_nt(stack(ac_ref[0, blk(i), sl]), with_prev(i, sl, vp_ref, vc_ref))
                st_scr[j] = _dot_nt(stack(kc_ref[0, blk(i), sl] * ATT_SCALE), with_next(i, sl, qc_ref, qn_ref))
                dpt_scr[j] = _dot_nt(stack(vc_ref[0, blk(i), sl]), with_next(i, sl, ac_ref, an_ref))
        for i in range(qb):
            bias = bq_ref[jnp.minimum(n, 1)] if i == 0 else bq_ref[1]
            bias_t = bk_ref[jnp.minimum(nsteps - 1 - n, 1)] if i == qb - 1 else bk_ref[1]
            bias2 = jnp.concatenate([bias, bias], axis=0)
            bias_t2 = jnp.concatenate([bias_t, bias_t], axis=0)
            dl_c = lc_ref[0, blk(i), :]
            dl_t = with_next(i, slice(None), lc_ref, ln_ref).T
            for hp in range(4):
                j = 4 * i + hp
                at = [STAT_LANES * (2 * hp), STAT_LANES * (2 * hp + 1)]
                col = lambda t, o: jnp.concatenate([t[:, a + o: a + o + 1] for a in at], axis=0)
                p = jnp.exp(s_scr[j] + bias2 - col(dl_c, STAT_LSE_LANE))
                ds_scr[j] = (p * (dp_scr[j] - col(dl_c, 0))).astype(BF16)
                row = lambda t, o: jnp.concatenate([jnp.broadcast_to(t[a + o: a + o + 1, :], (ATT_BLK, 2 * ATT_BLK)) for a in at], axis=0)
                pt = jnp.exp(st_scr[j] + bias_t2 - row(dl_t, STAT_LSE_LANE))
                pt_scr[j] = pt.astype(BF16)
                dst_scr[j] = (pt * (dpt_scr[j] - row(dl_t, 0))).astype(BF16)
        for i in range(qb):
            for hp, sl in enumerate(pairs):
                j = 4 * i + hp
                dq = unstack(_dot(ds_scr[j], with_prev(i, sl, kp_ref, kc_ref))) * ATT_SCALE
                dk = unstack(_dot(dst_scr[j], with_next(i, sl, qc_ref, qn_ref))) * ATT_SCALE
                dv = unstack(_dot(pt_scr[j], with_next(i, sl, ac_ref, an_ref)))
                dqkv_ref[0, blk(i), sl] = dq.astype(BF16)
                dqkv_ref[0, blk(i), AT_W + HEAD_PAIR * hp: AT_W + HEAD_PAIR * (hp + 1)] = dk.astype(BF16)
                dqkv_ref[0, blk(i), 2 * AT_W + HEAD_PAIR * hp: 2 * AT_W + HEAD_PAIR * (hp + 1)] = dv.astype(BF16)

    cur = lambda j: pl.BlockSpec((1, rows, AT_W), lambda r, n: (r, n, j))
    prev = lambda j: pl.BlockSpec((1, ATT_BLK, AT_W), lambda r, n: (r, jnp.maximum(qb * n - 1, 0), j))
    nxt_blk = lambda n: jnp.minimum(qb * (n + 1), nb - 1)
    sq = (4 * qb, 2 * ATT_BLK, 2 * ATT_BLK)
    return pl.pallas_call(
        body, name=f"att_bwd_d{d}", grid=(d, nsteps),
        in_specs=[cur(0), pl.BlockSpec((1, ATT_BLK, AT_W), lambda r, n: (r, nxt_blk(n), 0)), prev(1), cur(1), prev(2), cur(2),
                  pl.BlockSpec((1, rows, AT_W), lambda r, n: (r, n, 0)),
                  pl.BlockSpec((1, ATT_BLK, AT_W), lambda r, n: (r, nxt_blk(n), 0)),
                  pl.BlockSpec((1, rows, STAT_W), lambda r, n: (r, n, 0)),
                  pl.BlockSpec((1, ATT_BLK, STAT_W), lambda r, n: (r, nxt_blk(n), 0)),
                  _full((2, ATT_BLK, 2 * ATT_BLK)), _full((2, ATT_BLK, 2 * ATT_BLK))],
        out_specs=pl.BlockSpec((1, rows, 3 * AT_W), lambda r, n: (r, n, 0)),
        out_shape=jax.ShapeDtypeStruct((d, L, 3 * AT_W), BF16),
        scratch_shapes=[pltpu.VMEM(sq, F32)] * 4 + [pltpu.VMEM(sq, BF16)] * 3,
        compiler_params=_params(("parallel", "parallel")),
    )(qkv_d, qkv_d, qkv_d, qkv_d, qkv_d, qkv_d, do_d, do_d, dl_d, dl_d, *_att_bias())


def _att_bwd_combine(dqkvs, cos2, sin2):
    S = dqkvs[0].shape[1]
    tm = 512

    def body(a_ref, b4_ref, c16_ref, cos_ref, sin_ref, dq_ref, dk_ref, dv_ref, b_scr, c_scr):
        _merge_residues(b4_ref, b_scr, 4)
        _merge_residues(c16_ref, c_scr, 16)
        t = a_ref[...].astype(F32) + _from_slabs(b_scr) + _from_slabs(c_scr)
        dy = t[:, : 2 * AT_W]
        cosf = jnp.tile(cos_ref[...], (1, 8))
        sinf = jnp.tile(sin_ref[...], (1, 8))
        dx = dy * cosf - _rope_rot(dy) * sinf
        dq_ref[...] = dx[:, :AT_W].astype(BF16)
        dk_ref[...] = dx[:, AT_W:].astype(BF16)
        dv_ref[...] = t[:, 2 * AT_W:].astype(BF16)

    row = lambda w: pl.BlockSpec((tm, w), lambda s: (s, 0))
    act = jax.ShapeDtypeStruct((S, AT_W), BF16)
    return pl.pallas_call(
        body, name="att_bwd_combine", grid=(S // tm,),
        in_specs=[row(3 * AT_W), _dil_spec(4, tm, 3 * AT_W), _dil_spec(16, tm, 3 * AT_W), row(128), row(128)],
        out_specs=[row(AT_W), row(AT_W), row(AT_W)],
        out_shape=[act, act, act],
        scratch_shapes=[_slab_scratch(tm, 3 * AT_W), _slab_scratch(tm, 3 * AT_W)],
        compiler_params=_params(("parallel",)),
    )(dqkvs[0].reshape(S, 3 * AT_W), dqkvs[1], dqkvs[2], cos2, sin2)


def _outproj(x2, tgt2, mix_hg, mix_at, mixt_hg, mixt_at, w_out_full, fnw):
    S = x2.shape[0]
    tm = 256
    ns = S // tm

    def body(x_ref, t_ref, mh_ref, ma_ref, mht_ref, mat_ref, w_ref, fw_ref,
             dh_ref, dmh_ref, dma_ref, gw_ref, gfw_ref, loss_ref):
        s = pl.program_id(0)

        @pl.when(s == 0)
        def _():
            gw_ref[...] = jnp.zeros_like(gw_ref)
            gfw_ref[...] = jnp.zeros_like(gfw_ref)
            loss_ref[...] = jnp.zeros_like(loss_ref)

        y = _dot(mh_ref[...], w_ref[:HG_W, :]) + _dot(ma_ref[...], w_ref[HG_W:, :])
        h = x_ref[...] + y
        r = lax.rsqrt(jnp.mean(h * h, axis=-1, keepdims=True) + EPS)
        hn = h * r
        fw = fw_ref[...]
        err = hn * fw - t_ref[...]
        loss_ref[...] += 0.5 * jnp.sum(jnp.mean(err * err, axis=-1, keepdims=True))
        dout = err * (1.0 / D_MODEL)
        gfw_ref[...] += jnp.sum(dout * hn, axis=0, keepdims=True)
        dhn = dout * fw
        dh = r * (dhn - hn * jnp.mean(dhn * hn, axis=-1, keepdims=True))
        dh_ref[...] = dh
        dhb = dh.astype(BF16)
        dmh_ref[...] = _dot_nt(dhb, w_ref[:HG_W, :])
        dma_ref[...] = _dot_nt(dhb, w_ref[HG_W:, :])
        gw_ref[:HG_W, :] += _dot(mht_ref[...], dhb)
        gw_ref[HG_W:, :] += _dot(mat_ref[...], dhb)

    row = lambda w: pl.BlockSpec((tm, w), lambda s: (s, 0))
    colb = pl.BlockSpec((HG_W, tm), lambda s: (0, s))
    return pl.pallas_call(
        body, name="outproj", grid=(ns,),
        in_specs=[row(D_MODEL), row(D_MODEL), row(HG_W), row(AT_W), colb, colb,
                  _full((D_MODEL, D_MODEL)), _full((1, D_MODEL))],
        out_specs=[row(D_MODEL), row(HG_W), row(AT_W), _full((D_MODEL, D_MODEL)), _full((1, D_MODEL)), _full((8, 128))],
        out_shape=[jax.ShapeDtypeStruct((S, D_MODEL), F32), jax.ShapeDtypeStruct((S, HG_W), F32),
                   jax.ShapeDtypeStruct((S, AT_W), F32), jax.ShapeDtypeStruct((D_MODEL, D_MODEL), F32),
                   jax.ShapeDtypeStruct((1, D_MODEL), F32), jax.ShapeDtypeStruct((8, 128), F32)],
        compiler_params=_params(("arbitrary",)),
    )(x2, tgt2, mix_hg, mix_at, mixt_hg, mixt_at, w_out_full, fnw)


def _inproj_bwd_x(dps, w_in_full, x2, norm_w, dh):
    S = x2.shape[0]
    tm = 256

    def body(d0, d1, d2, d3, d4, d5, d6, d7, w_ref, x_ref, nw_ref, dh_ref, gx_ref, gnw_ref):
        s = pl.program_id(0)

        @pl.when(s == 0)
        def _():
            gnw_ref[...] = jnp.zeros_like(gnw_ref)

        du = jnp.zeros((tm, D_MODEL), F32)
        for i, dref in enumerate((d0, d1, d2, d3, d4, d5, d6, d7)):
            j, half = divmod(i, 2)
            du = du + _dot_nt(dref[...], w_ref[j, :, 512 * half: 512 * (half + 1)])
        x = x_ref[...]
        r = lax.rsqrt(jnp.mean(x * x, axis=-1, keepdims=True) + EPS)
        xh = x * r
        gnw_ref[...] += jnp.sum(du * xh, axis=0, keepdims=True)
        dun = du * nw_ref[...]
        gx_ref[...] = dh_ref[...] + r * (dun - xh * jnp.mean(dun * xh, axis=-1, keepdims=True))

    row = lambda w: pl.BlockSpec((tm, w), lambda s: (s, 0))
    return pl.pallas_call(
        body, name="inproj_bwd_x", grid=(S // tm,),
        in_specs=[row(512)] * 8 + [_full((4, D_MODEL, 1024)), row(D_MODEL), _full((1, D_MODEL)), row(D_MODEL)],
        out_specs=[row(D_MODEL), _full((1, D_MODEL))],
        out_shape=[jax.ShapeDtypeStruct((S, D_MODEL), F32), jax.ShapeDtypeStruct((1, D_MODEL), F32)],
        compiler_params=_params(("arbitrary",)),
    )(*dps, w_in_full, x2, norm_w, dh)


def _inproj_bwd_w(ut, dps):
    S = ut.shape[1]
    tm = 256

    def body(ut_ref, d0, d1, d2, d3, d4, d5, d6, d7, g_ref):
        @pl.when(pl.program_id(0) == 0)
        def _():
            g_ref[...] = jnp.zeros_like(g_ref)

        utb = ut_ref[...]
        for i, dref in enumerate((d0, d1, d2, d3, d4, d5, d6, d7)):
            j, half = divmod(i, 2)
            g_ref[j, :, 512 * half: 512 * (half + 1)] += _dot(utb, dref[...])

    return pl.pallas_call(
        body, name="inproj_bwd_w", grid=(S // tm,),
        in_specs=[pl.BlockSpec((D_MODEL, tm), lambda s: (0, s))] + [pl.BlockSpec((tm, 512), lambda s: (s, 0))] * 8,
        out_specs=_full((4, D_MODEL, 1024)),
        out_shape=jax.ShapeDtypeStruct((4, D_MODEL, 1024), F32),
        compiler_params=_params(("arbitrary",)),
    )(ut, *dps)


def _adamw_update(gg, w_ref, m_ref, v_ref, d_ref, nm_ref, nv_ref):
    nm = ADAM_B1 * m_ref[...] + (1.0 - ADAM_B1) * gg
    nv = ADAM_B2 * v_ref[...] + (1.0 - ADAM_B2) * (gg * gg)
    m_hat = nm / (1.0 - ADAM_B1 ** ADAM_STEP)
    v_hat = nv / (1.0 - ADAM_B2 ** ADAM_STEP)
    d_ref[...] = -ADAM_LR * (m_hat / (jnp.sqrt(v_hat) + ADAM_EPS) + ADAM_WD * w_ref[...])
    nm_ref[...] = nm
    nv_ref[...] = nv


def _adamw(w, g, m, v, name):
    rows, cols = w.shape
    tr = min(rows, 256)

    def body(w_ref, g_ref, m_ref, v_ref, d_ref, nm_ref, nv_ref):
        _adamw_update(g_ref[...], w_ref, m_ref, v_ref, d_ref, nm_ref, nv_ref)

    spec = pl.BlockSpec((tr, cols), lambda i: (i, 0))
    sds = jax.ShapeDtypeStruct((rows, cols), F32)
    return pl.pallas_call(
        body, name=name, grid=(rows // tr,),
        in_specs=[spec] * 4, out_specs=[spec] * 3, out_shape=[sds] * 3,
        compiler_params=_params(("parallel",)),
    )(w, g, m, v)


def _place():
    return lax.axis_index("x"), lax.axis_index("y"), lax.axis_index("c")


def _gather_weights(w_in_s, w_out_s):
    def body(win_ref, wout_ref, fin_ref, fout_ref, send_sems, recv_sems):
        x, y, c = _place()
        me = (x, y, c)
        sib = (x, y, 1 - c)
        mine = 2 * x + y
        fin_ref[mine] = win_ref[...].astype(BF16)
        fout_ref[mine] = wout_ref[...].astype(BF16)
        chips = [(1 - x, y), (x, 1 - y), (1 - x, 1 - y)]

        def halves(chip, half):
            return (fin_ref.at[chip, pl.ds(half * 512, 512), :], fout_ref.at[chip, pl.ds(half * 128, 128), :])

        def copy(k, ref, to):
            return pltpu.make_async_remote_copy(src_ref=ref, dst_ref=ref, send_sem=send_sems.at[k],
                                                recv_sem=recv_sems.at[k], device_id=to, device_id_type=MESH)

        first, passed = [], []
        for j, (cx, cy) in enumerate(chips):
            for a, ref in enumerate(halves(mine, c)):
                first.append(copy(2 * j + a, ref, (cx, cy, c)))
        for cp in first:
            cp.start()
        for j, (cx, cy) in enumerate(chips):
            for a, ref in enumerate(halves(2 * cx + cy, c)):
                copy(2 * j + a, ref, me).wait_recv()
                fwd = copy(6 + 2 * j + a, ref, sib)
                fwd.start()
                passed.append(fwd)
        for j, (cx, cy) in enumerate(chips):
            for a, ref in enumerate(halves(2 * cx + cy, 1 - c)):
                copy(6 + 2 * j + a, ref, me).wait_recv()
        for cp in first + passed:
            cp.wait_send()

    vm = pl.BlockSpec(memory_space=pltpu.VMEM)
    return pl.pallas_call(
        body, name="gather_weights",
        in_specs=[vm, vm], out_specs=[vm, vm],
        out_shape=[jax.ShapeDtypeStruct((4, D_MODEL, 1024), BF16), jax.ShapeDtypeStruct((4, 256, D_MODEL), BF16)],
        scratch_shapes=[pltpu.SemaphoreType.DMA((12,)), pltpu.SemaphoreType.DMA((12,))],
        compiler_params=pltpu.CompilerParams(vmem_limit_bytes=VMEM_LIMIT),
    )(w_in_s, w_out_s)


def _swap_halves(g_in, g_out):
    def body(gin_ref, gout_ref, rin_ref, rout_ref, send_sems, recv_sems):
        x, y, c = _place()
        sib = (x, y, 1 - c)
        cps = [pltpu.make_async_remote_copy(src_ref=src.at[:, 1 - c], dst_ref=dst, send_sem=send_sems.at[k],
                                            recv_sem=recv_sems.at[k], device_id=sib, device_id_type=MESH)
               for k, (src, dst) in enumerate(((gin_ref, rin_ref), (gout_ref, rout_ref)))]
        for cp in cps:
            cp.start()
        for cp in cps:
            cp.wait()

    hbm = pl.BlockSpec(memory_space=pl.ANY)
    return pl.pallas_call(
        body, name="swap_halves",
        in_specs=[hbm, hbm], out_specs=[hbm, hbm],
        out_shape=[jax.ShapeDtypeStruct((4,) + g.shape[2:], F32) for g in (g_in, g_out)],
        scratch_shapes=[pltpu.SemaphoreType.DMA((2,)), pltpu.SemaphoreType.DMA((2,))],
    )(g_in, g_out)


def _add_half(g, r, cidx, name):
    n, _, rows, cols = g.shape

    def body(c_ref, g_ref, r_ref, o_ref):
        o_ref[0] = (g_ref[0, 0] + r_ref[0]).astype(BF16)

    return pl.pallas_call(
        body, name=name,
        grid_spec=pltpu.PrefetchScalarGridSpec(
            num_scalar_prefetch=1, grid=(n,),
            in_specs=[pl.BlockSpec((1, 1, rows, cols), lambda j, c_ref: (j, c_ref[0], 0, 0)),
                      pl.BlockSpec((1, rows, cols), lambda j, c_ref: (j, 0, 0))],
            out_specs=pl.BlockSpec((1, rows, cols), lambda j, c_ref: (j, 0, 0))),
        out_shape=jax.ShapeDtypeStruct((n, rows, cols), BF16),
        compiler_params=_params(("parallel",)),
    )(cidx, g, r)


def _exchange_chips(cs_in, cs_out):
    def body(in_ref, out_ref, rin_ref, rout_ref, send_sems, recv_sems):
        x, y, c = _place()
        chips = [(1 - x, y), (x, 1 - y), (1 - x, 1 - y)]
        cps = []
        for k, (cx, cy) in enumerate(chips):
            for a, (src, dst) in enumerate(((in_ref, rin_ref), (out_ref, rout_ref))):
                cps.append(pltpu.make_async_remote_copy(
                    src_ref=src.at[2 * cx + cy], dst_ref=dst.at[k], send_sem=send_sems.at[2 * k + a],
                    recv_sem=recv_sems.at[2 * k + a], device_id=(cx, cy, c), device_id_type=MESH))
        for cp in cps:
            cp.start()
        for cp in cps:
            cp.wait()

    hbm = pl.BlockSpec(memory_space=pl.ANY)
    return pl.pallas_call(
        body, name="exchange_chips",
        in_specs=[hbm, hbm], out_specs=[hbm, hbm],
        out_shape=[jax.ShapeDtypeStruct((3,) + a.shape[1:], a.dtype) for a in (cs_in, cs_out)],
        scratch_shapes=[pltpu.SemaphoreType.DMA((6,)), pltpu.SemaphoreType.DMA((6,))],
    )(cs_in, cs_out)


PEER_XOR = (2, 1, 3)


def _sum_chips(cs, r, chip_idx, name):
    _, rows, cols = r.shape
    tr = min(rows, 256)

    def body(m_ref, cs_ref, r_ref, o_ref):
        mine = m_ref[0]
        own = cs_ref[0].astype(F32)
        got = [r_ref[k].astype(F32) for k in range(3)]
        acc = None
        for s in range(4):
            rel = mine ^ s
            term = jnp.where(rel == 0, own, jnp.where(rel == PEER_XOR[0], got[0],
                                                      jnp.where(rel == PEER_XOR[1], got[1], got[2])))
            acc = term if acc is None else acc + term
        o_ref[...] = acc

    return pl.pallas_call(
        body, name=name,
        grid_spec=pltpu.PrefetchScalarGridSpec(
            num_scalar_prefetch=1, grid=(rows // tr,),
            in_specs=[pl.BlockSpec((1, tr, cols), lambda i, m_ref: (m_ref[0], i, 0)),
                      pl.BlockSpec((3, tr, cols), lambda i, m_ref: (0, i, 0))],
            out_specs=pl.BlockSpec((tr, cols), lambda i, m_ref: (i, 0))),
        out_shape=jax.ShapeDtypeStruct((rows, cols), F32),
        compiler_params=_params(("parallel",)),
    )(chip_idx, cs, r)


def _swap_reduced(h_in, h_out):
    def body(in_ref, out_ref, rin_ref, rout_ref, send_sems, recv_sems):
        x, y, c = _place()
        cps = [pltpu.make_async_remote_copy(src_ref=src, dst_ref=dst, send_sem=send_sems.at[k],
                                            recv_sem=recv_sems.at[k], device_id=(x, y, 1 - c), device_id_type=MESH)
               for k, (src, dst) in enumerate(((in_ref, rin_ref), (out_ref, rout_ref)))]
        for cp in cps:
            cp.start()
        for cp in cps:
            cp.wait()

    hbm = pl.BlockSpec(memory_space=pl.ANY)
    return pl.pallas_call(
        body, name="swap_reduced",
        in_specs=[hbm, hbm], out_specs=[hbm, hbm],
        out_shape=[jax.ShapeDtypeStruct(h.shape, F32) for h in (h_in, h_out)],
        scratch_shapes=[pltpu.SemaphoreType.DMA((2,)), pltpu.SemaphoreType.DMA((2,))],
    )(h_in, h_out)


def _adamw_halves(w, mine, theirs, m, v, cidx, name):
    rows, cols = w.shape
    half = rows // 2
    tr = min(half, 256)
    nbh = half // tr

    def body(c_ref, w_ref, a_ref, b_ref, m_ref, v_ref, g_ref, d_ref, nm_ref, nv_ref):
        gg = jnp.where(pl.program_id(0) // nbh == c_ref[0], a_ref[...], b_ref[...])
        g_ref[...] = gg
        _adamw_update(gg, w_ref, m_ref, v_ref, d_ref, nm_ref, nv_ref)

    spec = pl.BlockSpec((tr, cols), lambda i, c_ref: (i, 0))
    hspec = pl.BlockSpec((tr, cols), lambda i, c_ref: (i % nbh, 0))
    sds = jax.ShapeDtypeStruct((rows, cols), F32)
    return pl.pallas_call(
        body, name=name,
        grid_spec=pltpu.PrefetchScalarGridSpec(
            num_scalar_prefetch=1, grid=(rows // tr,),
            in_specs=[spec, hspec, hspec, spec, spec], out_specs=[spec] * 4),
        out_shape=[sds] * 4,
        compiler_params=_params(("parallel",)),
    )(cidx, w, mine, theirs, m, v)


def _allreduce_small(g_nw, g_fw, g_hgw, g_lbl, loss8):
    def body(nw_ref, fw_ref, hgw_ref, lbl_ref, loss_ref, out_ref, slots, send_sems, recv_sems):
        x, y, c = _place()
        me = 4 * x + 2 * y + c
        slots[me] = jnp.zeros((8, D_MODEL), F32)
        slots[me, 0:1, :] = nw_ref[...]
        slots[me, 1:2, :] = fw_ref[...]
        slots[me, 2:3, 0:HG_W] = hgw_ref[...]
        slots[me, 3:4, 0:HG_W] = lbl_ref[0:1, :]
        slots[me, 3:4, HG_W:] = lbl_ref[1:2, :]
        slots[me, 4:5, 0:128] = loss_ref[0:1, :]
        cps = []
        for k in range(1, 8):
            dx, dy, dc = (k >> 2) & 1, (k >> 1) & 1, k & 1
            to = (x ^ dx, y ^ dy, c ^ dc)
            cps.append(pltpu.make_async_remote_copy(
                src_ref=slots.at[me], dst_ref=slots.at[me], send_sem=send_sems.at[k - 1],
                recv_sem=recv_sems.at[k - 1], device_id=to, device_id_type=MESH))
        for cp in cps:
            cp.start()
        for cp in cps:
            cp.wait()
        acc = slots[0]
        for i in range(1, 8):
            acc = acc + slots[i]
        out_ref[...] = acc

    vm = pl.BlockSpec(memory_space=pltpu.VMEM)
    return pl.pallas_call(
        body, name="allreduce_small",
        in_specs=[vm] * 5, out_specs=vm,
        out_shape=jax.ShapeDtypeStruct((8, D_MODEL), F32),
        scratch_shapes=[pltpu.VMEM((8, 8, D_MODEL), F32), pltpu.SemaphoreType.DMA((7,)), pltpu.SemaphoreType.DMA((7,))],
    )(g_nw, g_fw, g_hgw, g_lbl, loss8)


def _rope_tables(S):
    inv_freq = 1.0 / (ROPE_THETA ** (jnp.arange(ROPE_HALF, dtype=F32) / ROPE_HALF))
    ang = jnp.arange(S, dtype=jnp.int32).astype(F32)[:, None] * inv_freq[None, :]
    cos, sin = jnp.cos(ang), jnp.sin(ang)
    cos2 = jnp.concatenate([cos, cos, cos, cos], axis=-1)
    sin2 = jnp.concatenate([-sin, sin, -sin, sin], axis=-1)
    return cos2, sin2


def _local_step(x2, tgt2, norm_w, w_in_full, lbl, hg_norm_w, w_out4, fnw):
    S = x2.shape[0]
    cos2, sin2 = _rope_tables(S)
    w_out_full = w_out4.reshape(D_MODEL, D_MODEL)
    p0, p1, qkv, qkv4, qkv16, p3, ut = _inproj(x2, norm_w, w_in_full, cos2, sin2)
    o_hg, mix_hg, mixt_hg, states = _hg_fwd(p0, p1, lbl, hg_norm_w)
    qkv_ds = [qkv.reshape(1, S, 3 * AT_W), qkv4, qkv16]
    os_d, ls_d = zip(*[_att_fwd(q) for q in qkv_ds])
    o_at, lse, mix_at, mixt_at = _att_combine(os_d, ls_d, p3)
    dh, dm_hg, dm_at, g_wout, g_fw, loss8 = _outproj(x2, tgt2, mix_hg, mix_at, mixt_hg, mixt_at, w_out_full, fnw)
    dqr, dfl, dv_hg, dz_hg, g_lbl, g_hgw = _hg_bwd(p0, p1, o_hg, dm_hg, states, lbl, hg_norm_w)
    do1, do4, do16, dl1, dl4, dl16, dz_at = _att_gate_bwd(dm_at, o_at, lse, p3)
    do_ds = [do1.reshape(1, S, AT_W), do4, do16]
    dl_ds = [dl1.reshape(1, S, STAT_W), dl4, dl16]
    dqkvs = [_att_bwd(q, a, b) for q, a, b in zip(qkv_ds, do_ds, dl_ds)]
    dq_at, dk_at, dv_at = _att_bwd_combine(dqkvs, cos2, sin2)
    dps = [dqr, dfl, dv_hg, dz_hg, dq_at, dk_at, dv_at, dz_at]
    grad_x, g_nw = _inproj_bwd_x(dps, w_in_full, x2, norm_w, dh)
    g_win = _inproj_bwd_w(ut, dps)
    return loss8, grad_x, g_nw, g_win, g_lbl, g_hgw, g_wout, g_fw


def kernel(x, norm_w, w_in, hgrn_lb_logits, hg_norm_w, w_out, final_norm_w, loss_target, m_norm_w, m_w_in, m_hgrn_lb_logits, m_hg_norm_w, m_w_out, m_final_norm_w, v_norm_w, v_w_in, v_hgrn_lb_logits, v_hg_norm_w, v_w_out, v_final_norm_w):
    S = x.shape[1]
    w_in_full, w_out_full = _gather_weights(w_in[0], w_out[0])
    loss8, grad_x, g_nw, g_win, g_lbl, g_hgw, g_wout, g_fw = _local_step(
        x[0], loss_target[0], norm_w, w_in_full, hgrn_lb_logits, hg_norm_w,
        w_out_full, final_norm_w.reshape(1, D_MODEL))

    cidx = lax.axis_index("c").astype(jnp.int32).reshape(1)
    g_in4 = g_win.reshape(4, 2, 512, 1024)
    g_out4 = g_wout.reshape(4, 2, 128, D_MODEL)
    r_in, r_out = _swap_halves(g_in4, g_out4)
    cs_in = _add_half(g_in4, r_in, cidx, "add_half_in")
    cs_out = _add_half(g_out4, r_out, cidx, "add_half_out")
    x_in, x_out = _exchange_chips(cs_in, cs_out)
    chip_idx = (2 * lax.axis_index("x") + lax.axis_index("y")).astype(jnp.int32).reshape(1)
    h_in = _sum_chips(cs_in, x_in, chip_idx, "sum_chips_in")
    h_out = _sum_chips(cs_out, x_out, chip_idx, "sum_chips_out")
    s_in, s_out = _swap_reduced(h_in, h_out)

    red = _allreduce_small(g_nw, g_fw, g_hgw, g_lbl, loss8)
    loss = red[4, 0]
    grad_norm_w = red[0:1, :]
    grad_final_norm_w = red[1, :]
    grad_hg_norm_w = red[2:3, :HG_W]
    grad_lbl = jnp.concatenate([red[3:4, :HG_W], red[3:4, HG_W:]], axis=0)

    d_nw, m_nw, v_nw = _adamw(norm_w, grad_norm_w, m_norm_w, v_norm_w, "adamw_norm_w")
    grad_w_in, d_win, m_win, v_win = _adamw_halves(w_in[0], h_in, s_in, m_w_in[0], v_w_in[0], cidx, "adamw_w_in")
    d_lbl, m_lbl, v_lbl = _adamw(hgrn_lb_logits, grad_lbl, m_hgrn_lb_logits, v_hgrn_lb_logits, "adamw_lb_logits")
    d_hgw, m_hgw, v_hgw = _adamw(hg_norm_w, grad_hg_norm_w, m_hg_norm_w, v_hg_norm_w, "adamw_hg_norm_w")
    grad_w_out, d_wout, m_wout, v_wout = _adamw_halves(w_out[0], h_out, s_out, m_w_out[0], v_w_out[0], cidx, "adamw_w_out")
    d_fw, m_fw, v_fw = _adamw(final_norm_w.reshape(1, D_MODEL), grad_final_norm_w.reshape(1, D_MODEL),
                              m_final_norm_w.reshape(1, D_MODEL), v_final_norm_w.reshape(1, D_MODEL), "adamw_final_norm_w")
    e1 = lambda a: a[None]
    flat = lambda a: a.reshape(D_MODEL)
    return (loss, grad_x[None], grad_norm_w, e1(grad_w_in), grad_lbl, grad_hg_norm_w, e1(grad_w_out), grad_final_norm_w,
            d_nw, e1(d_win), d_lbl, d_hgw, e1(d_wout), flat(d_fw),
            m_nw, e1(m_win), m_lbl, m_hgw, e1(m_wout), flat(m_fw),
            v_nw, e1(v_win), v_lbl, v_hgw, e1(v_wout), flat(v_fw))
```

```python
import functools

import jax
import jax.numpy as jnp
import numpy as np
from jax import lax
from jax.experimental import pallas as pl
from jax.experimental.pallas import tpu as pltpu

F32 = jnp.float32
BF16 = jnp.bfloat16
MESH = pl.DeviceIdType.MESH

D_MODEL = 1024
HG_W = 512
AT_W = 512
HEAD_PAIR = 128
ROPE_HALF = 32
ROPE_THETA = 10000.0
EPS = 1e-6
CHUNK = 128
LEVELS = (64, 32, 16, 8, 4, 2, 1)
DIAG = 1
SUBLANES = 8
ATT_BLK = 128
ATT_QB = 4
DILATIONS = (1, 4, 16)
ATT_SCALE = 0.125
STAT_W = 128
STAT_LANES = 16
STAT_LSE_LANE = 8
NEG = -1e30
VMEM_LIMIT = 56 * 1024 * 1024

ADAM_LR = 0.001
ADAM_B1 = 0.9
ADAM_B2 = 0.999
ADAM_EPS = 1e-08
ADAM_WD = 0.01
ADAM_STEP = 10


def _iota(shape, dim):
    return lax.broadcasted_iota(jnp.int32, shape, dim)


def _dot(a, b):
    return jnp.dot(a, b, preferred_element_type=F32)


def _dot_nt(a, b):
    return lax.dot_general(a, b, (((1,), (1,)), ((), ())), preferred_element_type=F32)


def _dot_hi(a, b):
    return jnp.dot(a, b, preferred_element_type=F32, precision=lax.Precision.HIGHEST)


def _sig(v):
    return 1.0 / (1.0 + jnp.exp(-v))


def _params(sem=None, vmem=VMEM_LIMIT):
    return pltpu.CompilerParams(dimension_semantics=sem, vmem_limit_bytes=vmem)


def _full(shape):
    n = len(shape)
    return pl.BlockSpec(shape, lambda *_: (0,) * n)


def _rope_rot(y):
    n = y.shape[1]
    first = (_iota(y.shape, 1) & (2 * ROPE_HALF - 1)) < ROPE_HALF
    return jnp.where(first, pltpu.roll(y, n - ROPE_HALF, 1), pltpu.roll(y, ROPE_HALF, 1))


def _dil_spec(d, tm, width):
    return pl.BlockSpec((d, tm // d, width), lambda s: (0, s, 0))


LANES = 128


def _slab_scratch(tm, width):
    return pltpu.VMEM((width // LANES, tm, LANES), F32)


def _to_slabs(v, slabs_ref):
    for j in range(slabs_ref.shape[0]):
        slabs_ref[j] = v[:, LANES * j: LANES * (j + 1)]


def _from_slabs(slabs_ref):
    return jnp.concatenate([slabs_ref[j] for j in range(slabs_ref.shape[0])], axis=1)


def _split_residues(slabs_ref, dst_ref, d, dtype):
    nslab, tm, _ = slabs_ref.shape
    for r in range(d):
        for j in range(nslab):
            dst_ref[r, :, LANES * j: LANES * (j + 1)] = slabs_ref[j, pl.ds(r, tm // d, stride=d), :].astype(dtype)


def _merge_residues(src_ref, slabs_ref, d):
    nslab, tm, _ = slabs_ref.shape
    for r in range(d):
        for j in range(nslab):
            slabs_ref[j, pl.ds(r, tm // d, stride=d), :] = src_ref[r, :, LANES * j: LANES * (j + 1)].astype(F32)


def _inproj(x2, norm_w, w_in_full, cos2, sin2):
    S = x2.shape[0]
    tm = 256

    def body(x_ref, nw_ref, w_ref, cos_ref, sin_ref, p0_ref, p1_ref, qkv_ref, qkv4_ref, qkv16_ref, p3_ref, ut_ref, scr):
        x = x_ref[...]
        r = lax.rsqrt(jnp.mean(x * x, axis=-1, keepdims=True) + EPS)
        u = x * r * nw_ref[...]
        ub = u.astype(BF16)
        ut_ref[...] = u.T.astype(BF16)
        p0_ref[...] = _dot(ub, w_ref[0])
        p1_ref[...] = _dot(ub, w_ref[1])
        y2 = _dot(ub, w_ref[2])
        cosf = jnp.tile(cos_ref[...], (1, 8))
        sinf = jnp.tile(sin_ref[...], (1, 8))
        y3 = _dot(ub, w_ref[3])
        p3_ref[...] = y3
        qkv = jnp.concatenate([y2 * cosf + _rope_rot(y2) * sinf, y3[:, :AT_W]], axis=1)
        qkv_ref[...] = qkv.astype(BF16)
        _to_slabs(qkv, scr)
        _split_residues(scr, qkv4_ref, 4, BF16)
        _split_residues(scr, qkv16_ref, 16, BF16)

    row = lambda w: pl.BlockSpec((tm, w), lambda s: (s, 0))
    qkv_w = 3 * AT_W
    return pl.pallas_call(
        body, name="inproj", grid=(S // tm,),
        in_specs=[row(D_MODEL), _full((1, D_MODEL)), _full((4, D_MODEL, 1024)), row(128), row(128)],
        out_specs=[row(1024), row(1024), row(qkv_w), _dil_spec(4, tm, qkv_w), _dil_spec(16, tm, qkv_w), row(1024),
                   pl.BlockSpec((D_MODEL, tm), lambda s: (0, s))],
        out_shape=[jax.ShapeDtypeStruct((S, 1024), F32), jax.ShapeDtypeStruct((S, 1024), F32),
                   jax.ShapeDtypeStruct((S, qkv_w), BF16), jax.ShapeDtypeStruct((4, S // 4, qkv_w), BF16),
                   jax.ShapeDtypeStruct((16, S // 16, qkv_w), BF16), jax.ShapeDtypeStruct((S, 1024), F32),
                   jax.ShapeDtypeStruct((D_MODEL, S), BF16)],
        scratch_shapes=[_slab_scratch(tm, qkv_w)],
        compiler_params=_params(("parallel",)),
    )(x2, norm_w, w_in_full, cos2, sin2)


HG_HPS = 4
N_LEV = len(LEVELS)


def _hg_const_arrays():
    r = np.arange(CHUNK)[:, None]
    c = np.arange(CHUNK)[None, :]
    tris = np.stack([r >= c, r <= c])
    lm = [((r // (2 * m)) == (c // (2 * m))) & (r % (2 * m) >= m) & (c % (2 * m) < m) for m in LEVELS]
    dm = [(c == r - dl) & (r % DIAG >= dl) for dl in range(DIAG)]
    masks = np.stack(lm + [x.T for x in lm] + dm)
    return jnp.asarray(tris, BF16), jnp.asarray(masks, F32)


def _split2(a):
    hi = a.astype(BF16)
    return hi, (a - hi.astype(F32)).astype(BF16)


def _dot3(a, b):
    ah, al = _split2(a)
    bh, bl = _split2(b)
    n = b.shape[1]
    p = _dot(ah, jnp.concatenate([bh, bl], axis=1))
    return (p[:, :n] + p[:, n:]) + _dot(al, bh)


def _split3(a):
    a1 = a.astype(BF16)
    r1 = a - a1.astype(F32)
    a2 = r1.astype(BF16)
    return a1, a2, (r1 - a2.astype(F32)).astype(BF16)


def _tri_dot(tri, a):
    n = a.shape[1]
    p = _dot(tri, jnp.concatenate(_split3(a), axis=1))
    return (p[:, :n] + p[:, n:2 * n]) + p[:, 2 * n:]


def _dot_sel(a, sel):
    a1, a2, a3 = _split3(a)
    return (_dot(a1, sel) + _dot(a2, sel)) + _dot(a3, sel)


def _rowsum(t):
    return _dot(t.astype(BF16), jnp.ones((t.shape[1], t.shape[1]), BF16))


def _level_refs(b):
    refs = []
    pos = _iota(b.shape, 0)
    for m in LEVELS:
        if 2 * m >= SUBLANES:
            parts = [jnp.broadcast_to(b[r0 + m - 1: r0 + m, :], (2 * m, b.shape[1])) for r0 in range(0, CHUNK, 2 * m)]
            refs.append(parts[0] if len(parts) == 1 else jnp.concatenate(parts, axis=0))
        else:
            p = pos & (2 * m - 1)
            ref = b
            for off in range(-(m - 1), m + 1):
                if off != 0:
                    ref = jnp.where(p == m - 1 + off, pltpu.roll(b, off % CHUNK, 0), ref)
            refs.append(ref)
    return refs


def _hg_lb(lbl_ref):
    l0 = lbl_ref[0:1, :]
    l1 = lbl_ref[1:2, :]
    mx = jnp.maximum(l0, l1)
    e0 = jnp.exp(l0 - mx)
    e1 = jnp.exp(l1 - mx)
    p0 = e0 / (e0 + e1)
    lb = jnp.clip(p0, 1e-6, 1.0 - 1e-6)
    inside = (p0 >= 1e-6) & (p0 <= 1.0 - 1e-6)
    dlb_dl0 = jnp.where(inside, p0 * (e1 / (e0 + e1)), 0.0)
    return lb, dlb_dl0


def _sigmoid(v):
    return 0.5 * jnp.tanh(0.5 * v) + 0.5


def _hg_gates(qr, fl, lb):
    sig = _sigmoid(fl)
    f = lb + (1.0 - lb) * sig
    g = jnp.log(f)
    k = (1.0 - lb) * (1.0 - sig)
    sq = _sigmoid(qr)
    q = qr * sq
    return sig, f, g, k, sq, q


def _hg_levels(q, k, b, mk_ref):
    refs = _level_refs(b)
    a = jnp.zeros((CHUNK, CHUNK), F32)
    es, qts, kts = [], [], []
    for i in range(N_LEV):
        diff = b - refs[i]
        e = jnp.exp(jnp.minimum(diff, -diff))
        qt = (q * e).astype(BF16)
        kt = (k * e).astype(BF16)
        a = a + _dot_nt(qt, kt) * mk_ref[i]
        es.append(e); qts.append(qt); kts.append(kt)
    return a, es, qts, kts


def _hg_specs(nc, rev):
    cc = (lambda c: nc - 1 - c) if rev else (lambda c: c)
    w = 128 * HG_HPS
    blk = lambda off: pl.BlockSpec((CHUNK, w), lambda h, c: (cc(c), h + off))
    vec = pl.BlockSpec((1, w), lambda h, c: (0, h))
    lb2 = pl.BlockSpec((2, w), lambda h, c: (0, h))
    st = pl.BlockSpec((1, HG_HPS, 128, 128), lambda h, c: (cc(c), h, 0, 0))
    consts = [_full((2, CHUNK, CHUNK)), _full((2 * N_LEV + DIAG, CHUNK, CHUNK))]
    return blk, vec, lb2, st, consts


def _hg_fwd(p0, p1, lbl, gw):
    S = p0.shape[0]
    nc = S // CHUNK
    ng = 4 // HG_HPS

    def body(qr_ref, fl_ref, v_ref, z_ref, lbl_ref, gw_ref, tri_ref, mk_ref,
             o_ref, mix_ref, mixt_ref, st_ref, state):
        c = pl.program_id(1)

        @pl.when(c == 0)
        def _():
            state[...] = jnp.zeros_like(state)

        lb_all, _ = _hg_lb(lbl_ref)
        heads = [slice(128 * hh, 128 * (hh + 1)) for hh in range(HG_HPS)]
        qs, ks, bs, mats = [], [], [], []
        for sl in heads:
            _, _, g, k, _, q = _hg_gates(qr_ref[:, sl], fl_ref[:, sl], lb_all[:, sl])
            qs.append(q); ks.append(k); bs.append(_tri_dot(tri_ref[0], g))
        for hh in range(HG_HPS):
            a, _, _, _ = _hg_levels(qs[hh], ks[hh], bs[hh], mk_ref)
            mats.append(a + _rowsum(qs[hh] * ks[hh]) * mk_ref[2 * N_LEV])
        for hh, sl in enumerate(heads):
            q, k, b, a, v, z = qs[hh], ks[hh], bs[hh], mats[hh], v_ref[:, sl], z_ref[:, sl]
            b_last = b[CHUNK - 1: CHUNK, :]
            st = state[hh]
            st_ref[0, hh] = st
            o = _dot_nt((q * jnp.exp(b)).astype(BF16), st.astype(BF16)) + _dot(a.astype(BF16), v.astype(BF16))
            state[hh] = st * jnp.exp(b_last) + _dot3(v.T, k * jnp.exp(b_last - b))
            o_ref[:, sl] = o
            rs = lax.rsqrt(jnp.mean(o * o, axis=-1, keepdims=True) + EPS)
            mixed = o * rs * gw_ref[:, sl] * (z * _sigmoid(z))
            mix_ref[:, sl] = mixed.astype(BF16)
            mixt_ref[sl, :] = mixed.T.astype(BF16)

    blk, vec, lb2, st_spec, consts = _hg_specs(nc, False)
    tris, masks = _hg_const_arrays()
    return pl.pallas_call(
        body, name="hg_fwd", grid=(ng, nc),
        in_specs=[blk(0), blk(ng), blk(0), blk(ng), lb2, vec] + consts,
        out_specs=[blk(0), blk(0), pl.BlockSpec((128 * HG_HPS, CHUNK), lambda h, c: (h, c)), st_spec],
        out_shape=[jax.ShapeDtypeStruct((S, HG_W), F32), jax.ShapeDtypeStruct((S, HG_W), BF16),
                   jax.ShapeDtypeStruct((HG_W, S), BF16), jax.ShapeDtypeStruct((nc, 4, 128, 128), F32)],
        scratch_shapes=[pltpu.VMEM((HG_HPS, 128, 128), F32)],
        compiler_params=_params(("parallel", "arbitrary")),
    )(p0, p0, p1, p1, lbl, gw, tris, masks)


def _hg_bwd(p0, p1, o_raw, dm, states, lbl, gw):
    S = p0.shape[0]
    nc = S // CHUNK
    ng = 4 // HG_HPS
    w = 128 * HG_HPS

    def body(qr_ref, fl_ref, v_ref, z_ref, o_ref, dm_ref, st_ref, lbl_ref, gw_ref, tri_ref, mk_ref, mkb_ref,
             dqr_ref, dfl_ref, dv_ref, dz_ref, glbl_ref, ggw_ref, dstate, carry, acc_lb, acc_gw):
        c = pl.program_id(1)

        @pl.when(c == 0)
        def _():
            dstate[...] = jnp.zeros_like(dstate)
            carry[...] = jnp.zeros_like(carry)
            acc_lb[...] = jnp.zeros_like(acc_lb)
            acc_gw[...] = jnp.zeros_like(acc_gw)

        lb_all, dlb_dl0 = _hg_lb(lbl_ref)
        heads = [slice(128 * hh, 128 * (hh + 1)) for hh in range(HG_HPS)]
        diag_mask = mk_ref[2 * N_LEV]
        gates, bs, dos = [], [], []
        for sl in heads:
            gt = _hg_gates(qr_ref[:, sl], fl_ref[:, sl], lb_all[:, sl])
            gates.append(gt)
            bs.append(_tri_dot(tri_ref[0], gt[2]))
        for hh, sl in enumerate(heads):
            o, z, dmix, gwv = o_ref[:, sl], z_ref[:, sl], dm_ref[:, sl], gw_ref[:, sl]
            rs = lax.rsqrt(jnp.mean(o * o, axis=-1, keepdims=True) + EPS)
            oh = o * rs
            sz = _sigmoid(z)
            dz_ref[:, sl] = (dmix * (oh * gwv) * (sz * (1.0 + z * (1.0 - sz)))).astype(BF16)
            don = dmix * (z * sz)
            acc_gw[0:1, sl] += jnp.sum(don * oh, axis=0, keepdims=True)
            dy = don * gwv
            dos.append(rs * (dy - oh * jnp.mean(dy * oh, axis=-1, keepdims=True)))
        inter = []
        for hh, sl in enumerate(heads):
            _, _, _, k, _, q = gates[hh]
            b, do, v = bs[hh], dos[hh], v_ref[:, sl]
            b_last = b[CHUNK - 1: CHUNK, :]
            eb = jnp.exp(b)
            edec = jnp.exp(b_last - b)
            dst = dstate[hh]
            dq = _dot3(do, st_ref[0, hh]) * eb
            dk = _dot3(v, dst) * edec
            da = _dot_nt(do.astype(BF16), v.astype(BF16))
            dv_state = _dot_nt((k * edec).astype(BF16), dst.astype(BF16))
            dstate[hh] = dst * jnp.exp(b_last) + _dot3(do.T, q * eb)
            inter.append((dq, dk, da, dv_state))
        for hh, sl in enumerate(heads):
            sig, f, _, k, sq, q = gates[hh]
            dq, dk, da, dv_state = inter[hh]
            b = bs[hh]
            db = q * dq - k * dk
            dab = da.astype(BF16)
            datb = da.T.astype(BF16)
            refs = _level_refs(b)
            a = _rowsum(q * k) * diag_mask
            for i in range(N_LEV):
                diff = b - refs[i]
                e = jnp.exp(jnp.minimum(diff, -diff))
                qt = (q * e).astype(BF16)
                kt = (k * e).astype(BF16)
                a = a + _dot_nt(qt, kt) * mk_ref[i]
                gq = _dot(dab * mkb_ref[i], kt)
                gk = _dot(datb * mkb_ref[N_LEV + i], qt)
                dq = dq + e * gq
                dk = dk + e * gk
                db = db + (qt.astype(F32) * gq - kt.astype(F32) * gk)
            dc = _rowsum(da * diag_mask)
            dq = dq + dc * k
            dk = dk + dc * q
            dv_ref[:, sl] = (_dot(a.T.astype(BF16), dos[hh].astype(BF16)) + dv_state).astype(BF16)
            dg = _tri_dot(tri_ref[1], db) + carry[0:1, sl]
            carry[0:1, sl] += jnp.sum(db, axis=0, keepdims=True)
            lb = lb_all[:, sl]
            qr = qr_ref[:, sl]
            t = dg / f - dk
            dfl_ref[:, sl] = (t * (1.0 - lb) * sig * (1.0 - sig)).astype(BF16)
            acc_lb[0:1, sl] += jnp.sum(t * (1.0 - sig), axis=0, keepdims=True)
            dqr_ref[:, sl] = (dq * (sq * (1.0 + qr * (1.0 - sq)))).astype(BF16)

        @pl.when(c == nc - 1)
        def _():
            gl0 = acc_lb[0:1, :] * dlb_dl0
            glbl_ref[0:1, :] = gl0
            glbl_ref[1:2, :] = -gl0
            ggw_ref[...] = acc_gw[0:1, :]

    blk, vec, lb2, st_spec, consts = _hg_specs(nc, True)
    tris, masks = _hg_const_arrays()
    act = jax.ShapeDtypeStruct((S, HG_W), BF16)
    return pl.pallas_call(
        body, name="hg_bwd", grid=(ng, nc),
        in_specs=[blk(0), blk(ng), blk(0), blk(ng), blk(0), blk(0), st_spec, lb2, vec] + consts + consts[1:],
        out_specs=[blk(0), blk(0), blk(0), blk(0), lb2, vec],
        out_shape=[act, act, act, act, jax.ShapeDtypeStruct((2, HG_W), F32), jax.ShapeDtypeStruct((1, HG_W), F32)],
        scratch_shapes=[pltpu.VMEM((HG_HPS, 128, 128), F32), pltpu.VMEM((8, w), F32),
                        pltpu.VMEM((8, w), F32), pltpu.VMEM((8, w), F32)],
        compiler_params=_params(("parallel", "arbitrary")),
    )(p0, p0, p1, p1, o_raw, dm, states, lbl, gw, tris, masks, masks.astype(BF16))


def _stat_head(lane):
    return lane >> 4


def _stat_lane(lane):
    return lane & (STAT_LANES - 1)


def _o_head(lane):
    return lane >> 6


def _att_bias():
    qi = np.arange(ATT_BLK)[:, None]
    kj = np.arange(2 * ATT_BLK)[None, :]
    band = (kj >= qi) & (kj <= qi + ATT_BLK)
    qm = np.stack([band & (kj >= ATT_BLK), band])
    cur = (kj < ATT_BLK) & (qi <= kj)
    km = np.stack([cur, cur | ((kj >= ATT_BLK) & (qi >= kj - ATT_BLK))])
    to_bias = lambda m: jnp.asarray(np.where(m, 0.0, NEG), F32)
    return to_bias(qm), to_bias(km)


def _att_fwd(qkv_d):
    d, L, _ = qkv_d.shape
    qb = ATT_QB
    rows = qb * ATT_BLK

    def body(q_ref, kp_ref, kc_ref, vp_ref, vc_ref, bias_ref, o_ref, lse_ref, s_scr, p_scr):
        first = _iota((ATT_BLK, HEAD_PAIR), 1) < 64
        head_of_lane = _stat_head(_iota((ATT_BLK, STAT_W), 1))
        pairs = [slice(HEAD_PAIR * hp, HEAD_PAIR * (hp + 1)) for hp in range(4)]
        blk = lambda i: slice(ATT_BLK * i, ATT_BLK * (i + 1))

        def keys(i, sl, prev_ref, cur_ref):
            before = prev_ref[0, :, sl] if i == 0 else cur_ref[0, blk(i - 1), sl]
            return jnp.concatenate([before, cur_ref[0, blk(i), sl]], axis=0)

        for i in range(qb):
            for hp, sl in enumerate(pairs):
                q2 = q_ref[0, blk(i), sl] * ATT_SCALE
                zero = jnp.zeros_like(q2)
                qs = jnp.concatenate([jnp.where(first, q2, zero), jnp.where(first, zero, q2)], axis=0)
                s_scr[4 * i + hp] = _dot_nt(qs, keys(i, sl, kp_ref, kc_ref))
        stats = []
        for i in range(qb):
            bias = bias_ref[jnp.minimum(pl.program_id(1), 1)] if i == 0 else bias_ref[1]
            bias2 = jnp.concatenate([bias, bias], axis=0)
            for hp in range(4):
                s = s_scr[4 * i + hp] + bias2
                m = jnp.max(s, axis=-1, keepdims=True)
                p = jnp.exp(s - m)
                l = jnp.sum(p, axis=-1, keepdims=True)
                p_scr[4 * i + hp] = p.astype(BF16)
                stats.append((l, m + jnp.log(l)))
        for i in range(qb):
            lse_blk = jnp.zeros((ATT_BLK, STAT_W), F32)
            for hp, sl in enumerate(pairs):
                l, lse = stats[4 * i + hp]
                o = _dot(p_scr[4 * i + hp], keys(i, sl, vp_ref, vc_ref)) / l
                o_ref[0, blk(i), sl] = jnp.where(first, o[:ATT_BLK], o[ATT_BLK:]).astype(BF16)
                lse_blk = jnp.where(head_of_lane == 2 * hp, lse[:ATT_BLK],
                                    jnp.where(head_of_lane == 2 * hp + 1, lse[ATT_BLK:], lse_blk))
            lse_ref[0, blk(i), :] = lse_blk

    cur = lambda j: pl.BlockSpec((1, rows, AT_W), lambda r, n: (r, n, j))
    prev = lambda j: pl.BlockSpec((1, ATT_BLK, AT_W), lambda r, n: (r, jnp.maximum(qb * n - 1, 0), j))
    return pl.pallas_call(
        body, name=f"att_fwd_d{d}", grid=(d, L // rows),
        in_specs=[cur(0), prev(1), cur(1), prev(2), cur(2), _full((2, ATT_BLK, 2 * ATT_BLK))],
        out_specs=[pl.BlockSpec((1, rows, AT_W), lambda r, n: (r, n, 0)), pl.BlockSpec((1, rows, STAT_W), lambda r, n: (r, n, 0))],
        out_shape=[jax.ShapeDtypeStruct((d, L, AT_W), BF16), jax.ShapeDtypeStruct((d, L, STAT_W), F32)],
        scratch_shapes=[pltpu.VMEM((4 * qb, 2 * ATT_BLK, 2 * ATT_BLK), F32), pltpu.VMEM((4 * qb, 2 * ATT_BLK, 2 * ATT_BLK), BF16)],
        compiler_params=_params(("parallel", "parallel")),
    )(qkv_d, qkv_d, qkv_d, qkv_d, qkv_d, _att_bias()[0])


def _att_combine(os_d, ls_d, p3):
    S = p3.shape[0]
    tm = 512

    def body(oa_ref, ob4_ref, oc16_ref, la_ref, lb4_ref, lc16_ref, z_ref,
             oat_ref, lse_ref, mix_ref, mixt_ref, ob_scr, oc_scr, lb_scr, lc_scr):
        _merge_residues(ob4_ref, ob_scr, 4)
        _merge_residues(oc16_ref, oc_scr, 16)
        _merge_residues(lb4_ref, lb_scr, 4)
        _merge_residues(lc16_ref, lc_scr, 16)
        ls = (la_ref[...], lb_scr[0], lc_scr[0])
        mx = jnp.maximum(jnp.maximum(ls[0], ls[1]), ls[2])
        es = [jnp.exp(l - mx) for l in ls]
        zs = es[0] + es[1] + es[2]
        lse_ref[...] = mx + jnp.log(zs)
        spread = ((_o_head(_iota((STAT_W, AT_W), 1)) == _stat_head(_iota((STAT_W, AT_W), 0)))
                  & (_stat_lane(_iota((STAT_W, AT_W), 0)) == 0)).astype(BF16)
        os_ = (oa_ref[...].astype(F32), _from_slabs(ob_scr), _from_slabs(oc_scr))
        o = jnp.zeros((tm, AT_W), F32)
        for e, oi in zip(es, os_):
            hi, lo = _split2(e / zs)
            o = o + (_dot(hi, spread) + _dot(lo, spread)) * oi
        oat_ref[...] = o
        z = z_ref[...]
        mixed = o * (z * _sig(z))
        mix_ref[...] = mixed.astype(BF16)
        mixt_ref[...] = mixed.T.astype(BF16)

    row = lambda w: pl.BlockSpec((tm, w), lambda s: (s, 0))
    return pl.pallas_call(
        body, name="att_combine", grid=(S // tm,),
        in_specs=[row(AT_W), _dil_spec(4, tm, AT_W), _dil_spec(16, tm, AT_W),
                  row(STAT_W), _dil_spec(4, tm, STAT_W), _dil_spec(16, tm, STAT_W),
                  pl.BlockSpec((tm, AT_W), lambda s: (s, 1))],
        out_specs=[row(AT_W), row(STAT_W), row(AT_W), pl.BlockSpec((AT_W, tm), lambda s: (0, s))],
        out_shape=[jax.ShapeDtypeStruct((S, AT_W), F32), jax.ShapeDtypeStruct((S, STAT_W), F32),
                   jax.ShapeDtypeStruct((S, AT_W), BF16), jax.ShapeDtypeStruct((AT_W, S), BF16)],
        scratch_shapes=[_slab_scratch(tm, AT_W), _slab_scratch(tm, AT_W), _slab_scratch(tm, STAT_W), _slab_scratch(tm, STAT_W)],
        compiler_params=_params(("parallel",)),
    )(os_d[0].reshape(S, AT_W), os_d[1], os_d[2], ls_d[0].reshape(S, STAT_W), ls_d[1], ls_d[2], p3)


def _att_gate_bwd(dm_at, o_at, lse, p3):
    S = p3.shape[0]
    tm = 512

    def body(dm_ref, o_ref, l_ref, z_ref, do_ref, do4_ref, do16_ref, dl_ref, dl4_ref, dl16_ref, dz_ref, do_scr, dl_scr):
        o = o_ref[...]
        z = z_ref[...]
        dm = dm_ref[...]
        sz = _sig(z)
        dz_ref[...] = (dm * o * (sz * (1.0 + z * (1.0 - sz)))).astype(BF16)
        do = dm * (z * sz)
        do_ref[...] = do.astype(BF16)
        gather = (_o_head(_iota((AT_W, STAT_W), 0)) == _stat_head(_iota((AT_W, STAT_W), 1))).astype(BF16)
        dl = jnp.where(_stat_lane(_iota((tm, STAT_W), 1)) < STAT_LSE_LANE, _dot_sel(do * o, gather), l_ref[...])
        dl_ref[...] = dl
        _to_slabs(do, do_scr)
        _to_slabs(dl, dl_scr)
        _split_residues(do_scr, do4_ref, 4, BF16)
        _split_residues(do_scr, do16_ref, 16, BF16)
        _split_residues(dl_scr, dl4_ref, 4, F32)
        _split_residues(dl_scr, dl16_ref, 16, F32)

    row = lambda w: pl.BlockSpec((tm, w), lambda s: (s, 0))
    sds = jax.ShapeDtypeStruct
    return pl.pallas_call(
        body, name="att_gate_bwd", grid=(S // tm,),
        in_specs=[row(AT_W), row(AT_W), row(STAT_W), pl.BlockSpec((tm, AT_W), lambda s: (s, 1))],
        out_specs=[row(AT_W), _dil_spec(4, tm, AT_W), _dil_spec(16, tm, AT_W),
                   row(STAT_W), _dil_spec(4, tm, STAT_W), _dil_spec(16, tm, STAT_W), row(AT_W)],
        out_shape=[sds((S, AT_W), BF16), sds((4, S // 4, AT_W), BF16), sds((16, S // 16, AT_W), BF16),
                   sds((S, STAT_W), F32), sds((4, S // 4, STAT_W), F32), sds((16, S // 16, STAT_W), F32),
                   sds((S, AT_W), BF16)],
        scratch_shapes=[_slab_scratch(tm, AT_W), _slab_scratch(tm, STAT_W)],
        compiler_params=_params(("parallel",)),
    )(dm_at, o_at, lse, p3)


def _att_bwd(qkv_d, do_d, dl_d):
    d, L, _ = qkv_d.shape
    nb = L // ATT_BLK
    qb = ATT_QB
    rows = qb * ATT_BLK
    nsteps = L // rows

    def body(qc_ref, qn_ref, kp_ref, kc_ref, vp_ref, vc_ref, ac_ref, an_ref, lc_ref, ln_ref, bq_ref, bk_ref, dqkv_ref,
             s_scr, dp_scr, st_scr, dpt_scr, ds_scr, pt_scr, dst_scr):
        n = pl.program_id(1)
        first = _iota((ATT_BLK, HEAD_PAIR), 1) < 64
        pairs = [slice(HEAD_PAIR * hp, HEAD_PAIR * (hp + 1)) for hp in range(4)]
        blk = lambda i: slice(ATT_BLK * i, ATT_BLK * (i + 1))

        def stack(t):
            zero = jnp.zeros_like(t)
            return jnp.concatenate([jnp.where(first, t, zero), jnp.where(first, zero, t)], axis=0)

        def unstack(t2):
            return jnp.where(first, t2[:ATT_BLK], t2[ATT_BLK:])

        def with_prev(i, sl, prev_ref, cur_ref):
            before = prev_ref[0, :, sl] if i == 0 else cur_ref[0, blk(i - 1), sl]
            return jnp.concatenate([before, cur_ref[0, blk(i), sl]], axis=0)

        def with_next(i, sl, cur_ref, next_ref):
            after = next_ref[0, :, sl] if i == qb - 1 else cur_ref[0, blk(i + 1), sl]
            return jnp.concatenate([cur_ref[0, blk(i), sl], after], axis=0)

        for i in range(qb):
            for hp, sl in enumerate(pairs):
                j = 4 * i + hp
                s_scr[j] = _dot_nt(stack(qc_ref[0, blk(i), sl] * ATT_SCALE), with_prev(i, sl, kp_ref, kc_ref))
                dp_scr[j] = _dot_nt(stack(ac_ref[0, blk(i), sl]), with_prev(i, sl, vp_ref, vc_ref))
                st_scr[j] = _dot_nt(stack(kc_ref[0, blk(i), sl] * ATT_SCALE), with_next(i, sl, qc_ref, qn_ref))
                dpt_scr[j] = _dot_nt(stack(vc_ref[0, blk(i), sl]), with_next(i, sl, ac_ref, an_ref))
        for i in range(qb):
            bias = bq_ref[jnp.minimum(n, 1)] if i == 0 else bq_ref[1]
            bias_t = bk_ref[jnp.minimum(nsteps - 1 - n, 1)] if i == qb - 1 else bk_ref[1]
            bias2 = jnp.concatenate([bias, bias], axis=0)
            bias_t2 = jnp.concatenate([bias_t, bias_t], axis=0)
            dl_c = lc_ref[0, blk(i), :]
            dl_t = with_next(i, slice(None), lc_ref, ln_ref).T
            for hp in range(4):
                j = 4 * i + hp
                at = [STAT_LANES * (2 * hp), STAT_LANES * (2 * hp + 1)]
                col = lambda t, o: jnp.concatenate([t[:, a + o: a + o + 1] for a in at], axis=0)
                p = jnp.exp(s_scr[j] + bias2 - col(dl_c, STAT_LSE_LANE))
                ds_scr[j] = (p * (dp_scr[j] - col(dl_c, 0))).astype(BF16)
                row = lambda t, o: jnp.concatenate([jnp.broadcast_to(t[a + o: a + o + 1, :], (ATT_BLK, 2 * ATT_BLK)) for a in at], axis=0)
                pt = jnp.exp(st_scr[j] + bias_t2 - row(dl_t, STAT_LSE_LANE))
                pt_scr[j] = pt.astype(BF16)
                dst_scr[j] = (pt * (dpt_scr[j] - row(dl_t, 0))).astype(BF16)
        for i in range(qb):
            for hp, sl in enumerate(pairs):
                j = 4 * i + hp
                dq = unstack(_dot(ds_scr[j], with_prev(i, sl, kp_ref, kc_ref))) * ATT_SCALE
                dk = unstack(_dot(dst_scr[j], with_next(i, sl, qc_ref, qn_ref))) * ATT_SCALE
                dv = unstack(_dot(pt_scr[j], with_next(i, sl, ac_ref, an_ref)))
                dqkv_ref[0, blk(i), sl] = dq.astype(BF16)
                dqkv_ref[0, blk(i), AT_W + HEAD_PAIR * hp: AT_W + HEAD_PAIR * (hp + 1)] = dk.astype(BF16)
                dqkv_ref[0, blk(i), 2 * AT_W + HEAD_PAIR * hp: 2 * AT_W + HEAD_PAIR * (hp + 1)] = dv.astype(BF16)

    cur = lambda j: pl.BlockSpec((1, rows, AT_W), lambda r, n: (r, n, j))
    prev = lambda j: pl.BlockSpec((1, ATT_BLK, AT_W), lambda r, n: (r, jnp.maximum(qb * n - 1, 0), j))
    nxt_blk = lambda n: jnp.minimum(qb * (n + 1), nb - 1)
    sq = (4 * qb, 2 * ATT_BLK, 2 * ATT_BLK)
    return pl.pallas_call(
        body, name=f"att_bwd_d{d}", grid=(d, nsteps),
        in_specs=[cur(0), pl.BlockSpec((1, ATT_BLK, AT_W), lambda r, n: (r, nxt_blk(n), 0)), prev(1), cur(1), prev(2), cur(2),
                  pl.BlockSpec((1, rows, AT_W), lambda r, n: (r, n, 0)),
                  pl.BlockSpec((1, ATT_BLK, AT_W), lambda r, n: (r, nxt_blk(n), 0)),
                  pl.BlockSpec((1, rows, STAT_W), lambda r, n: (r, n, 0)),
                  pl.BlockSpec((1, ATT_BLK, STAT_W), lambda r, n: (r, nxt_blk(n), 0)),
                  _full((2, ATT_BLK, 2 * ATT_BLK)), _full((2, ATT_BLK, 2 * ATT_BLK))],
        out_specs=pl.BlockSpec((1, rows, 3 * AT_W), lambda r, n: (r, n, 0)),
        out_shape=jax.ShapeDtypeStruct((d, L, 3 * AT_W), BF16),
        scratch_shapes=[pltpu.VMEM(sq, F32)] * 4 + [pltpu.VMEM(sq, BF16)] * 3,
        compiler_params=_params(("parallel", "parallel")),
    )(qkv_d, qkv_d, qkv_d, qkv_d, qkv_d, qkv_d, do_d, do_d, dl_d, dl_d, *_att_bias())


def _att_bwd_combine(dqkvs, cos2, sin2):
    S = dqkvs[0].shape[1]
    tm = 512

    def body(a_ref, b4_ref, c16_ref, cos_ref, sin_ref, dq_ref, dk_ref, dv_ref, b_scr, c_scr):
        _merge_residues(b4_ref, b_scr, 4)
        _merge_residues(c16_ref, c_scr, 16)
        t = a_ref[...].astype(F32) + _from_slabs(b_scr) + _from_slabs(c_scr)
        dy = t[:, : 2 * AT_W]
        cosf = jnp.tile(cos_ref[...], (1, 8))
        sinf = jnp.tile(sin_ref[...], (1, 8))
        dx = dy * cosf - _rope_rot(dy) * sinf
        dq_ref[...] = dx[:, :AT_W].astype(BF16)
        dk_ref[...] = dx[:, AT_W:].astype(BF16)
        dv_ref[...] = t[:, 2 * AT_W:].astype(BF16)

    row = lambda w: pl.BlockSpec((tm, w), lambda s: (s, 0))
    act = jax.ShapeDtypeStruct((S, AT_W), BF16)
    return pl.pallas_call(
        body, name="att_bwd_combine", grid=(S // tm,),
        in_specs=[row(3 * AT_W), _dil_spec(4, tm, 3 * AT_W), _dil_spec(16, tm, 3 * AT_W), row(128), row(128)],
        out_specs=[row(AT_W), row(AT_W), row(AT_W)],
        out_shape=[act, act, act],
        scratch_shapes=[_slab_scratch(tm, 3 * AT_W), _slab_scratch(tm, 3 * AT_W)],
        compiler_params=_params(("parallel",)),
    )(dqkvs[0].reshape(S, 3 * AT_W), dqkvs[1], dqkvs[2], cos2, sin2)


def _outproj(x2, tgt2, mix_hg, mix_at, mixt_hg, mixt_at, w_out_full, fnw):
    S = x2.shape[0]
    tm = 256
    ns = S // tm

    def body(x_ref, t_ref, mh_ref, ma_ref, mht_ref, mat_ref, w_ref, fw_ref,
             dh_ref, dmh_ref, dma_ref, gw_ref, gfw_ref, loss_ref):
        s = pl.program_id(0)

        @pl.when(s == 0)
        def _():
            gw_ref[...] = jnp.zeros_like(gw_ref)
            gfw_ref[...] = jnp.zeros_like(gfw_ref)
            loss_ref[...] = jnp.zeros_like(loss_ref)

        y = _dot(mh_ref[...], w_ref[:HG_W, :]) + _dot(ma_ref[...], w_ref[HG_W:, :])
        h = x_ref[...] + y
        r = lax.rsqrt(jnp.mean(h * h, axis=-1, keepdims=True) + EPS)
        hn = h * r
        fw = fw_ref[...]
        err = hn * fw - t_ref[...]
        loss_ref[...] += 0.5 * jnp.sum(jnp.mean(err * err, axis=-1, keepdims=True))
        dout = err * (1.0 / D_MODEL)
        gfw_ref[...] += jnp.sum(dout * hn, axis=0, keepdims=True)
        dhn = dout * fw
        dh = r * (dhn - hn * jnp.mean(dhn * hn, axis=-1, keepdims=True))
        dh_ref[...] = dh
        dhb = dh.astype(BF16)
        dmh_ref[...] = _dot_nt(dhb, w_ref[:HG_W, :])
        dma_ref[...] = _dot_nt(dhb, w_ref[HG_W:, :])
        gw_ref[:HG_W, :] += _dot(mht_ref[...], dhb)
        gw_ref[HG_W:, :] += _dot(mat_ref[...], dhb)

    row = lambda w: pl.BlockSpec((tm, w), lambda s: (s, 0))
    colb = pl.BlockSpec((HG_W, tm), lambda s: (0, s))
    return pl.pallas_call(
        body, name="outproj", grid=(ns,),
        in_specs=[row(D_MODEL), row(D_MODEL), row(HG_W), row(AT_W), colb, colb,
                  _full((D_MODEL, D_MODEL)), _full((1, D_MODEL))],
        out_specs=[row(D_MODEL), row(HG_W), row(AT_W), _full((D_MODEL, D_MODEL)), _full((1, D_MODEL)), _full((8, 128))],
        out_shape=[jax.ShapeDtypeStruct((S, D_MODEL), F32), jax.ShapeDtypeStruct((S, HG_W), F32),
                   jax.ShapeDtypeStruct((S, AT_W), F32), jax.ShapeDtypeStruct((D_MODEL, D_MODEL), F32),
                   jax.ShapeDtypeStruct((1, D_MODEL), F32), jax.ShapeDtypeStruct((8, 128), F32)],
        compiler_params=_params(("arbitrary",)),
    )(x2, tgt2, mix_hg, mix_at, mixt_hg, mixt_at, w_out_full, fnw)


def _inproj_bwd_x(dps, w_in_full, x2, norm_w, dh):
    S = x2.shape[0]
    tm = 256

    def body(d0, d1, d2, d3, d4, d5, d6, d7, w_ref, x_ref, nw_ref, dh_ref, gx_ref, gnw_ref):
        s = pl.program_id(0)

        @pl.when(s == 0)
        def _():
            gnw_ref[...] = jnp.zeros_like(gnw_ref)

        du = jnp.zeros((tm, D_MODEL), F32)
        for i, dref in enumerate((d0, d1, d2, d3, d4, d5, d6, d7)):
            j, half = divmod(i, 2)
            du = du + _dot_nt(dref[...], w_ref[j, :, 512 * half: 512 * (half + 1)])
        x = x_ref[...]
        r = lax.rsqrt(jnp.mean(x * x, axis=-1, keepdims=True) + EPS)
        xh = x * r
        gnw_ref[...] += jnp.sum(du * xh, axis=0, keepdims=True)
        dun = du * nw_ref[...]
        gx_ref[...] = dh_ref[...] + r * (dun - xh * jnp.mean(dun * xh, axis=-1, keepdims=True))

    row = lambda w: pl.BlockSpec((tm, w), lambda s: (s, 0))
    return pl.pallas_call(
        body, name="inproj_bwd_x", grid=(S // tm,),
        in_specs=[row(512)] * 8 + [_full((4, D_MODEL, 1024)), row(D_MODEL), _full((1, D_MODEL)), row(D_MODEL)],
        out_specs=[row(D_MODEL), _full((1, D_MODEL))],
        out_shape=[jax.ShapeDtypeStruct((S, D_MODEL), F32), jax.ShapeDtypeStruct((1, D_MODEL), F32)],
        compiler_params=_params(("arbitrary",)),
    )(*dps, w_in_full, x2, norm_w, dh)


def _inproj_bwd_w(ut, dps):
    S = ut.shape[1]
    tm = 256

    def body(ut_ref, d0, d1, d2, d3, d4, d5, d6, d7, g_ref):
        @pl.when(pl.program_id(0) == 0)
        def _():
            g_ref[...] = jnp.zeros_like(g_ref)

        utb = ut_ref[...]
        for i, dref in enumerate((d0, d1, d2, d3, d4, d5, d6, d7)):
            j, half = divmod(i, 2)
            g_ref[j, :, 512 * half: 512 * (half + 1)] += _dot(utb, dref[...])

    return pl.pallas_call(
        body, name="inproj_bwd_w", grid=(S // tm,),
        in_specs=[pl.BlockSpec((D_MODEL, tm), lambda s: (0, s))] + [pl.BlockSpec((tm, 512), lambda s: (s, 0))] * 8,
        out_specs=_full((4, D_MODEL, 1024)),
        out_shape=jax.ShapeDtypeStruct((4, D_MODEL, 1024), F32),
        compiler_params=_params(("arbitrary",)),
    )(ut, *dps)


def _adamw_update(gg, w_ref, m_ref, v_ref, d_ref, nm_ref, nv_ref):
    nm = ADAM_B1 * m_ref[...] + (1.0 - ADAM_B1) * gg
    nv = ADAM_B2 * v_ref[...] + (1.0 - ADAM_B2) * (gg * gg)
    m_hat = nm / (1.0 - ADAM_B1 ** ADAM_STEP)
    v_hat = nv / (1.0 - ADAM_B2 ** ADAM_STEP)
    d_ref[...] = -ADAM_LR * (m_hat / (jnp.sqrt(v_hat) + ADAM_EPS) + ADAM_WD * w_ref[...])
    nm_ref[...] = nm
    nv_ref[...] = nv


def _adamw(w, g, m, v, name):
    rows, cols = w.shape
    tr = min(rows, 256)

    def body(w_ref, g_ref, m_ref, v_ref, d_ref, nm_ref, nv_ref):
        _adamw_update(g_ref[...], w_ref, m_ref, v_ref, d_ref, nm_ref, nv_ref)

    spec = pl.BlockSpec((tr, cols), lambda i: (i, 0))
    sds = jax.ShapeDtypeStruct((rows, cols), F32)
    return pl.pallas_call(
        body, name=name, grid=(rows // tr,),
        in_specs=[spec] * 4, out_specs=[spec] * 3, out_shape=[sds] * 3,
        compiler_params=_params(("parallel",)),
    )(w, g, m, v)


def _place():
    return lax.axis_index("x"), lax.axis_index("y"), lax.axis_index("c")


def _gather_weights(w_in_s, w_out_s):
    def body(win_ref, wout_ref, fin_ref, fout_ref, send_sems, recv_sems):
        x, y, c = _place()
        me = (x, y, c)
        sib = (x, y, 1 - c)
        mine = 2 * x + y
        fin_ref[mine] = win_ref[...].astype(BF16)
        fout_ref[mine] = wout_ref[...].astype(BF16)
        chips = [(1 - x, y), (x, 1 - y), (1 - x, 1 - y)]

        def halves(chip, half):
            return (fin_ref.at[chip, pl.ds(half * 512, 512), :], fout_ref.at[chip, pl.ds(half * 128, 128), :])

        def copy(k, ref, to):
            return pltpu.make_async_remote_copy(src_ref=ref, dst_ref=ref, send_sem=send_sems.at[k],
                                                recv_sem=recv_sems.at[k], device_id=to, device_id_type=MESH)

        first, passed = [], []
        for j, (cx, cy) in enumerate(chips):
            for a, ref in enumerate(halves(mine, c)):
                first.append(copy(2 * j + a, ref, (cx, cy, c)))
        for cp in first:
            cp.start()
        for j, (cx, cy) in enumerate(chips):
            for a, ref in enumerate(halves(2 * cx + cy, c)):
                copy(2 * j + a, ref, me).wait_recv()
                fwd = copy(6 + 2 * j + a, ref, sib)
                fwd.start()
                passed.append(fwd)
        for j, (cx, cy) in enumerate(chips):
            for a, ref in enumerate(halves(2 * cx + cy, 1 - c)):
                copy(6 + 2 * j + a, ref, me).wait_recv()
        for cp in first + passed:
            cp.wait_send()

    vm = pl.BlockSpec(memory_space=pltpu.VMEM)
    return pl.pallas_call(
        body, name="gather_weights",
        in_specs=[vm, vm], out_specs=[vm, vm],
        out_shape=[jax.ShapeDtypeStruct((4, D_MODEL, 1024), BF16), jax.ShapeDtypeStruct((4, 256, D_MODEL), BF16)],
        scratch_shapes=[pltpu.SemaphoreType.DMA((12,)), pltpu.SemaphoreType.DMA((12,))],
        compiler_params=pltpu.CompilerParams(vmem_limit_bytes=VMEM_LIMIT),
    )(w_in_s, w_out_s)


def _swap_halves(g_in, g_out):
    def body(gin_ref, gout_ref, rin_ref, rout_ref, send_sems, recv_sems):
        x, y, c = _place()
        sib = (x, y, 1 - c)
        cps = [pltpu.make_async_remote_copy(src_ref=src.at[:, 1 - c], dst_ref=dst, send_sem=send_sems.at[k],
                                            recv_sem=recv_sems.at[k], device_id=sib, device_id_type=MESH)
               for k, (src, dst) in enumerate(((gin_ref, rin_ref), (gout_ref, rout_ref)))]
        for cp in cps:
            cp.start()
        for cp in cps:
            cp.wait()

    hbm = pl.BlockSpec(memory_space=pl.ANY)
    return pl.pallas_call(
        body, name="swap_halves",
        in_specs=[hbm, hbm], out_specs=[hbm, hbm],
        out_shape=[jax.ShapeDtypeStruct((4,) + g.shape[2:], F32) for g in (g_in, g_out)],
        scratch_shapes=[pltpu.SemaphoreType.DMA((2,)), pltpu.SemaphoreType.DMA((2,))],
    )(g_in, g_out)


def _add_half(g, r, cidx, name):
    n, _, rows, cols = g.shape

    def body(c_ref, g_ref, r_ref, o_ref):
        o_ref[0] = (g_ref[0, 0] + r_ref[0]).astype(BF16)

    return pl.pallas_call(
        body, name=name,
        grid_spec=pltpu.PrefetchScalarGridSpec(
            num_scalar_prefetch=1, grid=(n,),
            in_specs=[pl.BlockSpec((1, 1, rows, cols), lambda j, c_ref: (j, c_ref[0], 0, 0)),
                      pl.BlockSpec((1, rows, cols), lambda j, c_ref: (j, 0, 0))],
            out_specs=pl.BlockSpec((1, rows, cols), lambda j, c_ref: (j, 0, 0))),
        out_shape=jax.ShapeDtypeStruct((n, rows, cols), BF16),
        compiler_params=_params(("parallel",)),
    )(cidx, g, r)


def _exchange_copies(in_ref, out_ref, lin_ref, lout_ref, send_sems, recv_sems):
    x, y, c = _place()
    cps = []
    for k, (cx, cy) in enumerate([(1 - x, y), (x, 1 - y), (1 - x, 1 - y)]):
        for a, (src, dst) in enumerate(((in_ref, lin_ref), (out_ref, lout_ref))):
            cps.append(pltpu.make_async_remote_copy(
                src_ref=src.at[2 * cx + cy], dst_ref=dst.at[k], send_sem=send_sems.at[2 * k + a],
                recv_sem=recv_sems.at[2 * k + a], device_id=(cx, cy, c), device_id_type=MESH))
    return cps


def _exchange_start(cs_in, cs_out):
    def body(in_ref, out_ref, lin_ref, lout_ref, send_sems, recv_sems, in_thru, out_thru, lin_thru, lout_thru, token):
        for cp in _exchange_copies(in_ref, out_ref, lin_ref, lout_ref, send_sems, recv_sems):
            cp.start()
        token[...] = jnp.zeros_like(token)

    lands = [lax.empty((3,) + a.shape[1:], a.dtype) for a in (cs_in, cs_out)]
    bufs = [pltpu.with_memory_space_constraint(a, pltpu.HBM) for a in (cs_in, cs_out, *lands)]
    hbm = pl.BlockSpec(memory_space=pltpu.HBM)
    sem = pl.BlockSpec(memory_space=pltpu.SEMAPHORE)
    return pl.pallas_call(
        body, name="exchange_start",
        in_specs=[hbm] * 4,
        out_specs=[sem, sem, hbm, hbm, hbm, hbm, pl.BlockSpec(memory_space=pltpu.VMEM)],
        out_shape=[pltpu.SemaphoreType.DMA((6,)), pltpu.SemaphoreType.DMA((6,))]
                  + [pltpu.HBM(b.shape, b.dtype) for b in bufs] + [jax.ShapeDtypeStruct((8, 128), F32)],
        input_output_aliases={0: 2, 1: 3, 2: 4, 3: 5},
        compiler_params=pltpu.CompilerParams(has_side_effects=pltpu.SideEffectType.DATAFLOW_SIDE_EFFECTING),
    )(*bufs)


def _exchange_wait(send_sems, recv_sems, in_thru, out_thru, lin_thru, lout_thru, after):
    def body(in_ref, out_ref, lin_ref, lout_ref, send_sems, recv_sems, after_ref, in_dead, out_dead, got_in, got_out):
        for cp in _exchange_copies(in_ref, out_ref, lin_ref, lout_ref, send_sems, recv_sems):
            cp.wait_send()
            cp.wait_recv()

    hbm = pl.BlockSpec(memory_space=pltpu.HBM)
    sem = pl.BlockSpec(memory_space=pltpu.SEMAPHORE)
    bufs = (in_thru, out_thru, lin_thru, lout_thru)
    return pl.pallas_call(
        body, name="exchange_wait",
        in_specs=[hbm] * 4 + [sem, sem, pl.BlockSpec(memory_space=pl.ANY)],
        out_specs=[hbm] * 4,
        out_shape=[pltpu.HBM(b.shape, b.dtype) for b in bufs],
        input_output_aliases={0: 0, 1: 1, 2: 2, 3: 3},
        compiler_params=pltpu.CompilerParams(has_side_effects=pltpu.SideEffectType.DATAFLOW_SIDE_EFFECTING),
    )(*bufs, send_sems, recv_sems, after)


PEER_XOR = (2, 1, 3)


def _sum_chips(cs, r, chip_idx, name):
    _, rows, cols = r.shape
    tr = min(rows, 256)

    def body(m_ref, cs_ref, r_ref, o_ref):
        mine = m_ref[0]
        own = cs_ref[0].astype(F32)
        got = [r_ref[k].astype(F32) for k in range(3)]
        acc = None
        for s in range(4):
            rel = mine ^ s
            term = jnp.where(rel == 0, own, jnp.where(rel == PEER_XOR[0], got[0],
                                                      jnp.where(rel == PEER_XOR[1], got[1], got[2])))
            acc = term if acc is None else acc + term
        o_ref[...] = acc

    return pl.pallas_call(
        body, name=name,
        grid_spec=pltpu.PrefetchScalarGridSpec(
            num_scalar_prefetch=1, grid=(rows // tr,),
            in_specs=[pl.BlockSpec((1, tr, cols), lambda i, m_ref: (m_ref[0], i, 0)),
                      pl.BlockSpec((3, tr, cols), lambda i, m_ref: (0, i, 0))],
            out_specs=pl.BlockSpec((tr, cols), lambda i, m_ref: (i, 0))),
        out_shape=jax.ShapeDtypeStruct((rows, cols), F32),
        compiler_params=_params(("parallel",)),
    )(chip_idx, cs, r)


def _swap_reduced(h_in, h_out):
    def body(in_ref, out_ref, rin_ref, rout_ref, send_sems, recv_sems):
        x, y, c = _place()
        cps = [pltpu.make_async_remote_copy(src_ref=src, dst_ref=dst, send_sem=send_sems.at[k],
                                            recv_sem=recv_sems.at[k], device_id=(x, y, 1 - c), device_id_type=MESH)
               for k, (src, dst) in enumerate(((in_ref, rin_ref), (out_ref, rout_ref)))]
        for cp in cps:
            cp.start()
        for cp in cps:
            cp.wait()

    hbm = pl.BlockSpec(memory_space=pl.ANY)
    return pl.pallas_call(
        body, name="swap_reduced",
        in_specs=[hbm, hbm], out_specs=[hbm, hbm],
        out_shape=[jax.ShapeDtypeStruct(h.shape, F32) for h in (h_in, h_out)],
        scratch_shapes=[pltpu.SemaphoreType.DMA((2,)), pltpu.SemaphoreType.DMA((2,))],
    )(h_in, h_out)


def _adamw_halves(w, mine, theirs, m, v, cidx, name):
    rows, cols = w.shape
    half = rows // 2
    tr = min(half, 256)
    nbh = half // tr

    def body(c_ref, w_ref, a_ref, b_ref, m_ref, v_ref, g_ref, d_ref, nm_ref, nv_ref):
        gg = jnp.where(pl.program_id(0) // nbh == c_ref[0], a_ref[...], b_ref[...])
        g_ref[...] = gg
        _adamw_update(gg, w_ref, m_ref, v_ref, d_ref, nm_ref, nv_ref)

    spec = pl.BlockSpec((tr, cols), lambda i, c_ref: (i, 0))
    hspec = pl.BlockSpec((tr, cols), lambda i, c_ref: (i % nbh, 0))
    sds = jax.ShapeDtypeStruct((rows, cols), F32)
    return pl.pallas_call(
        body, name=name,
        grid_spec=pltpu.PrefetchScalarGridSpec(
            num_scalar_prefetch=1, grid=(rows // tr,),
            in_specs=[spec, hspec, hspec, spec, spec], out_specs=[spec] * 4),
        out_shape=[sds] * 4,
        compiler_params=_params(("parallel",)),
    )(cidx, w, mine, theirs, m, v)


def _allreduce_small(g_nw, g_fw, g_hgw, g_lbl, loss8):
    def body(nw_ref, fw_ref, hgw_ref, lbl_ref, loss_ref, out_ref, slots, send_sems, recv_sems):
        x, y, c = _place()
        me = 4 * x + 2 * y + c
        slots[me] = jnp.zeros((8, D_MODEL), F32)
        slots[me, 0:1, :] = nw_ref[...]
        slots[me, 1:2, :] = fw_ref[...]
        slots[me, 2:3, 0:HG_W] = hgw_ref[...]
        slots[me, 3:4, 0:HG_W] = lbl_ref[0:1, :]
        slots[me, 3:4, HG_W:] = lbl_ref[1:2, :]
        slots[me, 4:5, 0:128] = loss_ref[0:1, :]
        cps = []
        for k in range(1, 8):
            dx, dy, dc = (k >> 2) & 1, (k >> 1) & 1, k & 1
            to = (x ^ dx, y ^ dy, c ^ dc)
            cps.append(pltpu.make_async_remote_copy(
                src_ref=slots.at[me], dst_ref=slots.at[me], send_sem=send_sems.at[k - 1],
                recv_sem=recv_sems.at[k - 1], device_id=to, device_id_type=MESH))
        for cp in cps:
            cp.start()
        for cp in cps:
            cp.wait()
        acc = slots[0]
        for i in range(1, 8):
            acc = acc + slots[i]
        out_ref[...] = acc

    vm = pl.BlockSpec(memory_space=pltpu.VMEM)
    return pl.pallas_call(
        body, name="allreduce_small",
        in_specs=[vm] * 5, out_specs=vm,
        out_shape=jax.ShapeDtypeStruct((8, D_MODEL), F32),
        scratch_shapes=[pltpu.VMEM((8, 8, D_MODEL), F32), pltpu.SemaphoreType.DMA((7,)), pltpu.SemaphoreType.DMA((7,))],
    )(g_nw, g_fw, g_hgw, g_lbl, loss8)


def _rope_tables(S):
    inv_freq = 1.0 / (ROPE_THETA ** (jnp.arange(ROPE_HALF, dtype=F32) / ROPE_HALF))
    ang = jnp.arange(S, dtype=jnp.int32).astype(F32)[:, None] * inv_freq[None, :]
    cos, sin = jnp.cos(ang), jnp.sin(ang)
    cos2 = jnp.concatenate([cos, cos, cos, cos], axis=-1)
    sin2 = jnp.concatenate([-sin, sin, -sin, sin], axis=-1)
    return cos2, sin2


def _local_step(x2, tgt2, norm_w, w_in_full, lbl, hg_norm_w, w_out4, fnw):
    S = x2.shape[0]
    cos2, sin2 = _rope_tables(S)
    w_out_full = w_out4.reshape(D_MODEL, D_MODEL)
    p0, p1, qkv, qkv4, qkv16, p3, ut = _inproj(x2, norm_w, w_in_full, cos2, sin2)
    o_hg, mix_hg, mixt_hg, states = _hg_fwd(p0, p1, lbl, hg_norm_w)
    qkv_ds = [qkv.reshape(1, S, 3 * AT_W), qkv4, qkv16]
    os_d, ls_d = zip(*[_att_fwd(q) for q in qkv_ds])
    o_at, lse, mix_at, mixt_at = _att_combine(os_d, ls_d, p3)
    dh, dm_hg, dm_at, g_wout, g_fw, loss8 = _outproj(x2, tgt2, mix_hg, mix_at, mixt_hg, mixt_at, w_out_full, fnw)
    dqr, dfl, dv_hg, dz_hg, g_lbl, g_hgw = _hg_bwd(p0, p1, o_hg, dm_hg, states, lbl, hg_norm_w)
    do1, do4, do16, dl1, dl4, dl16, dz_at = _att_gate_bwd(dm_at, o_at, lse, p3)
    do_ds = [do1.reshape(1, S, AT_W), do4, do16]
    dl_ds = [dl1.reshape(1, S, STAT_W), dl4, dl16]
    dqkvs = [_att_bwd(q, a, b) for q, a, b in zip(qkv_ds, do_ds, dl_ds)]
    dq_at, dk_at, dv_at = _att_bwd_combine(dqkvs, cos2, sin2)
    dps = [dqr, dfl, dv_hg, dz_hg, dq_at, dk_at, dv_at, dz_at]
    return loss8, dps, ut, dh, g_lbl, g_hgw, g_wout, g_fw


def kernel(x, norm_w, w_in, hgrn_lb_logits, hg_norm_w, w_out, final_norm_w, loss_target, m_norm_w, m_w_in, m_hgrn_lb_logits, m_hg_norm_w, m_w_out, m_final_norm_w, v_norm_w, v_w_in, v_hgrn_lb_logits, v_hg_norm_w, v_w_out, v_final_norm_w):
    S = x.shape[1]
    w_in_full, w_out_full = _gather_weights(w_in[0], w_out[0])
    loss8, dps, ut, dh, g_lbl, g_hgw, g_wout, g_fw = _local_step(
        x[0], loss_target[0], norm_w, w_in_full, hgrn_lb_logits, hg_norm_w,
        w_out_full, final_norm_w.reshape(1, D_MODEL))

    cidx = lax.axis_index("c").astype(jnp.int32).reshape(1)
    g_win = _inproj_bwd_w(ut, dps)
    g_in4 = g_win.reshape(4, 2, 512, 1024)
    g_out4 = g_wout.reshape(4, 2, 128, D_MODEL)
    r_in, r_out = _swap_halves(g_in4, g_out4)
    cs_in = _add_half(g_in4, r_in, cidx, "add_half_in")
    cs_out = _add_half(g_out4, r_out, cidx, "add_half_out")
    *started, token = _exchange_start(cs_in, cs_out)
    grad_x, g_nw = _inproj_bwd_x(dps, w_in_full, x[0], norm_w + token[0:1, 0:1], dh)
    cs_in, cs_out, x_in, x_out = _exchange_wait(*started, g_nw)
    chip_idx = (2 * lax.axis_index("x") + lax.axis_index("y")).astype(jnp.int32).reshape(1)
    h_in = _sum_chips(cs_in, x_in, chip_idx, "sum_chips_in")
    h_out = _sum_chips(cs_out, x_out, chip_idx, "sum_chips_out")
    s_in, s_out = _swap_reduced(h_in, h_out)

    red = _allreduce_small(g_nw, g_fw, g_hgw, g_lbl, loss8)
    loss = red[4, 0]
    grad_norm_w = red[0:1, :]
    grad_final_norm_w = red[1, :]
    grad_hg_norm_w = red[2:3, :HG_W]
    grad_lbl = jnp.concatenate([red[3:4, :HG_W], red[3:4, HG_W:]], axis=0)

    d_nw, m_nw, v_nw = _adamw(norm_w, grad_norm_w, m_norm_w, v_norm_w, "adamw_norm_w")
    grad_w_in, d_win, m_win, v_win = _adamw_halves(w_in[0], h_in, s_in, m_w_in[0], v_w_in[0], cidx, "adamw_w_in")
    d_lbl, m_lbl, v_lbl = _adamw(hgrn_lb_logits, grad_lbl, m_hgrn_lb_logits, v_hgrn_lb_logits, "adamw_lb_logits")
    d_hgw, m_hgw, v_hgw = _adamw(hg_norm_w, grad_hg_norm_w, m_hg_norm_w, v_hg_norm_w, "adamw_hg_norm_w")
    grad_w_out, d_wout, m_wout, v_wout = _adamw_halves(w_out[0], h_out, s_out, m_w_out[0], v_w_out[0], cidx, "adamw_w_out")
    d_fw, m_fw, v_fw = _adamw(final_norm_w.reshape(1, D_MODEL), grad_final_norm_w.reshape(1, D_MODEL),
                              m_final_norm_w.reshape(1, D_MODEL), v_final_norm_w.reshape(1, D_MODEL), "adamw_final_norm_w")
    e1 = lambda a: a[None]
    flat = lambda a: a.reshape(D_MODEL)
    return (loss, grad_x[None], grad_norm_w, e1(grad_w_in), grad_lbl, grad_hg_norm_w, e1(grad_w_out), grad_final_norm_w,
            d_nw, e1(d_win), d_lbl, d_hgw, e1(d_wout), flat(d_fw),
            m_nw, e1(m_win), m_lbl, m_hgw, e1(m_wout), flat(m_fw),
            v_nw, e1(v_win), v_lbl, v_hgw, e1(v_wout), flat(v_fw))
```

```python
import jax
import jax.numpy as jnp
import numpy as np
from jax import lax
from jax.experimental import pallas as pl
from jax.experimental.pallas import tpu as pltpu

F32 = jnp.float32
BF16 = jnp.bfloat16
MESH = pl.DeviceIdType.MESH

D_MODEL = 1024
HG_W = 512
AT_W = 512
HEAD_PAIR = 128
ROPE_HALF = 32
ROPE_THETA = 10000.0
EPS = 1e-6
CHUNK = 128
LEVELS = (64, 32, 16, 8, 4, 2, 1)
DIAG = 1
SUBLANES = 8
ATT_BLK = 128
ATT_QB = 4
DILATIONS = (1, 4, 16)
ATT_SCALE = 0.125
STAT_W = 128
STAT_LANES = 16
STAT_LSE_LANE = 8
NEG = -1e30
VMEM_LIMIT = 56 * 1024 * 1024

ADAM_LR = 0.001
ADAM_B1 = 0.9
ADAM_B2 = 0.999
ADAM_EPS = 1e-08
ADAM_WD = 0.01
ADAM_STEP = 10


def _iota(shape, dim):
    return lax.broadcasted_iota(jnp.int32, shape, dim)


def _dot(a, b):
    return jnp.dot(a, b, preferred_element_type=F32)


def _dot_nt(a, b):
    return lax.dot_general(a, b, (((1,), (1,)), ((), ())), preferred_element_type=F32)


def _sigmoid(v):
    return 0.5 * jnp.tanh(0.5 * v) + 0.5


def _params(sem=None, vmem=VMEM_LIMIT):
    return pltpu.CompilerParams(dimension_semantics=sem, vmem_limit_bytes=vmem)


def _full(shape):
    n = len(shape)
    return pl.BlockSpec(shape, lambda *_: (0,) * n)


def _rope_rot(y):
    n = y.shape[1]
    first = (_iota(y.shape, 1) & (2 * ROPE_HALF - 1)) < ROPE_HALF
    return jnp.where(first, pltpu.roll(y, n - ROPE_HALF, 1), pltpu.roll(y, ROPE_HALF, 1))


def _dil_spec(d, tm, width):
    return pl.BlockSpec((d, tm // d, width), lambda s: (0, s, 0))


LANES = 128


def _slab_scratch(tm, width):
    return pltpu.VMEM((width // LANES, tm, LANES), F32)


def _to_slabs(v, slabs_ref):
    for j in range(slabs_ref.shape[0]):
        slabs_ref[j] = v[:, LANES * j: LANES * (j + 1)]


def _from_slabs(slabs_ref):
    return jnp.concatenate([slabs_ref[j] for j in range(slabs_ref.shape[0])], axis=1)


def _split_residues(slabs_ref, dst_ref, d, dtype):
    nslab, tm, _ = slabs_ref.shape
    for r in range(d):
        for j in range(nslab):
            dst_ref[r, :, LANES * j: LANES * (j + 1)] = slabs_ref[j, pl.ds(r, tm // d, stride=d), :].astype(dtype)


def _merge_residues(src_ref, slabs_ref, d):
    nslab, tm, _ = slabs_ref.shape
    for r in range(d):
        for j in range(nslab):
            slabs_ref[j, pl.ds(r, tm // d, stride=d), :] = src_ref[r, :, LANES * j: LANES * (j + 1)].astype(F32)


def _inproj(x2, norm_w, w_in_full, cos2, sin2):
    S = x2.shape[0]
    tm = 256

    def body(x_ref, nw_ref, w_ref, cos_ref, sin_ref, p0_ref, p1_ref, qkv_ref, qkv4_ref, qkv16_ref, p3_ref, ut_ref, scr):
        x = x_ref[...]
        r = lax.rsqrt(jnp.mean(x * x, axis=-1, keepdims=True) + EPS)
        u = x * r * nw_ref[...]
        ub = u.astype(BF16)
        ut_ref[...] = u.T.astype(BF16)
        p0_ref[...] = _dot(ub, w_ref[0])
        p1_ref[...] = _dot(ub, w_ref[1])
        y2 = _dot(ub, w_ref[2])
        cosf = jnp.tile(cos_ref[...], (1, 8))
        sinf = jnp.tile(sin_ref[...], (1, 8))
        y3 = _dot(ub, w_ref[3])
        p3_ref[...] = y3
        qkv = jnp.concatenate([y2 * cosf + _rope_rot(y2) * sinf, y3[:, :AT_W]], axis=1)
        qkv_ref[...] = qkv.astype(BF16)
        _to_slabs(qkv, scr)
        _split_residues(scr, qkv4_ref, 4, BF16)
        _split_residues(scr, qkv16_ref, 16, BF16)

    row = lambda w: pl.BlockSpec((tm, w), lambda s: (s, 0))
    qkv_w = 3 * AT_W
    return pl.pallas_call(
        body, name="inproj", grid=(S // tm,),
        in_specs=[row(D_MODEL), _full((1, D_MODEL)), _full((4, D_MODEL, 1024)), row(128), row(128)],
        out_specs=[row(1024), row(1024), row(qkv_w), _dil_spec(4, tm, qkv_w), _dil_spec(16, tm, qkv_w), row(1024),
                   pl.BlockSpec((D_MODEL, tm), lambda s: (0, s))],
        out_shape=[jax.ShapeDtypeStruct((S, 1024), F32), jax.ShapeDtypeStruct((S, 1024), F32),
                   jax.ShapeDtypeStruct((S, qkv_w), BF16), jax.ShapeDtypeStruct((4, S // 4, qkv_w), BF16),
                   jax.ShapeDtypeStruct((16, S // 16, qkv_w), BF16), jax.ShapeDtypeStruct((S, 1024), F32),
                   jax.ShapeDtypeStruct((D_MODEL, S), BF16)],
        scratch_shapes=[_slab_scratch(tm, qkv_w)],
        compiler_params=_params(("parallel",)),
    )(x2, norm_w, w_in_full, cos2, sin2)


HG_HPS = 4
N_LEV = len(LEVELS)


def _hg_const_arrays():
    r = np.arange(CHUNK)[:, None]
    c = np.arange(CHUNK)[None, :]
    tris = np.stack([r >= c, r <= c])
    lm = [((r // (2 * m)) == (c // (2 * m))) & (r % (2 * m) >= m) & (c % (2 * m) < m) for m in LEVELS]
    dm = [(c == r - dl) & (r % DIAG >= dl) for dl in range(DIAG)]
    masks = np.stack(lm + [x.T for x in lm] + dm)
    return jnp.asarray(tris, BF16), jnp.asarray(masks, F32)


def _split2(a):
    hi = a.astype(BF16)
    return hi, (a - hi.astype(F32)).astype(BF16)


def _dot3(a, b):
    ah, al = _split2(a)
    bh, bl = _split2(b)
    n = b.shape[1]
    p = _dot(ah, jnp.concatenate([bh, bl], axis=1))
    return (p[:, :n] + p[:, n:]) + _dot(al, bh)


def _split3(a):
    a1 = a.astype(BF16)
    r1 = a - a1.astype(F32)
    a2 = r1.astype(BF16)
    return a1, a2, (r1 - a2.astype(F32)).astype(BF16)


def _tri_dot(tri, a):
    n = a.shape[1]
    p = _dot(tri, jnp.concatenate(_split3(a), axis=1))
    return (p[:, :n] + p[:, n:2 * n]) + p[:, 2 * n:]


def _dot_sel(a, sel):
    a1, a2, a3 = _split3(a)
    return (_dot(a1, sel) + _dot(a2, sel)) + _dot(a3, sel)


def _rowsum(t):
    return _dot(t.astype(BF16), jnp.ones((t.shape[1], t.shape[1]), BF16))


def _level_refs(b):
    refs = []
    pos = _iota(b.shape, 0)
    for m in LEVELS:
        if 2 * m >= SUBLANES:
            parts = [jnp.broadcast_to(b[r0 + m - 1: r0 + m, :], (2 * m, b.shape[1])) for r0 in range(0, CHUNK, 2 * m)]
            refs.append(parts[0] if len(parts) == 1 else jnp.concatenate(parts, axis=0))
        else:
            p = pos & (2 * m - 1)
            ref = b
            for off in range(-(m - 1), m + 1):
                if off != 0:
                    ref = jnp.where(p == m - 1 + off, pltpu.roll(b, off % CHUNK, 0), ref)
            refs.append(ref)
    return refs


def _hg_lb(lbl_ref):
    l0 = lbl_ref[0:1, :]
    l1 = lbl_ref[1:2, :]
    mx = jnp.maximum(l0, l1)
    e0 = jnp.exp(l0 - mx)
    e1 = jnp.exp(l1 - mx)
    p0 = e0 / (e0 + e1)
    lb = jnp.clip(p0, 1e-6, 1.0 - 1e-6)
    inside = (p0 >= 1e-6) & (p0 <= 1.0 - 1e-6)
    dlb_dl0 = jnp.where(inside, p0 * (e1 / (e0 + e1)), 0.0)
    return lb, dlb_dl0


def _hg_gates(qr, fl, lb):
    sig = _sigmoid(fl)
    f = lb + (1.0 - lb) * sig
    g = jnp.log(f)
    k = (1.0 - lb) * (1.0 - sig)
    sq = _sigmoid(qr)
    q = qr * sq
    return sig, f, g, k, sq, q


def _hg_levels(q, k, b, mk_ref):
    refs = _level_refs(b)
    a = jnp.zeros((CHUNK, CHUNK), F32)
    es, qts, kts = [], [], []
    for i in range(N_LEV):
        diff = b - refs[i]
        e = jnp.exp(jnp.minimum(diff, -diff))
        qt = (q * e).astype(BF16)
        kt = (k * e).astype(BF16)
        a = a + _dot_nt(qt, kt) * mk_ref[i]
        es.append(e); qts.append(qt); kts.append(kt)
    return a, es, qts, kts


def _hg_specs(nc, rev):
    cc = (lambda c: nc - 1 - c) if rev else (lambda c: c)
    w = 128 * HG_HPS
    blk = lambda off: pl.BlockSpec((CHUNK, w), lambda h, c: (cc(c), h + off))
    vec = pl.BlockSpec((1, w), lambda h, c: (0, h))
    lb2 = pl.BlockSpec((2, w), lambda h, c: (0, h))
    st = pl.BlockSpec((1, HG_HPS, 128, 128), lambda h, c: (cc(c), h, 0, 0))
    consts = [_full((2, CHUNK, CHUNK)), _full((2 * N_LEV + DIAG, CHUNK, CHUNK))]
    return blk, vec, lb2, st, consts


def _hg_fwd(p0, p1, lbl, gw):
    S = p0.shape[0]
    nc = S // CHUNK
    ng = 4 // HG_HPS

    def body(qr_ref, fl_ref, v_ref, z_ref, lbl_ref, gw_ref, tri_ref, mk_ref,
             o_ref, mix_ref, mixt_ref, st_ref, state):
        c = pl.program_id(1)

        @pl.when(c == 0)
        def _():
            state[...] = jnp.zeros_like(state)

        lb_all, _ = _hg_lb(lbl_ref)
        heads = [slice(128 * hh, 128 * (hh + 1)) for hh in range(HG_HPS)]
        qs, ks, bs, mats = [], [], [], []
        for sl in heads:
            _, _, g, k, _, q = _hg_gates(qr_ref[:, sl], fl_ref[:, sl], lb_all[:, sl])
            qs.append(q); ks.append(k); bs.append(_tri_dot(tri_ref[0], g))
        for hh in range(HG_HPS):
            a, _, _, _ = _hg_levels(qs[hh], ks[hh], bs[hh], mk_ref)
            mats.append(a + _rowsum(qs[hh] * ks[hh]) * mk_ref[2 * N_LEV])
        for hh, sl in enumerate(heads):
            q, k, b, a, v, z = qs[hh], ks[hh], bs[hh], mats[hh], v_ref[:, sl], z_ref[:, sl]
            b_last = b[CHUNK - 1: CHUNK, :]
            st = state[hh]
            st_ref[0, hh] = st
            o = _dot_nt((q * jnp.exp(b)).astype(BF16), st.astype(BF16)) + _dot(a.astype(BF16), v.astype(BF16))
            state[hh] = st * jnp.exp(b_last) + _dot3(v.T, k * jnp.exp(b_last - b))
            o_ref[:, sl] = o
            rs = lax.rsqrt(jnp.mean(o * o, axis=-1, keepdims=True) + EPS)
            mixed = o * rs * gw_ref[:, sl] * (z * _sigmoid(z))
            mix_ref[:, sl] = mixed.astype(BF16)
            mixt_ref[sl, :] = mixed.T.astype(BF16)

    blk, vec, lb2, st_spec, consts = _hg_specs(nc, False)
    tris, masks = _hg_const_arrays()
    return pl.pallas_call(
        body, name="hg_fwd", grid=(ng, nc),
        in_specs=[blk(0), blk(ng), blk(0), blk(ng), lb2, vec] + consts,
        out_specs=[blk(0), blk(0), pl.BlockSpec((128 * HG_HPS, CHUNK), lambda h, c: (h, c)), st_spec],
        out_shape=[jax.ShapeDtypeStruct((S, HG_W), F32), jax.ShapeDtypeStruct((S, HG_W), BF16),
                   jax.ShapeDtypeStruct((HG_W, S), BF16), jax.ShapeDtypeStruct((nc, 4, 128, 128), F32)],
        scratch_shapes=[pltpu.VMEM((HG_HPS, 128, 128), F32)],
        compiler_params=_params(("parallel", "arbitrary")),
    )(p0, p0, p1, p1, lbl, gw, tris, masks)


def _hg_bwd(p0, p1, o_raw, dm, states, lbl, gw):
    S = p0.shape[0]
    nc = S // CHUNK
    ng = 4 // HG_HPS
    w = 128 * HG_HPS

    def body(qr_ref, fl_ref, v_ref, z_ref, o_ref, dm_ref, st_ref, lbl_ref, gw_ref, tri_ref, mk_ref, mkb_ref,
             dqr_ref, dfl_ref, dv_ref, dz_ref, glbl_ref, ggw_ref, dstate, carry, acc_lb, acc_gw):
        c = pl.program_id(1)

        @pl.when(c == 0)
        def _():
            dstate[...] = jnp.zeros_like(dstate)
            carry[...] = jnp.zeros_like(carry)
            acc_lb[...] = jnp.zeros_like(acc_lb)
            acc_gw[...] = jnp.zeros_like(acc_gw)

        lb_all, dlb_dl0 = _hg_lb(lbl_ref)
        heads = [slice(128 * hh, 128 * (hh + 1)) for hh in range(HG_HPS)]
        diag_mask = mk_ref[2 * N_LEV]
        gates, bs, dos = [], [], []
        for sl in heads:
            gt = _hg_gates(qr_ref[:, sl], fl_ref[:, sl], lb_all[:, sl])
            gates.append(gt)
            bs.append(_tri_dot(tri_ref[0], gt[2]))
        for hh, sl in enumerate(heads):
            o, z, dmix, gwv = o_ref[:, sl], z_ref[:, sl], dm_ref[:, sl], gw_ref[:, sl]
            rs = lax.rsqrt(jnp.mean(o * o, axis=-1, keepdims=True) + EPS)
            oh = o * rs
            sz = _sigmoid(z)
            dz_ref[:, sl] = (dmix * (oh * gwv) * (sz * (1.0 + z * (1.0 - sz)))).astype(BF16)
            don = dmix * (z * sz)
            acc_gw[0:1, sl] += jnp.sum(don * oh, axis=0, keepdims=True)
            dy = don * gwv
            dos.append(rs * (dy - oh * jnp.mean(dy * oh, axis=-1, keepdims=True)))
        inter = []
        for hh, sl in enumerate(heads):
            _, _, _, k, _, q = gates[hh]
            b, do, v = bs[hh], dos[hh], v_ref[:, sl]
            b_last = b[CHUNK - 1: CHUNK, :]
            eb = jnp.exp(b)
            edec = jnp.exp(b_last - b)
            dst = dstate[hh]
            dq = _dot3(do, st_ref[0, hh]) * eb
            dk = _dot3(v, dst) * edec
            da = _dot_nt(do.astype(BF16), v.astype(BF16))
            dv_state = _dot_nt((k * edec).astype(BF16), dst.astype(BF16))
            dstate[hh] = dst * jnp.exp(b_last) + _dot3(do.T, q * eb)
            inter.append((dq, dk, da, dv_state))
        for hh, sl in enumerate(heads):
            sig, f, _, k, sq, q = gates[hh]
            dq, dk, da, dv_state = inter[hh]
            b = bs[hh]
            db = q * dq - k * dk
            dab = da.astype(BF16)
            datb = da.T.astype(BF16)
            refs = _level_refs(b)
            a = _rowsum(q * k) * diag_mask
            for i in range(N_LEV):
                diff = b - refs[i]
                e = jnp.exp(jnp.minimum(diff, -diff))
                qt = (q * e).astype(BF16)
                kt = (k * e).astype(BF16)
                a = a + _dot_nt(qt, kt) * mk_ref[i]
                gq = _dot(dab * mkb_ref[i], kt)
                gk = _dot(datb * mkb_ref[N_LEV + i], qt)
                dq = dq + e * gq
                dk = dk + e * gk
                db = db + (qt.astype(F32) * gq - kt.astype(F32) * gk)
            dc = _rowsum(da * diag_mask)
            dq = dq + dc * k
            dk = dk + dc * q
            dv_ref[:, sl] = (_dot(a.T.astype(BF16), dos[hh].astype(BF16)) + dv_state).astype(BF16)
            dg = _tri_dot(tri_ref[1], db) + carry[0:1, sl]
            carry[0:1, sl] += jnp.sum(db, axis=0, keepdims=True)
            lb = lb_all[:, sl]
            qr = qr_ref[:, sl]
            t = dg / f - dk
            dfl_ref[:, sl] = (t * (1.0 - lb) * sig * (1.0 - sig)).astype(BF16)
            acc_lb[0:1, sl] += jnp.sum(t * (1.0 - sig), axis=0, keepdims=True)
            dqr_ref[:, sl] = (dq * (sq * (1.0 + qr * (1.0 - sq)))).astype(BF16)

        @pl.when(c == nc - 1)
        def _():
            gl0 = acc_lb[0:1, :] * dlb_dl0
            glbl_ref[0:1, :] = gl0
            glbl_ref[1:2, :] = -gl0
            ggw_ref[...] = acc_gw[0:1, :]

    blk, vec, lb2, st_spec, consts = _hg_specs(nc, True)
    tris, masks = _hg_const_arrays()
    act = jax.ShapeDtypeStruct((S, HG_W), BF16)
    return pl.pallas_call(
        body, name="hg_bwd", grid=(ng, nc),
        in_specs=[blk(0), blk(ng), blk(0), blk(ng), blk(0), blk(0), st_spec, lb2, vec] + consts + consts[1:],
        out_specs=[blk(0), blk(0), blk(0), blk(0), lb2, vec],
        out_shape=[act, act, act, act, jax.ShapeDtypeStruct((2, HG_W), F32), jax.ShapeDtypeStruct((1, HG_W), F32)],
        scratch_shapes=[pltpu.VMEM((HG_HPS, 128, 128), F32), pltpu.VMEM((8, w), F32),
                        pltpu.VMEM((8, w), F32), pltpu.VMEM((8, w), F32)],
        compiler_params=_params(("parallel", "arbitrary")),
    )(p0, p0, p1, p1, o_raw, dm, states, lbl, gw, tris, masks, masks.astype(BF16))


def _stat_head(lane):
    return lane >> 4


def _stat_lane(lane):
    return lane & (STAT_LANES - 1)


def _o_head(lane):
    return lane >> 6


def _att_bias():
    qi = np.arange(ATT_BLK)[:, None]
    kj = np.arange(2 * ATT_BLK)[None, :]
    band = (kj >= qi) & (kj <= qi + ATT_BLK)
    qm = np.stack([band & (kj >= ATT_BLK), band])
    cur = (kj < ATT_BLK) & (qi <= kj)
    km = np.stack([cur, cur | ((kj >= ATT_BLK) & (qi >= kj - ATT_BLK))])
    to_bias = lambda m: jnp.asarray(np.where(m, 0.0, NEG), F32)
    return to_bias(qm), to_bias(km)


def _att_fwd(qkv_d):
    d, L, _ = qkv_d.shape
    qb = ATT_QB
    rows = qb * ATT_BLK

    def body(q_ref, kp_ref, kc_ref, vp_ref, vc_ref, bias_ref, o_ref, lse_ref, s_scr, p_scr):
        first = _iota((ATT_BLK, HEAD_PAIR), 1) < 64
        head_of_lane = _stat_head(_iota((ATT_BLK, STAT_W), 1))
        pairs = [slice(HEAD_PAIR * hp, HEAD_PAIR * (hp + 1)) for hp in range(4)]
        blk = lambda i: slice(ATT_BLK * i, ATT_BLK * (i + 1))

        def keys(i, sl, prev_ref, cur_ref):
            before = prev_ref[0, :, sl] if i == 0 else cur_ref[0, blk(i - 1), sl]
            return jnp.concatenate([before, cur_ref[0, blk(i), sl]], axis=0)

        for i in range(qb):
            for hp, sl in enumerate(pairs):
                q2 = q_ref[0, blk(i), sl] * ATT_SCALE
                zero = jnp.zeros_like(q2)
                qs = jnp.concatenate([jnp.where(first, q2, zero), jnp.where(first, zero, q2)], axis=0)
                s_scr[4 * i + hp] = _dot_nt(qs, keys(i, sl, kp_ref, kc_ref))
        stats = []
        for i in range(qb):
            bias = bias_ref[jnp.minimum(pl.program_id(1), 1)] if i == 0 else bias_ref[1]
            bias2 = jnp.concatenate([bias, bias], axis=0)
            for hp in range(4):
                s = s_scr[4 * i + hp] + bias2
                m = jnp.max(s, axis=-1, keepdims=True)
                p = jnp.exp(s - m)
                l = jnp.sum(p, axis=-1, keepdims=True)
                p_scr[4 * i + hp] = p.astype(BF16)
                stats.append((l, m + jnp.log(l)))
        for i in range(qb):
            lse_blk = jnp.zeros((ATT_BLK, STAT_W), F32)
            for hp, sl in enumerate(pairs):
                l, lse = stats[4 * i + hp]
                o = _dot(p_scr[4 * i + hp], keys(i, sl, vp_ref, vc_ref)) / l
                o_ref[0, blk(i), sl] = jnp.where(first, o[:ATT_BLK], o[ATT_BLK:]).astype(BF16)
                lse_blk = jnp.where(head_of_lane == 2 * hp, lse[:ATT_BLK],
                                    jnp.where(head_of_lane == 2 * hp + 1, lse[ATT_BLK:], lse_blk))
            lse_ref[0, blk(i), :] = lse_blk

    cur = lambda j: pl.BlockSpec((1, rows, AT_W), lambda r, n: (r, n, j))
    prev = lambda j: pl.BlockSpec((1, ATT_BLK, AT_W), lambda r, n: (r, jnp.maximum(qb * n - 1, 0), j))
    return pl.pallas_call(
        body, name=f"att_fwd_d{d}", grid=(d, L // rows),
        in_specs=[cur(0), prev(1), cur(1), prev(2), cur(2), _full((2, ATT_BLK, 2 * ATT_BLK))],
        out_specs=[pl.BlockSpec((1, rows, AT_W), lambda r, n: (r, n, 0)), pl.BlockSpec((1, rows, STAT_W), lambda r, n: (r, n, 0))],
        out_shape=[jax.ShapeDtypeStruct((d, L, AT_W), BF16), jax.ShapeDtypeStruct((d, L, STAT_W), F32)],
        scratch_shapes=[pltpu.VMEM((4 * qb, 2 * ATT_BLK, 2 * ATT_BLK), F32), pltpu.VMEM((4 * qb, 2 * ATT_BLK, 2 * ATT_BLK), BF16)],
        compiler_params=_params(("parallel", "parallel")),
    )(qkv_d, qkv_d, qkv_d, qkv_d, qkv_d, _att_bias()[0])


def _att_combine(os_d, ls_d, p3):
    S = p3.shape[0]
    tm = 512

    def body(oa_ref, ob4_ref, oc16_ref, la_ref, lb4_ref, lc16_ref, z_ref,
             oat_ref, lse_ref, mix_ref, mixt_ref, ob_scr, oc_scr, lb_scr, lc_scr):
        _merge_residues(ob4_ref, ob_scr, 4)
        _merge_residues(oc16_ref, oc_scr, 16)
        _merge_residues(lb4_ref, lb_scr, 4)
        _merge_residues(lc16_ref, lc_scr, 16)
        ls = (la_ref[...], lb_scr[0], lc_scr[0])
        mx = jnp.maximum(jnp.maximum(ls[0], ls[1]), ls[2])
        es = [jnp.exp(l - mx) for l in ls]
        zs = es[0] + es[1] + es[2]
        lse_ref[...] = mx + jnp.log(zs)
        spread = ((_o_head(_iota((STAT_W, AT_W), 1)) == _stat_head(_iota((STAT_W, AT_W), 0)))
                  & (_stat_lane(_iota((STAT_W, AT_W), 0)) == 0)).astype(BF16)
        os_ = (oa_ref[...].astype(F32), _from_slabs(ob_scr), _from_slabs(oc_scr))
        o = jnp.zeros((tm, AT_W), F32)
        for e, oi in zip(es, os_):
            hi, lo = _split2(e / zs)
            o = o + (_dot(hi, spread) + _dot(lo, spread)) * oi
        oat_ref[...] = o
        z = z_ref[...]
        mixed = o * (z * _sigmoid(z))
        mix_ref[...] = mixed.astype(BF16)
        mixt_ref[...] = mixed.T.astype(BF16)

    row = lambda w: pl.BlockSpec((tm, w), lambda s: (s, 0))
    return pl.pallas_call(
        body, name="att_combine", grid=(S // tm,),
        in_specs=[row(AT_W), _dil_spec(4, tm, AT_W), _dil_spec(16, tm, AT_W),
                  row(STAT_W), _dil_spec(4, tm, STAT_W), _dil_spec(16, tm, STAT_W),
                  pl.BlockSpec((tm, AT_W), lambda s: (s, 1))],
        out_specs=[row(AT_W), row(STAT_W), row(AT_W), pl.BlockSpec((AT_W, tm), lambda s: (0, s))],
        out_shape=[jax.ShapeDtypeStruct((S, AT_W), F32), jax.ShapeDtypeStruct((S, STAT_W), F32),
                   jax.ShapeDtypeStruct((S, AT_W), BF16), jax.ShapeDtypeStruct((AT_W, S), BF16)],
        scratch_shapes=[_slab_scratch(tm, AT_W), _slab_scratch(tm, AT_W), _slab_scratch(tm, STAT_W), _slab_scratch(tm, STAT_W)],
        compiler_params=_params(("parallel",)),
    )(os_d[0].reshape(S, AT_W), os_d[1], os_d[2], ls_d[0].reshape(S, STAT_W), ls_d[1], ls_d[2], p3)


def _att_gate_bwd(dm_at, o_at, lse, p3):
    S = p3.shape[0]
    tm = 512

    def body(dm_ref, o_ref, l_ref, z_ref, do_ref, do4_ref, do16_ref, dl_ref, dl4_ref, dl16_ref, dz_ref, do_scr, dl_scr):
        o = o_ref[...]
        z = z_ref[...]
        dm = dm_ref[...]
        sz = _sigmoid(z)
        dz_ref[...] = (dm * o * (sz * (1.0 + z * (1.0 - sz)))).astype(BF16)
        do = dm * (z * sz)
        do_ref[...] = do.astype(BF16)
        gather = (_o_head(_iota((AT_W, STAT_W), 0)) == _stat_head(_iota((AT_W, STAT_W), 1))).astype(BF16)
        dl = jnp.where(_stat_lane(_iota((tm, STAT_W), 1)) < STAT_LSE_LANE, _dot_sel(do * o, gather), l_ref[...])
        dl_ref[...] = dl
        _to_slabs(do, do_scr)
        _to_slabs(dl, dl_scr)
        _split_residues(do_scr, do4_ref, 4, BF16)
        _split_residues(do_scr, do16_ref, 16, BF16)
        _split_residues(dl_scr, dl4_ref, 4, F32)
        _split_residues(dl_scr, dl16_ref, 16, F32)

    row = lambda w: pl.BlockSpec((tm, w), lambda s: (s, 0))
    sds = jax.ShapeDtypeStruct
    return pl.pallas_call(
        body, name="att_gate_bwd", grid=(S // tm,),
        in_specs=[row(AT_W), row(AT_W), row(STAT_W), pl.BlockSpec((tm, AT_W), lambda s: (s, 1))],
        out_specs=[row(AT_W), _dil_spec(4, tm, AT_W), _dil_spec(16, tm, AT_W),
                   row(STAT_W), _dil_spec(4, tm, STAT_W), _dil_spec(16, tm, STAT_W), row(AT_W)],
        out_shape=[sds((S, AT_W), BF16), sds((4, S // 4, AT_W), BF16), sds((16, S // 16, AT_W), BF16),
                   sds((S, STAT_W), F32), sds((4, S // 4, STAT_W), F32), sds((16, S // 16, STAT_W), F32),
                   sds((S, AT_W), BF16)],
        scratch_shapes=[_slab_scratch(tm, AT_W), _slab_scratch(tm, STAT_W)],
        compiler_params=_params(("parallel",)),
    )(dm_at, o_at, lse, p3)


def _att_bwd(qkv_d, do_d, dl_d):
    d, L, _ = qkv_d.shape
    nb = L // ATT_BLK
    qb = ATT_QB
    rows = qb * ATT_BLK
    nsteps = L // rows

    def body(qc_ref, qn_ref, kp_ref, kc_ref, vp_ref, vc_ref, ac_ref, an_ref, lc_ref, ln_ref, bq_ref, bk_ref, dqkv_ref,
             s_scr, dp_scr, st_scr, dpt_scr, ds_scr, pt_scr, dst_scr):
        n = pl.program_id(1)
        first = _iota((ATT_BLK, HEAD_PAIR), 1) < 64
        pairs = [slice(HEAD_PAIR * hp, HEAD_PAIR * (hp + 1)) for hp in range(4)]
        blk = lambda i: slice(ATT_BLK * i, ATT_BLK * (i + 1))

        def stack(t):
            zero = jnp.zeros_like(t)
            return jnp.concatenate([jnp.where(first, t, zero), jnp.where(first, zero, t)], axis=0)

        def unstack(t2):
            return jnp.where(first, t2[:ATT_BLK], t2[ATT_BLK:])

        def with_prev(i, sl, prev_ref, cur_ref):
            before = prev_ref[0, :, sl] if i == 0 else cur_ref[0, blk(i - 1), sl]
            return jnp.concatenate([before, cur_ref[0, blk(i), sl]], axis=0)

        def with_next(i, sl, cur_ref, next_ref):
            after = next_ref[0, :, sl] if i == qb - 1 else cur_ref[0, blk(i + 1), sl]
            return jnp.concatenate([cur_ref[0, blk(i), sl], after], axis=0)

        for i in range(qb):
            for hp, sl in enumerate(pairs):
                j = 4 * i + hp
                s_scr[j] = _dot_nt(stack(qc_ref[0, blk(i), sl] * ATT_SCALE), with_prev(i, sl, kp_ref, kc_ref))
                dp_scr[j] = _dot_nt(stack(ac_ref[0, blk(i), sl]), with_prev(i, sl, vp_ref, vc_ref))
                st_scr[j] = _dot_nt(stack(kc_ref[0, blk(i), sl] * ATT_SCALE), with_next(i, sl, qc_ref, qn_ref))
                dpt_scr[j] = _dot_nt(stack(vc_ref[0, blk(i), sl]), with_next(i, sl, ac_ref, an_ref))
        for i in range(qb):
            bias = bq_ref[jnp.minimum(n, 1)] if i == 0 else bq_ref[1]
            bias_t = bk_ref[jnp.minimum(nsteps - 1 - n, 1)] if i == qb - 1 else bk_ref[1]
            bias2 = jnp.concatenate([bias, bias], axis=0)
            bias_t2 = jnp.concatenate([bias_t, bias_t], axis=0)
            dl_c = lc_ref[0, blk(i), :]
            dl_t = with_next(i, slice(None), lc_ref, ln_ref).T
            for hp in range(4):
                j = 4 * i + hp
                at = [STAT_LANES * (2 * hp), STAT_LANES * (2 * hp + 1)]
                col = lambda t, o: jnp.concatenate([t[:, a + o: a + o + 1] for a in at], axis=0)
                p = jnp.exp(s_scr[j] + bias2 - col(dl_c, STAT_LSE_LANE))
                ds_scr[j] = (p * (dp_scr[j] - col(dl_c, 0))).astype(BF16)
                row = lambda t, o: jnp.concatenate([jnp.broadcast_to(t[a + o: a + o + 1, :], (ATT_BLK, 2 * ATT_BLK)) for a in at], axis=0)
                pt = jnp.exp(st_scr[j] + bias_t2 - row(dl_t, STAT_LSE_LANE))
                pt_scr[j] = pt.astype(BF16)
                dst_scr[j] = (pt * (dpt_scr[j] - row(dl_t, 0))).astype(BF16)
        for i in range(qb):
            for hp, sl in enumerate(pairs):
                j = 4 * i + hp
                dq = unstack(_dot(ds_scr[j], with_prev(i, sl, kp_ref, kc_ref))) * ATT_SCALE
                dk = unstack(_dot(dst_scr[j], with_next(i, sl, qc_ref, qn_ref))) * ATT_SCALE
                dv = unstack(_dot(pt_scr[j], with_next(i, sl, ac_ref, an_ref)))
                dqkv_ref[0, blk(i), sl] = dq.astype(BF16)
                dqkv_ref[0, blk(i), AT_W + HEAD_PAIR * hp: AT_W + HEAD_PAIR * (hp + 1)] = dk.astype(BF16)
                dqkv_ref[0, blk(i), 2 * AT_W + HEAD_PAIR * hp: 2 * AT_W + HEAD_PAIR * (hp + 1)] = dv.astype(BF16)

    cur = lambda j: pl.BlockSpec((1, rows, AT_W), lambda r, n: (r, n, j))
    prev = lambda j: pl.BlockSpec((1, ATT_BLK, AT_W), lambda r, n: (r, jnp.maximum(qb * n - 1, 0), j))
    nxt_blk = lambda n: jnp.minimum(qb * (n + 1), nb - 1)
    sq = (4 * qb, 2 * ATT_BLK, 2 * ATT_BLK)
    return pl.pallas_call(
        body, name=f"att_bwd_d{d}", grid=(d, nsteps),
        in_specs=[cur(0), pl.BlockSpec((1, ATT_BLK, AT_W), lambda r, n: (r, nxt_blk(n), 0)), prev(1), cur(1), prev(2), cur(2),
                  pl.BlockSpec((1, rows, AT_W), lambda r, n: (r, n, 0)),
                  pl.BlockSpec((1, ATT_BLK, AT_W), lambda r, n: (r, nxt_blk(n), 0)),
                  pl.BlockSpec((1, rows, STAT_W), lambda r, n: (r, n, 0)),
                  pl.BlockSpec((1, ATT_BLK, STAT_W), lambda r, n: (r, nxt_blk(n), 0)),
                  _full((2, ATT_BLK, 2 * ATT_BLK)), _full((2, ATT_BLK, 2 * ATT_BLK))],
        out_specs=pl.BlockSpec((1, rows, 3 * AT_W), lambda r, n: (r, n, 0)),
        out_shape=jax.ShapeDtypeStruct((d, L, 3 * AT_W), BF16),
        scratch_shapes=[pltpu.VMEM(sq, F32)] * 4 + [pltpu.VMEM(sq, BF16)] * 3,
        compiler_params=_params(("parallel", "parallel")),
    )(qkv_d, qkv_d, qkv_d, qkv_d, qkv_d, qkv_d, do_d, do_d, dl_d, dl_d, *_att_bias())


def _att_bwd_combine(dqkvs, cos2, sin2):
    S = dqkvs[0].shape[1]
    tm = 512

    def body(a_ref, b4_ref, c16_ref, cos_ref, sin_ref, dq_ref, dk_ref, dv_ref, b_scr, c_scr):
        _merge_residues(b4_ref, b_scr, 4)
        _merge_residues(c16_ref, c_scr, 16)
        t = a_ref[...].astype(F32) + _from_slabs(b_scr) + _from_slabs(c_scr)
        dy = t[:, : 2 * AT_W]
        cosf = jnp.tile(cos_ref[...], (1, 8))
        sinf = jnp.tile(sin_ref[...], (1, 8))
        dx = dy * cosf - _rope_rot(dy) * sinf
        dq_ref[...] = dx[:, :AT_W].astype(BF16)
        dk_ref[...] = dx[:, AT_W:].astype(BF16)
        dv_ref[...] = t[:, 2 * AT_W:].astype(BF16)

    row = lambda w: pl.BlockSpec((tm, w), lambda s: (s, 0))
    act = jax.ShapeDtypeStruct((S, AT_W), BF16)
    return pl.pallas_call(
        body, name="att_bwd_combine", grid=(S // tm,),
        in_specs=[row(3 * AT_W), _dil_spec(4, tm, 3 * AT_W), _dil_spec(16, tm, 3 * AT_W), row(128), row(128)],
        out_specs=[row(AT_W), row(AT_W), row(AT_W)],
        out_shape=[act, act, act],
        scratch_shapes=[_slab_scratch(tm, 3 * AT_W), _slab_scratch(tm, 3 * AT_W)],
        compiler_params=_params(("parallel",)),
    )(dqkvs[0].reshape(S, 3 * AT_W), dqkvs[1], dqkvs[2], cos2, sin2)


def _outproj(x2, tgt2, mix_hg, mix_at, mixt_hg, mixt_at, w_out_full, fnw):
    S = x2.shape[0]
    tm = 256
    ns = S // tm

    def body(x_ref, t_ref, mh_ref, ma_ref, mht_ref, mat_ref, w_ref, fw_ref,
             dh_ref, dmh_ref, dma_ref, gw_ref, gfw_ref, loss_ref):
        s = pl.program_id(0)

        @pl.when(s == 0)
        def _():
            gw_ref[...] = jnp.zeros_like(gw_ref)
            gfw_ref[...] = jnp.zeros_like(gfw_ref)
            loss_ref[...] = jnp.zeros_like(loss_ref)

        y = _dot(mh_ref[...], w_ref[:HG_W, :]) + _dot(ma_ref[...], w_ref[HG_W:, :])
        h = x_ref[...] + y
        r = lax.rsqrt(jnp.mean(h * h, axis=-1, keepdims=True) + EPS)
        hn = h * r
        fw = fw_ref[...]
        err = hn * fw - t_ref[...]
        loss_ref[...] += 0.5 * jnp.sum(jnp.mean(err * err, axis=-1, keepdims=True))
        dout = err * (1.0 / D_MODEL)
        gfw_ref[...] += jnp.sum(dout * hn, axis=0, keepdims=True)
        dhn = dout * fw
        dh = r * (dhn - hn * jnp.mean(dhn * hn, axis=-1, keepdims=True))
        dh_ref[...] = dh
        dhb = dh.astype(BF16)
        dmh_ref[...] = _dot_nt(dhb, w_ref[:HG_W, :])
        dma_ref[...] = _dot_nt(dhb, w_ref[HG_W:, :])
        gw_ref[:HG_W, :] += _dot(mht_ref[...], dhb)
        gw_ref[HG_W:, :] += _dot(mat_ref[...], dhb)

    row = lambda w: pl.BlockSpec((tm, w), lambda s: (s, 0))
    colb = pl.BlockSpec((HG_W, tm), lambda s: (0, s))
    return pl.pallas_call(
        body, name="outproj", grid=(ns,),
        in_specs=[row(D_MODEL), row(D_MODEL), row(HG_W), row(AT_W), colb, colb,
                  _full((D_MODEL, D_MODEL)), _full((1, D_MODEL))],
        out_specs=[row(D_MODEL), row(HG_W), row(AT_W), _full((D_MODEL, D_MODEL)), _full((1, D_MODEL)), _full((8, 128))],
        out_shape=[jax.ShapeDtypeStruct((S, D_MODEL), F32), jax.ShapeDtypeStruct((S, HG_W), F32),
                   jax.ShapeDtypeStruct((S, AT_W), F32), jax.ShapeDtypeStruct((D_MODEL, D_MODEL), F32),
                   jax.ShapeDtypeStruct((1, D_MODEL), F32), jax.ShapeDtypeStruct((8, 128), F32)],
        compiler_params=_params(("arbitrary",)),
    )(x2, tgt2, mix_hg, mix_at, mixt_hg, mixt_at, w_out_full, fnw)


def _inproj_bwd_x(dps, w_in_full, x2, norm_w, dh):
    S = x2.shape[0]
    tm = 256

    def body(d0, d1, d2, d3, d4, d5, d6, d7, w_ref, x_ref, nw_ref, dh_ref, gx_ref, gnw_ref):
        s = pl.program_id(0)

        @pl.when(s == 0)
        def _():
            gnw_ref[...] = jnp.zeros_like(gnw_ref)

        du = jnp.zeros((tm, D_MODEL), F32)
        for i, dref in enumerate((d0, d1, d2, d3, d4, d5, d6, d7)):
            j, half = divmod(i, 2)
            du = du + _dot_nt(dref[...], w_ref[j, :, 512 * half: 512 * (half + 1)])
        x = x_ref[...]
        r = lax.rsqrt(jnp.mean(x * x, axis=-1, keepdims=True) + EPS)
        xh = x * r
        gnw_ref[...] += jnp.sum(du * xh, axis=0, keepdims=True)
        dun = du * nw_ref[...]
        gx_ref[...] = dh_ref[...] + r * (dun - xh * jnp.mean(dun * xh, axis=-1, keepdims=True))

    row = lambda w: pl.BlockSpec((tm, w), lambda s: (s, 0))
    return pl.pallas_call(
        body, name="inproj_bwd_x", grid=(S // tm,),
        in_specs=[row(512)] * 8 + [_full((4, D_MODEL, 1024)), row(D_MODEL), _full((1, D_MODEL)), row(D_MODEL)],
        out_specs=[row(D_MODEL), _full((1, D_MODEL))],
        out_shape=[jax.ShapeDtypeStruct((S, D_MODEL), F32), jax.ShapeDtypeStruct((1, D_MODEL), F32)],
        compiler_params=_params(("arbitrary",)),
    )(*dps, w_in_full, x2, norm_w, dh)


def _inproj_bwd_w(ut, dps):
    S = ut.shape[1]
    tm = 256

    def body(ut_ref, d0, d1, d2, d3, d4, d5, d6, d7, g_ref):
        @pl.when(pl.program_id(0) == 0)
        def _():
            g_ref[...] = jnp.zeros_like(g_ref)

        utb = ut_ref[...]
        for i, dref in enumerate((d0, d1, d2, d3, d4, d5, d6, d7)):
            j, half = divmod(i, 2)
            g_ref[j, :, 512 * half: 512 * (half + 1)] += _dot(utb, dref[...])

    return pl.pallas_call(
        body, name="inproj_bwd_w", grid=(S // tm,),
        in_specs=[pl.BlockSpec((D_MODEL, tm), lambda s: (0, s))] + [pl.BlockSpec((tm, 512), lambda s: (s, 0))] * 8,
        out_specs=_full((4, D_MODEL, 1024)),
        out_shape=jax.ShapeDtypeStruct((4, D_MODEL, 1024), F32),
        compiler_params=_params(("arbitrary",)),
    )(ut, *dps)


def _adamw_update(gg, w_ref, m_ref, v_ref, d_ref, nm_ref, nv_ref):
    nm = ADAM_B1 * m_ref[...] + (1.0 - ADAM_B1) * gg
    nv = ADAM_B2 * v_ref[...] + (1.0 - ADAM_B2) * (gg * gg)
    m_hat = nm / (1.0 - ADAM_B1 ** ADAM_STEP)
    v_hat = nv / (1.0 - ADAM_B2 ** ADAM_STEP)
    d_ref[...] = -ADAM_LR * (m_hat / (jnp.sqrt(v_hat) + ADAM_EPS) + ADAM_WD * w_ref[...])
    nm_ref[...] = nm
    nv_ref[...] = nv


def _adamw(w, g, m, v, name):
    rows, cols = w.shape
    tr = min(rows, 256)

    def body(w_ref, g_ref, m_ref, v_ref, d_ref, nm_ref, nv_ref):
        _adamw_update(g_ref[...], w_ref, m_ref, v_ref, d_ref, nm_ref, nv_ref)

    spec = pl.BlockSpec((tr, cols), lambda i: (i, 0))
    sds = jax.ShapeDtypeStruct((rows, cols), F32)
    return pl.pallas_call(
        body, name=name, grid=(rows // tr,),
        in_specs=[spec] * 4, out_specs=[spec] * 3, out_shape=[sds] * 3,
        compiler_params=_params(("parallel",)),
    )(w, g, m, v)


def _place():
    return lax.axis_index("x"), lax.axis_index("y"), lax.axis_index("c")


def _gather_weights(w_in_s, w_out_s):
    def body(win_ref, wout_ref, fin_ref, fout_ref, send_sems, recv_sems):
        x, y, c = _place()
        me = (x, y, c)
        sib = (x, y, 1 - c)
        mine = 2 * x + y
        fin_ref[mine] = win_ref[...].astype(BF16)
        fout_ref[mine] = wout_ref[...].astype(BF16)
        chips = [(1 - x, y), (x, 1 - y), (1 - x, 1 - y)]

        def halves(chip, half):
            return (fin_ref.at[chip, pl.ds(half * 512, 512), :], fout_ref.at[chip, pl.ds(half * 128, 128), :])

        def copy(k, ref, to):
            return pltpu.make_async_remote_copy(src_ref=ref, dst_ref=ref, send_sem=send_sems.at[k],
                                                recv_sem=recv_sems.at[k], device_id=to, device_id_type=MESH)

        first, passed = [], []
        for j, (cx, cy) in enumerate(chips):
            for a, ref in enumerate(halves(mine, c)):
                first.append(copy(2 * j + a, ref, (cx, cy, c)))
        for cp in first:
            cp.start()
        for j, (cx, cy) in enumerate(chips):
            for a, ref in enumerate(halves(2 * cx + cy, c)):
                copy(2 * j + a, ref, me).wait_recv()
                fwd = copy(6 + 2 * j + a, ref, sib)
                fwd.start()
                passed.append(fwd)
        for j, (cx, cy) in enumerate(chips):
            for a, ref in enumerate(halves(2 * cx + cy, 1 - c)):
                copy(6 + 2 * j + a, ref, me).wait_recv()
        for cp in first + passed:
            cp.wait_send()

    vm = pl.BlockSpec(memory_space=pltpu.VMEM)
    return pl.pallas_call(
        body, name="gather_weights",
        in_specs=[vm, vm], out_specs=[vm, vm],
        out_shape=[jax.ShapeDtypeStruct((4, D_MODEL, 1024), BF16), jax.ShapeDtypeStruct((4, 256, D_MODEL), BF16)],
        scratch_shapes=[pltpu.SemaphoreType.DMA((12,)), pltpu.SemaphoreType.DMA((12,))],
        compiler_params=pltpu.CompilerParams(vmem_limit_bytes=VMEM_LIMIT),
    )(w_in_s, w_out_s)


def _swap_halves(g_in, g_out):
    def body(gin_ref, gout_ref, rin_ref, rout_ref, send_sems, recv_sems):
        x, y, c = _place()
        sib = (x, y, 1 - c)
        cps = [pltpu.make_async_remote_copy(src_ref=src.at[:, 1 - c], dst_ref=dst, send_sem=send_sems.at[k],
                                            recv_sem=recv_sems.at[k], device_id=sib, device_id_type=MESH)
               for k, (src, dst) in enumerate(((gin_ref, rin_ref), (gout_ref, rout_ref)))]
        for cp in cps:
            cp.start()
        for cp in cps:
            cp.wait()

    hbm = pl.BlockSpec(memory_space=pl.ANY)
    return pl.pallas_call(
        body, name="swap_halves",
        in_specs=[hbm, hbm], out_specs=[hbm, hbm],
        out_shape=[jax.ShapeDtypeStruct((4,) + g.shape[2:], F32) for g in (g_in, g_out)],
        scratch_shapes=[pltpu.SemaphoreType.DMA((2,)), pltpu.SemaphoreType.DMA((2,))],
    )(g_in, g_out)


def _add_half(g, r, cidx, name):
    n, _, rows, cols = g.shape

    def body(c_ref, g_ref, r_ref, o_ref):
        o_ref[0] = (g_ref[0, 0] + r_ref[0]).astype(BF16)

    return pl.pallas_call(
        body, name=name,
        grid_spec=pltpu.PrefetchScalarGridSpec(
            num_scalar_prefetch=1, grid=(n,),
            in_specs=[pl.BlockSpec((1, 1, rows, cols), lambda j, c_ref: (j, c_ref[0], 0, 0)),
                      pl.BlockSpec((1, rows, cols), lambda j, c_ref: (j, 0, 0))],
            out_specs=pl.BlockSpec((1, rows, cols), lambda j, c_ref: (j, 0, 0))),
        out_shape=jax.ShapeDtypeStruct((n, rows, cols), BF16),
        compiler_params=_params(("parallel",)),
    )(cidx, g, r)


def _exchange_copies(in_ref, out_ref, lin_ref, lout_ref, send_sems, recv_sems):
    x, y, c = _place()
    cps = []
    for k, (cx, cy) in enumerate([(1 - x, y), (x, 1 - y), (1 - x, 1 - y)]):
        for a, (src, dst) in enumerate(((in_ref, lin_ref), (out_ref, lout_ref))):
            cps.append(pltpu.make_async_remote_copy(
                src_ref=src.at[2 * cx + cy], dst_ref=dst.at[k], send_sem=send_sems.at[2 * k + a],
                recv_sem=recv_sems.at[2 * k + a], device_id=(cx, cy, c), device_id_type=MESH))
    return cps


def _exchange_start(cs_in, cs_out):
    def body(in_ref, out_ref, lin_ref, lout_ref, send_sems, recv_sems, in_thru, out_thru, lin_thru, lout_thru, token):
        for cp in _exchange_copies(in_ref, out_ref, lin_ref, lout_ref, send_sems, recv_sems):
            cp.start()
        token[...] = jnp.zeros_like(token)

    lands = [lax.empty((3,) + a.shape[1:], a.dtype) for a in (cs_in, cs_out)]
    bufs = [pltpu.with_memory_space_constraint(a, pltpu.HBM) for a in (cs_in, cs_out, *lands)]
    hbm = pl.BlockSpec(memory_space=pltpu.HBM)
    sem = pl.BlockSpec(memory_space=pltpu.SEMAPHORE)
    return pl.pallas_call(
        body, name="exchange_start",
        in_specs=[hbm] * 4,
        out_specs=[sem, sem, hbm, hbm, hbm, hbm, pl.BlockSpec(memory_space=pltpu.VMEM)],
        out_shape=[pltpu.SemaphoreType.DMA((6,)), pltpu.SemaphoreType.DMA((6,))]
                  + [pltpu.HBM(b.shape, b.dtype) for b in bufs] + [jax.ShapeDtypeStruct((8, 128), F32)],
        input_output_aliases={0: 2, 1: 3, 2: 4, 3: 5},
        compiler_params=pltpu.CompilerParams(has_side_effects=pltpu.SideEffectType.DATAFLOW_SIDE_EFFECTING),
    )(*bufs)


def _exchange_wait(send_sems, recv_sems, in_thru, out_thru, lin_thru, lout_thru, after):
    def body(in_ref, out_ref, lin_ref, lout_ref, send_sems, recv_sems, after_ref, in_dead, out_dead, got_in, got_out):
        for cp in _exchange_copies(in_ref, out_ref, lin_ref, lout_ref, send_sems, recv_sems):
            cp.wait_send()
            cp.wait_recv()

    hbm = pl.BlockSpec(memory_space=pltpu.HBM)
    sem = pl.BlockSpec(memory_space=pltpu.SEMAPHORE)
    bufs = (in_thru, out_thru, lin_thru, lout_thru)
    return pl.pallas_call(
        body, name="exchange_wait",
        in_specs=[hbm] * 4 + [sem, sem, pl.BlockSpec(memory_space=pl.ANY)],
        out_specs=[hbm] * 4,
        out_shape=[pltpu.HBM(b.shape, b.dtype) for b in bufs],
        input_output_aliases={0: 0, 1: 1, 2: 2, 3: 3},
        compiler_params=pltpu.CompilerParams(has_side_effects=pltpu.SideEffectType.DATAFLOW_SIDE_EFFECTING),
    )(*bufs, send_sems, recv_sems, after)


PEER_XOR = (2, 1, 3)


def _sum_chips(cs, r, chip_idx, name):
    _, rows, cols = r.shape
    tr = min(rows, 256)

    def body(m_ref, cs_ref, r_ref, o_ref):
        mine = m_ref[0]
        own = cs_ref[0].astype(F32)
        got = [r_ref[k].astype(F32) for k in range(3)]
        acc = None
        for s in range(4):
            rel = mine ^ s
            term = jnp.where(rel == 0, own, jnp.where(rel == PEER_XOR[0], got[0],
                                                      jnp.where(rel == PEER_XOR[1], got[1], got[2])))
            acc = term if acc is None else acc + term
        o_ref[...] = acc

    return pl.pallas_call(
        body, name=name,
        grid_spec=pltpu.PrefetchScalarGridSpec(
            num_scalar_prefetch=1, grid=(rows // tr,),
            in_specs=[pl.BlockSpec((1, tr, cols), lambda i, m_ref: (m_ref[0], i, 0)),
                      pl.BlockSpec((3, tr, cols), lambda i, m_ref: (0, i, 0))],
            out_specs=pl.BlockSpec((tr, cols), lambda i, m_ref: (i, 0))),
        out_shape=jax.ShapeDtypeStruct((rows, cols), F32),
        compiler_params=_params(("parallel",)),
    )(chip_idx, cs, r)


def _swap_reduced(h_in, h_out):
    def body(in_ref, out_ref, rin_ref, rout_ref, send_sems, recv_sems):
        x, y, c = _place()
        cps = [pltpu.make_async_remote_copy(src_ref=src, dst_ref=dst, send_sem=send_sems.at[k],
                                            recv_sem=recv_sems.at[k], device_id=(x, y, 1 - c), device_id_type=MESH)
               for k, (src, dst) in enumerate(((in_ref, rin_ref), (out_ref, rout_ref)))]
        for cp in cps:
            cp.start()
        for cp in cps:
            cp.wait()

    hbm = pl.BlockSpec(memory_space=pl.ANY)
    return pl.pallas_call(
        body, name="swap_reduced",
        in_specs=[hbm, hbm], out_specs=[hbm, hbm],
        out_shape=[jax.ShapeDtypeStruct(h.shape, F32) for h in (h_in, h_out)],
        scratch_shapes=[pltpu.SemaphoreType.DMA((2,)), pltpu.SemaphoreType.DMA((2,))],
    )(h_in, h_out)


def _adamw_halves(w, mine, theirs, m, v, cidx, name):
    rows, cols = w.shape
    half = rows // 2
    tr = min(half, 256)
    nbh = half // tr

    def body(c_ref, w_ref, a_ref, b_ref, m_ref, v_ref, g_ref, d_ref, nm_ref, nv_ref):
        gg = jnp.where(pl.program_id(0) // nbh == c_ref[0], a_ref[...], b_ref[...])
        g_ref[...] = gg
        _adamw_update(gg, w_ref, m_ref, v_ref, d_ref, nm_ref, nv_ref)

    spec = pl.BlockSpec((tr, cols), lambda i, c_ref: (i, 0))
    hspec = pl.BlockSpec((tr, cols), lambda i, c_ref: (i % nbh, 0))
    sds = jax.ShapeDtypeStruct((rows, cols), F32)
    return pl.pallas_call(
        body, name=name,
        grid_spec=pltpu.PrefetchScalarGridSpec(
            num_scalar_prefetch=1, grid=(rows // tr,),
            in_specs=[spec, hspec, hspec, spec, spec], out_specs=[spec] * 4),
        out_shape=[sds] * 4,
        compiler_params=_params(("parallel",)),
    )(cidx, w, mine, theirs, m, v)


def _allreduce_small(g_nw, g_fw, g_hgw, g_lbl, loss8):
    def body(nw_ref, fw_ref, hgw_ref, lbl_ref, loss_ref, out_ref, slots, send_sems, recv_sems):
        x, y, c = _place()
        me = 4 * x + 2 * y + c
        slots[me] = jnp.zeros((8, D_MODEL), F32)
        slots[me, 0:1, :] = nw_ref[...]
        slots[me, 1:2, :] = fw_ref[...]
        slots[me, 2:3, 0:HG_W] = hgw_ref[...]
        slots[me, 3:4, 0:HG_W] = lbl_ref[0:1, :]
        slots[me, 3:4, HG_W:] = lbl_ref[1:2, :]
        slots[me, 4:5, 0:128] = loss_ref[0:1, :]
        cps = []
        for k in range(1, 8):
            dx, dy, dc = (k >> 2) & 1, (k >> 1) & 1, k & 1
            to = (x ^ dx, y ^ dy, c ^ dc)
            cps.append(pltpu.make_async_remote_copy(
                src_ref=slots.at[me], dst_ref=slots.at[me], send_sem=send_sems.at[k - 1],
                recv_sem=recv_sems.at[k - 1], device_id=to, device_id_type=MESH))
        for cp in cps:
            cp.start()
        for cp in cps:
            cp.wait()
        acc = slots[0]
        for i in range(1, 8):
            acc = acc + slots[i]
        out_ref[...] = acc

    vm = pl.BlockSpec(memory_space=pltpu.VMEM)
    return pl.pallas_call(
        body, name="allreduce_small",
        in_specs=[vm] * 5, out_specs=vm,
        out_shape=jax.ShapeDtypeStruct((8, D_MODEL), F32),
        scratch_shapes=[pltpu.VMEM((8, 8, D_MODEL), F32), pltpu.SemaphoreType.DMA((7,)), pltpu.SemaphoreType.DMA((7,))],
    )(g_nw, g_fw, g_hgw, g_lbl, loss8)


def _rope_tables(S):
    inv_freq = (np.float32(1.0) / np.power(np.float32(ROPE_THETA), np.arange(ROPE_HALF, dtype=np.float32) / np.float32(ROPE_HALF))).astype(np.float32)
    ang = (np.arange(S, dtype=np.float32)[:, None] * inv_freq[None, :]).astype(np.float32)
    cos, sin = np.cos(ang).astype(np.float32), np.sin(ang).astype(np.float32)
    cos2 = np.concatenate([cos, cos, cos, cos], axis=-1)
    sin2 = np.concatenate([-sin, sin, -sin, sin], axis=-1)
    return jnp.asarray(cos2), jnp.asarray(sin2)


def _local_step(x2, tgt2, norm_w, w_in_full, lbl, hg_norm_w, w_out4, fnw):
    S = x2.shape[0]
    cos2, sin2 = _rope_tables(S)
    w_out_full = w_out4.reshape(D_MODEL, D_MODEL)
    p0, p1, qkv, qkv4, qkv16, p3, ut = _inproj(x2, norm_w, w_in_full, cos2, sin2)
    o_hg, mix_hg, mixt_hg, states = _hg_fwd(p0, p1, lbl, hg_norm_w)
    qkv_ds = [qkv.reshape(1, S, 3 * AT_W), qkv4, qkv16]
    os_d, ls_d = zip(*[_att_fwd(q) for q in qkv_ds])
    o_at, lse, mix_at, mixt_at = _att_combine(os_d, ls_d, p3)
    dh, dm_hg, dm_at, g_wout, g_fw, loss8 = _outproj(x2, tgt2, mix_hg, mix_at, mixt_hg, mixt_at, w_out_full, fnw)
    dqr, dfl, dv_hg, dz_hg, g_lbl, g_hgw = _hg_bwd(p0, p1, o_hg, dm_hg, states, lbl, hg_norm_w)
    do1, do4, do16, dl1, dl4, dl16, dz_at = _att_gate_bwd(dm_at, o_at, lse, p3)
    do_ds = [do1.reshape(1, S, AT_W), do4, do16]
    dl_ds = [dl1.reshape(1, S, STAT_W), dl4, dl16]
    dqkvs = [_att_bwd(q, a, b) for q, a, b in zip(qkv_ds, do_ds, dl_ds)]
    dq_at, dk_at, dv_at = _att_bwd_combine(dqkvs, cos2, sin2)
    dps = [dqr, dfl, dv_hg, dz_hg, dq_at, dk_at, dv_at, dz_at]
    return loss8, dps, ut, dh, g_lbl, g_hgw, g_wout, g_fw


def kernel(x, norm_w, w_in, hgrn_lb_logits, hg_norm_w, w_out, final_norm_w, loss_target, m_norm_w, m_w_in, m_hgrn_lb_logits, m_hg_norm_w, m_w_out, m_final_norm_w, v_norm_w, v_w_in, v_hgrn_lb_logits, v_hg_norm_w, v_w_out, v_final_norm_w):
    S = x.shape[1]
    w_in_full, w_out_full = _gather_weights(w_in[0], w_out[0])
    loss8, dps, ut, dh, g_lbl, g_hgw, g_wout, g_fw = _local_step(
        x[0], loss_target[0], norm_w, w_in_full, hgrn_lb_logits, hg_norm_w,
        w_out_full, final_norm_w.reshape(1, D_MODEL))

    cidx = lax.axis_index("c").astype(jnp.int32).reshape(1)
    g_win = _inproj_bwd_w(ut, dps)
    g_in4 = g_win.reshape(4, 2, 512, 1024)
    g_out4 = g_wout.reshape(4, 2, 128, D_MODEL)
    r_in, r_out = _swap_halves(g_in4, g_out4)
    cs_in = _add_half(g_in4, r_in, cidx, "add_half_in")
    cs_out = _add_half(g_out4, r_out, cidx, "add_half_out")
    *started, token = _exchange_start(cs_in, cs_out)
    grad_x, g_nw = _inproj_bwd_x(dps, w_in_full, x[0], norm_w + token[0:1, 0:1], dh)
    cs_in, cs_out, x_in, x_out = _exchange_wait(*started, g_nw)
    chip_idx = (2 * lax.axis_index("x") + lax.axis_index("y")).astype(jnp.int32).reshape(1)
    h_in = _sum_chips(cs_in, x_in, chip_idx, "sum_chips_in")
    h_out = _sum_chips(cs_out, x_out, chip_idx, "sum_chips_out")
    s_in, s_out = _swap_reduced(h_in, h_out)

    red = _allreduce_small(g_nw, g_fw, g_hgw, g_lbl, loss8)
    loss = red[4, 0]
    grad_norm_w = red[0:1, :]
    grad_final_norm_w = red[1, :]
    grad_hg_norm_w = red[2:3, :HG_W]
    grad_lbl = jnp.concatenate([red[3:4, :HG_W], red[3:4, HG_W:]], axis=0)

    d_nw, m_nw, v_nw = _adamw(norm_w, grad_norm_w, m_norm_w, v_norm_w, "adamw_norm_w")
    grad_w_in, d_win, m_win, v_win = _adamw_halves(w_in[0], h_in, s_in, m_w_in[0], v_w_in[0], cidx, "adamw_w_in")
    d_lbl, m_lbl, v_lbl = _adamw(hgrn_lb_logits, grad_lbl, m_hgrn_lb_logits, v_hgrn_lb_logits, "adamw_lb_logits")
    d_hgw, m_hgw, v_hgw = _adamw(hg_norm_w, grad_hg_norm_w, m_hg_norm_w, v_hg_norm_w, "adamw_hg_norm_w")
    grad_w_out, d_wout, m_wout, v_wout = _adamw_halves(w_out[0], h_out, s_out, m_w_out[0], v_w_out[0], cidx, "adamw_w_out")
    d_fw, m_fw, v_fw = _adamw(final_norm_w.reshape(1, D_MODEL), grad_final_norm_w.reshape(1, D_MODEL),
                              m_final_norm_w.reshape(1, D_MODEL), v_final_norm_w.reshape(1, D_MODEL), "adamw_final_norm_w")
    e1 = lambda a: a[None]
    flat = lambda a: a.reshape(D_MODEL)
    return (loss, grad_x[None], grad_norm_w, e1(grad_w_in), grad_lbl, grad_hg_norm_w, e1(grad_w_out), grad_final_norm_w,
            d_nw, e1(d_win), d_lbl, d_hgw, e1(d_wout), flat(d_fw),
            m_nw, e1(m_win), m_lbl, m_hgw, e1(m_wout), flat(m_fw),
            v_nw, e1(v_win), v_lbl, v_hgw, e1(v_wout), flat(v_fw))
```

```python
import jax
import jax.numpy as jnp
import numpy as np
from jax import lax
from jax.experimental import pallas as pl
from jax.experimental.pallas import tpu as pltpu

F32 = jnp.float32
BF16 = jnp.bfloat16
MESH = pl.DeviceIdType.MESH

D_MODEL = 1024
HG_W = 512
AT_W = 512
HEAD_PAIR = 128
ROPE_HALF = 32
ROPE_THETA = 10000.0
EPS = 1e-6
CHUNK = 128
LEVELS = (64, 32, 16, 8, 4, 2, 1)
DIAG = 1
SUBLANES = 8
ATT_BLK = 128
ATT_QB = 4
DILATIONS = (1, 4, 16)
ATT_SCALE = 0.125
STAT_W = 128
STAT_LANES = 16
STAT_LSE_LANE = 8
NEG = -1e30
VMEM_LIMIT = 56 * 1024 * 1024

ADAM_LR = 0.001
ADAM_B1 = 0.9
ADAM_B2 = 0.999
ADAM_EPS = 1e-08
ADAM_WD = 0.01
ADAM_STEP = 10


def _iota(shape, dim):
    return lax.broadcasted_iota(jnp.int32, shape, dim)


def _dot(a, b):
    return jnp.dot(a, b, preferred_element_type=F32)


def _dot_nt(a, b):
    return lax.dot_general(a, b, (((1,), (1,)), ((), ())), preferred_element_type=F32)


def _dot_tn(a, b):
    return lax.dot_general(a, b, (((0,), (0,)), ((), ())), preferred_element_type=F32)


def _sigmoid(v):
    return 0.5 * jnp.tanh(0.5 * v) + 0.5


def _params(sem=None, vmem=VMEM_LIMIT):
    return pltpu.CompilerParams(dimension_semantics=sem, vmem_limit_bytes=vmem)


def _full(shape):
    n = len(shape)
    return pl.BlockSpec(shape, lambda *_: (0,) * n)


def _rope_rot(y):
    n = y.shape[1]
    first = (_iota(y.shape, 1) & (2 * ROPE_HALF - 1)) < ROPE_HALF
    return jnp.where(first, pltpu.roll(y, n - ROPE_HALF, 1), pltpu.roll(y, ROPE_HALF, 1))


def _dil_spec(d, tm, width):
    return pl.BlockSpec((d, tm // d, width), lambda s: (0, s, 0))


LANES = 128


def _slab_scratch(tm, width):
    return pltpu.VMEM((width // LANES, tm, LANES), F32)


def _to_slabs(v, slabs_ref):
    for j in range(slabs_ref.shape[0]):
        slabs_ref[j] = v[:, LANES * j: LANES * (j + 1)]


def _from_slabs(slabs_ref):
    return jnp.concatenate([slabs_ref[j] for j in range(slabs_ref.shape[0])], axis=1)


def _split_residues(slabs_ref, dst_ref, d, dtype):
    nslab, tm, _ = slabs_ref.shape
    for r in range(d):
        for j in range(nslab):
            dst_ref[r, :, LANES * j: LANES * (j + 1)] = slabs_ref[j, pl.ds(r, tm // d, stride=d), :].astype(dtype)


def _merge_residues(src_ref, slabs_ref, d):
    nslab, tm, _ = slabs_ref.shape
    for r in range(d):
        for j in range(nslab):
            slabs_ref[j, pl.ds(r, tm // d, stride=d), :] = src_ref[r, :, LANES * j: LANES * (j + 1)].astype(F32)


def _inproj(x2, norm_w, w_in_full, cos2, sin2):
    S = x2.shape[0]
    tm = 256

    def body(x_ref, nw_ref, w_ref, cos_ref, sin_ref, p0_ref, p1_ref, qkv_ref, qkv4_ref, qkv16_ref, p3_ref, u_ref, scr):
        x = x_ref[...]
        r = lax.rsqrt(jnp.mean(x * x, axis=-1, keepdims=True) + EPS)
        u = x * r * nw_ref[...]
        ub = u.astype(BF16)
        u_ref[...] = ub
        p0_ref[...] = _dot(ub, w_ref[0])
        p1_ref[...] = _dot(ub, w_ref[1])
        y2 = _dot(ub, w_ref[2])
        cosf = jnp.tile(cos_ref[...], (1, 8))
        sinf = jnp.tile(sin_ref[...], (1, 8))
        y3 = _dot(ub, w_ref[3])
        p3_ref[...] = y3
        qkv = jnp.concatenate([y2 * cosf + _rope_rot(y2) * sinf, y3[:, :AT_W]], axis=1)
        qkv_ref[...] = qkv.astype(BF16)
        _to_slabs(qkv, scr)
        _split_residues(scr, qkv4_ref, 4, BF16)
        _split_residues(scr, qkv16_ref, 16, BF16)

    row = lambda w: pl.BlockSpec((tm, w), lambda s: (s, 0))
    qkv_w = 3 * AT_W
    return pl.pallas_call(
        body, name="inproj", grid=(S // tm,),
        in_specs=[row(D_MODEL), _full((1, D_MODEL)), _full((4, D_MODEL, 1024)), row(128), row(128)],
        out_specs=[row(1024), row(1024), row(qkv_w), _dil_spec(4, tm, qkv_w), _dil_spec(16, tm, qkv_w), row(1024),
                   row(D_MODEL)],
        out_shape=[jax.ShapeDtypeStruct((S, 1024), F32), jax.ShapeDtypeStruct((S, 1024), F32),
                   jax.ShapeDtypeStruct((S, qkv_w), BF16), jax.ShapeDtypeStruct((4, S // 4, qkv_w), BF16),
                   jax.ShapeDtypeStruct((16, S // 16, qkv_w), BF16), jax.ShapeDtypeStruct((S, 1024), F32),
                   jax.ShapeDtypeStruct((S, D_MODEL), BF16)],
        scratch_shapes=[_slab_scratch(tm, qkv_w)],
        compiler_params=_params(("parallel",)),
    )(x2, norm_w, w_in_full, cos2, sin2)


HG_HPS = 4
N_LEV = len(LEVELS)


def _hg_const_arrays():
    r = np.arange(CHUNK)[:, None]
    c = np.arange(CHUNK)[None, :]
    tris = np.stack([r >= c, r <= c])
    lm = [((r // (2 * m)) == (c // (2 * m))) & (r % (2 * m) >= m) & (c % (2 * m) < m) for m in LEVELS]
    dm = [(c == r - dl) & (r % DIAG >= dl) for dl in range(DIAG)]
    masks = np.stack(lm + dm)
    return jnp.asarray(tris, BF16), jnp.asarray(masks, F32)


def _split2(a):
    hi = a.astype(BF16)
    return hi, (a - hi.astype(F32)).astype(BF16)


def _dot3(a, b, dot=_dot):
    ah, al = _split2(a)
    bh, bl = _split2(b)
    n = b.shape[1]
    p = dot(ah, jnp.concatenate([bh, bl], axis=1))
    return (p[:, :n] + p[:, n:]) + dot(al, bh)


def _split3(a):
    a1 = a.astype(BF16)
    r1 = a - a1.astype(F32)
    a2 = r1.astype(BF16)
    return a1, a2, (r1 - a2.astype(F32)).astype(BF16)


def _tri_dot(tri, a):
    n = a.shape[1]
    p = _dot(tri, jnp.concatenate(_split3(a), axis=1))
    return (p[:, :n] + p[:, n:2 * n]) + p[:, 2 * n:]


def _dot_sel(a, sel):
    a1, a2, a3 = _split3(a)
    return (_dot(a1, sel) + _dot(a2, sel)) + _dot(a3, sel)


def _rowsum(t):
    return _dot(t.astype(BF16), jnp.ones((t.shape[1], t.shape[1]), BF16))


def _level_refs(b):
    refs = []
    pos = _iota(b.shape, 0)
    for m in LEVELS:
        if 2 * m >= SUBLANES:
            parts = [jnp.broadcast_to(b[r0 + m - 1: r0 + m, :], (2 * m, b.shape[1])) for r0 in range(0, CHUNK, 2 * m)]
            refs.append(parts[0] if len(parts) == 1 else jnp.concatenate(parts, axis=0))
        else:
            p = pos & (2 * m - 1)
            ref = b
            for off in range(-(m - 1), m + 1):
                if off != 0:
                    ref = jnp.where(p == m - 1 + off, pltpu.roll(b, off % CHUNK, 0), ref)
            refs.append(ref)
    return refs


def _hg_lb(lbl_ref):
    l0 = lbl_ref[0:1, :]
    l1 = lbl_ref[1:2, :]
    mx = jnp.maximum(l0, l1)
    e0 = jnp.exp(l0 - mx)
    e1 = jnp.exp(l1 - mx)
    p0 = e0 / (e0 + e1)
    lb = jnp.clip(p0, 1e-6, 1.0 - 1e-6)
    inside = (p0 >= 1e-6) & (p0 <= 1.0 - 1e-6)
    dlb_dl0 = jnp.where(inside, p0 * (e1 / (e0 + e1)), 0.0)
    return lb, dlb_dl0


def _hg_gates(qr, fl, lb):
    sig = _sigmoid(fl)
    f = lb + (1.0 - lb) * sig
    g = jnp.log(f)
    k = (1.0 - lb) * (1.0 - sig)
    sq = _sigmoid(qr)
    q = qr * sq
    return sig, f, g, k, sq, q


def _hg_levels(q, k, b, mk_ref):
    refs = _level_refs(b)
    a = jnp.zeros((CHUNK, CHUNK), F32)
    es, qts, kts = [], [], []
    for i in range(N_LEV):
        diff = b - refs[i]
        e = jnp.exp(jnp.minimum(diff, -diff))
        qt = (q * e).astype(BF16)
        kt = (k * e).astype(BF16)
        a = a + _dot_nt(qt, kt) * mk_ref[i]
        es.append(e); qts.append(qt); kts.append(kt)
    return a, es, qts, kts


def _hg_specs(nc, rev):
    cc = (lambda c: nc - 1 - c) if rev else (lambda c: c)
    w = 128 * HG_HPS
    blk = lambda off: pl.BlockSpec((CHUNK, w), lambda h, c: (cc(c), h + off))
    vec = pl.BlockSpec((1, w), lambda h, c: (0, h))
    lb2 = pl.BlockSpec((2, w), lambda h, c: (0, h))
    st = pl.BlockSpec((1, HG_HPS, 128, 128), lambda h, c: (cc(c), h, 0, 0))
    consts = [_full((2, CHUNK, CHUNK)), _full((N_LEV + DIAG, CHUNK, CHUNK))]
    return blk, vec, lb2, st, consts


def _hg_fwd(p0, p1, lbl, gw):
    S = p0.shape[0]
    nc = S // CHUNK
    ng = 4 // HG_HPS

    def body(qr_ref, fl_ref, v_ref, z_ref, lbl_ref, gw_ref, tri_ref, mk_ref,
             o_ref, mix_ref, st_ref, state):
        c = pl.program_id(1)

        @pl.when(c == 0)
        def _():
            state[...] = jnp.zeros_like(state)

        lb_all, _ = _hg_lb(lbl_ref)
        heads = [slice(128 * hh, 128 * (hh + 1)) for hh in range(HG_HPS)]
        qs, ks, bs, mats = [], [], [], []
        for sl in heads:
            _, _, g, k, _, q = _hg_gates(qr_ref[:, sl], fl_ref[:, sl], lb_all[:, sl])
            qs.append(q); ks.append(k); bs.append(_tri_dot(tri_ref[0], g))
        for hh in range(HG_HPS):
            a, _, _, _ = _hg_levels(qs[hh], ks[hh], bs[hh], mk_ref)
            mats.append(a + _rowsum(qs[hh] * ks[hh]) * mk_ref[N_LEV])
        for hh, sl in enumerate(heads):
            q, k, b, a, v, z = qs[hh], ks[hh], bs[hh], mats[hh], v_ref[:, sl], z_ref[:, sl]
            b_last = b[CHUNK - 1: CHUNK, :]
            st = state[hh]
            st_ref[0, hh] = st
            o = _dot_nt((q * jnp.exp(b)).astype(BF16), st.astype(BF16)) + _dot(a.astype(BF16), v.astype(BF16))
            state[hh] = st * jnp.exp(b_last) + _dot3(v, k * jnp.exp(b_last - b), _dot_tn)
            o_ref[:, sl] = o
            rs = lax.rsqrt(jnp.mean(o * o, axis=-1, keepdims=True) + EPS)
            mixed = o * rs * gw_ref[:, sl] * (z * _sigmoid(z))
            mix_ref[:, sl] = mixed.astype(BF16)

    blk, vec, lb2, st_spec, consts = _hg_specs(nc, False)
    tris, masks = _hg_const_arrays()
    return pl.pallas_call(
        body, name="hg_fwd", grid=(ng, nc),
        in_specs=[blk(0), blk(ng), blk(0), blk(ng), lb2, vec] + consts,
        out_specs=[blk(0), blk(0), st_spec],
        out_shape=[jax.ShapeDtypeStruct((S, HG_W), F32), jax.ShapeDtypeStruct((S, HG_W), BF16),
                   jax.ShapeDtypeStruct((nc, 4, 128, 128), F32)],
        scratch_shapes=[pltpu.VMEM((HG_HPS, 128, 128), F32)],
        compiler_params=_params(("parallel", "arbitrary")),
    )(p0, p0, p1, p1, lbl, gw, tris, masks)


def _hg_bwd(p0, p1, o_raw, dm, states, lbl, gw):
    S = p0.shape[0]
    nc = S // CHUNK
    ng = 4 // HG_HPS
    w = 128 * HG_HPS

    def body(qr_ref, fl_ref, v_ref, z_ref, o_ref, dm_ref, st_ref, lbl_ref, gw_ref, tri_ref, mk_ref, mkb_ref,
             dqr_ref, dfl_ref, dv_ref, dz_ref, glbl_ref, ggw_ref, dstate, carry, acc_lb, acc_gw):
        c = pl.program_id(1)

        @pl.when(c == 0)
        def _():
            dstate[...] = jnp.zeros_like(dstate)
            carry[...] = jnp.zeros_like(carry)
            acc_lb[...] = jnp.zeros_like(acc_lb)
            acc_gw[...] = jnp.zeros_like(acc_gw)

        lb_all, dlb_dl0 = _hg_lb(lbl_ref)
        heads = [slice(128 * hh, 128 * (hh + 1)) for hh in range(HG_HPS)]
        diag_mask = mk_ref[N_LEV]
        gates, bs, dos = [], [], []
        for sl in heads:
            gt = _hg_gates(qr_ref[:, sl], fl_ref[:, sl], lb_all[:, sl])
            gates.append(gt)
            bs.append(_tri_dot(tri_ref[0], gt[2]))
        for hh, sl in enumerate(heads):
            o, z, dmix, gwv = o_ref[:, sl], z_ref[:, sl], dm_ref[:, sl], gw_ref[:, sl]
            rs = lax.rsqrt(jnp.mean(o * o, axis=-1, keepdims=True) + EPS)
            oh = o * rs
            sz = _sigmoid(z)
            dz_ref[:, sl] = (dmix * (oh * gwv) * (sz * (1.0 + z * (1.0 - sz)))).astype(BF16)
            don = dmix * (z * sz)
            acc_gw[0:1, sl] += jnp.sum(don * oh, axis=0, keepdims=True)
            dy = don * gwv
            dos.append(rs * (dy - oh * jnp.mean(dy * oh, axis=-1, keepdims=True)))
        inter = []
        for hh, sl in enumerate(heads):
            _, _, _, k, _, q = gates[hh]
            b, do, v = bs[hh], dos[hh], v_ref[:, sl]
            b_last = b[CHUNK - 1: CHUNK, :]
            eb = jnp.exp(b)
            edec = jnp.exp(b_last - b)
            dst = dstate[hh]
            dq = _dot3(do, st_ref[0, hh]) * eb
            dk = _dot3(v, dst) * edec
            da = _dot_nt(do.astype(BF16), v.astype(BF16))
            dv_state = _dot_nt((k * edec).astype(BF16), dst.astype(BF16))
            dstate[hh] = dst * jnp.exp(b_last) + _dot3(do, q * eb, _dot_tn)
            inter.append((dq, dk, da, dv_state))
        for hh, sl in enumerate(heads):
            sig, f, _, k, sq, q = gates[hh]
            dq, dk, da, dv_state = inter[hh]
            b = bs[hh]
            db = q * dq - k * dk
            dab = da.astype(BF16)
            refs = _level_refs(b)
            a = _rowsum(q * k) * diag_mask
            for i in range(N_LEV):
                diff = b - refs[i]
                e = jnp.exp(jnp.minimum(diff, -diff))
                qt = (q * e).astype(BF16)
                kt = (k * e).astype(BF16)
                a = a + _dot_nt(qt, kt) * mk_ref[i]
                dam = dab * mkb_ref[i]
                gq = _dot(dam, kt)
                gk = _dot_tn(dam, qt)
                dq = dq + e * gq
                dk = dk + e * gk
                db = db + (qt.astype(F32) * gq - kt.astype(F32) * gk)
            dc = _rowsum(da * diag_mask)
            dq = dq + dc * k
            dk = dk + dc * q
            dv_ref[:, sl] = (_dot_tn(a.astype(BF16), dos[hh].astype(BF16)) + dv_state).astype(BF16)
            dg = _tri_dot(tri_ref[1], db) + carry[0:1, sl]
            carry[0:1, sl] += jnp.sum(db, axis=0, keepdims=True)
            lb = lb_all[:, sl]
            qr = qr_ref[:, sl]
            t = dg / f - dk
            dfl_ref[:, sl] = (t * (1.0 - lb) * sig * (1.0 - sig)).astype(BF16)
            acc_lb[0:1, sl] += jnp.sum(t * (1.0 - sig), axis=0, keepdims=True)
            dqr_ref[:, sl] = (dq * (sq * (1.0 + qr * (1.0 - sq)))).astype(BF16)

        @pl.when(c == nc - 1)
        def _():
            gl0 = acc_lb[0:1, :] * dlb_dl0
            glbl_ref[0:1, :] = gl0
            glbl_ref[1:2, :] = -gl0
            ggw_ref[...] = acc_gw[0:1, :]

    blk, vec, lb2, st_spec, consts = _hg_specs(nc, True)
    tris, masks = _hg_const_arrays()
    act = jax.ShapeDtypeStruct((S, HG_W), BF16)
    return pl.pallas_call(
        body, name="hg_bwd", grid=(ng, nc),
        in_specs=[blk(0), blk(ng), blk(0), blk(ng), blk(0), blk(0), st_spec, lb2, vec] + consts + consts[1:],
        out_specs=[blk(0), blk(0), blk(0), blk(0), lb2, vec],
        out_shape=[act, act, act, act, jax.ShapeDtypeStruct((2, HG_W), F32), jax.ShapeDtypeStruct((1, HG_W), F32)],
        scratch_shapes=[pltpu.VMEM((HG_HPS, 128, 128), F32), pltpu.VMEM((8, w), F32),
                        pltpu.VMEM((8, w), F32), pltpu.VMEM((8, w), F32)],
        compiler_params=_params(("parallel", "arbitrary")),
    )(p0, p0, p1, p1, o_raw, dm, states, lbl, gw, tris, masks, masks.astype(BF16))


def _stat_head(lane):
    return lane >> 4


def _stat_lane(lane):
    return lane & (STAT_LANES - 1)


def _o_head(lane):
    return lane >> 6


def _att_bias():
    qi = np.arange(ATT_BLK)[:, None]
    kj = np.arange(2 * ATT_BLK)[None, :]
    band = (kj >= qi) & (kj <= qi + ATT_BLK)
    qm = np.stack([band & (kj >= ATT_BLK), band])
    cur = (kj < ATT_BLK) & (qi <= kj)
    km = np.stack([cur, cur | ((kj >= ATT_BLK) & (qi >= kj - ATT_BLK))])
    to_bias = lambda m: jnp.asarray(np.where(m, 0.0, NEG), F32)
    return to_bias(qm), to_bias(km)


def _att_fwd(qkv_d):
    d, L, _ = qkv_d.shape
    qb = ATT_QB
    rows = qb * ATT_BLK

    def body(q_ref, kp_ref, kc_ref, vp_ref, vc_ref, bias_ref, o_ref, lse_ref, s_scr, p_scr):
        first = _iota((ATT_BLK, HEAD_PAIR), 1) < 64
        head_of_lane = _stat_head(_iota((ATT_BLK, STAT_W), 1))
        pairs = [slice(HEAD_PAIR * hp, HEAD_PAIR * (hp + 1)) for hp in range(4)]
        blk = lambda i: slice(ATT_BLK * i, ATT_BLK * (i + 1))

        def keys(i, sl, prev_ref, cur_ref):
            before = prev_ref[0, :, sl] if i == 0 else cur_ref[0, blk(i - 1), sl]
            return jnp.concatenate([before, cur_ref[0, blk(i), sl]], axis=0)

        for i in range(qb):
            for hp, sl in enumerate(pairs):
                q2 = q_ref[0, blk(i), sl] * ATT_SCALE
                zero = jnp.zeros_like(q2)
                qs = jnp.concatenate([jnp.where(first, q2, zero), jnp.where(first, zero, q2)], axis=0)
                s_scr[4 * i + hp] = _dot_nt(qs, keys(i, sl, kp_ref, kc_ref))
        stats = []
        for i in range(qb):
            bias = bias_ref[jnp.minimum(pl.program_id(1), 1)] if i == 0 else bias_ref[1]
            bias2 = jnp.concatenate([bias, bias], axis=0)
            for hp in range(4):
                s = s_scr[4 * i + hp] + bias2
                m = jnp.max(s, axis=-1, keepdims=True)
                p = jnp.exp(s - m)
                l = jnp.sum(p, axis=-1, keepdims=True)
                p_scr[4 * i + hp] = p.astype(BF16)
                stats.append((l, m + jnp.log(l)))
        for i in range(qb):
            lse_blk = jnp.zeros((ATT_BLK, STAT_W), F32)
            for hp, sl in enumerate(pairs):
                l, lse = stats[4 * i + hp]
                o = _dot(p_scr[4 * i + hp], keys(i, sl, vp_ref, vc_ref)) / l
                o_ref[0, blk(i), sl] = jnp.where(first, o[:ATT_BLK], o[ATT_BLK:]).astype(BF16)
                lse_blk = jnp.where(head_of_lane == 2 * hp, lse[:ATT_BLK],
                                    jnp.where(head_of_lane == 2 * hp + 1, lse[ATT_BLK:], lse_blk))
            lse_ref[0, blk(i), :] = lse_blk

    cur = lambda j: pl.BlockSpec((1, rows, AT_W), lambda r, n: (r, n, j))
    prev = lambda j: pl.BlockSpec((1, ATT_BLK, AT_W), lambda r, n: (r, jnp.maximum(qb * n - 1, 0), j))
    return pl.pallas_call(
        body, name=f"att_fwd_d{d}", grid=(d, L // rows),
        in_specs=[cur(0), prev(1), cur(1), prev(2), cur(2), _full((2, ATT_BLK, 2 * ATT_BLK))],
        out_specs=[pl.BlockSpec((1, rows, AT_W), lambda r, n: (r, n, 0)), pl.BlockSpec((1, rows, STAT_W), lambda r, n: (r, n, 0))],
        out_shape=[jax.ShapeDtypeStruct((d, L, AT_W), BF16), jax.ShapeDtypeStruct((d, L, STAT_W), F32)],
        scratch_shapes=[pltpu.VMEM((4 * qb, 2 * ATT_BLK, 2 * ATT_BLK), F32), pltpu.VMEM((4 * qb, 2 * ATT_BLK, 2 * ATT_BLK), BF16)],
        compiler_params=_params(("parallel", "parallel")),
    )(qkv_d, qkv_d, qkv_d, qkv_d, qkv_d, _att_bias()[0])


def _att_combine(os_d, ls_d, p3):
    S = p3.shape[0]
    tm = 512

    def body(oa_ref, ob4_ref, oc16_ref, la_ref, lb4_ref, lc16_ref, z_ref,
             oat_ref, lse_ref, mix_ref, ob_scr, oc_scr, lb_scr, lc_scr):
        _merge_residues(ob4_ref, ob_scr, 4)
        _merge_residues(oc16_ref, oc_scr, 16)
        _merge_residues(lb4_ref, lb_scr, 4)
        _merge_residues(lc16_ref, lc_scr, 16)
        ls = (la_ref[...], lb_scr[0], lc_scr[0])
        mx = jnp.maximum(jnp.maximum(ls[0], ls[1]), ls[2])
        es = [jnp.exp(l - mx) for l in ls]
        zs = es[0] + es[1] + es[2]
        lse_ref[...] = mx + jnp.log(zs)
        spread = ((_o_head(_iota((STAT_W, AT_W), 1)) == _stat_head(_iota((STAT_W, AT_W), 0)))
                  & (_stat_lane(_iota((STAT_W, AT_W), 0)) == 0)).astype(BF16)
        os_ = (oa_ref[...].astype(F32), _from_slabs(ob_scr), _from_slabs(oc_scr))
        o = jnp.zeros((tm, AT_W), F32)
        for e, oi in zip(es, os_):
            hi, lo = _split2(e / zs)
            o = o + (_dot(hi, spread) + _dot(lo, spread)) * oi
        oat_ref[...] = o
        z = z_ref[...]
        mixed = o * (z * _sigmoid(z))
        mix_ref[...] = mixed.astype(BF16)

    row = lambda w: pl.BlockSpec((tm, w), lambda s: (s, 0))
    return pl.pallas_call(
        body, name="att_combine", grid=(S // tm,),
        in_specs=[row(AT_W), _dil_spec(4, tm, AT_W), _dil_spec(16, tm, AT_W),
                  row(STAT_W), _dil_spec(4, tm, STAT_W), _dil_spec(16, tm, STAT_W),
                  pl.BlockSpec((tm, AT_W), lambda s: (s, 1))],
        out_specs=[row(AT_W), row(STAT_W), row(AT_W)],
        out_shape=[jax.ShapeDtypeStruct((S, AT_W), F32), jax.ShapeDtypeStruct((S, STAT_W), F32),
                   jax.ShapeDtypeStruct((S, AT_W), BF16)],
        scratch_shapes=[_slab_scratch(tm, AT_W), _slab_scratch(tm, AT_W), _slab_scratch(tm, STAT_W), _slab_scratch(tm, STAT_W)],
        compiler_params=_params(("parallel",)),
    )(os_d[0].reshape(S, AT_W), os_d[1], os_d[2], ls_d[0].reshape(S, STAT_W), ls_d[1], ls_d[2], p3)


def _att_gate_bwd(dm_at, o_at, lse, p3):
    S = p3.shape[0]
    tm = 512

    def body(dm_ref, o_ref, l_ref, z_ref, do_ref, do4_ref, do16_ref, dl_ref, dl4_ref, dl16_ref, dz_ref, do_scr, dl_scr):
        o = o_ref[...]
        z = z_ref[...]
        dm = dm_ref[...]
        sz = _sigmoid(z)
        dz_ref[...] = (dm * o * (sz * (1.0 + z * (1.0 - sz)))).astype(BF16)
        do = dm * (z * sz)
        do_ref[...] = do.astype(BF16)
        gather = (_o_head(_iota((AT_W, STAT_W), 0)) == _stat_head(_iota((AT_W, STAT_W), 1))).astype(BF16)
        dl = jnp.where(_stat_lane(_iota((tm, STAT_W), 1)) < STAT_LSE_LANE, _dot_sel(do * o, gather), l_ref[...])
        dl_ref[...] = dl
        _to_slabs(do, do_scr)
        _to_slabs(dl, dl_scr)
        _split_residues(do_scr, do4_ref, 4, BF16)
        _split_residues(do_scr, do16_ref, 16, BF16)
        _split_residues(dl_scr, dl4_ref, 4, F32)
        _split_residues(dl_scr, dl16_ref, 16, F32)

    row = lambda w: pl.BlockSpec((tm, w), lambda s: (s, 0))
    sds = jax.ShapeDtypeStruct
    return pl.pallas_call(
        body, name="att_gate_bwd", grid=(S // tm,),
        in_specs=[row(AT_W), row(AT_W), row(STAT_W), pl.BlockSpec((tm, AT_W), lambda s: (s, 1))],
        out_specs=[row(AT_W), _dil_spec(4, tm, AT_W), _dil_spec(16, tm, AT_W),
                   row(STAT_W), _dil_spec(4, tm, STAT_W), _dil_spec(16, tm, STAT_W), row(AT_W)],
        out_shape=[sds((S, AT_W), BF16), sds((4, S // 4, AT_W), BF16), sds((16, S // 16, AT_W), BF16),
                   sds((S, STAT_W), F32), sds((4, S // 4, STAT_W), F32), sds((16, S // 16, STAT_W), F32),
                   sds((S, AT_W), BF16)],
        scratch_shapes=[_slab_scratch(tm, AT_W), _slab_scratch(tm, STAT_W)],
        compiler_params=_params(("parallel",)),
    )(dm_at, o_at, lse, p3)


def _att_bwd(qkv_d, do_d, dl_d):
    d, L, _ = qkv_d.shape
    nb = L // ATT_BLK
    qb = ATT_QB
    rows = qb * ATT_BLK
    nsteps = L // rows

    def body(qc_ref, qn_ref, kp_ref, kc_ref, vp_ref, vc_ref, ac_ref, an_ref, lc_ref, ln_ref, bq_ref, bk_ref, dqkv_ref,
             s_scr, dp_scr, st_scr, dpt_scr, ds_scr, pt_scr, dst_scr):
        n = pl.program_id(1)
        first = _iota((ATT_BLK, HEAD_PAIR), 1) < 64
        pairs = [slice(HEAD_PAIR * hp, HEAD_PAIR * (hp + 1)) for hp in range(4)]
        blk = lambda i: slice(ATT_BLK * i, ATT_BLK * (i + 1))

        def stack(t):
            zero = jnp.zeros_like(t)
            return jnp.concatenate([jnp.where(first, t, zero), jnp.where(first, zero, t)], axis=0)

        def unstack(t2):
            return jnp.where(first, t2[:ATT_BLK], t2[ATT_BLK:])

        def with_prev(i, sl, prev_ref, cur_ref):
            before = prev_ref[0, :, sl] if i == 0 else cur_ref[0, blk(i - 1), sl]
            return jnp.concatenate([before, cur_ref[0, blk(i), sl]], axis=0)

        def with_next(i, sl, cur_ref, next_ref):
            after = next_ref[0, :, sl] if i == qb - 1 else cur_ref[0, blk(i + 1), sl]
            return jnp.concatenate([cur_ref[0, blk(i), sl], after], axis=0)

        for i in range(qb):
            for hp, sl in enumerate(pairs):
                j = 4 * i + hp
                s_scr[j] = _dot_nt(stack(qc_ref[0, blk(i), sl] * ATT_SCALE), with_prev(i, sl, kp_ref, kc_ref))
                dp_scr[j] = _dot_nt(stack(ac_ref[0, blk(i), sl]), with_prev(i, sl, vp_ref, vc_ref))
                st_scr[j] = _dot_nt(stack(kc_ref[0, blk(i), sl] * ATT_SCALE), with_next(i, sl, qc_ref, qn_ref))
                dpt_scr[j] = _dot_nt(stack(vc_ref[0, blk(i), sl]), with_next(i, sl, ac_ref, an_ref))
        for i in range(qb):
            bias = bq_ref[jnp.minimum(n, 1)] if i == 0 else bq_ref[1]
            bias_t = bk_ref[jnp.minimum(nsteps - 1 - n, 1)] if i == qb - 1 else bk_ref[1]
            bias2 = jnp.concatenate([bias, bias], axis=0)
            bias_t2 = jnp.concatenate([bias_t, bias_t], axis=0)
            dl_c = lc_ref[0, blk(i), :]
            dl_t = with_next(i, slice(None), lc_ref, ln_ref).T
            for hp in range(4):
                j = 4 * i + hp
                at = [STAT_LANES * (2 * hp), STAT_LANES * (2 * hp + 1)]
                col = lambda t, o: jnp.concatenate([t[:, a + o: a + o + 1] for a in at], axis=0)
                p = jnp.exp(s_scr[j] + bias2 - col(dl_c, STAT_LSE_LANE))
                ds_scr[j] = (p * (dp_scr[j] - col(dl_c, 0))).astype(BF16)
                row = lambda t, o: jnp.concatenate([jnp.broadcast_to(t[a + o: a + o + 1, :], (ATT_BLK, 2 * ATT_BLK)) for a in at], axis=0)
                pt = jnp.exp(st_scr[j] + bias_t2 - row(dl_t, STAT_LSE_LANE))
                pt_scr[j] = pt.astype(BF16)
                dst_scr[j] = (pt * (dpt_scr[j] - row(dl_t, 0))).astype(BF16)
        for i in range(qb):
            for hp, sl in enumerate(pairs):
                j = 4 * i + hp
                dq = unstack(_dot(ds_scr[j], with_prev(i, sl, kp_ref, kc_ref))) * ATT_SCALE
                dk = unstack(_dot(dst_scr[j], with_next(i, sl, qc_ref, qn_ref))) * ATT_SCALE
                dv = unstack(_dot(pt_scr[j], with_next(i, sl, ac_ref, an_ref)))
                dqkv_ref[0, blk(i), sl] = dq.astype(BF16)
                dqkv_ref[0, blk(i), AT_W + HEAD_PAIR * hp: AT_W + HEAD_PAIR * (hp + 1)] = dk.astype(BF16)
                dqkv_ref[0, blk(i), 2 * AT_W + HEAD_PAIR * hp: 2 * AT_W + HEAD_PAIR * (hp + 1)] = dv.astype(BF16)

    cur = lambda j: pl.BlockSpec((1, rows, AT_W), lambda r, n: (r, n, j))
    prev = lambda j: pl.BlockSpec((1, ATT_BLK, AT_W), lambda r, n: (r, jnp.maximum(qb * n - 1, 0), j))
    nxt_blk = lambda n: jnp.minimum(qb * (n + 1), nb - 1)
    sq = (4 * qb, 2 * ATT_BLK, 2 * ATT_BLK)
    return pl.pallas_call(
        body, name=f"att_bwd_d{d}", grid=(d, nsteps),
        in_specs=[cur(0), pl.BlockSpec((1, ATT_BLK, AT_W), lambda r, n: (r, nxt_blk(n), 0)), prev(1), cur(1), prev(2), cur(2),
                  pl.BlockSpec((1, rows, AT_W), lambda r, n: (r, n, 0)),
                  pl.BlockSpec((1, ATT_BLK, AT_W), lambda r, n: (r, nxt_blk(n), 0)),
                  pl.BlockSpec((1, rows, STAT_W), lambda r, n: (r, n, 0)),
                  pl.BlockSpec((1, ATT_BLK, STAT_W), lambda r, n: (r, nxt_blk(n), 0)),
                  _full((2, ATT_BLK, 2 * ATT_BLK)), _full((2, ATT_BLK, 2 * ATT_BLK))],
        out_specs=pl.BlockSpec((1, rows, 3 * AT_W), lambda r, n: (r, n, 0)),
        out_shape=jax.ShapeDtypeStruct((d, L, 3 * AT_W), BF16),
        scratch_shapes=[pltpu.VMEM(sq, F32)] * 4 + [pltpu.VMEM(sq, BF16)] * 3,
        compiler_params=_params(("parallel", "parallel")),
    )(qkv_d, qkv_d, qkv_d, qkv_d, qkv_d, qkv_d, do_d, do_d, dl_d, dl_d, *_att_bias())


def _att_bwd_combine(dqkvs, cos2, sin2):
    S = dqkvs[0].shape[1]
    tm = 512

    def body(a_ref, b4_ref, c16_ref, cos_ref, sin_ref, dq_ref, dk_ref, dv_ref, b_scr, c_scr):
        _merge_residues(b4_ref, b_scr, 4)
        _merge_residues(c16_ref, c_scr, 16)
        t = a_ref[...].astype(F32) + _from_slabs(b_scr) + _from_slabs(c_scr)
        dy = t[:, : 2 * AT_W]
        cosf = jnp.tile(cos_ref[...], (1, 8))
        sinf = jnp.tile(sin_ref[...], (1, 8))
        dx = dy * cosf - _rope_rot(dy) * sinf
        dq_ref[...] = dx[:, :AT_W].astype(BF16)
        dk_ref[...] = dx[:, AT_W:].astype(BF16)
        dv_ref[...] = t[:, 2 * AT_W:].astype(BF16)

    row = lambda w: pl.BlockSpec((tm, w), lambda s: (s, 0))
    act = jax.ShapeDtypeStruct((S, AT_W), BF16)
    return pl.pallas_call(
        body, name="att_bwd_combine", grid=(S // tm,),
        in_specs=[row(3 * AT_W), _dil_spec(4, tm, 3 * AT_W), _dil_spec(16, tm, 3 * AT_W), row(128), row(128)],
        out_specs=[row(AT_W), row(AT_W), row(AT_W)],
        out_shape=[act, act, act],
        scratch_shapes=[_slab_scratch(tm, 3 * AT_W), _slab_scratch(tm, 3 * AT_W)],
        compiler_params=_params(("parallel",)),
    )(dqkvs[0].reshape(S, 3 * AT_W), dqkvs[1], dqkvs[2], cos2, sin2)


def _outproj(x2, tgt2, mix_hg, mix_at, w_out_full, fnw):
    S = x2.shape[0]
    tm = 256
    ns = S // tm

    def body(x_ref, t_ref, mh_ref, ma_ref, w_ref, fw_ref,
             dh_ref, dmh_ref, dma_ref, gw_ref, gfw_ref, loss_ref):
        s = pl.program_id(0)

        @pl.when(s == 0)
        def _():
            gw_ref[...] = jnp.zeros_like(gw_ref)
            gfw_ref[...] = jnp.zeros_like(gfw_ref)
            loss_ref[...] = jnp.zeros_like(loss_ref)

        y = _dot(mh_ref[...], w_ref[:HG_W, :]) + _dot(ma_ref[...], w_ref[HG_W:, :])
        h = x_ref[...] + y
        r = lax.rsqrt(jnp.mean(h * h, axis=-1, keepdims=True) + EPS)
        hn = h * r
        fw = fw_ref[...]
        err = hn * fw - t_ref[...]
        loss_ref[...] += 0.5 * jnp.sum(jnp.mean(err * err, axis=-1, keepdims=True))
        dout = err * (1.0 / D_MODEL)
        gfw_ref[...] += jnp.sum(dout * hn, axis=0, keepdims=True)
        dhn = dout * fw
        dh = r * (dhn - hn * jnp.mean(dhn * hn, axis=-1, keepdims=True))
        dh_ref[...] = dh
        dhb = dh.astype(BF16)
        dmh_ref[...] = _dot_nt(dhb, w_ref[:HG_W, :])
        dma_ref[...] = _dot_nt(dhb, w_ref[HG_W:, :])
        gw_ref[:HG_W, :] += _dot_tn(mh_ref[...], dhb)
        gw_ref[HG_W:, :] += _dot_tn(ma_ref[...], dhb)

    row = lambda w: pl.BlockSpec((tm, w), lambda s: (s, 0))
    return pl.pallas_call(
        body, name="outproj", grid=(ns,),
        in_specs=[row(D_MODEL), row(D_MODEL), row(HG_W), row(AT_W),
                  _full((D_MODEL, D_MODEL)), _full((1, D_MODEL))],
        out_specs=[row(D_MODEL), row(HG_W), row(AT_W), _full((D_MODEL, D_MODEL)), _full((1, D_MODEL)), _full((8, 128))],
        out_shape=[jax.ShapeDtypeStruct((S, D_MODEL), F32), jax.ShapeDtypeStruct((S, HG_W), F32),
                   jax.ShapeDtypeStruct((S, AT_W), F32), jax.ShapeDtypeStruct((D_MODEL, D_MODEL), F32),
                   jax.ShapeDtypeStruct((1, D_MODEL), F32), jax.ShapeDtypeStruct((8, 128), F32)],
        compiler_params=_params(("arbitrary",)),
    )(x2, tgt2, mix_hg, mix_at, w_out_full, fnw)


def _inproj_bwd_x(dps, w_in_full, x2, norm_w, dh):
    S = x2.shape[0]
    tm = 256

    def body(d0, d1, d2, d3, d4, d5, d6, d7, w_ref, x_ref, nw_ref, dh_ref, gx_ref, gnw_ref):
        s = pl.program_id(0)

        @pl.when(s == 0)
        def _():
            gnw_ref[...] = jnp.zeros_like(gnw_ref)

        du = jnp.zeros((tm, D_MODEL), F32)
        for i, dref in enumerate((d0, d1, d2, d3, d4, d5, d6, d7)):
            j, half = divmod(i, 2)
            du = du + _dot_nt(dref[...], w_ref[j, :, 512 * half: 512 * (half + 1)])
        x = x_ref[...]
        r = lax.rsqrt(jnp.mean(x * x, axis=-1, keepdims=True) + EPS)
        xh = x * r
        gnw_ref[...] += jnp.sum(du * xh, axis=0, keepdims=True)
        dun = du * nw_ref[...]
        gx_ref[...] = dh_ref[...] + r * (dun - xh * jnp.mean(dun * xh, axis=-1, keepdims=True))

    row = lambda w: pl.BlockSpec((tm, w), lambda s: (s, 0))
    return pl.pallas_call(
        body, name="inproj_bwd_x", grid=(S // tm,),
        in_specs=[row(512)] * 8 + [_full((4, D_MODEL, 1024)), row(D_MODEL), _full((1, D_MODEL)), row(D_MODEL)],
        out_specs=[row(D_MODEL), _full((1, D_MODEL))],
        out_shape=[jax.ShapeDtypeStruct((S, D_MODEL), F32), jax.ShapeDtypeStruct((1, D_MODEL), F32)],
        compiler_params=_params(("arbitrary",)),
    )(*dps, w_in_full, x2, norm_w, dh)


def _inproj_bwd_w(u, dps):
    S = u.shape[0]
    tm = 256

    def body(u_ref, d0, d1, d2, d3, d4, d5, d6, d7, g_ref):
        @pl.when(pl.program_id(0) == 0)
        def _():
            g_ref[...] = jnp.zeros_like(g_ref)

        ub = u_ref[...]
        for i, dref in enumerate((d0, d1, d2, d3, d4, d5, d6, d7)):
            j, half = divmod(i, 2)
            g_ref[j, :, 512 * half: 512 * (half + 1)] += _dot_tn(ub, dref[...])

    return pl.pallas_call(
        body, name="inproj_bwd_w", grid=(S // tm,),
        in_specs=[pl.BlockSpec((tm, D_MODEL), lambda s: (s, 0))] + [pl.BlockSpec((tm, 512), lambda s: (s, 0))] * 8,
        out_specs=_full((4, D_MODEL, 1024)),
        out_shape=jax.ShapeDtypeStruct((4, D_MODEL, 1024), F32),
        compiler_params=_params(("arbitrary",)),
    )(u, *dps)


def _adamw_update(gg, w_ref, m_ref, v_ref, d_ref, nm_ref, nv_ref):
    nm = ADAM_B1 * m_ref[...] + (1.0 - ADAM_B1) * gg
    nv = ADAM_B2 * v_ref[...] + (1.0 - ADAM_B2) * (gg * gg)
    m_hat = nm / (1.0 - ADAM_B1 ** ADAM_STEP)
    v_hat = nv / (1.0 - ADAM_B2 ** ADAM_STEP)
    d_ref[...] = -ADAM_LR * (m_hat / (jnp.sqrt(v_hat) + ADAM_EPS) + ADAM_WD * w_ref[...])
    nm_ref[...] = nm
    nv_ref[...] = nv


def _adamw(w, g, m, v, name):
    rows, cols = w.shape
    tr = min(rows, 256)

    def body(w_ref, g_ref, m_ref, v_ref, d_ref, nm_ref, nv_ref):
        _adamw_update(g_ref[...], w_ref, m_ref, v_ref, d_ref, nm_ref, nv_ref)

    spec = pl.BlockSpec((tr, cols), lambda i: (i, 0))
    sds = jax.ShapeDtypeStruct((rows, cols), F32)
    return pl.pallas_call(
        body, name=name, grid=(rows // tr,),
        in_specs=[spec] * 4, out_specs=[spec] * 3, out_shape=[sds] * 3,
        compiler_params=_params(("parallel",)),
    )(w, g, m, v)


def _place():
    return lax.axis_index("x"), lax.axis_index("y"), lax.axis_index("c")


def _gather_weights(w_in_s, w_out_s):
    def body(win_ref, wout_ref, fin_ref, fout_ref, send_sems, recv_sems):
        x, y, c = _place()
        me = (x, y, c)
        sib = (x, y, 1 - c)
        mine = 2 * x + y
        fin_ref[mine] = win_ref[...].astype(BF16)
        fout_ref[mine] = wout_ref[...].astype(BF16)
        chips = [(1 - x, y), (x, 1 - y), (1 - x, 1 - y)]

        def halves(chip, half):
            return (fin_ref.at[chip, pl.ds(half * 512, 512), :], fout_ref.at[chip, pl.ds(half * 128, 128), :])

        def copy(k, ref, to):
            return pltpu.make_async_remote_copy(src_ref=ref, dst_ref=ref, send_sem=send_sems.at[k],
                                                recv_sem=recv_sems.at[k], device_id=to, device_id_type=MESH)

        first, passed = [], []
        for j, (cx, cy) in enumerate(chips):
            for a, ref in enumerate(halves(mine, c)):
                first.append(copy(2 * j + a, ref, (cx, cy, c)))
        for cp in first:
            cp.start()
        for j, (cx, cy) in enumerate(chips):
            for a, ref in enumerate(halves(2 * cx + cy, c)):
                copy(2 * j + a, ref, me).wait_recv()
                fwd = copy(6 + 2 * j + a, ref, sib)
                fwd.start()
                passed.append(fwd)
        for j, (cx, cy) in enumerate(chips):
            for a, ref in enumerate(halves(2 * cx + cy, 1 - c)):
                copy(6 + 2 * j + a, ref, me).wait_recv()
        for cp in first + passed:
            cp.wait_send()

    vm = pl.BlockSpec(memory_space=pltpu.VMEM)
    return pl.pallas_call(
        body, name="gather_weights",
        in_specs=[vm, vm], out_specs=[vm, vm],
        out_shape=[jax.ShapeDtypeStruct((4, D_MODEL, 1024), BF16), jax.ShapeDtypeStruct((4, 256, D_MODEL), BF16)],
        scratch_shapes=[pltpu.SemaphoreType.DMA((12,)), pltpu.SemaphoreType.DMA((12,))],
        compiler_params=pltpu.CompilerParams(vmem_limit_bytes=VMEM_LIMIT),
    )(w_in_s, w_out_s)


def _swap_halves(g_in, g_out):
    def body(gin_ref, gout_ref, rin_ref, rout_ref, send_sems, recv_sems):
        x, y, c = _place()
        sib = (x, y, 1 - c)
        cps = [pltpu.make_async_remote_copy(src_ref=src.at[:, 1 - c], dst_ref=dst, send_sem=send_sems.at[k],
                                            recv_sem=recv_sems.at[k], device_id=sib, device_id_type=MESH)
               for k, (src, dst) in enumerate(((gin_ref, rin_ref), (gout_ref, rout_ref)))]
        for cp in cps:
            cp.start()
        for cp in cps:
            cp.wait()

    hbm = pl.BlockSpec(memory_space=pl.ANY)
    return pl.pallas_call(
        body, name="swap_halves",
        in_specs=[hbm, hbm], out_specs=[hbm, hbm],
        out_shape=[jax.ShapeDtypeStruct((4,) + g.shape[2:], F32) for g in (g_in, g_out)],
        scratch_shapes=[pltpu.SemaphoreType.DMA((2,)), pltpu.SemaphoreType.DMA((2,))],
    )(g_in, g_out)


def _add_half(g, r, cidx, name):
    n, _, rows, cols = g.shape

    def body(c_ref, g_ref, r_ref, o_ref):
        o_ref[0] = (g_ref[0, 0] + r_ref[0]).astype(BF16)

    return pl.pallas_call(
        body, name=name,
        grid_spec=pltpu.PrefetchScalarGridSpec(
            num_scalar_prefetch=1, grid=(n,),
            in_specs=[pl.BlockSpec((1, 1, rows, cols), lambda j, c_ref: (j, c_ref[0], 0, 0)),
                      pl.BlockSpec((1, rows, cols), lambda j, c_ref: (j, 0, 0))],
            out_specs=pl.BlockSpec((1, rows, cols), lambda j, c_ref: (j, 0, 0))),
        out_shape=jax.ShapeDtypeStruct((n, rows, cols), BF16),
        compiler_params=_params(("parallel",)),
    )(cidx, g, r)


def _exchange_copies(in_ref, out_ref, lin_ref, lout_ref, send_sems, recv_sems):
    x, y, c = _place()
    cps = []
    for k, (cx, cy) in enumerate([(1 - x, y), (x, 1 - y), (1 - x, 1 - y)]):
        for a, (src, dst) in enumerate(((in_ref, lin_ref), (out_ref, lout_ref))):
            cps.append(pltpu.make_async_remote_copy(
                src_ref=src.at[2 * cx + cy], dst_ref=dst.at[k], send_sem=send_sems.at[2 * k + a],
                recv_sem=recv_sems.at[2 * k + a], device_id=(cx, cy, c), device_id_type=MESH))
    return cps


def _exchange_start(cs_in, cs_out):
    def body(in_ref, out_ref, lin_ref, lout_ref, send_sems, recv_sems, in_thru, out_thru, lin_thru, lout_thru, token):
        for cp in _exchange_copies(in_ref, out_ref, lin_ref, lout_ref, send_sems, recv_sems):
            cp.start()
        token[...] = jnp.zeros_like(token)

    lands = [lax.empty((3,) + a.shape[1:], a.dtype) for a in (cs_in, cs_out)]
    bufs = [pltpu.with_memory_space_constraint(a, pltpu.HBM) for a in (cs_in, cs_out, *lands)]
    hbm = pl.BlockSpec(memory_space=pltpu.HBM)
    sem = pl.BlockSpec(memory_space=pltpu.SEMAPHORE)
    return pl.pallas_call(
        body, name="exchange_start",
        in_specs=[hbm] * 4,
        out_specs=[sem, sem, hbm, hbm, hbm, hbm, pl.BlockSpec(memory_space=pltpu.VMEM)],
        out_shape=[pltpu.SemaphoreType.DMA((6,)), pltpu.SemaphoreType.DMA((6,))]
                  + [pltpu.HBM(b.shape, b.dtype) for b in bufs] + [jax.ShapeDtypeStruct((8, 128), F32)],
        input_output_aliases={0: 2, 1: 3, 2: 4, 3: 5},
        compiler_params=pltpu.CompilerParams(has_side_effects=pltpu.SideEffectType.DATAFLOW_SIDE_EFFECTING),
    )(*bufs)


def _exchange_wait(send_sems, recv_sems, in_thru, out_thru, lin_thru, lout_thru, after):
    def body(in_ref, out_ref, lin_ref, lout_ref, send_sems, recv_sems, after_ref, in_dead, out_dead, got_in, got_out):
        for cp in _exchange_copies(in_ref, out_ref, lin_ref, lout_ref, send_sems, recv_sems):
            cp.wait_send()
            cp.wait_recv()

    hbm = pl.BlockSpec(memory_space=pltpu.HBM)
    sem = pl.BlockSpec(memory_space=pltpu.SEMAPHORE)
    bufs = (in_thru, out_thru, lin_thru, lout_thru)
    return pl.pallas_call(
        body, name="exchange_wait",
        in_specs=[hbm] * 4 + [sem, sem, pl.BlockSpec(memory_space=pl.ANY)],
        out_specs=[hbm] * 4,
        out_shape=[pltpu.HBM(b.shape, b.dtype) for b in bufs],
        input_output_aliases={0: 0, 1: 1, 2: 2, 3: 3},
        compiler_params=pltpu.CompilerParams(has_side_effects=pltpu.SideEffectType.DATAFLOW_SIDE_EFFECTING),
    )(*bufs, send_sems, recv_sems, after)


PEER_XOR = (2, 1, 3)


def _sum_chips(cs, r, chip_idx, name):
    _, rows, cols = r.shape
    tr = min(rows, 256)

    def body(m_ref, cs_ref, r_ref, o_ref):
        mine = m_ref[0]
        own = cs_ref[0].astype(F32)
        got = [r_ref[k].astype(F32) for k in range(3)]
        acc = None
        for s in range(4):
            rel = mine ^ s
            term = jnp.where(rel == 0, own, jnp.where(rel == PEER_XOR[0], got[0],
                                                      jnp.where(rel == PEER_XOR[1], got[1], got[2])))
            acc = term if acc is None else acc + term
        o_ref[...] = acc

    return pl.pallas_call(
        body, name=name,
        grid_spec=pltpu.PrefetchScalarGridSpec(
            num_scalar_prefetch=1, grid=(rows // tr,),
            in_specs=[pl.BlockSpec((1, tr, cols), lambda i, m_ref: (m_ref[0], i, 0)),
                      pl.BlockSpec((3, tr, cols), lambda i, m_ref: (0, i, 0))],
            out_specs=pl.BlockSpec((tr, cols), lambda i, m_ref: (i, 0))),
        out_shape=jax.ShapeDtypeStruct((rows, cols), F32),
        compiler_params=_params(("parallel",)),
    )(chip_idx, cs, r)


def _swap_reduced(h_in, h_out):
    def body(in_ref, out_ref, rin_ref, rout_ref, send_sems, recv_sems):
        x, y, c = _place()
        cps = [pltpu.make_async_remote_copy(src_ref=src, dst_ref=dst, send_sem=send_sems.at[k],
                                            recv_sem=recv_sems.at[k], device_id=(x, y, 1 - c), device_id_type=MESH)
               for k, (src, dst) in enumerate(((in_ref, rin_ref), (out_ref, rout_ref)))]
        for cp in cps:
            cp.start()
        for cp in cps:
            cp.wait()

    hbm = pl.BlockSpec(memory_space=pl.ANY)
    return pl.pallas_call(
        body, name="swap_reduced",
        in_specs=[hbm, hbm], out_specs=[hbm, hbm],
        out_shape=[jax.ShapeDtypeStruct(h.shape, F32) for h in (h_in, h_out)],
        scratch_shapes=[pltpu.SemaphoreType.DMA((2,)), pltpu.SemaphoreType.DMA((2,))],
    )(h_in, h_out)


def _adamw_halves(w, mine, theirs, m, v, cidx, name):
    rows, cols = w.shape
    half = rows // 2
    tr = min(half, 256)
    nbh = half // tr

    def body(c_ref, w_ref, a_ref, b_ref, m_ref, v_ref, g_ref, d_ref, nm_ref, nv_ref):
        gg = jnp.where(pl.program_id(0) // nbh == c_ref[0], a_ref[...], b_ref[...])
        g_ref[...] = gg
        _adamw_update(gg, w_ref, m_ref, v_ref, d_ref, nm_ref, nv_ref)

    spec = pl.BlockSpec((tr, cols), lambda i, c_ref: (i, 0))
    hspec = pl.BlockSpec((tr, cols), lambda i, c_ref: (i % nbh, 0))
    sds = jax.ShapeDtypeStruct((rows, cols), F32)
    return pl.pallas_call(
        body, name=name,
        grid_spec=pltpu.PrefetchScalarGridSpec(
            num_scalar_prefetch=1, grid=(rows // tr,),
            in_specs=[spec, hspec, hspec, spec, spec], out_specs=[spec] * 4),
        out_shape=[sds] * 4,
        compiler_params=_params(("parallel",)),
    )(cidx, w, mine, theirs, m, v)


def _allreduce_small(g_nw, g_fw, g_hgw, g_lbl, loss8):
    def body(nw_ref, fw_ref, hgw_ref, lbl_ref, loss_ref, out_ref, slots, send_sems, recv_sems):
        x, y, c = _place()
        me = 4 * x + 2 * y + c
        slots[me] = jnp.zeros((8, D_MODEL), F32)
        slots[me, 0:1, :] = nw_ref[...]
        slots[me, 1:2, :] = fw_ref[...]
        slots[me, 2:3, 0:HG_W] = hgw_ref[...]
        slots[me, 3:4, 0:HG_W] = lbl_ref[0:1, :]
        slots[me, 3:4, HG_W:] = lbl_ref[1:2, :]
        slots[me, 4:5, 0:128] = loss_ref[0:1, :]
        cps = []
        for k in range(1, 8):
            dx, dy, dc = (k >> 2) & 1, (k >> 1) & 1, k & 1
            to = (x ^ dx, y ^ dy, c ^ dc)
            cps.append(pltpu.make_async_remote_copy(
                src_ref=slots.at[me], dst_ref=slots.at[me], send_sem=send_sems.at[k - 1],
                recv_sem=recv_sems.at[k - 1], device_id=to, device_id_type=MESH))
        for cp in cps:
            cp.start()
        for cp in cps:
            cp.wait()
        acc = slots[0]
        for i in range(1, 8):
            acc = acc + slots[i]
        out_ref[...] = acc

    vm = pl.BlockSpec(memory_space=pltpu.VMEM)
    return pl.pallas_call(
        body, name="allreduce_small",
        in_specs=[vm] * 5, out_specs=vm,
        out_shape=jax.ShapeDtypeStruct((8, D_MODEL), F32),
        scratch_shapes=[pltpu.VMEM((8, 8, D_MODEL), F32), pltpu.SemaphoreType.DMA((7,)), pltpu.SemaphoreType.DMA((7,))],
    )(g_nw, g_fw, g_hgw, g_lbl, loss8)


def _rope_tables(S):
    inv_freq = (np.float32(1.0) / np.power(np.float32(ROPE_THETA), np.arange(ROPE_HALF, dtype=np.float32) / np.float32(ROPE_HALF))).astype(np.float32)
    ang = (np.arange(S, dtype=np.float32)[:, None] * inv_freq[None, :]).astype(np.float32)
    cos, sin = np.cos(ang).astype(np.float32), np.sin(ang).astype(np.float32)
    cos2 = np.concatenate([cos, cos, cos, cos], axis=-1)
    sin2 = np.concatenate([-sin, sin, -sin, sin], axis=-1)
    return jnp.asarray(cos2), jnp.asarray(sin2)


def _local_step(x2, tgt2, norm_w, w_in_full, lbl, hg_norm_w, w_out4, fnw):
    S = x2.shape[0]
    cos2, sin2 = _rope_tables(S)
    w_out_full = w_out4.reshape(D_MODEL, D_MODEL)
    p0, p1, qkv, qkv4, qkv16, p3, u = _inproj(x2, norm_w, w_in_full, cos2, sin2)
    o_hg, mix_hg, states = _hg_fwd(p0, p1, lbl, hg_norm_w)
    qkv_ds = [qkv.reshape(1, S, 3 * AT_W), qkv4, qkv16]
    os_d, ls_d = zip(*[_att_fwd(q) for q in qkv_ds])
    o_at, lse, mix_at = _att_combine(os_d, ls_d, p3)
    dh, dm_hg, dm_at, g_wout, g_fw, loss8 = _outproj(x2, tgt2, mix_hg, mix_at, w_out_full, fnw)
    dqr, dfl, dv_hg, dz_hg, g_lbl, g_hgw = _hg_bwd(p0, p1, o_hg, dm_hg, states, lbl, hg_norm_w)
    do1, do4, do16, dl1, dl4, dl16, dz_at = _att_gate_bwd(dm_at, o_at, lse, p3)
    do_ds = [do1.reshape(1, S, AT_W), do4, do16]
    dl_ds = [dl1.reshape(1, S, STAT_W), dl4, dl16]
    dqkvs = [_att_bwd(q, a, b) for q, a, b in zip(qkv_ds, do_ds, dl_ds)]
    dq_at, dk_at, dv_at = _att_bwd_combine(dqkvs, cos2, sin2)
    dps = [dqr, dfl, dv_hg, dz_hg, dq_at, dk_at, dv_at, dz_at]
    return loss8, dps, u, dh, g_lbl, g_hgw, g_wout, g_fw


def kernel(x, norm_w, w_in, hgrn_lb_logits, hg_norm_w, w_out, final_norm_w, loss_target, m_norm_w, m_w_in, m_hgrn_lb_logits, m_hg_norm_w, m_w_out, m_final_norm_w, v_norm_w, v_w_in, v_hgrn_lb_logits, v_hg_norm_w, v_w_out, v_final_norm_w):
    S = x.shape[1]
    w_in_full, w_out_full = _gather_weights(w_in[0], w_out[0])
    loss8, dps, u, dh, g_lbl, g_hgw, g_wout, g_fw = _local_step(
        x[0], loss_target[0], norm_w, w_in_full, hgrn_lb_logits, hg_norm_w,
        w_out_full, final_norm_w.reshape(1, D_MODEL))

    cidx = lax.axis_index("c").astype(jnp.int32).reshape(1)
    g_win = _inproj_bwd_w(u, dps)
    g_in4 = g_win.reshape(4, 2, 512, 1024)
    g_out4 = g_wout.reshape(4, 2, 128, D_MODEL)
    r_in, r_out = _swap_halves(g_in4, g_out4)
    cs_in = _add_half(g_in4, r_in, cidx, "add_half_in")
    cs_out = _add_half(g_out4, r_out, cidx, "add_half_out")
    *started, token = _exchange_start(cs_in, cs_out)
    grad_x, g_nw = _inproj_bwd_x(dps, w_in_full, x[0], norm_w + token[0:1, 0:1], dh)
    cs_in, cs_out, x_in, x_out = _exchange_wait(*started, g_nw)
    chip_idx = (2 * lax.axis_index("x") + lax.axis_index("y")).astype(jnp.int32).reshape(1)
    h_in = _sum_chips(cs_in, x_in, chip_idx, "sum_chips_in")
    h_out = _sum_chips(cs_out, x_out, chip_idx, "sum_chips_out")
    s_in, s_out = _swap_reduced(h_in, h_out)

    red = _allreduce_small(g_nw, g_fw, g_hgw, g_lbl, loss8)
    loss = red[4, 0]
    grad_norm_w = red[0:1, :]
    grad_final_norm_w = red[1, :]
    grad_hg_norm_w = red[2:3, :HG_W]
    grad_lbl = jnp.concatenate([red[3:4, :HG_W], red[3:4, HG_W:]], axis=0)

    d_nw, m_nw, v_nw = _adamw(norm_w, grad_norm_w, m_norm_w, v_norm_w, "adamw_norm_w")
    grad_w_in, d_win, m_win, v_win = _adamw_halves(w_in[0], h_in, s_in, m_w_in[0], v_w_in[0], cidx, "adamw_w_in")
    d_lbl, m_lbl, v_lbl = _adamw(hgrn_lb_logits, grad_lbl, m_hgrn_lb_logits, v_hgrn_lb_logits, "adamw_lb_logits")
    d_hgw, m_hgw, v_hgw = _adamw(hg_norm_w, grad_hg_norm_w, m_hg_norm_w, v_hg_norm_w, "adamw_hg_norm_w")
    grad_w_out, d_wout, m_wout, v_wout = _adamw_halves(w_out[0], h_out, s_out, m_w_out[0], v_w_out[0], cidx, "adamw_w_out")
    d_fw, m_fw, v_fw = _adamw(final_norm_w.reshape(1, D_MODEL), grad_final_norm_w.reshape(1, D_MODEL),
                              m_final_norm_w.reshape(1, D_MODEL), v_final_norm_w.reshape(1, D_MODEL), "adamw_final_norm_w")
    e1 = lambda a: a[None]
    flat = lambda a: a.reshape(D_MODEL)
    return (loss, grad_x[None], grad_norm_w, e1(grad_w_in), grad_lbl, grad_hg_norm_w, e1(grad_w_out), grad_final_norm_w,
            d_nw, e1(d_win), d_lbl, d_hgw, e1(d_wout), flat(d_fw),
            m_nw, e1(m_win), m_lbl, m_hgw, e1(m_wout), flat(m_fw),
            v_nw, e1(v_win), v_lbl, v_hgw, e1(v_wout), flat(v_fw))
```

```python
import jax
import jax.numpy as jnp
import numpy as np
from jax import lax
from jax.experimental import pallas as pl
from jax.experimental.pallas import tpu as pltpu

F32 = jnp.float32
BF16 = jnp.bfloat16
MESH = pl.DeviceIdType.MESH

D_MODEL = 1024
HG_W = 512
AT_W = 512
HEAD_PAIR = 128
ROPE_HALF = 32
ROPE_THETA = 10000.0
EPS = 1e-6
CHUNK = 128
LEVELS = (64, 32, 16, 8, 4, 2, 1)
DIAG = 1
SUBLANES = 8
ATT_BLK = 128
ATT_QB = 4
DILATIONS = (1, 4, 16)
ATT_SCALE = 0.125
STAT_W = 128
STAT_LANES = 16
STAT_LSE_LANE = 8
NEG = -1e30
VMEM_LIMIT = 56 * 1024 * 1024
PROJ_TM = 512

ADAM_LR = 0.001
ADAM_B1 = 0.9
ADAM_B2 = 0.999
ADAM_EPS = 1e-08
ADAM_WD = 0.01
ADAM_STEP = 10


def _iota(shape, dim):
    return lax.broadcasted_iota(jnp.int32, shape, dim)


def _dot(a, b):
    return jnp.dot(a, b, preferred_element_type=F32)


def _dot_nt(a, b):
    return lax.dot_general(a, b, (((1,), (1,)), ((), ())), preferred_element_type=F32)


def _dot_tn(a, b):
    return lax.dot_general(a, b, (((0,), (0,)), ((), ())), preferred_element_type=F32)


def _sigmoid(v):
    return 0.5 * jnp.tanh(0.5 * v) + 0.5


def _params(sem=None, vmem=VMEM_LIMIT):
    return pltpu.CompilerParams(dimension_semantics=sem, vmem_limit_bytes=vmem)


def _full(shape):
    n = len(shape)
    return pl.BlockSpec(shape, lambda *_: (0,) * n)


def _rope_rot(y):
    n = y.shape[1]
    first = (_iota(y.shape, 1) & (2 * ROPE_HALF - 1)) < ROPE_HALF
    return jnp.where(first, pltpu.roll(y, n - ROPE_HALF, 1), pltpu.roll(y, ROPE_HALF, 1))


def _dil_spec(d, tm, width):
    return pl.BlockSpec((d, tm // d, width), lambda s: (0, s, 0))


LANES = 128


def _slab_scratch(tm, width):
    return pltpu.VMEM((width // LANES, tm, LANES), F32)


def _to_slabs(v, slabs_ref):
    for j in range(slabs_ref.shape[0]):
        slabs_ref[j] = v[:, LANES * j: LANES * (j + 1)]


def _from_slabs(slabs_ref):
    return jnp.concatenate([slabs_ref[j] for j in range(slabs_ref.shape[0])], axis=1)


def _split_residues(slabs_ref, dst_ref, d, dtype):
    nslab, tm, _ = slabs_ref.shape
    for r in range(d):
        for j in range(nslab):
            dst_ref[r, :, LANES * j: LANES * (j + 1)] = slabs_ref[j, pl.ds(r, tm // d, stride=d), :].astype(dtype)


def _merge_residues(src_ref, slabs_ref, d):
    nslab, tm, _ = slabs_ref.shape
    for r in range(d):
        for j in range(nslab):
            slabs_ref[j, pl.ds(r, tm // d, stride=d), :] = src_ref[r, :, LANES * j: LANES * (j + 1)].astype(F32)


def _inproj(x2, norm_w, w_in_full, cos2, sin2):
    S = x2.shape[0]
    tm = PROJ_TM

    def body(x_ref, nw_ref, w_ref, cos_ref, sin_ref, p0_ref, p1_ref, qkv_ref, qkv4_ref, qkv16_ref, p3_ref, u_ref, scr):
        x = x_ref[...]
        r = lax.rsqrt(jnp.mean(x * x, axis=-1, keepdims=True) + EPS)
        u = x * r * nw_ref[...]
        ub = u.astype(BF16)
        u_ref[...] = ub
        p0_ref[...] = _dot(ub, w_ref[0])
        p1_ref[...] = _dot(ub, w_ref[1])
        y2 = _dot(ub, w_ref[2])
        cosf = jnp.tile(cos_ref[...], (1, 8))
        sinf = jnp.tile(sin_ref[...], (1, 8))
        y3 = _dot(ub, w_ref[3])
        p3_ref[...] = y3
        qkv = jnp.concatenate([y2 * cosf + _rope_rot(y2) * sinf, y3[:, :AT_W]], axis=1)
        qkv_ref[...] = qkv.astype(BF16)
        _to_slabs(qkv, scr)
        _split_residues(scr, qkv4_ref, 4, BF16)
        _split_residues(scr, qkv16_ref, 16, BF16)

    row = lambda w: pl.BlockSpec((tm, w), lambda s: (s, 0))
    qkv_w = 3 * AT_W
    return pl.pallas_call(
        body, name="inproj", grid=(S // tm,),
        in_specs=[row(D_MODEL), _full((1, D_MODEL)), _full((4, D_MODEL, 1024)), row(128), row(128)],
        out_specs=[row(1024), row(1024), row(qkv_w), _dil_spec(4, tm, qkv_w), _dil_spec(16, tm, qkv_w), row(1024),
                   row(D_MODEL)],
        out_shape=[jax.ShapeDtypeStruct((S, 1024), F32), jax.ShapeDtypeStruct((S, 1024), F32),
                   jax.ShapeDtypeStruct((S, qkv_w), BF16), jax.ShapeDtypeStruct((4, S // 4, qkv_w), BF16),
                   jax.ShapeDtypeStruct((16, S // 16, qkv_w), BF16), jax.ShapeDtypeStruct((S, 1024), F32),
                   jax.ShapeDtypeStruct((S, D_MODEL), BF16)],
        scratch_shapes=[_slab_scratch(tm, qkv_w)],
        compiler_params=_params(("parallel",)),
    )(x2, norm_w, w_in_full, cos2, sin2)


HG_HPS = 4
N_LEV = len(LEVELS)


def _hg_const_arrays():
    r = np.arange(CHUNK)[:, None]
    c = np.arange(CHUNK)[None, :]
    tris = np.stack([r >= c, r <= c])
    lm = [((r // (2 * m)) == (c // (2 * m))) & (r % (2 * m) >= m) & (c % (2 * m) < m) for m in LEVELS]
    dm = [(c == r - dl) & (r % DIAG >= dl) for dl in range(DIAG)]
    masks = np.stack(lm + dm)
    return jnp.asarray(tris, BF16), jnp.asarray(masks, F32)


def _split2(a):
    hi = a.astype(BF16)
    return hi, (a - hi.astype(F32)).astype(BF16)


def _dot3(a, b, dot=_dot):
    ah, al = _split2(a)
    bh, bl = _split2(b)
    n = b.shape[1]
    p = dot(ah, jnp.concatenate([bh, bl], axis=1))
    return (p[:, :n] + p[:, n:]) + dot(al, bh)


def _split3(a):
    a1 = a.astype(BF16)
    r1 = a - a1.astype(F32)
    a2 = r1.astype(BF16)
    return a1, a2, (r1 - a2.astype(F32)).astype(BF16)


def _tri_dot(tri, a):
    n = a.shape[1]
    p = _dot(tri, jnp.concatenate(_split3(a), axis=1))
    return (p[:, :n] + p[:, n:2 * n]) + p[:, 2 * n:]


def _dot_sel(a, sel):
    a1, a2, a3 = _split3(a)
    return (_dot(a1, sel) + _dot(a2, sel)) + _dot(a3, sel)


def _rowsum(t):
    return _dot(t.astype(BF16), jnp.ones((t.shape[1], t.shape[1]), BF16))


def _level_refs(b):
    refs = []
    pos = _iota(b.shape, 0)
    for m in LEVELS:
        if 2 * m >= SUBLANES:
            parts = [jnp.broadcast_to(b[r0 + m - 1: r0 + m, :], (2 * m, b.shape[1])) for r0 in range(0, CHUNK, 2 * m)]
            refs.append(parts[0] if len(parts) == 1 else jnp.concatenate(parts, axis=0))
        else:
            p = pos & (2 * m - 1)
            ref = b
            for off in range(-(m - 1), m + 1):
                if off != 0:
                    ref = jnp.where(p == m - 1 + off, pltpu.roll(b, off % CHUNK, 0), ref)
            refs.append(ref)
    return refs


def _hg_lb(lbl_ref):
    l0 = lbl_ref[0:1, :]
    l1 = lbl_ref[1:2, :]
    mx = jnp.maximum(l0, l1)
    e0 = jnp.exp(l0 - mx)
    e1 = jnp.exp(l1 - mx)
    p0 = e0 / (e0 + e1)
    lb = jnp.clip(p0, 1e-6, 1.0 - 1e-6)
    inside = (p0 >= 1e-6) & (p0 <= 1.0 - 1e-6)
    dlb_dl0 = jnp.where(inside, p0 * (e1 / (e0 + e1)), 0.0)
    return lb, dlb_dl0


def _hg_gates(qr, fl, lb):
    sig = _sigmoid(fl)
    f = lb + (1.0 - lb) * sig
    g = jnp.log(f)
    k = (1.0 - lb) * (1.0 - sig)
    sq = _sigmoid(qr)
    q = qr * sq
    return sig, f, g, k, sq, q


def _hg_levels(q, k, b, mk_ref):
    refs = _level_refs(b)
    a = jnp.zeros((CHUNK, CHUNK), F32)
    es, qts, kts = [], [], []
    for i in range(N_LEV):
        diff = b - refs[i]
        e = jnp.exp(jnp.minimum(diff, -diff))
        qt = (q * e).astype(BF16)
        kt = (k * e).astype(BF16)
        a = a + _dot_nt(qt, kt) * mk_ref[i]
        es.append(e); qts.append(qt); kts.append(kt)
    return a, es, qts, kts


def _hg_specs(nc, rev):
    cc = (lambda c: nc - 1 - c) if rev else (lambda c: c)
    w = 128 * HG_HPS
    blk = lambda off: pl.BlockSpec((CHUNK, w), lambda h, c: (cc(c), h + off))
    vec = pl.BlockSpec((1, w), lambda h, c: (0, h))
    lb2 = pl.BlockSpec((2, w), lambda h, c: (0, h))
    st = pl.BlockSpec((1, HG_HPS, 128, 128), lambda h, c: (cc(c), h, 0, 0))
    consts = [_full((2, CHUNK, CHUNK)), _full((N_LEV + DIAG, CHUNK, CHUNK))]
    return blk, vec, lb2, st, consts


def _hg_fwd(p0, p1, lbl, gw):
    S = p0.shape[0]
    nc = S // CHUNK
    ng = 4 // HG_HPS

    def body(qr_ref, fl_ref, v_ref, z_ref, lbl_ref, gw_ref, tri_ref, mk_ref,
             o_ref, mix_ref, st_ref, state):
        c = pl.program_id(1)

        @pl.when(c == 0)
        def _():
            state[...] = jnp.zeros_like(state)

        lb_all, _ = _hg_lb(lbl_ref)
        heads = [slice(128 * hh, 128 * (hh + 1)) for hh in range(HG_HPS)]
        qs, ks, bs, mats = [], [], [], []
        for sl in heads:
            _, _, g, k, _, q = _hg_gates(qr_ref[:, sl], fl_ref[:, sl], lb_all[:, sl])
            qs.append(q); ks.append(k); bs.append(_tri_dot(tri_ref[0], g))
        for hh in range(HG_HPS):
            a, _, _, _ = _hg_levels(qs[hh], ks[hh], bs[hh], mk_ref)
            mats.append(a + _rowsum(qs[hh] * ks[hh]) * mk_ref[N_LEV])
        for hh, sl in enumerate(heads):
            q, k, b, a, v, z = qs[hh], ks[hh], bs[hh], mats[hh], v_ref[:, sl], z_ref[:, sl]
            b_last = b[CHUNK - 1: CHUNK, :]
            st = state[hh]
            st_ref[0, hh] = st
            o = _dot_nt((q * jnp.exp(b)).astype(BF16), st.astype(BF16)) + _dot(a.astype(BF16), v.astype(BF16))
            state[hh] = st * jnp.exp(b_last) + _dot3(v, k * jnp.exp(b_last - b), _dot_tn)
            o_ref[:, sl] = o
            rs = lax.rsqrt(jnp.mean(o * o, axis=-1, keepdims=True) + EPS)
            mixed = o * rs * gw_ref[:, sl] * (z * _sigmoid(z))
            mix_ref[:, sl] = mixed.astype(BF16)

    blk, vec, lb2, st_spec, consts = _hg_specs(nc, False)
    tris, masks = _hg_const_arrays()
    return pl.pallas_call(
        body, name="hg_fwd", grid=(ng, nc),
        in_specs=[blk(0), blk(ng), blk(0), blk(ng), lb2, vec] + consts,
        out_specs=[blk(0), blk(0), st_spec],
        out_shape=[jax.ShapeDtypeStruct((S, HG_W), F32), jax.ShapeDtypeStruct((S, HG_W), BF16),
                   jax.ShapeDtypeStruct((nc, 4, 128, 128), F32)],
        scratch_shapes=[pltpu.VMEM((HG_HPS, 128, 128), F32)],
        compiler_params=_params(("parallel", "arbitrary")),
    )(p0, p0, p1, p1, lbl, gw, tris, masks)


def _hg_bwd(p0, p1, o_raw, dm, states, lbl, gw):
    S = p0.shape[0]
    nc = S // CHUNK
    ng = 4 // HG_HPS
    w = 128 * HG_HPS

    def body(qr_ref, fl_ref, v_ref, z_ref, o_ref, dm_ref, st_ref, lbl_ref, gw_ref, tri_ref, mk_ref, mkb_ref,
             dqr_ref, dfl_ref, dv_ref, dz_ref, glbl_ref, ggw_ref, dstate, carry, acc_lb, acc_gw):
        c = pl.program_id(1)

        @pl.when(c == 0)
        def _():
            dstate[...] = jnp.zeros_like(dstate)
            carry[...] = jnp.zeros_like(carry)
            acc_lb[...] = jnp.zeros_like(acc_lb)
            acc_gw[...] = jnp.zeros_like(acc_gw)

        lb_all, dlb_dl0 = _hg_lb(lbl_ref)
        heads = [slice(128 * hh, 128 * (hh + 1)) for hh in range(HG_HPS)]
        diag_mask = mk_ref[N_LEV]
        gates, bs, dos = [], [], []
        for sl in heads:
            gt = _hg_gates(qr_ref[:, sl], fl_ref[:, sl], lb_all[:, sl])
            gates.append(gt)
            bs.append(_tri_dot(tri_ref[0], gt[2]))
        for hh, sl in enumerate(heads):
            o, z, dmix, gwv = o_ref[:, sl], z_ref[:, sl], dm_ref[:, sl], gw_ref[:, sl]
            rs = lax.rsqrt(jnp.mean(o * o, axis=-1, keepdims=True) + EPS)
            oh = o * rs
            sz = _sigmoid(z)
            dz_ref[:, sl] = (dmix * (oh * gwv) * (sz * (1.0 + z * (1.0 - sz)))).astype(BF16)
            don = dmix * (z * sz)
            acc_gw[0:1, sl] += jnp.sum(don * oh, axis=0, keepdims=True)
            dy = don * gwv
            dos.append(rs * (dy - oh * jnp.mean(dy * oh, axis=-1, keepdims=True)))
        inter = []
        for hh, sl in enumerate(heads):
            _, _, _, k, _, q = gates[hh]
            b, do, v = bs[hh], dos[hh], v_ref[:, sl]
            b_last = b[CHUNK - 1: CHUNK, :]
            eb = jnp.exp(b)
            edec = jnp.exp(b_last - b)
            dst = dstate[hh]
            dq = _dot3(do, st_ref[0, hh]) * eb
            dk = _dot3(v, dst) * edec
            da = _dot_nt(do.astype(BF16), v.astype(BF16))
            dv_state = _dot_nt((k * edec).astype(BF16), dst.astype(BF16))
            dstate[hh] = dst * jnp.exp(b_last) + _dot3(do, q * eb, _dot_tn)
            inter.append((dq, dk, da, dv_state))
        for hh, sl in enumerate(heads):
            sig, f, _, k, sq, q = gates[hh]
            dq, dk, da, dv_state = inter[hh]
            b = bs[hh]
            db = q * dq - k * dk
            dab = da.astype(BF16)
            refs = _level_refs(b)
            a = _rowsum(q * k) * diag_mask
            for i in range(N_LEV):
                diff = b - refs[i]
                e = jnp.exp(jnp.minimum(diff, -diff))
                qt = (q * e).astype(BF16)
                kt = (k * e).astype(BF16)
                a = a + _dot_nt(qt, kt) * mk_ref[i]
                dam = dab * mkb_ref[i]
                gq = _dot(dam, kt)
                gk = _dot_tn(dam, qt)
                dq = dq + e * gq
                dk = dk + e * gk
                db = db + (qt.astype(F32) * gq - kt.astype(F32) * gk)
            dc = _rowsum(da * diag_mask)
            dq = dq + dc * k
            dk = dk + dc * q
            dv_ref[:, sl] = (_dot_tn(a.astype(BF16), dos[hh].astype(BF16)) + dv_state).astype(BF16)
            dg = _tri_dot(tri_ref[1], db) + carry[0:1, sl]
            carry[0:1, sl] += jnp.sum(db, axis=0, keepdims=True)
            lb = lb_all[:, sl]
            qr = qr_ref[:, sl]
            t = dg / f - dk
            dfl_ref[:, sl] = (t * (1.0 - lb) * sig * (1.0 - sig)).astype(BF16)
            acc_lb[0:1, sl] += jnp.sum(t * (1.0 - sig), axis=0, keepdims=True)
            dqr_ref[:, sl] = (dq * (sq * (1.0 + qr * (1.0 - sq)))).astype(BF16)

        @pl.when(c == nc - 1)
        def _():
            gl0 = acc_lb[0:1, :] * dlb_dl0
            glbl_ref[0:1, :] = gl0
            glbl_ref[1:2, :] = -gl0
            ggw_ref[...] = acc_gw[0:1, :]

    blk, vec, lb2, st_spec, consts = _hg_specs(nc, True)
    tris, masks = _hg_const_arrays()
    act = jax.ShapeDtypeStruct((S, HG_W), BF16)
    return pl.pallas_call(
        body, name="hg_bwd", grid=(ng, nc),
        in_specs=[blk(0), blk(ng), blk(0), blk(ng), blk(0), blk(0), st_spec, lb2, vec] + consts + consts[1:],
        out_specs=[blk(0), blk(0), blk(0), blk(0), lb2, vec],
        out_shape=[act, act, act, act, jax.ShapeDtypeStruct((2, HG_W), F32), jax.ShapeDtypeStruct((1, HG_W), F32)],
        scratch_shapes=[pltpu.VMEM((HG_HPS, 128, 128), F32), pltpu.VMEM((8, w), F32),
                        pltpu.VMEM((8, w), F32), pltpu.VMEM((8, w), F32)],
        compiler_params=_params(("parallel", "arbitrary")),
    )(p0, p0, p1, p1, o_raw, dm, states, lbl, gw, tris, masks, masks.astype(BF16))


def _stat_head(lane):
    return lane >> 4


def _stat_lane(lane):
    return lane & (STAT_LANES - 1)


def _o_head(lane):
    return lane >> 6


def _att_bias():
    qi = np.arange(ATT_BLK)[:, None]
    kj = np.arange(2 * ATT_BLK)[None, :]
    band = (kj >= qi) & (kj <= qi + ATT_BLK)
    qm = np.stack([band & (kj >= ATT_BLK), band])
    cur = (kj < ATT_BLK) & (qi <= kj)
    km = np.stack([cur, cur | ((kj >= ATT_BLK) & (qi >= kj - ATT_BLK))])
    to_bias = lambda m: jnp.asarray(np.where(m, 0.0, NEG), F32)
    return to_bias(qm), to_bias(km)


def _att_fwd(qkv_d):
    d, L, _ = qkv_d.shape
    qb = ATT_QB
    rows = qb * ATT_BLK

    def body(q_ref, kp_ref, kc_ref, vp_ref, vc_ref, bias_ref, o_ref, lse_ref, s_scr, p_scr):
        first = _iota((ATT_BLK, HEAD_PAIR), 1) < 64
        head_of_lane = _stat_head(_iota((ATT_BLK, STAT_W), 1))
        pairs = [slice(HEAD_PAIR * hp, HEAD_PAIR * (hp + 1)) for hp in range(4)]
        blk = lambda i: slice(ATT_BLK * i, ATT_BLK * (i + 1))

        def keys(i, sl, prev_ref, cur_ref):
            before = prev_ref[0, :, sl] if i == 0 else cur_ref[0, blk(i - 1), sl]
            return jnp.concatenate([before, cur_ref[0, blk(i), sl]], axis=0)

        for i in range(qb):
            for hp, sl in enumerate(pairs):
                q2 = q_ref[0, blk(i), sl] * ATT_SCALE
                zero = jnp.zeros_like(q2)
                qs = jnp.concatenate([jnp.where(first, q2, zero), jnp.where(first, zero, q2)], axis=0)
                s_scr[4 * i + hp] = _dot_nt(qs, keys(i, sl, kp_ref, kc_ref))
        stats = []
        for i in range(qb):
            bias = bias_ref[jnp.minimum(pl.program_id(1), 1)] if i == 0 else bias_ref[1]
            bias2 = jnp.concatenate([bias, bias], axis=0)
            for hp in range(4):
                s = s_scr[4 * i + hp] + bias2
                m = jnp.max(s, axis=-1, keepdims=True)
                p = jnp.exp(s - m)
                l = jnp.sum(p, axis=-1, keepdims=True)
                p_scr[4 * i + hp] = p.astype(BF16)
                stats.append((l, m + jnp.log(l)))
        for i in range(qb):
            lse_blk = jnp.zeros((ATT_BLK, STAT_W), F32)
            for hp, sl in enumerate(pairs):
                l, lse = stats[4 * i + hp]
                o = _dot(p_scr[4 * i + hp], keys(i, sl, vp_ref, vc_ref)) / l
                o_ref[0, blk(i), sl] = jnp.where(first, o[:ATT_BLK], o[ATT_BLK:]).astype(BF16)
                lse_blk = jnp.where(head_of_lane == 2 * hp, lse[:ATT_BLK],
                                    jnp.where(head_of_lane == 2 * hp + 1, lse[ATT_BLK:], lse_blk))
            lse_ref[0, blk(i), :] = lse_blk

    cur = lambda j: pl.BlockSpec((1, rows, AT_W), lambda r, n: (r, n, j))
    prev = lambda j: pl.BlockSpec((1, ATT_BLK, AT_W), lambda r, n: (r, jnp.maximum(qb * n - 1, 0), j))
    return pl.pallas_call(
        body, name=f"att_fwd_d{d}", grid=(d, L // rows),
        in_specs=[cur(0), prev(1), cur(1), prev(2), cur(2), _full((2, ATT_BLK, 2 * ATT_BLK))],
        out_specs=[pl.BlockSpec((1, rows, AT_W), lambda r, n: (r, n, 0)), pl.BlockSpec((1, rows, STAT_W), lambda r, n: (r, n, 0))],
        out_shape=[jax.ShapeDtypeStruct((d, L, AT_W), BF16), jax.ShapeDtypeStruct((d, L, STAT_W), F32)],
        scratch_shapes=[pltpu.VMEM((4 * qb, 2 * ATT_BLK, 2 * ATT_BLK), F32), pltpu.VMEM((4 * qb, 2 * ATT_BLK, 2 * ATT_BLK), BF16)],
        compiler_params=_params(("parallel", "parallel")),
    )(qkv_d, qkv_d, qkv_d, qkv_d, qkv_d, _att_bias()[0])


def _att_combine(os_d, ls_d, p3):
    S = p3.shape[0]
    tm = 512

    def body(oa_ref, ob4_ref, oc16_ref, la_ref, lb4_ref, lc16_ref, z_ref,
             oat_ref, lse_ref, mix_ref, ob_scr, oc_scr, lb_scr, lc_scr):
        _merge_residues(ob4_ref, ob_scr, 4)
        _merge_residues(oc16_ref, oc_scr, 16)
        _merge_residues(lb4_ref, lb_scr, 4)
        _merge_residues(lc16_ref, lc_scr, 16)
        ls = (la_ref[...], lb_scr[0], lc_scr[0])
        mx = jnp.maximum(jnp.maximum(ls[0], ls[1]), ls[2])
        es = [jnp.exp(l - mx) for l in ls]
        zs = es[0] + es[1] + es[2]
        lse_ref[...] = mx + jnp.log(zs)
        spread = ((_o_head(_iota((STAT_W, AT_W), 1)) == _stat_head(_iota((STAT_W, AT_W), 0)))
                  & (_stat_lane(_iota((STAT_W, AT_W), 0)) == 0)).astype(BF16)
        os_ = (oa_ref[...].astype(F32), _from_slabs(ob_scr), _from_slabs(oc_scr))
        o = jnp.zeros((tm, AT_W), F32)
        for e, oi in zip(es, os_):
            hi, lo = _split2(e / zs)
            o = o + (_dot(hi, spread) + _dot(lo, spread)) * oi
        oat_ref[...] = o
        z = z_ref[...]
        mixed = o * (z * _sigmoid(z))
        mix_ref[...] = mixed.astype(BF16)

    row = lambda w: pl.BlockSpec((tm, w), lambda s: (s, 0))
    return pl.pallas_call(
        body, name="att_combine", grid=(S // tm,),
        in_specs=[row(AT_W), _dil_spec(4, tm, AT_W), _dil_spec(16, tm, AT_W),
                  row(STAT_W), _dil_spec(4, tm, STAT_W), _dil_spec(16, tm, STAT_W),
                  pl.BlockSpec((tm, AT_W), lambda s: (s, 1))],
        out_specs=[row(AT_W), row(STAT_W), row(AT_W)],
        out_shape=[jax.ShapeDtypeStruct((S, AT_W), F32), jax.ShapeDtypeStruct((S, STAT_W), F32),
                   jax.ShapeDtypeStruct((S, AT_W), BF16)],
        scratch_shapes=[_slab_scratch(tm, AT_W), _slab_scratch(tm, AT_W), _slab_scratch(tm, STAT_W), _slab_scratch(tm, STAT_W)],
        compiler_params=_params(("parallel",)),
    )(os_d[0].reshape(S, AT_W), os_d[1], os_d[2], ls_d[0].reshape(S, STAT_W), ls_d[1], ls_d[2], p3)


def _att_gate_bwd(dm_at, o_at, lse, p3):
    S = p3.shape[0]
    tm = 512

    def body(dm_ref, o_ref, l_ref, z_ref, do_ref, do4_ref, do16_ref, dl_ref, dl4_ref, dl16_ref, dz_ref, do_scr, dl_scr):
        o = o_ref[...]
        z = z_ref[...]
        dm = dm_ref[...]
        sz = _sigmoid(z)
        dz_ref[...] = (dm * o * (sz * (1.0 + z * (1.0 - sz)))).astype(BF16)
        do = dm * (z * sz)
        do_ref[...] = do.astype(BF16)
        gather = (_o_head(_iota((AT_W, STAT_W), 0)) == _stat_head(_iota((AT_W, STAT_W), 1))).astype(BF16)
        dl = jnp.where(_stat_lane(_iota((tm, STAT_W), 1)) < STAT_LSE_LANE, _dot_sel(do * o, gather), l_ref[...])
        dl_ref[...] = dl
        _to_slabs(do, do_scr)
        _to_slabs(dl, dl_scr)
        _split_residues(do_scr, do4_ref, 4, BF16)
        _split_residues(do_scr, do16_ref, 16, BF16)
        _split_residues(dl_scr, dl4_ref, 4, F32)
        _split_residues(dl_scr, dl16_ref, 16, F32)

    row = lambda w: pl.BlockSpec((tm, w), lambda s: (s, 0))
    sds = jax.ShapeDtypeStruct
    return pl.pallas_call(
        body, name="att_gate_bwd", grid=(S // tm,),
        in_specs=[row(AT_W), row(AT_W), row(STAT_W), pl.BlockSpec((tm, AT_W), lambda s: (s, 1))],
        out_specs=[row(AT_W), _dil_spec(4, tm, AT_W), _dil_spec(16, tm, AT_W),
                   row(STAT_W), _dil_spec(4, tm, STAT_W), _dil_spec(16, tm, STAT_W), row(AT_W)],
        out_shape=[sds((S, AT_W), BF16), sds((4, S // 4, AT_W), BF16), sds((16, S // 16, AT_W), BF16),
                   sds((S, STAT_W), F32), sds((4, S // 4, STAT_W), F32), sds((16, S // 16, STAT_W), F32),
                   sds((S, AT_W), BF16)],
        scratch_shapes=[_slab_scratch(tm, AT_W), _slab_scratch(tm, STAT_W)],
        compiler_params=_params(("parallel",)),
    )(dm_at, o_at, lse, p3)


def _att_bwd(qkv_d, do_d, dl_d):
    d, L, _ = qkv_d.shape
    nb = L // ATT_BLK
    qb = ATT_QB
    rows = qb * ATT_BLK
    nsteps = L // rows

    def body(qc_ref, qn_ref, kp_ref, kc_ref, vp_ref, vc_ref, ac_ref, an_ref, lc_ref, ln_ref, bq_ref, bk_ref, dqkv_ref,
             s_scr, dp_scr, st_scr, dpt_scr, ds_scr, pt_scr, dst_scr):
        n = pl.program_id(1)
        first = _iota((ATT_BLK, HEAD_PAIR), 1) < 64
        pairs = [slice(HEAD_PAIR * hp, HEAD_PAIR * (hp + 1)) for hp in range(4)]
        blk = lambda i: slice(ATT_BLK * i, ATT_BLK * (i + 1))

        def stack(t):
            zero = jnp.zeros_like(t)
            return jnp.concatenate([jnp.where(first, t, zero), jnp.where(first, zero, t)], axis=0)

        def unstack(t2):
            return jnp.where(first, t2[:ATT_BLK], t2[ATT_BLK:])

        def with_prev(i, sl, prev_ref, cur_ref):
            before = prev_ref[0, :, sl] if i == 0 else cur_ref[0, blk(i - 1), sl]
            return jnp.concatenate([before, cur_ref[0, blk(i), sl]], axis=0)

        def with_next(i, sl, cur_ref, next_ref):
            after = next_ref[0, :, sl] if i == qb - 1 else cur_ref[0, blk(i + 1), sl]
            return jnp.concatenate([cur_ref[0, blk(i), sl], after], axis=0)

        for i in range(qb):
            for hp, sl in enumerate(pairs):
                j = 4 * i + hp
                s_scr[j] = _dot_nt(stack(qc_ref[0, blk(i), sl] * ATT_SCALE), with_prev(i, sl, kp_ref, kc_ref))
                dp_scr[j] = _dot_nt(stack(ac_ref[0, blk(i), sl]), with_prev(i, sl, vp_ref, vc_ref))
                st_scr[j] = _dot_nt(stack(kc_ref[0, blk(i), sl] * ATT_SCALE), with_next(i, sl, qc_ref, qn_ref))
                dpt_scr[j] = _dot_nt(stack(vc_ref[0, blk(i), sl]), with_next(i, sl, ac_ref, an_ref))
        for i in range(qb):
            bias = bq_ref[jnp.minimum(n, 1)] if i == 0 else bq_ref[1]
            bias_t = bk_ref[jnp.minimum(nsteps - 1 - n, 1)] if i == qb - 1 else bk_ref[1]
            bias2 = jnp.concatenate([bias, bias], axis=0)
            bias_t2 = jnp.concatenate([bias_t, bias_t], axis=0)
            dl_c = lc_ref[0, blk(i), :]
            dl_t = with_next(i, slice(None), lc_ref, ln_ref).T
            for hp in range(4):
                j = 4 * i + hp
                at = [STAT_LANES * (2 * hp), STAT_LANES * (2 * hp + 1)]
                col = lambda t, o: jnp.concatenate([t[:, a + o: a + o + 1] for a in at], axis=0)
                p = jnp.exp(s_scr[j] + bias2 - col(dl_c, STAT_LSE_LANE))
                ds_scr[j] = (p * (dp_scr[j] - col(dl_c, 0))).astype(BF16)
                row = lambda t, o: jnp.concatenate([jnp.broadcast_to(t[a + o: a + o + 1, :], (ATT_BLK, 2 * ATT_BLK)) for a in at], axis=0)
                pt = jnp.exp(st_scr[j] + bias_t2 - row(dl_t, STAT_LSE_LANE))
                pt_scr[j] = pt.astype(BF16)
                dst_scr[j] = (pt * (dpt_scr[j] - row(dl_t, 0))).astype(BF16)
        for i in range(qb):
            for hp, sl in enumerate(pairs):
                j = 4 * i + hp
                dq = unstack(_dot(ds_scr[j], with_prev(i, sl, kp_ref, kc_ref))) * ATT_SCALE
                dk = unstack(_dot(dst_scr[j], with_next(i, sl, qc_ref, qn_ref))) * ATT_SCALE
                dv = unstack(_dot(pt_scr[j], with_next(i, sl, ac_ref, an_ref)))
                dqkv_ref[0, blk(i), sl] = dq.astype(BF16)
                dqkv_ref[0, blk(i), AT_W + HEAD_PAIR * hp: AT_W + HEAD_PAIR * (hp + 1)] = dk.astype(BF16)
                dqkv_ref[0, blk(i), 2 * AT_W + HEAD_PAIR * hp: 2 * AT_W + HEAD_PAIR * (hp + 1)] = dv.astype(BF16)

    cur = lambda j: pl.BlockSpec((1, rows, AT_W), lambda r, n: (r, n, j))
    prev = lambda j: pl.BlockSpec((1, ATT_BLK, AT_W), lambda r, n: (r, jnp.maximum(qb * n - 1, 0), j))
    nxt_blk = lambda n: jnp.minimum(qb * (n + 1), nb - 1)
    sq = (4 * qb, 2 * ATT_BLK, 2 * ATT_BLK)
    return pl.pallas_call(
        body, name=f"att_bwd_d{d}", grid=(d, nsteps),
        in_specs=[cur(0), pl.BlockSpec((1, ATT_BLK, AT_W), lambda r, n: (r, nxt_blk(n), 0)), prev(1), cur(1), prev(2), cur(2),
                  pl.BlockSpec((1, rows, AT_W), lambda r, n: (r, n, 0)),
                  pl.BlockSpec((1, ATT_BLK, AT_W), lambda r, n: (r, nxt_blk(n), 0)),
                  pl.BlockSpec((1, rows, STAT_W), lambda r, n: (r, n, 0)),
                  pl.BlockSpec((1, ATT_BLK, STAT_W), lambda r, n: (r, nxt_blk(n), 0)),
                  _full((2, ATT_BLK, 2 * ATT_BLK)), _full((2, ATT_BLK, 2 * ATT_BLK))],
        out_specs=pl.BlockSpec((1, rows, 3 * AT_W), lambda r, n: (r, n, 0)),
        out_shape=jax.ShapeDtypeStruct((d, L, 3 * AT_W), BF16),
        scratch_shapes=[pltpu.VMEM(sq, F32)] * 4 + [pltpu.VMEM(sq, BF16)] * 3,
        compiler_params=_params(("parallel", "parallel")),
    )(qkv_d, qkv_d, qkv_d, qkv_d, qkv_d, qkv_d, do_d, do_d, dl_d, dl_d, *_att_bias())


def _att_bwd_combine(dqkvs, cos2, sin2):
    S = dqkvs[0].shape[1]
    tm = 512

    def body(a_ref, b4_ref, c16_ref, cos_ref, sin_ref, dq_ref, dk_ref, dv_ref, b_scr, c_scr):
        _merge_residues(b4_ref, b_scr, 4)
        _merge_residues(c16_ref, c_scr, 16)
        t = a_ref[...].astype(F32) + _from_slabs(b_scr) + _from_slabs(c_scr)
        dy = t[:, : 2 * AT_W]
        cosf = jnp.tile(cos_ref[...], (1, 8))
        sinf = jnp.tile(sin_ref[...], (1, 8))
        dx = dy * cosf - _rope_rot(dy) * sinf
        dq_ref[...] = dx[:, :AT_W].astype(BF16)
        dk_ref[...] = dx[:, AT_W:].astype(BF16)
        dv_ref[...] = t[:, 2 * AT_W:].astype(BF16)

    row = lambda w: pl.BlockSpec((tm, w), lambda s: (s, 0))
    act = jax.ShapeDtypeStruct((S, AT_W), BF16)
    return pl.pallas_call(
        body, name="att_bwd_combine", grid=(S // tm,),
        in_specs=[row(3 * AT_W), _dil_spec(4, tm, 3 * AT_W), _dil_spec(16, tm, 3 * AT_W), row(128), row(128)],
        out_specs=[row(AT_W), row(AT_W), row(AT_W)],
        out_shape=[act, act, act],
        scratch_shapes=[_slab_scratch(tm, 3 * AT_W), _slab_scratch(tm, 3 * AT_W)],
        compiler_params=_params(("parallel",)),
    )(dqkvs[0].reshape(S, 3 * AT_W), dqkvs[1], dqkvs[2], cos2, sin2)


def _outproj(x2, tgt2, mix_hg, mix_at, w_out_full, fnw):
    S = x2.shape[0]
    tm = PROJ_TM
    ns = S // tm

    def body(x_ref, t_ref, mh_ref, ma_ref, w_ref, fw_ref,
             dh_ref, dmh_ref, dma_ref, gw_ref, gfw_ref, loss_ref):
        s = pl.program_id(0)

        @pl.when(s == 0)
        def _():
            gw_ref[...] = jnp.zeros_like(gw_ref)
            gfw_ref[...] = jnp.zeros_like(gfw_ref)
            loss_ref[...] = jnp.zeros_like(loss_ref)

        y = _dot(mh_ref[...], w_ref[:HG_W, :]) + _dot(ma_ref[...], w_ref[HG_W:, :])
        h = x_ref[...] + y
        r = lax.rsqrt(jnp.mean(h * h, axis=-1, keepdims=True) + EPS)
        hn = h * r
        fw = fw_ref[...]
        err = hn * fw - t_ref[...]
        loss_ref[...] += 0.5 * jnp.sum(jnp.mean(err * err, axis=-1, keepdims=True))
        dout = err * (1.0 / D_MODEL)
        gfw_ref[...] += jnp.sum(dout * hn, axis=0, keepdims=True)
        dhn = dout * fw
        dh = r * (dhn - hn * jnp.mean(dhn * hn, axis=-1, keepdims=True))
        dh_ref[...] = dh
        dhb = dh.astype(BF16)
        dmh_ref[...] = _dot_nt(dhb, w_ref[:HG_W, :])
        dma_ref[...] = _dot_nt(dhb, w_ref[HG_W:, :])
        gw_ref[:HG_W, :] += _dot_tn(mh_ref[...], dhb)
        gw_ref[HG_W:, :] += _dot_tn(ma_ref[...], dhb)

    row = lambda w: pl.BlockSpec((tm, w), lambda s: (s, 0))
    return pl.pallas_call(
        body, name="outproj", grid=(ns,),
        in_specs=[row(D_MODEL), row(D_MODEL), row(HG_W), row(AT_W),
                  _full((D_MODEL, D_MODEL)), _full((1, D_MODEL))],
        out_specs=[row(D_MODEL), row(HG_W), row(AT_W), _full((D_MODEL, D_MODEL)), _full((1, D_MODEL)), _full((8, 128))],
        out_shape=[jax.ShapeDtypeStruct((S, D_MODEL), F32), jax.ShapeDtypeStruct((S, HG_W), F32),
                   jax.ShapeDtypeStruct((S, AT_W), F32), jax.ShapeDtypeStruct((D_MODEL, D_MODEL), F32),
                   jax.ShapeDtypeStruct((1, D_MODEL), F32), jax.ShapeDtypeStruct((8, 128), F32)],
        compiler_params=_params(("arbitrary",)),
    )(x2, tgt2, mix_hg, mix_at, w_out_full, fnw)


def _inproj_bwd_x(dps, w_in_full, x2, norm_w, dh):
    S = x2.shape[0]
    tm = PROJ_TM

    def body(d0, d1, d2, d3, d4, d5, d6, d7, w_ref, x_ref, nw_ref, dh_ref, gx_ref, gnw_ref):
        s = pl.program_id(0)

        @pl.when(s == 0)
        def _():
            gnw_ref[...] = jnp.zeros_like(gnw_ref)

        du = jnp.zeros((tm, D_MODEL), F32)
        for i, dref in enumerate((d0, d1, d2, d3, d4, d5, d6, d7)):
            j, half = divmod(i, 2)
            du = du + _dot_nt(dref[...], w_ref[j, :, 512 * half: 512 * (half + 1)])
        x = x_ref[...]
        r = lax.rsqrt(jnp.mean(x * x, axis=-1, keepdims=True) + EPS)
        xh = x * r
        gnw_ref[...] += jnp.sum(du * xh, axis=0, keepdims=True)
        dun = du * nw_ref[...]
        gx_ref[...] = dh_ref[...] + r * (dun - xh * jnp.mean(dun * xh, axis=-1, keepdims=True))

    row = lambda w: pl.BlockSpec((tm, w), lambda s: (s, 0))
    return pl.pallas_call(
        body, name="inproj_bwd_x", grid=(S // tm,),
        in_specs=[row(512)] * 8 + [_full((4, D_MODEL, 1024)), row(D_MODEL), _full((1, D_MODEL)), row(D_MODEL)],
        out_specs=[row(D_MODEL), _full((1, D_MODEL))],
        out_shape=[jax.ShapeDtypeStruct((S, D_MODEL), F32), jax.ShapeDtypeStruct((1, D_MODEL), F32)],
        compiler_params=_params(("arbitrary",)),
    )(*dps, w_in_full, x2, norm_w, dh)


def _inproj_bwd_w(u, dps):
    S = u.shape[0]
    tm = PROJ_TM

    def body(u_ref, d0, d1, d2, d3, d4, d5, d6, d7, g_ref):
        @pl.when(pl.program_id(0) == 0)
        def _():
            g_ref[...] = jnp.zeros_like(g_ref)

        ub = u_ref[...]
        for i, dref in enumerate((d0, d1, d2, d3, d4, d5, d6, d7)):
            j, half = divmod(i, 2)
            g_ref[j, :, 512 * half: 512 * (half + 1)] += _dot_tn(ub, dref[...])

    return pl.pallas_call(
        body, name="inproj_bwd_w", grid=(S // tm,),
        in_specs=[pl.BlockSpec((tm, D_MODEL), lambda s: (s, 0))] + [pl.BlockSpec((tm, 512), lambda s: (s, 0))] * 8,
        out_specs=_full((4, D_MODEL, 1024)),
        out_shape=jax.ShapeDtypeStruct((4, D_MODEL, 1024), F32),
        compiler_params=_params(("arbitrary",)),
    )(u, *dps)


def _adamw_update(gg, w_ref, m_ref, v_ref, d_ref, nm_ref, nv_ref):
    nm = ADAM_B1 * m_ref[...] + (1.0 - ADAM_B1) * gg
    nv = ADAM_B2 * v_ref[...] + (1.0 - ADAM_B2) * (gg * gg)
    m_hat = nm / (1.0 - ADAM_B1 ** ADAM_STEP)
    v_hat = nv / (1.0 - ADAM_B2 ** ADAM_STEP)
    d_ref[...] = -ADAM_LR * (m_hat / (jnp.sqrt(v_hat) + ADAM_EPS) + ADAM_WD * w_ref[...])
    nm_ref[...] = nm
    nv_ref[...] = nv


def _adamw(w, g, m, v, name):
    rows, cols = w.shape
    tr = min(rows, 256)

    def body(w_ref, g_ref, m_ref, v_ref, d_ref, nm_ref, nv_ref):
        _adamw_update(g_ref[...], w_ref, m_ref, v_ref, d_ref, nm_ref, nv_ref)

    spec = pl.BlockSpec((tr, cols), lambda i: (i, 0))
    sds = jax.ShapeDtypeStruct((rows, cols), F32)
    return pl.pallas_call(
        body, name=name, grid=(rows // tr,),
        in_specs=[spec] * 4, out_specs=[spec] * 3, out_shape=[sds] * 3,
        compiler_params=_params(("parallel",)),
    )(w, g, m, v)


def _place():
    return lax.axis_index("x"), lax.axis_index("y"), lax.axis_index("c")


def _gather_weights(w_in_s, w_out_s):
    def body(win_ref, wout_ref, fin_ref, fout_ref, send_sems, recv_sems):
        x, y, c = _place()
        me = (x, y, c)
        sib = (x, y, 1 - c)
        mine = 2 * x + y
        fin_ref[mine] = win_ref[...].astype(BF16)
        fout_ref[mine] = wout_ref[...].astype(BF16)
        chips = [(1 - x, y), (x, 1 - y), (1 - x, 1 - y)]

        def halves(chip, half):
            return (fin_ref.at[chip, pl.ds(half * 512, 512), :], fout_ref.at[chip, pl.ds(half * 128, 128), :])

        def copy(k, ref, to):
            return pltpu.make_async_remote_copy(src_ref=ref, dst_ref=ref, send_sem=send_sems.at[k],
                                                recv_sem=recv_sems.at[k], device_id=to, device_id_type=MESH)

        first, passed = [], []
        for j, (cx, cy) in enumerate(chips):
            for a, ref in enumerate(halves(mine, c)):
                first.append(copy(2 * j + a, ref, (cx, cy, c)))
        for cp in first:
            cp.start()
        for j, (cx, cy) in enumerate(chips):
            for a, ref in enumerate(halves(2 * cx + cy, c)):
                copy(2 * j + a, ref, me).wait_recv()
                fwd = copy(6 + 2 * j + a, ref, sib)
                fwd.start()
                passed.append(fwd)
        for j, (cx, cy) in enumerate(chips):
            for a, ref in enumerate(halves(2 * cx + cy, 1 - c)):
                copy(6 + 2 * j + a, ref, me).wait_recv()
        for cp in first + passed:
            cp.wait_send()

    vm = pl.BlockSpec(memory_space=pltpu.VMEM)
    return pl.pallas_call(
        body, name="gather_weights",
        in_specs=[vm, vm], out_specs=[vm, vm],
        out_shape=[jax.ShapeDtypeStruct((4, D_MODEL, 1024), BF16), jax.ShapeDtypeStruct((4, 256, D_MODEL), BF16)],
        scratch_shapes=[pltpu.SemaphoreType.DMA((12,)), pltpu.SemaphoreType.DMA((12,))],
        compiler_params=pltpu.CompilerParams(vmem_limit_bytes=VMEM_LIMIT),
    )(w_in_s, w_out_s)


def _swap_halves(g_in, g_out):
    def body(gin_ref, gout_ref, rin_ref, rout_ref, send_sems, recv_sems):
        x, y, c = _place()
        sib = (x, y, 1 - c)
        cps = [pltpu.make_async_remote_copy(src_ref=src.at[:, 1 - c], dst_ref=dst, send_sem=send_sems.at[k],
                                            recv_sem=recv_sems.at[k], device_id=sib, device_id_type=MESH)
               for k, (src, dst) in enumerate(((gin_ref, rin_ref), (gout_ref, rout_ref)))]
        for cp in cps:
            cp.start()
        for cp in cps:
            cp.wait()

    hbm = pl.BlockSpec(memory_space=pl.ANY)
    return pl.pallas_call(
        body, name="swap_halves",
        in_specs=[hbm, hbm], out_specs=[hbm, hbm],
        out_shape=[jax.ShapeDtypeStruct((4,) + g.shape[2:], F32) for g in (g_in, g_out)],
        scratch_shapes=[pltpu.SemaphoreType.DMA((2,)), pltpu.SemaphoreType.DMA((2,))],
    )(g_in, g_out)


def _add_half(g, r, cidx, name):
    n, _, rows, cols = g.shape

    def body(c_ref, g_ref, r_ref, o_ref):
        o_ref[0] = (g_ref[0, 0] + r_ref[0]).astype(BF16)

    return pl.pallas_call(
        body, name=name,
        grid_spec=pltpu.PrefetchScalarGridSpec(
            num_scalar_prefetch=1, grid=(n,),
            in_specs=[pl.BlockSpec((1, 1, rows, cols), lambda j, c_ref: (j, c_ref[0], 0, 0)),
                      pl.BlockSpec((1, rows, cols), lambda j, c_ref: (j, 0, 0))],
            out_specs=pl.BlockSpec((1, rows, cols), lambda j, c_ref: (j, 0, 0))),
        out_shape=jax.ShapeDtypeStruct((n, rows, cols), BF16),
        compiler_params=_params(("parallel",)),
    )(cidx, g, r)


def _exchange_copies(in_ref, out_ref, lin_ref, lout_ref, send_sems, recv_sems):
    x, y, c = _place()
    cps = []
    for k, (cx, cy) in enumerate([(1 - x, y), (x, 1 - y), (1 - x, 1 - y)]):
        for a, (src, dst) in enumerate(((in_ref, lin_ref), (out_ref, lout_ref))):
            cps.append(pltpu.make_async_remote_copy(
                src_ref=src.at[2 * cx + cy], dst_ref=dst.at[k], send_sem=send_sems.at[2 * k + a],
                recv_sem=recv_sems.at[2 * k + a], device_id=(cx, cy, c), device_id_type=MESH))
    return cps


def _exchange_start(cs_in, cs_out):
    def body(in_ref, out_ref, lin_ref, lout_ref, send_sems, recv_sems, in_thru, out_thru, lin_thru, lout_thru, token):
        for cp in _exchange_copies(in_ref, out_ref, lin_ref, lout_ref, send_sems, recv_sems):
            cp.start()
        token[...] = jnp.zeros_like(token)

    lands = [lax.empty((3,) + a.shape[1:], a.dtype) for a in (cs_in, cs_out)]
    bufs = [pltpu.with_memory_space_constraint(a, pltpu.HBM) for a in (cs_in, cs_out, *lands)]
    hbm = pl.BlockSpec(memory_space=pltpu.HBM)
    sem = pl.BlockSpec(memory_space=pltpu.SEMAPHORE)
    return pl.pallas_call(
        body, name="exchange_start",
        in_specs=[hbm] * 4,
        out_specs=[sem, sem, hbm, hbm, hbm, hbm, pl.BlockSpec(memory_space=pltpu.VMEM)],
        out_shape=[pltpu.SemaphoreType.DMA((6,)), pltpu.SemaphoreType.DMA((6,))]
                  + [pltpu.HBM(b.shape, b.dtype) for b in bufs] + [jax.ShapeDtypeStruct((8, 128), F32)],
        input_output_aliases={0: 2, 1: 3, 2: 4, 3: 5},
        compiler_params=pltpu.CompilerParams(has_side_effects=pltpu.SideEffectType.DATAFLOW_SIDE_EFFECTING),
    )(*bufs)


def _exchange_wait(send_sems, recv_sems, in_thru, out_thru, lin_thru, lout_thru, after):
    def body(in_ref, out_ref, lin_ref, lout_ref, send_sems, recv_sems, after_ref, in_dead, out_dead, got_in, got_out):
        for cp in _exchange_copies(in_ref, out_ref, lin_ref, lout_ref, send_sems, recv_sems):
            cp.wait_send()
            cp.wait_recv()

    hbm = pl.BlockSpec(memory_space=pltpu.HBM)
    sem = pl.BlockSpec(memory_space=pltpu.SEMAPHORE)
    bufs = (in_thru, out_thru, lin_thru, lout_thru)
    return pl.pallas_call(
        body, name="exchange_wait",
        in_specs=[hbm] * 4 + [sem, sem, pl.BlockSpec(memory_space=pl.ANY)],
        out_specs=[hbm] * 4,
        out_shape=[pltpu.HBM(b.shape, b.dtype) for b in bufs],
        input_output_aliases={0: 0, 1: 1, 2: 2, 3: 3},
        compiler_params=pltpu.CompilerParams(has_side_effects=pltpu.SideEffectType.DATAFLOW_SIDE_EFFECTING),
    )(*bufs, send_sems, recv_sems, after)


PEER_XOR = (2, 1, 3)


def _sum_chips(cs, r, chip_idx, name):
    _, rows, cols = r.shape
    tr = min(rows, 256)

    def body(m_ref, cs_ref, r_ref, o_ref):
        mine = m_ref[0]
        own = cs_ref[0].astype(F32)
        got = [r_ref[k].astype(F32) for k in range(3)]
        acc = None
        for s in range(4):
            rel = mine ^ s
            term = jnp.where(rel == 0, own, jnp.where(rel == PEER_XOR[0], got[0],
                                                      jnp.where(rel == PEER_XOR[1], got[1], got[2])))
            acc = term if acc is None else acc + term
        o_ref[...] = acc

    return pl.pallas_call(
        body, name=name,
        grid_spec=pltpu.PrefetchScalarGridSpec(
            num_scalar_prefetch=1, grid=(rows // tr,),
            in_specs=[pl.BlockSpec((1, tr, cols), lambda i, m_ref: (m_ref[0], i, 0)),
                      pl.BlockSpec((3, tr, cols), lambda i, m_ref: (0, i, 0))],
            out_specs=pl.BlockSpec((tr, cols), lambda i, m_ref: (i, 0))),
        out_shape=jax.ShapeDtypeStruct((rows, cols), F32),
        compiler_params=_params(("parallel",)),
    )(chip_idx, cs, r)


def _swap_reduced(h_in, h_out):
    def body(in_ref, out_ref, rin_ref, rout_ref, send_sems, recv_sems):
        x, y, c = _place()
        cps = [pltpu.make_async_remote_copy(src_ref=src, dst_ref=dst, send_sem=send_sems.at[k],
                                            recv_sem=recv_sems.at[k], device_id=(x, y, 1 - c), device_id_type=MESH)
               for k, (src, dst) in enumerate(((in_ref, rin_ref), (out_ref, rout_ref)))]
        for cp in cps:
            cp.start()
        for cp in cps:
            cp.wait()

    hbm = pl.BlockSpec(memory_space=pl.ANY)
    return pl.pallas_call(
        body, name="swap_reduced",
        in_specs=[hbm, hbm], out_specs=[hbm, hbm],
        out_shape=[jax.ShapeDtypeStruct(h.shape, F32) for h in (h_in, h_out)],
        scratch_shapes=[pltpu.SemaphoreType.DMA((2,)), pltpu.SemaphoreType.DMA((2,))],
    )(h_in, h_out)


def _adamw_halves(w, mine, theirs, m, v, cidx, name):
    rows, cols = w.shape
    half = rows // 2
    tr = min(half, 256)
    nbh = half // tr

    def body(c_ref, w_ref, a_ref, b_ref, m_ref, v_ref, g_ref, d_ref, nm_ref, nv_ref):
        gg = jnp.where(pl.program_id(0) // nbh == c_ref[0], a_ref[...], b_ref[...])
        g_ref[...] = gg
        _adamw_update(gg, w_ref, m_ref, v_ref, d_ref, nm_ref, nv_ref)

    spec = pl.BlockSpec((tr, cols), lambda i, c_ref: (i, 0))
    hspec = pl.BlockSpec((tr, cols), lambda i, c_ref: (i % nbh, 0))
    sds = jax.ShapeDtypeStruct((rows, cols), F32)
    return pl.pallas_call(
        body, name=name,
        grid_spec=pltpu.PrefetchScalarGridSpec(
            num_scalar_prefetch=1, grid=(rows // tr,),
            in_specs=[spec, hspec, hspec, spec, spec], out_specs=[spec] * 4),
        out_shape=[sds] * 4,
        compiler_params=_params(("parallel",)),
    )(cidx, w, mine, theirs, m, v)


def _allreduce_small(g_nw, g_fw, g_hgw, g_lbl, loss8):
    def body(nw_ref, fw_ref, hgw_ref, lbl_ref, loss_ref, out_ref, slots, send_sems, recv_sems):
        x, y, c = _place()
        me = 4 * x + 2 * y + c
        slots[me] = jnp.zeros((8, D_MODEL), F32)
        slots[me, 0:1, :] = nw_ref[...]
        slots[me, 1:2, :] = fw_ref[...]
        slots[me, 2:3, 0:HG_W] = hgw_ref[...]
        slots[me, 3:4, 0:HG_W] = lbl_ref[0:1, :]
        slots[me, 3:4, HG_W:] = lbl_ref[1:2, :]
        slots[me, 4:5, 0:128] = loss_ref[0:1, :]
        cps = []
        for k in range(1, 8):
            dx, dy, dc = (k >> 2) & 1, (k >> 1) & 1, k & 1
            to = (x ^ dx, y ^ dy, c ^ dc)
            cps.append(pltpu.make_async_remote_copy(
                src_ref=slots.at[me], dst_ref=slots.at[me], send_sem=send_sems.at[k - 1],
                recv_sem=recv_sems.at[k - 1], device_id=to, device_id_type=MESH))
        for cp in cps:
            cp.start()
        for cp in cps:
            cp.wait()
        acc = slots[0]
        for i in range(1, 8):
            acc = acc + slots[i]
        out_ref[...] = acc

    vm = pl.BlockSpec(memory_space=pltpu.VMEM)
    return pl.pallas_call(
        body, name="allreduce_small",
        in_specs=[vm] * 5, out_specs=vm,
        out_shape=jax.ShapeDtypeStruct((8, D_MODEL), F32),
        scratch_shapes=[pltpu.VMEM((8, 8, D_MODEL), F32), pltpu.SemaphoreType.DMA((7,)), pltpu.SemaphoreType.DMA((7,))],
    )(g_nw, g_fw, g_hgw, g_lbl, loss8)


def _rope_tables(S):
    inv_freq = (np.float32(1.0) / np.power(np.float32(ROPE_THETA), np.arange(ROPE_HALF, dtype=np.float32) / np.float32(ROPE_HALF))).astype(np.float32)
    ang = (np.arange(S, dtype=np.float32)[:, None] * inv_freq[None, :]).astype(np.float32)
    cos, sin = np.cos(ang).astype(np.float32), np.sin(ang).astype(np.float32)
    cos2 = np.concatenate([cos, cos, cos, cos], axis=-1)
    sin2 = np.concatenate([-sin, sin, -sin, sin], axis=-1)
    return jnp.asarray(cos2), jnp.asarray(sin2)


def _local_step(x2, tgt2, norm_w, w_in_full, lbl, hg_norm_w, w_out4, fnw):
    S = x2.shape[0]
    cos2, sin2 = _rope_tables(S)
    w_out_full = w_out4.reshape(D_MODEL, D_MODEL)
    p0, p1, qkv, qkv4, qkv16, p3, u = _inproj(x2, norm_w, w_in_full, cos2, sin2)
    o_hg, mix_hg, states = _hg_fwd(p0, p1, lbl, hg_norm_w)
    qkv_ds = [qkv.reshape(1, S, 3 * AT_W), qkv4, qkv16]
    os_d, ls_d = zip(*[_att_fwd(q) for q in qkv_ds])
    o_at, lse, mix_at = _att_combine(os_d, ls_d, p3)
    dh, dm_hg, dm_at, g_wout, g_fw, loss8 = _outproj(x2, tgt2, mix_hg, mix_at, w_out_full, fnw)
    dqr, dfl, dv_hg, dz_hg, g_lbl, g_hgw = _hg_bwd(p0, p1, o_hg, dm_hg, states, lbl, hg_norm_w)
    do1, do4, do16, dl1, dl4, dl16, dz_at = _att_gate_bwd(dm_at, o_at, lse, p3)
    do_ds = [do1.reshape(1, S, AT_W), do4, do16]
    dl_ds = [dl1.reshape(1, S, STAT_W), dl4, dl16]
    dqkvs = [_att_bwd(q, a, b) for q, a, b in zip(qkv_ds, do_ds, dl_ds)]
    dq_at, dk_at, dv_at = _att_bwd_combine(dqkvs, cos2, sin2)
    dps = [dqr, dfl, dv_hg, dz_hg, dq_at, dk_at, dv_at, dz_at]
    return loss8, dps, u, dh, g_lbl, g_hgw, g_wout, g_fw


def kernel(x, norm_w, w_in, hgrn_lb_logits, hg_norm_w, w_out, final_norm_w, loss_target, m_norm_w, m_w_in, m_hgrn_lb_logits, m_hg_norm_w, m_w_out, m_final_norm_w, v_norm_w, v_w_in, v_hgrn_lb_logits, v_hg_norm_w, v_w_out, v_final_norm_w):
    S = x.shape[1]
    w_in_full, w_out_full = _gather_weights(w_in[0], w_out[0])
    loss8, dps, u, dh, g_lbl, g_hgw, g_wout, g_fw = _local_step(
        x[0], loss_target[0], norm_w, w_in_full, hgrn_lb_logits, hg_norm_w,
        w_out_full, final_norm_w.reshape(1, D_MODEL))

    cidx = lax.axis_index("c").astype(jnp.int32).reshape(1)
    g_win = _inproj_bwd_w(u, dps)
    g_in4 = g_win.reshape(4, 2, 512, 1024)
    g_out4 = g_wout.reshape(4, 2, 128, D_MODEL)
    r_in, r_out = _swap_halves(g_in4, g_out4)
    cs_in = _add_half(g_in4, r_in, cidx, "add_half_in")
    cs_out = _add_half(g_out4, r_out, cidx, "add_half_out")
    *started, token = _exchange_start(cs_in, cs_out)
    grad_x, g_nw = _inproj_bwd_x(dps, w_in_full, x[0], norm_w + token[0:1, 0:1], dh)
    cs_in, cs_out, x_in, x_out = _exchange_wait(*started, g_nw)
    chip_idx = (2 * lax.axis_index("x") + lax.axis_index("y")).astype(jnp.int32).reshape(1)
    h_in = _sum_chips(cs_in, x_in, chip_idx, "sum_chips_in")
    h_out = _sum_chips(cs_out, x_out, chip_idx, "sum_chips_out")
    s_in, s_out = _swap_reduced(h_in, h_out)

    red = _allreduce_small(g_nw, g_fw, g_hgw, g_lbl, loss8)
    loss = red[4, 0]
    grad_norm_w = red[0:1, :]
    grad_final_norm_w = red[1, :]
    grad_hg_norm_w = red[2:3, :HG_W]
    grad_lbl = jnp.concatenate([red[3:4, :HG_W], red[3:4, HG_W:]], axis=0)

    d_nw, m_nw, v_nw = _adamw(norm_w, grad_norm_w, m_norm_w, v_norm_w, "adamw_norm_w")
    grad_w_in, d_win, m_win, v_win = _adamw_halves(w_in[0], h_in, s_in, m_w_in[0], v_w_in[0], cidx, "adamw_w_in")
    d_lbl, m_lbl, v_lbl = _adamw(hgrn_lb_logits, grad_lbl, m_hgrn_lb_logits, v_hgrn_lb_logits, "adamw_lb_logits")
    d_hgw, m_hgw, v_hgw = _adamw(hg_norm_w, grad_hg_norm_w, m_hg_norm_w, v_hg_norm_w, "adamw_hg_norm_w")
    grad_w_out, d_wout, m_wout, v_wout = _adamw_halves(w_out[0], h_out, s_out, m_w_out[0], v_w_out[0], cidx, "adamw_w_out")
    d_fw, m_fw, v_fw = _adamw(final_norm_w.reshape(1, D_MODEL), grad_final_norm_w.reshape(1, D_MODEL),
                              m_final_norm_w.reshape(1, D_MODEL), v_final_norm_w.reshape(1, D_MODEL), "adamw_final_norm_w")
    e1 = lambda a: a[None]
    flat = lambda a: a.reshape(D_MODEL)
    return (loss, grad_x[None], grad_norm_w, e1(grad_w_in), grad_lbl, grad_hg_norm_w, e1(grad_w_out), grad_final_norm_w,
            d_nw, e1(d_win), d_lbl, d_hgw, e1(d_wout), flat(d_fw),
            m_nw, e1(m_win), m_lbl, m_hgw, e1(m_wout), flat(m_fw),
            v_nw, e1(v_win), v_lbl, v_hgw, e1(v_wout), flat(v_fw))
```

```python
import jax
import jax.numpy as jnp
import numpy as np
from jax import lax
from jax.experimental import pallas as pl
from jax.experimental.pallas import tpu as pltpu

F32 = jnp.float32
BF16 = jnp.bfloat16
MESH = pl.DeviceIdType.MESH

D_MODEL = 1024
HG_W = 512
AT_W = 512
HEAD_PAIR = 128
ROPE_HALF = 32
ROPE_THETA = 10000.0
EPS = 1e-6
CHUNK = 128
LEVELS = (64, 32, 16, 8, 4, 2, 1)
DIAG = 1
SUBLANES = 8
ATT_BLK = 128
ATT_QB = 4
DILATIONS = (1, 4, 16)
ATT_SCALE = 0.125
STAT_W = 128
STAT_LANES = 16
STAT_LSE_LANE = 8
NEG = -1e30
VMEM_LIMIT = 56 * 1024 * 1024
MIX_TM = 1024
PROJ_TM = 512

ADAM_LR = 0.001
ADAM_B1 = 0.9
ADAM_B2 = 0.999
ADAM_EPS = 1e-08
ADAM_WD = 0.01
ADAM_STEP = 10


def _iota(shape, dim):
    return lax.broadcasted_iota(jnp.int32, shape, dim)


def _dot(a, b):
    return jnp.dot(a, b, preferred_element_type=F32)


def _dot_nt(a, b):
    return lax.dot_general(a, b, (((1,), (1,)), ((), ())), preferred_element_type=F32)


def _dot_tn(a, b):
    return lax.dot_general(a, b, (((0,), (0,)), ((), ())), preferred_element_type=F32)


def _sigmoid(v):
    return 0.5 * jnp.tanh(0.5 * v) + 0.5


def _params(sem=None, vmem=VMEM_LIMIT):
    return pltpu.CompilerParams(dimension_semantics=sem, vmem_limit_bytes=vmem)


def _full(shape):
    n = len(shape)
    return pl.BlockSpec(shape, lambda *_: (0,) * n)


def _rope_rot(y):
    n = y.shape[1]
    first = (_iota(y.shape, 1) & (2 * ROPE_HALF - 1)) < ROPE_HALF
    return jnp.where(first, pltpu.roll(y, n - ROPE_HALF, 1), pltpu.roll(y, ROPE_HALF, 1))


def _dil_spec(d, tm, width):
    return pl.BlockSpec((d, tm // d, width), lambda s: (0, s, 0))


LANES = 128


def _slab_scratch(tm, width):
    return pltpu.VMEM((width // LANES, tm, LANES), F32)


def _to_slabs(v, slabs_ref):
    for j in range(slabs_ref.shape[0]):
        slabs_ref[j] = v[:, LANES * j: LANES * (j + 1)]


def _from_slabs(slabs_ref):
    return jnp.concatenate([slabs_ref[j] for j in range(slabs_ref.shape[0])], axis=1)


def _split_residues(slabs_ref, dst_ref, d, dtype):
    nslab, tm, _ = slabs_ref.shape
    for r in range(d):
        for j in range(nslab):
            dst_ref[r, :, LANES * j: LANES * (j + 1)] = slabs_ref[j, pl.ds(r, tm // d, stride=d), :].astype(dtype)


def _merge_residues(src_ref, slabs_ref, d):
    nslab, tm, _ = slabs_ref.shape
    for r in range(d):
        for j in range(nslab):
            slabs_ref[j, pl.ds(r, tm // d, stride=d), :] = src_ref[r, :, LANES * j: LANES * (j + 1)].astype(F32)


def _inproj(x2, norm_w, w_in_full, cos2, sin2):
    S = x2.shape[0]
    tm = PROJ_TM

    def body(x_ref, nw_ref, w_ref, cos_ref, sin_ref, p0_ref, p1_ref, qkv_ref, qkv4_ref, qkv16_ref, p3_ref, u_ref, scr):
        x = x_ref[...]
        r = lax.rsqrt(jnp.mean(x * x, axis=-1, keepdims=True) + EPS)
        u = x * r * nw_ref[...]
        ub = u.astype(BF16)
        u_ref[...] = ub
        p0_ref[...] = _dot(ub, w_ref[0])
        p1_ref[...] = _dot(ub, w_ref[1])
        y2 = _dot(ub, w_ref[2])
        cosf = jnp.tile(cos_ref[...], (1, 8))
        sinf = jnp.tile(sin_ref[...], (1, 8))
        y3 = _dot(ub, w_ref[3])
        p3_ref[...] = y3
        qkv = jnp.concatenate([y2 * cosf + _rope_rot(y2) * sinf, y3[:, :AT_W]], axis=1)
        qkv_ref[...] = qkv.astype(BF16)
        _to_slabs(qkv, scr)
        _split_residues(scr, qkv4_ref, 4, BF16)
        _split_residues(scr, qkv16_ref, 16, BF16)

    row = lambda w: pl.BlockSpec((tm, w), lambda s: (s, 0))
    qkv_w = 3 * AT_W
    return pl.pallas_call(
        body, name="inproj", grid=(S // tm,),
        in_specs=[row(D_MODEL), _full((1, D_MODEL)), _full((4, D_MODEL, 1024)), row(128), row(128)],
        out_specs=[row(1024), row(1024), row(qkv_w), _dil_spec(4, tm, qkv_w), _dil_spec(16, tm, qkv_w), row(1024),
                   row(D_MODEL)],
        out_shape=[jax.ShapeDtypeStruct((S, 1024), F32), jax.ShapeDtypeStruct((S, 1024), F32),
                   jax.ShapeDtypeStruct((S, qkv_w), BF16), jax.ShapeDtypeStruct((4, S // 4, qkv_w), BF16),
                   jax.ShapeDtypeStruct((16, S // 16, qkv_w), BF16), jax.ShapeDtypeStruct((S, 1024), F32),
                   jax.ShapeDtypeStruct((S, D_MODEL), BF16)],
        scratch_shapes=[_slab_scratch(tm, qkv_w)],
        compiler_params=_params(("parallel",)),
    )(x2, norm_w, w_in_full, cos2, sin2)


HG_HPS = 4
N_LEV = len(LEVELS)


def _hg_const_arrays():
    r = np.arange(CHUNK)[:, None]
    c = np.arange(CHUNK)[None, :]
    tris = np.stack([r >= c, r <= c])
    lm = [((r // (2 * m)) == (c // (2 * m))) & (r % (2 * m) >= m) & (c % (2 * m) < m) for m in LEVELS]
    dm = [(c == r - dl) & (r % DIAG >= dl) for dl in range(DIAG)]
    masks = np.stack(lm + dm)
    return jnp.asarray(tris, BF16), jnp.asarray(masks, F32)


def _split2(a):
    hi = a.astype(BF16)
    return hi, (a - hi.astype(F32)).astype(BF16)


def _dot3(a, b, dot=_dot):
    ah, al = _split2(a)
    bh, bl = _split2(b)
    n = b.shape[1]
    p = dot(ah, jnp.concatenate([bh, bl], axis=1))
    return (p[:, :n] + p[:, n:]) + dot(al, bh)


def _split3(a):
    a1 = a.astype(BF16)
    r1 = a - a1.astype(F32)
    a2 = r1.astype(BF16)
    return a1, a2, (r1 - a2.astype(F32)).astype(BF16)


def _tri_dot(tri, a):
    n = a.shape[1]
    p = _dot(tri, jnp.concatenate(_split3(a), axis=1))
    return (p[:, :n] + p[:, n:2 * n]) + p[:, 2 * n:]


def _dot_sel(a, sel):
    a1, a2, a3 = _split3(a)
    return (_dot(a1, sel) + _dot(a2, sel)) + _dot(a3, sel)


def _rowsum(t):
    return _dot(t.astype(BF16), jnp.ones((t.shape[1], t.shape[1]), BF16))


def _level_refs(b):
    refs = []
    pos = _iota(b.shape, 0)
    for m in LEVELS:
        if 2 * m >= SUBLANES:
            parts = [jnp.broadcast_to(b[r0 + m - 1: r0 + m, :], (2 * m, b.shape[1])) for r0 in range(0, CHUNK, 2 * m)]
            refs.append(parts[0] if len(parts) == 1 else jnp.concatenate(parts, axis=0))
        else:
            p = pos & (2 * m - 1)
            ref = b
            for off in range(-(m - 1), m + 1):
                if off != 0:
                    ref = jnp.where(p == m - 1 + off, pltpu.roll(b, off % CHUNK, 0), ref)
            refs.append(ref)
    return refs


def _hg_lb(lbl_ref):
    l0 = lbl_ref[0:1, :]
    l1 = lbl_ref[1:2, :]
    mx = jnp.maximum(l0, l1)
    e0 = jnp.exp(l0 - mx)
    e1 = jnp.exp(l1 - mx)
    p0 = e0 / (e0 + e1)
    lb = jnp.clip(p0, 1e-6, 1.0 - 1e-6)
    inside = (p0 >= 1e-6) & (p0 <= 1.0 - 1e-6)
    dlb_dl0 = jnp.where(inside, p0 * (e1 / (e0 + e1)), 0.0)
    return lb, dlb_dl0


def _hg_gates(qr, fl, lb):
    sig = _sigmoid(fl)
    f = lb + (1.0 - lb) * sig
    g = jnp.log(f)
    k = (1.0 - lb) * (1.0 - sig)
    sq = _sigmoid(qr)
    q = qr * sq
    return sig, f, g, k, sq, q


def _hg_levels(q, k, b, mk_ref):
    refs = _level_refs(b)
    a = jnp.zeros((CHUNK, CHUNK), F32)
    es, qts, kts = [], [], []
    for i in range(N_LEV):
        diff = b - refs[i]
        e = jnp.exp(jnp.minimum(diff, -diff))
        qt = (q * e).astype(BF16)
        kt = (k * e).astype(BF16)
        a = a + _dot_nt(qt, kt) * mk_ref[i]
        es.append(e); qts.append(qt); kts.append(kt)
    return a, es, qts, kts


def _hg_specs(nc, rev):
    cc = (lambda c: nc - 1 - c) if rev else (lambda c: c)
    w = 128 * HG_HPS
    blk = lambda off: pl.BlockSpec((CHUNK, w), lambda h, c: (cc(c), h + off))
    vec = pl.BlockSpec((1, w), lambda h, c: (0, h))
    lb2 = pl.BlockSpec((2, w), lambda h, c: (0, h))
    st = pl.BlockSpec((1, HG_HPS, 128, 128), lambda h, c: (cc(c), h, 0, 0))
    consts = [_full((2, CHUNK, CHUNK)), _full((N_LEV + DIAG, CHUNK, CHUNK))]
    return blk, vec, lb2, st, consts


def _hg_fwd(p0, p1, lbl, gw):
    S = p0.shape[0]
    nc = S // CHUNK
    ng = 4 // HG_HPS

    def body(qr_ref, fl_ref, v_ref, z_ref, lbl_ref, gw_ref, tri_ref, mk_ref,
             o_ref, mix_ref, st_ref, state):
        c = pl.program_id(1)

        @pl.when(c == 0)
        def _():
            state[...] = jnp.zeros_like(state)

        lb_all, _ = _hg_lb(lbl_ref)
        heads = [slice(128 * hh, 128 * (hh + 1)) for hh in range(HG_HPS)]
        qs, ks, bs, mats = [], [], [], []
        for sl in heads:
            _, _, g, k, _, q = _hg_gates(qr_ref[:, sl], fl_ref[:, sl], lb_all[:, sl])
            qs.append(q); ks.append(k); bs.append(_tri_dot(tri_ref[0], g))
        for hh in range(HG_HPS):
            a, _, _, _ = _hg_levels(qs[hh], ks[hh], bs[hh], mk_ref)
            mats.append(a + _rowsum(qs[hh] * ks[hh]) * mk_ref[N_LEV])
        for hh, sl in enumerate(heads):
            q, k, b, a, v, z = qs[hh], ks[hh], bs[hh], mats[hh], v_ref[:, sl], z_ref[:, sl]
            b_last = b[CHUNK - 1: CHUNK, :]
            st = state[hh]
            st_ref[0, hh] = st
            o = _dot_nt((q * jnp.exp(b)).astype(BF16), st.astype(BF16)) + _dot(a.astype(BF16), v.astype(BF16))
            state[hh] = st * jnp.exp(b_last) + _dot3(v, k * jnp.exp(b_last - b), _dot_tn)
            o_ref[:, sl] = o
            rs = lax.rsqrt(jnp.mean(o * o, axis=-1, keepdims=True) + EPS)
            mixed = o * rs * gw_ref[:, sl] * (z * _sigmoid(z))
            mix_ref[:, sl] = mixed.astype(BF16)

    blk, vec, lb2, st_spec, consts = _hg_specs(nc, False)
    tris, masks = _hg_const_arrays()
    return pl.pallas_call(
        body, name="hg_fwd", grid=(ng, nc),
        in_specs=[blk(0), blk(ng), blk(0), blk(ng), lb2, vec] + consts,
        out_specs=[blk(0), blk(0), st_spec],
        out_shape=[jax.ShapeDtypeStruct((S, HG_W), F32), jax.ShapeDtypeStruct((S, HG_W), BF16),
                   jax.ShapeDtypeStruct((nc, 4, 128, 128), F32)],
        scratch_shapes=[pltpu.VMEM((HG_HPS, 128, 128), F32)],
        compiler_params=_params(("parallel", "arbitrary")),
    )(p0, p0, p1, p1, lbl, gw, tris, masks)


def _hg_bwd(p0, p1, do_hg, states, lbl):
    S = p0.shape[0]
    nc = S // CHUNK
    ng = 4 // HG_HPS
    w = 128 * HG_HPS

    def body(qr_ref, fl_ref, v_ref, do_ref, st_ref, lbl_ref, tri_ref, mk_ref, mkb_ref,
             dqr_ref, dfl_ref, dv_ref, glbl_ref, dstate, carry, acc_lb):
        c = pl.program_id(1)

        @pl.when(c == 0)
        def _():
            dstate[...] = jnp.zeros_like(dstate)
            carry[...] = jnp.zeros_like(carry)
            acc_lb[...] = jnp.zeros_like(acc_lb)

        lb_all, dlb_dl0 = _hg_lb(lbl_ref)
        heads = [slice(128 * hh, 128 * (hh + 1)) for hh in range(HG_HPS)]
        diag_mask = mk_ref[N_LEV]
        gates, bs, dos = [], [], []
        for sl in heads:
            gt = _hg_gates(qr_ref[:, sl], fl_ref[:, sl], lb_all[:, sl])
            gates.append(gt)
            bs.append(_tri_dot(tri_ref[0], gt[2]))
        for sl in heads:
            dos.append(do_ref[:, sl])
        inter = []
        for hh, sl in enumerate(heads):
            _, _, _, k, _, q = gates[hh]
            b, do, v = bs[hh], dos[hh], v_ref[:, sl]
            b_last = b[CHUNK - 1: CHUNK, :]
            eb = jnp.exp(b)
            edec = jnp.exp(b_last - b)
            dst = dstate[hh]
            dq = _dot3(do, st_ref[0, hh]) * eb
            dk = _dot3(v, dst) * edec
            da = _dot_nt(do.astype(BF16), v.astype(BF16))
            dv_state = _dot_nt((k * edec).astype(BF16), dst.astype(BF16))
            dstate[hh] = dst * jnp.exp(b_last) + _dot3(do, q * eb, _dot_tn)
            inter.append((dq, dk, da, dv_state))
        for hh, sl in enumerate(heads):
            sig, f, _, k, sq, q = gates[hh]
            dq, dk, da, dv_state = inter[hh]
            b = bs[hh]
            db = q * dq - k * dk
            dab = da.astype(BF16)
            refs = _level_refs(b)
            a = _rowsum(q * k) * diag_mask
            for i in range(N_LEV):
                diff = b - refs[i]
                e = jnp.exp(jnp.minimum(diff, -diff))
                qt = (q * e).astype(BF16)
                kt = (k * e).astype(BF16)
                a = a + _dot_nt(qt, kt) * mk_ref[i]
                dam = dab * mkb_ref[i]
                gq = _dot(dam, kt)
                gk = _dot_tn(dam, qt)
                dq = dq + e * gq
                dk = dk + e * gk
                db = db + (qt.astype(F32) * gq - kt.astype(F32) * gk)
            dc = _rowsum(da * diag_mask)
            dq = dq + dc * k
            dk = dk + dc * q
            dv_ref[:, sl] = (_dot_tn(a.astype(BF16), dos[hh].astype(BF16)) + dv_state).astype(BF16)
            dg = _tri_dot(tri_ref[1], db) + carry[0:1, sl]
            carry[0:1, sl] += jnp.sum(db, axis=0, keepdims=True)
            lb = lb_all[:, sl]
            qr = qr_ref[:, sl]
            t = dg / f - dk
            dfl_ref[:, sl] = (t * (1.0 - lb) * sig * (1.0 - sig)).astype(BF16)
            acc_lb[0:1, sl] += jnp.sum(t * (1.0 - sig), axis=0, keepdims=True)
            dqr_ref[:, sl] = (dq * (sq * (1.0 + qr * (1.0 - sq)))).astype(BF16)

        @pl.when(c == nc - 1)
        def _():
            gl0 = acc_lb[0:1, :] * dlb_dl0
            glbl_ref[0:1, :] = gl0
            glbl_ref[1:2, :] = -gl0

    blk, vec, lb2, st_spec, consts = _hg_specs(nc, True)
    tris, masks = _hg_const_arrays()
    act = jax.ShapeDtypeStruct((S, HG_W), BF16)
    return pl.pallas_call(
        body, name="hg_bwd", grid=(ng, nc),
        in_specs=[blk(0), blk(ng), blk(0), blk(0), st_spec, lb2] + consts + consts[1:],
        out_specs=[blk(0), blk(0), blk(0), lb2],
        out_shape=[act, act, act, jax.ShapeDtypeStruct((2, HG_W), F32)],
        scratch_shapes=[pltpu.VMEM((HG_HPS, 128, 128), F32), pltpu.VMEM((8, w), F32), pltpu.VMEM((8, w), F32)],
        compiler_params=_params(("parallel", "arbitrary")),
    )(p0, p0, p1, do_hg, states, lbl, tris, masks, masks.astype(BF16))


def _stat_head(lane):
    return lane >> 4


def _stat_lane(lane):
    return lane & (STAT_LANES - 1)


def _o_head(lane):
    return lane >> 6


def _att_bias():
    qi = np.arange(ATT_BLK)[:, None]
    kj = np.arange(2 * ATT_BLK)[None, :]
    band = (kj >= qi) & (kj <= qi + ATT_BLK)
    qm = np.stack([band & (kj >= ATT_BLK), band])
    cur = (kj < ATT_BLK) & (qi <= kj)
    km = np.stack([cur, cur | ((kj >= ATT_BLK) & (qi >= kj - ATT_BLK))])
    to_bias = lambda m: jnp.asarray(np.where(m, 0.0, NEG), F32)
    return to_bias(qm), to_bias(km)


def _att_fwd(qkv_d):
    d, L, _ = qkv_d.shape
    qb = ATT_QB
    rows = qb * ATT_BLK

    def body(q_ref, kp_ref, kc_ref, vp_ref, vc_ref, bias_ref, o_ref, lse_ref, s_scr, p_scr):
        first = _iota((ATT_BLK, HEAD_PAIR), 1) < 64
        head_of_lane = _stat_head(_iota((ATT_BLK, STAT_W), 1))
        pairs = [slice(HEAD_PAIR * hp, HEAD_PAIR * (hp + 1)) for hp in range(4)]
        blk = lambda i: slice(ATT_BLK * i, ATT_BLK * (i + 1))

        def keys(i, sl, prev_ref, cur_ref):
            before = prev_ref[0, :, sl] if i == 0 else cur_ref[0, blk(i - 1), sl]
            return jnp.concatenate([before, cur_ref[0, blk(i), sl]], axis=0)

        for i in range(qb):
            for hp, sl in enumerate(pairs):
                q2 = q_ref[0, blk(i), sl] * ATT_SCALE
                zero = jnp.zeros_like(q2)
                qs = jnp.concatenate([jnp.where(first, q2, zero), jnp.where(first, zero, q2)], axis=0)
                s_scr[4 * i + hp] = _dot_nt(qs, keys(i, sl, kp_ref, kc_ref))
        stats = []
        for i in range(qb):
            bias = bias_ref[jnp.minimum(pl.program_id(1), 1)] if i == 0 else bias_ref[1]
            bias2 = jnp.concatenate([bias, bias], axis=0)
            for hp in range(4):
                s = s_scr[4 * i + hp] + bias2
                m = jnp.max(s, axis=-1, keepdims=True)
                p = jnp.exp(s - m)
                l = jnp.sum(p, axis=-1, keepdims=True)
                p_scr[4 * i + hp] = p.astype(BF16)
                stats.append((l, m + jnp.log(l)))
        for i in range(qb):
            lse_blk = jnp.zeros((ATT_BLK, STAT_W), F32)
            for hp, sl in enumerate(pairs):
                l, lse = stats[4 * i + hp]
                o = _dot(p_scr[4 * i + hp], keys(i, sl, vp_ref, vc_ref)) / l
                o_ref[0, blk(i), sl] = jnp.where(first, o[:ATT_BLK], o[ATT_BLK:]).astype(BF16)
                lse_blk = jnp.where(head_of_lane == 2 * hp, lse[:ATT_BLK],
                                    jnp.where(head_of_lane == 2 * hp + 1, lse[ATT_BLK:], lse_blk))
            lse_ref[0, blk(i), :] = lse_blk

    cur = lambda j: pl.BlockSpec((1, rows, AT_W), lambda r, n: (r, n, j))
    prev = lambda j: pl.BlockSpec((1, ATT_BLK, AT_W), lambda r, n: (r, jnp.maximum(qb * n - 1, 0), j))
    return pl.pallas_call(
        body, name=f"att_fwd_d{d}", grid=(d, L // rows),
        in_specs=[cur(0), prev(1), cur(1), prev(2), cur(2), _full((2, ATT_BLK, 2 * ATT_BLK))],
        out_specs=[pl.BlockSpec((1, rows, AT_W), lambda r, n: (r, n, 0)), pl.BlockSpec((1, rows, STAT_W), lambda r, n: (r, n, 0))],
        out_shape=[jax.ShapeDtypeStruct((d, L, AT_W), BF16), jax.ShapeDtypeStruct((d, L, STAT_W), F32)],
        scratch_shapes=[pltpu.VMEM((4 * qb, 2 * ATT_BLK, 2 * ATT_BLK), F32), pltpu.VMEM((4 * qb, 2 * ATT_BLK, 2 * ATT_BLK), BF16)],
        compiler_params=_params(("parallel", "parallel")),
    )(qkv_d, qkv_d, qkv_d, qkv_d, qkv_d, _att_bias()[0])


def _att_combine(os_d, ls_d, p3):
    S = p3.shape[0]
    tm = MIX_TM

    def body(oa_ref, ob4_ref, oc16_ref, la_ref, lb4_ref, lc16_ref, z_ref,
             oat_ref, lse_ref, mix_ref, ob_scr, oc_scr, lb_scr, lc_scr):
        _merge_residues(ob4_ref, ob_scr, 4)
        _merge_residues(oc16_ref, oc_scr, 16)
        _merge_residues(lb4_ref, lb_scr, 4)
        _merge_residues(lc16_ref, lc_scr, 16)
        ls = (la_ref[...], lb_scr[0], lc_scr[0])
        mx = jnp.maximum(jnp.maximum(ls[0], ls[1]), ls[2])
        es = [jnp.exp(l - mx) for l in ls]
        zs = es[0] + es[1] + es[2]
        lse_ref[...] = mx + jnp.log(zs)
        spread = ((_o_head(_iota((STAT_W, AT_W), 1)) == _stat_head(_iota((STAT_W, AT_W), 0)))
                  & (_stat_lane(_iota((STAT_W, AT_W), 0)) == 0)).astype(BF16)
        os_ = (oa_ref[...].astype(F32), _from_slabs(ob_scr), _from_slabs(oc_scr))
        o = jnp.zeros((tm, AT_W), F32)
        for e, oi in zip(es, os_):
            hi, lo = _split2(e / zs)
            o = o + (_dot(hi, spread) + _dot(lo, spread)) * oi
        oat_ref[...] = o
        z = z_ref[...]
        mixed = o * (z * _sigmoid(z))
        mix_ref[...] = mixed.astype(BF16)

    row = lambda w: pl.BlockSpec((tm, w), lambda s: (s, 0))
    return pl.pallas_call(
        body, name="att_combine", grid=(S // tm,),
        in_specs=[row(AT_W), _dil_spec(4, tm, AT_W), _dil_spec(16, tm, AT_W),
                  row(STAT_W), _dil_spec(4, tm, STAT_W), _dil_spec(16, tm, STAT_W),
                  pl.BlockSpec((tm, AT_W), lambda s: (s, 1))],
        out_specs=[row(AT_W), row(STAT_W), row(AT_W)],
        out_shape=[jax.ShapeDtypeStruct((S, AT_W), F32), jax.ShapeDtypeStruct((S, STAT_W), F32),
                   jax.ShapeDtypeStruct((S, AT_W), BF16)],
        scratch_shapes=[_slab_scratch(tm, AT_W), _slab_scratch(tm, AT_W), _slab_scratch(tm, STAT_W), _slab_scratch(tm, STAT_W)],
        compiler_params=_params(("parallel",)),
    )(os_d[0].reshape(S, AT_W), os_d[1], os_d[2], ls_d[0].reshape(S, STAT_W), ls_d[1], ls_d[2], p3)


def _att_gate_bwd(dm_at, o_at, lse, p3):
    S = p3.shape[0]
    tm = MIX_TM

    def body(dm_ref, o_ref, l_ref, z_ref, do_ref, do4_ref, do16_ref, dl_ref, dl4_ref, dl16_ref, dz_ref, do_scr, dl_scr):
        o = o_ref[...]
        z = z_ref[...]
        dm = dm_ref[...]
        sz = _sigmoid(z)
        dz_ref[...] = (dm * o * (sz * (1.0 + z * (1.0 - sz)))).astype(BF16)
        do = dm * (z * sz)
        do_ref[...] = do.astype(BF16)
        gather = (_o_head(_iota((AT_W, STAT_W), 0)) == _stat_head(_iota((AT_W, STAT_W), 1))).astype(BF16)
        dl = jnp.where(_stat_lane(_iota((tm, STAT_W), 1)) < STAT_LSE_LANE, _dot_sel(do * o, gather), l_ref[...])
        dl_ref[...] = dl
        _to_slabs(do, do_scr)
        _to_slabs(dl, dl_scr)
        _split_residues(do_scr, do4_ref, 4, BF16)
        _split_residues(do_scr, do16_ref, 16, BF16)
        _split_residues(dl_scr, dl4_ref, 4, F32)
        _split_residues(dl_scr, dl16_ref, 16, F32)

    row = lambda w: pl.BlockSpec((tm, w), lambda s: (s, 0))
    sds = jax.ShapeDtypeStruct
    return pl.pallas_call(
        body, name="att_gate_bwd", grid=(S // tm,),
        in_specs=[row(AT_W), row(AT_W), row(STAT_W), pl.BlockSpec((tm, AT_W), lambda s: (s, 1))],
        out_specs=[row(AT_W), _dil_spec(4, tm, AT_W), _dil_spec(16, tm, AT_W),
                   row(STAT_W), _dil_spec(4, tm, STAT_W), _dil_spec(16, tm, STAT_W), row(AT_W)],
        out_shape=[sds((S, AT_W), BF16), sds((4, S // 4, AT_W), BF16), sds((16, S // 16, AT_W), BF16),
                   sds((S, STAT_W), F32), sds((4, S // 4, STAT_W), F32), sds((16, S // 16, STAT_W), F32),
                   sds((S, AT_W), BF16)],
        scratch_shapes=[_slab_scratch(tm, AT_W), _slab_scratch(tm, STAT_W)],
        compiler_params=_params(("parallel",)),
    )(dm_at, o_at, lse, p3)


def _att_bwd(qkv_d, do_d, dl_d):
    d, L, _ = qkv_d.shape
    nb = L // ATT_BLK
    qb = ATT_QB
    rows = qb * ATT_BLK
    nsteps = L // rows

    def body(qc_ref, qn_ref, kp_ref, kc_ref, vp_ref, vc_ref, ac_ref, an_ref, lc_ref, ln_ref, bq_ref, bk_ref, dqkv_ref,
             s_scr, dp_scr, st_scr, dpt_scr, ds_scr, pt_scr, dst_scr):
        n = pl.program_id(1)
        first = _iota((ATT_BLK, HEAD_PAIR), 1) < 64
        pairs = [slice(HEAD_PAIR * hp, HEAD_PAIR * (hp + 1)) for hp in range(4)]
        blk = lambda i: slice(ATT_BLK * i, ATT_BLK * (i + 1))

        def stack(t):
            zero = jnp.zeros_like(t)
            return jnp.concatenate([jnp.where(first, t, zero), jnp.where(first, zero, t)], axis=0)

        def unstack(t2):
            return jnp.where(first, t2[:ATT_BLK], t2[ATT_BLK:])

        def with_prev(i, sl, prev_ref, cur_ref):
            before = prev_ref[0, :, sl] if i == 0 else cur_ref[0, blk(i - 1), sl]
            return jnp.concatenate([before, cur_ref[0, blk(i), sl]], axis=0)

        def with_next(i, sl, cur_ref, next_ref):
            after = next_ref[0, :, sl] if i == qb - 1 else cur_ref[0, blk(i + 1), sl]
            return jnp.concatenate([cur_ref[0, blk(i), sl], after], axis=0)

        for i in range(qb):
            for hp, sl in enumerate(pairs):
                j = 4 * i + hp
                s_scr[j] = _dot_nt(stack(qc_ref[0, blk(i), sl] * ATT_SCALE), with_prev(i, sl, kp_ref, kc_ref))
                dp_scr[j] = _dot_nt(stack(ac_ref[0, blk(i), sl]), with_prev(i, sl, vp_ref, vc_ref))
                st_scr[j] = _dot_nt(stack(kc_ref[0, blk(i), sl] * ATT_SCALE), with_next(i, sl, qc_ref, qn_ref))
                dpt_scr[j] = _dot_nt(stack(vc_ref[0, blk(i), sl]), with_next(i, sl, ac_ref, an_ref))
        for i in range(qb):
            bias = bq_ref[jnp.minimum(n, 1)] if i == 0 else bq_ref[1]
            bias_t = bk_ref[jnp.minimum(nsteps - 1 - n, 1)] if i == qb - 1 else bk_ref[1]
            bias2 = jnp.concatenate([bias, bias], axis=0)
            bias_t2 = jnp.concatenate([bias_t, bias_t], axis=0)
            dl_c = lc_ref[0, blk(i), :]
            dl_t = with_next(i, slice(None), lc_ref, ln_ref).T
            for hp in range(4):
                j = 4 * i + hp
                at = [STAT_LANES * (2 * hp), STAT_LANES * (2 * hp + 1)]
                col = lambda t, o: jnp.concatenate([t[:, a + o: a + o + 1] for a in at], axis=0)
                p = jnp.exp(s_scr[j] + bias2 - col(dl_c, STAT_LSE_LANE))
                ds_scr[j] = (p * (dp_scr[j] - col(dl_c, 0))).astype(BF16)
                row = lambda t, o: jnp.concatenate([jnp.broadcast_to(t[a + o: a + o + 1, :], (ATT_BLK, 2 * ATT_BLK)) for a in at], axis=0)
                pt = jnp.exp(st_scr[j] + bias_t2 - row(dl_t, STAT_LSE_LANE))
                pt_scr[j] = pt.astype(BF16)
                dst_scr[j] = (pt * (dpt_scr[j] - row(dl_t, 0))).astype(BF16)
        for i in range(qb):
            for hp, sl in enumerate(pairs):
                j = 4 * i + hp
                dq = unstack(_dot(ds_scr[j], with_prev(i, sl, kp_ref, kc_ref))) * ATT_SCALE
                dk = unstack(_dot(dst_scr[j], with_next(i, sl, qc_ref, qn_ref))) * ATT_SCALE
                dv = unstack(_dot(pt_scr[j], with_next(i, sl, ac_ref, an_ref)))
                dqkv_ref[0, blk(i), sl] = dq.astype(BF16)
                dqkv_ref[0, blk(i), AT_W + HEAD_PAIR * hp: AT_W + HEAD_PAIR * (hp + 1)] = dk.astype(BF16)
                dqkv_ref[0, blk(i), 2 * AT_W + HEAD_PAIR * hp: 2 * AT_W + HEAD_PAIR * (hp + 1)] = dv.astype(BF16)

    cur = lambda j: pl.BlockSpec((1, rows, AT_W), lambda r, n: (r, n, j))
    prev = lambda j: pl.BlockSpec((1, ATT_BLK, AT_W), lambda r, n: (r, jnp.maximum(qb * n - 1, 0), j))
    nxt_blk = lambda n: jnp.minimum(qb * (n + 1), nb - 1)
    sq = (4 * qb, 2 * ATT_BLK, 2 * ATT_BLK)
    return pl.pallas_call(
        body, name=f"att_bwd_d{d}", grid=(d, nsteps),
        in_specs=[cur(0), pl.BlockSpec((1, ATT_BLK, AT_W), lambda r, n: (r, nxt_blk(n), 0)), prev(1), cur(1), prev(2), cur(2),
                  pl.BlockSpec((1, rows, AT_W), lambda r, n: (r, n, 0)),
                  pl.BlockSpec((1, ATT_BLK, AT_W), lambda r, n: (r, nxt_blk(n), 0)),
                  pl.BlockSpec((1, rows, STAT_W), lambda r, n: (r, n, 0)),
                  pl.BlockSpec((1, ATT_BLK, STAT_W), lambda r, n: (r, nxt_blk(n), 0)),
                  _full((2, ATT_BLK, 2 * ATT_BLK)), _full((2, ATT_BLK, 2 * ATT_BLK))],
        out_specs=pl.BlockSpec((1, rows, 3 * AT_W), lambda r, n: (r, n, 0)),
        out_shape=jax.ShapeDtypeStruct((d, L, 3 * AT_W), BF16),
        scratch_shapes=[pltpu.VMEM(sq, F32)] * 4 + [pltpu.VMEM(sq, BF16)] * 3,
        compiler_params=_params(("parallel", "parallel")),
    )(qkv_d, qkv_d, qkv_d, qkv_d, qkv_d, qkv_d, do_d, do_d, dl_d, dl_d, *_att_bias())


def _att_bwd_combine(dqkvs, cos2, sin2):
    S = dqkvs[0].shape[1]
    tm = MIX_TM

    def body(a_ref, b4_ref, c16_ref, cos_ref, sin_ref, dq_ref, dk_ref, dv_ref, b_scr, c_scr):
        _merge_residues(b4_ref, b_scr, 4)
        _merge_residues(c16_ref, c_scr, 16)
        t = a_ref[...].astype(F32) + _from_slabs(b_scr) + _from_slabs(c_scr)
        dy = t[:, : 2 * AT_W]
        cosf = jnp.tile(cos_ref[...], (1, 8))
        sinf = jnp.tile(sin_ref[...], (1, 8))
        dx = dy * cosf - _rope_rot(dy) * sinf
        dq_ref[...] = dx[:, :AT_W].astype(BF16)
        dk_ref[...] = dx[:, AT_W:].astype(BF16)
        dv_ref[...] = t[:, 2 * AT_W:].astype(BF16)

    row = lambda w: pl.BlockSpec((tm, w), lambda s: (s, 0))
    act = jax.ShapeDtypeStruct((S, AT_W), BF16)
    return pl.pallas_call(
        body, name="att_bwd_combine", grid=(S // tm,),
        in_specs=[row(3 * AT_W), _dil_spec(4, tm, 3 * AT_W), _dil_spec(16, tm, 3 * AT_W), row(128), row(128)],
        out_specs=[row(AT_W), row(AT_W), row(AT_W)],
        out_shape=[act, act, act],
        scratch_shapes=[_slab_scratch(tm, 3 * AT_W), _slab_scratch(tm, 3 * AT_W)],
        compiler_params=_params(("parallel",)),
    )(dqkvs[0].reshape(S, 3 * AT_W), dqkvs[1], dqkvs[2], cos2, sin2)


def _outproj(x2, tgt2, mix_hg, mix_at, w_out_full, fnw, o_hg, p1, hg_norm_w):
    S = x2.shape[0]
    tm = PROJ_TM
    ns = S // tm

    def body(x_ref, t_ref, mh_ref, ma_ref, w_ref, fw_ref, o_ref, z_ref, hgw_ref,
             dh_ref, doh_ref, dzh_ref, dma_ref, gw_ref, gfw_ref, ghgw_ref, loss_ref):
        s = pl.program_id(0)

        @pl.when(s == 0)
        def _():
            gw_ref[...] = jnp.zeros_like(gw_ref)
            gfw_ref[...] = jnp.zeros_like(gfw_ref)
            ghgw_ref[...] = jnp.zeros_like(ghgw_ref)
            loss_ref[...] = jnp.zeros_like(loss_ref)

        y = _dot(mh_ref[...], w_ref[:HG_W, :]) + _dot(ma_ref[...], w_ref[HG_W:, :])
        h = x_ref[...] + y
        r = lax.rsqrt(jnp.mean(h * h, axis=-1, keepdims=True) + EPS)
        hn = h * r
        fw = fw_ref[...]
        err = hn * fw - t_ref[...]
        loss_ref[...] += 0.5 * jnp.sum(jnp.mean(err * err, axis=-1, keepdims=True))
        dout = err * (1.0 / D_MODEL)
        gfw_ref[...] += jnp.sum(dout * hn, axis=0, keepdims=True)
        dhn = dout * fw
        dh = r * (dhn - hn * jnp.mean(dhn * hn, axis=-1, keepdims=True))
        dh_ref[...] = dh
        dhb = dh.astype(BF16)
        dma_ref[...] = _dot_nt(dhb, w_ref[HG_W:, :])
        dmh = _dot_nt(dhb, w_ref[:HG_W, :])
        for hh in range(HG_W // 128):
            sl = slice(128 * hh, 128 * (hh + 1))
            o, z, dmix, gwv = o_ref[:, sl], z_ref[:, sl], dmh[:, sl], hgw_ref[:, sl]
            rs = lax.rsqrt(jnp.mean(o * o, axis=-1, keepdims=True) + EPS)
            oh = o * rs
            sz = _sigmoid(z)
            dzh_ref[:, sl] = (dmix * (oh * gwv) * (sz * (1.0 + z * (1.0 - sz)))).astype(BF16)
            don = dmix * (z * sz)
            ghgw_ref[:, sl] += jnp.sum(don * oh, axis=0, keepdims=True)
            dy = don * gwv
            doh_ref[:, sl] = rs * (dy - oh * jnp.mean(dy * oh, axis=-1, keepdims=True))
        gw_ref[:HG_W, :] += _dot_tn(mh_ref[...], dhb)
        gw_ref[HG_W:, :] += _dot_tn(ma_ref[...], dhb)

    row = lambda w: pl.BlockSpec((tm, w), lambda s: (s, 0))
    return pl.pallas_call(
        body, name="outproj", grid=(ns,),
        in_specs=[row(D_MODEL), row(D_MODEL), row(HG_W), row(AT_W),
                  _full((D_MODEL, D_MODEL)), _full((1, D_MODEL)),
                  row(HG_W), pl.BlockSpec((tm, HG_W), lambda s: (s, 1)), _full((1, HG_W))],
        out_specs=[row(D_MODEL), row(HG_W), row(HG_W), row(AT_W), _full((D_MODEL, D_MODEL)), _full((1, D_MODEL)),
                   _full((1, HG_W)), _full((8, 128))],
        out_shape=[jax.ShapeDtypeStruct((S, D_MODEL), F32), jax.ShapeDtypeStruct((S, HG_W), F32),
                   jax.ShapeDtypeStruct((S, HG_W), BF16), jax.ShapeDtypeStruct((S, AT_W), F32),
                   jax.ShapeDtypeStruct((D_MODEL, D_MODEL), F32), jax.ShapeDtypeStruct((1, D_MODEL), F32),
                   jax.ShapeDtypeStruct((1, HG_W), F32), jax.ShapeDtypeStruct((8, 128), F32)],
        compiler_params=_params(("arbitrary",)),
    )(x2, tgt2, mix_hg, mix_at, w_out_full, fnw, o_hg, p1, hg_norm_w)


def _inproj_bwd_x(dps, w_in_full, x2, norm_w, dh):
    S = x2.shape[0]
    tm = PROJ_TM

    def body(d0, d1, d2, d3, d4, d5, d6, d7, w_ref, x_ref, nw_ref, dh_ref, gx_ref, gnw_ref):
        s = pl.program_id(0)

        @pl.when(s == 0)
        def _():
            gnw_ref[...] = jnp.zeros_like(gnw_ref)

        du = jnp.zeros((tm, D_MODEL), F32)
        for i, dref in enumerate((d0, d1, d2, d3, d4, d5, d6, d7)):
            j, half = divmod(i, 2)
            du = du + _dot_nt(dref[...], w_ref[j, :, 512 * half: 512 * (half + 1)])
        x = x_ref[...]
        r = lax.rsqrt(jnp.mean(x * x, axis=-1, keepdims=True) + EPS)
        xh = x * r
        gnw_ref[...] += jnp.sum(du * xh, axis=0, keepdims=True)
        dun = du * nw_ref[...]
        gx_ref[...] = dh_ref[...] + r * (dun - xh * jnp.mean(dun * xh, axis=-1, keepdims=True))

    row = lambda w: pl.BlockSpec((tm, w), lambda s: (s, 0))
    return pl.pallas_call(
        body, name="inproj_bwd_x", grid=(S // tm,),
        in_specs=[row(512)] * 8 + [_full((4, D_MODEL, 1024)), row(D_MODEL), _full((1, D_MODEL)), row(D_MODEL)],
        out_specs=[row(D_MODEL), _full((1, D_MODEL))],
        out_shape=[jax.ShapeDtypeStruct((S, D_MODEL), F32), jax.ShapeDtypeStruct((1, D_MODEL), F32)],
        compiler_params=_params(("arbitrary",)),
    )(*dps, w_in_full, x2, norm_w, dh)


def _inproj_bwd_w(u, dps):
    S = u.shape[0]
    tm = PROJ_TM

    def body(u_ref, d0, d1, d2, d3, d4, d5, d6, d7, g_ref):
        @pl.when(pl.program_id(0) == 0)
        def _():
            g_ref[...] = jnp.zeros_like(g_ref)

        ub = u_ref[...]
        for i, dref in enumerate((d0, d1, d2, d3, d4, d5, d6, d7)):
            j, half = divmod(i, 2)
            g_ref[j, :, 512 * half: 512 * (half + 1)] += _dot_tn(ub, dref[...])

    return pl.pallas_call(
        body, name="inproj_bwd_w", grid=(S // tm,),
        in_specs=[pl.BlockSpec((tm, D_MODEL), lambda s: (s, 0))] + [pl.BlockSpec((tm, 512), lambda s: (s, 0))] * 8,
        out_specs=_full((4, D_MODEL, 1024)),
        out_shape=jax.ShapeDtypeStruct((4, D_MODEL, 1024), F32),
        compiler_params=_params(("arbitrary",)),
    )(u, *dps)


def _adamw_update(gg, w_ref, m_ref, v_ref, d_ref, nm_ref, nv_ref):
    nm = ADAM_B1 * m_ref[...] + (1.0 - ADAM_B1) * gg
    nv = ADAM_B2 * v_ref[...] + (1.0 - ADAM_B2) * (gg * gg)
    m_hat = nm / (1.0 - ADAM_B1 ** ADAM_STEP)
    v_hat = nv / (1.0 - ADAM_B2 ** ADAM_STEP)
    d_ref[...] = -ADAM_LR * (m_hat / (jnp.sqrt(v_hat) + ADAM_EPS) + ADAM_WD * w_ref[...])
    nm_ref[...] = nm
    nv_ref[...] = nv


def _adamw(w, g, m, v, name):
    rows, cols = w.shape
    tr = min(rows, 256)

    def body(w_ref, g_ref, m_ref, v_ref, d_ref, nm_ref, nv_ref):
        _adamw_update(g_ref[...], w_ref, m_ref, v_ref, d_ref, nm_ref, nv_ref)

    spec = pl.BlockSpec((tr, cols), lambda i: (i, 0))
    sds = jax.ShapeDtypeStruct((rows, cols), F32)
    return pl.pallas_call(
        body, name=name, grid=(rows // tr,),
        in_specs=[spec] * 4, out_specs=[spec] * 3, out_shape=[sds] * 3,
        compiler_params=_params(("parallel",)),
    )(w, g, m, v)


def _place():
    return lax.axis_index("x"), lax.axis_index("y"), lax.axis_index("c")


def _gather_weights(w_in_s, w_out_s):
    def body(win_ref, wout_ref, fin_ref, fout_ref, send_sems, recv_sems):
        x, y, c = _place()
        me = (x, y, c)
        sib = (x, y, 1 - c)
        mine = 2 * x + y
        fin_ref[mine] = win_ref[...].astype(BF16)
        fout_ref[mine] = wout_ref[...].astype(BF16)
        chips = [(1 - x, y), (x, 1 - y), (1 - x, 1 - y)]

        def halves(chip, half):
            return (fin_ref.at[chip, pl.ds(half * 512, 512), :], fout_ref.at[chip, pl.ds(half * 128, 128), :])

        def copy(k, ref, to):
            return pltpu.make_async_remote_copy(src_ref=ref, dst_ref=ref, send_sem=send_sems.at[k],
                                                recv_sem=recv_sems.at[k], device_id=to, device_id_type=MESH)

        first, passed = [], []
        for j, (cx, cy) in enumerate(chips):
            for a, ref in enumerate(halves(mine, c)):
                first.append(copy(2 * j + a, ref, (cx, cy, c)))
        for cp in first:
            cp.start()
        for j, (cx, cy) in enumerate(chips):
            for a, ref in enumerate(halves(2 * cx + cy, c)):
                copy(2 * j + a, ref, me).wait_recv()
                fwd = copy(6 + 2 * j + a, ref, sib)
                fwd.start()
                passed.append(fwd)
        for j, (cx, cy) in enumerate(chips):
            for a, ref in enumerate(halves(2 * cx + cy, 1 - c)):
                copy(6 + 2 * j + a, ref, me).wait_recv()
        for cp in first + passed:
            cp.wait_send()

    vm = pl.BlockSpec(memory_space=pltpu.VMEM)
    return pl.pallas_call(
        body, name="gather_weights",
        in_specs=[vm, vm], out_specs=[vm, vm],
        out_shape=[jax.ShapeDtypeStruct((4, D_MODEL, 1024), BF16), jax.ShapeDtypeStruct((4, 256, D_MODEL), BF16)],
        scratch_shapes=[pltpu.SemaphoreType.DMA((12,)), pltpu.SemaphoreType.DMA((12,))],
        compiler_params=pltpu.CompilerParams(vmem_limit_bytes=VMEM_LIMIT),
    )(w_in_s, w_out_s)


def _swap_halves(g_in, g_out):
    def body(gin_ref, gout_ref, rin_ref, rout_ref, send_sems, recv_sems):
        x, y, c = _place()
        sib = (x, y, 1 - c)
        cps = [pltpu.make_async_remote_copy(src_ref=src.at[:, 1 - c], dst_ref=dst, send_sem=send_sems.at[k],
                                            recv_sem=recv_sems.at[k], device_id=sib, device_id_type=MESH)
               for k, (src, dst) in enumerate(((gin_ref, rin_ref), (gout_ref, rout_ref)))]
        for cp in cps:
            cp.start()
        for cp in cps:
            cp.wait()

    hbm = pl.BlockSpec(memory_space=pl.ANY)
    return pl.pallas_call(
        body, name="swap_halves",
        in_specs=[hbm, hbm], out_specs=[hbm, hbm],
        out_shape=[jax.ShapeDtypeStruct((4,) + g.shape[2:], F32) for g in (g_in, g_out)],
        scratch_shapes=[pltpu.SemaphoreType.DMA((2,)), pltpu.SemaphoreType.DMA((2,))],
    )(g_in, g_out)


def _add_half(g, r, cidx, name):
    n, _, rows, cols = g.shape

    def body(c_ref, g_ref, r_ref, o_ref):
        o_ref[0] = (g_ref[0, 0] + r_ref[0]).astype(BF16)

    return pl.pallas_call(
        body, name=name,
        grid_spec=pltpu.PrefetchScalarGridSpec(
            num_scalar_prefetch=1, grid=(n,),
            in_specs=[pl.BlockSpec((1, 1, rows, cols), lambda j, c_ref: (j, c_ref[0], 0, 0)),
                      pl.BlockSpec((1, rows, cols), lambda j, c_ref: (j, 0, 0))],
            out_specs=pl.BlockSpec((1, rows, cols), lambda j, c_ref: (j, 0, 0))),
        out_shape=jax.ShapeDtypeStruct((n, rows, cols), BF16),
        compiler_params=_params(("parallel",)),
    )(cidx, g, r)


def _exchange_copies(in_ref, out_ref, lin_ref, lout_ref, send_sems, recv_sems):
    x, y, c = _place()
    cps = []
    for k, (cx, cy) in enumerate([(1 - x, y), (x, 1 - y), (1 - x, 1 - y)]):
        for a, (src, dst) in enumerate(((in_ref, lin_ref), (out_ref, lout_ref))):
            cps.append(pltpu.make_async_remote_copy(
                src_ref=src.at[2 * cx + cy], dst_ref=dst.at[k], send_sem=send_sems.at[2 * k + a],
                recv_sem=recv_sems.at[2 * k + a], device_id=(cx, cy, c), device_id_type=MESH))
    return cps


def _exchange_start(cs_in, cs_out):
    def body(in_ref, out_ref, lin_ref, lout_ref, send_sems, recv_sems, in_thru, out_thru, lin_thru, lout_thru, token):
        for cp in _exchange_copies(in_ref, out_ref, lin_ref, lout_ref, send_sems, recv_sems):
            cp.start()
        token[...] = jnp.zeros_like(token)

    lands = [lax.empty((3,) + a.shape[1:], a.dtype) for a in (cs_in, cs_out)]
    bufs = [pltpu.with_memory_space_constraint(a, pltpu.HBM) for a in (cs_in, cs_out, *lands)]
    hbm = pl.BlockSpec(memory_space=pltpu.HBM)
    sem = pl.BlockSpec(memory_space=pltpu.SEMAPHORE)
    return pl.pallas_call(
        body, name="exchange_start",
        in_specs=[hbm] * 4,
        out_specs=[sem, sem, hbm, hbm, hbm, hbm, pl.BlockSpec(memory_space=pltpu.VMEM)],
        out_shape=[pltpu.SemaphoreType.DMA((6,)), pltpu.SemaphoreType.DMA((6,))]
                  + [pltpu.HBM(b.shape, b.dtype) for b in bufs] + [jax.ShapeDtypeStruct((8, 128), F32)],
        input_output_aliases={0: 2, 1: 3, 2: 4, 3: 5},
        compiler_params=pltpu.CompilerParams(has_side_effects=pltpu.SideEffectType.DATAFLOW_SIDE_EFFECTING),
    )(*bufs)


def _exchange_wait(send_sems, recv_sems, in_thru, out_thru, lin_thru, lout_thru, after):
    def body(in_ref, out_ref, lin_ref, lout_ref, send_sems, recv_sems, after_ref, in_dead, out_dead, got_in, got_out):
        for cp in _exchange_copies(in_ref, out_ref, lin_ref, lout_ref, send_sems, recv_sems):
            cp.wait_send()
            cp.wait_recv()

    hbm = pl.BlockSpec(memory_space=pltpu.HBM)
    sem = pl.BlockSpec(memory_space=pltpu.SEMAPHORE)
    bufs = (in_thru, out_thru, lin_thru, lout_thru)
    return pl.pallas_call(
        body, name="exchange_wait",
        in_specs=[hbm] * 4 + [sem, sem, pl.BlockSpec(memory_space=pl.ANY)],
        out_specs=[hbm] * 4,
        out_shape=[pltpu.HBM(b.shape, b.dtype) for b in bufs],
        input_output_aliases={0: 0, 1: 1, 2: 2, 3: 3},
        compiler_params=pltpu.CompilerParams(has_side_effects=pltpu.SideEffectType.DATAFLOW_SIDE_EFFECTING),
    )(*bufs, send_sems, recv_sems, after)


PEER_XOR = (2, 1, 3)


def _sum_chips(cs, r, chip_idx, name):
    _, rows, cols = r.shape
    tr = min(rows, 256)

    def body(m_ref, cs_ref, r_ref, o_ref):
        mine = m_ref[0]
        own = cs_ref[0].astype(F32)
        got = [r_ref[k].astype(F32) for k in range(3)]
        acc = None
        for s in range(4):
            rel = mine ^ s
            term = jnp.where(rel == 0, own, jnp.where(rel == PEER_XOR[0], got[0],
                                                      jnp.where(rel == PEER_XOR[1], got[1], got[2])))
            acc = term if acc is None else acc + term
        o_ref[...] = acc

    return pl.pallas_call(
        body, name=name,
        grid_spec=pltpu.PrefetchScalarGridSpec(
            num_scalar_prefetch=1, grid=(rows // tr,),
            in_specs=[pl.BlockSpec((1, tr, cols), lambda i, m_ref: (m_ref[0], i, 0)),
                      pl.BlockSpec((3, tr, cols), lambda i, m_ref: (0, i, 0))],
            out_specs=pl.BlockSpec((tr, cols), lambda i, m_ref: (i, 0))),
        out_shape=jax.ShapeDtypeStruct((rows, cols), F32),
        compiler_params=_params(("parallel",)),
    )(chip_idx, cs, r)


def _swap_reduced(h_in, h_out):
    def body(in_ref, out_ref, rin_ref, rout_ref, send_sems, recv_sems):
        x, y, c = _place()
        cps = [pltpu.make_async_remote_copy(src_ref=src, dst_ref=dst, send_sem=send_sems.at[k],
                                            recv_sem=recv_sems.at[k], device_id=(x, y, 1 - c), device_id_type=MESH)
               for k, (src, dst) in enumerate(((in_ref, rin_ref), (out_ref, rout_ref)))]
        for cp in cps:
            cp.start()
        for cp in cps:
            cp.wait()

    hbm = pl.BlockSpec(memory_space=pl.ANY)
    return pl.pallas_call(
        body, name="swap_reduced",
        in_specs=[hbm, hbm], out_specs=[hbm, hbm],
        out_shape=[jax.ShapeDtypeStruct(h.shape, F32) for h in (h_in, h_out)],
        scratch_shapes=[pltpu.SemaphoreType.DMA((2,)), pltpu.SemaphoreType.DMA((2,))],
    )(h_in, h_out)


def _adamw_halves(w, mine, theirs, m, v, cidx, name):
    rows, cols = w.shape
    half = rows // 2
    tr = min(half, 256)
    nbh = half // tr

    def body(c_ref, w_ref, a_ref, b_ref, m_ref, v_ref, g_ref, d_ref, nm_ref, nv_ref):
        gg = jnp.where(pl.program_id(0) // nbh == c_ref[0], a_ref[...], b_ref[...])
        g_ref[...] = gg
        _adamw_update(gg, w_ref, m_ref, v_ref, d_ref, nm_ref, nv_ref)

    spec = pl.BlockSpec((tr, cols), lambda i, c_ref: (i, 0))
    hspec = pl.BlockSpec((tr, cols), lambda i, c_ref: (i % nbh, 0))
    sds = jax.ShapeDtypeStruct((rows, cols), F32)
    return pl.pallas_call(
        body, name=name,
        grid_spec=pltpu.PrefetchScalarGridSpec(
            num_scalar_prefetch=1, grid=(rows // tr,),
            in_specs=[spec, hspec, hspec, spec, spec], out_specs=[spec] * 4),
        out_shape=[sds] * 4,
        compiler_params=_params(("parallel",)),
    )(cidx, w, mine, theirs, m, v)


def _allreduce_small(g_nw, g_fw, g_hgw, g_lbl, loss8):
    def body(nw_ref, fw_ref, hgw_ref, lbl_ref, loss_ref, out_ref, slots, send_sems, recv_sems):
        x, y, c = _place()
        me = 4 * x + 2 * y + c
        slots[me] = jnp.zeros((8, D_MODEL), F32)
        slots[me, 0:1, :] = nw_ref[...]
        slots[me, 1:2, :] = fw_ref[...]
        slots[me, 2:3, 0:HG_W] = hgw_ref[...]
        slots[me, 3:4, 0:HG_W] = lbl_ref[0:1, :]
        slots[me, 3:4, HG_W:] = lbl_ref[1:2, :]
        slots[me, 4:5, 0:128] = loss_ref[0:1, :]
        cps = []
        for k in range(1, 8):
            dx, dy, dc = (k >> 2) & 1, (k >> 1) & 1, k & 1
            to = (x ^ dx, y ^ dy, c ^ dc)
            cps.append(pltpu.make_async_remote_copy(
                src_ref=slots.at[me], dst_ref=slots.at[me], send_sem=send_sems.at[k - 1],
                recv_sem=recv_sems.at[k - 1], device_id=to, device_id_type=MESH))
        for cp in cps:
            cp.start()
        for cp in cps:
            cp.wait()
        acc = slots[0]
        for i in range(1, 8):
            acc = acc + slots[i]
        out_ref[...] = acc

    vm = pl.BlockSpec(memory_space=pltpu.VMEM)
    return pl.pallas_call(
        body, name="allreduce_small",
        in_specs=[vm] * 5, out_specs=vm,
        out_shape=jax.ShapeDtypeStruct((8, D_MODEL), F32),
        scratch_shapes=[pltpu.VMEM((8, 8, D_MODEL), F32), pltpu.SemaphoreType.DMA((7,)), pltpu.SemaphoreType.DMA((7,))],
    )(g_nw, g_fw, g_hgw, g_lbl, loss8)


def _rope_tables(S):
    inv_freq = (np.float32(1.0) / np.power(np.float32(ROPE_THETA), np.arange(ROPE_HALF, dtype=np.float32) / np.float32(ROPE_HALF))).astype(np.float32)
    ang = (np.arange(S, dtype=np.float32)[:, None] * inv_freq[None, :]).astype(np.float32)
    cos, sin = np.cos(ang).astype(np.float32), np.sin(ang).astype(np.float32)
    cos2 = np.concatenate([cos, cos, cos, cos], axis=-1)
    sin2 = np.concatenate([-sin, sin, -sin, sin], axis=-1)
    return jnp.asarray(cos2), jnp.asarray(sin2)


def _local_step(x2, tgt2, norm_w, w_in_full, lbl, hg_norm_w, w_out4, fnw):
    S = x2.shape[0]
    cos2, sin2 = _rope_tables(S)
    w_out_full = w_out4.reshape(D_MODEL, D_MODEL)
    p0, p1, qkv, qkv4, qkv16, p3, u = _inproj(x2, norm_w, w_in_full, cos2, sin2)
    o_hg, mix_hg, states = _hg_fwd(p0, p1, lbl, hg_norm_w)
    qkv_ds = [qkv.reshape(1, S, 3 * AT_W), qkv4, qkv16]
    os_d, ls_d = zip(*[_att_fwd(q) for q in qkv_ds])
    o_at, lse, mix_at = _att_combine(os_d, ls_d, p3)
    dh, do_hg, dz_hg, dm_at, g_wout, g_fw, g_hgw, loss8 = _outproj(x2, tgt2, mix_hg, mix_at, w_out_full, fnw, o_hg, p1, hg_norm_w)
    dqr, dfl, dv_hg, g_lbl = _hg_bwd(p0, p1, do_hg, states, lbl)
    do1, do4, do16, dl1, dl4, dl16, dz_at = _att_gate_bwd(dm_at, o_at, lse, p3)
    do_ds = [do1.reshape(1, S, AT_W), do4, do16]
    dl_ds = [dl1.reshape(1, S, STAT_W), dl4, dl16]
    dqkvs = [_att_bwd(q, a, b) for q, a, b in zip(qkv_ds, do_ds, dl_ds)]
    dq_at, dk_at, dv_at = _att_bwd_combine(dqkvs, cos2, sin2)
    dps = [dqr, dfl, dv_hg, dz_hg, dq_at, dk_at, dv_at, dz_at]
    return loss8, dps, u, dh, g_lbl, g_hgw, g_wout, g_fw


def kernel(x, norm_w, w_in, hgrn_lb_logits, hg_norm_w, w_out, final_norm_w, loss_target, m_norm_w, m_w_in, m_hgrn_lb_logits, m_hg_norm_w, m_w_out, m_final_norm_w, v_norm_w, v_w_in, v_hgrn_lb_logits, v_hg_norm_w, v_w_out, v_final_norm_w):
    S = x.shape[1]
    w_in_full, w_out_full = _gather_weights(w_in[0], w_out[0])
    loss8, dps, u, dh, g_lbl, g_hgw, g_wout, g_fw = _local_step(
        x[0], loss_target[0], norm_w, w_in_full, hgrn_lb_logits, hg_norm_w,
        w_out_full, final_norm_w.reshape(1, D_MODEL))

    cidx = lax.axis_index("c").astype(jnp.int32).reshape(1)
    g_win = _inproj_bwd_w(u, dps)
    g_in4 = g_win.reshape(4, 2, 512, 1024)
    g_out4 = g_wout.reshape(4, 2, 128, D_MODEL)
    r_in, r_out = _swap_halves(g_in4, g_out4)
    cs_in = _add_half(g_in4, r_in, cidx, "add_half_in")
    cs_out = _add_half(g_out4, r_out, cidx, "add_half_out")
    *started, token = _exchange_start(cs_in, cs_out)
    grad_x, g_nw = _inproj_bwd_x(dps, w_in_full, x[0], norm_w + token[0:1, 0:1], dh)
    cs_in, cs_out, x_in, x_out = _exchange_wait(*started, g_nw)
    chip_idx = (2 * lax.axis_index("x") + lax.axis_index("y")).astype(jnp.int32).reshape(1)
    h_in = _sum_chips(cs_in, x_in, chip_idx, "sum_chips_in")
    h_out = _sum_chips(cs_out, x_out, chip_idx, "sum_chips_out")
    s_in, s_out = _swap_reduced(h_in, h_out)

    red = _allreduce_small(g_nw, g_fw, g_hgw, g_lbl, loss8)
    loss = red[4, 0]
    grad_norm_w = red[0:1, :]
    grad_final_norm_w = red[1, :]
    grad_hg_norm_w = red[2:3, :HG_W]
    grad_lbl = jnp.concatenate([red[3:4, :HG_W], red[3:4, HG_W:]], axis=0)

    d_nw, m_nw, v_nw = _adamw(norm_w, grad_norm_w, m_norm_w, v_norm_w, "adamw_norm_w")
    grad_w_in, d_win, m_win, v_win = _adamw_halves(w_in[0], h_in, s_in, m_w_in[0], v_w_in[0], cidx, "adamw_w_in")
    d_lbl, m_lbl, v_lbl = _adamw(hgrn_lb_logits, grad_lbl, m_hgrn_lb_logits, v_hgrn_lb_logits, "adamw_lb_logits")
    d_hgw, m_hgw, v_hgw = _adamw(hg_norm_w, grad_hg_norm_w, m_hg_norm_w, v_hg_norm_w, "adamw_hg_norm_w")
    grad_w_out, d_wout, m_wout, v_wout = _adamw_halves(w_out[0], h_out, s_out, m_w_out[0], v_w_out[0], cidx, "adamw_w_out")
    d_fw, m_fw, v_fw = _adamw(final_norm_w.reshape(1, D_MODEL), grad_final_norm_w.reshape(1, D_MODEL),
                              m_final_norm_w.reshape(1, D_MODEL), v_final_norm_w.reshape(1, D_MODEL), "adamw_final_norm_w")
    e1 = lambda a: a[None]
    flat = lambda a: a.reshape(D_MODEL)
    return (loss, grad_x[None], grad_norm_w, e1(grad_w_in), grad_lbl, grad_hg_norm_w, e1(grad_w_out), grad_final_norm_w,
            d_nw, e1(d_win), d_lbl, d_hgw, e1(d_wout), flat(d_fw),
            m_nw, e1(m_win), m_lbl, m_hgw, e1(m_wout), flat(m_fw),
            v_nw, e1(v_win), v_lbl, v_hgw, e1(v_wout), flat(v_fw))
```

```python
import jax
import jax.numpy as jnp
import numpy as np
from jax import lax
from jax.experimental import pallas as pl
from jax.experimental.pallas import tpu as pltpu

F32 = jnp.float32
BF16 = jnp.bfloat16
MESH = pl.DeviceIdType.MESH

D_MODEL = 1024
HG_W = 512
AT_W = 512
HEAD_PAIR = 128
ROPE_HALF = 32
ROPE_THETA = 10000.0
EPS = 1e-6
CHUNK = 128
LEVELS = (64, 32, 16, 8, 4, 2, 1)
DIAG = 1
SUBLANES = 8
ATT_BLK = 128
ATT_QB = 4
DILATIONS = (1, 4, 16)
ATT_SCALE = 0.125
STAT_W = 128
STAT_LANES = 16
STAT_LSE_LANE = 8
NEG = -1e30
VMEM_LIMIT = 56 * 1024 * 1024
MIX_TM = 1024
PROJ_TM = 512

ADAM_LR = 0.001
ADAM_B1 = 0.9
ADAM_B2 = 0.999
ADAM_EPS = 1e-08
ADAM_WD = 0.01
ADAM_STEP = 10


def _iota(shape, dim):
    return lax.broadcasted_iota(jnp.int32, shape, dim)


def _dot(a, b):
    return jnp.dot(a, b, preferred_element_type=F32)


def _dot_nt(a, b):
    return lax.dot_general(a, b, (((1,), (1,)), ((), ())), preferred_element_type=F32)


def _dot_tn(a, b):
    return lax.dot_general(a, b, (((0,), (0,)), ((), ())), preferred_element_type=F32)


def _sigmoid(v):
    return 0.5 * jnp.tanh(0.5 * v) + 0.5


def _params(sem=None, vmem=VMEM_LIMIT):
    return pltpu.CompilerParams(dimension_semantics=sem, vmem_limit_bytes=vmem)


def _full(shape):
    n = len(shape)
    return pl.BlockSpec(shape, lambda *_: (0,) * n)


def _rope_rot(y):
    n = y.shape[1]
    first = (_iota(y.shape, 1) & (2 * ROPE_HALF - 1)) < ROPE_HALF
    return jnp.where(first, pltpu.roll(y, n - ROPE_HALF, 1), pltpu.roll(y, ROPE_HALF, 1))


def _dil_spec(d, tm, width):
    return pl.BlockSpec((d, tm // d, width), lambda s: (0, s, 0))


LANES = 128


def _slab_scratch(tm, width):
    return pltpu.VMEM((width // LANES, tm, LANES), F32)


def _to_slabs(v, slabs_ref):
    for j in range(slabs_ref.shape[0]):
        slabs_ref[j] = v[:, LANES * j: LANES * (j + 1)]


def _from_slabs(slabs_ref):
    return jnp.concatenate([slabs_ref[j] for j in range(slabs_ref.shape[0])], axis=1)


def _split_residues(slabs_ref, dst_ref, d, dtype):
    nslab, tm, _ = slabs_ref.shape
    for r in range(d):
        for j in range(nslab):
            dst_ref[r, :, LANES * j: LANES * (j + 1)] = slabs_ref[j, pl.ds(r, tm // d, stride=d), :].astype(dtype)


def _merge_residues(src_ref, slabs_ref, d):
    nslab, tm, _ = slabs_ref.shape
    for r in range(d):
        for j in range(nslab):
            slabs_ref[j, pl.ds(r, tm // d, stride=d), :] = src_ref[r, :, LANES * j: LANES * (j + 1)].astype(F32)


def _inproj(x2, norm_w, w_in_full, cos2, sin2):
    S = x2.shape[0]
    tm = PROJ_TM

    def body(x_ref, nw_ref, w_ref, cos_ref, sin_ref, p0_ref, p1_ref, qkv_ref, qkv4_ref, qkv16_ref, p3_ref, u_ref, scr):
        x = x_ref[...]
        r = lax.rsqrt(jnp.mean(x * x, axis=-1, keepdims=True) + EPS)
        u = x * r * nw_ref[...]
        ub = u.astype(BF16)
        u_ref[...] = ub
        p0_ref[...] = _dot(ub, w_ref[0])
        p1_ref[...] = _dot(ub, w_ref[1])
        y2 = _dot(ub, w_ref[2])
        cosf = jnp.tile(cos_ref[...], (1, 8))
        sinf = jnp.tile(sin_ref[...], (1, 8))
        y3 = _dot(ub, w_ref[3])
        p3_ref[...] = y3
        qkv = jnp.concatenate([y2 * cosf + _rope_rot(y2) * sinf, y3[:, :AT_W]], axis=1)
        qkv_ref[...] = qkv.astype(BF16)
        _to_slabs(qkv, scr)
        _split_residues(scr, qkv4_ref, 4, BF16)
        _split_residues(scr, qkv16_ref, 16, BF16)

    row = lambda w: pl.BlockSpec((tm, w), lambda s: (s, 0))
    qkv_w = 3 * AT_W
    return pl.pallas_call(
        body, name="inproj", grid=(S // tm,),
        in_specs=[row(D_MODEL), _full((1, D_MODEL)), _full((4, D_MODEL, 1024)), row(128), row(128)],
        out_specs=[row(1024), row(1024), row(qkv_w), _dil_spec(4, tm, qkv_w), _dil_spec(16, tm, qkv_w), row(1024),
                   row(D_MODEL)],
        out_shape=[jax.ShapeDtypeStruct((S, 1024), F32), jax.ShapeDtypeStruct((S, 1024), F32),
                   jax.ShapeDtypeStruct((S, qkv_w), BF16), jax.ShapeDtypeStruct((4, S // 4, qkv_w), BF16),
                   jax.ShapeDtypeStruct((16, S // 16, qkv_w), BF16), jax.ShapeDtypeStruct((S, 1024), F32),
                   jax.ShapeDtypeStruct((S, D_MODEL), BF16)],
        scratch_shapes=[_slab_scratch(tm, qkv_w)],
        compiler_params=_params(("parallel",)),
    )(x2, norm_w, w_in_full, cos2, sin2)


HG_HPS = 4
N_LEV = len(LEVELS)


def _hg_const_arrays():
    r = np.arange(CHUNK)[:, None]
    c = np.arange(CHUNK)[None, :]
    tris = np.stack([r >= c, r <= c])
    lm = [((r // (2 * m)) == (c // (2 * m))) & (r % (2 * m) >= m) & (c % (2 * m) < m) for m in LEVELS]
    dm = [(c == r - dl) & (r % DIAG >= dl) for dl in range(DIAG)]
    masks = np.stack(lm + dm)
    return jnp.asarray(tris, BF16), jnp.asarray(masks, F32)


def _split2(a):
    hi = a.astype(BF16)
    return hi, (a - hi.astype(F32)).astype(BF16)


def _dot3(a, b, dot=_dot):
    ah, al = _split2(a)
    bh, bl = _split2(b)
    n = b.shape[1]
    p = dot(ah, jnp.concatenate([bh, bl], axis=1))
    return (p[:, :n] + p[:, n:]) + dot(al, bh)


def _split3(a):
    a1 = a.astype(BF16)
    r1 = a - a1.astype(F32)
    a2 = r1.astype(BF16)
    return a1, a2, (r1 - a2.astype(F32)).astype(BF16)


def _tri_dot(tri, a):
    n = a.shape[1]
    p = _dot(tri, jnp.concatenate(_split3(a), axis=1))
    return (p[:, :n] + p[:, n:2 * n]) + p[:, 2 * n:]


def _dot_sel(a, sel):
    a1, a2, a3 = _split3(a)
    return (_dot(a1, sel) + _dot(a2, sel)) + _dot(a3, sel)


def _rowsum(t):
    return _dot(t.astype(BF16), jnp.ones((t.shape[1], t.shape[1]), BF16))


def _level_refs(b):
    refs = []
    pos = _iota(b.shape, 0)
    for m in LEVELS:
        if 2 * m >= SUBLANES:
            parts = [jnp.broadcast_to(b[r0 + m - 1: r0 + m, :], (2 * m, b.shape[1])) for r0 in range(0, CHUNK, 2 * m)]
            refs.append(parts[0] if len(parts) == 1 else jnp.concatenate(parts, axis=0))
        else:
            p = pos & (2 * m - 1)
            ref = b
            for off in range(-(m - 1), m + 1):
                if off != 0:
                    ref = jnp.where(p == m - 1 + off, pltpu.roll(b, off % CHUNK, 0), ref)
            refs.append(ref)
    return refs


def _hg_lb(lbl_ref):
    l0 = lbl_ref[0:1, :]
    l1 = lbl_ref[1:2, :]
    mx = jnp.maximum(l0, l1)
    e0 = jnp.exp(l0 - mx)
    e1 = jnp.exp(l1 - mx)
    p0 = e0 / (e0 + e1)
    lb = jnp.clip(p0, 1e-6, 1.0 - 1e-6)
    inside = (p0 >= 1e-6) & (p0 <= 1.0 - 1e-6)
    dlb_dl0 = jnp.where(inside, p0 * (e1 / (e0 + e1)), 0.0)
    return lb, dlb_dl0


def _hg_gates(qr, fl, lb):
    sig = _sigmoid(fl)
    f = lb + (1.0 - lb) * sig
    g = jnp.log(f)
    k = (1.0 - lb) * (1.0 - sig)
    sq = _sigmoid(qr)
    q = qr * sq
    return sig, f, g, k, sq, q


def _hg_levels(q, k, b, mk_ref):
    refs = _level_refs(b)
    a = jnp.zeros((CHUNK, CHUNK), F32)
    es, qts, kts = [], [], []
    for i in range(N_LEV):
        diff = b - refs[i]
        e = jnp.exp(jnp.minimum(diff, -diff))
        qt = (q * e).astype(BF16)
        kt = (k * e).astype(BF16)
        a = a + _dot_nt(qt, kt) * mk_ref[i]
        es.append(e); qts.append(qt); kts.append(kt)
    return a, es, qts, kts


def _hg_specs(nc, rev):
    cc = (lambda c: nc - 1 - c) if rev else (lambda c: c)
    w = 128 * HG_HPS
    blk = lambda off: pl.BlockSpec((CHUNK, w), lambda h, c: (cc(c), h + off))
    vec = pl.BlockSpec((1, w), lambda h, c: (0, h))
    lb2 = pl.BlockSpec((2, w), lambda h, c: (0, h))
    st = pl.BlockSpec((1, HG_HPS, 128, 128), lambda h, c: (cc(c), h, 0, 0))
    consts = [_full((2, CHUNK, CHUNK)), _full((N_LEV + DIAG, CHUNK, CHUNK))]
    return blk, vec, lb2, st, consts


def _hg_fwd(p0, p1, lbl):
    S = p0.shape[0]
    nc = S // CHUNK
    ng = 4 // HG_HPS

    def body(qr_ref, fl_ref, v_ref, lbl_ref, tri_ref, mk_ref, o_ref, st_ref, state):
        c = pl.program_id(1)

        @pl.when(c == 0)
        def _():
            state[...] = jnp.zeros_like(state)

        lb_all, _ = _hg_lb(lbl_ref)
        heads = [slice(128 * hh, 128 * (hh + 1)) for hh in range(HG_HPS)]
        qs, ks, bs, mats = [], [], [], []
        for sl in heads:
            _, _, g, k, _, q = _hg_gates(qr_ref[:, sl], fl_ref[:, sl], lb_all[:, sl])
            qs.append(q); ks.append(k); bs.append(_tri_dot(tri_ref[0], g))
        for hh in range(HG_HPS):
            a, _, _, _ = _hg_levels(qs[hh], ks[hh], bs[hh], mk_ref)
            mats.append(a + _rowsum(qs[hh] * ks[hh]) * mk_ref[N_LEV])
        for hh, sl in enumerate(heads):
            q, k, b, a, v = qs[hh], ks[hh], bs[hh], mats[hh], v_ref[:, sl]
            b_last = b[CHUNK - 1: CHUNK, :]
            st = state[hh]
            st_ref[0, hh] = st
            o = _dot_nt((q * jnp.exp(b)).astype(BF16), st.astype(BF16)) + _dot(a.astype(BF16), v.astype(BF16))
            state[hh] = st * jnp.exp(b_last) + _dot3(v, k * jnp.exp(b_last - b), _dot_tn)
            o_ref[:, sl] = o

    blk, vec, lb2, st_spec, consts = _hg_specs(nc, False)
    tris, masks = _hg_const_arrays()
    return pl.pallas_call(
        body, name="hg_fwd", grid=(ng, nc),
        in_specs=[blk(0), blk(ng), blk(0), lb2] + consts,
        out_specs=[blk(0), st_spec],
        out_shape=[jax.ShapeDtypeStruct((S, HG_W), F32), jax.ShapeDtypeStruct((nc, 4, 128, 128), F32)],
        scratch_shapes=[pltpu.VMEM((HG_HPS, 128, 128), F32)],
        compiler_params=_params(("parallel", "arbitrary")),
    )(p0, p0, p1, lbl, tris, masks)


def _hg_bwd(p0, p1, do_hg, states, lbl):
    S = p0.shape[0]
    nc = S // CHUNK
    ng = 4 // HG_HPS
    w = 128 * HG_HPS

    def body(qr_ref, fl_ref, v_ref, do_ref, st_ref, lbl_ref, tri_ref, mk_ref, mkb_ref,
             dqr_ref, dfl_ref, dv_ref, glbl_ref, dstate, carry, acc_lb):
        c = pl.program_id(1)

        @pl.when(c == 0)
        def _():
            dstate[...] = jnp.zeros_like(dstate)
            carry[...] = jnp.zeros_like(carry)
            acc_lb[...] = jnp.zeros_like(acc_lb)

        lb_all, dlb_dl0 = _hg_lb(lbl_ref)
        heads = [slice(128 * hh, 128 * (hh + 1)) for hh in range(HG_HPS)]
        diag_mask = mk_ref[N_LEV]
        gates, bs, dos = [], [], []
        for sl in heads:
            gt = _hg_gates(qr_ref[:, sl], fl_ref[:, sl], lb_all[:, sl])
            gates.append(gt)
            bs.append(_tri_dot(tri_ref[0], gt[2]))
        for sl in heads:
            dos.append(do_ref[:, sl])
        inter = []
        for hh, sl in enumerate(heads):
            _, _, _, k, _, q = gates[hh]
            b, do, v = bs[hh], dos[hh], v_ref[:, sl]
            b_last = b[CHUNK - 1: CHUNK, :]
            eb = jnp.exp(b)
            edec = jnp.exp(b_last - b)
            dst = dstate[hh]
            dq = _dot3(do, st_ref[0, hh]) * eb
            dk = _dot3(v, dst) * edec
            da = _dot_nt(do.astype(BF16), v.astype(BF16))
            dv_state = _dot_nt((k * edec).astype(BF16), dst.astype(BF16))
            dstate[hh] = dst * jnp.exp(b_last) + _dot3(do, q * eb, _dot_tn)
            inter.append((dq, dk, da, dv_state))
        for hh, sl in enumerate(heads):
            sig, f, _, k, sq, q = gates[hh]
            dq, dk, da, dv_state = inter[hh]
            b = bs[hh]
            db = q * dq - k * dk
            dab = da.astype(BF16)
            refs = _level_refs(b)
            a = _rowsum(q * k) * diag_mask
            for i in range(N_LEV):
                diff = b - refs[i]
                e = jnp.exp(jnp.minimum(diff, -diff))
                qt = (q * e).astype(BF16)
                kt = (k * e).astype(BF16)
                a = a + _dot_nt(qt, kt) * mk_ref[i]
                dam = dab * mkb_ref[i]
                gq = _dot(dam, kt)
                gk = _dot_tn(dam, qt)
                dq = dq + e * gq
                dk = dk + e * gk
                db = db + (qt.astype(F32) * gq - kt.astype(F32) * gk)
            dc = _rowsum(da * diag_mask)
            dq = dq + dc * k
            dk = dk + dc * q
            dv_ref[:, sl] = (_dot_tn(a.astype(BF16), dos[hh].astype(BF16)) + dv_state).astype(BF16)
            dg = _tri_dot(tri_ref[1], db) + carry[0:1, sl]
            carry[0:1, sl] += jnp.sum(db, axis=0, keepdims=True)
            lb = lb_all[:, sl]
            qr = qr_ref[:, sl]
            t = dg / f - dk
            dfl_ref[:, sl] = (t * (1.0 - lb) * sig * (1.0 - sig)).astype(BF16)
            acc_lb[0:1, sl] += jnp.sum(t * (1.0 - sig), axis=0, keepdims=True)
            dqr_ref[:, sl] = (dq * (sq * (1.0 + qr * (1.0 - sq)))).astype(BF16)

        @pl.when(c == nc - 1)
        def _():
            gl0 = acc_lb[0:1, :] * dlb_dl0
            glbl_ref[0:1, :] = gl0
            glbl_ref[1:2, :] = -gl0

    blk, vec, lb2, st_spec, consts = _hg_specs(nc, True)
    tris, masks = _hg_const_arrays()
    act = jax.ShapeDtypeStruct((S, HG_W), BF16)
    return pl.pallas_call(
        body, name="hg_bwd", grid=(ng, nc),
        in_specs=[blk(0), blk(ng), blk(0), blk(0), st_spec, lb2] + consts + consts[1:],
        out_specs=[blk(0), blk(0), blk(0), lb2],
        out_shape=[act, act, act, jax.ShapeDtypeStruct((2, HG_W), F32)],
        scratch_shapes=[pltpu.VMEM((HG_HPS, 128, 128), F32), pltpu.VMEM((8, w), F32), pltpu.VMEM((8, w), F32)],
        compiler_params=_params(("parallel", "arbitrary")),
    )(p0, p0, p1, do_hg, states, lbl, tris, masks, masks.astype(BF16))


def _stat_head(lane):
    return lane >> 4


def _stat_lane(lane):
    return lane & (STAT_LANES - 1)


def _o_head(lane):
    return lane >> 6


def _att_bias():
    qi = np.arange(ATT_BLK)[:, None]
    kj = np.arange(2 * ATT_BLK)[None, :]
    band = (kj >= qi) & (kj <= qi + ATT_BLK)
    qm = np.stack([band & (kj >= ATT_BLK), band])
    cur = (kj < ATT_BLK) & (qi <= kj)
    km = np.stack([cur, cur | ((kj >= ATT_BLK) & (qi >= kj - ATT_BLK))])
    to_bias = lambda m: jnp.asarray(np.where(m, 0.0, NEG), F32)
    return to_bias(qm), to_bias(km)


def _att_fwd(qkv_d):
    d, L, _ = qkv_d.shape
    qb = ATT_QB
    rows = qb * ATT_BLK

    def body(q_ref, kp_ref, kc_ref, vp_ref, vc_ref, bias_ref, o_ref, lse_ref, s_scr, p_scr):
        first = _iota((ATT_BLK, HEAD_PAIR), 1) < 64
        head_of_lane = _stat_head(_iota((ATT_BLK, STAT_W), 1))
        pairs = [slice(HEAD_PAIR * hp, HEAD_PAIR * (hp + 1)) for hp in range(4)]
        blk = lambda i: slice(ATT_BLK * i, ATT_BLK * (i + 1))

        def keys(i, sl, prev_ref, cur_ref):
            before = prev_ref[0, :, sl] if i == 0 else cur_ref[0, blk(i - 1), sl]
            return jnp.concatenate([before, cur_ref[0, blk(i), sl]], axis=0)

        for i in range(qb):
            for hp, sl in enumerate(pairs):
                q2 = q_ref[0, blk(i), sl] * ATT_SCALE
                zero = jnp.zeros_like(q2)
                qs = jnp.concatenate([jnp.where(first, q2, zero), jnp.where(first, zero, q2)], axis=0)
                s_scr[4 * i + hp] = _dot_nt(qs, keys(i, sl, kp_ref, kc_ref))
        stats = []
        for i in range(qb):
            bias = bias_ref[jnp.minimum(pl.program_id(1), 1)] if i == 0 else bias_ref[1]
            bias2 = jnp.concatenate([bias, bias], axis=0)
            for hp in range(4):
                s = s_scr[4 * i + hp] + bias2
                m = jnp.max(s, axis=-1, keepdims=True)
                p = jnp.exp(s - m)
                l = jnp.sum(p, axis=-1, keepdims=True)
                p_scr[4 * i + hp] = p.astype(BF16)
                stats.append((l, m + jnp.log(l)))
        for i in range(qb):
            lse_blk = jnp.zeros((ATT_BLK, STAT_W), F32)
            for hp, sl in enumerate(pairs):
                l, lse = stats[4 * i + hp]
                o = _dot(p_scr[4 * i + hp], keys(i, sl, vp_ref, vc_ref)) / l
                o_ref[0, blk(i), sl] = jnp.where(first, o[:ATT_BLK], o[ATT_BLK:]).astype(BF16)
                lse_blk = jnp.where(head_of_lane == 2 * hp, lse[:ATT_BLK],
                                    jnp.where(head_of_lane == 2 * hp + 1, lse[ATT_BLK:], lse_blk))
            lse_ref[0, blk(i), :] = lse_blk

    cur = lambda j: pl.BlockSpec((1, rows, AT_W), lambda r, n: (r, n, j))
    prev = lambda j: pl.BlockSpec((1, ATT_BLK, AT_W), lambda r, n: (r, jnp.maximum(qb * n - 1, 0), j))
    return pl.pallas_call(
        body, name=f"att_fwd_d{d}", grid=(d, L // rows),
        in_specs=[cur(0), prev(1), cur(1), prev(2), cur(2), _full((2, ATT_BLK, 2 * ATT_BLK))],
        out_specs=[pl.BlockSpec((1, rows, AT_W), lambda r, n: (r, n, 0)), pl.BlockSpec((1, rows, STAT_W), lambda r, n: (r, n, 0))],
        out_shape=[jax.ShapeDtypeStruct((d, L, AT_W), BF16), jax.ShapeDtypeStruct((d, L, STAT_W), F32)],
        scratch_shapes=[pltpu.VMEM((4 * qb, 2 * ATT_BLK, 2 * ATT_BLK), F32), pltpu.VMEM((4 * qb, 2 * ATT_BLK, 2 * ATT_BLK), BF16)],
        compiler_params=_params(("parallel", "parallel")),
    )(qkv_d, qkv_d, qkv_d, qkv_d, qkv_d, _att_bias()[0])


def _att_combine(os_d, ls_d, p3):
    S = p3.shape[0]
    tm = MIX_TM

    def body(oa_ref, ob4_ref, oc16_ref, la_ref, lb4_ref, lc16_ref, z_ref,
             oat_ref, lse_ref, mix_ref, ob_scr, oc_scr, lb_scr, lc_scr):
        _merge_residues(ob4_ref, ob_scr, 4)
        _merge_residues(oc16_ref, oc_scr, 16)
        _merge_residues(lb4_ref, lb_scr, 4)
        _merge_residues(lc16_ref, lc_scr, 16)
        ls = (la_ref[...], lb_scr[0], lc_scr[0])
        mx = jnp.maximum(jnp.maximum(ls[0], ls[1]), ls[2])
        es = [jnp.exp(l - mx) for l in ls]
        zs = es[0] + es[1] + es[2]
        lse_ref[...] = mx + jnp.log(zs)
        spread = ((_o_head(_iota((STAT_W, AT_W), 1)) == _stat_head(_iota((STAT_W, AT_W), 0)))
                  & (_stat_lane(_iota((STAT_W, AT_W), 0)) == 0)).astype(BF16)
        os_ = (oa_ref[...].astype(F32), _from_slabs(ob_scr), _from_slabs(oc_scr))
        o = jnp.zeros((tm, AT_W), F32)
        for e, oi in zip(es, os_):
            hi, lo = _split2(e / zs)
            o = o + (_dot(hi, spread) + _dot(lo, spread)) * oi
        oat_ref[...] = o
        z = z_ref[...]
        mixed = o * (z * _sigmoid(z))
        mix_ref[...] = mixed.astype(BF16)

    row = lambda w: pl.BlockSpec((tm, w), lambda s: (s, 0))
    return pl.pallas_call(
        body, name="att_combine", grid=(S // tm,),
        in_specs=[row(AT_W), _dil_spec(4, tm, AT_W), _dil_spec(16, tm, AT_W),
                  row(STAT_W), _dil_spec(4, tm, STAT_W), _dil_spec(16, tm, STAT_W),
                  pl.BlockSpec((tm, AT_W), lambda s: (s, 1))],
        out_specs=[row(AT_W), row(STAT_W), row(AT_W)],
        out_shape=[jax.ShapeDtypeStruct((S, AT_W), F32), jax.ShapeDtypeStruct((S, STAT_W), F32),
                   jax.ShapeDtypeStruct((S, AT_W), BF16)],
        scratch_shapes=[_slab_scratch(tm, AT_W), _slab_scratch(tm, AT_W), _slab_scratch(tm, STAT_W), _slab_scratch(tm, STAT_W)],
        compiler_params=_params(("parallel",)),
    )(os_d[0].reshape(S, AT_W), os_d[1], os_d[2], ls_d[0].reshape(S, STAT_W), ls_d[1], ls_d[2], p3)


def _att_gate_bwd(dm_at, o_at, lse, p3):
    S = p3.shape[0]
    tm = MIX_TM

    def body(dm_ref, o_ref, l_ref, z_ref, do_ref, do4_ref, do16_ref, dl_ref, dl4_ref, dl16_ref, dz_ref, do_scr, dl_scr):
        o = o_ref[...]
        z = z_ref[...]
        dm = dm_ref[...]
        sz = _sigmoid(z)
        dz_ref[...] = (dm * o * (sz * (1.0 + z * (1.0 - sz)))).astype(BF16)
        do = dm * (z * sz)
        do_ref[...] = do.astype(BF16)
        gather = (_o_head(_iota((AT_W, STAT_W), 0)) == _stat_head(_iota((AT_W, STAT_W), 1))).astype(BF16)
        dl = jnp.where(_stat_lane(_iota((tm, STAT_W), 1)) < STAT_LSE_LANE, _dot_sel(do * o, gather), l_ref[...])
        dl_ref[...] = dl
        _to_slabs(do, do_scr)
        _to_slabs(dl, dl_scr)
        _split_residues(do_scr, do4_ref, 4, BF16)
        _split_residues(do_scr, do16_ref, 16, BF16)
        _split_residues(dl_scr, dl4_ref, 4, F32)
        _split_residues(dl_scr, dl16_ref, 16, F32)

    row = lambda w: pl.BlockSpec((tm, w), lambda s: (s, 0))
    sds = jax.ShapeDtypeStruct
    return pl.pallas_call(
        body, name="att_gate_bwd", grid=(S // tm,),
        in_specs=[row(AT_W), row(AT_W), row(STAT_W), pl.BlockSpec((tm, AT_W), lambda s: (s, 1))],
        out_specs=[row(AT_W), _dil_spec(4, tm, AT_W), _dil_spec(16, tm, AT_W),
                   row(STAT_W), _dil_spec(4, tm, STAT_W), _dil_spec(16, tm, STAT_W), row(AT_W)],
        out_shape=[sds((S, AT_W), BF16), sds((4, S // 4, AT_W), BF16), sds((16, S // 16, AT_W), BF16),
                   sds((S, STAT_W), F32), sds((4, S // 4, STAT_W), F32), sds((16, S // 16, STAT_W), F32),
                   sds((S, AT_W), BF16)],
        scratch_shapes=[_slab_scratch(tm, AT_W), _slab_scratch(tm, STAT_W)],
        compiler_params=_params(("parallel",)),
    )(dm_at, o_at, lse, p3)


def _att_bwd(qkv_d, do_d, dl_d):
    d, L, _ = qkv_d.shape
    nb = L // ATT_BLK
    qb = ATT_QB
    rows = qb * ATT_BLK
    nsteps = L // rows

    def body(qc_ref, qn_ref, kp_ref, kc_ref, vp_ref, vc_ref, ac_ref, an_ref, lc_ref, ln_ref, bq_ref, bk_ref, dqkv_ref,
             s_scr, dp_scr, st_scr, dpt_scr, ds_scr, pt_scr, dst_scr):
        n = pl.program_id(1)
        first = _iota((ATT_BLK, HEAD_PAIR), 1) < 64
        pairs = [slice(HEAD_PAIR * hp, HEAD_PAIR * (hp + 1)) for hp in range(4)]
        blk = lambda i: slice(ATT_BLK * i, ATT_BLK * (i + 1))

        def stack(t):
            zero = jnp.zeros_like(t)
            return jnp.concatenate([jnp.where(first, t, zero), jnp.where(first, zero, t)], axis=0)

        def unstack(t2):
            return jnp.where(first, t2[:ATT_BLK], t2[ATT_BLK:])

        def with_prev(i, sl, prev_ref, cur_ref):
            before = prev_ref[0, :, sl] if i == 0 else cur_ref[0, blk(i - 1), sl]
            return jnp.concatenate([before, cur_ref[0, blk(i), sl]], axis=0)

        def with_next(i, sl, cur_ref, next_ref):
            after = next_ref[0, :, sl] if i == qb - 1 else cur_ref[0, blk(i + 1), sl]
            return jnp.concatenate([cur_ref[0, blk(i), sl], after], axis=0)

        for i in range(qb):
            for hp, sl in enumerate(pairs):
                j = 4 * i + hp
                s_scr[j] = _dot_nt(stack(qc_ref[0, blk(i), sl] * ATT_SCALE), with_prev(i, sl, kp_ref, kc_ref))
                dp_scr[j] = _dot_nt(stack(ac_ref[0, blk(i), sl]), with_prev(i, sl, vp_ref, vc_ref))
                st_scr[j] = _dot_nt(stack(kc_ref[0, blk(i), sl] * ATT_SCALE), with_next(i, sl, qc_ref, qn_ref))
                dpt_scr[j] = _dot_nt(stack(vc_ref[0, blk(i), sl]), with_next(i, sl, ac_ref, an_ref))
        for i in range(qb):
            bias = bq_ref[jnp.minimum(n, 1)] if i == 0 else bq_ref[1]
            bias_t = bk_ref[jnp.minimum(nsteps - 1 - n, 1)] if i == qb - 1 else bk_ref[1]
            bias2 = jnp.concatenate([bias, bias], axis=0)
            bias_t2 = jnp.concatenate([bias_t, bias_t], axis=0)
            dl_c = lc_ref[0, blk(i), :]
            dl_t = with_next(i, slice(None), lc_ref, ln_ref).T
            for hp in range(4):
                j = 4 * i + hp
                at = [STAT_LANES * (2 * hp), STAT_LANES * (2 * hp + 1)]
                col = lambda t, o: jnp.concatenate([t[:, a + o: a + o + 1] for a in at], axis=0)
                p = jnp.exp(s_scr[j] + bias2 - col(dl_c, STAT_LSE_LANE))
                ds_scr[j] = (p * (dp_scr[j] - col(dl_c, 0))).astype(BF16)
                row = lambda t, o: jnp.concatenate([jnp.broadcast_to(t[a + o: a + o + 1, :], (ATT_BLK, 2 * ATT_BLK)) for a in at], axis=0)
                pt = jnp.exp(st_scr[j] + bias_t2 - row(dl_t, STAT_LSE_LANE))
                pt_scr[j] = pt.astype(BF16)
                dst_scr[j] = (pt * (dpt_scr[j] - row(dl_t, 0))).astype(BF16)
        for i in range(qb):
            for hp, sl in enumerate(pairs):
                j = 4 * i + hp
                dq = unstack(_dot(ds_scr[j], with_prev(i, sl, kp_ref, kc_ref))) * ATT_SCALE
                dk = unstack(_dot(dst_scr[j], with_next(i, sl, qc_ref, qn_ref))) * ATT_SCALE
                dv = unstack(_dot(pt_scr[j], with_next(i, sl, ac_ref, an_ref)))
                dqkv_ref[0, blk(i), sl] = dq.astype(BF16)
                dqkv_ref[0, blk(i), AT_W + HEAD_PAIR * hp: AT_W + HEAD_PAIR * (hp + 1)] = dk.astype(BF16)
                dqkv_ref[0, blk(i), 2 * AT_W + HEAD_PAIR * hp: 2 * AT_W + HEAD_PAIR * (hp + 1)] = dv.astype(BF16)

    cur = lambda j: pl.BlockSpec((1, rows, AT_W), lambda r, n: (r, n, j))
    prev = lambda j: pl.BlockSpec((1, ATT_BLK, AT_W), lambda r, n: (r, jnp.maximum(qb * n - 1, 0), j))
    nxt_blk = lambda n: jnp.minimum(qb * (n + 1), nb - 1)
    sq = (4 * qb, 2 * ATT_BLK, 2 * ATT_BLK)
    return pl.pallas_call(
        body, name=f"att_bwd_d{d}", grid=(d, nsteps),
        in_specs=[cur(0), pl.BlockSpec((1, ATT_BLK, AT_W), lambda r, n: (r, nxt_blk(n), 0)), prev(1), cur(1), prev(2), cur(2),
                  pl.BlockSpec((1, rows, AT_W), lambda r, n: (r, n, 0)),
                  pl.BlockSpec((1, ATT_BLK, AT_W), lambda r, n: (r, nxt_blk(n), 0)),
                  pl.BlockSpec((1, rows, STAT_W), lambda r, n: (r, n, 0)),
                  pl.BlockSpec((1, ATT_BLK, STAT_W), lambda r, n: (r, nxt_blk(n), 0)),
                  _full((2, ATT_BLK, 2 * ATT_BLK)), _full((2, ATT_BLK, 2 * ATT_BLK))],
        out_specs=pl.BlockSpec((1, rows, 3 * AT_W), lambda r, n: (r, n, 0)),
        out_shape=jax.ShapeDtypeStruct((d, L, 3 * AT_W), BF16),
        scratch_shapes=[pltpu.VMEM(sq, F32)] * 4 + [pltpu.VMEM(sq, BF16)] * 3,
        compiler_params=_params(("parallel", "parallel")),
    )(qkv_d, qkv_d, qkv_d, qkv_d, qkv_d, qkv_d, do_d, do_d, dl_d, dl_d, *_att_bias())


def _att_bwd_combine(dqkvs, cos2, sin2):
    S = dqkvs[0].shape[1]
    tm = MIX_TM

    def body(a_ref, b4_ref, c16_ref, cos_ref, sin_ref, dq_ref, dk_ref, dv_ref, b_scr, c_scr):
        _merge_residues(b4_ref, b_scr, 4)
        _merge_residues(c16_ref, c_scr, 16)
        t = a_ref[...].astype(F32) + _from_slabs(b_scr) + _from_slabs(c_scr)
        dy = t[:, : 2 * AT_W]
        cosf = jnp.tile(cos_ref[...], (1, 8))
        sinf = jnp.tile(sin_ref[...], (1, 8))
        dx = dy * cosf - _rope_rot(dy) * sinf
        dq_ref[...] = dx[:, :AT_W].astype(BF16)
        dk_ref[...] = dx[:, AT_W:].astype(BF16)
        dv_ref[...] = t[:, 2 * AT_W:].astype(BF16)

    row = lambda w: pl.BlockSpec((tm, w), lambda s: (s, 0))
    act = jax.ShapeDtypeStruct((S, AT_W), BF16)
    return pl.pallas_call(
        body, name="att_bwd_combine", grid=(S // tm,),
        in_specs=[row(3 * AT_W), _dil_spec(4, tm, 3 * AT_W), _dil_spec(16, tm, 3 * AT_W), row(128), row(128)],
        out_specs=[row(AT_W), row(AT_W), row(AT_W)],
        out_shape=[act, act, act],
        scratch_shapes=[_slab_scratch(tm, 3 * AT_W), _slab_scratch(tm, 3 * AT_W)],
        compiler_params=_params(("parallel",)),
    )(dqkvs[0].reshape(S, 3 * AT_W), dqkvs[1], dqkvs[2], cos2, sin2)


def _outproj(x2, tgt2, mix_at, w_out_full, fnw, o_hg, p1, hg_norm_w):
    S = x2.shape[0]
    tm = PROJ_TM
    ns = S // tm

    def body(x_ref, t_ref, ma_ref, w_ref, fw_ref, o_ref, z_ref, hgw_ref,
             dh_ref, doh_ref, dzh_ref, dma_ref, gw_ref, gfw_ref, ghgw_ref, loss_ref):
        s = pl.program_id(0)

        @pl.when(s == 0)
        def _():
            gw_ref[...] = jnp.zeros_like(gw_ref)
            gfw_ref[...] = jnp.zeros_like(gfw_ref)
            ghgw_ref[...] = jnp.zeros_like(ghgw_ref)
            loss_ref[...] = jnp.zeros_like(loss_ref)

        heads = [slice(128 * hh, 128 * (hh + 1)) for hh in range(HG_W // 128)]
        norm = []
        for sl in heads:
            o, z = o_ref[:, sl], z_ref[:, sl]
            rs = lax.rsqrt(jnp.mean(o * o, axis=-1, keepdims=True) + EPS)
            norm.append((rs, o * rs, _sigmoid(z)))
        mh = jnp.concatenate([(oh * hgw_ref[:, sl] * (z_ref[:, sl] * sz)).astype(BF16)
                              for sl, (_, oh, sz) in zip(heads, norm)], axis=1)
        y = _dot(mh, w_ref[:HG_W, :]) + _dot(ma_ref[...], w_ref[HG_W:, :])
        h = x_ref[...] + y
        r = lax.rsqrt(jnp.mean(h * h, axis=-1, keepdims=True) + EPS)
        hn = h * r
        fw = fw_ref[...]
        err = hn * fw - t_ref[...]
        loss_ref[...] += 0.5 * jnp.sum(jnp.mean(err * err, axis=-1, keepdims=True))
        dout = err * (1.0 / D_MODEL)
        gfw_ref[...] += jnp.sum(dout * hn, axis=0, keepdims=True)
        dhn = dout * fw
        dh = r * (dhn - hn * jnp.mean(dhn * hn, axis=-1, keepdims=True))
        dh_ref[...] = dh
        dhb = dh.astype(BF16)
        dma_ref[...] = _dot_nt(dhb, w_ref[HG_W:, :])
        dmh = _dot_nt(dhb, w_ref[:HG_W, :])
        for sl, (rs, oh, sz) in zip(heads, norm):
            z, dmix, gwv = z_ref[:, sl], dmh[:, sl], hgw_ref[:, sl]
            dzh_ref[:, sl] = (dmix * (oh * gwv) * (sz * (1.0 + z * (1.0 - sz)))).astype(BF16)
            don = dmix * (z * sz)
            ghgw_ref[:, sl] += jnp.sum(don * oh, axis=0, keepdims=True)
            dy = don * gwv
            doh_ref[:, sl] = rs * (dy - oh * jnp.mean(dy * oh, axis=-1, keepdims=True))
        gw_ref[:HG_W, :] += _dot_tn(mh, dhb)
        gw_ref[HG_W:, :] += _dot_tn(ma_ref[...], dhb)

    row = lambda w: pl.BlockSpec((tm, w), lambda s: (s, 0))
    return pl.pallas_call(
        body, name="outproj", grid=(ns,),
        in_specs=[row(D_MODEL), row(D_MODEL), row(AT_W),
                  _full((D_MODEL, D_MODEL)), _full((1, D_MODEL)),
                  row(HG_W), pl.BlockSpec((tm, HG_W), lambda s: (s, 1)), _full((1, HG_W))],
        out_specs=[row(D_MODEL), row(HG_W), row(HG_W), row(AT_W), _full((D_MODEL, D_MODEL)), _full((1, D_MODEL)),
                   _full((1, HG_W)), _full((8, 128))],
        out_shape=[jax.ShapeDtypeStruct((S, D_MODEL), F32), jax.ShapeDtypeStruct((S, HG_W), F32),
                   jax.ShapeDtypeStruct((S, HG_W), BF16), jax.ShapeDtypeStruct((S, AT_W), F32),
                   jax.ShapeDtypeStruct((D_MODEL, D_MODEL), F32), jax.ShapeDtypeStruct((1, D_MODEL), F32),
                   jax.ShapeDtypeStruct((1, HG_W), F32), jax.ShapeDtypeStruct((8, 128), F32)],
        compiler_params=_params(("arbitrary",)),
    )(x2, tgt2, mix_at, w_out_full, fnw, o_hg, p1, hg_norm_w)


def _inproj_bwd_x(dps, w_in_full, x2, norm_w, dh):
    S = x2.shape[0]
    tm = PROJ_TM

    def body(d0, d1, d2, d3, d4, d5, d6, d7, w_ref, x_ref, nw_ref, dh_ref, gx_ref, gnw_ref):
        s = pl.program_id(0)

        @pl.when(s == 0)
        def _():
            gnw_ref[...] = jnp.zeros_like(gnw_ref)

        du = jnp.zeros((tm, D_MODEL), F32)
        for i, dref in enumerate((d0, d1, d2, d3, d4, d5, d6, d7)):
            j, half = divmod(i, 2)
            du = du + _dot_nt(dref[...], w_ref[j, :, 512 * half: 512 * (half + 1)])
        x = x_ref[...]
        r = lax.rsqrt(jnp.mean(x * x, axis=-1, keepdims=True) + EPS)
        xh = x * r
        gnw_ref[...] += jnp.sum(du * xh, axis=0, keepdims=True)
        dun = du * nw_ref[...]
        gx_ref[...] = dh_ref[...] + r * (dun - xh * jnp.mean(dun * xh, axis=-1, keepdims=True))

    row = lambda w: pl.BlockSpec((tm, w), lambda s: (s, 0))
    return pl.pallas_call(
        body, name="inproj_bwd_x", grid=(S // tm,),
        in_specs=[row(512)] * 8 + [_full((4, D_MODEL, 1024)), row(D_MODEL), _full((1, D_MODEL)), row(D_MODEL)],
        out_specs=[row(D_MODEL), _full((1, D_MODEL))],
        out_shape=[jax.ShapeDtypeStruct((S, D_MODEL), F32), jax.ShapeDtypeStruct((1, D_MODEL), F32)],
        compiler_params=_params(("arbitrary",)),
    )(*dps, w_in_full, x2, norm_w, dh)


def _inproj_bwd_w(u, dps):
    S = u.shape[0]
    tm = PROJ_TM

    def body(u_ref, d0, d1, d2, d3, d4, d5, d6, d7, g_ref):
        @pl.when(pl.program_id(0) == 0)
        def _():
            g_ref[...] = jnp.zeros_like(g_ref)

        ub = u_ref[...]
        for i, dref in enumerate((d0, d1, d2, d3, d4, d5, d6, d7)):
            j, half = divmod(i, 2)
            g_ref[j, :, 512 * half: 512 * (half + 1)] += _dot_tn(ub, dref[...])

    return pl.pallas_call(
        body, name="inproj_bwd_w", grid=(S // tm,),
        in_specs=[pl.BlockSpec((tm, D_MODEL), lambda s: (s, 0))] + [pl.BlockSpec((tm, 512), lambda s: (s, 0))] * 8,
        out_specs=_full((4, D_MODEL, 1024)),
        out_shape=jax.ShapeDtypeStruct((4, D_MODEL, 1024), F32),
        compiler_params=_params(("arbitrary",)),
    )(u, *dps)


def _adamw_update(gg, w_ref, m_ref, v_ref, d_ref, nm_ref, nv_ref):
    nm = ADAM_B1 * m_ref[...] + (1.0 - ADAM_B1) * gg
    nv = ADAM_B2 * v_ref[...] + (1.0 - ADAM_B2) * (gg * gg)
    m_hat = nm / (1.0 - ADAM_B1 ** ADAM_STEP)
    v_hat = nv / (1.0 - ADAM_B2 ** ADAM_STEP)
    d_ref[...] = -ADAM_LR * (m_hat / (jnp.sqrt(v_hat) + ADAM_EPS) + ADAM_WD * w_ref[...])
    nm_ref[...] = nm
    nv_ref[...] = nv


def _adamw(w, g, m, v, name):
    rows, cols = w.shape
    tr = min(rows, 256)

    def body(w_ref, g_ref, m_ref, v_ref, d_ref, nm_ref, nv_ref):
        _adamw_update(g_ref[...], w_ref, m_ref, v_ref, d_ref, nm_ref, nv_ref)

    spec = pl.BlockSpec((tr, cols), lambda i: (i, 0))
    sds = jax.ShapeDtypeStruct((rows, cols), F32)
    return pl.pallas_call(
        body, name=name, grid=(rows // tr,),
        in_specs=[spec] * 4, out_specs=[spec] * 3, out_shape=[sds] * 3,
        compiler_params=_params(("parallel",)),
    )(w, g, m, v)


def _place():
    return lax.axis_index("x"), lax.axis_index("y"), lax.axis_index("c")


def _gather_weights(w_in_s, w_out_s):
    def body(win_ref, wout_ref, fin_ref, fout_ref, send_sems, recv_sems):
        x, y, c = _place()
        me = (x, y, c)
        sib = (x, y, 1 - c)
        mine = 2 * x + y
        fin_ref[mine] = win_ref[...].astype(BF16)
        fout_ref[mine] = wout_ref[...].astype(BF16)
        chips = [(1 - x, y), (x, 1 - y), (1 - x, 1 - y)]

        def halves(chip, half):
            return (fin_ref.at[chip, pl.ds(half * 512, 512), :], fout_ref.at[chip, pl.ds(half * 128, 128), :])

        def copy(k, ref, to):
            return pltpu.make_async_remote_copy(src_ref=ref, dst_ref=ref, send_sem=send_sems.at[k],
                                                recv_sem=recv_sems.at[k], device_id=to, device_id_type=MESH)

        first, passed = [], []
        for j, (cx, cy) in enumerate(chips):
            for a, ref in enumerate(halves(mine, c)):
                first.append(copy(2 * j + a, ref, (cx, cy, c)))
        for cp in first:
            cp.start()
        for j, (cx, cy) in enumerate(chips):
            for a, ref in enumerate(halves(2 * cx + cy, c)):
                copy(2 * j + a, ref, me).wait_recv()
                fwd = copy(6 + 2 * j + a, ref, sib)
                fwd.start()
                passed.append(fwd)
        for j, (cx, cy) in enumerate(chips):
            for a, ref in enumerate(halves(2 * cx + cy, 1 - c)):
                copy(6 + 2 * j + a, ref, me).wait_recv()
        for cp in first + passed:
            cp.wait_send()

    vm = pl.BlockSpec(memory_space=pltpu.VMEM)
    return pl.pallas_call(
        body, name="gather_weights",
        in_specs=[vm, vm], out_specs=[vm, vm],
        out_shape=[jax.ShapeDtypeStruct((4, D_MODEL, 1024), BF16), jax.ShapeDtypeStruct((4, 256, D_MODEL), BF16)],
        scratch_shapes=[pltpu.SemaphoreType.DMA((12,)), pltpu.SemaphoreType.DMA((12,))],
        compiler_params=pltpu.CompilerParams(vmem_limit_bytes=VMEM_LIMIT),
    )(w_in_s, w_out_s)


def _swap_halves(g_in, g_out):
    def body(gin_ref, gout_ref, rin_ref, rout_ref, send_sems, recv_sems):
        x, y, c = _place()
        sib = (x, y, 1 - c)
        cps = [pltpu.make_async_remote_copy(src_ref=src.at[:, 1 - c], dst_ref=dst, send_sem=send_sems.at[k],
                                            recv_sem=recv_sems.at[k], device_id=sib, device_id_type=MESH)
               for k, (src, dst) in enumerate(((gin_ref, rin_ref), (gout_ref, rout_ref)))]
        for cp in cps:
            cp.start()
        for cp in cps:
            cp.wait()

    hbm = pl.BlockSpec(memory_space=pl.ANY)
    return pl.pallas_call(
        body, name="swap_halves",
        in_specs=[hbm, hbm], out_specs=[hbm, hbm],
        out_shape=[jax.ShapeDtypeStruct((4,) + g.shape[2:], F32) for g in (g_in, g_out)],
        scratch_shapes=[pltpu.SemaphoreType.DMA((2,)), pltpu.SemaphoreType.DMA((2,))],
    )(g_in, g_out)


def _add_half(g, r, cidx, name):
    n, _, rows, cols = g.shape

    def body(c_ref, g_ref, r_ref, o_ref):
        o_ref[0] = (g_ref[0, 0] + r_ref[0]).astype(BF16)

    return pl.pallas_call(
        body, name=name,
        grid_spec=pltpu.PrefetchScalarGridSpec(
            num_scalar_prefetch=1, grid=(n,),
            in_specs=[pl.BlockSpec((1, 1, rows, cols), lambda j, c_ref: (j, c_ref[0], 0, 0)),
                      pl.BlockSpec((1, rows, cols), lambda j, c_ref: (j, 0, 0))],
            out_specs=pl.BlockSpec((1, rows, cols), lambda j, c_ref: (j, 0, 0))),
        out_shape=jax.ShapeDtypeStruct((n, rows, cols), BF16),
        compiler_params=_params(("parallel",)),
    )(cidx, g, r)


def _exchange_copies(in_ref, out_ref, lin_ref, lout_ref, send_sems, recv_sems):
    x, y, c = _place()
    cps = []
    for k, (cx, cy) in enumerate([(1 - x, y), (x, 1 - y), (1 - x, 1 - y)]):
        for a, (src, dst) in enumerate(((in_ref, lin_ref), (out_ref, lout_ref))):
            cps.append(pltpu.make_async_remote_copy(
                src_ref=src.at[2 * cx + cy], dst_ref=dst.at[k], send_sem=send_sems.at[2 * k + a],
                recv_sem=recv_sems.at[2 * k + a], device_id=(cx, cy, c), device_id_type=MESH))
    return cps


def _exchange_start(cs_in, cs_out):
    def body(in_ref, out_ref, lin_ref, lout_ref, send_sems, recv_sems, in_thru, out_thru, lin_thru, lout_thru, token):
        for cp in _exchange_copies(in_ref, out_ref, lin_ref, lout_ref, send_sems, recv_sems):
            cp.start()
        token[...] = jnp.zeros_like(token)

    lands = [lax.empty((3,) + a.shape[1:], a.dtype) for a in (cs_in, cs_out)]
    bufs = [pltpu.with_memory_space_constraint(a, pltpu.HBM) for a in (cs_in, cs_out, *lands)]
    hbm = pl.BlockSpec(memory_space=pltpu.HBM)
    sem = pl.BlockSpec(memory_space=pltpu.SEMAPHORE)
    return pl.pallas_call(
        body, name="exchange_start",
        in_specs=[hbm] * 4,
        out_specs=[sem, sem, hbm, hbm, hbm, hbm, pl.BlockSpec(memory_space=pltpu.VMEM)],
        out_shape=[pltpu.SemaphoreType.DMA((6,)), pltpu.SemaphoreType.DMA((6,))]
                  + [pltpu.HBM(b.shape, b.dtype) for b in bufs] + [jax.ShapeDtypeStruct((8, 128), F32)],
        input_output_aliases={0: 2, 1: 3, 2: 4, 3: 5},
        compiler_params=pltpu.CompilerParams(has_side_effects=pltpu.SideEffectType.DATAFLOW_SIDE_EFFECTING),
    )(*bufs)


def _exchange_wait(send_sems, recv_sems, in_thru, out_thru, lin_thru, lout_thru, after):
    def body(in_ref, out_ref, lin_ref, lout_ref, send_sems, recv_sems, after_ref, in_dead, out_dead, got_in, got_out):
        for cp in _exchange_copies(in_ref, out_ref, lin_ref, lout_ref, send_sems, recv_sems):
            cp.wait_send()
            cp.wait_recv()

    hbm = pl.BlockSpec(memory_space=pltpu.HBM)
    sem = pl.BlockSpec(memory_space=pltpu.SEMAPHORE)
    bufs = (in_thru, out_thru, lin_thru, lout_thru)
    return pl.pallas_call(
        body, name="exchange_wait",
        in_specs=[hbm] * 4 + [sem, sem, pl.BlockSpec(memory_space=pl.ANY)],
        out_specs=[hbm] * 4,
        out_shape=[pltpu.HBM(b.shape, b.dtype) for b in bufs],
        input_output_aliases={0: 0, 1: 1, 2: 2, 3: 3},
        compiler_params=pltpu.CompilerParams(has_side_effects=pltpu.SideEffectType.DATAFLOW_SIDE_EFFECTING),
    )(*bufs, send_sems, recv_sems, after)


PEER_XOR = (2, 1, 3)


def _sum_chips(cs, r, chip_idx, name):
    _, rows, cols = r.shape
    tr = min(rows, 256)

    def body(m_ref, cs_ref, r_ref, o_ref):
        mine = m_ref[0]
        own = cs_ref[0].astype(F32)
        got = [r_ref[k].astype(F32) for k in range(3)]
        acc = None
        for s in range(4):
            rel = mine ^ s
            term = jnp.where(rel == 0, own, jnp.where(rel == PEER_XOR[0], got[0],
                                                      jnp.where(rel == PEER_XOR[1], got[1], got[2])))
            acc = term if acc is None else acc + term
        o_ref[...] = acc

    return pl.pallas_call(
        body, name=name,
        grid_spec=pltpu.PrefetchScalarGridSpec(
            num_scalar_prefetch=1, grid=(rows // tr,),
            in_specs=[pl.BlockSpec((1, tr, cols), lambda i, m_ref: (m_ref[0], i, 0)),
                      pl.BlockSpec((3, tr, cols), lambda i, m_ref: (0, i, 0))],
            out_specs=pl.BlockSpec((tr, cols), lambda i, m_ref: (i, 0))),
        out_shape=jax.ShapeDtypeStruct((rows, cols), F32),
        compiler_params=_params(("parallel",)),
    )(chip_idx, cs, r)


def _swap_reduced(h_in, h_out):
    def body(in_ref, out_ref, rin_ref, rout_ref, send_sems, recv_sems):
        x, y, c = _place()
        cps = [pltpu.make_async_remote_copy(src_ref=src, dst_ref=dst, send_sem=send_sems.at[k],
                                            recv_sem=recv_sems.at[k], device_id=(x, y, 1 - c), device_id_type=MESH)
               for k, (src, dst) in enumerate(((in_ref, rin_ref), (out_ref, rout_ref)))]
        for cp in cps:
            cp.start()
        for cp in cps:
            cp.wait()

    hbm = pl.BlockSpec(memory_space=pl.ANY)
    return pl.pallas_call(
        body, name="swap_reduced",
        in_specs=[hbm, hbm], out_specs=[hbm, hbm],
        out_shape=[jax.ShapeDtypeStruct(h.shape, F32) for h in (h_in, h_out)],
        scratch_shapes=[pltpu.SemaphoreType.DMA((2,)), pltpu.SemaphoreType.DMA((2,))],
    )(h_in, h_out)


def _adamw_halves(w, mine, theirs, m, v, cidx, name):
    rows, cols = w.shape
    half = rows // 2
    tr = min(half, 256)
    nbh = half // tr

    def body(c_ref, w_ref, a_ref, b_ref, m_ref, v_ref, g_ref, d_ref, nm_ref, nv_ref):
        gg = jnp.where(pl.program_id(0) // nbh == c_ref[0], a_ref[...], b_ref[...])
        g_ref[...] = gg
        _adamw_update(gg, w_ref, m_ref, v_ref, d_ref, nm_ref, nv_ref)

    spec = pl.BlockSpec((tr, cols), lambda i, c_ref: (i, 0))
    hspec = pl.BlockSpec((tr, cols), lambda i, c_ref: (i % nbh, 0))
    sds = jax.ShapeDtypeStruct((rows, cols), F32)
    return pl.pallas_call(
        body, name=name,
        grid_spec=pltpu.PrefetchScalarGridSpec(
            num_scalar_prefetch=1, grid=(rows // tr,),
            in_specs=[spec, hspec, hspec, spec, spec], out_specs=[spec] * 4),
        out_shape=[sds] * 4,
        compiler_params=_params(("parallel",)),
    )(cidx, w, mine, theirs, m, v)


def _allreduce_small(g_nw, g_fw, g_hgw, g_lbl, loss8):
    def body(nw_ref, fw_ref, hgw_ref, lbl_ref, loss_ref, out_ref, slots, send_sems, recv_sems):
        x, y, c = _place()
        me = 4 * x + 2 * y + c
        slots[me] = jnp.zeros((8, D_MODEL), F32)
        slots[me, 0:1, :] = nw_ref[...]
        slots[me, 1:2, :] = fw_ref[...]
        slots[me, 2:3, 0:HG_W] = hgw_ref[...]
        slots[me, 3:4, 0:HG_W] = lbl_ref[0:1, :]
        slots[me, 3:4, HG_W:] = lbl_ref[1:2, :]
        slots[me, 4:5, 0:128] = loss_ref[0:1, :]
        cps = []
        for k in range(1, 8):
            dx, dy, dc = (k >> 2) & 1, (k >> 1) & 1, k & 1
            to = (x ^ dx, y ^ dy, c ^ dc)
            cps.append(pltpu.make_async_remote_copy(
                src_ref=slots.at[me], dst_ref=slots.at[me], send_sem=send_sems.at[k - 1],
                recv_sem=recv_sems.at[k - 1], device_id=to, device_id_type=MESH))
        for cp in cps:
            cp.start()
        for cp in cps:
            cp.wait()
        acc = slots[0]
        for i in range(1, 8):
            acc = acc + slots[i]
        out_ref[...] = acc

    vm = pl.BlockSpec(memory_space=pltpu.VMEM)
    return pl.pallas_call(
        body, name="allreduce_small",
        in_specs=[vm] * 5, out_specs=vm,
        out_shape=jax.ShapeDtypeStruct((8, D_MODEL), F32),
        scratch_shapes=[pltpu.VMEM((8, 8, D_MODEL), F32), pltpu.SemaphoreType.DMA((7,)), pltpu.SemaphoreType.DMA((7,))],
    )(g_nw, g_fw, g_hgw, g_lbl, loss8)


def _rope_tables(S):
    inv_freq = (np.float32(1.0) / np.power(np.float32(ROPE_THETA), np.arange(ROPE_HALF, dtype=np.float32) / np.float32(ROPE_HALF))).astype(np.float32)
    ang = (np.arange(S, dtype=np.float32)[:, None] * inv_freq[None, :]).astype(np.float32)
    cos, sin = np.cos(ang).astype(np.float32), np.sin(ang).astype(np.float32)
    cos2 = np.concatenate([cos, cos, cos, cos], axis=-1)
    sin2 = np.concatenate([-sin, sin, -sin, sin], axis=-1)
    return jnp.asarray(cos2), jnp.asarray(sin2)


def _local_step(x2, tgt2, norm_w, w_in_full, lbl, hg_norm_w, w_out4, fnw):
    S = x2.shape[0]
    cos2, sin2 = _rope_tables(S)
    w_out_full = w_out4.reshape(D_MODEL, D_MODEL)
    p0, p1, qkv, qkv4, qkv16, p3, u = _inproj(x2, norm_w, w_in_full, cos2, sin2)
    o_hg, states = _hg_fwd(p0, p1, lbl)
    qkv_ds = [qkv.reshape(1, S, 3 * AT_W), qkv4, qkv16]
    os_d, ls_d = zip(*[_att_fwd(q) for q in qkv_ds])
    o_at, lse, mix_at = _att_combine(os_d, ls_d, p3)
    dh, do_hg, dz_hg, dm_at, g_wout, g_fw, g_hgw, loss8 = _outproj(x2, tgt2, mix_at, w_out_full, fnw, o_hg, p1, hg_norm_w)
    dqr, dfl, dv_hg, g_lbl = _hg_bwd(p0, p1, do_hg, states, lbl)
    do1, do4, do16, dl1, dl4, dl16, dz_at = _att_gate_bwd(dm_at, o_at, lse, p3)
    do_ds = [do1.reshape(1, S, AT_W), do4, do16]
    dl_ds = [dl1.reshape(1, S, STAT_W), dl4, dl16]
    dqkvs = [_att_bwd(q, a, b) for q, a, b in zip(qkv_ds, do_ds, dl_ds)]
    dq_at, dk_at, dv_at = _att_bwd_combine(dqkvs, cos2, sin2)
    dps = [dqr, dfl, dv_hg, dz_hg, dq_at, dk_at, dv_at, dz_at]
    return loss8, dps, u, dh, g_lbl, g_hgw, g_wout, g_fw


def kernel(x, norm_w, w_in, hgrn_lb_logits, hg_norm_w, w_out, final_norm_w, loss_target, m_norm_w, m_w_in, m_hgrn_lb_logits, m_hg_norm_w, m_w_out, m_final_norm_w, v_norm_w, v_w_in, v_hgrn_lb_logits, v_hg_norm_w, v_w_out, v_final_norm_w):
    S = x.shape[1]
    w_in_full, w_out_full = _gather_weights(w_in[0], w_out[0])
    loss8, dps, u, dh, g_lbl, g_hgw, g_wout, g_fw = _local_step(
        x[0], loss_target[0], norm_w, w_in_full, hgrn_lb_logits, hg_norm_w,
        w_out_full, final_norm_w.reshape(1, D_MODEL))

    cidx = lax.axis_index("c").astype(jnp.int32).reshape(1)
    g_win = _inproj_bwd_w(u, dps)
    g_in4 = g_win.reshape(4, 2, 512, 1024)
    g_out4 = g_wout.reshape(4, 2, 128, D_MODEL)
    r_in, r_out = _swap_halves(g_in4, g_out4)
    cs_in = _add_half(g_in4, r_in, cidx, "add_half_in")
    cs_out = _add_half(g_out4, r_out, cidx, "add_half_out")
    *started, token = _exchange_start(cs_in, cs_out)
    grad_x, g_nw = _inproj_bwd_x(dps, w_in_full, x[0], norm_w + token[0:1, 0:1], dh)
    cs_in, cs_out, x_in, x_out = _exchange_wait(*started, g_nw)
    chip_idx = (2 * lax.axis_index("x") + lax.axis_index("y")).astype(jnp.int32).reshape(1)
    h_in = _sum_chips(cs_in, x_in, chip_idx, "sum_chips_in")
    h_out = _sum_chips(cs_out, x_out, chip_idx, "sum_chips_out")
    s_in, s_out = _swap_reduced(h_in, h_out)

    red = _allreduce_small(g_nw, g_fw, g_hgw, g_lbl, loss8)
    loss = red[4, 0]
    grad_norm_w = red[0:1, :]
    grad_final_norm_w = red[1, :]
    grad_hg_norm_w = red[2:3, :HG_W]
    grad_lbl = jnp.concatenate([red[3:4, :HG_W], red[3:4, HG_W:]], axis=0)

    d_nw, m_nw, v_nw = _adamw(norm_w, grad_norm_w, m_norm_w, v_norm_w, "adamw_norm_w")
    grad_w_in, d_win, m_win, v_win = _adamw_halves(w_in[0], h_in, s_in, m_w_in[0], v_w_in[0], cidx, "adamw_w_in")
    d_lbl, m_lbl, v_lbl = _adamw(hgrn_lb_logits, grad_lbl, m_hgrn_lb_logits, v_hgrn_lb_logits, "adamw_lb_logits")
    d_hgw, m_hgw, v_hgw = _adamw(hg_norm_w, grad_hg_norm_w, m_hg_norm_w, v_hg_norm_w, "adamw_hg_norm_w")
    grad_w_out, d_wout, m_wout, v_wout = _adamw_halves(w_out[0], h_out, s_out, m_w_out[0], v_w_out[0], cidx, "adamw_w_out")
    d_fw, m_fw, v_fw = _adamw(final_norm_w.reshape(1, D_MODEL), grad_final_norm_w.reshape(1, D_MODEL),
                              m_final_norm_w.reshape(1, D_MODEL), v_final_norm_w.reshape(1, D_MODEL), "adamw_final_norm_w")
    e1 = lambda a: a[None]
    flat = lambda a: a.reshape(D_MODEL)
    return (loss, grad_x[None], grad_norm_w, e1(grad_w_in), grad_lbl, grad_hg_norm_w, e1(grad_w_out), grad_final_norm_w,
            d_nw, e1(d_win), d_lbl, d_hgw, e1(d_wout), flat(d_fw),
            m_nw, e1(m_win), m_lbl, m_hgw, e1(m_wout), flat(m_fw),
            v_nw, e1(v_win), v_lbl, v_hgw, e1(v_wout), flat(v_fw))
```

```python
import jax
import jax.numpy as jnp
import numpy as np
from jax import lax
from jax.experimental import pallas as pl
from jax.experimental.pallas import tpu as pltpu

F32 = jnp.float32
BF16 = jnp.bfloat16
MESH = pl.DeviceIdType.MESH

D_MODEL = 1024
HG_W = 512
AT_W = 512
HEAD_PAIR = 128
ROPE_HALF = 32
ROPE_THETA = 10000.0
EPS = 1e-6
CHUNK = 128
LEVELS = (64, 32, 16, 8, 4, 2, 1)
DIAG = 1
SUBLANES = 8
ATT_BLK = 128
ATT_QB = 4
DILATIONS = (1, 4, 16)
ATT_SCALE = 0.125
STAT_W = 128
STAT_LANES = 16
STAT_LSE_LANE = 8
NEG = -1e30
VMEM_LIMIT = 56 * 1024 * 1024
MIX_TM = 1024
PROJ_TM = 512

ADAM_LR = 0.001
ADAM_B1 = 0.9
ADAM_B2 = 0.999
ADAM_EPS = 1e-08
ADAM_WD = 0.01
ADAM_STEP = 10


def _iota(shape, dim):
    return lax.broadcasted_iota(jnp.int32, shape, dim)


def _dot(a, b):
    return jnp.dot(a, b, preferred_element_type=F32)


def _dot_nt(a, b):
    return lax.dot_general(a, b, (((1,), (1,)), ((), ())), preferred_element_type=F32)


def _dot_tn(a, b):
    return lax.dot_general(a, b, (((0,), (0,)), ((), ())), preferred_element_type=F32)


def _sigmoid(v):
    return 0.5 * jnp.tanh(0.5 * v) + 0.5


def _params(sem=None, vmem=VMEM_LIMIT):
    return pltpu.CompilerParams(dimension_semantics=sem, vmem_limit_bytes=vmem)


def _full(shape):
    n = len(shape)
    return pl.BlockSpec(shape, lambda *_: (0,) * n)


def _rope_rot(y):
    n = y.shape[1]
    first = (_iota(y.shape, 1) & (2 * ROPE_HALF - 1)) < ROPE_HALF
    return jnp.where(first, pltpu.roll(y, n - ROPE_HALF, 1), pltpu.roll(y, ROPE_HALF, 1))


def _dil_spec(d, tm, width):
    return pl.BlockSpec((d, tm // d, width), lambda s: (0, s, 0))


LANES = 128


def _slab_scratch(tm, width):
    return pltpu.VMEM((width // LANES, tm, LANES), F32)


def _to_slabs(v, slabs_ref):
    for j in range(slabs_ref.shape[0]):
        slabs_ref[j] = v[:, LANES * j: LANES * (j + 1)]


def _from_slabs(slabs_ref):
    return jnp.concatenate([slabs_ref[j] for j in range(slabs_ref.shape[0])], axis=1)


def _split_residues(slabs_ref, dst_ref, d, dtype):
    nslab, tm, _ = slabs_ref.shape
    for r in range(d):
        for j in range(nslab):
            dst_ref[r, :, LANES * j: LANES * (j + 1)] = slabs_ref[j, pl.ds(r, tm // d, stride=d), :].astype(dtype)


def _merge_residues(src_ref, slabs_ref, d):
    nslab, tm, _ = slabs_ref.shape
    for r in range(d):
        for j in range(nslab):
            slabs_ref[j, pl.ds(r, tm // d, stride=d), :] = src_ref[r, :, LANES * j: LANES * (j + 1)].astype(F32)


def _prenorm(x2, norm_w):
    S = x2.shape[0]
    tm = MIX_TM

    def body(x_ref, nw_ref, u_ref):
        x = x_ref[...]
        r = lax.rsqrt(jnp.mean(x * x, axis=-1, keepdims=True) + EPS)
        u_ref[...] = (x * r * nw_ref[...]).astype(BF16)

    row = pl.BlockSpec((tm, D_MODEL), lambda s: (s, 0))
    return pl.pallas_call(
        body, name="prenorm", grid=(S // tm,),
        in_specs=[row, _full((1, D_MODEL))], out_specs=row,
        out_shape=jax.ShapeDtypeStruct((S, D_MODEL), BF16),
        compiler_params=_params(("parallel",)),
    )(x2, norm_w)


def _inproj(u, w_in_full, cos2, sin2):
    S = u.shape[0]
    tm = PROJ_TM

    def body(u_ref, w_ref, cos_ref, sin_ref, p0_ref, p1_ref, qkv_ref, qkv4_ref, qkv16_ref, p3_ref, scr):
        ub = u_ref[...]
        p0_ref[...] = _dot(ub, w_ref[0])
        p1_ref[...] = _dot(ub, w_ref[1])
        y2 = _dot(ub, w_ref[2])
        cosf = jnp.tile(cos_ref[...], (1, 8))
        sinf = jnp.tile(sin_ref[...], (1, 8))
        y3 = _dot(ub, w_ref[3])
        p3_ref[...] = y3
        qkv = jnp.concatenate([y2 * cosf + _rope_rot(y2) * sinf, y3[:, :AT_W]], axis=1)
        qkv_ref[...] = qkv.astype(BF16)
        _to_slabs(qkv, scr)
        _split_residues(scr, qkv4_ref, 4, BF16)
        _split_residues(scr, qkv16_ref, 16, BF16)

    row = lambda w: pl.BlockSpec((tm, w), lambda s: (s, 0))
    qkv_w = 3 * AT_W
    return pl.pallas_call(
        body, name="inproj", grid=(S // tm,),
        in_specs=[row(D_MODEL), _full((4, D_MODEL, 1024)), row(128), row(128)],
        out_specs=[row(1024), row(1024), row(qkv_w), _dil_spec(4, tm, qkv_w), _dil_spec(16, tm, qkv_w), row(1024)],
        out_shape=[jax.ShapeDtypeStruct((S, 1024), F32), jax.ShapeDtypeStruct((S, 1024), F32),
                   jax.ShapeDtypeStruct((S, qkv_w), BF16), jax.ShapeDtypeStruct((4, S // 4, qkv_w), BF16),
                   jax.ShapeDtypeStruct((16, S // 16, qkv_w), BF16), jax.ShapeDtypeStruct((S, 1024), F32)],
        scratch_shapes=[_slab_scratch(tm, qkv_w)],
        compiler_params=_params(("parallel",)),
    )(u, w_in_full, cos2, sin2)


HG_HPS = 4
N_LEV = len(LEVELS)


def _hg_const_arrays():
    r = np.arange(CHUNK)[:, None]
    c = np.arange(CHUNK)[None, :]
    tris = np.stack([r >= c, r <= c])
    lm = [((r // (2 * m)) == (c // (2 * m))) & (r % (2 * m) >= m) & (c % (2 * m) < m) for m in LEVELS]
    dm = [(c == r - dl) & (r % DIAG >= dl) for dl in range(DIAG)]
    masks = np.stack(lm + dm)
    return jnp.asarray(tris, BF16), jnp.asarray(masks, F32)


def _split2(a):
    hi = a.astype(BF16)
    return hi, (a - hi.astype(F32)).astype(BF16)


def _dot3(a, b, dot=_dot):
    ah, al = _split2(a)
    bh, bl = _split2(b)
    n = b.shape[1]
    p = dot(ah, jnp.concatenate([bh, bl], axis=1))
    return (p[:, :n] + p[:, n:]) + dot(al, bh)


def _split3(a):
    a1 = a.astype(BF16)
    r1 = a - a1.astype(F32)
    a2 = r1.astype(BF16)
    return a1, a2, (r1 - a2.astype(F32)).astype(BF16)


def _tri_dot(tri, a):
    n = a.shape[1]
    p = _dot(tri, jnp.concatenate(_split3(a), axis=1))
    return (p[:, :n] + p[:, n:2 * n]) + p[:, 2 * n:]


def _dot_sel(a, sel):
    a1, a2, a3 = _split3(a)
    return (_dot(a1, sel) + _dot(a2, sel)) + _dot(a3, sel)


def _rowsum(t):
    return _dot(t.astype(BF16), jnp.ones((t.shape[1], t.shape[1]), BF16))


def _level_refs(b):
    refs = []
    pos = _iota(b.shape, 0)
    for m in LEVELS:
        if 2 * m >= SUBLANES:
            parts = [jnp.broadcast_to(b[r0 + m - 1: r0 + m, :], (2 * m, b.shape[1])) for r0 in range(0, CHUNK, 2 * m)]
            refs.append(parts[0] if len(parts) == 1 else jnp.concatenate(parts, axis=0))
        else:
            p = pos & (2 * m - 1)
            ref = b
            for off in range(-(m - 1), m + 1):
                if off != 0:
                    ref = jnp.where(p == m - 1 + off, pltpu.roll(b, off % CHUNK, 0), ref)
            refs.append(ref)
    return refs


def _hg_lb(lbl_ref):
    l0 = lbl_ref[0:1, :]
    l1 = lbl_ref[1:2, :]
    mx = jnp.maximum(l0, l1)
    e0 = jnp.exp(l0 - mx)
    e1 = jnp.exp(l1 - mx)
    p0 = e0 / (e0 + e1)
    lb = jnp.clip(p0, 1e-6, 1.0 - 1e-6)
    inside = (p0 >= 1e-6) & (p0 <= 1.0 - 1e-6)
    dlb_dl0 = jnp.where(inside, p0 * (e1 / (e0 + e1)), 0.0)
    return lb, dlb_dl0


def _hg_gates(qr, fl, lb):
    sig = _sigmoid(fl)
    f = lb + (1.0 - lb) * sig
    g = jnp.log(f)
    k = (1.0 - lb) * (1.0 - sig)
    sq = _sigmoid(qr)
    q = qr * sq
    return sig, f, g, k, sq, q


def _hg_levels(q, k, b, mk_ref):
    refs = _level_refs(b)
    a = jnp.zeros((CHUNK, CHUNK), F32)
    es, qts, kts = [], [], []
    for i in range(N_LEV):
        diff = b - refs[i]
        e = jnp.exp(jnp.minimum(diff, -diff))
        qt = (q * e).astype(BF16)
        kt = (k * e).astype(BF16)
        a = a + _dot_nt(qt, kt) * mk_ref[i]
        es.append(e); qts.append(qt); kts.append(kt)
    return a, es, qts, kts


def _hg_specs(nc, rev):
    cc = (lambda c: nc - 1 - c) if rev else (lambda c: c)
    w = 128 * HG_HPS
    blk = lambda off: pl.BlockSpec((CHUNK, w), lambda h, c: (cc(c), h + off))
    vec = pl.BlockSpec((1, w), lambda h, c: (0, h))
    lb2 = pl.BlockSpec((2, w), lambda h, c: (0, h))
    st = pl.BlockSpec((1, HG_HPS, 128, 128), lambda h, c: (cc(c), h, 0, 0))
    consts = [_full((2, CHUNK, CHUNK)), _full((N_LEV + DIAG, CHUNK, CHUNK))]
    return blk, vec, lb2, st, consts


def _hg_fwd(p0, p1, lbl):
    S = p0.shape[0]
    nc = S // CHUNK
    ng = 4 // HG_HPS

    def body(qr_ref, fl_ref, v_ref, lbl_ref, tri_ref, mk_ref, o_ref, st_ref, state):
        c = pl.program_id(1)

        @pl.when(c == 0)
        def _():
            state[...] = jnp.zeros_like(state)

        lb_all, _ = _hg_lb(lbl_ref)
        heads = [slice(128 * hh, 128 * (hh + 1)) for hh in range(HG_HPS)]
        qs, ks, bs, mats = [], [], [], []
        for sl in heads:
            _, _, g, k, _, q = _hg_gates(qr_ref[:, sl], fl_ref[:, sl], lb_all[:, sl])
            qs.append(q); ks.append(k); bs.append(_tri_dot(tri_ref[0], g))
        for hh in range(HG_HPS):
            a, _, _, _ = _hg_levels(qs[hh], ks[hh], bs[hh], mk_ref)
            mats.append(a + _rowsum(qs[hh] * ks[hh]) * mk_ref[N_LEV])
        for hh, sl in enumerate(heads):
            q, k, b, a, v = qs[hh], ks[hh], bs[hh], mats[hh], v_ref[:, sl]
            b_last = b[CHUNK - 1: CHUNK, :]
            st = state[hh]
            st_ref[0, hh] = st
            o = _dot_nt((q * jnp.exp(b)).astype(BF16), st.astype(BF16)) + _dot(a.astype(BF16), v.astype(BF16))
            state[hh] = st * jnp.exp(b_last) + _dot3(v, k * jnp.exp(b_last - b), _dot_tn)
            o_ref[:, sl] = o

    blk, vec, lb2, st_spec, consts = _hg_specs(nc, False)
    tris, masks = _hg_const_arrays()
    return pl.pallas_call(
        body, name="hg_fwd", grid=(ng, nc),
        in_specs=[blk(0), blk(ng), blk(0), lb2] + consts,
        out_specs=[blk(0), st_spec],
        out_shape=[jax.ShapeDtypeStruct((S, HG_W), F32), jax.ShapeDtypeStruct((nc, 4, 128, 128), F32)],
        scratch_shapes=[pltpu.VMEM((HG_HPS, 128, 128), F32)],
        compiler_params=_params(("parallel", "arbitrary")),
    )(p0, p0, p1, lbl, tris, masks)


def _hg_bwd(p0, p1, do_hg, states, lbl):
    S = p0.shape[0]
    nc = S // CHUNK
    ng = 4 // HG_HPS
    w = 128 * HG_HPS

    def body(qr_ref, fl_ref, v_ref, do_ref, st_ref, lbl_ref, tri_ref, mk_ref, mkb_ref,
             dqr_ref, dfl_ref, dv_ref, glbl_ref, dstate, carry, acc_lb):
        c = pl.program_id(1)

        @pl.when(c == 0)
        def _():
            dstate[...] = jnp.zeros_like(dstate)
            carry[...] = jnp.zeros_like(carry)
            acc_lb[...] = jnp.zeros_like(acc_lb)

        lb_all, dlb_dl0 = _hg_lb(lbl_ref)
        heads = [slice(128 * hh, 128 * (hh + 1)) for hh in range(HG_HPS)]
        diag_mask = mk_ref[N_LEV]
        gates, bs, dos = [], [], []
        for sl in heads:
            gt = _hg_gates(qr_ref[:, sl], fl_ref[:, sl], lb_all[:, sl])
            gates.append(gt)
            bs.append(_tri_dot(tri_ref[0], gt[2]))
        for sl in heads:
            dos.append(do_ref[:, sl])
        inter = []
        for hh, sl in enumerate(heads):
            _, _, _, k, _, q = gates[hh]
            b, do, v = bs[hh], dos[hh], v_ref[:, sl]
            b_last = b[CHUNK - 1: CHUNK, :]
            eb = jnp.exp(b)
            edec = jnp.exp(b_last - b)
            dst = dstate[hh]
            dq = _dot3(do, st_ref[0, hh]) * eb
            dk = _dot3(v, dst) * edec
            da = _dot_nt(do.astype(BF16), v.astype(BF16))
            dv_state = _dot_nt((k * edec).astype(BF16), dst.astype(BF16))
            dstate[hh] = dst * jnp.exp(b_last) + _dot3(do, q * eb, _dot_tn)
            inter.append((dq, dk, da, dv_state))
        for hh, sl in enumerate(heads):
            sig, f, _, k, sq, q = gates[hh]
            dq, dk, da, dv_state = inter[hh]
            b = bs[hh]
            db = q * dq - k * dk
            dab = da.astype(BF16)
            refs = _level_refs(b)
            a = _rowsum(q * k) * diag_mask
            for i in range(N_LEV):
                diff = b - refs[i]
                e = jnp.exp(jnp.minimum(diff, -diff))
                qt = (q * e).astype(BF16)
                kt = (k * e).astype(BF16)
                a = a + _dot_nt(qt, kt) * mk_ref[i]
                dam = dab * mkb_ref[i]
                gq = _dot(dam, kt)
                gk = _dot_tn(dam, qt)
                dq = dq + e * gq
                dk = dk + e * gk
                db = db + (qt.astype(F32) * gq - kt.astype(F32) * gk)
            dc = _rowsum(da * diag_mask)
            dq = dq + dc * k
            dk = dk + dc * q
            dv_ref[:, sl] = (_dot_tn(a.astype(BF16), dos[hh].astype(BF16)) + dv_state).astype(BF16)
            dg = _tri_dot(tri_ref[1], db) + carry[0:1, sl]
            carry[0:1, sl] += jnp.sum(db, axis=0, keepdims=True)
            lb = lb_all[:, sl]
            qr = qr_ref[:, sl]
            t = dg / f - dk
            dfl_ref[:, sl] = (t * (1.0 - lb) * sig * (1.0 - sig)).astype(BF16)
            acc_lb[0:1, sl] += jnp.sum(t * (1.0 - sig), axis=0, keepdims=True)
            dqr_ref[:, sl] = (dq * (sq * (1.0 + qr * (1.0 - sq)))).astype(BF16)

        @pl.when(c == nc - 1)
        def _():
            gl0 = acc_lb[0:1, :] * dlb_dl0
            glbl_ref[0:1, :] = gl0
            glbl_ref[1:2, :] = -gl0

    blk, vec, lb2, st_spec, consts = _hg_specs(nc, True)
    tris, masks = _hg_const_arrays()
    act = jax.ShapeDtypeStruct((S, HG_W), BF16)
    return pl.pallas_call(
        body, name="hg_bwd", grid=(ng, nc),
        in_specs=[blk(0), blk(ng), blk(0), blk(0), st_spec, lb2] + consts + consts[1:],
        out_specs=[blk(0), blk(0), blk(0), lb2],
        out_shape=[act, act, act, jax.ShapeDtypeStruct((2, HG_W), F32)],
        scratch_shapes=[pltpu.VMEM((HG_HPS, 128, 128), F32), pltpu.VMEM((8, w), F32), pltpu.VMEM((8, w), F32)],
        compiler_params=_params(("parallel", "arbitrary")),
    )(p0, p0, p1, do_hg, states, lbl, tris, masks, masks.astype(BF16))


def _stat_head(lane):
    return lane >> 4


def _stat_lane(lane):
    return lane & (STAT_LANES - 1)


def _o_head(lane):
    return lane >> 6


def _att_bias():
    qi = np.arange(ATT_BLK)[:, None]
    kj = np.arange(2 * ATT_BLK)[None, :]
    band = (kj >= qi) & (kj <= qi + ATT_BLK)
    qm = np.stack([band & (kj >= ATT_BLK), band])
    cur = (kj < ATT_BLK) & (qi <= kj)
    km = np.stack([cur, cur | ((kj >= ATT_BLK) & (qi >= kj - ATT_BLK))])
    to_bias = lambda m: jnp.asarray(np.where(m, 0.0, NEG), F32)
    return to_bias(qm), to_bias(km)


def _att_fwd(qkv_d):
    d, L, _ = qkv_d.shape
    qb = ATT_QB
    rows = qb * ATT_BLK

    def body(q_ref, kp_ref, kc_ref, vp_ref, vc_ref, bias_ref, o_ref, lse_ref, s_scr, p_scr):
        first = _iota((ATT_BLK, HEAD_PAIR), 1) < 64
        head_of_lane = _stat_head(_iota((ATT_BLK, STAT_W), 1))
        pairs = [slice(HEAD_PAIR * hp, HEAD_PAIR * (hp + 1)) for hp in range(4)]
        blk = lambda i: slice(ATT_BLK * i, ATT_BLK * (i + 1))

        def keys(i, sl, prev_ref, cur_ref):
            before = prev_ref[0, :, sl] if i == 0 else cur_ref[0, blk(i - 1), sl]
            return jnp.concatenate([before, cur_ref[0, blk(i), sl]], axis=0)

        for i in range(qb):
            for hp, sl in enumerate(pairs):
                q2 = q_ref[0, blk(i), sl] * ATT_SCALE
                zero = jnp.zeros_like(q2)
                qs = jnp.concatenate([jnp.where(first, q2, zero), jnp.where(first, zero, q2)], axis=0)
                s_scr[4 * i + hp] = _dot_nt(qs, keys(i, sl, kp_ref, kc_ref))
        stats = []
        for i in range(qb):
            bias = bias_ref[jnp.minimum(pl.program_id(1), 1)] if i == 0 else bias_ref[1]
            bias2 = jnp.concatenate([bias, bias], axis=0)
            for hp in range(4):
                s = s_scr[4 * i + hp] + bias2
                m = jnp.max(s, axis=-1, keepdims=True)
                p = jnp.exp(s - m)
                l = jnp.sum(p, axis=-1, keepdims=True)
                p_scr[4 * i + hp] = p.astype(BF16)
                stats.append((l, m + jnp.log(l)))
        for i in range(qb):
            lse_blk = jnp.zeros((ATT_BLK, STAT_W), F32)
            for hp, sl in enumerate(pairs):
                l, lse = stats[4 * i + hp]
                o = _dot(p_scr[4 * i + hp], keys(i, sl, vp_ref, vc_ref)) / l
                o_ref[0, blk(i), sl] = jnp.where(first, o[:ATT_BLK], o[ATT_BLK:]).astype(BF16)
                lse_blk = jnp.where(head_of_lane == 2 * hp, lse[:ATT_BLK],
                                    jnp.where(head_of_lane == 2 * hp + 1, lse[ATT_BLK:], lse_blk))
            lse_ref[0, blk(i), :] = lse_blk

    cur = lambda j: pl.BlockSpec((1, rows, AT_W), lambda r, n: (r, n, j))
    prev = lambda j: pl.BlockSpec((1, ATT_BLK, AT_W), lambda r, n: (r, jnp.maximum(qb * n - 1, 0), j))
    return pl.pallas_call(
        body, name=f"att_fwd_d{d}", grid=(d, L // rows),
        in_specs=[cur(0), prev(1), cur(1), prev(2), cur(2), _full((2, ATT_BLK, 2 * ATT_BLK))],
        out_specs=[pl.BlockSpec((1, rows, AT_W), lambda r, n: (r, n, 0)), pl.BlockSpec((1, rows, STAT_W), lambda r, n: (r, n, 0))],
        out_shape=[jax.ShapeDtypeStruct((d, L, AT_W), BF16), jax.ShapeDtypeStruct((d, L, STAT_W), F32)],
        scratch_shapes=[pltpu.VMEM((4 * qb, 2 * ATT_BLK, 2 * ATT_BLK), F32), pltpu.VMEM((4 * qb, 2 * ATT_BLK, 2 * ATT_BLK), BF16)],
        compiler_params=_params(("parallel", "parallel")),
    )(qkv_d, qkv_d, qkv_d, qkv_d, qkv_d, _att_bias()[0])


def _att_combine(os_d, ls_d, p3):
    S = p3.shape[0]
    tm = MIX_TM

    def body(oa_ref, ob4_ref, oc16_ref, la_ref, lb4_ref, lc16_ref, z_ref,
             oat_ref, lse_ref, mix_ref, ob_scr, oc_scr, lb_scr, lc_scr):
        _merge_residues(ob4_ref, ob_scr, 4)
        _merge_residues(oc16_ref, oc_scr, 16)
        _merge_residues(lb4_ref, lb_scr, 4)
        _merge_residues(lc16_ref, lc_scr, 16)
        ls = (la_ref[...], lb_scr[0], lc_scr[0])
        mx = jnp.maximum(jnp.maximum(ls[0], ls[1]), ls[2])
        es = [jnp.exp(l - mx) for l in ls]
        zs = es[0] + es[1] + es[2]
        lse_ref[...] = mx + jnp.log(zs)
        spread = ((_o_head(_iota((STAT_W, AT_W), 1)) == _stat_head(_iota((STAT_W, AT_W), 0)))
                  & (_stat_lane(_iota((STAT_W, AT_W), 0)) == 0)).astype(BF16)
        os_ = (oa_ref[...].astype(F32), _from_slabs(ob_scr), _from_slabs(oc_scr))
        o = jnp.zeros((tm, AT_W), F32)
        for e, oi in zip(es, os_):
            hi, lo = _split2(e / zs)
            o = o + (_dot(hi, spread) + _dot(lo, spread)) * oi
        oat_ref[...] = o
        z = z_ref[...]
        mixed = o * (z * _sigmoid(z))
        mix_ref[...] = mixed.astype(BF16)

    row = lambda w: pl.BlockSpec((tm, w), lambda s: (s, 0))
    return pl.pallas_call(
        body, name="att_combine", grid=(S // tm,),
        in_specs=[row(AT_W), _dil_spec(4, tm, AT_W), _dil_spec(16, tm, AT_W),
                  row(STAT_W), _dil_spec(4, tm, STAT_W), _dil_spec(16, tm, STAT_W),
                  pl.BlockSpec((tm, AT_W), lambda s: (s, 1))],
        out_specs=[row(AT_W), row(STAT_W), row(AT_W)],
        out_shape=[jax.ShapeDtypeStruct((S, AT_W), F32), jax.ShapeDtypeStruct((S, STAT_W), F32),
                   jax.ShapeDtypeStruct((S, AT_W), BF16)],
        scratch_shapes=[_slab_scratch(tm, AT_W), _slab_scratch(tm, AT_W), _slab_scratch(tm, STAT_W), _slab_scratch(tm, STAT_W)],
        compiler_params=_params(("parallel",)),
    )(os_d[0].reshape(S, AT_W), os_d[1], os_d[2], ls_d[0].reshape(S, STAT_W), ls_d[1], ls_d[2], p3)


def _att_gate_bwd(dm_at, o_at, lse, p3):
    S = p3.shape[0]
    tm = MIX_TM

    def body(dm_ref, o_ref, l_ref, z_ref, do_ref, do4_ref, do16_ref, dl_ref, dl4_ref, dl16_ref, dz_ref, do_scr, dl_scr):
        o = o_ref[...]
        z = z_ref[...]
        dm = dm_ref[...]
        sz = _sigmoid(z)
        dz_ref[...] = (dm * o * (sz * (1.0 + z * (1.0 - sz)))).astype(BF16)
        do = dm * (z * sz)
        do_ref[...] = do.astype(BF16)
        gather = (_o_head(_iota((AT_W, STAT_W), 0)) == _stat_head(_iota((AT_W, STAT_W), 1))).astype(BF16)
        dl = jnp.where(_stat_lane(_iota((tm, STAT_W), 1)) < STAT_LSE_LANE, _dot_sel(do * o, gather), l_ref[...])
        dl_ref[...] = dl
        _to_slabs(do, do_scr)
        _to_slabs(dl, dl_scr)
        _split_residues(do_scr, do4_ref, 4, BF16)
        _split_residues(do_scr, do16_ref, 16, BF16)
        _split_residues(dl_scr, dl4_ref, 4, F32)
        _split_residues(dl_scr, dl16_ref, 16, F32)

    row = lambda w: pl.BlockSpec((tm, w), lambda s: (s, 0))
    sds = jax.ShapeDtypeStruct
    return pl.pallas_call(
        body, name="att_gate_bwd", grid=(S // tm,),
        in_specs=[row(AT_W), row(AT_W), row(STAT_W), pl.BlockSpec((tm, AT_W), lambda s: (s, 1))],
        out_specs=[row(AT_W), _dil_spec(4, tm, AT_W), _dil_spec(16, tm, AT_W),
                   row(STAT_W), _dil_spec(4, tm, STAT_W), _dil_spec(16, tm, STAT_W), row(AT_W)],
        out_shape=[sds((S, AT_W), BF16), sds((4, S // 4, AT_W), BF16), sds((16, S // 16, AT_W), BF16),
                   sds((S, STAT_W), F32), sds((4, S // 4, STAT_W), F32), sds((16, S // 16, STAT_W), F32),
                   sds((S, AT_W), BF16)],
        scratch_shapes=[_slab_scratch(tm, AT_W), _slab_scratch(tm, STAT_W)],
        compiler_params=_params(("parallel",)),
    )(dm_at, o_at, lse, p3)


def _att_bwd(qkv_d, do_d, dl_d):
    d, L, _ = qkv_d.shape
    nb = L // ATT_BLK
    qb = ATT_QB
    rows = qb * ATT_BLK
    nsteps = L // rows

    def body(qc_ref, qn_ref, kp_ref, kc_ref, vp_ref, vc_ref, ac_ref, an_ref, lc_ref, ln_ref, bq_ref, bk_ref, dqkv_ref,
             s_scr, dp_scr, st_scr, dpt_scr, ds_scr, pt_scr, dst_scr):
        n = pl.program_id(1)
        first = _iota((ATT_BLK, HEAD_PAIR), 1) < 64
        pairs = [slice(HEAD_PAIR * hp, HEAD_PAIR * (hp + 1)) for hp in range(4)]
        blk = lambda i: slice(ATT_BLK * i, ATT_BLK * (i + 1))

        def stack(t):
            zero = jnp.zeros_like(t)
            return jnp.concatenate([jnp.where(first, t, zero), jnp.where(first, zero, t)], axis=0)

        def unstack(t2):
            return jnp.where(first, t2[:ATT_BLK], t2[ATT_BLK:])

        def with_prev(i, sl, prev_ref, cur_ref):
            before = prev_ref[0, :, sl] if i == 0 else cur_ref[0, blk(i - 1), sl]
            return jnp.concatenate([before, cur_ref[0, blk(i), sl]], axis=0)

        def with_next(i, sl, cur_ref, next_ref):
            after = next_ref[0, :, sl] if i == qb - 1 else cur_ref[0, blk(i + 1), sl]
            return jnp.concatenate([cur_ref[0, blk(i), sl], after], axis=0)

        for i in range(qb):
            for hp, sl in enumerate(pairs):
                j = 4 * i + hp
                s_scr[j] = _dot_nt(stack(qc_ref[0, blk(i), sl] * ATT_SCALE), with_prev(i, sl, kp_ref, kc_ref))
                dp_scr[j] = _dot_nt(stack(ac_ref[0, blk(i), sl]), with_prev(i, sl, vp_ref, vc_ref))
                st_scr[j] = _dot_nt(stack(kc_ref[0, blk(i), sl] * ATT_SCALE), with_next(i, sl, qc_ref, qn_ref))
                dpt_scr[j] = _dot_nt(stack(vc_ref[0, blk(i), sl]), with_next(i, sl, ac_ref, an_ref))
        for i in range(qb):
            bias = bq_ref[jnp.minimum(n, 1)] if i == 0 else bq_ref[1]
            bias_t = bk_ref[jnp.minimum(nsteps - 1 - n, 1)] if i == qb - 1 else bk_ref[1]
            bias2 = jnp.concatenate([bias, bias], axis=0)
            bias_t2 = jnp.concatenate([bias_t, bias_t], axis=0)
            dl_c = lc_ref[0, blk(i), :]
            dl_t = with_next(i, slice(None), lc_ref, ln_ref).T
            for hp in range(4):
                j = 4 * i + hp
                at = [STAT_LANES * (2 * hp), STAT_LANES * (2 * hp + 1)]
                col = lambda t, o: jnp.concatenate([t[:, a + o: a + o + 1] for a in at], axis=0)
                p = jnp.exp(s_scr[j] + bias2 - col(dl_c, STAT_LSE_LANE))
                ds_scr[j] = (p * (dp_scr[j] - col(dl_c, 0))).astype(BF16)
                row = lambda t, o: jnp.concatenate([jnp.broadcast_to(t[a + o: a + o + 1, :], (ATT_BLK, 2 * ATT_BLK)) for a in at], axis=0)
                pt = jnp.exp(st_scr[j] + bias_t2 - row(dl_t, STAT_LSE_LANE))
                pt_scr[j] = pt.astype(BF16)
                dst_scr[j] = (pt * (dpt_scr[j] - row(dl_t, 0))).astype(BF16)
        for i in range(qb):
            for hp, sl in enumerate(pairs):
                j = 4 * i + hp
                dq = unstack(_dot(ds_scr[j], with_prev(i, sl, kp_ref, kc_ref))) * ATT_SCALE
                dk = unstack(_dot(dst_scr[j], with_next(i, sl, qc_ref, qn_ref))) * ATT_SCALE
                dv = unstack(_dot(pt_scr[j], with_next(i, sl, ac_ref, an_ref)))
                dqkv_ref[0, blk(i), sl] = dq.astype(BF16)
                dqkv_ref[0, blk(i), AT_W + HEAD_PAIR * hp: AT_W + HEAD_PAIR * (hp + 1)] = dk.astype(BF16)
                dqkv_ref[0, blk(i), 2 * AT_W + HEAD_PAIR * hp: 2 * AT_W + HEAD_PAIR * (hp + 1)] = dv.astype(BF16)

    cur = lambda j: pl.BlockSpec((1, rows, AT_W), lambda r, n: (r, n, j))
    prev = lambda j: pl.BlockSpec((1, ATT_BLK, AT_W), lambda r, n: (r, jnp.maximum(qb * n - 1, 0), j))
    nxt_blk = lambda n: jnp.minimum(qb * (n + 1), nb - 1)
    sq = (4 * qb, 2 * ATT_BLK, 2 * ATT_BLK)
    return pl.pallas_call(
        body, name=f"att_bwd_d{d}", grid=(d, nsteps),
        in_specs=[cur(0), pl.BlockSpec((1, ATT_BLK, AT_W), lambda r, n: (r, nxt_blk(n), 0)), prev(1), cur(1), prev(2), cur(2),
                  pl.BlockSpec((1, rows, AT_W), lambda r, n: (r, n, 0)),
                  pl.BlockSpec((1, ATT_BLK, AT_W), lambda r, n: (r, nxt_blk(n), 0)),
                  pl.BlockSpec((1, rows, STAT_W), lambda r, n: (r, n, 0)),
                  pl.BlockSpec((1, ATT_BLK, STAT_W), lambda r, n: (r, nxt_blk(n), 0)),
                  _full((2, ATT_BLK, 2 * ATT_BLK)), _full((2, ATT_BLK, 2 * ATT_BLK))],
        out_specs=pl.BlockSpec((1, rows, 3 * AT_W), lambda r, n: (r, n, 0)),
        out_shape=jax.ShapeDtypeStruct((d, L, 3 * AT_W), BF16),
        scratch_shapes=[pltpu.VMEM(sq, F32)] * 4 + [pltpu.VMEM(sq, BF16)] * 3,
        compiler_params=_params(("parallel", "parallel")),
    )(qkv_d, qkv_d, qkv_d, qkv_d, qkv_d, qkv_d, do_d, do_d, dl_d, dl_d, *_att_bias())


def _att_bwd_combine(dqkvs, cos2, sin2):
    S = dqkvs[0].shape[1]
    tm = MIX_TM

    def body(a_ref, b4_ref, c16_ref, cos_ref, sin_ref, dq_ref, dk_ref, dv_ref, b_scr, c_scr):
        _merge_residues(b4_ref, b_scr, 4)
        _merge_residues(c16_ref, c_scr, 16)
        t = a_ref[...].astype(F32) + _from_slabs(b_scr) + _from_slabs(c_scr)
        dy = t[:, : 2 * AT_W]
        cosf = jnp.tile(cos_ref[...], (1, 8))
        sinf = jnp.tile(sin_ref[...], (1, 8))
        dx = dy * cosf - _rope_rot(dy) * sinf
        dq_ref[...] = dx[:, :AT_W].astype(BF16)
        dk_ref[...] = dx[:, AT_W:].astype(BF16)
        dv_ref[...] = t[:, 2 * AT_W:].astype(BF16)

    row = lambda w: pl.BlockSpec((tm, w), lambda s: (s, 0))
    act = jax.ShapeDtypeStruct((S, AT_W), BF16)
    return pl.pallas_call(
        body, name="att_bwd_combine", grid=(S // tm,),
        in_specs=[row(3 * AT_W), _dil_spec(4, tm, 3 * AT_W), _dil_spec(16, tm, 3 * AT_W), row(128), row(128)],
        out_specs=[row(AT_W), row(AT_W), row(AT_W)],
        out_shape=[act, act, act],
        scratch_shapes=[_slab_scratch(tm, 3 * AT_W), _slab_scratch(tm, 3 * AT_W)],
        compiler_params=_params(("parallel",)),
    )(dqkvs[0].reshape(S, 3 * AT_W), dqkvs[1], dqkvs[2], cos2, sin2)


def _outproj(x2, tgt2, mix_at, w_out_full, fnw, o_hg, p1, hg_norm_w):
    S = x2.shape[0]
    tm = PROJ_TM
    ns = S // tm

    def body(x_ref, t_ref, ma_ref, w_ref, fw_ref, o_ref, z_ref, hgw_ref,
             dh_ref, doh_ref, dzh_ref, dma_ref, gw_ref, gfw_ref, ghgw_ref, loss_ref):
        s = pl.program_id(0)

        @pl.when(s == 0)
        def _():
            gw_ref[...] = jnp.zeros_like(gw_ref)
            gfw_ref[...] = jnp.zeros_like(gfw_ref)
            ghgw_ref[...] = jnp.zeros_like(ghgw_ref)
            loss_ref[...] = jnp.zeros_like(loss_ref)

        heads = [slice(128 * hh, 128 * (hh + 1)) for hh in range(HG_W // 128)]
        norm = []
        for sl in heads:
            o, z = o_ref[:, sl], z_ref[:, sl]
            rs = lax.rsqrt(jnp.mean(o * o, axis=-1, keepdims=True) + EPS)
            norm.append((rs, o * rs, _sigmoid(z)))
        mh = jnp.concatenate([(oh * hgw_ref[:, sl] * (z_ref[:, sl] * sz)).astype(BF16)
                              for sl, (_, oh, sz) in zip(heads, norm)], axis=1)
        y = _dot(mh, w_ref[:HG_W, :]) + _dot(ma_ref[...], w_ref[HG_W:, :])
        h = x_ref[...] + y
        r = lax.rsqrt(jnp.mean(h * h, axis=-1, keepdims=True) + EPS)
        hn = h * r
        fw = fw_ref[...]
        err = hn * fw - t_ref[...]
        loss_ref[...] += 0.5 * jnp.sum(jnp.mean(err * err, axis=-1, keepdims=True))
        dout = err * (1.0 / D_MODEL)
        gfw_ref[...] += jnp.sum(dout * hn, axis=0, keepdims=True)
        dhn = dout * fw
        dh = r * (dhn - hn * jnp.mean(dhn * hn, axis=-1, keepdims=True))
        dh_ref[...] = dh
        dhb = dh.astype(BF16)
        dma_ref[...] = _dot_nt(dhb, w_ref[HG_W:, :])
        dmh = _dot_nt(dhb, w_ref[:HG_W, :])
        for sl, (rs, oh, sz) in zip(heads, norm):
            z, dmix, gwv = z_ref[:, sl], dmh[:, sl], hgw_ref[:, sl]
            dzh_ref[:, sl] = (dmix * (oh * gwv) * (sz * (1.0 + z * (1.0 - sz)))).astype(BF16)
            don = dmix * (z * sz)
            ghgw_ref[:, sl] += jnp.sum(don * oh, axis=0, keepdims=True)
            dy = don * gwv
            doh_ref[:, sl] = rs * (dy - oh * jnp.mean(dy * oh, axis=-1, keepdims=True))
        gw_ref[:HG_W, :] += _dot_tn(mh, dhb)
        gw_ref[HG_W:, :] += _dot_tn(ma_ref[...], dhb)

    row = lambda w: pl.BlockSpec((tm, w), lambda s: (s, 0))
    return pl.pallas_call(
        body, name="outproj", grid=(ns,),
        in_specs=[row(D_MODEL), row(D_MODEL), row(AT_W),
                  _full((D_MODEL, D_MODEL)), _full((1, D_MODEL)),
                  row(HG_W), pl.BlockSpec((tm, HG_W), lambda s: (s, 1)), _full((1, HG_W))],
        out_specs=[row(D_MODEL), row(HG_W), row(HG_W), row(AT_W), _full((D_MODEL, D_MODEL)), _full((1, D_MODEL)),
                   _full((1, HG_W)), _full((8, 128))],
        out_shape=[jax.ShapeDtypeStruct((S, D_MODEL), F32), jax.ShapeDtypeStruct((S, HG_W), F32),
                   jax.ShapeDtypeStruct((S, HG_W), BF16), jax.ShapeDtypeStruct((S, AT_W), F32),
                   jax.ShapeDtypeStruct((D_MODEL, D_MODEL), F32), jax.ShapeDtypeStruct((1, D_MODEL), F32),
                   jax.ShapeDtypeStruct((1, HG_W), F32), jax.ShapeDtypeStruct((8, 128), F32)],
        compiler_params=_params(("arbitrary",)),
    )(x2, tgt2, mix_at, w_out_full, fnw, o_hg, p1, hg_norm_w)


def _inproj_bwd_x(dps, w_in_full, x2, norm_w, dh):
    S = x2.shape[0]
    tm = PROJ_TM

    def body(d0, d1, d2, d3, d4, d5, d6, d7, w_ref, x_ref, nw_ref, dh_ref, gx_ref, gnw_ref):
        s = pl.program_id(0)

        @pl.when(s == 0)
        def _():
            gnw_ref[...] = jnp.zeros_like(gnw_ref)

        du = jnp.zeros((tm, D_MODEL), F32)
        for i, dref in enumerate((d0, d1, d2, d3, d4, d5, d6, d7)):
            j, half = divmod(i, 2)
            du = du + _dot_nt(dref[...], w_ref[j, :, 512 * half: 512 * (half + 1)])
        x = x_ref[...]
        r = lax.rsqrt(jnp.mean(x * x, axis=-1, keepdims=True) + EPS)
        xh = x * r
        gnw_ref[...] += jnp.sum(du * xh, axis=0, keepdims=True)
        dun = du * nw_ref[...]
        gx_ref[...] = dh_ref[...] + r * (dun - xh * jnp.mean(dun * xh, axis=-1, keepdims=True))

    row = lambda w: pl.BlockSpec((tm, w), lambda s: (s, 0))
    return pl.pallas_call(
        body, name="inproj_bwd_x", grid=(S // tm,),
        in_specs=[row(512)] * 8 + [_full((4, D_MODEL, 1024)), row(D_MODEL), _full((1, D_MODEL)), row(D_MODEL)],
        out_specs=[row(D_MODEL), _full((1, D_MODEL))],
        out_shape=[jax.ShapeDtypeStruct((S, D_MODEL), F32), jax.ShapeDtypeStruct((1, D_MODEL), F32)],
        compiler_params=_params(("arbitrary",)),
    )(*dps, w_in_full, x2, norm_w, dh)


def _inproj_bwd_w(u, dps):
    S = u.shape[0]
    tm = PROJ_TM

    def body(u_ref, d0, d1, d2, d3, d4, d5, d6, d7, g_ref):
        @pl.when(pl.program_id(0) == 0)
        def _():
            g_ref[...] = jnp.zeros_like(g_ref)

        ub = u_ref[...]
        for i, dref in enumerate((d0, d1, d2, d3, d4, d5, d6, d7)):
            j, half = divmod(i, 2)
            g_ref[j, :, 512 * half: 512 * (half + 1)] += _dot_tn(ub, dref[...])

    return pl.pallas_call(
        body, name="inproj_bwd_w", grid=(S // tm,),
        in_specs=[pl.BlockSpec((tm, D_MODEL), lambda s: (s, 0))] + [pl.BlockSpec((tm, 512), lambda s: (s, 0))] * 8,
        out_specs=_full((4, D_MODEL, 1024)),
        out_shape=jax.ShapeDtypeStruct((4, D_MODEL, 1024), F32),
        compiler_params=_params(("arbitrary",)),
    )(u, *dps)


def _adamw_update(gg, w_ref, m_ref, v_ref, d_ref, nm_ref, nv_ref):
    nm = ADAM_B1 * m_ref[...] + (1.0 - ADAM_B1) * gg
    nv = ADAM_B2 * v_ref[...] + (1.0 - ADAM_B2) * (gg * gg)
    m_hat = nm / (1.0 - ADAM_B1 ** ADAM_STEP)
    v_hat = nv / (1.0 - ADAM_B2 ** ADAM_STEP)
    d_ref[...] = -ADAM_LR * (m_hat / (jnp.sqrt(v_hat) + ADAM_EPS) + ADAM_WD * w_ref[...])
    nm_ref[...] = nm
    nv_ref[...] = nv


def _adamw(w, g, m, v, name):
    rows, cols = w.shape
    tr = min(rows, 256)

    def body(w_ref, g_ref, m_ref, v_ref, d_ref, nm_ref, nv_ref):
        _adamw_update(g_ref[...], w_ref, m_ref, v_ref, d_ref, nm_ref, nv_ref)

    spec = pl.BlockSpec((tr, cols), lambda i: (i, 0))
    sds = jax.ShapeDtypeStruct((rows, cols), F32)
    return pl.pallas_call(
        body, name=name, grid=(rows // tr,),
        in_specs=[spec] * 4, out_specs=[spec] * 3, out_shape=[sds] * 3,
        compiler_params=_params(("parallel",)),
    )(w, g, m, v)


def _place():
    return lax.axis_index("x"), lax.axis_index("y"), lax.axis_index("c")


def _cast_shards(w_in_s, w_out_s):
    def body(win_ref, wout_ref, bin_ref, bout_ref):
        bin_ref[...] = win_ref[...].astype(BF16)
        bout_ref[...] = wout_ref[...].astype(BF16)

    vm = pl.BlockSpec(memory_space=pltpu.VMEM)
    return pl.pallas_call(
        body, name="cast_shards", in_specs=[vm, vm], out_specs=[vm, vm],
        out_shape=[jax.ShapeDtypeStruct(w_in_s.shape, BF16), jax.ShapeDtypeStruct(w_out_s.shape, BF16)],
        compiler_params=pltpu.CompilerParams(vmem_limit_bytes=VMEM_LIMIT),
    )(w_in_s, w_out_s)


def _spread_weights(win_b, wout_b, lin, lout):
    def body(win_ref, wout_ref, lin_ref, lout_ref, fin_ref, fout_ref, send_sems, recv_sems):
        x, y, c = _place()
        sib = (x, y, 1 - c)
        mine = 2 * x + y
        chips = [2 * (1 - x) + y, 2 * x + (1 - y), 2 * (1 - x) + (1 - y)]
        fin_ref[mine] = win_ref[...]
        fout_ref[mine] = wout_ref[...]

        def halves(chip, half):
            return (fin_ref.at[chip, pl.ds(half * 512, 512), :], fout_ref.at[chip, pl.ds(half * 128, 128), :])

        passed = []
        for k, chip in enumerate(chips):
            for a, (dst, src) in enumerate(zip(halves(chip, c), (lin_ref, lout_ref))):
                dst[...] = src[k]
                fwd = pltpu.make_async_remote_copy(src_ref=dst, dst_ref=dst, send_sem=send_sems.at[2 * k + a],
                                                   recv_sem=recv_sems.at[2 * k + a], device_id=sib, device_id_type=MESH)
                fwd.start()
                passed.append(fwd)
        for k, chip in enumerate(chips):
            for a, ref in enumerate(halves(chip, 1 - c)):
                pltpu.make_async_remote_copy(src_ref=ref, dst_ref=ref, send_sem=send_sems.at[2 * k + a],
                                             recv_sem=recv_sems.at[2 * k + a], device_id=sib, device_id_type=MESH).wait_recv()
        for cp in passed:
            cp.wait_send()

    vm = pl.BlockSpec(memory_space=pltpu.VMEM)
    return pl.pallas_call(
        body, name="spread_weights",
        in_specs=[vm] * 4, out_specs=[vm, vm],
        out_shape=[jax.ShapeDtypeStruct((4, D_MODEL, 1024), BF16), jax.ShapeDtypeStruct((4, 256, D_MODEL), BF16)],
        scratch_shapes=[pltpu.SemaphoreType.DMA((6,)), pltpu.SemaphoreType.DMA((6,))],
        compiler_params=pltpu.CompilerParams(vmem_limit_bytes=VMEM_LIMIT),
    )(win_b, wout_b, lin, lout)


def _swap_halves(g_in, g_out):
    def body(gin_ref, gout_ref, rin_ref, rout_ref, send_sems, recv_sems):
        x, y, c = _place()
        sib = (x, y, 1 - c)
        cps = [pltpu.make_async_remote_copy(src_ref=src.at[:, 1 - c], dst_ref=dst, send_sem=send_sems.at[k],
                                            recv_sem=recv_sems.at[k], device_id=sib, device_id_type=MESH)
               for k, (src, dst) in enumerate(((gin_ref, rin_ref), (gout_ref, rout_ref)))]
        for cp in cps:
            cp.start()
        for cp in cps:
            cp.wait()

    hbm = pl.BlockSpec(memory_space=pl.ANY)
    return pl.pallas_call(
        body, name="swap_halves",
        in_specs=[hbm, hbm], out_specs=[hbm, hbm],
        out_shape=[jax.ShapeDtypeStruct((4,) + g.shape[2:], F32) for g in (g_in, g_out)],
        scratch_shapes=[pltpu.SemaphoreType.DMA((2,)), pltpu.SemaphoreType.DMA((2,))],
    )(g_in, g_out)


def _add_half(g, r, cidx, name):
    n, _, rows, cols = g.shape

    def body(c_ref, g_ref, r_ref, o_ref):
        o_ref[0] = (g_ref[0, 0] + r_ref[0]).astype(BF16)

    return pl.pallas_call(
        body, name=name,
        grid_spec=pltpu.PrefetchScalarGridSpec(
            num_scalar_prefetch=1, grid=(n,),
            in_specs=[pl.BlockSpec((1, 1, rows, cols), lambda j, c_ref: (j, c_ref[0], 0, 0)),
                      pl.BlockSpec((1, rows, cols), lambda j, c_ref: (j, 0, 0))],
            out_specs=pl.BlockSpec((1, rows, cols), lambda j, c_ref: (j, 0, 0))),
        out_shape=jax.ShapeDtypeStruct((n, rows, cols), BF16),
        compiler_params=_params(("parallel",)),
    )(cidx, g, r)


def _exchange_copies(in_ref, out_ref, lin_ref, lout_ref, send_sems, recv_sems):
    x, y, c = _place()
    cps = []
    for k, (cx, cy) in enumerate([(1 - x, y), (x, 1 - y), (1 - x, 1 - y)]):
        for a, (src, dst) in enumerate(((in_ref, lin_ref), (out_ref, lout_ref))):
            cps.append(pltpu.make_async_remote_copy(
                src_ref=src.at[2 * cx + cy], dst_ref=dst.at[k], send_sem=send_sems.at[2 * k + a],
                recv_sem=recv_sems.at[2 * k + a], device_id=(cx, cy, c), device_id_type=MESH))
    return cps


def _gather_copies(win_ref, wout_ref, lin_ref, lout_ref, send_sems, recv_sems):
    x, y, c = _place()
    cps = []
    for k, (cx, cy) in enumerate([(1 - x, y), (x, 1 - y), (1 - x, 1 - y)]):
        for a, (src, dst) in enumerate(((win_ref, lin_ref), (wout_ref, lout_ref))):
            half = src.shape[0] // 2
            cps.append(pltpu.make_async_remote_copy(
                src_ref=src.at[pl.ds(c * half, half), :], dst_ref=dst.at[k], send_sem=send_sems.at[2 * k + a],
                recv_sem=recv_sems.at[2 * k + a], device_id=(cx, cy, c), device_id_type=MESH))
    return cps


def _start_copies(name, copies, srcs, land_shapes):
    def body(a_ref, b_ref, la_ref, lb_ref, send_sems, recv_sems, a_thru, b_thru, la_thru, lb_thru, token):
        for cp in copies(a_ref, b_ref, la_ref, lb_ref, send_sems, recv_sems):
            cp.start()
        token[...] = jnp.zeros_like(token)

    lands = [lax.empty(shape, a.dtype) for shape, a in zip(land_shapes, srcs)]
    bufs = [pltpu.with_memory_space_constraint(a, pltpu.HBM) for a in (*srcs, *lands)]
    hbm = pl.BlockSpec(memory_space=pltpu.HBM)
    sem = pl.BlockSpec(memory_space=pltpu.SEMAPHORE)
    return pl.pallas_call(
        body, name=name,
        in_specs=[hbm] * 4,
        out_specs=[sem, sem, hbm, hbm, hbm, hbm, pl.BlockSpec(memory_space=pltpu.VMEM)],
        out_shape=[pltpu.SemaphoreType.DMA((6,)), pltpu.SemaphoreType.DMA((6,))]
                  + [pltpu.HBM(b.shape, b.dtype) for b in bufs] + [jax.ShapeDtypeStruct((8, 128), F32)],
        input_output_aliases={0: 2, 1: 3, 2: 4, 3: 5},
        compiler_params=pltpu.CompilerParams(has_side_effects=pltpu.SideEffectType.DATAFLOW_SIDE_EFFECTING),
    )(*bufs)


def _wait_copies(name, copies, send_sems, recv_sems, a_thru, b_thru, la_thru, lb_thru, after):
    def body(a_ref, b_ref, la_ref, lb_ref, send_sems, recv_sems, after_ref, a_dead, b_dead, got_a, got_b):
        for cp in copies(a_ref, b_ref, la_ref, lb_ref, send_sems, recv_sems):
            cp.wait_send()
            cp.wait_recv()

    hbm = pl.BlockSpec(memory_space=pltpu.HBM)
    sem = pl.BlockSpec(memory_space=pltpu.SEMAPHORE)
    bufs = (a_thru, b_thru, la_thru, lb_thru)
    return pl.pallas_call(
        body, name=name,
        in_specs=[hbm] * 4 + [sem, sem, pl.BlockSpec(memory_space=pl.ANY)],
        out_specs=[hbm] * 4,
        out_shape=[pltpu.HBM(b.shape, b.dtype) for b in bufs],
        input_output_aliases={0: 0, 1: 1, 2: 2, 3: 3},
        compiler_params=pltpu.CompilerParams(has_side_effects=pltpu.SideEffectType.DATAFLOW_SIDE_EFFECTING),
    )(*bufs, send_sems, recv_sems, after)


PEER_XOR = (2, 1, 3)


def _sum_chips(cs, r, chip_idx, name):
    _, rows, cols = r.shape
    tr = min(rows, 256)

    def body(m_ref, cs_ref, r_ref, o_ref):
        mine = m_ref[0]
        own = cs_ref[0].astype(F32)
        got = [r_ref[k].astype(F32) for k in range(3)]
        acc = None
        for s in range(4):
            rel = mine ^ s
            term = jnp.where(rel == 0, own, jnp.where(rel == PEER_XOR[0], got[0],
                                                      jnp.where(rel == PEER_XOR[1], got[1], got[2])))
            acc = term if acc is None else acc + term
        o_ref[...] = acc

    return pl.pallas_call(
        body, name=name,
        grid_spec=pltpu.PrefetchScalarGridSpec(
            num_scalar_prefetch=1, grid=(rows // tr,),
            in_specs=[pl.BlockSpec((1, tr, cols), lambda i, m_ref: (m_ref[0], i, 0)),
                      pl.BlockSpec((3, tr, cols), lambda i, m_ref: (0, i, 0))],
            out_specs=pl.BlockSpec((tr, cols), lambda i, m_ref: (i, 0))),
        out_shape=jax.ShapeDtypeStruct((rows, cols), F32),
        compiler_params=_params(("parallel",)),
    )(chip_idx, cs, r)


def _swap_reduced(h_in, h_out):
    def body(in_ref, out_ref, rin_ref, rout_ref, send_sems, recv_sems):
        x, y, c = _place()
        cps = [pltpu.make_async_remote_copy(src_ref=src, dst_ref=dst, send_sem=send_sems.at[k],
                                            recv_sem=recv_sems.at[k], device_id=(x, y, 1 - c), device_id_type=MESH)
               for k, (src, dst) in enumerate(((in_ref, rin_ref), (out_ref, rout_ref)))]
        for cp in cps:
            cp.start()
        for cp in cps:
            cp.wait()

    hbm = pl.BlockSpec(memory_space=pl.ANY)
    return pl.pallas_call(
        body, name="swap_reduced",
        in_specs=[hbm, hbm], out_specs=[hbm, hbm],
        out_shape=[jax.ShapeDtypeStruct(h.shape, F32) for h in (h_in, h_out)],
        scratch_shapes=[pltpu.SemaphoreType.DMA((2,)), pltpu.SemaphoreType.DMA((2,))],
    )(h_in, h_out)


def _adamw_halves(w, mine, theirs, m, v, cidx, name):
    rows, cols = w.shape
    half = rows // 2
    tr = min(half, 256)
    nbh = half // tr

    def body(c_ref, w_ref, a_ref, b_ref, m_ref, v_ref, g_ref, d_ref, nm_ref, nv_ref):
        gg = jnp.where(pl.program_id(0) // nbh == c_ref[0], a_ref[...], b_ref[...])
        g_ref[...] = gg
        _adamw_update(gg, w_ref, m_ref, v_ref, d_ref, nm_ref, nv_ref)

    spec = pl.BlockSpec((tr, cols), lambda i, c_ref: (i, 0))
    hspec = pl.BlockSpec((tr, cols), lambda i, c_ref: (i % nbh, 0))
    sds = jax.ShapeDtypeStruct((rows, cols), F32)
    return pl.pallas_call(
        body, name=name,
        grid_spec=pltpu.PrefetchScalarGridSpec(
            num_scalar_prefetch=1, grid=(rows // tr,),
            in_specs=[spec, hspec, hspec, spec, spec], out_specs=[spec] * 4),
        out_shape=[sds] * 4,
        compiler_params=_params(("parallel",)),
    )(cidx, w, mine, theirs, m, v)


def _allreduce_small(g_nw, g_fw, g_hgw, g_lbl, loss8):
    def body(nw_ref, fw_ref, hgw_ref, lbl_ref, loss_ref, out_ref, slots, send_sems, recv_sems):
        x, y, c = _place()
        me = 4 * x + 2 * y + c
        slots[me] = jnp.zeros((8, D_MODEL), F32)
        slots[me, 0:1, :] = nw_ref[...]
        slots[me, 1:2, :] = fw_ref[...]
        slots[me, 2:3, 0:HG_W] = hgw_ref[...]
        slots[me, 3:4, 0:HG_W] = lbl_ref[0:1, :]
        slots[me, 3:4, HG_W:] = lbl_ref[1:2, :]
        slots[me, 4:5, 0:128] = loss_ref[0:1, :]
        cps = []
        for k in range(1, 8):
            dx, dy, dc = (k >> 2) & 1, (k >> 1) & 1, k & 1
            to = (x ^ dx, y ^ dy, c ^ dc)
            cps.append(pltpu.make_async_remote_copy(
                src_ref=slots.at[me], dst_ref=slots.at[me], send_sem=send_sems.at[k - 1],
                recv_sem=recv_sems.at[k - 1], device_id=to, device_id_type=MESH))
        for cp in cps:
            cp.start()
        for cp in cps:
            cp.wait()
        acc = slots[0]
        for i in range(1, 8):
            acc = acc + slots[i]
        out_ref[...] = acc

    vm = pl.BlockSpec(memory_space=pltpu.VMEM)
    return pl.pallas_call(
        body, name="allreduce_small",
        in_specs=[vm] * 5, out_specs=vm,
        out_shape=jax.ShapeDtypeStruct((8, D_MODEL), F32),
        scratch_shapes=[pltpu.VMEM((8, 8, D_MODEL), F32), pltpu.SemaphoreType.DMA((7,)), pltpu.SemaphoreType.DMA((7,))],
    )(g_nw, g_fw, g_hgw, g_lbl, loss8)


def _rope_tables(S):
    inv_freq = (np.float32(1.0) / np.power(np.float32(ROPE_THETA), np.arange(ROPE_HALF, dtype=np.float32) / np.float32(ROPE_HALF))).astype(np.float32)
    ang = (np.arange(S, dtype=np.float32)[:, None] * inv_freq[None, :]).astype(np.float32)
    cos, sin = np.cos(ang).astype(np.float32), np.sin(ang).astype(np.float32)
    cos2 = np.concatenate([cos, cos, cos, cos], axis=-1)
    sin2 = np.concatenate([-sin, sin, -sin, sin], axis=-1)
    return jnp.asarray(cos2), jnp.asarray(sin2)


def _local_step(u, x2, tgt2, w_in_full, lbl, hg_norm_w, w_out4, fnw):
    S = x2.shape[0]
    cos2, sin2 = _rope_tables(S)
    w_out_full = w_out4.reshape(D_MODEL, D_MODEL)
    p0, p1, qkv, qkv4, qkv16, p3 = _inproj(u, w_in_full, cos2, sin2)
    o_hg, states = _hg_fwd(p0, p1, lbl)
    qkv_ds = [qkv.reshape(1, S, 3 * AT_W), qkv4, qkv16]
    os_d, ls_d = zip(*[_att_fwd(q) for q in qkv_ds])
    o_at, lse, mix_at = _att_combine(os_d, ls_d, p3)
    dh, do_hg, dz_hg, dm_at, g_wout, g_fw, g_hgw, loss8 = _outproj(x2, tgt2, mix_at, w_out_full, fnw, o_hg, p1, hg_norm_w)
    dqr, dfl, dv_hg, g_lbl = _hg_bwd(p0, p1, do_hg, states, lbl)
    do1, do4, do16, dl1, dl4, dl16, dz_at = _att_gate_bwd(dm_at, o_at, lse, p3)
    do_ds = [do1.reshape(1, S, AT_W), do4, do16]
    dl_ds = [dl1.reshape(1, S, STAT_W), dl4, dl16]
    dqkvs = [_att_bwd(q, a, b) for q, a, b in zip(qkv_ds, do_ds, dl_ds)]
    dq_at, dk_at, dv_at = _att_bwd_combine(dqkvs, cos2, sin2)
    dps = [dqr, dfl, dv_hg, dz_hg, dq_at, dk_at, dv_at, dz_at]
    return loss8, dps, dh, g_lbl, g_hgw, g_wout, g_fw


def kernel(x, norm_w, w_in, hgrn_lb_logits, hg_norm_w, w_out, final_norm_w, loss_target, m_norm_w, m_w_in, m_hgrn_lb_logits, m_hg_norm_w, m_w_out, m_final_norm_w, v_norm_w, v_w_in, v_hgrn_lb_logits, v_hg_norm_w, v_w_out, v_final_norm_w):
    S = x.shape[1]
    win_b, wout_b = _cast_shards(w_in[0], w_out[0])
    *gathering, token = _start_copies("gather_start", _gather_copies, (win_b, wout_b), ((3, 512, 1024), (3, 128, D_MODEL)))
    u = _prenorm(x[0], norm_w + token[0:1, 0:1])
    win_b, wout_b, lin, lout = _wait_copies("gather_wait", _gather_copies, *gathering, u)
    w_in_full, w_out_full = _spread_weights(win_b, wout_b, lin, lout)
    loss8, dps, dh, g_lbl, g_hgw, g_wout, g_fw = _local_step(
        u, x[0], loss_target[0], w_in_full, hgrn_lb_logits, hg_norm_w, w_out_full, final_norm_w.reshape(1, D_MODEL))

    cidx = lax.axis_index("c").astype(jnp.int32).reshape(1)
    g_win = _inproj_bwd_w(u, dps)
    g_in4 = g_win.reshape(4, 2, 512, 1024)
    g_out4 = g_wout.reshape(4, 2, 128, D_MODEL)
    r_in, r_out = _swap_halves(g_in4, g_out4)
    cs_in = _add_half(g_in4, r_in, cidx, "add_half_in")
    cs_out = _add_half(g_out4, r_out, cidx, "add_half_out")
    *started, token = _start_copies("exchange_start", _exchange_copies, (cs_in, cs_out),
                                    ((3,) + cs_in.shape[1:], (3,) + cs_out.shape[1:]))
    grad_x, g_nw = _inproj_bwd_x(dps, w_in_full, x[0], norm_w + token[0:1, 0:1], dh)
    cs_in, cs_out, x_in, x_out = _wait_copies("exchange_wait", _exchange_copies, *started, g_nw)
    chip_idx = (2 * lax.axis_index("x") + lax.axis_index("y")).astype(jnp.int32).reshape(1)
    h_in = _sum_chips(cs_in, x_in, chip_idx, "sum_chips_in")
    h_out = _sum_chips(cs_out, x_out, chip_idx, "sum_chips_out")
    s_in, s_out = _swap_reduced(h_in, h_out)

    red = _allreduce_small(g_nw, g_fw, g_hgw, g_lbl, loss8)
    loss = red[4, 0]
    grad_norm_w = red[0:1, :]
    grad_final_norm_w = red[1, :]
    grad_hg_norm_w = red[2:3, :HG_W]
    grad_lbl = jnp.concatenate([red[3:4, :HG_W], red[3:4, HG_W:]], axis=0)

    d_nw, m_nw, v_nw = _adamw(norm_w, grad_norm_w, m_norm_w, v_norm_w, "adamw_norm_w")
    grad_w_in, d_win, m_win, v_win = _adamw_halves(w_in[0], h_in, s_in, m_w_in[0], v_w_in[0], cidx, "adamw_w_in")
    d_lbl, m_lbl, v_lbl = _adamw(hgrn_lb_logits, grad_lbl, m_hgrn_lb_logits, v_hgrn_lb_logits, "adamw_lb_logits")
    d_hgw, m_hgw, v_hgw = _adamw(hg_norm_w, grad_hg_norm_w, m_hg_norm_w, v_hg_norm_w, "adamw_hg_norm_w")
    grad_w_out, d_wout, m_wout, v_wout = _adamw_halves(w_out[0], h_out, s_out, m_w_out[0], v_w_out[0], cidx, "adamw_w_out")
    d_fw, m_fw, v_fw = _adamw(final_norm_w.reshape(1, D_MODEL), grad_final_norm_w.reshape(1, D_MODEL),
                              m_final_norm_w.reshape(1, D_MODEL), v_final_norm_w.reshape(1, D_MODEL), "adamw_final_norm_w")
    e1 = lambda a: a[None]
    flat = lambda a: a.reshape(D_MODEL)
    return (loss, grad_x[None], grad_norm_w, e1(grad_w_in), grad_lbl, grad_hg_norm_w, e1(grad_w_out), grad_final_norm_w,
            d_nw, e1(d_win), d_lbl, d_hgw, e1(d_wout), flat(d_fw),
            m_nw, e1(m_win), m_lbl, m_hgw, e1(m_wout), flat(m_fw),
            v_nw, e1(v_win), v_lbl, v_hgw, e1(v_wout), flat(v_fw))
```

```python
import jax
import jax.numpy as jnp
import numpy as np
from jax import lax
from jax.experimental import pallas as pl
from jax.experimental.pallas import tpu as pltpu

F32 = jnp.float32
BF16 = jnp.bfloat16
MESH = pl.DeviceIdType.MESH

D_MODEL = 1024
HG_W = 512
AT_W = 512
HEAD_PAIR = 128
ROPE_HALF = 32
ROPE_THETA = 10000.0
EPS = 1e-6
CHUNK = 128
LEVELS = (64, 32, 16, 8, 4, 2, 1)
DIAG = 1
SUBLANES = 8
ATT_BLK = 128
ATT_QB = 4
DILATIONS = (1, 4, 16)
ATT_SCALE = 0.125
STAT_W = 128
STAT_LANES = 16
STAT_LSE_LANE = 8
NEG = -1e30
VMEM_LIMIT = 56 * 1024 * 1024
MIX_TM = 1024
PROJ_TM = 512

ADAM_LR = 0.001
ADAM_B1 = 0.9
ADAM_B2 = 0.999
ADAM_EPS = 1e-08
ADAM_WD = 0.01
ADAM_STEP = 10


def _iota(shape, dim):
    return lax.broadcasted_iota(jnp.int32, shape, dim)


def _dot(a, b):
    return jnp.dot(a, b, preferred_element_type=F32)


def _dot_nt(a, b):
    return lax.dot_general(a, b, (((1,), (1,)), ((), ())), preferred_element_type=F32)


def _dot_tn(a, b):
    return lax.dot_general(a, b, (((0,), (0,)), ((), ())), preferred_element_type=F32)


def _sigmoid(v):
    return 0.5 * jnp.tanh(0.5 * v) + 0.5


def _params(sem=None, vmem=VMEM_LIMIT):
    return pltpu.CompilerParams(dimension_semantics=sem, vmem_limit_bytes=vmem)


def _full(shape):
    n = len(shape)
    return pl.BlockSpec(shape, lambda *_: (0,) * n)


def _rope_rot(y):
    n = y.shape[1]
    first = (_iota(y.shape, 1) & (2 * ROPE_HALF - 1)) < ROPE_HALF
    return jnp.where(first, pltpu.roll(y, n - ROPE_HALF, 1), pltpu.roll(y, ROPE_HALF, 1))


def _dil_spec(d, tm, width):
    return pl.BlockSpec((d, tm // d, width), lambda s: (0, s, 0))


LANES = 128


def _slab_scratch(tm, width):
    return pltpu.VMEM((width // LANES, tm, LANES), F32)


def _to_slabs(v, slabs_ref):
    for j in range(slabs_ref.shape[0]):
        slabs_ref[j] = v[:, LANES * j: LANES * (j + 1)]


def _from_slabs(slabs_ref):
    return jnp.concatenate([slabs_ref[j] for j in range(slabs_ref.shape[0])], axis=1)


def _split_residues(slabs_ref, dst_ref, d, dtype):
    nslab, tm, _ = slabs_ref.shape
    for r in range(d):
        for j in range(nslab):
            dst_ref[r, :, LANES * j: LANES * (j + 1)] = slabs_ref[j, pl.ds(r, tm // d, stride=d), :].astype(dtype)


def _merge_residues(src_ref, slabs_ref, d):
    nslab, tm, _ = slabs_ref.shape
    for r in range(d):
        for j in range(nslab):
            slabs_ref[j, pl.ds(r, tm // d, stride=d), :] = src_ref[r, :, LANES * j: LANES * (j + 1)].astype(F32)


PERM_ROWS = 256


def _perm_matrix(d):
    n = PERM_ROWS // d
    t = np.arange(PERM_ROWS)
    p = np.zeros((PERM_ROWS, PERM_ROWS), np.float32)
    p[t, (t % d) * n + t // d] = 1.0
    return jnp.asarray(p, BF16)


def _merge_bf16(src_ref, perm_ref, d):
    n = PERM_ROWS // d
    tm = src_ref.shape[1] * d
    outs = []
    for b in range(tm // PERM_ROWS):
        flat = jnp.concatenate([src_ref[r, n * b: n * (b + 1), :] for r in range(d)], axis=0)
        outs.append(_dot(perm_ref[...], flat))
    return jnp.concatenate(outs, axis=0)


def _inproj(x2, norm_w, w_in_full, cos2, sin2):
    S = x2.shape[0]
    tm = PROJ_TM

    def body(x_ref, nw_ref, w_ref, cos_ref, sin_ref, p0_ref, p1_ref, qkv_ref, qkv4_ref, qkv16_ref, p3_ref, u_ref, scr):
        x = x_ref[...]
        r = lax.rsqrt(jnp.mean(x * x, axis=-1, keepdims=True) + EPS)
        u = x * r * nw_ref[...]
        ub = u.astype(BF16)
        u_ref[...] = ub
        p0_ref[...] = _dot(ub, w_ref[0])
        p1_ref[...] = _dot(ub, w_ref[1])
        y2 = _dot(ub, w_ref[2])
        cosf = jnp.tile(cos_ref[...], (1, 8))
        sinf = jnp.tile(sin_ref[...], (1, 8))
        y3 = _dot(ub, w_ref[3])
        p3_ref[...] = y3
        qkv = jnp.concatenate([y2 * cosf + _rope_rot(y2) * sinf, y3[:, :AT_W]], axis=1)
        qkv_ref[...] = qkv.astype(BF16)
        _to_slabs(qkv, scr)
        _split_residues(scr, qkv4_ref, 4, BF16)
        _split_residues(scr, qkv16_ref, 16, BF16)

    row = lambda w: pl.BlockSpec((tm, w), lambda s: (s, 0))
    qkv_w = 3 * AT_W
    return pl.pallas_call(
        body, name="inproj", grid=(S // tm,),
        in_specs=[row(D_MODEL), _full((1, D_MODEL)), _full((4, D_MODEL, 1024)), row(128), row(128)],
        out_specs=[row(1024), row(1024), row(qkv_w), _dil_spec(4, tm, qkv_w), _dil_spec(16, tm, qkv_w), row(1024),
                   row(D_MODEL)],
        out_shape=[jax.ShapeDtypeStruct((S, 1024), F32), jax.ShapeDtypeStruct((S, 1024), F32),
                   jax.ShapeDtypeStruct((S, qkv_w), BF16), jax.ShapeDtypeStruct((4, S // 4, qkv_w), BF16),
                   jax.ShapeDtypeStruct((16, S // 16, qkv_w), BF16), jax.ShapeDtypeStruct((S, 1024), F32),
                   jax.ShapeDtypeStruct((S, D_MODEL), BF16)],
        scratch_shapes=[_slab_scratch(tm, qkv_w)],
        compiler_params=_params(("parallel",)),
    )(x2, norm_w, w_in_full, cos2, sin2)


HG_HPS = 4
N_LEV = len(LEVELS)


def _hg_const_arrays():
    r = np.arange(CHUNK)[:, None]
    c = np.arange(CHUNK)[None, :]
    tris = np.stack([r >= c, r <= c])
    lm = [((r // (2 * m)) == (c // (2 * m))) & (r % (2 * m) >= m) & (c % (2 * m) < m) for m in LEVELS]
    dm = [(c == r - dl) & (r % DIAG >= dl) for dl in range(DIAG)]
    masks = np.stack(lm + dm)
    return jnp.asarray(tris, BF16), jnp.asarray(masks, F32)


def _split2(a):
    hi = a.astype(BF16)
    return hi, (a - hi.astype(F32)).astype(BF16)


def _dot3(a, b, dot=_dot):
    ah, al = _split2(a)
    bh, bl = _split2(b)
    n = b.shape[1]
    p = dot(ah, jnp.concatenate([bh, bl], axis=1))
    return (p[:, :n] + p[:, n:]) + dot(al, bh)


def _split3(a):
    a1 = a.astype(BF16)
    r1 = a - a1.astype(F32)
    a2 = r1.astype(BF16)
    return a1, a2, (r1 - a2.astype(F32)).astype(BF16)


def _tri_dot(tri, a):
    n = a.shape[1]
    p = _dot(tri, jnp.concatenate(_split3(a), axis=1))
    return (p[:, :n] + p[:, n:2 * n]) + p[:, 2 * n:]


def _dot_sel(a, sel):
    a1, a2, a3 = _split3(a)
    return (_dot(a1, sel) + _dot(a2, sel)) + _dot(a3, sel)


def _rowsum(t):
    return _dot(t.astype(BF16), jnp.ones((t.shape[1], t.shape[1]), BF16))


def _level_refs(b):
    refs = []
    pos = _iota(b.shape, 0)
    for m in LEVELS:
        if 2 * m >= SUBLANES:
            parts = [jnp.broadcast_to(b[r0 + m - 1: r0 + m, :], (2 * m, b.shape[1])) for r0 in range(0, CHUNK, 2 * m)]
            refs.append(parts[0] if len(parts) == 1 else jnp.concatenate(parts, axis=0))
        else:
            p = pos & (2 * m - 1)
            ref = b
            for off in range(-(m - 1), m + 1):
                if off != 0:
                    ref = jnp.where(p == m - 1 + off, pltpu.roll(b, off % CHUNK, 0), ref)
            refs.append(ref)
    return refs


def _hg_lb(lbl_ref):
    l0 = lbl_ref[0:1, :]
    l1 = lbl_ref[1:2, :]
    mx = jnp.maximum(l0, l1)
    e0 = jnp.exp(l0 - mx)
    e1 = jnp.exp(l1 - mx)
    p0 = e0 / (e0 + e1)
    lb = jnp.clip(p0, 1e-6, 1.0 - 1e-6)
    inside = (p0 >= 1e-6) & (p0 <= 1.0 - 1e-6)
    dlb_dl0 = jnp.where(inside, p0 * (e1 / (e0 + e1)), 0.0)
    return lb, dlb_dl0


def _hg_gates(qr, fl, lb):
    sig = _sigmoid(fl)
    f = lb + (1.0 - lb) * sig
    g = jnp.log(f)
    k = (1.0 - lb) * (1.0 - sig)
    sq = _sigmoid(qr)
    q = qr * sq
    return sig, f, g, k, sq, q


def _hg_levels(q, k, b, mk_ref):
    refs = _level_refs(b)
    a = jnp.zeros((CHUNK, CHUNK), F32)
    es, qts, kts = [], [], []
    for i in range(N_LEV):
        diff = b - refs[i]
        e = jnp.exp(jnp.minimum(diff, -diff))
        qt = (q * e).astype(BF16)
        kt = (k * e).astype(BF16)
        a = a + _dot_nt(qt, kt) * mk_ref[i]
        es.append(e); qts.append(qt); kts.append(kt)
    return a, es, qts, kts


def _hg_specs(nc, rev):
    cc = (lambda c: nc - 1 - c) if rev else (lambda c: c)
    w = 128 * HG_HPS
    blk = lambda off: pl.BlockSpec((CHUNK, w), lambda h, c: (cc(c), h + off))
    vec = pl.BlockSpec((1, w), lambda h, c: (0, h))
    lb2 = pl.BlockSpec((2, w), lambda h, c: (0, h))
    st = pl.BlockSpec((1, HG_HPS, 128, 128), lambda h, c: (cc(c), h, 0, 0))
    consts = [_full((2, CHUNK, CHUNK)), _full((N_LEV + DIAG, CHUNK, CHUNK))]
    return blk, vec, lb2, st, consts


def _hg_fwd(p0, p1, lbl):
    S = p0.shape[0]
    nc = S // CHUNK
    ng = 4 // HG_HPS

    def body(qr_ref, fl_ref, v_ref, lbl_ref, tri_ref, mk_ref, o_ref, st_ref, state):
        c = pl.program_id(1)

        @pl.when(c == 0)
        def _():
            state[...] = jnp.zeros_like(state)

        lb_all, _ = _hg_lb(lbl_ref)
        heads = [slice(128 * hh, 128 * (hh + 1)) for hh in range(HG_HPS)]
        qs, ks, bs, mats = [], [], [], []
        for sl in heads:
            _, _, g, k, _, q = _hg_gates(qr_ref[:, sl], fl_ref[:, sl], lb_all[:, sl])
            qs.append(q); ks.append(k); bs.append(_tri_dot(tri_ref[0], g))
        for hh in range(HG_HPS):
            a, _, _, _ = _hg_levels(qs[hh], ks[hh], bs[hh], mk_ref)
            mats.append(a + _rowsum(qs[hh] * ks[hh]) * mk_ref[N_LEV])
        for hh, sl in enumerate(heads):
            q, k, b, a, v = qs[hh], ks[hh], bs[hh], mats[hh], v_ref[:, sl]
            b_last = b[CHUNK - 1: CHUNK, :]
            st = state[hh]
            st_ref[0, hh] = st
            o = _dot_nt((q * jnp.exp(b)).astype(BF16), st.astype(BF16)) + _dot(a.astype(BF16), v.astype(BF16))
            state[hh] = st * jnp.exp(b_last) + _dot3(v, k * jnp.exp(b_last - b), _dot_tn)
            o_ref[:, sl] = o

    blk, vec, lb2, st_spec, consts = _hg_specs(nc, False)
    tris, masks = _hg_const_arrays()
    return pl.pallas_call(
        body, name="hg_fwd", grid=(ng, nc),
        in_specs=[blk(0), blk(ng), blk(0), lb2] + consts,
        out_specs=[blk(0), st_spec],
        out_shape=[jax.ShapeDtypeStruct((S, HG_W), F32), jax.ShapeDtypeStruct((nc, 4, 128, 128), F32)],
        scratch_shapes=[pltpu.VMEM((HG_HPS, 128, 128), F32)],
        compiler_params=_params(("parallel", "arbitrary")),
    )(p0, p0, p1, lbl, tris, masks)


def _hg_bwd(p0, p1, do_hg, states, lbl):
    S = p0.shape[0]
    nc = S // CHUNK
    ng = 4 // HG_HPS
    w = 128 * HG_HPS

    def body(qr_ref, fl_ref, v_ref, do_ref, st_ref, lbl_ref, tri_ref, mk_ref, mkb_ref,
             dqr_ref, dfl_ref, dv_ref, glbl_ref, dstate, carry, acc_lb):
        c = pl.program_id(1)

        @pl.when(c == 0)
        def _():
            dstate[...] = jnp.zeros_like(dstate)
            carry[...] = jnp.zeros_like(carry)
            acc_lb[...] = jnp.zeros_like(acc_lb)

        lb_all, dlb_dl0 = _hg_lb(lbl_ref)
        heads = [slice(128 * hh, 128 * (hh + 1)) for hh in range(HG_HPS)]
        diag_mask = mk_ref[N_LEV]
        gates, bs, dos = [], [], []
        for sl in heads:
            gt = _hg_gates(qr_ref[:, sl], fl_ref[:, sl], lb_all[:, sl])
            gates.append(gt)
            bs.append(_tri_dot(tri_ref[0], gt[2]))
        for sl in heads:
            dos.append(do_ref[:, sl])
        inter = []
        for hh, sl in enumerate(heads):
            _, _, _, k, _, q = gates[hh]
            b, do, v = bs[hh], dos[hh], v_ref[:, sl]
            b_last = b[CHUNK - 1: CHUNK, :]
            eb = jnp.exp(b)
            edec = jnp.exp(b_last - b)
            dst = dstate[hh]
            dq = _dot3(do, st_ref[0, hh]) * eb
            dk = _dot3(v, dst) * edec
            da = _dot_nt(do.astype(BF16), v.astype(BF16))
            dv_state = _dot_nt((k * edec).astype(BF16), dst.astype(BF16))
            dstate[hh] = dst * jnp.exp(b_last) + _dot3(do, q * eb, _dot_tn)
            inter.append((dq, dk, da, dv_state))
        for hh, sl in enumerate(heads):
            sig, f, _, k, sq, q = gates[hh]
            dq, dk, da, dv_state = inter[hh]
            b = bs[hh]
            db = q * dq - k * dk
            dab = da.astype(BF16)
            refs = _level_refs(b)
            a = _rowsum(q * k) * diag_mask
            for i in range(N_LEV):
                diff = b - refs[i]
                e = jnp.exp(jnp.minimum(diff, -diff))
                qt = (q * e).astype(BF16)
                kt = (k * e).astype(BF16)
                a = a + _dot_nt(qt, kt) * mk_ref[i]
                dam = dab * mkb_ref[i]
                gq = _dot(dam, kt)
                gk = _dot_tn(dam, qt)
                dq = dq + e * gq
                dk = dk + e * gk
                db = db + (qt.astype(F32) * gq - kt.astype(F32) * gk)
            dc = _rowsum(da * diag_mask)
            dq = dq + dc * k
            dk = dk + dc * q
            dv_ref[:, sl] = (_dot_tn(a.astype(BF16), dos[hh].astype(BF16)) + dv_state).astype(BF16)
            dg = _tri_dot(tri_ref[1], db) + carry[0:1, sl]
            carry[0:1, sl] += jnp.sum(db, axis=0, keepdims=True)
            lb = lb_all[:, sl]
            qr = qr_ref[:, sl]
            t = dg / f - dk
            dfl_ref[:, sl] = (t * (1.0 - lb) * sig * (1.0 - sig)).astype(BF16)
            acc_lb[0:1, sl] += jnp.sum(t * (1.0 - sig), axis=0, keepdims=True)
            dqr_ref[:, sl] = (dq * (sq * (1.0 + qr * (1.0 - sq)))).astype(BF16)

        @pl.when(c == nc - 1)
        def _():
            gl0 = acc_lb[0:1, :] * dlb_dl0
            glbl_ref[0:1, :] = gl0
            glbl_ref[1:2, :] = -gl0

    blk, vec, lb2, st_spec, consts = _hg_specs(nc, True)
    tris, masks = _hg_const_arrays()
    act = jax.ShapeDtypeStruct((S, HG_W), BF16)
    return pl.pallas_call(
        body, name="hg_bwd", grid=(ng, nc),
        in_specs=[blk(0), blk(ng), blk(0), blk(0), st_spec, lb2] + consts + consts[1:],
        out_specs=[blk(0), blk(0), blk(0), lb2],
        out_shape=[act, act, act, jax.ShapeDtypeStruct((2, HG_W), F32)],
        scratch_shapes=[pltpu.VMEM((HG_HPS, 128, 128), F32), pltpu.VMEM((8, w), F32), pltpu.VMEM((8, w), F32)],
        compiler_params=_params(("parallel", "arbitrary")),
    )(p0, p0, p1, do_hg, states, lbl, tris, masks, masks.astype(BF16))


def _stat_head(lane):
    return lane >> 4


def _stat_lane(lane):
    return lane & (STAT_LANES - 1)


def _o_head(lane):
    return lane >> 6


def _att_bias():
    qi = np.arange(ATT_BLK)[:, None]
    kj = np.arange(2 * ATT_BLK)[None, :]
    band = (kj >= qi) & (kj <= qi + ATT_BLK)
    qm = np.stack([band & (kj >= ATT_BLK), band])
    cur = (kj < ATT_BLK) & (qi <= kj)
    km = np.stack([cur, cur | ((kj >= ATT_BLK) & (qi >= kj - ATT_BLK))])
    to_bias = lambda m: jnp.asarray(np.where(m, 0.0, NEG), F32)
    return to_bias(qm), to_bias(km)


def _att_fwd(qkv_d):
    d, L, _ = qkv_d.shape
    qb = ATT_QB
    rows = qb * ATT_BLK

    def body(q_ref, kp_ref, kc_ref, vp_ref, vc_ref, bias_ref, o_ref, lse_ref, s_scr, p_scr):
        first = _iota((ATT_BLK, HEAD_PAIR), 1) < 64
        head_of_lane = _stat_head(_iota((ATT_BLK, STAT_W), 1))
        pairs = [slice(HEAD_PAIR * hp, HEAD_PAIR * (hp + 1)) for hp in range(4)]
        blk = lambda i: slice(ATT_BLK * i, ATT_BLK * (i + 1))

        def keys(i, sl, prev_ref, cur_ref):
            before = prev_ref[0, :, sl] if i == 0 else cur_ref[0, blk(i - 1), sl]
            return jnp.concatenate([before, cur_ref[0, blk(i), sl]], axis=0)

        for i in range(qb):
            for hp, sl in enumerate(pairs):
                q2 = q_ref[0, blk(i), sl] * ATT_SCALE
                zero = jnp.zeros_like(q2)
                qs = jnp.concatenate([jnp.where(first, q2, zero), jnp.where(first, zero, q2)], axis=0)
                s_scr[4 * i + hp] = _dot_nt(qs, keys(i, sl, kp_ref, kc_ref))
        stats = []
        for i in range(qb):
            bias = bias_ref[jnp.minimum(pl.program_id(1), 1)] if i == 0 else bias_ref[1]
            bias2 = jnp.concatenate([bias, bias], axis=0)
            for hp in range(4):
                s = s_scr[4 * i + hp] + bias2
                m = jnp.max(s, axis=-1, keepdims=True)
                p = jnp.exp(s - m)
                l = jnp.sum(p, axis=-1, keepdims=True)
                p_scr[4 * i + hp] = p.astype(BF16)
                stats.append((l, m + jnp.log(l)))
        for i in range(qb):
            lse_blk = jnp.zeros((ATT_BLK, STAT_W), F32)
            for hp, sl in enumerate(pairs):
                l, lse = stats[4 * i + hp]
                o = _dot(p_scr[4 * i + hp], keys(i, sl, vp_ref, vc_ref)) / l
                o_ref[0, blk(i), sl] = jnp.where(first, o[:ATT_BLK], o[ATT_BLK:]).astype(BF16)
                lse_blk = jnp.where(head_of_lane == 2 * hp, lse[:ATT_BLK],
                                    jnp.where(head_of_lane == 2 * hp + 1, lse[ATT_BLK:], lse_blk))
            lse_ref[0, blk(i), :] = lse_blk

    cur = lambda j: pl.BlockSpec((1, rows, AT_W), lambda r, n: (r, n, j))
    prev = lambda j: pl.BlockSpec((1, ATT_BLK, AT_W), lambda r, n: (r, jnp.maximum(qb * n - 1, 0), j))
    return pl.pallas_call(
        body, name=f"att_fwd_d{d}", grid=(d, L // rows),
        in_specs=[cur(0), prev(1), cur(1), prev(2), cur(2), _full((2, ATT_BLK, 2 * ATT_BLK))],
        out_specs=[pl.BlockSpec((1, rows, AT_W), lambda r, n: (r, n, 0)), pl.BlockSpec((1, rows, STAT_W), lambda r, n: (r, n, 0))],
        out_shape=[jax.ShapeDtypeStruct((d, L, AT_W), BF16), jax.ShapeDtypeStruct((d, L, STAT_W), F32)],
        scratch_shapes=[pltpu.VMEM((4 * qb, 2 * ATT_BLK, 2 * ATT_BLK), F32), pltpu.VMEM((4 * qb, 2 * ATT_BLK, 2 * ATT_BLK), BF16)],
        compiler_params=_params(("parallel", "parallel")),
    )(qkv_d, qkv_d, qkv_d, qkv_d, qkv_d, _att_bias()[0])


def _att_combine(os_d, ls_d, p3):
    S = p3.shape[0]
    tm = MIX_TM

    def body(oa_ref, ob4_ref, oc16_ref, la_ref, lb4_ref, lc16_ref, p4_ref, p16_ref, z_ref,
             oat_ref, lse_ref, mix_ref, lb_scr, lc_scr):
        _merge_residues(lb4_ref, lb_scr, 4)
        _merge_residues(lc16_ref, lc_scr, 16)
        ls = (la_ref[...], lb_scr[0], lc_scr[0])
        mx = jnp.maximum(jnp.maximum(ls[0], ls[1]), ls[2])
        es = [jnp.exp(l - mx) for l in ls]
        zs = es[0] + es[1] + es[2]
        lse_ref[...] = mx + jnp.log(zs)
        spread = ((_o_head(_iota((STAT_W, AT_W), 1)) == _stat_head(_iota((STAT_W, AT_W), 0)))
                  & (_stat_lane(_iota((STAT_W, AT_W), 0)) == 0)).astype(BF16)
        os_ = (oa_ref[...].astype(F32), _merge_bf16(ob4_ref, p4_ref, 4), _merge_bf16(oc16_ref, p16_ref, 16))
        o = jnp.zeros((tm, AT_W), F32)
        for e, oi in zip(es, os_):
            hi, lo = _split2(e / zs)
            o = o + (_dot(hi, spread) + _dot(lo, spread)) * oi
        oat_ref[...] = o
        z = z_ref[...]
        mixed = o * (z * _sigmoid(z))
        mix_ref[...] = mixed.astype(BF16)

    row = lambda w: pl.BlockSpec((tm, w), lambda s: (s, 0))
    return pl.pallas_call(
        body, name="att_combine", grid=(S // tm,),
        in_specs=[row(AT_W), _dil_spec(4, tm, AT_W), _dil_spec(16, tm, AT_W),
                  row(STAT_W), _dil_spec(4, tm, STAT_W), _dil_spec(16, tm, STAT_W),
                  _full((PERM_ROWS, PERM_ROWS)), _full((PERM_ROWS, PERM_ROWS)),
                  pl.BlockSpec((tm, AT_W), lambda s: (s, 1))],
        out_specs=[row(AT_W), row(STAT_W), row(AT_W)],
        out_shape=[jax.ShapeDtypeStruct((S, AT_W), F32), jax.ShapeDtypeStruct((S, STAT_W), F32),
                   jax.ShapeDtypeStruct((S, AT_W), BF16)],
        scratch_shapes=[_slab_scratch(tm, STAT_W), _slab_scratch(tm, STAT_W)],
        compiler_params=_params(("parallel",)),
    )(os_d[0].reshape(S, AT_W), os_d[1], os_d[2], ls_d[0].reshape(S, STAT_W), ls_d[1], ls_d[2],
      _perm_matrix(4), _perm_matrix(16), p3)


def _att_gate_bwd(dm_at, o_at, lse, p3):
    S = p3.shape[0]
    tm = MIX_TM

    def body(dm_ref, o_ref, l_ref, z_ref, do_ref, do4_ref, do16_ref, dl_ref, dl4_ref, dl16_ref, dz_ref, do_scr, dl_scr):
        o = o_ref[...]
        z = z_ref[...]
        dm = dm_ref[...]
        sz = _sigmoid(z)
        dz_ref[...] = (dm * o * (sz * (1.0 + z * (1.0 - sz)))).astype(BF16)
        do = dm * (z * sz)
        do_ref[...] = do.astype(BF16)
        gather = (_o_head(_iota((AT_W, STAT_W), 0)) == _stat_head(_iota((AT_W, STAT_W), 1))).astype(BF16)
        dl = jnp.where(_stat_lane(_iota((tm, STAT_W), 1)) < STAT_LSE_LANE, _dot_sel(do * o, gather), l_ref[...])
        dl_ref[...] = dl
        _to_slabs(do, do_scr)
        _to_slabs(dl, dl_scr)
        _split_residues(do_scr, do4_ref, 4, BF16)
        _split_residues(do_scr, do16_ref, 16, BF16)
        _split_residues(dl_scr, dl4_ref, 4, F32)
        _split_residues(dl_scr, dl16_ref, 16, F32)

    row = lambda w: pl.BlockSpec((tm, w), lambda s: (s, 0))
    sds = jax.ShapeDtypeStruct
    return pl.pallas_call(
        body, name="att_gate_bwd", grid=(S // tm,),
        in_specs=[row(AT_W), row(AT_W), row(STAT_W), pl.BlockSpec((tm, AT_W), lambda s: (s, 1))],
        out_specs=[row(AT_W), _dil_spec(4, tm, AT_W), _dil_spec(16, tm, AT_W),
                   row(STAT_W), _dil_spec(4, tm, STAT_W), _dil_spec(16, tm, STAT_W), row(AT_W)],
        out_shape=[sds((S, AT_W), BF16), sds((4, S // 4, AT_W), BF16), sds((16, S // 16, AT_W), BF16),
                   sds((S, STAT_W), F32), sds((4, S // 4, STAT_W), F32), sds((16, S // 16, STAT_W), F32),
                   sds((S, AT_W), BF16)],
        scratch_shapes=[_slab_scratch(tm, AT_W), _slab_scratch(tm, STAT_W)],
        compiler_params=_params(("parallel",)),
    )(dm_at, o_at, lse, p3)


def _att_bwd(qkv_d, do_d, dl_d):
    d, L, _ = qkv_d.shape
    nb = L // ATT_BLK
    qb = ATT_QB
    rows = qb * ATT_BLK
    nsteps = L // rows

    def body(qc_ref, qn_ref, kp_ref, kc_ref, vp_ref, vc_ref, ac_ref, an_ref, lc_ref, ln_ref, bq_ref, bk_ref, dqkv_ref,
             s_scr, dp_scr, st_scr, dpt_scr, ds_scr, pt_scr, dst_scr):
        n = pl.program_id(1)
        first = _iota((ATT_BLK, HEAD_PAIR), 1) < 64
        pairs = [slice(HEAD_PAIR * hp, HEAD_PAIR * (hp + 1)) for hp in range(4)]
        blk = lambda i: slice(ATT_BLK * i, ATT_BLK * (i + 1))

        def stack(t):
            zero = jnp.zeros_like(t)
            return jnp.concatenate([jnp.where(first, t, zero), jnp.where(first, zero, t)], axis=0)

        def unstack(t2):
            return jnp.where(first, t2[:ATT_BLK], t2[ATT_BLK:])

        def with_prev(i, sl, prev_ref, cur_ref):
            before = prev_ref[0, :, sl] if i == 0 else cur_ref[0, blk(i - 1), sl]
            return jnp.concatenate([before, cur_ref[0, blk(i), sl]], axis=0)

        def with_next(i, sl, cur_ref, next_ref):
            after = next_ref[0, :, sl] if i == qb - 1 else cur_ref[0, blk(i + 1), sl]
            return jnp.concatenate([cur_ref[0, blk(i), sl], after], axis=0)

        for i in range(qb):
            for hp, sl in enumerate(pairs):
                j = 4 * i + hp
                s_scr[j] = _dot_nt(stack(qc_ref[0, blk(i), sl] * ATT_SCALE), with_prev(i, sl, kp_ref, kc_ref))
                dp_scr[j] = _dot_nt(stack(ac_ref[0, blk(i), sl]), with_prev(i, sl, vp_ref, vc_ref))
                st_scr[j] = _dot_nt(stack(kc_ref[0, blk(i), sl] * ATT_SCALE), with_next(i, sl, qc_ref, qn_ref))
                dpt_scr[j] = _dot_nt(stack(vc_ref[0, blk(i), sl]), with_next(i, sl, ac_ref, an_ref))
        for i in range(qb):
            bias = bq_ref[jnp.minimum(n, 1)] if i == 0 else bq_ref[1]
            bias_t = bk_ref[jnp.minimum(nsteps - 1 - n, 1)] if i == qb - 1 else bk_ref[1]
            bias2 = jnp.concatenate([bias, bias], axis=0)
            bias_t2 = jnp.concatenate([bias_t, bias_t], axis=0)
            dl_c = lc_ref[0, blk(i), :]
            dl_t = with_next(i, slice(None), lc_ref, ln_ref).T
            for hp in range(4):
                j = 4 * i + hp
                at = [STAT_LANES * (2 * hp), STAT_LANES * (2 * hp + 1)]
                col = lambda t, o: jnp.concatenate([t[:, a + o: a + o + 1] for a in at], axis=0)
                p = jnp.exp(s_scr[j] + bias2 - col(dl_c, STAT_LSE_LANE))
                ds_scr[j] = (p * (dp_scr[j] - col(dl_c, 0))).astype(BF16)
                row = lambda t, o: jnp.concatenate([jnp.broadcast_to(t[a + o: a + o + 1, :], (ATT_BLK, 2 * ATT_BLK)) for a in at], axis=0)
                pt = jnp.exp(st_scr[j] + bias_t2 - row(dl_t, STAT_LSE_LANE))
                pt_scr[j] = pt.astype(BF16)
                dst_scr[j] = (pt * (dpt_scr[j] - row(dl_t, 0))).astype(BF16)
        for i in range(qb):
            for hp, sl in enumerate(pairs):
                j = 4 * i + hp
                dq = unstack(_dot(ds_scr[j], with_prev(i, sl, kp_ref, kc_ref))) * ATT_SCALE
                dk = unstack(_dot(dst_scr[j], with_next(i, sl, qc_ref, qn_ref))) * ATT_SCALE
                dv = unstack(_dot(pt_scr[j], with_next(i, sl, ac_ref, an_ref)))
                dqkv_ref[0, blk(i), sl] = dq.astype(BF16)
                dqkv_ref[0, blk(i), AT_W + HEAD_PAIR * hp: AT_W + HEAD_PAIR * (hp + 1)] = dk.astype(BF16)
                dqkv_ref[0, blk(i), 2 * AT_W + HEAD_PAIR * hp: 2 * AT_W + HEAD_PAIR * (hp + 1)] = dv.astype(BF16)

    cur = lambda j: pl.BlockSpec((1, rows, AT_W), lambda r, n: (r, n, j))
    prev = lambda j: pl.BlockSpec((1, ATT_BLK, AT_W), lambda r, n: (r, jnp.maximum(qb * n - 1, 0), j))
    nxt_blk = lambda n: jnp.minimum(qb * (n + 1), nb - 1)
    sq = (4 * qb, 2 * ATT_BLK, 2 * ATT_BLK)
    return pl.pallas_call(
        body, name=f"att_bwd_d{d}", grid=(d, nsteps),
        in_specs=[cur(0), pl.BlockSpec((1, ATT_BLK, AT_W), lambda r, n: (r, nxt_blk(n), 0)), prev(1), cur(1), prev(2), cur(2),
                  pl.BlockSpec((1, rows, AT_W), lambda r, n: (r, n, 0)),
                  pl.BlockSpec((1, ATT_BLK, AT_W), lambda r, n: (r, nxt_blk(n), 0)),
                  pl.BlockSpec((1, rows, STAT_W), lambda r, n: (r, n, 0)),
                  pl.BlockSpec((1, ATT_BLK, STAT_W), lambda r, n: (r, nxt_blk(n), 0)),
                  _full((2, ATT_BLK, 2 * ATT_BLK)), _full((2, ATT_BLK, 2 * ATT_BLK))],
        out_specs=pl.BlockSpec((1, rows, 3 * AT_W), lambda r, n: (r, n, 0)),
        out_shape=jax.ShapeDtypeStruct((d, L, 3 * AT_W), BF16),
        scratch_shapes=[pltpu.VMEM(sq, F32)] * 4 + [pltpu.VMEM(sq, BF16)] * 3,
        compiler_params=_params(("parallel", "parallel")),
    )(qkv_d, qkv_d, qkv_d, qkv_d, qkv_d, qkv_d, do_d, do_d, dl_d, dl_d, *_att_bias())


def _att_bwd_combine(dqkvs, cos2, sin2):
    S = dqkvs[0].shape[1]
    tm = MIX_TM

    def body(a_ref, b4_ref, c16_ref, p4_ref, p16_ref, cos_ref, sin_ref, dq_ref, dk_ref, dv_ref):
        t = a_ref[...].astype(F32) + _merge_bf16(b4_ref, p4_ref, 4) + _merge_bf16(c16_ref, p16_ref, 16)
        dy = t[:, : 2 * AT_W]
        cosf = jnp.tile(cos_ref[...], (1, 8))
        sinf = jnp.tile(sin_ref[...], (1, 8))
        dx = dy * cosf - _rope_rot(dy) * sinf
        dq_ref[...] = dx[:, :AT_W].astype(BF16)
        dk_ref[...] = dx[:, AT_W:].astype(BF16)
        dv_ref[...] = t[:, 2 * AT_W:].astype(BF16)

    row = lambda w: pl.BlockSpec((tm, w), lambda s: (s, 0))
    act = jax.ShapeDtypeStruct((S, AT_W), BF16)
    return pl.pallas_call(
        body, name="att_bwd_combine", grid=(S // tm,),
        in_specs=[row(3 * AT_W), _dil_spec(4, tm, 3 * AT_W), _dil_spec(16, tm, 3 * AT_W),
                  _full((PERM_ROWS, PERM_ROWS)), _full((PERM_ROWS, PERM_ROWS)), row(128), row(128)],
        out_specs=[row(AT_W), row(AT_W), row(AT_W)],
        out_shape=[act, act, act],
        compiler_params=_params(("parallel",)),
    )(dqkvs[0].reshape(S, 3 * AT_W), dqkvs[1], dqkvs[2], _perm_matrix(4), _perm_matrix(16), cos2, sin2)


def _outproj(x2, tgt2, mix_at, w_out_full, fnw, o_hg, p1, hg_norm_w):
    S = x2.shape[0]
    tm = PROJ_TM
    ns = S // tm

    def body(x_ref, t_ref, ma_ref, w_ref, fw_ref, o_ref, z_ref, hgw_ref,
             dh_ref, doh_ref, dzh_ref, dma_ref, gw_ref, gfw_ref, ghgw_ref, loss_ref):
        s = pl.program_id(0)

        @pl.when(s == 0)
        def _():
            gw_ref[...] = jnp.zeros_like(gw_ref)
            gfw_ref[...] = jnp.zeros_like(gfw_ref)
            ghgw_ref[...] = jnp.zeros_like(ghgw_ref)
            loss_ref[...] = jnp.zeros_like(loss_ref)

        heads = [slice(128 * hh, 128 * (hh + 1)) for hh in range(HG_W // 128)]
        norm = []
        for sl in heads:
            o, z = o_ref[:, sl], z_ref[:, sl]
            rs = lax.rsqrt(jnp.mean(o * o, axis=-1, keepdims=True) + EPS)
            norm.append((rs, o * rs, _sigmoid(z)))
        mh = jnp.concatenate([(oh * hgw_ref[:, sl] * (z_ref[:, sl] * sz)).astype(BF16)
                              for sl, (_, oh, sz) in zip(heads, norm)], axis=1)
        y = _dot(mh, w_ref[:HG_W, :]) + _dot(ma_ref[...], w_ref[HG_W:, :])
        h = x_ref[...] + y
        r = lax.rsqrt(jnp.mean(h * h, axis=-1, keepdims=True) + EPS)
        hn = h * r
        fw = fw_ref[...]
        err = hn * fw - t_ref[...]
        loss_ref[...] += 0.5 * jnp.sum(jnp.mean(err * err, axis=-1, keepdims=True))
        dout = err * (1.0 / D_MODEL)
        gfw_ref[...] += jnp.sum(dout * hn, axis=0, keepdims=True)
        dhn = dout * fw
        dh = r * (dhn - hn * jnp.mean(dhn * hn, axis=-1, keepdims=True))
        dh_ref[...] = dh
        dhb = dh.astype(BF16)
        dma_ref[...] = _dot_nt(dhb, w_ref[HG_W:, :])
        dmh = _dot_nt(dhb, w_ref[:HG_W, :])
        for sl, (rs, oh, sz) in zip(heads, norm):
            z, dmix, gwv = z_ref[:, sl], dmh[:, sl], hgw_ref[:, sl]
            dzh_ref[:, sl] = (dmix * (oh * gwv) * (sz * (1.0 + z * (1.0 - sz)))).astype(BF16)
            don = dmix * (z * sz)
            ghgw_ref[:, sl] += jnp.sum(don * oh, axis=0, keepdims=True)
            dy = don * gwv
            doh_ref[:, sl] = rs * (dy - oh * jnp.mean(dy * oh, axis=-1, keepdims=True))
        gw_ref[:HG_W, :] += _dot_tn(mh, dhb)
        gw_ref[HG_W:, :] += _dot_tn(ma_ref[...], dhb)

    row = lambda w: pl.BlockSpec((tm, w), lambda s: (s, 0))
    return pl.pallas_call(
        body, name="outproj", grid=(ns,),
        in_specs=[row(D_MODEL), row(D_MODEL), row(AT_W),
                  _full((D_MODEL, D_MODEL)), _full((1, D_MODEL)),
                  row(HG_W), pl.BlockSpec((tm, HG_W), lambda s: (s, 1)), _full((1, HG_W))],
        out_specs=[row(D_MODEL), row(HG_W), row(HG_W), row(AT_W), _full((D_MODEL, D_MODEL)), _full((1, D_MODEL)),
                   _full((1, HG_W)), _full((8, 128))],
        out_shape=[jax.ShapeDtypeStruct((S, D_MODEL), F32), jax.ShapeDtypeStruct((S, HG_W), F32),
                   jax.ShapeDtypeStruct((S, HG_W), BF16), jax.ShapeDtypeStruct((S, AT_W), F32),
                   jax.ShapeDtypeStruct((D_MODEL, D_MODEL), F32), jax.ShapeDtypeStruct((1, D_MODEL), F32),
                   jax.ShapeDtypeStruct((1, HG_W), F32), jax.ShapeDtypeStruct((8, 128), F32)],
        compiler_params=_params(("arbitrary",)),
    )(x2, tgt2, mix_at, w_out_full, fnw, o_hg, p1, hg_norm_w)


def _inproj_bwd_x(dps, w_in_full, x2, norm_w, dh):
    S = x2.shape[0]
    tm = PROJ_TM

    def body(d0, d1, d2, d3, d4, d5, d6, d7, w_ref, x_ref, nw_ref, dh_ref, gx_ref, gnw_ref):
        s = pl.program_id(0)

        @pl.when(s == 0)
        def _():
            gnw_ref[...] = jnp.zeros_like(gnw_ref)

        du = jnp.zeros((tm, D_MODEL), F32)
        for i, dref in enumerate((d0, d1, d2, d3, d4, d5, d6, d7)):
            j, half = divmod(i, 2)
            du = du + _dot_nt(dref[...], w_ref[j, :, 512 * half: 512 * (half + 1)])
        x = x_ref[...]
        r = lax.rsqrt(jnp.mean(x * x, axis=-1, keepdims=True) + EPS)
        xh = x * r
        gnw_ref[...] += jnp.sum(du * xh, axis=0, keepdims=True)
        dun = du * nw_ref[...]
        gx_ref[...] = dh_ref[...] + r * (dun - xh * jnp.mean(dun * xh, axis=-1, keepdims=True))

    row = lambda w: pl.BlockSpec((tm, w), lambda s: (s, 0))
    return pl.pallas_call(
        body, name="inproj_bwd_x", grid=(S // tm,),
        in_specs=[row(512)] * 8 + [_full((4, D_MODEL, 1024)), row(D_MODEL), _full((1, D_MODEL)), row(D_MODEL)],
        out_specs=[row(D_MODEL), _full((1, D_MODEL))],
        out_shape=[jax.ShapeDtypeStruct((S, D_MODEL), F32), jax.ShapeDtypeStruct((1, D_MODEL), F32)],
        compiler_params=_params(("arbitrary",)),
    )(*dps, w_in_full, x2, norm_w, dh)


def _inproj_bwd_w(u, dps):
    S = u.shape[0]
    tm = PROJ_TM

    def body(u_ref, d0, d1, d2, d3, d4, d5, d6, d7, g_ref):
        @pl.when(pl.program_id(0) == 0)
        def _():
            g_ref[...] = jnp.zeros_like(g_ref)

        ub = u_ref[...]
        for i, dref in enumerate((d0, d1, d2, d3, d4, d5, d6, d7)):
            j, half = divmod(i, 2)
            g_ref[j, :, 512 * half: 512 * (half + 1)] += _dot_tn(ub, dref[...])

    return pl.pallas_call(
        body, name="inproj_bwd_w", grid=(S // tm,),
        in_specs=[pl.BlockSpec((tm, D_MODEL), lambda s: (s, 0))] + [pl.BlockSpec((tm, 512), lambda s: (s, 0))] * 8,
        out_specs=_full((4, D_MODEL, 1024)),
        out_shape=jax.ShapeDtypeStruct((4, D_MODEL, 1024), F32),
        compiler_params=_params(("arbitrary",)),
    )(u, *dps)


def _adamw_update(gg, w_ref, m_ref, v_ref, d_ref, nm_ref, nv_ref):
    nm = ADAM_B1 * m_ref[...] + (1.0 - ADAM_B1) * gg
    nv = ADAM_B2 * v_ref[...] + (1.0 - ADAM_B2) * (gg * gg)
    m_hat = nm / (1.0 - ADAM_B1 ** ADAM_STEP)
    v_hat = nv / (1.0 - ADAM_B2 ** ADAM_STEP)
    d_ref[...] = -ADAM_LR * (m_hat / (jnp.sqrt(v_hat) + ADAM_EPS) + ADAM_WD * w_ref[...])
    nm_ref[...] = nm
    nv_ref[...] = nv


def _adamw(w, g, m, v, name):
    rows, cols = w.shape
    tr = min(rows, 256)

    def body(w_ref, g_ref, m_ref, v_ref, d_ref, nm_ref, nv_ref):
        _adamw_update(g_ref[...], w_ref, m_ref, v_ref, d_ref, nm_ref, nv_ref)

    spec = pl.BlockSpec((tr, cols), lambda i: (i, 0))
    sds = jax.ShapeDtypeStruct((rows, cols), F32)
    return pl.pallas_call(
        body, name=name, grid=(rows // tr,),
        in_specs=[spec] * 4, out_specs=[spec] * 3, out_shape=[sds] * 3,
        compiler_params=_params(("parallel",)),
    )(w, g, m, v)


def _place():
    return lax.axis_index("x"), lax.axis_index("y"), lax.axis_index("c")


def _gather_weights(w_in_s, w_out_s):
    def body(win_ref, wout_ref, fin_ref, fout_ref, send_sems, recv_sems):
        x, y, c = _place()
        me = (x, y, c)
        sib = (x, y, 1 - c)
        mine = 2 * x + y
        fin_ref[mine] = win_ref[...].astype(BF16)
        fout_ref[mine] = wout_ref[...].astype(BF16)
        chips = [(1 - x, y), (x, 1 - y), (1 - x, 1 - y)]

        def halves(chip, half):
            return (fin_ref.at[chip, pl.ds(half * 512, 512), :], fout_ref.at[chip, pl.ds(half * 128, 128), :])

        def copy(k, ref, to):
            return pltpu.make_async_remote_copy(src_ref=ref, dst_ref=ref, send_sem=send_sems.at[k],
                                                recv_sem=recv_sems.at[k], device_id=to, device_id_type=MESH)

        first, passed = [], []
        for j, (cx, cy) in enumerate(chips):
            for a, ref in enumerate(halves(mine, c)):
                first.append(copy(2 * j + a, ref, (cx, cy, c)))
        for cp in first:
            cp.start()
        for j, (cx, cy) in enumerate(chips):
            for a, ref in enumerate(halves(2 * cx + cy, c)):
                copy(2 * j + a, ref, me).wait_recv()
                fwd = copy(6 + 2 * j + a, ref, sib)
                fwd.start()
                passed.append(fwd)
        for j, (cx, cy) in enumerate(chips):
            for a, ref in enumerate(halves(2 * cx + cy, 1 - c)):
                copy(6 + 2 * j + a, ref, me).wait_recv()
        for cp in first + passed:
            cp.wait_send()

    vm = pl.BlockSpec(memory_space=pltpu.VMEM)
    return pl.pallas_call(
        body, name="gather_weights",
        in_specs=[vm, vm], out_specs=[vm, vm],
        out_shape=[jax.ShapeDtypeStruct((4, D_MODEL, 1024), BF16), jax.ShapeDtypeStruct((4, 256, D_MODEL), BF16)],
        scratch_shapes=[pltpu.SemaphoreType.DMA((12,)), pltpu.SemaphoreType.DMA((12,))],
        compiler_params=pltpu.CompilerParams(vmem_limit_bytes=VMEM_LIMIT),
    )(w_in_s, w_out_s)


def _swap_halves(g_in, g_out):
    def body(gin_ref, gout_ref, rin_ref, rout_ref, send_sems, recv_sems):
        x, y, c = _place()
        sib = (x, y, 1 - c)
        cps = [pltpu.make_async_remote_copy(src_ref=src.at[:, 1 - c], dst_ref=dst, send_sem=send_sems.at[k],
                                            recv_sem=recv_sems.at[k], device_id=sib, device_id_type=MESH)
               for k, (src, dst) in enumerate(((gin_ref, rin_ref), (gout_ref, rout_ref)))]
        for cp in cps:
            cp.start()
        for cp in cps:
            cp.wait()

    hbm = pl.BlockSpec(memory_space=pl.ANY)
    return pl.pallas_call(
        body, name="swap_halves",
        in_specs=[hbm, hbm], out_specs=[hbm, hbm],
        out_shape=[jax.ShapeDtypeStruct((4,) + g.shape[2:], F32) for g in (g_in, g_out)],
        scratch_shapes=[pltpu.SemaphoreType.DMA((2,)), pltpu.SemaphoreType.DMA((2,))],
    )(g_in, g_out)


def _add_half(g, r, cidx, name):
    n, _, rows, cols = g.shape

    def body(c_ref, g_ref, r_ref, o_ref):
        o_ref[0] = (g_ref[0, 0] + r_ref[0]).astype(BF16)

    return pl.pallas_call(
        body, name=name,
        grid_spec=pltpu.PrefetchScalarGridSpec(
            num_scalar_prefetch=1, grid=(n,),
            in_specs=[pl.BlockSpec((1, 1, rows, cols), lambda j, c_ref: (j, c_ref[0], 0, 0)),
                      pl.BlockSpec((1, rows, cols), lambda j, c_ref: (j, 0, 0))],
            out_specs=pl.BlockSpec((1, rows, cols), lambda j, c_ref: (j, 0, 0))),
        out_shape=jax.ShapeDtypeStruct((n, rows, cols), BF16),
        compiler_params=_params(("parallel",)),
    )(cidx, g, r)


def _exchange_copies(in_ref, out_ref, lin_ref, lout_ref, send_sems, recv_sems):
    x, y, c = _place()
    cps = []
    for k, (cx, cy) in enumerate([(1 - x, y), (x, 1 - y), (1 - x, 1 - y)]):
        for a, (src, dst) in enumerate(((in_ref, lin_ref), (out_ref, lout_ref))):
            cps.append(pltpu.make_async_remote_copy(
                src_ref=src.at[2 * cx + cy], dst_ref=dst.at[k], send_sem=send_sems.at[2 * k + a],
                recv_sem=recv_sems.at[2 * k + a], device_id=(cx, cy, c), device_id_type=MESH))
    return cps


def _exchange_start(cs_in, cs_out):
    def body(in_ref, out_ref, lin_ref, lout_ref, send_sems, recv_sems, in_thru, out_thru, lin_thru, lout_thru, token):
        for cp in _exchange_copies(in_ref, out_ref, lin_ref, lout_ref, send_sems, recv_sems):
            cp.start()
        token[...] = jnp.zeros_like(token)

    lands = [lax.empty((3,) + a.shape[1:], a.dtype) for a in (cs_in, cs_out)]
    bufs = [pltpu.with_memory_space_constraint(a, pltpu.HBM) for a in (cs_in, cs_out, *lands)]
    hbm = pl.BlockSpec(memory_space=pltpu.HBM)
    sem = pl.BlockSpec(memory_space=pltpu.SEMAPHORE)
    return pl.pallas_call(
        body, name="exchange_start",
        in_specs=[hbm] * 4,
        out_specs=[sem, sem, hbm, hbm, hbm, hbm, pl.BlockSpec(memory_space=pltpu.VMEM)],
        out_shape=[pltpu.SemaphoreType.DMA((6,)), pltpu.SemaphoreType.DMA((6,))]
                  + [pltpu.HBM(b.shape, b.dtype) for b in bufs] + [jax.ShapeDtypeStruct((8, 128), F32)],
        input_output_aliases={0: 2, 1: 3, 2: 4, 3: 5},
        compiler_params=pltpu.CompilerParams(has_side_effects=pltpu.SideEffectType.DATAFLOW_SIDE_EFFECTING),
    )(*bufs)


def _exchange_wait(send_sems, recv_sems, in_thru, out_thru, lin_thru, lout_thru, after):
    def body(in_ref, out_ref, lin_ref, lout_ref, send_sems, recv_sems, after_ref, in_dead, out_dead, got_in, got_out):
        for cp in _exchange_copies(in_ref, out_ref, lin_ref, lout_ref, send_sems, recv_sems):
            cp.wait_send()
            cp.wait_recv()

    hbm = pl.BlockSpec(memory_space=pltpu.HBM)
    sem = pl.BlockSpec(memory_space=pltpu.SEMAPHORE)
    bufs = (in_thru, out_thru, lin_thru, lout_thru)
    return pl.pallas_call(
        body, name="exchange_wait",
        in_specs=[hbm] * 4 + [sem, sem, pl.BlockSpec(memory_space=pl.ANY)],
        out_specs=[hbm] * 4,
        out_shape=[pltpu.HBM(b.shape, b.dtype) for b in bufs],
        input_output_aliases={0: 0, 1: 1, 2: 2, 3: 3},
        compiler_params=pltpu.CompilerParams(has_side_effects=pltpu.SideEffectType.DATAFLOW_SIDE_EFFECTING),
    )(*bufs, send_sems, recv_sems, after)


PEER_XOR = (2, 1, 3)


def _sum_chips(cs, r, chip_idx, name):
    _, rows, cols = r.shape
    tr = min(rows, 256)

    def body(m_ref, cs_ref, r_ref, o_ref):
        mine = m_ref[0]
        own = cs_ref[0].astype(F32)
        got = [r_ref[k].astype(F32) for k in range(3)]
        acc = None
        for s in range(4):
            rel = mine ^ s
            term = jnp.where(rel == 0, own, jnp.where(rel == PEER_XOR[0], got[0],
                                                      jnp.where(rel == PEER_XOR[1], got[1], got[2])))
            acc = term if acc is None else acc + term
        o_ref[...] = acc

    return pl.pallas_call(
        body, name=name,
        grid_spec=pltpu.PrefetchScalarGridSpec(
            num_scalar_prefetch=1, grid=(rows // tr,),
            in_specs=[pl.BlockSpec((1, tr, cols), lambda i, m_ref: (m_ref[0], i, 0)),
                      pl.BlockSpec((3, tr, cols), lambda i, m_ref: (0, i, 0))],
            out_specs=pl.BlockSpec((tr, cols), lambda i, m_ref: (i, 0))),
        out_shape=jax.ShapeDtypeStruct((rows, cols), F32),
        compiler_params=_params(("parallel",)),
    )(chip_idx, cs, r)


def _swap_reduced(h_in, h_out):
    def body(in_ref, out_ref, rin_ref, rout_ref, send_sems, recv_sems):
        x, y, c = _place()
        cps = [pltpu.make_async_remote_copy(src_ref=src, dst_ref=dst, send_sem=send_sems.at[k],
                                            recv_sem=recv_sems.at[k], device_id=(x, y, 1 - c), device_id_type=MESH)
               for k, (src, dst) in enumerate(((in_ref, rin_ref), (out_ref, rout_ref)))]
        for cp in cps:
            cp.start()
        for cp in cps:
            cp.wait()

    hbm = pl.BlockSpec(memory_space=pl.ANY)
    return pl.pallas_call(
        body, name="swap_reduced",
        in_specs=[hbm, hbm], out_specs=[hbm, hbm],
        out_shape=[jax.ShapeDtypeStruct(h.shape, F32) for h in (h_in, h_out)],
        scratch_shapes=[pltpu.SemaphoreType.DMA((2,)), pltpu.SemaphoreType.DMA((2,))],
    )(h_in, h_out)


def _adamw_halves(w, mine, theirs, m, v, cidx, name):
    rows, cols = w.shape
    half = rows // 2
    tr = min(half, 256)
    nbh = half // tr

    def body(c_ref, w_ref, a_ref, b_ref, m_ref, v_ref, g_ref, d_ref, nm_ref, nv_ref):
        gg = jnp.where(pl.program_id(0) // nbh == c_ref[0], a_ref[...], b_ref[...])
        g_ref[...] = gg
        _adamw_update(gg, w_ref, m_ref, v_ref, d_ref, nm_ref, nv_ref)

    spec = pl.BlockSpec((tr, cols), lambda i, c_ref: (i, 0))
    hspec = pl.BlockSpec((tr, cols), lambda i, c_ref: (i % nbh, 0))
    sds = jax.ShapeDtypeStruct((rows, cols), F32)
    return pl.pallas_call(
        body, name=name,
        grid_spec=pltpu.PrefetchScalarGridSpec(
            num_scalar_prefetch=1, grid=(rows // tr,),
            in_specs=[spec, hspec, hspec, spec, spec], out_specs=[spec] * 4),
        out_shape=[sds] * 4,
        compiler_params=_params(("parallel",)),
    )(cidx, w, mine, theirs, m, v)


def _allreduce_small(g_nw, g_fw, g_hgw, g_lbl, loss8):
    def body(nw_ref, fw_ref, hgw_ref, lbl_ref, loss_ref, out_ref, slots, send_sems, recv_sems):
        x, y, c = _place()
        me = 4 * x + 2 * y + c
        slots[me] = jnp.zeros((8, D_MODEL), F32)
        slots[me, 0:1, :] = nw_ref[...]
        slots[me, 1:2, :] = fw_ref[...]
        slots[me, 2:3, 0:HG_W] = hgw_ref[...]
        slots[me, 3:4, 0:HG_W] = lbl_ref[0:1, :]
        slots[me, 3:4, HG_W:] = lbl_ref[1:2, :]
        slots[me, 4:5, 0:128] = loss_ref[0:1, :]
        cps = []
        for k in range(1, 8):
            dx, dy, dc = (k >> 2) & 1, (k >> 1) & 1, k & 1
            to = (x ^ dx, y ^ dy, c ^ dc)
            cps.append(pltpu.make_async_remote_copy(
                src_ref=slots.at[me], dst_ref=slots.at[me], send_sem=send_sems.at[k - 1],
                recv_sem=recv_sems.at[k - 1], device_id=to, device_id_type=MESH))
        for cp in cps:
            cp.start()
        for cp in cps:
            cp.wait()
        acc = slots[0]
        for i in range(1, 8):
            acc = acc + slots[i]
        out_ref[...] = acc

    vm = pl.BlockSpec(memory_space=pltpu.VMEM)
    return pl.pallas_call(
        body, name="allreduce_small",
        in_specs=[vm] * 5, out_specs=vm,
        out_shape=jax.ShapeDtypeStruct((8, D_MODEL), F32),
        scratch_shapes=[pltpu.VMEM((8, 8, D_MODEL), F32), pltpu.SemaphoreType.DMA((7,)), pltpu.SemaphoreType.DMA((7,))],
    )(g_nw, g_fw, g_hgw, g_lbl, loss8)


def _rope_tables(S):
    inv_freq = (np.float32(1.0) / np.power(np.float32(ROPE_THETA), np.arange(ROPE_HALF, dtype=np.float32) / np.float32(ROPE_HALF))).astype(np.float32)
    ang = (np.arange(S, dtype=np.float32)[:, None] * inv_freq[None, :]).astype(np.float32)
    cos, sin = np.cos(ang).astype(np.float32), np.sin(ang).astype(np.float32)
    cos2 = np.concatenate([cos, cos, cos, cos], axis=-1)
    sin2 = np.concatenate([-sin, sin, -sin, sin], axis=-1)
    return jnp.asarray(cos2), jnp.asarray(sin2)


def _local_step(x2, tgt2, norm_w, w_in_full, lbl, hg_norm_w, w_out4, fnw):
    S = x2.shape[0]
    cos2, sin2 = _rope_tables(S)
    w_out_full = w_out4.reshape(D_MODEL, D_MODEL)
    p0, p1, qkv, qkv4, qkv16, p3, u = _inproj(x2, norm_w, w_in_full, cos2, sin2)
    o_hg, states = _hg_fwd(p0, p1, lbl)
    qkv_ds = [qkv.reshape(1, S, 3 * AT_W), qkv4, qkv16]
    os_d, ls_d = zip(*[_att_fwd(q) for q in qkv_ds])
    o_at, lse, mix_at = _att_combine(os_d, ls_d, p3)
    dh, do_hg, dz_hg, dm_at, g_wout, g_fw, g_hgw, loss8 = _outproj(x2, tgt2, mix_at, w_out_full, fnw, o_hg, p1, hg_norm_w)
    dqr, dfl, dv_hg, g_lbl = _hg_bwd(p0, p1, do_hg, states, lbl)
    do1, do4, do16, dl1, dl4, dl16, dz_at = _att_gate_bwd(dm_at, o_at, lse, p3)
    do_ds = [do1.reshape(1, S, AT_W), do4, do16]
    dl_ds = [dl1.reshape(1, S, STAT_W), dl4, dl16]
    dqkvs = [_att_bwd(q, a, b) for q, a, b in zip(qkv_ds, do_ds, dl_ds)]
    dq_at, dk_at, dv_at = _att_bwd_combine(dqkvs, cos2, sin2)
    dps = [dqr, dfl, dv_hg, dz_hg, dq_at, dk_at, dv_at, dz_at]
    return loss8, dps, u, dh, g_lbl, g_hgw, g_wout, g_fw


def kernel(x, norm_w, w_in, hgrn_lb_logits, hg_norm_w, w_out, final_norm_w, loss_target, m_norm_w, m_w_in, m_hgrn_lb_logits, m_hg_norm_w, m_w_out, m_final_norm_w, v_norm_w, v_w_in, v_hgrn_lb_logits, v_hg_norm_w, v_w_out, v_final_norm_w):
    S = x.shape[1]
    w_in_full, w_out_full = _gather_weights(w_in[0], w_out[0])
    loss8, dps, u, dh, g_lbl, g_hgw, g_wout, g_fw = _local_step(
        x[0], loss_target[0], norm_w, w_in_full, hgrn_lb_logits, hg_norm_w,
        w_out_full, final_norm_w.reshape(1, D_MODEL))

    cidx = lax.axis_index("c").astype(jnp.int32).reshape(1)
    g_win = _inproj_bwd_w(u, dps)
    g_in4 = g_win.reshape(4, 2, 512, 1024)
    g_out4 = g_wout.reshape(4, 2, 128, D_MODEL)
    r_in, r_out = _swap_halves(g_in4, g_out4)
    cs_in = _add_half(g_in4, r_in, cidx, "add_half_in")
    cs_out = _add_half(g_out4, r_out, cidx, "add_half_out")
    *started, token = _exchange_start(cs_in, cs_out)
    grad_x, g_nw = _inproj_bwd_x(dps, w_in_full, x[0], norm_w + token[0:1, 0:1], dh)
    cs_in, cs_out, x_in, x_out = _exchange_wait(*started, g_nw)
    chip_idx = (2 * lax.axis_index("x") + lax.axis_index("y")).astype(jnp.int32).reshape(1)
    h_in = _sum_chips(cs_in, x_in, chip_idx, "sum_chips_in")
    h_out = _sum_chips(cs_out, x_out, chip_idx, "sum_chips_out")
    s_in, s_out = _swap_reduced(h_in, h_out)

    red = _allreduce_small(g_nw, g_fw, g_hgw, g_lbl, loss8)
    loss = red[4, 0]
    grad_norm_w = red[0:1, :]
    grad_final_norm_w = red[1, :]
    grad_hg_norm_w = red[2:3, :HG_W]
    grad_lbl = jnp.concatenate([red[3:4, :HG_W], red[3:4, HG_W:]], axis=0)

    d_nw, m_nw, v_nw = _adamw(norm_w, grad_norm_w, m_norm_w, v_norm_w, "adamw_norm_w")
    grad_w_in, d_win, m_win, v_win = _adamw_halves(w_in[0], h_in, s_in, m_w_in[0], v_w_in[0], cidx, "adamw_w_in")
    d_lbl, m_lbl, v_lbl = _adamw(hgrn_lb_logits, grad_lbl, m_hgrn_lb_logits, v_hgrn_lb_logits, "adamw_lb_logits")
    d_hgw, m_hgw, v_hgw = _adamw(hg_norm_w, grad_hg_norm_w, m_hg_norm_w, v_hg_norm_w, "adamw_hg_norm_w")
    grad_w_out, d_wout, m_wout, v_wout = _adamw_halves(w_out[0], h_out, s_out, m_w_out[0], v_w_out[0], cidx, "adamw_w_out")
    d_fw, m_fw, v_fw = _adamw(final_norm_w.reshape(1, D_MODEL), grad_final_norm_w.reshape(1, D_MODEL),
                              m_final_norm_w.reshape(1, D_MODEL), v_final_norm_w.reshape(1, D_MODEL), "adamw_final_norm_w")
    e1 = lambda a: a[None]
    flat = lambda a: a.reshape(D_MODEL)
    return (loss, grad_x[None], grad_norm_w, e1(grad_w_in), grad_lbl, grad_hg_norm_w, e1(grad_w_out), grad_final_norm_w,
            d_nw, e1(d_win), d_lbl, d_hgw, e1(d_wout), flat(d_fw),
            m_nw, e1(m_win), m_lbl, m_hgw, e1(m_wout), flat(m_fw),
            v_nw, e1(v_win), v_lbl, v_hgw, e1(v_wout), flat(v_fw))
```

```python
import jax
import jax.numpy as jnp
import numpy as np
from jax import lax
from jax.experimental import pallas as pl
from jax.experimental.pallas import tpu as pltpu

F32 = jnp.float32
BF16 = jnp.bfloat16
MESH = pl.DeviceIdType.MESH

D_MODEL = 1024
HG_HEADS = 4
HG_HEAD = 128
HG_W = HG_HEADS * HG_HEAD
AT_HEAD = 64
AT_W = 512
HEAD_PAIR = 2 * AT_HEAD
ROPE_HALF = 32
ROPE_THETA = 10000.0
EPS = 1e-6
CHUNK = 128
LEVELS = (64, 32, 16, 8, 4, 2, 1)
DIAG = 1
SUBLANES = 8
ATT_BLK = 128
ATT_QB = 4
ATT_SCALE = 0.125
STAT_W = 128
STAT_LANES = 16
STAT_LSE_LANE = 8
NEG = -1e30
VMEM_LIMIT = 56 * 1024 * 1024
MIX_TM = 1024
PROJ_TM = 512

ADAM_LR = 0.001
ADAM_B1 = 0.9
ADAM_B2 = 0.999
ADAM_EPS = 1e-08
ADAM_WD = 0.01
ADAM_STEP = 10


def _iota(shape, dim):
    return lax.broadcasted_iota(jnp.int32, shape, dim)


def _dot(a, b):
    return jnp.dot(a, b, preferred_element_type=F32)


def _dot_nt(a, b):
    return lax.dot_general(a, b, (((1,), (1,)), ((), ())), preferred_element_type=F32)


def _dot_tn(a, b):
    return lax.dot_general(a, b, (((0,), (0,)), ((), ())), preferred_element_type=F32)


def _sigmoid(v):
    return 0.5 * jnp.tanh(0.5 * v) + 0.5


def _params(sem=None, vmem=VMEM_LIMIT):
    return pltpu.CompilerParams(dimension_semantics=sem, vmem_limit_bytes=vmem)


def _full(shape):
    n = len(shape)
    return pl.BlockSpec(shape, lambda *_: (0,) * n)


def _rope_rot(y):
    n = y.shape[1]
    first = (_iota(y.shape, 1) & (2 * ROPE_HALF - 1)) < ROPE_HALF
    return jnp.where(first, pltpu.roll(y, n - ROPE_HALF, 1), pltpu.roll(y, ROPE_HALF, 1))


def _dil_spec(d, tm, width):
    return pl.BlockSpec((d, tm // d, width), lambda s: (0, s, 0))


LANES = 128


def _slab_scratch(tm, width):
    return pltpu.VMEM((width // LANES, tm, LANES), F32)


def _to_slabs(v, slabs_ref):
    for j in range(slabs_ref.shape[0]):
        slabs_ref[j] = v[:, LANES * j: LANES * (j + 1)]


def _from_slabs(slabs_ref):
    return jnp.concatenate([slabs_ref[j] for j in range(slabs_ref.shape[0])], axis=1)


def _split_residues(slabs_ref, dst_ref, d, dtype):
    nslab, tm, _ = slabs_ref.shape
    for r in range(d):
        for j in range(nslab):
            dst_ref[r, :, LANES * j: LANES * (j + 1)] = slabs_ref[j, pl.ds(r, tm // d, stride=d), :].astype(dtype)


def _merge_residues(src_ref, slabs_ref, d):
    nslab, tm, _ = slabs_ref.shape
    for r in range(d):
        for j in range(nslab):
            slabs_ref[j, pl.ds(r, tm // d, stride=d), :] = src_ref[r, :, LANES * j: LANES * (j + 1)].astype(F32)


PERM_ROWS = 256


def _perm_matrix(d):
    n = PERM_ROWS // d
    t = np.arange(PERM_ROWS)
    p = np.zeros((PERM_ROWS, PERM_ROWS), np.float32)
    p[t, (t % d) * n + t // d] = 1.0
    return jnp.asarray(p, BF16)


def _merge_bf16(src_ref, perm_ref, d):
    n = PERM_ROWS // d
    tm = src_ref.shape[1] * d
    outs = []
    for b in range(tm // PERM_ROWS):
        flat = jnp.concatenate([src_ref[r, n * b: n * (b + 1), :] for r in range(d)], axis=0)
        outs.append(_dot(perm_ref[...], flat))
    return jnp.concatenate(outs, axis=0)


def _inproj(x2, norm_w, w_in_full, cos2, sin2):
    S = x2.shape[0]
    tm = PROJ_TM

    def body(x_ref, nw_ref, w_ref, cos_ref, sin_ref, p0_ref, p1_ref, qkv_ref, qkv4_ref, qkv16_ref, z_ref, u_ref, scr):
        x = x_ref[...]
        r = lax.rsqrt(jnp.mean(x * x, axis=-1, keepdims=True) + EPS)
        u = x * r * nw_ref[...]
        ub = u.astype(BF16)
        u_ref[...] = ub
        p0_ref[...] = _dot(ub, w_ref[0])
        p1_ref[...] = _dot(ub, w_ref[1])
        y2 = _dot(ub, w_ref[2])
        cosf = jnp.tile(cos_ref[...], (1, 8))
        sinf = jnp.tile(sin_ref[...], (1, 8))
        y3 = _dot(ub, w_ref[3])
        z_ref[...] = y3[:, AT_W:]
        qkv = jnp.concatenate([y2 * cosf + _rope_rot(y2) * sinf, y3[:, :AT_W]], axis=1)
        qkv_ref[...] = qkv.astype(BF16)
        _to_slabs(qkv, scr)
        _split_residues(scr, qkv4_ref, 4, BF16)
        _split_residues(scr, qkv16_ref, 16, BF16)

    row = lambda w: pl.BlockSpec((tm, w), lambda s: (s, 0))
    qkv_w = 3 * AT_W
    return pl.pallas_call(
        body, name="inproj", grid=(S // tm,),
        in_specs=[row(D_MODEL), _full((1, D_MODEL)), _full((4, D_MODEL, 1024)), row(128), row(128)],
        out_specs=[row(1024), row(1024), row(qkv_w), _dil_spec(4, tm, qkv_w), _dil_spec(16, tm, qkv_w), row(AT_W),
                   row(D_MODEL)],
        out_shape=[jax.ShapeDtypeStruct((S, 1024), F32), jax.ShapeDtypeStruct((S, 1024), F32),
                   jax.ShapeDtypeStruct((S, qkv_w), BF16), jax.ShapeDtypeStruct((4, S // 4, qkv_w), BF16),
                   jax.ShapeDtypeStruct((16, S // 16, qkv_w), BF16), jax.ShapeDtypeStruct((S, AT_W), F32),
                   jax.ShapeDtypeStruct((S, D_MODEL), BF16)],
        scratch_shapes=[_slab_scratch(tm, qkv_w)],
        compiler_params=_params(("parallel",)),
    )(x2, norm_w, w_in_full, cos2, sin2)


HG_HPS = 4
N_LEV = len(LEVELS)


def _hg_const_arrays():
    r = np.arange(CHUNK)[:, None]
    c = np.arange(CHUNK)[None, :]
    tris = np.stack([r >= c, r <= c])
    lm = [((r // (2 * m)) == (c // (2 * m))) & (r % (2 * m) >= m) & (c % (2 * m) < m) for m in LEVELS]
    dm = [(c == r - dl) & (r % DIAG >= dl) for dl in range(DIAG)]
    masks = np.stack(lm + dm)
    return jnp.asarray(tris, BF16), jnp.asarray(masks, F32)


def _split2(a):
    hi = a.astype(BF16)
    return hi, (a - hi.astype(F32)).astype(BF16)


def _dot3(a, b, dot=_dot):
    ah, al = _split2(a)
    bh, bl = _split2(b)
    n = b.shape[1]
    p = dot(ah, jnp.concatenate([bh, bl], axis=1))
    return (p[:, :n] + p[:, n:]) + dot(al, bh)


def _split3(a):
    a1 = a.astype(BF16)
    r1 = a - a1.astype(F32)
    a2 = r1.astype(BF16)
    return a1, a2, (r1 - a2.astype(F32)).astype(BF16)


def _tri_dot(tri, a):
    n = a.shape[1]
    p = _dot(tri, jnp.concatenate(_split3(a), axis=1))
    return (p[:, :n] + p[:, n:2 * n]) + p[:, 2 * n:]


def _dot_sel(a, sel):
    a1, a2, a3 = _split3(a)
    return (_dot(a1, sel) + _dot(a2, sel)) + _dot(a3, sel)


def _rowsum(t):
    return _dot(t.astype(BF16), jnp.ones((t.shape[1], t.shape[1]), BF16))


def _level_refs(b):
    refs = []
    pos = _iota(b.shape, 0)
    for m in LEVELS:
        if 2 * m >= SUBLANES:
            parts = [jnp.broadcast_to(b[r0 + m - 1: r0 + m, :], (2 * m, b.shape[1])) for r0 in range(0, CHUNK, 2 * m)]
            refs.append(parts[0] if len(parts) == 1 else jnp.concatenate(parts, axis=0))
        else:
            p = pos & (2 * m - 1)
            ref = b
            for off in range(-(m - 1), m + 1):
                if off != 0:
                    ref = jnp.where(p == m - 1 + off, pltpu.roll(b, off % CHUNK, 0), ref)
            refs.append(ref)
    return refs


def _hg_lb(lbl_ref):
    l0 = lbl_ref[0:1, :]
    l1 = lbl_ref[1:2, :]
    mx = jnp.maximum(l0, l1)
    e0 = jnp.exp(l0 - mx)
    e1 = jnp.exp(l1 - mx)
    p0 = e0 / (e0 + e1)
    lb = jnp.clip(p0, 1e-6, 1.0 - 1e-6)
    inside = (p0 >= 1e-6) & (p0 <= 1.0 - 1e-6)
    dlb_dl0 = jnp.where(inside, p0 * (e1 / (e0 + e1)), 0.0)
    return lb, dlb_dl0


def _hg_gates(qr, fl, lb):
    sig = _sigmoid(fl)
    f = lb + (1.0 - lb) * sig
    g = jnp.log(f)
    k = (1.0 - lb) * (1.0 - sig)
    sq = _sigmoid(qr)
    q = qr * sq
    return sig, f, g, k, sq, q


def _hg_levels(q, k, b, mk_ref):
    refs = _level_refs(b)
    a = jnp.zeros((CHUNK, CHUNK), F32)
    es, qts, kts = [], [], []
    for i in range(N_LEV):
        diff = b - refs[i]
        e = jnp.exp(jnp.minimum(diff, -diff))
        qt = (q * e).astype(BF16)
        kt = (k * e).astype(BF16)
        a = a + _dot_nt(qt, kt) * mk_ref[i]
        es.append(e); qts.append(qt); kts.append(kt)
    return a, es, qts, kts


def _hg_specs(nc, rev):
    cc = (lambda c: nc - 1 - c) if rev else (lambda c: c)
    w = HG_HEAD * HG_HPS
    blk = lambda off: pl.BlockSpec((CHUNK, w), lambda h, c: (cc(c), h + off))
    lb2 = pl.BlockSpec((2, w), lambda h, c: (0, h))
    st = pl.BlockSpec((1, HG_HPS, HG_HEAD, HG_HEAD), lambda h, c: (cc(c), h, 0, 0))
    consts = [_full((2, CHUNK, CHUNK)), _full((N_LEV + DIAG, CHUNK, CHUNK))]
    return blk, lb2, st, consts


def _hg_fwd(p0, p1, lbl):
    S = p0.shape[0]
    nc = S // CHUNK
    ng = HG_HEADS // HG_HPS

    def body(qr_ref, fl_ref, v_ref, lbl_ref, tri_ref, mk_ref, o_ref, st_ref, state):
        c = pl.program_id(1)

        @pl.when(c == 0)
        def _():
            state[...] = jnp.zeros_like(state)

        lb_all, _ = _hg_lb(lbl_ref)
        heads = [slice(HG_HEAD * hh, HG_HEAD * (hh + 1)) for hh in range(HG_HPS)]
        qs, ks, bs, mats = [], [], [], []
        for sl in heads:
            _, _, g, k, _, q = _hg_gates(qr_ref[:, sl], fl_ref[:, sl], lb_all[:, sl])
            qs.append(q); ks.append(k); bs.append(_tri_dot(tri_ref[0], g))
        for hh in range(HG_HPS):
            a, _, _, _ = _hg_levels(qs[hh], ks[hh], bs[hh], mk_ref)
            mats.append(a + _rowsum(qs[hh] * ks[hh]) * mk_ref[N_LEV])
        for hh, sl in enumerate(heads):
            q, k, b, a, v = qs[hh], ks[hh], bs[hh], mats[hh], v_ref[:, sl]
            b_last = b[CHUNK - 1: CHUNK, :]
            st = state[hh]
            st_ref[0, hh] = st
            o = _dot_nt((q * jnp.exp(b)).astype(BF16), st.astype(BF16)) + _dot(a.astype(BF16), v.astype(BF16))
            state[hh] = st * jnp.exp(b_last) + _dot3(v, k * jnp.exp(b_last - b), _dot_tn)
            o_ref[:, sl] = o

    blk, lb2, st_spec, consts = _hg_specs(nc, False)
    tris, masks = _hg_const_arrays()
    return pl.pallas_call(
        body, name="hg_fwd", grid=(ng, nc),
        in_specs=[blk(0), blk(ng), blk(0), lb2] + consts,
        out_specs=[blk(0), st_spec],
        out_shape=[jax.ShapeDtypeStruct((S, HG_W), F32), jax.ShapeDtypeStruct((nc, HG_HEADS, HG_HEAD, HG_HEAD), F32)],
        scratch_shapes=[pltpu.VMEM((HG_HPS, HG_HEAD, HG_HEAD), F32)],
        compiler_params=_params(("parallel", "arbitrary")),
    )(p0, p0, p1, lbl, tris, masks)


def _hg_bwd(p0, p1, do_hg, states, lbl):
    S = p0.shape[0]
    nc = S // CHUNK
    ng = HG_HEADS // HG_HPS
    w = HG_HEAD * HG_HPS

    def body(qr_ref, fl_ref, v_ref, do_ref, st_ref, lbl_ref, tri_ref, mk_ref, mkb_ref,
             dqr_ref, dfl_ref, dv_ref, glbl_ref, dstate, carry, acc_lb):
        c = pl.program_id(1)

        @pl.when(c == 0)
        def _():
            dstate[...] = jnp.zeros_like(dstate)
            carry[...] = jnp.zeros_like(carry)
            acc_lb[...] = jnp.zeros_like(acc_lb)

        lb_all, dlb_dl0 = _hg_lb(lbl_ref)
        heads = [slice(HG_HEAD * hh, HG_HEAD * (hh + 1)) for hh in range(HG_HPS)]
        diag_mask = mk_ref[N_LEV]
        gates, bs, dos = [], [], []
        for sl in heads:
            gt = _hg_gates(qr_ref[:, sl], fl_ref[:, sl], lb_all[:, sl])
            gates.append(gt)
            bs.append(_tri_dot(tri_ref[0], gt[2]))
        for sl in heads:
            dos.append(do_ref[:, sl])
        inter = []
        for hh, sl in enumerate(heads):
            _, _, _, k, _, q = gates[hh]
            b, do, v = bs[hh], dos[hh], v_ref[:, sl]
            b_last = b[CHUNK - 1: CHUNK, :]
            eb = jnp.exp(b)
            edec = jnp.exp(b_last - b)
            dst = dstate[hh]
            dq = _dot3(do, st_ref[0, hh]) * eb
            dk = _dot3(v, dst) * edec
            da = _dot_nt(do.astype(BF16), v.astype(BF16))
            dv_state = _dot_nt((k * edec).astype(BF16), dst.astype(BF16))
            dstate[hh] = dst * jnp.exp(b_last) + _dot3(do, q * eb, _dot_tn)
            inter.append((dq, dk, da, dv_state))
        for hh, sl in enumerate(heads):
            sig, f, _, k, sq, q = gates[hh]
            dq, dk, da, dv_state = inter[hh]
            b = bs[hh]
            db = q * dq - k * dk
            dab = da.astype(BF16)
            refs = _level_refs(b)
            a = _rowsum(q * k) * diag_mask
            for i in range(N_LEV):
                diff = b - refs[i]
                e = jnp.exp(jnp.minimum(diff, -diff))
                qt = (q * e).astype(BF16)
                kt = (k * e).astype(BF16)
                a = a + _dot_nt(qt, kt) * mk_ref[i]
                dam = dab * mkb_ref[i]
                gq = _dot(dam, kt)
                gk = _dot_tn(dam, qt)
                dq = dq + e * gq
                dk = dk + e * gk
                db = db + (qt.astype(F32) * gq - kt.astype(F32) * gk)
            dc = _rowsum(da * diag_mask)
            dq = dq + dc * k
            dk = dk + dc * q
            dv_ref[:, sl] = (_dot_tn(a.astype(BF16), dos[hh].astype(BF16)) + dv_state).astype(BF16)
            dg = _tri_dot(tri_ref[1], db) + carry[0:1, sl]
            carry[0:1, sl] += jnp.sum(db, axis=0, keepdims=True)
            lb = lb_all[:, sl]
            qr = qr_ref[:, sl]
            t = dg / f - dk
            dfl_ref[:, sl] = (t * (1.0 - lb) * sig * (1.0 - sig)).astype(BF16)
            acc_lb[0:1, sl] += jnp.sum(t * (1.0 - sig), axis=0, keepdims=True)
            dqr_ref[:, sl] = (dq * (sq * (1.0 + qr * (1.0 - sq)))).astype(BF16)

        @pl.when(c == nc - 1)
        def _():
            gl0 = acc_lb[0:1, :] * dlb_dl0
            glbl_ref[0:1, :] = gl0
            glbl_ref[1:2, :] = -gl0

    blk, lb2, st_spec, consts = _hg_specs(nc, True)
    tris, masks = _hg_const_arrays()
    act = jax.ShapeDtypeStruct((S, HG_W), BF16)
    return pl.pallas_call(
        body, name="hg_bwd", grid=(ng, nc),
        in_specs=[blk(0), blk(ng), blk(0), blk(0), st_spec, lb2] + consts + consts[1:],
        out_specs=[blk(0), blk(0), blk(0), lb2],
        out_shape=[act, act, act, jax.ShapeDtypeStruct((2, HG_W), F32)],
        scratch_shapes=[pltpu.VMEM((HG_HPS, HG_HEAD, HG_HEAD), F32), pltpu.VMEM((8, w), F32), pltpu.VMEM((8, w), F32)],
        compiler_params=_params(("parallel", "arbitrary")),
    )(p0, p0, p1, do_hg, states, lbl, tris, masks, masks.astype(BF16))


def _stat_head(lane):
    return lane >> 4


def _stat_lane(lane):
    return lane & (STAT_LANES - 1)


def _o_head(lane):
    return lane >> 6


def _att_bias():
    qi = np.arange(ATT_BLK)[:, None]
    kj = np.arange(2 * ATT_BLK)[None, :]
    band = (kj >= qi) & (kj <= qi + ATT_BLK)
    qm = np.stack([band & (kj >= ATT_BLK), band])
    cur = (kj < ATT_BLK) & (qi <= kj)
    km = np.stack([cur, cur | ((kj >= ATT_BLK) & (qi >= kj - ATT_BLK))])
    to_bias = lambda m: jnp.asarray(np.where(m, 0.0, NEG), F32)
    return to_bias(qm), to_bias(km)


def _att_fwd(qkv_d):
    d, L, _ = qkv_d.shape
    qb = ATT_QB
    rows = qb * ATT_BLK

    def body(q_ref, kp_ref, kc_ref, vp_ref, vc_ref, bias_ref, o_ref, lse_ref, s_scr, p_scr):
        first = _iota((ATT_BLK, HEAD_PAIR), 1) < AT_HEAD
        head_of_lane = _stat_head(_iota((ATT_BLK, STAT_W), 1))
        pairs = [slice(HEAD_PAIR * hp, HEAD_PAIR * (hp + 1)) for hp in range(4)]
        blk = lambda i: slice(ATT_BLK * i, ATT_BLK * (i + 1))

        def keys(i, sl, prev_ref, cur_ref):
            before = prev_ref[0, :, sl] if i == 0 else cur_ref[0, blk(i - 1), sl]
            return jnp.concatenate([before, cur_ref[0, blk(i), sl]], axis=0)

        for i in range(qb):
            for hp, sl in enumerate(pairs):
                q2 = q_ref[0, blk(i), sl] * ATT_SCALE
                zero = jnp.zeros_like(q2)
                qs = jnp.concatenate([jnp.where(first, q2, zero), jnp.where(first, zero, q2)], axis=0)
                s_scr[4 * i + hp] = _dot_nt(qs, keys(i, sl, kp_ref, kc_ref))
        stats = []
        for i in range(qb):
            bias = bias_ref[jnp.minimum(pl.program_id(1), 1)] if i == 0 else bias_ref[1]
            bias2 = jnp.concatenate([bias, bias], axis=0)
            for hp in range(4):
                s = s_scr[4 * i + hp] + bias2
                m = jnp.max(s, axis=-1, keepdims=True)
                p = jnp.exp(s - m)
                l = jnp.sum(p, axis=-1, keepdims=True)
                p_scr[4 * i + hp] = p.astype(BF16)
                stats.append((l, m + jnp.log(l)))
        for i in range(qb):
            lse_blk = jnp.zeros((ATT_BLK, STAT_W), F32)
            for hp, sl in enumerate(pairs):
                l, lse = stats[4 * i + hp]
                o = _dot(p_scr[4 * i + hp], keys(i, sl, vp_ref, vc_ref)) / l
                o_ref[0, blk(i), sl] = jnp.where(first, o[:ATT_BLK], o[ATT_BLK:]).astype(BF16)
                lse_blk = jnp.where(head_of_lane == 2 * hp, lse[:ATT_BLK],
                                    jnp.where(head_of_lane == 2 * hp + 1, lse[ATT_BLK:], lse_blk))
            lse_ref[0, blk(i), :] = lse_blk

    cur = lambda j: pl.BlockSpec((1, rows, AT_W), lambda r, n: (r, n, j))
    prev = lambda j: pl.BlockSpec((1, ATT_BLK, AT_W), lambda r, n: (r, jnp.maximum(qb * n - 1, 0), j))
    return pl.pallas_call(
        body, name=f"att_fwd_d{d}", grid=(d, L // rows),
        in_specs=[cur(0), prev(1), cur(1), prev(2), cur(2), _full((2, ATT_BLK, 2 * ATT_BLK))],
        out_specs=[pl.BlockSpec((1, rows, AT_W), lambda r, n: (r, n, 0)), pl.BlockSpec((1, rows, STAT_W), lambda r, n: (r, n, 0))],
        out_shape=[jax.ShapeDtypeStruct((d, L, AT_W), BF16), jax.ShapeDtypeStruct((d, L, STAT_W), F32)],
        scratch_shapes=[pltpu.VMEM((4 * qb, 2 * ATT_BLK, 2 * ATT_BLK), F32), pltpu.VMEM((4 * qb, 2 * ATT_BLK, 2 * ATT_BLK), BF16)],
        compiler_params=_params(("parallel", "parallel")),
    )(qkv_d, qkv_d, qkv_d, qkv_d, qkv_d, _att_bias()[0])


def _att_combine(os_d, ls_d, z_at):
    S = z_at.shape[0]
    tm = MIX_TM

    def body(oa_ref, ob4_ref, oc16_ref, la_ref, lb4_ref, lc16_ref, p4_ref, p16_ref, z_ref,
             oat_ref, lse_ref, mix_ref, lb_scr, lc_scr):
        _merge_residues(lb4_ref, lb_scr, 4)
        _merge_residues(lc16_ref, lc_scr, 16)
        ls = (la_ref[...], lb_scr[0], lc_scr[0])
        mx = jnp.maximum(jnp.maximum(ls[0], ls[1]), ls[2])
        es = [jnp.exp(l - mx) for l in ls]
        zs = es[0] + es[1] + es[2]
        lse_ref[...] = mx + jnp.log(zs)
        spread = ((_o_head(_iota((STAT_W, AT_W), 1)) == _stat_head(_iota((STAT_W, AT_W), 0)))
                  & (_stat_lane(_iota((STAT_W, AT_W), 0)) == 0)).astype(BF16)
        os_ = (oa_ref[...].astype(F32), _merge_bf16(ob4_ref, p4_ref, 4), _merge_bf16(oc16_ref, p16_ref, 16))
        o = jnp.zeros((tm, AT_W), F32)
        for e, oi in zip(es, os_):
            hi, lo = _split2(e / zs)
            o = o + (_dot(hi, spread) + _dot(lo, spread)) * oi
        oat_ref[...] = o
        z = z_ref[...]
        mixed = o * (z * _sigmoid(z))
        mix_ref[...] = mixed.astype(BF16)

    row = lambda w: pl.BlockSpec((tm, w), lambda s: (s, 0))
    return pl.pallas_call(
        body, name="att_combine", grid=(S // tm,),
        in_specs=[row(AT_W), _dil_spec(4, tm, AT_W), _dil_spec(16, tm, AT_W),
                  row(STAT_W), _dil_spec(4, tm, STAT_W), _dil_spec(16, tm, STAT_W),
                  _full((PERM_ROWS, PERM_ROWS)), _full((PERM_ROWS, PERM_ROWS)), row(AT_W)],
        out_specs=[row(AT_W), row(STAT_W), row(AT_W)],
        out_shape=[jax.ShapeDtypeStruct((S, AT_W), F32), jax.ShapeDtypeStruct((S, STAT_W), F32),
                   jax.ShapeDtypeStruct((S, AT_W), BF16)],
        scratch_shapes=[_slab_scratch(tm, STAT_W), _slab_scratch(tm, STAT_W)],
        compiler_params=_params(("parallel",)),
    )(os_d[0].reshape(S, AT_W), os_d[1], os_d[2], ls_d[0].reshape(S, STAT_W), ls_d[1], ls_d[2],
      _perm_matrix(4), _perm_matrix(16), z_at)


def _att_gate_bwd(dm_at, o_at, lse, z_at):
    S = z_at.shape[0]
    tm = MIX_TM

    def body(dm_ref, o_ref, l_ref, z_ref, do_ref, do4_ref, do16_ref, dl_ref, dl4_ref, dl16_ref, dz_ref, do_scr, dl_scr):
        o = o_ref[...]
        z = z_ref[...]
        dm = dm_ref[...]
        sz = _sigmoid(z)
        dz_ref[...] = (dm * o * (sz * (1.0 + z * (1.0 - sz)))).astype(BF16)
        do = dm * (z * sz)
        do_ref[...] = do.astype(BF16)
        gather = (_o_head(_iota((AT_W, STAT_W), 0)) == _stat_head(_iota((AT_W, STAT_W), 1))).astype(BF16)
        dl = jnp.where(_stat_lane(_iota((tm, STAT_W), 1)) < STAT_LSE_LANE, _dot_sel(do * o, gather), l_ref[...])
        dl_ref[...] = dl
        _to_slabs(do, do_scr)
        _to_slabs(dl, dl_scr)
        _split_residues(do_scr, do4_ref, 4, BF16)
        _split_residues(do_scr, do16_ref, 16, BF16)
        _split_residues(dl_scr, dl4_ref, 4, F32)
        _split_residues(dl_scr, dl16_ref, 16, F32)

    row = lambda w: pl.BlockSpec((tm, w), lambda s: (s, 0))
    sds = jax.ShapeDtypeStruct
    return pl.pallas_call(
        body, name="att_gate_bwd", grid=(S // tm,),
        in_specs=[row(AT_W), row(AT_W), row(STAT_W), row(AT_W)],
        out_specs=[row(AT_W), _dil_spec(4, tm, AT_W), _dil_spec(16, tm, AT_W),
                   row(STAT_W), _dil_spec(4, tm, STAT_W), _dil_spec(16, tm, STAT_W), row(AT_W)],
        out_shape=[sds((S, AT_W), BF16), sds((4, S // 4, AT_W), BF16), sds((16, S // 16, AT_W), BF16),
                   sds((S, STAT_W), F32), sds((4, S // 4, STAT_W), F32), sds((16, S // 16, STAT_W), F32),
                   sds((S, AT_W), BF16)],
        scratch_shapes=[_slab_scratch(tm, AT_W), _slab_scratch(tm, STAT_W)],
        compiler_params=_params(("parallel",)),
    )(dm_at, o_at, lse, z_at)


def _att_bwd(qkv_d, do_d, dl_d):
    d, L, _ = qkv_d.shape
    nb = L // ATT_BLK
    qb = ATT_QB
    rows = qb * ATT_BLK
    nsteps = L // rows

    def body(qc_ref, qn_ref, kp_ref, kc_ref, vp_ref, vc_ref, ac_ref, an_ref, lc_ref, ln_ref, bq_ref, bk_ref, dqkv_ref,
             s_scr, dp_scr, st_scr, dpt_scr, ds_scr, pt_scr, dst_scr):
        n = pl.program_id(1)
        first = _iota((ATT_BLK, HEAD_PAIR), 1) < AT_HEAD
        pairs = [slice(HEAD_PAIR * hp, HEAD_PAIR * (hp + 1)) for hp in range(4)]
        blk = lambda i: slice(ATT_BLK * i, ATT_BLK * (i + 1))

        def stack(t):
            zero = jnp.zeros_like(t)
            return jnp.concatenate([jnp.where(first, t, zero), jnp.where(first, zero, t)], axis=0)

        def unstack(t2):
            return jnp.where(first, t2[:ATT_BLK], t2[ATT_BLK:])

        def with_prev(i, sl, prev_ref, cur_ref):
            before = prev_ref[0, :, sl] if i == 0 else cur_ref[0, blk(i - 1), sl]
            return jnp.concatenate([before, cur_ref[0, blk(i), sl]], axis=0)

        def with_next(i, sl, cur_ref, next_ref):
            after = next_ref[0, :, sl] if i == qb - 1 else cur_ref[0, blk(i + 1), sl]
            return jnp.concatenate([cur_ref[0, blk(i), sl], after], axis=0)

        for i in range(qb):
            for hp, sl in enumerate(pairs):
                j = 4 * i + hp
                s_scr[j] = _dot_nt(stack(qc_ref[0, blk(i), sl] * ATT_SCALE), with_prev(i, sl, kp_ref, kc_ref))
                dp_scr[j] = _dot_nt(stack(ac_ref[0, blk(i), sl]), with_prev(i, sl, vp_ref, vc_ref))
                st_scr[j] = _dot_nt(stack(kc_ref[0, blk(i), sl] * ATT_SCALE), with_next(i, sl, qc_ref, qn_ref))
                dpt_scr[j] = _dot_nt(stack(vc_ref[0, blk(i), sl]), with_next(i, sl, ac_ref, an_ref))
        for i in range(qb):
            bias = bq_ref[jnp.minimum(n, 1)] if i == 0 else bq_ref[1]
            bias_t = bk_ref[jnp.minimum(nsteps - 1 - n, 1)] if i == qb - 1 else bk_ref[1]
            bias2 = jnp.concatenate([bias, bias], axis=0)
            bias_t2 = jnp.concatenate([bias_t, bias_t], axis=0)
            dl_c = lc_ref[0, blk(i), :]
            dl_t = with_next(i, slice(None), lc_ref, ln_ref).T
            for hp in range(4):
                j = 4 * i + hp
                at = [STAT_LANES * (2 * hp), STAT_LANES * (2 * hp + 1)]
                col = lambda t, o: jnp.concatenate([t[:, a + o: a + o + 1] for a in at], axis=0)
                p = jnp.exp(s_scr[j] + bias2 - col(dl_c, STAT_LSE_LANE))
                ds_scr[j] = (p * (dp_scr[j] - col(dl_c, 0))).astype(BF16)
                row = lambda t, o: jnp.concatenate([jnp.broadcast_to(t[a + o: a + o + 1, :], (ATT_BLK, 2 * ATT_BLK)) for a in at], axis=0)
                pt = jnp.exp(st_scr[j] + bias_t2 - row(dl_t, STAT_LSE_LANE))
                pt_scr[j] = pt.astype(BF16)
                dst_scr[j] = (pt * (dpt_scr[j] - row(dl_t, 0))).astype(BF16)
        for i in range(qb):
            for hp, sl in enumerate(pairs):
                j = 4 * i + hp
                dq = unstack(_dot(ds_scr[j], with_prev(i, sl, kp_ref, kc_ref))) * ATT_SCALE
                dk = unstack(_dot(dst_scr[j], with_next(i, sl, qc_ref, qn_ref))) * ATT_SCALE
                dv = unstack(_dot(pt_scr[j], with_next(i, sl, ac_ref, an_ref)))
                dqkv_ref[0, blk(i), sl] = dq.astype(BF16)
                dqkv_ref[0, blk(i), AT_W + HEAD_PAIR * hp: AT_W + HEAD_PAIR * (hp + 1)] = dk.astype(BF16)
                dqkv_ref[0, blk(i), 2 * AT_W + HEAD_PAIR * hp: 2 * AT_W + HEAD_PAIR * (hp + 1)] = dv.astype(BF16)

    cur = lambda j: pl.BlockSpec((1, rows, AT_W), lambda r, n: (r, n, j))
    prev = lambda j: pl.BlockSpec((1, ATT_BLK, AT_W), lambda r, n: (r, jnp.maximum(qb * n - 1, 0), j))
    nxt_blk = lambda n: jnp.minimum(qb * (n + 1), nb - 1)
    sq = (4 * qb, 2 * ATT_BLK, 2 * ATT_BLK)
    return pl.pallas_call(
        body, name=f"att_bwd_d{d}", grid=(d, nsteps),
        in_specs=[cur(0), pl.BlockSpec((1, ATT_BLK, AT_W), lambda r, n: (r, nxt_blk(n), 0)), prev(1), cur(1), prev(2), cur(2),
                  pl.BlockSpec((1, rows, AT_W), lambda r, n: (r, n, 0)),
                  pl.BlockSpec((1, ATT_BLK, AT_W), lambda r, n: (r, nxt_blk(n), 0)),
                  pl.BlockSpec((1, rows, STAT_W), lambda r, n: (r, n, 0)),
                  pl.BlockSpec((1, ATT_BLK, STAT_W), lambda r, n: (r, nxt_blk(n), 0)),
                  _full((2, ATT_BLK, 2 * ATT_BLK)), _full((2, ATT_BLK, 2 * ATT_BLK))],
        out_specs=pl.BlockSpec((1, rows, 3 * AT_W), lambda r, n: (r, n, 0)),
        out_shape=jax.ShapeDtypeStruct((d, L, 3 * AT_W), BF16),
        scratch_shapes=[pltpu.VMEM(sq, F32)] * 4 + [pltpu.VMEM(sq, BF16)] * 3,
        compiler_params=_params(("parallel", "parallel")),
    )(qkv_d, qkv_d, qkv_d, qkv_d, qkv_d, qkv_d, do_d, do_d, dl_d, dl_d, *_att_bias())


def _att_bwd_combine(dqkvs, cos2, sin2):
    S = dqkvs[0].shape[1]
    tm = MIX_TM

    def body(a_ref, b4_ref, c16_ref, p4_ref, p16_ref, cos_ref, sin_ref, dq_ref, dk_ref, dv_ref):
        t = a_ref[...].astype(F32) + _merge_bf16(b4_ref, p4_ref, 4) + _merge_bf16(c16_ref, p16_ref, 16)
        dy = t[:, : 2 * AT_W]
        cosf = jnp.tile(cos_ref[...], (1, 8))
        sinf = jnp.tile(sin_ref[...], (1, 8))
        dx = dy * cosf - _rope_rot(dy) * sinf
        dq_ref[...] = dx[:, :AT_W].astype(BF16)
        dk_ref[...] = dx[:, AT_W:].astype(BF16)
        dv_ref[...] = t[:, 2 * AT_W:].astype(BF16)

    row = lambda w: pl.BlockSpec((tm, w), lambda s: (s, 0))
    act = jax.ShapeDtypeStruct((S, AT_W), BF16)
    return pl.pallas_call(
        body, name="att_bwd_combine", grid=(S // tm,),
        in_specs=[row(3 * AT_W), _dil_spec(4, tm, 3 * AT_W), _dil_spec(16, tm, 3 * AT_W),
                  _full((PERM_ROWS, PERM_ROWS)), _full((PERM_ROWS, PERM_ROWS)), row(128), row(128)],
        out_specs=[row(AT_W), row(AT_W), row(AT_W)],
        out_shape=[act, act, act],
        compiler_params=_params(("parallel",)),
    )(dqkvs[0].reshape(S, 3 * AT_W), dqkvs[1], dqkvs[2], _perm_matrix(4), _perm_matrix(16), cos2, sin2)


def _outproj(x2, tgt2, mix_at, w_out_full, fnw, o_hg, p1, hg_norm_w):
    S = x2.shape[0]
    tm = PROJ_TM
    ns = S // tm

    def body(x_ref, t_ref, ma_ref, w_ref, fw_ref, o_ref, z_ref, hgw_ref,
             dh_ref, doh_ref, dzh_ref, dma_ref, gw_ref, gfw_ref, ghgw_ref, loss_ref):
        s = pl.program_id(0)

        @pl.when(s == 0)
        def _():
            gw_ref[...] = jnp.zeros_like(gw_ref)
            gfw_ref[...] = jnp.zeros_like(gfw_ref)
            ghgw_ref[...] = jnp.zeros_like(ghgw_ref)
            loss_ref[...] = jnp.zeros_like(loss_ref)

        heads = [slice(HG_HEAD * hh, HG_HEAD * (hh + 1)) for hh in range(HG_HEADS)]
        norm = []
        for sl in heads:
            o, z = o_ref[:, sl], z_ref[:, sl]
            rs = lax.rsqrt(jnp.mean(o * o, axis=-1, keepdims=True) + EPS)
            norm.append((rs, o * rs, _sigmoid(z)))
        mh = jnp.concatenate([(oh * hgw_ref[:, sl] * (z_ref[:, sl] * sz)).astype(BF16)
                              for sl, (_, oh, sz) in zip(heads, norm)], axis=1)
        y = _dot(mh, w_ref[:HG_W, :]) + _dot(ma_ref[...], w_ref[HG_W:, :])
        h = x_ref[...] + y
        r = lax.rsqrt(jnp.mean(h * h, axis=-1, keepdims=True) + EPS)
        hn = h * r
        fw = fw_ref[...]
        err = hn * fw - t_ref[...]
        loss_ref[...] += 0.5 * jnp.sum(jnp.mean(err * err, axis=-1, keepdims=True))
        dout = err * (1.0 / D_MODEL)
        gfw_ref[...] += jnp.sum(dout * hn, axis=0, keepdims=True)
        dhn = dout * fw
        dh = r * (dhn - hn * jnp.mean(dhn * hn, axis=-1, keepdims=True))
        dh_ref[...] = dh
        dhb = dh.astype(BF16)
        dma_ref[...] = _dot_nt(dhb, w_ref[HG_W:, :])
        dmh = _dot_nt(dhb, w_ref[:HG_W, :])
        for sl, (rs, oh, sz) in zip(heads, norm):
            z, dmix, gwv = z_ref[:, sl], dmh[:, sl], hgw_ref[:, sl]
            dzh_ref[:, sl] = (dmix * (oh * gwv) * (sz * (1.0 + z * (1.0 - sz)))).astype(BF16)
            don = dmix * (z * sz)
            ghgw_ref[:, sl] += jnp.sum(don * oh, axis=0, keepdims=True)
            dy = don * gwv
            doh_ref[:, sl] = rs * (dy - oh * jnp.mean(dy * oh, axis=-1, keepdims=True))
        gw_ref[:HG_W, :] += _dot_tn(mh, dhb)
        gw_ref[HG_W:, :] += _dot_tn(ma_ref[...], dhb)

    row = lambda w: pl.BlockSpec((tm, w), lambda s: (s, 0))
    return pl.pallas_call(
        body, name="outproj", grid=(ns,),
        in_specs=[row(D_MODEL), row(D_MODEL), row(AT_W),
                  _full((D_MODEL, D_MODEL)), _full((1, D_MODEL)),
                  row(HG_W), pl.BlockSpec((tm, HG_W), lambda s: (s, 1)), _full((1, HG_W))],
        out_specs=[row(D_MODEL), row(HG_W), row(HG_W), row(AT_W), _full((D_MODEL, D_MODEL)), _full((1, D_MODEL)),
                   _full((1, HG_W)), _full((8, 128))],
        out_shape=[jax.ShapeDtypeStruct((S, D_MODEL), F32), jax.ShapeDtypeStruct((S, HG_W), F32),
                   jax.ShapeDtypeStruct((S, HG_W), BF16), jax.ShapeDtypeStruct((S, AT_W), F32),
                   jax.ShapeDtypeStruct((D_MODEL, D_MODEL), F32), jax.ShapeDtypeStruct((1, D_MODEL), F32),
                   jax.ShapeDtypeStruct((1, HG_W), F32), jax.ShapeDtypeStruct((8, 128), F32)],
        compiler_params=_params(("arbitrary",)),
    )(x2, tgt2, mix_at, w_out_full, fnw, o_hg, p1, hg_norm_w)


def _inproj_bwd_x(dps, w_in_full, x2, norm_w, dh):
    S = x2.shape[0]
    tm = PROJ_TM

    def body(d0, d1, d2, d3, d4, d5, d6, d7, w_ref, x_ref, nw_ref, dh_ref, gx_ref, gnw_ref):
        s = pl.program_id(0)

        @pl.when(s == 0)
        def _():
            gnw_ref[...] = jnp.zeros_like(gnw_ref)

        du = jnp.zeros((tm, D_MODEL), F32)
        for i, dref in enumerate((d0, d1, d2, d3, d4, d5, d6, d7)):
            j, half = divmod(i, 2)
            du = du + _dot_nt(dref[...], w_ref[j, :, 512 * half: 512 * (half + 1)])
        x = x_ref[...]
        r = lax.rsqrt(jnp.mean(x * x, axis=-1, keepdims=True) + EPS)
        xh = x * r
        gnw_ref[...] += jnp.sum(du * xh, axis=0, keepdims=True)
        dun = du * nw_ref[...]
        gx_ref[...] = dh_ref[...] + r * (dun - xh * jnp.mean(dun * xh, axis=-1, keepdims=True))

    row = lambda w: pl.BlockSpec((tm, w), lambda s: (s, 0))
    return pl.pallas_call(
        body, name="inproj_bwd_x", grid=(S // tm,),
        in_specs=[row(512)] * 8 + [_full((4, D_MODEL, 1024)), row(D_MODEL), _full((1, D_MODEL)), row(D_MODEL)],
        out_specs=[row(D_MODEL), _full((1, D_MODEL))],
        out_shape=[jax.ShapeDtypeStruct((S, D_MODEL), F32), jax.ShapeDtypeStruct((1, D_MODEL), F32)],
        compiler_params=_params(("arbitrary",)),
    )(*dps, w_in_full, x2, norm_w, dh)


def _inproj_bwd_w(u, dps):
    S = u.shape[0]
    tm = PROJ_TM

    def body(u_ref, d0, d1, d2, d3, d4, d5, d6, d7, g_ref):
        @pl.when(pl.program_id(0) == 0)
        def _():
            g_ref[...] = jnp.zeros_like(g_ref)

        ub = u_ref[...]
        for i, dref in enumerate((d0, d1, d2, d3, d4, d5, d6, d7)):
            j, half = divmod(i, 2)
            g_ref[j, :, 512 * half: 512 * (half + 1)] += _dot_tn(ub, dref[...])

    return pl.pallas_call(
        body, name="inproj_bwd_w", grid=(S // tm,),
        in_specs=[pl.BlockSpec((tm, D_MODEL), lambda s: (s, 0))] + [pl.BlockSpec((tm, 512), lambda s: (s, 0))] * 8,
        out_specs=_full((4, D_MODEL, 1024)),
        out_shape=jax.ShapeDtypeStruct((4, D_MODEL, 1024), F32),
        compiler_params=_params(("arbitrary",)),
    )(u, *dps)


def _adamw_update(gg, w_ref, m_ref, v_ref, d_ref, nm_ref, nv_ref):
    nm = ADAM_B1 * m_ref[...] + (1.0 - ADAM_B1) * gg
    nv = ADAM_B2 * v_ref[...] + (1.0 - ADAM_B2) * (gg * gg)
    m_hat = nm / (1.0 - ADAM_B1 ** ADAM_STEP)
    v_hat = nv / (1.0 - ADAM_B2 ** ADAM_STEP)
    d_ref[...] = -ADAM_LR * (m_hat / (jnp.sqrt(v_hat) + ADAM_EPS) + ADAM_WD * w_ref[...])
    nm_ref[...] = nm
    nv_ref[...] = nv


def _adamw(w, g, m, v, name):
    rows, cols = w.shape
    tr = min(rows, 256)

    def body(w_ref, g_ref, m_ref, v_ref, d_ref, nm_ref, nv_ref):
        _adamw_update(g_ref[...], w_ref, m_ref, v_ref, d_ref, nm_ref, nv_ref)

    spec = pl.BlockSpec((tr, cols), lambda i: (i, 0))
    sds = jax.ShapeDtypeStruct((rows, cols), F32)
    return pl.pallas_call(
        body, name=name, grid=(rows // tr,),
        in_specs=[spec] * 4, out_specs=[spec] * 3, out_shape=[sds] * 3,
        compiler_params=_params(("parallel",)),
    )(w, g, m, v)


def _place():
    return lax.axis_index("x"), lax.axis_index("y"), lax.axis_index("c")


def _gather_weights(w_in_s, w_out_s):
    def body(win_ref, wout_ref, fin_ref, fout_ref, send_sems, recv_sems):
        x, y, c = _place()
        me = (x, y, c)
        sib = (x, y, 1 - c)
        mine = 2 * x + y
        fin_ref[mine] = win_ref[...].astype(BF16)
        fout_ref[mine] = wout_ref[...].astype(BF16)
        chips = [(1 - x, y), (x, 1 - y), (1 - x, 1 - y)]

        def halves(chip, half):
            return (fin_ref.at[chip, pl.ds(half * 512, 512), :], fout_ref.at[chip, pl.ds(half * 128, 128), :])

        def copy(k, ref, to):
            return pltpu.make_async_remote_copy(src_ref=ref, dst_ref=ref, send_sem=send_sems.at[k],
                                                recv_sem=recv_sems.at[k], device_id=to, device_id_type=MESH)

        first, passed = [], []
        for j, (cx, cy) in enumerate(chips):
            for a, ref in enumerate(halves(mine, c)):
                first.append(copy(2 * j + a, ref, (cx, cy, c)))
        for cp in first:
            cp.start()
        for j, (cx, cy) in enumerate(chips):
            for a, ref in enumerate(halves(2 * cx + cy, c)):
                copy(2 * j + a, ref, me).wait_recv()
                fwd = copy(6 + 2 * j + a, ref, sib)
                fwd.start()
                passed.append(fwd)
        for j, (cx, cy) in enumerate(chips):
            for a, ref in enumerate(halves(2 * cx + cy, 1 - c)):
                copy(6 + 2 * j + a, ref, me).wait_recv()
        for cp in first + passed:
            cp.wait_send()

    vm = pl.BlockSpec(memory_space=pltpu.VMEM)
    return pl.pallas_call(
        body, name="gather_weights",
        in_specs=[vm, vm], out_specs=[vm, vm],
        out_shape=[jax.ShapeDtypeStruct((4, D_MODEL, 1024), BF16), jax.ShapeDtypeStruct((4, 256, D_MODEL), BF16)],
        scratch_shapes=[pltpu.SemaphoreType.DMA((12,)), pltpu.SemaphoreType.DMA((12,))],
        compiler_params=pltpu.CompilerParams(vmem_limit_bytes=VMEM_LIMIT),
    )(w_in_s, w_out_s)


def _swap_halves(g_in, g_out):
    def body(gin_ref, gout_ref, rin_ref, rout_ref, send_sems, recv_sems):
        x, y, c = _place()
        sib = (x, y, 1 - c)
        cps = [pltpu.make_async_remote_copy(src_ref=src.at[:, 1 - c], dst_ref=dst, send_sem=send_sems.at[k],
                                            recv_sem=recv_sems.at[k], device_id=sib, device_id_type=MESH)
               for k, (src, dst) in enumerate(((gin_ref, rin_ref), (gout_ref, rout_ref)))]
        for cp in cps:
            cp.start()
        for cp in cps:
            cp.wait()

    hbm = pl.BlockSpec(memory_space=pl.ANY)
    return pl.pallas_call(
        body, name="swap_halves",
        in_specs=[hbm, hbm], out_specs=[hbm, hbm],
        out_shape=[jax.ShapeDtypeStruct((4,) + g.shape[2:], F32) for g in (g_in, g_out)],
        scratch_shapes=[pltpu.SemaphoreType.DMA((2,)), pltpu.SemaphoreType.DMA((2,))],
    )(g_in, g_out)


def _add_half(g, r, cidx, name):
    n, _, rows, cols = g.shape

    def body(c_ref, g_ref, r_ref, o_ref):
        o_ref[0] = (g_ref[0, 0] + r_ref[0]).astype(BF16)

    return pl.pallas_call(
        body, name=name,
        grid_spec=pltpu.PrefetchScalarGridSpec(
            num_scalar_prefetch=1, grid=(n,),
            in_specs=[pl.BlockSpec((1, 1, rows, cols), lambda j, c_ref: (j, c_ref[0], 0, 0)),
                      pl.BlockSpec((1, rows, cols), lambda j, c_ref: (j, 0, 0))],
            out_specs=pl.BlockSpec((1, rows, cols), lambda j, c_ref: (j, 0, 0))),
        out_shape=jax.ShapeDtypeStruct((n, rows, cols), BF16),
        compiler_params=_params(("parallel",)),
    )(cidx, g, r)


def _exchange_copies(in_ref, out_ref, lin_ref, lout_ref, send_sems, recv_sems):
    x, y, c = _place()
    cps = []
    for k, (cx, cy) in enumerate([(1 - x, y), (x, 1 - y), (1 - x, 1 - y)]):
        for a, (src, dst) in enumerate(((in_ref, lin_ref), (out_ref, lout_ref))):
            cps.append(pltpu.make_async_remote_copy(
                src_ref=src.at[2 * cx + cy], dst_ref=dst.at[k], send_sem=send_sems.at[2 * k + a],
                recv_sem=recv_sems.at[2 * k + a], device_id=(cx, cy, c), device_id_type=MESH))
    return cps


def _exchange_start(cs_in, cs_out):
    def body(in_ref, out_ref, lin_ref, lout_ref, send_sems, recv_sems, in_thru, out_thru, lin_thru, lout_thru, token):
        for cp in _exchange_copies(in_ref, out_ref, lin_ref, lout_ref, send_sems, recv_sems):
            cp.start()
        token[...] = jnp.zeros_like(token)

    lands = [lax.empty((3,) + a.shape[1:], a.dtype) for a in (cs_in, cs_out)]
    bufs = [pltpu.with_memory_space_constraint(a, pltpu.HBM) for a in (cs_in, cs_out, *lands)]
    hbm = pl.BlockSpec(memory_space=pltpu.HBM)
    sem = pl.BlockSpec(memory_space=pltpu.SEMAPHORE)
    return pl.pallas_call(
        body, name="exchange_start",
        in_specs=[hbm] * 4,
        out_specs=[sem, sem, hbm, hbm, hbm, hbm, pl.BlockSpec(memory_space=pltpu.VMEM)],
        out_shape=[pltpu.SemaphoreType.DMA((6,)), pltpu.SemaphoreType.DMA((6,))]
                  + [pltpu.HBM(b.shape, b.dtype) for b in bufs] + [jax.ShapeDtypeStruct((8, 128), F32)],
        input_output_aliases={0: 2, 1: 3, 2: 4, 3: 5},
        compiler_params=pltpu.CompilerParams(has_side_effects=pltpu.SideEffectType.DATAFLOW_SIDE_EFFECTING),
    )(*bufs)


def _exchange_wait(send_sems, recv_sems, in_thru, out_thru, lin_thru, lout_thru, after):
    def body(in_ref, out_ref, lin_ref, lout_ref, send_sems, recv_sems, after_ref, in_dead, out_dead, got_in, got_out):
        for cp in _exchange_copies(in_ref, out_ref, lin_ref, lout_ref, send_sems, recv_sems):
            cp.wait_send()
            cp.wait_recv()

    hbm = pl.BlockSpec(memory_space=pltpu.HBM)
    sem = pl.BlockSpec(memory_space=pltpu.SEMAPHORE)
    bufs = (in_thru, out_thru, lin_thru, lout_thru)
    return pl.pallas_call(
        body, name="exchange_wait",
        in_specs=[hbm] * 4 + [sem, sem, pl.BlockSpec(memory_space=pl.ANY)],
        out_specs=[hbm] * 4,
        out_shape=[pltpu.HBM(b.shape, b.dtype) for b in bufs],
        input_output_aliases={0: 0, 1: 1, 2: 2, 3: 3},
        compiler_params=pltpu.CompilerParams(has_side_effects=pltpu.SideEffectType.DATAFLOW_SIDE_EFFECTING),
    )(*bufs, send_sems, recv_sems, after)


PEER_XOR = (2, 1, 3)


def _sum_chips(cs, r, chip_idx, name):
    _, rows, cols = r.shape
    tr = min(rows, 256)

    def body(m_ref, cs_ref, r_ref, o_ref):
        mine = m_ref[0]
        own = cs_ref[0].astype(F32)
        got = [r_ref[k].astype(F32) for k in range(3)]
        acc = None
        for s in range(4):
            rel = mine ^ s
            term = jnp.where(rel == 0, own, jnp.where(rel == PEER_XOR[0], got[0],
                                                      jnp.where(rel == PEER_XOR[1], got[1], got[2])))
            acc = term if acc is None else acc + term
        o_ref[...] = acc

    return pl.pallas_call(
        body, name=name,
        grid_spec=pltpu.PrefetchScalarGridSpec(
            num_scalar_prefetch=1, grid=(rows // tr,),
            in_specs=[pl.BlockSpec((1, tr, cols), lambda i, m_ref: (m_ref[0], i, 0)),
                      pl.BlockSpec((3, tr, cols), lambda i, m_ref: (0, i, 0))],
            out_specs=pl.BlockSpec((tr, cols), lambda i, m_ref: (i, 0))),
        out_shape=jax.ShapeDtypeStruct((rows, cols), F32),
        compiler_params=_params(("parallel",)),
    )(chip_idx, cs, r)


def _swap_reduced(h_in, h_out):
    def body(in_ref, out_ref, rin_ref, rout_ref, send_sems, recv_sems):
        x, y, c = _place()
        cps = [pltpu.make_async_remote_copy(src_ref=src, dst_ref=dst, send_sem=send_sems.at[k],
                                            recv_sem=recv_sems.at[k], device_id=(x, y, 1 - c), device_id_type=MESH)
               for k, (src, dst) in enumerate(((in_ref, rin_ref), (out_ref, rout_ref)))]
        for cp in cps:
            cp.start()
        for cp in cps:
            cp.wait()

    hbm = pl.BlockSpec(memory_space=pl.ANY)
    return pl.pallas_call(
        body, name="swap_reduced",
        in_specs=[hbm, hbm], out_specs=[hbm, hbm],
        out_shape=[jax.ShapeDtypeStruct(h.shape, F32) for h in (h_in, h_out)],
        scratch_shapes=[pltpu.SemaphoreType.DMA((2,)), pltpu.SemaphoreType.DMA((2,))],
    )(h_in, h_out)


def _adamw_halves(w, mine, theirs, m, v, cidx, name):
    rows, cols = w.shape
    half = rows // 2
    tr = min(half, 256)
    nbh = half // tr

    def body(c_ref, w_ref, a_ref, b_ref, m_ref, v_ref, g_ref, d_ref, nm_ref, nv_ref):
        gg = jnp.where(pl.program_id(0) // nbh == c_ref[0], a_ref[...], b_ref[...])
        g_ref[...] = gg
        _adamw_update(gg, w_ref, m_ref, v_ref, d_ref, nm_ref, nv_ref)

    spec = pl.BlockSpec((tr, cols), lambda i, c_ref: (i, 0))
    hspec = pl.BlockSpec((tr, cols), lambda i, c_ref: (i % nbh, 0))
    sds = jax.ShapeDtypeStruct((rows, cols), F32)
    return pl.pallas_call(
        body, name=name,
        grid_spec=pltpu.PrefetchScalarGridSpec(
            num_scalar_prefetch=1, grid=(rows // tr,),
            in_specs=[spec, hspec, hspec, spec, spec], out_specs=[spec] * 4),
        out_shape=[sds] * 4,
        compiler_params=_params(("parallel",)),
    )(cidx, w, mine, theirs, m, v)


def _allreduce_small(g_nw, g_fw, g_hgw, g_lbl, loss8):
    def body(nw_ref, fw_ref, hgw_ref, lbl_ref, loss_ref, out_ref, slots, send_sems, recv_sems):
        x, y, c = _place()
        me = 4 * x + 2 * y + c
        slots[me] = jnp.zeros((8, D_MODEL), F32)
        slots[me, 0:1, :] = nw_ref[...]
        slots[me, 1:2, :] = fw_ref[...]
        slots[me, 2:3, 0:HG_W] = hgw_ref[...]
        slots[me, 3:4, 0:HG_W] = lbl_ref[0:1, :]
        slots[me, 3:4, HG_W:] = lbl_ref[1:2, :]
        slots[me, 4:5, 0:128] = loss_ref[0:1, :]
        cps = []
        for k in range(1, 8):
            dx, dy, dc = (k >> 2) & 1, (k >> 1) & 1, k & 1
            to = (x ^ dx, y ^ dy, c ^ dc)
            cps.append(pltpu.make_async_remote_copy(
                src_ref=slots.at[me], dst_ref=slots.at[me], send_sem=send_sems.at[k - 1],
                recv_sem=recv_sems.at[k - 1], device_id=to, device_id_type=MESH))
        for cp in cps:
            cp.start()
        for cp in cps:
            cp.wait()
        acc = slots[0]
        for i in range(1, 8):
            acc = acc + slots[i]
        out_ref[...] = acc

    vm = pl.BlockSpec(memory_space=pltpu.VMEM)
    return pl.pallas_call(
        body, name="allreduce_small",
        in_specs=[vm] * 5, out_specs=vm,
        out_shape=jax.ShapeDtypeStruct((8, D_MODEL), F32),
        scratch_shapes=[pltpu.VMEM((8, 8, D_MODEL), F32), pltpu.SemaphoreType.DMA((7,)), pltpu.SemaphoreType.DMA((7,))],
    )(g_nw, g_fw, g_hgw, g_lbl, loss8)


def _rope_tables(S):
    inv_freq = (np.float32(1.0) / np.power(np.float32(ROPE_THETA), np.arange(ROPE_HALF, dtype=np.float32) / np.float32(ROPE_HALF))).astype(np.float32)
    ang = (np.arange(S, dtype=np.float32)[:, None] * inv_freq[None, :]).astype(np.float32)
    cos, sin = np.cos(ang).astype(np.float32), np.sin(ang).astype(np.float32)
    cos2 = np.concatenate([cos, cos, cos, cos], axis=-1)
    sin2 = np.concatenate([-sin, sin, -sin, sin], axis=-1)
    return jnp.asarray(cos2), jnp.asarray(sin2)


def _local_step(x2, tgt2, norm_w, w_in_full, lbl, hg_norm_w, w_out4, fnw):
    S = x2.shape[0]
    cos2, sin2 = _rope_tables(S)
    w_out_full = w_out4.reshape(D_MODEL, D_MODEL)
    p0, p1, qkv, qkv4, qkv16, z_at, u = _inproj(x2, norm_w, w_in_full, cos2, sin2)
    o_hg, states = _hg_fwd(p0, p1, lbl)
    qkv_ds = [qkv.reshape(1, S, 3 * AT_W), qkv4, qkv16]
    os_d, ls_d = zip(*[_att_fwd(q) for q in qkv_ds])
    o_at, lse, mix_at = _att_combine(os_d, ls_d, z_at)
    dh, do_hg, dz_hg, dm_at, g_wout, g_fw, g_hgw, loss8 = _outproj(x2, tgt2, mix_at, w_out_full, fnw, o_hg, p1, hg_norm_w)
    dqr, dfl, dv_hg, g_lbl = _hg_bwd(p0, p1, do_hg, states, lbl)
    do1, do4, do16, dl1, dl4, dl16, dz_at = _att_gate_bwd(dm_at, o_at, lse, z_at)
    do_ds = [do1.reshape(1, S, AT_W), do4, do16]
    dl_ds = [dl1.reshape(1, S, STAT_W), dl4, dl16]
    dqkvs = [_att_bwd(q, a, b) for q, a, b in zip(qkv_ds, do_ds, dl_ds)]
    dq_at, dk_at, dv_at = _att_bwd_combine(dqkvs, cos2, sin2)
    dps = [dqr, dfl, dv_hg, dz_hg, dq_at, dk_at, dv_at, dz_at]
    return loss8, dps, u, dh, g_lbl, g_hgw, g_wout, g_fw


def kernel(x, norm_w, w_in, hgrn_lb_logits, hg_norm_w, w_out, final_norm_w, loss_target, m_norm_w, m_w_in, m_hgrn_lb_logits, m_hg_norm_w, m_w_out, m_final_norm_w, v_norm_w, v_w_in, v_hgrn_lb_logits, v_hg_norm_w, v_w_out, v_final_norm_w):
    S = x.shape[1]
    w_in_full, w_out_full = _gather_weights(w_in[0], w_out[0])
    loss8, dps, u, dh, g_lbl, g_hgw, g_wout, g_fw = _local_step(
        x[0], loss_target[0], norm_w, w_in_full, hgrn_lb_logits, hg_norm_w,
        w_out_full, final_norm_w.reshape(1, D_MODEL))

    cidx = lax.axis_index("c").astype(jnp.int32).reshape(1)
    g_win = _inproj_bwd_w(u, dps)
    g_in4 = g_win.reshape(4, 2, 512, 1024)
    g_out4 = g_wout.reshape(4, 2, 128, D_MODEL)
    r_in, r_out = _swap_halves(g_in4, g_out4)
    cs_in = _add_half(g_in4, r_in, cidx, "add_half_in")
    cs_out = _add_half(g_out4, r_out, cidx, "add_half_out")
    *started, token = _exchange_start(cs_in, cs_out)
    grad_x, g_nw = _inproj_bwd_x(dps, w_in_full, x[0], norm_w + token[0:1, 0:1], dh)
    cs_in, cs_out, x_in, x_out = _exchange_wait(*started, g_nw)
    chip_idx = (2 * lax.axis_index("x") + lax.axis_index("y")).astype(jnp.int32).reshape(1)
    h_in = _sum_chips(cs_in, x_in, chip_idx, "sum_chips_in")
    h_out = _sum_chips(cs_out, x_out, chip_idx, "sum_chips_out")
    s_in, s_out = _swap_reduced(h_in, h_out)

    red = _allreduce_small(g_nw, g_fw, g_hgw, g_lbl, loss8)
    loss = red[4, 0]
    grad_norm_w = red[0:1, :]
    grad_final_norm_w = red[1, :]
    grad_hg_norm_w = red[2:3, :HG_W]
    grad_lbl = jnp.concatenate([red[3:4, :HG_W], red[3:4, HG_W:]], axis=0)

    d_nw, m_nw, v_nw = _adamw(norm_w, grad_norm_w, m_norm_w, v_norm_w, "adamw_norm_w")
    grad_w_in, d_win, m_win, v_win = _adamw_halves(w_in[0], h_in, s_in, m_w_in[0], v_w_in[0], cidx, "adamw_w_in")
    d_lbl, m_lbl, v_lbl = _adamw(hgrn_lb_logits, grad_lbl, m_hgrn_lb_logits, v_hgrn_lb_logits, "adamw_lb_logits")
    d_hgw, m_hgw, v_hgw = _adamw(hg_norm_w, grad_hg_norm_w, m_hg_norm_w, v_hg_norm_w, "adamw_hg_norm_w")
    grad_w_out, d_wout, m_wout, v_wout = _adamw_halves(w_out[0], h_out, s_out, m_w_out[0], v_w_out[0], cidx, "adamw_w_out")
    d_fw, m_fw, v_fw = _adamw(final_norm_w.reshape(1, D_MODEL), grad_final_norm_w.reshape(1, D_MODEL),
                              m_final_norm_w.reshape(1, D_MODEL), v_final_norm_w.reshape(1, D_MODEL), "adamw_final_norm_w")
    e1 = lambda a: a[None]
    flat = lambda a: a.reshape(D_MODEL)
    return (loss, grad_x[None], grad_norm_w, e1(grad_w_in), grad_lbl, grad_hg_norm_w, e1(grad_w_out), grad_final_norm_w,
            d_nw, e1(d_win), d_lbl, d_hgw, e1(d_wout), flat(d_fw),
            m_nw, e1(m_win), m_lbl, m_hgw, e1(m_wout), flat(m_fw),
            v_nw, e1(v_win), v_lbl, v_hgw, e1(v_wout), flat(v_fw))
```

```python
import jax
import jax.numpy as jnp
import numpy as np
from jax import lax
from jax.experimental import pallas as pl
from jax.experimental.pallas import tpu as pltpu

F32 = jnp.float32
BF16 = jnp.bfloat16
MESH = pl.DeviceIdType.MESH

D_MODEL = 1024
HG_HEADS = 4
HG_HEAD = 128
HG_W = HG_HEADS * HG_HEAD
AT_HEAD = 64
AT_W = 512
HEAD_PAIR = 2 * AT_HEAD
ROPE_HALF = 32
ROPE_THETA = 10000.0
EPS = 1e-6
CHUNK = 128
LEVELS = (64, 32, 16, 8, 4, 2, 1)
DIAG = 1
SUBLANES = 8
ATT_BLK = 128
ATT_QB = 4
ATT_SCALE = 0.125
STAT_W = 128
STAT_LANES = 16
STAT_LSE_LANE = 8
NEG = -1e30
VMEM_LIMIT = 56 * 1024 * 1024
MIX_TM = 1024
PROJ_TM = 512

ADAM_LR = 0.001
ADAM_B1 = 0.9
ADAM_B2 = 0.999
ADAM_EPS = 1e-08
ADAM_WD = 0.01
ADAM_STEP = 10


def _iota(shape, dim):
    return lax.broadcasted_iota(jnp.int32, shape, dim)


def _dot(a, b):
    return jnp.dot(a, b, preferred_element_type=F32)


def _dot_nt(a, b):
    return lax.dot_general(a, b, (((1,), (1,)), ((), ())), preferred_element_type=F32)


def _dot_tn(a, b):
    return lax.dot_general(a, b, (((0,), (0,)), ((), ())), preferred_element_type=F32)


def _sigmoid(v):
    return 0.5 * jnp.tanh(0.5 * v) + 0.5


def _params(sem=None, vmem=VMEM_LIMIT):
    return pltpu.CompilerParams(dimension_semantics=sem, vmem_limit_bytes=vmem)


def _full(shape):
    n = len(shape)
    return pl.BlockSpec(shape, lambda *_: (0,) * n)


def _rope_rot(y):
    n = y.shape[1]
    first = (_iota(y.shape, 1) & (2 * ROPE_HALF - 1)) < ROPE_HALF
    return jnp.where(first, pltpu.roll(y, n - ROPE_HALF, 1), pltpu.roll(y, ROPE_HALF, 1))


def _dil_spec(d, tm, width):
    return pl.BlockSpec((d, tm // d, width), lambda s: (0, s, 0))


LANES = 128


def _slab_scratch(tm, width):
    return pltpu.VMEM((width // LANES, tm, LANES), F32)


def _to_slabs(v, slabs_ref):
    for j in range(slabs_ref.shape[0]):
        slabs_ref[j] = v[:, LANES * j: LANES * (j + 1)]


def _from_slabs(slabs_ref):
    return jnp.concatenate([slabs_ref[j] for j in range(slabs_ref.shape[0])], axis=1)


def _split_residues(slabs_ref, dst_ref, d, dtype):
    nslab, tm, _ = slabs_ref.shape
    for r in range(d):
        for j in range(nslab):
            dst_ref[r, :, LANES * j: LANES * (j + 1)] = slabs_ref[j, pl.ds(r, tm // d, stride=d), :].astype(dtype)


def _merge_residues(src_ref, slabs_ref, d):
    nslab, tm, _ = slabs_ref.shape
    for r in range(d):
        for j in range(nslab):
            slabs_ref[j, pl.ds(r, tm // d, stride=d), :] = src_ref[r, :, LANES * j: LANES * (j + 1)].astype(F32)


PERM_ROWS = 256


def _perm_matrix(d):
    n = PERM_ROWS // d
    t = np.arange(PERM_ROWS)
    p = np.zeros((PERM_ROWS, PERM_ROWS), np.float32)
    p[t, (t % d) * n + t // d] = 1.0
    return jnp.asarray(p, BF16)


def _merge_bf16(src_ref, perm_ref, d):
    n = PERM_ROWS // d
    tm = src_ref.shape[1] * d
    outs = []
    for b in range(tm // PERM_ROWS):
        flat = jnp.concatenate([src_ref[r, n * b: n * (b + 1), :] for r in range(d)], axis=0)
        outs.append(_dot(perm_ref[...], flat))
    return jnp.concatenate(outs, axis=0)


def _inproj(x2, norm_w, w_in_full, cos2, sin2):
    S = x2.shape[0]
    tm = PROJ_TM

    def body(x_ref, nw_ref, w_ref, cos_ref, sin_ref, p0_ref, p1_ref, qkv_ref, qkv4_ref, qkv16_ref, z_ref, u_ref, scr):
        x = x_ref[...]
        r = lax.rsqrt(jnp.mean(x * x, axis=-1, keepdims=True) + EPS)
        u = x * r * nw_ref[...]
        ub = u.astype(BF16)
        u_ref[...] = ub
        p0_ref[...] = _dot(ub, w_ref[0])
        p1_ref[...] = _dot(ub, w_ref[1])
        y2 = _dot(ub, w_ref[2])
        cosf = jnp.tile(cos_ref[...], (1, 8))
        sinf = jnp.tile(sin_ref[...], (1, 8))
        y3 = _dot(ub, w_ref[3])
        z_ref[...] = y3[:, AT_W:]
        qkv = jnp.concatenate([y2 * cosf + _rope_rot(y2) * sinf, y3[:, :AT_W]], axis=1)
        qkv_ref[...] = qkv.astype(BF16)
        _to_slabs(qkv, scr)
        _split_residues(scr, qkv4_ref, 4, BF16)
        _split_residues(scr, qkv16_ref, 16, BF16)

    row = lambda w: pl.BlockSpec((tm, w), lambda s: (s, 0))
    qkv_w = 3 * AT_W
    return pl.pallas_call(
        body, name="inproj", grid=(S // tm,),
        in_specs=[row(D_MODEL), _full((1, D_MODEL)), _full((4, D_MODEL, 1024)), row(128), row(128)],
        out_specs=[row(1024), row(1024), row(qkv_w), _dil_spec(4, tm, qkv_w), _dil_spec(16, tm, qkv_w), row(AT_W),
                   row(D_MODEL)],
        out_shape=[jax.ShapeDtypeStruct((S, 1024), F32), jax.ShapeDtypeStruct((S, 1024), F32),
                   jax.ShapeDtypeStruct((S, qkv_w), BF16), jax.ShapeDtypeStruct((4, S // 4, qkv_w), BF16),
                   jax.ShapeDtypeStruct((16, S // 16, qkv_w), BF16), jax.ShapeDtypeStruct((S, AT_W), F32),
                   jax.ShapeDtypeStruct((S, D_MODEL), BF16)],
        scratch_shapes=[_slab_scratch(tm, qkv_w)],
        compiler_params=_params(("parallel",)),
    )(x2, norm_w, w_in_full, cos2, sin2)


HG_HPS = 4
HG_CPS = 4
N_LEV = len(LEVELS)


def _hg_const_arrays():
    r = np.arange(CHUNK)[:, None]
    c = np.arange(CHUNK)[None, :]
    tris = np.stack([r >= c, r <= c])
    lm = [((r // (2 * m)) == (c // (2 * m))) & (r % (2 * m) >= m) & (c % (2 * m) < m) for m in LEVELS]
    dm = [(c == r - dl) & (r % DIAG >= dl) for dl in range(DIAG)]
    masks = np.stack(lm + dm)
    return jnp.asarray(tris, BF16), jnp.asarray(masks, F32)


def _split2(a):
    hi = a.astype(BF16)
    return hi, (a - hi.astype(F32)).astype(BF16)


def _dot3(a, b, dot=_dot):
    ah, al = _split2(a)
    bh, bl = _split2(b)
    n = b.shape[1]
    p = dot(ah, jnp.concatenate([bh, bl], axis=1))
    return (p[:, :n] + p[:, n:]) + dot(al, bh)


def _split3(a):
    a1 = a.astype(BF16)
    r1 = a - a1.astype(F32)
    a2 = r1.astype(BF16)
    return a1, a2, (r1 - a2.astype(F32)).astype(BF16)


def _tri_dot(tri, a):
    n = a.shape[1]
    p = _dot(tri, jnp.concatenate(_split3(a), axis=1))
    return (p[:, :n] + p[:, n:2 * n]) + p[:, 2 * n:]


def _dot_sel(a, sel):
    a1, a2, a3 = _split3(a)
    return (_dot(a1, sel) + _dot(a2, sel)) + _dot(a3, sel)


def _rowsum(t):
    return _dot(t.astype(BF16), jnp.ones((t.shape[1], t.shape[1]), BF16))


def _level_refs(b):
    refs = []
    pos = _iota(b.shape, 0)
    for m in LEVELS:
        if 2 * m >= SUBLANES:
            parts = [jnp.broadcast_to(b[r0 + m - 1: r0 + m, :], (2 * m, b.shape[1])) for r0 in range(0, CHUNK, 2 * m)]
            refs.append(parts[0] if len(parts) == 1 else jnp.concatenate(parts, axis=0))
        else:
            p = pos & (2 * m - 1)
            ref = b
            for off in range(-(m - 1), m + 1):
                if off != 0:
                    ref = jnp.where(p == m - 1 + off, pltpu.roll(b, off % CHUNK, 0), ref)
            refs.append(ref)
    return refs


def _hg_lb(lbl_ref):
    l0 = lbl_ref[0:1, :]
    l1 = lbl_ref[1:2, :]
    mx = jnp.maximum(l0, l1)
    e0 = jnp.exp(l0 - mx)
    e1 = jnp.exp(l1 - mx)
    p0 = e0 / (e0 + e1)
    lb = jnp.clip(p0, 1e-6, 1.0 - 1e-6)
    inside = (p0 >= 1e-6) & (p0 <= 1.0 - 1e-6)
    dlb_dl0 = jnp.where(inside, p0 * (e1 / (e0 + e1)), 0.0)
    return lb, dlb_dl0


def _hg_gates(qr, fl, lb):
    sig = _sigmoid(fl)
    f = lb + (1.0 - lb) * sig
    g = jnp.log(f)
    k = (1.0 - lb) * (1.0 - sig)
    sq = _sigmoid(qr)
    q = qr * sq
    return sig, f, g, k, sq, q


def _hg_levels(q, k, b, mk_ref):
    refs = _level_refs(b)
    a = jnp.zeros((CHUNK, CHUNK), F32)
    es, qts, kts = [], [], []
    for i in range(N_LEV):
        diff = b - refs[i]
        e = jnp.exp(jnp.minimum(diff, -diff))
        qt = (q * e).astype(BF16)
        kt = (k * e).astype(BF16)
        a = a + _dot_nt(qt, kt) * mk_ref[i]
        es.append(e); qts.append(qt); kts.append(kt)
    return a, es, qts, kts


def _hg_specs(nc, rev):
    cc = (lambda c: nc - 1 - c) if rev else (lambda c: c)
    w = HG_HEAD * HG_HPS
    blk = lambda off: pl.BlockSpec((HG_CPS * CHUNK, w), lambda h, c: (cc(c), h + off))
    lb2 = pl.BlockSpec((2, w), lambda h, c: (0, h))
    st = pl.BlockSpec((HG_CPS, HG_HPS, HG_HEAD, HG_HEAD), lambda h, c: (cc(c), h, 0, 0))
    consts = [_full((2, CHUNK, CHUNK)), _full((N_LEV + DIAG, CHUNK, CHUNK))]
    return blk, lb2, st, consts


def _hg_fwd(p0, p1, lbl):
    S = p0.shape[0]
    nc = S // (HG_CPS * CHUNK)
    ng = HG_HEADS // HG_HPS

    def body(qr_ref, fl_ref, v_ref, lbl_ref, tri_ref, mk_ref, o_ref, st_ref, state):
        c = pl.program_id(1)

        @pl.when(c == 0)
        def _():
            state[...] = jnp.zeros_like(state)

        lb_all, _ = _hg_lb(lbl_ref)
        heads = [slice(HG_HEAD * hh, HG_HEAD * (hh + 1)) for hh in range(HG_HPS)]
        for j in range(HG_CPS):
            rows = slice(CHUNK * j, CHUNK * (j + 1))
            qs, ks, bs, mats = [], [], [], []
            for sl in heads:
                _, _, g, k, _, q = _hg_gates(qr_ref[rows, sl], fl_ref[rows, sl], lb_all[:, sl])
                qs.append(q); ks.append(k); bs.append(_tri_dot(tri_ref[0], g))
            for hh in range(HG_HPS):
                a, _, _, _ = _hg_levels(qs[hh], ks[hh], bs[hh], mk_ref)
                mats.append(a + _rowsum(qs[hh] * ks[hh]) * mk_ref[N_LEV])
            for hh, sl in enumerate(heads):
                q, k, b, a, v = qs[hh], ks[hh], bs[hh], mats[hh], v_ref[rows, sl]
                b_last = b[CHUNK - 1: CHUNK, :]
                st = state[hh]
                st_ref[j, hh] = st
                o = _dot_nt((q * jnp.exp(b)).astype(BF16), st.astype(BF16)) + _dot(a.astype(BF16), v.astype(BF16))
                state[hh] = st * jnp.exp(b_last) + _dot3(v, k * jnp.exp(b_last - b), _dot_tn)
                o_ref[rows, sl] = o

    blk, lb2, st_spec, consts = _hg_specs(nc, False)
    tris, masks = _hg_const_arrays()
    return pl.pallas_call(
        body, name="hg_fwd", grid=(ng, nc),
        in_specs=[blk(0), blk(ng), blk(0), lb2] + consts,
        out_specs=[blk(0), st_spec],
        out_shape=[jax.ShapeDtypeStruct((S, HG_W), F32), jax.ShapeDtypeStruct((S // CHUNK, HG_HEADS, HG_HEAD, HG_HEAD), F32)],
        scratch_shapes=[pltpu.VMEM((HG_HPS, HG_HEAD, HG_HEAD), F32)],
        compiler_params=_params(("parallel", "arbitrary")),
    )(p0, p0, p1, lbl, tris, masks)


def _hg_bwd(p0, p1, do_hg, states, lbl):
    S = p0.shape[0]
    nc = S // (HG_CPS * CHUNK)
    ng = HG_HEADS // HG_HPS
    w = HG_HEAD * HG_HPS

    def body(qr_ref, fl_ref, v_ref, do_ref, st_ref, lbl_ref, tri_ref, mk_ref, mkb_ref,
             dqr_ref, dfl_ref, dv_ref, glbl_ref, dstate, carry, acc_lb):
        c = pl.program_id(1)

        @pl.when(c == 0)
        def _():
            dstate[...] = jnp.zeros_like(dstate)
            carry[...] = jnp.zeros_like(carry)
            acc_lb[...] = jnp.zeros_like(acc_lb)

        lb_all, dlb_dl0 = _hg_lb(lbl_ref)
        heads = [slice(HG_HEAD * hh, HG_HEAD * (hh + 1)) for hh in range(HG_HPS)]
        diag_mask = mk_ref[N_LEV]
        for j in reversed(range(HG_CPS)):
            rows = slice(CHUNK * j, CHUNK * (j + 1))
            gates, bs, dos = [], [], []
            for sl in heads:
                gt = _hg_gates(qr_ref[rows, sl], fl_ref[rows, sl], lb_all[:, sl])
                gates.append(gt)
                bs.append(_tri_dot(tri_ref[0], gt[2]))
            for sl in heads:
                dos.append(do_ref[rows, sl])
            inter = []
            for hh, sl in enumerate(heads):
                _, _, _, k, _, q = gates[hh]
                b, do, v = bs[hh], dos[hh], v_ref[rows, sl]
                b_last = b[CHUNK - 1: CHUNK, :]
                eb = jnp.exp(b)
                edec = jnp.exp(b_last - b)
                dst = dstate[hh]
                dq = _dot3(do, st_ref[j, hh]) * eb
                dk = _dot3(v, dst) * edec
                da = _dot_nt(do.astype(BF16), v.astype(BF16))
                dv_state = _dot_nt((k * edec).astype(BF16), dst.astype(BF16))
                dstate[hh] = dst * jnp.exp(b_last) + _dot3(do, q * eb, _dot_tn)
                inter.append((dq, dk, da, dv_state))
            for hh, sl in enumerate(heads):
                sig, f, _, k, sq, q = gates[hh]
                dq, dk, da, dv_state = inter[hh]
                b = bs[hh]
                db = q * dq - k * dk
                dab = da.astype(BF16)
                refs = _level_refs(b)
                a = _rowsum(q * k) * diag_mask
                for i in range(N_LEV):
                    diff = b - refs[i]
                    e = jnp.exp(jnp.minimum(diff, -diff))
                    qt = (q * e).astype(BF16)
                    kt = (k * e).astype(BF16)
                    a = a + _dot_nt(qt, kt) * mk_ref[i]
                    dam = dab * mkb_ref[i]
                    gq = _dot(dam, kt)
                    gk = _dot_tn(dam, qt)
                    dq = dq + e * gq
                    dk = dk + e * gk
                    db = db + (qt.astype(F32) * gq - kt.astype(F32) * gk)
                dc = _rowsum(da * diag_mask)
                dq = dq + dc * k
                dk = dk + dc * q
                dv_ref[rows, sl] = (_dot_tn(a.astype(BF16), dos[hh].astype(BF16)) + dv_state).astype(BF16)
                dg = _tri_dot(tri_ref[1], db) + carry[0:1, sl]
                carry[0:1, sl] += jnp.sum(db, axis=0, keepdims=True)
                lb = lb_all[:, sl]
                qr = qr_ref[rows, sl]
                t = dg / f - dk
                dfl_ref[rows, sl] = (t * (1.0 - lb) * sig * (1.0 - sig)).astype(BF16)
                acc_lb[0:1, sl] += jnp.sum(t * (1.0 - sig), axis=0, keepdims=True)
                dqr_ref[rows, sl] = (dq * (sq * (1.0 + qr * (1.0 - sq)))).astype(BF16)

        @pl.when(c == nc - 1)
        def _():
            gl0 = acc_lb[0:1, :] * dlb_dl0
            glbl_ref[0:1, :] = gl0
            glbl_ref[1:2, :] = -gl0

    blk, lb2, st_spec, consts = _hg_specs(nc, True)
    tris, masks = _hg_const_arrays()
    act = jax.ShapeDtypeStruct((S, HG_W), BF16)
    return pl.pallas_call(
        body, name="hg_bwd", grid=(ng, nc),
        in_specs=[blk(0), blk(ng), blk(0), blk(0), st_spec, lb2] + consts + consts[1:],
        out_specs=[blk(0), blk(0), blk(0), lb2],
        out_shape=[act, act, act, jax.ShapeDtypeStruct((2, HG_W), F32)],
        scratch_shapes=[pltpu.VMEM((HG_HPS, HG_HEAD, HG_HEAD), F32), pltpu.VMEM((8, w), F32), pltpu.VMEM((8, w), F32)],
        compiler_params=_params(("parallel", "arbitrary")),
    )(p0, p0, p1, do_hg, states, lbl, tris, masks, masks.astype(BF16))


def _stat_head(lane):
    return lane >> 4


def _stat_lane(lane):
    return lane & (STAT_LANES - 1)


def _o_head(lane):
    return lane >> 6


def _att_bias():
    qi = np.arange(ATT_BLK)[:, None]
    kj = np.arange(2 * ATT_BLK)[None, :]
    band = (kj >= qi) & (kj <= qi + ATT_BLK)
    qm = np.stack([band & (kj >= ATT_BLK), band])
    cur = (kj < ATT_BLK) & (qi <= kj)
    km = np.stack([cur, cur | ((kj >= ATT_BLK) & (qi >= kj - ATT_BLK))])
    to_bias = lambda m: jnp.asarray(np.where(m, 0.0, NEG), F32)
    return to_bias(qm), to_bias(km)


def _att_fwd(qkv_d):
    d, L, _ = qkv_d.shape
    qb = ATT_QB
    rows = qb * ATT_BLK

    def body(q_ref, kp_ref, kc_ref, vp_ref, vc_ref, bias_ref, o_ref, lse_ref, s_scr, p_scr):
        first = _iota((ATT_BLK, HEAD_PAIR), 1) < AT_HEAD
        head_of_lane = _stat_head(_iota((ATT_BLK, STAT_W), 1))
        pairs = [slice(HEAD_PAIR * hp, HEAD_PAIR * (hp + 1)) for hp in range(4)]
        blk = lambda i: slice(ATT_BLK * i, ATT_BLK * (i + 1))

        def keys(i, sl, prev_ref, cur_ref):
            before = prev_ref[0, :, sl] if i == 0 else cur_ref[0, blk(i - 1), sl]
            return jnp.concatenate([before, cur_ref[0, blk(i), sl]], axis=0)

        for i in range(qb):
            for hp, sl in enumerate(pairs):
                q2 = q_ref[0, blk(i), sl] * ATT_SCALE
                zero = jnp.zeros_like(q2)
                qs = jnp.concatenate([jnp.where(first, q2, zero), jnp.where(first, zero, q2)], axis=0)
                s_scr[4 * i + hp] = _dot_nt(qs, keys(i, sl, kp_ref, kc_ref))
        stats = []
        for i in range(qb):
            bias = bias_ref[jnp.minimum(pl.program_id(1), 1)] if i == 0 else bias_ref[1]
            bias2 = jnp.concatenate([bias, bias], axis=0)
            for hp in range(4):
                s = s_scr[4 * i + hp] + bias2
                m = jnp.max(s, axis=-1, keepdims=True)
                p = jnp.exp(s - m)
                l = jnp.sum(p, axis=-1, keepdims=True)
                p_scr[4 * i + hp] = p.astype(BF16)
                stats.append((l, m + jnp.log(l)))
        for i in range(qb):
            lse_blk = jnp.zeros((ATT_BLK, STAT_W), F32)
            for hp, sl in enumerate(pairs):
                l, lse = stats[4 * i + hp]
                o = _dot(p_scr[4 * i + hp], keys(i, sl, vp_ref, vc_ref)) / l
                o_ref[0, blk(i), sl] = jnp.where(first, o[:ATT_BLK], o[ATT_BLK:]).astype(BF16)
                lse_blk = jnp.where(head_of_lane == 2 * hp, lse[:ATT_BLK],
                                    jnp.where(head_of_lane == 2 * hp + 1, lse[ATT_BLK:], lse_blk))
            lse_ref[0, blk(i), :] = lse_blk

    cur = lambda j: pl.BlockSpec((1, rows, AT_W), lambda r, n: (r, n, j))
    prev = lambda j: pl.BlockSpec((1, ATT_BLK, AT_W), lambda r, n: (r, jnp.maximum(qb * n - 1, 0), j))
    return pl.pallas_call(
        body, name=f"att_fwd_d{d}", grid=(d, L // rows),
        in_specs=[cur(0), prev(1), cur(1), prev(2), cur(2), _full((2, ATT_BLK, 2 * ATT_BLK))],
        out_specs=[pl.BlockSpec((1, rows, AT_W), lambda r, n: (r, n, 0)), pl.BlockSpec((1, rows, STAT_W), lambda r, n: (r, n, 0))],
        out_shape=[jax.ShapeDtypeStruct((d, L, AT_W), BF16), jax.ShapeDtypeStruct((d, L, STAT_W), F32)],
        scratch_shapes=[pltpu.VMEM((4 * qb, 2 * ATT_BLK, 2 * ATT_BLK), F32), pltpu.VMEM((4 * qb, 2 * ATT_BLK, 2 * ATT_BLK), BF16)],
        compiler_params=_params(("parallel", "parallel")),
    )(qkv_d, qkv_d, qkv_d, qkv_d, qkv_d, _att_bias()[0])


def _att_combine(os_d, ls_d, z_at):
    S = z_at.shape[0]
    tm = MIX_TM

    def body(oa_ref, ob4_ref, oc16_ref, la_ref, lb4_ref, lc16_ref, p4_ref, p16_ref, z_ref,
             oat_ref, lse_ref, mix_ref, lb_scr, lc_scr):
        _merge_residues(lb4_ref, lb_scr, 4)
        _merge_residues(lc16_ref, lc_scr, 16)
        ls = (la_ref[...], lb_scr[0], lc_scr[0])
        mx = jnp.maximum(jnp.maximum(ls[0], ls[1]), ls[2])
        es = [jnp.exp(l - mx) for l in ls]
        zs = es[0] + es[1] + es[2]
        lse_ref[...] = mx + jnp.log(zs)
        spread = ((_o_head(_iota((STAT_W, AT_W), 1)) == _stat_head(_iota((STAT_W, AT_W), 0)))
                  & (_stat_lane(_iota((STAT_W, AT_W), 0)) == 0)).astype(BF16)
        os_ = (oa_ref[...].astype(F32), _merge_bf16(ob4_ref, p4_ref, 4), _merge_bf16(oc16_ref, p16_ref, 16))
        o = jnp.zeros((tm, AT_W), F32)
        for e, oi in zip(es, os_):
            hi, lo = _split2(e / zs)
            o = o + (_dot(hi, spread) + _dot(lo, spread)) * oi
        oat_ref[...] = o
        z = z_ref[...]
        mixed = o * (z * _sigmoid(z))
        mix_ref[...] = mixed.astype(BF16)

    row = lambda w: pl.BlockSpec((tm, w), lambda s: (s, 0))
    return pl.pallas_call(
        body, name="att_combine", grid=(S // tm,),
        in_specs=[row(AT_W), _dil_spec(4, tm, AT_W), _dil_spec(16, tm, AT_W),
                  row(STAT_W), _dil_spec(4, tm, STAT_W), _dil_spec(16, tm, STAT_W),
                  _full((PERM_ROWS, PERM_ROWS)), _full((PERM_ROWS, PERM_ROWS)), row(AT_W)],
        out_specs=[row(AT_W), row(STAT_W), row(AT_W)],
        out_shape=[jax.ShapeDtypeStruct((S, AT_W), F32), jax.ShapeDtypeStruct((S, STAT_W), F32),
                   jax.ShapeDtypeStruct((S, AT_W), BF16)],
        scratch_shapes=[_slab_scratch(tm, STAT_W), _slab_scratch(tm, STAT_W)],
        compiler_params=_params(("parallel",)),
    )(os_d[0].reshape(S, AT_W), os_d[1], os_d[2], ls_d[0].reshape(S, STAT_W), ls_d[1], ls_d[2],
      _perm_matrix(4), _perm_matrix(16), z_at)


def _att_gate_bwd(dm_at, o_at, lse, z_at):
    S = z_at.shape[0]
    tm = MIX_TM

    def body(dm_ref, o_ref, l_ref, z_ref, do_ref, do4_ref, do16_ref, dl_ref, dl4_ref, dl16_ref, dz_ref, do_scr, dl_scr):
        o = o_ref[...]
        z = z_ref[...]
        dm = dm_ref[...]
        sz = _sigmoid(z)
        dz_ref[...] = (dm * o * (sz * (1.0 + z * (1.0 - sz)))).astype(BF16)
        do = dm * (z * sz)
        do_ref[...] = do.astype(BF16)
        gather = (_o_head(_iota((AT_W, STAT_W), 0)) == _stat_head(_iota((AT_W, STAT_W), 1))).astype(BF16)
        dl = jnp.where(_stat_lane(_iota((tm, STAT_W), 1)) < STAT_LSE_LANE, _dot_sel(do * o, gather), l_ref[...])
        dl_ref[...] = dl
        _to_slabs(do, do_scr)
        _to_slabs(dl, dl_scr)
        _split_residues(do_scr, do4_ref, 4, BF16)
        _split_residues(do_scr, do16_ref, 16, BF16)
        _split_residues(dl_scr, dl4_ref, 4, F32)
        _split_residues(dl_scr, dl16_ref, 16, F32)

    row = lambda w: pl.BlockSpec((tm, w), lambda s: (s, 0))
    sds = jax.ShapeDtypeStruct
    return pl.pallas_call(
        body, name="att_gate_bwd", grid=(S // tm,),
        in_specs=[row(AT_W), row(AT_W), row(STAT_W), row(AT_W)],
        out_specs=[row(AT_W), _dil_spec(4, tm, AT_W), _dil_spec(16, tm, AT_W),
                   row(STAT_W), _dil_spec(4, tm, STAT_W), _dil_spec(16, tm, STAT_W), row(AT_W)],
        out_shape=[sds((S, AT_W), BF16), sds((4, S // 4, AT_W), BF16), sds((16, S // 16, AT_W), BF16),
                   sds((S, STAT_W), F32), sds((4, S // 4, STAT_W), F32), sds((16, S // 16, STAT_W), F32),
                   sds((S, AT_W), BF16)],
        scratch_shapes=[_slab_scratch(tm, AT_W), _slab_scratch(tm, STAT_W)],
        compiler_params=_params(("parallel",)),
    )(dm_at, o_at, lse, z_at)


def _att_bwd(qkv_d, do_d, dl_d):
    d, L, _ = qkv_d.shape
    nb = L // ATT_BLK
    qb = ATT_QB
    rows = qb * ATT_BLK
    nsteps = L // rows

    def body(qc_ref, qn_ref, kp_ref, kc_ref, vp_ref, vc_ref, ac_ref, an_ref, lc_ref, ln_ref, bq_ref, bk_ref, dqkv_ref,
             s_scr, dp_scr, st_scr, dpt_scr, ds_scr, pt_scr, dst_scr):
        n = pl.program_id(1)
        first = _iota((ATT_BLK, HEAD_PAIR), 1) < AT_HEAD
        pairs = [slice(HEAD_PAIR * hp, HEAD_PAIR * (hp + 1)) for hp in range(4)]
        blk = lambda i: slice(ATT_BLK * i, ATT_BLK * (i + 1))

        def stack(t):
            zero = jnp.zeros_like(t)
            return jnp.concatenate([jnp.where(first, t, zero), jnp.where(first, zero, t)], axis=0)

        def unstack(t2):
            return jnp.where(first, t2[:ATT_BLK], t2[ATT_BLK:])

        def with_prev(i, sl, prev_ref, cur_ref):
            before = prev_ref[0, :, sl] if i == 0 else cur_ref[0, blk(i - 1), sl]
            return jnp.concatenate([before, cur_ref[0, blk(i), sl]], axis=0)

        def with_next(i, sl, cur_ref, next_ref):
            after = next_ref[0, :, sl] if i == qb - 1 else cur_ref[0, blk(i + 1), sl]
            return jnp.concatenate([cur_ref[0, blk(i), sl], after], axis=0)

        for i in range(qb):
            for hp, sl in enumerate(pairs):
                j = 4 * i + hp
                s_scr[j] = _dot_nt(stack(qc_ref[0, blk(i), sl] * ATT_SCALE), with_prev(i, sl, kp_ref, kc_ref))
                dp_scr[j] = _dot_nt(stack(ac_ref[0, blk(i), sl]), with_prev(i, sl, vp_ref, vc_ref))
                st_scr[j] = _dot_nt(stack(kc_ref[0, blk(i), sl] * ATT_SCALE), with_next(i, sl, qc_ref, qn_ref))
                dpt_scr[j] = _dot_nt(stack(vc_ref[0, blk(i), sl]), with_next(i, sl, ac_ref, an_ref))
        for i in range(qb):
            bias = bq_ref[jnp.minimum(n, 1)] if i == 0 else bq_ref[1]
            bias_t = bk_ref[jnp.minimum(nsteps - 1 - n, 1)] if i == qb - 1 else bk_ref[1]
            bias2 = jnp.concatenate([bias, bias], axis=0)
            bias_t2 = jnp.concatenate([bias_t, bias_t], axis=0)
            dl_c = lc_ref[0, blk(i), :]
            dl_t = with_next(i, slice(None), lc_ref, ln_ref).T
            for hp in range(4):
                j = 4 * i + hp
                at = [STAT_LANES * (2 * hp), STAT_LANES * (2 * hp + 1)]
                col = lambda t, o: jnp.concatenate([t[:, a + o: a + o + 1] for a in at], axis=0)
                p = jnp.exp(s_scr[j] + bias2 - col(dl_c, STAT_LSE_LANE))
                ds_scr[j] = (p * (dp_scr[j] - col(dl_c, 0))).astype(BF16)
                row = lambda t, o: jnp.concatenate([jnp.broadcast_to(t[a + o: a + o + 1, :], (ATT_BLK, 2 * ATT_BLK)) for a in at], axis=0)
                pt = jnp.exp(st_scr[j] + bias_t2 - row(dl_t, STAT_LSE_LANE))
                pt_scr[j] = pt.astype(BF16)
                dst_scr[j] = (pt * (dpt_scr[j] - row(dl_t, 0))).astype(BF16)
        for i in range(qb):
            for hp, sl in enumerate(pairs):
                j = 4 * i + hp
                dq = unstack(_dot(ds_scr[j], with_prev(i, sl, kp_ref, kc_ref))) * ATT_SCALE
                dk = unstack(_dot(dst_scr[j], with_next(i, sl, qc_ref, qn_ref))) * ATT_SCALE
                dv = unstack(_dot(pt_scr[j], with_next(i, sl, ac_ref, an_ref)))
                dqkv_ref[0, blk(i), sl] = dq.astype(BF16)
                dqkv_ref[0, blk(i), AT_W + HEAD_PAIR * hp: AT_W + HEAD_PAIR * (hp + 1)] = dk.astype(BF16)
                dqkv_ref[0, blk(i), 2 * AT_W + HEAD_PAIR * hp: 2 * AT_W + HEAD_PAIR * (hp + 1)] = dv.astype(BF16)

    cur = lambda j: pl.BlockSpec((1, rows, AT_W), lambda r, n: (r, n, j))
    prev = lambda j: pl.BlockSpec((1, ATT_BLK, AT_W), lambda r, n: (r, jnp.maximum(qb * n - 1, 0), j))
    nxt_blk = lambda n: jnp.minimum(qb * (n + 1), nb - 1)
    sq = (4 * qb, 2 * ATT_BLK, 2 * ATT_BLK)
    return pl.pallas_call(
        body, name=f"att_bwd_d{d}", grid=(d, nsteps),
        in_specs=[cur(0), pl.BlockSpec((1, ATT_BLK, AT_W), lambda r, n: (r, nxt_blk(n), 0)), prev(1), cur(1), prev(2), cur(2),
                  pl.BlockSpec((1, rows, AT_W), lambda r, n: (r, n, 0)),
                  pl.BlockSpec((1, ATT_BLK, AT_W), lambda r, n: (r, nxt_blk(n), 0)),
                  pl.BlockSpec((1, rows, STAT_W), lambda r, n: (r, n, 0)),
                  pl.BlockSpec((1, ATT_BLK, STAT_W), lambda r, n: (r, nxt_blk(n), 0)),
                  _full((2, ATT_BLK, 2 * ATT_BLK)), _full((2, ATT_BLK, 2 * ATT_BLK))],
        out_specs=pl.BlockSpec((1, rows, 3 * AT_W), lambda r, n: (r, n, 0)),
        out_shape=jax.ShapeDtypeStruct((d, L, 3 * AT_W), BF16),
        scratch_shapes=[pltpu.VMEM(sq, F32)] * 4 + [pltpu.VMEM(sq, BF16)] * 3,
        compiler_params=_params(("parallel", "parallel")),
    )(qkv_d, qkv_d, qkv_d, qkv_d, qkv_d, qkv_d, do_d, do_d, dl_d, dl_d, *_att_bias())


def _att_bwd_combine(dqkvs, cos2, sin2):
    S = dqkvs[0].shape[1]
    tm = MIX_TM

    def body(a_ref, b4_ref, c16_ref, p4_ref, p16_ref, cos_ref, sin_ref, dq_ref, dk_ref, dv_ref):
        t = a_ref[...].astype(F32) + _merge_bf16(b4_ref, p4_ref, 4) + _merge_bf16(c16_ref, p16_ref, 16)
        dy = t[:, : 2 * AT_W]
        cosf = jnp.tile(cos_ref[...], (1, 8))
        sinf = jnp.tile(sin_ref[...], (1, 8))
        dx = dy * cosf - _rope_rot(dy) * sinf
        dq_ref[...] = dx[:, :AT_W].astype(BF16)
        dk_ref[...] = dx[:, AT_W:].astype(BF16)
        dv_ref[...] = t[:, 2 * AT_W:].astype(BF16)

    row = lambda w: pl.BlockSpec((tm, w), lambda s: (s, 0))
    act = jax.ShapeDtypeStruct((S, AT_W), BF16)
    return pl.pallas_call(
        body, name="att_bwd_combine", grid=(S // tm,),
        in_specs=[row(3 * AT_W), _dil_spec(4, tm, 3 * AT_W), _dil_spec(16, tm, 3 * AT_W),
                  _full((PERM_ROWS, PERM_ROWS)), _full((PERM_ROWS, PERM_ROWS)), row(128), row(128)],
        out_specs=[row(AT_W), row(AT_W), row(AT_W)],
        out_shape=[act, act, act],
        compiler_params=_params(("parallel",)),
    )(dqkvs[0].reshape(S, 3 * AT_W), dqkvs[1], dqkvs[2], _perm_matrix(4), _perm_matrix(16), cos2, sin2)


def _outproj(x2, tgt2, mix_at, w_out_full, fnw, o_hg, p1, hg_norm_w):
    S = x2.shape[0]
    tm = PROJ_TM
    ns = S // tm

    def body(x_ref, t_ref, ma_ref, w_ref, fw_ref, o_ref, z_ref, hgw_ref,
             dh_ref, doh_ref, dzh_ref, dma_ref, gw_ref, gfw_ref, ghgw_ref, loss_ref):
        s = pl.program_id(0)

        @pl.when(s == 0)
        def _():
            gw_ref[...] = jnp.zeros_like(gw_ref)
            gfw_ref[...] = jnp.zeros_like(gfw_ref)
            ghgw_ref[...] = jnp.zeros_like(ghgw_ref)
            loss_ref[...] = jnp.zeros_like(loss_ref)

        heads = [slice(HG_HEAD * hh, HG_HEAD * (hh + 1)) for hh in range(HG_HEADS)]
        norm = []
        for sl in heads:
            o, z = o_ref[:, sl], z_ref[:, sl]
            rs = lax.rsqrt(jnp.mean(o * o, axis=-1, keepdims=True) + EPS)
            norm.append((rs, o * rs, _sigmoid(z)))
        mh = jnp.concatenate([(oh * hgw_ref[:, sl] * (z_ref[:, sl] * sz)).astype(BF16)
                              for sl, (_, oh, sz) in zip(heads, norm)], axis=1)
        y = _dot(mh, w_ref[:HG_W, :]) + _dot(ma_ref[...], w_ref[HG_W:, :])
        h = x_ref[...] + y
        r = lax.rsqrt(jnp.mean(h * h, axis=-1, keepdims=True) + EPS)
        hn = h * r
        fw = fw_ref[...]
        err = hn * fw - t_ref[...]
        loss_ref[...] += 0.5 * jnp.sum(jnp.mean(err * err, axis=-1, keepdims=True))
        dout = err * (1.0 / D_MODEL)
        gfw_ref[...] += jnp.sum(dout * hn, axis=0, keepdims=True)
        dhn = dout * fw
        dh = r * (dhn - hn * jnp.mean(dhn * hn, axis=-1, keepdims=True))
        dh_ref[...] = dh
        dhb = dh.astype(BF16)
        dma_ref[...] = _dot_nt(dhb, w_ref[HG_W:, :])
        dmh = _dot_nt(dhb, w_ref[:HG_W, :])
        for sl, (rs, oh, sz) in zip(heads, norm):
            z, dmix, gwv = z_ref[:, sl], dmh[:, sl], hgw_ref[:, sl]
            dzh_ref[:, sl] = (dmix * (oh * gwv) * (sz * (1.0 + z * (1.0 - sz)))).astype(BF16)
            don = dmix * (z * sz)
            ghgw_ref[:, sl] += jnp.sum(don * oh, axis=0, keepdims=True)
            dy = don * gwv
            doh_ref[:, sl] = rs * (dy - oh * jnp.mean(dy * oh, axis=-1, keepdims=True))
        gw_ref[:HG_W, :] += _dot_tn(mh, dhb)
        gw_ref[HG_W:, :] += _dot_tn(ma_ref[...], dhb)

    row = lambda w: pl.BlockSpec((tm, w), lambda s: (s, 0))
    return pl.pallas_call(
        body, name="outproj", grid=(ns,),
        in_specs=[row(D_MODEL), row(D_MODEL), row(AT_W),
                  _full((D_MODEL, D_MODEL)), _full((1, D_MODEL)),
                  row(HG_W), pl.BlockSpec((tm, HG_W), lambda s: (s, 1)), _full((1, HG_W))],
        out_specs=[row(D_MODEL), row(HG_W), row(HG_W), row(AT_W), _full((D_MODEL, D_MODEL)), _full((1, D_MODEL)),
                   _full((1, HG_W)), _full((8, 128))],
        out_shape=[jax.ShapeDtypeStruct((S, D_MODEL), F32), jax.ShapeDtypeStruct((S, HG_W), F32),
                   jax.ShapeDtypeStruct((S, HG_W), BF16), jax.ShapeDtypeStruct((S, AT_W), F32),
                   jax.ShapeDtypeStruct((D_MODEL, D_MODEL), F32), jax.ShapeDtypeStruct((1, D_MODEL), F32),
                   jax.ShapeDtypeStruct((1, HG_W), F32), jax.ShapeDtypeStruct((8, 128), F32)],
        compiler_params=_params(("arbitrary",)),
    )(x2, tgt2, mix_at, w_out_full, fnw, o_hg, p1, hg_norm_w)


def _inproj_bwd_x(dps, w_in_full, x2, norm_w, dh):
    S = x2.shape[0]
    tm = PROJ_TM

    def body(d0, d1, d2, d3, d4, d5, d6, d7, w_ref, x_ref, nw_ref, dh_ref, gx_ref, gnw_ref):
        s = pl.program_id(0)

        @pl.when(s == 0)
        def _():
            gnw_ref[...] = jnp.zeros_like(gnw_ref)

        du = jnp.zeros((tm, D_MODEL), F32)
        for i, dref in enumerate((d0, d1, d2, d3, d4, d5, d6, d7)):
            j, half = divmod(i, 2)
            du = du + _dot_nt(dref[...], w_ref[j, :, 512 * half: 512 * (half + 1)])
        x = x_ref[...]
        r = lax.rsqrt(jnp.mean(x * x, axis=-1, keepdims=True) + EPS)
        xh = x * r
        gnw_ref[...] += jnp.sum(du * xh, axis=0, keepdims=True)
        dun = du * nw_ref[...]
        gx_ref[...] = dh_ref[...] + r * (dun - xh * jnp.mean(dun * xh, axis=-1, keepdims=True))

    row = lambda w: pl.BlockSpec((tm, w), lambda s: (s, 0))
    return pl.pallas_call(
        body, name="inproj_bwd_x", grid=(S // tm,),
        in_specs=[row(512)] * 8 + [_full((4, D_MODEL, 1024)), row(D_MODEL), _full((1, D_MODEL)), row(D_MODEL)],
        out_specs=[row(D_MODEL), _full((1, D_MODEL))],
        out_shape=[jax.ShapeDtypeStruct((S, D_MODEL), F32), jax.ShapeDtypeStruct((1, D_MODEL), F32)],
        compiler_params=_params(("arbitrary",)),
    )(*dps, w_in_full, x2, norm_w, dh)


def _inproj_bwd_w(u, dps):
    S = u.shape[0]
    tm = PROJ_TM

    def body(u_ref, d0, d1, d2, d3, d4, d5, d6, d7, g_ref):
        @pl.when(pl.program_id(0) == 0)
        def _():
            g_ref[...] = jnp.zeros_like(g_ref)

        ub = u_ref[...]
        for i, dref in enumerate((d0, d1, d2, d3, d4, d5, d6, d7)):
            j, half = divmod(i, 2)
            g_ref[j, :, 512 * half: 512 * (half + 1)] += _dot_tn(ub, dref[...])

    return pl.pallas_call(
        body, name="inproj_bwd_w", grid=(S // tm,),
        in_specs=[pl.BlockSpec((tm, D_MODEL), lambda s: (s, 0))] + [pl.BlockSpec((tm, 512), lambda s: (s, 0))] * 8,
        out_specs=_full((4, D_MODEL, 1024)),
        out_shape=jax.ShapeDtypeStruct((4, D_MODEL, 1024), F32),
        compiler_params=_params(("arbitrary",)),
    )(u, *dps)


def _adamw_update(gg, w_ref, m_ref, v_ref, d_ref, nm_ref, nv_ref):
    nm = ADAM_B1 * m_ref[...] + (1.0 - ADAM_B1) * gg
    nv = ADAM_B2 * v_ref[...] + (1.0 - ADAM_B2) * (gg * gg)
    m_hat = nm / (1.0 - ADAM_B1 ** ADAM_STEP)
    v_hat = nv / (1.0 - ADAM_B2 ** ADAM_STEP)
    d_ref[...] = -ADAM_LR * (m_hat / (jnp.sqrt(v_hat) + ADAM_EPS) + ADAM_WD * w_ref[...])
    nm_ref[...] = nm
    nv_ref[...] = nv


def _adamw(w, g, m, v, name):
    rows, cols = w.shape
    tr = min(rows, 256)

    def body(w_ref, g_ref, m_ref, v_ref, d_ref, nm_ref, nv_ref):
        _adamw_update(g_ref[...], w_ref, m_ref, v_ref, d_ref, nm_ref, nv_ref)

    spec = pl.BlockSpec((tr, cols), lambda i: (i, 0))
    sds = jax.ShapeDtypeStruct((rows, cols), F32)
    return pl.pallas_call(
        body, name=name, grid=(rows // tr,),
        in_specs=[spec] * 4, out_specs=[spec] * 3, out_shape=[sds] * 3,
        compiler_params=_params(("parallel",)),
    )(w, g, m, v)


def _place():
    return lax.axis_index("x"), lax.axis_index("y"), lax.axis_index("c")


def _gather_weights(w_in_s, w_out_s):
    def body(win_ref, wout_ref, fin_ref, fout_ref, send_sems, recv_sems):
        x, y, c = _place()
        me = (x, y, c)
        sib = (x, y, 1 - c)
        mine = 2 * x + y
        fin_ref[mine] = win_ref[...].astype(BF16)
        fout_ref[mine] = wout_ref[...].astype(BF16)
        chips = [(1 - x, y), (x, 1 - y), (1 - x, 1 - y)]

        def halves(chip, half):
            return (fin_ref.at[chip, pl.ds(half * 512, 512), :], fout_ref.at[chip, pl.ds(half * 128, 128), :])

        def copy(k, ref, to):
            return pltpu.make_async_remote_copy(src_ref=ref, dst_ref=ref, send_sem=send_sems.at[k],
                                                recv_sem=recv_sems.at[k], device_id=to, device_id_type=MESH)

        first, passed = [], []
        for j, (cx, cy) in enumerate(chips):
            for a, ref in enumerate(halves(mine, c)):
                first.append(copy(2 * j + a, ref, (cx, cy, c)))
        for cp in first:
            cp.start()
        for j, (cx, cy) in enumerate(chips):
            for a, ref in enumerate(halves(2 * cx + cy, c)):
                copy(2 * j + a, ref, me).wait_recv()
                fwd = copy(6 + 2 * j + a, ref, sib)
                fwd.start()
                passed.append(fwd)
        for j, (cx, cy) in enumerate(chips):
            for a, ref in enumerate(halves(2 * cx + cy, 1 - c)):
                copy(6 + 2 * j + a, ref, me).wait_recv()
        for cp in first + passed:
            cp.wait_send()

    vm = pl.BlockSpec(memory_space=pltpu.VMEM)
    return pl.pallas_call(
        body, name="gather_weights",
        in_specs=[vm, vm], out_specs=[vm, vm],
        out_shape=[jax.ShapeDtypeStruct((4, D_MODEL, 1024), BF16), jax.ShapeDtypeStruct((4, 256, D_MODEL), BF16)],
        scratch_shapes=[pltpu.SemaphoreType.DMA((12,)), pltpu.SemaphoreType.DMA((12,))],
        compiler_params=pltpu.CompilerParams(vmem_limit_bytes=VMEM_LIMIT),
    )(w_in_s, w_out_s)


def _swap_halves(g_in, g_out):
    def body(gin_ref, gout_ref, rin_ref, rout_ref, send_sems, recv_sems):
        x, y, c = _place()
        sib = (x, y, 1 - c)
        cps = [pltpu.make_async_remote_copy(src_ref=src.at[:, 1 - c], dst_ref=dst, send_sem=send_sems.at[k],
                                            recv_sem=recv_sems.at[k], device_id=sib, device_id_type=MESH)
               for k, (src, dst) in enumerate(((gin_ref, rin_ref), (gout_ref, rout_ref)))]
        for cp in cps:
            cp.start()
        for cp in cps:
            cp.wait()

    hbm = pl.BlockSpec(memory_space=pl.ANY)
    return pl.pallas_call(
        body, name="swap_halves",
        in_specs=[hbm, hbm], out_specs=[hbm, hbm],
        out_shape=[jax.ShapeDtypeStruct((4,) + g.shape[2:], F32) for g in (g_in, g_out)],
        scratch_shapes=[pltpu.SemaphoreType.DMA((2,)), pltpu.SemaphoreType.DMA((2,))],
    )(g_in, g_out)


def _add_half(g, r, cidx, name):
    n, _, rows, cols = g.shape

    def body(c_ref, g_ref, r_ref, o_ref):
        o_ref[0] = (g_ref[0, 0] + r_ref[0]).astype(BF16)

    return pl.pallas_call(
        body, name=name,
        grid_spec=pltpu.PrefetchScalarGridSpec(
            num_scalar_prefetch=1, grid=(n,),
            in_specs=[pl.BlockSpec((1, 1, rows, cols), lambda j, c_ref: (j, c_ref[0], 0, 0)),
                      pl.BlockSpec((1, rows, cols), lambda j, c_ref: (j, 0, 0))],
            out_specs=pl.BlockSpec((1, rows, cols), lambda j, c_ref: (j, 0, 0))),
        out_shape=jax.ShapeDtypeStruct((n, rows, cols), BF16),
        compiler_params=_params(("parallel",)),
    )(cidx, g, r)


def _exchange_copies(in_ref, out_ref, lin_ref, lout_ref, send_sems, recv_sems):
    x, y, c = _place()
    cps = []
    for k, (cx, cy) in enumerate([(1 - x, y), (x, 1 - y), (1 - x, 1 - y)]):
        for a, (src, dst) in enumerate(((in_ref, lin_ref), (out_ref, lout_ref))):
            cps.append(pltpu.make_async_remote_copy(
                src_ref=src.at[2 * cx + cy], dst_ref=dst.at[k], send_sem=send_sems.at[2 * k + a],
                recv_sem=recv_sems.at[2 * k + a], device_id=(cx, cy, c), device_id_type=MESH))
    return cps


def _exchange_start(cs_in, cs_out):
    def body(in_ref, out_ref, lin_ref, lout_ref, send_sems, recv_sems, in_thru, out_thru, lin_thru, lout_thru, token):
        for cp in _exchange_copies(in_ref, out_ref, lin_ref, lout_ref, send_sems, recv_sems):
            cp.start()
        token[...] = jnp.zeros_like(token)

    lands = [lax.empty((3,) + a.shape[1:], a.dtype) for a in (cs_in, cs_out)]
    bufs = [pltpu.with_memory_space_constraint(a, pltpu.HBM) for a in (cs_in, cs_out, *lands)]
    hbm = pl.BlockSpec(memory_space=pltpu.HBM)
    sem = pl.BlockSpec(memory_space=pltpu.SEMAPHORE)
    return pl.pallas_call(
        body, name="exchange_start",
        in_specs=[hbm] * 4,
        out_specs=[sem, sem, hbm, hbm, hbm, hbm, pl.BlockSpec(memory_space=pltpu.VMEM)],
        out_shape=[pltpu.SemaphoreType.DMA((6,)), pltpu.SemaphoreType.DMA((6,))]
                  + [pltpu.HBM(b.shape, b.dtype) for b in bufs] + [jax.ShapeDtypeStruct((8, 128), F32)],
        input_output_aliases={0: 2, 1: 3, 2: 4, 3: 5},
        compiler_params=pltpu.CompilerParams(has_side_effects=pltpu.SideEffectType.DATAFLOW_SIDE_EFFECTING),
    )(*bufs)


def _exchange_wait(send_sems, recv_sems, in_thru, out_thru, lin_thru, lout_thru, after):
    def body(in_ref, out_ref, lin_ref, lout_ref, send_sems, recv_sems, after_ref, in_dead, out_dead, got_in, got_out):
        for cp in _exchange_copies(in_ref, out_ref, lin_ref, lout_ref, send_sems, recv_sems):
            cp.wait_send()
            cp.wait_recv()

    hbm = pl.BlockSpec(memory_space=pltpu.HBM)
    sem = pl.BlockSpec(memory_space=pltpu.SEMAPHORE)
    bufs = (in_thru, out_thru, lin_thru, lout_thru)
    return pl.pallas_call(
        body, name="exchange_wait",
        in_specs=[hbm] * 4 + [sem, sem, pl.BlockSpec(memory_space=pl.ANY)],
        out_specs=[hbm] * 4,
        out_shape=[pltpu.HBM(b.shape, b.dtype) for b in bufs],
        input_output_aliases={0: 0, 1: 1, 2: 2, 3: 3},
        compiler_params=pltpu.CompilerParams(has_side_effects=pltpu.SideEffectType.DATAFLOW_SIDE_EFFECTING),
    )(*bufs, send_sems, recv_sems, after)


PEER_XOR = (2, 1, 3)


def _sum_chips(cs, r, chip_idx, name):
    _, rows, cols = r.shape
    tr = min(rows, 256)

    def body(m_ref, cs_ref, r_ref, o_ref):
        mine = m_ref[0]
        own = cs_ref[0].astype(F32)
        got = [r_ref[k].astype(F32) for k in range(3)]
        acc = None
        for s in range(4):
            rel = mine ^ s
            term = jnp.where(rel == 0, own, jnp.where(rel == PEER_XOR[0], got[0],
                                                      jnp.where(rel == PEER_XOR[1], got[1], got[2])))
            acc = term if acc is None else acc + term
        o_ref[...] = acc

    return pl.pallas_call(
        body, name=name,
        grid_spec=pltpu.PrefetchScalarGridSpec(
            num_scalar_prefetch=1, grid=(rows // tr,),
            in_specs=[pl.BlockSpec((1, tr, cols), lambda i, m_ref: (m_ref[0], i, 0)),
                      pl.BlockSpec((3, tr, cols), lambda i, m_ref: (0, i, 0))],
            out_specs=pl.BlockSpec((tr, cols), lambda i, m_ref: (i, 0))),
        out_shape=jax.ShapeDtypeStruct((rows, cols), F32),
        compiler_params=_params(("parallel",)),
    )(chip_idx, cs, r)


def _swap_reduced(h_in, h_out):
    def body(in_ref, out_ref, rin_ref, rout_ref, send_sems, recv_sems):
        x, y, c = _place()
        cps = [pltpu.make_async_remote_copy(src_ref=src, dst_ref=dst, send_sem=send_sems.at[k],
                                            recv_sem=recv_sems.at[k], device_id=(x, y, 1 - c), device_id_type=MESH)
               for k, (src, dst) in enumerate(((in_ref, rin_ref), (out_ref, rout_ref)))]
        for cp in cps:
            cp.start()
        for cp in cps:
            cp.wait()

    hbm = pl.BlockSpec(memory_space=pl.ANY)
    return pl.pallas_call(
        body, name="swap_reduced",
        in_specs=[hbm, hbm], out_specs=[hbm, hbm],
        out_shape=[jax.ShapeDtypeStruct(h.shape, F32) for h in (h_in, h_out)],
        scratch_shapes=[pltpu.SemaphoreType.DMA((2,)), pltpu.SemaphoreType.DMA((2,))],
    )(h_in, h_out)


def _adamw_halves(w, mine, theirs, m, v, cidx, name):
    rows, cols = w.shape
    half = rows // 2
    tr = min(half, 256)
    nbh = half // tr

    def body(c_ref, w_ref, a_ref, b_ref, m_ref, v_ref, g_ref, d_ref, nm_ref, nv_ref):
        gg = jnp.where(pl.program_id(0) // nbh == c_ref[0], a_ref[...], b_ref[...])
        g_ref[...] = gg
        _adamw_update(gg, w_ref, m_ref, v_ref, d_ref, nm_ref, nv_ref)

    spec = pl.BlockSpec((tr, cols), lambda i, c_ref: (i, 0))
    hspec = pl.BlockSpec((tr, cols), lambda i, c_ref: (i % nbh, 0))
    sds = jax.ShapeDtypeStruct((rows, cols), F32)
    return pl.pallas_call(
        body, name=name,
        grid_spec=pltpu.PrefetchScalarGridSpec(
            num_scalar_prefetch=1, grid=(rows // tr,),
            in_specs=[spec, hspec, hspec, spec, spec], out_specs=[spec] * 4),
        out_shape=[sds] * 4,
        compiler_params=_params(("parallel",)),
    )(cidx, w, mine, theirs, m, v)


def _allreduce_small(g_nw, g_fw, g_hgw, g_lbl, loss8):
    def body(nw_ref, fw_ref, hgw_ref, lbl_ref, loss_ref, out_ref, slots, send_sems, recv_sems):
        x, y, c = _place()
        me = 4 * x + 2 * y + c
        slots[me] = jnp.zeros((8, D_MODEL), F32)
        slots[me, 0:1, :] = nw_ref[...]
        slots[me, 1:2, :] = fw_ref[...]
        slots[me, 2:3, 0:HG_W] = hgw_ref[...]
        slots[me, 3:4, 0:HG_W] = lbl_ref[0:1, :]
        slots[me, 3:4, HG_W:] = lbl_ref[1:2, :]
        slots[me, 4:5, 0:128] = loss_ref[0:1, :]
        cps = []
        for k in range(1, 8):
            dx, dy, dc = (k >> 2) & 1, (k >> 1) & 1, k & 1
            to = (x ^ dx, y ^ dy, c ^ dc)
            cps.append(pltpu.make_async_remote_copy(
                src_ref=slots.at[me], dst_ref=slots.at[me], send_sem=send_sems.at[k - 1],
                recv_sem=recv_sems.at[k - 1], device_id=to, device_id_type=MESH))
        for cp in cps:
            cp.start()
        for cp in cps:
            cp.wait()
        acc = slots[0]
        for i in range(1, 8):
            acc = acc + slots[i]
        out_ref[...] = acc

    vm = pl.BlockSpec(memory_space=pltpu.VMEM)
    return pl.pallas_call(
        body, name="allreduce_small",
        in_specs=[vm] * 5, out_specs=vm,
        out_shape=jax.ShapeDtypeStruct((8, D_MODEL), F32),
        scratch_shapes=[pltpu.VMEM((8, 8, D_MODEL), F32), pltpu.SemaphoreType.DMA((7,)), pltpu.SemaphoreType.DMA((7,))],
    )(g_nw, g_fw, g_hgw, g_lbl, loss8)


def _rope_tables(S):
    inv_freq = (np.float32(1.0) / np.power(np.float32(ROPE_THETA), np.arange(ROPE_HALF, dtype=np.float32) / np.float32(ROPE_HALF))).astype(np.float32)
    ang = (np.arange(S, dtype=np.float32)[:, None] * inv_freq[None, :]).astype(np.float32)
    cos, sin = np.cos(ang).astype(np.float32), np.sin(ang).astype(np.float32)
    cos2 = np.concatenate([cos, cos, cos, cos], axis=-1)
    sin2 = np.concatenate([-sin, sin, -sin, sin], axis=-1)
    return jnp.asarray(cos2), jnp.asarray(sin2)


def _local_step(x2, tgt2, norm_w, w_in_full, lbl, hg_norm_w, w_out4, fnw):
    S = x2.shape[0]
    cos2, sin2 = _rope_tables(S)
    w_out_full = w_out4.reshape(D_MODEL, D_MODEL)
    p0, p1, qkv, qkv4, qkv16, z_at, u = _inproj(x2, norm_w, w_in_full, cos2, sin2)
    o_hg, states = _hg_fwd(p0, p1, lbl)
    qkv_ds = [qkv.reshape(1, S, 3 * AT_W), qkv4, qkv16]
    os_d, ls_d = zip(*[_att_fwd(q) for q in qkv_ds])
    o_at, lse, mix_at = _att_combine(os_d, ls_d, z_at)
    dh, do_hg, dz_hg, dm_at, g_wout, g_fw, g_hgw, loss8 = _outproj(x2, tgt2, mix_at, w_out_full, fnw, o_hg, p1, hg_norm_w)
    dqr, dfl, dv_hg, g_lbl = _hg_bwd(p0, p1, do_hg, states, lbl)
    do1, do4, do16, dl1, dl4, dl16, dz_at = _att_gate_bwd(dm_at, o_at, lse, z_at)
    do_ds = [do1.reshape(1, S, AT_W), do4, do16]
    dl_ds = [dl1.reshape(1, S, STAT_W), dl4, dl16]
    dqkvs = [_att_bwd(q, a, b) for q, a, b in zip(qkv_ds, do_ds, dl_ds)]
    dq_at, dk_at, dv_at = _att_bwd_combine(dqkvs, cos2, sin2)
    dps = [dqr, dfl, dv_hg, dz_hg, dq_at, dk_at, dv_at, dz_at]
    return loss8, dps, u, dh, g_lbl, g_hgw, g_wout, g_fw


def kernel(x, norm_w, w_in, hgrn_lb_logits, hg_norm_w, w_out, final_norm_w, loss_target, m_norm_w, m_w_in, m_hgrn_lb_logits, m_hg_norm_w, m_w_out, m_final_norm_w, v_norm_w, v_w_in, v_hgrn_lb_logits, v_hg_norm_w, v_w_out, v_final_norm_w):
    S = x.shape[1]
    w_in_full, w_out_full = _gather_weights(w_in[0], w_out[0])
    loss8, dps, u, dh, g_lbl, g_hgw, g_wout, g_fw = _local_step(
        x[0], loss_target[0], norm_w, w_in_full, hgrn_lb_logits, hg_norm_w,
        w_out_full, final_norm_w.reshape(1, D_MODEL))

    cidx = lax.axis_index("c").astype(jnp.int32).reshape(1)
    g_win = _inproj_bwd_w(u, dps)
    g_in4 = g_win.reshape(4, 2, 512, 1024)
    g_out4 = g_wout.reshape(4, 2, 128, D_MODEL)
    r_in, r_out = _swap_halves(g_in4, g_out4)
    cs_in = _add_half(g_in4, r_in, cidx, "add_half_in")
    cs_out = _add_half(g_out4, r_out, cidx, "add_half_out")
    *started, token = _exchange_start(cs_in, cs_out)
    grad_x, g_nw = _inproj_bwd_x(dps, w_in_full, x[0], norm_w + token[0:1, 0:1], dh)
    cs_in, cs_out, x_in, x_out = _exchange_wait(*started, g_nw)
    chip_idx = (2 * lax.axis_index("x") + lax.axis_index("y")).astype(jnp.int32).reshape(1)
    h_in = _sum_chips(cs_in, x_in, chip_idx, "sum_chips_in")
    h_out = _sum_chips(cs_out, x_out, chip_idx, "sum_chips_out")
    s_in, s_out = _swap_reduced(h_in, h_out)

    red = _allreduce_small(g_nw, g_fw, g_hgw, g_lbl, loss8)
    loss = red[4, 0]
    grad_norm_w = red[0:1, :]
    grad_final_norm_w = red[1, :]
    grad_hg_norm_w = red[2:3, :HG_W]
    grad_lbl = jnp.concatenate([red[3:4, :HG_W], red[3:4, HG_W:]], axis=0)

    d_nw, m_nw, v_nw = _adamw(norm_w, grad_norm_w, m_norm_w, v_norm_w, "adamw_norm_w")
    grad_w_in, d_win, m_win, v_win = _adamw_halves(w_in[0], h_in, s_in, m_w_in[0], v_w_in[0], cidx, "adamw_w_in")
    d_lbl, m_lbl, v_lbl = _adamw(hgrn_lb_logits, grad_lbl, m_hgrn_lb_logits, v_hgrn_lb_logits, "adamw_lb_logits")
    d_hgw, m_hgw, v_hgw = _adamw(hg_norm_w, grad_hg_norm_w, m_hg_norm_w, v_hg_norm_w, "adamw_hg_norm_w")
    grad_w_out, d_wout, m_wout, v_wout = _adamw_halves(w_out[0], h_out, s_out, m_w_out[0], v_w_out[0], cidx, "adamw_w_out")
    d_fw, m_fw, v_fw = _adamw(final_norm_w.reshape(1, D_MODEL), grad_final_norm_w.reshape(1, D_MODEL),
                              m_final_norm_w.reshape(1, D_MODEL), v_final_norm_w.reshape(1, D_MODEL), "adamw_final_norm_w")
    e1 = lambda a: a[None]
    flat = lambda a: a.reshape(D_MODEL)
    return (loss, grad_x[None], grad_norm_w, e1(grad_w_in), grad_lbl, grad_hg_norm_w, e1(grad_w_out), grad_final_norm_w,
            d_nw, e1(d_win), d_lbl, d_hgw, e1(d_wout), flat(d_fw),
            m_nw, e1(m_win), m_lbl, m_hgw, e1(m_wout), flat(m_fw),
            v_nw, e1(v_win), v_lbl, v_hgw, e1(v_wout), flat(v_fw))
```

```python
import jax
import jax.numpy as jnp
import numpy as np
from jax import lax
from jax.experimental import pallas as pl
from jax.experimental.pallas import tpu as pltpu

F32 = jnp.float32
BF16 = jnp.bfloat16
MESH = pl.DeviceIdType.MESH

D_MODEL = 1024
HG_HEADS = 4
HG_HEAD = 128
HG_W = HG_HEADS * HG_HEAD
AT_HEAD = 64
AT_W = 512
HEAD_PAIR = 2 * AT_HEAD
ROPE_HALF = 32
ROPE_THETA = 10000.0
EPS = 1e-6
CHUNK = 128
LEVELS = (64, 32, 16, 8, 4, 2, 1)
DIAG = 1
SUBLANES = 8
ATT_BLK = 128
ATT_QB = 4
ATT_SCALE = 0.125
STAT_W = 128
STAT_LANES = 16
STAT_LSE_LANE = 8
NEG = -1e30
VMEM_LIMIT = 56 * 1024 * 1024
MIX_TM = 1024
PROJ_TM = 512

ADAM_LR = 0.001
ADAM_B1 = 0.9
ADAM_B2 = 0.999
ADAM_EPS = 1e-08
ADAM_WD = 0.01
ADAM_STEP = 10


def _iota(shape, dim):
    return lax.broadcasted_iota(jnp.int32, shape, dim)


def _dot(a, b):
    return jnp.dot(a, b, preferred_element_type=F32)


def _dot_nt(a, b):
    return lax.dot_general(a, b, (((1,), (1,)), ((), ())), preferred_element_type=F32)


def _dot_tn(a, b):
    return lax.dot_general(a, b, (((0,), (0,)), ((), ())), preferred_element_type=F32)


def _sigmoid(v):
    return 0.5 * jnp.tanh(0.5 * v) + 0.5


def _params(sem=None, vmem=VMEM_LIMIT):
    return pltpu.CompilerParams(dimension_semantics=sem, vmem_limit_bytes=vmem)


def _full(shape):
    n = len(shape)
    return pl.BlockSpec(shape, lambda *_: (0,) * n)


def _rope_rot(y):
    n = y.shape[1]
    first = (_iota(y.shape, 1) & (2 * ROPE_HALF - 1)) < ROPE_HALF
    return jnp.where(first, pltpu.roll(y, n - ROPE_HALF, 1), pltpu.roll(y, ROPE_HALF, 1))


def _dil_spec(d, tm, width):
    return pl.BlockSpec((d, tm // d, width), lambda s: (0, s, 0))


LANES = 128


def _slab_scratch(tm, width):
    return pltpu.VMEM((width // LANES, tm, LANES), F32)


def _to_slabs(v, slabs_ref):
    for j in range(slabs_ref.shape[0]):
        slabs_ref[j] = v[:, LANES * j: LANES * (j + 1)]


def _from_slabs(slabs_ref):
    return jnp.concatenate([slabs_ref[j] for j in range(slabs_ref.shape[0])], axis=1)


def _split_residues(slabs_ref, dst_ref, d, dtype):
    nslab, tm, _ = slabs_ref.shape
    for r in range(d):
        for j in range(nslab):
            dst_ref[r, :, LANES * j: LANES * (j + 1)] = slabs_ref[j, pl.ds(r, tm // d, stride=d), :].astype(dtype)


def _merge_residues(src_ref, slabs_ref, d):
    nslab, tm, _ = slabs_ref.shape
    for r in range(d):
        for j in range(nslab):
            slabs_ref[j, pl.ds(r, tm // d, stride=d), :] = src_ref[r, :, LANES * j: LANES * (j + 1)].astype(F32)


PERM_ROWS = 256


def _perm_matrix(d):
    n = PERM_ROWS // d
    t = np.arange(PERM_ROWS)
    p = np.zeros((PERM_ROWS, PERM_ROWS), np.float32)
    p[t, (t % d) * n + t // d] = 1.0
    return jnp.asarray(p, BF16)


def _merge_bf16(src_ref, perm_ref, d):
    n = PERM_ROWS // d
    tm = src_ref.shape[1] * d
    outs = []
    for b in range(tm // PERM_ROWS):
        flat = jnp.concatenate([src_ref[r, n * b: n * (b + 1), :] for r in range(d)], axis=0)
        outs.append(_dot(perm_ref[...], flat))
    return jnp.concatenate(outs, axis=0)


def _inproj(x2, norm_w, w_in_full, cos2, sin2):
    S = x2.shape[0]
    tm = PROJ_TM

    def body(x_ref, nw_ref, w_ref, cos_ref, sin_ref, p0_ref, p1_ref, qkv_ref, qkv4_ref, qkv16_ref, z_ref, u_ref, scr):
        x = x_ref[...]
        r = lax.rsqrt(jnp.mean(x * x, axis=-1, keepdims=True) + EPS)
        u = x * r * nw_ref[...]
        ub = u.astype(BF16)
        u_ref[...] = ub
        p0_ref[...] = _dot(ub, w_ref[0])
        p1_ref[...] = _dot(ub, w_ref[1])
        y2 = _dot(ub, w_ref[2])
        cosf = jnp.tile(cos_ref[...], (1, 8))
        sinf = jnp.tile(sin_ref[...], (1, 8))
        y3 = _dot(ub, w_ref[3])
        z_ref[...] = y3[:, AT_W:]
        qkv = jnp.concatenate([y2 * cosf + _rope_rot(y2) * sinf, y3[:, :AT_W]], axis=1)
        qkv_ref[...] = qkv.astype(BF16)
        _to_slabs(qkv, scr)
        _split_residues(scr, qkv4_ref, 4, BF16)
        _split_residues(scr, qkv16_ref, 16, BF16)

    row = lambda w: pl.BlockSpec((tm, w), lambda s: (s, 0))
    qkv_w = 3 * AT_W
    return pl.pallas_call(
        body, name="inproj", grid=(S // tm,),
        in_specs=[row(D_MODEL), _full((1, D_MODEL)), _full((4, D_MODEL, 1024)), row(128), row(128)],
        out_specs=[row(1024), row(1024), row(qkv_w), _dil_spec(4, tm, qkv_w), _dil_spec(16, tm, qkv_w), row(AT_W),
                   row(D_MODEL)],
        out_shape=[jax.ShapeDtypeStruct((S, 1024), F32), jax.ShapeDtypeStruct((S, 1024), F32),
                   jax.ShapeDtypeStruct((S, qkv_w), BF16), jax.ShapeDtypeStruct((4, S // 4, qkv_w), BF16),
                   jax.ShapeDtypeStruct((16, S // 16, qkv_w), BF16), jax.ShapeDtypeStruct((S, AT_W), F32),
                   jax.ShapeDtypeStruct((S, D_MODEL), BF16)],
        scratch_shapes=[_slab_scratch(tm, qkv_w)],
        compiler_params=_params(("parallel",)),
    )(x2, norm_w, w_in_full, cos2, sin2)


HG_HPS = 4
HG_CPS = 8
N_LEV = len(LEVELS)


def _hg_const_arrays():
    r = np.arange(CHUNK)[:, None]
    c = np.arange(CHUNK)[None, :]
    tris = np.stack([r >= c, r <= c])
    lm = [((r // (2 * m)) == (c // (2 * m))) & (r % (2 * m) >= m) & (c % (2 * m) < m) for m in LEVELS]
    dm = [(c == r - dl) & (r % DIAG >= dl) for dl in range(DIAG)]
    masks = np.stack(lm + dm)
    return jnp.asarray(tris, BF16), jnp.asarray(masks, F32)


def _split2(a):
    hi = a.astype(BF16)
    return hi, (a - hi.astype(F32)).astype(BF16)


def _dot3(a, b, dot=_dot):
    ah, al = _split2(a)
    bh, bl = _split2(b)
    n = b.shape[1]
    p = dot(ah, jnp.concatenate([bh, bl], axis=1))
    return (p[:, :n] + p[:, n:]) + dot(al, bh)


def _split3(a):
    a1 = a.astype(BF16)
    r1 = a - a1.astype(F32)
    a2 = r1.astype(BF16)
    return a1, a2, (r1 - a2.astype(F32)).astype(BF16)


def _tri_dot(tri, a):
    n = a.shape[1]
    p = _dot(tri, jnp.concatenate(_split3(a), axis=1))
    return (p[:, :n] + p[:, n:2 * n]) + p[:, 2 * n:]


def _dot_sel(a, sel):
    a1, a2, a3 = _split3(a)
    return (_dot(a1, sel) + _dot(a2, sel)) + _dot(a3, sel)


def _rowsum(t):
    return _dot(t.astype(BF16), jnp.ones((t.shape[1], t.shape[1]), BF16))


def _level_refs(b):
    refs = []
    pos = _iota(b.shape, 0)
    for m in LEVELS:
        if 2 * m >= SUBLANES:
            parts = [jnp.broadcast_to(b[r0 + m - 1: r0 + m, :], (2 * m, b.shape[1])) for r0 in range(0, CHUNK, 2 * m)]
            refs.append(parts[0] if len(parts) == 1 else jnp.concatenate(parts, axis=0))
        else:
            p = pos & (2 * m - 1)
            ref = b
            for off in range(-(m - 1), m + 1):
                if off != 0:
                    ref = jnp.where(p == m - 1 + off, pltpu.roll(b, off % CHUNK, 0), ref)
            refs.append(ref)
    return refs


def _hg_lb(lbl_ref):
    l0 = lbl_ref[0:1, :]
    l1 = lbl_ref[1:2, :]
    mx = jnp.maximum(l0, l1)
    e0 = jnp.exp(l0 - mx)
    e1 = jnp.exp(l1 - mx)
    p0 = e0 / (e0 + e1)
    lb = jnp.clip(p0, 1e-6, 1.0 - 1e-6)
    inside = (p0 >= 1e-6) & (p0 <= 1.0 - 1e-6)
    dlb_dl0 = jnp.where(inside, p0 * (e1 / (e0 + e1)), 0.0)
    return lb, dlb_dl0


def _hg_gates(qr, fl, lb):
    sig = _sigmoid(fl)
    f = lb + (1.0 - lb) * sig
    g = jnp.log(f)
    k = (1.0 - lb) * (1.0 - sig)
    sq = _sigmoid(qr)
    q = qr * sq
    return sig, f, g, k, sq, q


def _hg_levels(q, k, b, mk_ref):
    refs = _level_refs(b)
    a = jnp.zeros((CHUNK, CHUNK), F32)
    es, qts, kts = [], [], []
    for i in range(N_LEV):
        diff = b - refs[i]
        e = jnp.exp(jnp.minimum(diff, -diff))
        qt = (q * e).astype(BF16)
        kt = (k * e).astype(BF16)
        a = a + _dot_nt(qt, kt) * mk_ref[i]
        es.append(e); qts.append(qt); kts.append(kt)
    return a, es, qts, kts


def _hg_specs(nc, rev):
    cc = (lambda c: nc - 1 - c) if rev else (lambda c: c)
    w = HG_HEAD * HG_HPS
    blk = lambda off: pl.BlockSpec((HG_CPS * CHUNK, w), lambda h, c: (cc(c), h + off))
    lb2 = pl.BlockSpec((2, w), lambda h, c: (0, h))
    st = pl.BlockSpec((HG_CPS, HG_HPS, HG_HEAD, HG_HEAD), lambda h, c: (cc(c), h, 0, 0))
    consts = [_full((2, CHUNK, CHUNK)), _full((N_LEV + DIAG, CHUNK, CHUNK))]
    return blk, lb2, st, consts


def _hg_fwd(p0, p1, lbl):
    S = p0.shape[0]
    nc = S // (HG_CPS * CHUNK)
    ng = HG_HEADS // HG_HPS

    def body(qr_ref, fl_ref, v_ref, lbl_ref, tri_ref, mk_ref, o_ref, st_ref, state):
        c = pl.program_id(1)

        @pl.when(c == 0)
        def _():
            state[...] = jnp.zeros_like(state)

        lb_all, _ = _hg_lb(lbl_ref)
        heads = [slice(HG_HEAD * hh, HG_HEAD * (hh + 1)) for hh in range(HG_HPS)]
        for j in range(HG_CPS):
            rows = slice(CHUNK * j, CHUNK * (j + 1))
            qs, ks, bs, mats = [], [], [], []
            for sl in heads:
                _, _, g, k, _, q = _hg_gates(qr_ref[rows, sl], fl_ref[rows, sl], lb_all[:, sl])
                qs.append(q); ks.append(k); bs.append(_tri_dot(tri_ref[0], g))
            for hh in range(HG_HPS):
                a, _, _, _ = _hg_levels(qs[hh], ks[hh], bs[hh], mk_ref)
                mats.append(a + _rowsum(qs[hh] * ks[hh]) * mk_ref[N_LEV])
            for hh, sl in enumerate(heads):
                q, k, b, a, v = qs[hh], ks[hh], bs[hh], mats[hh], v_ref[rows, sl]
                b_last = b[CHUNK - 1: CHUNK, :]
                st = state[hh]
                st_ref[j, hh] = st
                o = _dot_nt((q * jnp.exp(b)).astype(BF16), st.astype(BF16)) + _dot(a.astype(BF16), v.astype(BF16))
                state[hh] = st * jnp.exp(b_last) + _dot3(v, k * jnp.exp(b_last - b), _dot_tn)
                o_ref[rows, sl] = o

    blk, lb2, st_spec, consts = _hg_specs(nc, False)
    tris, masks = _hg_const_arrays()
    return pl.pallas_call(
        body, name="hg_fwd", grid=(ng, nc),
        in_specs=[blk(0), blk(ng), blk(0), lb2] + consts,
        out_specs=[blk(0), st_spec],
        out_shape=[jax.ShapeDtypeStruct((S, HG_W), F32), jax.ShapeDtypeStruct((S // CHUNK, HG_HEADS, HG_HEAD, HG_HEAD), F32)],
        scratch_shapes=[pltpu.VMEM((HG_HPS, HG_HEAD, HG_HEAD), F32)],
        compiler_params=_params(("parallel", "arbitrary")),
    )(p0, p0, p1, lbl, tris, masks)


def _hg_bwd(p0, p1, do_hg, states, lbl):
    S = p0.shape[0]
    nc = S // (HG_CPS * CHUNK)
    ng = HG_HEADS // HG_HPS
    w = HG_HEAD * HG_HPS

    def body(qr_ref, fl_ref, v_ref, do_ref, st_ref, lbl_ref, tri_ref, mk_ref, mkb_ref,
             dqr_ref, dfl_ref, dv_ref, glbl_ref, dstate, carry, acc_lb):
        c = pl.program_id(1)

        @pl.when(c == 0)
        def _():
            dstate[...] = jnp.zeros_like(dstate)
            carry[...] = jnp.zeros_like(carry)
            acc_lb[...] = jnp.zeros_like(acc_lb)

        lb_all, dlb_dl0 = _hg_lb(lbl_ref)
        heads = [slice(HG_HEAD * hh, HG_HEAD * (hh + 1)) for hh in range(HG_HPS)]
        diag_mask = mk_ref[N_LEV]
        for j in reversed(range(HG_CPS)):
            rows = slice(CHUNK * j, CHUNK * (j + 1))
            gates, bs, dos = [], [], []
            for sl in heads:
                gt = _hg_gates(qr_ref[rows, sl], fl_ref[rows, sl], lb_all[:, sl])
                gates.append(gt)
                bs.append(_tri_dot(tri_ref[0], gt[2]))
            for sl in heads:
                dos.append(do_ref[rows, sl])
            inter = []
            for hh, sl in enumerate(heads):
                _, _, _, k, _, q = gates[hh]
                b, do, v = bs[hh], dos[hh], v_ref[rows, sl]
                b_last = b[CHUNK - 1: CHUNK, :]
                eb = jnp.exp(b)
                edec = jnp.exp(b_last - b)
                dst = dstate[hh]
                dq = _dot3(do, st_ref[j, hh]) * eb
                dk = _dot3(v, dst) * edec
                da = _dot_nt(do.astype(BF16), v.astype(BF16))
                dv_state = _dot_nt((k * edec).astype(BF16), dst.astype(BF16))
                dstate[hh] = dst * jnp.exp(b_last) + _dot3(do, q * eb, _dot_tn)
                inter.append((dq, dk, da, dv_state))
            for hh, sl in enumerate(heads):
                sig, f, _, k, sq, q = gates[hh]
                dq, dk, da, dv_state = inter[hh]
                b = bs[hh]
                db = q * dq - k * dk
                dab = da.astype(BF16)
                refs = _level_refs(b)
                a = _rowsum(q * k) * diag_mask
                for i in range(N_LEV):
                    diff = b - refs[i]
                    e = jnp.exp(jnp.minimum(diff, -diff))
                    qt = (q * e).astype(BF16)
                    kt = (k * e).astype(BF16)
                    a = a + _dot_nt(qt, kt) * mk_ref[i]
                    dam = dab * mkb_ref[i]
                    gq = _dot(dam, kt)
                    gk = _dot_tn(dam, qt)
                    dq = dq + e * gq
                    dk = dk + e * gk
                    db = db + (qt.astype(F32) * gq - kt.astype(F32) * gk)
                dc = _rowsum(da * diag_mask)
                dq = dq + dc * k
                dk = dk + dc * q
                dv_ref[rows, sl] = (_dot_tn(a.astype(BF16), dos[hh].astype(BF16)) + dv_state).astype(BF16)
                dg = _tri_dot(tri_ref[1], db) + carry[0:1, sl]
                carry[0:1, sl] += jnp.sum(db, axis=0, keepdims=True)
                lb = lb_all[:, sl]
                qr = qr_ref[rows, sl]
                t = dg / f - dk
                dfl_ref[rows, sl] = (t * (1.0 - lb) * sig * (1.0 - sig)).astype(BF16)
                acc_lb[0:1, sl] += jnp.sum(t * (1.0 - sig), axis=0, keepdims=True)
                dqr_ref[rows, sl] = (dq * (sq * (1.0 + qr * (1.0 - sq)))).astype(BF16)

        @pl.when(c == nc - 1)
        def _():
            gl0 = acc_lb[0:1, :] * dlb_dl0
            glbl_ref[0:1, :] = gl0
            glbl_ref[1:2, :] = -gl0

    blk, lb2, st_spec, consts = _hg_specs(nc, True)
    tris, masks = _hg_const_arrays()
    act = jax.ShapeDtypeStruct((S, HG_W), BF16)
    return pl.pallas_call(
        body, name="hg_bwd", grid=(ng, nc),
        in_specs=[blk(0), blk(ng), blk(0), blk(0), st_spec, lb2] + consts + consts[1:],
        out_specs=[blk(0), blk(0), blk(0), lb2],
        out_shape=[act, act, act, jax.ShapeDtypeStruct((2, HG_W), F32)],
        scratch_shapes=[pltpu.VMEM((HG_HPS, HG_HEAD, HG_HEAD), F32), pltpu.VMEM((8, w), F32), pltpu.VMEM((8, w), F32)],
        compiler_params=_params(("parallel", "arbitrary")),
    )(p0, p0, p1, do_hg, states, lbl, tris, masks, masks.astype(BF16))


def _stat_head(lane):
    return lane >> 4


def _stat_lane(lane):
    return lane & (STAT_LANES - 1)


def _o_head(lane):
    return lane >> 6


def _att_bias():
    qi = np.arange(ATT_BLK)[:, None]
    kj = np.arange(2 * ATT_BLK)[None, :]
    band = (kj >= qi) & (kj <= qi + ATT_BLK)
    qm = np.stack([band & (kj >= ATT_BLK), band])
    cur = (kj < ATT_BLK) & (qi <= kj)
    km = np.stack([cur, cur | ((kj >= ATT_BLK) & (qi >= kj - ATT_BLK))])
    to_bias = lambda m: jnp.asarray(np.where(m, 0.0, NEG), F32)
    return to_bias(qm), to_bias(km)


def _att_fwd(qkv_d):
    d, L, _ = qkv_d.shape
    qb = ATT_QB
    rows = qb * ATT_BLK

    def body(q_ref, kp_ref, kc_ref, vp_ref, vc_ref, bias_ref, o_ref, lse_ref, s_scr, p_scr):
        first = _iota((ATT_BLK, HEAD_PAIR), 1) < AT_HEAD
        head_of_lane = _stat_head(_iota((ATT_BLK, STAT_W), 1))
        pairs = [slice(HEAD_PAIR * hp, HEAD_PAIR * (hp + 1)) for hp in range(4)]
        blk = lambda i: slice(ATT_BLK * i, ATT_BLK * (i + 1))

        def keys(i, sl, prev_ref, cur_ref):
            before = prev_ref[0, :, sl] if i == 0 else cur_ref[0, blk(i - 1), sl]
            return jnp.concatenate([before, cur_ref[0, blk(i), sl]], axis=0)

        for i in range(qb):
            for hp, sl in enumerate(pairs):
                q2 = q_ref[0, blk(i), sl] * ATT_SCALE
                zero = jnp.zeros_like(q2)
                qs = jnp.concatenate([jnp.where(first, q2, zero), jnp.where(first, zero, q2)], axis=0)
                s_scr[4 * i + hp] = _dot_nt(qs, keys(i, sl, kp_ref, kc_ref))
        stats = []
        for i in range(qb):
            bias = bias_ref[jnp.minimum(pl.program_id(1), 1)] if i == 0 else bias_ref[1]
            bias2 = jnp.concatenate([bias, bias], axis=0)
            for hp in range(4):
                s = s_scr[4 * i + hp] + bias2
                m = jnp.max(s, axis=-1, keepdims=True)
                p = jnp.exp(s - m)
                l = jnp.sum(p, axis=-1, keepdims=True)
                p_scr[4 * i + hp] = p.astype(BF16)
                stats.append((l, m + jnp.log(l)))
        for i in range(qb):
            lse_blk = jnp.zeros((ATT_BLK, STAT_W), F32)
            for hp, sl in enumerate(pairs):
                l, lse = stats[4 * i + hp]
                o = _dot(p_scr[4 * i + hp], keys(i, sl, vp_ref, vc_ref)) / l
                o_ref[0, blk(i), sl] = jnp.where(first, o[:ATT_BLK], o[ATT_BLK:]).astype(BF16)
                lse_blk = jnp.where(head_of_lane == 2 * hp, lse[:ATT_BLK],
                                    jnp.where(head_of_lane == 2 * hp + 1, lse[ATT_BLK:], lse_blk))
            lse_ref[0, blk(i), :] = lse_blk

    cur = lambda j: pl.BlockSpec((1, rows, AT_W), lambda r, n: (r, n, j))
    prev = lambda j: pl.BlockSpec((1, ATT_BLK, AT_W), lambda r, n: (r, jnp.maximum(qb * n - 1, 0), j))
    return pl.pallas_call(
        body, name=f"att_fwd_d{d}", grid=(d, L // rows),
        in_specs=[cur(0), prev(1), cur(1), prev(2), cur(2), _full((2, ATT_BLK, 2 * ATT_BLK))],
        out_specs=[pl.BlockSpec((1, rows, AT_W), lambda r, n: (r, n, 0)), pl.BlockSpec((1, rows, STAT_W), lambda r, n: (r, n, 0))],
        out_shape=[jax.ShapeDtypeStruct((d, L, AT_W), BF16), jax.ShapeDtypeStruct((d, L, STAT_W), F32)],
        scratch_shapes=[pltpu.VMEM((4 * qb, 2 * ATT_BLK, 2 * ATT_BLK), F32), pltpu.VMEM((4 * qb, 2 * ATT_BLK, 2 * ATT_BLK), BF16)],
        compiler_params=_params(("parallel", "parallel")),
    )(qkv_d, qkv_d, qkv_d, qkv_d, qkv_d, _att_bias()[0])


def _att_combine(os_d, ls_d, z_at):
    S = z_at.shape[0]
    tm = MIX_TM

    def body(oa_ref, ob4_ref, oc16_ref, la_ref, lb4_ref, lc16_ref, p4_ref, p16_ref, z_ref,
             oat_ref, lse_ref, mix_ref, lb_scr, lc_scr):
        _merge_residues(lb4_ref, lb_scr, 4)
        _merge_residues(lc16_ref, lc_scr, 16)
        ls = (la_ref[...], lb_scr[0], lc_scr[0])
        mx = jnp.maximum(jnp.maximum(ls[0], ls[1]), ls[2])
        es = [jnp.exp(l - mx) for l in ls]
        zs = es[0] + es[1] + es[2]
        lse_ref[...] = mx + jnp.log(zs)
        spread = ((_o_head(_iota((STAT_W, AT_W), 1)) == _stat_head(_iota((STAT_W, AT_W), 0)))
                  & (_stat_lane(_iota((STAT_W, AT_W), 0)) == 0)).astype(BF16)
        os_ = (oa_ref[...].astype(F32), _merge_bf16(ob4_ref, p4_ref, 4), _merge_bf16(oc16_ref, p16_ref, 16))
        o = jnp.zeros((tm, AT_W), F32)
        for e, oi in zip(es, os_):
            hi, lo = _split2(e / zs)
            o = o + (_dot(hi, spread) + _dot(lo, spread)) * oi
        oat_ref[...] = o
        z = z_ref[...]
        mixed = o * (z * _sigmoid(z))
        mix_ref[...] = mixed.astype(BF16)

    row = lambda w: pl.BlockSpec((tm, w), lambda s: (s, 0))
    return pl.pallas_call(
        body, name="att_combine", grid=(S // tm,),
        in_specs=[row(AT_W), _dil_spec(4, tm, AT_W), _dil_spec(16, tm, AT_W),
                  row(STAT_W), _dil_spec(4, tm, STAT_W), _dil_spec(16, tm, STAT_W),
                  _full((PERM_ROWS, PERM_ROWS)), _full((PERM_ROWS, PERM_ROWS)), row(AT_W)],
        out_specs=[row(AT_W), row(STAT_W), row(AT_W)],
        out_shape=[jax.ShapeDtypeStruct((S, AT_W), F32), jax.ShapeDtypeStruct((S, STAT_W), F32),
                   jax.ShapeDtypeStruct((S, AT_W), BF16)],
        scratch_shapes=[_slab_scratch(tm, STAT_W), _slab_scratch(tm, STAT_W)],
        compiler_params=_params(("parallel",)),
    )(os_d[0].reshape(S, AT_W), os_d[1], os_d[2], ls_d[0].reshape(S, STAT_W), ls_d[1], ls_d[2],
      _perm_matrix(4), _perm_matrix(16), z_at)


def _att_gate_bwd(dm_at, o_at, lse, z_at):
    S = z_at.shape[0]
    tm = MIX_TM

    def body(dm_ref, o_ref, l_ref, z_ref, do_ref, do4_ref, do16_ref, dl_ref, dl4_ref, dl16_ref, dz_ref, do_scr, dl_scr):
        o = o_ref[...]
        z = z_ref[...]
        dm = dm_ref[...]
        sz = _sigmoid(z)
        dz_ref[...] = (dm * o * (sz * (1.0 + z * (1.0 - sz)))).astype(BF16)
        do = dm * (z * sz)
        do_ref[...] = do.astype(BF16)
        gather = (_o_head(_iota((AT_W, STAT_W), 0)) == _stat_head(_iota((AT_W, STAT_W), 1))).astype(BF16)
        dl = jnp.where(_stat_lane(_iota((tm, STAT_W), 1)) < STAT_LSE_LANE, _dot_sel(do * o, gather), l_ref[...])
        dl_ref[...] = dl
        _to_slabs(do, do_scr)
        _to_slabs(dl, dl_scr)
        _split_residues(do_scr, do4_ref, 4, BF16)
        _split_residues(do_scr, do16_ref, 16, BF16)
        _split_residues(dl_scr, dl4_ref, 4, F32)
        _split_residues(dl_scr, dl16_ref, 16, F32)

    row = lambda w: pl.BlockSpec((tm, w), lambda s: (s, 0))
    sds = jax.ShapeDtypeStruct
    return pl.pallas_call(
        body, name="att_gate_bwd", grid=(S // tm,),
        in_specs=[row(AT_W), row(AT_W), row(STAT_W), row(AT_W)],
        out_specs=[row(AT_W), _dil_spec(4, tm, AT_W), _dil_spec(16, tm, AT_W),
                   row(STAT_W), _dil_spec(4, tm, STAT_W), _dil_spec(16, tm, STAT_W), row(AT_W)],
        out_shape=[sds((S, AT_W), BF16), sds((4, S // 4, AT_W), BF16), sds((16, S // 16, AT_W), BF16),
                   sds((S, STAT_W), F32), sds((4, S // 4, STAT_W), F32), sds((16, S // 16, STAT_W), F32),
                   sds((S, AT_W), BF16)],
        scratch_shapes=[_slab_scratch(tm, AT_W), _slab_scratch(tm, STAT_W)],
        compiler_params=_params(("parallel",)),
    )(dm_at, o_at, lse, z_at)


def _att_bwd(qkv_d, do_d, dl_d):
    d, L, _ = qkv_d.shape
    nb = L // ATT_BLK
    qb = ATT_QB
    rows = qb * ATT_BLK
    nsteps = L // rows

    def body(qc_ref, qn_ref, kp_ref, kc_ref, vp_ref, vc_ref, ac_ref, an_ref, lc_ref, ln_ref, bq_ref, bk_ref, dqkv_ref,
             s_scr, dp_scr, st_scr, dpt_scr, ds_scr, pt_scr, dst_scr):
        n = pl.program_id(1)
        first = _iota((ATT_BLK, HEAD_PAIR), 1) < AT_HEAD
        pairs = [slice(HEAD_PAIR * hp, HEAD_PAIR * (hp + 1)) for hp in range(4)]
        blk = lambda i: slice(ATT_BLK * i, ATT_BLK * (i + 1))

        def stack(t):
            zero = jnp.zeros_like(t)
            return jnp.concatenate([jnp.where(first, t, zero), jnp.where(first, zero, t)], axis=0)

        def unstack(t2):
            return jnp.where(first, t2[:ATT_BLK], t2[ATT_BLK:])

        def with_prev(i, sl, prev_ref, cur_ref):
            before = prev_ref[0, :, sl] if i == 0 else cur_ref[0, blk(i - 1), sl]
            return jnp.concatenate([before, cur_ref[0, blk(i), sl]], axis=0)

        def with_next(i, sl, cur_ref, next_ref):
            after = next_ref[0, :, sl] if i == qb - 1 else cur_ref[0, blk(i + 1), sl]
            return jnp.concatenate([cur_ref[0, blk(i), sl], after], axis=0)

        for i in range(qb):
            for hp, sl in enumerate(pairs):
                j = 4 * i + hp
                s_scr[j] = _dot_nt(stack(qc_ref[0, blk(i), sl] * ATT_SCALE), with_prev(i, sl, kp_ref, kc_ref))
                dp_scr[j] = _dot_nt(stack(ac_ref[0, blk(i), sl]), with_prev(i, sl, vp_ref, vc_ref))
                st_scr[j] = _dot_nt(stack(kc_ref[0, blk(i), sl] * ATT_SCALE), with_next(i, sl, qc_ref, qn_ref))
                dpt_scr[j] = _dot_nt(stack(vc_ref[0, blk(i), sl]), with_next(i, sl, ac_ref, an_ref))
        for i in range(qb):
            bias = bq_ref[jnp.minimum(n, 1)] if i == 0 else bq_ref[1]
            bias_t = bk_ref[jnp.minimum(nsteps - 1 - n, 1)] if i == qb - 1 else bk_ref[1]
            bias2 = jnp.concatenate([bias, bias], axis=0)
            bias_t2 = jnp.concatenate([bias_t, bias_t], axis=0)
            dl_c = lc_ref[0, blk(i), :]
            dl_t = with_next(i, slice(None), lc_ref, ln_ref).T
            for hp in range(4):
                j = 4 * i + hp
                at = [STAT_LANES * (2 * hp), STAT_LANES * (2 * hp + 1)]
                col = lambda t, o: jnp.concatenate([t[:, a + o: a + o + 1] for a in at], axis=0)
                p = jnp.exp(s_scr[j] + bias2 - col(dl_c, STAT_LSE_LANE))
                ds_scr[j] = (p * (dp_scr[j] - col(dl_c, 0))).astype(BF16)
                row = lambda t, o: jnp.concatenate([jnp.broadcast_to(t[a + o: a + o + 1, :], (ATT_BLK, 2 * ATT_BLK)) for a in at], axis=0)
                pt = jnp.exp(st_scr[j] + bias_t2 - row(dl_t, STAT_LSE_LANE))
                pt_scr[j] = pt.astype(BF16)
                dst_scr[j] = (pt * (dpt_scr[j] - row(dl_t, 0))).astype(BF16)
        for i in range(qb):
            for hp, sl in enumerate(pairs):
                j = 4 * i + hp
                dq = unstack(_dot(ds_scr[j], with_prev(i, sl, kp_ref, kc_ref))) * ATT_SCALE
                dk = unstack(_dot(dst_scr[j], with_next(i, sl, qc_ref, qn_ref))) * ATT_SCALE
                dv = unstack(_dot(pt_scr[j], with_next(i, sl, ac_ref, an_ref)))
                dqkv_ref[0, blk(i), sl] = dq.astype(BF16)
                dqkv_ref[0, blk(i), AT_W + HEAD_PAIR * hp: AT_W + HEAD_PAIR * (hp + 1)] = dk.astype(BF16)
                dqkv_ref[0, blk(i), 2 * AT_W + HEAD_PAIR * hp: 2 * AT_W + HEAD_PAIR * (hp + 1)] = dv.astype(BF16)

    cur = lambda j: pl.BlockSpec((1, rows, AT_W), lambda r, n: (r, n, j))
    prev = lambda j: pl.BlockSpec((1, ATT_BLK, AT_W), lambda r, n: (r, jnp.maximum(qb * n - 1, 0), j))
    nxt_blk = lambda n: jnp.minimum(qb * (n + 1), nb - 1)
    sq = (4 * qb, 2 * ATT_BLK, 2 * ATT_BLK)
    return pl.pallas_call(
        body, name=f"att_bwd_d{d}", grid=(d, nsteps),
        in_specs=[cur(0), pl.BlockSpec((1, ATT_BLK, AT_W), lambda r, n: (r, nxt_blk(n), 0)), prev(1), cur(1), prev(2), cur(2),
                  pl.BlockSpec((1, rows, AT_W), lambda r, n: (r, n, 0)),
                  pl.BlockSpec((1, ATT_BLK, AT_W), lambda r, n: (r, nxt_blk(n), 0)),
                  pl.BlockSpec((1, rows, STAT_W), lambda r, n: (r, n, 0)),
                  pl.BlockSpec((1, ATT_BLK, STAT_W), lambda r, n: (r, nxt_blk(n), 0)),
                  _full((2, ATT_BLK, 2 * ATT_BLK)), _full((2, ATT_BLK, 2 * ATT_BLK))],
        out_specs=pl.BlockSpec((1, rows, 3 * AT_W), lambda r, n: (r, n, 0)),
        out_shape=jax.ShapeDtypeStruct((d, L, 3 * AT_W), BF16),
        scratch_shapes=[pltpu.VMEM(sq, F32)] * 4 + [pltpu.VMEM(sq, BF16)] * 3,
        compiler_params=_params(("parallel", "parallel")),
    )(qkv_d, qkv_d, qkv_d, qkv_d, qkv_d, qkv_d, do_d, do_d, dl_d, dl_d, *_att_bias())


def _att_bwd_combine(dqkvs, cos2, sin2):
    S = dqkvs[0].shape[1]
    tm = MIX_TM

    def body(a_ref, b4_ref, c16_ref, p4_ref, p16_ref, cos_ref, sin_ref, dq_ref, dk_ref, dv_ref):
        t = a_ref[...].astype(F32) + _merge_bf16(b4_ref, p4_ref, 4) + _merge_bf16(c16_ref, p16_ref, 16)
        dy = t[:, : 2 * AT_W]
        cosf = jnp.tile(cos_ref[...], (1, 8))
        sinf = jnp.tile(sin_ref[...], (1, 8))
        dx = dy * cosf - _rope_rot(dy) * sinf
        dq_ref[...] = dx[:, :AT_W].astype(BF16)
        dk_ref[...] = dx[:, AT_W:].astype(BF16)
        dv_ref[...] = t[:, 2 * AT_W:].astype(BF16)

    row = lambda w: pl.BlockSpec((tm, w), lambda s: (s, 0))
    act = jax.ShapeDtypeStruct((S, AT_W), BF16)
    return pl.pallas_call(
        body, name="att_bwd_combine", grid=(S // tm,),
        in_specs=[row(3 * AT_W), _dil_spec(4, tm, 3 * AT_W), _dil_spec(16, tm, 3 * AT_W),
                  _full((PERM_ROWS, PERM_ROWS)), _full((PERM_ROWS, PERM_ROWS)), row(128), row(128)],
        out_specs=[row(AT_W), row(AT_W), row(AT_W)],
        out_shape=[act, act, act],
        compiler_params=_params(("parallel",)),
    )(dqkvs[0].reshape(S, 3 * AT_W), dqkvs[1], dqkvs[2], _perm_matrix(4), _perm_matrix(16), cos2, sin2)


def _outproj(x2, tgt2, mix_at, w_out_full, fnw, o_hg, p1, hg_norm_w):
    S = x2.shape[0]
    tm = PROJ_TM
    ns = S // tm

    def body(x_ref, t_ref, ma_ref, w_ref, fw_ref, o_ref, z_ref, hgw_ref,
             dh_ref, doh_ref, dzh_ref, dma_ref, gw_ref, gfw_ref, ghgw_ref, loss_ref):
        s = pl.program_id(0)

        @pl.when(s == 0)
        def _():
            gw_ref[...] = jnp.zeros_like(gw_ref)
            gfw_ref[...] = jnp.zeros_like(gfw_ref)
            ghgw_ref[...] = jnp.zeros_like(ghgw_ref)
            loss_ref[...] = jnp.zeros_like(loss_ref)

        heads = [slice(HG_HEAD * hh, HG_HEAD * (hh + 1)) for hh in range(HG_HEADS)]
        norm = []
        for sl in heads:
            o, z = o_ref[:, sl], z_ref[:, sl]
            rs = lax.rsqrt(jnp.mean(o * o, axis=-1, keepdims=True) + EPS)
            norm.append((rs, o * rs, _sigmoid(z)))
        mh = jnp.concatenate([(oh * hgw_ref[:, sl] * (z_ref[:, sl] * sz)).astype(BF16)
                              for sl, (_, oh, sz) in zip(heads, norm)], axis=1)
        y = _dot(mh, w_ref[:HG_W, :]) + _dot(ma_ref[...], w_ref[HG_W:, :])
        h = x_ref[...] + y
        r = lax.rsqrt(jnp.mean(h * h, axis=-1, keepdims=True) + EPS)
        hn = h * r
        fw = fw_ref[...]
        err = hn * fw - t_ref[...]
        loss_ref[...] += 0.5 * jnp.sum(jnp.mean(err * err, axis=-1, keepdims=True))
        dout = err * (1.0 / D_MODEL)
        gfw_ref[...] += jnp.sum(dout * hn, axis=0, keepdims=True)
        dhn = dout * fw
        dh = r * (dhn - hn * jnp.mean(dhn * hn, axis=-1, keepdims=True))
        dh_ref[...] = dh
        dhb = dh.astype(BF16)
        dma_ref[...] = _dot_nt(dhb, w_ref[HG_W:, :])
        dmh = _dot_nt(dhb, w_ref[:HG_W, :])
        for sl, (rs, oh, sz) in zip(heads, norm):
            z, dmix, gwv = z_ref[:, sl], dmh[:, sl], hgw_ref[:, sl]
            dzh_ref[:, sl] = (dmix * (oh * gwv) * (sz * (1.0 + z * (1.0 - sz)))).astype(BF16)
            don = dmix * (z * sz)
            ghgw_ref[:, sl] += jnp.sum(don * oh, axis=0, keepdims=True)
            dy = don * gwv
            doh_ref[:, sl] = rs * (dy - oh * jnp.mean(dy * oh, axis=-1, keepdims=True))
        gw_ref[:HG_W, :] += _dot_tn(mh, dhb)
        gw_ref[HG_W:, :] += _dot_tn(ma_ref[...], dhb)

    row = lambda w: pl.BlockSpec((tm, w), lambda s: (s, 0))
    return pl.pallas_call(
        body, name="outproj", grid=(ns,),
        in_specs=[row(D_MODEL), row(D_MODEL), row(AT_W),
                  _full((D_MODEL, D_MODEL)), _full((1, D_MODEL)),
                  row(HG_W), pl.BlockSpec((tm, HG_W), lambda s: (s, 1)), _full((1, HG_W))],
        out_specs=[row(D_MODEL), row(HG_W), row(HG_W), row(AT_W), _full((D_MODEL, D_MODEL)), _full((1, D_MODEL)),
                   _full((1, HG_W)), _full((8, 128))],
        out_shape=[jax.ShapeDtypeStruct((S, D_MODEL), F32), jax.ShapeDtypeStruct((S, HG_W), F32),
                   jax.ShapeDtypeStruct((S, HG_W), BF16), jax.ShapeDtypeStruct((S, AT_W), F32),
                   jax.ShapeDtypeStruct((D_MODEL, D_MODEL), F32), jax.ShapeDtypeStruct((1, D_MODEL), F32),
                   jax.ShapeDtypeStruct((1, HG_W), F32), jax.ShapeDtypeStruct((8, 128), F32)],
        compiler_params=_params(("arbitrary",)),
    )(x2, tgt2, mix_at, w_out_full, fnw, o_hg, p1, hg_norm_w)


def _inproj_bwd_x(dps, w_in_full, x2, norm_w, dh):
    S = x2.shape[0]
    tm = PROJ_TM

    def body(d0, d1, d2, d3, d4, d5, d6, d7, w_ref, x_ref, nw_ref, dh_ref, gx_ref, gnw_ref):
        s = pl.program_id(0)

        @pl.when(s == 0)
        def _():
            gnw_ref[...] = jnp.zeros_like(gnw_ref)

        du = jnp.zeros((tm, D_MODEL), F32)
        for i, dref in enumerate((d0, d1, d2, d3, d4, d5, d6, d7)):
            j, half = divmod(i, 2)
            du = du + _dot_nt(dref[...], w_ref[j, :, 512 * half: 512 * (half + 1)])
        x = x_ref[...]
        r = lax.rsqrt(jnp.mean(x * x, axis=-1, keepdims=True) + EPS)
        xh = x * r
        gnw_ref[...] += jnp.sum(du * xh, axis=0, keepdims=True)
        dun = du * nw_ref[...]
        gx_ref[...] = dh_ref[...] + r * (dun - xh * jnp.mean(dun * xh, axis=-1, keepdims=True))

    row = lambda w: pl.BlockSpec((tm, w), lambda s: (s, 0))
    return pl.pallas_call(
        body, name="inproj_bwd_x", grid=(S // tm,),
        in_specs=[row(512)] * 8 + [_full((4, D_MODEL, 1024)), row(D_MODEL), _full((1, D_MODEL)), row(D_MODEL)],
        out_specs=[row(D_MODEL), _full((1, D_MODEL))],
        out_shape=[jax.ShapeDtypeStruct((S, D_MODEL), F32), jax.ShapeDtypeStruct((1, D_MODEL), F32)],
        compiler_params=_params(("arbitrary",)),
    )(*dps, w_in_full, x2, norm_w, dh)


def _inproj_bwd_w(u, dps):
    S = u.shape[0]
    tm = PROJ_TM

    def body(u_ref, d0, d1, d2, d3, d4, d5, d6, d7, g_ref):
        @pl.when(pl.program_id(0) == 0)
        def _():
            g_ref[...] = jnp.zeros_like(g_ref)

        ub = u_ref[...]
        for i, dref in enumerate((d0, d1, d2, d3, d4, d5, d6, d7)):
            j, half = divmod(i, 2)
            g_ref[j, :, 512 * half: 512 * (half + 1)] += _dot_tn(ub, dref[...])

    return pl.pallas_call(
        body, name="inproj_bwd_w", grid=(S // tm,),
        in_specs=[pl.BlockSpec((tm, D_MODEL), lambda s: (s, 0))] + [pl.BlockSpec((tm, 512), lambda s: (s, 0))] * 8,
        out_specs=_full((4, D_MODEL, 1024)),
        out_shape=jax.ShapeDtypeStruct((4, D_MODEL, 1024), F32),
        compiler_params=_params(("arbitrary",)),
    )(u, *dps)


def _adamw_update(gg, w_ref, m_ref, v_ref, d_ref, nm_ref, nv_ref):
    nm = ADAM_B1 * m_ref[...] + (1.0 - ADAM_B1) * gg
    nv = ADAM_B2 * v_ref[...] + (1.0 - ADAM_B2) * (gg * gg)
    m_hat = nm / (1.0 - ADAM_B1 ** ADAM_STEP)
    v_hat = nv / (1.0 - ADAM_B2 ** ADAM_STEP)
    d_ref[...] = -ADAM_LR * (m_hat / (jnp.sqrt(v_hat) + ADAM_EPS) + ADAM_WD * w_ref[...])
    nm_ref[...] = nm
    nv_ref[...] = nv


def _adamw(w, g, m, v, name):
    rows, cols = w.shape
    tr = min(rows, 256)

    def body(w_ref, g_ref, m_ref, v_ref, d_ref, nm_ref, nv_ref):
        _adamw_update(g_ref[...], w_ref, m_ref, v_ref, d_ref, nm_ref, nv_ref)

    spec = pl.BlockSpec((tr, cols), lambda i: (i, 0))
    sds = jax.ShapeDtypeStruct((rows, cols), F32)
    return pl.pallas_call(
        body, name=name, grid=(rows // tr,),
        in_specs=[spec] * 4, out_specs=[spec] * 3, out_shape=[sds] * 3,
        compiler_params=_params(("parallel",)),
    )(w, g, m, v)


def _place():
    return lax.axis_index("x"), lax.axis_index("y"), lax.axis_index("c")


def _gather_weights(w_in_s, w_out_s):
    def body(win_ref, wout_ref, fin_ref, fout_ref, send_sems, recv_sems):
        x, y, c = _place()
        me = (x, y, c)
        sib = (x, y, 1 - c)
        mine = 2 * x + y
        fin_ref[mine] = win_ref[...].astype(BF16)
        fout_ref[mine] = wout_ref[...].astype(BF16)
        chips = [(1 - x, y), (x, 1 - y), (1 - x, 1 - y)]

        def halves(chip, half):
            return (fin_ref.at[chip, pl.ds(half * 512, 512), :], fout_ref.at[chip, pl.ds(half * 128, 128), :])

        def copy(k, ref, to):
            return pltpu.make_async_remote_copy(src_ref=ref, dst_ref=ref, send_sem=send_sems.at[k],
                                                recv_sem=recv_sems.at[k], device_id=to, device_id_type=MESH)

        first, passed = [], []
        for j, (cx, cy) in enumerate(chips):
            for a, ref in enumerate(halves(mine, c)):
                first.append(copy(2 * j + a, ref, (cx, cy, c)))
        for cp in first:
            cp.start()
        for j, (cx, cy) in enumerate(chips):
            for a, ref in enumerate(halves(2 * cx + cy, c)):
                copy(2 * j + a, ref, me).wait_recv()
                fwd = copy(6 + 2 * j + a, ref, sib)
                fwd.start()
                passed.append(fwd)
        for j, (cx, cy) in enumerate(chips):
            for a, ref in enumerate(halves(2 * cx + cy, 1 - c)):
                copy(6 + 2 * j + a, ref, me).wait_recv()
        for cp in first + passed:
            cp.wait_send()

    vm = pl.BlockSpec(memory_space=pltpu.VMEM)
    return pl.pallas_call(
        body, name="gather_weights",
        in_specs=[vm, vm], out_specs=[vm, vm],
        out_shape=[jax.ShapeDtypeStruct((4, D_MODEL, 1024), BF16), jax.ShapeDtypeStruct((4, 256, D_MODEL), BF16)],
        scratch_shapes=[pltpu.SemaphoreType.DMA((12,)), pltpu.SemaphoreType.DMA((12,))],
        compiler_params=pltpu.CompilerParams(vmem_limit_bytes=VMEM_LIMIT),
    )(w_in_s, w_out_s)


def _swap_halves(g_in, g_out):
    def body(gin_ref, gout_ref, rin_ref, rout_ref, send_sems, recv_sems):
        x, y, c = _place()
        sib = (x, y, 1 - c)
        cps = [pltpu.make_async_remote_copy(src_ref=src.at[:, 1 - c], dst_ref=dst, send_sem=send_sems.at[k],
                                            recv_sem=recv_sems.at[k], device_id=sib, device_id_type=MESH)
               for k, (src, dst) in enumerate(((gin_ref, rin_ref), (gout_ref, rout_ref)))]
        for cp in cps:
            cp.start()
        for cp in cps:
            cp.wait()

    hbm = pl.BlockSpec(memory_space=pl.ANY)
    return pl.pallas_call(
        body, name="swap_halves",
        in_specs=[hbm, hbm], out_specs=[hbm, hbm],
        out_shape=[jax.ShapeDtypeStruct((4,) + g.shape[2:], F32) for g in (g_in, g_out)],
        scratch_shapes=[pltpu.SemaphoreType.DMA((2,)), pltpu.SemaphoreType.DMA((2,))],
    )(g_in, g_out)


def _add_half(g, r, cidx, name):
    n, _, rows, cols = g.shape

    def body(c_ref, g_ref, r_ref, o_ref):
        o_ref[0] = (g_ref[0, 0] + r_ref[0]).astype(BF16)

    return pl.pallas_call(
        body, name=name,
        grid_spec=pltpu.PrefetchScalarGridSpec(
            num_scalar_prefetch=1, grid=(n,),
            in_specs=[pl.BlockSpec((1, 1, rows, cols), lambda j, c_ref: (j, c_ref[0], 0, 0)),
                      pl.BlockSpec((1, rows, cols), lambda j, c_ref: (j, 0, 0))],
            out_specs=pl.BlockSpec((1, rows, cols), lambda j, c_ref: (j, 0, 0))),
        out_shape=jax.ShapeDtypeStruct((n, rows, cols), BF16),
        compiler_params=_params(("parallel",)),
    )(cidx, g, r)


def _exchange_copies(in_ref, out_ref, lin_ref, lout_ref, send_sems, recv_sems):
    x, y, c = _place()
    cps = []
    for k, (cx, cy) in enumerate([(1 - x, y), (x, 1 - y), (1 - x, 1 - y)]):
        for a, (src, dst) in enumerate(((in_ref, lin_ref), (out_ref, lout_ref))):
            cps.append(pltpu.make_async_remote_copy(
                src_ref=src.at[2 * cx + cy], dst_ref=dst.at[k], send_sem=send_sems.at[2 * k + a],
                recv_sem=recv_sems.at[2 * k + a], device_id=(cx, cy, c), device_id_type=MESH))
    return cps


def _exchange_start(cs_in, cs_out):
    def body(in_ref, out_ref, lin_ref, lout_ref, send_sems, recv_sems, in_thru, out_thru, lin_thru, lout_thru, token):
        for cp in _exchange_copies(in_ref, out_ref, lin_ref, lout_ref, send_sems, recv_sems):
            cp.start()
        token[...] = jnp.zeros_like(token)

    lands = [lax.empty((3,) + a.shape[1:], a.dtype) for a in (cs_in, cs_out)]
    bufs = [pltpu.with_memory_space_constraint(a, pltpu.HBM) for a in (cs_in, cs_out, *lands)]
    hbm = pl.BlockSpec(memory_space=pltpu.HBM)
    sem = pl.BlockSpec(memory_space=pltpu.SEMAPHORE)
    return pl.pallas_call(
        body, name="exchange_start",
        in_specs=[hbm] * 4,
        out_specs=[sem, sem, hbm, hbm, hbm, hbm, pl.BlockSpec(memory_space=pltpu.VMEM)],
        out_shape=[pltpu.SemaphoreType.DMA((6,)), pltpu.SemaphoreType.DMA((6,))]
                  + [pltpu.HBM(b.shape, b.dtype) for b in bufs] + [jax.ShapeDtypeStruct((8, 128), F32)],
        input_output_aliases={0: 2, 1: 3, 2: 4, 3: 5},
        compiler_params=pltpu.CompilerParams(has_side_effects=pltpu.SideEffectType.DATAFLOW_SIDE_EFFECTING),
    )(*bufs)


def _exchange_wait(send_sems, recv_sems, in_thru, out_thru, lin_thru, lout_thru, after):
    def body(in_ref, out_ref, lin_ref, lout_ref, send_sems, recv_sems, after_ref, in_dead, out_dead, got_in, got_out):
        for cp in _exchange_copies(in_ref, out_ref, lin_ref, lout_ref, send_sems, recv_sems):
            cp.wait_send()
            cp.wait_recv()

    hbm = pl.BlockSpec(memory_space=pltpu.HBM)
    sem = pl.BlockSpec(memory_space=pltpu.SEMAPHORE)
    bufs = (in_thru, out_thru, lin_thru, lout_thru)
    return pl.pallas_call(
        body, name="exchange_wait",
        in_specs=[hbm] * 4 + [sem, sem, pl.BlockSpec(memory_space=pl.ANY)],
        out_specs=[hbm] * 4,
        out_shape=[pltpu.HBM(b.shape, b.dtype) for b in bufs],
        input_output_aliases={0: 0, 1: 1, 2: 2, 3: 3},
        compiler_params=pltpu.CompilerParams(has_side_effects=pltpu.SideEffectType.DATAFLOW_SIDE_EFFECTING),
    )(*bufs, send_sems, recv_sems, after)


PEER_XOR = (2, 1, 3)


def _sum_chips(cs, r, chip_idx, name):
    _, rows, cols = r.shape
    tr = min(rows, 256)

    def body(m_ref, cs_ref, r_ref, o_ref):
        mine = m_ref[0]
        own = cs_ref[0].astype(F32)
        got = [r_ref[k].astype(F32) for k in range(3)]
        acc = None
        for s in range(4):
            rel = mine ^ s
            term = jnp.where(rel == 0, own, jnp.where(rel == PEER_XOR[0], got[0],
                                                      jnp.where(rel == PEER_XOR[1], got[1], got[2])))
            acc = term if acc is None else acc + term
        o_ref[...] = acc

    return pl.pallas_call(
        body, name=name,
        grid_spec=pltpu.PrefetchScalarGridSpec(
            num_scalar_prefetch=1, grid=(rows // tr,),
            in_specs=[pl.BlockSpec((1, tr, cols), lambda i, m_ref: (m_ref[0], i, 0)),
                      pl.BlockSpec((3, tr, cols), lambda i, m_ref: (0, i, 0))],
            out_specs=pl.BlockSpec((tr, cols), lambda i, m_ref: (i, 0))),
        out_shape=jax.ShapeDtypeStruct((rows, cols), F32),
        compiler_params=_params(("parallel",)),
    )(chip_idx, cs, r)


def _swap_reduced(h_in, h_out):
    def body(in_ref, out_ref, rin_ref, rout_ref, send_sems, recv_sems):
        x, y, c = _place()
        cps = [pltpu.make_async_remote_copy(src_ref=src, dst_ref=dst, send_sem=send_sems.at[k],
                                            recv_sem=recv_sems.at[k], device_id=(x, y, 1 - c), device_id_type=MESH)
               for k, (src, dst) in enumerate(((in_ref, rin_ref), (out_ref, rout_ref)))]
        for cp in cps:
            cp.start()
        for cp in cps:
            cp.wait()

    hbm = pl.BlockSpec(memory_space=pl.ANY)
    return pl.pallas_call(
        body, name="swap_reduced",
        in_specs=[hbm, hbm], out_specs=[hbm, hbm],
        out_shape=[jax.ShapeDtypeStruct(h.shape, F32) for h in (h_in, h_out)],
        scratch_shapes=[pltpu.SemaphoreType.DMA((2,)), pltpu.SemaphoreType.DMA((2,))],
    )(h_in, h_out)


def _adamw_halves(w, mine, theirs, m, v, cidx, name):
    rows, cols = w.shape
    half = rows // 2
    tr = min(half, 256)
    nbh = half // tr

    def body(c_ref, w_ref, a_ref, b_ref, m_ref, v_ref, g_ref, d_ref, nm_ref, nv_ref):
        gg = jnp.where(pl.program_id(0) // nbh == c_ref[0], a_ref[...], b_ref[...])
        g_ref[...] = gg
        _adamw_update(gg, w_ref, m_ref, v_ref, d_ref, nm_ref, nv_ref)

    spec = pl.BlockSpec((tr, cols), lambda i, c_ref: (i, 0))
    hspec = pl.BlockSpec((tr, cols), lambda i, c_ref: (i % nbh, 0))
    sds = jax.ShapeDtypeStruct((rows, cols), F32)
    return pl.pallas_call(
        body, name=name,
        grid_spec=pltpu.PrefetchScalarGridSpec(
            num_scalar_prefetch=1, grid=(rows // tr,),
            in_specs=[spec, hspec, hspec, spec, spec], out_specs=[spec] * 4),
        out_shape=[sds] * 4,
        compiler_params=_params(("parallel",)),
    )(cidx, w, mine, theirs, m, v)


def _allreduce_small(g_nw, g_fw, g_hgw, g_lbl, loss8):
    def body(nw_ref, fw_ref, hgw_ref, lbl_ref, loss_ref, out_ref, slots, send_sems, recv_sems):
        x, y, c = _place()
        me = 4 * x + 2 * y + c
        slots[me] = jnp.zeros((8, D_MODEL), F32)
        slots[me, 0:1, :] = nw_ref[...]
        slots[me, 1:2, :] = fw_ref[...]
        slots[me, 2:3, 0:HG_W] = hgw_ref[...]
        slots[me, 3:4, 0:HG_W] = lbl_ref[0:1, :]
        slots[me, 3:4, HG_W:] = lbl_ref[1:2, :]
        slots[me, 4:5, 0:128] = loss_ref[0:1, :]
        cps = []
        for k in range(1, 8):
            dx, dy, dc = (k >> 2) & 1, (k >> 1) & 1, k & 1
            to = (x ^ dx, y ^ dy, c ^ dc)
            cps.append(pltpu.make_async_remote_copy(
                src_ref=slots.at[me], dst_ref=slots.at[me], send_sem=send_sems.at[k - 1],
                recv_sem=recv_sems.at[k - 1], device_id=to, device_id_type=MESH))
        for cp in cps:
            cp.start()
        for cp in cps:
            cp.wait()
        acc = slots[0]
        for i in range(1, 8):
            acc = acc + slots[i]
        out_ref[...] = acc

    vm = pl.BlockSpec(memory_space=pltpu.VMEM)
    return pl.pallas_call(
        body, name="allreduce_small",
        in_specs=[vm] * 5, out_specs=vm,
        out_shape=jax.ShapeDtypeStruct((8, D_MODEL), F32),
        scratch_shapes=[pltpu.VMEM((8, 8, D_MODEL), F32), pltpu.SemaphoreType.DMA((7,)), pltpu.SemaphoreType.DMA((7,))],
    )(g_nw, g_fw, g_hgw, g_lbl, loss8)


def _rope_tables(S):
    inv_freq = (np.float32(1.0) / np.power(np.float32(ROPE_THETA), np.arange(ROPE_HALF, dtype=np.float32) / np.float32(ROPE_HALF))).astype(np.float32)
    ang = (np.arange(S, dtype=np.float32)[:, None] * inv_freq[None, :]).astype(np.float32)
    cos, sin = np.cos(ang).astype(np.float32), np.sin(ang).astype(np.float32)
    cos2 = np.concatenate([cos, cos, cos, cos], axis=-1)
    sin2 = np.concatenate([-sin, sin, -sin, sin], axis=-1)
    return jnp.asarray(cos2), jnp.asarray(sin2)


def _local_step(x2, tgt2, norm_w, w_in_full, lbl, hg_norm_w, w_out4, fnw):
    S = x2.shape[0]
    cos2, sin2 = _rope_tables(S)
    w_out_full = w_out4.reshape(D_MODEL, D_MODEL)
    p0, p1, qkv, qkv4, qkv16, z_at, u = _inproj(x2, norm_w, w_in_full, cos2, sin2)
    o_hg, states = _hg_fwd(p0, p1, lbl)
    qkv_ds = [qkv.reshape(1, S, 3 * AT_W), qkv4, qkv16]
    os_d, ls_d = zip(*[_att_fwd(q) for q in qkv_ds])
    o_at, lse, mix_at = _att_combine(os_d, ls_d, z_at)
    dh, do_hg, dz_hg, dm_at, g_wout, g_fw, g_hgw, loss8 = _outproj(x2, tgt2, mix_at, w_out_full, fnw, o_hg, p1, hg_norm_w)
    dqr, dfl, dv_hg, g_lbl = _hg_bwd(p0, p1, do_hg, states, lbl)
    do1, do4, do16, dl1, dl4, dl16, dz_at = _att_gate_bwd(dm_at, o_at, lse, z_at)
    do_ds = [do1.reshape(1, S, AT_W), do4, do16]
    dl_ds = [dl1.reshape(1, S, STAT_W), dl4, dl16]
    dqkvs = [_att_bwd(q, a, b) for q, a, b in zip(qkv_ds, do_ds, dl_ds)]
    dq_at, dk_at, dv_at = _att_bwd_combine(dqkvs, cos2, sin2)
    dps = [dqr, dfl, dv_hg, dz_hg, dq_at, dk_at, dv_at, dz_at]
    return loss8, dps, u, dh, g_lbl, g_hgw, g_wout, g_fw


def kernel(x, norm_w, w_in, hgrn_lb_logits, hg_norm_w, w_out, final_norm_w, loss_target, m_norm_w, m_w_in, m_hgrn_lb_logits, m_hg_norm_w, m_w_out, m_final_norm_w, v_norm_w, v_w_in, v_hgrn_lb_logits, v_hg_norm_w, v_w_out, v_final_norm_w):
    S = x.shape[1]
    w_in_full, w_out_full = _gather_weights(w_in[0], w_out[0])
    loss8, dps, u, dh, g_lbl, g_hgw, g_wout, g_fw = _local_step(
        x[0], loss_target[0], norm_w, w_in_full, hgrn_lb_logits, hg_norm_w,
        w_out_full, final_norm_w.reshape(1, D_MODEL))

    cidx = lax.axis_index("c").astype(jnp.int32).reshape(1)
    g_win = _inproj_bwd_w(u, dps)
    g_in4 = g_win.reshape(4, 2, 512, 1024)
    g_out4 = g_wout.reshape(4, 2, 128, D_MODEL)
    r_in, r_out = _swap_halves(g_in4, g_out4)
    cs_in = _add_half(g_in4, r_in, cidx, "add_half_in")
    cs_out = _add_half(g_out4, r_out, cidx, "add_half_out")
    *started, token = _exchange_start(cs_in, cs_out)
    grad_x, g_nw = _inproj_bwd_x(dps, w_in_full, x[0], norm_w + token[0:1, 0:1], dh)
    cs_in, cs_out, x_in, x_out = _exchange_wait(*started, g_nw)
    chip_idx = (2 * lax.axis_index("x") + lax.axis_index("y")).astype(jnp.int32).reshape(1)
    h_in = _sum_chips(cs_in, x_in, chip_idx, "sum_chips_in")
    h_out = _sum_chips(cs_out, x_out, chip_idx, "sum_chips_out")
    s_in, s_out = _swap_reduced(h_in, h_out)

    red = _allreduce_small(g_nw, g_fw, g_hgw, g_lbl, loss8)
    loss = red[4, 0]
    grad_norm_w = red[0:1, :]
    grad_final_norm_w = red[1, :]
    grad_hg_norm_w = red[2:3, :HG_W]
    grad_lbl = jnp.concatenate([red[3:4, :HG_W], red[3:4, HG_W:]], axis=0)

    d_nw, m_nw, v_nw = _adamw(norm_w, grad_norm_w, m_norm_w, v_norm_w, "adamw_norm_w")
    grad_w_in, d_win, m_win, v_win = _adamw_halves(w_in[0], h_in, s_in, m_w_in[0], v_w_in[0], cidx, "adamw_w_in")
    d_lbl, m_lbl, v_lbl = _adamw(hgrn_lb_logits, grad_lbl, m_hgrn_lb_logits, v_hgrn_lb_logits, "adamw_lb_logits")
    d_hgw, m_hgw, v_hgw = _adamw(hg_norm_w, grad_hg_norm_w, m_hg_norm_w, v_hg_norm_w, "adamw_hg_norm_w")
    grad_w_out, d_wout, m_wout, v_wout = _adamw_halves(w_out[0], h_out, s_out, m_w_out[0], v_w_out[0], cidx, "adamw_w_out")
    d_fw, m_fw, v_fw = _adamw(final_norm_w.reshape(1, D_MODEL), grad_final_norm_w.reshape(1, D_MODEL),
                              m_final_norm_w.reshape(1, D_MODEL), v_final_norm_w.reshape(1, D_MODEL), "adamw_final_norm_w")
    e1 = lambda a: a[None]
    flat = lambda a: a.reshape(D_MODEL)
    return (loss, grad_x[None], grad_norm_w, e1(grad_w_in), grad_lbl, grad_hg_norm_w, e1(grad_w_out), grad_final_norm_w,
            d_nw, e1(d_win), d_lbl, d_hgw, e1(d_wout), flat(d_fw),
            m_nw, e1(m_win), m_lbl, m_hgw, e1(m_wout), flat(m_fw),
            v_nw, e1(v_win), v_lbl, v_hgw, e1(v_wout), flat(v_fw))
```

```python
import jax
import jax.numpy as jnp
import numpy as np
from jax import lax
from jax.experimental import pallas as pl
from jax.experimental.pallas import tpu as pltpu

F32 = jnp.float32
BF16 = jnp.bfloat16
MESH = pl.DeviceIdType.MESH

D_MODEL = 1024
HG_HEADS = 4
HG_HEAD = 128
HG_W = HG_HEADS * HG_HEAD
AT_HEAD = 64
AT_W = 512
HEAD_PAIR = 2 * AT_HEAD
ROPE_HALF = 32
ROPE_THETA = 10000.0
EPS = 1e-6
CHUNK = 128
LEVELS = (64, 32, 16, 8, 4, 2, 1)
DIAG = 1
SUBLANES = 8
ATT_BLK = 128
ATT_QB = 4
ATT_SCALE = 0.125
STAT_W = 128
STAT_LANES = 16
STAT_LSE_LANE = 8
NEG = -1e30
VMEM_LIMIT = 56 * 1024 * 1024
MIX_TM = 1024
PROJ_TM = 512

ADAM_LR = 0.001
ADAM_B1 = 0.9
ADAM_B2 = 0.999
ADAM_EPS = 1e-08
ADAM_WD = 0.01
ADAM_STEP = 10


def _iota(shape, dim):
    return lax.broadcasted_iota(jnp.int32, shape, dim)


def _dot(a, b):
    return jnp.dot(a, b, preferred_element_type=F32)


def _dot_nt(a, b):
    return lax.dot_general(a, b, (((1,), (1,)), ((), ())), preferred_element_type=F32)


def _dot_tn(a, b):
    return lax.dot_general(a, b, (((0,), (0,)), ((), ())), preferred_element_type=F32)


def _sigmoid(v):
    return 0.5 * jnp.tanh(0.5 * v) + 0.5


def _params(sem=None, vmem=VMEM_LIMIT):
    return pltpu.CompilerParams(dimension_semantics=sem, vmem_limit_bytes=vmem)


def _full(shape):
    n = len(shape)
    return pl.BlockSpec(shape, lambda *_: (0,) * n)


def _rope_rot(y):
    n = y.shape[1]
    first = (_iota(y.shape, 1) & (2 * ROPE_HALF - 1)) < ROPE_HALF
    return jnp.where(first, pltpu.roll(y, n - ROPE_HALF, 1), pltpu.roll(y, ROPE_HALF, 1))


def _dil_spec(d, tm, width):
    return pl.BlockSpec((d, tm // d, width), lambda s: (0, s, 0))


LANES = 128


def _slab_scratch(tm, width):
    return pltpu.VMEM((width // LANES, tm, LANES), F32)


def _to_slabs(v, slabs_ref):
    for j in range(slabs_ref.shape[0]):
        slabs_ref[j] = v[:, LANES * j: LANES * (j + 1)]


def _from_slabs(slabs_ref):
    return jnp.concatenate([slabs_ref[j] for j in range(slabs_ref.shape[0])], axis=1)


def _split_residues(slabs_ref, dst_ref, d, dtype):
    nslab, tm, _ = slabs_ref.shape
    for r in range(d):
        for j in range(nslab):
            dst_ref[r, :, LANES * j: LANES * (j + 1)] = slabs_ref[j, pl.ds(r, tm // d, stride=d), :].astype(dtype)


def _merge_residues(src_ref, slabs_ref, d):
    nslab, tm, _ = slabs_ref.shape
    for r in range(d):
        for j in range(nslab):
            slabs_ref[j, pl.ds(r, tm // d, stride=d), :] = src_ref[r, :, LANES * j: LANES * (j + 1)].astype(F32)


PERM_ROWS = 256


def _perm_matrix(d):
    n = PERM_ROWS // d
    t = np.arange(PERM_ROWS)
    p = np.zeros((PERM_ROWS, PERM_ROWS), np.float32)
    p[t, (t % d) * n + t // d] = 1.0
    return jnp.asarray(p, BF16)


def _merge_bf16(src_ref, perm_ref, d):
    n = PERM_ROWS // d
    tm = src_ref.shape[1] * d
    outs = []
    for b in range(tm // PERM_ROWS):
        flat = jnp.concatenate([src_ref[r, n * b: n * (b + 1), :] for r in range(d)], axis=0)
        outs.append(_dot(perm_ref[...], flat))
    return jnp.concatenate(outs, axis=0)


def _inproj(x2, norm_w, w_in_full, cos2, sin2):
    S = x2.shape[0]
    tm = PROJ_TM

    def body(x_ref, nw_ref, w_ref, cos_ref, sin_ref, p0_ref, p1_ref, qkv_ref, qkv4_ref, qkv16_ref, z_ref, u_ref, scr):
        x = x_ref[...]
        r = lax.rsqrt(jnp.mean(x * x, axis=-1, keepdims=True) + EPS)
        u = x * r * nw_ref[...]
        ub = u.astype(BF16)
        u_ref[...] = ub
        p0_ref[...] = _dot(ub, w_ref[0])
        p1_ref[...] = _dot(ub, w_ref[1])
        y2 = _dot(ub, w_ref[2])
        cosf = jnp.tile(cos_ref[...], (1, 8))
        sinf = jnp.tile(sin_ref[...], (1, 8))
        y3 = _dot(ub, w_ref[3])
        z_ref[...] = y3[:, AT_W:]
        qkv = jnp.concatenate([y2 * cosf + _rope_rot(y2) * sinf, y3[:, :AT_W]], axis=1)
        qkv_ref[...] = qkv.astype(BF16)
        _to_slabs(qkv, scr)
        _split_residues(scr, qkv4_ref, 4, BF16)
        _split_residues(scr, qkv16_ref, 16, BF16)

    row = lambda w: pl.BlockSpec((tm, w), lambda s: (s, 0))
    qkv_w = 3 * AT_W
    return pl.pallas_call(
        body, name="inproj", grid=(S // tm,),
        in_specs=[row(D_MODEL), _full((1, D_MODEL)), _full((4, D_MODEL, 1024)), row(128), row(128)],
        out_specs=[row(1024), row(1024), row(qkv_w), _dil_spec(4, tm, qkv_w), _dil_spec(16, tm, qkv_w), row(AT_W),
                   row(D_MODEL)],
        out_shape=[jax.ShapeDtypeStruct((S, 1024), F32), jax.ShapeDtypeStruct((S, 1024), F32),
                   jax.ShapeDtypeStruct((S, qkv_w), BF16), jax.ShapeDtypeStruct((4, S // 4, qkv_w), BF16),
                   jax.ShapeDtypeStruct((16, S // 16, qkv_w), BF16), jax.ShapeDtypeStruct((S, AT_W), F32),
                   jax.ShapeDtypeStruct((S, D_MODEL), BF16)],
        scratch_shapes=[_slab_scratch(tm, qkv_w)],
        compiler_params=_params(("parallel",)),
    )(x2, norm_w, w_in_full, cos2, sin2)


HG_HPS = 4
HG_CPS = 4
N_LEV = len(LEVELS)


def _hg_const_arrays():
    r = np.arange(CHUNK)[:, None]
    c = np.arange(CHUNK)[None, :]
    tris = np.stack([r >= c, r <= c])
    lm = [((r // (2 * m)) == (c // (2 * m))) & (r % (2 * m) >= m) & (c % (2 * m) < m) for m in LEVELS]
    dm = [(c == r - dl) & (r % DIAG >= dl) for dl in range(DIAG)]
    masks = np.stack(lm + dm)
    return jnp.asarray(tris, BF16), jnp.asarray(masks, F32)


def _split2(a):
    hi = a.astype(BF16)
    return hi, (a - hi.astype(F32)).astype(BF16)


def _dot3(a, b, dot=_dot):
    ah, al = _split2(a)
    bh, bl = _split2(b)
    n = b.shape[1]
    p = dot(ah, jnp.concatenate([bh, bl], axis=1))
    return (p[:, :n] + p[:, n:]) + dot(al, bh)


def _split3(a):
    a1 = a.astype(BF16)
    r1 = a - a1.astype(F32)
    a2 = r1.astype(BF16)
    return a1, a2, (r1 - a2.astype(F32)).astype(BF16)


def _tri_dot(tri, a):
    n = a.shape[1]
    p = _dot(tri, jnp.concatenate(_split3(a), axis=1))
    return (p[:, :n] + p[:, n:2 * n]) + p[:, 2 * n:]


def _dot_sel(a, sel):
    a1, a2, a3 = _split3(a)
    return (_dot(a1, sel) + _dot(a2, sel)) + _dot(a3, sel)


def _rowsum(t):
    return _dot(t.astype(BF16), jnp.ones((t.shape[1], t.shape[1]), BF16))


def _level_refs(b):
    refs = []
    pos = _iota(b.shape, 0)
    for m in LEVELS:
        if 2 * m >= SUBLANES:
            parts = [jnp.broadcast_to(b[r0 + m - 1: r0 + m, :], (2 * m, b.shape[1])) for r0 in range(0, CHUNK, 2 * m)]
            refs.append(parts[0] if len(parts) == 1 else jnp.concatenate(parts, axis=0))
        else:
            p = pos & (2 * m - 1)
            ref = b
            for off in range(-(m - 1), m + 1):
                if off != 0:
                    ref = jnp.where(p == m - 1 + off, pltpu.roll(b, off % CHUNK, 0), ref)
            refs.append(ref)
    return refs


def _hg_lb(lbl_ref):
    l0 = lbl_ref[0:1, :]
    l1 = lbl_ref[1:2, :]
    mx = jnp.maximum(l0, l1)
    e0 = jnp.exp(l0 - mx)
    e1 = jnp.exp(l1 - mx)
    p0 = e0 / (e0 + e1)
    lb = jnp.clip(p0, 1e-6, 1.0 - 1e-6)
    inside = (p0 >= 1e-6) & (p0 <= 1.0 - 1e-6)
    dlb_dl0 = jnp.where(inside, p0 * (e1 / (e0 + e1)), 0.0)
    return lb, dlb_dl0


def _hg_gates(qr, fl, lb):
    sig = _sigmoid(fl)
    f = lb + (1.0 - lb) * sig
    g = jnp.log(f)
    k = (1.0 - lb) * (1.0 - sig)
    sq = _sigmoid(qr)
    q = qr * sq
    return sig, f, g, k, sq, q


def _neg_abs(v):
    bits = lax.bitcast_convert_type(v, jnp.uint32) | jnp.uint32(0x80000000)
    return lax.bitcast_convert_type(bits, F32)


def _hg_levels(q, k, b, mk_ref):
    refs = _level_refs(b)
    a = jnp.zeros((CHUNK, CHUNK), F32)
    es, qts, kts = [], [], []
    for i in range(N_LEV):
        e = jnp.exp(_neg_abs(b - refs[i]))
        qt = (q * e).astype(BF16)
        kt = (k * e).astype(BF16)
        a = a + _dot_nt(qt, kt) * mk_ref[i]
        es.append(e); qts.append(qt); kts.append(kt)
    return a, es, qts, kts


def _hg_specs(nc, rev):
    cc = (lambda c: nc - 1 - c) if rev else (lambda c: c)
    w = HG_HEAD * HG_HPS
    blk = lambda off: pl.BlockSpec((HG_CPS * CHUNK, w), lambda h, c: (cc(c), h + off))
    lb2 = pl.BlockSpec((2, w), lambda h, c: (0, h))
    st = pl.BlockSpec((HG_CPS, HG_HPS, HG_HEAD, HG_HEAD), lambda h, c: (cc(c), h, 0, 0))
    consts = [_full((2, CHUNK, CHUNK)), _full((N_LEV + DIAG, CHUNK, CHUNK))]
    return blk, lb2, st, consts


def _hg_fwd(p0, p1, lbl):
    S = p0.shape[0]
    nc = S // (HG_CPS * CHUNK)
    ng = HG_HEADS // HG_HPS

    def body(qr_ref, fl_ref, v_ref, lbl_ref, tri_ref, mk_ref, o_ref, st_ref, state):
        c = pl.program_id(1)

        @pl.when(c == 0)
        def _():
            state[...] = jnp.zeros_like(state)

        lb_all, _ = _hg_lb(lbl_ref)
        heads = [slice(HG_HEAD * hh, HG_HEAD * (hh + 1)) for hh in range(HG_HPS)]
        for j in range(HG_CPS):
            rows = slice(CHUNK * j, CHUNK * (j + 1))
            qs, ks, bs, mats = [], [], [], []
            for sl in heads:
                _, _, g, k, _, q = _hg_gates(qr_ref[rows, sl], fl_ref[rows, sl], lb_all[:, sl])
                qs.append(q); ks.append(k); bs.append(_tri_dot(tri_ref[0], g))
            for hh in range(HG_HPS):
                a, _, _, _ = _hg_levels(qs[hh], ks[hh], bs[hh], mk_ref)
                mats.append(a + _rowsum(qs[hh] * ks[hh]) * mk_ref[N_LEV])
            for hh, sl in enumerate(heads):
                q, k, b, a, v = qs[hh], ks[hh], bs[hh], mats[hh], v_ref[rows, sl]
                b_last = b[CHUNK - 1: CHUNK, :]
                st = state[hh]
                st_ref[j, hh] = st
                o = _dot_nt((q * jnp.exp(b)).astype(BF16), st.astype(BF16)) + _dot(a.astype(BF16), v.astype(BF16))
                state[hh] = st * jnp.exp(b_last) + _dot3(v, k * jnp.exp(b_last - b), _dot_tn)
                o_ref[rows, sl] = o

    blk, lb2, st_spec, consts = _hg_specs(nc, False)
    tris, masks = _hg_const_arrays()
    return pl.pallas_call(
        body, name="hg_fwd", grid=(ng, nc),
        in_specs=[blk(0), blk(ng), blk(0), lb2] + consts,
        out_specs=[blk(0), st_spec],
        out_shape=[jax.ShapeDtypeStruct((S, HG_W), F32), jax.ShapeDtypeStruct((S // CHUNK, HG_HEADS, HG_HEAD, HG_HEAD), F32)],
        scratch_shapes=[pltpu.VMEM((HG_HPS, HG_HEAD, HG_HEAD), F32)],
        compiler_params=_params(("parallel", "arbitrary")),
    )(p0, p0, p1, lbl, tris, masks)


def _hg_bwd(p0, p1, do_hg, states, lbl):
    S = p0.shape[0]
    nc = S // (HG_CPS * CHUNK)
    ng = HG_HEADS // HG_HPS
    w = HG_HEAD * HG_HPS

    def body(qr_ref, fl_ref, v_ref, do_ref, st_ref, lbl_ref, tri_ref, mk_ref, mkb_ref,
             dqr_ref, dfl_ref, dv_ref, glbl_ref, dstate, carry, acc_lb):
        c = pl.program_id(1)

        @pl.when(c == 0)
        def _():
            dstate[...] = jnp.zeros_like(dstate)
            carry[...] = jnp.zeros_like(carry)
            acc_lb[...] = jnp.zeros_like(acc_lb)

        lb_all, dlb_dl0 = _hg_lb(lbl_ref)
        heads = [slice(HG_HEAD * hh, HG_HEAD * (hh + 1)) for hh in range(HG_HPS)]
        diag_mask = mk_ref[N_LEV]
        for j in reversed(range(HG_CPS)):
            rows = slice(CHUNK * j, CHUNK * (j + 1))
            gates, bs, dos = [], [], []
            for sl in heads:
                gt = _hg_gates(qr_ref[rows, sl], fl_ref[rows, sl], lb_all[:, sl])
                gates.append(gt)
                bs.append(_tri_dot(tri_ref[0], gt[2]))
            for sl in heads:
                dos.append(do_ref[rows, sl])
            inter = []
            for hh, sl in enumerate(heads):
                _, _, _, k, _, q = gates[hh]
                b, do, v = bs[hh], dos[hh], v_ref[rows, sl]
                b_last = b[CHUNK - 1: CHUNK, :]
                eb = jnp.exp(b)
                edec = jnp.exp(b_last - b)
                dst = dstate[hh]
                dq = _dot3(do, st_ref[j, hh]) * eb
                dk = _dot3(v, dst) * edec
                da = _dot_nt(do.astype(BF16), v.astype(BF16))
                dv_state = _dot_nt((k * edec).astype(BF16), dst.astype(BF16))
                dstate[hh] = dst * jnp.exp(b_last) + _dot3(do, q * eb, _dot_tn)
                inter.append((dq, dk, da, dv_state))
            for hh, sl in enumerate(heads):
                sig, f, _, k, sq, q = gates[hh]
                dq, dk, da, dv_state = inter[hh]
                b = bs[hh]
                db = q * dq - k * dk
                dab = da.astype(BF16)
                refs = _level_refs(b)
                a = _rowsum(q * k) * diag_mask
                for i in range(N_LEV):
                    e = jnp.exp(_neg_abs(b - refs[i]))
                    qt = (q * e).astype(BF16)
                    kt = (k * e).astype(BF16)
                    a = a + _dot_nt(qt, kt) * mk_ref[i]
                    dam = dab * mkb_ref[i]
                    gq = _dot(dam, kt)
                    gk = _dot_tn(dam, qt)
                    dq = dq + e * gq
                    dk = dk + e * gk
                    db = db + (qt.astype(F32) * gq - kt.astype(F32) * gk)
                dc = _rowsum(da * diag_mask)
                dq = dq + dc * k
                dk = dk + dc * q
                dv_ref[rows, sl] = (_dot_tn(a.astype(BF16), dos[hh].astype(BF16)) + dv_state).astype(BF16)
                dg = _tri_dot(tri_ref[1], db) + carry[0:1, sl]
                carry[0:1, sl] += jnp.sum(db, axis=0, keepdims=True)
                lb = lb_all[:, sl]
                qr = qr_ref[rows, sl]
                t = dg / f - dk
                dfl_ref[rows, sl] = (t * (1.0 - lb) * sig * (1.0 - sig)).astype(BF16)
                acc_lb[0:1, sl] += jnp.sum(t * (1.0 - sig), axis=0, keepdims=True)
                dqr_ref[rows, sl] = (dq * (sq * (1.0 + qr * (1.0 - sq)))).astype(BF16)

        @pl.when(c == nc - 1)
        def _():
            gl0 = acc_lb[0:1, :] * dlb_dl0
            glbl_ref[0:1, :] = gl0
            glbl_ref[1:2, :] = -gl0

    blk, lb2, st_spec, consts = _hg_specs(nc, True)
    tris, masks = _hg_const_arrays()
    act = jax.ShapeDtypeStruct((S, HG_W), BF16)
    return pl.pallas_call(
        body, name="hg_bwd", grid=(ng, nc),
        in_specs=[blk(0), blk(ng), blk(0), blk(0), st_spec, lb2] + consts + consts[1:],
        out_specs=[blk(0), blk(0), blk(0), lb2],
        out_shape=[act, act, act, jax.ShapeDtypeStruct((2, HG_W), F32)],
        scratch_shapes=[pltpu.VMEM((HG_HPS, HG_HEAD, HG_HEAD), F32), pltpu.VMEM((8, w), F32), pltpu.VMEM((8, w), F32)],
        compiler_params=_params(("parallel", "arbitrary")),
    )(p0, p0, p1, do_hg, states, lbl, tris, masks, masks.astype(BF16))


def _stat_head(lane):
    return lane >> 4


def _stat_lane(lane):
    return lane & (STAT_LANES - 1)


def _o_head(lane):
    return lane >> 6


def _att_bias():
    qi = np.arange(ATT_BLK)[:, None]
    kj = np.arange(2 * ATT_BLK)[None, :]
    band = (kj >= qi) & (kj <= qi + ATT_BLK)
    qm = np.stack([band & (kj >= ATT_BLK), band])
    cur = (kj < ATT_BLK) & (qi <= kj)
    km = np.stack([cur, cur | ((kj >= ATT_BLK) & (qi >= kj - ATT_BLK))])
    to_bias = lambda m: jnp.asarray(np.where(m, 0.0, NEG), F32)
    return to_bias(qm), to_bias(km)


def _att_fwd(qkv_d):
    d, L, _ = qkv_d.shape
    qb = ATT_QB
    rows = qb * ATT_BLK

    def body(q_ref, kp_ref, kc_ref, vp_ref, vc_ref, bias_ref, o_ref, lse_ref, s_scr, p_scr):
        first = _iota((ATT_BLK, HEAD_PAIR), 1) < AT_HEAD
        head_of_lane = _stat_head(_iota((ATT_BLK, STAT_W), 1))
        pairs = [slice(HEAD_PAIR * hp, HEAD_PAIR * (hp + 1)) for hp in range(4)]
        blk = lambda i: slice(ATT_BLK * i, ATT_BLK * (i + 1))

        def keys(i, sl, prev_ref, cur_ref):
            before = prev_ref[0, :, sl] if i == 0 else cur_ref[0, blk(i - 1), sl]
            return jnp.concatenate([before, cur_ref[0, blk(i), sl]], axis=0)

        for i in range(qb):
            for hp, sl in enumerate(pairs):
                q2 = q_ref[0, blk(i), sl] * ATT_SCALE
                zero = jnp.zeros_like(q2)
                qs = jnp.concatenate([jnp.where(first, q2, zero), jnp.where(first, zero, q2)], axis=0)
                s_scr[4 * i + hp] = _dot_nt(qs, keys(i, sl, kp_ref, kc_ref))
        stats = []
        for i in range(qb):
            bias = bias_ref[jnp.minimum(pl.program_id(1), 1)] if i == 0 else bias_ref[1]
            bias2 = jnp.concatenate([bias, bias], axis=0)
            for hp in range(4):
                s = s_scr[4 * i + hp] + bias2
                m = jnp.max(s, axis=-1, keepdims=True)
                p = jnp.exp(s - m)
                l = jnp.sum(p, axis=-1, keepdims=True)
                p_scr[4 * i + hp] = p.astype(BF16)
                stats.append((l, m + jnp.log(l)))
        for i in range(qb):
            lse_blk = jnp.zeros((ATT_BLK, STAT_W), F32)
            for hp, sl in enumerate(pairs):
                l, lse = stats[4 * i + hp]
                o = _dot(p_scr[4 * i + hp], keys(i, sl, vp_ref, vc_ref)) / l
                o_ref[0, blk(i), sl] = jnp.where(first, o[:ATT_BLK], o[ATT_BLK:]).astype(BF16)
                lse_blk = jnp.where(head_of_lane == 2 * hp, lse[:ATT_BLK],
                                    jnp.where(head_of_lane == 2 * hp + 1, lse[ATT_BLK:], lse_blk))
            lse_ref[0, blk(i), :] = lse_blk

    cur = lambda j: pl.BlockSpec((1, rows, AT_W), lambda r, n: (r, n, j))
    prev = lambda j: pl.BlockSpec((1, ATT_BLK, AT_W), lambda r, n: (r, jnp.maximum(qb * n - 1, 0), j))
    return pl.pallas_call(
        body, name=f"att_fwd_d{d}", grid=(d, L // rows),
        in_specs=[cur(0), prev(1), cur(1), prev(2), cur(2), _full((2, ATT_BLK, 2 * ATT_BLK))],
        out_specs=[pl.BlockSpec((1, rows, AT_W), lambda r, n: (r, n, 0)), pl.BlockSpec((1, rows, STAT_W), lambda r, n: (r, n, 0))],
        out_shape=[jax.ShapeDtypeStruct((d, L, AT_W), BF16), jax.ShapeDtypeStruct((d, L, STAT_W), F32)],
        scratch_shapes=[pltpu.VMEM((4 * qb, 2 * ATT_BLK, 2 * ATT_BLK), F32), pltpu.VMEM((4 * qb, 2 * ATT_BLK, 2 * ATT_BLK), BF16)],
        compiler_params=_params(("parallel", "parallel")),
    )(qkv_d, qkv_d, qkv_d, qkv_d, qkv_d, _att_bias()[0])


def _att_combine(os_d, ls_d, z_at):
    S = z_at.shape[0]
    tm = MIX_TM

    def body(oa_ref, ob4_ref, oc16_ref, la_ref, lb4_ref, lc16_ref, p4_ref, p16_ref, z_ref,
             oat_ref, lse_ref, mix_ref, lb_scr, lc_scr):
        _merge_residues(lb4_ref, lb_scr, 4)
        _merge_residues(lc16_ref, lc_scr, 16)
        ls = (la_ref[...], lb_scr[0], lc_scr[0])
        mx = jnp.maximum(jnp.maximum(ls[0], ls[1]), ls[2])
        es = [jnp.exp(l - mx) for l in ls]
        zs = es[0] + es[1] + es[2]
        lse_ref[...] = mx + jnp.log(zs)
        spread = ((_o_head(_iota((STAT_W, AT_W), 1)) == _stat_head(_iota((STAT_W, AT_W), 0)))
                  & (_stat_lane(_iota((STAT_W, AT_W), 0)) == 0)).astype(BF16)
        os_ = (oa_ref[...].astype(F32), _merge_bf16(ob4_ref, p4_ref, 4), _merge_bf16(oc16_ref, p16_ref, 16))
        o = jnp.zeros((tm, AT_W), F32)
        for e, oi in zip(es, os_):
            hi, lo = _split2(e / zs)
            o = o + (_dot(hi, spread) + _dot(lo, spread)) * oi
        oat_ref[...] = o
        z = z_ref[...]
        mixed = o * (z * _sigmoid(z))
        mix_ref[...] = mixed.astype(BF16)

    row = lambda w: pl.BlockSpec((tm, w), lambda s: (s, 0))
    return pl.pallas_call(
        body, name="att_combine", grid=(S // tm,),
        in_specs=[row(AT_W), _dil_spec(4, tm, AT_W), _dil_spec(16, tm, AT_W),
                  row(STAT_W), _dil_spec(4, tm, STAT_W), _dil_spec(16, tm, STAT_W),
                  _full((PERM_ROWS, PERM_ROWS)), _full((PERM_ROWS, PERM_ROWS)), row(AT_W)],
        out_specs=[row(AT_W), row(STAT_W), row(AT_W)],
        out_shape=[jax.ShapeDtypeStruct((S, AT_W), F32), jax.ShapeDtypeStruct((S, STAT_W), F32),
                   jax.ShapeDtypeStruct((S, AT_W), BF16)],
        scratch_shapes=[_slab_scratch(tm, STAT_W), _slab_scratch(tm, STAT_W)],
        compiler_params=_params(("parallel",)),
    )(os_d[0].reshape(S, AT_W), os_d[1], os_d[2], ls_d[0].reshape(S, STAT_W), ls_d[1], ls_d[2],
      _perm_matrix(4), _perm_matrix(16), z_at)


def _att_gate_bwd(dm_at, o_at, lse, z_at):
    S = z_at.shape[0]
    tm = MIX_TM

    def body(dm_ref, o_ref, l_ref, z_ref, do_ref, do4_ref, do16_ref, dl_ref, dl4_ref, dl16_ref, dz_ref, do_scr, dl_scr):
        o = o_ref[...]
        z = z_ref[...]
        dm = dm_ref[...]
        sz = _sigmoid(z)
        dz_ref[...] = (dm * o * (sz * (1.0 + z * (1.0 - sz)))).astype(BF16)
        do = dm * (z * sz)
        do_ref[...] = do.astype(BF16)
        gather = (_o_head(_iota((AT_W, STAT_W), 0)) == _stat_head(_iota((AT_W, STAT_W), 1))).astype(BF16)
        dl = jnp.where(_stat_lane(_iota((tm, STAT_W), 1)) < STAT_LSE_LANE, _dot_sel(do * o, gather), l_ref[...])
        dl_ref[...] = dl
        _to_slabs(do, do_scr)
        _to_slabs(dl, dl_scr)
        _split_residues(do_scr, do4_ref, 4, BF16)
        _split_residues(do_scr, do16_ref, 16, BF16)
        _split_residues(dl_scr, dl4_ref, 4, F32)
        _split_residues(dl_scr, dl16_ref, 16, F32)

    row = lambda w: pl.BlockSpec((tm, w), lambda s: (s, 0))
    sds = jax.ShapeDtypeStruct
    return pl.pallas_call(
        body, name="att_gate_bwd", grid=(S // tm,),
        in_specs=[row(AT_W), row(AT_W), row(STAT_W), row(AT_W)],
        out_specs=[row(AT_W), _dil_spec(4, tm, AT_W), _dil_spec(16, tm, AT_W),
                   row(STAT_W), _dil_spec(4, tm, STAT_W), _dil_spec(16, tm, STAT_W), row(AT_W)],
        out_shape=[sds((S, AT_W), BF16), sds((4, S // 4, AT_W), BF16), sds((16, S // 16, AT_W), BF16),
                   sds((S, STAT_W), F32), sds((4, S // 4, STAT_W), F32), sds((16, S // 16, STAT_W), F32),
                   sds((S, AT_W), BF16)],
        scratch_shapes=[_slab_scratch(tm, AT_W), _slab_scratch(tm, STAT_W)],
        compiler_params=_params(("parallel",)),
    )(dm_at, o_at, lse, z_at)


def _att_bwd(qkv_d, do_d, dl_d):
    d, L, _ = qkv_d.shape
    nb = L // ATT_BLK
    qb = ATT_QB
    rows = qb * ATT_BLK
    nsteps = L // rows

    def body(qc_ref, qn_ref, kp_ref, kc_ref, vp_ref, vc_ref, ac_ref, an_ref, lc_ref, ln_ref, bq_ref, bk_ref, dqkv_ref,
             s_scr, dp_scr, st_scr, dpt_scr, ds_scr, pt_scr, dst_scr):
        n = pl.program_id(1)
        first = _iota((ATT_BLK, HEAD_PAIR), 1) < AT_HEAD
        pairs = [slice(HEAD_PAIR * hp, HEAD_PAIR * (hp + 1)) for hp in range(4)]
        blk = lambda i: slice(ATT_BLK * i, ATT_BLK * (i + 1))

        def stack(t):
            zero = jnp.zeros_like(t)
            return jnp.concatenate([jnp.where(first, t, zero), jnp.where(first, zero, t)], axis=0)

        def unstack(t2):
            return jnp.where(first, t2[:ATT_BLK], t2[ATT_BLK:])

        def with_prev(i, sl, prev_ref, cur_ref):
            before = prev_ref[0, :, sl] if i == 0 else cur_ref[0, blk(i - 1), sl]
            return jnp.concatenate([before, cur_ref[0, blk(i), sl]], axis=0)

        def with_next(i, sl, cur_ref, next_ref):
            after = next_ref[0, :, sl] if i == qb - 1 else cur_ref[0, blk(i + 1), sl]
            return jnp.concatenate([cur_ref[0, blk(i), sl], after], axis=0)

        for i in range(qb):
            for hp, sl in enumerate(pairs):
                j = 4 * i + hp
                s_scr[j] = _dot_nt(stack(qc_ref[0, blk(i), sl] * ATT_SCALE), with_prev(i, sl, kp_ref, kc_ref))
                dp_scr[j] = _dot_nt(stack(ac_ref[0, blk(i), sl]), with_prev(i, sl, vp_ref, vc_ref))
                st_scr[j] = _dot_nt(stack(kc_ref[0, blk(i), sl] * ATT_SCALE), with_next(i, sl, qc_ref, qn_ref))
                dpt_scr[j] = _dot_nt(stack(vc_ref[0, blk(i), sl]), with_next(i, sl, ac_ref, an_ref))
        for i in range(qb):
            bias = bq_ref[jnp.minimum(n, 1)] if i == 0 else bq_ref[1]
            bias_t = bk_ref[jnp.minimum(nsteps - 1 - n, 1)] if i == qb - 1 else bk_ref[1]
            bias2 = jnp.concatenate([bias, bias], axis=0)
            bias_t2 = jnp.concatenate([bias_t, bias_t], axis=0)
            dl_c = lc_ref[0, blk(i), :]
            dl_t = with_next(i, slice(None), lc_ref, ln_ref).T
            for hp in range(4):
                j = 4 * i + hp
                at = [STAT_LANES * (2 * hp), STAT_LANES * (2 * hp + 1)]
                col = lambda t, o: jnp.concatenate([t[:, a + o: a + o + 1] for a in at], axis=0)
                p = jnp.exp(s_scr[j] + bias2 - col(dl_c, STAT_LSE_LANE))
                ds_scr[j] = (p * (dp_scr[j] - col(dl_c, 0))).astype(BF16)
                row = lambda t, o: jnp.concatenate([jnp.broadcast_to(t[a + o: a + o + 1, :], (ATT_BLK, 2 * ATT_BLK)) for a in at], axis=0)
                pt = jnp.exp(st_scr[j] + bias_t2 - row(dl_t, STAT_LSE_LANE))
                pt_scr[j] = pt.astype(BF16)
                dst_scr[j] = (pt * (dpt_scr[j] - row(dl_t, 0))).astype(BF16)
        for i in range(qb):
            for hp, sl in enumerate(pairs):
                j = 4 * i + hp
                dq = unstack(_dot(ds_scr[j], with_prev(i, sl, kp_ref, kc_ref))) * ATT_SCALE
                dk = unstack(_dot(dst_scr[j], with_next(i, sl, qc_ref, qn_ref))) * ATT_SCALE
                dv = unstack(_dot(pt_scr[j], with_next(i, sl, ac_ref, an_ref)))
                dqkv_ref[0, blk(i), sl] = dq.astype(BF16)
                dqkv_ref[0, blk(i), AT_W + HEAD_PAIR * hp: AT_W + HEAD_PAIR * (hp + 1)] = dk.astype(BF16)
                dqkv_ref[0, blk(i), 2 * AT_W + HEAD_PAIR * hp: 2 * AT_W + HEAD_PAIR * (hp + 1)] = dv.astype(BF16)

    cur = lambda j: pl.BlockSpec((1, rows, AT_W), lambda r, n: (r, n, j))
    prev = lambda j: pl.BlockSpec((1, ATT_BLK, AT_W), lambda r, n: (r, jnp.maximum(qb * n - 1, 0), j))
    nxt_blk = lambda n: jnp.minimum(qb * (n + 1), nb - 1)
    sq = (4 * qb, 2 * ATT_BLK, 2 * ATT_BLK)
    return pl.pallas_call(
        body, name=f"att_bwd_d{d}", grid=(d, nsteps),
        in_specs=[cur(0), pl.BlockSpec((1, ATT_BLK, AT_W), lambda r, n: (r, nxt_blk(n), 0)), prev(1), cur(1), prev(2), cur(2),
                  pl.BlockSpec((1, rows, AT_W), lambda r, n: (r, n, 0)),
                  pl.BlockSpec((1, ATT_BLK, AT_W), lambda r, n: (r, nxt_blk(n), 0)),
                  pl.BlockSpec((1, rows, STAT_W), lambda r, n: (r, n, 0)),
                  pl.BlockSpec((1, ATT_BLK, STAT_W), lambda r, n: (r, nxt_blk(n), 0)),
                  _full((2, ATT_BLK, 2 * ATT_BLK)), _full((2, ATT_BLK, 2 * ATT_BLK))],
        out_specs=pl.BlockSpec((1, rows, 3 * AT_W), lambda r, n: (r, n, 0)),
        out_shape=jax.ShapeDtypeStruct((d, L, 3 * AT_W), BF16),
        scratch_shapes=[pltpu.VMEM(sq, F32)] * 4 + [pltpu.VMEM(sq, BF16)] * 3,
        compiler_params=_params(("parallel", "parallel")),
    )(qkv_d, qkv_d, qkv_d, qkv_d, qkv_d, qkv_d, do_d, do_d, dl_d, dl_d, *_att_bias())


def _att_bwd_combine(dqkvs, cos2, sin2):
    S = dqkvs[0].shape[1]
    tm = MIX_TM

    def body(a_ref, b4_ref, c16_ref, p4_ref, p16_ref, cos_ref, sin_ref, dq_ref, dk_ref, dv_ref):
        t = a_ref[...].astype(F32) + _merge_bf16(b4_ref, p4_ref, 4) + _merge_bf16(c16_ref, p16_ref, 16)
        dy = t[:, : 2 * AT_W]
        cosf = jnp.tile(cos_ref[...], (1, 8))
        sinf = jnp.tile(sin_ref[...], (1, 8))
        dx = dy * cosf - _rope_rot(dy) * sinf
        dq_ref[...] = dx[:, :AT_W].astype(BF16)
        dk_ref[...] = dx[:, AT_W:].astype(BF16)
        dv_ref[...] = t[:, 2 * AT_W:].astype(BF16)

    row = lambda w: pl.BlockSpec((tm, w), lambda s: (s, 0))
    act = jax.ShapeDtypeStruct((S, AT_W), BF16)
    return pl.pallas_call(
        body, name="att_bwd_combine", grid=(S // tm,),
        in_specs=[row(3 * AT_W), _dil_spec(4, tm, 3 * AT_W), _dil_spec(16, tm, 3 * AT_W),
                  _full((PERM_ROWS, PERM_ROWS)), _full((PERM_ROWS, PERM_ROWS)), row(128), row(128)],
        out_specs=[row(AT_W), row(AT_W), row(AT_W)],
        out_shape=[act, act, act],
        compiler_params=_params(("parallel",)),
    )(dqkvs[0].reshape(S, 3 * AT_W), dqkvs[1], dqkvs[2], _perm_matrix(4), _perm_matrix(16), cos2, sin2)


def _outproj(x2, tgt2, mix_at, w_out_full, fnw, o_hg, p1, hg_norm_w):
    S = x2.shape[0]
    tm = PROJ_TM
    ns = S // tm

    def body(x_ref, t_ref, ma_ref, w_ref, fw_ref, o_ref, z_ref, hgw_ref,
             dh_ref, doh_ref, dzh_ref, dma_ref, gw_ref, gfw_ref, ghgw_ref, loss_ref):
        s = pl.program_id(0)

        @pl.when(s == 0)
        def _():
            gw_ref[...] = jnp.zeros_like(gw_ref)
            gfw_ref[...] = jnp.zeros_like(gfw_ref)
            ghgw_ref[...] = jnp.zeros_like(ghgw_ref)
            loss_ref[...] = jnp.zeros_like(loss_ref)

        heads = [slice(HG_HEAD * hh, HG_HEAD * (hh + 1)) for hh in range(HG_HEADS)]
        norm = []
        for sl in heads:
            o, z = o_ref[:, sl], z_ref[:, sl]
            rs = lax.rsqrt(jnp.mean(o * o, axis=-1, keepdims=True) + EPS)
            norm.append((rs, o * rs, _sigmoid(z)))
        mh = jnp.concatenate([(oh * hgw_ref[:, sl] * (z_ref[:, sl] * sz)).astype(BF16)
                              for sl, (_, oh, sz) in zip(heads, norm)], axis=1)
        y = _dot(mh, w_ref[:HG_W, :]) + _dot(ma_ref[...], w_ref[HG_W:, :])
        h = x_ref[...] + y
        r = lax.rsqrt(jnp.mean(h * h, axis=-1, keepdims=True) + EPS)
        hn = h * r
        fw = fw_ref[...]
        err = hn * fw - t_ref[...]
        loss_ref[...] += 0.5 * jnp.sum(jnp.mean(err * err, axis=-1, keepdims=True))
        dout = err * (1.0 / D_MODEL)
        gfw_ref[...] += jnp.sum(dout * hn, axis=0, keepdims=True)
        dhn = dout * fw
        dh = r * (dhn - hn * jnp.mean(dhn * hn, axis=-1, keepdims=True))
        dh_ref[...] = dh
        dhb = dh.astype(BF16)
        dma_ref[...] = _dot_nt(dhb, w_ref[HG_W:, :])
        dmh = _dot_nt(dhb, w_ref[:HG_W, :])
        for sl, (rs, oh, sz) in zip(heads, norm):
            z, dmix, gwv = z_ref[:, sl], dmh[:, sl], hgw_ref[:, sl]
            dzh_ref[:, sl] = (dmix * (oh * gwv) * (sz * (1.0 + z * (1.0 - sz)))).astype(BF16)
            don = dmix * (z * sz)
            ghgw_ref[:, sl] += jnp.sum(don * oh, axis=0, keepdims=True)
            dy = don * gwv
            doh_ref[:, sl] = rs * (dy - oh * jnp.mean(dy * oh, axis=-1, keepdims=True))
        gw_ref[:HG_W, :] += _dot_tn(mh, dhb)
        gw_ref[HG_W:, :] += _dot_tn(ma_ref[...], dhb)

    row = lambda w: pl.BlockSpec((tm, w), lambda s: (s, 0))
    return pl.pallas_call(
        body, name="outproj", grid=(ns,),
        in_specs=[row(D_MODEL), row(D_MODEL), row(AT_W),
                  _full((D_MODEL, D_MODEL)), _full((1, D_MODEL)),
                  row(HG_W), pl.BlockSpec((tm, HG_W), lambda s: (s, 1)), _full((1, HG_W))],
        out_specs=[row(D_MODEL), row(HG_W), row(HG_W), row(AT_W), _full((D_MODEL, D_MODEL)), _full((1, D_MODEL)),
                   _full((1, HG_W)), _full((8, 128))],
        out_shape=[jax.ShapeDtypeStruct((S, D_MODEL), F32), jax.ShapeDtypeStruct((S, HG_W), F32),
                   jax.ShapeDtypeStruct((S, HG_W), BF16), jax.ShapeDtypeStruct((S, AT_W), F32),
                   jax.ShapeDtypeStruct((D_MODEL, D_MODEL), F32), jax.ShapeDtypeStruct((1, D_MODEL), F32),
                   jax.ShapeDtypeStruct((1, HG_W), F32), jax.ShapeDtypeStruct((8, 128), F32)],
        compiler_params=_params(("arbitrary",)),
    )(x2, tgt2, mix_at, w_out_full, fnw, o_hg, p1, hg_norm_w)


def _inproj_bwd_x(dps, w_in_full, x2, norm_w, dh):
    S = x2.shape[0]
    tm = PROJ_TM

    def body(d0, d1, d2, d3, d4, d5, d6, d7, w_ref, x_ref, nw_ref, dh_ref, gx_ref, gnw_ref):
        s = pl.program_id(0)

        @pl.when(s == 0)
        def _():
            gnw_ref[...] = jnp.zeros_like(gnw_ref)

        du = jnp.zeros((tm, D_MODEL), F32)
        for i, dref in enumerate((d0, d1, d2, d3, d4, d5, d6, d7)):
            j, half = divmod(i, 2)
            du = du + _dot_nt(dref[...], w_ref[j, :, 512 * half: 512 * (half + 1)])
        x = x_ref[...]
        r = lax.rsqrt(jnp.mean(x * x, axis=-1, keepdims=True) + EPS)
        xh = x * r
        gnw_ref[...] += jnp.sum(du * xh, axis=0, keepdims=True)
        dun = du * nw_ref[...]
        gx_ref[...] = dh_ref[...] + r * (dun - xh * jnp.mean(dun * xh, axis=-1, keepdims=True))

    row = lambda w: pl.BlockSpec((tm, w), lambda s: (s, 0))
    return pl.pallas_call(
        body, name="inproj_bwd_x", grid=(S // tm,),
        in_specs=[row(512)] * 8 + [_full((4, D_MODEL, 1024)), row(D_MODEL), _full((1, D_MODEL)), row(D_MODEL)],
        out_specs=[row(D_MODEL), _full((1, D_MODEL))],
        out_shape=[jax.ShapeDtypeStruct((S, D_MODEL), F32), jax.ShapeDtypeStruct((1, D_MODEL), F32)],
        compiler_params=_params(("arbitrary",)),
    )(*dps, w_in_full, x2, norm_w, dh)


def _inproj_bwd_w(u, dps):
    S = u.shape[0]
    tm = PROJ_TM

    def body(u_ref, d0, d1, d2, d3, d4, d5, d6, d7, g_ref):
        @pl.when(pl.program_id(0) == 0)
        def _():
            g_ref[...] = jnp.zeros_like(g_ref)

        ub = u_ref[...]
        for i, dref in enumerate((d0, d1, d2, d3, d4, d5, d6, d7)):
            j, half = divmod(i, 2)
            g_ref[j, :, 512 * half: 512 * (half + 1)] += _dot_tn(ub, dref[...])

    return pl.pallas_call(
        body, name="inproj_bwd_w", grid=(S // tm,),
        in_specs=[pl.BlockSpec((tm, D_MODEL), lambda s: (s, 0))] + [pl.BlockSpec((tm, 512), lambda s: (s, 0))] * 8,
        out_specs=_full((4, D_MODEL, 1024)),
        out_shape=jax.ShapeDtypeStruct((4, D_MODEL, 1024), F32),
        compiler_params=_params(("arbitrary",)),
    )(u, *dps)


def _adamw_update(gg, w_ref, m_ref, v_ref, d_ref, nm_ref, nv_ref):
    nm = ADAM_B1 * m_ref[...] + (1.0 - ADAM_B1) * gg
    nv = ADAM_B2 * v_ref[...] + (1.0 - ADAM_B2) * (gg * gg)
    m_hat = nm / (1.0 - ADAM_B1 ** ADAM_STEP)
    v_hat = nv / (1.0 - ADAM_B2 ** ADAM_STEP)
    d_ref[...] = -ADAM_LR * (m_hat / (jnp.sqrt(v_hat) + ADAM_EPS) + ADAM_WD * w_ref[...])
    nm_ref[...] = nm
    nv_ref[...] = nv


def _adamw(w, g, m, v, name):
    rows, cols = w.shape
    tr = min(rows, 256)

    def body(w_ref, g_ref, m_ref, v_ref, d_ref, nm_ref, nv_ref):
        _adamw_update(g_ref[...], w_ref, m_ref, v_ref, d_ref, nm_ref, nv_ref)

    spec = pl.BlockSpec((tr, cols), lambda i: (i, 0))
    sds = jax.ShapeDtypeStruct((rows, cols), F32)
    return pl.pallas_call(
        body, name=name, grid=(rows // tr,),
        in_specs=[spec] * 4, out_specs=[spec] * 3, out_shape=[sds] * 3,
        compiler_params=_params(("parallel",)),
    )(w, g, m, v)


def _place():
    return lax.axis_index("x"), lax.axis_index("y"), lax.axis_index("c")


def _gather_weights(w_in_s, w_out_s):
    def body(win_ref, wout_ref, fin_ref, fout_ref, send_sems, recv_sems):
        x, y, c = _place()
        me = (x, y, c)
        sib = (x, y, 1 - c)
        mine = 2 * x + y
        fin_ref[mine] = win_ref[...].astype(BF16)
        fout_ref[mine] = wout_ref[...].astype(BF16)
        chips = [(1 - x, y), (x, 1 - y), (1 - x, 1 - y)]

        def halves(chip, half):
            return (fin_ref.at[chip, pl.ds(half * 512, 512), :], fout_ref.at[chip, pl.ds(half * 128, 128), :])

        def copy(k, ref, to):
            return pltpu.make_async_remote_copy(src_ref=ref, dst_ref=ref, send_sem=send_sems.at[k],
                                                recv_sem=recv_sems.at[k], device_id=to, device_id_type=MESH)

        first, passed = [], []
        for j, (cx, cy) in enumerate(chips):
            for a, ref in enumerate(halves(mine, c)):
                first.append(copy(2 * j + a, ref, (cx, cy, c)))
        for cp in first:
            cp.start()
        for j, (cx, cy) in enumerate(chips):
            for a, ref in enumerate(halves(2 * cx + cy, c)):
                copy(2 * j + a, ref, me).wait_recv()
                fwd = copy(6 + 2 * j + a, ref, sib)
                fwd.start()
                passed.append(fwd)
        for j, (cx, cy) in enumerate(chips):
            for a, ref in enumerate(halves(2 * cx + cy, 1 - c)):
                copy(6 + 2 * j + a, ref, me).wait_recv()
        for cp in first + passed:
            cp.wait_send()

    vm = pl.BlockSpec(memory_space=pltpu.VMEM)
    return pl.pallas_call(
        body, name="gather_weights",
        in_specs=[vm, vm], out_specs=[vm, vm],
        out_shape=[jax.ShapeDtypeStruct((4, D_MODEL, 1024), BF16), jax.ShapeDtypeStruct((4, 256, D_MODEL), BF16)],
        scratch_shapes=[pltpu.SemaphoreType.DMA((12,)), pltpu.SemaphoreType.DMA((12,))],
        compiler_params=pltpu.CompilerParams(vmem_limit_bytes=VMEM_LIMIT),
    )(w_in_s, w_out_s)


def _swap_halves(g_in, g_out):
    def body(gin_ref, gout_ref, rin_ref, rout_ref, send_sems, recv_sems):
        x, y, c = _place()
        sib = (x, y, 1 - c)
        cps = [pltpu.make_async_remote_copy(src_ref=src.at[:, 1 - c], dst_ref=dst, send_sem=send_sems.at[k],
                                            recv_sem=recv_sems.at[k], device_id=sib, device_id_type=MESH)
               for k, (src, dst) in enumerate(((gin_ref, rin_ref), (gout_ref, rout_ref)))]
        for cp in cps:
            cp.start()
        for cp in cps:
            cp.wait()

    hbm = pl.BlockSpec(memory_space=pl.ANY)
    return pl.pallas_call(
        body, name="swap_halves",
        in_specs=[hbm, hbm], out_specs=[hbm, hbm],
        out_shape=[jax.ShapeDtypeStruct((4,) + g.shape[2:], F32) for g in (g_in, g_out)],
        scratch_shapes=[pltpu.SemaphoreType.DMA((2,)), pltpu.SemaphoreType.DMA((2,))],
    )(g_in, g_out)


def _add_half(g, r, cidx, name):
    n, _, rows, cols = g.shape

    def body(c_ref, g_ref, r_ref, o_ref):
        o_ref[0] = (g_ref[0, 0] + r_ref[0]).astype(BF16)

    return pl.pallas_call(
        body, name=name,
        grid_spec=pltpu.PrefetchScalarGridSpec(
            num_scalar_prefetch=1, grid=(n,),
            in_specs=[pl.BlockSpec((1, 1, rows, cols), lambda j, c_ref: (j, c_ref[0], 0, 0)),
                      pl.BlockSpec((1, rows, cols), lambda j, c_ref: (j, 0, 0))],
            out_specs=pl.BlockSpec((1, rows, cols), lambda j, c_ref: (j, 0, 0))),
        out_shape=jax.ShapeDtypeStruct((n, rows, cols), BF16),
        compiler_params=_params(("parallel",)),
    )(cidx, g, r)


def _exchange_copies(in_ref, out_ref, lin_ref, lout_ref, send_sems, recv_sems):
    x, y, c = _place()
    cps = []
    for k, (cx, cy) in enumerate([(1 - x, y), (x, 1 - y), (1 - x, 1 - y)]):
        for a, (src, dst) in enumerate(((in_ref, lin_ref), (out_ref, lout_ref))):
            cps.append(pltpu.make_async_remote_copy(
                src_ref=src.at[2 * cx + cy], dst_ref=dst.at[k], send_sem=send_sems.at[2 * k + a],
                recv_sem=recv_sems.at[2 * k + a], device_id=(cx, cy, c), device_id_type=MESH))
    return cps


def _exchange_start(cs_in, cs_out):
    def body(in_ref, out_ref, lin_ref, lout_ref, send_sems, recv_sems, in_thru, out_thru, lin_thru, lout_thru, token):
        for cp in _exchange_copies(in_ref, out_ref, lin_ref, lout_ref, send_sems, recv_sems):
            cp.start()
        token[...] = jnp.zeros_like(token)

    lands = [lax.empty((3,) + a.shape[1:], a.dtype) for a in (cs_in, cs_out)]
    bufs = [pltpu.with_memory_space_constraint(a, pltpu.HBM) for a in (cs_in, cs_out, *lands)]
    hbm = pl.BlockSpec(memory_space=pltpu.HBM)
    sem = pl.BlockSpec(memory_space=pltpu.SEMAPHORE)
    return pl.pallas_call(
        body, name="exchange_start",
        in_specs=[hbm] * 4,
        out_specs=[sem, sem, hbm, hbm, hbm, hbm, pl.BlockSpec(memory_space=pltpu.VMEM)],
        out_shape=[pltpu.SemaphoreType.DMA((6,)), pltpu.SemaphoreType.DMA((6,))]
                  + [pltpu.HBM(b.shape, b.dtype) for b in bufs] + [jax.ShapeDtypeStruct((8, 128), F32)],
        input_output_aliases={0: 2, 1: 3, 2: 4, 3: 5},
        compiler_params=pltpu.CompilerParams(has_side_effects=pltpu.SideEffectType.DATAFLOW_SIDE_EFFECTING),
    )(*bufs)


def _exchange_wait(send_sems, recv_sems, in_thru, out_thru, lin_thru, lout_thru, after):
    def body(in_ref, out_ref, lin_ref, lout_ref, send_sems, recv_sems, after_ref, in_dead, out_dead, got_in, got_out):
        for cp in _exchange_copies(in_ref, out_ref, lin_ref, lout_ref, send_sems, recv_sems):
            cp.wait_send()
            cp.wait_recv()

    hbm = pl.BlockSpec(memory_space=pltpu.HBM)
    sem = pl.BlockSpec(memory_space=pltpu.SEMAPHORE)
    bufs = (in_thru, out_thru, lin_thru, lout_thru)
    return pl.pallas_call(
        body, name="exchange_wait",
        in_specs=[hbm] * 4 + [sem, sem, pl.BlockSpec(memory_space=pl.ANY)],
        out_specs=[hbm] * 4,
        out_shape=[pltpu.HBM(b.shape, b.dtype) for b in bufs],
        input_output_aliases={0: 0, 1: 1, 2: 2, 3: 3},
        compiler_params=pltpu.CompilerParams(has_side_effects=pltpu.SideEffectType.DATAFLOW_SIDE_EFFECTING),
    )(*bufs, send_sems, recv_sems, after)


PEER_XOR = (2, 1, 3)


def _sum_chips(cs, r, chip_idx, name):
    _, rows, cols = r.shape
    tr = min(rows, 256)

    def body(m_ref, cs_ref, r_ref, o_ref):
        mine = m_ref[0]
        own = cs_ref[0].astype(F32)
        got = [r_ref[k].astype(F32) for k in range(3)]
        acc = None
        for s in range(4):
            rel = mine ^ s
            term = jnp.where(rel == 0, own, jnp.where(rel == PEER_XOR[0], got[0],
                                                      jnp.where(rel == PEER_XOR[1], got[1], got[2])))
            acc = term if acc is None else acc + term
        o_ref[...] = acc

    return pl.pallas_call(
        body, name=name,
        grid_spec=pltpu.PrefetchScalarGridSpec(
            num_scalar_prefetch=1, grid=(rows // tr,),
            in_specs=[pl.BlockSpec((1, tr, cols), lambda i, m_ref: (m_ref[0], i, 0)),
                      pl.BlockSpec((3, tr, cols), lambda i, m_ref: (0, i, 0))],
            out_specs=pl.BlockSpec((tr, cols), lambda i, m_ref: (i, 0))),
        out_shape=jax.ShapeDtypeStruct((rows, cols), F32),
        compiler_params=_params(("parallel",)),
    )(chip_idx, cs, r)


def _swap_reduced(h_in, h_out):
    def body(in_ref, out_ref, rin_ref, rout_ref, send_sems, recv_sems):
        x, y, c = _place()
        cps = [pltpu.make_async_remote_copy(src_ref=src, dst_ref=dst, send_sem=send_sems.at[k],
                                            recv_sem=recv_sems.at[k], device_id=(x, y, 1 - c), device_id_type=MESH)
               for k, (src, dst) in enumerate(((in_ref, rin_ref), (out_ref, rout_ref)))]
        for cp in cps:
            cp.start()
        for cp in cps:
            cp.wait()

    hbm = pl.BlockSpec(memory_space=pl.ANY)
    return pl.pallas_call(
        body, name="swap_reduced",
        in_specs=[hbm, hbm], out_specs=[hbm, hbm],
        out_shape=[jax.ShapeDtypeStruct(h.shape, F32) for h in (h_in, h_out)],
        scratch_shapes=[pltpu.SemaphoreType.DMA((2,)), pltpu.SemaphoreType.DMA((2,))],
    )(h_in, h_out)


def _adamw_halves(w, mine, theirs, m, v, cidx, name):
    rows, cols = w.shape
    half = rows // 2
    tr = min(half, 256)
    nbh = half // tr

    def body(c_ref, w_ref, a_ref, b_ref, m_ref, v_ref, g_ref, d_ref, nm_ref, nv_ref):
        gg = jnp.where(pl.program_id(0) // nbh == c_ref[0], a_ref[...], b_ref[...])
        g_ref[...] = gg
        _adamw_update(gg, w_ref, m_ref, v_ref, d_ref, nm_ref, nv_ref)

    spec = pl.BlockSpec((tr, cols), lambda i, c_ref: (i, 0))
    hspec = pl.BlockSpec((tr, cols), lambda i, c_ref: (i % nbh, 0))
    sds = jax.ShapeDtypeStruct((rows, cols), F32)
    return pl.pallas_call(
        body, name=name,
        grid_spec=pltpu.PrefetchScalarGridSpec(
            num_scalar_prefetch=1, grid=(rows // tr,),
            in_specs=[spec, hspec, hspec, spec, spec], out_specs=[spec] * 4),
        out_shape=[sds] * 4,
        compiler_params=_params(("parallel",)),
    )(cidx, w, mine, theirs, m, v)


def _allreduce_small(g_nw, g_fw, g_hgw, g_lbl, loss8):
    def body(nw_ref, fw_ref, hgw_ref, lbl_ref, loss_ref, out_ref, slots, send_sems, recv_sems):
        x, y, c = _place()
        me = 4 * x + 2 * y + c
        slots[me] = jnp.zeros((8, D_MODEL), F32)
        slots[me, 0:1, :] = nw_ref[...]
        slots[me, 1:2, :] = fw_ref[...]
        slots[me, 2:3, 0:HG_W] = hgw_ref[...]
        slots[me, 3:4, 0:HG_W] = lbl_ref[0:1, :]
        slots[me, 3:4, HG_W:] = lbl_ref[1:2, :]
        slots[me, 4:5, 0:128] = loss_ref[0:1, :]
        cps = []
        for k in range(1, 8):
            dx, dy, dc = (k >> 2) & 1, (k >> 1) & 1, k & 1
            to = (x ^ dx, y ^ dy, c ^ dc)
            cps.append(pltpu.make_async_remote_copy(
                src_ref=slots.at[me], dst_ref=slots.at[me], send_sem=send_sems.at[k - 1],
                recv_sem=recv_sems.at[k - 1], device_id=to, device_id_type=MESH))
        for cp in cps:
            cp.start()
        for cp in cps:
            cp.wait()
        acc = slots[0]
        for i in range(1, 8):
            acc = acc + slots[i]
        out_ref[...] = acc

    vm = pl.BlockSpec(memory_space=pltpu.VMEM)
    return pl.pallas_call(
        body, name="allreduce_small",
        in_specs=[vm] * 5, out_specs=vm,
        out_shape=jax.ShapeDtypeStruct((8, D_MODEL), F32),
        scratch_shapes=[pltpu.VMEM((8, 8, D_MODEL), F32), pltpu.SemaphoreType.DMA((7,)), pltpu.SemaphoreType.DMA((7,))],
    )(g_nw, g_fw, g_hgw, g_lbl, loss8)


def _rope_tables(S):
    inv_freq = (np.float32(1.0) / np.power(np.float32(ROPE_THETA), np.arange(ROPE_HALF, dtype=np.float32) / np.float32(ROPE_HALF))).astype(np.float32)
    ang = (np.arange(S, dtype=np.float32)[:, None] * inv_freq[None, :]).astype(np.float32)
    cos, sin = np.cos(ang).astype(np.float32), np.sin(ang).astype(np.float32)
    cos2 = np.concatenate([cos, cos, cos, cos], axis=-1)
    sin2 = np.concatenate([-sin, sin, -sin, sin], axis=-1)
    return jnp.asarray(cos2), jnp.asarray(sin2)


def _local_step(x2, tgt2, norm_w, w_in_full, lbl, hg_norm_w, w_out4, fnw):
    S = x2.shape[0]
    cos2, sin2 = _rope_tables(S)
    w_out_full = w_out4.reshape(D_MODEL, D_MODEL)
    p0, p1, qkv, qkv4, qkv16, z_at, u = _inproj(x2, norm_w, w_in_full, cos2, sin2)
    o_hg, states = _hg_fwd(p0, p1, lbl)
    qkv_ds = [qkv.reshape(1, S, 3 * AT_W), qkv4, qkv16]
    os_d, ls_d = zip(*[_att_fwd(q) for q in qkv_ds])
    o_at, lse, mix_at = _att_combine(os_d, ls_d, z_at)
    dh, do_hg, dz_hg, dm_at, g_wout, g_fw, g_hgw, loss8 = _outproj(x2, tgt2, mix_at, w_out_full, fnw, o_hg, p1, hg_norm_w)
    dqr, dfl, dv_hg, g_lbl = _hg_bwd(p0, p1, do_hg, states, lbl)
    do1, do4, do16, dl1, dl4, dl16, dz_at = _att_gate_bwd(dm_at, o_at, lse, z_at)
    do_ds = [do1.reshape(1, S, AT_W), do4, do16]
    dl_ds = [dl1.reshape(1, S, STAT_W), dl4, dl16]
    dqkvs = [_att_bwd(q, a, b) for q, a, b in zip(qkv_ds, do_ds, dl_ds)]
    dq_at, dk_at, dv_at = _att_bwd_combine(dqkvs, cos2, sin2)
    dps = [dqr, dfl, dv_hg, dz_hg, dq_at, dk_at, dv_at, dz_at]
    return loss8, dps, u, dh, g_lbl, g_hgw, g_wout, g_fw


def kernel(x, norm_w, w_in, hgrn_lb_logits, hg_norm_w, w_out, final_norm_w, loss_target, m_norm_w, m_w_in, m_hgrn_lb_logits, m_hg_norm_w, m_w_out, m_final_norm_w, v_norm_w, v_w_in, v_hgrn_lb_logits, v_hg_norm_w, v_w_out, v_final_norm_w):
    S = x.shape[1]
    w_in_full, w_out_full = _gather_weights(w_in[0], w_out[0])
    loss8, dps, u, dh, g_lbl, g_hgw, g_wout, g_fw = _local_step(
        x[0], loss_target[0], norm_w, w_in_full, hgrn_lb_logits, hg_norm_w,
        w_out_full, final_norm_w.reshape(1, D_MODEL))

    cidx = lax.axis_index("c").astype(jnp.int32).reshape(1)
    g_win = _inproj_bwd_w(u, dps)
    g_in4 = g_win.reshape(4, 2, 512, 1024)
    g_out4 = g_wout.reshape(4, 2, 128, D_MODEL)
    r_in, r_out = _swap_halves(g_in4, g_out4)
    cs_in = _add_half(g_in4, r_in, cidx, "add_half_in")
    cs_out = _add_half(g_out4, r_out, cidx, "add_half_out")
    *started, token = _exchange_start(cs_in, cs_out)
    grad_x, g_nw = _inproj_bwd_x(dps, w_in_full, x[0], norm_w + token[0:1, 0:1], dh)
    cs_in, cs_out, x_in, x_out = _exchange_wait(*started, g_nw)
    chip_idx = (2 * lax.axis_index("x") + lax.axis_index("y")).astype(jnp.int32).reshape(1)
    h_in = _sum_chips(cs_in, x_in, chip_idx, "sum_chips_in")
    h_out = _sum_chips(cs_out, x_out, chip_idx, "sum_chips_out")
    s_in, s_out = _swap_reduced(h_in, h_out)

    red = _allreduce_small(g_nw, g_fw, g_hgw, g_lbl, loss8)
    loss = red[4, 0]
    grad_norm_w = red[0:1, :]
    grad_final_norm_w = red[1, :]
    grad_hg_norm_w = red[2:3, :HG_W]
    grad_lbl = jnp.concatenate([red[3:4, :HG_W], red[3:4, HG_W:]], axis=0)

    d_nw, m_nw, v_nw = _adamw(norm_w, grad_norm_w, m_norm_w, v_norm_w, "adamw_norm_w")
    grad_w_in, d_win, m_win, v_win = _adamw_halves(w_in[0], h_in, s_in, m_w_in[0], v_w_in[0], cidx, "adamw_w_in")
    d_lbl, m_lbl, v_lbl = _adamw(hgrn_lb_logits, grad_lbl, m_hgrn_lb_logits, v_hgrn_lb_logits, "adamw_lb_logits")
    d_hgw, m_hgw, v_hgw = _adamw(hg_norm_w, grad_hg_norm_w, m_hg_norm_w, v_hg_norm_w, "adamw_hg_norm_w")
    grad_w_out, d_wout, m_wout, v_wout = _adamw_halves(w_out[0], h_out, s_out, m_w_out[0], v_w_out[0], cidx, "adamw_w_out")
    d_fw, m_fw, v_fw = _adamw(final_norm_w.reshape(1, D_MODEL), grad_final_norm_w.reshape(1, D_MODEL),
                              m_final_norm_w.reshape(1, D_MODEL), v_final_norm_w.reshape(1, D_MODEL), "adamw_final_norm_w")
    e1 = lambda a: a[None]
    flat = lambda a: a.reshape(D_MODEL)
    return (loss, grad_x[None], grad_norm_w, e1(grad_w_in), grad_lbl, grad_hg_norm_w, e1(grad_w_out), grad_final_norm_w,
            d_nw, e1(d_win), d_lbl, d_hgw, e1(d_wout), flat(d_fw),
            m_nw, e1(m_win), m_lbl, m_hgw, e1(m_wout), flat(m_fw),
            v_nw, e1(v_win), v_lbl, v_hgw, e1(v_wout), flat(v_fw))
```

```python
import jax
import jax.numpy as jnp
import numpy as np
from jax import lax
from jax.experimental import pallas as pl
from jax.experimental.pallas import tpu as pltpu

F32 = jnp.float32
BF16 = jnp.bfloat16
MESH = pl.DeviceIdType.MESH

D_MODEL = 1024
HG_HEADS = 4
HG_HEAD = 128
HG_W = HG_HEADS * HG_HEAD
AT_HEAD = 64
AT_W = 512
HEAD_PAIR = 2 * AT_HEAD
ROPE_HALF = 32
ROPE_THETA = 10000.0
EPS = 1e-6
CHUNK = 128
LEVELS = (64, 32, 16, 8, 4, 2, 1)
DIAG = 1
SUBLANES = 8
ATT_BLK = 128
ATT_QB = 4
ATT_SCALE = 0.125
STAT_W = 128
STAT_LANES = 16
STAT_LSE_LANE = 8
NEG = -1e30
VMEM_LIMIT = 56 * 1024 * 1024
MIX_TM = 1024
PROJ_TM = 512

ADAM_LR = 0.001
ADAM_B1 = 0.9
ADAM_B2 = 0.999
ADAM_EPS = 1e-08
ADAM_WD = 0.01
ADAM_STEP = 10


def _iota(shape, dim):
    return lax.broadcasted_iota(jnp.int32, shape, dim)


def _dot(a, b):
    return jnp.dot(a, b, preferred_element_type=F32)


def _dot_nt(a, b):
    return lax.dot_general(a, b, (((1,), (1,)), ((), ())), preferred_element_type=F32)


def _dot_tn(a, b):
    return lax.dot_general(a, b, (((0,), (0,)), ((), ())), preferred_element_type=F32)


def _sigmoid(v):
    return 0.5 * jnp.tanh(0.5 * v) + 0.5


def _params(sem=None, vmem=VMEM_LIMIT):
    return pltpu.CompilerParams(dimension_semantics=sem, vmem_limit_bytes=vmem)


def _full(shape):
    n = len(shape)
    return pl.BlockSpec(shape, lambda *_: (0,) * n)


def _rope_rot(y):
    n = y.shape[1]
    first = (_iota(y.shape, 1) & (2 * ROPE_HALF - 1)) < ROPE_HALF
    return jnp.where(first, pltpu.roll(y, n - ROPE_HALF, 1), pltpu.roll(y, ROPE_HALF, 1))


def _dil_spec(d, tm, width):
    return pl.BlockSpec((d, tm // d, width), lambda s: (0, s, 0))


LANES = 128


def _slab_scratch(tm, width):
    return pltpu.VMEM((width // LANES, tm, LANES), F32)


def _to_slabs(v, slabs_ref):
    for j in range(slabs_ref.shape[0]):
        slabs_ref[j] = v[:, LANES * j: LANES * (j + 1)]


def _from_slabs(slabs_ref):
    return jnp.concatenate([slabs_ref[j] for j in range(slabs_ref.shape[0])], axis=1)


def _split_residues(slabs_ref, dst_ref, d, dtype):
    nslab, tm, _ = slabs_ref.shape
    for r in range(d):
        for j in range(nslab):
            dst_ref[r, :, LANES * j: LANES * (j + 1)] = slabs_ref[j, pl.ds(r, tm // d, stride=d), :].astype(dtype)


def _merge_residues(src_ref, slabs_ref, d):
    nslab, tm, _ = slabs_ref.shape
    for r in range(d):
        for j in range(nslab):
            slabs_ref[j, pl.ds(r, tm // d, stride=d), :] = src_ref[r, :, LANES * j: LANES * (j + 1)].astype(F32)


PERM_ROWS = 256


def _perm_matrix(d, inverse=False):
    n = PERM_ROWS // d
    t = np.arange(PERM_ROWS)
    p = np.zeros((PERM_ROWS, PERM_ROWS), np.float32)
    p[t, (t % d) * n + t // d] = 1.0
    return jnp.asarray(p.T if inverse else p, BF16)


def _merge_bf16(src_ref, perm_ref, d):
    n = PERM_ROWS // d
    tm = src_ref.shape[1] * d
    outs = []
    for b in range(tm // PERM_ROWS):
        flat = jnp.concatenate([src_ref[r, n * b: n * (b + 1), :] for r in range(d)], axis=0)
        outs.append(_dot(perm_ref[...], flat))
    return jnp.concatenate(outs, axis=0)


def _split_bf16(v, perm_t_ref, dst_ref, d):
    n = PERM_ROWS // d
    for b in range(v.shape[0] // PERM_ROWS):
        flat = _dot(perm_t_ref[...], v[PERM_ROWS * b: PERM_ROWS * (b + 1), :]).astype(BF16)
        for r in range(d):
            dst_ref[r, n * b: n * (b + 1), :] = flat[n * r: n * (r + 1), :]


def _inproj(x2, norm_w, w_in_full, cos2, sin2):
    S = x2.shape[0]
    tm = PROJ_TM

    def body(x_ref, nw_ref, w_ref, cos_ref, sin_ref, t4_ref, t16_ref, p0_ref, p1_ref, qkv_ref, qkv4_ref, qkv16_ref, z_ref,
             u_ref):
        x = x_ref[...]
        r = lax.rsqrt(jnp.mean(x * x, axis=-1, keepdims=True) + EPS)
        u = x * r * nw_ref[...]
        ub = u.astype(BF16)
        u_ref[...] = ub
        p0_ref[...] = _dot(ub, w_ref[0])
        p1_ref[...] = _dot(ub, w_ref[1])
        y2 = _dot(ub, w_ref[2])
        cosf = jnp.tile(cos_ref[...], (1, 8))
        sinf = jnp.tile(sin_ref[...], (1, 8))
        y3 = _dot(ub, w_ref[3])
        z_ref[...] = y3[:, AT_W:]
        qkv = jnp.concatenate([y2 * cosf + _rope_rot(y2) * sinf, y3[:, :AT_W]], axis=1).astype(BF16)
        qkv_ref[...] = qkv
        _split_bf16(qkv, t4_ref, qkv4_ref, 4)
        _split_bf16(qkv, t16_ref, qkv16_ref, 16)

    row = lambda w: pl.BlockSpec((tm, w), lambda s: (s, 0))
    qkv_w = 3 * AT_W
    perm = _full((PERM_ROWS, PERM_ROWS))
    return pl.pallas_call(
        body, name="inproj", grid=(S // tm,),
        in_specs=[row(D_MODEL), _full((1, D_MODEL)), _full((4, D_MODEL, 1024)), row(128), row(128), perm, perm],
        out_specs=[row(1024), row(1024), row(qkv_w), _dil_spec(4, tm, qkv_w), _dil_spec(16, tm, qkv_w), row(AT_W),
                   row(D_MODEL)],
        out_shape=[jax.ShapeDtypeStruct((S, 1024), F32), jax.ShapeDtypeStruct((S, 1024), F32),
                   jax.ShapeDtypeStruct((S, qkv_w), BF16), jax.ShapeDtypeStruct((4, S // 4, qkv_w), BF16),
                   jax.ShapeDtypeStruct((16, S // 16, qkv_w), BF16), jax.ShapeDtypeStruct((S, AT_W), F32),
                   jax.ShapeDtypeStruct((S, D_MODEL), BF16)],
        compiler_params=_params(("parallel",)),
    )(x2, norm_w, w_in_full, cos2, sin2, _perm_matrix(4, True), _perm_matrix(16, True))


HG_HPS = 4
HG_CPS = 4
N_LEV = len(LEVELS)


def _hg_const_arrays():
    r = np.arange(CHUNK)[:, None]
    c = np.arange(CHUNK)[None, :]
    tris = np.stack([r >= c, r <= c])
    lm = [((r // (2 * m)) == (c // (2 * m))) & (r % (2 * m) >= m) & (c % (2 * m) < m) for m in LEVELS]
    dm = [(c == r - dl) & (r % DIAG >= dl) for dl in range(DIAG)]
    masks = np.stack(lm + dm)
    return jnp.asarray(tris, BF16), jnp.asarray(masks, F32)


def _split2(a):
    hi = a.astype(BF16)
    return hi, (a - hi.astype(F32)).astype(BF16)


def _dot3(a, b, dot=_dot):
    ah, al = _split2(a)
    bh, bl = _split2(b)
    n = b.shape[1]
    p = dot(ah, jnp.concatenate([bh, bl], axis=1))
    return (p[:, :n] + p[:, n:]) + dot(al, bh)


def _split3(a):
    a1 = a.astype(BF16)
    r1 = a - a1.astype(F32)
    a2 = r1.astype(BF16)
    return a1, a2, (r1 - a2.astype(F32)).astype(BF16)


def _tri_dot(tri, a):
    n = a.shape[1]
    p = _dot(tri, jnp.concatenate(_split3(a), axis=1))
    return (p[:, :n] + p[:, n:2 * n]) + p[:, 2 * n:]


def _dot_sel(a, sel):
    a1, a2, a3 = _split3(a)
    return (_dot(a1, sel) + _dot(a2, sel)) + _dot(a3, sel)


def _rowsum(t):
    return _dot(t.astype(BF16), jnp.ones((t.shape[1], t.shape[1]), BF16))


def _level_refs(b):
    refs = []
    pos = _iota(b.shape, 0)
    for m in LEVELS:
        if 2 * m >= SUBLANES:
            parts = [jnp.broadcast_to(b[r0 + m - 1: r0 + m, :], (2 * m, b.shape[1])) for r0 in range(0, CHUNK, 2 * m)]
            refs.append(parts[0] if len(parts) == 1 else jnp.concatenate(parts, axis=0))
        else:
            p = pos & (2 * m - 1)
            ref = b
            for off in range(-(m - 1), m + 1):
                if off != 0:
                    ref = jnp.where(p == m - 1 + off, pltpu.roll(b, off % CHUNK, 0), ref)
            refs.append(ref)
    return refs


def _hg_lb(lbl_ref):
    l0 = lbl_ref[0:1, :]
    l1 = lbl_ref[1:2, :]
    mx = jnp.maximum(l0, l1)
    e0 = jnp.exp(l0 - mx)
    e1 = jnp.exp(l1 - mx)
    p0 = e0 / (e0 + e1)
    lb = jnp.clip(p0, 1e-6, 1.0 - 1e-6)
    inside = (p0 >= 1e-6) & (p0 <= 1.0 - 1e-6)
    dlb_dl0 = jnp.where(inside, p0 * (e1 / (e0 + e1)), 0.0)
    return lb, dlb_dl0


def _hg_gates(qr, fl, lb):
    sig = _sigmoid(fl)
    f = lb + (1.0 - lb) * sig
    g = jnp.log(f)
    k = (1.0 - lb) * (1.0 - sig)
    sq = _sigmoid(qr)
    q = qr * sq
    return sig, f, g, k, sq, q


def _neg_abs(v):
    bits = lax.bitcast_convert_type(v, jnp.uint32) | jnp.uint32(0x80000000)
    return lax.bitcast_convert_type(bits, F32)


def _hg_levels(q, k, b, mk_ref):
    refs = _level_refs(b)
    a = jnp.zeros((CHUNK, CHUNK), F32)
    es, qts, kts = [], [], []
    for i in range(N_LEV):
        e = jnp.exp(_neg_abs(b - refs[i]))
        qt = (q * e).astype(BF16)
        kt = (k * e).astype(BF16)
        a = a + _dot_nt(qt, kt) * mk_ref[i]
        es.append(e); qts.append(qt); kts.append(kt)
    return a, es, qts, kts


def _hg_specs(nc, rev):
    cc = (lambda c: nc - 1 - c) if rev else (lambda c: c)
    w = HG_HEAD * HG_HPS
    blk = lambda off: pl.BlockSpec((HG_CPS * CHUNK, w), lambda h, c: (cc(c), h + off))
    lb2 = pl.BlockSpec((2, w), lambda h, c: (0, h))
    st = pl.BlockSpec((HG_CPS, HG_HPS, HG_HEAD, HG_HEAD), lambda h, c: (cc(c), h, 0, 0))
    consts = [_full((2, CHUNK, CHUNK)), _full((N_LEV + DIAG, CHUNK, CHUNK))]
    return blk, lb2, st, consts


def _hg_fwd(p0, p1, lbl):
    S = p0.shape[0]
    nc = S // (HG_CPS * CHUNK)
    ng = HG_HEADS // HG_HPS

    def body(qr_ref, fl_ref, v_ref, lbl_ref, tri_ref, mk_ref, o_ref, st_ref, state):
        c = pl.program_id(1)

        @pl.when(c == 0)
        def _():
            state[...] = jnp.zeros_like(state)

        lb_all, _ = _hg_lb(lbl_ref)
        heads = [slice(HG_HEAD * hh, HG_HEAD * (hh + 1)) for hh in range(HG_HPS)]
        for j in range(HG_CPS):
            rows = slice(CHUNK * j, CHUNK * (j + 1))
            qs, ks, bs, mats = [], [], [], []
            for sl in heads:
                _, _, g, k, _, q = _hg_gates(qr_ref[rows, sl], fl_ref[rows, sl], lb_all[:, sl])
                qs.append(q); ks.append(k); bs.append(_tri_dot(tri_ref[0], g))
            for hh in range(HG_HPS):
                a, _, _, _ = _hg_levels(qs[hh], ks[hh], bs[hh], mk_ref)
                mats.append(a + _rowsum(qs[hh] * ks[hh]) * mk_ref[N_LEV])
            for hh, sl in enumerate(heads):
                q, k, b, a, v = qs[hh], ks[hh], bs[hh], mats[hh], v_ref[rows, sl]
                b_last = b[CHUNK - 1: CHUNK, :]
                st = state[hh]
                st_ref[j, hh] = st
                o = _dot_nt((q * jnp.exp(b)).astype(BF16), st.astype(BF16)) + _dot(a.astype(BF16), v.astype(BF16))
                state[hh] = st * jnp.exp(b_last) + _dot3(v, k * jnp.exp(b_last - b), _dot_tn)
                o_ref[rows, sl] = o

    blk, lb2, st_spec, consts = _hg_specs(nc, False)
    tris, masks = _hg_const_arrays()
    return pl.pallas_call(
        body, name="hg_fwd", grid=(ng, nc),
        in_specs=[blk(0), blk(ng), blk(0), lb2] + consts,
        out_specs=[blk(0), st_spec],
        out_shape=[jax.ShapeDtypeStruct((S, HG_W), F32), jax.ShapeDtypeStruct((S // CHUNK, HG_HEADS, HG_HEAD, HG_HEAD), F32)],
        scratch_shapes=[pltpu.VMEM((HG_HPS, HG_HEAD, HG_HEAD), F32)],
        compiler_params=_params(("parallel", "arbitrary")),
    )(p0, p0, p1, lbl, tris, masks)


def _hg_bwd(p0, p1, do_hg, states, lbl):
    S = p0.shape[0]
    nc = S // (HG_CPS * CHUNK)
    ng = HG_HEADS // HG_HPS
    w = HG_HEAD * HG_HPS

    def body(qr_ref, fl_ref, v_ref, do_ref, st_ref, lbl_ref, tri_ref, mk_ref, mkb_ref,
             dqr_ref, dfl_ref, dv_ref, glbl_ref, dstate, carry, acc_lb):
        c = pl.program_id(1)

        @pl.when(c == 0)
        def _():
            dstate[...] = jnp.zeros_like(dstate)
            carry[...] = jnp.zeros_like(carry)
            acc_lb[...] = jnp.zeros_like(acc_lb)

        lb_all, dlb_dl0 = _hg_lb(lbl_ref)
        heads = [slice(HG_HEAD * hh, HG_HEAD * (hh + 1)) for hh in range(HG_HPS)]
        diag_mask = mk_ref[N_LEV]
        for j in reversed(range(HG_CPS)):
            rows = slice(CHUNK * j, CHUNK * (j + 1))
            gates, bs, dos = [], [], []
            for sl in heads:
                gt = _hg_gates(qr_ref[rows, sl], fl_ref[rows, sl], lb_all[:, sl])
                gates.append(gt)
                bs.append(_tri_dot(tri_ref[0], gt[2]))
            for sl in heads:
                dos.append(do_ref[rows, sl])
            inter = []
            for hh, sl in enumerate(heads):
                _, _, _, k, _, q = gates[hh]
                b, do, v = bs[hh], dos[hh], v_ref[rows, sl]
                b_last = b[CHUNK - 1: CHUNK, :]
                eb = jnp.exp(b)
                edec = jnp.exp(b_last - b)
                dst = dstate[hh]
                dq = _dot3(do, st_ref[j, hh]) * eb
                dk = _dot3(v, dst) * edec
                da = _dot_nt(do.astype(BF16), v.astype(BF16))
                dv_state = _dot_nt((k * edec).astype(BF16), dst.astype(BF16))
                dstate[hh] = dst * jnp.exp(b_last) + _dot3(do, q * eb, _dot_tn)
                inter.append((dq, dk, da, dv_state))
            for hh, sl in enumerate(heads):
                sig, f, _, k, sq, q = gates[hh]
                dq, dk, da, dv_state = inter[hh]
                b = bs[hh]
                db = q * dq - k * dk
                dab = da.astype(BF16)
                refs = _level_refs(b)
                a = _rowsum(q * k) * diag_mask
                for i in range(N_LEV):
                    e = jnp.exp(_neg_abs(b - refs[i]))
                    qt = (q * e).astype(BF16)
                    kt = (k * e).astype(BF16)
                    a = a + _dot_nt(qt, kt) * mk_ref[i]
                    dam = dab * mkb_ref[i]
                    gq = _dot(dam, kt)
                    gk = _dot_tn(dam, qt)
                    dq = dq + e * gq
                    dk = dk + e * gk
                    db = db + (qt.astype(F32) * gq - kt.astype(F32) * gk)
                dc = _rowsum(da * diag_mask)
                dq = dq + dc * k
                dk = dk + dc * q
                dv_ref[rows, sl] = (_dot_tn(a.astype(BF16), dos[hh].astype(BF16)) + dv_state).astype(BF16)
                dg = _tri_dot(tri_ref[1], db) + carry[0:1, sl]
                carry[0:1, sl] += jnp.sum(db, axis=0, keepdims=True)
                lb = lb_all[:, sl]
                qr = qr_ref[rows, sl]
                t = dg / f - dk
                dfl_ref[rows, sl] = (t * (1.0 - lb) * sig * (1.0 - sig)).astype(BF16)
                acc_lb[0:1, sl] += jnp.sum(t * (1.0 - sig), axis=0, keepdims=True)
                dqr_ref[rows, sl] = (dq * (sq * (1.0 + qr * (1.0 - sq)))).astype(BF16)

        @pl.when(c == nc - 1)
        def _():
            gl0 = acc_lb[0:1, :] * dlb_dl0
            glbl_ref[0:1, :] = gl0
            glbl_ref[1:2, :] = -gl0

    blk, lb2, st_spec, consts = _hg_specs(nc, True)
    tris, masks = _hg_const_arrays()
    act = jax.ShapeDtypeStruct((S, HG_W), BF16)
    return pl.pallas_call(
        body, name="hg_bwd", grid=(ng, nc),
        in_specs=[blk(0), blk(ng), blk(0), blk(0), st_spec, lb2] + consts + consts[1:],
        out_specs=[blk(0), blk(0), blk(0), lb2],
        out_shape=[act, act, act, jax.ShapeDtypeStruct((2, HG_W), F32)],
        scratch_shapes=[pltpu.VMEM((HG_HPS, HG_HEAD, HG_HEAD), F32), pltpu.VMEM((8, w), F32), pltpu.VMEM((8, w), F32)],
        compiler_params=_params(("parallel", "arbitrary")),
    )(p0, p0, p1, do_hg, states, lbl, tris, masks, masks.astype(BF16))


def _stat_head(lane):
    return lane >> 4


def _stat_lane(lane):
    return lane & (STAT_LANES - 1)


def _o_head(lane):
    return lane >> 6


def _att_bias():
    qi = np.arange(ATT_BLK)[:, None]
    kj = np.arange(2 * ATT_BLK)[None, :]
    band = (kj >= qi) & (kj <= qi + ATT_BLK)
    qm = np.stack([band & (kj >= ATT_BLK), band])
    cur = (kj < ATT_BLK) & (qi <= kj)
    km = np.stack([cur, cur | ((kj >= ATT_BLK) & (qi >= kj - ATT_BLK))])
    to_bias = lambda m: jnp.asarray(np.where(m, 0.0, NEG), F32)
    return to_bias(qm), to_bias(km)


def _att_fwd(qkv_d):
    d, L, _ = qkv_d.shape
    qb = ATT_QB
    rows = qb * ATT_BLK

    def body(q_ref, kp_ref, kc_ref, vp_ref, vc_ref, bias_ref, o_ref, lse_ref, s_scr, p_scr):
        first = _iota((ATT_BLK, HEAD_PAIR), 1) < AT_HEAD
        head_of_lane = _stat_head(_iota((ATT_BLK, STAT_W), 1))
        pairs = [slice(HEAD_PAIR * hp, HEAD_PAIR * (hp + 1)) for hp in range(4)]
        blk = lambda i: slice(ATT_BLK * i, ATT_BLK * (i + 1))

        def keys(i, sl, prev_ref, cur_ref):
            before = prev_ref[0, :, sl] if i == 0 else cur_ref[0, blk(i - 1), sl]
            return jnp.concatenate([before, cur_ref[0, blk(i), sl]], axis=0)

        for i in range(qb):
            for hp, sl in enumerate(pairs):
                q2 = q_ref[0, blk(i), sl] * ATT_SCALE
                zero = jnp.zeros_like(q2)
                qs = jnp.concatenate([jnp.where(first, q2, zero), jnp.where(first, zero, q2)], axis=0)
                s_scr[4 * i + hp] = _dot_nt(qs, keys(i, sl, kp_ref, kc_ref))
        stats = []
        for i in range(qb):
            bias = bias_ref[jnp.minimum(pl.program_id(1), 1)] if i == 0 else bias_ref[1]
            bias2 = jnp.concatenate([bias, bias], axis=0)
            for hp in range(4):
                s = s_scr[4 * i + hp] + bias2
                m = jnp.max(s, axis=-1, keepdims=True)
                p = jnp.exp(s - m)
                l = jnp.sum(p, axis=-1, keepdims=True)
                p_scr[4 * i + hp] = p.astype(BF16)
                stats.append((l, m + jnp.log(l)))
        for i in range(qb):
            lse_blk = jnp.zeros((ATT_BLK, STAT_W), F32)
            for hp, sl in enumerate(pairs):
                l, lse = stats[4 * i + hp]
                o = _dot(p_scr[4 * i + hp], keys(i, sl, vp_ref, vc_ref)) / l
                o_ref[0, blk(i), sl] = jnp.where(first, o[:ATT_BLK], o[ATT_BLK:]).astype(BF16)
                lse_blk = jnp.where(head_of_lane == 2 * hp, lse[:ATT_BLK],
                                    jnp.where(head_of_lane == 2 * hp + 1, lse[ATT_BLK:], lse_blk))
            lse_ref[0, blk(i), :] = lse_blk

    cur = lambda j: pl.BlockSpec((1, rows, AT_W), lambda r, n: (r, n, j))
    prev = lambda j: pl.BlockSpec((1, ATT_BLK, AT_W), lambda r, n: (r, jnp.maximum(qb * n - 1, 0), j))
    return pl.pallas_call(
        body, name=f"att_fwd_d{d}", grid=(d, L // rows),
        in_specs=[cur(0), prev(1), cur(1), prev(2), cur(2), _full((2, ATT_BLK, 2 * ATT_BLK))],
        out_specs=[pl.BlockSpec((1, rows, AT_W), lambda r, n: (r, n, 0)), pl.BlockSpec((1, rows, STAT_W), lambda r, n: (r, n, 0))],
        out_shape=[jax.ShapeDtypeStruct((d, L, AT_W), BF16), jax.ShapeDtypeStruct((d, L, STAT_W), F32)],
        scratch_shapes=[pltpu.VMEM((4 * qb, 2 * ATT_BLK, 2 * ATT_BLK), F32), pltpu.VMEM((4 * qb, 2 * ATT_BLK, 2 * ATT_BLK), BF16)],
        compiler_params=_params(("parallel", "parallel")),
    )(qkv_d, qkv_d, qkv_d, qkv_d, qkv_d, _att_bias()[0])


def _att_combine(os_d, ls_d, z_at):
    S = z_at.shape[0]
    tm = MIX_TM

    def body(oa_ref, ob4_ref, oc16_ref, la_ref, lb4_ref, lc16_ref, p4_ref, p16_ref, z_ref,
             oat_ref, lse_ref, mix_ref, lb_scr, lc_scr):
        _merge_residues(lb4_ref, lb_scr, 4)
        _merge_residues(lc16_ref, lc_scr, 16)
        ls = (la_ref[...], lb_scr[0], lc_scr[0])
        mx = jnp.maximum(jnp.maximum(ls[0], ls[1]), ls[2])
        es = [jnp.exp(l - mx) for l in ls]
        zs = es[0] + es[1] + es[2]
        lse_ref[...] = mx + jnp.log(zs)
        spread = ((_o_head(_iota((STAT_W, AT_W), 1)) == _stat_head(_iota((STAT_W, AT_W), 0)))
                  & (_stat_lane(_iota((STAT_W, AT_W), 0)) == 0)).astype(BF16)
        os_ = (oa_ref[...].astype(F32), _merge_bf16(ob4_ref, p4_ref, 4), _merge_bf16(oc16_ref, p16_ref, 16))
        o = jnp.zeros((tm, AT_W), F32)
        for e, oi in zip(es, os_):
            hi, lo = _split2(e / zs)
            o = o + (_dot(hi, spread) + _dot(lo, spread)) * oi
        oat_ref[...] = o
        z = z_ref[...]
        mixed = o * (z * _sigmoid(z))
        mix_ref[...] = mixed.astype(BF16)

    row = lambda w: pl.BlockSpec((tm, w), lambda s: (s, 0))
    return pl.pallas_call(
        body, name="att_combine", grid=(S // tm,),
        in_specs=[row(AT_W), _dil_spec(4, tm, AT_W), _dil_spec(16, tm, AT_W),
                  row(STAT_W), _dil_spec(4, tm, STAT_W), _dil_spec(16, tm, STAT_W),
                  _full((PERM_ROWS, PERM_ROWS)), _full((PERM_ROWS, PERM_ROWS)), row(AT_W)],
        out_specs=[row(AT_W), row(STAT_W), row(AT_W)],
        out_shape=[jax.ShapeDtypeStruct((S, AT_W), F32), jax.ShapeDtypeStruct((S, STAT_W), F32),
                   jax.ShapeDtypeStruct((S, AT_W), BF16)],
        scratch_shapes=[_slab_scratch(tm, STAT_W), _slab_scratch(tm, STAT_W)],
        compiler_params=_params(("parallel",)),
    )(os_d[0].reshape(S, AT_W), os_d[1], os_d[2], ls_d[0].reshape(S, STAT_W), ls_d[1], ls_d[2],
      _perm_matrix(4), _perm_matrix(16), z_at)


def _att_gate_bwd(dm_at, o_at, lse, z_at):
    S = z_at.shape[0]
    tm = MIX_TM

    def body(dm_ref, o_ref, l_ref, z_ref, t4_ref, t16_ref, do_ref, do4_ref, do16_ref, dl_ref, dl4_ref, dl16_ref, dz_ref,
             dl_scr):
        o = o_ref[...]
        z = z_ref[...]
        dm = dm_ref[...]
        sz = _sigmoid(z)
        dz_ref[...] = (dm * o * (sz * (1.0 + z * (1.0 - sz)))).astype(BF16)
        do = dm * (z * sz)
        dob = do.astype(BF16)
        do_ref[...] = dob
        gather = (_o_head(_iota((AT_W, STAT_W), 0)) == _stat_head(_iota((AT_W, STAT_W), 1))).astype(BF16)
        dl = jnp.where(_stat_lane(_iota((tm, STAT_W), 1)) < STAT_LSE_LANE, _dot_sel(do * o, gather), l_ref[...])
        dl_ref[...] = dl
        _split_bf16(dob, t4_ref, do4_ref, 4)
        _split_bf16(dob, t16_ref, do16_ref, 16)
        _to_slabs(dl, dl_scr)
        _split_residues(dl_scr, dl4_ref, 4, F32)
        _split_residues(dl_scr, dl16_ref, 16, F32)

    row = lambda w: pl.BlockSpec((tm, w), lambda s: (s, 0))
    sds = jax.ShapeDtypeStruct
    return pl.pallas_call(
        body, name="att_gate_bwd", grid=(S // tm,),
        in_specs=[row(AT_W), row(AT_W), row(STAT_W), row(AT_W)] + [_full((PERM_ROWS, PERM_ROWS))] * 2,
        out_specs=[row(AT_W), _dil_spec(4, tm, AT_W), _dil_spec(16, tm, AT_W),
                   row(STAT_W), _dil_spec(4, tm, STAT_W), _dil_spec(16, tm, STAT_W), row(AT_W)],
        out_shape=[sds((S, AT_W), BF16), sds((4, S // 4, AT_W), BF16), sds((16, S // 16, AT_W), BF16),
                   sds((S, STAT_W), F32), sds((4, S // 4, STAT_W), F32), sds((16, S // 16, STAT_W), F32),
                   sds((S, AT_W), BF16)],
        scratch_shapes=[_slab_scratch(tm, STAT_W)],
        compiler_params=_params(("parallel",)),
    )(dm_at, o_at, lse, z_at, _perm_matrix(4, True), _perm_matrix(16, True))


def _att_bwd(qkv_d, do_d, dl_d):
    d, L, _ = qkv_d.shape
    nb = L // ATT_BLK
    qb = ATT_QB
    rows = qb * ATT_BLK
    nsteps = L // rows

    def body(qc_ref, qn_ref, kp_ref, kc_ref, vp_ref, vc_ref, ac_ref, an_ref, lc_ref, ln_ref, bq_ref, bk_ref, dqkv_ref,
             s_scr, dp_scr, st_scr, dpt_scr, ds_scr, pt_scr, dst_scr):
        n = pl.program_id(1)
        first = _iota((ATT_BLK, HEAD_PAIR), 1) < AT_HEAD
        pairs = [slice(HEAD_PAIR * hp, HEAD_PAIR * (hp + 1)) for hp in range(4)]
        blk = lambda i: slice(ATT_BLK * i, ATT_BLK * (i + 1))

        def stack(t):
            zero = jnp.zeros_like(t)
            return jnp.concatenate([jnp.where(first, t, zero), jnp.where(first, zero, t)], axis=0)

        def unstack(t2):
            return jnp.where(first, t2[:ATT_BLK], t2[ATT_BLK:])

        def with_prev(i, sl, prev_ref, cur_ref):
            before = prev_ref[0, :, sl] if i == 0 else cur_ref[0, blk(i - 1), sl]
            return jnp.concatenate([before, cur_ref[0, blk(i), sl]], axis=0)

        def with_next(i, sl, cur_ref, next_ref):
            after = next_ref[0, :, sl] if i == qb - 1 else cur_ref[0, blk(i + 1), sl]
            return jnp.concatenate([cur_ref[0, blk(i), sl], after], axis=0)

        for i in range(qb):
            for hp, sl in enumerate(pairs):
                j = 4 * i + hp
                s_scr[j] = _dot_nt(stack(qc_ref[0, blk(i), sl] * ATT_SCALE), with_prev(i, sl, kp_ref, kc_ref))
                dp_scr[j] = _dot_nt(stack(ac_ref[0, blk(i), sl]), with_prev(i, sl, vp_ref, vc_ref))
                st_scr[j] = _dot_nt(stack(kc_ref[0, blk(i), sl] * ATT_SCALE), with_next(i, sl, qc_ref, qn_ref))
                dpt_scr[j] = _dot_nt(stack(vc_ref[0, blk(i), sl]), with_next(i, sl, ac_ref, an_ref))
        for i in range(qb):
            bias = bq_ref[jnp.minimum(n, 1)] if i == 0 else bq_ref[1]
            bias_t = bk_ref[jnp.minimum(nsteps - 1 - n, 1)] if i == qb - 1 else bk_ref[1]
            bias2 = jnp.concatenate([bias, bias], axis=0)
            bias_t2 = jnp.concatenate([bias_t, bias_t], axis=0)
            dl_c = lc_ref[0, blk(i), :]
            dl_t = with_next(i, slice(None), lc_ref, ln_ref).T
            for hp in range(4):
                j = 4 * i + hp
                at = [STAT_LANES * (2 * hp), STAT_LANES * (2 * hp + 1)]
                col = lambda t, o: jnp.concatenate([t[:, a + o: a + o + 1] for a in at], axis=0)
                p = jnp.exp(s_scr[j] + bias2 - col(dl_c, STAT_LSE_LANE))
                ds_scr[j] = (p * (dp_scr[j] - col(dl_c, 0))).astype(BF16)
                row = lambda t, o: jnp.concatenate([jnp.broadcast_to(t[a + o: a + o + 1, :], (ATT_BLK, 2 * ATT_BLK)) for a in at], axis=0)
                pt = jnp.exp(st_scr[j] + bias_t2 - row(dl_t, STAT_LSE_LANE))
                pt_scr[j] = pt.astype(BF16)
                dst_scr[j] = (pt * (dpt_scr[j] - row(dl_t, 0))).astype(BF16)
        for i in range(qb):
            for hp, sl in enumerate(pairs):
                j = 4 * i + hp
                dq = unstack(_dot(ds_scr[j], with_prev(i, sl, kp_ref, kc_ref))) * ATT_SCALE
                dk = unstack(_dot(dst_scr[j], with_next(i, sl, qc_ref, qn_ref))) * ATT_SCALE
                dv = unstack(_dot(pt_scr[j], with_next(i, sl, ac_ref, an_ref)))
                dqkv_ref[0, blk(i), sl] = dq.astype(BF16)
                dqkv_ref[0, blk(i), AT_W + HEAD_PAIR * hp: AT_W + HEAD_PAIR * (hp + 1)] = dk.astype(BF16)
                dqkv_ref[0, blk(i), 2 * AT_W + HEAD_PAIR * hp: 2 * AT_W + HEAD_PAIR * (hp + 1)] = dv.astype(BF16)

    cur = lambda j: pl.BlockSpec((1, rows, AT_W), lambda r, n: (r, n, j))
    prev = lambda j: pl.BlockSpec((1, ATT_BLK, AT_W), lambda r, n: (r, jnp.maximum(qb * n - 1, 0), j))
    nxt_blk = lambda n: jnp.minimum(qb * (n + 1), nb - 1)
    sq = (4 * qb, 2 * ATT_BLK, 2 * ATT_BLK)
    return pl.pallas_call(
        body, name=f"att_bwd_d{d}", grid=(d, nsteps),
        in_specs=[cur(0), pl.BlockSpec((1, ATT_BLK, AT_W), lambda r, n: (r, nxt_blk(n), 0)), prev(1), cur(1), prev(2), cur(2),
                  pl.BlockSpec((1, rows, AT_W), lambda r, n: (r, n, 0)),
                  pl.BlockSpec((1, ATT_BLK, AT_W), lambda r, n: (r, nxt_blk(n), 0)),
                  pl.BlockSpec((1, rows, STAT_W), lambda r, n: (r, n, 0)),
                  pl.BlockSpec((1, ATT_BLK, STAT_W), lambda r, n: (r, nxt_blk(n), 0)),
                  _full((2, ATT_BLK, 2 * ATT_BLK)), _full((2, ATT_BLK, 2 * ATT_BLK))],
        out_specs=pl.BlockSpec((1, rows, 3 * AT_W), lambda r, n: (r, n, 0)),
        out_shape=jax.ShapeDtypeStruct((d, L, 3 * AT_W), BF16),
        scratch_shapes=[pltpu.VMEM(sq, F32)] * 4 + [pltpu.VMEM(sq, BF16)] * 3,
        compiler_params=_params(("parallel", "parallel")),
    )(qkv_d, qkv_d, qkv_d, qkv_d, qkv_d, qkv_d, do_d, do_d, dl_d, dl_d, *_att_bias())


def _att_bwd_combine(dqkvs, cos2, sin2):
    S = dqkvs[0].shape[1]
    tm = MIX_TM

    def body(a_ref, b4_ref, c16_ref, p4_ref, p16_ref, cos_ref, sin_ref, dq_ref, dk_ref, dv_ref):
        t = a_ref[...].astype(F32) + _merge_bf16(b4_ref, p4_ref, 4) + _merge_bf16(c16_ref, p16_ref, 16)
        dy = t[:, : 2 * AT_W]
        cosf = jnp.tile(cos_ref[...], (1, 8))
        sinf = jnp.tile(sin_ref[...], (1, 8))
        dx = dy * cosf - _rope_rot(dy) * sinf
        dq_ref[...] = dx[:, :AT_W].astype(BF16)
        dk_ref[...] = dx[:, AT_W:].astype(BF16)
        dv_ref[...] = t[:, 2 * AT_W:].astype(BF16)

    row = lambda w: pl.BlockSpec((tm, w), lambda s: (s, 0))
    act = jax.ShapeDtypeStruct((S, AT_W), BF16)
    return pl.pallas_call(
        body, name="att_bwd_combine", grid=(S // tm,),
        in_specs=[row(3 * AT_W), _dil_spec(4, tm, 3 * AT_W), _dil_spec(16, tm, 3 * AT_W),
                  _full((PERM_ROWS, PERM_ROWS)), _full((PERM_ROWS, PERM_ROWS)), row(128), row(128)],
        out_specs=[row(AT_W), row(AT_W), row(AT_W)],
        out_shape=[act, act, act],
        compiler_params=_params(("parallel",)),
    )(dqkvs[0].reshape(S, 3 * AT_W), dqkvs[1], dqkvs[2], _perm_matrix(4), _perm_matrix(16), cos2, sin2)


def _outproj(x2, tgt2, mix_at, w_out_full, fnw, o_hg, p1, hg_norm_w):
    S = x2.shape[0]
    tm = PROJ_TM
    ns = S // tm

    def body(x_ref, t_ref, ma_ref, w_ref, fw_ref, o_ref, z_ref, hgw_ref,
             dh_ref, doh_ref, dzh_ref, dma_ref, gw_ref, gfw_ref, ghgw_ref, loss_ref):
        s = pl.program_id(0)

        @pl.when(s == 0)
        def _():
            gw_ref[...] = jnp.zeros_like(gw_ref)
            gfw_ref[...] = jnp.zeros_like(gfw_ref)
            ghgw_ref[...] = jnp.zeros_like(ghgw_ref)
            loss_ref[...] = jnp.zeros_like(loss_ref)

        heads = [slice(HG_HEAD * hh, HG_HEAD * (hh + 1)) for hh in range(HG_HEADS)]
        norm = []
        for sl in heads:
            o, z = o_ref[:, sl], z_ref[:, sl]
            rs = lax.rsqrt(jnp.mean(o * o, axis=-1, keepdims=True) + EPS)
            norm.append((rs, o * rs, _sigmoid(z)))
        mh = jnp.concatenate([(oh * hgw_ref[:, sl] * (z_ref[:, sl] * sz)).astype(BF16)
                              for sl, (_, oh, sz) in zip(heads, norm)], axis=1)
        y = _dot(mh, w_ref[:HG_W, :]) + _dot(ma_ref[...], w_ref[HG_W:, :])
        h = x_ref[...] + y
        r = lax.rsqrt(jnp.mean(h * h, axis=-1, keepdims=True) + EPS)
        hn = h * r
        fw = fw_ref[...]
        err = hn * fw - t_ref[...]
        loss_ref[...] += 0.5 * jnp.sum(jnp.mean(err * err, axis=-1, keepdims=True))
        dout = err * (1.0 / D_MODEL)
        gfw_ref[...] += jnp.sum(dout * hn, axis=0, keepdims=True)
        dhn = dout * fw
        dh = r * (dhn - hn * jnp.mean(dhn * hn, axis=-1, keepdims=True))
        dh_ref[...] = dh
        dhb = dh.astype(BF16)
        dma_ref[...] = _dot_nt(dhb, w_ref[HG_W:, :])
        dmh = _dot_nt(dhb, w_ref[:HG_W, :])
        for sl, (rs, oh, sz) in zip(heads, norm):
            z, dmix, gwv = z_ref[:, sl], dmh[:, sl], hgw_ref[:, sl]
            dzh_ref[:, sl] = (dmix * (oh * gwv) * (sz * (1.0 + z * (1.0 - sz)))).astype(BF16)
            don = dmix * (z * sz)
            ghgw_ref[:, sl] += jnp.sum(don * oh, axis=0, keepdims=True)
            dy = don * gwv
            doh_ref[:, sl] = rs * (dy - oh * jnp.mean(dy * oh, axis=-1, keepdims=True))
        gw_ref[:HG_W, :] += _dot_tn(mh, dhb)
        gw_ref[HG_W:, :] += _dot_tn(ma_ref[...], dhb)

    row = lambda w: pl.BlockSpec((tm, w), lambda s: (s, 0))
    return pl.pallas_call(
        body, name="outproj", grid=(ns,),
        in_specs=[row(D_MODEL), row(D_MODEL), row(AT_W),
                  _full((D_MODEL, D_MODEL)), _full((1, D_MODEL)),
                  row(HG_W), pl.BlockSpec((tm, HG_W), lambda s: (s, 1)), _full((1, HG_W))],
        out_specs=[row(D_MODEL), row(HG_W), row(HG_W), row(AT_W), _full((D_MODEL, D_MODEL)), _full((1, D_MODEL)),
                   _full((1, HG_W)), _full((8, 128))],
        out_shape=[jax.ShapeDtypeStruct((S, D_MODEL), F32), jax.ShapeDtypeStruct((S, HG_W), F32),
                   jax.ShapeDtypeStruct((S, HG_W), BF16), jax.ShapeDtypeStruct((S, AT_W), F32),
                   jax.ShapeDtypeStruct((D_MODEL, D_MODEL), F32), jax.ShapeDtypeStruct((1, D_MODEL), F32),
                   jax.ShapeDtypeStruct((1, HG_W), F32), jax.ShapeDtypeStruct((8, 128), F32)],
        compiler_params=_params(("arbitrary",)),
    )(x2, tgt2, mix_at, w_out_full, fnw, o_hg, p1, hg_norm_w)


def _inproj_bwd_x(dps, w_in_full, x2, norm_w, dh):
    S = x2.shape[0]
    tm = PROJ_TM

    def body(d0, d1, d2, d3, d4, d5, d6, d7, w_ref, x_ref, nw_ref, dh_ref, gx_ref, gnw_ref):
        s = pl.program_id(0)

        @pl.when(s == 0)
        def _():
            gnw_ref[...] = jnp.zeros_like(gnw_ref)

        du = jnp.zeros((tm, D_MODEL), F32)
        for i, dref in enumerate((d0, d1, d2, d3, d4, d5, d6, d7)):
            j, half = divmod(i, 2)
            du = du + _dot_nt(dref[...], w_ref[j, :, 512 * half: 512 * (half + 1)])
        x = x_ref[...]
        r = lax.rsqrt(jnp.mean(x * x, axis=-1, keepdims=True) + EPS)
        xh = x * r
        gnw_ref[...] += jnp.sum(du * xh, axis=0, keepdims=True)
        dun = du * nw_ref[...]
        gx_ref[...] = dh_ref[...] + r * (dun - xh * jnp.mean(dun * xh, axis=-1, keepdims=True))

    row = lambda w: pl.BlockSpec((tm, w), lambda s: (s, 0))
    return pl.pallas_call(
        body, name="inproj_bwd_x", grid=(S // tm,),
        in_specs=[row(512)] * 8 + [_full((4, D_MODEL, 1024)), row(D_MODEL), _full((1, D_MODEL)), row(D_MODEL)],
        out_specs=[row(D_MODEL), _full((1, D_MODEL))],
        out_shape=[jax.ShapeDtypeStruct((S, D_MODEL), F32), jax.ShapeDtypeStruct((1, D_MODEL), F32)],
        compiler_params=_params(("arbitrary",)),
    )(*dps, w_in_full, x2, norm_w, dh)


def _inproj_bwd_w(u, dps):
    S = u.shape[0]
    tm = PROJ_TM

    def body(u_ref, d0, d1, d2, d3, d4, d5, d6, d7, g_ref):
        @pl.when(pl.program_id(0) == 0)
        def _():
            g_ref[...] = jnp.zeros_like(g_ref)

        ub = u_ref[...]
        for i, dref in enumerate((d0, d1, d2, d3, d4, d5, d6, d7)):
            j, half = divmod(i, 2)
            g_ref[j, :, 512 * half: 512 * (half + 1)] += _dot_tn(ub, dref[...])

    return pl.pallas_call(
        body, name="inproj_bwd_w", grid=(S // tm,),
        in_specs=[pl.BlockSpec((tm, D_MODEL), lambda s: (s, 0))] + [pl.BlockSpec((tm, 512), lambda s: (s, 0))] * 8,
        out_specs=_full((4, D_MODEL, 1024)),
        out_shape=jax.ShapeDtypeStruct((4, D_MODEL, 1024), F32),
        compiler_params=_params(("arbitrary",)),
    )(u, *dps)


def _adamw_update(gg, w_ref, m_ref, v_ref, d_ref, nm_ref, nv_ref):
    nm = ADAM_B1 * m_ref[...] + (1.0 - ADAM_B1) * gg
    nv = ADAM_B2 * v_ref[...] + (1.0 - ADAM_B2) * (gg * gg)
    m_hat = nm / (1.0 - ADAM_B1 ** ADAM_STEP)
    v_hat = nv / (1.0 - ADAM_B2 ** ADAM_STEP)
    d_ref[...] = -ADAM_LR * (m_hat / (jnp.sqrt(v_hat) + ADAM_EPS) + ADAM_WD * w_ref[...])
    nm_ref[...] = nm
    nv_ref[...] = nv


def _adamw(w, g, m, v, name):
    rows, cols = w.shape
    tr = min(rows, 256)

    def body(w_ref, g_ref, m_ref, v_ref, d_ref, nm_ref, nv_ref):
        _adamw_update(g_ref[...], w_ref, m_ref, v_ref, d_ref, nm_ref, nv_ref)

    spec = pl.BlockSpec((tr, cols), lambda i: (i, 0))
    sds = jax.ShapeDtypeStruct((rows, cols), F32)
    return pl.pallas_call(
        body, name=name, grid=(rows // tr,),
        in_specs=[spec] * 4, out_specs=[spec] * 3, out_shape=[sds] * 3,
        compiler_params=_params(("parallel",)),
    )(w, g, m, v)


def _place():
    return lax.axis_index("x"), lax.axis_index("y"), lax.axis_index("c")


def _gather_weights(w_in_s, w_out_s):
    def body(win_ref, wout_ref, fin_ref, fout_ref, send_sems, recv_sems):
        x, y, c = _place()
        me = (x, y, c)
        sib = (x, y, 1 - c)
        mine = 2 * x + y
        fin_ref[mine] = win_ref[...].astype(BF16)
        fout_ref[mine] = wout_ref[...].astype(BF16)
        chips = [(1 - x, y), (x, 1 - y), (1 - x, 1 - y)]

        def halves(chip, half):
            return (fin_ref.at[chip, pl.ds(half * 512, 512), :], fout_ref.at[chip, pl.ds(half * 128, 128), :])

        def copy(k, ref, to):
            return pltpu.make_async_remote_copy(src_ref=ref, dst_ref=ref, send_sem=send_sems.at[k],
                                                recv_sem=recv_sems.at[k], device_id=to, device_id_type=MESH)

        first, passed = [], []
        for j, (cx, cy) in enumerate(chips):
            for a, ref in enumerate(halves(mine, c)):
                first.append(copy(2 * j + a, ref, (cx, cy, c)))
        for cp in first:
            cp.start()
        for j, (cx, cy) in enumerate(chips):
            for a, ref in enumerate(halves(2 * cx + cy, c)):
                copy(2 * j + a, ref, me).wait_recv()
                fwd = copy(6 + 2 * j + a, ref, sib)
                fwd.start()
                passed.append(fwd)
        for j, (cx, cy) in enumerate(chips):
            for a, ref in enumerate(halves(2 * cx + cy, 1 - c)):
                copy(6 + 2 * j + a, ref, me).wait_recv()
        for cp in first + passed:
            cp.wait_send()

    vm = pl.BlockSpec(memory_space=pltpu.VMEM)
    return pl.pallas_call(
        body, name="gather_weights",
        in_specs=[vm, vm], out_specs=[vm, vm],
        out_shape=[jax.ShapeDtypeStruct((4, D_MODEL, 1024), BF16), jax.ShapeDtypeStruct((4, 256, D_MODEL), BF16)],
        scratch_shapes=[pltpu.SemaphoreType.DMA((12,)), pltpu.SemaphoreType.DMA((12,))],
        compiler_params=pltpu.CompilerParams(vmem_limit_bytes=VMEM_LIMIT),
    )(w_in_s, w_out_s)


def _swap_halves(g_in, g_out):
    def body(gin_ref, gout_ref, rin_ref, rout_ref, send_sems, recv_sems):
        x, y, c = _place()
        sib = (x, y, 1 - c)
        cps = [pltpu.make_async_remote_copy(src_ref=src.at[:, 1 - c], dst_ref=dst, send_sem=send_sems.at[k],
                                            recv_sem=recv_sems.at[k], device_id=sib, device_id_type=MESH)
               for k, (src, dst) in enumerate(((gin_ref, rin_ref), (gout_ref, rout_ref)))]
        for cp in cps:
            cp.start()
        for cp in cps:
            cp.wait()

    hbm = pl.BlockSpec(memory_space=pl.ANY)
    return pl.pallas_call(
        body, name="swap_halves",
        in_specs=[hbm, hbm], out_specs=[hbm, hbm],
        out_shape=[jax.ShapeDtypeStruct((4,) + g.shape[2:], F32) for g in (g_in, g_out)],
        scratch_shapes=[pltpu.SemaphoreType.DMA((2,)), pltpu.SemaphoreType.DMA((2,))],
    )(g_in, g_out)


def _add_half(g, r, cidx, name):
    n, _, rows, cols = g.shape

    def body(c_ref, g_ref, r_ref, o_ref):
        o_ref[0] = (g_ref[0, 0] + r_ref[0]).astype(BF16)

    return pl.pallas_call(
        body, name=name,
        grid_spec=pltpu.PrefetchScalarGridSpec(
            num_scalar_prefetch=1, grid=(n,),
            in_specs=[pl.BlockSpec((1, 1, rows, cols), lambda j, c_ref: (j, c_ref[0], 0, 0)),
                      pl.BlockSpec((1, rows, cols), lambda j, c_ref: (j, 0, 0))],
            out_specs=pl.BlockSpec((1, rows, cols), lambda j, c_ref: (j, 0, 0))),
        out_shape=jax.ShapeDtypeStruct((n, rows, cols), BF16),
        compiler_params=_params(("parallel",)),
    )(cidx, g, r)


def _exchange_copies(in_ref, out_ref, lin_ref, lout_ref, send_sems, recv_sems):
    x, y, c = _place()
    cps = []
    for k, (cx, cy) in enumerate([(1 - x, y), (x, 1 - y), (1 - x, 1 - y)]):
        for a, (src, dst) in enumerate(((in_ref, lin_ref), (out_ref, lout_ref))):
            cps.append(pltpu.make_async_remote_copy(
                src_ref=src.at[2 * cx + cy], dst_ref=dst.at[k], send_sem=send_sems.at[2 * k + a],
                recv_sem=recv_sems.at[2 * k + a], device_id=(cx, cy, c), device_id_type=MESH))
    return cps


def _exchange_start(cs_in, cs_out):
    def body(in_ref, out_ref, lin_ref, lout_ref, send_sems, recv_sems, in_thru, out_thru, lin_thru, lout_thru, token):
        for cp in _exchange_copies(in_ref, out_ref, lin_ref, lout_ref, send_sems, recv_sems):
            cp.start()
        token[...] = jnp.zeros_like(token)

    lands = [lax.empty((3,) + a.shape[1:], a.dtype) for a in (cs_in, cs_out)]
    bufs = [pltpu.with_memory_space_constraint(a, pltpu.HBM) for a in (cs_in, cs_out, *lands)]
    hbm = pl.BlockSpec(memory_space=pltpu.HBM)
    sem = pl.BlockSpec(memory_space=pltpu.SEMAPHORE)
    return pl.pallas_call(
        body, name="exchange_start",
        in_specs=[hbm] * 4,
        out_specs=[sem, sem, hbm, hbm, hbm, hbm, pl.BlockSpec(memory_space=pltpu.VMEM)],
        out_shape=[pltpu.SemaphoreType.DMA((6,)), pltpu.SemaphoreType.DMA((6,))]
                  + [pltpu.HBM(b.shape, b.dtype) for b in bufs] + [jax.ShapeDtypeStruct((8, 128), F32)],
        input_output_aliases={0: 2, 1: 3, 2: 4, 3: 5},
        compiler_params=pltpu.CompilerParams(has_side_effects=pltpu.SideEffectType.DATAFLOW_SIDE_EFFECTING),
    )(*bufs)


def _exchange_wait(send_sems, recv_sems, in_thru, out_thru, lin_thru, lout_thru, after):
    def body(in_ref, out_ref, lin_ref, lout_ref, send_sems, recv_sems, after_ref, in_dead, out_dead, got_in, got_out):
        for cp in _exchange_copies(in_ref, out_ref, lin_ref, lout_ref, send_sems, recv_sems):
            cp.wait_send()
            cp.wait_recv()

    hbm = pl.BlockSpec(memory_space=pltpu.HBM)
    sem = pl.BlockSpec(memory_space=pltpu.SEMAPHORE)
    bufs = (in_thru, out_thru, lin_thru, lout_thru)
    return pl.pallas_call(
        body, name="exchange_wait",
        in_specs=[hbm] * 4 + [sem, sem, pl.BlockSpec(memory_space=pl.ANY)],
        out_specs=[hbm] * 4,
        out_shape=[pltpu.HBM(b.shape, b.dtype) for b in bufs],
        input_output_aliases={0: 0, 1: 1, 2: 2, 3: 3},
        compiler_params=pltpu.CompilerParams(has_side_effects=pltpu.SideEffectType.DATAFLOW_SIDE_EFFECTING),
    )(*bufs, send_sems, recv_sems, after)


PEER_XOR = (2, 1, 3)


def _sum_chips(cs, r, chip_idx, name):
    _, rows, cols = r.shape
    tr = min(rows, 256)

    def body(m_ref, cs_ref, r_ref, o_ref):
        mine = m_ref[0]
        own = cs_ref[0].astype(F32)
        got = [r_ref[k].astype(F32) for k in range(3)]
        acc = None
        for s in range(4):
            rel = mine ^ s
            term = jnp.where(rel == 0, own, jnp.where(rel == PEER_XOR[0], got[0],
                                                      jnp.where(rel == PEER_XOR[1], got[1], got[2])))
            acc = term if acc is None else acc + term
        o_ref[...] = acc

    return pl.pallas_call(
        body, name=name,
        grid_spec=pltpu.PrefetchScalarGridSpec(
            num_scalar_prefetch=1, grid=(rows // tr,),
            in_specs=[pl.BlockSpec((1, tr, cols), lambda i, m_ref: (m_ref[0], i, 0)),
                      pl.BlockSpec((3, tr, cols), lambda i, m_ref: (0, i, 0))],
            out_specs=pl.BlockSpec((tr, cols), lambda i, m_ref: (i, 0))),
        out_shape=jax.ShapeDtypeStruct((rows, cols), F32),
        compiler_params=_params(("parallel",)),
    )(chip_idx, cs, r)


def _swap_reduced(h_in, h_out):
    def body(in_ref, out_ref, rin_ref, rout_ref, send_sems, recv_sems):
        x, y, c = _place()
        cps = [pltpu.make_async_remote_copy(src_ref=src, dst_ref=dst, send_sem=send_sems.at[k],
                                            recv_sem=recv_sems.at[k], device_id=(x, y, 1 - c), device_id_type=MESH)
               for k, (src, dst) in enumerate(((in_ref, rin_ref), (out_ref, rout_ref)))]
        for cp in cps:
            cp.start()
        for cp in cps:
            cp.wait()

    hbm = pl.BlockSpec(memory_space=pl.ANY)
    return pl.pallas_call(
        body, name="swap_reduced",
        in_specs=[hbm, hbm], out_specs=[hbm, hbm],
        out_shape=[jax.ShapeDtypeStruct(h.shape, F32) for h in (h_in, h_out)],
        scratch_shapes=[pltpu.SemaphoreType.DMA((2,)), pltpu.SemaphoreType.DMA((2,))],
    )(h_in, h_out)


def _adamw_halves(w, mine, theirs, m, v, cidx, name):
    rows, cols = w.shape
    half = rows // 2
    tr = min(half, 256)
    nbh = half // tr

    def body(c_ref, w_ref, a_ref, b_ref, m_ref, v_ref, g_ref, d_ref, nm_ref, nv_ref):
        gg = jnp.where(pl.program_id(0) // nbh == c_ref[0], a_ref[...], b_ref[...])
        g_ref[...] = gg
        _adamw_update(gg, w_ref, m_ref, v_ref, d_ref, nm_ref, nv_ref)

    spec = pl.BlockSpec((tr, cols), lambda i, c_ref: (i, 0))
    hspec = pl.BlockSpec((tr, cols), lambda i, c_ref: (i % nbh, 0))
    sds = jax.ShapeDtypeStruct((rows, cols), F32)
    return pl.pallas_call(
        body, name=name,
        grid_spec=pltpu.PrefetchScalarGridSpec(
            num_scalar_prefetch=1, grid=(rows // tr,),
            in_specs=[spec, hspec, hspec, spec, spec], out_specs=[spec] * 4),
        out_shape=[sds] * 4,
        compiler_params=_params(("parallel",)),
    )(cidx, w, mine, theirs, m, v)


def _allreduce_small(g_nw, g_fw, g_hgw, g_lbl, loss8):
    def body(nw_ref, fw_ref, hgw_ref, lbl_ref, loss_ref, out_ref, slots, send_sems, recv_sems):
        x, y, c = _place()
        me = 4 * x + 2 * y + c
        slots[me] = jnp.zeros((8, D_MODEL), F32)
        slots[me, 0:1, :] = nw_ref[...]
        slots[me, 1:2, :] = fw_ref[...]
        slots[me, 2:3, 0:HG_W] = hgw_ref[...]
        slots[me, 3:4, 0:HG_W] = lbl_ref[0:1, :]
        slots[me, 3:4, HG_W:] = lbl_ref[1:2, :]
        slots[me, 4:5, 0:128] = loss_ref[0:1, :]
        cps = []
        for k in range(1, 8):
            dx, dy, dc = (k >> 2) & 1, (k >> 1) & 1, k & 1
            to = (x ^ dx, y ^ dy, c ^ dc)
            cps.append(pltpu.make_async_remote_copy(
                src_ref=slots.at[me], dst_ref=slots.at[me], send_sem=send_sems.at[k - 1],
                recv_sem=recv_sems.at[k - 1], device_id=to, device_id_type=MESH))
        for cp in cps:
            cp.start()
        for cp in cps:
            cp.wait()
        acc = slots[0]
        for i in range(1, 8):
            acc = acc + slots[i]
        out_ref[...] = acc

    vm = pl.BlockSpec(memory_space=pltpu.VMEM)
    return pl.pallas_call(
        body, name="allreduce_small",
        in_specs=[vm] * 5, out_specs=vm,
        out_shape=jax.ShapeDtypeStruct((8, D_MODEL), F32),
        scratch_shapes=[pltpu.VMEM((8, 8, D_MODEL), F32), pltpu.SemaphoreType.DMA((7,)), pltpu.SemaphoreType.DMA((7,))],
    )(g_nw, g_fw, g_hgw, g_lbl, loss8)


def _rope_tables(S):
    inv_freq = (np.float32(1.0) / np.power(np.float32(ROPE_THETA), np.arange(ROPE_HALF, dtype=np.float32) / np.float32(ROPE_HALF))).astype(np.float32)
    ang = (np.arange(S, dtype=np.float32)[:, None] * inv_freq[None, :]).astype(np.float32)
    cos, sin = np.cos(ang).astype(np.float32), np.sin(ang).astype(np.float32)
    cos2 = np.concatenate([cos, cos, cos, cos], axis=-1)
    sin2 = np.concatenate([-sin, sin, -sin, sin], axis=-1)
    return jnp.asarray(cos2), jnp.asarray(sin2)


def _local_step(x2, tgt2, norm_w, w_in_full, lbl, hg_norm_w, w_out4, fnw):
    S = x2.shape[0]
    cos2, sin2 = _rope_tables(S)
    w_out_full = w_out4.reshape(D_MODEL, D_MODEL)
    p0, p1, qkv, qkv4, qkv16, z_at, u = _inproj(x2, norm_w, w_in_full, cos2, sin2)
    o_hg, states = _hg_fwd(p0, p1, lbl)
    qkv_ds = [qkv.reshape(1, S, 3 * AT_W), qkv4, qkv16]
    os_d, ls_d = zip(*[_att_fwd(q) for q in qkv_ds])
    o_at, lse, mix_at = _att_combine(os_d, ls_d, z_at)
    dh, do_hg, dz_hg, dm_at, g_wout, g_fw, g_hgw, loss8 = _outproj(x2, tgt2, mix_at, w_out_full, fnw, o_hg, p1, hg_norm_w)
    dqr, dfl, dv_hg, g_lbl = _hg_bwd(p0, p1, do_hg, states, lbl)
    do1, do4, do16, dl1, dl4, dl16, dz_at = _att_gate_bwd(dm_at, o_at, lse, z_at)
    do_ds = [do1.reshape(1, S, AT_W), do4, do16]
    dl_ds = [dl1.reshape(1, S, STAT_W), dl4, dl16]
    dqkvs = [_att_bwd(q, a, b) for q, a, b in zip(qkv_ds, do_ds, dl_ds)]
    dq_at, dk_at, dv_at = _att_bwd_combine(dqkvs, cos2, sin2)
    dps = [dqr, dfl, dv_hg, dz_hg, dq_at, dk_at, dv_at, dz_at]
    return loss8, dps, u, dh, g_lbl, g_hgw, g_wout, g_fw


def kernel(x, norm_w, w_in, hgrn_lb_logits, hg_norm_w, w_out, final_norm_w, loss_target, m_norm_w, m_w_in, m_hgrn_lb_logits, m_hg_norm_w, m_w_out, m_final_norm_w, v_norm_w, v_w_in, v_hgrn_lb_logits, v_hg_norm_w, v_w_out, v_final_norm_w):
    S = x.shape[1]
    w_in_full, w_out_full = _gather_weights(w_in[0], w_out[0])
    loss8, dps, u, dh, g_lbl, g_hgw, g_wout, g_fw = _local_step(
        x[0], loss_target[0], norm_w, w_in_full, hgrn_lb_logits, hg_norm_w,
        w_out_full, final_norm_w.reshape(1, D_MODEL))

    cidx = lax.axis_index("c").astype(jnp.int32).reshape(1)
    g_win = _inproj_bwd_w(u, dps)
    g_in4 = g_win.reshape(4, 2, 512, 1024)
    g_out4 = g_wout.reshape(4, 2, 128, D_MODEL)
    r_in, r_out = _swap_halves(g_in4, g_out4)
    cs_in = _add_half(g_in4, r_in, cidx, "add_half_in")
    cs_out = _add_half(g_out4, r_out, cidx, "add_half_out")
    *started, token = _exchange_start(cs_in, cs_out)
    grad_x, g_nw = _inproj_bwd_x(dps, w_in_full, x[0], norm_w + token[0:1, 0:1], dh)
    cs_in, cs_out, x_in, x_out = _exchange_wait(*started, g_nw)
    chip_idx = (2 * lax.axis_index("x") + lax.axis_index("y")).astype(jnp.int32).reshape(1)
    h_in = _sum_chips(cs_in, x_in, chip_idx, "sum_chips_in")
    h_out = _sum_chips(cs_out, x_out, chip_idx, "sum_chips_out")
    s_in, s_out = _swap_reduced(h_in, h_out)

    red = _allreduce_small(g_nw, g_fw, g_hgw, g_lbl, loss8)
    loss = red[4, 0]
    grad_norm_w = red[0:1, :]
    grad_final_norm_w = red[1, :]
    grad_hg_norm_w = red[2:3, :HG_W]
    grad_lbl = jnp.concatenate([red[3:4, :HG_W], red[3:4, HG_W:]], axis=0)

    d_nw, m_nw, v_nw = _adamw(norm_w, grad_norm_w, m_norm_w, v_norm_w, "adamw_norm_w")
    grad_w_in, d_win, m_win, v_win = _adamw_halves(w_in[0], h_in, s_in, m_w_in[0], v_w_in[0], cidx, "adamw_w_in")
    d_lbl, m_lbl, v_lbl = _adamw(hgrn_lb_logits, grad_lbl, m_hgrn_lb_logits, v_hgrn_lb_logits, "adamw_lb_logits")
    d_hgw, m_hgw, v_hgw = _adamw(hg_norm_w, grad_hg_norm_w, m_hg_norm_w, v_hg_norm_w, "adamw_hg_norm_w")
    grad_w_out, d_wout, m_wout, v_wout = _adamw_halves(w_out[0], h_out, s_out, m_w_out[0], v_w_out[0], cidx, "adamw_w_out")
    d_fw, m_fw, v_fw = _adamw(final_norm_w.reshape(1, D_MODEL), grad_final_norm_w.reshape(1, D_MODEL),
                              m_final_norm_w.reshape(1, D_MODEL), v_final_norm_w.reshape(1, D_MODEL), "adamw_final_norm_w")
    e1 = lambda a: a[None]
    flat = lambda a: a.reshape(D_MODEL)
    return (loss, grad_x[None], grad_norm_w, e1(grad_w_in), grad_lbl, grad_hg_norm_w, e1(grad_w_out), grad_final_norm_w,
            d_nw, e1(d_win), d_lbl, d_hgw, e1(d_wout), flat(d_fw),
            m_nw, e1(m_win), m_lbl, m_hgw, e1(m_wout), flat(m_fw),
            v_nw, e1(v_win), v_lbl, v_hgw, e1(v_wout), flat(v_fw))
```

```python
import jax
import jax.numpy as jnp
import numpy as np
from jax import lax
from jax.experimental import pallas as pl
from jax.experimental.pallas import tpu as pltpu

F32 = jnp.float32
BF16 = jnp.bfloat16
MESH = pl.DeviceIdType.MESH

D_MODEL = 1024
HG_HEADS = 4
HG_HEAD = 128
HG_W = HG_HEADS * HG_HEAD
AT_HEAD = 64
AT_W = 512
HEAD_PAIR = 2 * AT_HEAD
ROPE_HALF = 32
ROPE_THETA = 10000.0
EPS = 1e-6
CHUNK = 128
LEVELS = (64, 32, 16, 8, 4, 2, 1)
DIAG = 1
SUBLANES = 8
ATT_BLK = 128
ATT_QB = 4
ATT_SCALE = 0.125
STAT_W = 128
STAT_LANES = 16
STAT_LSE_LANE = 8
NEG = -1e30
VMEM_LIMIT = 56 * 1024 * 1024
MIX_TM = 1024
PROJ_TM = 512

ADAM_LR = 0.001
ADAM_B1 = 0.9
ADAM_B2 = 0.999
ADAM_EPS = 1e-08
ADAM_WD = 0.01
ADAM_STEP = 10


def _iota(shape, dim):
    return lax.broadcasted_iota(jnp.int32, shape, dim)


def _dot(a, b):
    return jnp.dot(a, b, preferred_element_type=F32)


def _dot_nt(a, b):
    return lax.dot_general(a, b, (((1,), (1,)), ((), ())), preferred_element_type=F32)


def _dot_tn(a, b):
    return lax.dot_general(a, b, (((0,), (0,)), ((), ())), preferred_element_type=F32)


def _sigmoid(v):
    return 0.5 * jnp.tanh(0.5 * v) + 0.5


def _params(sem=None, vmem=VMEM_LIMIT):
    return pltpu.CompilerParams(dimension_semantics=sem, vmem_limit_bytes=vmem)


def _full(shape):
    n = len(shape)
    return pl.BlockSpec(shape, lambda *_: (0,) * n)


def _rope_rot(y):
    n = y.shape[1]
    first = (_iota(y.shape, 1) & (2 * ROPE_HALF - 1)) < ROPE_HALF
    return jnp.where(first, pltpu.roll(y, n - ROPE_HALF, 1), pltpu.roll(y, ROPE_HALF, 1))


def _dil_spec(d, tm, width):
    return pl.BlockSpec((d, tm // d, width), lambda s: (0, s, 0))


LANES = 128


def _slab_scratch(tm, width):
    return pltpu.VMEM((width // LANES, tm, LANES), F32)


def _to_slabs(v, slabs_ref):
    for j in range(slabs_ref.shape[0]):
        slabs_ref[j] = v[:, LANES * j: LANES * (j + 1)]


def _from_slabs(slabs_ref):
    return jnp.concatenate([slabs_ref[j] for j in range(slabs_ref.shape[0])], axis=1)


def _split_residues(slabs_ref, dst_ref, d, dtype):
    nslab, tm, _ = slabs_ref.shape
    for r in range(d):
        for j in range(nslab):
            dst_ref[r, :, LANES * j: LANES * (j + 1)] = slabs_ref[j, pl.ds(r, tm // d, stride=d), :].astype(dtype)


def _merge_residues(src_ref, slabs_ref, d):
    nslab, tm, _ = slabs_ref.shape
    for r in range(d):
        for j in range(nslab):
            slabs_ref[j, pl.ds(r, tm // d, stride=d), :] = src_ref[r, :, LANES * j: LANES * (j + 1)].astype(F32)


PERM_ROWS = 256


def _perm_matrix(d, inverse=False):
    n = PERM_ROWS // d
    t = np.arange(PERM_ROWS)
    p = np.zeros((PERM_ROWS, PERM_ROWS), np.float32)
    p[t, (t % d) * n + t // d] = 1.0
    return jnp.asarray(p.T if inverse else p, BF16)


def _merge_bf16(src_ref, perm_ref, d):
    n = PERM_ROWS // d
    tm = src_ref.shape[1] * d
    outs = []
    for b in range(tm // PERM_ROWS):
        flat = jnp.concatenate([src_ref[r, n * b: n * (b + 1), :] for r in range(d)], axis=0)
        outs.append(_dot(perm_ref[...], flat))
    return jnp.concatenate(outs, axis=0)


def _split_bf16(v, perm_t_ref, dst_ref, d):
    n = PERM_ROWS // d
    for b in range(v.shape[0] // PERM_ROWS):
        flat = _dot(perm_t_ref[...], v[PERM_ROWS * b: PERM_ROWS * (b + 1), :]).astype(BF16)
        for r in range(d):
            dst_ref[r, n * b: n * (b + 1), :] = flat[n * r: n * (r + 1), :]


def _inproj(x2, norm_w, w_in_full, cos2, sin2):
    S = x2.shape[0]
    tm = PROJ_TM

    def body(x_ref, nw_ref, w_ref, cos_ref, sin_ref, t4_ref, t16_ref, p0_ref, p1_ref, qkv_ref, qkv4_ref, qkv16_ref, z_ref,
             u_ref):
        x = x_ref[...]
        r = lax.rsqrt(jnp.mean(x * x, axis=-1, keepdims=True) + EPS)
        u = x * r * nw_ref[...]
        ub = u.astype(BF16)
        u_ref[...] = ub
        p0_ref[...] = _dot(ub, w_ref[0])
        p1_ref[...] = _dot(ub, w_ref[1])
        y2 = _dot(ub, w_ref[2])
        cosf = jnp.tile(cos_ref[...], (1, 8))
        sinf = jnp.tile(sin_ref[...], (1, 8))
        y3 = _dot(ub, w_ref[3])
        z_ref[...] = y3[:, AT_W:]
        qkv = jnp.concatenate([y2 * cosf + _rope_rot(y2) * sinf, y3[:, :AT_W]], axis=1).astype(BF16)
        qkv_ref[...] = qkv
        _split_bf16(qkv, t4_ref, qkv4_ref, 4)
        _split_bf16(qkv, t16_ref, qkv16_ref, 16)

    row = lambda w: pl.BlockSpec((tm, w), lambda s: (s, 0))
    qkv_w = 3 * AT_W
    perm = _full((PERM_ROWS, PERM_ROWS))
    return pl.pallas_call(
        body, name="inproj", grid=(S // tm,),
        in_specs=[row(D_MODEL), _full((1, D_MODEL)), _full((4, D_MODEL, 1024)), row(128), row(128), perm, perm],
        out_specs=[row(1024), row(1024), row(qkv_w), _dil_spec(4, tm, qkv_w), _dil_spec(16, tm, qkv_w), row(AT_W),
                   row(D_MODEL)],
        out_shape=[jax.ShapeDtypeStruct((S, 1024), F32), jax.ShapeDtypeStruct((S, 1024), F32),
                   jax.ShapeDtypeStruct((S, qkv_w), BF16), jax.ShapeDtypeStruct((4, S // 4, qkv_w), BF16),
                   jax.ShapeDtypeStruct((16, S // 16, qkv_w), BF16), jax.ShapeDtypeStruct((S, AT_W), F32),
                   jax.ShapeDtypeStruct((S, D_MODEL), BF16)],
        compiler_params=_params(("parallel",)),
    )(x2, norm_w, w_in_full, cos2, sin2, _perm_matrix(4, True), _perm_matrix(16, True))


HG_HPS = 4
HG_CPS = 4
N_LEV = len(LEVELS)


def _hg_const_arrays():
    r = np.arange(CHUNK)[:, None]
    c = np.arange(CHUNK)[None, :]
    tris = np.stack([r >= c, r <= c])
    lm = [((r // (2 * m)) == (c // (2 * m))) & (r % (2 * m) >= m) & (c % (2 * m) < m) for m in LEVELS]
    dm = [(c == r - dl) & (r % DIAG >= dl) for dl in range(DIAG)]
    masks = np.stack(lm + dm)
    return jnp.asarray(tris, BF16), jnp.asarray(masks, F32)


def _split2(a):
    hi = a.astype(BF16)
    return hi, (a - hi.astype(F32)).astype(BF16)


def _dot3(a, b, dot=_dot):
    ah, al = _split2(a)
    bh, bl = _split2(b)
    n = b.shape[1]
    p = dot(ah, jnp.concatenate([bh, bl], axis=1))
    return (p[:, :n] + p[:, n:]) + dot(al, bh)


def _split3(a):
    a1 = a.astype(BF16)
    r1 = a - a1.astype(F32)
    a2 = r1.astype(BF16)
    return a1, a2, (r1 - a2.astype(F32)).astype(BF16)


def _tri_dot(tri, a):
    n = a.shape[1]
    p = _dot(tri, jnp.concatenate(_split3(a), axis=1))
    return (p[:, :n] + p[:, n:2 * n]) + p[:, 2 * n:]


def _dot_sel(a, sel):
    a1, a2, a3 = _split3(a)
    return (_dot(a1, sel) + _dot(a2, sel)) + _dot(a3, sel)


def _rowsum(t):
    return _dot(t.astype(BF16), jnp.ones((t.shape[1], t.shape[1]), BF16))


def _level_refs(b):
    refs = []
    pos = _iota(b.shape, 0)
    for m in LEVELS:
        if 2 * m >= SUBLANES:
            parts = [jnp.broadcast_to(b[r0 + m - 1: r0 + m, :], (2 * m, b.shape[1])) for r0 in range(0, CHUNK, 2 * m)]
            refs.append(parts[0] if len(parts) == 1 else jnp.concatenate(parts, axis=0))
        else:
            p = pos & (2 * m - 1)
            ref = b
            for off in range(-(m - 1), m + 1):
                if off != 0:
                    ref = jnp.where(p == m - 1 + off, pltpu.roll(b, off % CHUNK, 0), ref)
            refs.append(ref)
    return refs


def _hg_lb(lbl_ref):
    l0 = lbl_ref[0:1, :]
    l1 = lbl_ref[1:2, :]
    mx = jnp.maximum(l0, l1)
    e0 = jnp.exp(l0 - mx)
    e1 = jnp.exp(l1 - mx)
    p0 = e0 / (e0 + e1)
    lb = jnp.clip(p0, 1e-6, 1.0 - 1e-6)
    inside = (p0 >= 1e-6) & (p0 <= 1.0 - 1e-6)
    dlb_dl0 = jnp.where(inside, p0 * (e1 / (e0 + e1)), 0.0)
    return lb, dlb_dl0


def _hg_gates(qr, fl, lb):
    sig = _sigmoid(fl)
    f = lb + (1.0 - lb) * sig
    g = jnp.log(f)
    k = (1.0 - lb) * (1.0 - sig)
    sq = _sigmoid(qr)
    q = qr * sq
    return sig, f, g, k, sq, q


def _neg_abs(v):
    bits = lax.bitcast_convert_type(v, jnp.uint32) | jnp.uint32(0x80000000)
    return lax.bitcast_convert_type(bits, F32)


def _hg_levels(q, k, b, mk_ref):
    refs = _level_refs(b)
    a = jnp.zeros((CHUNK, CHUNK), F32)
    es, qts, kts = [], [], []
    for i in range(N_LEV):
        e = jnp.exp(_neg_abs(b - refs[i]))
        qt = (q * e).astype(BF16)
        kt = (k * e).astype(BF16)
        a = a + _dot_nt(qt, kt) * mk_ref[i]
        es.append(e); qts.append(qt); kts.append(kt)
    return a, es, qts, kts


def _hg_specs(nc, rev):
    cc = (lambda c: nc - 1 - c) if rev else (lambda c: c)
    w = HG_HEAD * HG_HPS
    blk = lambda off: pl.BlockSpec((HG_CPS * CHUNK, w), lambda h, c: (cc(c), h + off))
    lb2 = pl.BlockSpec((2, w), lambda h, c: (0, h))
    st = pl.BlockSpec((HG_CPS, HG_HPS, HG_HEAD, HG_HEAD), lambda h, c: (cc(c), h, 0, 0))
    consts = [_full((2, CHUNK, CHUNK)), _full((N_LEV + DIAG, CHUNK, CHUNK))]
    return blk, lb2, st, consts


def _hg_fwd(p0, p1, lbl):
    S = p0.shape[0]
    nc = S // (HG_CPS * CHUNK)
    ng = HG_HEADS // HG_HPS

    def body(qr_ref, fl_ref, v_ref, lbl_ref, tri_ref, mk_ref, o_ref, st_ref, state):
        c = pl.program_id(1)

        @pl.when(c == 0)
        def _():
            state[...] = jnp.zeros_like(state)

        lb_all, _ = _hg_lb(lbl_ref)
        heads = [slice(HG_HEAD * hh, HG_HEAD * (hh + 1)) for hh in range(HG_HPS)]
        for j in range(HG_CPS):
            rows = slice(CHUNK * j, CHUNK * (j + 1))
            qs, ks, bs, mats = [], [], [], []
            for sl in heads:
                _, _, g, k, _, q = _hg_gates(qr_ref[rows, sl], fl_ref[rows, sl], lb_all[:, sl])
                qs.append(q); ks.append(k); bs.append(_tri_dot(tri_ref[0], g))
            for hh in range(HG_HPS):
                a, _, _, _ = _hg_levels(qs[hh], ks[hh], bs[hh], mk_ref)
                mats.append(a + _rowsum(qs[hh] * ks[hh]) * mk_ref[N_LEV])
            for hh, sl in enumerate(heads):
                q, k, b, a, v = qs[hh], ks[hh], bs[hh], mats[hh], v_ref[rows, sl]
                b_last = b[CHUNK - 1: CHUNK, :]
                st = state[hh]
                st_ref[j, hh] = st
                o = _dot_nt((q * jnp.exp(b)).astype(BF16), st.astype(BF16)) + _dot(a.astype(BF16), v.astype(BF16))
                state[hh] = st * jnp.exp(b_last) + _dot3(v, k * jnp.exp(b_last - b), _dot_tn)
                o_ref[rows, sl] = o

    blk, lb2, st_spec, consts = _hg_specs(nc, False)
    tris, masks = _hg_const_arrays()
    return pl.pallas_call(
        body, name="hg_fwd", grid=(ng, nc),
        in_specs=[blk(0), blk(ng), blk(0), lb2] + consts,
        out_specs=[blk(0), st_spec],
        out_shape=[jax.ShapeDtypeStruct((S, HG_W), F32), jax.ShapeDtypeStruct((S // CHUNK, HG_HEADS, HG_HEAD, HG_HEAD), F32)],
        scratch_shapes=[pltpu.VMEM((HG_HPS, HG_HEAD, HG_HEAD), F32)],
        compiler_params=_params(("parallel", "arbitrary")),
    )(p0, p0, p1, lbl, tris, masks)


def _hg_bwd(p0, p1, do_hg, states, lbl):
    S = p0.shape[0]
    nc = S // (HG_CPS * CHUNK)
    ng = HG_HEADS // HG_HPS
    w = HG_HEAD * HG_HPS

    def body(qr_ref, fl_ref, v_ref, do_ref, st_ref, lbl_ref, tri_ref, mk_ref, mkb_ref,
             dqr_ref, dfl_ref, dv_ref, glbl_ref, dstate, carry, acc_lb):
        c = pl.program_id(1)

        @pl.when(c == 0)
        def _():
            dstate[...] = jnp.zeros_like(dstate)
            carry[...] = jnp.zeros_like(carry)
            acc_lb[...] = jnp.zeros_like(acc_lb)

        lb_all, dlb_dl0 = _hg_lb(lbl_ref)
        heads = [slice(HG_HEAD * hh, HG_HEAD * (hh + 1)) for hh in range(HG_HPS)]
        diag_mask = mk_ref[N_LEV]
        for j in reversed(range(HG_CPS)):
            rows = slice(CHUNK * j, CHUNK * (j + 1))
            gates, bs, dos = [], [], []
            for sl in heads:
                gt = _hg_gates(qr_ref[rows, sl], fl_ref[rows, sl], lb_all[:, sl])
                gates.append(gt)
                bs.append(_tri_dot(tri_ref[0], gt[2]))
            for sl in heads:
                dos.append(do_ref[rows, sl])
            inter = []
            for hh, sl in enumerate(heads):
                _, _, _, k, _, q = gates[hh]
                b, do, v = bs[hh], dos[hh], v_ref[rows, sl]
                b_last = b[CHUNK - 1: CHUNK, :]
                eb = jnp.exp(b)
                edec = jnp.exp(b_last - b)
                dst = dstate[hh]
                dq = _dot3(do, st_ref[j, hh]) * eb
                dk = _dot3(v, dst) * edec
                da = _dot_nt(do.astype(BF16), v.astype(BF16))
                dv_state = _dot_nt((k * edec).astype(BF16), dst.astype(BF16))
                dstate[hh] = dst * jnp.exp(b_last) + _dot3(do, q * eb, _dot_tn)
                inter.append((dq, dk, da, dv_state))
            for hh, sl in enumerate(heads):
                sig, f, _, k, sq, q = gates[hh]
                dq, dk, da, dv_state = inter[hh]
                b = bs[hh]
                db = q * dq - k * dk
                dab = da.astype(BF16)
                refs = _level_refs(b)
                a = _rowsum(q * k) * diag_mask
                for i in range(N_LEV):
                    e = jnp.exp(_neg_abs(b - refs[i]))
                    qt = (q * e).astype(BF16)
                    kt = (k * e).astype(BF16)
                    a = a + _dot_nt(qt, kt) * mk_ref[i]
                    dam = dab * mkb_ref[i]
                    gq = _dot(dam, kt)
                    gk = _dot_tn(dam, qt)
                    dq = dq + e * gq
                    dk = dk + e * gk
                    db = db + (qt.astype(F32) * gq - kt.astype(F32) * gk)
                dc = _rowsum(da * diag_mask)
                dq = dq + dc * k
                dk = dk + dc * q
                dv_ref[rows, sl] = (_dot_tn(a.astype(BF16), dos[hh].astype(BF16)) + dv_state).astype(BF16)
                dg = _tri_dot(tri_ref[1], db) + carry[0:1, sl]
                carry[0:1, sl] += jnp.sum(db, axis=0, keepdims=True)
                lb = lb_all[:, sl]
                qr = qr_ref[rows, sl]
                t = dg / f - dk
                dfl_ref[rows, sl] = (t * (1.0 - lb) * sig * (1.0 - sig)).astype(BF16)
                acc_lb[0:1, sl] += jnp.sum(t * (1.0 - sig), axis=0, keepdims=True)
                dqr_ref[rows, sl] = (dq * (sq * (1.0 + qr * (1.0 - sq)))).astype(BF16)

        @pl.when(c == nc - 1)
        def _():
            gl0 = acc_lb[0:1, :] * dlb_dl0
            glbl_ref[0:1, :] = gl0
            glbl_ref[1:2, :] = -gl0

    blk, lb2, st_spec, consts = _hg_specs(nc, True)
    tris, masks = _hg_const_arrays()
    act = jax.ShapeDtypeStruct((S, HG_W), BF16)
    return pl.pallas_call(
        body, name="hg_bwd", grid=(ng, nc),
        in_specs=[blk(0), blk(ng), blk(0), blk(0), st_spec, lb2] + consts + consts[1:],
        out_specs=[blk(0), blk(0), blk(0), lb2],
        out_shape=[act, act, act, jax.ShapeDtypeStruct((2, HG_W), F32)],
        scratch_shapes=[pltpu.VMEM((HG_HPS, HG_HEAD, HG_HEAD), F32), pltpu.VMEM((8, w), F32), pltpu.VMEM((8, w), F32)],
        compiler_params=_params(("parallel", "arbitrary")),
    )(p0, p0, p1, do_hg, states, lbl, tris, masks, masks.astype(BF16))


def _stat_head(lane):
    return lane >> 4


def _stat_lane(lane):
    return lane & (STAT_LANES - 1)


def _o_head(lane):
    return lane >> 6


def _att_bias():
    qi = np.arange(ATT_BLK)[:, None]
    kj = np.arange(2 * ATT_BLK)[None, :]
    band = (kj >= qi) & (kj <= qi + ATT_BLK)
    qm = np.stack([band & (kj >= ATT_BLK), band])
    cur = (kj < ATT_BLK) & (qi <= kj)
    km = np.stack([cur, cur | ((kj >= ATT_BLK) & (qi >= kj - ATT_BLK))])
    to_bias = lambda m: jnp.asarray(np.where(m, 0.0, NEG), F32)
    return to_bias(qm), to_bias(km)


def _att_fwd(qkv_d):
    d, L, _ = qkv_d.shape
    qb = ATT_QB
    rows = qb * ATT_BLK

    def body(q_ref, kp_ref, kc_ref, vp_ref, vc_ref, bias_ref, o_ref, lse_ref, s_scr, p_scr):
        first = _iota((ATT_BLK, HEAD_PAIR), 1) < AT_HEAD
        head_of_lane = _stat_head(_iota((ATT_BLK, STAT_W), 1))
        pairs = [slice(HEAD_PAIR * hp, HEAD_PAIR * (hp + 1)) for hp in range(4)]
        blk = lambda i: slice(ATT_BLK * i, ATT_BLK * (i + 1))

        def keys(i, sl, prev_ref, cur_ref):
            before = prev_ref[0, :, sl] if i == 0 else cur_ref[0, blk(i - 1), sl]
            return jnp.concatenate([before, cur_ref[0, blk(i), sl]], axis=0)

        for i in range(qb):
            for hp, sl in enumerate(pairs):
                q2 = q_ref[0, blk(i), sl] * ATT_SCALE
                zero = jnp.zeros_like(q2)
                qs = jnp.concatenate([jnp.where(first, q2, zero), jnp.where(first, zero, q2)], axis=0)
                s_scr[4 * i + hp] = _dot_nt(qs, keys(i, sl, kp_ref, kc_ref))
        stats = []
        for i in range(qb):
            bias = bias_ref[jnp.minimum(pl.program_id(1), 1)] if i == 0 else bias_ref[1]
            bias2 = jnp.concatenate([bias, bias], axis=0)
            for hp in range(4):
                s = s_scr[4 * i + hp] + bias2
                m = jnp.max(s, axis=-1, keepdims=True)
                p = jnp.exp(s - m)
                l = jnp.sum(p, axis=-1, keepdims=True)
                p_scr[4 * i + hp] = p.astype(BF16)
                stats.append((l, m + jnp.log(l)))
        for i in range(qb):
            lse_blk = jnp.zeros((ATT_BLK, STAT_W), F32)
            for hp, sl in enumerate(pairs):
                l, lse = stats[4 * i + hp]
                o = _dot(p_scr[4 * i + hp], keys(i, sl, vp_ref, vc_ref)) / l
                o_ref[0, blk(i), sl] = jnp.where(first, o[:ATT_BLK], o[ATT_BLK:]).astype(BF16)
                lse_blk = jnp.where(head_of_lane == 2 * hp, lse[:ATT_BLK],
                                    jnp.where(head_of_lane == 2 * hp + 1, lse[ATT_BLK:], lse_blk))
            lse_ref[0, blk(i), :] = lse_blk

    cur = lambda j: pl.BlockSpec((1, rows, AT_W), lambda r, n: (r, n, j))
    prev = lambda j: pl.BlockSpec((1, ATT_BLK, AT_W), lambda r, n: (r, jnp.maximum(qb * n - 1, 0), j))
    return pl.pallas_call(
        body, name=f"att_fwd_d{d}", grid=(d, L // rows),
        in_specs=[cur(0), prev(1), cur(1), prev(2), cur(2), _full((2, ATT_BLK, 2 * ATT_BLK))],
        out_specs=[pl.BlockSpec((1, rows, AT_W), lambda r, n: (r, n, 0)), pl.BlockSpec((1, rows, STAT_W), lambda r, n: (r, n, 0))],
        out_shape=[jax.ShapeDtypeStruct((d, L, AT_W), BF16), jax.ShapeDtypeStruct((d, L, STAT_W), F32)],
        scratch_shapes=[pltpu.VMEM((4 * qb, 2 * ATT_BLK, 2 * ATT_BLK), F32), pltpu.VMEM((4 * qb, 2 * ATT_BLK, 2 * ATT_BLK), BF16)],
        compiler_params=_params(("parallel", "parallel")),
    )(qkv_d, qkv_d, qkv_d, qkv_d, qkv_d, _att_bias()[0])


def _att_combine(os_d, ls_d, z_at):
    S = z_at.shape[0]
    tm = MIX_TM

    def body(oa_ref, ob4_ref, oc16_ref, la_ref, lb4_ref, lc16_ref, p4_ref, p16_ref, z_ref,
             oat_ref, lse_ref, mix_ref, lb_scr, lc_scr):
        _merge_residues(lb4_ref, lb_scr, 4)
        _merge_residues(lc16_ref, lc_scr, 16)
        ls = (la_ref[...], lb_scr[0], lc_scr[0])
        mx = jnp.maximum(jnp.maximum(ls[0], ls[1]), ls[2])
        es = [jnp.exp(l - mx) for l in ls]
        zs = es[0] + es[1] + es[2]
        lse_ref[...] = mx + jnp.log(zs)
        spread = ((_o_head(_iota((STAT_W, AT_W), 1)) == _stat_head(_iota((STAT_W, AT_W), 0)))
                  & (_stat_lane(_iota((STAT_W, AT_W), 0)) == 0)).astype(BF16)
        os_ = (oa_ref[...].astype(F32), _merge_bf16(ob4_ref, p4_ref, 4), _merge_bf16(oc16_ref, p16_ref, 16))
        o = jnp.zeros((tm, AT_W), F32)
        for e, oi in zip(es, os_):
            hi, lo = _split2(e / zs)
            o = o + (_dot(hi, spread) + _dot(lo, spread)) * oi
        oat_ref[...] = o
        z = z_ref[...]
        mixed = o * (z * _sigmoid(z))
        mix_ref[...] = mixed.astype(BF16)

    row = lambda w: pl.BlockSpec((tm, w), lambda s: (s, 0))
    return pl.pallas_call(
        body, name="att_combine", grid=(S // tm,),
        in_specs=[row(AT_W), _dil_spec(4, tm, AT_W), _dil_spec(16, tm, AT_W),
                  row(STAT_W), _dil_spec(4, tm, STAT_W), _dil_spec(16, tm, STAT_W),
                  _full((PERM_ROWS, PERM_ROWS)), _full((PERM_ROWS, PERM_ROWS)), row(AT_W)],
        out_specs=[row(AT_W), row(STAT_W), row(AT_W)],
        out_shape=[jax.ShapeDtypeStruct((S, AT_W), F32), jax.ShapeDtypeStruct((S, STAT_W), F32),
                   jax.ShapeDtypeStruct((S, AT_W), BF16)],
        scratch_shapes=[_slab_scratch(tm, STAT_W), _slab_scratch(tm, STAT_W)],
        compiler_params=_params(("parallel",)),
    )(os_d[0].reshape(S, AT_W), os_d[1], os_d[2], ls_d[0].reshape(S, STAT_W), ls_d[1], ls_d[2],
      _perm_matrix(4), _perm_matrix(16), z_at)


def _att_gate_bwd(dm_at, o_at, lse, z_at):
    S = z_at.shape[0]
    tm = MIX_TM

    def body(dm_ref, o_ref, l_ref, z_ref, t4_ref, t16_ref, do_ref, do4_ref, do16_ref, dl_ref, dl4_ref, dl16_ref, dz_ref,
             dl_scr):
        o = o_ref[...]
        z = z_ref[...]
        dm = dm_ref[...]
        sz = _sigmoid(z)
        dz_ref[...] = (dm * o * (sz * (1.0 + z * (1.0 - sz)))).astype(BF16)
        do = dm * (z * sz)
        dob = do.astype(BF16)
        do_ref[...] = dob
        gather = (_o_head(_iota((AT_W, STAT_W), 0)) == _stat_head(_iota((AT_W, STAT_W), 1))).astype(BF16)
        dl = jnp.where(_stat_lane(_iota((tm, STAT_W), 1)) < STAT_LSE_LANE, _dot_sel(do * o, gather), l_ref[...])
        dl_ref[...] = dl
        _split_bf16(dob, t4_ref, do4_ref, 4)
        _split_bf16(dob, t16_ref, do16_ref, 16)
        _to_slabs(dl, dl_scr)
        _split_residues(dl_scr, dl4_ref, 4, F32)
        _split_residues(dl_scr, dl16_ref, 16, F32)

    row = lambda w: pl.BlockSpec((tm, w), lambda s: (s, 0))
    sds = jax.ShapeDtypeStruct
    return pl.pallas_call(
        body, name="att_gate_bwd", grid=(S // tm,),
        in_specs=[row(AT_W), row(AT_W), row(STAT_W), row(AT_W)] + [_full((PERM_ROWS, PERM_ROWS))] * 2,
        out_specs=[row(AT_W), _dil_spec(4, tm, AT_W), _dil_spec(16, tm, AT_W),
                   row(STAT_W), _dil_spec(4, tm, STAT_W), _dil_spec(16, tm, STAT_W), row(AT_W)],
        out_shape=[sds((S, AT_W), BF16), sds((4, S // 4, AT_W), BF16), sds((16, S // 16, AT_W), BF16),
                   sds((S, STAT_W), F32), sds((4, S // 4, STAT_W), F32), sds((16, S // 16, STAT_W), F32),
                   sds((S, AT_W), BF16)],
        scratch_shapes=[_slab_scratch(tm, STAT_W)],
        compiler_params=_params(("parallel",)),
    )(dm_at, o_at, lse, z_at, _perm_matrix(4, True), _perm_matrix(16, True))


def _att_bwd(qkv_d, do_d, dl_d):
    d, L, _ = qkv_d.shape
    nb = L // ATT_BLK
    qb = ATT_QB
    rows = qb * ATT_BLK
    nsteps = L // rows

    def body(qc_ref, qn_ref, kp_ref, kc_ref, vp_ref, vc_ref, ac_ref, an_ref, lc_ref, ln_ref, bq_ref, bk_ref, dqkv_ref,
             s_scr, dp_scr, st_scr, dpt_scr, ds_scr, pt_scr, dst_scr):
        n = pl.program_id(1)
        first = _iota((ATT_BLK, HEAD_PAIR), 1) < AT_HEAD
        pairs = [slice(HEAD_PAIR * hp, HEAD_PAIR * (hp + 1)) for hp in range(4)]
        blk = lambda i: slice(ATT_BLK * i, ATT_BLK * (i + 1))

        def stack(t):
            zero = jnp.zeros_like(t)
            return jnp.concatenate([jnp.where(first, t, zero), jnp.where(first, zero, t)], axis=0)

        def unstack(t2):
            return jnp.where(first, t2[:ATT_BLK], t2[ATT_BLK:])

        def with_prev(i, sl, prev_ref, cur_ref):
            before = prev_ref[0, :, sl] if i == 0 else cur_ref[0, blk(i - 1), sl]
            return jnp.concatenate([before, cur_ref[0, blk(i), sl]], axis=0)

        def with_next(i, sl, cur_ref, next_ref):
            after = next_ref[0, :, sl] if i == qb - 1 else cur_ref[0, blk(i + 1), sl]
            return jnp.concatenate([cur_ref[0, blk(i), sl], after], axis=0)

        for i in range(qb):
            for hp, sl in enumerate(pairs):
                j = 4 * i + hp
                s_scr[j] = _dot_nt(stack(qc_ref[0, blk(i), sl] * ATT_SCALE), with_prev(i, sl, kp_ref, kc_ref))
                dp_scr[j] = _dot_nt(stack(ac_ref[0, blk(i), sl]), with_prev(i, sl, vp_ref, vc_ref))
                st_scr[j] = _dot_nt(stack(kc_ref[0, blk(i), sl] * ATT_SCALE), with_next(i, sl, qc_ref, qn_ref))
                dpt_scr[j] = _dot_nt(stack(vc_ref[0, blk(i), sl]), with_next(i, sl, ac_ref, an_ref))
        for i in range(qb):
            bias = bq_ref[jnp.minimum(n, 1)] if i == 0 else bq_ref[1]
            bias_t = bk_ref[jnp.minimum(nsteps - 1 - n, 1)] if i == qb - 1 else bk_ref[1]
            bias2 = jnp.concatenate([bias, bias], axis=0)
            bias_t2 = jnp.concatenate([bias_t, bias_t], axis=0)
            dl_c = lc_ref[0, blk(i), :]
            dl_t = with_next(i, slice(None), lc_ref, ln_ref).T
            for hp in range(4):
                j = 4 * i + hp
                at = [STAT_LANES * (2 * hp), STAT_LANES * (2 * hp + 1)]
                col = lambda t, o: jnp.concatenate([t[:, a + o: a + o + 1] for a in at], axis=0)
                p = jnp.exp(s_scr[j] + bias2 - col(dl_c, STAT_LSE_LANE))
                ds_scr[j] = (p * (dp_scr[j] - col(dl_c, 0))).astype(BF16)
                row = lambda t, o: jnp.concatenate([jnp.broadcast_to(t[a + o: a + o + 1, :], (ATT_BLK, 2 * ATT_BLK)) for a in at], axis=0)
                pt = jnp.exp(st_scr[j] + bias_t2 - row(dl_t, STAT_LSE_LANE))
                pt_scr[j] = pt.astype(BF16)
                dst_scr[j] = (pt * (dpt_scr[j] - row(dl_t, 0))).astype(BF16)
        for i in range(qb):
            for hp, sl in enumerate(pairs):
                j = 4 * i + hp
                dq = unstack(_dot(ds_scr[j], with_prev(i, sl, kp_ref, kc_ref))) * ATT_SCALE
                dk = unstack(_dot(dst_scr[j], with_next(i, sl, qc_ref, qn_ref))) * ATT_SCALE
                dv = unstack(_dot(pt_scr[j], with_next(i, sl, ac_ref, an_ref)))
                dqkv_ref[0, blk(i), sl] = dq.astype(BF16)
                dqkv_ref[0, blk(i), AT_W + HEAD_PAIR * hp: AT_W + HEAD_PAIR * (hp + 1)] = dk.astype(BF16)
                dqkv_ref[0, blk(i), 2 * AT_W + HEAD_PAIR * hp: 2 * AT_W + HEAD_PAIR * (hp + 1)] = dv.astype(BF16)

    cur = lambda j: pl.BlockSpec((1, rows, AT_W), lambda r, n: (r, n, j))
    prev = lambda j: pl.BlockSpec((1, ATT_BLK, AT_W), lambda r, n: (r, jnp.maximum(qb * n - 1, 0), j))
    nxt_blk = lambda n: jnp.minimum(qb * (n + 1), nb - 1)
    sq = (4 * qb, 2 * ATT_BLK, 2 * ATT_BLK)
    return pl.pallas_call(
        body, name=f"att_bwd_d{d}", grid=(d, nsteps),
        in_specs=[cur(0), pl.BlockSpec((1, ATT_BLK, AT_W), lambda r, n: (r, nxt_blk(n), 0)), prev(1), cur(1), prev(2), cur(2),
                  pl.BlockSpec((1, rows, AT_W), lambda r, n: (r, n, 0)),
                  pl.BlockSpec((1, ATT_BLK, AT_W), lambda r, n: (r, nxt_blk(n), 0)),
                  pl.BlockSpec((1, rows, STAT_W), lambda r, n: (r, n, 0)),
                  pl.BlockSpec((1, ATT_BLK, STAT_W), lambda r, n: (r, nxt_blk(n), 0)),
                  _full((2, ATT_BLK, 2 * ATT_BLK)), _full((2, ATT_BLK, 2 * ATT_BLK))],
        out_specs=pl.BlockSpec((1, rows, 3 * AT_W), lambda r, n: (r, n, 0)),
        out_shape=jax.ShapeDtypeStruct((d, L, 3 * AT_W), BF16),
        scratch_shapes=[pltpu.VMEM(sq, F32)] * 4 + [pltpu.VMEM(sq, BF16)] * 3,
        compiler_params=_params(("parallel", "parallel")),
    )(qkv_d, qkv_d, qkv_d, qkv_d, qkv_d, qkv_d, do_d, do_d, dl_d, dl_d, *_att_bias())


def _att_bwd_combine(dqkvs, cos2, sin2):
    S = dqkvs[0].shape[1]
    tm = MIX_TM

    def body(a_ref, b4_ref, c16_ref, p4_ref, p16_ref, cos_ref, sin_ref, dq_ref, dk_ref, dv_ref):
        t = a_ref[...].astype(F32) + _merge_bf16(b4_ref, p4_ref, 4) + _merge_bf16(c16_ref, p16_ref, 16)
        dy = t[:, : 2 * AT_W]
        cosf = jnp.tile(cos_ref[...], (1, 8))
        sinf = jnp.tile(sin_ref[...], (1, 8))
        dx = dy * cosf - _rope_rot(dy) * sinf
        dq_ref[...] = dx[:, :AT_W].astype(BF16)
        dk_ref[...] = dx[:, AT_W:].astype(BF16)
        dv_ref[...] = t[:, 2 * AT_W:].astype(BF16)

    row = lambda w: pl.BlockSpec((tm, w), lambda s: (s, 0))
    act = jax.ShapeDtypeStruct((S, AT_W), BF16)
    return pl.pallas_call(
        body, name="att_bwd_combine", grid=(S // tm,),
        in_specs=[row(3 * AT_W), _dil_spec(4, tm, 3 * AT_W), _dil_spec(16, tm, 3 * AT_W),
                  _full((PERM_ROWS, PERM_ROWS)), _full((PERM_ROWS, PERM_ROWS)), row(128), row(128)],
        out_specs=[row(AT_W), row(AT_W), row(AT_W)],
        out_shape=[act, act, act],
        compiler_params=_params(("parallel",)),
    )(dqkvs[0].reshape(S, 3 * AT_W), dqkvs[1], dqkvs[2], _perm_matrix(4), _perm_matrix(16), cos2, sin2)


def _outproj(x2, tgt2, mix_at, w_out_full, fnw, o_hg, p1, hg_norm_w):
    S = x2.shape[0]
    tm = PROJ_TM
    ns = S // tm

    def body(x_ref, t_ref, ma_ref, w_ref, fw_ref, o_ref, z_ref, hgw_ref,
             dh_ref, doh_ref, dzh_ref, dma_ref, gw_ref, gfw_ref, ghgw_ref, loss_ref):
        s = pl.program_id(0)

        @pl.when(s == 0)
        def _():
            gw_ref[...] = jnp.zeros_like(gw_ref)
            gfw_ref[...] = jnp.zeros_like(gfw_ref)
            ghgw_ref[...] = jnp.zeros_like(ghgw_ref)
            loss_ref[...] = jnp.zeros_like(loss_ref)

        heads = [slice(HG_HEAD * hh, HG_HEAD * (hh + 1)) for hh in range(HG_HEADS)]
        norm = []
        for sl in heads:
            o, z = o_ref[:, sl], z_ref[:, sl]
            rs = lax.rsqrt(jnp.mean(o * o, axis=-1, keepdims=True) + EPS)
            norm.append((rs, o * rs, _sigmoid(z)))
        mh = jnp.concatenate([(oh * hgw_ref[:, sl] * (z_ref[:, sl] * sz)).astype(BF16)
                              for sl, (_, oh, sz) in zip(heads, norm)], axis=1)
        y = _dot(mh, w_ref[:HG_W, :]) + _dot(ma_ref[...], w_ref[HG_W:, :])
        h = x_ref[...] + y
        r = lax.rsqrt(jnp.mean(h * h, axis=-1, keepdims=True) + EPS)
        hn = h * r
        fw = fw_ref[...]
        err = hn * fw - t_ref[...]
        loss_ref[...] += 0.5 * jnp.sum(jnp.mean(err * err, axis=-1, keepdims=True))
        dout = err * (1.0 / D_MODEL)
        gfw_ref[...] += jnp.sum(dout * hn, axis=0, keepdims=True)
        dhn = dout * fw
        dh = r * (dhn - hn * jnp.mean(dhn * hn, axis=-1, keepdims=True))
        dh_ref[...] = dh
        dhb = dh.astype(BF16)
        dma_ref[...] = _dot_nt(dhb, w_ref[HG_W:, :])
        dmh = _dot_nt(dhb, w_ref[:HG_W, :])
        for sl, (rs, oh, sz) in zip(heads, norm):
            z, dmix, gwv = z_ref[:, sl], dmh[:, sl], hgw_ref[:, sl]
            dzh_ref[:, sl] = (dmix * (oh * gwv) * (sz * (1.0 + z * (1.0 - sz)))).astype(BF16)
            don = dmix * (z * sz)
            ghgw_ref[:, sl] += jnp.sum(don * oh, axis=0, keepdims=True)
            dy = don * gwv
            doh_ref[:, sl] = rs * (dy - oh * jnp.mean(dy * oh, axis=-1, keepdims=True))
        gw_ref[:HG_W, :] += _dot_tn(mh, dhb)
        gw_ref[HG_W:, :] += _dot_tn(ma_ref[...], dhb)

    row = lambda w: pl.BlockSpec((tm, w), lambda s: (s, 0))
    return pl.pallas_call(
        body, name="outproj", grid=(ns,),
        in_specs=[row(D_MODEL), row(D_MODEL), row(AT_W),
                  _full((D_MODEL, D_MODEL)), _full((1, D_MODEL)),
                  row(HG_W), pl.BlockSpec((tm, HG_W), lambda s: (s, 1)), _full((1, HG_W))],
        out_specs=[row(D_MODEL), row(HG_W), row(HG_W), row(AT_W), _full((D_MODEL, D_MODEL)), _full((1, D_MODEL)),
                   _full((1, HG_W)), _full((8, 128))],
        out_shape=[jax.ShapeDtypeStruct((S, D_MODEL), F32), jax.ShapeDtypeStruct((S, HG_W), F32),
                   jax.ShapeDtypeStruct((S, HG_W), BF16), jax.ShapeDtypeStruct((S, AT_W), F32),
                   jax.ShapeDtypeStruct((D_MODEL, D_MODEL), F32), jax.ShapeDtypeStruct((1, D_MODEL), F32),
                   jax.ShapeDtypeStruct((1, HG_W), F32), jax.ShapeDtypeStruct((8, 128), F32)],
        compiler_params=_params(("arbitrary",)),
    )(x2, tgt2, mix_at, w_out_full, fnw, o_hg, p1, hg_norm_w)


def _inproj_bwd_x(dps, w_in_full, x2, norm_w, dh):
    S = x2.shape[0]
    tm = PROJ_TM

    def body(d0, d1, d2, d3, d4, d5, d6, d7, w_ref, x_ref, nw_ref, dh_ref, gx_ref, gnw_ref):
        s = pl.program_id(0)

        @pl.when(s == 0)
        def _():
            gnw_ref[...] = jnp.zeros_like(gnw_ref)

        du = jnp.zeros((tm, D_MODEL), F32)
        for i, dref in enumerate((d0, d1, d2, d3, d4, d5, d6, d7)):
            j, half = divmod(i, 2)
            du = du + _dot_nt(dref[...], w_ref[j, :, 512 * half: 512 * (half + 1)])
        x = x_ref[...]
        r = lax.rsqrt(jnp.mean(x * x, axis=-1, keepdims=True) + EPS)
        xh = x * r
        gnw_ref[...] += jnp.sum(du * xh, axis=0, keepdims=True)
        dun = du * nw_ref[...]
        gx_ref[...] = dh_ref[...] + r * (dun - xh * jnp.mean(dun * xh, axis=-1, keepdims=True))

    row = lambda w: pl.BlockSpec((tm, w), lambda s: (s, 0))
    return pl.pallas_call(
        body, name="inproj_bwd_x", grid=(S // tm,),
        in_specs=[row(512)] * 8 + [_full((4, D_MODEL, 1024)), row(D_MODEL), _full((1, D_MODEL)), row(D_MODEL)],
        out_specs=[row(D_MODEL), _full((1, D_MODEL))],
        out_shape=[jax.ShapeDtypeStruct((S, D_MODEL), F32), jax.ShapeDtypeStruct((1, D_MODEL), F32)],
        compiler_params=_params(("arbitrary",)),
    )(*dps, w_in_full, x2, norm_w, dh)


def _inproj_bwd_w(u, dps):
    S = u.shape[0]
    tm = 2 * PROJ_TM

    def body(u_ref, d0, d1, d2, d3, d4, d5, d6, d7, g_ref):
        @pl.when(pl.program_id(0) == 0)
        def _():
            g_ref[...] = jnp.zeros_like(g_ref)

        ub = u_ref[...]
        for i, dref in enumerate((d0, d1, d2, d3, d4, d5, d6, d7)):
            j, half = divmod(i, 2)
            g_ref[j, :, 512 * half: 512 * (half + 1)] += _dot_tn(ub, dref[...])

    return pl.pallas_call(
        body, name="inproj_bwd_w", grid=(S // tm,),
        in_specs=[pl.BlockSpec((tm, D_MODEL), lambda s: (s, 0))] + [pl.BlockSpec((tm, 512), lambda s: (s, 0))] * 8,
        out_specs=_full((4, D_MODEL, 1024)),
        out_shape=jax.ShapeDtypeStruct((4, D_MODEL, 1024), F32),
        compiler_params=_params(("arbitrary",)),
    )(u, *dps)


def _adamw_update(gg, w_ref, m_ref, v_ref, d_ref, nm_ref, nv_ref):
    nm = ADAM_B1 * m_ref[...] + (1.0 - ADAM_B1) * gg
    nv = ADAM_B2 * v_ref[...] + (1.0 - ADAM_B2) * (gg * gg)
    m_hat = nm / (1.0 - ADAM_B1 ** ADAM_STEP)
    v_hat = nv / (1.0 - ADAM_B2 ** ADAM_STEP)
    d_ref[...] = -ADAM_LR * (m_hat / (jnp.sqrt(v_hat) + ADAM_EPS) + ADAM_WD * w_ref[...])
    nm_ref[...] = nm
    nv_ref[...] = nv


def _adamw(w, g, m, v, name):
    rows, cols = w.shape
    tr = min(rows, 256)

    def body(w_ref, g_ref, m_ref, v_ref, d_ref, nm_ref, nv_ref):
        _adamw_update(g_ref[...], w_ref, m_ref, v_ref, d_ref, nm_ref, nv_ref)

    spec = pl.BlockSpec((tr, cols), lambda i: (i, 0))
    sds = jax.ShapeDtypeStruct((rows, cols), F32)
    return pl.pallas_call(
        body, name=name, grid=(rows // tr,),
        in_specs=[spec] * 4, out_specs=[spec] * 3, out_shape=[sds] * 3,
        compiler_params=_params(("parallel",)),
    )(w, g, m, v)


def _place():
    return lax.axis_index("x"), lax.axis_index("y"), lax.axis_index("c")


def _gather_weights(w_in_s, w_out_s):
    def body(win_ref, wout_ref, fin_ref, fout_ref, send_sems, recv_sems):
        x, y, c = _place()
        me = (x, y, c)
        sib = (x, y, 1 - c)
        mine = 2 * x + y
        fin_ref[mine] = win_ref[...].astype(BF16)
        fout_ref[mine] = wout_ref[...].astype(BF16)
        chips = [(1 - x, y), (x, 1 - y), (1 - x, 1 - y)]

        def halves(chip, half):
            return (fin_ref.at[chip, pl.ds(half * 512, 512), :], fout_ref.at[chip, pl.ds(half * 128, 128), :])

        def copy(k, ref, to):
            return pltpu.make_async_remote_copy(src_ref=ref, dst_ref=ref, send_sem=send_sems.at[k],
                                                recv_sem=recv_sems.at[k], device_id=to, device_id_type=MESH)

        first, passed = [], []
        for j, (cx, cy) in enumerate(chips):
            for a, ref in enumerate(halves(mine, c)):
                first.append(copy(2 * j + a, ref, (cx, cy, c)))
        for cp in first:
            cp.start()
        for j, (cx, cy) in enumerate(chips):
            for a, ref in enumerate(halves(2 * cx + cy, c)):
                copy(2 * j + a, ref, me).wait_recv()
                fwd = copy(6 + 2 * j + a, ref, sib)
                fwd.start()
                passed.append(fwd)
        for j, (cx, cy) in enumerate(chips):
            for a, ref in enumerate(halves(2 * cx + cy, 1 - c)):
                copy(6 + 2 * j + a, ref, me).wait_recv()
        for cp in first + passed:
            cp.wait_send()

    vm = pl.BlockSpec(memory_space=pltpu.VMEM)
    return pl.pallas_call(
        body, name="gather_weights",
        in_specs=[vm, vm], out_specs=[vm, vm],
        out_shape=[jax.ShapeDtypeStruct((4, D_MODEL, 1024), BF16), jax.ShapeDtypeStruct((4, 256, D_MODEL), BF16)],
        scratch_shapes=[pltpu.SemaphoreType.DMA((12,)), pltpu.SemaphoreType.DMA((12,))],
        compiler_params=pltpu.CompilerParams(vmem_limit_bytes=VMEM_LIMIT),
    )(w_in_s, w_out_s)


def _swap_halves(g_in, g_out):
    def body(gin_ref, gout_ref, rin_ref, rout_ref, send_sems, recv_sems):
        x, y, c = _place()
        sib = (x, y, 1 - c)
        cps = [pltpu.make_async_remote_copy(src_ref=src.at[:, 1 - c], dst_ref=dst, send_sem=send_sems.at[k],
                                            recv_sem=recv_sems.at[k], device_id=sib, device_id_type=MESH)
               for k, (src, dst) in enumerate(((gin_ref, rin_ref), (gout_ref, rout_ref)))]
        for cp in cps:
            cp.start()
        for cp in cps:
            cp.wait()

    hbm = pl.BlockSpec(memory_space=pl.ANY)
    return pl.pallas_call(
        body, name="swap_halves",
        in_specs=[hbm, hbm], out_specs=[hbm, hbm],
        out_shape=[jax.ShapeDtypeStruct((4,) + g.shape[2:], F32) for g in (g_in, g_out)],
        scratch_shapes=[pltpu.SemaphoreType.DMA((2,)), pltpu.SemaphoreType.DMA((2,))],
    )(g_in, g_out)


def _add_half(g, r, cidx, name):
    n, _, rows, cols = g.shape

    def body(c_ref, g_ref, r_ref, o_ref):
        o_ref[0] = (g_ref[0, 0] + r_ref[0]).astype(BF16)

    return pl.pallas_call(
        body, name=name,
        grid_spec=pltpu.PrefetchScalarGridSpec(
            num_scalar_prefetch=1, grid=(n,),
            in_specs=[pl.BlockSpec((1, 1, rows, cols), lambda j, c_ref: (j, c_ref[0], 0, 0)),
                      pl.BlockSpec((1, rows, cols), lambda j, c_ref: (j, 0, 0))],
            out_specs=pl.BlockSpec((1, rows, cols), lambda j, c_ref: (j, 0, 0))),
        out_shape=jax.ShapeDtypeStruct((n, rows, cols), BF16),
        compiler_params=_params(("parallel",)),
    )(cidx, g, r)


def _exchange_copies(in_ref, out_ref, lin_ref, lout_ref, send_sems, recv_sems):
    x, y, c = _place()
    cps = []
    for k, (cx, cy) in enumerate([(1 - x, y), (x, 1 - y), (1 - x, 1 - y)]):
        for a, (src, dst) in enumerate(((in_ref, lin_ref), (out_ref, lout_ref))):
            cps.append(pltpu.make_async_remote_copy(
                src_ref=src.at[2 * cx + cy], dst_ref=dst.at[k], send_sem=send_sems.at[2 * k + a],
                recv_sem=recv_sems.at[2 * k + a], device_id=(cx, cy, c), device_id_type=MESH))
    return cps


def _exchange_start(cs_in, cs_out):
    def body(in_ref, out_ref, lin_ref, lout_ref, send_sems, recv_sems, in_thru, out_thru, lin_thru, lout_thru, token):
        for cp in _exchange_copies(in_ref, out_ref, lin_ref, lout_ref, send_sems, recv_sems):
            cp.start()
        token[...] = jnp.zeros_like(token)

    lands = [lax.empty((3,) + a.shape[1:], a.dtype) for a in (cs_in, cs_out)]
    bufs = [pltpu.with_memory_space_constraint(a, pltpu.HBM) for a in (cs_in, cs_out, *lands)]
    hbm = pl.BlockSpec(memory_space=pltpu.HBM)
    sem = pl.BlockSpec(memory_space=pltpu.SEMAPHORE)
    return pl.pallas_call(
        body, name="exchange_start",
        in_specs=[hbm] * 4,
        out_specs=[sem, sem, hbm, hbm, hbm, hbm, pl.BlockSpec(memory_space=pltpu.VMEM)],
        out_shape=[pltpu.SemaphoreType.DMA((6,)), pltpu.SemaphoreType.DMA((6,))]
                  + [pltpu.HBM(b.shape, b.dtype) for b in bufs] + [jax.ShapeDtypeStruct((8, 128), F32)],
        input_output_aliases={0: 2, 1: 3, 2: 4, 3: 5},
        compiler_params=pltpu.CompilerParams(has_side_effects=pltpu.SideEffectType.DATAFLOW_SIDE_EFFECTING),
    )(*bufs)


def _exchange_wait(send_sems, recv_sems, in_thru, out_thru, lin_thru, lout_thru, after):
    def body(in_ref, out_ref, lin_ref, lout_ref, send_sems, recv_sems, after_ref, in_dead, out_dead, got_in, got_out):
        for cp in _exchange_copies(in_ref, out_ref, lin_ref, lout_ref, send_sems, recv_sems):
            cp.wait_send()
            cp.wait_recv()

    hbm = pl.BlockSpec(memory_space=pltpu.HBM)
    sem = pl.BlockSpec(memory_space=pltpu.SEMAPHORE)
    bufs = (in_thru, out_thru, lin_thru, lout_thru)
    return pl.pallas_call(
        body, name="exchange_wait",
        in_specs=[hbm] * 4 + [sem, sem, pl.BlockSpec(memory_space=pl.ANY)],
        out_specs=[hbm] * 4,
        out_shape=[pltpu.HBM(b.shape, b.dtype) for b in bufs],
        input_output_aliases={0: 0, 1: 1, 2: 2, 3: 3},
        compiler_params=pltpu.CompilerParams(has_side_effects=pltpu.SideEffectType.DATAFLOW_SIDE_EFFECTING),
    )(*bufs, send_sems, recv_sems, after)


PEER_XOR = (2, 1, 3)


def _sum_chips(cs, r, chip_idx, name):
    _, rows, cols = r.shape
    tr = min(rows, 256)

    def body(m_ref, cs_ref, r_ref, o_ref):
        mine = m_ref[0]
        own = cs_ref[0].astype(F32)
        got = [r_ref[k].astype(F32) for k in range(3)]
        acc = None
        for s in range(4):
            rel = mine ^ s
            term = jnp.where(rel == 0, own, jnp.where(rel == PEER_XOR[0], got[0],
                                                      jnp.where(rel == PEER_XOR[1], got[1], got[2])))
            acc = term if acc is None else acc + term
        o_ref[...] = acc

    return pl.pallas_call(
        body, name=name,
        grid_spec=pltpu.PrefetchScalarGridSpec(
            num_scalar_prefetch=1, grid=(rows // tr,),
            in_specs=[pl.BlockSpec((1, tr, cols), lambda i, m_ref: (m_ref[0], i, 0)),
                      pl.BlockSpec((3, tr, cols), lambda i, m_ref: (0, i, 0))],
            out_specs=pl.BlockSpec((tr, cols), lambda i, m_ref: (i, 0))),
        out_shape=jax.ShapeDtypeStruct((rows, cols), F32),
        compiler_params=_params(("parallel",)),
    )(chip_idx, cs, r)


def _swap_reduced(h_in, h_out):
    def body(in_ref, out_ref, rin_ref, rout_ref, send_sems, recv_sems):
        x, y, c = _place()
        cps = [pltpu.make_async_remote_copy(src_ref=src, dst_ref=dst, send_sem=send_sems.at[k],
                                            recv_sem=recv_sems.at[k], device_id=(x, y, 1 - c), device_id_type=MESH)
               for k, (src, dst) in enumerate(((in_ref, rin_ref), (out_ref, rout_ref)))]
        for cp in cps:
            cp.start()
        for cp in cps:
            cp.wait()

    hbm = pl.BlockSpec(memory_space=pl.ANY)
    return pl.pallas_call(
        body, name="swap_reduced",
        in_specs=[hbm, hbm], out_specs=[hbm, hbm],
        out_shape=[jax.ShapeDtypeStruct(h.shape, F32) for h in (h_in, h_out)],
        scratch_shapes=[pltpu.SemaphoreType.DMA((2,)), pltpu.SemaphoreType.DMA((2,))],
    )(h_in, h_out)


def _adamw_halves(w, mine, theirs, m, v, cidx, name):
    rows, cols = w.shape
    half = rows // 2
    tr = min(half, 256)
    nbh = half // tr

    def body(c_ref, w_ref, a_ref, b_ref, m_ref, v_ref, g_ref, d_ref, nm_ref, nv_ref):
        gg = jnp.where(pl.program_id(0) // nbh == c_ref[0], a_ref[...], b_ref[...])
        g_ref[...] = gg
        _adamw_update(gg, w_ref, m_ref, v_ref, d_ref, nm_ref, nv_ref)

    spec = pl.BlockSpec((tr, cols), lambda i, c_ref: (i, 0))
    hspec = pl.BlockSpec((tr, cols), lambda i, c_ref: (i % nbh, 0))
    sds = jax.ShapeDtypeStruct((rows, cols), F32)
    return pl.pallas_call(
        body, name=name,
        grid_spec=pltpu.PrefetchScalarGridSpec(
            num_scalar_prefetch=1, grid=(rows // tr,),
            in_specs=[spec, hspec, hspec, spec, spec], out_specs=[spec] * 4),
        out_shape=[sds] * 4,
        compiler_params=_params(("parallel",)),
    )(cidx, w, mine, theirs, m, v)


def _allreduce_small(g_nw, g_fw, g_hgw, g_lbl, loss8):
    def body(nw_ref, fw_ref, hgw_ref, lbl_ref, loss_ref, out_ref, slots, send_sems, recv_sems):
        x, y, c = _place()
        me = 4 * x + 2 * y + c
        slots[me] = jnp.zeros((8, D_MODEL), F32)
        slots[me, 0:1, :] = nw_ref[...]
        slots[me, 1:2, :] = fw_ref[...]
        slots[me, 2:3, 0:HG_W] = hgw_ref[...]
        slots[me, 3:4, 0:HG_W] = lbl_ref[0:1, :]
        slots[me, 3:4, HG_W:] = lbl_ref[1:2, :]
        slots[me, 4:5, 0:128] = loss_ref[0:1, :]
        cps = []
        for k in range(1, 8):
            dx, dy, dc = (k >> 2) & 1, (k >> 1) & 1, k & 1
            to = (x ^ dx, y ^ dy, c ^ dc)
            cps.append(pltpu.make_async_remote_copy(
                src_ref=slots.at[me], dst_ref=slots.at[me], send_sem=send_sems.at[k - 1],
                recv_sem=recv_sems.at[k - 1], device_id=to, device_id_type=MESH))
        for cp in cps:
            cp.start()
        for cp in cps:
            cp.wait()
        acc = slots[0]
        for i in range(1, 8):
            acc = acc + slots[i]
        out_ref[...] = acc

    vm = pl.BlockSpec(memory_space=pltpu.VMEM)
    return pl.pallas_call(
        body, name="allreduce_small",
        in_specs=[vm] * 5, out_specs=vm,
        out_shape=jax.ShapeDtypeStruct((8, D_MODEL), F32),
        scratch_shapes=[pltpu.VMEM((8, 8, D_MODEL), F32), pltpu.SemaphoreType.DMA((7,)), pltpu.SemaphoreType.DMA((7,))],
    )(g_nw, g_fw, g_hgw, g_lbl, loss8)


def _rope_tables(S):
    inv_freq = (np.float32(1.0) / np.power(np.float32(ROPE_THETA), np.arange(ROPE_HALF, dtype=np.float32) / np.float32(ROPE_HALF))).astype(np.float32)
    ang = (np.arange(S, dtype=np.float32)[:, None] * inv_freq[None, :]).astype(np.float32)
    cos, sin = np.cos(ang).astype(np.float32), np.sin(ang).astype(np.float32)
    cos2 = np.concatenate([cos, cos, cos, cos], axis=-1)
    sin2 = np.concatenate([-sin, sin, -sin, sin], axis=-1)
    return jnp.asarray(cos2), jnp.asarray(sin2)


def _local_step(x2, tgt2, norm_w, w_in_full, lbl, hg_norm_w, w_out4, fnw):
    S = x2.shape[0]
    cos2, sin2 = _rope_tables(S)
    w_out_full = w_out4.reshape(D_MODEL, D_MODEL)
    p0, p1, qkv, qkv4, qkv16, z_at, u = _inproj(x2, norm_w, w_in_full, cos2, sin2)
    o_hg, states = _hg_fwd(p0, p1, lbl)
    qkv_ds = [qkv.reshape(1, S, 3 * AT_W), qkv4, qkv16]
    os_d, ls_d = zip(*[_att_fwd(q) for q in qkv_ds])
    o_at, lse, mix_at = _att_combine(os_d, ls_d, z_at)
    dh, do_hg, dz_hg, dm_at, g_wout, g_fw, g_hgw, loss8 = _outproj(x2, tgt2, mix_at, w_out_full, fnw, o_hg, p1, hg_norm_w)
    dqr, dfl, dv_hg, g_lbl = _hg_bwd(p0, p1, do_hg, states, lbl)
    do1, do4, do16, dl1, dl4, dl16, dz_at = _att_gate_bwd(dm_at, o_at, lse, z_at)
    do_ds = [do1.reshape(1, S, AT_W), do4, do16]
    dl_ds = [dl1.reshape(1, S, STAT_W), dl4, dl16]
    dqkvs = [_att_bwd(q, a, b) for q, a, b in zip(qkv_ds, do_ds, dl_ds)]
    dq_at, dk_at, dv_at = _att_bwd_combine(dqkvs, cos2, sin2)
    dps = [dqr, dfl, dv_hg, dz_hg, dq_at, dk_at, dv_at, dz_at]
    return loss8, dps, u, dh, g_lbl, g_hgw, g_wout, g_fw


def kernel(x, norm_w, w_in, hgrn_lb_logits, hg_norm_w, w_out, final_norm_w, loss_target, m_norm_w, m_w_in, m_hgrn_lb_logits, m_hg_norm_w, m_w_out, m_final_norm_w, v_norm_w, v_w_in, v_hgrn_lb_logits, v_hg_norm_w, v_w_out, v_final_norm_w):
    S = x.shape[1]
    w_in_full, w_out_full = _gather_weights(w_in[0], w_out[0])
    loss8, dps, u, dh, g_lbl, g_hgw, g_wout, g_fw = _local_step(
        x[0], loss_target[0], norm_w, w_in_full, hgrn_lb_logits, hg_norm_w,
        w_out_full, final_norm_w.reshape(1, D_MODEL))

    cidx = lax.axis_index("c").astype(jnp.int32).reshape(1)
    g_win = _inproj_bwd_w(u, dps)
    g_in4 = g_win.reshape(4, 2, 512, 1024)
    g_out4 = g_wout.reshape(4, 2, 128, D_MODEL)
    r_in, r_out = _swap_halves(g_in4, g_out4)
    cs_in = _add_half(g_in4, r_in, cidx, "add_half_in")
    cs_out = _add_half(g_out4, r_out, cidx, "add_half_out")
    *started, token = _exchange_start(cs_in, cs_out)
    grad_x, g_nw = _inproj_bwd_x(dps, w_in_full, x[0], norm_w + token[0:1, 0:1], dh)
    cs_in, cs_out, x_in, x_out = _exchange_wait(*started, g_nw)
    chip_idx = (2 * lax.axis_index("x") + lax.axis_index("y")).astype(jnp.int32).reshape(1)
    h_in = _sum_chips(cs_in, x_in, chip_idx, "sum_chips_in")
    h_out = _sum_chips(cs_out, x_out, chip_idx, "sum_chips_out")
    s_in, s_out = _swap_reduced(h_in, h_out)

    red = _allreduce_small(g_nw, g_fw, g_hgw, g_lbl, loss8)
    loss = red[4, 0]
    grad_norm_w = red[0:1, :]
    grad_final_norm_w = red[1, :]
    grad_hg_norm_w = red[2:3, :HG_W]
    grad_lbl = jnp.concatenate([red[3:4, :HG_W], red[3:4, HG_W:]], axis=0)

    d_nw, m_nw, v_nw = _adamw(norm_w, grad_norm_w, m_norm_w, v_norm_w, "adamw_norm_w")
    grad_w_in, d_win, m_win, v_win = _adamw_halves(w_in[0], h_in, s_in, m_w_in[0], v_w_in[0], cidx, "adamw_w_in")
    d_lbl, m_lbl, v_lbl = _adamw(hgrn_lb_logits, grad_lbl, m_hgrn_lb_logits, v_hgrn_lb_logits, "adamw_lb_logits")
    d_hgw, m_hgw, v_hgw = _adamw(hg_norm_w, grad_hg_norm_w, m_hg_norm_w, v_hg_norm_w, "adamw_hg_norm_w")
    grad_w_out, d_wout, m_wout, v_wout = _adamw_halves(w_out[0], h_out, s_out, m_w_out[0], v_w_out[0], cidx, "adamw_w_out")
    d_fw, m_fw, v_fw = _adamw(final_norm_w.reshape(1, D_MODEL), grad_final_norm_w.reshape(1, D_MODEL),
                              m_final_norm_w.reshape(1, D_MODEL), v_final_norm_w.reshape(1, D_MODEL), "adamw_final_norm_w")
    e1 = lambda a: a[None]
    flat = lambda a: a.reshape(D_MODEL)
    return (loss, grad_x[None], grad_norm_w, e1(grad_w_in), grad_lbl, grad_hg_norm_w, e1(grad_w_out), grad_final_norm_w,
            d_nw, e1(d_win), d_lbl, d_hgw, e1(d_wout), flat(d_fw),
            m_nw, e1(m_win), m_lbl, m_hgw, e1(m_wout), flat(m_fw),
            v_nw, e1(v_win), v_lbl, v_hgw, e1(v_wout), flat(v_fw))
```

```python
import jax
import jax.numpy as jnp
import numpy as np
from jax import lax
from jax.experimental import pallas as pl
from jax.experimental.pallas import tpu as pltpu

F32 = jnp.float32
BF16 = jnp.bfloat16
MESH = pl.DeviceIdType.MESH

D_MODEL = 1024
HG_HEADS = 4
HG_HEAD = 128
HG_W = HG_HEADS * HG_HEAD
AT_HEAD = 64
AT_W = 512
HEAD_PAIR = 2 * AT_HEAD
ROPE_HALF = 32
ROPE_THETA = 10000.0
EPS = 1e-6
CHUNK = 128
LEVELS = (64, 32, 16, 8, 4, 2, 1)
DIAG = 1
SUBLANES = 8
ATT_BLK = 128
ATT_QB = 4
ATT_SCALE = 0.125
STAT_W = 128
STAT_LANES = 16
STAT_LSE_LANE = 8
NEG = -1e30
VMEM_LIMIT = 56 * 1024 * 1024
MIX_TM = 1024
PROJ_TM = 512

ADAM_LR = 0.001
ADAM_B1 = 0.9
ADAM_B2 = 0.999
ADAM_EPS = 1e-08
ADAM_WD = 0.01
ADAM_STEP = 10


def _iota(shape, dim):
    return lax.broadcasted_iota(jnp.int32, shape, dim)


def _dot(a, b):
    return jnp.dot(a, b, preferred_element_type=F32)


def _dot_nt(a, b):
    return lax.dot_general(a, b, (((1,), (1,)), ((), ())), preferred_element_type=F32)


def _dot_tn(a, b):
    return lax.dot_general(a, b, (((0,), (0,)), ((), ())), preferred_element_type=F32)


def _sigmoid(v):
    return 0.5 * jnp.tanh(0.5 * v) + 0.5


def _params(sem=None, vmem=VMEM_LIMIT):
    return pltpu.CompilerParams(dimension_semantics=sem, vmem_limit_bytes=vmem)


def _full(shape):
    n = len(shape)
    return pl.BlockSpec(shape, lambda *_: (0,) * n)


def _rope_rot(y):
    n = y.shape[1]
    first = (_iota(y.shape, 1) & (2 * ROPE_HALF - 1)) < ROPE_HALF
    return jnp.where(first, pltpu.roll(y, n - ROPE_HALF, 1), pltpu.roll(y, ROPE_HALF, 1))


def _dil_spec(d, tm, width):
    return pl.BlockSpec((d, tm // d, width), lambda s: (0, s, 0))


LANES = 128


def _slab_scratch(tm, width):
    return pltpu.VMEM((width // LANES, tm, LANES), F32)


def _to_slabs(v, slabs_ref):
    for j in range(slabs_ref.shape[0]):
        slabs_ref[j] = v[:, LANES * j: LANES * (j + 1)]


def _from_slabs(slabs_ref):
    return jnp.concatenate([slabs_ref[j] for j in range(slabs_ref.shape[0])], axis=1)


def _split_residues(slabs_ref, dst_ref, d, dtype):
    nslab, tm, _ = slabs_ref.shape
    for r in range(d):
        for j in range(nslab):
            dst_ref[r, :, LANES * j: LANES * (j + 1)] = slabs_ref[j, pl.ds(r, tm // d, stride=d), :].astype(dtype)


def _merge_residues(src_ref, slabs_ref, d):
    nslab, tm, _ = slabs_ref.shape
    for r in range(d):
        for j in range(nslab):
            slabs_ref[j, pl.ds(r, tm // d, stride=d), :] = src_ref[r, :, LANES * j: LANES * (j + 1)].astype(F32)


PERM_ROWS = 256


def _perm_matrix(d, inverse=False):
    n = PERM_ROWS // d
    t = np.arange(PERM_ROWS)
    p = np.zeros((PERM_ROWS, PERM_ROWS), np.float32)
    p[t, (t % d) * n + t // d] = 1.0
    return jnp.asarray(p.T if inverse else p, BF16)


def _merge_bf16(src_ref, perm_ref, d):
    n = PERM_ROWS // d
    tm = src_ref.shape[1] * d
    outs = []
    for b in range(tm // PERM_ROWS):
        flat = jnp.concatenate([src_ref[r, n * b: n * (b + 1), :] for r in range(d)], axis=0)
        outs.append(_dot(perm_ref[...], flat))
    return jnp.concatenate(outs, axis=0)


def _split_bf16(v, perm_t_ref, dst_ref, d):
    n = PERM_ROWS // d
    for b in range(v.shape[0] // PERM_ROWS):
        flat = _dot(perm_t_ref[...], v[PERM_ROWS * b: PERM_ROWS * (b + 1), :]).astype(BF16)
        for r in range(d):
            dst_ref[r, n * b: n * (b + 1), :] = flat[n * r: n * (r + 1), :]


def _inproj(x2, norm_w, w_in_full, cos2, sin2):
    S = x2.shape[0]
    tm = PROJ_TM

    def body(x_ref, nw_ref, w_ref, cos_ref, sin_ref, t4_ref, t16_ref, p0_ref, p1_ref, qkv_ref, qkv4_ref, qkv16_ref, z_ref,
             u_ref):
        x = x_ref[...]
        r = lax.rsqrt(jnp.mean(x * x, axis=-1, keepdims=True) + EPS)
        u = x * r * nw_ref[...]
        ub = u.astype(BF16)
        u_ref[...] = ub
        p0_ref[...] = _dot(ub, w_ref[0])
        p1_ref[...] = _dot(ub, w_ref[1])
        y2 = _dot(ub, w_ref[2])
        cosf = jnp.tile(cos_ref[...], (1, 8))
        sinf = jnp.tile(sin_ref[...], (1, 8))
        y3 = _dot(ub, w_ref[3])
        z_ref[...] = y3[:, AT_W:]
        qkv = jnp.concatenate([y2 * cosf + _rope_rot(y2) * sinf, y3[:, :AT_W]], axis=1).astype(BF16)
        qkv_ref[...] = qkv
        _split_bf16(qkv, t4_ref, qkv4_ref, 4)
        _split_bf16(qkv, t16_ref, qkv16_ref, 16)

    row = lambda w: pl.BlockSpec((tm, w), lambda s: (s, 0))
    qkv_w = 3 * AT_W
    perm = _full((PERM_ROWS, PERM_ROWS))
    return pl.pallas_call(
        body, name="inproj", grid=(S // tm,),
        in_specs=[row(D_MODEL), _full((1, D_MODEL)), _full((4, D_MODEL, 1024)), row(128), row(128), perm, perm],
        out_specs=[row(1024), row(1024), row(qkv_w), _dil_spec(4, tm, qkv_w), _dil_spec(16, tm, qkv_w), row(AT_W),
                   row(D_MODEL)],
        out_shape=[jax.ShapeDtypeStruct((S, 1024), F32), jax.ShapeDtypeStruct((S, 1024), F32),
                   jax.ShapeDtypeStruct((S, qkv_w), BF16), jax.ShapeDtypeStruct((4, S // 4, qkv_w), BF16),
                   jax.ShapeDtypeStruct((16, S // 16, qkv_w), BF16), jax.ShapeDtypeStruct((S, AT_W), F32),
                   jax.ShapeDtypeStruct((S, D_MODEL), BF16)],
        compiler_params=_params(("parallel",)),
    )(x2, norm_w, w_in_full, cos2, sin2, _perm_matrix(4, True), _perm_matrix(16, True))


HG_HPS = 4
HG_CPS = 4
N_LEV = len(LEVELS)


def _hg_const_arrays():
    r = np.arange(CHUNK)[:, None]
    c = np.arange(CHUNK)[None, :]
    tris = np.stack([r >= c, r <= c])
    lm = [((r // (2 * m)) == (c // (2 * m))) & (r % (2 * m) >= m) & (c % (2 * m) < m) for m in LEVELS]
    dm = [(c == r - dl) & (r % DIAG >= dl) for dl in range(DIAG)]
    masks = np.stack(lm + dm)
    return jnp.asarray(tris, BF16), jnp.asarray(masks, F32)


def _split2(a):
    hi = a.astype(BF16)
    return hi, (a - hi.astype(F32)).astype(BF16)


def _dot3(a, b, dot=_dot):
    ah, al = _split2(a)
    bh, bl = _split2(b)
    n = b.shape[1]
    p = dot(ah, jnp.concatenate([bh, bl], axis=1))
    return (p[:, :n] + p[:, n:]) + dot(al, bh)


def _split3(a):
    a1 = a.astype(BF16)
    r1 = a - a1.astype(F32)
    a2 = r1.astype(BF16)
    return a1, a2, (r1 - a2.astype(F32)).astype(BF16)


def _tri_dot(tri, a):
    n = a.shape[1]
    p = _dot(tri, jnp.concatenate(_split3(a), axis=1))
    return (p[:, :n] + p[:, n:2 * n]) + p[:, 2 * n:]


def _dot_sel(a, sel):
    a1, a2, a3 = _split3(a)
    return (_dot(a1, sel) + _dot(a2, sel)) + _dot(a3, sel)


def _rowsum(t):
    return _dot(t.astype(BF16), jnp.ones((t.shape[1], t.shape[1]), BF16))


def _level_refs(b):
    refs = []
    pos = _iota(b.shape, 0)
    for m in LEVELS:
        if 2 * m >= SUBLANES:
            parts = [jnp.broadcast_to(b[r0 + m - 1: r0 + m, :], (2 * m, b.shape[1])) for r0 in range(0, CHUNK, 2 * m)]
            refs.append(parts[0] if len(parts) == 1 else jnp.concatenate(parts, axis=0))
        else:
            p = pos & (2 * m - 1)
            ref = b
            for off in range(-(m - 1), m + 1):
                if off != 0:
                    ref = jnp.where(p == m - 1 + off, pltpu.roll(b, off % CHUNK, 0), ref)
            refs.append(ref)
    return refs


def _hg_lb(lbl_ref):
    l0 = lbl_ref[0:1, :]
    l1 = lbl_ref[1:2, :]
    mx = jnp.maximum(l0, l1)
    e0 = jnp.exp(l0 - mx)
    e1 = jnp.exp(l1 - mx)
    p0 = e0 / (e0 + e1)
    lb = jnp.clip(p0, 1e-6, 1.0 - 1e-6)
    inside = (p0 >= 1e-6) & (p0 <= 1.0 - 1e-6)
    dlb_dl0 = jnp.where(inside, p0 * (e1 / (e0 + e1)), 0.0)
    return lb, dlb_dl0


def _hg_gates(qr, fl, lb):
    sig = _sigmoid(fl)
    f = lb + (1.0 - lb) * sig
    g = jnp.log(f)
    k = (1.0 - lb) * (1.0 - sig)
    sq = _sigmoid(qr)
    q = qr * sq
    return sig, f, g, k, sq, q


def _neg_abs(v):
    bits = lax.bitcast_convert_type(v, jnp.uint32) | jnp.uint32(0x80000000)
    return lax.bitcast_convert_type(bits, F32)


def _hg_levels(q, k, b, mk_ref):
    refs = _level_refs(b)
    a = jnp.zeros((CHUNK, CHUNK), F32)
    es, qts, kts = [], [], []
    for i in range(N_LEV):
        e = jnp.exp(_neg_abs(b - refs[i]))
        qt = (q * e).astype(BF16)
        kt = (k * e).astype(BF16)
        a = a + _dot_nt(qt, kt) * mk_ref[i]
        es.append(e); qts.append(qt); kts.append(kt)
    return a, es, qts, kts


def _hg_specs(nc, rev):
    cc = (lambda c: nc - 1 - c) if rev else (lambda c: c)
    w = HG_HEAD * HG_HPS
    blk = lambda off: pl.BlockSpec((HG_CPS * CHUNK, w), lambda h, c: (cc(c), h + off))
    lb2 = pl.BlockSpec((2, w), lambda h, c: (0, h))
    st = pl.BlockSpec((HG_CPS, HG_HPS, HG_HEAD, HG_HEAD), lambda h, c: (cc(c), h, 0, 0))
    consts = [_full((2, CHUNK, CHUNK)), _full((N_LEV + DIAG, CHUNK, CHUNK))]
    return blk, lb2, st, consts


def _hg_fwd(p0, p1, lbl):
    S = p0.shape[0]
    nc = S // (HG_CPS * CHUNK)
    ng = HG_HEADS // HG_HPS

    def body(qr_ref, fl_ref, v_ref, lbl_ref, tri_ref, mk_ref, o_ref, st_ref, state):
        c = pl.program_id(1)

        @pl.when(c == 0)
        def _():
            state[...] = jnp.zeros_like(state)

        lb_all, _ = _hg_lb(lbl_ref)
        heads = [slice(HG_HEAD * hh, HG_HEAD * (hh + 1)) for hh in range(HG_HPS)]
        for j in range(HG_CPS):
            rows = slice(CHUNK * j, CHUNK * (j + 1))
            qs, ks, bs, mats = [], [], [], []
            for sl in heads:
                _, _, g, k, _, q = _hg_gates(qr_ref[rows, sl], fl_ref[rows, sl], lb_all[:, sl])
                qs.append(q); ks.append(k); bs.append(_tri_dot(tri_ref[0], g))
            for hh in range(HG_HPS):
                a, _, _, _ = _hg_levels(qs[hh], ks[hh], bs[hh], mk_ref)
                mats.append(a + _rowsum(qs[hh] * ks[hh]) * mk_ref[N_LEV])
            for hh, sl in enumerate(heads):
                q, k, b, a, v = qs[hh], ks[hh], bs[hh], mats[hh], v_ref[rows, sl]
                b_last = b[CHUNK - 1: CHUNK, :]
                st = state[hh]
                st_ref[j, hh] = st
                o = _dot_nt((q * jnp.exp(b)).astype(BF16), st.astype(BF16)) + _dot(a.astype(BF16), v.astype(BF16))
                state[hh] = st * jnp.exp(b_last) + _dot3(v, k * jnp.exp(b_last - b), _dot_tn)
                o_ref[rows, sl] = o

    blk, lb2, st_spec, consts = _hg_specs(nc, False)
    tris, masks = _hg_const_arrays()
    return pl.pallas_call(
        body, name="hg_fwd", grid=(ng, nc),
        in_specs=[blk(0), blk(ng), blk(0), lb2] + consts,
        out_specs=[blk(0), st_spec],
        out_shape=[jax.ShapeDtypeStruct((S, HG_W), F32), jax.ShapeDtypeStruct((S // CHUNK, HG_HEADS, HG_HEAD, HG_HEAD), F32)],
        scratch_shapes=[pltpu.VMEM((HG_HPS, HG_HEAD, HG_HEAD), F32)],
        compiler_params=_params(("parallel", "arbitrary")),
    )(p0, p0, p1, lbl, tris, masks)


def _hg_bwd(p0, p1, do_hg, states, lbl):
    S = p0.shape[0]
    nc = S // (HG_CPS * CHUNK)
    ng = HG_HEADS // HG_HPS
    w = HG_HEAD * HG_HPS

    def body(qr_ref, fl_ref, v_ref, do_ref, st_ref, lbl_ref, tri_ref, mk_ref, mkb_ref,
             dqr_ref, dfl_ref, dv_ref, glbl_ref, dstate, carry, acc_lb):
        c = pl.program_id(1)

        @pl.when(c == 0)
        def _():
            dstate[...] = jnp.zeros_like(dstate)
            carry[...] = jnp.zeros_like(carry)
            acc_lb[...] = jnp.zeros_like(acc_lb)

        lb_all, dlb_dl0 = _hg_lb(lbl_ref)
        heads = [slice(HG_HEAD * hh, HG_HEAD * (hh + 1)) for hh in range(HG_HPS)]
        diag_mask = mk_ref[N_LEV]
        for j in reversed(range(HG_CPS)):
            rows = slice(CHUNK * j, CHUNK * (j + 1))
            gates, bs, dos = [], [], []
            for sl in heads:
                gt = _hg_gates(qr_ref[rows, sl], fl_ref[rows, sl], lb_all[:, sl])
                gates.append(gt)
                bs.append(_tri_dot(tri_ref[0], gt[2]))
            for sl in heads:
                dos.append(do_ref[rows, sl])
            inter = []
            for hh, sl in enumerate(heads):
                _, _, _, k, _, q = gates[hh]
                b, do, v = bs[hh], dos[hh], v_ref[rows, sl]
                b_last = b[CHUNK - 1: CHUNK, :]
                eb = jnp.exp(b)
                edec = jnp.exp(b_last - b)
                dst = dstate[hh]
                dq = _dot3(do, st_ref[j, hh]) * eb
                dk = _dot3(v, dst) * edec
                da = _dot_nt(do.astype(BF16), v.astype(BF16))
                dv_state = _dot_nt((k * edec).astype(BF16), dst.astype(BF16))
                dstate[hh] = dst * jnp.exp(b_last) + _dot3(do, q * eb, _dot_tn)
                inter.append((dq, dk, da, dv_state))
            for hh, sl in enumerate(heads):
                sig, f, _, k, sq, q = gates[hh]
                dq, dk, da, dv_state = inter[hh]
                b = bs[hh]
                db = q * dq - k * dk
                dab = da.astype(BF16)
                refs = _level_refs(b)
                a = _rowsum(q * k) * diag_mask
                for i in range(N_LEV):
                    e = jnp.exp(_neg_abs(b - refs[i]))
                    qt = (q * e).astype(BF16)
                    kt = (k * e).astype(BF16)
                    a = a + _dot_nt(qt, kt) * mk_ref[i]
                    dam = dab * mkb_ref[i]
                    gq = _dot(dam, kt)
                    gk = _dot_tn(dam, qt)
                    dq = dq + e * gq
                    dk = dk + e * gk
                    db = db + (qt.astype(F32) * gq - kt.astype(F32) * gk)
                dc = _rowsum(da * diag_mask)
                dq = dq + dc * k
                dk = dk + dc * q
                dv_ref[rows, sl] = (_dot_tn(a.astype(BF16), dos[hh].astype(BF16)) + dv_state).astype(BF16)
                dg = _tri_dot(tri_ref[1], db) + carry[0:1, sl]
                carry[0:1, sl] += jnp.sum(db, axis=0, keepdims=True)
                lb = lb_all[:, sl]
                qr = qr_ref[rows, sl]
                t = dg / f - dk
                dfl_ref[rows, sl] = (t * (1.0 - lb) * sig * (1.0 - sig)).astype(BF16)
                acc_lb[0:1, sl] += jnp.sum(t * (1.0 - sig), axis=0, keepdims=True)
                dqr_ref[rows, sl] = (dq * (sq * (1.0 + qr * (1.0 - sq)))).astype(BF16)

        @pl.when(c == nc - 1)
        def _():
            gl0 = acc_lb[0:1, :] * dlb_dl0
            glbl_ref[0:1, :] = gl0
            glbl_ref[1:2, :] = -gl0

    blk, lb2, st_spec, consts = _hg_specs(nc, True)
    tris, masks = _hg_const_arrays()
    act = jax.ShapeDtypeStruct((S, HG_W), BF16)
    return pl.pallas_call(
        body, name="hg_bwd", grid=(ng, nc),
        in_specs=[blk(0), blk(ng), blk(0), blk(0), st_spec, lb2] + consts + consts[1:],
        out_specs=[blk(0), blk(0), blk(0), lb2],
        out_shape=[act, act, act, jax.ShapeDtypeStruct((2, HG_W), F32)],
        scratch_shapes=[pltpu.VMEM((HG_HPS, HG_HEAD, HG_HEAD), F32), pltpu.VMEM((8, w), F32), pltpu.VMEM((8, w), F32)],
        compiler_params=_params(("parallel", "arbitrary")),
    )(p0, p0, p1, do_hg, states, lbl, tris, masks, masks.astype(BF16))


def _stat_head(lane):
    return lane >> 4


def _stat_lane(lane):
    return lane & (STAT_LANES - 1)


def _o_head(lane):
    return lane >> 6


def _att_bias():
    qi = np.arange(ATT_BLK)[:, None]
    kj = np.arange(2 * ATT_BLK)[None, :]
    band = (kj >= qi) & (kj <= qi + ATT_BLK)
    qm = np.stack([band & (kj >= ATT_BLK), band])
    cur = (kj < ATT_BLK) & (qi <= kj)
    km = np.stack([cur, cur | ((kj >= ATT_BLK) & (qi >= kj - ATT_BLK))])
    to_bias = lambda m: jnp.asarray(np.where(m, 0.0, NEG), F32)
    return to_bias(qm), to_bias(km)


def _att_fwd(qkv_d):
    d, L, _ = qkv_d.shape
    qb = ATT_QB
    rows = qb * ATT_BLK

    def body(q_ref, kp_ref, kc_ref, vp_ref, vc_ref, bias_ref, o_ref, lse_ref, s_scr, p_scr):
        first = _iota((ATT_BLK, HEAD_PAIR), 1) < AT_HEAD
        head_of_lane = _stat_head(_iota((ATT_BLK, STAT_W), 1))
        pairs = [slice(HEAD_PAIR * hp, HEAD_PAIR * (hp + 1)) for hp in range(4)]
        blk = lambda i: slice(ATT_BLK * i, ATT_BLK * (i + 1))

        def keys(i, sl, prev_ref, cur_ref):
            before = prev_ref[0, :, sl] if i == 0 else cur_ref[0, blk(i - 1), sl]
            return jnp.concatenate([before, cur_ref[0, blk(i), sl]], axis=0)

        for i in range(qb):
            for hp, sl in enumerate(pairs):
                q2 = q_ref[0, blk(i), sl] * ATT_SCALE
                zero = jnp.zeros_like(q2)
                qs = jnp.concatenate([jnp.where(first, q2, zero), jnp.where(first, zero, q2)], axis=0)
                s_scr[4 * i + hp] = _dot_nt(qs, keys(i, sl, kp_ref, kc_ref))
        stats = []
        for i in range(qb):
            bias = bias_ref[jnp.minimum(pl.program_id(1), 1)] if i == 0 else bias_ref[1]
            bias2 = jnp.concatenate([bias, bias], axis=0)
            for hp in range(4):
                s = s_scr[4 * i + hp] + bias2
                m = jnp.max(s, axis=-1, keepdims=True)
                p = jnp.exp(s - m)
                l = jnp.sum(p, axis=-1, keepdims=True)
                p_scr[4 * i + hp] = p.astype(BF16)
                stats.append((l, m + jnp.log(l)))
        for i in range(qb):
            lse_blk = jnp.zeros((ATT_BLK, STAT_W), F32)
            for hp, sl in enumerate(pairs):
                l, lse = stats[4 * i + hp]
                o = _dot(p_scr[4 * i + hp], keys(i, sl, vp_ref, vc_ref)) / l
                o_ref[0, blk(i), sl] = jnp.where(first, o[:ATT_BLK], o[ATT_BLK:]).astype(BF16)
                lse_blk = jnp.where(head_of_lane == 2 * hp, lse[:ATT_BLK],
                                    jnp.where(head_of_lane == 2 * hp + 1, lse[ATT_BLK:], lse_blk))
            lse_ref[0, blk(i), :] = lse_blk

    cur = lambda j: pl.BlockSpec((1, rows, AT_W), lambda r, n: (r, n, j))
    prev = lambda j: pl.BlockSpec((1, ATT_BLK, AT_W), lambda r, n: (r, jnp.maximum(qb * n - 1, 0), j))
    return pl.pallas_call(
        body, name=f"att_fwd_d{d}", grid=(d, L // rows),
        in_specs=[cur(0), prev(1), cur(1), prev(2), cur(2), _full((2, ATT_BLK, 2 * ATT_BLK))],
        out_specs=[pl.BlockSpec((1, rows, AT_W), lambda r, n: (r, n, 0)), pl.BlockSpec((1, rows, STAT_W), lambda r, n: (r, n, 0))],
        out_shape=[jax.ShapeDtypeStruct((d, L, AT_W), BF16), jax.ShapeDtypeStruct((d, L, STAT_W), F32)],
        scratch_shapes=[pltpu.VMEM((4 * qb, 2 * ATT_BLK, 2 * ATT_BLK), F32), pltpu.VMEM((4 * qb, 2 * ATT_BLK, 2 * ATT_BLK), BF16)],
        compiler_params=_params(("parallel", "parallel")),
    )(qkv_d, qkv_d, qkv_d, qkv_d, qkv_d, _att_bias()[0])


def _att_combine(os_d, ls_d, z_at):
    S = z_at.shape[0]
    tm = MIX_TM

    def body(oa_ref, ob4_ref, oc16_ref, la_ref, lb4_ref, lc16_ref, p4_ref, p16_ref, z_ref,
             oat_ref, lse_ref, mix_ref, lb_scr, lc_scr):
        _merge_residues(lb4_ref, lb_scr, 4)
        _merge_residues(lc16_ref, lc_scr, 16)
        ls = (la_ref[...], lb_scr[0], lc_scr[0])
        mx = jnp.maximum(jnp.maximum(ls[0], ls[1]), ls[2])
        es = [jnp.exp(l - mx) for l in ls]
        zs = es[0] + es[1] + es[2]
        lse_ref[...] = mx + jnp.log(zs)
        spread = ((_o_head(_iota((STAT_W, AT_W), 1)) == _stat_head(_iota((STAT_W, AT_W), 0)))
                  & (_stat_lane(_iota((STAT_W, AT_W), 0)) == 0)).astype(BF16)
        os_ = (oa_ref[...].astype(F32), _merge_bf16(ob4_ref, p4_ref, 4), _merge_bf16(oc16_ref, p16_ref, 16))
        o = jnp.zeros((tm, AT_W), F32)
        for e, oi in zip(es, os_):
            hi, lo = _split2(e / zs)
            o = o + (_dot(hi, spread) + _dot(lo, spread)) * oi
        oat_ref[...] = o
        z = z_ref[...]
        mixed = o * (z * _sigmoid(z))
        mix_ref[...] = mixed.astype(BF16)

    row = lambda w: pl.BlockSpec((tm, w), lambda s: (s, 0))
    return pl.pallas_call(
        body, name="att_combine", grid=(S // tm,),
        in_specs=[row(AT_W), _dil_spec(4, tm, AT_W), _dil_spec(16, tm, AT_W),
                  row(STAT_W), _dil_spec(4, tm, STAT_W), _dil_spec(16, tm, STAT_W),
                  _full((PERM_ROWS, PERM_ROWS)), _full((PERM_ROWS, PERM_ROWS)), row(AT_W)],
        out_specs=[row(AT_W), row(STAT_W), row(AT_W)],
        out_shape=[jax.ShapeDtypeStruct((S, AT_W), F32), jax.ShapeDtypeStruct((S, STAT_W), F32),
                   jax.ShapeDtypeStruct((S, AT_W), BF16)],
        scratch_shapes=[_slab_scratch(tm, STAT_W), _slab_scratch(tm, STAT_W)],
        compiler_params=_params(("parallel",)),
    )(os_d[0].reshape(S, AT_W), os_d[1], os_d[2], ls_d[0].reshape(S, STAT_W), ls_d[1], ls_d[2],
      _perm_matrix(4), _perm_matrix(16), z_at)


def _att_gate_bwd(dm_at, o_at, lse, z_at):
    S = z_at.shape[0]
    tm = MIX_TM

    def body(dm_ref, o_ref, l_ref, z_ref, t4_ref, t16_ref, do_ref, do4_ref, do16_ref, dl_ref, dl4_ref, dl16_ref, dz_ref,
             dl_scr):
        o = o_ref[...]
        z = z_ref[...]
        dm = dm_ref[...]
        sz = _sigmoid(z)
        dz_ref[...] = (dm * o * (sz * (1.0 + z * (1.0 - sz)))).astype(BF16)
        do = dm * (z * sz)
        dob = do.astype(BF16)
        do_ref[...] = dob
        gather = (_o_head(_iota((AT_W, STAT_W), 0)) == _stat_head(_iota((AT_W, STAT_W), 1))).astype(BF16)
        dl = jnp.where(_stat_lane(_iota((tm, STAT_W), 1)) < STAT_LSE_LANE, _dot_sel(do * o, gather), l_ref[...])
        dl_ref[...] = dl
        _split_bf16(dob, t4_ref, do4_ref, 4)
        _split_bf16(dob, t16_ref, do16_ref, 16)
        _to_slabs(dl, dl_scr)
        _split_residues(dl_scr, dl4_ref, 4, F32)
        _split_residues(dl_scr, dl16_ref, 16, F32)

    row = lambda w: pl.BlockSpec((tm, w), lambda s: (s, 0))
    sds = jax.ShapeDtypeStruct
    return pl.pallas_call(
        body, name="att_gate_bwd", grid=(S // tm,),
        in_specs=[row(AT_W), row(AT_W), row(STAT_W), row(AT_W)] + [_full((PERM_ROWS, PERM_ROWS))] * 2,
        out_specs=[row(AT_W), _dil_spec(4, tm, AT_W), _dil_spec(16, tm, AT_W),
                   row(STAT_W), _dil_spec(4, tm, STAT_W), _dil_spec(16, tm, STAT_W), row(AT_W)],
        out_shape=[sds((S, AT_W), BF16), sds((4, S // 4, AT_W), BF16), sds((16, S // 16, AT_W), BF16),
                   sds((S, STAT_W), F32), sds((4, S // 4, STAT_W), F32), sds((16, S // 16, STAT_W), F32),
                   sds((S, AT_W), BF16)],
        scratch_shapes=[_slab_scratch(tm, STAT_W)],
        compiler_params=_params(("parallel",)),
    )(dm_at, o_at, lse, z_at, _perm_matrix(4, True), _perm_matrix(16, True))


def _att_bwd(qkv_d, do_d, dl_d):
    d, L, _ = qkv_d.shape
    nb = L // ATT_BLK
    qb = ATT_QB
    rows = qb * ATT_BLK
    nsteps = L // rows

    def body(qc_ref, qn_ref, kp_ref, kc_ref, vp_ref, vc_ref, ac_ref, an_ref, lc_ref, ln_ref, bq_ref, bk_ref, dqkv_ref,
             s_scr, dp_scr, st_scr, dpt_scr, ds_scr, pt_scr, dst_scr):
        n = pl.program_id(1)
        first = _iota((ATT_BLK, HEAD_PAIR), 1) < AT_HEAD
        pairs = [slice(HEAD_PAIR * hp, HEAD_PAIR * (hp + 1)) for hp in range(4)]
        blk = lambda i: slice(ATT_BLK * i, ATT_BLK * (i + 1))

        def stack(t):
            zero = jnp.zeros_like(t)
            return jnp.concatenate([jnp.where(first, t, zero), jnp.where(first, zero, t)], axis=0)

        def unstack(t2):
            return jnp.where(first, t2[:ATT_BLK], t2[ATT_BLK:])

        def with_prev(i, sl, prev_ref, cur_ref):
            before = prev_ref[0, :, sl] if i == 0 else cur_ref[0, blk(i - 1), sl]
            return jnp.concatenate([before, cur_ref[0, blk(i), sl]], axis=0)

        def with_next(i, sl, cur_ref, next_ref):
            after = next_ref[0, :, sl] if i == qb - 1 else cur_ref[0, blk(i + 1), sl]
            return jnp.concatenate([cur_ref[0, blk(i), sl], after], axis=0)

        for i in range(qb):
            for hp, sl in enumerate(pairs):
                j = 4 * i + hp
                s_scr[j] = _dot_nt(stack(qc_ref[0, blk(i), sl] * ATT_SCALE), with_prev(i, sl, kp_ref, kc_ref))
                dp_scr[j] = _dot_nt(stack(ac_ref[0, blk(i), sl]), with_prev(i, sl, vp_ref, vc_ref))
                st_scr[j] = _dot_nt(stack(kc_ref[0, blk(i), sl] * ATT_SCALE), with_next(i, sl, qc_ref, qn_ref))
                dpt_scr[j] = _dot_nt(stack(vc_ref[0, blk(i), sl]), with_next(i, sl, ac_ref, an_ref))
        for i in range(qb):
            bias = bq_ref[jnp.minimum(n, 1)] if i == 0 else bq_ref[1]
            bias_t = bk_ref[jnp.minimum(nsteps - 1 - n, 1)] if i == qb - 1 else bk_ref[1]
            bias2 = jnp.concatenate([bias, bias], axis=0)
            bias_t2 = jnp.concatenate([bias_t, bias_t], axis=0)
            dl_c = lc_ref[0, blk(i), :]
            dl_t = with_next(i, slice(None), lc_ref, ln_ref).T
            for hp in range(4):
                j = 4 * i + hp
                at = [STAT_LANES * (2 * hp), STAT_LANES * (2 * hp + 1)]
                col = lambda t, o: jnp.concatenate([t[:, a + o: a + o + 1] for a in at], axis=0)
                p = jnp.exp(s_scr[j] + bias2 - col(dl_c, STAT_LSE_LANE))
                ds_scr[j] = (p * (dp_scr[j] - col(dl_c, 0))).astype(BF16)
                row = lambda t, o: jnp.concatenate([jnp.broadcast_to(t[a + o: a + o + 1, :], (ATT_BLK, 2 * ATT_BLK)) for a in at], axis=0)
                pt = jnp.exp(st_scr[j] + bias_t2 - row(dl_t, STAT_LSE_LANE))
                pt_scr[j] = pt.astype(BF16)
                dst_scr[j] = (pt * (dpt_scr[j] - row(dl_t, 0))).astype(BF16)
        for i in range(qb):
            for hp, sl in enumerate(pairs):
                j = 4 * i + hp
                dq = unstack(_dot(ds_scr[j], with_prev(i, sl, kp_ref, kc_ref))) * ATT_SCALE
                dk = unstack(_dot(dst_scr[j], with_next(i, sl, qc_ref, qn_ref))) * ATT_SCALE
                dv = unstack(_dot(pt_scr[j], with_next(i, sl, ac_ref, an_ref)))
                dqkv_ref[0, blk(i), sl] = dq.astype(BF16)
                dqkv_ref[0, blk(i), AT_W + HEAD_PAIR * hp: AT_W + HEAD_PAIR * (hp + 1)] = dk.astype(BF16)
                dqkv_ref[0, blk(i), 2 * AT_W + HEAD_PAIR * hp: 2 * AT_W + HEAD_PAIR * (hp + 1)] = dv.astype(BF16)

    cur = lambda j: pl.BlockSpec((1, rows, AT_W), lambda r, n: (r, n, j))
    prev = lambda j: pl.BlockSpec((1, ATT_BLK, AT_W), lambda r, n: (r, jnp.maximum(qb * n - 1, 0), j))
    nxt_blk = lambda n: jnp.minimum(qb * (n + 1), nb - 1)
    sq = (4 * qb, 2 * ATT_BLK, 2 * ATT_BLK)
    return pl.pallas_call(
        body, name=f"att_bwd_d{d}", grid=(d, nsteps),
        in_specs=[cur(0), pl.BlockSpec((1, ATT_BLK, AT_W), lambda r, n: (r, nxt_blk(n), 0)), prev(1), cur(1), prev(2), cur(2),
                  pl.BlockSpec((1, rows, AT_W), lambda r, n: (r, n, 0)),
                  pl.BlockSpec((1, ATT_BLK, AT_W), lambda r, n: (r, nxt_blk(n), 0)),
                  pl.BlockSpec((1, rows, STAT_W), lambda r, n: (r, n, 0)),
                  pl.BlockSpec((1, ATT_BLK, STAT_W), lambda r, n: (r, nxt_blk(n), 0)),
                  _full((2, ATT_BLK, 2 * ATT_BLK)), _full((2, ATT_BLK, 2 * ATT_BLK))],
        out_specs=pl.BlockSpec((1, rows, 3 * AT_W), lambda r, n: (r, n, 0)),
        out_shape=jax.ShapeDtypeStruct((d, L, 3 * AT_W), BF16),
        scratch_shapes=[pltpu.VMEM(sq, F32)] * 4 + [pltpu.VMEM(sq, BF16)] * 3,
        compiler_params=_params(("parallel", "parallel")),
    )(qkv_d, qkv_d, qkv_d, qkv_d, qkv_d, qkv_d, do_d, do_d, dl_d, dl_d, *_att_bias())


def _att_bwd_combine(dqkvs, cos2, sin2):
    S = dqkvs[0].shape[1]
    tm = MIX_TM

    def body(a_ref, b4_ref, c16_ref, p4_ref, p16_ref, cos_ref, sin_ref, dq_ref, dk_ref, dv_ref):
        t = a_ref[...].astype(F32) + _merge_bf16(b4_ref, p4_ref, 4) + _merge_bf16(c16_ref, p16_ref, 16)
        dy = t[:, : 2 * AT_W]
        cosf = jnp.tile(cos_ref[...], (1, 8))
        sinf = jnp.tile(sin_ref[...], (1, 8))
        dx = dy * cosf - _rope_rot(dy) * sinf
        dq_ref[...] = dx[:, :AT_W].astype(BF16)
        dk_ref[...] = dx[:, AT_W:].astype(BF16)
        dv_ref[...] = t[:, 2 * AT_W:].astype(BF16)

    row = lambda w: pl.BlockSpec((tm, w), lambda s: (s, 0))
    act = jax.ShapeDtypeStruct((S, AT_W), BF16)
    return pl.pallas_call(
        body, name="att_bwd_combine", grid=(S // tm,),
        in_specs=[row(3 * AT_W), _dil_spec(4, tm, 3 * AT_W), _dil_spec(16, tm, 3 * AT_W),
                  _full((PERM_ROWS, PERM_ROWS)), _full((PERM_ROWS, PERM_ROWS)), row(128), row(128)],
        out_specs=[row(AT_W), row(AT_W), row(AT_W)],
        out_shape=[act, act, act],
        compiler_params=_params(("parallel",)),
    )(dqkvs[0].reshape(S, 3 * AT_W), dqkvs[1], dqkvs[2], _perm_matrix(4), _perm_matrix(16), cos2, sin2)


def _outproj(x2, tgt2, mix_at, w_out_full, fnw, o_hg, p1, hg_norm_w):
    S = x2.shape[0]
    tm = PROJ_TM
    ns = S // tm

    def body(x_ref, t_ref, ma_ref, w_ref, fw_ref, o_ref, z_ref, hgw_ref,
             dh_ref, doh_ref, dzh_ref, dma_ref, gw_ref, gfw_ref, ghgw_ref, loss_ref):
        s = pl.program_id(0)

        @pl.when(s == 0)
        def _():
            gw_ref[...] = jnp.zeros_like(gw_ref)
            gfw_ref[...] = jnp.zeros_like(gfw_ref)
            ghgw_ref[...] = jnp.zeros_like(ghgw_ref)
            loss_ref[...] = jnp.zeros_like(loss_ref)

        heads = [slice(HG_HEAD * hh, HG_HEAD * (hh + 1)) for hh in range(HG_HEADS)]
        norm = []
        for sl in heads:
            o, z = o_ref[:, sl], z_ref[:, sl]
            rs = lax.rsqrt(jnp.mean(o * o, axis=-1, keepdims=True) + EPS)
            norm.append((rs, o * rs, _sigmoid(z)))
        mh = jnp.concatenate([(oh * hgw_ref[:, sl] * (z_ref[:, sl] * sz)).astype(BF16)
                              for sl, (_, oh, sz) in zip(heads, norm)], axis=1)
        y = _dot(mh, w_ref[:HG_W, :]) + _dot(ma_ref[...], w_ref[HG_W:, :])
        h = x_ref[...] + y
        r = lax.rsqrt(jnp.mean(h * h, axis=-1, keepdims=True) + EPS)
        hn = h * r
        fw = fw_ref[...]
        err = hn * fw - t_ref[...]
        loss_ref[...] += 0.5 * jnp.sum(jnp.mean(err * err, axis=-1, keepdims=True))
        dout = err * (1.0 / D_MODEL)
        gfw_ref[...] += jnp.sum(dout * hn, axis=0, keepdims=True)
        dhn = dout * fw
        dh = r * (dhn - hn * jnp.mean(dhn * hn, axis=-1, keepdims=True))
        dh_ref[...] = dh
        dhb = dh.astype(BF16)
        dma_ref[...] = _dot_nt(dhb, w_ref[HG_W:, :])
        dmh = _dot_nt(dhb, w_ref[:HG_W, :])
        for sl, (rs, oh, sz) in zip(heads, norm):
            z, dmix, gwv = z_ref[:, sl], dmh[:, sl], hgw_ref[:, sl]
            dzh_ref[:, sl] = (dmix * (oh * gwv) * (sz * (1.0 + z * (1.0 - sz)))).astype(BF16)
            don = dmix * (z * sz)
            ghgw_ref[:, sl] += jnp.sum(don * oh, axis=0, keepdims=True)
            dy = don * gwv
            doh_ref[:, sl] = rs * (dy - oh * jnp.mean(dy * oh, axis=-1, keepdims=True))
        gw_ref[:HG_W, :] += _dot_tn(mh, dhb)
        gw_ref[HG_W:, :] += _dot_tn(ma_ref[...], dhb)

    row = lambda w: pl.BlockSpec((tm, w), lambda s: (s, 0))
    return pl.pallas_call(
        body, name="outproj", grid=(ns,),
        in_specs=[row(D_MODEL), row(D_MODEL), row(AT_W),
                  _full((D_MODEL, D_MODEL)), _full((1, D_MODEL)),
                  row(HG_W), pl.BlockSpec((tm, HG_W), lambda s: (s, 1)), _full((1, HG_W))],
        out_specs=[row(D_MODEL), row(HG_W), row(HG_W), row(AT_W), _full((D_MODEL, D_MODEL)), _full((1, D_MODEL)),
                   _full((1, HG_W)), _full((8, 128))],
        out_shape=[jax.ShapeDtypeStruct((S, D_MODEL), F32), jax.ShapeDtypeStruct((S, HG_W), F32),
                   jax.ShapeDtypeStruct((S, HG_W), BF16), jax.ShapeDtypeStruct((S, AT_W), F32),
                   jax.ShapeDtypeStruct((D_MODEL, D_MODEL), F32), jax.ShapeDtypeStruct((1, D_MODEL), F32),
                   jax.ShapeDtypeStruct((1, HG_W), F32), jax.ShapeDtypeStruct((8, 128), F32)],
        compiler_params=_params(("arbitrary",)),
    )(x2, tgt2, mix_at, w_out_full, fnw, o_hg, p1, hg_norm_w)


def _inproj_bwd_x(dps, w_in_full, x2, norm_w, dh):
    S = x2.shape[0]
    tm = PROJ_TM

    def body(d0, d1, d2, d3, d4, d5, d6, d7, w_ref, x_ref, nw_ref, dh_ref, gx_ref, gnw_ref):
        s = pl.program_id(0)

        @pl.when(s == 0)
        def _():
            gnw_ref[...] = jnp.zeros_like(gnw_ref)

        du = jnp.zeros((tm, D_MODEL), F32)
        for i, dref in enumerate((d0, d1, d2, d3, d4, d5, d6, d7)):
            j, half = divmod(i, 2)
            du = du + _dot_nt(dref[...], w_ref[j, :, 512 * half: 512 * (half + 1)])
        x = x_ref[...]
        r = lax.rsqrt(jnp.mean(x * x, axis=-1, keepdims=True) + EPS)
        xh = x * r
        gnw_ref[...] += jnp.sum(du * xh, axis=0, keepdims=True)
        dun = du * nw_ref[...]
        gx_ref[...] = dh_ref[...] + r * (dun - xh * jnp.mean(dun * xh, axis=-1, keepdims=True))

    row = lambda w: pl.BlockSpec((tm, w), lambda s: (s, 0))
    return pl.pallas_call(
        body, name="inproj_bwd_x", grid=(S // tm,),
        in_specs=[row(512)] * 8 + [_full((4, D_MODEL, 1024)), row(D_MODEL), _full((1, D_MODEL)), row(D_MODEL)],
        out_specs=[row(D_MODEL), _full((1, D_MODEL))],
        out_shape=[jax.ShapeDtypeStruct((S, D_MODEL), F32), jax.ShapeDtypeStruct((1, D_MODEL), F32)],
        compiler_params=_params(("arbitrary",)),
    )(*dps, w_in_full, x2, norm_w, dh)


def _inproj_bwd_w(u, dps):
    S = u.shape[0]
    tm = 2 * PROJ_TM

    def body(u_ref, d0, d1, d2, d3, d4, d5, d6, d7, g_ref):
        @pl.when(pl.program_id(0) == 0)
        def _():
            g_ref[...] = jnp.zeros_like(g_ref)

        ub = u_ref[...]
        for i, dref in enumerate((d0, d1, d2, d3, d4, d5, d6, d7)):
            j, half = divmod(i, 2)
            g_ref[j, :, 512 * half: 512 * (half + 1)] += _dot_tn(ub, dref[...])

    return pl.pallas_call(
        body, name="inproj_bwd_w", grid=(S // tm,),
        in_specs=[pl.BlockSpec((tm, D_MODEL), lambda s: (s, 0))] + [pl.BlockSpec((tm, 512), lambda s: (s, 0))] * 8,
        out_specs=_full((4, D_MODEL, 1024)),
        out_shape=jax.ShapeDtypeStruct((4, D_MODEL, 1024), F32),
        compiler_params=_params(("arbitrary",)),
    )(u, *dps)


def _adamw_update(gg, w_ref, m_ref, v_ref, d_ref, nm_ref, nv_ref):
    nm = ADAM_B1 * m_ref[...] + (1.0 - ADAM_B1) * gg
    nv = ADAM_B2 * v_ref[...] + (1.0 - ADAM_B2) * (gg * gg)
    m_hat = nm / (1.0 - ADAM_B1 ** ADAM_STEP)
    v_hat = nv / (1.0 - ADAM_B2 ** ADAM_STEP)
    d_ref[...] = -ADAM_LR * (m_hat / (jnp.sqrt(v_hat) + ADAM_EPS) + ADAM_WD * w_ref[...])
    nm_ref[...] = nm
    nv_ref[...] = nv


def _adamw_small(red, ws, ms, vs):
    def body(red_ref, *refs):
        w, m, v, g, d, nm, nv = (refs[4 * i: 4 * (i + 1)] for i in range(7))
        loss_ref = refs[28]
        g[0][...] = red_ref[0:1, :]
        g[1][0:1, :] = red_ref[3:4, :HG_W]
        g[1][1:2, :] = red_ref[3:4, HG_W:]
        g[2][...] = red_ref[2:3, :HG_W]
        g[3][...] = red_ref[1:2, :]
        loss_ref[...] = red_ref[4:5, 0:1]
        for i in range(4):
            _adamw_update(g[i][...], w[i], m[i], v[i], d[i], nm[i], nv[i])

    shapes = [a.shape for a in ws]
    sds = [jax.ShapeDtypeStruct(s, F32) for s in shapes]
    out = pl.pallas_call(
        body, name="adamw_small",
        in_specs=[_full(red.shape)] + [_full(s) for s in shapes] * 3,
        out_specs=[_full(s) for s in shapes] * 4 + [_full((1, 1))],
        out_shape=sds * 4 + [jax.ShapeDtypeStruct((1, 1), F32)],
        compiler_params=_params(),
    )(red, *ws, *ms, *vs)
    return out[0:4], out[4:8], out[8:12], out[12:16], out[16]


def _place():
    return lax.axis_index("x"), lax.axis_index("y"), lax.axis_index("c")


def _gather_weights(w_in_s, w_out_s):
    def body(win_ref, wout_ref, fin_ref, fout_ref, send_sems, recv_sems):
        x, y, c = _place()
        me = (x, y, c)
        sib = (x, y, 1 - c)
        mine = 2 * x + y
        fin_ref[mine] = win_ref[...].astype(BF16)
        fout_ref[mine] = wout_ref[...].astype(BF16)
        chips = [(1 - x, y), (x, 1 - y), (1 - x, 1 - y)]

        def halves(chip, half):
            return (fin_ref.at[chip, pl.ds(half * 512, 512), :], fout_ref.at[chip, pl.ds(half * 128, 128), :])

        def copy(k, ref, to):
            return pltpu.make_async_remote_copy(src_ref=ref, dst_ref=ref, send_sem=send_sems.at[k],
                                                recv_sem=recv_sems.at[k], device_id=to, device_id_type=MESH)

        first, passed = [], []
        for j, (cx, cy) in enumerate(chips):
            for a, ref in enumerate(halves(mine, c)):
                first.append(copy(2 * j + a, ref, (cx, cy, c)))
        for cp in first:
            cp.start()
        for j, (cx, cy) in enumerate(chips):
            for a, ref in enumerate(halves(2 * cx + cy, c)):
                copy(2 * j + a, ref, me).wait_recv()
                fwd = copy(6 + 2 * j + a, ref, sib)
                fwd.start()
                passed.append(fwd)
        for j, (cx, cy) in enumerate(chips):
            for a, ref in enumerate(halves(2 * cx + cy, 1 - c)):
                copy(6 + 2 * j + a, ref, me).wait_recv()
        for cp in first + passed:
            cp.wait_send()

    vm = pl.BlockSpec(memory_space=pltpu.VMEM)
    return pl.pallas_call(
        body, name="gather_weights",
        in_specs=[vm, vm], out_specs=[vm, vm],
        out_shape=[jax.ShapeDtypeStruct((4, D_MODEL, 1024), BF16), jax.ShapeDtypeStruct((4, 256, D_MODEL), BF16)],
        scratch_shapes=[pltpu.SemaphoreType.DMA((12,)), pltpu.SemaphoreType.DMA((12,))],
        compiler_params=pltpu.CompilerParams(vmem_limit_bytes=VMEM_LIMIT),
    )(w_in_s, w_out_s)


def _swap_halves(g_in, g_out):
    def body(gin_ref, gout_ref, rin_ref, rout_ref, send_sems, recv_sems):
        x, y, c = _place()
        sib = (x, y, 1 - c)
        cps = [pltpu.make_async_remote_copy(src_ref=src.at[:, 1 - c], dst_ref=dst, send_sem=send_sems.at[k],
                                            recv_sem=recv_sems.at[k], device_id=sib, device_id_type=MESH)
               for k, (src, dst) in enumerate(((gin_ref, rin_ref), (gout_ref, rout_ref)))]
        for cp in cps:
            cp.start()
        for cp in cps:
            cp.wait()

    hbm = pl.BlockSpec(memory_space=pl.ANY)
    return pl.pallas_call(
        body, name="swap_halves",
        in_specs=[hbm, hbm], out_specs=[hbm, hbm],
        out_shape=[jax.ShapeDtypeStruct((4,) + g.shape[2:], F32) for g in (g_in, g_out)],
        scratch_shapes=[pltpu.SemaphoreType.DMA((2,)), pltpu.SemaphoreType.DMA((2,))],
    )(g_in, g_out)


def _add_half(g, r, cidx, name):
    n, _, rows, cols = g.shape

    def body(c_ref, g_ref, r_ref, o_ref):
        o_ref[0] = (g_ref[0, 0] + r_ref[0]).astype(BF16)

    return pl.pallas_call(
        body, name=name,
        grid_spec=pltpu.PrefetchScalarGridSpec(
            num_scalar_prefetch=1, grid=(n,),
            in_specs=[pl.BlockSpec((1, 1, rows, cols), lambda j, c_ref: (j, c_ref[0], 0, 0)),
                      pl.BlockSpec((1, rows, cols), lambda j, c_ref: (j, 0, 0))],
            out_specs=pl.BlockSpec((1, rows, cols), lambda j, c_ref: (j, 0, 0))),
        out_shape=jax.ShapeDtypeStruct((n, rows, cols), BF16),
        compiler_params=_params(("parallel",)),
    )(cidx, g, r)


def _exchange_copies(in_ref, out_ref, lin_ref, lout_ref, send_sems, recv_sems):
    x, y, c = _place()
    cps = []
    for k, (cx, cy) in enumerate([(1 - x, y), (x, 1 - y), (1 - x, 1 - y)]):
        for a, (src, dst) in enumerate(((in_ref, lin_ref), (out_ref, lout_ref))):
            cps.append(pltpu.make_async_remote_copy(
                src_ref=src.at[2 * cx + cy], dst_ref=dst.at[k], send_sem=send_sems.at[2 * k + a],
                recv_sem=recv_sems.at[2 * k + a], device_id=(cx, cy, c), device_id_type=MESH))
    return cps


def _exchange_start(cs_in, cs_out):
    def body(in_ref, out_ref, lin_ref, lout_ref, send_sems, recv_sems, in_thru, out_thru, lin_thru, lout_thru, token):
        for cp in _exchange_copies(in_ref, out_ref, lin_ref, lout_ref, send_sems, recv_sems):
            cp.start()
        token[...] = jnp.zeros_like(token)

    lands = [lax.empty((3,) + a.shape[1:], a.dtype) for a in (cs_in, cs_out)]
    bufs = [pltpu.with_memory_space_constraint(a, pltpu.HBM) for a in (cs_in, cs_out, *lands)]
    hbm = pl.BlockSpec(memory_space=pltpu.HBM)
    sem = pl.BlockSpec(memory_space=pltpu.SEMAPHORE)
    return pl.pallas_call(
        body, name="exchange_start",
        in_specs=[hbm] * 4,
        out_specs=[sem, sem, hbm, hbm, hbm, hbm, pl.BlockSpec(memory_space=pltpu.VMEM)],
        out_shape=[pltpu.SemaphoreType.DMA((6,)), pltpu.SemaphoreType.DMA((6,))]
                  + [pltpu.HBM(b.shape, b.dtype) for b in bufs] + [jax.ShapeDtypeStruct((8, 128), F32)],
        input_output_aliases={0: 2, 1: 3, 2: 4, 3: 5},
        compiler_params=pltpu.CompilerParams(has_side_effects=pltpu.SideEffectType.DATAFLOW_SIDE_EFFECTING),
    )(*bufs)


def _exchange_wait(send_sems, recv_sems, in_thru, out_thru, lin_thru, lout_thru, after):
    def body(in_ref, out_ref, lin_ref, lout_ref, send_sems, recv_sems, after_ref, in_dead, out_dead, got_in, got_out):
        for cp in _exchange_copies(in_ref, out_ref, lin_ref, lout_ref, send_sems, recv_sems):
            cp.wait_send()
            cp.wait_recv()

    hbm = pl.BlockSpec(memory_space=pltpu.HBM)
    sem = pl.BlockSpec(memory_space=pltpu.SEMAPHORE)
    bufs = (in_thru, out_thru, lin_thru, lout_thru)
    return pl.pallas_call(
        body, name="exchange_wait",
        in_specs=[hbm] * 4 + [sem, sem, pl.BlockSpec(memory_space=pl.ANY)],
        out_specs=[hbm] * 4,
        out_shape=[pltpu.HBM(b.shape, b.dtype) for b in bufs],
        input_output_aliases={0: 0, 1: 1, 2: 2, 3: 3},
        compiler_params=pltpu.CompilerParams(has_side_effects=pltpu.SideEffectType.DATAFLOW_SIDE_EFFECTING),
    )(*bufs, send_sems, recv_sems, after)


PEER_XOR = (2, 1, 3)


def _sum_chips(cs, r, chip_idx, name):
    _, rows, cols = r.shape
    tr = min(rows, 256)

    def body(m_ref, cs_ref, r_ref, o_ref):
        mine = m_ref[0]
        own = cs_ref[0].astype(F32)
        got = [r_ref[k].astype(F32) for k in range(3)]
        acc = None
        for s in range(4):
            rel = mine ^ s
            term = jnp.where(rel == 0, own, jnp.where(rel == PEER_XOR[0], got[0],
                                                      jnp.where(rel == PEER_XOR[1], got[1], got[2])))
            acc = term if acc is None else acc + term
        o_ref[...] = acc

    return pl.pallas_call(
        body, name=name,
        grid_spec=pltpu.PrefetchScalarGridSpec(
            num_scalar_prefetch=1, grid=(rows // tr,),
            in_specs=[pl.BlockSpec((1, tr, cols), lambda i, m_ref: (m_ref[0], i, 0)),
                      pl.BlockSpec((3, tr, cols), lambda i, m_ref: (0, i, 0))],
            out_specs=pl.BlockSpec((tr, cols), lambda i, m_ref: (i, 0))),
        out_shape=jax.ShapeDtypeStruct((rows, cols), F32),
        compiler_params=_params(("parallel",)),
    )(chip_idx, cs, r)


def _swap_reduced(h_in, h_out):
    def body(in_ref, out_ref, rin_ref, rout_ref, send_sems, recv_sems):
        x, y, c = _place()
        cps = [pltpu.make_async_remote_copy(src_ref=src, dst_ref=dst, send_sem=send_sems.at[k],
                                            recv_sem=recv_sems.at[k], device_id=(x, y, 1 - c), device_id_type=MESH)
               for k, (src, dst) in enumerate(((in_ref, rin_ref), (out_ref, rout_ref)))]
        for cp in cps:
            cp.start()
        for cp in cps:
            cp.wait()

    hbm = pl.BlockSpec(memory_space=pl.ANY)
    return pl.pallas_call(
        body, name="swap_reduced",
        in_specs=[hbm, hbm], out_specs=[hbm, hbm],
        out_shape=[jax.ShapeDtypeStruct(h.shape, F32) for h in (h_in, h_out)],
        scratch_shapes=[pltpu.SemaphoreType.DMA((2,)), pltpu.SemaphoreType.DMA((2,))],
    )(h_in, h_out)


def _adamw_halves(w, mine, theirs, m, v, cidx, name):
    rows, cols = w.shape
    half = rows // 2
    tr = min(half, 256)
    nbh = half // tr

    def body(c_ref, w_ref, a_ref, b_ref, m_ref, v_ref, g_ref, d_ref, nm_ref, nv_ref):
        gg = jnp.where(pl.program_id(0) // nbh == c_ref[0], a_ref[...], b_ref[...])
        g_ref[...] = gg
        _adamw_update(gg, w_ref, m_ref, v_ref, d_ref, nm_ref, nv_ref)

    spec = pl.BlockSpec((tr, cols), lambda i, c_ref: (i, 0))
    hspec = pl.BlockSpec((tr, cols), lambda i, c_ref: (i % nbh, 0))
    sds = jax.ShapeDtypeStruct((rows, cols), F32)
    return pl.pallas_call(
        body, name=name,
        grid_spec=pltpu.PrefetchScalarGridSpec(
            num_scalar_prefetch=1, grid=(rows // tr,),
            in_specs=[spec, hspec, hspec, spec, spec], out_specs=[spec] * 4),
        out_shape=[sds] * 4,
        compiler_params=_params(("parallel",)),
    )(cidx, w, mine, theirs, m, v)


def _allreduce_small(g_nw, g_fw, g_hgw, g_lbl, loss8):
    def body(nw_ref, fw_ref, hgw_ref, lbl_ref, loss_ref, out_ref, slots, send_sems, recv_sems):
        x, y, c = _place()
        me = 4 * x + 2 * y + c
        slots[me] = jnp.zeros((8, D_MODEL), F32)
        slots[me, 0:1, :] = nw_ref[...]
        slots[me, 1:2, :] = fw_ref[...]
        slots[me, 2:3, 0:HG_W] = hgw_ref[...]
        slots[me, 3:4, 0:HG_W] = lbl_ref[0:1, :]
        slots[me, 3:4, HG_W:] = lbl_ref[1:2, :]
        slots[me, 4:5, 0:128] = loss_ref[0:1, :]
        cps = []
        for k in range(1, 8):
            dx, dy, dc = (k >> 2) & 1, (k >> 1) & 1, k & 1
            to = (x ^ dx, y ^ dy, c ^ dc)
            cps.append(pltpu.make_async_remote_copy(
                src_ref=slots.at[me], dst_ref=slots.at[me], send_sem=send_sems.at[k - 1],
                recv_sem=recv_sems.at[k - 1], device_id=to, device_id_type=MESH))
        for cp in cps:
            cp.start()
        for cp in cps:
            cp.wait()
        acc = slots[0]
        for i in range(1, 8):
            acc = acc + slots[i]
        out_ref[...] = acc

    vm = pl.BlockSpec(memory_space=pltpu.VMEM)
    return pl.pallas_call(
        body, name="allreduce_small",
        in_specs=[vm] * 5, out_specs=vm,
        out_shape=jax.ShapeDtypeStruct((8, D_MODEL), F32),
        scratch_shapes=[pltpu.VMEM((8, 8, D_MODEL), F32), pltpu.SemaphoreType.DMA((7,)), pltpu.SemaphoreType.DMA((7,))],
    )(g_nw, g_fw, g_hgw, g_lbl, loss8)


def _rope_tables(S):
    inv_freq = (np.float32(1.0) / np.power(np.float32(ROPE_THETA), np.arange(ROPE_HALF, dtype=np.float32) / np.float32(ROPE_HALF))).astype(np.float32)
    ang = (np.arange(S, dtype=np.float32)[:, None] * inv_freq[None, :]).astype(np.float32)
    cos, sin = np.cos(ang).astype(np.float32), np.sin(ang).astype(np.float32)
    cos2 = np.concatenate([cos, cos, cos, cos], axis=-1)
    sin2 = np.concatenate([-sin, sin, -sin, sin], axis=-1)
    return jnp.asarray(cos2), jnp.asarray(sin2)


def _local_step(x2, tgt2, norm_w, w_in_full, lbl, hg_norm_w, w_out4, fnw):
    S = x2.shape[0]
    cos2, sin2 = _rope_tables(S)
    w_out_full = w_out4.reshape(D_MODEL, D_MODEL)
    p0, p1, qkv, qkv4, qkv16, z_at, u = _inproj(x2, norm_w, w_in_full, cos2, sin2)
    o_hg, states = _hg_fwd(p0, p1, lbl)
    qkv_ds = [qkv.reshape(1, S, 3 * AT_W), qkv4, qkv16]
    os_d, ls_d = zip(*[_att_fwd(q) for q in qkv_ds])
    o_at, lse, mix_at = _att_combine(os_d, ls_d, z_at)
    dh, do_hg, dz_hg, dm_at, g_wout, g_fw, g_hgw, loss8 = _outproj(x2, tgt2, mix_at, w_out_full, fnw, o_hg, p1, hg_norm_w)
    dqr, dfl, dv_hg, g_lbl = _hg_bwd(p0, p1, do_hg, states, lbl)
    do1, do4, do16, dl1, dl4, dl16, dz_at = _att_gate_bwd(dm_at, o_at, lse, z_at)
    do_ds = [do1.reshape(1, S, AT_W), do4, do16]
    dl_ds = [dl1.reshape(1, S, STAT_W), dl4, dl16]
    dqkvs = [_att_bwd(q, a, b) for q, a, b in zip(qkv_ds, do_ds, dl_ds)]
    dq_at, dk_at, dv_at = _att_bwd_combine(dqkvs, cos2, sin2)
    dps = [dqr, dfl, dv_hg, dz_hg, dq_at, dk_at, dv_at, dz_at]
    return loss8, dps, u, dh, g_lbl, g_hgw, g_wout, g_fw


def kernel(x, norm_w, w_in, hgrn_lb_logits, hg_norm_w, w_out, final_norm_w, loss_target, m_norm_w, m_w_in, m_hgrn_lb_logits, m_hg_norm_w, m_w_out, m_final_norm_w, v_norm_w, v_w_in, v_hgrn_lb_logits, v_hg_norm_w, v_w_out, v_final_norm_w):
    S = x.shape[1]
    w_in_full, w_out_full = _gather_weights(w_in[0], w_out[0])
    loss8, dps, u, dh, g_lbl, g_hgw, g_wout, g_fw = _local_step(
        x[0], loss_target[0], norm_w, w_in_full, hgrn_lb_logits, hg_norm_w,
        w_out_full, final_norm_w.reshape(1, D_MODEL))

    cidx = lax.axis_index("c").astype(jnp.int32).reshape(1)
    g_win = _inproj_bwd_w(u, dps)
    g_in4 = g_win.reshape(4, 2, 512, 1024)
    g_out4 = g_wout.reshape(4, 2, 128, D_MODEL)
    r_in, r_out = _swap_halves(g_in4, g_out4)
    cs_in = _add_half(g_in4, r_in, cidx, "add_half_in")
    cs_out = _add_half(g_out4, r_out, cidx, "add_half_out")
    *started, token = _exchange_start(cs_in, cs_out)
    grad_x, g_nw = _inproj_bwd_x(dps, w_in_full, x[0], norm_w + token[0:1, 0:1], dh)
    cs_in, cs_out, x_in, x_out = _exchange_wait(*started, g_nw)
    chip_idx = (2 * lax.axis_index("x") + lax.axis_index("y")).astype(jnp.int32).reshape(1)
    h_in = _sum_chips(cs_in, x_in, chip_idx, "sum_chips_in")
    h_out = _sum_chips(cs_out, x_out, chip_idx, "sum_chips_out")
    s_in, s_out = _swap_reduced(h_in, h_out)

    red = _allreduce_small(g_nw, g_fw, g_hgw, g_lbl, loss8)
    as_row = lambda a: a.reshape(1, D_MODEL)
    grads, deltas, new_m, new_v, loss = _adamw_small(
        red, (norm_w, hgrn_lb_logits, hg_norm_w, as_row(final_norm_w)),
        (m_norm_w, m_hgrn_lb_logits, m_hg_norm_w, as_row(m_final_norm_w)),
        (v_norm_w, v_hgrn_lb_logits, v_hg_norm_w, as_row(v_final_norm_w)))
    grad_norm_w, grad_lbl, grad_hg_norm_w, grad_final_norm_w = grads
    d_nw, d_lbl, d_hgw, d_fw = deltas
    m_nw, m_lbl, m_hgw, m_fw = new_m
    v_nw, v_lbl, v_hgw, v_fw = new_v
    grad_w_in, d_win, m_win, v_win = _adamw_halves(w_in[0], h_in, s_in, m_w_in[0], v_w_in[0], cidx, "adamw_w_in")
    grad_w_out, d_wout, m_wout, v_wout = _adamw_halves(w_out[0], h_out, s_out, m_w_out[0], v_w_out[0], cidx, "adamw_w_out")
    e1 = lambda a: a[None]
    flat = lambda a: a.reshape(D_MODEL)
    return (loss.reshape(()), grad_x[None], grad_norm_w, e1(grad_w_in), grad_lbl, grad_hg_norm_w, e1(grad_w_out),
            flat(grad_final_norm_w),
            d_nw, e1(d_win), d_lbl, d_hgw, e1(d_wout), flat(d_fw),
            m_nw, e1(m_win), m_lbl, m_hgw, e1(m_wout), flat(m_fw),
            v_nw, e1(v_win), v_lbl, v_hgw, e1(v_wout), flat(v_fw))
```

```python
import jax
import jax.numpy as jnp
import numpy as np
from jax import lax
from jax.experimental import pallas as pl
from jax.experimental.pallas import tpu as pltpu

F32 = jnp.float32
BF16 = jnp.bfloat16
MESH = pl.DeviceIdType.MESH

D_MODEL = 1024
HG_HEADS = 4
HG_HEAD = 128
HG_W = HG_HEADS * HG_HEAD
AT_HEAD = 64
AT_W = 512
HEAD_PAIR = 2 * AT_HEAD
ROPE_HALF = 32
ROPE_THETA = 10000.0
EPS = 1e-6
CHUNK = 128
LEVELS = (64, 32, 16, 8, 4, 2, 1)
DIAG = 1
SUBLANES = 8
ATT_BLK = 128
ATT_QB = 4
ATT_SCALE = 0.125
STAT_W = 128
STAT_LANES = 16
STAT_LSE_LANE = 8
NEG = -1e30
VMEM_LIMIT = 56 * 1024 * 1024
MIX_TM = 1024
PROJ_TM = 512

ADAM_LR = 0.001
ADAM_B1 = 0.9
ADAM_B2 = 0.999
ADAM_EPS = 1e-08
ADAM_WD = 0.01
ADAM_STEP = 10


def _iota(shape, dim):
    return lax.broadcasted_iota(jnp.int32, shape, dim)


def _dot(a, b):
    return jnp.dot(a, b, preferred_element_type=F32)


def _dot_nt(a, b):
    return lax.dot_general(a, b, (((1,), (1,)), ((), ())), preferred_element_type=F32)


def _dot_tn(a, b):
    return lax.dot_general(a, b, (((0,), (0,)), ((), ())), preferred_element_type=F32)


def _sigmoid(v):
    return 0.5 * jnp.tanh(0.5 * v) + 0.5


def _params(sem=None, vmem=VMEM_LIMIT):
    return pltpu.CompilerParams(dimension_semantics=sem, vmem_limit_bytes=vmem)


def _full(shape):
    n = len(shape)
    return pl.BlockSpec(shape, lambda *_: (0,) * n)


def _rope_rot(y):
    n = y.shape[1]
    first = (_iota(y.shape, 1) & (2 * ROPE_HALF - 1)) < ROPE_HALF
    return jnp.where(first, pltpu.roll(y, n - ROPE_HALF, 1), pltpu.roll(y, ROPE_HALF, 1))


def _dil_spec(d, tm, width):
    return pl.BlockSpec((d, tm // d, width), lambda s: (0, s, 0))


LANES = 128


def _slab_scratch(tm, width):
    return pltpu.VMEM((width // LANES, tm, LANES), F32)


def _to_slabs(v, slabs_ref):
    for j in range(slabs_ref.shape[0]):
        slabs_ref[j] = v[:, LANES * j: LANES * (j + 1)]


def _from_slabs(slabs_ref):
    return jnp.concatenate([slabs_ref[j] for j in range(slabs_ref.shape[0])], axis=1)


def _split_residues(slabs_ref, dst_ref, d, dtype):
    nslab, tm, _ = slabs_ref.shape
    for r in range(d):
        for j in range(nslab):
            dst_ref[r, :, LANES * j: LANES * (j + 1)] = slabs_ref[j, pl.ds(r, tm // d, stride=d), :].astype(dtype)


def _merge_residues(src_ref, slabs_ref, d):
    nslab, tm, _ = slabs_ref.shape
    for r in range(d):
        for j in range(nslab):
            slabs_ref[j, pl.ds(r, tm // d, stride=d), :] = src_ref[r, :, LANES * j: LANES * (j + 1)].astype(F32)


PERM_ROWS = 256


def _perm_matrix(d, inverse=False):
    n = PERM_ROWS // d
    t = np.arange(PERM_ROWS)
    p = np.zeros((PERM_ROWS, PERM_ROWS), np.float32)
    p[t, (t % d) * n + t // d] = 1.0
    return jnp.asarray(p.T if inverse else p, BF16)


def _merge_bf16(src_ref, perm_ref, d):
    n = PERM_ROWS // d
    tm = src_ref.shape[1] * d
    outs = []
    for b in range(tm // PERM_ROWS):
        flat = jnp.concatenate([src_ref[r, n * b: n * (b + 1), :] for r in range(d)], axis=0)
        outs.append(_dot(perm_ref[...], flat))
    return jnp.concatenate(outs, axis=0)


def _split_bf16(v, perm_t_ref, dst_ref, d):
    n = PERM_ROWS // d
    for b in range(v.shape[0] // PERM_ROWS):
        flat = _dot(perm_t_ref[...], v[PERM_ROWS * b: PERM_ROWS * (b + 1), :]).astype(BF16)
        for r in range(d):
            dst_ref[r, n * b: n * (b + 1), :] = flat[n * r: n * (r + 1), :]


def _inproj(x2, norm_w, w_in_full, cos2, sin2):
    S = x2.shape[0]
    tm = PROJ_TM

    def body(x_ref, nw_ref, w_ref, cos_ref, sin_ref, t4_ref, t16_ref, p0_ref, p1_ref, qkv_ref, qkv4_ref, qkv16_ref, z_ref,
             u_ref):
        x = x_ref[...]
        r = lax.rsqrt(jnp.mean(x * x, axis=-1, keepdims=True) + EPS)
        u = x * r * nw_ref[...]
        ub = u.astype(BF16)
        u_ref[...] = ub
        p0_ref[...] = _dot(ub, w_ref[0])
        p1_ref[...] = _dot(ub, w_ref[1])
        y2 = _dot(ub, w_ref[2])
        cosf = jnp.tile(cos_ref[...], (1, 8))
        sinf = jnp.tile(sin_ref[...], (1, 8))
        y3 = _dot(ub, w_ref[3])
        z_ref[...] = y3[:, AT_W:]
        qkv = jnp.concatenate([y2 * cosf + _rope_rot(y2) * sinf, y3[:, :AT_W]], axis=1).astype(BF16)
        qkv_ref[...] = qkv
        _split_bf16(qkv, t4_ref, qkv4_ref, 4)
        _split_bf16(qkv, t16_ref, qkv16_ref, 16)

    row = lambda w: pl.BlockSpec((tm, w), lambda s: (s, 0))
    qkv_w = 3 * AT_W
    perm = _full((PERM_ROWS, PERM_ROWS))
    return pl.pallas_call(
        body, name="inproj", grid=(S // tm,),
        in_specs=[row(D_MODEL), _full((1, D_MODEL)), _full((4, D_MODEL, 1024)), row(128), row(128), perm, perm],
        out_specs=[row(1024), row(1024), row(qkv_w), _dil_spec(4, tm, qkv_w), _dil_spec(16, tm, qkv_w), row(AT_W),
                   row(D_MODEL)],
        out_shape=[jax.ShapeDtypeStruct((S, 1024), F32), jax.ShapeDtypeStruct((S, 1024), F32),
                   jax.ShapeDtypeStruct((S, qkv_w), BF16), jax.ShapeDtypeStruct((4, S // 4, qkv_w), BF16),
                   jax.ShapeDtypeStruct((16, S // 16, qkv_w), BF16), jax.ShapeDtypeStruct((S, AT_W), F32),
                   jax.ShapeDtypeStruct((S, D_MODEL), BF16)],
        compiler_params=_params(("parallel",)),
    )(x2, norm_w, w_in_full, cos2, sin2, _perm_matrix(4, True), _perm_matrix(16, True))


HG_HPS = 4
HG_CPS = 4
N_LEV = len(LEVELS)


def _hg_const_arrays():
    r = np.arange(CHUNK)[:, None]
    c = np.arange(CHUNK)[None, :]
    tris = np.stack([r >= c, r <= c])
    lm = [((r // (2 * m)) == (c // (2 * m))) & (r % (2 * m) >= m) & (c % (2 * m) < m) for m in LEVELS]
    dm = [(c == r - dl) & (r % DIAG >= dl) for dl in range(DIAG)]
    masks = np.stack(lm + dm)
    return jnp.asarray(tris, BF16), jnp.asarray(masks, F32)


def _split2(a):
    hi = a.astype(BF16)
    return hi, (a - hi.astype(F32)).astype(BF16)


def _dot3(a, b, dot=_dot):
    ah, al = _split2(a)
    bh, bl = _split2(b)
    n = b.shape[1]
    p = dot(ah, jnp.concatenate([bh, bl], axis=1))
    return (p[:, :n] + p[:, n:]) + dot(al, bh)


def _split3(a):
    a1 = a.astype(BF16)
    r1 = a - a1.astype(F32)
    a2 = r1.astype(BF16)
    return a1, a2, (r1 - a2.astype(F32)).astype(BF16)


def _tri_dot(tri, a):
    n = a.shape[1]
    p = _dot(tri, jnp.concatenate(_split3(a), axis=1))
    return (p[:, :n] + p[:, n:2 * n]) + p[:, 2 * n:]


def _dot_sel(a, sel):
    a1, a2, a3 = _split3(a)
    return (_dot(a1, sel) + _dot(a2, sel)) + _dot(a3, sel)


def _rowsum(t):
    return _dot(t.astype(BF16), jnp.ones((t.shape[1], t.shape[1]), BF16))


def _level_refs(b):
    refs = []
    pos = _iota(b.shape, 0)
    for m in LEVELS:
        if 2 * m >= SUBLANES:
            parts = [jnp.broadcast_to(b[r0 + m - 1: r0 + m, :], (2 * m, b.shape[1])) for r0 in range(0, CHUNK, 2 * m)]
            refs.append(parts[0] if len(parts) == 1 else jnp.concatenate(parts, axis=0))
        else:
            p = pos & (2 * m - 1)
            ref = b
            for off in range(-(m - 1), m + 1):
                if off != 0:
                    ref = jnp.where(p == m - 1 + off, pltpu.roll(b, off % CHUNK, 0), ref)
            refs.append(ref)
    return refs


def _hg_lb(lbl_ref):
    l0 = lbl_ref[0:1, :]
    l1 = lbl_ref[1:2, :]
    mx = jnp.maximum(l0, l1)
    e0 = jnp.exp(l0 - mx)
    e1 = jnp.exp(l1 - mx)
    p0 = e0 / (e0 + e1)
    lb = jnp.clip(p0, 1e-6, 1.0 - 1e-6)
    inside = (p0 >= 1e-6) & (p0 <= 1.0 - 1e-6)
    dlb_dl0 = jnp.where(inside, p0 * (e1 / (e0 + e1)), 0.0)
    return lb, dlb_dl0


def _hg_gates(qr, fl, lb):
    sig = _sigmoid(fl)
    f = lb + (1.0 - lb) * sig
    g = jnp.log(f)
    k = (1.0 - lb) * (1.0 - sig)
    sq = _sigmoid(qr)
    q = qr * sq
    return sig, f, g, k, sq, q


def _neg_abs(v):
    bits = lax.bitcast_convert_type(v, jnp.uint32) | jnp.uint32(0x80000000)
    return lax.bitcast_convert_type(bits, F32)


def _hg_levels(q, k, b, mk_ref):
    refs = _level_refs(b)
    a = jnp.zeros((CHUNK, CHUNK), F32)
    es, qts, kts = [], [], []
    for i in range(N_LEV):
        e = jnp.exp(_neg_abs(b - refs[i]))
        qt = (q * e).astype(BF16)
        kt = (k * e).astype(BF16)
        a = a + _dot_nt(qt, kt) * mk_ref[i]
        es.append(e); qts.append(qt); kts.append(kt)
    return a, es, qts, kts


def _hg_specs(nc, rev):
    cc = (lambda c: nc - 1 - c) if rev else (lambda c: c)
    w = HG_HEAD * HG_HPS
    blk = lambda off: pl.BlockSpec((HG_CPS * CHUNK, w), lambda h, c: (cc(c), h + off))
    lb2 = pl.BlockSpec((2, w), lambda h, c: (0, h))
    st = pl.BlockSpec((HG_CPS, HG_HPS, HG_HEAD, HG_HEAD), lambda h, c: (cc(c), h, 0, 0))
    consts = [_full((2, CHUNK, CHUNK)), _full((N_LEV + DIAG, CHUNK, CHUNK))]
    return blk, lb2, st, consts


def _hg_fwd(p0, p1, lbl):
    S = p0.shape[0]
    nc = S // (HG_CPS * CHUNK)
    ng = HG_HEADS // HG_HPS

    def body(qr_ref, fl_ref, v_ref, lbl_ref, tri_ref, mk_ref, o_ref, st_ref, state):
        c = pl.program_id(1)

        @pl.when(c == 0)
        def _():
            state[...] = jnp.zeros_like(state)

        lb_all, _ = _hg_lb(lbl_ref)
        heads = [slice(HG_HEAD * hh, HG_HEAD * (hh + 1)) for hh in range(HG_HPS)]
        for j in range(HG_CPS):
            rows = slice(CHUNK * j, CHUNK * (j + 1))
            qs, ks, bs, mats = [], [], [], []
            for sl in heads:
                _, _, g, k, _, q = _hg_gates(qr_ref[rows, sl], fl_ref[rows, sl], lb_all[:, sl])
                qs.append(q); ks.append(k); bs.append(_tri_dot(tri_ref[0], g))
            for hh in range(HG_HPS):
                a, _, _, _ = _hg_levels(qs[hh], ks[hh], bs[hh], mk_ref)
                mats.append(a + _rowsum(qs[hh] * ks[hh]) * mk_ref[N_LEV])
            for hh, sl in enumerate(heads):
                q, k, b, a, v = qs[hh], ks[hh], bs[hh], mats[hh], v_ref[rows, sl]
                b_last = b[CHUNK - 1: CHUNK, :]
                st = state[hh]
                st_ref[j, hh] = st
                o = _dot_nt((q * jnp.exp(b)).astype(BF16), st.astype(BF16)) + _dot(a.astype(BF16), v.astype(BF16))
                state[hh] = st * jnp.exp(b_last) + _dot3(v, k * jnp.exp(b_last - b), _dot_tn)
                o_ref[rows, sl] = o

    blk, lb2, st_spec, consts = _hg_specs(nc, False)
    tris, masks = _hg_const_arrays()
    return pl.pallas_call(
        body, name="hg_fwd", grid=(ng, nc),
        in_specs=[blk(0), blk(ng), blk(0), lb2] + consts,
        out_specs=[blk(0), st_spec],
        out_shape=[jax.ShapeDtypeStruct((S, HG_W), F32), jax.ShapeDtypeStruct((S // CHUNK, HG_HEADS, HG_HEAD, HG_HEAD), F32)],
        scratch_shapes=[pltpu.VMEM((HG_HPS, HG_HEAD, HG_HEAD), F32)],
        compiler_params=_params(("parallel", "arbitrary")),
    )(p0, p0, p1, lbl, tris, masks)


def _hg_bwd(p0, p1, do_hg, states, lbl):
    S = p0.shape[0]
    nc = S // (HG_CPS * CHUNK)
    ng = HG_HEADS // HG_HPS
    w = HG_HEAD * HG_HPS

    def body(qr_ref, fl_ref, v_ref, do_ref, st_ref, lbl_ref, tri_ref, mk_ref, mkb_ref,
             dqr_ref, dfl_ref, dv_ref, glbl_ref, dstate, carry, acc_lb):
        c = pl.program_id(1)

        @pl.when(c == 0)
        def _():
            dstate[...] = jnp.zeros_like(dstate)
            carry[...] = jnp.zeros_like(carry)
            acc_lb[...] = jnp.zeros_like(acc_lb)

        lb_all, dlb_dl0 = _hg_lb(lbl_ref)
        heads = [slice(HG_HEAD * hh, HG_HEAD * (hh + 1)) for hh in range(HG_HPS)]
        diag_mask = mk_ref[N_LEV]
        for j in reversed(range(HG_CPS)):
            rows = slice(CHUNK * j, CHUNK * (j + 1))
            gates, bs, dos = [], [], []
            for sl in heads:
                gt = _hg_gates(qr_ref[rows, sl], fl_ref[rows, sl], lb_all[:, sl])
                gates.append(gt)
                bs.append(_tri_dot(tri_ref[0], gt[2]))
            for sl in heads:
                dos.append(do_ref[rows, sl])
            inter = []
            for hh, sl in enumerate(heads):
                _, _, _, k, _, q = gates[hh]
                b, do, v = bs[hh], dos[hh], v_ref[rows, sl]
                b_last = b[CHUNK - 1: CHUNK, :]
                eb = jnp.exp(b)
                edec = jnp.exp(b_last - b)
                dst = dstate[hh]
                dq = _dot3(do, st_ref[j, hh]) * eb
                dk = _dot3(v, dst) * edec
                da = _dot_nt(do.astype(BF16), v.astype(BF16))
                dv_state = _dot_nt((k * edec).astype(BF16), dst.astype(BF16))
                dstate[hh] = dst * jnp.exp(b_last) + _dot3(do, q * eb, _dot_tn)
                inter.append((dq, dk, da, dv_state))
            for hh, sl in enumerate(heads):
                sig, f, _, k, sq, q = gates[hh]
                dq, dk, da, dv_state = inter[hh]
                b = bs[hh]
                db = q * dq - k * dk
                dab = da.astype(BF16)
                refs = _level_refs(b)
                a = _rowsum(q * k) * diag_mask
                for i in range(N_LEV):
                    e = jnp.exp(_neg_abs(b - refs[i]))
                    qt = (q * e).astype(BF16)
                    kt = (k * e).astype(BF16)
                    a = a + _dot_nt(qt, kt) * mk_ref[i]
                    dam = dab * mkb_ref[i]
                    gq = _dot(dam, kt)
                    gk = _dot_tn(dam, qt)
                    dq = dq + e * gq
                    dk = dk + e * gk
                    db = db + (qt.astype(F32) * gq - kt.astype(F32) * gk)
                dc = _rowsum(da * diag_mask)
                dq = dq + dc * k
                dk = dk + dc * q
                dv_ref[rows, sl] = (_dot_tn(a.astype(BF16), dos[hh].astype(BF16)) + dv_state).astype(BF16)
                dg = _tri_dot(tri_ref[1], db) + carry[0:1, sl]
                carry[0:1, sl] += jnp.sum(db, axis=0, keepdims=True)
                lb = lb_all[:, sl]
                qr = qr_ref[rows, sl]
                t = dg / f - dk
                dfl_ref[rows, sl] = (t * (1.0 - lb) * sig * (1.0 - sig)).astype(BF16)
                acc_lb[0:1, sl] += jnp.sum(t * (1.0 - sig), axis=0, keepdims=True)
                dqr_ref[rows, sl] = (dq * (sq * (1.0 + qr * (1.0 - sq)))).astype(BF16)

        @pl.when(c == nc - 1)
        def _():
            gl0 = acc_lb[0:1, :] * dlb_dl0
            glbl_ref[0:1, :] = gl0
            glbl_ref[1:2, :] = -gl0

    blk, lb2, st_spec, consts = _hg_specs(nc, True)
    tris, masks = _hg_const_arrays()
    act = jax.ShapeDtypeStruct((S, HG_W), BF16)
    return pl.pallas_call(
        body, name="hg_bwd", grid=(ng, nc),
        in_specs=[blk(0), blk(ng), blk(0), blk(0), st_spec, lb2] + consts + consts[1:],
        out_specs=[blk(0), blk(0), blk(0), lb2],
        out_shape=[act, act, act, jax.ShapeDtypeStruct((2, HG_W), F32)],
        scratch_shapes=[pltpu.VMEM((HG_HPS, HG_HEAD, HG_HEAD), F32), pltpu.VMEM((8, w), F32), pltpu.VMEM((8, w), F32)],
        compiler_params=_params(("parallel", "arbitrary")),
    )(p0, p0, p1, do_hg, states, lbl, tris, masks, masks.astype(BF16))


def _stat_head(lane):
    return lane >> 4


def _stat_lane(lane):
    return lane & (STAT_LANES - 1)


def _o_head(lane):
    return lane >> 6


def _att_bias():
    qi = np.arange(ATT_BLK)[:, None]
    kj = np.arange(2 * ATT_BLK)[None, :]
    band = (kj >= qi) & (kj <= qi + ATT_BLK)
    qm = np.stack([band & (kj >= ATT_BLK), band])
    cur = (kj < ATT_BLK) & (qi <= kj)
    km = np.stack([cur, cur | ((kj >= ATT_BLK) & (qi >= kj - ATT_BLK))])
    to_bias = lambda m: jnp.asarray(np.where(m, 0.0, NEG), F32)
    return to_bias(qm), to_bias(km)


def _att_fwd(qkv_d):
    d, L, _ = qkv_d.shape
    qb = ATT_QB
    rows = qb * ATT_BLK

    def body(q_ref, kp_ref, kc_ref, vp_ref, vc_ref, bias_ref, o_ref, lse_ref, s_scr, p_scr):
        first = _iota((ATT_BLK, HEAD_PAIR), 1) < AT_HEAD
        head_of_lane = _stat_head(_iota((ATT_BLK, STAT_W), 1))
        pairs = [slice(HEAD_PAIR * hp, HEAD_PAIR * (hp + 1)) for hp in range(4)]
        blk = lambda i: slice(ATT_BLK * i, ATT_BLK * (i + 1))

        def keys(i, sl, prev_ref, cur_ref):
            before = prev_ref[0, :, sl] if i == 0 else cur_ref[0, blk(i - 1), sl]
            return jnp.concatenate([before, cur_ref[0, blk(i), sl]], axis=0)

        for i in range(qb):
            for hp, sl in enumerate(pairs):
                q2 = q_ref[0, blk(i), sl] * ATT_SCALE
                zero = jnp.zeros_like(q2)
                qs = jnp.concatenate([jnp.where(first, q2, zero), jnp.where(first, zero, q2)], axis=0)
                s_scr[4 * i + hp] = _dot_nt(qs, keys(i, sl, kp_ref, kc_ref))
        stats = []
        for i in range(qb):
            bias = bias_ref[jnp.minimum(pl.program_id(1), 1)] if i == 0 else bias_ref[1]
            bias2 = jnp.concatenate([bias, bias], axis=0)
            for hp in range(4):
                s = s_scr[4 * i + hp] + bias2
                m = jnp.max(s, axis=-1, keepdims=True)
                p = jnp.exp(s - m)
                l = jnp.sum(p, axis=-1, keepdims=True)
                p_scr[4 * i + hp] = p.astype(BF16)
                stats.append((l, m + jnp.log(l)))
        for i in range(qb):
            lse_blk = jnp.zeros((ATT_BLK, STAT_W), F32)
            for hp, sl in enumerate(pairs):
                l, lse = stats[4 * i + hp]
                o = _dot(p_scr[4 * i + hp], keys(i, sl, vp_ref, vc_ref)) / l
                o_ref[0, blk(i), sl] = jnp.where(first, o[:ATT_BLK], o[ATT_BLK:]).astype(BF16)
                lse_blk = jnp.where(head_of_lane == 2 * hp, lse[:ATT_BLK],
                                    jnp.where(head_of_lane == 2 * hp + 1, lse[ATT_BLK:], lse_blk))
            lse_ref[0, blk(i), :] = lse_blk

    cur = lambda j: pl.BlockSpec((1, rows, AT_W), lambda r, n: (r, n, j))
    prev = lambda j: pl.BlockSpec((1, ATT_BLK, AT_W), lambda r, n: (r, jnp.maximum(qb * n - 1, 0), j))
    return pl.pallas_call(
        body, name=f"att_fwd_d{d}", grid=(d, L // rows),
        in_specs=[cur(0), prev(1), cur(1), prev(2), cur(2), _full((2, ATT_BLK, 2 * ATT_BLK))],
        out_specs=[pl.BlockSpec((1, rows, AT_W), lambda r, n: (r, n, 0)), pl.BlockSpec((1, rows, STAT_W), lambda r, n: (r, n, 0))],
        out_shape=[jax.ShapeDtypeStruct((d, L, AT_W), BF16), jax.ShapeDtypeStruct((d, L, STAT_W), F32)],
        scratch_shapes=[pltpu.VMEM((4 * qb, 2 * ATT_BLK, 2 * ATT_BLK), F32), pltpu.VMEM((4 * qb, 2 * ATT_BLK, 2 * ATT_BLK), BF16)],
        compiler_params=_params(("parallel", "parallel")),
    )(qkv_d, qkv_d, qkv_d, qkv_d, qkv_d, _att_bias()[0])


def _att_combine(os_d, ls_d, z_at):
    S = z_at.shape[0]
    tm = MIX_TM

    def body(oa_ref, ob4_ref, oc16_ref, la_ref, lb4_ref, lc16_ref, p4_ref, p16_ref, z_ref,
             oat_ref, lse_ref, mix_ref, lb_scr, lc_scr):
        _merge_residues(lb4_ref, lb_scr, 4)
        _merge_residues(lc16_ref, lc_scr, 16)
        ls = (la_ref[...], lb_scr[0], lc_scr[0])
        mx = jnp.maximum(jnp.maximum(ls[0], ls[1]), ls[2])
        es = [jnp.exp(l - mx) for l in ls]
        zs = es[0] + es[1] + es[2]
        lse_ref[...] = mx + jnp.log(zs)
        spread = ((_o_head(_iota((STAT_W, AT_W), 1)) == _stat_head(_iota((STAT_W, AT_W), 0)))
                  & (_stat_lane(_iota((STAT_W, AT_W), 0)) == 0)).astype(BF16)
        os_ = (oa_ref[...].astype(F32), _merge_bf16(ob4_ref, p4_ref, 4), _merge_bf16(oc16_ref, p16_ref, 16))
        o = jnp.zeros((tm, AT_W), F32)
        for e, oi in zip(es, os_):
            hi, lo = _split2(e / zs)
            o = o + (_dot(hi, spread) + _dot(lo, spread)) * oi
        oat_ref[...] = o
        z = z_ref[...]
        mixed = o * (z * _sigmoid(z))
        mix_ref[...] = mixed.astype(BF16)

    row = lambda w: pl.BlockSpec((tm, w), lambda s: (s, 0))
    return pl.pallas_call(
        body, name="att_combine", grid=(S // tm,),
        in_specs=[row(AT_W), _dil_spec(4, tm, AT_W), _dil_spec(16, tm, AT_W),
                  row(STAT_W), _dil_spec(4, tm, STAT_W), _dil_spec(16, tm, STAT_W),
                  _full((PERM_ROWS, PERM_ROWS)), _full((PERM_ROWS, PERM_ROWS)), row(AT_W)],
        out_specs=[row(AT_W), row(STAT_W), row(AT_W)],
        out_shape=[jax.ShapeDtypeStruct((S, AT_W), F32), jax.ShapeDtypeStruct((S, STAT_W), F32),
                   jax.ShapeDtypeStruct((S, AT_W), BF16)],
        scratch_shapes=[_slab_scratch(tm, STAT_W), _slab_scratch(tm, STAT_W)],
        compiler_params=_params(("parallel",)),
    )(os_d[0].reshape(S, AT_W), os_d[1], os_d[2], ls_d[0].reshape(S, STAT_W), ls_d[1], ls_d[2],
      _perm_matrix(4), _perm_matrix(16), z_at)


def _att_gate_bwd(dm_at, o_at, lse, z_at):
    S = z_at.shape[0]
    tm = MIX_TM

    def body(dm_ref, o_ref, l_ref, z_ref, t4_ref, t16_ref, do_ref, do4_ref, do16_ref, dl_ref, dl4_ref, dl16_ref, dz_ref,
             dl_scr):
        o = o_ref[...]
        z = z_ref[...]
        dm = dm_ref[...]
        sz = _sigmoid(z)
        dz_ref[...] = (dm * o * (sz * (1.0 + z * (1.0 - sz)))).astype(BF16)
        do = dm * (z * sz)
        dob = do.astype(BF16)
        do_ref[...] = dob
        gather = (_o_head(_iota((AT_W, STAT_W), 0)) == _stat_head(_iota((AT_W, STAT_W), 1))).astype(BF16)
        dl = jnp.where(_stat_lane(_iota((tm, STAT_W), 1)) < STAT_LSE_LANE, _dot_sel(do * o, gather), l_ref[...])
        dl_ref[...] = dl
        _split_bf16(dob, t4_ref, do4_ref, 4)
        _split_bf16(dob, t16_ref, do16_ref, 16)
        _to_slabs(dl, dl_scr)
        _split_residues(dl_scr, dl4_ref, 4, F32)
        _split_residues(dl_scr, dl16_ref, 16, F32)

    row = lambda w: pl.BlockSpec((tm, w), lambda s: (s, 0))
    sds = jax.ShapeDtypeStruct
    return pl.pallas_call(
        body, name="att_gate_bwd", grid=(S // tm,),
        in_specs=[row(AT_W), row(AT_W), row(STAT_W), row(AT_W)] + [_full((PERM_ROWS, PERM_ROWS))] * 2,
        out_specs=[row(AT_W), _dil_spec(4, tm, AT_W), _dil_spec(16, tm, AT_W),
                   row(STAT_W), _dil_spec(4, tm, STAT_W), _dil_spec(16, tm, STAT_W), row(AT_W)],
        out_shape=[sds((S, AT_W), BF16), sds((4, S // 4, AT_W), BF16), sds((16, S // 16, AT_W), BF16),
                   sds((S, STAT_W), F32), sds((4, S // 4, STAT_W), F32), sds((16, S // 16, STAT_W), F32),
                   sds((S, AT_W), BF16)],
        scratch_shapes=[_slab_scratch(tm, STAT_W)],
        compiler_params=_params(("parallel",)),
    )(dm_at, o_at, lse, z_at, _perm_matrix(4, True), _perm_matrix(16, True))


def _att_bwd(qkv_d, do_d, dl_d):
    d, L, _ = qkv_d.shape
    nb = L // ATT_BLK
    qb = ATT_QB
    rows = qb * ATT_BLK
    nsteps = L // rows

    def body(qc_ref, qn_ref, kp_ref, kc_ref, vp_ref, vc_ref, ac_ref, an_ref, lc_ref, ln_ref, bq_ref, bk_ref, dqkv_ref,
             s_scr, dp_scr, st_scr, dpt_scr, ds_scr, pt_scr, dst_scr):
        n = pl.program_id(1)
        first = _iota((ATT_BLK, HEAD_PAIR), 1) < AT_HEAD
        pairs = [slice(HEAD_PAIR * hp, HEAD_PAIR * (hp + 1)) for hp in range(4)]
        blk = lambda i: slice(ATT_BLK * i, ATT_BLK * (i + 1))

        def stack(t):
            zero = jnp.zeros_like(t)
            return jnp.concatenate([jnp.where(first, t, zero), jnp.where(first, zero, t)], axis=0)

        def unstack(t2):
            return jnp.where(first, t2[:ATT_BLK], t2[ATT_BLK:])

        def with_prev(i, sl, prev_ref, cur_ref):
            before = prev_ref[0, :, sl] if i == 0 else cur_ref[0, blk(i - 1), sl]
            return jnp.concatenate([before, cur_ref[0, blk(i), sl]], axis=0)

        def with_next(i, sl, cur_ref, next_ref):
            after = next_ref[0, :, sl] if i == qb - 1 else cur_ref[0, blk(i + 1), sl]
            return jnp.concatenate([cur_ref[0, blk(i), sl], after], axis=0)

        for i in range(qb):
            for hp, sl in enumerate(pairs):
                j = 4 * i + hp
                s_scr[j] = _dot_nt(stack(qc_ref[0, blk(i), sl] * ATT_SCALE), with_prev(i, sl, kp_ref, kc_ref))
                dp_scr[j] = _dot_nt(stack(ac_ref[0, blk(i), sl]), with_prev(i, sl, vp_ref, vc_ref))
                st_scr[j] = _dot_nt(stack(kc_ref[0, blk(i), sl] * ATT_SCALE), with_next(i, sl, qc_ref, qn_ref))
                dpt_scr[j] = _dot_nt(stack(vc_ref[0, blk(i), sl]), with_next(i, sl, ac_ref, an_ref))
        for i in range(qb):
            bias = bq_ref[jnp.minimum(n, 1)] if i == 0 else bq_ref[1]
            bias_t = bk_ref[jnp.minimum(nsteps - 1 - n, 1)] if i == qb - 1 else bk_ref[1]
            bias2 = jnp.concatenate([bias, bias], axis=0)
            bias_t2 = jnp.concatenate([bias_t, bias_t], axis=0)
            dl_c = lc_ref[0, blk(i), :]
            dl_t = with_next(i, slice(None), lc_ref, ln_ref).T
            for hp in range(4):
                j = 4 * i + hp
                at = [STAT_LANES * (2 * hp), STAT_LANES * (2 * hp + 1)]
                col = lambda t, o: jnp.concatenate([t[:, a + o: a + o + 1] for a in at], axis=0)
                p = jnp.exp(s_scr[j] + bias2 - col(dl_c, STAT_LSE_LANE))
                ds_scr[j] = (p * (dp_scr[j] - col(dl_c, 0))).astype(BF16)
                row = lambda t, o: jnp.concatenate([jnp.broadcast_to(t[a + o: a + o + 1, :], (ATT_BLK, 2 * ATT_BLK)) for a in at], axis=0)
                pt = jnp.exp(st_scr[j] + bias_t2 - row(dl_t, STAT_LSE_LANE))
                pt_scr[j] = pt.astype(BF16)
                dst_scr[j] = (pt * (dpt_scr[j] - row(dl_t, 0))).astype(BF16)
        for i in range(qb):
            for hp, sl in enumerate(pairs):
                j = 4 * i + hp
                dq = unstack(_dot(ds_scr[j], with_prev(i, sl, kp_ref, kc_ref))) * ATT_SCALE
                dk = unstack(_dot(dst_scr[j], with_next(i, sl, qc_ref, qn_ref))) * ATT_SCALE
                dv = unstack(_dot(pt_scr[j], with_next(i, sl, ac_ref, an_ref)))
                dqkv_ref[0, blk(i), sl] = dq.astype(BF16)
                dqkv_ref[0, blk(i), AT_W + HEAD_PAIR * hp: AT_W + HEAD_PAIR * (hp + 1)] = dk.astype(BF16)
                dqkv_ref[0, blk(i), 2 * AT_W + HEAD_PAIR * hp: 2 * AT_W + HEAD_PAIR * (hp + 1)] = dv.astype(BF16)

    cur = lambda j: pl.BlockSpec((1, rows, AT_W), lambda r, n: (r, n, j))
    prev = lambda j: pl.BlockSpec((1, ATT_BLK, AT_W), lambda r, n: (r, jnp.maximum(qb * n - 1, 0), j))
    nxt_blk = lambda n: jnp.minimum(qb * (n + 1), nb - 1)
    sq = (4 * qb, 2 * ATT_BLK, 2 * ATT_BLK)
    return pl.pallas_call(
        body, name=f"att_bwd_d{d}", grid=(d, nsteps),
        in_specs=[cur(0), pl.BlockSpec((1, ATT_BLK, AT_W), lambda r, n: (r, nxt_blk(n), 0)), prev(1), cur(1), prev(2), cur(2),
                  pl.BlockSpec((1, rows, AT_W), lambda r, n: (r, n, 0)),
                  pl.BlockSpec((1, ATT_BLK, AT_W), lambda r, n: (r, nxt_blk(n), 0)),
                  pl.BlockSpec((1, rows, STAT_W), lambda r, n: (r, n, 0)),
                  pl.BlockSpec((1, ATT_BLK, STAT_W), lambda r, n: (r, nxt_blk(n), 0)),
                  _full((2, ATT_BLK, 2 * ATT_BLK)), _full((2, ATT_BLK, 2 * ATT_BLK))],
        out_specs=pl.BlockSpec((1, rows, 3 * AT_W), lambda r, n: (r, n, 0)),
        out_shape=jax.ShapeDtypeStruct((d, L, 3 * AT_W), BF16),
        scratch_shapes=[pltpu.VMEM(sq, F32)] * 4 + [pltpu.VMEM(sq, BF16)] * 3,
        compiler_params=_params(("parallel", "parallel")),
    )(qkv_d, qkv_d, qkv_d, qkv_d, qkv_d, qkv_d, do_d, do_d, dl_d, dl_d, *_att_bias())


def _att_bwd_combine(dqkvs, cos2, sin2):
    S = dqkvs[0].shape[1]
    tm = MIX_TM

    def body(a_ref, b4_ref, c16_ref, p4_ref, p16_ref, cos_ref, sin_ref, dq_ref, dk_ref, dv_ref):
        t = a_ref[...].astype(F32) + _merge_bf16(b4_ref, p4_ref, 4) + _merge_bf16(c16_ref, p16_ref, 16)
        dy = t[:, : 2 * AT_W]
        cosf = jnp.tile(cos_ref[...], (1, 8))
        sinf = jnp.tile(sin_ref[...], (1, 8))
        dx = dy * cosf - _rope_rot(dy) * sinf
        dq_ref[...] = dx[:, :AT_W].astype(BF16)
        dk_ref[...] = dx[:, AT_W:].astype(BF16)
        dv_ref[...] = t[:, 2 * AT_W:].astype(BF16)

    row = lambda w: pl.BlockSpec((tm, w), lambda s: (s, 0))
    act = jax.ShapeDtypeStruct((S, AT_W), BF16)
    return pl.pallas_call(
        body, name="att_bwd_combine", grid=(S // tm,),
        in_specs=[row(3 * AT_W), _dil_spec(4, tm, 3 * AT_W), _dil_spec(16, tm, 3 * AT_W),
                  _full((PERM_ROWS, PERM_ROWS)), _full((PERM_ROWS, PERM_ROWS)), row(128), row(128)],
        out_specs=[row(AT_W), row(AT_W), row(AT_W)],
        out_shape=[act, act, act],
        compiler_params=_params(("parallel",)),
    )(dqkvs[0].reshape(S, 3 * AT_W), dqkvs[1], dqkvs[2], _perm_matrix(4), _perm_matrix(16), cos2, sin2)


def _outproj(x2, tgt2, mix_at, w_out_full, fnw, o_hg, p1, hg_norm_w):
    S = x2.shape[0]
    tm = PROJ_TM
    ns = S // tm

    def body(x_ref, t_ref, ma_ref, w_ref, fw_ref, o_ref, z_ref, hgw_ref,
             dh_ref, doh_ref, dzh_ref, dma_ref, gw_ref, gfw_ref, ghgw_ref, loss_ref):
        s = pl.program_id(0)

        @pl.when(s == 0)
        def _():
            gw_ref[...] = jnp.zeros_like(gw_ref)
            gfw_ref[...] = jnp.zeros_like(gfw_ref)
            ghgw_ref[...] = jnp.zeros_like(ghgw_ref)
            loss_ref[...] = jnp.zeros_like(loss_ref)

        heads = [slice(HG_HEAD * hh, HG_HEAD * (hh + 1)) for hh in range(HG_HEADS)]
        norm = []
        for sl in heads:
            o, z = o_ref[:, sl], z_ref[:, sl]
            rs = lax.rsqrt(jnp.mean(o * o, axis=-1, keepdims=True) + EPS)
            norm.append((rs, o * rs, _sigmoid(z)))
        mh = jnp.concatenate([(oh * hgw_ref[:, sl] * (z_ref[:, sl] * sz)).astype(BF16)
                              for sl, (_, oh, sz) in zip(heads, norm)], axis=1)
        y = _dot(mh, w_ref[:HG_W, :]) + _dot(ma_ref[...], w_ref[HG_W:, :])
        h = x_ref[...] + y
        r = lax.rsqrt(jnp.mean(h * h, axis=-1, keepdims=True) + EPS)
        hn = h * r
        fw = fw_ref[...]
        err = hn * fw - t_ref[...]
        loss_ref[...] += 0.5 * jnp.sum(jnp.mean(err * err, axis=-1, keepdims=True))
        dout = err * (1.0 / D_MODEL)
        gfw_ref[...] += jnp.sum(dout * hn, axis=0, keepdims=True)
        dhn = dout * fw
        dh = r * (dhn - hn * jnp.mean(dhn * hn, axis=-1, keepdims=True))
        dh_ref[...] = dh
        dhb = dh.astype(BF16)
        dma_ref[...] = _dot_nt(dhb, w_ref[HG_W:, :])
        dmh = _dot_nt(dhb, w_ref[:HG_W, :])
        for sl, (rs, oh, sz) in zip(heads, norm):
            z, dmix, gwv = z_ref[:, sl], dmh[:, sl], hgw_ref[:, sl]
            dzh_ref[:, sl] = (dmix * (oh * gwv) * (sz * (1.0 + z * (1.0 - sz)))).astype(BF16)
            don = dmix * (z * sz)
            ghgw_ref[:, sl] += jnp.sum(don * oh, axis=0, keepdims=True)
            dy = don * gwv
            doh_ref[:, sl] = rs * (dy - oh * jnp.mean(dy * oh, axis=-1, keepdims=True))
        gw_ref[:HG_W, :] += _dot_tn(mh, dhb)
        gw_ref[HG_W:, :] += _dot_tn(ma_ref[...], dhb)

    row = lambda w: pl.BlockSpec((tm, w), lambda s: (s, 0))
    return pl.pallas_call(
        body, name="outproj", grid=(ns,),
        in_specs=[row(D_MODEL), row(D_MODEL), row(AT_W),
                  _full((D_MODEL, D_MODEL)), _full((1, D_MODEL)),
                  row(HG_W), pl.BlockSpec((tm, HG_W), lambda s: (s, 1)), _full((1, HG_W))],
        out_specs=[row(D_MODEL), row(HG_W), row(HG_W), row(AT_W), _full((D_MODEL, D_MODEL)), _full((1, D_MODEL)),
                   _full((1, HG_W)), _full((8, 128))],
        out_shape=[jax.ShapeDtypeStruct((S, D_MODEL), F32), jax.ShapeDtypeStruct((S, HG_W), F32),
                   jax.ShapeDtypeStruct((S, HG_W), BF16), jax.ShapeDtypeStruct((S, AT_W), F32),
                   jax.ShapeDtypeStruct((D_MODEL, D_MODEL), F32), jax.ShapeDtypeStruct((1, D_MODEL), F32),
                   jax.ShapeDtypeStruct((1, HG_W), F32), jax.ShapeDtypeStruct((8, 128), F32)],
        compiler_params=_params(("arbitrary",)),
    )(x2, tgt2, mix_at, w_out_full, fnw, o_hg, p1, hg_norm_w)


def _inproj_bwd_x(dps, w_in_full, x2, norm_w, dh, after=()):
    S = x2.shape[0]
    tm = PROJ_TM

    def body(d0, d1, d2, d3, d4, d5, d6, d7, w_ref, x_ref, nw_ref, dh_ref, *rest):
        gx_ref, gnw_ref = rest[len(after):]
        s = pl.program_id(0)

        @pl.when(s == 0)
        def _():
            gnw_ref[...] = jnp.zeros_like(gnw_ref)

        du = jnp.zeros((tm, D_MODEL), F32)
        for i, dref in enumerate((d0, d1, d2, d3, d4, d5, d6, d7)):
            j, half = divmod(i, 2)
            du = du + _dot_nt(dref[...], w_ref[j, :, 512 * half: 512 * (half + 1)])
        x = x_ref[...]
        r = lax.rsqrt(jnp.mean(x * x, axis=-1, keepdims=True) + EPS)
        xh = x * r
        gnw_ref[...] += jnp.sum(du * xh, axis=0, keepdims=True)
        dun = du * nw_ref[...]
        gx_ref[...] = dh_ref[...] + r * (dun - xh * jnp.mean(dun * xh, axis=-1, keepdims=True))

    row = lambda w: pl.BlockSpec((tm, w), lambda s: (s, 0))
    return pl.pallas_call(
        body, name="inproj_bwd_x", grid=(S // tm,),
        in_specs=[row(512)] * 8 + [_full((4, D_MODEL, 1024)), row(D_MODEL), _full((1, D_MODEL)), row(D_MODEL)]
        + [pl.BlockSpec(memory_space=pl.ANY)] * len(after),
        out_specs=[row(D_MODEL), _full((1, D_MODEL))],
        out_shape=[jax.ShapeDtypeStruct((S, D_MODEL), F32), jax.ShapeDtypeStruct((1, D_MODEL), F32)],
        compiler_params=_params(("arbitrary",)),
    )(*dps, w_in_full, x2, norm_w, dh, *after)


def _inproj_bwd_w(u, dps):
    S = u.shape[0]
    tm = 2 * PROJ_TM

    def body(u_ref, d0, d1, d2, d3, d4, d5, d6, d7, g_ref):
        @pl.when(pl.program_id(0) == 0)
        def _():
            g_ref[...] = jnp.zeros_like(g_ref)

        ub = u_ref[...]
        for i, dref in enumerate((d0, d1, d2, d3, d4, d5, d6, d7)):
            j, half = divmod(i, 2)
            g_ref[j, :, 512 * half: 512 * (half + 1)] += _dot_tn(ub, dref[...])

    return pl.pallas_call(
        body, name="inproj_bwd_w", grid=(S // tm,),
        in_specs=[pl.BlockSpec((tm, D_MODEL), lambda s: (s, 0))] + [pl.BlockSpec((tm, 512), lambda s: (s, 0))] * 8,
        out_specs=_full((4, D_MODEL, 1024)),
        out_shape=jax.ShapeDtypeStruct((4, D_MODEL, 1024), F32),
        compiler_params=_params(("arbitrary",)),
    )(u, *dps)


def _adamw_update(gg, w_ref, m_ref, v_ref, d_ref, nm_ref, nv_ref):
    nm = ADAM_B1 * m_ref[...] + (1.0 - ADAM_B1) * gg
    nv = ADAM_B2 * v_ref[...] + (1.0 - ADAM_B2) * (gg * gg)
    m_hat = nm / (1.0 - ADAM_B1 ** ADAM_STEP)
    v_hat = nv / (1.0 - ADAM_B2 ** ADAM_STEP)
    d_ref[...] = -ADAM_LR * (m_hat / (jnp.sqrt(v_hat) + ADAM_EPS) + ADAM_WD * w_ref[...])
    nm_ref[...] = nm
    nv_ref[...] = nv


def _adamw_small(red, ws, ms, vs):
    def body(red_ref, *refs):
        w, m, v, g, d, nm, nv = (refs[4 * i: 4 * (i + 1)] for i in range(7))
        loss_ref = refs[28]
        g[0][...] = red_ref[0:1, :]
        g[1][0:1, :] = red_ref[3:4, :HG_W]
        g[1][1:2, :] = red_ref[3:4, HG_W:]
        g[2][...] = red_ref[2:3, :HG_W]
        g[3][...] = red_ref[1:2, :]
        loss_ref[...] = red_ref[4:5, 0:1]
        for i in range(4):
            _adamw_update(g[i][...], w[i], m[i], v[i], d[i], nm[i], nv[i])

    shapes = [a.shape for a in ws]
    sds = [jax.ShapeDtypeStruct(s, F32) for s in shapes]
    out = pl.pallas_call(
        body, name="adamw_small",
        in_specs=[_full(red.shape)] + [_full(s) for s in shapes] * 3,
        out_specs=[_full(s) for s in shapes] * 4 + [_full((1, 1))],
        out_shape=sds * 4 + [jax.ShapeDtypeStruct((1, 1), F32)],
        compiler_params=_params(),
    )(red, *ws, *ms, *vs)
    return out[0:4], out[4:8], out[8:12], out[12:16], out[16]


def _place():
    return lax.axis_index("x"), lax.axis_index("y"), lax.axis_index("c")


def _gather_weights(w_in_s, w_out_s):
    def body(win_ref, wout_ref, fin_ref, fout_ref, send_sems, recv_sems):
        x, y, c = _place()
        me = (x, y, c)
        sib = (x, y, 1 - c)
        mine = 2 * x + y
        fin_ref[mine] = win_ref[...].astype(BF16)
        fout_ref[mine] = wout_ref[...].astype(BF16)
        chips = [(1 - x, y), (x, 1 - y), (1 - x, 1 - y)]

        def halves(chip, half):
            return (fin_ref.at[chip, pl.ds(half * 512, 512), :], fout_ref.at[chip, pl.ds(half * 128, 128), :])

        def copy(k, ref, to):
            return pltpu.make_async_remote_copy(src_ref=ref, dst_ref=ref, send_sem=send_sems.at[k],
                                                recv_sem=recv_sems.at[k], device_id=to, device_id_type=MESH)

        first, passed = [], []
        for j, (cx, cy) in enumerate(chips):
            for a, ref in enumerate(halves(mine, c)):
                first.append(copy(2 * j + a, ref, (cx, cy, c)))
        for cp in first:
            cp.start()
        for j, (cx, cy) in enumerate(chips):
            for a, ref in enumerate(halves(2 * cx + cy, c)):
                copy(2 * j + a, ref, me).wait_recv()
                fwd = copy(6 + 2 * j + a, ref, sib)
                fwd.start()
                passed.append(fwd)
        for j, (cx, cy) in enumerate(chips):
            for a, ref in enumerate(halves(2 * cx + cy, 1 - c)):
                copy(6 + 2 * j + a, ref, me).wait_recv()
        for cp in first + passed:
            cp.wait_send()

    vm = pl.BlockSpec(memory_space=pltpu.VMEM)
    return pl.pallas_call(
        body, name="gather_weights",
        in_specs=[vm, vm], out_specs=[vm, vm],
        out_shape=[jax.ShapeDtypeStruct((4, D_MODEL, 1024), BF16), jax.ShapeDtypeStruct((4, 256, D_MODEL), BF16)],
        scratch_shapes=[pltpu.SemaphoreType.DMA((12,)), pltpu.SemaphoreType.DMA((12,))],
        compiler_params=pltpu.CompilerParams(vmem_limit_bytes=VMEM_LIMIT),
    )(w_in_s, w_out_s)


def _swap_halves(g_in, g_out):
    def body(gin_ref, gout_ref, rin_ref, rout_ref, send_sems, recv_sems):
        x, y, c = _place()
        sib = (x, y, 1 - c)
        cps = [pltpu.make_async_remote_copy(src_ref=src.at[:, 1 - c], dst_ref=dst, send_sem=send_sems.at[k],
                                            recv_sem=recv_sems.at[k], device_id=sib, device_id_type=MESH)
               for k, (src, dst) in enumerate(((gin_ref, rin_ref), (gout_ref, rout_ref)))]
        for cp in cps:
            cp.start()
        for cp in cps:
            cp.wait()

    hbm = pl.BlockSpec(memory_space=pl.ANY)
    return pl.pallas_call(
        body, name="swap_halves",
        in_specs=[hbm, hbm], out_specs=[hbm, hbm],
        out_shape=[jax.ShapeDtypeStruct((4,) + g.shape[2:], F32) for g in (g_in, g_out)],
        scratch_shapes=[pltpu.SemaphoreType.DMA((2,)), pltpu.SemaphoreType.DMA((2,))],
    )(g_in, g_out)


def _add_half(g, r, cidx, name):
    n, _, rows, cols = g.shape

    def body(c_ref, g_ref, r_ref, o_ref):
        o_ref[0] = (g_ref[0, 0] + r_ref[0]).astype(BF16)

    return pl.pallas_call(
        body, name=name,
        grid_spec=pltpu.PrefetchScalarGridSpec(
            num_scalar_prefetch=1, grid=(n,),
            in_specs=[pl.BlockSpec((1, 1, rows, cols), lambda j, c_ref: (j, c_ref[0], 0, 0)),
                      pl.BlockSpec((1, rows, cols), lambda j, c_ref: (j, 0, 0))],
            out_specs=pl.BlockSpec((1, rows, cols), lambda j, c_ref: (j, 0, 0))),
        out_shape=jax.ShapeDtypeStruct((n, rows, cols), BF16),
        compiler_params=_params(("parallel",)),
    )(cidx, g, r)


def _exchange_copies(in_ref, out_ref, lin_ref, lout_ref, send_sems, recv_sems):
    x, y, c = _place()
    cps = []
    for k, (cx, cy) in enumerate([(1 - x, y), (x, 1 - y), (1 - x, 1 - y)]):
        for a, (src, dst) in enumerate(((in_ref, lin_ref), (out_ref, lout_ref))):
            cps.append(pltpu.make_async_remote_copy(
                src_ref=src.at[2 * cx + cy], dst_ref=dst.at[k], send_sem=send_sems.at[2 * k + a],
                recv_sem=recv_sems.at[2 * k + a], device_id=(cx, cy, c), device_id_type=MESH))
    return cps


def _exchange_start(cs_in, cs_out):
    def body(in_ref, out_ref, lin_ref, lout_ref, send_sems, recv_sems, in_thru, out_thru, lin_thru, lout_thru, token):
        for cp in _exchange_copies(in_ref, out_ref, lin_ref, lout_ref, send_sems, recv_sems):
            cp.start()
        token[...] = jnp.zeros_like(token)

    lands = [lax.empty((3,) + a.shape[1:], a.dtype) for a in (cs_in, cs_out)]
    bufs = [pltpu.with_memory_space_constraint(a, pltpu.HBM) for a in (cs_in, cs_out, *lands)]
    hbm = pl.BlockSpec(memory_space=pltpu.HBM)
    sem = pl.BlockSpec(memory_space=pltpu.SEMAPHORE)
    return pl.pallas_call(
        body, name="exchange_start",
        in_specs=[hbm] * 4,
        out_specs=[sem, sem, hbm, hbm, hbm, hbm, pl.BlockSpec(memory_space=pltpu.VMEM)],
        out_shape=[pltpu.SemaphoreType.DMA((6,)), pltpu.SemaphoreType.DMA((6,))]
                  + [pltpu.HBM(b.shape, b.dtype) for b in bufs] + [jax.ShapeDtypeStruct((8, 128), F32)],
        input_output_aliases={0: 2, 1: 3, 2: 4, 3: 5},
        compiler_params=pltpu.CompilerParams(has_side_effects=pltpu.SideEffectType.DATAFLOW_SIDE_EFFECTING),
    )(*bufs)


def _exchange_wait(send_sems, recv_sems, in_thru, out_thru, lin_thru, lout_thru, after):
    def body(in_ref, out_ref, lin_ref, lout_ref, send_sems, recv_sems, after_ref, in_dead, out_dead, got_in, got_out):
        for cp in _exchange_copies(in_ref, out_ref, lin_ref, lout_ref, send_sems, recv_sems):
            cp.wait_send()
            cp.wait_recv()

    hbm = pl.BlockSpec(memory_space=pltpu.HBM)
    sem = pl.BlockSpec(memory_space=pltpu.SEMAPHORE)
    bufs = (in_thru, out_thru, lin_thru, lout_thru)
    return pl.pallas_call(
        body, name="exchange_wait",
        in_specs=[hbm] * 4 + [sem, sem, pl.BlockSpec(memory_space=pl.ANY)],
        out_specs=[hbm] * 4,
        out_shape=[pltpu.HBM(b.shape, b.dtype) for b in bufs],
        input_output_aliases={0: 0, 1: 1, 2: 2, 3: 3},
        compiler_params=pltpu.CompilerParams(has_side_effects=pltpu.SideEffectType.DATAFLOW_SIDE_EFFECTING),
    )(*bufs, send_sems, recv_sems, after)


PEER_XOR = (2, 1, 3)


def _sum_chips(cs, r, chip_idx, name):
    _, rows, cols = r.shape
    tr = min(rows, 256)

    def body(m_ref, cs_ref, r_ref, o_ref):
        mine = m_ref[0]
        own = cs_ref[0].astype(F32)
        got = [r_ref[k].astype(F32) for k in range(3)]
        acc = None
        for s in range(4):
            rel = mine ^ s
            term = jnp.where(rel == 0, own, jnp.where(rel == PEER_XOR[0], got[0],
                                                      jnp.where(rel == PEER_XOR[1], got[1], got[2])))
            acc = term if acc is None else acc + term
        o_ref[...] = acc

    return pl.pallas_call(
        body, name=name,
        grid_spec=pltpu.PrefetchScalarGridSpec(
            num_scalar_prefetch=1, grid=(rows // tr,),
            in_specs=[pl.BlockSpec((1, tr, cols), lambda i, m_ref: (m_ref[0], i, 0)),
                      pl.BlockSpec((3, tr, cols), lambda i, m_ref: (0, i, 0))],
            out_specs=pl.BlockSpec((tr, cols), lambda i, m_ref: (i, 0))),
        out_shape=jax.ShapeDtypeStruct((rows, cols), F32),
        compiler_params=_params(("parallel",)),
    )(chip_idx, cs, r)


def _swap_reduced(h_in, h_out):
    def body(in_ref, out_ref, rin_ref, rout_ref, send_sems, recv_sems):
        x, y, c = _place()
        cps = [pltpu.make_async_remote_copy(src_ref=src, dst_ref=dst, send_sem=send_sems.at[k],
                                            recv_sem=recv_sems.at[k], device_id=(x, y, 1 - c), device_id_type=MESH)
               for k, (src, dst) in enumerate(((in_ref, rin_ref), (out_ref, rout_ref)))]
        for cp in cps:
            cp.start()
        for cp in cps:
            cp.wait()

    hbm = pl.BlockSpec(memory_space=pl.ANY)
    return pl.pallas_call(
        body, name="swap_reduced",
        in_specs=[hbm, hbm], out_specs=[hbm, hbm],
        out_shape=[jax.ShapeDtypeStruct(h.shape, F32) for h in (h_in, h_out)],
        scratch_shapes=[pltpu.SemaphoreType.DMA((2,)), pltpu.SemaphoreType.DMA((2,))],
    )(h_in, h_out)


def _adamw_halves(w, mine, theirs, m, v, cidx, name):
    rows, cols = w.shape
    half = rows // 2
    tr = min(half, 256)
    nbh = half // tr

    def body(c_ref, w_ref, a_ref, b_ref, m_ref, v_ref, g_ref, d_ref, nm_ref, nv_ref):
        gg = jnp.where(pl.program_id(0) // nbh == c_ref[0], a_ref[...], b_ref[...])
        g_ref[...] = gg
        _adamw_update(gg, w_ref, m_ref, v_ref, d_ref, nm_ref, nv_ref)

    spec = pl.BlockSpec((tr, cols), lambda i, c_ref: (i, 0))
    hspec = pl.BlockSpec((tr, cols), lambda i, c_ref: (i % nbh, 0))
    sds = jax.ShapeDtypeStruct((rows, cols), F32)
    return pl.pallas_call(
        body, name=name,
        grid_spec=pltpu.PrefetchScalarGridSpec(
            num_scalar_prefetch=1, grid=(rows // tr,),
            in_specs=[spec, hspec, hspec, spec, spec], out_specs=[spec] * 4),
        out_shape=[sds] * 4,
        compiler_params=_params(("parallel",)),
    )(cidx, w, mine, theirs, m, v)


def _allreduce_small(g_nw, g_fw, g_hgw, g_lbl, loss8):
    def body(nw_ref, fw_ref, hgw_ref, lbl_ref, loss_ref, out_ref, slots, send_sems, recv_sems):
        x, y, c = _place()
        me = 4 * x + 2 * y + c
        slots[me] = jnp.zeros((8, D_MODEL), F32)
        slots[me, 0:1, :] = nw_ref[...]
        slots[me, 1:2, :] = fw_ref[...]
        slots[me, 2:3, 0:HG_W] = hgw_ref[...]
        slots[me, 3:4, 0:HG_W] = lbl_ref[0:1, :]
        slots[me, 3:4, HG_W:] = lbl_ref[1:2, :]
        slots[me, 4:5, 0:128] = loss_ref[0:1, :]
        cps = []
        for k in range(1, 8):
            dx, dy, dc = (k >> 2) & 1, (k >> 1) & 1, k & 1
            to = (x ^ dx, y ^ dy, c ^ dc)
            cps.append(pltpu.make_async_remote_copy(
                src_ref=slots.at[me], dst_ref=slots.at[me], send_sem=send_sems.at[k - 1],
                recv_sem=recv_sems.at[k - 1], device_id=to, device_id_type=MESH))
        for cp in cps:
            cp.start()
        for cp in cps:
            cp.wait()
        acc = slots[0]
        for i in range(1, 8):
            acc = acc + slots[i]
        out_ref[...] = acc

    vm = pl.BlockSpec(memory_space=pltpu.VMEM)
    return pl.pallas_call(
        body, name="allreduce_small",
        in_specs=[vm] * 5, out_specs=vm,
        out_shape=jax.ShapeDtypeStruct((8, D_MODEL), F32),
        scratch_shapes=[pltpu.VMEM((8, 8, D_MODEL), F32), pltpu.SemaphoreType.DMA((7,)), pltpu.SemaphoreType.DMA((7,))],
    )(g_nw, g_fw, g_hgw, g_lbl, loss8)


def _rope_tables(S):
    inv_freq = (np.float32(1.0) / np.power(np.float32(ROPE_THETA), np.arange(ROPE_HALF, dtype=np.float32) / np.float32(ROPE_HALF))).astype(np.float32)
    ang = (np.arange(S, dtype=np.float32)[:, None] * inv_freq[None, :]).astype(np.float32)
    cos, sin = np.cos(ang).astype(np.float32), np.sin(ang).astype(np.float32)
    cos2 = np.concatenate([cos, cos, cos, cos], axis=-1)
    sin2 = np.concatenate([-sin, sin, -sin, sin], axis=-1)
    return jnp.asarray(cos2), jnp.asarray(sin2)


def _local_step(x2, tgt2, norm_w, w_in_full, lbl, hg_norm_w, w_out4, fnw):
    S = x2.shape[0]
    cos2, sin2 = _rope_tables(S)
    w_out_full = w_out4.reshape(D_MODEL, D_MODEL)
    p0, p1, qkv, qkv4, qkv16, z_at, u = _inproj(x2, norm_w, w_in_full, cos2, sin2)
    o_hg, states = _hg_fwd(p0, p1, lbl)
    qkv_ds = [qkv.reshape(1, S, 3 * AT_W), qkv4, qkv16]
    os_d, ls_d = zip(*[_att_fwd(q) for q in qkv_ds])
    o_at, lse, mix_at = _att_combine(os_d, ls_d, z_at)
    dh, do_hg, dz_hg, dm_at, g_wout, g_fw, g_hgw, loss8 = _outproj(x2, tgt2, mix_at, w_out_full, fnw, o_hg, p1, hg_norm_w)
    dqr, dfl, dv_hg, g_lbl = _hg_bwd(p0, p1, do_hg, states, lbl)
    do1, do4, do16, dl1, dl4, dl16, dz_at = _att_gate_bwd(dm_at, o_at, lse, z_at)
    do_ds = [do1.reshape(1, S, AT_W), do4, do16]
    dl_ds = [dl1.reshape(1, S, STAT_W), dl4, dl16]
    dqkvs = [_att_bwd(q, a, b) for q, a, b in zip(qkv_ds, do_ds, dl_ds)]
    dq_at, dk_at, dv_at = _att_bwd_combine(dqkvs, cos2, sin2)
    dps = [dqr, dfl, dv_hg, dz_hg, dq_at, dk_at, dv_at, dz_at]
    return loss8, dps, u, dh, g_lbl, g_hgw, g_wout, g_fw


def kernel(x, norm_w, w_in, hgrn_lb_logits, hg_norm_w, w_out, final_norm_w, loss_target, m_norm_w, m_w_in, m_hgrn_lb_logits, m_hg_norm_w, m_w_out, m_final_norm_w, v_norm_w, v_w_in, v_hgrn_lb_logits, v_hg_norm_w, v_w_out, v_final_norm_w):
    S = x.shape[1]
    w_in_full, w_out_full = _gather_weights(w_in[0], w_out[0])
    loss8, dps, u, dh, g_lbl, g_hgw, g_wout, g_fw = _local_step(
        x[0], loss_target[0], norm_w, w_in_full, hgrn_lb_logits, hg_norm_w,
        w_out_full, final_norm_w.reshape(1, D_MODEL))

    cidx = lax.axis_index("c").astype(jnp.int32).reshape(1)
    g_win = _inproj_bwd_w(u, dps)
    g_in4 = g_win.reshape(4, 2, 512, 1024)
    g_out4 = g_wout.reshape(4, 2, 128, D_MODEL)
    r_in, r_out = _swap_halves(g_in4, g_out4)
    cs_in = _add_half(g_in4, r_in, cidx, "add_half_in")
    cs_out = _add_half(g_out4, r_out, cidx, "add_half_out")
    *started, token = _exchange_start(cs_in, cs_out)
    grad_x, g_nw = _inproj_bwd_x(dps, w_in_full, x[0], norm_w, dh, after=(token,))
    cs_in, cs_out, x_in, x_out = _exchange_wait(*started, g_nw)
    chip_idx = (2 * lax.axis_index("x") + lax.axis_index("y")).astype(jnp.int32).reshape(1)
    h_in = _sum_chips(cs_in, x_in, chip_idx, "sum_chips_in")
    h_out = _sum_chips(cs_out, x_out, chip_idx, "sum_chips_out")
    s_in, s_out = _swap_reduced(h_in, h_out)

    red = _allreduce_small(g_nw, g_fw, g_hgw, g_lbl, loss8)
    as_row = lambda a: a.reshape(1, D_MODEL)
    grads, deltas, new_m, new_v, loss = _adamw_small(
        red, (norm_w, hgrn_lb_logits, hg_norm_w, as_row(final_norm_w)),
        (m_norm_w, m_hgrn_lb_logits, m_hg_norm_w, as_row(m_final_norm_w)),
        (v_norm_w, v_hgrn_lb_logits, v_hg_norm_w, as_row(v_final_norm_w)))
    grad_norm_w, grad_lbl, grad_hg_norm_w, grad_final_norm_w = grads
    d_nw, d_lbl, d_hgw, d_fw = deltas
    m_nw, m_lbl, m_hgw, m_fw = new_m
    v_nw, v_lbl, v_hgw, v_fw = new_v
    grad_w_in, d_win, m_win, v_win = _adamw_halves(w_in[0], h_in, s_in, m_w_in[0], v_w_in[0], cidx, "adamw_w_in")
    grad_w_out, d_wout, m_wout, v_wout = _adamw_halves(w_out[0], h_out, s_out, m_w_out[0], v_w_out[0], cidx, "adamw_w_out")
    e1 = lambda a: a[None]
    flat = lambda a: a.reshape(D_MODEL)
    return (loss.reshape(()), grad_x[None], grad_norm_w, e1(grad_w_in), grad_lbl, grad_hg_norm_w, e1(grad_w_out),
            flat(grad_final_norm_w),
            d_nw, e1(d_win), d_lbl, d_hgw, e1(d_wout), flat(d_fw),
            m_nw, e1(m_win), m_lbl, m_hgw, e1(m_wout), flat(m_fw),
            v_nw, e1(v_win), v_lbl, v_hgw, e1(v_wout), flat(v_fw))
```

```python
import jax
import jax.numpy as jnp
import numpy as np
from jax import lax
from jax.experimental import pallas as pl
from jax.experimental.pallas import tpu as pltpu

F32 = jnp.float32
BF16 = jnp.bfloat16
MESH = pl.DeviceIdType.MESH

D_MODEL = 1024
HG_HEADS = 4
HG_HEAD = 128
HG_W = HG_HEADS * HG_HEAD
AT_HEAD = 64
AT_W = 512
HEAD_PAIR = 2 * AT_HEAD
ROPE_HALF = 32
ROPE_THETA = 10000.0
EPS = 1e-6
CHUNK = 128
LEVELS = (64, 32, 16, 8, 4, 2, 1)
DIAG = 1
SUBLANES = 8
ATT_BLK = 128
ATT_QB = 4
ATT_SCALE = 0.125
STAT_W = 128
STAT_LANES = 16
STAT_LSE_LANE = 8
NEG = -1e30
VMEM_LIMIT = 56 * 1024 * 1024
MIX_TM = 1024
PROJ_TM = 512

ADAM_LR = 0.001
ADAM_B1 = 0.9
ADAM_B2 = 0.999
ADAM_EPS = 1e-08
ADAM_WD = 0.01
ADAM_STEP = 10


def _iota(shape, dim):
    return lax.broadcasted_iota(jnp.int32, shape, dim)


def _dot(a, b):
    return jnp.dot(a, b, preferred_element_type=F32)


def _dot_nt(a, b):
    return lax.dot_general(a, b, (((1,), (1,)), ((), ())), preferred_element_type=F32)


def _dot_tn(a, b):
    return lax.dot_general(a, b, (((0,), (0,)), ((), ())), preferred_element_type=F32)


def _sigmoid(v):
    return 0.5 * jnp.tanh(0.5 * v) + 0.5


def _params(sem=None, vmem=VMEM_LIMIT):
    return pltpu.CompilerParams(dimension_semantics=sem, vmem_limit_bytes=vmem)


def _full(shape):
    n = len(shape)
    return pl.BlockSpec(shape, lambda *_: (0,) * n)


def _rope_rot(y):
    n = y.shape[1]
    first = (_iota(y.shape, 1) & (2 * ROPE_HALF - 1)) < ROPE_HALF
    return jnp.where(first, pltpu.roll(y, n - ROPE_HALF, 1), pltpu.roll(y, ROPE_HALF, 1))


def _dil_spec(d, tm, width):
    return pl.BlockSpec((d, tm // d, width), lambda s: (0, s, 0))


LANES = 128


def _slab_scratch(tm, width):
    return pltpu.VMEM((width // LANES, tm, LANES), F32)


def _to_slabs(v, slabs_ref):
    for j in range(slabs_ref.shape[0]):
        slabs_ref[j] = v[:, LANES * j: LANES * (j + 1)]


def _from_slabs(slabs_ref):
    return jnp.concatenate([slabs_ref[j] for j in range(slabs_ref.shape[0])], axis=1)


def _split_residues(slabs_ref, dst_ref, d, dtype):
    nslab, tm, _ = slabs_ref.shape
    for r in range(d):
        for j in range(nslab):
            dst_ref[r, :, LANES * j: LANES * (j + 1)] = slabs_ref[j, pl.ds(r, tm // d, stride=d), :].astype(dtype)


def _merge_residues(src_ref, slabs_ref, d):
    nslab, tm, _ = slabs_ref.shape
    for r in range(d):
        for j in range(nslab):
            slabs_ref[j, pl.ds(r, tm // d, stride=d), :] = src_ref[r, :, LANES * j: LANES * (j + 1)].astype(F32)


PERM_ROWS = 256


def _perm_matrix(d, inverse=False):
    n = PERM_ROWS // d
    t = np.arange(PERM_ROWS)
    p = np.zeros((PERM_ROWS, PERM_ROWS), np.float32)
    p[t, (t % d) * n + t // d] = 1.0
    return jnp.asarray(p.T if inverse else p, BF16)


def _merge_bf16(src_ref, perm_ref, d):
    n = PERM_ROWS // d
    tm = src_ref.shape[1] * d
    outs = []
    for b in range(tm // PERM_ROWS):
        flat = jnp.concatenate([src_ref[r, n * b: n * (b + 1), :] for r in range(d)], axis=0)
        outs.append(_dot(perm_ref[...], flat))
    return jnp.concatenate(outs, axis=0)


def _split_bf16(v, perm_t_ref, dst_ref, d):
    n = PERM_ROWS // d
    for b in range(v.shape[0] // PERM_ROWS):
        flat = _dot(perm_t_ref[...], v[PERM_ROWS * b: PERM_ROWS * (b + 1), :]).astype(BF16)
        for r in range(d):
            dst_ref[r, n * b: n * (b + 1), :] = flat[n * r: n * (r + 1), :]


def _inproj(x2, norm_w, w_in_full, cos2, sin2):
    S = x2.shape[0]
    tm = PROJ_TM

    def body(x_ref, nw_ref, w_ref, cos_ref, sin_ref, t4_ref, t16_ref, p0_ref, p1_ref, qkv_ref, qkv4_ref, qkv16_ref, z_ref,
             u_ref):
        x = x_ref[...]
        r = lax.rsqrt(jnp.mean(x * x, axis=-1, keepdims=True) + EPS)
        u = x * r * nw_ref[...]
        ub = u.astype(BF16)
        u_ref[...] = ub
        p0_ref[...] = _dot(ub, w_ref[0])
        p1_ref[...] = _dot(ub, w_ref[1])
        y2 = _dot(ub, w_ref[2])
        cosf = jnp.tile(cos_ref[...], (1, 8))
        sinf = jnp.tile(sin_ref[...], (1, 8))
        y3 = _dot(ub, w_ref[3])
        z_ref[...] = y3[:, AT_W:]
        qkv = jnp.concatenate([y2 * cosf + _rope_rot(y2) * sinf, y3[:, :AT_W]], axis=1).astype(BF16)
        qkv_ref[...] = qkv
        _split_bf16(qkv, t4_ref, qkv4_ref, 4)
        _split_bf16(qkv, t16_ref, qkv16_ref, 16)

    row = lambda w: pl.BlockSpec((tm, w), lambda s: (s, 0))
    qkv_w = 3 * AT_W
    perm = _full((PERM_ROWS, PERM_ROWS))
    return pl.pallas_call(
        body, name="inproj", grid=(S // tm,),
        in_specs=[row(D_MODEL), _full((1, D_MODEL)), _full((4, D_MODEL, 1024)), row(128), row(128), perm, perm],
        out_specs=[row(1024), row(1024), row(qkv_w), _dil_spec(4, tm, qkv_w), _dil_spec(16, tm, qkv_w), row(AT_W),
                   row(D_MODEL)],
        out_shape=[jax.ShapeDtypeStruct((S, 1024), F32), jax.ShapeDtypeStruct((S, 1024), F32),
                   jax.ShapeDtypeStruct((S, qkv_w), BF16), jax.ShapeDtypeStruct((4, S // 4, qkv_w), BF16),
                   jax.ShapeDtypeStruct((16, S // 16, qkv_w), BF16), jax.ShapeDtypeStruct((S, AT_W), F32),
                   jax.ShapeDtypeStruct((S, D_MODEL), BF16)],
        compiler_params=_params(("parallel",)),
    )(x2, norm_w, w_in_full, cos2, sin2, _perm_matrix(4, True), _perm_matrix(16, True))


HG_HPS = 4
HG_CPS = 4
N_LEV = len(LEVELS)


def _hg_const_arrays():
    r = np.arange(CHUNK)[:, None]
    c = np.arange(CHUNK)[None, :]
    tris = np.stack([r >= c, r <= c])
    lm = [((r // (2 * m)) == (c // (2 * m))) & (r % (2 * m) >= m) & (c % (2 * m) < m) for m in LEVELS]
    dm = [(c == r - dl) & (r % DIAG >= dl) for dl in range(DIAG)]
    masks = np.stack(lm + dm)
    return jnp.asarray(tris, BF16), jnp.asarray(masks, F32)


def _split2(a):
    hi = a.astype(BF16)
    return hi, (a - hi.astype(F32)).astype(BF16)


def _dot3(a, b, dot=_dot):
    ah, al = _split2(a)
    bh, bl = _split2(b)
    n = b.shape[1]
    p = dot(ah, jnp.concatenate([bh, bl], axis=1))
    return (p[:, :n] + p[:, n:]) + dot(al, bh)


def _split3(a):
    a1 = a.astype(BF16)
    r1 = a - a1.astype(F32)
    a2 = r1.astype(BF16)
    return a1, a2, (r1 - a2.astype(F32)).astype(BF16)


def _tri_dot(tri, a):
    n = a.shape[1]
    p = _dot(tri, jnp.concatenate(_split3(a), axis=1))
    return (p[:, :n] + p[:, n:2 * n]) + p[:, 2 * n:]


def _dot_sel(a, sel):
    a1, a2, a3 = _split3(a)
    return (_dot(a1, sel) + _dot(a2, sel)) + _dot(a3, sel)


def _rowsum(t):
    return _dot(t.astype(BF16), jnp.ones((t.shape[1], t.shape[1]), BF16))


def _level_refs(b):
    refs = []
    pos = _iota(b.shape, 0)
    for m in LEVELS:
        if 2 * m >= SUBLANES:
            parts = [jnp.broadcast_to(b[r0 + m - 1: r0 + m, :], (2 * m, b.shape[1])) for r0 in range(0, CHUNK, 2 * m)]
            refs.append(parts[0] if len(parts) == 1 else jnp.concatenate(parts, axis=0))
        else:
            p = pos & (2 * m - 1)
            ref = b
            for off in range(-(m - 1), m + 1):
                if off != 0:
                    ref = jnp.where(p == m - 1 + off, pltpu.roll(b, off % CHUNK, 0), ref)
            refs.append(ref)
    return refs


def _hg_lb(lbl_ref):
    l0 = lbl_ref[0:1, :]
    l1 = lbl_ref[1:2, :]
    mx = jnp.maximum(l0, l1)
    e0 = jnp.exp(l0 - mx)
    e1 = jnp.exp(l1 - mx)
    p0 = e0 / (e0 + e1)
    lb = jnp.clip(p0, 1e-6, 1.0 - 1e-6)
    inside = (p0 >= 1e-6) & (p0 <= 1.0 - 1e-6)
    dlb_dl0 = jnp.where(inside, p0 * (e1 / (e0 + e1)), 0.0)
    return lb, dlb_dl0


def _hg_gates(qr, fl, lb):
    sig = _sigmoid(fl)
    f = lb + (1.0 - lb) * sig
    g = jnp.log(f)
    k = (1.0 - lb) * (1.0 - sig)
    sq = _sigmoid(qr)
    q = qr * sq
    return sig, f, g, k, sq, q


def _neg_abs(v):
    bits = lax.bitcast_convert_type(v, jnp.uint32) | jnp.uint32(0x80000000)
    return lax.bitcast_convert_type(bits, F32)


def _hg_levels(q, k, b, mk_ref):
    refs = _level_refs(b)
    a = jnp.zeros((CHUNK, CHUNK), F32)
    es, qts, kts = [], [], []
    for i in range(N_LEV):
        e = jnp.exp(_neg_abs(b - refs[i]))
        qt = (q * e).astype(BF16)
        kt = (k * e).astype(BF16)
        a = a + _dot_nt(qt, kt) * mk_ref[i]
        es.append(e); qts.append(qt); kts.append(kt)
    return a, es, qts, kts


def _hg_specs(nc, rev):
    cc = (lambda c: nc - 1 - c) if rev else (lambda c: c)
    w = HG_HEAD * HG_HPS
    blk = lambda off: pl.BlockSpec((HG_CPS * CHUNK, w), lambda h, c: (cc(c), h + off))
    lb2 = pl.BlockSpec((2, w), lambda h, c: (0, h))
    st = pl.BlockSpec((HG_CPS, HG_HPS, HG_HEAD, HG_HEAD), lambda h, c: (cc(c), h, 0, 0))
    consts = [_full((2, CHUNK, CHUNK)), _full((N_LEV + DIAG, CHUNK, CHUNK))]
    return blk, lb2, st, consts


def _hg_fwd(p0, p1, lbl):
    S = p0.shape[0]
    nc = S // (HG_CPS * CHUNK)
    ng = HG_HEADS // HG_HPS

    def body(qr_ref, fl_ref, v_ref, lbl_ref, tri_ref, mk_ref, o_ref, st_ref, state):
        c = pl.program_id(1)

        @pl.when(c == 0)
        def _():
            state[...] = jnp.zeros_like(state)

        lb_all, _ = _hg_lb(lbl_ref)
        heads = [slice(HG_HEAD * hh, HG_HEAD * (hh + 1)) for hh in range(HG_HPS)]
        for j in range(HG_CPS):
            rows = slice(CHUNK * j, CHUNK * (j + 1))
            qs, ks, bs, mats = [], [], [], []
            for sl in heads:
                _, _, g, k, _, q = _hg_gates(qr_ref[rows, sl], fl_ref[rows, sl], lb_all[:, sl])
                qs.append(q); ks.append(k); bs.append(_tri_dot(tri_ref[0], g))
            for hh in range(HG_HPS):
                a, _, _, _ = _hg_levels(qs[hh], ks[hh], bs[hh], mk_ref)
                mats.append(a + _rowsum(qs[hh] * ks[hh]) * mk_ref[N_LEV])
            for hh, sl in enumerate(heads):
                q, k, b, a, v = qs[hh], ks[hh], bs[hh], mats[hh], v_ref[rows, sl]
                b_last = b[CHUNK - 1: CHUNK, :]
                st = state[hh]
                st_ref[j, hh] = st
                o = _dot_nt((q * jnp.exp(b)).astype(BF16), st.astype(BF16)) + _dot(a.astype(BF16), v.astype(BF16))
                state[hh] = st * jnp.exp(b_last) + _dot3(v, k * jnp.exp(b_last - b), _dot_tn)
                o_ref[rows, sl] = o

    blk, lb2, st_spec, consts = _hg_specs(nc, False)
    tris, masks = _hg_const_arrays()
    return pl.pallas_call(
        body, name="hg_fwd", grid=(ng, nc),
        in_specs=[blk(0), blk(ng), blk(0), lb2] + consts,
        out_specs=[blk(0), st_spec],
        out_shape=[jax.ShapeDtypeStruct((S, HG_W), F32), jax.ShapeDtypeStruct((S // CHUNK, HG_HEADS, HG_HEAD, HG_HEAD), F32)],
        scratch_shapes=[pltpu.VMEM((HG_HPS, HG_HEAD, HG_HEAD), F32)],
        compiler_params=_params(("parallel", "arbitrary")),
    )(p0, p0, p1, lbl, tris, masks)


def _hg_bwd(p0, p1, do_hg, states, lbl):
    S = p0.shape[0]
    nc = S // (HG_CPS * CHUNK)
    ng = HG_HEADS // HG_HPS
    w = HG_HEAD * HG_HPS

    def body(qr_ref, fl_ref, v_ref, do_ref, st_ref, lbl_ref, tri_ref, mk_ref, mkb_ref,
             dqr_ref, dfl_ref, dv_ref, glbl_ref, dstate, carry, acc_lb):
        c = pl.program_id(1)

        @pl.when(c == 0)
        def _():
            dstate[...] = jnp.zeros_like(dstate)
            carry[...] = jnp.zeros_like(carry)
            acc_lb[...] = jnp.zeros_like(acc_lb)

        lb_all, dlb_dl0 = _hg_lb(lbl_ref)
        heads = [slice(HG_HEAD * hh, HG_HEAD * (hh + 1)) for hh in range(HG_HPS)]
        diag_mask = mk_ref[N_LEV]
        for j in reversed(range(HG_CPS)):
            rows = slice(CHUNK * j, CHUNK * (j + 1))
            gates, bs, dos = [], [], []
            for sl in heads:
                gt = _hg_gates(qr_ref[rows, sl], fl_ref[rows, sl], lb_all[:, sl])
                gates.append(gt)
                bs.append(_tri_dot(tri_ref[0], gt[2]))
            for sl in heads:
                dos.append(do_ref[rows, sl])
            inter = []
            for hh, sl in enumerate(heads):
                _, _, _, k, _, q = gates[hh]
                b, do, v = bs[hh], dos[hh], v_ref[rows, sl]
                b_last = b[CHUNK - 1: CHUNK, :]
                eb = jnp.exp(b)
                edec = jnp.exp(b_last - b)
                dst = dstate[hh]
                dq = _dot3(do, st_ref[j, hh]) * eb
                dk = _dot3(v, dst) * edec
                da = _dot_nt(do.astype(BF16), v.astype(BF16))
                dv_state = _dot_nt((k * edec).astype(BF16), dst.astype(BF16))
                dstate[hh] = dst * jnp.exp(b_last) + _dot3(do, q * eb, _dot_tn)
                inter.append((dq, dk, da, dv_state))
            for hh, sl in enumerate(heads):
                sig, f, _, k, sq, q = gates[hh]
                dq, dk, da, dv_state = inter[hh]
                b = bs[hh]
                db = q * dq - k * dk
                dab = da.astype(BF16)
                refs = _level_refs(b)
                a = _rowsum(q * k) * diag_mask
                for i in range(N_LEV):
                    e = jnp.exp(_neg_abs(b - refs[i]))
                    qt = (q * e).astype(BF16)
                    kt = (k * e).astype(BF16)
                    a = a + _dot_nt(qt, kt) * mk_ref[i]
                    dam = dab * mkb_ref[i]
                    gq = _dot(dam, kt)
                    gk = _dot_tn(dam, qt)
                    dq = dq + e * gq
                    dk = dk + e * gk
                    db = db + (qt.astype(F32) * gq - kt.astype(F32) * gk)
                dc = _rowsum(da * diag_mask)
                dq = dq + dc * k
                dk = dk + dc * q
                dv_ref[rows, sl] = (_dot_tn(a.astype(BF16), dos[hh].astype(BF16)) + dv_state).astype(BF16)
                dg = _tri_dot(tri_ref[1], db) + carry[0:1, sl]
                carry[0:1, sl] += jnp.sum(db, axis=0, keepdims=True)
                lb = lb_all[:, sl]
                qr = qr_ref[rows, sl]
                t = dg / f - dk
                dfl_ref[rows, sl] = (t * (1.0 - lb) * sig * (1.0 - sig)).astype(BF16)
                acc_lb[0:1, sl] += jnp.sum(t * (1.0 - sig), axis=0, keepdims=True)
                dqr_ref[rows, sl] = (dq * (sq * (1.0 + qr * (1.0 - sq)))).astype(BF16)

        @pl.when(c == nc - 1)
        def _():
            gl0 = acc_lb[0:1, :] * dlb_dl0
            glbl_ref[0:1, :] = gl0
            glbl_ref[1:2, :] = -gl0

    blk, lb2, st_spec, consts = _hg_specs(nc, True)
    tris, masks = _hg_const_arrays()
    act = jax.ShapeDtypeStruct((S, HG_W), BF16)
    return pl.pallas_call(
        body, name="hg_bwd", grid=(ng, nc),
        in_specs=[blk(0), blk(ng), blk(0), blk(0), st_spec, lb2] + consts + consts[1:],
        out_specs=[blk(0), blk(0), blk(0), lb2],
        out_shape=[act, act, act, jax.ShapeDtypeStruct((2, HG_W), F32)],
        scratch_shapes=[pltpu.VMEM((HG_HPS, HG_HEAD, HG_HEAD), F32), pltpu.VMEM((8, w), F32), pltpu.VMEM((8, w), F32)],
        compiler_params=_params(("parallel", "arbitrary")),
    )(p0, p0, p1, do_hg, states, lbl, tris, masks, masks.astype(BF16))


def _stat_head(lane):
    return lane >> 4


def _stat_lane(lane):
    return lane & (STAT_LANES - 1)


def _o_head(lane):
    return lane >> 6


def _att_bias():
    qi = np.arange(ATT_BLK)[:, None]
    kj = np.arange(2 * ATT_BLK)[None, :]
    band = (kj >= qi) & (kj <= qi + ATT_BLK)
    qm = np.stack([band & (kj >= ATT_BLK), band])
    cur = (kj < ATT_BLK) & (qi <= kj)
    km = np.stack([cur, cur | ((kj >= ATT_BLK) & (qi >= kj - ATT_BLK))])
    to_bias = lambda m: jnp.asarray(np.where(m, 0.0, NEG), F32)
    return to_bias(qm), to_bias(km)


def _att_fwd(qkv_d):
    d, L, _ = qkv_d.shape
    qb = ATT_QB
    rows = qb * ATT_BLK

    def body(q_ref, kp_ref, kc_ref, vp_ref, vc_ref, bias_ref, o_ref, lse_ref, s_scr, p_scr):
        first = _iota((ATT_BLK, HEAD_PAIR), 1) < AT_HEAD
        head_of_lane = _stat_head(_iota((ATT_BLK, STAT_W), 1))
        pairs = [slice(HEAD_PAIR * hp, HEAD_PAIR * (hp + 1)) for hp in range(4)]
        blk = lambda i: slice(ATT_BLK * i, ATT_BLK * (i + 1))

        def keys(i, sl, prev_ref, cur_ref):
            before = prev_ref[0, :, sl] if i == 0 else cur_ref[0, blk(i - 1), sl]
            return jnp.concatenate([before, cur_ref[0, blk(i), sl]], axis=0)

        for i in range(qb):
            for hp, sl in enumerate(pairs):
                q2 = q_ref[0, blk(i), sl] * ATT_SCALE
                zero = jnp.zeros_like(q2)
                qs = jnp.concatenate([jnp.where(first, q2, zero), jnp.where(first, zero, q2)], axis=0)
                s_scr[4 * i + hp] = _dot_nt(qs, keys(i, sl, kp_ref, kc_ref))
        stats = []
        for i in range(qb):
            bias = bias_ref[jnp.minimum(pl.program_id(1), 1)] if i == 0 else bias_ref[1]
            bias2 = jnp.concatenate([bias, bias], axis=0)
            for hp in range(4):
                s = s_scr[4 * i + hp] + bias2
                m = jnp.max(s, axis=-1, keepdims=True)
                p = jnp.exp(s - m)
                l = jnp.sum(p, axis=-1, keepdims=True)
                p_scr[4 * i + hp] = p.astype(BF16)
                stats.append((l, m + jnp.log(l)))
        for i in range(qb):
            lse_blk = jnp.zeros((ATT_BLK, STAT_W), F32)
            for hp, sl in enumerate(pairs):
                l, lse = stats[4 * i + hp]
                o = _dot(p_scr[4 * i + hp], keys(i, sl, vp_ref, vc_ref)) / l
                o_ref[0, blk(i), sl] = jnp.where(first, o[:ATT_BLK], o[ATT_BLK:]).astype(BF16)
                lse_blk = jnp.where(head_of_lane == 2 * hp, lse[:ATT_BLK],
                                    jnp.where(head_of_lane == 2 * hp + 1, lse[ATT_BLK:], lse_blk))
            lse_ref[0, blk(i), :] = lse_blk

    cur = lambda j: pl.BlockSpec((1, rows, AT_W), lambda r, n: (r, n, j))
    prev = lambda j: pl.BlockSpec((1, ATT_BLK, AT_W), lambda r, n: (r, jnp.maximum(qb * n - 1, 0), j))
    return pl.pallas_call(
        body, name=f"att_fwd_d{d}", grid=(d, L // rows),
        in_specs=[cur(0), prev(1), cur(1), prev(2), cur(2), _full((2, ATT_BLK, 2 * ATT_BLK))],
        out_specs=[pl.BlockSpec((1, rows, AT_W), lambda r, n: (r, n, 0)), pl.BlockSpec((1, rows, STAT_W), lambda r, n: (r, n, 0))],
        out_shape=[jax.ShapeDtypeStruct((d, L, AT_W), BF16), jax.ShapeDtypeStruct((d, L, STAT_W), F32)],
        scratch_shapes=[pltpu.VMEM((4 * qb, 2 * ATT_BLK, 2 * ATT_BLK), F32), pltpu.VMEM((4 * qb, 2 * ATT_BLK, 2 * ATT_BLK), BF16)],
        compiler_params=_params(("parallel", "parallel")),
    )(qkv_d, qkv_d, qkv_d, qkv_d, qkv_d, _att_bias()[0])


def _att_combine(os_d, ls_d, z_at):
    S = z_at.shape[0]
    tm = MIX_TM

    def body(oa_ref, ob4_ref, oc16_ref, la_ref, lb4_ref, lc16_ref, p4_ref, p16_ref, z_ref,
             oat_ref, lse_ref, mix_ref, lb_scr, lc_scr):
        _merge_residues(lb4_ref, lb_scr, 4)
        _merge_residues(lc16_ref, lc_scr, 16)
        ls = (la_ref[...], lb_scr[0], lc_scr[0])
        mx = jnp.maximum(jnp.maximum(ls[0], ls[1]), ls[2])
        es = [jnp.exp(l - mx) for l in ls]
        zs = es[0] + es[1] + es[2]
        lse_ref[...] = mx + jnp.log(zs)
        spread = ((_o_head(_iota((STAT_W, AT_W), 1)) == _stat_head(_iota((STAT_W, AT_W), 0)))
                  & (_stat_lane(_iota((STAT_W, AT_W), 0)) == 0)).astype(BF16)
        os_ = (oa_ref[...].astype(F32), _merge_bf16(ob4_ref, p4_ref, 4), _merge_bf16(oc16_ref, p16_ref, 16))
        o = jnp.zeros((tm, AT_W), F32)
        for e, oi in zip(es, os_):
            hi, lo = _split2(e / zs)
            o = o + (_dot(hi, spread) + _dot(lo, spread)) * oi
        oat_ref[...] = o
        z = z_ref[...]
        mixed = o * (z * _sigmoid(z))
        mix_ref[...] = mixed.astype(BF16)

    row = lambda w: pl.BlockSpec((tm, w), lambda s: (s, 0))
    return pl.pallas_call(
        body, name="att_combine", grid=(S // tm,),
        in_specs=[row(AT_W), _dil_spec(4, tm, AT_W), _dil_spec(16, tm, AT_W),
                  row(STAT_W), _dil_spec(4, tm, STAT_W), _dil_spec(16, tm, STAT_W),
                  _full((PERM_ROWS, PERM_ROWS)), _full((PERM_ROWS, PERM_ROWS)), row(AT_W)],
        out_specs=[row(AT_W), row(STAT_W), row(AT_W)],
        out_shape=[jax.ShapeDtypeStruct((S, AT_W), F32), jax.ShapeDtypeStruct((S, STAT_W), F32),
                   jax.ShapeDtypeStruct((S, AT_W), BF16)],
        scratch_shapes=[_slab_scratch(tm, STAT_W), _slab_scratch(tm, STAT_W)],
        compiler_params=_params(("parallel",)),
    )(os_d[0].reshape(S, AT_W), os_d[1], os_d[2], ls_d[0].reshape(S, STAT_W), ls_d[1], ls_d[2],
      _perm_matrix(4), _perm_matrix(16), z_at)


def _att_gate_bwd(dm_at, o_at, lse, z_at):
    S = z_at.shape[0]
    tm = MIX_TM

    def body(dm_ref, o_ref, l_ref, z_ref, t4_ref, t16_ref, do_ref, do4_ref, do16_ref, dl_ref, dl4_ref, dl16_ref, dz_ref,
             dl_scr):
        o = o_ref[...]
        z = z_ref[...]
        dm = dm_ref[...]
        sz = _sigmoid(z)
        dz_ref[...] = (dm * o * (sz * (1.0 + z * (1.0 - sz)))).astype(BF16)
        do = dm * (z * sz)
        dob = do.astype(BF16)
        do_ref[...] = dob
        gather = (_o_head(_iota((AT_W, STAT_W), 0)) == _stat_head(_iota((AT_W, STAT_W), 1))).astype(BF16)
        dl = jnp.where(_stat_lane(_iota((tm, STAT_W), 1)) < STAT_LSE_LANE, _dot_sel(do * o, gather), l_ref[...])
        dl_ref[...] = dl
        _split_bf16(dob, t4_ref, do4_ref, 4)
        _split_bf16(dob, t16_ref, do16_ref, 16)
        _to_slabs(dl, dl_scr)
        _split_residues(dl_scr, dl4_ref, 4, F32)
        _split_residues(dl_scr, dl16_ref, 16, F32)

    row = lambda w: pl.BlockSpec((tm, w), lambda s: (s, 0))
    sds = jax.ShapeDtypeStruct
    return pl.pallas_call(
        body, name="att_gate_bwd", grid=(S // tm,),
        in_specs=[row(AT_W), row(AT_W), row(STAT_W), row(AT_W)] + [_full((PERM_ROWS, PERM_ROWS))] * 2,
        out_specs=[row(AT_W), _dil_spec(4, tm, AT_W), _dil_spec(16, tm, AT_W),
                   row(STAT_W), _dil_spec(4, tm, STAT_W), _dil_spec(16, tm, STAT_W), row(AT_W)],
        out_shape=[sds((S, AT_W), BF16), sds((4, S // 4, AT_W), BF16), sds((16, S // 16, AT_W), BF16),
                   sds((S, STAT_W), F32), sds((4, S // 4, STAT_W), F32), sds((16, S // 16, STAT_W), F32),
                   sds((S, AT_W), BF16)],
        scratch_shapes=[_slab_scratch(tm, STAT_W)],
        compiler_params=_params(("parallel",)),
    )(dm_at, o_at, lse, z_at, _perm_matrix(4, True), _perm_matrix(16, True))


def _att_bwd(qkv_d, do_d, dl_d):
    d, L, _ = qkv_d.shape
    nb = L // ATT_BLK
    qb = ATT_QB
    rows = qb * ATT_BLK
    nsteps = L // rows

    def body(qc_ref, qn_ref, kp_ref, kc_ref, vp_ref, vc_ref, ac_ref, an_ref, lc_ref, ln_ref, bq_ref, bk_ref, dqkv_ref,
             s_scr, dp_scr, st_scr, dpt_scr, ds_scr, pt_scr, dst_scr):
        n = pl.program_id(1)
        first = _iota((ATT_BLK, HEAD_PAIR), 1) < AT_HEAD
        pairs = [slice(HEAD_PAIR * hp, HEAD_PAIR * (hp + 1)) for hp in range(4)]
        blk = lambda i: slice(ATT_BLK * i, ATT_BLK * (i + 1))

        def stack(t):
            zero = jnp.zeros_like(t)
            return jnp.concatenate([jnp.where(first, t, zero), jnp.where(first, zero, t)], axis=0)

        def unstack(t2):
            return jnp.where(first, t2[:ATT_BLK], t2[ATT_BLK:])

        def with_prev(i, sl, prev_ref, cur_ref):
            before = prev_ref[0, :, sl] if i == 0 else cur_ref[0, blk(i - 1), sl]
            return jnp.concatenate([before, cur_ref[0, blk(i), sl]], axis=0)

        def with_next(i, sl, cur_ref, next_ref):
            after = next_ref[0, :, sl] if i == qb - 1 else cur_ref[0, blk(i + 1), sl]
            return jnp.concatenate([cur_ref[0, blk(i), sl], after], axis=0)

        for i in range(qb):
            for hp, sl in enumerate(pairs):
                j = 4 * i + hp
                s_scr[j] = _dot_nt(stack(qc_ref[0, blk(i), sl] * ATT_SCALE), with_prev(i, sl, kp_ref, kc_ref))
                dp_scr[j] = _dot_nt(stack(ac_ref[0, blk(i), sl]), with_prev(i, sl, vp_ref, vc_ref))
                st_scr[j] = _dot_nt(stack(kc_ref[0, blk(i), sl] * ATT_SCALE), with_next(i, sl, qc_ref, qn_ref))
                dpt_scr[j] = _dot_nt(stack(vc_ref[0, blk(i), sl]), with_next(i, sl, ac_ref, an_ref))
        for i in range(qb):
            bias = bq_ref[jnp.minimum(n, 1)] if i == 0 else bq_ref[1]
            bias_t = bk_ref[jnp.minimum(nsteps - 1 - n, 1)] if i == qb - 1 else bk_ref[1]
            bias2 = jnp.concatenate([bias, bias], axis=0)
            bias_t2 = jnp.concatenate([bias_t, bias_t], axis=0)
            dl_c = lc_ref[0, blk(i), :]
            dl_t = with_next(i, slice(None), lc_ref, ln_ref).T
            for hp in range(4):
                j = 4 * i + hp
                at = [STAT_LANES * (2 * hp), STAT_LANES * (2 * hp + 1)]
                col = lambda t, o: jnp.concatenate([t[:, a + o: a + o + 1] for a in at], axis=0)
                p = jnp.exp(s_scr[j] + bias2 - col(dl_c, STAT_LSE_LANE))
                ds_scr[j] = (p * (dp_scr[j] - col(dl_c, 0))).astype(BF16)
                row = lambda t, o: jnp.concatenate([jnp.broadcast_to(t[a + o: a + o + 1, :], (ATT_BLK, 2 * ATT_BLK)) for a in at], axis=0)
                pt = jnp.exp(st_scr[j] + bias_t2 - row(dl_t, STAT_LSE_LANE))
                pt_scr[j] = pt.astype(BF16)
                dst_scr[j] = (pt * (dpt_scr[j] - row(dl_t, 0))).astype(BF16)
        for i in range(qb):
            for hp, sl in enumerate(pairs):
                j = 4 * i + hp
                dq = unstack(_dot(ds_scr[j], with_prev(i, sl, kp_ref, kc_ref))) * ATT_SCALE
                dk = unstack(_dot(dst_scr[j], with_next(i, sl, qc_ref, qn_ref))) * ATT_SCALE
                dv = unstack(_dot(pt_scr[j], with_next(i, sl, ac_ref, an_ref)))
                dqkv_ref[0, blk(i), sl] = dq.astype(BF16)
                dqkv_ref[0, blk(i), AT_W + HEAD_PAIR * hp: AT_W + HEAD_PAIR * (hp + 1)] = dk.astype(BF16)
                dqkv_ref[0, blk(i), 2 * AT_W + HEAD_PAIR * hp: 2 * AT_W + HEAD_PAIR * (hp + 1)] = dv.astype(BF16)

    cur = lambda j: pl.BlockSpec((1, rows, AT_W), lambda r, n: (r, n, j))
    prev = lambda j: pl.BlockSpec((1, ATT_BLK, AT_W), lambda r, n: (r, jnp.maximum(qb * n - 1, 0), j))
    nxt_blk = lambda n: jnp.minimum(qb * (n + 1), nb - 1)
    sq = (4 * qb, 2 * ATT_BLK, 2 * ATT_BLK)
    return pl.pallas_call(
        body, name=f"att_bwd_d{d}", grid=(d, nsteps),
        in_specs=[cur(0), pl.BlockSpec((1, ATT_BLK, AT_W), lambda r, n: (r, nxt_blk(n), 0)), prev(1), cur(1), prev(2), cur(2),
                  pl.BlockSpec((1, rows, AT_W), lambda r, n: (r, n, 0)),
                  pl.BlockSpec((1, ATT_BLK, AT_W), lambda r, n: (r, nxt_blk(n), 0)),
                  pl.BlockSpec((1, rows, STAT_W), lambda r, n: (r, n, 0)),
                  pl.BlockSpec((1, ATT_BLK, STAT_W), lambda r, n: (r, nxt_blk(n), 0)),
                  _full((2, ATT_BLK, 2 * ATT_BLK)), _full((2, ATT_BLK, 2 * ATT_BLK))],
        out_specs=pl.BlockSpec((1, rows, 3 * AT_W), lambda r, n: (r, n, 0)),
        out_shape=jax.ShapeDtypeStruct((d, L, 3 * AT_W), BF16),
        scratch_shapes=[pltpu.VMEM(sq, F32)] * 4 + [pltpu.VMEM(sq, BF16)] * 3,
        compiler_params=_params(("parallel", "parallel")),
    )(qkv_d, qkv_d, qkv_d, qkv_d, qkv_d, qkv_d, do_d, do_d, dl_d, dl_d, *_att_bias())


def _att_bwd_combine(dqkvs, cos2, sin2):
    S = dqkvs[0].shape[1]
    tm = MIX_TM

    def body(a_ref, b4_ref, c16_ref, p4_ref, p16_ref, cos_ref, sin_ref, dq_ref, dk_ref, dv_ref):
        t = a_ref[...].astype(F32) + _merge_bf16(b4_ref, p4_ref, 4) + _merge_bf16(c16_ref, p16_ref, 16)
        dy = t[:, : 2 * AT_W]
        cosf = jnp.tile(cos_ref[...], (1, 8))
        sinf = jnp.tile(sin_ref[...], (1, 8))
        dx = dy * cosf - _rope_rot(dy) * sinf
        dq_ref[...] = dx[:, :AT_W].astype(BF16)
        dk_ref[...] = dx[:, AT_W:].astype(BF16)
        dv_ref[...] = t[:, 2 * AT_W:].astype(BF16)

    row = lambda w: pl.BlockSpec((tm, w), lambda s: (s, 0))
    act = jax.ShapeDtypeStruct((S, AT_W), BF16)
    return pl.pallas_call(
        body, name="att_bwd_combine", grid=(S // tm,),
        in_specs=[row(3 * AT_W), _dil_spec(4, tm, 3 * AT_W), _dil_spec(16, tm, 3 * AT_W),
                  _full((PERM_ROWS, PERM_ROWS)), _full((PERM_ROWS, PERM_ROWS)), row(128), row(128)],
        out_specs=[row(AT_W), row(AT_W), row(AT_W)],
        out_shape=[act, act, act],
        compiler_params=_params(("parallel",)),
    )(dqkvs[0].reshape(S, 3 * AT_W), dqkvs[1], dqkvs[2], _perm_matrix(4), _perm_matrix(16), cos2, sin2)


def _outproj(x2, tgt2, mix_at, w_out_full, fnw, o_hg, p1, hg_norm_w):
    S = x2.shape[0]
    tm = PROJ_TM
    ns = S // tm

    def body(x_ref, t_ref, ma_ref, w_ref, fw_ref, o_ref, z_ref, hgw_ref,
             dh_ref, doh_ref, dzh_ref, dma_ref, gw_ref, gfw_ref, ghgw_ref, loss_ref):
        s = pl.program_id(0)

        @pl.when(s == 0)
        def _():
            gw_ref[...] = jnp.zeros_like(gw_ref)
            gfw_ref[...] = jnp.zeros_like(gfw_ref)
            ghgw_ref[...] = jnp.zeros_like(ghgw_ref)
            loss_ref[...] = jnp.zeros_like(loss_ref)

        heads = [slice(HG_HEAD * hh, HG_HEAD * (hh + 1)) for hh in range(HG_HEADS)]
        norm = []
        for sl in heads:
            o, z = o_ref[:, sl], z_ref[:, sl]
            rs = lax.rsqrt(jnp.mean(o * o, axis=-1, keepdims=True) + EPS)
            norm.append((rs, o * rs, _sigmoid(z)))
        mh = jnp.concatenate([(oh * hgw_ref[:, sl] * (z_ref[:, sl] * sz)).astype(BF16)
                              for sl, (_, oh, sz) in zip(heads, norm)], axis=1)
        y = _dot(mh, w_ref[:HG_W, :]) + _dot(ma_ref[...], w_ref[HG_W:, :])
        h = x_ref[...] + y
        r = lax.rsqrt(jnp.mean(h * h, axis=-1, keepdims=True) + EPS)
        hn = h * r
        fw = fw_ref[...]
        err = hn * fw - t_ref[...]
        loss_ref[...] += 0.5 * jnp.sum(jnp.mean(err * err, axis=-1, keepdims=True))
        dout = err * (1.0 / D_MODEL)
        gfw_ref[...] += jnp.sum(dout * hn, axis=0, keepdims=True)
        dhn = dout * fw
        dh = r * (dhn - hn * jnp.mean(dhn * hn, axis=-1, keepdims=True))
        dh_ref[...] = dh
        dhb = dh.astype(BF16)
        dma_ref[...] = _dot_nt(dhb, w_ref[HG_W:, :])
        dmh = _dot_nt(dhb, w_ref[:HG_W, :])
        for sl, (rs, oh, sz) in zip(heads, norm):
            z, dmix, gwv = z_ref[:, sl], dmh[:, sl], hgw_ref[:, sl]
            dzh_ref[:, sl] = (dmix * (oh * gwv) * (sz * (1.0 + z * (1.0 - sz)))).astype(BF16)
            don = dmix * (z * sz)
            ghgw_ref[:, sl] += jnp.sum(don * oh, axis=0, keepdims=True)
            dy = don * gwv
            doh_ref[:, sl] = rs * (dy - oh * jnp.mean(dy * oh, axis=-1, keepdims=True))
        gw_ref[:HG_W, :] += _dot_tn(mh, dhb)
        gw_ref[HG_W:, :] += _dot_tn(ma_ref[...], dhb)

    row = lambda w: pl.BlockSpec((tm, w), lambda s: (s, 0))
    return pl.pallas_call(
        body, name="outproj", grid=(ns,),
        in_specs=[row(D_MODEL), row(D_MODEL), row(AT_W),
                  _full((D_MODEL, D_MODEL)), _full((1, D_MODEL)),
                  row(HG_W), pl.BlockSpec((tm, HG_W), lambda s: (s, 1)), _full((1, HG_W))],
        out_specs=[row(D_MODEL), row(HG_W), row(HG_W), row(AT_W), _full((D_MODEL, D_MODEL)), _full((1, D_MODEL)),
                   _full((1, HG_W)), _full((8, 128))],
        out_shape=[jax.ShapeDtypeStruct((S, D_MODEL), F32), jax.ShapeDtypeStruct((S, HG_W), F32),
                   jax.ShapeDtypeStruct((S, HG_W), BF16), jax.ShapeDtypeStruct((S, AT_W), F32),
                   jax.ShapeDtypeStruct((D_MODEL, D_MODEL), F32), jax.ShapeDtypeStruct((1, D_MODEL), F32),
                   jax.ShapeDtypeStruct((1, HG_W), F32), jax.ShapeDtypeStruct((8, 128), F32)],
        compiler_params=_params(("arbitrary",)),
    )(x2, tgt2, mix_at, w_out_full, fnw, o_hg, p1, hg_norm_w)


def _inproj_bwd_x(dps, w_in_full, x2, norm_w, dh):
    S = x2.shape[0]
    tm = PROJ_TM

    def body(d0, d1, d2, d3, d4, d5, d6, d7, w_ref, x_ref, nw_ref, dh_ref, gx_ref, gnw_ref):
        s = pl.program_id(0)

        @pl.when(s == 0)
        def _():
            gnw_ref[...] = jnp.zeros_like(gnw_ref)

        du = jnp.zeros((tm, D_MODEL), F32)
        for i, dref in enumerate((d0, d1, d2, d3, d4, d5, d6, d7)):
            j, half = divmod(i, 2)
            du = du + _dot_nt(dref[...], w_ref[j, :, 512 * half: 512 * (half + 1)])
        x = x_ref[...]
        r = lax.rsqrt(jnp.mean(x * x, axis=-1, keepdims=True) + EPS)
        xh = x * r
        gnw_ref[...] += jnp.sum(du * xh, axis=0, keepdims=True)
        dun = du * nw_ref[...]
        gx_ref[...] = dh_ref[...] + r * (dun - xh * jnp.mean(dun * xh, axis=-1, keepdims=True))

    row = lambda w: pl.BlockSpec((tm, w), lambda s: (s, 0))
    return pl.pallas_call(
        body, name="inproj_bwd_x", grid=(S // tm,),
        in_specs=[row(512)] * 8 + [_full((4, D_MODEL, 1024)), row(D_MODEL), _full((1, D_MODEL)), row(D_MODEL)],
        out_specs=[row(D_MODEL), _full((1, D_MODEL))],
        out_shape=[jax.ShapeDtypeStruct((S, D_MODEL), F32), jax.ShapeDtypeStruct((1, D_MODEL), F32)],
        compiler_params=_params(("arbitrary",)),
    )(*dps, w_in_full, x2, norm_w, dh)


def _inproj_bwd_w(u, dps):
    S = u.shape[0]
    tm = 2 * PROJ_TM

    def body(u_ref, d0, d1, d2, d3, d4, d5, d6, d7, g_ref):
        @pl.when(pl.program_id(0) == 0)
        def _():
            g_ref[...] = jnp.zeros_like(g_ref)

        ub = u_ref[...]
        for i, dref in enumerate((d0, d1, d2, d3, d4, d5, d6, d7)):
            j, half = divmod(i, 2)
            g_ref[j, :, 512 * half: 512 * (half + 1)] += _dot_tn(ub, dref[...])

    return pl.pallas_call(
        body, name="inproj_bwd_w", grid=(S // tm,),
        in_specs=[pl.BlockSpec((tm, D_MODEL), lambda s: (s, 0))] + [pl.BlockSpec((tm, 512), lambda s: (s, 0))] * 8,
        out_specs=_full((4, D_MODEL, 1024)),
        out_shape=jax.ShapeDtypeStruct((4, D_MODEL, 1024), F32),
        compiler_params=_params(("arbitrary",)),
    )(u, *dps)


def _adamw_update(gg, w_ref, m_ref, v_ref, d_ref, nm_ref, nv_ref):
    nm = ADAM_B1 * m_ref[...] + (1.0 - ADAM_B1) * gg
    nv = ADAM_B2 * v_ref[...] + (1.0 - ADAM_B2) * (gg * gg)
    m_hat = nm / (1.0 - ADAM_B1 ** ADAM_STEP)
    v_hat = nv / (1.0 - ADAM_B2 ** ADAM_STEP)
    d_ref[...] = -ADAM_LR * (m_hat / (jnp.sqrt(v_hat) + ADAM_EPS) + ADAM_WD * w_ref[...])
    nm_ref[...] = nm
    nv_ref[...] = nv


def _adamw_small(red, ws, ms, vs):
    def body(red_ref, *refs):
        w, m, v, g, d, nm, nv = (refs[4 * i: 4 * (i + 1)] for i in range(7))
        loss_ref = refs[28]
        g[0][...] = red_ref[0:1, :]
        g[1][0:1, :] = red_ref[3:4, :HG_W]
        g[1][1:2, :] = red_ref[3:4, HG_W:]
        g[2][...] = red_ref[2:3, :HG_W]
        g[3][...] = red_ref[1:2, :]
        loss_ref[...] = red_ref[4:5, 0:1]
        for i in range(4):
            _adamw_update(g[i][...], w[i], m[i], v[i], d[i], nm[i], nv[i])

    shapes = [a.shape for a in ws]
    sds = [jax.ShapeDtypeStruct(s, F32) for s in shapes]
    out = pl.pallas_call(
        body, name="adamw_small",
        in_specs=[_full(red.shape)] + [_full(s) for s in shapes] * 3,
        out_specs=[_full(s) for s in shapes] * 4 + [_full((1, 1))],
        out_shape=sds * 4 + [jax.ShapeDtypeStruct((1, 1), F32)],
        compiler_params=_params(),
    )(red, *ws, *ms, *vs)
    return out[0:4], out[4:8], out[8:12], out[12:16], out[16]


def _place():
    return lax.axis_index("x"), lax.axis_index("y"), lax.axis_index("c")


def _gather_weights(w_in_s, w_out_s):
    def body(win_ref, wout_ref, fin_ref, fout_ref, send_sems, recv_sems):
        x, y, c = _place()
        me = (x, y, c)
        sib = (x, y, 1 - c)
        mine = 2 * x + y
        fin_ref[mine] = win_ref[...].astype(BF16)
        fout_ref[mine] = wout_ref[...].astype(BF16)
        chips = [(1 - x, y), (x, 1 - y), (1 - x, 1 - y)]

        def halves(chip, half):
            return (fin_ref.at[chip, pl.ds(half * 512, 512), :], fout_ref.at[chip, pl.ds(half * 128, 128), :])

        def copy(k, ref, to):
            return pltpu.make_async_remote_copy(src_ref=ref, dst_ref=ref, send_sem=send_sems.at[k],
                                                recv_sem=recv_sems.at[k], device_id=to, device_id_type=MESH)

        first, passed = [], []
        for j, (cx, cy) in enumerate(chips):
            for a, ref in enumerate(halves(mine, c)):
                first.append(copy(2 * j + a, ref, (cx, cy, c)))
        for cp in first:
            cp.start()
        for j, (cx, cy) in enumerate(chips):
            for a, ref in enumerate(halves(2 * cx + cy, c)):
                copy(2 * j + a, ref, me).wait_recv()
                fwd = copy(6 + 2 * j + a, ref, sib)
                fwd.start()
                passed.append(fwd)
        for j, (cx, cy) in enumerate(chips):
            for a, ref in enumerate(halves(2 * cx + cy, 1 - c)):
                copy(6 + 2 * j + a, ref, me).wait_recv()
        for cp in first + passed:
            cp.wait_send()

    vm = pl.BlockSpec(memory_space=pltpu.VMEM)
    return pl.pallas_call(
        body, name="gather_weights",
        in_specs=[vm, vm], out_specs=[vm, vm],
        out_shape=[jax.ShapeDtypeStruct((4, D_MODEL, 1024), BF16), jax.ShapeDtypeStruct((4, 256, D_MODEL), BF16)],
        scratch_shapes=[pltpu.SemaphoreType.DMA((12,)), pltpu.SemaphoreType.DMA((12,))],
        compiler_params=pltpu.CompilerParams(vmem_limit_bytes=VMEM_LIMIT),
    )(w_in_s, w_out_s)


def _swap_halves(g_in, g_out):
    def body(gin_ref, gout_ref, rin_ref, rout_ref, send_sems, recv_sems):
        x, y, c = _place()
        sib = (x, y, 1 - c)
        cps = [pltpu.make_async_remote_copy(src_ref=src.at[:, 1 - c], dst_ref=dst, send_sem=send_sems.at[k],
                                            recv_sem=recv_sems.at[k], device_id=sib, device_id_type=MESH)
               for k, (src, dst) in enumerate(((gin_ref, rin_ref), (gout_ref, rout_ref)))]
        for cp in cps:
            cp.start()
        for cp in cps:
            cp.wait()

    hbm = pl.BlockSpec(memory_space=pl.ANY)
    return pl.pallas_call(
        body, name="swap_halves",
        in_specs=[hbm, hbm], out_specs=[hbm, hbm],
        out_shape=[jax.ShapeDtypeStruct((4,) + g.shape[2:], F32) for g in (g_in, g_out)],
        scratch_shapes=[pltpu.SemaphoreType.DMA((2,)), pltpu.SemaphoreType.DMA((2,))],
    )(g_in, g_out)


def _add_halves(gs, rs, cidx):
    n = gs[0].shape[0]

    def body(c_ref, *refs):
        k = len(gs)
        for g_ref, r_ref, o_ref in zip(refs[:k], refs[k: 2 * k], refs[2 * k:]):
            o_ref[0] = (g_ref[0, 0] + r_ref[0]).astype(BF16)

    mine = lambda g: pl.BlockSpec((1, 1) + g.shape[2:], lambda j, c_ref: (j, c_ref[0], 0, 0))
    chunk = lambda g: pl.BlockSpec((1,) + g.shape[2:], lambda j, c_ref: (j, 0, 0))
    return pl.pallas_call(
        body, name="add_halves",
        grid_spec=pltpu.PrefetchScalarGridSpec(
            num_scalar_prefetch=1, grid=(n,),
            in_specs=[mine(g) for g in gs] + [chunk(g) for g in gs],
            out_specs=[chunk(g) for g in gs]),
        out_shape=[jax.ShapeDtypeStruct((n,) + g.shape[2:], BF16) for g in gs],
        compiler_params=_params(("parallel",)),
    )(cidx, *gs, *rs)


def _exchange_copies(in_ref, out_ref, lin_ref, lout_ref, send_sems, recv_sems):
    x, y, c = _place()
    cps = []
    for k, (cx, cy) in enumerate([(1 - x, y), (x, 1 - y), (1 - x, 1 - y)]):
        for a, (src, dst) in enumerate(((in_ref, lin_ref), (out_ref, lout_ref))):
            cps.append(pltpu.make_async_remote_copy(
                src_ref=src.at[2 * cx + cy], dst_ref=dst.at[k], send_sem=send_sems.at[2 * k + a],
                recv_sem=recv_sems.at[2 * k + a], device_id=(cx, cy, c), device_id_type=MESH))
    return cps


def _exchange_start(cs_in, cs_out):
    def body(in_ref, out_ref, lin_ref, lout_ref, send_sems, recv_sems, in_thru, out_thru, lin_thru, lout_thru, token):
        for cp in _exchange_copies(in_ref, out_ref, lin_ref, lout_ref, send_sems, recv_sems):
            cp.start()
        token[...] = jnp.zeros_like(token)

    lands = [lax.empty((3,) + a.shape[1:], a.dtype) for a in (cs_in, cs_out)]
    bufs = [pltpu.with_memory_space_constraint(a, pltpu.HBM) for a in (cs_in, cs_out, *lands)]
    hbm = pl.BlockSpec(memory_space=pltpu.HBM)
    sem = pl.BlockSpec(memory_space=pltpu.SEMAPHORE)
    return pl.pallas_call(
        body, name="exchange_start",
        in_specs=[hbm] * 4,
        out_specs=[sem, sem, hbm, hbm, hbm, hbm, pl.BlockSpec(memory_space=pltpu.VMEM)],
        out_shape=[pltpu.SemaphoreType.DMA((6,)), pltpu.SemaphoreType.DMA((6,))]
                  + [pltpu.HBM(b.shape, b.dtype) for b in bufs] + [jax.ShapeDtypeStruct((8, 128), F32)],
        input_output_aliases={0: 2, 1: 3, 2: 4, 3: 5},
        compiler_params=pltpu.CompilerParams(has_side_effects=pltpu.SideEffectType.DATAFLOW_SIDE_EFFECTING),
    )(*bufs)


def _exchange_wait(send_sems, recv_sems, in_thru, out_thru, lin_thru, lout_thru, after):
    def body(in_ref, out_ref, lin_ref, lout_ref, send_sems, recv_sems, after_ref, in_dead, out_dead, got_in, got_out):
        for cp in _exchange_copies(in_ref, out_ref, lin_ref, lout_ref, send_sems, recv_sems):
            cp.wait_send()
            cp.wait_recv()

    hbm = pl.BlockSpec(memory_space=pltpu.HBM)
    sem = pl.BlockSpec(memory_space=pltpu.SEMAPHORE)
    bufs = (in_thru, out_thru, lin_thru, lout_thru)
    return pl.pallas_call(
        body, name="exchange_wait",
        in_specs=[hbm] * 4 + [sem, sem, pl.BlockSpec(memory_space=pl.ANY)],
        out_specs=[hbm] * 4,
        out_shape=[pltpu.HBM(b.shape, b.dtype) for b in bufs],
        input_output_aliases={0: 0, 1: 1, 2: 2, 3: 3},
        compiler_params=pltpu.CompilerParams(has_side_effects=pltpu.SideEffectType.DATAFLOW_SIDE_EFFECTING),
    )(*bufs, send_sems, recv_sems, after)


PEER_XOR = (2, 1, 3)


SUM_STEPS = 2


def _sum_chips(css, rs, chip_idx):
    def body(m_ref, *refs):
        k = len(css)
        mine = m_ref[0]
        for cs_ref, r_ref, o_ref in zip(refs[:k], refs[k: 2 * k], refs[2 * k:]):
            own = cs_ref[0].astype(F32)
            got = [r_ref[j].astype(F32) for j in range(3)]
            acc = None
            for s in range(4):
                rel = mine ^ s
                term = jnp.where(rel == 0, own, jnp.where(rel == PEER_XOR[0], got[0],
                                                          jnp.where(rel == PEER_XOR[1], got[1], got[2])))
                acc = term if acc is None else acc + term
            o_ref[...] = acc

    tile = lambda r: (r.shape[1] // SUM_STEPS, r.shape[2])
    return pl.pallas_call(
        body, name="sum_chips",
        grid_spec=pltpu.PrefetchScalarGridSpec(
            num_scalar_prefetch=1, grid=(SUM_STEPS,),
            in_specs=[pl.BlockSpec((1,) + tile(r), lambda i, m_ref: (m_ref[0], i, 0)) for r in rs]
            + [pl.BlockSpec((3,) + tile(r), lambda i, m_ref: (0, i, 0)) for r in rs],
            out_specs=[pl.BlockSpec(tile(r), lambda i, m_ref: (i, 0)) for r in rs]),
        out_shape=[jax.ShapeDtypeStruct(r.shape[1:], F32) for r in rs],
        compiler_params=_params(("parallel",)),
    )(chip_idx, *css, *rs)


def _swap_reduced(h_in, h_out):
    def body(in_ref, out_ref, rin_ref, rout_ref, send_sems, recv_sems):
        x, y, c = _place()
        cps = [pltpu.make_async_remote_copy(src_ref=src, dst_ref=dst, send_sem=send_sems.at[k],
                                            recv_sem=recv_sems.at[k], device_id=(x, y, 1 - c), device_id_type=MESH)
               for k, (src, dst) in enumerate(((in_ref, rin_ref), (out_ref, rout_ref)))]
        for cp in cps:
            cp.start()
        for cp in cps:
            cp.wait()

    hbm = pl.BlockSpec(memory_space=pl.ANY)
    return pl.pallas_call(
        body, name="swap_reduced",
        in_specs=[hbm, hbm], out_specs=[hbm, hbm],
        out_shape=[jax.ShapeDtypeStruct(h.shape, F32) for h in (h_in, h_out)],
        scratch_shapes=[pltpu.SemaphoreType.DMA((2,)), pltpu.SemaphoreType.DMA((2,))],
    )(h_in, h_out)


def _adamw_halves(w, mine, theirs, m, v, cidx, name):
    rows, cols = w.shape
    half = rows // 2
    tr = min(half, 256)
    nbh = half // tr

    def body(c_ref, w_ref, a_ref, b_ref, m_ref, v_ref, g_ref, d_ref, nm_ref, nv_ref):
        gg = jnp.where(pl.program_id(0) // nbh == c_ref[0], a_ref[...], b_ref[...])
        g_ref[...] = gg
        _adamw_update(gg, w_ref, m_ref, v_ref, d_ref, nm_ref, nv_ref)

    spec = pl.BlockSpec((tr, cols), lambda i, c_ref: (i, 0))
    hspec = pl.BlockSpec((tr, cols), lambda i, c_ref: (i % nbh, 0))
    sds = jax.ShapeDtypeStruct((rows, cols), F32)
    return pl.pallas_call(
        body, name=name,
        grid_spec=pltpu.PrefetchScalarGridSpec(
            num_scalar_prefetch=1, grid=(rows // tr,),
            in_specs=[spec, hspec, hspec, spec, spec], out_specs=[spec] * 4),
        out_shape=[sds] * 4,
        compiler_params=_params(("parallel",)),
    )(cidx, w, mine, theirs, m, v)


def _allreduce_small(g_nw, g_fw, g_hgw, g_lbl, loss8):
    def body(nw_ref, fw_ref, hgw_ref, lbl_ref, loss_ref, out_ref, slots, send_sems, recv_sems):
        x, y, c = _place()
        me = 4 * x + 2 * y + c
        slots[me] = jnp.zeros((8, D_MODEL), F32)
        slots[me, 0:1, :] = nw_ref[...]
        slots[me, 1:2, :] = fw_ref[...]
        slots[me, 2:3, 0:HG_W] = hgw_ref[...]
        slots[me, 3:4, 0:HG_W] = lbl_ref[0:1, :]
        slots[me, 3:4, HG_W:] = lbl_ref[1:2, :]
        slots[me, 4:5, 0:128] = loss_ref[0:1, :]
        cps = []
        for k in range(1, 8):
            dx, dy, dc = (k >> 2) & 1, (k >> 1) & 1, k & 1
            to = (x ^ dx, y ^ dy, c ^ dc)
            cps.append(pltpu.make_async_remote_copy(
                src_ref=slots.at[me], dst_ref=slots.at[me], send_sem=send_sems.at[k - 1],
                recv_sem=recv_sems.at[k - 1], device_id=to, device_id_type=MESH))
        for cp in cps:
            cp.start()
        for cp in cps:
            cp.wait()
        acc = slots[0]
        for i in range(1, 8):
            acc = acc + slots[i]
        out_ref[...] = acc

    vm = pl.BlockSpec(memory_space=pltpu.VMEM)
    return pl.pallas_call(
        body, name="allreduce_small",
        in_specs=[vm] * 5, out_specs=vm,
        out_shape=jax.ShapeDtypeStruct((8, D_MODEL), F32),
        scratch_shapes=[pltpu.VMEM((8, 8, D_MODEL), F32), pltpu.SemaphoreType.DMA((7,)), pltpu.SemaphoreType.DMA((7,))],
    )(g_nw, g_fw, g_hgw, g_lbl, loss8)


def _rope_tables(S):
    inv_freq = (np.float32(1.0) / np.power(np.float32(ROPE_THETA), np.arange(ROPE_HALF, dtype=np.float32) / np.float32(ROPE_HALF))).astype(np.float32)
    ang = (np.arange(S, dtype=np.float32)[:, None] * inv_freq[None, :]).astype(np.float32)
    cos, sin = np.cos(ang).astype(np.float32), np.sin(ang).astype(np.float32)
    cos2 = np.concatenate([cos, cos, cos, cos], axis=-1)
    sin2 = np.concatenate([-sin, sin, -sin, sin], axis=-1)
    return jnp.asarray(cos2), jnp.asarray(sin2)


def _local_step(x2, tgt2, norm_w, w_in_full, lbl, hg_norm_w, w_out4, fnw):
    S = x2.shape[0]
    cos2, sin2 = _rope_tables(S)
    w_out_full = w_out4.reshape(D_MODEL, D_MODEL)
    p0, p1, qkv, qkv4, qkv16, z_at, u = _inproj(x2, norm_w, w_in_full, cos2, sin2)
    o_hg, states = _hg_fwd(p0, p1, lbl)
    qkv_ds = [qkv.reshape(1, S, 3 * AT_W), qkv4, qkv16]
    os_d, ls_d = zip(*[_att_fwd(q) for q in qkv_ds])
    o_at, lse, mix_at = _att_combine(os_d, ls_d, z_at)
    dh, do_hg, dz_hg, dm_at, g_wout, g_fw, g_hgw, loss8 = _outproj(x2, tgt2, mix_at, w_out_full, fnw, o_hg, p1, hg_norm_w)
    dqr, dfl, dv_hg, g_lbl = _hg_bwd(p0, p1, do_hg, states, lbl)
    do1, do4, do16, dl1, dl4, dl16, dz_at = _att_gate_bwd(dm_at, o_at, lse, z_at)
    do_ds = [do1.reshape(1, S, AT_W), do4, do16]
    dl_ds = [dl1.reshape(1, S, STAT_W), dl4, dl16]
    dqkvs = [_att_bwd(q, a, b) for q, a, b in zip(qkv_ds, do_ds, dl_ds)]
    dq_at, dk_at, dv_at = _att_bwd_combine(dqkvs, cos2, sin2)
    dps = [dqr, dfl, dv_hg, dz_hg, dq_at, dk_at, dv_at, dz_at]
    return loss8, dps, u, dh, g_lbl, g_hgw, g_wout, g_fw


def kernel(x, norm_w, w_in, hgrn_lb_logits, hg_norm_w, w_out, final_norm_w, loss_target, m_norm_w, m_w_in, m_hgrn_lb_logits, m_hg_norm_w, m_w_out, m_final_norm_w, v_norm_w, v_w_in, v_hgrn_lb_logits, v_hg_norm_w, v_w_out, v_final_norm_w):
    S = x.shape[1]
    w_in_full, w_out_full = _gather_weights(w_in[0], w_out[0])
    loss8, dps, u, dh, g_lbl, g_hgw, g_wout, g_fw = _local_step(
        x[0], loss_target[0], norm_w, w_in_full, hgrn_lb_logits, hg_norm_w,
        w_out_full, final_norm_w.reshape(1, D_MODEL))

    cidx = lax.axis_index("c").astype(jnp.int32).reshape(1)
    g_win = _inproj_bwd_w(u, dps)
    g_in4 = g_win.reshape(4, 2, 512, 1024)
    g_out4 = g_wout.reshape(4, 2, 128, D_MODEL)
    r_in, r_out = _swap_halves(g_in4, g_out4)
    cs_in, cs_out = _add_halves((g_in4, g_out4), (r_in, r_out), cidx)
    *started, token = _exchange_start(cs_in, cs_out)
    grad_x, g_nw = _inproj_bwd_x(dps, w_in_full, x[0], norm_w + token[0:1, 0:1], dh)
    cs_in, cs_out, x_in, x_out = _exchange_wait(*started, g_nw)
    chip_idx = (2 * lax.axis_index("x") + lax.axis_index("y")).astype(jnp.int32).reshape(1)
    h_in, h_out = _sum_chips((cs_in, cs_out), (x_in, x_out), chip_idx)
    s_in, s_out = _swap_reduced(h_in, h_out)

    red = _allreduce_small(g_nw, g_fw, g_hgw, g_lbl, loss8)
    as_row = lambda a: a.reshape(1, D_MODEL)
    grads, deltas, new_m, new_v, loss = _adamw_small(
        red, (norm_w, hgrn_lb_logits, hg_norm_w, as_row(final_norm_w)),
        (m_norm_w, m_hgrn_lb_logits, m_hg_norm_w, as_row(m_final_norm_w)),
        (v_norm_w, v_hgrn_lb_logits, v_hg_norm_w, as_row(v_final_norm_w)))
    grad_norm_w, grad_lbl, grad_hg_norm_w, grad_final_norm_w = grads
    d_nw, d_lbl, d_hgw, d_fw = deltas
    m_nw, m_lbl, m_hgw, m_fw = new_m
    v_nw, v_lbl, v_hgw, v_fw = new_v
    grad_w_in, d_win, m_win, v_win = _adamw_halves(w_in[0], h_in, s_in, m_w_in[0], v_w_in[0], cidx, "adamw_w_in")
    grad_w_out, d_wout, m_wout, v_wout = _adamw_halves(w_out[0], h_out, s_out, m_w_out[0], v_w_out[0], cidx, "adamw_w_out")
    e1 = lambda a: a[None]
    flat = lambda a: a.reshape(D_MODEL)
    return (loss.reshape(()), grad_x[None], grad_norm_w, e1(grad_w_in), grad_lbl, grad_hg_norm_w, e1(grad_w_out),
            flat(grad_final_norm_w),
            d_nw, e1(d_win), d_lbl, d_hgw, e1(d_wout), flat(d_fw),
            m_nw, e1(m_win), m_lbl, m_hgw, e1(m_wout), flat(m_fw),
            v_nw, e1(v_win), v_lbl, v_hgw, e1(v_wout), flat(v_fw))
```

```python
import jax
import jax.numpy as jnp
import numpy as np
from jax import lax
from jax.experimental import pallas as pl
from jax.experimental.pallas import tpu as pltpu

F32 = jnp.float32
BF16 = jnp.bfloat16
MESH = pl.DeviceIdType.MESH

D_MODEL = 1024
HG_HEADS = 4
HG_HEAD = 128
HG_W = HG_HEADS * HG_HEAD
AT_HEAD = 64
AT_W = 512
HEAD_PAIR = 2 * AT_HEAD
ROPE_HALF = 32
ROPE_THETA = 10000.0
EPS = 1e-6
CHUNK = 128
LEVELS = (64, 32, 16, 8, 4, 2, 1)
DIAG = 1
SUBLANES = 8
ATT_BLK = 128
ATT_QB = 4
ATT_SCALE = 0.125
STAT_W = 128
STAT_LANES = 16
STAT_LSE_LANE = 8
NEG = -1e30
VMEM_LIMIT = 56 * 1024 * 1024
MIX_TM = 1024
PROJ_TM = 512

ADAM_LR = 0.001
ADAM_B1 = 0.9
ADAM_B2 = 0.999
ADAM_EPS = 1e-08
ADAM_WD = 0.01
ADAM_STEP = 10


def _iota(shape, dim):
    return lax.broadcasted_iota(jnp.int32, shape, dim)


def _dot(a, b):
    return jnp.dot(a, b, preferred_element_type=F32)


def _dot_nt(a, b):
    return lax.dot_general(a, b, (((1,), (1,)), ((), ())), preferred_element_type=F32)


def _dot_tn(a, b):
    return lax.dot_general(a, b, (((0,), (0,)), ((), ())), preferred_element_type=F32)


def _sigmoid(v):
    return 0.5 * jnp.tanh(0.5 * v) + 0.5


def _params(sem=None, vmem=VMEM_LIMIT):
    return pltpu.CompilerParams(dimension_semantics=sem, vmem_limit_bytes=vmem)


def _full(shape):
    n = len(shape)
    return pl.BlockSpec(shape, lambda *_: (0,) * n)


def _rope_rot(y):
    n = y.shape[1]
    first = (_iota(y.shape, 1) & (2 * ROPE_HALF - 1)) < ROPE_HALF
    return jnp.where(first, pltpu.roll(y, n - ROPE_HALF, 1), pltpu.roll(y, ROPE_HALF, 1))


def _dil_spec(d, tm, width):
    return pl.BlockSpec((d, tm // d, width), lambda s: (0, s, 0))


LANES = 128


def _slab_scratch(tm, width):
    return pltpu.VMEM((width // LANES, tm, LANES), F32)


def _to_slabs(v, slabs_ref):
    for j in range(slabs_ref.shape[0]):
        slabs_ref[j] = v[:, LANES * j: LANES * (j + 1)]


def _from_slabs(slabs_ref):
    return jnp.concatenate([slabs_ref[j] for j in range(slabs_ref.shape[0])], axis=1)


def _split_residues(slabs_ref, dst_ref, d, dtype):
    nslab, tm, _ = slabs_ref.shape
    for r in range(d):
        for j in range(nslab):
            dst_ref[r, :, LANES * j: LANES * (j + 1)] = slabs_ref[j, pl.ds(r, tm // d, stride=d), :].astype(dtype)


def _merge_residues(src_ref, slabs_ref, d):
    nslab, tm, _ = slabs_ref.shape
    for r in range(d):
        for j in range(nslab):
            slabs_ref[j, pl.ds(r, tm // d, stride=d), :] = src_ref[r, :, LANES * j: LANES * (j + 1)].astype(F32)


PERM_ROWS = 256


def _perm_matrix(d, inverse=False):
    n = PERM_ROWS // d
    t = np.arange(PERM_ROWS)
    p = np.zeros((PERM_ROWS, PERM_ROWS), np.float32)
    p[t, (t % d) * n + t // d] = 1.0
    return jnp.asarray(p.T if inverse else p, BF16)


def _merge_bf16(src_ref, perm_ref, d):
    n = PERM_ROWS // d
    tm = src_ref.shape[1] * d
    outs = []
    for b in range(tm // PERM_ROWS):
        flat = jnp.concatenate([src_ref[r, n * b: n * (b + 1), :] for r in range(d)], axis=0)
        outs.append(_dot(perm_ref[...], flat))
    return jnp.concatenate(outs, axis=0)


def _split_bf16(v, perm_t_ref, dst_ref, d):
    n = PERM_ROWS // d
    for b in range(v.shape[0] // PERM_ROWS):
        flat = _dot(perm_t_ref[...], v[PERM_ROWS * b: PERM_ROWS * (b + 1), :]).astype(BF16)
        for r in range(d):
            dst_ref[r, n * b: n * (b + 1), :] = flat[n * r: n * (r + 1), :]


def _inproj(x2, norm_w, w_in_full, cos2, sin2):
    S = x2.shape[0]
    tm = PROJ_TM

    def body(x_ref, nw_ref, w_ref, cos_ref, sin_ref, t4_ref, t16_ref, p0_ref, p1_ref, qkv_ref, qkv4_ref, qkv16_ref, z_ref,
             u_ref):
        x = x_ref[...]
        r = lax.rsqrt(jnp.mean(x * x, axis=-1, keepdims=True) + EPS)
        u = x * r * nw_ref[...]
        ub = u.astype(BF16)
        u_ref[...] = ub
        p0_ref[...] = _dot(ub, w_ref[0])
        p1_ref[...] = _dot(ub, w_ref[1])
        y2 = _dot(ub, w_ref[2])
        cosf = jnp.tile(cos_ref[...], (1, 8))
        sinf = jnp.tile(sin_ref[...], (1, 8))
        y3 = _dot(ub, w_ref[3])
        z_ref[...] = y3[:, AT_W:]
        qkv = jnp.concatenate([y2 * cosf + _rope_rot(y2) * sinf, y3[:, :AT_W]], axis=1).astype(BF16)
        qkv_ref[...] = qkv
        _split_bf16(qkv, t4_ref, qkv4_ref, 4)
        _split_bf16(qkv, t16_ref, qkv16_ref, 16)

    row = lambda w: pl.BlockSpec((tm, w), lambda s: (s, 0))
    qkv_w = 3 * AT_W
    perm = _full((PERM_ROWS, PERM_ROWS))
    return pl.pallas_call(
        body, name="inproj", grid=(S // tm,),
        in_specs=[row(D_MODEL), _full((1, D_MODEL)), _full((4, D_MODEL, 1024)), row(128), row(128), perm, perm],
        out_specs=[row(1024), row(1024), row(qkv_w), _dil_spec(4, tm, qkv_w), _dil_spec(16, tm, qkv_w), row(AT_W),
                   row(D_MODEL)],
        out_shape=[jax.ShapeDtypeStruct((S, 1024), F32), jax.ShapeDtypeStruct((S, 1024), F32),
                   jax.ShapeDtypeStruct((S, qkv_w), BF16), jax.ShapeDtypeStruct((4, S // 4, qkv_w), BF16),
                   jax.ShapeDtypeStruct((16, S // 16, qkv_w), BF16), jax.ShapeDtypeStruct((S, AT_W), F32),
                   jax.ShapeDtypeStruct((S, D_MODEL), BF16)],
        compiler_params=_params(("parallel",)),
    )(x2, norm_w, w_in_full, cos2, sin2, _perm_matrix(4, True), _perm_matrix(16, True))


HG_HPS = 4
HG_CPS = 4
N_LEV = len(LEVELS)


def _hg_const_arrays():
    r = np.arange(CHUNK)[:, None]
    c = np.arange(CHUNK)[None, :]
    tris = np.stack([r >= c, r <= c])
    lm = [((r // (2 * m)) == (c // (2 * m))) & (r % (2 * m) >= m) & (c % (2 * m) < m) for m in LEVELS]
    dm = [(c == r - dl) & (r % DIAG >= dl) for dl in range(DIAG)]
    masks = np.stack(lm + dm)
    return jnp.asarray(tris, BF16), jnp.asarray(masks, F32)


def _split2(a):
    hi = a.astype(BF16)
    return hi, (a - hi.astype(F32)).astype(BF16)


def _dot3(a, b, dot=_dot):
    ah, al = _split2(a)
    bh, bl = _split2(b)
    n = b.shape[1]
    p = dot(ah, jnp.concatenate([bh, bl], axis=1))
    return (p[:, :n] + p[:, n:]) + dot(al, bh)


def _split3(a):
    a1 = a.astype(BF16)
    r1 = a - a1.astype(F32)
    a2 = r1.astype(BF16)
    return a1, a2, (r1 - a2.astype(F32)).astype(BF16)


def _tri_dot(tri, a):
    n = a.shape[1]
    p = _dot(tri, jnp.concatenate(_split3(a), axis=1))
    return (p[:, :n] + p[:, n:2 * n]) + p[:, 2 * n:]


def _dot_sel(a, sel):
    a1, a2, a3 = _split3(a)
    return (_dot(a1, sel) + _dot(a2, sel)) + _dot(a3, sel)


def _rowsum(t):
    return _dot(t.astype(BF16), jnp.ones((t.shape[1], t.shape[1]), BF16))


def _level_refs(b):
    refs = []
    pos = _iota(b.shape, 0)
    for m in LEVELS:
        if 2 * m >= SUBLANES:
            parts = [jnp.broadcast_to(b[r0 + m - 1: r0 + m, :], (2 * m, b.shape[1])) for r0 in range(0, CHUNK, 2 * m)]
            refs.append(parts[0] if len(parts) == 1 else jnp.concatenate(parts, axis=0))
        else:
            p = pos & (2 * m - 1)
            ref = b
            for off in range(-(m - 1), m + 1):
                if off != 0:
                    ref = jnp.where(p == m - 1 + off, pltpu.roll(b, off % CHUNK, 0), ref)
            refs.append(ref)
    return refs


def _hg_lb(lbl_ref):
    l0 = lbl_ref[0:1, :]
    l1 = lbl_ref[1:2, :]
    mx = jnp.maximum(l0, l1)
    e0 = jnp.exp(l0 - mx)
    e1 = jnp.exp(l1 - mx)
    p0 = e0 / (e0 + e1)
    lb = jnp.clip(p0, 1e-6, 1.0 - 1e-6)
    inside = (p0 >= 1e-6) & (p0 <= 1.0 - 1e-6)
    dlb_dl0 = jnp.where(inside, p0 * (e1 / (e0 + e1)), 0.0)
    return lb, dlb_dl0


def _hg_gates(qr, fl, lb):
    sig = _sigmoid(fl)
    f = lb + (1.0 - lb) * sig
    g = jnp.log(f)
    k = (1.0 - lb) * (1.0 - sig)
    sq = _sigmoid(qr)
    q = qr * sq
    return sig, f, g, k, sq, q


def _neg_abs(v):
    bits = lax.bitcast_convert_type(v, jnp.uint32) | jnp.uint32(0x80000000)
    return lax.bitcast_convert_type(bits, F32)


def _hg_levels(q, k, b, mk_ref):
    refs = _level_refs(b)
    a = jnp.zeros((CHUNK, CHUNK), F32)
    es, qts, kts = [], [], []
    for i in range(N_LEV):
        e = jnp.exp(_neg_abs(b - refs[i]))
        qt = (q * e).astype(BF16)
        kt = (k * e).astype(BF16)
        a = a + _dot_nt(qt, kt) * mk_ref[i]
        es.append(e); qts.append(qt); kts.append(kt)
    return a, es, qts, kts


def _hg_specs(nc, rev):
    cc = (lambda c: nc - 1 - c) if rev else (lambda c: c)
    w = HG_HEAD * HG_HPS
    blk = lambda off: pl.BlockSpec((HG_CPS * CHUNK, w), lambda h, c: (cc(c), h + off))
    lb2 = pl.BlockSpec((2, w), lambda h, c: (0, h))
    st = pl.BlockSpec((HG_CPS, HG_HPS, HG_HEAD, HG_HEAD), lambda h, c: (cc(c), h, 0, 0))
    consts = [_full((2, CHUNK, CHUNK)), _full((N_LEV + DIAG, CHUNK, CHUNK))]
    return blk, lb2, st, consts


def _hg_fwd(p0, p1, lbl):
    S = p0.shape[0]
    nc = S // (HG_CPS * CHUNK)
    ng = HG_HEADS // HG_HPS

    def body(qr_ref, fl_ref, v_ref, lbl_ref, tri_ref, mk_ref, o_ref, st_ref, state):
        c = pl.program_id(1)

        @pl.when(c == 0)
        def _():
            state[...] = jnp.zeros_like(state)

        lb_all, _ = _hg_lb(lbl_ref)
        heads = [slice(HG_HEAD * hh, HG_HEAD * (hh + 1)) for hh in range(HG_HPS)]
        for j in range(HG_CPS):
            rows = slice(CHUNK * j, CHUNK * (j + 1))
            qs, ks, bs, mats = [], [], [], []
            for sl in heads:
                _, _, g, k, _, q = _hg_gates(qr_ref[rows, sl], fl_ref[rows, sl], lb_all[:, sl])
                qs.append(q); ks.append(k); bs.append(_tri_dot(tri_ref[0], g))
            for hh in range(HG_HPS):
                a, _, _, _ = _hg_levels(qs[hh], ks[hh], bs[hh], mk_ref)
                mats.append(a + _rowsum(qs[hh] * ks[hh]) * mk_ref[N_LEV])
            for hh, sl in enumerate(heads):
                q, k, b, a, v = qs[hh], ks[hh], bs[hh], mats[hh], v_ref[rows, sl]
                b_last = b[CHUNK - 1: CHUNK, :]
                st = state[hh]
                st_ref[j, hh] = st
                o = _dot_nt((q * jnp.exp(b)).astype(BF16), st.astype(BF16)) + _dot(a.astype(BF16), v.astype(BF16))
                state[hh] = st * jnp.exp(b_last) + _dot3(v, k * jnp.exp(b_last - b), _dot_tn)
                o_ref[rows, sl] = o

    blk, lb2, st_spec, consts = _hg_specs(nc, False)
    tris, masks = _hg_const_arrays()
    return pl.pallas_call(
        body, name="hg_fwd", grid=(ng, nc),
        in_specs=[blk(0), blk(ng), blk(0), lb2] + consts,
        out_specs=[blk(0), st_spec],
        out_shape=[jax.ShapeDtypeStruct((S, HG_W), F32), jax.ShapeDtypeStruct((S // CHUNK, HG_HEADS, HG_HEAD, HG_HEAD), F32)],
        scratch_shapes=[pltpu.VMEM((HG_HPS, HG_HEAD, HG_HEAD), F32)],
        compiler_params=_params(("parallel", "arbitrary")),
    )(p0, p0, p1, lbl, tris, masks)


def _hg_bwd(p0, p1, do_hg, states, lbl):
    S = p0.shape[0]
    nc = S // (HG_CPS * CHUNK)
    ng = HG_HEADS // HG_HPS
    w = HG_HEAD * HG_HPS

    def body(qr_ref, fl_ref, v_ref, do_ref, st_ref, lbl_ref, tri_ref, mk_ref, mkb_ref,
             dqr_ref, dfl_ref, dv_ref, glbl_ref, dstate, carry, acc_lb):
        c = pl.program_id(1)

        @pl.when(c == 0)
        def _():
            dstate[...] = jnp.zeros_like(dstate)
            carry[...] = jnp.zeros_like(carry)
            acc_lb[...] = jnp.zeros_like(acc_lb)

        lb_all, dlb_dl0 = _hg_lb(lbl_ref)
        heads = [slice(HG_HEAD * hh, HG_HEAD * (hh + 1)) for hh in range(HG_HPS)]
        diag_mask = mk_ref[N_LEV]
        for j in reversed(range(HG_CPS)):
            rows = slice(CHUNK * j, CHUNK * (j + 1))
            gates, bs, dos = [], [], []
            for sl in heads:
                gt = _hg_gates(qr_ref[rows, sl], fl_ref[rows, sl], lb_all[:, sl])
                gates.append(gt)
                bs.append(_tri_dot(tri_ref[0], gt[2]))
            for sl in heads:
                dos.append(do_ref[rows, sl])
            inter = []
            for hh, sl in enumerate(heads):
                _, _, _, k, _, q = gates[hh]
                b, do, v = bs[hh], dos[hh], v_ref[rows, sl]
                b_last = b[CHUNK - 1: CHUNK, :]
                eb = jnp.exp(b)
                edec = jnp.exp(b_last - b)
                dst = dstate[hh]
                dq = _dot3(do, st_ref[j, hh]) * eb
                dk = _dot3(v, dst) * edec
                da = _dot_nt(do.astype(BF16), v.astype(BF16))
                dv_state = _dot_nt((k * edec).astype(BF16), dst.astype(BF16))
                dstate[hh] = dst * jnp.exp(b_last) + _dot3(do, q * eb, _dot_tn)
                inter.append((dq, dk, da, dv_state))
            for hh, sl in enumerate(heads):
                sig, f, _, k, sq, q = gates[hh]
                dq, dk, da, dv_state = inter[hh]
                b = bs[hh]
                db = q * dq - k * dk
                dab = da.astype(BF16)
                refs = _level_refs(b)
                a = _rowsum(q * k) * diag_mask
                for i in range(N_LEV):
                    e = jnp.exp(_neg_abs(b - refs[i]))
                    qt = (q * e).astype(BF16)
                    kt = (k * e).astype(BF16)
                    a = a + _dot_nt(qt, kt) * mk_ref[i]
                    dam = dab * mkb_ref[i]
                    gq = _dot(dam, kt)
                    gk = _dot_tn(dam, qt)
                    dq = dq + e * gq
                    dk = dk + e * gk
                    db = db + (qt.astype(F32) * gq - kt.astype(F32) * gk)
                dc = _rowsum(da * diag_mask)
                dq = dq + dc * k
                dk = dk + dc * q
                dv_ref[rows, sl] = (_dot_tn(a.astype(BF16), dos[hh].astype(BF16)) + dv_state).astype(BF16)
                dg = _tri_dot(tri_ref[1], db) + carry[0:1, sl]
                carry[0:1, sl] += jnp.sum(db, axis=0, keepdims=True)
                lb = lb_all[:, sl]
                qr = qr_ref[rows, sl]
                t = dg / f - dk
                dfl_ref[rows, sl] = (t * (1.0 - lb) * sig * (1.0 - sig)).astype(BF16)
                acc_lb[0:1, sl] += jnp.sum(t * (1.0 - sig), axis=0, keepdims=True)
                dqr_ref[rows, sl] = (dq * (sq * (1.0 + qr * (1.0 - sq)))).astype(BF16)

        @pl.when(c == nc - 1)
        def _():
            gl0 = acc_lb[0:1, :] * dlb_dl0
            glbl_ref[0:1, :] = gl0
            glbl_ref[1:2, :] = -gl0

    blk, lb2, st_spec, consts = _hg_specs(nc, True)
    tris, masks = _hg_const_arrays()
    act = jax.ShapeDtypeStruct((S, HG_W), BF16)
    return pl.pallas_call(
        body, name="hg_bwd", grid=(ng, nc),
        in_specs=[blk(0), blk(ng), blk(0), blk(0), st_spec, lb2] + consts + consts[1:],
        out_specs=[blk(0), blk(0), blk(0), lb2],
        out_shape=[act, act, act, jax.ShapeDtypeStruct((2, HG_W), F32)],
        scratch_shapes=[pltpu.VMEM((HG_HPS, HG_HEAD, HG_HEAD), F32), pltpu.VMEM((8, w), F32), pltpu.VMEM((8, w), F32)],
        compiler_params=_params(("parallel", "arbitrary")),
    )(p0, p0, p1, do_hg, states, lbl, tris, masks, masks.astype(BF16))


def _stat_head(lane):
    return lane >> 4


def _stat_lane(lane):
    return lane & (STAT_LANES - 1)


def _o_head(lane):
    return lane >> 6


def _att_bias():
    qi = np.arange(ATT_BLK)[:, None]
    kj = np.arange(2 * ATT_BLK)[None, :]
    band = (kj >= qi) & (kj <= qi + ATT_BLK)
    qm = np.stack([band & (kj >= ATT_BLK), band])
    cur = (kj < ATT_BLK) & (qi <= kj)
    km = np.stack([cur, cur | ((kj >= ATT_BLK) & (qi >= kj - ATT_BLK))])
    to_bias = lambda m: jnp.asarray(np.where(m, 0.0, NEG), F32)
    return to_bias(qm), to_bias(km)


def _att_fwd(qkv_d):
    d, L, _ = qkv_d.shape
    qb = ATT_QB
    rows = qb * ATT_BLK

    def body(q_ref, kp_ref, kc_ref, vp_ref, vc_ref, bias_ref, o_ref, lse_ref, s_scr, p_scr):
        first = _iota((ATT_BLK, HEAD_PAIR), 1) < AT_HEAD
        head_of_lane = _stat_head(_iota((ATT_BLK, STAT_W), 1))
        pairs = [slice(HEAD_PAIR * hp, HEAD_PAIR * (hp + 1)) for hp in range(4)]
        blk = lambda i: slice(ATT_BLK * i, ATT_BLK * (i + 1))

        def keys(i, sl, prev_ref, cur_ref):
            before = prev_ref[0, :, sl] if i == 0 else cur_ref[0, blk(i - 1), sl]
            return jnp.concatenate([before, cur_ref[0, blk(i), sl]], axis=0)

        for i in range(qb):
            for hp, sl in enumerate(pairs):
                q2 = q_ref[0, blk(i), sl] * ATT_SCALE
                zero = jnp.zeros_like(q2)
                qs = jnp.concatenate([jnp.where(first, q2, zero), jnp.where(first, zero, q2)], axis=0)
                s_scr[4 * i + hp] = _dot_nt(qs, keys(i, sl, kp_ref, kc_ref))
        stats = []
        for i in range(qb):
            bias = bias_ref[jnp.minimum(pl.program_id(1), 1)] if i == 0 else bias_ref[1]
            bias2 = jnp.concatenate([bias, bias], axis=0)
            for hp in range(4):
                s = s_scr[4 * i + hp] + bias2
                m = jnp.max(s, axis=-1, keepdims=True)
                p = jnp.exp(s - m)
                l = jnp.sum(p, axis=-1, keepdims=True)
                p_scr[4 * i + hp] = p.astype(BF16)
                stats.append((l, m + jnp.log(l)))
        for i in range(qb):
            lse_blk = jnp.zeros((ATT_BLK, STAT_W), F32)
            for hp, sl in enumerate(pairs):
                l, lse = stats[4 * i + hp]
                o = _dot(p_scr[4 * i + hp], keys(i, sl, vp_ref, vc_ref)) / l
                o_ref[0, blk(i), sl] = jnp.where(first, o[:ATT_BLK], o[ATT_BLK:]).astype(BF16)
                lse_blk = jnp.where(head_of_lane == 2 * hp, lse[:ATT_BLK],
                                    jnp.where(head_of_lane == 2 * hp + 1, lse[ATT_BLK:], lse_blk))
            lse_ref[0, blk(i), :] = lse_blk

    cur = lambda j: pl.BlockSpec((1, rows, AT_W), lambda r, n: (r, n, j))
    prev = lambda j: pl.BlockSpec((1, ATT_BLK, AT_W), lambda r, n: (r, jnp.maximum(qb * n - 1, 0), j))
    return pl.pallas_call(
        body, name=f"att_fwd_d{d}", grid=(d, L // rows),
        in_specs=[cur(0), prev(1), cur(1), prev(2), cur(2), _full((2, ATT_BLK, 2 * ATT_BLK))],
        out_specs=[pl.BlockSpec((1, rows, AT_W), lambda r, n: (r, n, 0)), pl.BlockSpec((1, rows, STAT_W), lambda r, n: (r, n, 0))],
        out_shape=[jax.ShapeDtypeStruct((d, L, AT_W), BF16), jax.ShapeDtypeStruct((d, L, STAT_W), F32)],
        scratch_shapes=[pltpu.VMEM((4 * qb, 2 * ATT_BLK, 2 * ATT_BLK), F32), pltpu.VMEM((4 * qb, 2 * ATT_BLK, 2 * ATT_BLK), BF16)],
        compiler_params=_params(("parallel", "parallel")),
    )(qkv_d, qkv_d, qkv_d, qkv_d, qkv_d, _att_bias()[0])


def _att_combine(os_d, ls_d, z_at):
    S = z_at.shape[0]
    tm = MIX_TM

    def body(oa_ref, ob4_ref, oc16_ref, la_ref, lb4_ref, lc16_ref, p4_ref, p16_ref, z_ref,
             oat_ref, lse_ref, mix_ref, lb_scr, lc_scr):
        _merge_residues(lb4_ref, lb_scr, 4)
        _merge_residues(lc16_ref, lc_scr, 16)
        ls = (la_ref[...], lb_scr[0], lc_scr[0])
        mx = jnp.maximum(jnp.maximum(ls[0], ls[1]), ls[2])
        es = [jnp.exp(l - mx) for l in ls]
        zs = es[0] + es[1] + es[2]
        lse_ref[...] = mx + jnp.log(zs)
        spread = ((_o_head(_iota((STAT_W, AT_W), 1)) == _stat_head(_iota((STAT_W, AT_W), 0)))
                  & (_stat_lane(_iota((STAT_W, AT_W), 0)) == 0)).astype(BF16)
        os_ = (oa_ref[...].astype(F32), _merge_bf16(ob4_ref, p4_ref, 4), _merge_bf16(oc16_ref, p16_ref, 16))
        o = jnp.zeros((tm, AT_W), F32)
        for e, oi in zip(es, os_):
            hi, lo = _split2(e / zs)
            o = o + (_dot(hi, spread) + _dot(lo, spread)) * oi
        oat_ref[...] = o
        z = z_ref[...]
        mixed = o * (z * _sigmoid(z))
        mix_ref[...] = mixed.astype(BF16)

    row = lambda w: pl.BlockSpec((tm, w), lambda s: (s, 0))
    return pl.pallas_call(
        body, name="att_combine", grid=(S // tm,),
        in_specs=[row(AT_W), _dil_spec(4, tm, AT_W), _dil_spec(16, tm, AT_W),
                  row(STAT_W), _dil_spec(4, tm, STAT_W), _dil_spec(16, tm, STAT_W),
                  _full((PERM_ROWS, PERM_ROWS)), _full((PERM_ROWS, PERM_ROWS)), row(AT_W)],
        out_specs=[row(AT_W), row(STAT_W), row(AT_W)],
        out_shape=[jax.ShapeDtypeStruct((S, AT_W), F32), jax.ShapeDtypeStruct((S, STAT_W), F32),
                   jax.ShapeDtypeStruct((S, AT_W), BF16)],
        scratch_shapes=[_slab_scratch(tm, STAT_W), _slab_scratch(tm, STAT_W)],
        compiler_params=_params(("parallel",)),
    )(os_d[0].reshape(S, AT_W), os_d[1], os_d[2], ls_d[0].reshape(S, STAT_W), ls_d[1], ls_d[2],
      _perm_matrix(4), _perm_matrix(16), z_at)


def _att_gate_bwd(dm_at, o_at, lse, z_at):
    S = z_at.shape[0]
    tm = MIX_TM

    def body(dm_ref, o_ref, l_ref, z_ref, t4_ref, t16_ref, do_ref, do4_ref, do16_ref, dl_ref, dl4_ref, dl16_ref, dz_ref,
             dl_scr):
        o = o_ref[...]
        z = z_ref[...]
        dm = dm_ref[...]
        sz = _sigmoid(z)
        dz_ref[...] = (dm * o * (sz * (1.0 + z * (1.0 - sz)))).astype(BF16)
        do = dm * (z * sz)
        dob = do.astype(BF16)
        do_ref[...] = dob
        gather = (_o_head(_iota((AT_W, STAT_W), 0)) == _stat_head(_iota((AT_W, STAT_W), 1))).astype(BF16)
        dl = jnp.where(_stat_lane(_iota((tm, STAT_W), 1)) < STAT_LSE_LANE, _dot_sel(do * o, gather), l_ref[...])
        dl_ref[...] = dl
        _split_bf16(dob, t4_ref, do4_ref, 4)
        _split_bf16(dob, t16_ref, do16_ref, 16)
        _to_slabs(dl, dl_scr)
        _split_residues(dl_scr, dl4_ref, 4, F32)
        _split_residues(dl_scr, dl16_ref, 16, F32)

    row = lambda w: pl.BlockSpec((tm, w), lambda s: (s, 0))
    sds = jax.ShapeDtypeStruct
    return pl.pallas_call(
        body, name="att_gate_bwd", grid=(S // tm,),
        in_specs=[row(AT_W), row(AT_W), row(STAT_W), row(AT_W)] + [_full((PERM_ROWS, PERM_ROWS))] * 2,
        out_specs=[row(AT_W), _dil_spec(4, tm, AT_W), _dil_spec(16, tm, AT_W),
                   row(STAT_W), _dil_spec(4, tm, STAT_W), _dil_spec(16, tm, STAT_W), row(AT_W)],
        out_shape=[sds((S, AT_W), BF16), sds((4, S // 4, AT_W), BF16), sds((16, S // 16, AT_W), BF16),
                   sds((S, STAT_W), F32), sds((4, S // 4, STAT_W), F32), sds((16, S // 16, STAT_W), F32),
                   sds((S, AT_W), BF16)],
        scratch_shapes=[_slab_scratch(tm, STAT_W)],
        compiler_params=_params(("parallel",)),
    )(dm_at, o_at, lse, z_at, _perm_matrix(4, True), _perm_matrix(16, True))


def _att_bwd(qkv_d, do_d, dl_d):
    d, L, _ = qkv_d.shape
    nb = L // ATT_BLK
    qb = ATT_QB
    rows = qb * ATT_BLK
    nsteps = L // rows

    def body(qc_ref, qn_ref, kp_ref, kc_ref, vp_ref, vc_ref, ac_ref, an_ref, lc_ref, ln_ref, bq_ref, bk_ref, dqkv_ref,
             s_scr, dp_scr, st_scr, dpt_scr, ds_scr, pt_scr, dst_scr):
        n = pl.program_id(1)
        first = _iota((ATT_BLK, HEAD_PAIR), 1) < AT_HEAD
        pairs = [slice(HEAD_PAIR * hp, HEAD_PAIR * (hp + 1)) for hp in range(4)]
        blk = lambda i: slice(ATT_BLK * i, ATT_BLK * (i + 1))

        def stack(t):
            zero = jnp.zeros_like(t)
            return jnp.concatenate([jnp.where(first, t, zero), jnp.where(first, zero, t)], axis=0)

        def unstack(t2):
            return jnp.where(first, t2[:ATT_BLK], t2[ATT_BLK:])

        def with_prev(i, sl, prev_ref, cur_ref):
            before = prev_ref[0, :, sl] if i == 0 else cur_ref[0, blk(i - 1), sl]
            return jnp.concatenate([before, cur_ref[0, blk(i), sl]], axis=0)

        def with_next(i, sl, cur_ref, next_ref):
            after = next_ref[0, :, sl] if i == qb - 1 else cur_ref[0, blk(i + 1), sl]
            return jnp.concatenate([cur_ref[0, blk(i), sl], after], axis=0)

        for i in range(qb):
            for hp, sl in enumerate(pairs):
                j = 4 * i + hp
                s_scr[j] = _dot_nt(stack(qc_ref[0, blk(i), sl] * ATT_SCALE), with_prev(i, sl, kp_ref, kc_ref))
                dp_scr[j] = _dot_nt(stack(ac_ref[0, blk(i), sl]), with_prev(i, sl, vp_ref, vc_ref))
                st_scr[j] = _dot_nt(stack(kc_ref[0, blk(i), sl] * ATT_SCALE), with_next(i, sl, qc_ref, qn_ref))
                dpt_scr[j] = _dot_nt(stack(vc_ref[0, blk(i), sl]), with_next(i, sl, ac_ref, an_ref))
        for i in range(qb):
            bias = bq_ref[jnp.minimum(n, 1)] if i == 0 else bq_ref[1]
            bias_t = bk_ref[jnp.minimum(nsteps - 1 - n, 1)] if i == qb - 1 else bk_ref[1]
            bias2 = jnp.concatenate([bias, bias], axis=0)
            bias_t2 = jnp.concatenate([bias_t, bias_t], axis=0)
            dl_c = lc_ref[0, blk(i), :]
            dl_t = with_next(i, slice(None), lc_ref, ln_ref).T
            for hp in range(4):
                j = 4 * i + hp
                at = [STAT_LANES * (2 * hp), STAT_LANES * (2 * hp + 1)]
                col = lambda t, o: jnp.concatenate([t[:, a + o: a + o + 1] for a in at], axis=0)
                p = jnp.exp(s_scr[j] + bias2 - col(dl_c, STAT_LSE_LANE))
                ds_scr[j] = (p * (dp_scr[j] - col(dl_c, 0))).astype(BF16)
                row = lambda t, o: jnp.concatenate([jnp.broadcast_to(t[a + o: a + o + 1, :], (ATT_BLK, 2 * ATT_BLK)) for a in at], axis=0)
                pt = jnp.exp(st_scr[j] + bias_t2 - row(dl_t, STAT_LSE_LANE))
                pt_scr[j] = pt.astype(BF16)
                dst_scr[j] = (pt * (dpt_scr[j] - row(dl_t, 0))).astype(BF16)
        for i in range(qb):
            for hp, sl in enumerate(pairs):
                j = 4 * i + hp
                dq = unstack(_dot(ds_scr[j], with_prev(i, sl, kp_ref, kc_ref))) * ATT_SCALE
                dk = unstack(_dot(dst_scr[j], with_next(i, sl, qc_ref, qn_ref))) * ATT_SCALE
                dv = unstack(_dot(pt_scr[j], with_next(i, sl, ac_ref, an_ref)))
                dqkv_ref[0, blk(i), sl] = dq.astype(BF16)
                dqkv_ref[0, blk(i), AT_W + HEAD_PAIR * hp: AT_W + HEAD_PAIR * (hp + 1)] = dk.astype(BF16)
                dqkv_ref[0, blk(i), 2 * AT_W + HEAD_PAIR * hp: 2 * AT_W + HEAD_PAIR * (hp + 1)] = dv.astype(BF16)

    cur = lambda j: pl.BlockSpec((1, rows, AT_W), lambda r, n: (r, n, j))
    prev = lambda j: pl.BlockSpec((1, ATT_BLK, AT_W), lambda r, n: (r, jnp.maximum(qb * n - 1, 0), j))
    nxt_blk = lambda n: jnp.minimum(qb * (n + 1), nb - 1)
    sq = (4 * qb, 2 * ATT_BLK, 2 * ATT_BLK)
    return pl.pallas_call(
        body, name=f"att_bwd_d{d}", grid=(d, nsteps),
        in_specs=[cur(0), pl.BlockSpec((1, ATT_BLK, AT_W), lambda r, n: (r, nxt_blk(n), 0)), prev(1), cur(1), prev(2), cur(2),
                  pl.BlockSpec((1, rows, AT_W), lambda r, n: (r, n, 0)),
                  pl.BlockSpec((1, ATT_BLK, AT_W), lambda r, n: (r, nxt_blk(n), 0)),
                  pl.BlockSpec((1, rows, STAT_W), lambda r, n: (r, n, 0)),
                  pl.BlockSpec((1, ATT_BLK, STAT_W), lambda r, n: (r, nxt_blk(n), 0)),
                  _full((2, ATT_BLK, 2 * ATT_BLK)), _full((2, ATT_BLK, 2 * ATT_BLK))],
        out_specs=pl.BlockSpec((1, rows, 3 * AT_W), lambda r, n: (r, n, 0)),
        out_shape=jax.ShapeDtypeStruct((d, L, 3 * AT_W), BF16),
        scratch_shapes=[pltpu.VMEM(sq, F32)] * 4 + [pltpu.VMEM(sq, BF16)] * 3,
        compiler_params=_params(("parallel", "parallel")),
    )(qkv_d, qkv_d, qkv_d, qkv_d, qkv_d, qkv_d, do_d, do_d, dl_d, dl_d, *_att_bias())


def _att_bwd_combine(dqkvs, cos2, sin2):
    S = dqkvs[0].shape[1]
    tm = MIX_TM

    def body(a_ref, b4_ref, c16_ref, p4_ref, p16_ref, cos_ref, sin_ref, dq_ref, dk_ref, dv_ref):
        t = a_ref[...].astype(F32) + _merge_bf16(b4_ref, p4_ref, 4) + _merge_bf16(c16_ref, p16_ref, 16)
        dy = t[:, : 2 * AT_W]
        cosf = jnp.tile(cos_ref[...], (1, 8))
        sinf = jnp.tile(sin_ref[...], (1, 8))
        dx = dy * cosf - _rope_rot(dy) * sinf
        dq_ref[...] = dx[:, :AT_W].astype(BF16)
        dk_ref[...] = dx[:, AT_W:].astype(BF16)
        dv_ref[...] = t[:, 2 * AT_W:].astype(BF16)

    row = lambda w: pl.BlockSpec((tm, w), lambda s: (s, 0))
    act = jax.ShapeDtypeStruct((S, AT_W), BF16)
    return pl.pallas_call(
        body, name="att_bwd_combine", grid=(S // tm,),
        in_specs=[row(3 * AT_W), _dil_spec(4, tm, 3 * AT_W), _dil_spec(16, tm, 3 * AT_W),
                  _full((PERM_ROWS, PERM_ROWS)), _full((PERM_ROWS, PERM_ROWS)), row(128), row(128)],
        out_specs=[row(AT_W), row(AT_W), row(AT_W)],
        out_shape=[act, act, act],
        compiler_params=_params(("parallel",)),
    )(dqkvs[0].reshape(S, 3 * AT_W), dqkvs[1], dqkvs[2], _perm_matrix(4), _perm_matrix(16), cos2, sin2)


def _outproj(x2, tgt2, mix_at, w_out_full, fnw, o_hg, p1, hg_norm_w):
    S = x2.shape[0]
    tm = PROJ_TM
    ns = S // tm

    def body(x_ref, t_ref, ma_ref, w_ref, fw_ref, o_ref, z_ref, hgw_ref,
             dh_ref, doh_ref, dzh_ref, dma_ref, gw_ref, gfw_ref, ghgw_ref, loss_ref):
        s = pl.program_id(0)

        @pl.when(s == 0)
        def _():
            gw_ref[...] = jnp.zeros_like(gw_ref)
            gfw_ref[...] = jnp.zeros_like(gfw_ref)
            ghgw_ref[...] = jnp.zeros_like(ghgw_ref)
            loss_ref[...] = jnp.zeros_like(loss_ref)

        heads = [slice(HG_HEAD * hh, HG_HEAD * (hh + 1)) for hh in range(HG_HEADS)]
        norm = []
        for sl in heads:
            o, z = o_ref[:, sl], z_ref[:, sl]
            rs = lax.rsqrt(jnp.mean(o * o, axis=-1, keepdims=True) + EPS)
            norm.append((rs, o * rs, _sigmoid(z)))
        mh = jnp.concatenate([(oh * hgw_ref[:, sl] * (z_ref[:, sl] * sz)).astype(BF16)
                              for sl, (_, oh, sz) in zip(heads, norm)], axis=1)
        y = _dot(mh, w_ref[:HG_W, :]) + _dot(ma_ref[...], w_ref[HG_W:, :])
        h = x_ref[...] + y
        r = lax.rsqrt(jnp.mean(h * h, axis=-1, keepdims=True) + EPS)
        hn = h * r
        fw = fw_ref[...]
        err = hn * fw - t_ref[...]
        loss_ref[...] += 0.5 * jnp.sum(jnp.mean(err * err, axis=-1, keepdims=True))
        dout = err * (1.0 / D_MODEL)
        gfw_ref[...] += jnp.sum(dout * hn, axis=0, keepdims=True)
        dhn = dout * fw
        dh = r * (dhn - hn * jnp.mean(dhn * hn, axis=-1, keepdims=True))
        dh_ref[...] = dh
        dhb = dh.astype(BF16)
        dma_ref[...] = _dot_nt(dhb, w_ref[HG_W:, :])
        dmh = _dot_nt(dhb, w_ref[:HG_W, :])
        for sl, (rs, oh, sz) in zip(heads, norm):
            z, dmix, gwv = z_ref[:, sl], dmh[:, sl], hgw_ref[:, sl]
            dzh_ref[:, sl] = (dmix * (oh * gwv) * (sz * (1.0 + z * (1.0 - sz)))).astype(BF16)
            don = dmix * (z * sz)
            ghgw_ref[:, sl] += jnp.sum(don * oh, axis=0, keepdims=True)
            dy = don * gwv
            doh_ref[:, sl] = rs * (dy - oh * jnp.mean(dy * oh, axis=-1, keepdims=True))
        gw_ref[:HG_W, :] += _dot_tn(mh, dhb)
        gw_ref[HG_W:, :] += _dot_tn(ma_ref[...], dhb)

    row = lambda w: pl.BlockSpec((tm, w), lambda s: (s, 0))
    return pl.pallas_call(
        body, name="outproj", grid=(ns,),
        in_specs=[row(D_MODEL), row(D_MODEL), row(AT_W),
                  _full((D_MODEL, D_MODEL)), _full((1, D_MODEL)),
                  row(HG_W), pl.BlockSpec((tm, HG_W), lambda s: (s, 1)), _full((1, HG_W))],
        out_specs=[row(D_MODEL), row(HG_W), row(HG_W), row(AT_W), _full((D_MODEL, D_MODEL)), _full((1, D_MODEL)),
                   _full((1, HG_W)), _full((8, 128))],
        out_shape=[jax.ShapeDtypeStruct((S, D_MODEL), F32), jax.ShapeDtypeStruct((S, HG_W), F32),
                   jax.ShapeDtypeStruct((S, HG_W), BF16), jax.ShapeDtypeStruct((S, AT_W), F32),
                   jax.ShapeDtypeStruct((D_MODEL, D_MODEL), F32), jax.ShapeDtypeStruct((1, D_MODEL), F32),
                   jax.ShapeDtypeStruct((1, HG_W), F32), jax.ShapeDtypeStruct((8, 128), F32)],
        compiler_params=_params(("arbitrary",)),
    )(x2, tgt2, mix_at, w_out_full, fnw, o_hg, p1, hg_norm_w)


def _inproj_bwd_x(dps, w_in_full, x2, norm_w, dh):
    S = x2.shape[0]
    tm = PROJ_TM

    def body(d0, d1, d2, d3, d4, d5, d6, d7, w_ref, x_ref, nw_ref, dh_ref, gx_ref, gnw_ref):
        s = pl.program_id(0)

        @pl.when(s == 0)
        def _():
            gnw_ref[...] = jnp.zeros_like(gnw_ref)

        du = jnp.zeros((tm, D_MODEL), F32)
        for i, dref in enumerate((d0, d1, d2, d3, d4, d5, d6, d7)):
            j, half = divmod(i, 2)
            du = du + _dot_nt(dref[...], w_ref[j, :, 512 * half: 512 * (half + 1)])
        x = x_ref[...]
        r = lax.rsqrt(jnp.mean(x * x, axis=-1, keepdims=True) + EPS)
        xh = x * r
        gnw_ref[...] += jnp.sum(du * xh, axis=0, keepdims=True)
        dun = du * nw_ref[...]
        gx_ref[...] = dh_ref[...] + r * (dun - xh * jnp.mean(dun * xh, axis=-1, keepdims=True))

    row = lambda w: pl.BlockSpec((tm, w), lambda s: (s, 0))
    return pl.pallas_call(
        body, name="inproj_bwd_x", grid=(S // tm,),
        in_specs=[row(512)] * 8 + [_full((4, D_MODEL, 1024)), row(D_MODEL), _full((1, D_MODEL)), row(D_MODEL)],
        out_specs=[row(D_MODEL), _full((1, D_MODEL))],
        out_shape=[jax.ShapeDtypeStruct((S, D_MODEL), F32), jax.ShapeDtypeStruct((1, D_MODEL), F32)],
        compiler_params=_params(("arbitrary",)),
    )(*dps, w_in_full, x2, norm_w, dh)


def _inproj_bwd_w(u, dps):
    S = u.shape[0]
    tm = 2 * PROJ_TM

    def body(u_ref, d0, d1, d2, d3, d4, d5, d6, d7, g_ref):
        @pl.when(pl.program_id(0) == 0)
        def _():
            g_ref[...] = jnp.zeros_like(g_ref)

        ub = u_ref[...]
        for i, dref in enumerate((d0, d1, d2, d3, d4, d5, d6, d7)):
            j, half = divmod(i, 2)
            g_ref[j, :, 512 * half: 512 * (half + 1)] += _dot_tn(ub, dref[...])

    return pl.pallas_call(
        body, name="inproj_bwd_w", grid=(S // tm,),
        in_specs=[pl.BlockSpec((tm, D_MODEL), lambda s: (s, 0))] + [pl.BlockSpec((tm, 512), lambda s: (s, 0))] * 8,
        out_specs=_full((4, D_MODEL, 1024)),
        out_shape=jax.ShapeDtypeStruct((4, D_MODEL, 1024), F32),
        compiler_params=_params(("arbitrary",)),
    )(u, *dps)


def _adamw_update(gg, w_ref, m_ref, v_ref, d_ref, nm_ref, nv_ref):
    nm = ADAM_B1 * m_ref[...] + (1.0 - ADAM_B1) * gg
    nv = ADAM_B2 * v_ref[...] + (1.0 - ADAM_B2) * (gg * gg)
    m_hat = nm / (1.0 - ADAM_B1 ** ADAM_STEP)
    v_hat = nv / (1.0 - ADAM_B2 ** ADAM_STEP)
    d_ref[...] = -ADAM_LR * (m_hat / (jnp.sqrt(v_hat) + ADAM_EPS) + ADAM_WD * w_ref[...])
    nm_ref[...] = nm
    nv_ref[...] = nv


def _adamw_small(red, ws, ms, vs):
    def body(red_ref, *refs):
        w, m, v, g, d, nm, nv = (refs[4 * i: 4 * (i + 1)] for i in range(7))
        loss_ref = refs[28]
        g[0][...] = red_ref[0:1, :]
        g[1][0:1, :] = red_ref[3:4, :HG_W]
        g[1][1:2, :] = red_ref[3:4, HG_W:]
        g[2][...] = red_ref[2:3, :HG_W]
        g[3][...] = red_ref[1:2, :]
        loss_ref[...] = red_ref[4:5, 0:1]
        for i in range(4):
            _adamw_update(g[i][...], w[i], m[i], v[i], d[i], nm[i], nv[i])

    shapes = [a.shape for a in ws]
    sds = [jax.ShapeDtypeStruct(s, F32) for s in shapes]
    out = pl.pallas_call(
        body, name="adamw_small",
        in_specs=[_full(red.shape)] + [_full(s) for s in shapes] * 3,
        out_specs=[_full(s) for s in shapes] * 4 + [_full((1, 1))],
        out_shape=sds * 4 + [jax.ShapeDtypeStruct((1, 1), F32)],
        compiler_params=_params(),
    )(red, *ws, *ms, *vs)
    return out[0:4], out[4:8], out[8:12], out[12:16], out[16]


def _place():
    return lax.axis_index("x"), lax.axis_index("y"), lax.axis_index("c")


def _gather_weights(w_in_s, w_out_s):
    def body(win_ref, wout_ref, fin_ref, fout_ref, send_sems, recv_sems):
        x, y, c = _place()
        me = (x, y, c)
        sib = (x, y, 1 - c)
        mine = 2 * x + y
        fin_ref[mine] = win_ref[...].astype(BF16)
        fout_ref[mine] = wout_ref[...].astype(BF16)
        chips = [(1 - x, y), (x, 1 - y), (1 - x, 1 - y)]

        def halves(chip, half):
            return (fin_ref.at[chip, pl.ds(half * 512, 512), :], fout_ref.at[chip, pl.ds(half * 128, 128), :])

        def copy(k, ref, to):
            return pltpu.make_async_remote_copy(src_ref=ref, dst_ref=ref, send_sem=send_sems.at[k],
                                                recv_sem=recv_sems.at[k], device_id=to, device_id_type=MESH)

        first, passed = [], []
        for j, (cx, cy) in enumerate(chips):
            for a, ref in enumerate(halves(mine, c)):
                first.append(copy(2 * j + a, ref, (cx, cy, c)))
        for cp in first:
            cp.start()
        for j, (cx, cy) in enumerate(chips):
            for a, ref in enumerate(halves(2 * cx + cy, c)):
                copy(2 * j + a, ref, me).wait_recv()
                fwd = copy(6 + 2 * j + a, ref, sib)
                fwd.start()
                passed.append(fwd)
        for j, (cx, cy) in enumerate(chips):
            for a, ref in enumerate(halves(2 * cx + cy, 1 - c)):
                copy(6 + 2 * j + a, ref, me).wait_recv()
        for cp in first + passed:
            cp.wait_send()

    vm = pl.BlockSpec(memory_space=pltpu.VMEM)
    return pl.pallas_call(
        body, name="gather_weights",
        in_specs=[vm, vm], out_specs=[vm, vm],
        out_shape=[jax.ShapeDtypeStruct((4, D_MODEL, 1024), BF16), jax.ShapeDtypeStruct((4, 256, D_MODEL), BF16)],
        scratch_shapes=[pltpu.SemaphoreType.DMA((12,)), pltpu.SemaphoreType.DMA((12,))],
        compiler_params=pltpu.CompilerParams(vmem_limit_bytes=VMEM_LIMIT),
    )(w_in_s, w_out_s)


def _swap_halves(g_in, g_out):
    def body(gin_ref, gout_ref, rin_ref, rout_ref, send_sems, recv_sems):
        x, y, c = _place()
        sib = (x, y, 1 - c)
        cps = [pltpu.make_async_remote_copy(src_ref=src.at[:, 1 - c], dst_ref=dst, send_sem=send_sems.at[k],
                                            recv_sem=recv_sems.at[k], device_id=sib, device_id_type=MESH)
               for k, (src, dst) in enumerate(((gin_ref, rin_ref), (gout_ref, rout_ref)))]
        for cp in cps:
            cp.start()
        for cp in cps:
            cp.wait()

    hbm = pl.BlockSpec(memory_space=pl.ANY)
    return pl.pallas_call(
        body, name="swap_halves",
        in_specs=[hbm, hbm], out_specs=[hbm, hbm],
        out_shape=[jax.ShapeDtypeStruct((4,) + g.shape[2:], F32) for g in (g_in, g_out)],
        scratch_shapes=[pltpu.SemaphoreType.DMA((2,)), pltpu.SemaphoreType.DMA((2,))],
    )(g_in, g_out)


def _add_halves(gs, rs, cidx):
    n = gs[0].shape[0]

    def body(c_ref, *refs):
        k = len(gs)
        for g_ref, r_ref, o_ref in zip(refs[:k], refs[k: 2 * k], refs[2 * k:]):
            o_ref[0] = (g_ref[0, 0] + r_ref[0]).astype(BF16)

    mine = lambda g: pl.BlockSpec((1, 1) + g.shape[2:], lambda j, c_ref: (j, c_ref[0], 0, 0))
    chunk = lambda g: pl.BlockSpec((1,) + g.shape[2:], lambda j, c_ref: (j, 0, 0))
    return pl.pallas_call(
        body, name="add_halves",
        grid_spec=pltpu.PrefetchScalarGridSpec(
            num_scalar_prefetch=1, grid=(n,),
            in_specs=[mine(g) for g in gs] + [chunk(g) for g in gs],
            out_specs=[chunk(g) for g in gs]),
        out_shape=[jax.ShapeDtypeStruct((n,) + g.shape[2:], BF16) for g in gs],
        compiler_params=_params(("parallel",)),
    )(cidx, *gs, *rs)


def _exchange_copies(in_ref, out_ref, lin_ref, lout_ref, send_sems, recv_sems):
    x, y, c = _place()
    cps = []
    for k, (cx, cy) in enumerate([(1 - x, y), (x, 1 - y), (1 - x, 1 - y)]):
        for a, (src, dst) in enumerate(((in_ref, lin_ref), (out_ref, lout_ref))):
            cps.append(pltpu.make_async_remote_copy(
                src_ref=src.at[2 * cx + cy], dst_ref=dst.at[k], send_sem=send_sems.at[2 * k + a],
                recv_sem=recv_sems.at[2 * k + a], device_id=(cx, cy, c), device_id_type=MESH))
    return cps


def _exchange_start(cs_in, cs_out):
    def body(in_ref, out_ref, lin_ref, lout_ref, send_sems, recv_sems, in_thru, out_thru, lin_thru, lout_thru, token):
        for cp in _exchange_copies(in_ref, out_ref, lin_ref, lout_ref, send_sems, recv_sems):
            cp.start()
        token[...] = jnp.zeros_like(token)

    lands = [lax.empty((3,) + a.shape[1:], a.dtype) for a in (cs_in, cs_out)]
    bufs = [pltpu.with_memory_space_constraint(a, pltpu.HBM) for a in (cs_in, cs_out, *lands)]
    hbm = pl.BlockSpec(memory_space=pltpu.HBM)
    sem = pl.BlockSpec(memory_space=pltpu.SEMAPHORE)
    return pl.pallas_call(
        body, name="exchange_start",
        in_specs=[hbm] * 4,
        out_specs=[sem, sem, hbm, hbm, hbm, hbm, pl.BlockSpec(memory_space=pltpu.VMEM)],
        out_shape=[pltpu.SemaphoreType.DMA((6,)), pltpu.SemaphoreType.DMA((6,))]
                  + [pltpu.HBM(b.shape, b.dtype) for b in bufs] + [jax.ShapeDtypeStruct((8, 128), F32)],
        input_output_aliases={0: 2, 1: 3, 2: 4, 3: 5},
        compiler_params=pltpu.CompilerParams(has_side_effects=pltpu.SideEffectType.DATAFLOW_SIDE_EFFECTING),
    )(*bufs)


def _exchange_wait(send_sems, recv_sems, in_thru, out_thru, lin_thru, lout_thru, after):
    def body(in_ref, out_ref, lin_ref, lout_ref, send_sems, recv_sems, after_ref, in_dead, out_dead, got_in, got_out):
        for cp in _exchange_copies(in_ref, out_ref, lin_ref, lout_ref, send_sems, recv_sems):
            cp.wait_send()
            cp.wait_recv()

    hbm = pl.BlockSpec(memory_space=pltpu.HBM)
    sem = pl.BlockSpec(memory_space=pltpu.SEMAPHORE)
    bufs = (in_thru, out_thru, lin_thru, lout_thru)
    return pl.pallas_call(
        body, name="exchange_wait",
        in_specs=[hbm] * 4 + [sem, sem, pl.BlockSpec(memory_space=pl.ANY)],
        out_specs=[hbm] * 4,
        out_shape=[pltpu.HBM(b.shape, b.dtype) for b in bufs],
        input_output_aliases={0: 0, 1: 1, 2: 2, 3: 3},
        compiler_params=pltpu.CompilerParams(has_side_effects=pltpu.SideEffectType.DATAFLOW_SIDE_EFFECTING),
    )(*bufs, send_sems, recv_sems, after)


PEER_XOR = (2, 1, 3)


SUM_STEPS = 2


def _sum_chips(css, rs, chip_idx):
    def body(m_ref, *refs):
        k = len(css)
        mine = m_ref[0]
        for cs_ref, r_ref, o_ref in zip(refs[:k], refs[k: 2 * k], refs[2 * k:]):
            own = cs_ref[0].astype(F32)
            got = [r_ref[j].astype(F32) for j in range(3)]
            acc = None
            for s in range(4):
                rel = mine ^ s
                term = jnp.where(rel == 0, own, jnp.where(rel == PEER_XOR[0], got[0],
                                                          jnp.where(rel == PEER_XOR[1], got[1], got[2])))
                acc = term if acc is None else acc + term
            o_ref[...] = acc

    tile = lambda r: (r.shape[1] // SUM_STEPS, r.shape[2])
    return pl.pallas_call(
        body, name="sum_chips",
        grid_spec=pltpu.PrefetchScalarGridSpec(
            num_scalar_prefetch=1, grid=(SUM_STEPS,),
            in_specs=[pl.BlockSpec((1,) + tile(r), lambda i, m_ref: (m_ref[0], i, 0)) for r in rs]
            + [pl.BlockSpec((3,) + tile(r), lambda i, m_ref: (0, i, 0)) for r in rs],
            out_specs=[pl.BlockSpec(tile(r), lambda i, m_ref: (i, 0)) for r in rs]),
        out_shape=[jax.ShapeDtypeStruct(r.shape[1:], F32) for r in rs],
        compiler_params=_params(("parallel",)),
    )(chip_idx, *css, *rs)


def _swap_reduced(h_in, h_out):
    def body(in_ref, out_ref, rin_ref, rout_ref, send_sems, recv_sems):
        x, y, c = _place()
        cps = [pltpu.make_async_remote_copy(src_ref=src, dst_ref=dst, send_sem=send_sems.at[k],
                                            recv_sem=recv_sems.at[k], device_id=(x, y, 1 - c), device_id_type=MESH)
               for k, (src, dst) in enumerate(((in_ref, rin_ref), (out_ref, rout_ref)))]
        for cp in cps:
            cp.start()
        for cp in cps:
            cp.wait()

    hbm = pl.BlockSpec(memory_space=pl.ANY)
    return pl.pallas_call(
        body, name="swap_reduced",
        in_specs=[hbm, hbm], out_specs=[hbm, hbm],
        out_shape=[jax.ShapeDtypeStruct(h.shape, F32) for h in (h_in, h_out)],
        scratch_shapes=[pltpu.SemaphoreType.DMA((2,)), pltpu.SemaphoreType.DMA((2,))],
    )(h_in, h_out)


ADAM_STEPS = 4


def _adamw_halves(ws, mines, theirs, ms, vs, cidx):
    k = len(ws)
    nbh = ADAM_STEPS // 2

    def body(c_ref, *refs):
        w, a, b, m, v, g, d, nm, nv = (refs[k * i: k * (i + 1)] for i in range(9))
        own = pl.program_id(0) // nbh == c_ref[0]
        for t in range(k):
            gg = jnp.where(own, a[t][...], b[t][...])
            g[t][...] = gg
            _adamw_update(gg, w[t], m[t], v[t], d[t], nm[t], nv[t])

    tile = lambda w: (w.shape[0] // ADAM_STEPS, w.shape[1])
    specs = [pl.BlockSpec(tile(w), lambda i, c_ref: (i, 0)) for w in ws]
    hspecs = [pl.BlockSpec(tile(w), lambda i, c_ref: (i % nbh, 0)) for w in ws]
    out = pl.pallas_call(
        body, name="adamw_halves",
        grid_spec=pltpu.PrefetchScalarGridSpec(
            num_scalar_prefetch=1, grid=(ADAM_STEPS,),
            in_specs=specs + hspecs * 2 + specs * 2, out_specs=specs * 4),
        out_shape=[jax.ShapeDtypeStruct(w.shape, F32) for w in ws] * 4,
        compiler_params=_params(("parallel",)),
    )(cidx, *ws, *mines, *theirs, *ms, *vs)
    return out[0:k], out[k: 2 * k], out[2 * k: 3 * k], out[3 * k:]


def _allreduce_small(g_nw, g_fw, g_hgw, g_lbl, loss8):
    def body(nw_ref, fw_ref, hgw_ref, lbl_ref, loss_ref, out_ref, slots, send_sems, recv_sems):
        x, y, c = _place()
        me = 4 * x + 2 * y + c
        slots[me] = jnp.zeros((8, D_MODEL), F32)
        slots[me, 0:1, :] = nw_ref[...]
        slots[me, 1:2, :] = fw_ref[...]
        slots[me, 2:3, 0:HG_W] = hgw_ref[...]
        slots[me, 3:4, 0:HG_W] = lbl_ref[0:1, :]
        slots[me, 3:4, HG_W:] = lbl_ref[1:2, :]
        slots[me, 4:5, 0:128] = loss_ref[0:1, :]
        cps = []
        for k in range(1, 8):
            dx, dy, dc = (k >> 2) & 1, (k >> 1) & 1, k & 1
            to = (x ^ dx, y ^ dy, c ^ dc)
            cps.append(pltpu.make_async_remote_copy(
                src_ref=slots.at[me], dst_ref=slots.at[me], send_sem=send_sems.at[k - 1],
                recv_sem=recv_sems.at[k - 1], device_id=to, device_id_type=MESH))
        for cp in cps:
            cp.start()
        for cp in cps:
            cp.wait()
        acc = slots[0]
        for i in range(1, 8):
            acc = acc + slots[i]
        out_ref[...] = acc

    vm = pl.BlockSpec(memory_space=pltpu.VMEM)
    return pl.pallas_call(
        body, name="allreduce_small",
        in_specs=[vm] * 5, out_specs=vm,
        out_shape=jax.ShapeDtypeStruct((8, D_MODEL), F32),
        scratch_shapes=[pltpu.VMEM((8, 8, D_MODEL), F32), pltpu.SemaphoreType.DMA((7,)), pltpu.SemaphoreType.DMA((7,))],
    )(g_nw, g_fw, g_hgw, g_lbl, loss8)


def _rope_tables(S):
    inv_freq = (np.float32(1.0) / np.power(np.float32(ROPE_THETA), np.arange(ROPE_HALF, dtype=np.float32) / np.float32(ROPE_HALF))).astype(np.float32)
    ang = (np.arange(S, dtype=np.float32)[:, None] * inv_freq[None, :]).astype(np.float32)
    cos, sin = np.cos(ang).astype(np.float32), np.sin(ang).astype(np.float32)
    cos2 = np.concatenate([cos, cos, cos, cos], axis=-1)
    sin2 = np.concatenate([-sin, sin, -sin, sin], axis=-1)
    return jnp.asarray(cos2), jnp.asarray(sin2)


def _local_step(x2, tgt2, norm_w, w_in_full, lbl, hg_norm_w, w_out4, fnw):
    S = x2.shape[0]
    cos2, sin2 = _rope_tables(S)
    w_out_full = w_out4.reshape(D_MODEL, D_MODEL)
    p0, p1, qkv, qkv4, qkv16, z_at, u = _inproj(x2, norm_w, w_in_full, cos2, sin2)
    o_hg, states = _hg_fwd(p0, p1, lbl)
    qkv_ds = [qkv.reshape(1, S, 3 * AT_W), qkv4, qkv16]
    os_d, ls_d = zip(*[_att_fwd(q) for q in qkv_ds])
    o_at, lse, mix_at = _att_combine(os_d, ls_d, z_at)
    dh, do_hg, dz_hg, dm_at, g_wout, g_fw, g_hgw, loss8 = _outproj(x2, tgt2, mix_at, w_out_full, fnw, o_hg, p1, hg_norm_w)
    dqr, dfl, dv_hg, g_lbl = _hg_bwd(p0, p1, do_hg, states, lbl)
    do1, do4, do16, dl1, dl4, dl16, dz_at = _att_gate_bwd(dm_at, o_at, lse, z_at)
    do_ds = [do1.reshape(1, S, AT_W), do4, do16]
    dl_ds = [dl1.reshape(1, S, STAT_W), dl4, dl16]
    dqkvs = [_att_bwd(q, a, b) for q, a, b in zip(qkv_ds, do_ds, dl_ds)]
    dq_at, dk_at, dv_at = _att_bwd_combine(dqkvs, cos2, sin2)
    dps = [dqr, dfl, dv_hg, dz_hg, dq_at, dk_at, dv_at, dz_at]
    return loss8, dps, u, dh, g_lbl, g_hgw, g_wout, g_fw


def kernel(x, norm_w, w_in, hgrn_lb_logits, hg_norm_w, w_out, final_norm_w, loss_target, m_norm_w, m_w_in, m_hgrn_lb_logits, m_hg_norm_w, m_w_out, m_final_norm_w, v_norm_w, v_w_in, v_hgrn_lb_logits, v_hg_norm_w, v_w_out, v_final_norm_w):
    S = x.shape[1]
    w_in_full, w_out_full = _gather_weights(w_in[0], w_out[0])
    loss8, dps, u, dh, g_lbl, g_hgw, g_wout, g_fw = _local_step(
        x[0], loss_target[0], norm_w, w_in_full, hgrn_lb_logits, hg_norm_w,
        w_out_full, final_norm_w.reshape(1, D_MODEL))

    cidx = lax.axis_index("c").astype(jnp.int32).reshape(1)
    g_win = _inproj_bwd_w(u, dps)
    g_in4 = g_win.reshape(4, 2, 512, 1024)
    g_out4 = g_wout.reshape(4, 2, 128, D_MODEL)
    r_in, r_out = _swap_halves(g_in4, g_out4)
    cs_in, cs_out = _add_halves((g_in4, g_out4), (r_in, r_out), cidx)
    *started, token = _exchange_start(cs_in, cs_out)
    grad_x, g_nw = _inproj_bwd_x(dps, w_in_full, x[0], norm_w + token[0:1, 0:1], dh)
    cs_in, cs_out, x_in, x_out = _exchange_wait(*started, g_nw)
    chip_idx = (2 * lax.axis_index("x") + lax.axis_index("y")).astype(jnp.int32).reshape(1)
    h_in, h_out = _sum_chips((cs_in, cs_out), (x_in, x_out), chip_idx)
    s_in, s_out = _swap_reduced(h_in, h_out)

    red = _allreduce_small(g_nw, g_fw, g_hgw, g_lbl, loss8)
    as_row = lambda a: a.reshape(1, D_MODEL)
    grads, deltas, new_m, new_v, loss = _adamw_small(
        red, (norm_w, hgrn_lb_logits, hg_norm_w, as_row(final_norm_w)),
        (m_norm_w, m_hgrn_lb_logits, m_hg_norm_w, as_row(m_final_norm_w)),
        (v_norm_w, v_hgrn_lb_logits, v_hg_norm_w, as_row(v_final_norm_w)))
    grad_norm_w, grad_lbl, grad_hg_norm_w, grad_final_norm_w = grads
    d_nw, d_lbl, d_hgw, d_fw = deltas
    m_nw, m_lbl, m_hgw, m_fw = new_m
    v_nw, v_lbl, v_hgw, v_fw = new_v
    (grad_w_in, grad_w_out), (d_win, d_wout), (m_win, m_wout), (v_win, v_wout) = _adamw_halves(
        (w_in[0], w_out[0]), (h_in, h_out), (s_in, s_out), (m_w_in[0], m_w_out[0]), (v_w_in[0], v_w_out[0]), cidx)
    e1 = lambda a: a[None]
    flat = lambda a: a.reshape(D_MODEL)
    return (loss.reshape(()), grad_x[None], grad_norm_w, e1(grad_w_in), grad_lbl, grad_hg_norm_w, e1(grad_w_out),
            flat(grad_final_norm_w),
            d_nw, e1(d_win), d_lbl, d_hgw, e1(d_wout), flat(d_fw),
            m_nw, e1(m_win), m_lbl, m_hgw, e1(m_wout), flat(m_fw),
            v_nw, e1(v_win), v_lbl, v_hgw, e1(v_wout), flat(v_fw))
```

```python
import jax
import jax.numpy as jnp
import numpy as np
from jax import lax
from jax.experimental import pallas as pl
from jax.experimental.pallas import tpu as pltpu

F32 = jnp.float32
BF16 = jnp.bfloat16
MESH = pl.DeviceIdType.MESH

D_MODEL = 1024
HG_HEADS = 4
HG_HEAD = 128
HG_W = HG_HEADS * HG_HEAD
AT_HEAD = 64
AT_W = 512
HEAD_PAIR = 2 * AT_HEAD
ROPE_HALF = 32
ROPE_THETA = 10000.0
EPS = 1e-6
CHUNK = 128
LEVELS = (64, 32, 16, 8, 4, 2, 1)
DIAG = 1
SUBLANES = 8
ATT_BLK = 128
ATT_QB = 4
ATT_SCALE = 0.125
STAT_W = 128
STAT_LANES = 16
STAT_LSE_LANE = 8
NEG = -1e30
VMEM_LIMIT = 56 * 1024 * 1024
MIX_TM = 1024
PROJ_TM = 512

ADAM_LR = 0.001
ADAM_B1 = 0.9
ADAM_B2 = 0.999
ADAM_EPS = 1e-08
ADAM_WD = 0.01
ADAM_STEP = 10


def _iota(shape, dim):
    return lax.broadcasted_iota(jnp.int32, shape, dim)


def _dot(a, b):
    return jnp.dot(a, b, preferred_element_type=F32)


def _dot_nt(a, b):
    return lax.dot_general(a, b, (((1,), (1,)), ((), ())), preferred_element_type=F32)


def _dot_tn(a, b):
    return lax.dot_general(a, b, (((0,), (0,)), ((), ())), preferred_element_type=F32)


def _sigmoid(v):
    return 0.5 * jnp.tanh(0.5 * v) + 0.5


def _params(sem=None, vmem=VMEM_LIMIT):
    return pltpu.CompilerParams(dimension_semantics=sem, vmem_limit_bytes=vmem)


def _full(shape):
    n = len(shape)
    return pl.BlockSpec(shape, lambda *_: (0,) * n)


def _rope_rot(y):
    n = y.shape[1]
    first = (_iota(y.shape, 1) & (2 * ROPE_HALF - 1)) < ROPE_HALF
    return jnp.where(first, pltpu.roll(y, n - ROPE_HALF, 1), pltpu.roll(y, ROPE_HALF, 1))


def _dil_spec(d, tm, width):
    return pl.BlockSpec((d, tm // d, width), lambda s: (0, s, 0))


LANES = 128


def _slab_scratch(tm, width):
    return pltpu.VMEM((width // LANES, tm, LANES), F32)


def _to_slabs(v, slabs_ref):
    for j in range(slabs_ref.shape[0]):
        slabs_ref[j] = v[:, LANES * j: LANES * (j + 1)]


def _from_slabs(slabs_ref):
    return jnp.concatenate([slabs_ref[j] for j in range(slabs_ref.shape[0])], axis=1)


def _split_residues(slabs_ref, dst_ref, d, dtype):
    nslab, tm, _ = slabs_ref.shape
    for r in range(d):
        for j in range(nslab):
            dst_ref[r, :, LANES * j: LANES * (j + 1)] = slabs_ref[j, pl.ds(r, tm // d, stride=d), :].astype(dtype)


def _merge_residues(src_ref, slabs_ref, d):
    nslab, tm, _ = slabs_ref.shape
    for r in range(d):
        for j in range(nslab):
            slabs_ref[j, pl.ds(r, tm // d, stride=d), :] = src_ref[r, :, LANES * j: LANES * (j + 1)].astype(F32)


PERM_ROWS = 256


def _perm_matrix(d, inverse=False):
    n = PERM_ROWS // d
    t = np.arange(PERM_ROWS)
    p = np.zeros((PERM_ROWS, PERM_ROWS), np.float32)
    p[t, (t % d) * n + t // d] = 1.0
    return jnp.asarray(p.T if inverse else p, BF16)


def _merge_bf16(src_ref, perm_ref, d):
    n = PERM_ROWS // d
    tm = src_ref.shape[1] * d
    outs = []
    for b in range(tm // PERM_ROWS):
        flat = jnp.concatenate([src_ref[r, n * b: n * (b + 1), :] for r in range(d)], axis=0)
        outs.append(_dot(perm_ref[...], flat))
    return jnp.concatenate(outs, axis=0)


def _split_bf16(v, perm_t_ref, dst_ref, d):
    n = PERM_ROWS // d
    for b in range(v.shape[0] // PERM_ROWS):
        flat = _dot(perm_t_ref[...], v[PERM_ROWS * b: PERM_ROWS * (b + 1), :]).astype(BF16)
        for r in range(d):
            dst_ref[r, n * b: n * (b + 1), :] = flat[n * r: n * (r + 1), :]


def _inproj(x2, norm_w, w_in_full, cos2, sin2):
    S = x2.shape[0]
    tm = PROJ_TM

    def body(x_ref, nw_ref, w_ref, cos_ref, sin_ref, t4_ref, t16_ref, p0_ref, p1_ref, qkv_ref, qkv4_ref, qkv16_ref, z_ref,
             u_ref):
        x = x_ref[...]
        r = lax.rsqrt(jnp.mean(x * x, axis=-1, keepdims=True) + EPS)
        u = x * r * nw_ref[...]
        ub = u.astype(BF16)
        u_ref[...] = ub
        p0_ref[...] = _dot(ub, w_ref[0])
        p1_ref[...] = _dot(ub, w_ref[1])
        y2 = _dot(ub, w_ref[2])
        cosf = jnp.tile(cos_ref[...], (1, 8))
        sinf = jnp.tile(sin_ref[...], (1, 8))
        y3 = _dot(ub, w_ref[3])
        z_ref[...] = y3[:, AT_W:]
        qkv = jnp.concatenate([y2 * cosf + _rope_rot(y2) * sinf, y3[:, :AT_W]], axis=1).astype(BF16)
        qkv_ref[...] = qkv
        _split_bf16(qkv, t4_ref, qkv4_ref, 4)
        _split_bf16(qkv, t16_ref, qkv16_ref, 16)

    row = lambda w: pl.BlockSpec((tm, w), lambda s: (s, 0))
    qkv_w = 3 * AT_W
    perm = _full((PERM_ROWS, PERM_ROWS))
    return pl.pallas_call(
        body, name="inproj", grid=(S // tm,),
        in_specs=[row(D_MODEL), _full((1, D_MODEL)), _full((4, D_MODEL, 1024)), row(128), row(128), perm, perm],
        out_specs=[row(1024), row(1024), row(qkv_w), _dil_spec(4, tm, qkv_w), _dil_spec(16, tm, qkv_w), row(AT_W),
                   row(D_MODEL)],
        out_shape=[jax.ShapeDtypeStruct((S, 1024), F32), jax.ShapeDtypeStruct((S, 1024), F32),
                   jax.ShapeDtypeStruct((S, qkv_w), BF16), jax.ShapeDtypeStruct((4, S // 4, qkv_w), BF16),
                   jax.ShapeDtypeStruct((16, S // 16, qkv_w), BF16), jax.ShapeDtypeStruct((S, AT_W), F32),
                   jax.ShapeDtypeStruct((S, D_MODEL), BF16)],
        compiler_params=_params(("parallel",)),
    )(x2, norm_w, w_in_full, cos2, sin2, _perm_matrix(4, True), _perm_matrix(16, True))


HG_HPS = 4
HG_CPS = 4
N_LEV = len(LEVELS)


def _hg_const_arrays():
    r = np.arange(CHUNK)[:, None]
    c = np.arange(CHUNK)[None, :]
    tris = np.stack([r >= c, r <= c])
    lm = [((r // (2 * m)) == (c // (2 * m))) & (r % (2 * m) >= m) & (c % (2 * m) < m) for m in LEVELS]
    dm = [(c == r - dl) & (r % DIAG >= dl) for dl in range(DIAG)]
    masks = np.stack(lm + dm)
    return jnp.asarray(tris, BF16), jnp.asarray(masks, F32)


def _split2(a):
    hi = a.astype(BF16)
    return hi, (a - hi.astype(F32)).astype(BF16)


def _dot3(a, b, dot=_dot):
    ah, al = _split2(a)
    bh, bl = _split2(b)
    n = b.shape[1]
    p = dot(ah, jnp.concatenate([bh, bl], axis=1))
    return (p[:, :n] + p[:, n:]) + dot(al, bh)


def _split3(a):
    a1 = a.astype(BF16)
    r1 = a - a1.astype(F32)
    a2 = r1.astype(BF16)
    return a1, a2, (r1 - a2.astype(F32)).astype(BF16)


def _tri_dot(tri, a):
    n = a.shape[1]
    p = _dot(tri, jnp.concatenate(_split3(a), axis=1))
    return (p[:, :n] + p[:, n:2 * n]) + p[:, 2 * n:]


def _dot_sel(a, sel):
    a1, a2, a3 = _split3(a)
    return (_dot(a1, sel) + _dot(a2, sel)) + _dot(a3, sel)


def _rowsum(t):
    return _dot(t.astype(BF16), jnp.ones((t.shape[1], t.shape[1]), BF16))


def _level_refs(b):
    refs = []
    pos = _iota(b.shape, 0)
    for m in LEVELS:
        if 2 * m >= SUBLANES:
            parts = [jnp.broadcast_to(b[r0 + m - 1: r0 + m, :], (2 * m, b.shape[1])) for r0 in range(0, CHUNK, 2 * m)]
            refs.append(parts[0] if len(parts) == 1 else jnp.concatenate(parts, axis=0))
        else:
            p = pos & (2 * m - 1)
            ref = b
            for off in range(-(m - 1), m + 1):
                if off != 0:
                    ref = jnp.where(p == m - 1 + off, pltpu.roll(b, off % CHUNK, 0), ref)
            refs.append(ref)
    return refs


def _hg_lb(lbl_ref):
    l0 = lbl_ref[0:1, :]
    l1 = lbl_ref[1:2, :]
    mx = jnp.maximum(l0, l1)
    e0 = jnp.exp(l0 - mx)
    e1 = jnp.exp(l1 - mx)
    p0 = e0 / (e0 + e1)
    lb = jnp.clip(p0, 1e-6, 1.0 - 1e-6)
    inside = (p0 >= 1e-6) & (p0 <= 1.0 - 1e-6)
    dlb_dl0 = jnp.where(inside, p0 * (e1 / (e0 + e1)), 0.0)
    return lb, dlb_dl0


def _hg_gates(qr, fl, lb):
    sig = _sigmoid(fl)
    f = lb + (1.0 - lb) * sig
    g = jnp.log(f)
    k = (1.0 - lb) * (1.0 - sig)
    sq = _sigmoid(qr)
    q = qr * sq
    return sig, f, g, k, sq, q


def _neg_abs(v):
    bits = lax.bitcast_convert_type(v, jnp.uint32) | jnp.uint32(0x80000000)
    return lax.bitcast_convert_type(bits, F32)


def _hg_levels(q, k, b, mk_ref):
    refs = _level_refs(b)
    a = jnp.zeros((CHUNK, CHUNK), F32)
    es, qts, kts = [], [], []
    for i in range(N_LEV):
        e = jnp.exp(_neg_abs(b - refs[i]))
        qt = (q * e).astype(BF16)
        kt = (k * e).astype(BF16)
        a = a + _dot_nt(qt, kt) * mk_ref[i]
        es.append(e); qts.append(qt); kts.append(kt)
    return a, es, qts, kts


def _hg_specs(nc, rev):
    cc = (lambda c: nc - 1 - c) if rev else (lambda c: c)
    w = HG_HEAD * HG_HPS
    blk = lambda off: pl.BlockSpec((HG_CPS * CHUNK, w), lambda h, c: (cc(c), h + off))
    lb2 = pl.BlockSpec((2, w), lambda h, c: (0, h))
    st = pl.BlockSpec((HG_CPS, HG_HPS, HG_HEAD, HG_HEAD), lambda h, c: (cc(c), h, 0, 0))
    consts = [_full((2, CHUNK, CHUNK)), _full((N_LEV + DIAG, CHUNK, CHUNK))]
    return blk, lb2, st, consts


def _hg_fwd(p0, p1, lbl):
    S = p0.shape[0]
    nc = S // (HG_CPS * CHUNK)
    ng = HG_HEADS // HG_HPS

    def body(qr_ref, fl_ref, v_ref, lbl_ref, tri_ref, mk_ref, o_ref, st_ref, state):
        c = pl.program_id(1)

        @pl.when(c == 0)
        def _():
            state[...] = jnp.zeros_like(state)

        lb_all, _ = _hg_lb(lbl_ref)
        heads = [slice(HG_HEAD * hh, HG_HEAD * (hh + 1)) for hh in range(HG_HPS)]
        for j in range(HG_CPS):
            rows = slice(CHUNK * j, CHUNK * (j + 1))
            qs, ks, bs, mats = [], [], [], []
            for sl in heads:
                _, _, g, k, _, q = _hg_gates(qr_ref[rows, sl], fl_ref[rows, sl], lb_all[:, sl])
                qs.append(q); ks.append(k); bs.append(_tri_dot(tri_ref[0], g))
            for hh in range(HG_HPS):
                a, _, _, _ = _hg_levels(qs[hh], ks[hh], bs[hh], mk_ref)
                mats.append(a + _rowsum(qs[hh] * ks[hh]) * mk_ref[N_LEV])
            for hh, sl in enumerate(heads):
                q, k, b, a, v = qs[hh], ks[hh], bs[hh], mats[hh], v_ref[rows, sl]
                b_last = b[CHUNK - 1: CHUNK, :]
                st = state[hh]
                st_ref[j, hh] = st
                o = _dot_nt((q * jnp.exp(b)).astype(BF16), st.astype(BF16)) + _dot(a.astype(BF16), v.astype(BF16))
                state[hh] = st * jnp.exp(b_last) + _dot3(v, k * jnp.exp(b_last - b), _dot_tn)
                o_ref[rows, sl] = o

    blk, lb2, st_spec, consts = _hg_specs(nc, False)
    tris, masks = _hg_const_arrays()
    return pl.pallas_call(
        body, name="hg_fwd", grid=(ng, nc),
        in_specs=[blk(0), blk(ng), blk(0), lb2] + consts,
        out_specs=[blk(0), st_spec],
        out_shape=[jax.ShapeDtypeStruct((S, HG_W), F32), jax.ShapeDtypeStruct((S // CHUNK, HG_HEADS, HG_HEAD, HG_HEAD), F32)],
        scratch_shapes=[pltpu.VMEM((HG_HPS, HG_HEAD, HG_HEAD), F32)],
        compiler_params=_params(("parallel", "arbitrary")),
    )(p0, p0, p1, lbl, tris, masks)


def _hg_bwd(p0, p1, do_hg, states, lbl):
    S = p0.shape[0]
    nc = S // (HG_CPS * CHUNK)
    ng = HG_HEADS // HG_HPS
    w = HG_HEAD * HG_HPS

    def body(qr_ref, fl_ref, v_ref, do_ref, st_ref, lbl_ref, tri_ref, mk_ref, mkb_ref,
             dqr_ref, dfl_ref, dv_ref, glbl_ref, dstate, carry, acc_lb):
        c = pl.program_id(1)

        @pl.when(c == 0)
        def _():
            dstate[...] = jnp.zeros_like(dstate)
            carry[...] = jnp.zeros_like(carry)
            acc_lb[...] = jnp.zeros_like(acc_lb)

        lb_all, dlb_dl0 = _hg_lb(lbl_ref)
        heads = [slice(HG_HEAD * hh, HG_HEAD * (hh + 1)) for hh in range(HG_HPS)]
        diag_mask = mk_ref[N_LEV]
        for j in reversed(range(HG_CPS)):
            rows = slice(CHUNK * j, CHUNK * (j + 1))
            gates, bs, dos = [], [], []
            for sl in heads:
                gt = _hg_gates(qr_ref[rows, sl], fl_ref[rows, sl], lb_all[:, sl])
                gates.append(gt)
                bs.append(_tri_dot(tri_ref[0], gt[2]))
            for sl in heads:
                dos.append(do_ref[rows, sl])
            inter = []
            for hh, sl in enumerate(heads):
                _, _, _, k, _, q = gates[hh]
                b, do, v = bs[hh], dos[hh], v_ref[rows, sl]
                b_last = b[CHUNK - 1: CHUNK, :]
                eb = jnp.exp(b)
                edec = jnp.exp(b_last - b)
                dst = dstate[hh]
                dq = _dot3(do, st_ref[j, hh]) * eb
                dk = _dot3(v, dst) * edec
                da = _dot_nt(do.astype(BF16), v.astype(BF16))
                dv_state = _dot_nt((k * edec).astype(BF16), dst.astype(BF16))
                dstate[hh] = dst * jnp.exp(b_last) + _dot3(do, q * eb, _dot_tn)
                inter.append((dq, dk, da, dv_state))
            for hh, sl in enumerate(heads):
                sig, f, _, k, sq, q = gates[hh]
                dq, dk, da, dv_state = inter[hh]
                b = bs[hh]
                db = q * dq - k * dk
                dab = da.astype(BF16)
                refs = _level_refs(b)
                a = _rowsum(q * k) * diag_mask
                for i in range(N_LEV):
                    e = jnp.exp(_neg_abs(b - refs[i]))
                    qt = (q * e).astype(BF16)
                    kt = (k * e).astype(BF16)
                    a = a + _dot_nt(qt, kt) * mk_ref[i]
                    dam = dab * mkb_ref[i]
                    gq = _dot(dam, kt)
                    gk = _dot_tn(dam, qt)
                    dq = dq + e * gq
                    dk = dk + e * gk
                    db = db + (qt.astype(F32) * gq - kt.astype(F32) * gk)
                dc = _rowsum(da * diag_mask)
                dq = dq + dc * k
                dk = dk + dc * q
                dv_ref[rows, sl] = (_dot_tn(a.astype(BF16), dos[hh].astype(BF16)) + dv_state).astype(BF16)
                dg = _tri_dot(tri_ref[1], db) + carry[0:1, sl]
                carry[0:1, sl] += jnp.sum(db, axis=0, keepdims=True)
                lb = lb_all[:, sl]
                qr = qr_ref[rows, sl]
                t = dg / f - dk
                dfl_ref[rows, sl] = (t * (1.0 - lb) * sig * (1.0 - sig)).astype(BF16)
                acc_lb[0:1, sl] += jnp.sum(t * (1.0 - sig), axis=0, keepdims=True)
                dqr_ref[rows, sl] = (dq * (sq * (1.0 + qr * (1.0 - sq)))).astype(BF16)

        @pl.when(c == nc - 1)
        def _():
            gl0 = acc_lb[0:1, :] * dlb_dl0
            glbl_ref[0:1, :] = gl0
            glbl_ref[1:2, :] = -gl0

    blk, lb2, st_spec, consts = _hg_specs(nc, True)
    tris, masks = _hg_const_arrays()
    act = jax.ShapeDtypeStruct((S, HG_W), BF16)
    return pl.pallas_call(
        body, name="hg_bwd", grid=(ng, nc),
        in_specs=[blk(0), blk(ng), blk(0), blk(0), st_spec, lb2] + consts + consts[1:],
        out_specs=[blk(0), blk(0), blk(0), lb2],
        out_shape=[act, act, act, jax.ShapeDtypeStruct((2, HG_W), F32)],
        scratch_shapes=[pltpu.VMEM((HG_HPS, HG_HEAD, HG_HEAD), F32), pltpu.VMEM((8, w), F32), pltpu.VMEM((8, w), F32)],
        compiler_params=_params(("parallel", "arbitrary")),
    )(p0, p0, p1, do_hg, states, lbl, tris, masks, masks.astype(BF16))


def _stat_head(lane):
    return lane >> 4


def _stat_lane(lane):
    return lane & (STAT_LANES - 1)


def _o_head(lane):
    return lane >> 6


def _att_bias():
    qi = np.arange(ATT_BLK)[:, None]
    kj = np.arange(2 * ATT_BLK)[None, :]
    band = (kj >= qi) & (kj <= qi + ATT_BLK)
    qm = np.stack([band & (kj >= ATT_BLK), band])
    cur = (kj < ATT_BLK) & (qi <= kj)
    km = np.stack([cur, cur | ((kj >= ATT_BLK) & (qi >= kj - ATT_BLK))])
    to_bias = lambda m: jnp.asarray(np.where(m, 0.0, NEG), F32)
    return to_bias(qm), to_bias(km)


def _att_fwd(qkv_d):
    d, L, _ = qkv_d.shape
    qb = ATT_QB
    rows = qb * ATT_BLK

    def body(q_ref, kp_ref, kc_ref, vp_ref, vc_ref, bias_ref, o_ref, lse_ref, s_scr, p_scr):
        first = _iota((ATT_BLK, HEAD_PAIR), 1) < AT_HEAD
        head_of_lane = _stat_head(_iota((ATT_BLK, STAT_W), 1))
        pairs = [slice(HEAD_PAIR * hp, HEAD_PAIR * (hp + 1)) for hp in range(4)]
        blk = lambda i: slice(ATT_BLK * i, ATT_BLK * (i + 1))

        def keys(i, sl, prev_ref, cur_ref):
            before = prev_ref[0, :, sl] if i == 0 else cur_ref[0, blk(i - 1), sl]
            return jnp.concatenate([before, cur_ref[0, blk(i), sl]], axis=0)

        for i in range(qb):
            for hp, sl in enumerate(pairs):
                q2 = q_ref[0, blk(i), sl] * ATT_SCALE
                zero = jnp.zeros_like(q2)
                qs = jnp.concatenate([jnp.where(first, q2, zero), jnp.where(first, zero, q2)], axis=0)
                s_scr[4 * i + hp] = _dot_nt(qs, keys(i, sl, kp_ref, kc_ref))
        stats = []
        for i in range(qb):
            bias = bias_ref[jnp.minimum(pl.program_id(1), 1)] if i == 0 else bias_ref[1]
            bias2 = jnp.concatenate([bias, bias], axis=0)
            for hp in range(4):
                s = s_scr[4 * i + hp] + bias2
                m = jnp.max(s, axis=-1, keepdims=True)
                p = jnp.exp(s - m)
                l = jnp.sum(p, axis=-1, keepdims=True)
                p_scr[4 * i + hp] = p.astype(BF16)
                stats.append((l, m + jnp.log(l)))
        for i in range(qb):
            lse_blk = jnp.zeros((ATT_BLK, STAT_W), F32)
            for hp, sl in enumerate(pairs):
                l, lse = stats[4 * i + hp]
                o = _dot(p_scr[4 * i + hp], keys(i, sl, vp_ref, vc_ref)) / l
                o_ref[0, blk(i), sl] = jnp.where(first, o[:ATT_BLK], o[ATT_BLK:]).astype(BF16)
                lse_blk = jnp.where(head_of_lane == 2 * hp, lse[:ATT_BLK],
                                    jnp.where(head_of_lane == 2 * hp + 1, lse[ATT_BLK:], lse_blk))
            lse_ref[0, blk(i), :] = lse_blk

    cur = lambda j: pl.BlockSpec((1, rows, AT_W), lambda r, n: (r, n, j))
    prev = lambda j: pl.BlockSpec((1, ATT_BLK, AT_W), lambda r, n: (r, jnp.maximum(qb * n - 1, 0), j))
    return pl.pallas_call(
        body, name=f"att_fwd_d{d}", grid=(d, L // rows),
        in_specs=[cur(0), prev(1), cur(1), prev(2), cur(2), _full((2, ATT_BLK, 2 * ATT_BLK))],
        out_specs=[pl.BlockSpec((1, rows, AT_W), lambda r, n: (r, n, 0)), pl.BlockSpec((1, rows, STAT_W), lambda r, n: (r, n, 0))],
        out_shape=[jax.ShapeDtypeStruct((d, L, AT_W), BF16), jax.ShapeDtypeStruct((d, L, STAT_W), F32)],
        scratch_shapes=[pltpu.VMEM((4 * qb, 2 * ATT_BLK, 2 * ATT_BLK), F32), pltpu.VMEM((4 * qb, 2 * ATT_BLK, 2 * ATT_BLK), BF16)],
        compiler_params=_params(("parallel", "parallel")),
    )(qkv_d, qkv_d, qkv_d, qkv_d, qkv_d, _att_bias()[0])


def _att_combine(os_d, ls_d, z_at):
    S = z_at.shape[0]
    tm = MIX_TM

    def body(oa_ref, ob4_ref, oc16_ref, la_ref, lb4_ref, lc16_ref, p4_ref, p16_ref, z_ref,
             oat_ref, lse_ref, mix_ref, lb_scr, lc_scr):
        _merge_residues(lb4_ref, lb_scr, 4)
        _merge_residues(lc16_ref, lc_scr, 16)
        ls = (la_ref[...], lb_scr[0], lc_scr[0])
        mx = jnp.maximum(jnp.maximum(ls[0], ls[1]), ls[2])
        es = [jnp.exp(l - mx) for l in ls]
        zs = es[0] + es[1] + es[2]
        lse_ref[...] = mx + jnp.log(zs)
        spread = ((_o_head(_iota((STAT_W, AT_W), 1)) == _stat_head(_iota((STAT_W, AT_W), 0)))
                  & (_stat_lane(_iota((STAT_W, AT_W), 0)) == 0)).astype(BF16)
        os_ = (oa_ref[...].astype(F32), _merge_bf16(ob4_ref, p4_ref, 4), _merge_bf16(oc16_ref, p16_ref, 16))
        o = jnp.zeros((tm, AT_W), F32)
        for e, oi in zip(es, os_):
            hi, lo = _split2(e / zs)
            o = o + (_dot(hi, spread) + _dot(lo, spread)) * oi
        oat_ref[...] = o
        z = z_ref[...]
        mixed = o * (z * _sigmoid(z))
        mix_ref[...] = mixed.astype(BF16)

    row = lambda w: pl.BlockSpec((tm, w), lambda s: (s, 0))
    return pl.pallas_call(
        body, name="att_combine", grid=(S // tm,),
        in_specs=[row(AT_W), _dil_spec(4, tm, AT_W), _dil_spec(16, tm, AT_W),
                  row(STAT_W), _dil_spec(4, tm, STAT_W), _dil_spec(16, tm, STAT_W),
                  _full((PERM_ROWS, PERM_ROWS)), _full((PERM_ROWS, PERM_ROWS)), row(AT_W)],
        out_specs=[row(AT_W), row(STAT_W), row(AT_W)],
        out_shape=[jax.ShapeDtypeStruct((S, AT_W), F32), jax.ShapeDtypeStruct((S, STAT_W), F32),
                   jax.ShapeDtypeStruct((S, AT_W), BF16)],
        scratch_shapes=[_slab_scratch(tm, STAT_W), _slab_scratch(tm, STAT_W)],
        compiler_params=_params(("parallel",)),
    )(os_d[0].reshape(S, AT_W), os_d[1], os_d[2], ls_d[0].reshape(S, STAT_W), ls_d[1], ls_d[2],
      _perm_matrix(4), _perm_matrix(16), z_at)


def _att_gate_bwd(dm, o_ref, l_ref, z_ref, t4_ref, t16_ref, do_ref, do4_ref, do16_ref, dl_ref, dl4_ref, dl16_ref, dz_ref,
                  dl_scr):
    tm = dm.shape[0]
    o = o_ref[...]
    z = z_ref[...]
    sz = _sigmoid(z)
    dz_ref[...] = (dm * o * (sz * (1.0 + z * (1.0 - sz)))).astype(BF16)
    do = dm * (z * sz)
    dob = do.astype(BF16)
    do_ref[...] = dob
    gather = (_o_head(_iota((AT_W, STAT_W), 0)) == _stat_head(_iota((AT_W, STAT_W), 1))).astype(BF16)
    dl = jnp.where(_stat_lane(_iota((tm, STAT_W), 1)) < STAT_LSE_LANE, _dot_sel(do * o, gather), l_ref[...])
    dl_ref[...] = dl
    _split_bf16(dob, t4_ref, do4_ref, 4)
    _split_bf16(dob, t16_ref, do16_ref, 16)
    _to_slabs(dl, dl_scr)
    _split_residues(dl_scr, dl4_ref, 4, F32)
    _split_residues(dl_scr, dl16_ref, 16, F32)


def _att_gate_bwd_specs(S, tm):
    row = lambda w: pl.BlockSpec((tm, w), lambda s: (s, 0))
    sds = jax.ShapeDtypeStruct
    ins = [row(AT_W), row(STAT_W), row(AT_W)] + [_full((PERM_ROWS, PERM_ROWS))] * 2
    outs = [row(AT_W), _dil_spec(4, tm, AT_W), _dil_spec(16, tm, AT_W),
            row(STAT_W), _dil_spec(4, tm, STAT_W), _dil_spec(16, tm, STAT_W), row(AT_W)]
    shapes = [sds((S, AT_W), BF16), sds((4, S // 4, AT_W), BF16), sds((16, S // 16, AT_W), BF16),
              sds((S, STAT_W), F32), sds((4, S // 4, STAT_W), F32), sds((16, S // 16, STAT_W), F32),
              sds((S, AT_W), BF16)]
    return ins, outs, shapes, _slab_scratch(tm, STAT_W)


def _att_bwd(qkv_d, do_d, dl_d):
    d, L, _ = qkv_d.shape
    nb = L // ATT_BLK
    qb = ATT_QB
    rows = qb * ATT_BLK
    nsteps = L // rows

    def body(qc_ref, qn_ref, kp_ref, kc_ref, vp_ref, vc_ref, ac_ref, an_ref, lc_ref, ln_ref, bq_ref, bk_ref, dqkv_ref,
             s_scr, dp_scr, st_scr, dpt_scr, ds_scr, pt_scr, dst_scr):
        n = pl.program_id(1)
        first = _iota((ATT_BLK, HEAD_PAIR), 1) < AT_HEAD
        pairs = [slice(HEAD_PAIR * hp, HEAD_PAIR * (hp + 1)) for hp in range(4)]
        blk = lambda i: slice(ATT_BLK * i, ATT_BLK * (i + 1))

        def stack(t):
            zero = jnp.zeros_like(t)
            return jnp.concatenate([jnp.where(first, t, zero), jnp.where(first, zero, t)], axis=0)

        def unstack(t2):
            return jnp.where(first, t2[:ATT_BLK], t2[ATT_BLK:])

        def with_prev(i, sl, prev_ref, cur_ref):
            before = prev_ref[0, :, sl] if i == 0 else cur_ref[0, blk(i - 1), sl]
            return jnp.concatenate([before, cur_ref[0, blk(i), sl]], axis=0)

        def with_next(i, sl, cur_ref, next_ref):
            after = next_ref[0, :, sl] if i == qb - 1 else cur_ref[0, blk(i + 1), sl]
            return jnp.concatenate([cur_ref[0, blk(i), sl], after], axis=0)

        for i in range(qb):
            for hp, sl in enumerate(pairs):
                j = 4 * i + hp
                s_scr[j] = _dot_nt(stack(qc_ref[0, blk(i), sl] * ATT_SCALE), with_prev(i, sl, kp_ref, kc_ref))
                dp_scr[j] = _dot_nt(stack(ac_ref[0, blk(i), sl]), with_prev(i, sl, vp_ref, vc_ref))
                st_scr[j] = _dot_nt(stack(kc_ref[0, blk(i), sl] * ATT_SCALE), with_next(i, sl, qc_ref, qn_ref))
                dpt_scr[j] = _dot_nt(stack(vc_ref[0, blk(i), sl]), with_next(i, sl, ac_ref, an_ref))
        for i in range(qb):
            bias = bq_ref[jnp.minimum(n, 1)] if i == 0 else bq_ref[1]
            bias_t = bk_ref[jnp.minimum(nsteps - 1 - n, 1)] if i == qb - 1 else bk_ref[1]
            bias2 = jnp.concatenate([bias, bias], axis=0)
            bias_t2 = jnp.concatenate([bias_t, bias_t], axis=0)
            dl_c = lc_ref[0, blk(i), :]
            dl_t = with_next(i, slice(None), lc_ref, ln_ref).T
            for hp in range(4):
                j = 4 * i + hp
                at = [STAT_LANES * (2 * hp), STAT_LANES * (2 * hp + 1)]
                col = lambda t, o: jnp.concatenate([t[:, a + o: a + o + 1] for a in at], axis=0)
                p = jnp.exp(s_scr[j] + bias2 - col(dl_c, STAT_LSE_LANE))
                ds_scr[j] = (p * (dp_scr[j] - col(dl_c, 0))).astype(BF16)
                row = lambda t, o: jnp.concatenate([jnp.broadcast_to(t[a + o: a + o + 1, :], (ATT_BLK, 2 * ATT_BLK)) for a in at], axis=0)
                pt = jnp.exp(st_scr[j] + bias_t2 - row(dl_t, STAT_LSE_LANE))
                pt_scr[j] = pt.astype(BF16)
                dst_scr[j] = (pt * (dpt_scr[j] - row(dl_t, 0))).astype(BF16)
        for i in range(qb):
            for hp, sl in enumerate(pairs):
                j = 4 * i + hp
                dq = unstack(_dot(ds_scr[j], with_prev(i, sl, kp_ref, kc_ref))) * ATT_SCALE
                dk = unstack(_dot(dst_scr[j], with_next(i, sl, qc_ref, qn_ref))) * ATT_SCALE
                dv = unstack(_dot(pt_scr[j], with_next(i, sl, ac_ref, an_ref)))
                dqkv_ref[0, blk(i), sl] = dq.astype(BF16)
                dqkv_ref[0, blk(i), AT_W + HEAD_PAIR * hp: AT_W + HEAD_PAIR * (hp + 1)] = dk.astype(BF16)
                dqkv_ref[0, blk(i), 2 * AT_W + HEAD_PAIR * hp: 2 * AT_W + HEAD_PAIR * (hp + 1)] = dv.astype(BF16)

    cur = lambda j: pl.BlockSpec((1, rows, AT_W), lambda r, n: (r, n, j))
    prev = lambda j: pl.BlockSpec((1, ATT_BLK, AT_W), lambda r, n: (r, jnp.maximum(qb * n - 1, 0), j))
    nxt_blk = lambda n: jnp.minimum(qb * (n + 1), nb - 1)
    sq = (4 * qb, 2 * ATT_BLK, 2 * ATT_BLK)
    return pl.pallas_call(
        body, name=f"att_bwd_d{d}", grid=(d, nsteps),
        in_specs=[cur(0), pl.BlockSpec((1, ATT_BLK, AT_W), lambda r, n: (r, nxt_blk(n), 0)), prev(1), cur(1), prev(2), cur(2),
                  pl.BlockSpec((1, rows, AT_W), lambda r, n: (r, n, 0)),
                  pl.BlockSpec((1, ATT_BLK, AT_W), lambda r, n: (r, nxt_blk(n), 0)),
                  pl.BlockSpec((1, rows, STAT_W), lambda r, n: (r, n, 0)),
                  pl.BlockSpec((1, ATT_BLK, STAT_W), lambda r, n: (r, nxt_blk(n), 0)),
                  _full((2, ATT_BLK, 2 * ATT_BLK)), _full((2, ATT_BLK, 2 * ATT_BLK))],
        out_specs=pl.BlockSpec((1, rows, 3 * AT_W), lambda r, n: (r, n, 0)),
        out_shape=jax.ShapeDtypeStruct((d, L, 3 * AT_W), BF16),
        scratch_shapes=[pltpu.VMEM(sq, F32)] * 4 + [pltpu.VMEM(sq, BF16)] * 3,
        compiler_params=_params(("parallel", "parallel")),
    )(qkv_d, qkv_d, qkv_d, qkv_d, qkv_d, qkv_d, do_d, do_d, dl_d, dl_d, *_att_bias())


def _att_bwd_combine(dqkvs, cos2, sin2):
    S = dqkvs[0].shape[1]
    tm = MIX_TM

    def body(a_ref, b4_ref, c16_ref, p4_ref, p16_ref, cos_ref, sin_ref, dq_ref, dk_ref, dv_ref):
        t = a_ref[...].astype(F32) + _merge_bf16(b4_ref, p4_ref, 4) + _merge_bf16(c16_ref, p16_ref, 16)
        dy = t[:, : 2 * AT_W]
        cosf = jnp.tile(cos_ref[...], (1, 8))
        sinf = jnp.tile(sin_ref[...], (1, 8))
        dx = dy * cosf - _rope_rot(dy) * sinf
        dq_ref[...] = dx[:, :AT_W].astype(BF16)
        dk_ref[...] = dx[:, AT_W:].astype(BF16)
        dv_ref[...] = t[:, 2 * AT_W:].astype(BF16)

    row = lambda w: pl.BlockSpec((tm, w), lambda s: (s, 0))
    act = jax.ShapeDtypeStruct((S, AT_W), BF16)
    return pl.pallas_call(
        body, name="att_bwd_combine", grid=(S // tm,),
        in_specs=[row(3 * AT_W), _dil_spec(4, tm, 3 * AT_W), _dil_spec(16, tm, 3 * AT_W),
                  _full((PERM_ROWS, PERM_ROWS)), _full((PERM_ROWS, PERM_ROWS)), row(128), row(128)],
        out_specs=[row(AT_W), row(AT_W), row(AT_W)],
        out_shape=[act, act, act],
        compiler_params=_params(("parallel",)),
    )(dqkvs[0].reshape(S, 3 * AT_W), dqkvs[1], dqkvs[2], _perm_matrix(4), _perm_matrix(16), cos2, sin2)


def _outproj(x2, tgt2, mix_at, w_out_full, fnw, o_hg, p1, hg_norm_w, o_at, lse, z_at):
    S = x2.shape[0]
    tm = PROJ_TM
    ns = S // tm

    def body(x_ref, t_ref, ma_ref, w_ref, fw_ref, o_ref, z_ref, hgw_ref, oat_ref, lse_ref, zat_ref, t4_ref, t16_ref,
             dh_ref, doh_ref, dzh_ref, gw_ref, gfw_ref, ghgw_ref, loss_ref, *gate_refs):
        s = pl.program_id(0)

        @pl.when(s == 0)
        def _():
            gw_ref[...] = jnp.zeros_like(gw_ref)
            gfw_ref[...] = jnp.zeros_like(gfw_ref)
            ghgw_ref[...] = jnp.zeros_like(ghgw_ref)
            loss_ref[...] = jnp.zeros_like(loss_ref)

        heads = [slice(HG_HEAD * hh, HG_HEAD * (hh + 1)) for hh in range(HG_HEADS)]
        norm = []
        for sl in heads:
            o, z = o_ref[:, sl], z_ref[:, sl]
            rs = lax.rsqrt(jnp.mean(o * o, axis=-1, keepdims=True) + EPS)
            norm.append((rs, o * rs, _sigmoid(z)))
        mh = jnp.concatenate([(oh * hgw_ref[:, sl] * (z_ref[:, sl] * sz)).astype(BF16)
                              for sl, (_, oh, sz) in zip(heads, norm)], axis=1)
        y = _dot(mh, w_ref[:HG_W, :]) + _dot(ma_ref[...], w_ref[HG_W:, :])
        h = x_ref[...] + y
        r = lax.rsqrt(jnp.mean(h * h, axis=-1, keepdims=True) + EPS)
        hn = h * r
        fw = fw_ref[...]
        err = hn * fw - t_ref[...]
        loss_ref[...] += 0.5 * jnp.sum(jnp.mean(err * err, axis=-1, keepdims=True))
        dout = err * (1.0 / D_MODEL)
        gfw_ref[...] += jnp.sum(dout * hn, axis=0, keepdims=True)
        dhn = dout * fw
        dh = r * (dhn - hn * jnp.mean(dhn * hn, axis=-1, keepdims=True))
        dh_ref[...] = dh
        dhb = dh.astype(BF16)
        _att_gate_bwd(_dot_nt(dhb, w_ref[HG_W:, :]), oat_ref, lse_ref, zat_ref, t4_ref, t16_ref, *gate_refs)
        dmh = _dot_nt(dhb, w_ref[:HG_W, :])
        for sl, (rs, oh, sz) in zip(heads, norm):
            z, dmix, gwv = z_ref[:, sl], dmh[:, sl], hgw_ref[:, sl]
            dzh_ref[:, sl] = (dmix * (oh * gwv) * (sz * (1.0 + z * (1.0 - sz)))).astype(BF16)
            don = dmix * (z * sz)
            ghgw_ref[:, sl] += jnp.sum(don * oh, axis=0, keepdims=True)
            dy = don * gwv
            doh_ref[:, sl] = rs * (dy - oh * jnp.mean(dy * oh, axis=-1, keepdims=True))
        gw_ref[:HG_W, :] += _dot_tn(mh, dhb)
        gw_ref[HG_W:, :] += _dot_tn(ma_ref[...], dhb)

    row = lambda w: pl.BlockSpec((tm, w), lambda s: (s, 0))
    gate_in, gate_out, gate_shapes, gate_scratch = _att_gate_bwd_specs(S, tm)
    return pl.pallas_call(
        body, name="outproj", grid=(ns,),
        in_specs=[row(D_MODEL), row(D_MODEL), row(AT_W),
                  _full((D_MODEL, D_MODEL)), _full((1, D_MODEL)),
                  row(HG_W), pl.BlockSpec((tm, HG_W), lambda s: (s, 1)), _full((1, HG_W))] + gate_in,
        out_specs=[row(D_MODEL), row(HG_W), row(HG_W), _full((D_MODEL, D_MODEL)), _full((1, D_MODEL)),
                   _full((1, HG_W)), _full((8, 128))] + gate_out,
        out_shape=[jax.ShapeDtypeStruct((S, D_MODEL), F32), jax.ShapeDtypeStruct((S, HG_W), F32),
                   jax.ShapeDtypeStruct((S, HG_W), BF16),
                   jax.ShapeDtypeStruct((D_MODEL, D_MODEL), F32), jax.ShapeDtypeStruct((1, D_MODEL), F32),
                   jax.ShapeDtypeStruct((1, HG_W), F32), jax.ShapeDtypeStruct((8, 128), F32)] + gate_shapes,
        scratch_shapes=[gate_scratch],
        compiler_params=_params(("arbitrary",)),
    )(x2, tgt2, mix_at, w_out_full, fnw, o_hg, p1, hg_norm_w, o_at, lse, z_at, _perm_matrix(4, True), _perm_matrix(16, True))


def _inproj_bwd_x(dps, w_in_full, x2, norm_w, dh):
    S = x2.shape[0]
    tm = PROJ_TM

    def body(d0, d1, d2, d3, d4, d5, d6, d7, w_ref, x_ref, nw_ref, dh_ref, gx_ref, gnw_ref):
        s = pl.program_id(0)

        @pl.when(s == 0)
        def _():
            gnw_ref[...] = jnp.zeros_like(gnw_ref)

        du = jnp.zeros((tm, D_MODEL), F32)
        for i, dref in enumerate((d0, d1, d2, d3, d4, d5, d6, d7)):
            j, half = divmod(i, 2)
            du = du + _dot_nt(dref[...], w_ref[j, :, 512 * half: 512 * (half + 1)])
        x = x_ref[...]
        r = lax.rsqrt(jnp.mean(x * x, axis=-1, keepdims=True) + EPS)
        xh = x * r
        gnw_ref[...] += jnp.sum(du * xh, axis=0, keepdims=True)
        dun = du * nw_ref[...]
        gx_ref[...] = dh_ref[...] + r * (dun - xh * jnp.mean(dun * xh, axis=-1, keepdims=True))

    row = lambda w: pl.BlockSpec((tm, w), lambda s: (s, 0))
    return pl.pallas_call(
        body, name="inproj_bwd_x", grid=(S // tm,),
        in_specs=[row(512)] * 8 + [_full((4, D_MODEL, 1024)), row(D_MODEL), _full((1, D_MODEL)), row(D_MODEL)],
        out_specs=[row(D_MODEL), _full((1, D_MODEL))],
        out_shape=[jax.ShapeDtypeStruct((S, D_MODEL), F32), jax.ShapeDtypeStruct((1, D_MODEL), F32)],
        compiler_params=_params(("arbitrary",)),
    )(*dps, w_in_full, x2, norm_w, dh)


def _inproj_bwd_w(u, dps):
    S = u.shape[0]
    tm = 2 * PROJ_TM

    def body(u_ref, d0, d1, d2, d3, d4, d5, d6, d7, g_ref):
        @pl.when(pl.program_id(0) == 0)
        def _():
            g_ref[...] = jnp.zeros_like(g_ref)

        ub = u_ref[...]
        for i, dref in enumerate((d0, d1, d2, d3, d4, d5, d6, d7)):
            j, half = divmod(i, 2)
            g_ref[j, :, 512 * half: 512 * (half + 1)] += _dot_tn(ub, dref[...])

    return pl.pallas_call(
        body, name="inproj_bwd_w", grid=(S // tm,),
        in_specs=[pl.BlockSpec((tm, D_MODEL), lambda s: (s, 0))] + [pl.BlockSpec((tm, 512), lambda s: (s, 0))] * 8,
        out_specs=_full((4, D_MODEL, 1024)),
        out_shape=jax.ShapeDtypeStruct((4, D_MODEL, 1024), F32),
        compiler_params=_params(("arbitrary",)),
    )(u, *dps)


def _adamw_update(gg, w_ref, m_ref, v_ref, d_ref, nm_ref, nv_ref):
    nm = ADAM_B1 * m_ref[...] + (1.0 - ADAM_B1) * gg
    nv = ADAM_B2 * v_ref[...] + (1.0 - ADAM_B2) * (gg * gg)
    m_hat = nm / (1.0 - ADAM_B1 ** ADAM_STEP)
    v_hat = nv / (1.0 - ADAM_B2 ** ADAM_STEP)
    d_ref[...] = -ADAM_LR * (m_hat / (jnp.sqrt(v_hat) + ADAM_EPS) + ADAM_WD * w_ref[...])
    nm_ref[...] = nm
    nv_ref[...] = nv


def _adamw_small(red, ws, ms, vs):
    def body(red_ref, *refs):
        w, m, v, g, d, nm, nv = (refs[4 * i: 4 * (i + 1)] for i in range(7))
        loss_ref = refs[28]
        g[0][...] = red_ref[0:1, :]
        g[1][0:1, :] = red_ref[3:4, :HG_W]
        g[1][1:2, :] = red_ref[3:4, HG_W:]
        g[2][...] = red_ref[2:3, :HG_W]
        g[3][...] = red_ref[1:2, :]
        loss_ref[...] = red_ref[4:5, 0:1]
        for i in range(4):
            _adamw_update(g[i][...], w[i], m[i], v[i], d[i], nm[i], nv[i])

    shapes = [a.shape for a in ws]
    sds = [jax.ShapeDtypeStruct(s, F32) for s in shapes]
    out = pl.pallas_call(
        body, name="adamw_small",
        in_specs=[_full(red.shape)] + [_full(s) for s in shapes] * 3,
        out_specs=[_full(s) for s in shapes] * 4 + [_full((1, 1))],
        out_shape=sds * 4 + [jax.ShapeDtypeStruct((1, 1), F32)],
        compiler_params=_params(),
    )(red, *ws, *ms, *vs)
    return out[0:4], out[4:8], out[8:12], out[12:16], out[16]


def _place():
    return lax.axis_index("x"), lax.axis_index("y"), lax.axis_index("c")


def _gather_weights(w_in_s, w_out_s):
    def body(win_ref, wout_ref, fin_ref, fout_ref, send_sems, recv_sems):
        x, y, c = _place()
        me = (x, y, c)
        sib = (x, y, 1 - c)
        mine = 2 * x + y
        fin_ref[mine] = win_ref[...].astype(BF16)
        fout_ref[mine] = wout_ref[...].astype(BF16)
        chips = [(1 - x, y), (x, 1 - y), (1 - x, 1 - y)]

        def halves(chip, half):
            return (fin_ref.at[chip, pl.ds(half * 512, 512), :], fout_ref.at[chip, pl.ds(half * 128, 128), :])

        def copy(k, ref, to):
            return pltpu.make_async_remote_copy(src_ref=ref, dst_ref=ref, send_sem=send_sems.at[k],
                                                recv_sem=recv_sems.at[k], device_id=to, device_id_type=MESH)

        first, passed = [], []
        for j, (cx, cy) in enumerate(chips):
            for a, ref in enumerate(halves(mine, c)):
                first.append(copy(2 * j + a, ref, (cx, cy, c)))
        for cp in first:
            cp.start()
        for j, (cx, cy) in enumerate(chips):
            for a, ref in enumerate(halves(2 * cx + cy, c)):
                copy(2 * j + a, ref, me).wait_recv()
                fwd = copy(6 + 2 * j + a, ref, sib)
                fwd.start()
                passed.append(fwd)
        for j, (cx, cy) in enumerate(chips):
            for a, ref in enumerate(halves(2 * cx + cy, 1 - c)):
                copy(6 + 2 * j + a, ref, me).wait_recv()
        for cp in first + passed:
            cp.wait_send()

    vm = pl.BlockSpec(memory_space=pltpu.VMEM)
    return pl.pallas_call(
        body, name="gather_weights",
        in_specs=[vm, vm], out_specs=[vm, vm],
        out_shape=[jax.ShapeDtypeStruct((4, D_MODEL, 1024), BF16), jax.ShapeDtypeStruct((4, 256, D_MODEL), BF16)],
        scratch_shapes=[pltpu.SemaphoreType.DMA((12,)), pltpu.SemaphoreType.DMA((12,))],
        compiler_params=pltpu.CompilerParams(vmem_limit_bytes=VMEM_LIMIT),
    )(w_in_s, w_out_s)


def _swap_halves(g_in, g_out):
    def body(gin_ref, gout_ref, rin_ref, rout_ref, send_sems, recv_sems):
        x, y, c = _place()
        sib = (x, y, 1 - c)
        cps = [pltpu.make_async_remote_copy(src_ref=src.at[:, 1 - c], dst_ref=dst, send_sem=send_sems.at[k],
                                            recv_sem=recv_sems.at[k], device_id=sib, device_id_type=MESH)
               for k, (src, dst) in enumerate(((gin_ref, rin_ref), (gout_ref, rout_ref)))]
        for cp in cps:
            cp.start()
        for cp in cps:
            cp.wait()

    hbm = pl.BlockSpec(memory_space=pl.ANY)
    return pl.pallas_call(
        body, name="swap_halves",
        in_specs=[hbm, hbm], out_specs=[hbm, hbm],
        out_shape=[jax.ShapeDtypeStruct((4,) + g.shape[2:], F32) for g in (g_in, g_out)],
        scratch_shapes=[pltpu.SemaphoreType.DMA((2,)), pltpu.SemaphoreType.DMA((2,))],
    )(g_in, g_out)


def _add_halves(gs, rs, cidx):
    n = gs[0].shape[0]

    def body(c_ref, *refs):
        k = len(gs)
        for g_ref, r_ref, o_ref in zip(refs[:k], refs[k: 2 * k], refs[2 * k:]):
            o_ref[0] = (g_ref[0, 0] + r_ref[0]).astype(BF16)

    mine = lambda g: pl.BlockSpec((1, 1) + g.shape[2:], lambda j, c_ref: (j, c_ref[0], 0, 0))
    chunk = lambda g: pl.BlockSpec((1,) + g.shape[2:], lambda j, c_ref: (j, 0, 0))
    return pl.pallas_call(
        body, name="add_halves",
        grid_spec=pltpu.PrefetchScalarGridSpec(
            num_scalar_prefetch=1, grid=(n,),
            in_specs=[mine(g) for g in gs] + [chunk(g) for g in gs],
            out_specs=[chunk(g) for g in gs]),
        out_shape=[jax.ShapeDtypeStruct((n,) + g.shape[2:], BF16) for g in gs],
        compiler_params=_params(("parallel",)),
    )(cidx, *gs, *rs)


def _exchange_copies(in_ref, out_ref, lin_ref, lout_ref, send_sems, recv_sems):
    x, y, c = _place()
    cps = []
    for k, (cx, cy) in enumerate([(1 - x, y), (x, 1 - y), (1 - x, 1 - y)]):
        for a, (src, dst) in enumerate(((in_ref, lin_ref), (out_ref, lout_ref))):
            cps.append(pltpu.make_async_remote_copy(
                src_ref=src.at[2 * cx + cy], dst_ref=dst.at[k], send_sem=send_sems.at[2 * k + a],
                recv_sem=recv_sems.at[2 * k + a], device_id=(cx, cy, c), device_id_type=MESH))
    return cps


def _exchange_start(cs_in, cs_out):
    def body(in_ref, out_ref, lin_ref, lout_ref, send_sems, recv_sems, in_thru, out_thru, lin_thru, lout_thru, token):
        for cp in _exchange_copies(in_ref, out_ref, lin_ref, lout_ref, send_sems, recv_sems):
            cp.start()
        token[...] = jnp.zeros_like(token)

    lands = [lax.empty((3,) + a.shape[1:], a.dtype) for a in (cs_in, cs_out)]
    bufs = [pltpu.with_memory_space_constraint(a, pltpu.HBM) for a in (cs_in, cs_out, *lands)]
    hbm = pl.BlockSpec(memory_space=pltpu.HBM)
    sem = pl.BlockSpec(memory_space=pltpu.SEMAPHORE)
    return pl.pallas_call(
        body, name="exchange_start",
        in_specs=[hbm] * 4,
        out_specs=[sem, sem, hbm, hbm, hbm, hbm, pl.BlockSpec(memory_space=pltpu.VMEM)],
        out_shape=[pltpu.SemaphoreType.DMA((6,)), pltpu.SemaphoreType.DMA((6,))]
                  + [pltpu.HBM(b.shape, b.dtype) for b in bufs] + [jax.ShapeDtypeStruct((8, 128), F32)],
        input_output_aliases={0: 2, 1: 3, 2: 4, 3: 5},
        compiler_params=pltpu.CompilerParams(has_side_effects=pltpu.SideEffectType.DATAFLOW_SIDE_EFFECTING),
    )(*bufs)


def _exchange_wait(send_sems, recv_sems, in_thru, out_thru, lin_thru, lout_thru, after):
    def body(in_ref, out_ref, lin_ref, lout_ref, send_sems, recv_sems, after_ref, in_dead, out_dead, got_in, got_out):
        for cp in _exchange_copies(in_ref, out_ref, lin_ref, lout_ref, send_sems, recv_sems):
            cp.wait_send()
            cp.wait_recv()

    hbm = pl.BlockSpec(memory_space=pltpu.HBM)
    sem = pl.BlockSpec(memory_space=pltpu.SEMAPHORE)
    bufs = (in_thru, out_thru, lin_thru, lout_thru)
    return pl.pallas_call(
        body, name="exchange_wait",
        in_specs=[hbm] * 4 + [sem, sem, pl.BlockSpec(memory_space=pl.ANY)],
        out_specs=[hbm] * 4,
        out_shape=[pltpu.HBM(b.shape, b.dtype) for b in bufs],
        input_output_aliases={0: 0, 1: 1, 2: 2, 3: 3},
        compiler_params=pltpu.CompilerParams(has_side_effects=pltpu.SideEffectType.DATAFLOW_SIDE_EFFECTING),
    )(*bufs, send_sems, recv_sems, after)


PEER_XOR = (2, 1, 3)


SUM_STEPS = 2


def _sum_chips(css, rs, chip_idx):
    def body(m_ref, *refs):
        k = len(css)
        mine = m_ref[0]
        for cs_ref, r_ref, o_ref in zip(refs[:k], refs[k: 2 * k], refs[2 * k:]):
            own = cs_ref[0].astype(F32)
            got = [r_ref[j].astype(F32) for j in range(3)]
            acc = None
            for s in range(4):
                rel = mine ^ s
                term = jnp.where(rel == 0, own, jnp.where(rel == PEER_XOR[0], got[0],
                                                          jnp.where(rel == PEER_XOR[1], got[1], got[2])))
                acc = term if acc is None else acc + term
            o_ref[...] = acc

    tile = lambda r: (r.shape[1] // SUM_STEPS, r.shape[2])
    return pl.pallas_call(
        body, name="sum_chips",
        grid_spec=pltpu.PrefetchScalarGridSpec(
            num_scalar_prefetch=1, grid=(SUM_STEPS,),
            in_specs=[pl.BlockSpec((1,) + tile(r), lambda i, m_ref: (m_ref[0], i, 0)) for r in rs]
            + [pl.BlockSpec((3,) + tile(r), lambda i, m_ref: (0, i, 0)) for r in rs],
            out_specs=[pl.BlockSpec(tile(r), lambda i, m_ref: (i, 0)) for r in rs]),
        out_shape=[jax.ShapeDtypeStruct(r.shape[1:], F32) for r in rs],
        compiler_params=_params(("parallel",)),
    )(chip_idx, *css, *rs)


def _swap_reduced(h_in, h_out):
    def body(in_ref, out_ref, rin_ref, rout_ref, send_sems, recv_sems):
        x, y, c = _place()
        cps = [pltpu.make_async_remote_copy(src_ref=src, dst_ref=dst, send_sem=send_sems.at[k],
                                            recv_sem=recv_sems.at[k], device_id=(x, y, 1 - c), device_id_type=MESH)
               for k, (src, dst) in enumerate(((in_ref, rin_ref), (out_ref, rout_ref)))]
        for cp in cps:
            cp.start()
        for cp in cps:
            cp.wait()

    hbm = pl.BlockSpec(memory_space=pl.ANY)
    return pl.pallas_call(
        body, name="swap_reduced",
        in_specs=[hbm, hbm], out_specs=[hbm, hbm],
        out_shape=[jax.ShapeDtypeStruct(h.shape, F32) for h in (h_in, h_out)],
        scratch_shapes=[pltpu.SemaphoreType.DMA((2,)), pltpu.SemaphoreType.DMA((2,))],
    )(h_in, h_out)


ADAM_STEPS = 4


def _adamw_halves(ws, mines, theirs, ms, vs, cidx):
    k = len(ws)
    nbh = ADAM_STEPS // 2

    def body(c_ref, *refs):
        w, a, b, m, v, g, d, nm, nv = (refs[k * i: k * (i + 1)] for i in range(9))
        own = pl.program_id(0) // nbh == c_ref[0]
        for t in range(k):
            gg = jnp.where(own, a[t][...], b[t][...])
            g[t][...] = gg
            _adamw_update(gg, w[t], m[t], v[t], d[t], nm[t], nv[t])

    tile = lambda w: (w.shape[0] // ADAM_STEPS, w.shape[1])
    specs = [pl.BlockSpec(tile(w), lambda i, c_ref: (i, 0)) for w in ws]
    hspecs = [pl.BlockSpec(tile(w), lambda i, c_ref: (i % nbh, 0)) for w in ws]
    out = pl.pallas_call(
        body, name="adamw_halves",
        grid_spec=pltpu.PrefetchScalarGridSpec(
            num_scalar_prefetch=1, grid=(ADAM_STEPS,),
            in_specs=specs + hspecs * 2 + specs * 2, out_specs=specs * 4),
        out_shape=[jax.ShapeDtypeStruct(w.shape, F32) for w in ws] * 4,
        compiler_params=_params(("parallel",)),
    )(cidx, *ws, *mines, *theirs, *ms, *vs)
    return out[0:k], out[k: 2 * k], out[2 * k: 3 * k], out[3 * k:]


def _allreduce_small(g_nw, g_fw, g_hgw, g_lbl, loss8):
    def body(nw_ref, fw_ref, hgw_ref, lbl_ref, loss_ref, out_ref, slots, send_sems, recv_sems):
        x, y, c = _place()
        me = 4 * x + 2 * y + c
        slots[me] = jnp.zeros((8, D_MODEL), F32)
        slots[me, 0:1, :] = nw_ref[...]
        slots[me, 1:2, :] = fw_ref[...]
        slots[me, 2:3, 0:HG_W] = hgw_ref[...]
        slots[me, 3:4, 0:HG_W] = lbl_ref[0:1, :]
        slots[me, 3:4, HG_W:] = lbl_ref[1:2, :]
        slots[me, 4:5, 0:128] = loss_ref[0:1, :]
        cps = []
        for k in range(1, 8):
            dx, dy, dc = (k >> 2) & 1, (k >> 1) & 1, k & 1
            to = (x ^ dx, y ^ dy, c ^ dc)
            cps.append(pltpu.make_async_remote_copy(
                src_ref=slots.at[me], dst_ref=slots.at[me], send_sem=send_sems.at[k - 1],
                recv_sem=recv_sems.at[k - 1], device_id=to, device_id_type=MESH))
        for cp in cps:
            cp.start()
        for cp in cps:
            cp.wait()
        acc = slots[0]
        for i in range(1, 8):
            acc = acc + slots[i]
        out_ref[...] = acc

    vm = pl.BlockSpec(memory_space=pltpu.VMEM)
    return pl.pallas_call(
        body, name="allreduce_small",
        in_specs=[vm] * 5, out_specs=vm,
        out_shape=jax.ShapeDtypeStruct((8, D_MODEL), F32),
        scratch_shapes=[pltpu.VMEM((8, 8, D_MODEL), F32), pltpu.SemaphoreType.DMA((7,)), pltpu.SemaphoreType.DMA((7,))],
    )(g_nw, g_fw, g_hgw, g_lbl, loss8)


def _rope_tables(S):
    inv_freq = (np.float32(1.0) / np.power(np.float32(ROPE_THETA), np.arange(ROPE_HALF, dtype=np.float32) / np.float32(ROPE_HALF))).astype(np.float32)
    ang = (np.arange(S, dtype=np.float32)[:, None] * inv_freq[None, :]).astype(np.float32)
    cos, sin = np.cos(ang).astype(np.float32), np.sin(ang).astype(np.float32)
    cos2 = np.concatenate([cos, cos, cos, cos], axis=-1)
    sin2 = np.concatenate([-sin, sin, -sin, sin], axis=-1)
    return jnp.asarray(cos2), jnp.asarray(sin2)


def _local_step(x2, tgt2, norm_w, w_in_full, lbl, hg_norm_w, w_out4, fnw):
    S = x2.shape[0]
    cos2, sin2 = _rope_tables(S)
    w_out_full = w_out4.reshape(D_MODEL, D_MODEL)
    p0, p1, qkv, qkv4, qkv16, z_at, u = _inproj(x2, norm_w, w_in_full, cos2, sin2)
    o_hg, states = _hg_fwd(p0, p1, lbl)
    qkv_ds = [qkv.reshape(1, S, 3 * AT_W), qkv4, qkv16]
    os_d, ls_d = zip(*[_att_fwd(q) for q in qkv_ds])
    o_at, lse, mix_at = _att_combine(os_d, ls_d, z_at)
    dh, do_hg, dz_hg, g_wout, g_fw, g_hgw, loss8, do1, do4, do16, dl1, dl4, dl16, dz_at = _outproj(
        x2, tgt2, mix_at, w_out_full, fnw, o_hg, p1, hg_norm_w, o_at, lse, z_at)
    dqr, dfl, dv_hg, g_lbl = _hg_bwd(p0, p1, do_hg, states, lbl)
    do_ds = [do1.reshape(1, S, AT_W), do4, do16]
    dl_ds = [dl1.reshape(1, S, STAT_W), dl4, dl16]
    dqkvs = [_att_bwd(q, a, b) for q, a, b in zip(qkv_ds, do_ds, dl_ds)]
    dq_at, dk_at, dv_at = _att_bwd_combine(dqkvs, cos2, sin2)
    dps = [dqr, dfl, dv_hg, dz_hg, dq_at, dk_at, dv_at, dz_at]
    return loss8, dps, u, dh, g_lbl, g_hgw, g_wout, g_fw


def kernel(x, norm_w, w_in, hgrn_lb_logits, hg_norm_w, w_out, final_norm_w, loss_target, m_norm_w, m_w_in, m_hgrn_lb_logits, m_hg_norm_w, m_w_out, m_final_norm_w, v_norm_w, v_w_in, v_hgrn_lb_logits, v_hg_norm_w, v_w_out, v_final_norm_w):
    S = x.shape[1]
    w_in_full, w_out_full = _gather_weights(w_in[0], w_out[0])
    loss8, dps, u, dh, g_lbl, g_hgw, g_wout, g_fw = _local_step(
        x[0], loss_target[0], norm_w, w_in_full, hgrn_lb_logits, hg_norm_w,
        w_out_full, final_norm_w.reshape(1, D_MODEL))

    cidx = lax.axis_index("c").astype(jnp.int32).reshape(1)
    g_win = _inproj_bwd_w(u, dps)
    g_in4 = g_win.reshape(4, 2, 512, 1024)
    g_out4 = g_wout.reshape(4, 2, 128, D_MODEL)
    r_in, r_out = _swap_halves(g_in4, g_out4)
    cs_in, cs_out = _add_halves((g_in4, g_out4), (r_in, r_out), cidx)
    *started, token = _exchange_start(cs_in, cs_out)
    grad_x, g_nw = _inproj_bwd_x(dps, w_in_full, x[0], norm_w + token[0:1, 0:1], dh)
    cs_in, cs_out, x_in, x_out = _exchange_wait(*started, g_nw)
    chip_idx = (2 * lax.axis_index("x") + lax.axis_index("y")).astype(jnp.int32).reshape(1)
    h_in, h_out = _sum_chips((cs_in, cs_out), (x_in, x_out), chip_idx)
    s_in, s_out = _swap_reduced(h_in, h_out)

    red = _allreduce_small(g_nw, g_fw, g_hgw, g_lbl, loss8)
    as_row = lambda a: a.reshape(1, D_MODEL)
    grads, deltas, new_m, new_v, loss = _adamw_small(
        red, (norm_w, hgrn_lb_logits, hg_norm_w, as_row(final_norm_w)),
        (m_norm_w, m_hgrn_lb_logits, m_hg_norm_w, as_row(m_final_norm_w)),
        (v_norm_w, v_hgrn_lb_logits, v_hg_norm_w, as_row(v_final_norm_w)))
    grad_norm_w, grad_lbl, grad_hg_norm_w, grad_final_norm_w = grads
    d_nw, d_lbl, d_hgw, d_fw = deltas
    m_nw, m_lbl, m_hgw, m_fw = new_m
    v_nw, v_lbl, v_hgw, v_fw = new_v
    (grad_w_in, grad_w_out), (d_win, d_wout), (m_win, m_wout), (v_win, v_wout) = _adamw_halves(
        (w_in[0], w_out[0]), (h_in, h_out), (s_in, s_out), (m_w_in[0], m_w_out[0]), (v_w_in[0], v_w_out[0]), cidx)
    e1 = lambda a: a[None]
    flat = lambda a: a.reshape(D_MODEL)
    return (loss.reshape(()), grad_x[None], grad_norm_w, e1(grad_w_in), grad_lbl, grad_hg_norm_w, e1(grad_w_out),
            flat(grad_final_norm_w),
            d_nw, e1(d_win), d_lbl, d_hgw, e1(d_wout), flat(d_fw),
            m_nw, e1(m_win), m_lbl, m_hgw, e1(m_wout), flat(m_fw),
            v_nw, e1(v_win), v_lbl, v_hgw, e1(v_wout), flat(v_fw))
```

```python
import jax
import jax.numpy as jnp
import numpy as np
from jax import lax
from jax.experimental import pallas as pl
from jax.experimental.pallas import tpu as pltpu

F32 = jnp.float32
BF16 = jnp.bfloat16
MESH = pl.DeviceIdType.MESH

D_MODEL = 1024
HG_HEADS = 4
HG_HEAD = 128
HG_W = HG_HEADS * HG_HEAD
AT_HEAD = 64
AT_W = 512
HEAD_PAIR = 2 * AT_HEAD
ROPE_HALF = 32
ROPE_THETA = 10000.0
EPS = 1e-6
CHUNK = 128
LEVELS = (64, 32, 16, 8, 4, 2, 1)
DIAG = 1
SUBLANES = 8
ATT_BLK = 128
ATT_QB = 4
ATT_SCALE = 0.125
STAT_W = 128
STAT_LANES = 16
STAT_LSE_LANE = 8
NEG = -1e30
VMEM_LIMIT = 56 * 1024 * 1024
MIX_TM = 1024
PROJ_TM = 512

ADAM_LR = 0.001
ADAM_B1 = 0.9
ADAM_B2 = 0.999
ADAM_EPS = 1e-08
ADAM_WD = 0.01
ADAM_STEP = 10


def _iota(shape, dim):
    return lax.broadcasted_iota(jnp.int32, shape, dim)


def _dot(a, b):
    return jnp.dot(a, b, preferred_element_type=F32)


def _dot_nt(a, b):
    return lax.dot_general(a, b, (((1,), (1,)), ((), ())), preferred_element_type=F32)


def _dot_tn(a, b):
    return lax.dot_general(a, b, (((0,), (0,)), ((), ())), preferred_element_type=F32)


def _sigmoid(v):
    return 0.5 * jnp.tanh(0.5 * v) + 0.5


def _params(sem=None, vmem=VMEM_LIMIT):
    return pltpu.CompilerParams(dimension_semantics=sem, vmem_limit_bytes=vmem)


def _full(shape):
    n = len(shape)
    return pl.BlockSpec(shape, lambda *_: (0,) * n)


def _rope_rot(y):
    n = y.shape[1]
    first = (_iota(y.shape, 1) & (2 * ROPE_HALF - 1)) < ROPE_HALF
    return jnp.where(first, pltpu.roll(y, n - ROPE_HALF, 1), pltpu.roll(y, ROPE_HALF, 1))


def _dil_spec(d, tm, width):
    return pl.BlockSpec((d, tm // d, width), lambda s: (0, s, 0))


LANES = 128


def _slab_scratch(tm, width):
    return pltpu.VMEM((width // LANES, tm, LANES), F32)


def _to_slabs(v, slabs_ref):
    for j in range(slabs_ref.shape[0]):
        slabs_ref[j] = v[:, LANES * j: LANES * (j + 1)]


def _from_slabs(slabs_ref):
    return jnp.concatenate([slabs_ref[j] for j in range(slabs_ref.shape[0])], axis=1)


def _split_residues(slabs_ref, dst_ref, d, dtype):
    nslab, tm, _ = slabs_ref.shape
    for r in range(d):
        for j in range(nslab):
            dst_ref[r, :, LANES * j: LANES * (j + 1)] = slabs_ref[j, pl.ds(r, tm // d, stride=d), :].astype(dtype)


def _merge_residues(src_ref, slabs_ref, d):
    nslab, tm, _ = slabs_ref.shape
    for r in range(d):
        for j in range(nslab):
            slabs_ref[j, pl.ds(r, tm // d, stride=d), :] = src_ref[r, :, LANES * j: LANES * (j + 1)].astype(F32)


PERM_ROWS = 256


def _perm_matrix(d, inverse=False):
    n = PERM_ROWS // d
    t = np.arange(PERM_ROWS)
    p = np.zeros((PERM_ROWS, PERM_ROWS), np.float32)
    p[t, (t % d) * n + t // d] = 1.0
    return jnp.asarray(p.T if inverse else p, BF16)


def _merge_bf16(src_ref, perm_ref, d):
    n = PERM_ROWS // d
    tm = src_ref.shape[1] * d
    outs = []
    for b in range(tm // PERM_ROWS):
        flat = jnp.concatenate([src_ref[r, n * b: n * (b + 1), :] for r in range(d)], axis=0)
        outs.append(_dot(perm_ref[...], flat))
    return jnp.concatenate(outs, axis=0)


def _split_bf16(v, perm_t_ref, dst_ref, d):
    n = PERM_ROWS // d
    for b in range(v.shape[0] // PERM_ROWS):
        flat = _dot(perm_t_ref[...], v[PERM_ROWS * b: PERM_ROWS * (b + 1), :]).astype(BF16)
        for r in range(d):
            dst_ref[r, n * b: n * (b + 1), :] = flat[n * r: n * (r + 1), :]


def _inproj(x2, norm_w, w_in_full, cos2, sin2):
    S = x2.shape[0]
    tm = PROJ_TM

    def body(x_ref, nw_ref, w_ref, cos_ref, sin_ref, t4_ref, t16_ref, p0_ref, p1_ref, qkv_ref, qkv4_ref, qkv16_ref, z_ref,
             u_ref):
        x = x_ref[...]
        r = lax.rsqrt(jnp.mean(x * x, axis=-1, keepdims=True) + EPS)
        u = x * r * nw_ref[...]
        ub = u.astype(BF16)
        u_ref[...] = ub
        p0_ref[...] = _dot(ub, w_ref[0])
        p1_ref[...] = _dot(ub, w_ref[1])
        y2 = _dot(ub, w_ref[2])
        cosf = jnp.tile(cos_ref[...], (1, 8))
        sinf = jnp.tile(sin_ref[...], (1, 8))
        y3 = _dot(ub, w_ref[3])
        z_ref[...] = y3[:, AT_W:]
        qkv = jnp.concatenate([y2 * cosf + _rope_rot(y2) * sinf, y3[:, :AT_W]], axis=1).astype(BF16)
        qkv_ref[...] = qkv
        _split_bf16(qkv, t4_ref, qkv4_ref, 4)
        _split_bf16(qkv, t16_ref, qkv16_ref, 16)

    row = lambda w: pl.BlockSpec((tm, w), lambda s: (s, 0))
    qkv_w = 3 * AT_W
    perm = _full((PERM_ROWS, PERM_ROWS))
    return pl.pallas_call(
        body, name="inproj", grid=(S // tm,),
        in_specs=[row(D_MODEL), _full((1, D_MODEL)), _full((4, D_MODEL, 1024)), row(128), row(128), perm, perm],
        out_specs=[row(1024), row(1024), row(qkv_w), _dil_spec(4, tm, qkv_w), _dil_spec(16, tm, qkv_w), row(AT_W),
                   row(D_MODEL)],
        out_shape=[jax.ShapeDtypeStruct((S, 1024), F32), jax.ShapeDtypeStruct((S, 1024), F32),
                   jax.ShapeDtypeStruct((S, qkv_w), BF16), jax.ShapeDtypeStruct((4, S // 4, qkv_w), BF16),
                   jax.ShapeDtypeStruct((16, S // 16, qkv_w), BF16), jax.ShapeDtypeStruct((S, AT_W), F32),
                   jax.ShapeDtypeStruct((S, D_MODEL), BF16)],
        compiler_params=_params(("parallel",)),
    )(x2, norm_w, w_in_full, cos2, sin2, _perm_matrix(4, True), _perm_matrix(16, True))


HG_HPS = 4
HG_CPS = 4
N_LEV = len(LEVELS)


def _hg_const_arrays():
    r = np.arange(CHUNK)[:, None]
    c = np.arange(CHUNK)[None, :]
    tris = np.stack([r >= c, r <= c])
    lm = [((r // (2 * m)) == (c // (2 * m))) & (r % (2 * m) >= m) & (c % (2 * m) < m) for m in LEVELS]
    dm = [(c == r - dl) & (r % DIAG >= dl) for dl in range(DIAG)]
    masks = np.stack(lm + dm)
    return jnp.asarray(tris, BF16), jnp.asarray(masks, F32)


def _split2(a):
    hi = a.astype(BF16)
    return hi, (a - hi.astype(F32)).astype(BF16)


def _dot3(a, b, dot=_dot):
    ah, al = _split2(a)
    bh, bl = _split2(b)
    n = b.shape[1]
    p = dot(ah, jnp.concatenate([bh, bl], axis=1))
    return (p[:, :n] + p[:, n:]) + dot(al, bh)


def _split3(a):
    a1 = a.astype(BF16)
    r1 = a - a1.astype(F32)
    a2 = r1.astype(BF16)
    return a1, a2, (r1 - a2.astype(F32)).astype(BF16)


def _tri_dot(tri, a):
    n = a.shape[1]
    p = _dot(tri, jnp.concatenate(_split3(a), axis=1))
    return (p[:, :n] + p[:, n:2 * n]) + p[:, 2 * n:]


def _dot_sel(a, sel):
    a1, a2, a3 = _split3(a)
    return (_dot(a1, sel) + _dot(a2, sel)) + _dot(a3, sel)


def _rowsum(t):
    return _dot(t.astype(BF16), jnp.ones((t.shape[1], t.shape[1]), BF16))


def _level_refs(b):
    refs = []
    pos = _iota(b.shape, 0)
    for m in LEVELS:
        if 2 * m >= SUBLANES:
            parts = [jnp.broadcast_to(b[r0 + m - 1: r0 + m, :], (2 * m, b.shape[1])) for r0 in range(0, CHUNK, 2 * m)]
            refs.append(parts[0] if len(parts) == 1 else jnp.concatenate(parts, axis=0))
        else:
            p = pos & (2 * m - 1)
            ref = b
            for off in range(-(m - 1), m + 1):
                if off != 0:
                    ref = jnp.where(p == m - 1 + off, pltpu.roll(b, off % CHUNK, 0), ref)
            refs.append(ref)
    return refs


def _hg_lb(lbl_ref):
    l0 = lbl_ref[0:1, :]
    l1 = lbl_ref[1:2, :]
    mx = jnp.maximum(l0, l1)
    e0 = jnp.exp(l0 - mx)
    e1 = jnp.exp(l1 - mx)
    p0 = e0 / (e0 + e1)
    lb = jnp.clip(p0, 1e-6, 1.0 - 1e-6)
    inside = (p0 >= 1e-6) & (p0 <= 1.0 - 1e-6)
    dlb_dl0 = jnp.where(inside, p0 * (e1 / (e0 + e1)), 0.0)
    return lb, dlb_dl0


def _hg_gates(qr, fl, lb):
    sig = _sigmoid(fl)
    f = lb + (1.0 - lb) * sig
    g = jnp.log(f)
    k = (1.0 - lb) * (1.0 - sig)
    sq = _sigmoid(qr)
    q = qr * sq
    return sig, f, g, k, sq, q


def _neg_abs(v):
    bits = lax.bitcast_convert_type(v, jnp.uint32) | jnp.uint32(0x80000000)
    return lax.bitcast_convert_type(bits, F32)


def _hg_levels(q, k, b, mk_ref):
    refs = _level_refs(b)
    a = jnp.zeros((CHUNK, CHUNK), F32)
    es, qts, kts = [], [], []
    for i in range(N_LEV):
        e = jnp.exp(_neg_abs(b - refs[i]))
        qt = (q * e).astype(BF16)
        kt = (k * e).astype(BF16)
        a = a + _dot_nt(qt, kt) * mk_ref[i]
        es.append(e); qts.append(qt); kts.append(kt)
    return a, es, qts, kts


def _hg_specs(nc, rev):
    cc = (lambda c: nc - 1 - c) if rev else (lambda c: c)
    w = HG_HEAD * HG_HPS
    blk = lambda off: pl.BlockSpec((HG_CPS * CHUNK, w), lambda h, c: (cc(c), h + off))
    lb2 = pl.BlockSpec((2, w), lambda h, c: (0, h))
    st = pl.BlockSpec((HG_CPS, HG_HPS, HG_HEAD, HG_HEAD), lambda h, c: (cc(c), h, 0, 0))
    consts = [_full((2, CHUNK, CHUNK)), _full((N_LEV + DIAG, CHUNK, CHUNK))]
    return blk, lb2, st, consts


def _hg_fwd(p0, p1, lbl):
    S = p0.shape[0]
    nc = S // (HG_CPS * CHUNK)
    ng = HG_HEADS // HG_HPS

    def body(qr_ref, fl_ref, v_ref, lbl_ref, tri_ref, mk_ref, o_ref, st_ref, state):
        c = pl.program_id(1)

        @pl.when(c == 0)
        def _():
            state[...] = jnp.zeros_like(state)

        lb_all, _ = _hg_lb(lbl_ref)
        heads = [slice(HG_HEAD * hh, HG_HEAD * (hh + 1)) for hh in range(HG_HPS)]
        for j in range(HG_CPS):
            rows = slice(CHUNK * j, CHUNK * (j + 1))
            qs, ks, bs, mats = [], [], [], []
            for sl in heads:
                _, _, g, k, _, q = _hg_gates(qr_ref[rows, sl], fl_ref[rows, sl], lb_all[:, sl])
                qs.append(q); ks.append(k); bs.append(_tri_dot(tri_ref[0], g))
            for hh in range(HG_HPS):
                a, _, _, _ = _hg_levels(qs[hh], ks[hh], bs[hh], mk_ref)
                mats.append(a + _rowsum(qs[hh] * ks[hh]) * mk_ref[N_LEV])
            for hh, sl in enumerate(heads):
                q, k, b, a, v = qs[hh], ks[hh], bs[hh], mats[hh], v_ref[rows, sl]
                b_last = b[CHUNK - 1: CHUNK, :]
                st = state[hh]
                st_ref[j, hh] = st
                o = _dot_nt((q * jnp.exp(b)).astype(BF16), st.astype(BF16)) + _dot(a.astype(BF16), v.astype(BF16))
                state[hh] = st * jnp.exp(b_last) + _dot3(v, k * jnp.exp(b_last - b), _dot_tn)
                o_ref[rows, sl] = o

    blk, lb2, st_spec, consts = _hg_specs(nc, False)
    tris, masks = _hg_const_arrays()
    return pl.pallas_call(
        body, name="hg_fwd", grid=(ng, nc),
        in_specs=[blk(0), blk(ng), blk(0), lb2] + consts,
        out_specs=[blk(0), st_spec],
        out_shape=[jax.ShapeDtypeStruct((S, HG_W), F32), jax.ShapeDtypeStruct((S // CHUNK, HG_HEADS, HG_HEAD, HG_HEAD), F32)],
        scratch_shapes=[pltpu.VMEM((HG_HPS, HG_HEAD, HG_HEAD), F32)],
        compiler_params=_params(("parallel", "arbitrary")),
    )(p0, p0, p1, lbl, tris, masks)


def _hg_bwd(p0, p1, do_hg, states, lbl):
    S = p0.shape[0]
    nc = S // (HG_CPS * CHUNK)
    ng = HG_HEADS // HG_HPS
    w = HG_HEAD * HG_HPS

    def body(qr_ref, fl_ref, v_ref, do_ref, st_ref, lbl_ref, tri_ref, mk_ref, mkb_ref,
             dqr_ref, dfl_ref, dv_ref, glbl_ref, dstate, carry, acc_lb):
        c = pl.program_id(1)

        @pl.when(c == 0)
        def _():
            dstate[...] = jnp.zeros_like(dstate)
            carry[...] = jnp.zeros_like(carry)
            acc_lb[...] = jnp.zeros_like(acc_lb)

        lb_all, dlb_dl0 = _hg_lb(lbl_ref)
        heads = [slice(HG_HEAD * hh, HG_HEAD * (hh + 1)) for hh in range(HG_HPS)]
        diag_mask = mk_ref[N_LEV]
        for j in reversed(range(HG_CPS)):
            rows = slice(CHUNK * j, CHUNK * (j + 1))
            gates, bs, dos = [], [], []
            for sl in heads:
                gt = _hg_gates(qr_ref[rows, sl], fl_ref[rows, sl], lb_all[:, sl])
                gates.append(gt)
                bs.append(_tri_dot(tri_ref[0], gt[2]))
            for sl in heads:
                dos.append(do_ref[rows, sl])
            inter = []
            for hh, sl in enumerate(heads):
                _, _, _, k, _, q = gates[hh]
                b, do, v = bs[hh], dos[hh], v_ref[rows, sl]
                b_last = b[CHUNK - 1: CHUNK, :]
                eb = jnp.exp(b)
                edec = jnp.exp(b_last - b)
                dst = dstate[hh]
                dq = _dot3(do, st_ref[j, hh]) * eb
                dk = _dot3(v, dst) * edec
                da = _dot_nt(do.astype(BF16), v.astype(BF16))
                dv_state = _dot_nt((k * edec).astype(BF16), dst.astype(BF16))
                dstate[hh] = dst * jnp.exp(b_last) + _dot3(do, q * eb, _dot_tn)
                inter.append((dq, dk, da, dv_state))
            for hh, sl in enumerate(heads):
                sig, f, _, k, sq, q = gates[hh]
                dq, dk, da, dv_state = inter[hh]
                b = bs[hh]
                db = q * dq - k * dk
                dab = da.astype(BF16)
                refs = _level_refs(b)
                a = _rowsum(q * k) * diag_mask
                for i in range(N_LEV):
                    e = jnp.exp(_neg_abs(b - refs[i]))
                    qt = (q * e).astype(BF16)
                    kt = (k * e).astype(BF16)
                    a = a + _dot_nt(qt, kt) * mk_ref[i]
                    dam = dab * mkb_ref[i]
                    gq = _dot(dam, kt)
                    gk = _dot_tn(dam, qt)
                    dq = dq + e * gq
                    dk = dk + e * gk
                    db = db + (qt.astype(F32) * gq - kt.astype(F32) * gk)
                dc = _rowsum(da * diag_mask)
                dq = dq + dc * k
                dk = dk + dc * q
                dv_ref[rows, sl] = (_dot_tn(a.astype(BF16), dos[hh].astype(BF16)) + dv_state).astype(BF16)
                dg = _tri_dot(tri_ref[1], db) + carry[0:1, sl]
                carry[0:1, sl] += jnp.sum(db, axis=0, keepdims=True)
                lb = lb_all[:, sl]
                qr = qr_ref[rows, sl]
                t = dg / f - dk
                dfl_ref[rows, sl] = (t * (1.0 - lb) * sig * (1.0 - sig)).astype(BF16)
                acc_lb[0:1, sl] += jnp.sum(t * (1.0 - sig), axis=0, keepdims=True)
                dqr_ref[rows, sl] = (dq * (sq * (1.0 + qr * (1.0 - sq)))).astype(BF16)

        @pl.when(c == nc - 1)
        def _():
            gl0 = acc_lb[0:1, :] * dlb_dl0
            glbl_ref[0:1, :] = gl0
            glbl_ref[1:2, :] = -gl0

    blk, lb2, st_spec, consts = _hg_specs(nc, True)
    tris, masks = _hg_const_arrays()
    act = jax.ShapeDtypeStruct((S, HG_W), BF16)
    return pl.pallas_call(
        body, name="hg_bwd", grid=(ng, nc),
        in_specs=[blk(0), blk(ng), blk(0), blk(0), st_spec, lb2] + consts + consts[1:],
        out_specs=[blk(0), blk(0), blk(0), lb2],
        out_shape=[act, act, act, jax.ShapeDtypeStruct((2, HG_W), F32)],
        scratch_shapes=[pltpu.VMEM((HG_HPS, HG_HEAD, HG_HEAD), F32), pltpu.VMEM((8, w), F32), pltpu.VMEM((8, w), F32)],
        compiler_params=_params(("parallel", "arbitrary")),
    )(p0, p0, p1, do_hg, states, lbl, tris, masks, masks.astype(BF16))


def _stat_head(lane):
    return lane >> 4


def _stat_lane(lane):
    return lane & (STAT_LANES - 1)


def _o_head(lane):
    return lane >> 6


def _att_bias():
    qi = np.arange(ATT_BLK)[:, None]
    kj = np.arange(2 * ATT_BLK)[None, :]
    band = (kj >= qi) & (kj <= qi + ATT_BLK)
    qm = np.stack([band & (kj >= ATT_BLK), band])
    cur = (kj < ATT_BLK) & (qi <= kj)
    km = np.stack([cur, cur | ((kj >= ATT_BLK) & (qi >= kj - ATT_BLK))])
    to_bias = lambda m: jnp.asarray(np.where(m, 0.0, NEG), F32)
    return to_bias(qm), to_bias(km)


def _att_fwd(qkv_d):
    d, L, _ = qkv_d.shape
    qb = ATT_QB
    rows = qb * ATT_BLK

    def body(q_ref, kp_ref, kc_ref, vp_ref, vc_ref, bias_ref, o_ref, lse_ref, s_scr, p_scr):
        first = _iota((ATT_BLK, HEAD_PAIR), 1) < AT_HEAD
        head_of_lane = _stat_head(_iota((ATT_BLK, STAT_W), 1))
        pairs = [slice(HEAD_PAIR * hp, HEAD_PAIR * (hp + 1)) for hp in range(4)]
        blk = lambda i: slice(ATT_BLK * i, ATT_BLK * (i + 1))

        def keys(i, sl, prev_ref, cur_ref):
            before = prev_ref[0, :, sl] if i == 0 else cur_ref[0, blk(i - 1), sl]
            return jnp.concatenate([before, cur_ref[0, blk(i), sl]], axis=0)

        for i in range(qb):
            for hp, sl in enumerate(pairs):
                q2 = q_ref[0, blk(i), sl] * ATT_SCALE
                zero = jnp.zeros_like(q2)
                qs = jnp.concatenate([jnp.where(first, q2, zero), jnp.where(first, zero, q2)], axis=0)
                s_scr[4 * i + hp] = _dot_nt(qs, keys(i, sl, kp_ref, kc_ref))
        stats = []
        for i in range(qb):
            bias = bias_ref[jnp.minimum(pl.program_id(1), 1)] if i == 0 else bias_ref[1]
            bias2 = jnp.concatenate([bias, bias], axis=0)
            for hp in range(4):
                s = s_scr[4 * i + hp] + bias2
                m = jnp.max(s, axis=-1, keepdims=True)
                p = jnp.exp(s - m)
                l = jnp.sum(p, axis=-1, keepdims=True)
                p_scr[4 * i + hp] = p.astype(BF16)
                stats.append((l, m + jnp.log(l)))
        for i in range(qb):
            lse_blk = jnp.zeros((ATT_BLK, STAT_W), F32)
            for hp, sl in enumerate(pairs):
                l, lse = stats[4 * i + hp]
                o = _dot(p_scr[4 * i + hp], keys(i, sl, vp_ref, vc_ref)) / l
                o_ref[0, blk(i), sl] = jnp.where(first, o[:ATT_BLK], o[ATT_BLK:]).astype(BF16)
                lse_blk = jnp.where(head_of_lane == 2 * hp, lse[:ATT_BLK],
                                    jnp.where(head_of_lane == 2 * hp + 1, lse[ATT_BLK:], lse_blk))
            lse_ref[0, blk(i), :] = lse_blk

    cur = lambda j: pl.BlockSpec((1, rows, AT_W), lambda r, n: (r, n, j))
    prev = lambda j: pl.BlockSpec((1, ATT_BLK, AT_W), lambda r, n: (r, jnp.maximum(qb * n - 1, 0), j))
    return pl.pallas_call(
        body, name=f"att_fwd_d{d}", grid=(d, L // rows),
        in_specs=[cur(0), prev(1), cur(1), prev(2), cur(2), _full((2, ATT_BLK, 2 * ATT_BLK))],
        out_specs=[pl.BlockSpec((1, rows, AT_W), lambda r, n: (r, n, 0)), pl.BlockSpec((1, rows, STAT_W), lambda r, n: (r, n, 0))],
        out_shape=[jax.ShapeDtypeStruct((d, L, AT_W), BF16), jax.ShapeDtypeStruct((d, L, STAT_W), F32)],
        scratch_shapes=[pltpu.VMEM((4 * qb, 2 * ATT_BLK, 2 * ATT_BLK), F32), pltpu.VMEM((4 * qb, 2 * ATT_BLK, 2 * ATT_BLK), BF16)],
        compiler_params=_params(("parallel", "parallel")),
    )(qkv_d, qkv_d, qkv_d, qkv_d, qkv_d, _att_bias()[0])


def _att_combine(oa_ref, ob4_ref, oc16_ref, la_ref, lb4_ref, lc16_ref, p4_ref, p16_ref, z_ref, oat_ref, lse_ref,
                 lb_scr, lc_scr):
    tm = z_ref.shape[0]
    _merge_residues(lb4_ref, lb_scr, 4)
    _merge_residues(lc16_ref, lc_scr, 16)
    ls = (la_ref[...], lb_scr[0], lc_scr[0])
    mx = jnp.maximum(jnp.maximum(ls[0], ls[1]), ls[2])
    es = [jnp.exp(l - mx) for l in ls]
    zs = es[0] + es[1] + es[2]
    lse_ref[...] = mx + jnp.log(zs)
    spread = ((_o_head(_iota((STAT_W, AT_W), 1)) == _stat_head(_iota((STAT_W, AT_W), 0)))
              & (_stat_lane(_iota((STAT_W, AT_W), 0)) == 0)).astype(BF16)
    os_ = (oa_ref[...].astype(F32), _merge_bf16(ob4_ref, p4_ref, 4), _merge_bf16(oc16_ref, p16_ref, 16))
    o = jnp.zeros((tm, AT_W), F32)
    for e, oi in zip(es, os_):
        hi, lo = _split2(e / zs)
        o = o + (_dot(hi, spread) + _dot(lo, spread)) * oi
    oat_ref[...] = o
    z = z_ref[...]
    return (o * (z * _sigmoid(z))).astype(BF16)


def _att_combine_specs(tm):
    row = lambda w: pl.BlockSpec((tm, w), lambda s: (s, 0))
    ins = [row(AT_W), _dil_spec(4, tm, AT_W), _dil_spec(16, tm, AT_W),
           row(STAT_W), _dil_spec(4, tm, STAT_W), _dil_spec(16, tm, STAT_W)] + [_full((PERM_ROWS, PERM_ROWS))] * 2
    scratch = [pltpu.VMEM((tm, AT_W), F32), pltpu.VMEM((tm, STAT_W), F32), _slab_scratch(tm, STAT_W),
               _slab_scratch(tm, STAT_W)]
    return ins, scratch


def _att_gate_bwd(dm, o_ref, l_ref, z_ref, t4_ref, t16_ref, do_ref, do4_ref, do16_ref, dl_ref, dl4_ref, dl16_ref, dz_ref,
                  dl_scr):
    tm = dm.shape[0]
    o = o_ref[...]
    z = z_ref[...]
    sz = _sigmoid(z)
    dz_ref[...] = (dm * o * (sz * (1.0 + z * (1.0 - sz)))).astype(BF16)
    do = dm * (z * sz)
    dob = do.astype(BF16)
    do_ref[...] = dob
    gather = (_o_head(_iota((AT_W, STAT_W), 0)) == _stat_head(_iota((AT_W, STAT_W), 1))).astype(BF16)
    dl = jnp.where(_stat_lane(_iota((tm, STAT_W), 1)) < STAT_LSE_LANE, _dot_sel(do * o, gather), l_ref[...])
    dl_ref[...] = dl
    _split_bf16(dob, t4_ref, do4_ref, 4)
    _split_bf16(dob, t16_ref, do16_ref, 16)
    _to_slabs(dl, dl_scr)
    _split_residues(dl_scr, dl4_ref, 4, F32)
    _split_residues(dl_scr, dl16_ref, 16, F32)


def _att_gate_bwd_specs(S, tm):
    row = lambda w: pl.BlockSpec((tm, w), lambda s: (s, 0))
    sds = jax.ShapeDtypeStruct
    ins = [row(AT_W)] + [_full((PERM_ROWS, PERM_ROWS))] * 2
    outs = [row(AT_W), _dil_spec(4, tm, AT_W), _dil_spec(16, tm, AT_W),
            row(STAT_W), _dil_spec(4, tm, STAT_W), _dil_spec(16, tm, STAT_W), row(AT_W)]
    shapes = [sds((S, AT_W), BF16), sds((4, S // 4, AT_W), BF16), sds((16, S // 16, AT_W), BF16),
              sds((S, STAT_W), F32), sds((4, S // 4, STAT_W), F32), sds((16, S // 16, STAT_W), F32),
              sds((S, AT_W), BF16)]
    return ins, outs, shapes, _slab_scratch(tm, STAT_W)


def _att_bwd(qkv_d, do_d, dl_d):
    d, L, _ = qkv_d.shape
    nb = L // ATT_BLK
    qb = ATT_QB
    rows = qb * ATT_BLK
    nsteps = L // rows

    def body(qc_ref, qn_ref, kp_ref, kc_ref, vp_ref, vc_ref, ac_ref, an_ref, lc_ref, ln_ref, bq_ref, bk_ref, dqkv_ref,
             s_scr, dp_scr, st_scr, dpt_scr, ds_scr, pt_scr, dst_scr):
        n = pl.program_id(1)
        first = _iota((ATT_BLK, HEAD_PAIR), 1) < AT_HEAD
        pairs = [slice(HEAD_PAIR * hp, HEAD_PAIR * (hp + 1)) for hp in range(4)]
        blk = lambda i: slice(ATT_BLK * i, ATT_BLK * (i + 1))

        def stack(t):
            zero = jnp.zeros_like(t)
            return jnp.concatenate([jnp.where(first, t, zero), jnp.where(first, zero, t)], axis=0)

        def unstack(t2):
            return jnp.where(first, t2[:ATT_BLK], t2[ATT_BLK:])

        def with_prev(i, sl, prev_ref, cur_ref):
            before = prev_ref[0, :, sl] if i == 0 else cur_ref[0, blk(i - 1), sl]
            return jnp.concatenate([before, cur_ref[0, blk(i), sl]], axis=0)

        def with_next(i, sl, cur_ref, next_ref):
            after = next_ref[0, :, sl] if i == qb - 1 else cur_ref[0, blk(i + 1), sl]
            return jnp.concatenate([cur_ref[0, blk(i), sl], after], axis=0)

        for i in range(qb):
            for hp, sl in enumerate(pairs):
                j = 4 * i + hp
                s_scr[j] = _dot_nt(stack(qc_ref[0, blk(i), sl] * ATT_SCALE), with_prev(i, sl, kp_ref, kc_ref))
                dp_scr[j] = _dot_nt(stack(ac_ref[0, blk(i), sl]), with_prev(i, sl, vp_ref, vc_ref))
                st_scr[j] = _dot_nt(stack(kc_ref[0, blk(i), sl] * ATT_SCALE), with_next(i, sl, qc_ref, qn_ref))
                dpt_scr[j] = _dot_nt(stack(vc_ref[0, blk(i), sl]), with_next(i, sl, ac_ref, an_ref))
        for i in range(qb):
            bias = bq_ref[jnp.minimum(n, 1)] if i == 0 else bq_ref[1]
            bias_t = bk_ref[jnp.minimum(nsteps - 1 - n, 1)] if i == qb - 1 else bk_ref[1]
            bias2 = jnp.concatenate([bias, bias], axis=0)
            bias_t2 = jnp.concatenate([bias_t, bias_t], axis=0)
            dl_c = lc_ref[0, blk(i), :]
            dl_t = with_next(i, slice(None), lc_ref, ln_ref).T
            for hp in range(4):
                j = 4 * i + hp
                at = [STAT_LANES * (2 * hp), STAT_LANES * (2 * hp + 1)]
                col = lambda t, o: jnp.concatenate([t[:, a + o: a + o + 1] for a in at], axis=0)
                p = jnp.exp(s_scr[j] + bias2 - col(dl_c, STAT_LSE_LANE))
                ds_scr[j] = (p * (dp_scr[j] - col(dl_c, 0))).astype(BF16)
                row = lambda t, o: jnp.concatenate([jnp.broadcast_to(t[a + o: a + o + 1, :], (ATT_BLK, 2 * ATT_BLK)) for a in at], axis=0)
                pt = jnp.exp(st_scr[j] + bias_t2 - row(dl_t, STAT_LSE_LANE))
                pt_scr[j] = pt.astype(BF16)
                dst_scr[j] = (pt * (dpt_scr[j] - row(dl_t, 0))).astype(BF16)
        for i in range(qb):
            for hp, sl in enumerate(pairs):
                j = 4 * i + hp
                dq = unstack(_dot(ds_scr[j], with_prev(i, sl, kp_ref, kc_ref))) * ATT_SCALE
                dk = unstack(_dot(dst_scr[j], with_next(i, sl, qc_ref, qn_ref))) * ATT_SCALE
                dv = unstack(_dot(pt_scr[j], with_next(i, sl, ac_ref, an_ref)))
                dqkv_ref[0, blk(i), sl] = dq.astype(BF16)
                dqkv_ref[0, blk(i), AT_W + HEAD_PAIR * hp: AT_W + HEAD_PAIR * (hp + 1)] = dk.astype(BF16)
                dqkv_ref[0, blk(i), 2 * AT_W + HEAD_PAIR * hp: 2 * AT_W + HEAD_PAIR * (hp + 1)] = dv.astype(BF16)

    cur = lambda j: pl.BlockSpec((1, rows, AT_W), lambda r, n: (r, n, j))
    prev = lambda j: pl.BlockSpec((1, ATT_BLK, AT_W), lambda r, n: (r, jnp.maximum(qb * n - 1, 0), j))
    nxt_blk = lambda n: jnp.minimum(qb * (n + 1), nb - 1)
    sq = (4 * qb, 2 * ATT_BLK, 2 * ATT_BLK)
    return pl.pallas_call(
        body, name=f"att_bwd_d{d}", grid=(d, nsteps),
        in_specs=[cur(0), pl.BlockSpec((1, ATT_BLK, AT_W), lambda r, n: (r, nxt_blk(n), 0)), prev(1), cur(1), prev(2), cur(2),
                  pl.BlockSpec((1, rows, AT_W), lambda r, n: (r, n, 0)),
                  pl.BlockSpec((1, ATT_BLK, AT_W), lambda r, n: (r, nxt_blk(n), 0)),
                  pl.BlockSpec((1, rows, STAT_W), lambda r, n: (r, n, 0)),
                  pl.BlockSpec((1, ATT_BLK, STAT_W), lambda r, n: (r, nxt_blk(n), 0)),
                  _full((2, ATT_BLK, 2 * ATT_BLK)), _full((2, ATT_BLK, 2 * ATT_BLK))],
        out_specs=pl.BlockSpec((1, rows, 3 * AT_W), lambda r, n: (r, n, 0)),
        out_shape=jax.ShapeDtypeStruct((d, L, 3 * AT_W), BF16),
        scratch_shapes=[pltpu.VMEM(sq, F32)] * 4 + [pltpu.VMEM(sq, BF16)] * 3,
        compiler_params=_params(("parallel", "parallel")),
    )(qkv_d, qkv_d, qkv_d, qkv_d, qkv_d, qkv_d, do_d, do_d, dl_d, dl_d, *_att_bias())


def _att_bwd_combine(dqkvs, cos2, sin2):
    S = dqkvs[0].shape[1]
    tm = MIX_TM

    def body(a_ref, b4_ref, c16_ref, p4_ref, p16_ref, cos_ref, sin_ref, dq_ref, dk_ref, dv_ref):
        t = a_ref[...].astype(F32) + _merge_bf16(b4_ref, p4_ref, 4) + _merge_bf16(c16_ref, p16_ref, 16)
        dy = t[:, : 2 * AT_W]
        cosf = jnp.tile(cos_ref[...], (1, 8))
        sinf = jnp.tile(sin_ref[...], (1, 8))
        dx = dy * cosf - _rope_rot(dy) * sinf
        dq_ref[...] = dx[:, :AT_W].astype(BF16)
        dk_ref[...] = dx[:, AT_W:].astype(BF16)
        dv_ref[...] = t[:, 2 * AT_W:].astype(BF16)

    row = lambda w: pl.BlockSpec((tm, w), lambda s: (s, 0))
    act = jax.ShapeDtypeStruct((S, AT_W), BF16)
    return pl.pallas_call(
        body, name="att_bwd_combine", grid=(S // tm,),
        in_specs=[row(3 * AT_W), _dil_spec(4, tm, 3 * AT_W), _dil_spec(16, tm, 3 * AT_W),
                  _full((PERM_ROWS, PERM_ROWS)), _full((PERM_ROWS, PERM_ROWS)), row(128), row(128)],
        out_specs=[row(AT_W), row(AT_W), row(AT_W)],
        out_shape=[act, act, act],
        compiler_params=_params(("parallel",)),
    )(dqkvs[0].reshape(S, 3 * AT_W), dqkvs[1], dqkvs[2], _perm_matrix(4), _perm_matrix(16), cos2, sin2)


def _outproj(x2, tgt2, os_d, ls_d, w_out_full, fnw, o_hg, p1, hg_norm_w, z_at):
    S = x2.shape[0]
    tm = PROJ_TM
    ns = S // tm

    def body(x_ref, t_ref, oa_ref, ob4_ref, oc16_ref, la_ref, lb4_ref, lc16_ref, p4_ref, p16_ref,
             w_ref, fw_ref, o_ref, z_ref, hgw_ref, zat_ref, t4_ref, t16_ref,
             dh_ref, doh_ref, dzh_ref, gw_ref, gfw_ref, ghgw_ref, loss_ref, *rest):
        gate_outs = rest[:7]
        oat_scr, lse_scr, lb_scr, lc_scr, dl_scr = rest[7:]
        s = pl.program_id(0)

        @pl.when(s == 0)
        def _():
            gw_ref[...] = jnp.zeros_like(gw_ref)
            gfw_ref[...] = jnp.zeros_like(gfw_ref)
            ghgw_ref[...] = jnp.zeros_like(ghgw_ref)
            loss_ref[...] = jnp.zeros_like(loss_ref)

        heads = [slice(HG_HEAD * hh, HG_HEAD * (hh + 1)) for hh in range(HG_HEADS)]
        norm = []
        for sl in heads:
            o, z = o_ref[:, sl], z_ref[:, sl]
            rs = lax.rsqrt(jnp.mean(o * o, axis=-1, keepdims=True) + EPS)
            norm.append((rs, o * rs, _sigmoid(z)))
        mh = jnp.concatenate([(oh * hgw_ref[:, sl] * (z_ref[:, sl] * sz)).astype(BF16)
                              for sl, (_, oh, sz) in zip(heads, norm)], axis=1)
        ma = _att_combine(oa_ref, ob4_ref, oc16_ref, la_ref, lb4_ref, lc16_ref, p4_ref, p16_ref, zat_ref,
                          oat_scr, lse_scr, lb_scr, lc_scr)
        y = _dot(mh, w_ref[:HG_W, :]) + _dot(ma, w_ref[HG_W:, :])
        h = x_ref[...] + y
        r = lax.rsqrt(jnp.mean(h * h, axis=-1, keepdims=True) + EPS)
        hn = h * r
        fw = fw_ref[...]
        err = hn * fw - t_ref[...]
        loss_ref[...] += 0.5 * jnp.sum(jnp.mean(err * err, axis=-1, keepdims=True))
        dout = err * (1.0 / D_MODEL)
        gfw_ref[...] += jnp.sum(dout * hn, axis=0, keepdims=True)
        dhn = dout * fw
        dh = r * (dhn - hn * jnp.mean(dhn * hn, axis=-1, keepdims=True))
        dh_ref[...] = dh
        dhb = dh.astype(BF16)
        _att_gate_bwd(_dot_nt(dhb, w_ref[HG_W:, :]), oat_scr, lse_scr, zat_ref, t4_ref, t16_ref, *gate_outs, dl_scr)
        dmh = _dot_nt(dhb, w_ref[:HG_W, :])
        for sl, (rs, oh, sz) in zip(heads, norm):
            z, dmix, gwv = z_ref[:, sl], dmh[:, sl], hgw_ref[:, sl]
            dzh_ref[:, sl] = (dmix * (oh * gwv) * (sz * (1.0 + z * (1.0 - sz)))).astype(BF16)
            don = dmix * (z * sz)
            ghgw_ref[:, sl] += jnp.sum(don * oh, axis=0, keepdims=True)
            dy = don * gwv
            doh_ref[:, sl] = rs * (dy - oh * jnp.mean(dy * oh, axis=-1, keepdims=True))
        gw_ref[:HG_W, :] += _dot_tn(mh, dhb)
        gw_ref[HG_W:, :] += _dot_tn(ma, dhb)

    row = lambda w: pl.BlockSpec((tm, w), lambda s: (s, 0))
    gate_in, gate_out, gate_shapes, gate_scratch = _att_gate_bwd_specs(S, tm)
    comb_in, comb_scratch = _att_combine_specs(tm)
    return pl.pallas_call(
        body, name="outproj", grid=(ns,),
        in_specs=[row(D_MODEL), row(D_MODEL)] + comb_in
        + [_full((D_MODEL, D_MODEL)), _full((1, D_MODEL)),
           row(HG_W), pl.BlockSpec((tm, HG_W), lambda s: (s, 1)), _full((1, HG_W))] + gate_in,
        out_specs=[row(D_MODEL), row(HG_W), row(HG_W), _full((D_MODEL, D_MODEL)), _full((1, D_MODEL)),
                   _full((1, HG_W)), _full((8, 128))] + gate_out,
        out_shape=[jax.ShapeDtypeStruct((S, D_MODEL), F32), jax.ShapeDtypeStruct((S, HG_W), F32),
                   jax.ShapeDtypeStruct((S, HG_W), BF16),
                   jax.ShapeDtypeStruct((D_MODEL, D_MODEL), F32), jax.ShapeDtypeStruct((1, D_MODEL), F32),
                   jax.ShapeDtypeStruct((1, HG_W), F32), jax.ShapeDtypeStruct((8, 128), F32)] + gate_shapes,
        scratch_shapes=comb_scratch + [gate_scratch],
        compiler_params=_params(("arbitrary",)),
    )(x2, tgt2, os_d[0].reshape(S, AT_W), os_d[1], os_d[2], ls_d[0].reshape(S, STAT_W), ls_d[1], ls_d[2],
      _perm_matrix(4), _perm_matrix(16), w_out_full, fnw, o_hg, p1, hg_norm_w, z_at,
      _perm_matrix(4, True), _perm_matrix(16, True))


def _inproj_bwd_x(dps, w_in_full, x2, norm_w, dh):
    S = x2.shape[0]
    tm = PROJ_TM

    def body(d0, d1, d2, d3, d4, d5, d6, d7, w_ref, x_ref, nw_ref, dh_ref, gx_ref, gnw_ref):
        s = pl.program_id(0)

        @pl.when(s == 0)
        def _():
            gnw_ref[...] = jnp.zeros_like(gnw_ref)

        du = jnp.zeros((tm, D_MODEL), F32)
        for i, dref in enumerate((d0, d1, d2, d3, d4, d5, d6, d7)):
            j, half = divmod(i, 2)
            du = du + _dot_nt(dref[...], w_ref[j, :, 512 * half: 512 * (half + 1)])
        x = x_ref[...]
        r = lax.rsqrt(jnp.mean(x * x, axis=-1, keepdims=True) + EPS)
        xh = x * r
        gnw_ref[...] += jnp.sum(du * xh, axis=0, keepdims=True)
        dun = du * nw_ref[...]
        gx_ref[...] = dh_ref[...] + r * (dun - xh * jnp.mean(dun * xh, axis=-1, keepdims=True))

    row = lambda w: pl.BlockSpec((tm, w), lambda s: (s, 0))
    return pl.pallas_call(
        body, name="inproj_bwd_x", grid=(S // tm,),
        in_specs=[row(512)] * 8 + [_full((4, D_MODEL, 1024)), row(D_MODEL), _full((1, D_MODEL)), row(D_MODEL)],
        out_specs=[row(D_MODEL), _full((1, D_MODEL))],
        out_shape=[jax.ShapeDtypeStruct((S, D_MODEL), F32), jax.ShapeDtypeStruct((1, D_MODEL), F32)],
        compiler_params=_params(("arbitrary",)),
    )(*dps, w_in_full, x2, norm_w, dh)


def _inproj_bwd_w(u, dps):
    S = u.shape[0]
    tm = 2 * PROJ_TM

    def body(u_ref, d0, d1, d2, d3, d4, d5, d6, d7, g_ref):
        @pl.when(pl.program_id(0) == 0)
        def _():
            g_ref[...] = jnp.zeros_like(g_ref)

        ub = u_ref[...]
        for i, dref in enumerate((d0, d1, d2, d3, d4, d5, d6, d7)):
            j, half = divmod(i, 2)
            g_ref[j, :, 512 * half: 512 * (half + 1)] += _dot_tn(ub, dref[...])

    return pl.pallas_call(
        body, name="inproj_bwd_w", grid=(S // tm,),
        in_specs=[pl.BlockSpec((tm, D_MODEL), lambda s: (s, 0))] + [pl.BlockSpec((tm, 512), lambda s: (s, 0))] * 8,
        out_specs=_full((4, D_MODEL, 1024)),
        out_shape=jax.ShapeDtypeStruct((4, D_MODEL, 1024), F32),
        compiler_params=_params(("arbitrary",)),
    )(u, *dps)


def _adamw_update(gg, w_ref, m_ref, v_ref, d_ref, nm_ref, nv_ref):
    nm = ADAM_B1 * m_ref[...] + (1.0 - ADAM_B1) * gg
    nv = ADAM_B2 * v_ref[...] + (1.0 - ADAM_B2) * (gg * gg)
    m_hat = nm / (1.0 - ADAM_B1 ** ADAM_STEP)
    v_hat = nv / (1.0 - ADAM_B2 ** ADAM_STEP)
    d_ref[...] = -ADAM_LR * (m_hat / (jnp.sqrt(v_hat) + ADAM_EPS) + ADAM_WD * w_ref[...])
    nm_ref[...] = nm
    nv_ref[...] = nv


def _adamw_small(red, ws, ms, vs):
    def body(red_ref, *refs):
        w, m, v, g, d, nm, nv = (refs[4 * i: 4 * (i + 1)] for i in range(7))
        loss_ref = refs[28]
        g[0][...] = red_ref[0:1, :]
        g[1][0:1, :] = red_ref[3:4, :HG_W]
        g[1][1:2, :] = red_ref[3:4, HG_W:]
        g[2][...] = red_ref[2:3, :HG_W]
        g[3][...] = red_ref[1:2, :]
        loss_ref[...] = red_ref[4:5, 0:1]
        for i in range(4):
            _adamw_update(g[i][...], w[i], m[i], v[i], d[i], nm[i], nv[i])

    shapes = [a.shape for a in ws]
    sds = [jax.ShapeDtypeStruct(s, F32) for s in shapes]
    out = pl.pallas_call(
        body, name="adamw_small",
        in_specs=[_full(red.shape)] + [_full(s) for s in shapes] * 3,
        out_specs=[_full(s) for s in shapes] * 4 + [_full((1, 1))],
        out_shape=sds * 4 + [jax.ShapeDtypeStruct((1, 1), F32)],
        compiler_params=_params(),
    )(red, *ws, *ms, *vs)
    return out[0:4], out[4:8], out[8:12], out[12:16], out[16]


def _place():
    return lax.axis_index("x"), lax.axis_index("y"), lax.axis_index("c")


def _gather_weights(w_in_s, w_out_s):
    def body(win_ref, wout_ref, fin_ref, fout_ref, send_sems, recv_sems):
        x, y, c = _place()
        me = (x, y, c)
        sib = (x, y, 1 - c)
        mine = 2 * x + y
        fin_ref[mine] = win_ref[...].astype(BF16)
        fout_ref[mine] = wout_ref[...].astype(BF16)
        chips = [(1 - x, y), (x, 1 - y), (1 - x, 1 - y)]

        def halves(chip, half):
            return (fin_ref.at[chip, pl.ds(half * 512, 512), :], fout_ref.at[chip, pl.ds(half * 128, 128), :])

        def copy(k, ref, to):
            return pltpu.make_async_remote_copy(src_ref=ref, dst_ref=ref, send_sem=send_sems.at[k],
                                                recv_sem=recv_sems.at[k], device_id=to, device_id_type=MESH)

        first, passed = [], []
        for j, (cx, cy) in enumerate(chips):
            for a, ref in enumerate(halves(mine, c)):
                first.append(copy(2 * j + a, ref, (cx, cy, c)))
        for cp in first:
            cp.start()
        for j, (cx, cy) in enumerate(chips):
            for a, ref in enumerate(halves(2 * cx + cy, c)):
                copy(2 * j + a, ref, me).wait_recv()
                fwd = copy(6 + 2 * j + a, ref, sib)
                fwd.start()
                passed.append(fwd)
        for j, (cx, cy) in enumerate(chips):
            for a, ref in enumerate(halves(2 * cx + cy, 1 - c)):
                copy(6 + 2 * j + a, ref, me).wait_recv()
        for cp in first + passed:
            cp.wait_send()

    vm = pl.BlockSpec(memory_space=pltpu.VMEM)
    return pl.pallas_call(
        body, name="gather_weights",
        in_specs=[vm, vm], out_specs=[vm, vm],
        out_shape=[jax.ShapeDtypeStruct((4, D_MODEL, 1024), BF16), jax.ShapeDtypeStruct((4, 256, D_MODEL), BF16)],
        scratch_shapes=[pltpu.SemaphoreType.DMA((12,)), pltpu.SemaphoreType.DMA((12,))],
        compiler_params=pltpu.CompilerParams(vmem_limit_bytes=VMEM_LIMIT),
    )(w_in_s, w_out_s)


def _swap_halves(g_in, g_out):
    def body(gin_ref, gout_ref, rin_ref, rout_ref, send_sems, recv_sems):
        x, y, c = _place()
        sib = (x, y, 1 - c)
        cps = [pltpu.make_async_remote_copy(src_ref=src.at[:, 1 - c], dst_ref=dst, send_sem=send_sems.at[k],
                                            recv_sem=recv_sems.at[k], device_id=sib, device_id_type=MESH)
               for k, (src, dst) in enumerate(((gin_ref, rin_ref), (gout_ref, rout_ref)))]
        for cp in cps:
            cp.start()
        for cp in cps:
            cp.wait()

    hbm = pl.BlockSpec(memory_space=pl.ANY)
    return pl.pallas_call(
        body, name="swap_halves",
        in_specs=[hbm, hbm], out_specs=[hbm, hbm],
        out_shape=[jax.ShapeDtypeStruct((4,) + g.shape[2:], F32) for g in (g_in, g_out)],
        scratch_shapes=[pltpu.SemaphoreType.DMA((2,)), pltpu.SemaphoreType.DMA((2,))],
    )(g_in, g_out)


def _add_halves(gs, rs, cidx):
    n = gs[0].shape[0]

    def body(c_ref, *refs):
        k = len(gs)
        for g_ref, r_ref, o_ref in zip(refs[:k], refs[k: 2 * k], refs[2 * k:]):
            o_ref[0] = (g_ref[0, 0] + r_ref[0]).astype(BF16)

    mine = lambda g: pl.BlockSpec((1, 1) + g.shape[2:], lambda j, c_ref: (j, c_ref[0], 0, 0))
    chunk = lambda g: pl.BlockSpec((1,) + g.shape[2:], lambda j, c_ref: (j, 0, 0))
    return pl.pallas_call(
        body, name="add_halves",
        grid_spec=pltpu.PrefetchScalarGridSpec(
            num_scalar_prefetch=1, grid=(n,),
            in_specs=[mine(g) for g in gs] + [chunk(g) for g in gs],
            out_specs=[chunk(g) for g in gs]),
        out_shape=[jax.ShapeDtypeStruct((n,) + g.shape[2:], BF16) for g in gs],
        compiler_params=_params(("parallel",)),
    )(cidx, *gs, *rs)


def _exchange_copies(in_ref, out_ref, lin_ref, lout_ref, send_sems, recv_sems):
    x, y, c = _place()
    cps = []
    for k, (cx, cy) in enumerate([(1 - x, y), (x, 1 - y), (1 - x, 1 - y)]):
        for a, (src, dst) in enumerate(((in_ref, lin_ref), (out_ref, lout_ref))):
            cps.append(pltpu.make_async_remote_copy(
                src_ref=src.at[2 * cx + cy], dst_ref=dst.at[k], send_sem=send_sems.at[2 * k + a],
                recv_sem=recv_sems.at[2 * k + a], device_id=(cx, cy, c), device_id_type=MESH))
    return cps


def _exchange_start(cs_in, cs_out):
    def body(in_ref, out_ref, lin_ref, lout_ref, send_sems, recv_sems, in_thru, out_thru, lin_thru, lout_thru, token):
        for cp in _exchange_copies(in_ref, out_ref, lin_ref, lout_ref, send_sems, recv_sems):
            cp.start()
        token[...] = jnp.zeros_like(token)

    lands = [lax.empty((3,) + a.shape[1:], a.dtype) for a in (cs_in, cs_out)]
    bufs = [pltpu.with_memory_space_constraint(a, pltpu.HBM) for a in (cs_in, cs_out, *lands)]
    hbm = pl.BlockSpec(memory_space=pltpu.HBM)
    sem = pl.BlockSpec(memory_space=pltpu.SEMAPHORE)
    return pl.pallas_call(
        body, name="exchange_start",
        in_specs=[hbm] * 4,
        out_specs=[sem, sem, hbm, hbm, hbm, hbm, pl.BlockSpec(memory_space=pltpu.VMEM)],
        out_shape=[pltpu.SemaphoreType.DMA((6,)), pltpu.SemaphoreType.DMA((6,))]
                  + [pltpu.HBM(b.shape, b.dtype) for b in bufs] + [jax.ShapeDtypeStruct((8, 128), F32)],
        input_output_aliases={0: 2, 1: 3, 2: 4, 3: 5},
        compiler_params=pltpu.CompilerParams(has_side_effects=pltpu.SideEffectType.DATAFLOW_SIDE_EFFECTING),
    )(*bufs)


def _exchange_wait(send_sems, recv_sems, in_thru, out_thru, lin_thru, lout_thru, after):
    def body(in_ref, out_ref, lin_ref, lout_ref, send_sems, recv_sems, after_ref, in_dead, out_dead, got_in, got_out):
        for cp in _exchange_copies(in_ref, out_ref, lin_ref, lout_ref, send_sems, recv_sems):
            cp.wait_send()
            cp.wait_recv()

    hbm = pl.BlockSpec(memory_space=pltpu.HBM)
    sem = pl.BlockSpec(memory_space=pltpu.SEMAPHORE)
    bufs = (in_thru, out_thru, lin_thru, lout_thru)
    return pl.pallas_call(
        body, name="exchange_wait",
        in_specs=[hbm] * 4 + [sem, sem, pl.BlockSpec(memory_space=pl.ANY)],
        out_specs=[hbm] * 4,
        out_shape=[pltpu.HBM(b.shape, b.dtype) for b in bufs],
        input_output_aliases={0: 0, 1: 1, 2: 2, 3: 3},
        compiler_params=pltpu.CompilerParams(has_side_effects=pltpu.SideEffectType.DATAFLOW_SIDE_EFFECTING),
    )(*bufs, send_sems, recv_sems, after)


PEER_XOR = (2, 1, 3)


SUM_STEPS = 2


def _sum_chips(css, rs, chip_idx):
    def body(m_ref, *refs):
        k = len(css)
        mine = m_ref[0]
        for cs_ref, r_ref, o_ref in zip(refs[:k], refs[k: 2 * k], refs[2 * k:]):
            own = cs_ref[0].astype(F32)
            got = [r_ref[j].astype(F32) for j in range(3)]
            acc = None
            for s in range(4):
                rel = mine ^ s
                term = jnp.where(rel == 0, own, jnp.where(rel == PEER_XOR[0], got[0],
                                                          jnp.where(rel == PEER_XOR[1], got[1], got[2])))
                acc = term if acc is None else acc + term
            o_ref[...] = acc

    tile = lambda r: (r.shape[1] // SUM_STEPS, r.shape[2])
    return pl.pallas_call(
        body, name="sum_chips",
        grid_spec=pltpu.PrefetchScalarGridSpec(
            num_scalar_prefetch=1, grid=(SUM_STEPS,),
            in_specs=[pl.BlockSpec((1,) + tile(r), lambda i, m_ref: (m_ref[0], i, 0)) for r in rs]
            + [pl.BlockSpec((3,) + tile(r), lambda i, m_ref: (0, i, 0)) for r in rs],
            out_specs=[pl.BlockSpec(tile(r), lambda i, m_ref: (i, 0)) for r in rs]),
        out_shape=[jax.ShapeDtypeStruct(r.shape[1:], F32) for r in rs],
        compiler_params=_params(("parallel",)),
    )(chip_idx, *css, *rs)


def _swap_reduced(h_in, h_out):
    def body(in_ref, out_ref, rin_ref, rout_ref, send_sems, recv_sems):
        x, y, c = _place()
        cps = [pltpu.make_async_remote_copy(src_ref=src, dst_ref=dst, send_sem=send_sems.at[k],
                                            recv_sem=recv_sems.at[k], device_id=(x, y, 1 - c), device_id_type=MESH)
               for k, (src, dst) in enumerate(((in_ref, rin_ref), (out_ref, rout_ref)))]
        for cp in cps:
            cp.start()
        for cp in cps:
            cp.wait()

    hbm = pl.BlockSpec(memory_space=pl.ANY)
    return pl.pallas_call(
        body, name="swap_reduced",
        in_specs=[hbm, hbm], out_specs=[hbm, hbm],
        out_shape=[jax.ShapeDtypeStruct(h.shape, F32) for h in (h_in, h_out)],
        scratch_shapes=[pltpu.SemaphoreType.DMA((2,)), pltpu.SemaphoreType.DMA((2,))],
    )(h_in, h_out)


ADAM_STEPS = 4


def _adamw_halves(ws, mines, theirs, ms, vs, cidx):
    k = len(ws)
    nbh = ADAM_STEPS // 2

    def body(c_ref, *refs):
        w, a, b, m, v, g, d, nm, nv = (refs[k * i: k * (i + 1)] for i in range(9))
        own = pl.program_id(0) // nbh == c_ref[0]
        for t in range(k):
            gg = jnp.where(own, a[t][...], b[t][...])
            g[t][...] = gg
            _adamw_update(gg, w[t], m[t], v[t], d[t], nm[t], nv[t])

    tile = lambda w: (w.shape[0] // ADAM_STEPS, w.shape[1])
    specs = [pl.BlockSpec(tile(w), lambda i, c_ref: (i, 0)) for w in ws]
    hspecs = [pl.BlockSpec(tile(w), lambda i, c_ref: (i % nbh, 0)) for w in ws]
    out = pl.pallas_call(
        body, name="adamw_halves",
        grid_spec=pltpu.PrefetchScalarGridSpec(
            num_scalar_prefetch=1, grid=(ADAM_STEPS,),
            in_specs=specs + hspecs * 2 + specs * 2, out_specs=specs * 4),
        out_shape=[jax.ShapeDtypeStruct(w.shape, F32) for w in ws] * 4,
        compiler_params=_params(("parallel",)),
    )(cidx, *ws, *mines, *theirs, *ms, *vs)
    return out[0:k], out[k: 2 * k], out[2 * k: 3 * k], out[3 * k:]


def _allreduce_small(g_nw, g_fw, g_hgw, g_lbl, loss8):
    def body(nw_ref, fw_ref, hgw_ref, lbl_ref, loss_ref, out_ref, slots, send_sems, recv_sems):
        x, y, c = _place()
        me = 4 * x + 2 * y + c
        slots[me] = jnp.zeros((8, D_MODEL), F32)
        slots[me, 0:1, :] = nw_ref[...]
        slots[me, 1:2, :] = fw_ref[...]
        slots[me, 2:3, 0:HG_W] = hgw_ref[...]
        slots[me, 3:4, 0:HG_W] = lbl_ref[0:1, :]
        slots[me, 3:4, HG_W:] = lbl_ref[1:2, :]
        slots[me, 4:5, 0:128] = loss_ref[0:1, :]
        cps = []
        for k in range(1, 8):
            dx, dy, dc = (k >> 2) & 1, (k >> 1) & 1, k & 1
            to = (x ^ dx, y ^ dy, c ^ dc)
            cps.append(pltpu.make_async_remote_copy(
                src_ref=slots.at[me], dst_ref=slots.at[me], send_sem=send_sems.at[k - 1],
                recv_sem=recv_sems.at[k - 1], device_id=to, device_id_type=MESH))
        for cp in cps:
            cp.start()
        for cp in cps:
            cp.wait()
        acc = slots[0]
        for i in range(1, 8):
            acc = acc + slots[i]
        out_ref[...] = acc

    vm = pl.BlockSpec(memory_space=pltpu.VMEM)
    return pl.pallas_call(
        body, name="allreduce_small",
        in_specs=[vm] * 5, out_specs=vm,
        out_shape=jax.ShapeDtypeStruct((8, D_MODEL), F32),
        scratch_shapes=[pltpu.VMEM((8, 8, D_MODEL), F32), pltpu.SemaphoreType.DMA((7,)), pltpu.SemaphoreType.DMA((7,))],
    )(g_nw, g_fw, g_hgw, g_lbl, loss8)


def _rope_tables(S):
    inv_freq = (np.float32(1.0) / np.power(np.float32(ROPE_THETA), np.arange(ROPE_HALF, dtype=np.float32) / np.float32(ROPE_HALF))).astype(np.float32)
    ang = (np.arange(S, dtype=np.float32)[:, None] * inv_freq[None, :]).astype(np.float32)
    cos, sin = np.cos(ang).astype(np.float32), np.sin(ang).astype(np.float32)
    cos2 = np.concatenate([cos, cos, cos, cos], axis=-1)
    sin2 = np.concatenate([-sin, sin, -sin, sin], axis=-1)
    return jnp.asarray(cos2), jnp.asarray(sin2)


def _local_step(x2, tgt2, norm_w, w_in_full, lbl, hg_norm_w, w_out4, fnw):
    S = x2.shape[0]
    cos2, sin2 = _rope_tables(S)
    w_out_full = w_out4.reshape(D_MODEL, D_MODEL)
    p0, p1, qkv, qkv4, qkv16, z_at, u = _inproj(x2, norm_w, w_in_full, cos2, sin2)
    o_hg, states = _hg_fwd(p0, p1, lbl)
    qkv_ds = [qkv.reshape(1, S, 3 * AT_W), qkv4, qkv16]
    os_d, ls_d = zip(*[_att_fwd(q) for q in qkv_ds])
    dh, do_hg, dz_hg, g_wout, g_fw, g_hgw, loss8, do1, do4, do16, dl1, dl4, dl16, dz_at = _outproj(
        x2, tgt2, os_d, ls_d, w_out_full, fnw, o_hg, p1, hg_norm_w, z_at)
    dqr, dfl, dv_hg, g_lbl = _hg_bwd(p0, p1, do_hg, states, lbl)
    do_ds = [do1.reshape(1, S, AT_W), do4, do16]
    dl_ds = [dl1.reshape(1, S, STAT_W), dl4, dl16]
    dqkvs = [_att_bwd(q, a, b) for q, a, b in zip(qkv_ds, do_ds, dl_ds)]
    dq_at, dk_at, dv_at = _att_bwd_combine(dqkvs, cos2, sin2)
    dps = [dqr, dfl, dv_hg, dz_hg, dq_at, dk_at, dv_at, dz_at]
    return loss8, dps, u, dh, g_lbl, g_hgw, g_wout, g_fw


def kernel(x, norm_w, w_in, hgrn_lb_logits, hg_norm_w, w_out, final_norm_w, loss_target, m_norm_w, m_w_in, m_hgrn_lb_logits, m_hg_norm_w, m_w_out, m_final_norm_w, v_norm_w, v_w_in, v_hgrn_lb_logits, v_hg_norm_w, v_w_out, v_final_norm_w):
    S = x.shape[1]
    w_in_full, w_out_full = _gather_weights(w_in[0], w_out[0])
    loss8, dps, u, dh, g_lbl, g_hgw, g_wout, g_fw = _local_step(
        x[0], loss_target[0], norm_w, w_in_full, hgrn_lb_logits, hg_norm_w,
        w_out_full, final_norm_w.reshape(1, D_MODEL))

    cidx = lax.axis_index("c").astype(jnp.int32).reshape(1)
    g_win = _inproj_bwd_w(u, dps)
    g_in4 = g_win.reshape(4, 2, 512, 1024)
    g_out4 = g_wout.reshape(4, 2, 128, D_MODEL)
    r_in, r_out = _swap_halves(g_in4, g_out4)
    cs_in, cs_out = _add_halves((g_in4, g_out4), (r_in, r_out), cidx)
    *started, token = _exchange_start(cs_in, cs_out)
    grad_x, g_nw = _inproj_bwd_x(dps, w_in_full, x[0], norm_w + token[0:1, 0:1], dh)
    cs_in, cs_out, x_in, x_out = _exchange_wait(*started, g_nw)
    chip_idx = (2 * lax.axis_index("x") + lax.axis_index("y")).astype(jnp.int32).reshape(1)
    h_in, h_out = _sum_chips((cs_in, cs_out), (x_in, x_out), chip_idx)
    s_in, s_out = _swap_reduced(h_in, h_out)

    red = _allreduce_small(g_nw, g_fw, g_hgw, g_lbl, loss8)
    as_row = lambda a: a.reshape(1, D_MODEL)
    grads, deltas, new_m, new_v, loss = _adamw_small(
        red, (norm_w, hgrn_lb_logits, hg_norm_w, as_row(final_norm_w)),
        (m_norm_w, m_hgrn_lb_logits, m_hg_norm_w, as_row(m_final_norm_w)),
        (v_norm_w, v_hgrn_lb_logits, v_hg_norm_w, as_row(v_final_norm_w)))
    grad_norm_w, grad_lbl, grad_hg_norm_w, grad_final_norm_w = grads
    d_nw, d_lbl, d_hgw, d_fw = deltas
    m_nw, m_lbl, m_hgw, m_fw = new_m
    v_nw, v_lbl, v_hgw, v_fw = new_v
    (grad_w_in, grad_w_out), (d_win, d_wout), (m_win, m_wout), (v_win, v_wout) = _adamw_halves(
        (w_in[0], w_out[0]), (h_in, h_out), (s_in, s_out), (m_w_in[0], m_w_out[0]), (v_w_in[0], v_w_out[0]), cidx)
    e1 = lambda a: a[None]
    flat = lambda a: a.reshape(D_MODEL)
    return (loss.reshape(()), grad_x[None], grad_norm_w, e1(grad_w_in), grad_lbl, grad_hg_norm_w, e1(grad_w_out),
            flat(grad_final_norm_w),
            d_nw, e1(d_win), d_lbl, d_hgw, e1(d_wout), flat(d_fw),
            m_nw, e1(m_win), m_lbl, m_hgw, e1(m_wout), flat(m_fw),
            v_nw, e1(v_win), v_lbl, v_hgw, e1(v_wout), flat(v_fw))
```
